```python
import jax, jax.numpy as jnp
from jax import lax
import numpy as np

D_MODEL = 2048
BATCH = 8
SEQ = 4096
DEPTH = 1

A_HEADS = 8
A_HEAD_DIM = 128
A_WIDTH = A_HEADS * A_HEAD_DIM
CONV_WIDTH = 4
CHUNK_A = 64
B_GROUPS = 8
B_GROUP_DIM = 128
B_WIDTH = B_GROUPS * B_GROUP_DIM
CHUNK_B = 128
MIX_WIDTH = A_WIDTH + B_WIDTH
IN_SIZES = (3 * A_WIDTH, A_WIDTH, A_HEADS, A_HEADS, B_WIDTH, B_WIDTH, B_WIDTH)
IN_WIDTH = int(sum(IN_SIZES))
IN_SPLITS = tuple(int(s) for s in np.cumsum(IN_SIZES)[:-1])
EPS = 1e-6

kernel_name = "hymba_style_gdn_gmlp_hybrid"


def rms_norm(x, w):
    xf = x.astype(jnp.float32)
    y = xf * lax.rsqrt(jnp.mean(xf * xf, axis=-1, keepdims=True) + EPS)
    return (y * w.astype(jnp.float32)).astype(x.dtype)


def layer_norm(x, w, b):
    xf = x.astype(jnp.float32)
    mu = jnp.mean(xf, axis=-1, keepdims=True)
    var = jnp.mean(jnp.square(xf - mu), axis=-1, keepdims=True)
    y = (xf - mu) * lax.rsqrt(var + EPS)
    return (y * w.astype(jnp.float32) + b.astype(jnp.float32)).astype(x.dtype)


def l2_normalize(x):
    return x * lax.rsqrt(jnp.sum(x * x, axis=-1, keepdims=True) + EPS)


def causal_depthwise_conv(x, w):
    C = x.shape[-1]
    return lax.conv_general_dilated(
        x, w[:, None, :].astype(x.dtype), window_strides=(1,),
        padding=[(CONV_WIDTH - 1, 0)],
        dimension_numbers=("NWC", "WIO", "NWC"), feature_group_count=C)


def chunk_gated_delta_rule(q, k, v, g, beta):
    B, T, H, D = q.shape
    N, C = T // CHUNK_A, CHUNK_A
    chunk4 = lambda t: t.reshape(B, N, C, H, D).transpose(0, 3, 1, 2, 4)
    chunk3 = lambda t: t.reshape(B, N, C, H).transpose(0, 3, 1, 2)
    q = chunk4(q) * (D ** -0.5)
    k, v = chunk4(k), chunk4(v)
    beta, g = chunk3(beta), chunk3(g)
    kb = k * beta[..., None]
    vb = v * beta[..., None]
    gc = jnp.cumsum(g, axis=-1)
    incl = jnp.tril(jnp.ones((C, C), dtype=bool))
    strict = jnp.tril(jnp.ones((C, C), dtype=bool), k=-1)
    diff = gc[..., :, None] - gc[..., None, :]
    decay = jnp.where(incl, jnp.exp(jnp.where(incl, diff, 0.0)), 0.0)
    L = jnp.where(strict, jnp.einsum('bhnid,bhnjd->bhnij', kb, k) * decay, 0.0)
    rhs = jnp.concatenate([vb, kb * jnp.exp(gc)[..., None]], axis=-1)
    sol = lax.linalg.triangular_solve(L + jnp.eye(C, dtype=L.dtype), rhs,
                                      left_side=True, lower=True, unit_diagonal=True)
    u, w = sol[..., :D], sol[..., D:]
    attn = jnp.where(incl, jnp.einsum('bhnid,bhnjd->bhnij', q, k) * decay, 0.0)
    xs = tuple(jnp.moveaxis(t, 2, 0) for t in (q, k, u, w, attn, gc))

    def step(S, inp):
        qi, ki, ui, wi, ai, gi = inp
        v_new = ui - jnp.einsum('bhcd,bhde->bhce', wi, S)
        o = (jnp.einsum('bhcd,bhde->bhce', qi * jnp.exp(gi)[..., None], S)
             + jnp.einsum('bhij,bhje->bhie', ai, v_new))
        g_last = gi[..., -1]
        k_dec = ki * jnp.exp(g_last[..., None] - gi)[..., None]
        S = S * jnp.exp(g_last)[..., None, None] + jnp.einsum('bhcd,bhce->bhde', k_dec, v_new)
        return S, o

    S0 = jnp.zeros((B, H, D, D), dtype=jnp.float32)
    _, o = lax.scan(step, S0, xs)
    return o.transpose(1, 0, 3, 2, 4).reshape(B, T, H, D)


def chunked_causal_sgu(v, w_s, b_s):
    Bn, T, _ = v.shape
    N = T // CHUNK_B
    vc = v.reshape(Bn, N, CHUNK_B, B_GROUPS, B_GROUP_DIM)
    mask = jnp.tril(jnp.ones((CHUNK_B, CHUNK_B), dtype=bool))
    w_m = jnp.where(mask[None], w_s, 0.0).astype(v.dtype)
    s = jnp.einsum('gts,bnsgc->bntgc', w_m, vc) + b_s.T.astype(v.dtype)[None, None, :, :, None]
    return s.reshape(Bn, T, B_WIDTH)


def _fwd_setup_inputs(seed: int = 0) -> dict:
    key = jax.random.key(seed)
    ks = jax.random.split(key, 14)
    f32 = jnp.float32
    x = jax.random.normal(ks[0], (BATCH, SEQ, D_MODEL), f32)
    norm_w = 1.0 + 0.02 * jax.random.normal(ks[1], (DEPTH, D_MODEL), f32)
    w_in = jax.random.normal(ks[2], (DEPTH, D_MODEL, IN_WIDTH), f32) * D_MODEL ** -0.5
    conv_w = jax.random.normal(ks[3], (DEPTH, CONV_WIDTH, 3 * A_WIDTH), f32) * CONV_WIDTH ** -0.5
    a_log = jnp.log(jax.random.uniform(ks[4], (DEPTH, A_HEADS), f32, 1.0, 16.0))
    dt = jnp.exp(jax.random.uniform(ks[5], (DEPTH, A_HEADS), f32,
                                    np.log(1e-3).astype(np.float32), np.log(1e-1).astype(np.float32)))
    dt_bias = dt + jnp.log(-jnp.expm1(-dt))
    head_norm_w = 1.0 + 0.02 * jax.random.normal(ks[6], (DEPTH, A_HEAD_DIM), f32)
    sgu_ln_w = 1.0 + 0.02 * jax.random.normal(ks[7], (DEPTH, B_WIDTH), f32)
    sgu_ln_b = 0.02 * jax.random.normal(ks[8], (DEPTH, B_WIDTH), f32)
    w_spatial = jax.random.normal(ks[9], (DEPTH, B_GROUPS, CHUNK_B, CHUNK_B), f32) * CHUNK_B ** -0.5
    b_spatial = 1.0 + 0.02 * jax.random.normal(ks[10], (DEPTH, B_GROUPS, CHUNK_B), f32)
    w_out = jax.random.normal(ks[11], (DEPTH, MIX_WIDTH, D_MODEL), f32) * MIX_WIDTH ** -0.5
    final_norm_w = 1.0 + 0.02 * jax.random.normal(ks[12], (D_MODEL,), f32)
    return {"x": x, "norm_w": norm_w, "w_in": w_in, "conv_w": conv_w, "a_log": a_log,
            "dt_bias": dt_bias, "head_norm_w": head_norm_w, "sgu_ln_w": sgu_ln_w,
            "sgu_ln_b": sgu_ln_b, "w_spatial": w_spatial, "b_spatial": b_spatial,
            "w_out": w_out, "final_norm_w": final_norm_w}


def _fwd_reference(x, norm_w, w_in, conv_w, a_log, dt_bias, head_norm_w, sgu_ln_w,
              sgu_ln_b, w_spatial, b_spatial, w_out, final_norm_w):
    Bn, T, _ = x.shape
    h = x
    for l in range(DEPTH):
        xn = rms_norm(h, norm_w[l])
        proj = xn @ w_in[l].astype(x.dtype)
        qkv, z_a, b_raw, a_raw, u_b, v_b, z_b = jnp.split(proj, IN_SPLITS, axis=-1)

        qkv = jax.nn.silu(causal_depthwise_conv(qkv, conv_w[l])).astype(jnp.float32)
        q, k, v = jnp.split(qkv, 3, axis=-1)
        q = l2_normalize(q.reshape(Bn, T, A_HEADS, A_HEAD_DIM))
        k = l2_normalize(k.reshape(Bn, T, A_HEADS, A_HEAD_DIM))
        v = v.reshape(Bn, T, A_HEADS, A_HEAD_DIM)
        beta = jax.nn.sigmoid(b_raw.astype(jnp.float32))
        g = -jnp.exp(a_log[l].astype(jnp.float32)) * jax.nn.softplus(
            a_raw.astype(jnp.float32) + dt_bias[l].astype(jnp.float32))
        o_a = chunk_gated_delta_rule(q, k, v, g, beta)
        o_a = rms_norm(o_a, head_norm_w[l]).astype(x.dtype)
        o_a = (o_a * jax.nn.silu(z_a.reshape(Bn, T, A_HEADS, A_HEAD_DIM))).reshape(Bn, T, A_WIDTH)

        v_n = layer_norm(v_b, sgu_ln_w[l], sgu_ln_b[l])
        o_b = u_b * chunked_causal_sgu(v_n, w_spatial[l], b_spatial[l]) * jax.nn.silu(z_b)

        mix = jnp.concatenate([o_a, o_b], axis=-1) @ w_out[l].astype(x.dtype)
        h = h + mix
    return rms_norm(h, final_norm_w)


import jax as _jax
import jax.numpy as _jnp

TWIN_FORMAT = 'train_step'
FWD_PARAMS = ['x', 'norm_w', 'w_in', 'conv_w', 'a_log', 'dt_bias', 'head_norm_w', 'sgu_ln_w', 'sgu_ln_b', 'w_spatial', 'b_spatial', 'w_out', 'final_norm_w']
TWIN_WEIGHTS = ['norm_w', 'w_in', 'conv_w', 'a_log', 'dt_bias', 'head_norm_w', 'sgu_ln_w', 'sgu_ln_b', 'w_spatial', 'b_spatial', 'w_out', 'final_norm_w']
TWIN_DIFF_INPUT = 'x'
TWIN_INPUTS = ['x', 'norm_w', 'w_in', 'conv_w', 'a_log', 'dt_bias', 'head_norm_w', 'sgu_ln_w', 'sgu_ln_b', 'w_spatial', 'b_spatial', 'w_out', 'final_norm_w', 'loss_target', 'm_norm_w', 'm_w_in', 'm_conv_w', 'm_a_log', 'm_dt_bias', 'm_head_norm_w', 'm_sgu_ln_w', 'm_sgu_ln_b', 'm_w_spatial', 'm_b_spatial', 'm_w_out', 'm_final_norm_w', 'v_norm_w', 'v_w_in', 'v_conv_w', 'v_a_log', 'v_dt_bias', 'v_head_norm_w', 'v_sgu_ln_w', 'v_sgu_ln_b', 'v_w_spatial', 'v_b_spatial', 'v_w_out', 'v_final_norm_w']
TWIN_OUTPUTS = ['loss', 'grad_x', 'grad_norm_w', 'grad_w_in', 'grad_conv_w', 'grad_a_log', 'grad_dt_bias', 'grad_head_norm_w', 'grad_sgu_ln_w', 'grad_sgu_ln_b', 'grad_w_spatial', 'grad_b_spatial', 'grad_w_out', 'grad_final_norm_w', 'delta_norm_w', 'delta_w_in', 'delta_conv_w', 'delta_a_log', 'delta_dt_bias', 'delta_head_norm_w', 'delta_sgu_ln_w', 'delta_sgu_ln_b', 'delta_w_spatial', 'delta_b_spatial', 'delta_w_out', 'delta_final_norm_w', 'new_m_norm_w', 'new_m_w_in', 'new_m_conv_w', 'new_m_a_log', 'new_m_dt_bias', 'new_m_head_norm_w', 'new_m_sgu_ln_w', 'new_m_sgu_ln_b', 'new_m_w_spatial', 'new_m_b_spatial', 'new_m_w_out', 'new_m_final_norm_w', 'new_v_norm_w', 'new_v_w_in', 'new_v_conv_w', 'new_v_a_log', 'new_v_dt_bias', 'new_v_head_norm_w', 'new_v_sgu_ln_w', 'new_v_sgu_ln_b', 'new_v_w_spatial', 'new_v_b_spatial', 'new_v_w_out', 'new_v_final_norm_w']
TWIN_LEAF_KINDS = {'loss': 'loss', 'grad_x': 'grad_x', 'grad_norm_w': 'grad_w', 'grad_w_in': 'grad_w', 'grad_conv_w': 'grad_w', 'grad_a_log': 'grad_w', 'grad_dt_bias': 'grad_w', 'grad_head_norm_w': 'grad_w', 'grad_sgu_ln_w': 'grad_w', 'grad_sgu_ln_b': 'grad_w', 'grad_w_spatial': 'grad_w', 'grad_b_spatial': 'grad_w', 'grad_w_out': 'grad_w', 'grad_final_norm_w': 'grad_w', 'delta_norm_w': 'delta_w', 'delta_w_in': 'delta_w', 'delta_conv_w': 'delta_w', 'delta_a_log': 'delta_w', 'delta_dt_bias': 'delta_w', 'delta_head_norm_w': 'delta_w', 'delta_sgu_ln_w': 'delta_w', 'delta_sgu_ln_b': 'delta_w', 'delta_w_spatial': 'delta_w', 'delta_b_spatial': 'delta_w', 'delta_w_out': 'delta_w', 'delta_final_norm_w': 'delta_w', 'new_m_norm_w': 'new_m', 'new_m_w_in': 'new_m', 'new_m_conv_w': 'new_m', 'new_m_a_log': 'new_m', 'new_m_dt_bias': 'new_m', 'new_m_head_norm_w': 'new_m', 'new_m_sgu_ln_w': 'new_m', 'new_m_sgu_ln_b': 'new_m', 'new_m_w_spatial': 'new_m', 'new_m_b_spatial': 'new_m', 'new_m_w_out': 'new_m', 'new_m_final_norm_w': 'new_m', 'new_v_norm_w': 'new_v', 'new_v_w_in': 'new_v', 'new_v_conv_w': 'new_v', 'new_v_a_log': 'new_v', 'new_v_dt_bias': 'new_v', 'new_v_head_norm_w': 'new_v', 'new_v_sgu_ln_w': 'new_v', 'new_v_sgu_ln_b': 'new_v', 'new_v_w_spatial': 'new_v', 'new_v_b_spatial': 'new_v', 'new_v_w_out': 'new_v', 'new_v_final_norm_w': 'new_v'}


def _forward(args):
    return _fwd_reference(*[args[k] for k in FWD_PARAMS])


def _output_shape():
    def fwd():
        inp = _fwd_setup_inputs(0)
        return _fwd_reference(*[inp[k] for k in FWD_PARAMS])
    out = _jax.eval_shape(fwd)
    return out.shape, out.dtype

N_MICROBATCH = 1
ADAM_LR = 0.001
ADAM_B1 = 0.9
ADAM_B2 = 0.999
ADAM_EPS = 1e-08
ADAM_WD = 0.01
ADAM_STEP = 10
PER_EXAMPLE_BATCH_AXIS = {'x': 0, 'loss_target': 0}
SHARED_INPUTS = []
_WEIGHT_DTYPES = {'norm_w': _jnp.float32, 'w_in': _jnp.float32, 'conv_w': _jnp.float32, 'a_log': _jnp.float32, 'dt_bias': _jnp.float32, 'head_norm_w': _jnp.float32, 'sgu_ln_w': _jnp.float32, 'sgu_ln_b': _jnp.float32, 'w_spatial': _jnp.float32, 'b_spatial': _jnp.float32, 'w_out': _jnp.float32, 'final_norm_w': _jnp.float32}
MOMENT_SCALE = {'norm_w': 7.776763e-02, 'w_in': 4.168380e-02, 'conv_w': 3.392230e-02, 'a_log': 2.267524e-01, 'dt_bias': 2.074061e-01, 'head_norm_w': 1.259870e-01, 'sgu_ln_w': 2.910721e-02, 'sgu_ln_b': 3.093819e-02, 'w_spatial': 3.092508e-02, 'b_spatial': 4.373419e-02, 'w_out': 4.884033e-02, 'final_norm_w': 1.600503e+01}


def _to_microbatches(a, axis):
    t = _jnp.moveaxis(a, axis, 0)
    t = t.reshape((N_MICROBATCH, t.shape[0] // N_MICROBATCH) + t.shape[1:])
    return _jnp.moveaxis(t, 1, axis + 1)


def setup_inputs(seed: int = 0) -> dict:
    inp = _fwd_setup_inputs(seed)
    key = _jax.random.fold_in(_jax.random.key(seed), 7919)
    shape, _ = _output_shape()
    out = dict(inp)
    out["loss_target"] = _jax.random.normal(_jax.random.fold_in(key, 0), shape, _jnp.float32)
    for i, name in enumerate(TWIN_WEIGHTS):
        w = inp[name].astype(_jnp.float32)
        if MOMENT_SCALE is None:
            s = _jnp.sqrt(_jnp.mean(_jnp.square(w)) + 1e-30)
        else:
            s = MOMENT_SCALE[name]
        km, kv = _jax.random.split(_jax.random.fold_in(key, i + 1))
        out[name] = w
        out["m_" + name] = s * _jax.random.normal(km, w.shape, _jnp.float32)
        out["v_" + name] = (s * s) * _jax.random.uniform(kv, w.shape, _jnp.float32, 0.5, 1.5)
    if N_MICROBATCH > 1:
        for name, axis in PER_EXAMPLE_BATCH_AXIS.items():
            out[name] = _to_microbatches(out[name], axis)
    return {'x': out['x'], 'norm_w': out['norm_w'], 'w_in': out['w_in'], 'conv_w': out['conv_w'], 'a_log': out['a_log'], 'dt_bias': out['dt_bias'], 'head_norm_w': out['head_norm_w'], 'sgu_ln_w': out['sgu_ln_w'], 'sgu_ln_b': out['sgu_ln_b'], 'w_spatial': out['w_spatial'], 'b_spatial': out['b_spatial'], 'w_out': out['w_out'], 'final_norm_w': out['final_norm_w'], 'loss_target': out['loss_target'], 'm_norm_w': out['m_norm_w'], 'm_w_in': out['m_w_in'], 'm_conv_w': out['m_conv_w'], 'm_a_log': out['m_a_log'], 'm_dt_bias': out['m_dt_bias'], 'm_head_norm_w': out['m_head_norm_w'], 'm_sgu_ln_w': out['m_sgu_ln_w'], 'm_sgu_ln_b': out['m_sgu_ln_b'], 'm_w_spatial': out['m_w_spatial'], 'm_b_spatial': out['m_b_spatial'], 'm_w_out': out['m_w_out'], 'm_final_norm_w': out['m_final_norm_w'], 'v_norm_w': out['v_norm_w'], 'v_w_in': out['v_w_in'], 'v_conv_w': out['v_conv_w'], 'v_a_log': out['v_a_log'], 'v_dt_bias': out['v_dt_bias'], 'v_head_norm_w': out['v_head_norm_w'], 'v_sgu_ln_w': out['v_sgu_ln_w'], 'v_sgu_ln_b': out['v_sgu_ln_b'], 'v_w_spatial': out['v_w_spatial'], 'v_b_spatial': out['v_b_spatial'], 'v_w_out': out['v_w_out'], 'v_final_norm_w': out['v_final_norm_w']}


def _loss(weights, diff, rest, loss_target):
    with _jax.named_scope("forward"):
        args = {**rest, TWIN_DIFF_INPUT: diff, **{k: w.astype(_WEIGHT_DTYPES[k]) for k, w in weights.items()}}
        y = _forward(args)
    with _jax.named_scope("loss_head"):
        err = _jnp.square(y.astype(_jnp.float32) - loss_target)
        return 0.5 * _jnp.sum(_jnp.mean(err, axis=-1)) if err.ndim else 0.5 * err


def _adamw(w, g, m, v):
    m = ADAM_B1 * m + (1.0 - ADAM_B1) * g
    v = ADAM_B2 * v + (1.0 - ADAM_B2) * _jnp.square(g)
    m_hat = m / (1.0 - ADAM_B1 ** ADAM_STEP)
    v_hat = v / (1.0 - ADAM_B2 ** ADAM_STEP)
    delta = -ADAM_LR * (m_hat / (_jnp.sqrt(v_hat) + ADAM_EPS) + ADAM_WD * w)
    return delta, m, v


def reference(x, norm_w, w_in, conv_w, a_log, dt_bias, head_norm_w, sgu_ln_w, sgu_ln_b, w_spatial, b_spatial, w_out, final_norm_w, loss_target, m_norm_w, m_w_in, m_conv_w, m_a_log, m_dt_bias, m_head_norm_w, m_sgu_ln_w, m_sgu_ln_b, m_w_spatial, m_b_spatial, m_w_out, m_final_norm_w, v_norm_w, v_w_in, v_conv_w, v_a_log, v_dt_bias, v_head_norm_w, v_sgu_ln_w, v_sgu_ln_b, v_w_spatial, v_b_spatial, v_w_out, v_final_norm_w):
    given = dict(x=x, norm_w=norm_w, w_in=w_in, conv_w=conv_w, a_log=a_log, dt_bias=dt_bias, head_norm_w=head_norm_w, sgu_ln_w=sgu_ln_w, sgu_ln_b=sgu_ln_b, w_spatial=w_spatial, b_spatial=b_spatial, w_out=w_out, final_norm_w=final_norm_w, loss_target=loss_target, m_norm_w=m_norm_w, m_w_in=m_w_in, m_conv_w=m_conv_w, m_a_log=m_a_log, m_dt_bias=m_dt_bias, m_head_norm_w=m_head_norm_w, m_sgu_ln_w=m_sgu_ln_w, m_sgu_ln_b=m_sgu_ln_b, m_w_spatial=m_w_spatial, m_b_spatial=m_b_spatial, m_w_out=m_w_out, m_final_norm_w=m_final_norm_w, v_norm_w=v_norm_w, v_w_in=v_w_in, v_conv_w=v_conv_w, v_a_log=v_a_log, v_dt_bias=v_dt_bias, v_head_norm_w=v_head_norm_w, v_sgu_ln_w=v_sgu_ln_w, v_sgu_ln_b=v_sgu_ln_b, v_w_spatial=v_w_spatial, v_b_spatial=v_b_spatial, v_w_out=v_w_out, v_final_norm_w=v_final_norm_w)
    weights = {n: given[n] for n in TWIN_WEIGHTS}
    shared = {n: given[n] for n in SHARED_INPUTS}
    per_example = {n: given[n] for n in ['x']}
    grad_fn = _jax.value_and_grad(_loss, argnums=(0, 1))

    def one_microbatch(ex, loss_target):
        ex = dict(ex)
        diff = ex.pop(TWIN_DIFF_INPUT)
        return grad_fn(weights, diff, {**shared, **ex}, loss_target)

    if N_MICROBATCH == 1:
        loss, (grad_w, grad_x) = one_microbatch(per_example, given["loss_target"])
    else:
        def body(carry, xs):
            loss_sum, grad_sum = carry
            l_k, (gw_k, gx_k) = one_microbatch(xs[0], xs[1])
            with _jax.named_scope("update"):
                return (loss_sum + l_k, _jax.tree.map(_jnp.add, grad_sum, gw_k)), gx_k

        init = (_jnp.zeros((), _jnp.float32), _jax.tree.map(_jnp.zeros_like, weights))
        (loss, grad_w), grad_x = _jax.lax.scan(body, init, (per_example, given["loss_target"]))
    with _jax.named_scope("update"):
        delta_w, new_m, new_v = {}, {}, {}
        for n in TWIN_WEIGHTS:
            delta_w[n], new_m[n], new_v[n] = _adamw(weights[n], grad_w[n], given["m_" + n], given["v_" + n])
    return (loss, grad_x, *[grad_w[n] for n in TWIN_WEIGHTS], *[delta_w[n] for n in TWIN_WEIGHTS],
            *[new_m[n] for n in TWIN_WEIGHTS], *[new_v[n] for n in TWIN_WEIGHTS])
```

```python
import functools

import jax
import jax.numpy as jnp
from jax import lax
from jax.experimental import pallas as pl
from jax.experimental.pallas import tpu as pltpu

F32 = jnp.float32
BF16 = jnp.bfloat16
EPS = 1e-6
HEAD_DIM = 128
CHUNK_A = 64
CHUNK_B = 128
CONV_WIDTH = 4
LANES = 128
HALO = 8
N_CHIPS = 4
ADAM_LR = 0.001
ADAM_B1 = 0.9
ADAM_B2 = 0.999
ADAM_EPS = 1e-08
ADAM_WD = 0.01
ADAM_STEP = 10
VMEM_LIMIT = 56 * 1024 * 1024
MESH_ID = pl.DeviceIdType.MESH
HI = lax.Precision.HIGHEST


def _cparams(sem=None, **kw):
    return pltpu.CompilerParams(dimension_semantics=sem, vmem_limit_bytes=VMEM_LIMIT, **kw)


def _dot(a, b, hi=False):
    return lax.dot_general(a, b, (((1,), (0,)), ((), ())), precision=HI if hi else None,
                           preferred_element_type=F32)


def _dot_nt(a, b, hi=False):
    return lax.dot_general(a, b, (((1,), (1,)), ((), ())), precision=HI if hi else None,
                           preferred_element_type=F32)


def _dot_tn(a, b, hi=False):
    return lax.dot_general(a, b, (((0,), (0,)), ((), ())), precision=HI if hi else None,
                           preferred_element_type=F32)


def _iota(shape, dim):
    return lax.broadcasted_iota(jnp.int32, shape, dim)


def _sigmoid(x):
    return 0.5 * (jnp.tanh(0.5 * x) + 1.0)


def _silu(x):
    return x * _sigmoid(x)


def _softplus(x):
    z = jnp.exp(-jnp.abs(x))
    small = z * (1.0 - z * (0.5 - z * (1.0 / 3.0)))
    return jnp.maximum(x, 0.0) + jnp.where(z < 1e-3, small, jnp.log(1.0 + z))


def _pick(n, pref):
    for t in pref:
        if n % t == 0:
            return t
    return n


def _mm_nn(a, b, out_dtype, name, tm=1024, tn=512, tk=None):
    M, K = a.shape
    _, N = b.shape
    tm = _pick(M, (tm, 512, 256, 128))
    tn = _pick(N, (tn, 512, 384, 256, 128))
    tk = K if tk is None else _pick(K, (tk,))
    nk = K // tk

    def body(a_ref, b_ref, o_ref, *scratch):
        part = _dot(a_ref[...], b_ref[...])
        if nk == 1:
            o_ref[...] = part.astype(out_dtype)
        else:
            acc_ref, = scratch
            k = pl.program_id(2)

            @pl.when(k == 0)
            def _():
                acc_ref[...] = part

            @pl.when(k > 0)
            def _():
                acc_ref[...] += part

            @pl.when(k == nk - 1)
            def _():
                o_ref[...] = acc_ref[...].astype(out_dtype)

    return pl.pallas_call(
        body, name=name, grid=(M // tm, N // tn, nk),
        in_specs=[pl.BlockSpec((tm, tk), lambda i, j, k: (i, k)),
                  pl.BlockSpec((tk, tn), lambda i, j, k: (k, j))],
        out_specs=pl.BlockSpec((tm, tn), lambda i, j, k: (i, j)),
        out_shape=jax.ShapeDtypeStruct((M, N), out_dtype),
        scratch_shapes=[] if nk == 1 else [pltpu.VMEM((tm, tn), F32)],
        compiler_params=_cparams(("parallel", "parallel", "arbitrary")),
    )(a, b)


def _mm_tn(a, b, name, tm=512, tn=512, tk=1024):
    K, M = a.shape
    _, N = b.shape
    tm = _pick(M, (tm, 256, 128))
    tn = _pick(N, (tn, 384, 256, 128))
    tk = _pick(K, (tk, 512, 256))
    nk = K // tk

    def body(a_ref, b_ref, o_ref):
        part = _dot_tn(a_ref[...], b_ref[...])
        k = pl.program_id(2)

        @pl.when(k == 0)
        def _():
            o_ref[...] = part

        @pl.when(k > 0)
        def _():
            o_ref[...] += part

    return pl.pallas_call(
        body, name=name, grid=(M // tm, N // tn, nk),
        in_specs=[pl.BlockSpec((tk, tm), lambda i, j, k: (k, i)),
                  pl.BlockSpec((tk, tn), lambda i, j, k: (k, j))],
        out_specs=pl.BlockSpec((tm, tn), lambda i, j, k: (i, j)),
        out_shape=jax.ShapeDtypeStruct((M, N), F32),
        compiler_params=_cparams(("parallel", "parallel", "arbitrary")),
    )(a, b)


def _rms_fn(x, w):
    r = lax.rsqrt(jnp.mean(x * x, axis=-1, keepdims=True) + EPS)
    return x * r * w


def _rms_in(x, w):
    T, D = x.shape
    tm = _pick(T, (512, 256, 128))

    def body(x_ref, w_ref, o_ref):
        o_ref[...] = _rms_fn(x_ref[...], w_ref[...]).astype(BF16)

    return pl.pallas_call(
        body, name="rms_in", grid=(T // tm,),
        in_specs=[pl.BlockSpec((tm, D), lambda i: (i, 0)), pl.BlockSpec((1, D), lambda i: (0, 0))],
        out_specs=pl.BlockSpec((tm, D), lambda i: (i, 0)),
        out_shape=jax.ShapeDtypeStruct((T, D), BF16),
        compiler_params=_cparams(("parallel",)),
    )(x, w)


def _rms_in_bwd(x, w, dxn, dh):
    T, D = x.shape
    tm = _pick(T, (256, 128))

    def body(x_ref, w_ref, dxn_ref, dh_ref, gx_ref, dw_ref):
        _, vjp = jax.vjp(_rms_fn, x_ref[...], w_ref[...])
        dx, dw = vjp(dxn_ref[...])
        gx_ref[...] = dh_ref[...] + dx

        @pl.when(pl.program_id(0) == 0)
        def _():
            dw_ref[...] = dw

        @pl.when(pl.program_id(0) > 0)
        def _():
            dw_ref[...] += dw

    tile = pl.BlockSpec((tm, D), lambda i: (i, 0))
    row = pl.BlockSpec((1, D), lambda i: (0, 0))
    return pl.pallas_call(
        body, name="rms_in_bwd", grid=(T // tm,),
        in_specs=[tile, row, tile, tile], out_specs=[tile, row],
        out_shape=[jax.ShapeDtypeStruct((T, D), F32), jax.ShapeDtypeStruct((1, D), F32)],
        compiler_params=_cparams(("arbitrary",)),
    )(x, w, dxn, dh)


def _conv_fwd(xcat, w, tm):
    c = None
    for k in range(CONV_WIDTH):
        s = CONV_WIDTH - 1 - k
        xs = xcat if s == 0 else pltpu.roll(xcat, s, 0)
        term = xs[HALO:, :] * w[k:k + 1, :]
        c = term if c is None else c + term
    return c


def _gdn_pointwise(c, ba, alog, dtb, H):
    A = H * HEAD_DIM
    s = _silu(c)
    beta = _sigmoid(ba)
    g = -jnp.exp(alog) * _softplus(ba + dtb)
    sel_row = _iota((LANES, LANES), 0)
    qs, ks, vs, gbs, bbs = [], [], [], [], []
    for h in range(H):
        lo = h * HEAD_DIM
        q = s[:, lo:lo + HEAD_DIM]
        k = s[:, A + lo:A + lo + HEAD_DIM]
        qs.append(q * lax.rsqrt(jnp.sum(q * q, axis=-1, keepdims=True) + EPS))
        ks.append(k * lax.rsqrt(jnp.sum(k * k, axis=-1, keepdims=True) + EPS))
        vs.append(s[:, 2 * A + lo:2 * A + lo + HEAD_DIM])
        bbs.append(_dot(beta, (sel_row == h).astype(F32), hi=True))
        gbs.append(_dot(g, (sel_row == H + h).astype(F32), hi=True))
    st = lambda xs: jnp.stack(xs, axis=0)
    return st(qs), st(ks), st(vs), st(gbs), st(bbs)


def _halo_prev(tm):
    return lambda i: (jnp.maximum(i * (tm // HALO) - 1, 0), 0)


def _gdn_pre(proj_m, proj_ba, conv_w, alog_row, dtb_row, H):
    T = proj_m.shape[0]
    A = H * HEAD_DIM
    tm = _pick(T, (256, 128))
    hs = pl.BlockSpec((H, tm, HEAD_DIM), lambda i: (0, i, 0))
    hshape = jax.ShapeDtypeStruct((H, T, HEAD_DIM), F32)

    def body(x_ref, halo_ref, ba_ref, w_ref, al_ref, dt_ref, q_ref, k_ref, v_ref, gb_ref, bb_ref):
        halo = jnp.where(pl.program_id(0) == 0, 0.0, halo_ref[...])
        c = _conv_fwd(jnp.concatenate([halo, x_ref[...]], axis=0), w_ref[...], tm)
        q, k, v, gb, bb = _gdn_pointwise(c, ba_ref[...], al_ref[...], dt_ref[...], H)
        q_ref[...] = q
        k_ref[...] = k
        v_ref[...] = v
        gb_ref[...] = gb
        bb_ref[...] = bb

    return pl.pallas_call(
        body, name="gdn_pre", grid=(T // tm,),
        in_specs=[pl.BlockSpec((tm, 3 * A), lambda i: (i, 0)),
                  pl.BlockSpec((HALO, 3 * A), _halo_prev(tm)),
                  pl.BlockSpec((tm, LANES), lambda i: (i, 0)),
                  pl.BlockSpec((CONV_WIDTH, 3 * A), lambda i: (0, 0)),
                  pl.BlockSpec((1, LANES), lambda i: (0, 0)),
                  pl.BlockSpec((1, LANES), lambda i: (0, 0))],
        out_specs=[hs] * 5, out_shape=[hshape] * 5,
        compiler_params=_cparams(("parallel",)),
    )(proj_m, proj_m, proj_ba, conv_w, alog_row, dtb_row)


def _gdn_pre_bwd(proj_m, proj_ba, conv_w, alog_row, dtb_row, dq, dk, dv, dgb, dbb, H):
    T = proj_m.shape[0]
    A = H * HEAD_DIM
    tm = _pick(T, (256, 128))
    hs = pl.BlockSpec((H, tm, HEAD_DIM), lambda i: (0, i, 0))
    row = pl.BlockSpec((1, LANES), lambda i: (0, 0))

    def body(x_ref, halo_ref, ba_ref, w_ref, al_ref, dt_ref, dq_ref, dk_ref, dv_ref, dgb_ref, dbb_ref,
             dc_ref, dba_ref, dal_ref, ddt_ref):
        halo = jnp.where(pl.program_id(0) == 0, 0.0, halo_ref[...])
        c = _conv_fwd(jnp.concatenate([halo, x_ref[...]], axis=0), w_ref[...], tm)
        _, vjp = jax.vjp(functools.partial(_gdn_pointwise, H=H), c, ba_ref[...], al_ref[...], dt_ref[...])
        dc, dba, dal, ddt = vjp((dq_ref[...], dk_ref[...], dv_ref[...], dgb_ref[...], dbb_ref[...]))
        dc_ref[...] = dc
        dba_ref[...] = dba.astype(BF16)

        @pl.when(pl.program_id(0) == 0)
        def _():
            dal_ref[...] = dal
            ddt_ref[...] = ddt

        @pl.when(pl.program_id(0) > 0)
        def _():
            dal_ref[...] += dal
            ddt_ref[...] += ddt

    return pl.pallas_call(
        body, name="gdn_pre_bwd", grid=(T // tm,),
        in_specs=[pl.BlockSpec((tm, 3 * A), lambda i: (i, 0)),
                  pl.BlockSpec((HALO, 3 * A), _halo_prev(tm)),
                  pl.BlockSpec((tm, LANES), lambda i: (i, 0)),
                  pl.BlockSpec((CONV_WIDTH, 3 * A), lambda i: (0, 0)),
                  row, row, hs, hs, hs, hs, hs],
        out_specs=[pl.BlockSpec((tm, 3 * A), lambda i: (i, 0)),
                   pl.BlockSpec((tm, LANES), lambda i: (i, 0)), row, row],
        out_shape=[jax.ShapeDtypeStruct((T, 3 * A), F32), jax.ShapeDtypeStruct((T, LANES), BF16),
                   jax.ShapeDtypeStruct((1, LANES), F32), jax.ShapeDtypeStruct((1, LANES), F32)],
        compiler_params=_cparams(("arbitrary",)),
    )(proj_m, proj_m, proj_ba, conv_w, alog_row, dtb_row, dq, dk, dv, dgb, dbb)


def _conv_bwd(proj_m, dc, conv_w, H):
    T = proj_m.shape[0]
    A = H * HEAD_DIM
    tm = _pick(T, (256, 128))
    nt = T // tm

    def body(x_ref, halo_ref, dc_ref, nxt_ref, w_ref, dx_ref, dw_ref):
        i = pl.program_id(0)
        halo = jnp.where(i == 0, 0.0, halo_ref[...])
        xcat = jnp.concatenate([halo, x_ref[...]], axis=0)
        nxt = jnp.where(i == nt - 1, 0.0, nxt_ref[...])
        dc = dc_ref[...]
        dcat = jnp.concatenate([dc, nxt], axis=0)
        w = w_ref[...]
        dx = None
        rows = []
        for k in range(CONV_WIDTH):
            s = CONV_WIDTH - 1 - k
            ds = dcat if s == 0 else pltpu.roll(dcat, tm + HALO - s, 0)
            term = ds[:tm, :] * w[k:k + 1, :]
            dx = term if dx is None else dx + term
            xs = xcat if s == 0 else pltpu.roll(xcat, s, 0)
            rows.append(jnp.sum(dc * xs[HALO:, :], axis=0, keepdims=True))
        dx_ref[...] = dx.astype(BF16)
        dw = jnp.concatenate(rows + [jnp.zeros((HALO - CONV_WIDTH, 3 * A), F32)], axis=0)

        @pl.when(i == 0)
        def _():
            dw_ref[...] = dw

        @pl.when(i > 0)
        def _():
            dw_ref[...] += dw

    return pl.pallas_call(
        body, name="conv_bwd", grid=(nt,),
        in_specs=[pl.BlockSpec((tm, 3 * A), lambda i: (i, 0)),
                  pl.BlockSpec((HALO, 3 * A), _halo_prev(tm)),
                  pl.BlockSpec((tm, 3 * A), lambda i: (i, 0)),
                  pl.BlockSpec((HALO, 3 * A), lambda i: (jnp.minimum((i + 1) * (tm // HALO), T // HALO - 1), 0)),
                  pl.BlockSpec((CONV_WIDTH, 3 * A), lambda i: (0, 0))],
        out_specs=[pl.BlockSpec((tm, 3 * A), lambda i: (i, 0)),
                   pl.BlockSpec((HALO, 3 * A), lambda i: (0, 0))],
        out_shape=[jax.ShapeDtypeStruct((T, 3 * A), BF16), jax.ShapeDtypeStruct((HALO, 3 * A), F32)],
        compiler_params=_cparams(("arbitrary",)),
    )(proj_m, proj_m, dc, dc, conv_w)


def _gdn_prep_fn(q, k, v, gb, bb):
    C = q.shape[0]
    row, col = _iota((C, C), 0), _iota((C, C), 1)
    incl, strict = row >= col, row > col
    tril = incl.astype(F32)
    triu = (row <= col).astype(F32)
    sel = (_iota((LANES, C), 0) == _iota((LANES, C), 1)).astype(F32)
    gc = _dot(tril, gb, hi=True)
    g_c = _dot(gb, sel, hi=True)
    gcc = _dot(tril, g_c, hi=True)
    gcr = _dot_tn(g_c, triu, hi=True)
    decay = jnp.where(incl, jnp.exp(jnp.where(incl, gcc - gcr, 0.0)), 0.0)
    kb = k * bb
    vb = v * bb
    qs = q * (HEAD_DIM ** -0.5)
    L = jnp.where(strict, _dot_nt(kb, k, hi=True) * decay, 0.0)
    X = -L
    P = (row == col).astype(F32) + X
    n_sq = C.bit_length() - 2
    for _ in range(n_sq):
        X = _dot(X, X, hi=True)
        P = P + _dot(P, X, hi=True)
    egc = jnp.exp(gc)
    u = _dot(P, vb, hi=True)
    w = _dot(P, kb * egc, hi=True)
    attn = jnp.where(incl, _dot_nt(qs, k, hi=True) * decay, 0.0)
    qg = qs * egc
    gl = jnp.sum(gb, axis=0, keepdims=True)
    kdec = k * jnp.exp(gl - gc)
    eg = jnp.exp(gl)
    return u, w, qg, kdec, attn, eg


def _gdn_chain_fn(S, qg, kdec, u, w, attn, eg):
    v_new = u - _dot(w, S, hi=True)
    o = _dot(qg, S, hi=True) + _dot(attn, v_new, hi=True)
    S2 = S * eg + _dot_tn(kdec, v_new, hi=True)
    return o, S2


def _prep_specs(H, T, cb):
    C = CHUNK_A
    lb = C * cb
    N = T // C
    hs = pl.BlockSpec((1, lb, HEAD_DIM), lambda h, n: (h, n, 0))
    at = pl.BlockSpec((1, lb, C), lambda h, n: (h, n, 0))
    es = pl.BlockSpec((1, cb, 1, LANES), lambda h, n: (h, n, 0, 0))
    hshape = jax.ShapeDtypeStruct((H, T, HEAD_DIM), F32)
    ashape = jax.ShapeDtypeStruct((H, T, C), F32)
    eshape = jax.ShapeDtypeStruct((H, N, 1, LANES), F32)
    return hs, at, es, hshape, ashape, eshape


def _gdn_prep(q, k, v, gb, bb):
    H, T, _ = q.shape
    C = CHUNK_A
    cb = _pick(T // C, (4, 2, 1))
    hs, at, es, hshape, ashape, eshape = _prep_specs(H, T, cb)

    def body(q_ref, k_ref, v_ref, gb_ref, bb_ref, u_ref, w_ref, qg_ref, kd_ref, at_ref, eg_ref):
        for j in range(cb):
            sl = pl.ds(j * C, C)
            u, w, qg, kd, attn, eg = _gdn_prep_fn(q_ref[0, sl, :], k_ref[0, sl, :], v_ref[0, sl, :],
                                                  gb_ref[0, sl, :], bb_ref[0, sl, :])
            u_ref[0, sl, :] = u
            w_ref[0, sl, :] = w
            qg_ref[0, sl, :] = qg
            kd_ref[0, sl, :] = kd
            at_ref[0, sl, :] = attn
            eg_ref[0, j] = eg

    return pl.pallas_call(
        body, name="gdn_prep", grid=(H, T // (C * cb)),
        in_specs=[hs] * 5, out_specs=[hs, hs, hs, hs, at, es],
        out_shape=[hshape, hshape, hshape, hshape, ashape, eshape],
        compiler_params=_cparams(("parallel", "parallel")),
    )(q, k, v, gb, bb)


def _gdn_prep_bwd(q, k, v, gb, bb, du, dw, dqg, dkd, dat, deg):
    H, T, _ = q.shape
    C = CHUNK_A
    cb = _pick(T // C, (4, 2, 1))
    hs, at, es, hshape, _, _ = _prep_specs(H, T, cb)

    def body(q_ref, k_ref, v_ref, gb_ref, bb_ref, du_ref, dw_ref, dqg_ref, dkd_ref, dat_ref, deg_ref,
             dq_ref, dk_ref, dv_ref, dgb_ref, dbb_ref):
        for j in range(cb):
            sl = pl.ds(j * C, C)
            _, vjp = jax.vjp(_gdn_prep_fn, q_ref[0, sl, :], k_ref[0, sl, :], v_ref[0, sl, :],
                             gb_ref[0, sl, :], bb_ref[0, sl, :])
            dq, dk, dv, dgb, dbb = vjp((du_ref[0, sl, :], dw_ref[0, sl, :], dqg_ref[0, sl, :],
                                        dkd_ref[0, sl, :], dat_ref[0, sl, :], deg_ref[0, j]))
            dq_ref[0, sl, :] = dq
            dk_ref[0, sl, :] = dk
            dv_ref[0, sl, :] = dv
            dgb_ref[0, sl, :] = dgb
            dbb_ref[0, sl, :] = dbb

    return pl.pallas_call(
        body, name="gdn_prep_bwd", grid=(H, T // (C * cb)),
        in_specs=[hs] * 9 + [at, es], out_specs=[hs] * 5, out_shape=[hshape] * 5,
        compiler_params=_cparams(("parallel", "parallel")),
    )(q, k, v, gb, bb, du, dw, dqg, dkd, dat, deg)


def _gdn_chain(qg, kd, u, w, attn, eg):
    H, T, _ = qg.shape
    C = CHUNK_A
    N = T // C
    hs = pl.BlockSpec((H, C, HEAD_DIM), lambda n: (0, n, 0))
    at = pl.BlockSpec((H, C, C), lambda n: (0, n, 0))
    es = pl.BlockSpec((H, 1, 1, LANES), lambda n: (0, n, 0, 0))
    ss = pl.BlockSpec((1, H, HEAD_DIM, HEAD_DIM), lambda n: (n, 0, 0, 0))

    def body(qg_ref, kd_ref, u_ref, w_ref, at_ref, eg_ref, o_ref, sall_ref, s_ref):
        @pl.when(pl.program_id(0) == 0)
        def _():
            s_ref[...] = jnp.zeros_like(s_ref)

        for h in range(H):
            S = s_ref[h]
            sall_ref[0, h] = S
            o, S2 = _gdn_chain_fn(S, qg_ref[h], kd_ref[h], u_ref[h], w_ref[h], at_ref[h], eg_ref[h, 0])
            o_ref[h] = o
            s_ref[h] = S2

    return pl.pallas_call(
        body, name="gdn_chain", grid=(N,),
        in_specs=[hs, hs, hs, hs, at, es], out_specs=[hs, ss],
        out_shape=[jax.ShapeDtypeStruct((H, T, HEAD_DIM), F32),
                   jax.ShapeDtypeStruct((N, H, HEAD_DIM, HEAD_DIM), F32)],
        scratch_shapes=[pltpu.VMEM((H, HEAD_DIM, HEAD_DIM), F32)],
        compiler_params=_cparams(("arbitrary",)),
    )(qg, kd, u, w, attn, eg)


def _gdn_chain_bwd(qg, kd, u, w, attn, eg, sall, do):
    H, T, _ = qg.shape
    C = CHUNK_A
    N = T // C
    hs = pl.BlockSpec((H, C, HEAD_DIM), lambda n: (0, N - 1 - n, 0))
    at = pl.BlockSpec((H, C, C), lambda n: (0, N - 1 - n, 0))
    es = pl.BlockSpec((H, 1, 1, LANES), lambda n: (0, N - 1 - n, 0, 0))
    ss = pl.BlockSpec((1, H, HEAD_DIM, HEAD_DIM), lambda n: (N - 1 - n, 0, 0, 0))
    hshape = jax.ShapeDtypeStruct((H, T, HEAD_DIM), F32)

    def body(qg_ref, kd_ref, u_ref, w_ref, at_ref, eg_ref, sall_ref, do_ref,
             dqg_ref, dkd_ref, du_ref, dw_ref, dat_ref, deg_ref, ds_ref):
        @pl.when(pl.program_id(0) == 0)
        def _():
            ds_ref[...] = jnp.zeros_like(ds_ref)

        for h in range(H):
            _, vjp = jax.vjp(_gdn_chain_fn, sall_ref[0, h], qg_ref[h], kd_ref[h], u_ref[h], w_ref[h],
                             at_ref[h], eg_ref[h, 0])
            dS, dqg, dkd, du, dw, dat, deg = vjp((do_ref[h], ds_ref[h]))
            ds_ref[h] = dS
            dqg_ref[h] = dqg
            dkd_ref[h] = dkd
            du_ref[h] = du
            dw_ref[h] = dw
            dat_ref[h] = dat
            deg_ref[h, 0] = deg

    return pl.pallas_call(
        body, name="gdn_chain_bwd", grid=(N,),
        in_specs=[hs, hs, hs, hs, at, es, ss, hs], out_specs=[hs, hs, hs, hs, at, es],
        out_shape=[hshape, hshape, hshape, hshape, jax.ShapeDtypeStruct((H, T, C), F32),
                   jax.ShapeDtypeStruct((H, N, 1, LANES), F32)],
        scratch_shapes=[pltpu.VMEM((H, HEAD_DIM, HEAD_DIM), F32)],
        compiler_params=_cparams(("arbitrary",)),
    )(qg, kd, u, w, attn, eg, sall, do)


def _post_fn(ogs, za, hw):
    outs = []
    for h, o in enumerate(ogs):
        r = lax.rsqrt(jnp.mean(o * o, axis=-1, keepdims=True) + EPS)
        outs.append(o * r * hw * _silu(za[:, h * HEAD_DIM:(h + 1) * HEAD_DIM]))
    return jnp.concatenate(outs, axis=1)


def _gdn_post(og, proj_m, hw):
    H, T, _ = og.shape
    A = H * HEAD_DIM
    tm = _pick(T, (512, 256, 128))

    def body(og_ref, za_ref, hw_ref, o_ref):
        o_ref[...] = _post_fn(tuple(og_ref[h] for h in range(H)), za_ref[...], hw_ref[...]).astype(BF16)

    return pl.pallas_call(
        body, name="gdn_post", grid=(T // tm,),
        in_specs=[pl.BlockSpec((H, tm, HEAD_DIM), lambda i: (0, i, 0)),
                  pl.BlockSpec((tm, A), lambda i: (i, 3)),
                  pl.BlockSpec((1, HEAD_DIM), lambda i: (0, 0))],
        out_specs=pl.BlockSpec((tm, A), lambda i: (i, 0)),
        out_shape=jax.ShapeDtypeStruct((T, A), BF16),
        compiler_params=_cparams(("parallel",)),
    )(og, proj_m, hw)


def _gdn_post_bwd(og, proj_m, hw, d_o):
    H, T, _ = og.shape
    A = H * HEAD_DIM
    tm = _pick(T, (256, 128))

    def body(og_ref, za_ref, hw_ref, do_ref, dog_ref, dza_ref, dhw_ref):
        _, vjp = jax.vjp(_post_fn, tuple(og_ref[h] for h in range(H)), za_ref[...], hw_ref[...])
        dog, dza, dhw = vjp(do_ref[...])
        for h in range(H):
            dog_ref[h] = dog[h]
        dza_ref[...] = dza.astype(BF16)

        @pl.when(pl.program_id(0) == 0)
        def _():
            dhw_ref[...] = dhw

        @pl.when(pl.program_id(0) > 0)
        def _():
            dhw_ref[...] += dhw

    return pl.pallas_call(
        body, name="gdn_post_bwd", grid=(T // tm,),
        in_specs=[pl.BlockSpec((H, tm, HEAD_DIM), lambda i: (0, i, 0)),
                  pl.BlockSpec((tm, A), lambda i: (i, 3)),
                  pl.BlockSpec((1, HEAD_DIM), lambda i: (0, 0)),
                  pl.BlockSpec((tm, A), lambda i: (i, 0))],
        out_specs=[pl.BlockSpec((H, tm, HEAD_DIM), lambda i: (0, i, 0)),
                   pl.BlockSpec((tm, A), lambda i: (i, 0)),
                   pl.BlockSpec((1, HEAD_DIM), lambda i: (0, 0))],
        out_shape=[jax.ShapeDtypeStruct((H, T, HEAD_DIM), F32), jax.ShapeDtypeStruct((T, A), BF16),
                   jax.ShapeDtypeStruct((1, HEAD_DIM), F32)],
        compiler_params=_cparams(("arbitrary",)),
    )(og, proj_m, hw, d_o)


def _sgu_fn(ub, vb, zb, lw, lb, W, bbc):
    G = len(W)
    tm = ub.shape[0]
    mu = jnp.mean(vb, axis=-1, keepdims=True)
    xc = vb - mu
    var = jnp.mean(xc * xc, axis=-1, keepdims=True)
    vn = xc * lax.rsqrt(var + EPS) * lw + lb
    mask = _iota((CHUNK_B, CHUNK_B), 0) >= _iota((CHUNK_B, CHUNK_B), 1)
    cols = []
    for g in range(G):
        wm = jnp.where(mask, W[g], 0.0).astype(BF16)
        rows = []
        for c in range(tm // CHUNK_B):
            blk = vn[c * CHUNK_B:(c + 1) * CHUNK_B, g * HEAD_DIM:(g + 1) * HEAD_DIM].astype(BF16)
            rows.append(_dot(wm, blk) + bbc[g])
        cols.append(jnp.concatenate(rows, axis=0) if len(rows) > 1 else rows[0])
    s = jnp.concatenate(cols, axis=1)
    return ub * s * _silu(zb)


def _sgu_cols(A, B):
    base = 4 * A // B
    return base, base + 1, base + 2


def _sgu_fwd(proj_m, lw, lb, W, bbc, A):
    T = proj_m.shape[0]
    G = W.shape[0]
    B = G * HEAD_DIM
    tm = _pick(T, (256, 128))
    cu, cv, cz = _sgu_cols(A, B)

    def body(u_ref, v_ref, z_ref, lw_ref, lb_ref, w_ref, b_ref, o_ref):
        o_ref[...] = _sgu_fn(u_ref[...], v_ref[...], z_ref[...], lw_ref[...], lb_ref[...],
                             tuple(w_ref[g] for g in range(G)), tuple(b_ref[g] for g in range(G))).astype(BF16)

    row = pl.BlockSpec((1, B), lambda i: (0, 0))
    cube = pl.BlockSpec((G, CHUNK_B, CHUNK_B), lambda i: (0, 0, 0))
    return pl.pallas_call(
        body, name="sgu_fwd", grid=(T // tm,),
        in_specs=[pl.BlockSpec((tm, B), lambda i: (i, cu)), pl.BlockSpec((tm, B), lambda i: (i, cv)),
                  pl.BlockSpec((tm, B), lambda i: (i, cz)), row, row, cube, cube],
        out_specs=pl.BlockSpec((tm, B), lambda i: (i, 0)),
        out_shape=jax.ShapeDtypeStruct((T, B), BF16),
        compiler_params=_cparams(("parallel",)),
    )(proj_m, proj_m, proj_m, lw, lb, W, bbc)


def _sgu_bwd(proj_m, lw, lb, W, bbc, d_o, A):
    T = proj_m.shape[0]
    G = W.shape[0]
    B = G * HEAD_DIM
    tm = _pick(T, (256, 128))
    nt = T // tm
    cu, cv, cz = _sgu_cols(A, B)

    def body(u_ref, v_ref, z_ref, lw_ref, lb_ref, w_ref, b_ref, do_ref,
             dp_ref, dlw_ref, dlb_ref, dw_ref, db_ref, dbb_ref):
        _, vjp = jax.vjp(_sgu_fn, u_ref[...], v_ref[...], z_ref[...], lw_ref[...], lb_ref[...],
                         tuple(w_ref[g] for g in range(G)), tuple(b_ref[g] for g in range(G)))
        du, dv, dz, dlw, dlb, dW, dbb = vjp(do_ref[...])
        dW, dbb = jnp.stack(dW, axis=0), jnp.stack(dbb, axis=0)
        dp_ref[:, 0:B] = du.astype(BF16)
        dp_ref[:, B:2 * B] = dv.astype(BF16)
        dp_ref[:, 2 * B:3 * B] = dz.astype(BF16)
        i = pl.program_id(0)

        @pl.when(i == 0)
        def _():
            dlw_ref[...] = dlw
            dlb_ref[...] = dlb
            dw_ref[...] = dW
            dbb_ref[...] = dbb

        @pl.when(i > 0)
        def _():
            dlw_ref[...] += dlw
            dlb_ref[...] += dlb
            dw_ref[...] += dW
            dbb_ref[...] += dbb

        @pl.when(i == nt - 1)
        def _():
            db_ref[...] = jnp.sum(dbb_ref[...], axis=-1, keepdims=True)

    row = pl.BlockSpec((1, B), lambda i: (0, 0))
    cube = pl.BlockSpec((G, CHUNK_B, CHUNK_B), lambda i: (0, 0, 0))
    return pl.pallas_call(
        body, name="sgu_bwd", grid=(nt,),
        in_specs=[pl.BlockSpec((tm, B), lambda i: (i, cu)), pl.BlockSpec((tm, B), lambda i: (i, cv)),
                  pl.BlockSpec((tm, B), lambda i: (i, cz)), row, row, cube, cube,
                  pl.BlockSpec((tm, B), lambda i: (i, A // B))],
        out_specs=[pl.BlockSpec((tm, 3 * B), lambda i: (i, 0)), row, row, cube,
                   pl.BlockSpec((G, CHUNK_B, 1), lambda i: (0, 0, 0))],
        out_shape=[jax.ShapeDtypeStruct((T, 3 * B), BF16), jax.ShapeDtypeStruct((1, B), F32),
                   jax.ShapeDtypeStruct((1, B), F32), jax.ShapeDtypeStruct((G, CHUNK_B, CHUNK_B), F32),
                   jax.ShapeDtypeStruct((G, CHUNK_B, 1), F32)],
        scratch_shapes=[pltpu.VMEM((G, CHUNK_B, CHUNK_B), F32)],
        compiler_params=_cparams(("arbitrary",)),
    )(proj_m, proj_m, proj_m, lw, lb, W, bbc, d_o)


def _head_fn(mix, x, fw, tgt):
    h = x + mix
    y = _rms_fn(h, fw)
    e = y - tgt
    return 0.5 * jnp.sum(jnp.mean(e * e, axis=-1, keepdims=True), axis=0, keepdims=True)


def _out_proj_loss(oa, ob, wout, x, tgt, fw):
    T, A = oa.shape
    B = ob.shape[1]
    D = x.shape[1]
    tm = _pick(T, (256, 128))

    def body(oa_ref, ob_ref, w_ref, x_ref, t_ref, fw_ref, dh_ref, dhb_ref, loss_ref, dfw_ref):
        mix = _dot(oa_ref[...], w_ref[0:A, :]) + _dot(ob_ref[...], w_ref[A:A + B, :])
        xv, tv = x_ref[...], t_ref[...]
        loss, vjp = jax.vjp(lambda m, f: _head_fn(m, xv, f, tv), mix, fw_ref[...])
        dh, dfw = vjp(jnp.ones((1, 1), F32))
        dh_ref[...] = dh
        dhb_ref[...] = dh.astype(BF16)
        lrow = jnp.broadcast_to(loss, (1, LANES))

        @pl.when(pl.program_id(0) == 0)
        def _():
            loss_ref[...] = lrow
            dfw_ref[...] = dfw

        @pl.when(pl.program_id(0) > 0)
        def _():
            loss_ref[...] += lrow
            dfw_ref[...] += dfw

    tile = pl.BlockSpec((tm, D), lambda i: (i, 0))
    return pl.pallas_call(
        body, name="out_proj_loss", grid=(T // tm,),
        in_specs=[pl.BlockSpec((tm, A), lambda i: (i, 0)), pl.BlockSpec((tm, B), lambda i: (i, 0)),
                  pl.BlockSpec((A + B, D), lambda i: (0, 0)), tile, tile,
                  pl.BlockSpec((1, D), lambda i: (0, 0))],
        out_specs=[tile, tile, pl.BlockSpec((1, LANES), lambda i: (0, 0)),
                   pl.BlockSpec((1, D), lambda i: (0, 0))],
        out_shape=[jax.ShapeDtypeStruct((T, D), F32), jax.ShapeDtypeStruct((T, D), BF16),
                   jax.ShapeDtypeStruct((1, LANES), F32), jax.ShapeDtypeStruct((1, D), F32)],
        compiler_params=_cparams(("arbitrary",)),
    )(oa, ob, wout, x, tgt, fw)


def _adamw(w, g, m, v, name):
    R, Cn = w.shape
    tr = R if R * Cn <= 512 * 1024 else _pick(R, (256, 128, 64, 32, 16, 8))

    def body(w_ref, g_ref, m_ref, v_ref, d_ref, mo_ref, vo_ref):
        g = g_ref[...]
        m = ADAM_B1 * m_ref[...] + (1.0 - ADAM_B1) * g
        v = ADAM_B2 * v_ref[...] + (1.0 - ADAM_B2) * jnp.square(g)
        m_hat = m / (1.0 - ADAM_B1 ** ADAM_STEP)
        v_hat = v / (1.0 - ADAM_B2 ** ADAM_STEP)
        d_ref[...] = -ADAM_LR * (m_hat / (jnp.sqrt(v_hat) + ADAM_EPS) + ADAM_WD * w_ref[...])
        mo_ref[...] = m
        vo_ref[...] = v

    tile = pl.BlockSpec((tr, Cn), lambda i: (i, 0))
    shape = jax.ShapeDtypeStruct((R, Cn), F32)
    return pl.pallas_call(
        body, name=name, grid=(R // tr,), in_specs=[tile] * 4, out_specs=[tile] * 3,
        out_shape=[shape] * 3, compiler_params=_cparams(("parallel",)),
    )(w, g, m, v)


def _place():
    x, y, c = lax.axis_index("x"), lax.axis_index("y"), lax.axis_index("c")
    others = [(1 - x, y), (x, 1 - y), (1 - x, 1 - y)]
    return x, y, c, others


def _chip_index(px, py):
    return 2 * px + py


ANY = pl.BlockSpec(memory_space=pl.ANY)


def _gather_weights(win_b, wout_b, conv_b):
    Din, Cb = win_b.shape
    Rb, D = wout_b.shape
    hi, ho = Din // 2, Rb // 2

    def body(win_ref, wout_ref, conv_ref, gin_ref, gout_ref, gconv_ref, send_sems, recv_sems, local_sems):
        x, y, c, others = _place()
        me = _chip_index(x, y)
        sibling = (x, y, 1 - c)
        local = [pltpu.make_async_copy(win_ref, gin_ref.at[me], local_sems.at[0]),
                 pltpu.make_async_copy(wout_ref, gout_ref.at[me], local_sems.at[1]),
                 pltpu.make_async_copy(conv_ref, gconv_ref.at[me], local_sems.at[2])]
        for cp in local:
            cp.start()

        def half(ref, chip, core, n):
            return ref.at[chip, pl.ds(core * n, n), :]

        def copy(sem, src, dst, to):
            return pltpu.make_async_remote_copy(src_ref=src, dst_ref=dst, send_sem=send_sems.at[sem],
                                                recv_sem=recv_sems.at[sem], device_id=to, device_id_type=MESH_ID)

        sends = []
        for j, chip in enumerate(others):
            to = (*chip, c)
            sends.append(copy(3 * j, win_ref.at[pl.ds(c * hi, hi), :], half(gin_ref, me, c, hi), to))
            sends.append(copy(3 * j + 1, wout_ref.at[pl.ds(c * ho, ho), :], half(gout_ref, me, c, ho), to))
            sends.append(copy(3 * j + 2, conv_ref, gconv_ref.at[me], to))
        for cp in sends:
            cp.start()
        passed = []
        for j, chip in enumerate(others):
            k = _chip_index(*chip)
            copy(3 * j, half(gin_ref, k, c, hi), half(gin_ref, k, c, hi), (*chip, c)).wait_recv()
            fw = copy(9 + 2 * j, half(gin_ref, k, c, hi), half(gin_ref, k, c, hi), sibling)
            fw.start()
            copy(3 * j + 1, half(gout_ref, k, c, ho), half(gout_ref, k, c, ho), (*chip, c)).wait_recv()
            fo = copy(10 + 2 * j, half(gout_ref, k, c, ho), half(gout_ref, k, c, ho), sibling)
            fo.start()
            copy(3 * j + 2, gconv_ref.at[k], gconv_ref.at[k], (*chip, c)).wait_recv()
            passed += [fw, fo]
        for j, chip in enumerate(others):
            k = _chip_index(*chip)
            copy(9 + 2 * j, half(gin_ref, k, 1 - c, hi), half(gin_ref, k, 1 - c, hi), sibling).wait_recv()
            copy(10 + 2 * j, half(gout_ref, k, 1 - c, ho), half(gout_ref, k, 1 - c, ho), sibling).wait_recv()
        for cp in sends + passed:
            cp.wait_send()
        for cp in local:
            cp.wait()

    return pl.pallas_call(
        body, name="gather_weights",
        in_specs=[ANY, ANY, ANY], out_specs=[ANY, ANY, ANY],
        out_shape=[jax.ShapeDtypeStruct((N_CHIPS, Din, Cb), win_b.dtype),
                   jax.ShapeDtypeStruct((N_CHIPS, Rb, D), wout_b.dtype),
                   jax.ShapeDtypeStruct((N_CHIPS,) + conv_b.shape, conv_b.dtype)],
        scratch_shapes=[pltpu.SemaphoreType.DMA((15,)), pltpu.SemaphoreType.DMA((15,)),
                        pltpu.SemaphoreType.DMA((3,))],
        compiler_params=pltpu.CompilerParams(has_side_effects=True),
    )(win_b, wout_b, conv_b)


def _allreduce_small(buf):
    R, L = buf.shape

    def body(in_ref, out_ref, sib_ref, pair_ref, chips_ref, send_sems, recv_sems):
        x, y, c, others = _place()
        me = _chip_index(x, y)
        sibling = (x, y, 1 - c)
        cp = pltpu.make_async_remote_copy(src_ref=in_ref, dst_ref=sib_ref, send_sem=send_sems.at[0],
                                          recv_sem=recv_sems.at[0], device_id=sibling, device_id_type=MESH_ID)
        cp.start()
        cp.wait()
        pair_ref[...] = in_ref[...] + sib_ref[...]
        sends = []
        for j, chip in enumerate(others):
            s = pltpu.make_async_remote_copy(src_ref=pair_ref, dst_ref=chips_ref.at[me],
                                             send_sem=send_sems.at[1 + j], recv_sem=recv_sems.at[1 + j],
                                             device_id=(*chip, c), device_id_type=MESH_ID)
            s.start()
            sends.append(s)
        chips_ref[me] = pair_ref[...]
        for j, chip in enumerate(others):
            k = _chip_index(*chip)
            pltpu.make_async_remote_copy(src_ref=pair_ref, dst_ref=chips_ref.at[k], send_sem=send_sems.at[1 + j],
                                         recv_sem=recv_sems.at[1 + j], device_id=(*chip, c),
                                         device_id_type=MESH_ID).wait_recv()
        for s in sends:
            s.wait_send()
        out_ref[...] = ((chips_ref[0] + chips_ref[1]) + chips_ref[2]) + chips_ref[3]

    vm = pl.BlockSpec(memory_space=pltpu.VMEM)
    return pl.pallas_call(
        body, name="allreduce_small", in_specs=[vm], out_specs=vm,
        out_shape=jax.ShapeDtypeStruct((R, L), F32),
        scratch_shapes=[pltpu.VMEM((R, L), F32), pltpu.VMEM((R, L), F32), pltpu.VMEM((N_CHIPS, R, L), F32),
                        pltpu.SemaphoreType.DMA((4,)), pltpu.SemaphoreType.DMA((4,))],
        compiler_params=pltpu.CompilerParams(vmem_limit_bytes=VMEM_LIMIT),
    )(buf)


def _pair_exchange(gw, go):
    _, Din, Cb = gw.shape
    _, Rb, D = go.shape
    hi, ho = Din // 2, Rb // 2

    def body(gw_ref, go_ref, lw_ref, lo_ref, send_sems, recv_sems):
        x, y, c, _ = _place()
        sibling = (x, y, 1 - c)
        a = pltpu.make_async_remote_copy(src_ref=gw_ref.at[:, pl.ds((1 - c) * hi, hi), :], dst_ref=lw_ref,
                                         send_sem=send_sems.at[0], recv_sem=recv_sems.at[0],
                                         device_id=sibling, device_id_type=MESH_ID)
        b = pltpu.make_async_remote_copy(src_ref=go_ref.at[:, pl.ds((1 - c) * ho, ho), :], dst_ref=lo_ref,
                                         send_sem=send_sems.at[1], recv_sem=recv_sems.at[1],
                                         device_id=sibling, device_id_type=MESH_ID)
        a.start()
        b.start()
        a.wait()
        b.wait()

    return pl.pallas_call(
        body, name="pair_exchange", in_specs=[ANY, ANY], out_specs=[ANY, ANY],
        out_shape=[jax.ShapeDtypeStruct((N_CHIPS, hi, Cb), gw.dtype),
                   jax.ShapeDtypeStruct((N_CHIPS, ho, D), go.dtype)],
        scratch_shapes=[pltpu.SemaphoreType.DMA((2,)), pltpu.SemaphoreType.DMA((2,))],
        compiler_params=pltpu.CompilerParams(has_side_effects=True),
    )(gw, go)


def _pair_sum(g, land, c_arr, name):
    nb, R, Cn = g.shape
    hr = R // 2
    tr = _pick(hr, (256, 128, 64, 32, 16))
    nt = hr // tr

    def body(c_ref, g_ref, l_ref, o_ref):
        o_ref[...] = (g_ref[...] + l_ref[...]).astype(BF16)

    return pl.pallas_call(
        body, name=name,
        grid_spec=pltpu.PrefetchScalarGridSpec(
            num_scalar_prefetch=1, grid=(nb, nt),
            in_specs=[pl.BlockSpec((1, tr, Cn), lambda b, i, c_ref: (b, c_ref[0] * nt + i, 0)),
                      pl.BlockSpec((1, tr, Cn), lambda b, i, c_ref: (b, i, 0))],
            out_specs=pl.BlockSpec((1, tr, Cn), lambda b, i, c_ref: (b, i, 0))),
        out_shape=jax.ShapeDtypeStruct((nb, hr, Cn), BF16),
        compiler_params=_cparams(("parallel", "parallel")),
    )(c_arr, g, land)


def _chip_exchange(pw, po):
    nb, hi, Cb = pw.shape
    _, ho, D = po.shape

    def body(pw_ref, po_ref, qw_ref, qo_ref, send_sems, recv_sems, local_sems):
        x, y, c, others = _place()
        me = _chip_index(x, y)
        local = [pltpu.make_async_copy(pw_ref.at[me], qw_ref.at[me], local_sems.at[0]),
                 pltpu.make_async_copy(po_ref.at[me], qo_ref.at[me], local_sems.at[1])]
        for cp in local:
            cp.start()
        sends = []
        for j, chip in enumerate(others):
            k = _chip_index(*chip)
            for n, (src, dst) in enumerate(((pw_ref, qw_ref), (po_ref, qo_ref))):
                s = pltpu.make_async_remote_copy(src_ref=src.at[k], dst_ref=dst.at[me],
                                                 send_sem=send_sems.at[2 * j + n], recv_sem=recv_sems.at[2 * j + n],
                                                 device_id=(*chip, c), device_id_type=MESH_ID)
                s.start()
                sends.append(s)
        for j, chip in enumerate(others):
            k = _chip_index(*chip)
            for n, (src, dst) in enumerate(((pw_ref, qw_ref), (po_ref, qo_ref))):
                pltpu.make_async_remote_copy(src_ref=src.at[k], dst_ref=dst.at[k],
                                             send_sem=send_sems.at[2 * j + n], recv_sem=recv_sems.at[2 * j + n],
                                             device_id=(*chip, c), device_id_type=MESH_ID).wait_recv()
        for s in sends:
            s.wait_send()
        for cp in local:
            cp.wait()

    return pl.pallas_call(
        body, name="chip_exchange", in_specs=[ANY, ANY], out_specs=[ANY, ANY],
        out_shape=[jax.ShapeDtypeStruct(pw.shape, pw.dtype), jax.ShapeDtypeStruct(po.shape, po.dtype)],
        scratch_shapes=[pltpu.SemaphoreType.DMA((6,)), pltpu.SemaphoreType.DMA((6,)),
                        pltpu.SemaphoreType.DMA((2,))],
        compiler_params=pltpu.CompilerParams(has_side_effects=True),
    )(pw, po)


def _chip_sum(q, name):
    nb, hr, Cn = q.shape
    tr = _pick(hr, (256, 128, 64, 32, 16))

    def body(q_ref, o_ref):
        f = lambda k: q_ref[k].astype(F32)
        o_ref[...] = ((f(0) + f(1)) + f(2)) + f(3)

    return pl.pallas_call(
        body, name=name, grid=(hr // tr,),
        in_specs=[pl.BlockSpec((nb, tr, Cn), lambda i: (0, i, 0))],
        out_specs=pl.BlockSpec((tr, Cn), lambda i: (i, 0)),
        out_shape=jax.ShapeDtypeStruct((hr, Cn), F32),
        compiler_params=_cparams(("parallel",)),
    )(q)


def _sibling_concat(rw, ro):
    hi, Cb = rw.shape
    ho, D = ro.shape

    def body(rw_ref, ro_ref, fw_ref, fo_ref, send_sems, recv_sems, local_sems):
        x, y, c, _ = _place()
        sibling = (x, y, 1 - c)
        mine_w = fw_ref.at[pl.ds(c * hi, hi), :]
        mine_o = fo_ref.at[pl.ds(c * ho, ho), :]
        local = [pltpu.make_async_copy(rw_ref, mine_w, local_sems.at[0]),
                 pltpu.make_async_copy(ro_ref, mine_o, local_sems.at[1])]
        for cp in local:
            cp.start()
        a = pltpu.make_async_remote_copy(src_ref=rw_ref, dst_ref=mine_w, send_sem=send_sems.at[0],
                                         recv_sem=recv_sems.at[0], device_id=sibling, device_id_type=MESH_ID)
        b = pltpu.make_async_remote_copy(src_ref=ro_ref, dst_ref=mine_o, send_sem=send_sems.at[1],
                                         recv_sem=recv_sems.at[1], device_id=sibling, device_id_type=MESH_ID)
        a.start()
        b.start()
        theirs_w = fw_ref.at[pl.ds((1 - c) * hi, hi), :]
        theirs_o = fo_ref.at[pl.ds((1 - c) * ho, ho), :]
        pltpu.make_async_remote_copy(src_ref=rw_ref, dst_ref=theirs_w, send_sem=send_sems.at[0],
                                     recv_sem=recv_sems.at[0], device_id=sibling, device_id_type=MESH_ID).wait_recv()
        pltpu.make_async_remote_copy(src_ref=ro_ref, dst_ref=theirs_o, send_sem=send_sems.at[1],
                                     recv_sem=recv_sems.at[1], device_id=sibling, device_id_type=MESH_ID).wait_recv()
        a.wait_send()
        b.wait_send()
        for cp in local:
            cp.wait()

    return pl.pallas_call(
        body, name="sibling_concat", in_specs=[ANY, ANY], out_specs=[ANY, ANY],
        out_shape=[jax.ShapeDtypeStruct((2 * hi, Cb), F32), jax.ShapeDtypeStruct((2 * ho, D), F32)],
        scratch_shapes=[pltpu.SemaphoreType.DMA((2,)), pltpu.SemaphoreType.DMA((2,)),
                        pltpu.SemaphoreType.DMA((2,))],
        compiler_params=pltpu.CompilerParams(has_side_effects=True),
    )(rw, ro)


def _pad_lanes(a, n=LANES):
    return jnp.pad(a, ((0, 0), (0, n - a.shape[1])))


def _local_grads(x, tgt, norm_w, w_full, conv_w, a_log, dt_bias, head_norm_w, sgu_ln_w, sgu_ln_b,
                 w_spatial, b_spatial, wout, final_norm_w):
    T, D = x.shape
    H = a_log.shape[1]
    A = H * HEAD_DIM
    G = w_spatial.shape[0]
    B = G * HEAD_DIM
    c_ba = 4 * A
    w_main = jnp.concatenate([w_full[:, :c_ba], w_full[:, c_ba + 2 * H:]], axis=1)
    w_ba = _pad_lanes(w_full[:, c_ba:c_ba + 2 * H])
    wt_all = jnp.concatenate([w_main, w_ba], axis=1).T
    alog_row = jnp.pad(a_log, ((0, 0), (H, LANES - 2 * H)))
    dtb_row = jnp.pad(dt_bias, ((0, 0), (H, LANES - 2 * H)))
    bbc = jnp.broadcast_to(b_spatial[:, :, None], (G, CHUNK_B, CHUNK_B))

    xn = _rms_in(x, norm_w)
    proj_m = _mm_nn(xn, w_main, F32, "in_proj")
    proj_ba = _mm_nn(xn, w_ba, F32, "in_proj_ba")
    q, k, v, gb, bb = _gdn_pre(proj_m, proj_ba, conv_w, alog_row, dtb_row, H)
    u, w, qg, kd, attn, eg = _gdn_prep(q, k, v, gb, bb)
    og, sall = _gdn_chain(qg, kd, u, w, attn, eg)
    oa = _gdn_post(og, proj_m, head_norm_w)
    ob = _sgu_fwd(proj_m, sgu_ln_w, sgu_ln_b, w_spatial, bbc, A)
    dh, dhb, loss_row, d_fnw = _out_proj_loss(oa, ob, wout, x, tgt, final_norm_w.reshape(1, D))

    d_o = _mm_nn(dhb, wout.T, F32, "out_proj_dx")
    d_wout = jnp.concatenate([_mm_tn(oa, dhb, "out_proj_dw_a"), _mm_tn(ob, dhb, "out_proj_dw_b")], axis=0)
    dpb, d_lw, d_lb, d_ws, d_bs = _sgu_bwd(proj_m, sgu_ln_w, sgu_ln_b, w_spatial, bbc, d_o, A)
    dog, dza, d_hw = _gdn_post_bwd(og, proj_m, head_norm_w, d_o)
    dqg, dkd, du, dw, dat, deg = _gdn_chain_bwd(qg, kd, u, w, attn, eg, sall, dog)
    dq, dk, dv, dgb, dbb = _gdn_prep_bwd(q, k, v, gb, bb, du, dw, dqg, dkd, dat, deg)
    dc, dba, d_al, d_dt = _gdn_pre_bwd(proj_m, proj_ba, conv_w, alog_row, dtb_row, dq, dk, dv, dgb, dbb, H)
    dqkv, d_conv = _conv_bwd(proj_m, dc, conv_w, H)
    dproj = jnp.concatenate([dqkv, dza, dpb, dba], axis=1)
    dxn = _mm_nn(dproj, wt_all, F32, "in_proj_dx", tm=512, tn=512, tk=2432)
    grad_x, d_nw = _rms_in_bwd(x, norm_w, dxn, dh)
    d_wall = _mm_tn(xn, dproj, "in_proj_dw")
    nm = w_main.shape[1]
    d_win = jnp.concatenate([d_wall[:, :c_ba], d_wall[:, nm:nm + 2 * H], d_wall[:, c_ba:nm]], axis=1)
    grads = dict(norm_w=d_nw, w_in=d_win, conv_w=d_conv[:CONV_WIDTH], a_log=d_al[:, H:2 * H],
                 dt_bias=d_dt[:, H:2 * H], head_norm_w=d_hw, sgu_ln_w=d_lw, sgu_ln_b=d_lb, w_spatial=d_ws,
                 b_spatial=d_bs[:, :, 0], w_out=d_wout, final_norm_w=d_fnw)
    return loss_row, grad_x, grads


SMALL = ("norm_w", "conv_w", "a_log", "dt_bias", "head_norm_w", "sgu_ln_w", "sgu_ln_b", "w_spatial",
         "b_spatial", "final_norm_w")


def _pack(parts):
    rows = []
    for p in parts:
        f = p.reshape(-1)
        f = jnp.pad(f, (0, (-f.shape[0]) % (8 * LANES)))
        rows.append(f.reshape(-1, LANES))
    return jnp.concatenate(rows, axis=0)


def _unpack(buf, shapes):
    out, r = [], 0
    for s in shapes:
        n = 1
        for d in s:
            n *= d
        nr = -(-n // (8 * LANES)) * 8
        out.append(buf[r:r + nr].reshape(-1)[:n].reshape(s))
        r += nr
    return out


def kernel(x, norm_w, w_in, conv_w, a_log, dt_bias, head_norm_w, sgu_ln_w, sgu_ln_b, w_spatial, b_spatial, w_out, final_norm_w, loss_target, m_norm_w, m_w_in, m_conv_w, m_a_log, m_dt_bias, m_head_norm_w, m_sgu_ln_w, m_sgu_ln_b, m_w_spatial, m_b_spatial, m_w_out, m_final_norm_w, v_norm_w, v_w_in, v_conv_w, v_a_log, v_dt_bias, v_head_norm_w, v_sgu_ln_w, v_sgu_ln_b, v_w_spatial, v_b_spatial, v_w_out, v_final_norm_w):
    T, D = x.shape[1], x.shape[2]
    weights = dict(norm_w=norm_w, w_in=w_in, conv_w=conv_w, a_log=a_log, dt_bias=dt_bias, head_norm_w=head_norm_w,
                   sgu_ln_w=sgu_ln_w, sgu_ln_b=sgu_ln_b, w_spatial=w_spatial, b_spatial=b_spatial, w_out=w_out,
                   final_norm_w=final_norm_w)
    mom_m = dict(norm_w=m_norm_w, w_in=m_w_in, conv_w=m_conv_w, a_log=m_a_log, dt_bias=m_dt_bias,
                 head_norm_w=m_head_norm_w, sgu_ln_w=m_sgu_ln_w, sgu_ln_b=m_sgu_ln_b, w_spatial=m_w_spatial,
                 b_spatial=m_b_spatial, w_out=m_w_out, final_norm_w=m_final_norm_w)
    mom_v = dict(norm_w=v_norm_w, w_in=v_w_in, conv_w=v_conv_w, a_log=v_a_log, dt_bias=v_dt_bias,
                 head_norm_w=v_head_norm_w, sgu_ln_w=v_sgu_ln_w, sgu_ln_b=v_sgu_ln_b, w_spatial=v_w_spatial,
                 b_spatial=v_b_spatial, w_out=v_w_out, final_norm_w=v_final_norm_w)
    me = _chip_index(lax.axis_index("x"), lax.axis_index("y"))
    c_arr = lax.axis_index("c").astype(jnp.int32).reshape(1)
    Din, Cb = w_in.shape[1], w_in.shape[2]
    Rb = w_out.shape[1]
    cconv = conv_w.shape[2]

    g_in, g_out, g_conv = _gather_weights(w_in[0].astype(BF16), w_out[0].astype(BF16), conv_w[0])
    w_full = g_in.transpose(1, 0, 2).reshape(Din, N_CHIPS * Cb)
    wout_full = g_out.reshape(N_CHIPS * Rb, D)
    conv_full = g_conv.transpose(1, 0, 2).reshape(CONV_WIDTH, N_CHIPS * cconv)

    loss_row, grad_x, g = _local_grads(
        x[0], loss_target[0], norm_w, w_full, conv_full, a_log, dt_bias, head_norm_w, sgu_ln_w, sgu_ln_b,
        w_spatial[0], b_spatial[0], wout_full, final_norm_w)

    gw = g["w_in"].reshape(Din, N_CHIPS, Cb).transpose(1, 0, 2)
    go = g["w_out"].reshape(N_CHIPS, Rb, D)
    lw, lo = _pair_exchange(gw, go)
    qw, qo = _chip_exchange(_pair_sum(gw, lw, c_arr, "pair_sum_w_in"), _pair_sum(go, lo, c_arr, "pair_sum_w_out"))
    gsum_in, gsum_out = _sibling_concat(_chip_sum(qw, "chip_sum_w_in"), _chip_sum(qo, "chip_sum_w_out"))
    small_shapes = [tuple(g[n].shape) for n in SMALL]
    small = _unpack(_allreduce_small(_pack([g[n] for n in SMALL])), small_shapes)
    gsmall = dict(zip(SMALL, small))
    gsmall["conv_w"] = lax.dynamic_slice_in_dim(gsmall["conv_w"], me * cconv, cconv, axis=1)

    grads, deltas, new_m, new_v = {}, {}, {}, {}
    for n, gs in (("w_in", gsum_in), ("w_out", gsum_out)):
        d, m2, v2 = _adamw(weights[n][0], gs, mom_m[n][0], mom_v[n][0], "adamw_" + n)
        grads[n], deltas[n], new_m[n], new_v[n] = gs[None], d[None], m2[None], v2[None]
    shapes = [tuple(weights[n].shape) for n in SMALL]
    ds, ms, vs = _adamw(_pack([weights[n] for n in SMALL]), _pack([gsmall[n] for n in SMALL]),
                        _pack([mom_m[n] for n in SMALL]), _pack([mom_v[n] for n in SMALL]), "adamw_small")
    for n, gq, d, m2, v2 in zip(SMALL, [gsmall[n] for n in SMALL], _unpack(ds, shapes), _unpack(ms, shapes),
                                _unpack(vs, shapes)):
        grads[n], deltas[n], new_m[n], new_v[n] = gq.reshape(weights[n].shape), d, m2, v2

    loss = lax.psum(loss_row[0, 0], ("x", "y", "c"))
    order = ("norm_w", "w_in", "conv_w", "a_log", "dt_bias", "head_norm_w", "sgu_ln_w", "sgu_ln_b", "w_spatial",
             "b_spatial", "w_out", "final_norm_w")
    return (loss, grad_x[None], *[grads[n] for n in order], *[deltas[n] for n in order],
            *[new_m[n] for n in order], *[new_v[n] for n in order])
```

```python
import functools

import jax
import jax.numpy as jnp
from jax import lax
from jax.experimental import pallas as pl
from jax.experimental.pallas import tpu as pltpu

F32 = jnp.float32
BF16 = jnp.bfloat16
EPS = 1e-6
HEAD_DIM = 128
CHUNK_A = 64
CHUNK_B = 128
CONV_WIDTH = 4
LANES = 128
HALO = 8
N_CHIPS = 4
ADAM_LR = 0.001
ADAM_B1 = 0.9
ADAM_B2 = 0.999
ADAM_EPS = 1e-08
ADAM_WD = 0.01
ADAM_STEP = 10
VMEM_LIMIT = 56 * 1024 * 1024
MESH_ID = pl.DeviceIdType.MESH
HI = lax.Precision.HIGHEST


def _cparams(sem=None, **kw):
    return pltpu.CompilerParams(dimension_semantics=sem, vmem_limit_bytes=VMEM_LIMIT, **kw)


def _dot(a, b, hi=False):
    return lax.dot_general(a, b, (((1,), (0,)), ((), ())), precision=HI if hi else None,
                           preferred_element_type=F32)


def _dot_nt(a, b, hi=False):
    return lax.dot_general(a, b, (((1,), (1,)), ((), ())), precision=HI if hi else None,
                           preferred_element_type=F32)


def _dot_tn(a, b, hi=False):
    return lax.dot_general(a, b, (((0,), (0,)), ((), ())), precision=HI if hi else None,
                           preferred_element_type=F32)


def _iota(shape, dim):
    return lax.broadcasted_iota(jnp.int32, shape, dim)


def _sigmoid(x):
    return 0.5 * (jnp.tanh(0.5 * x) + 1.0)


def _silu(x):
    return x * _sigmoid(x)


def _softplus(x):
    z = jnp.exp(-jnp.abs(x))
    small = z * (1.0 - z * (0.5 - z * (1.0 / 3.0)))
    return jnp.maximum(x, 0.0) + jnp.where(z < 1e-3, small, jnp.log(1.0 + z))


def _pick(n, pref):
    for t in pref:
        if n % t == 0:
            return t
    return n


def _mm_nn(a, b, out_dtype, name, tm=1024, tn=512, tk=None):
    M, K = a.shape
    _, N = b.shape
    tm = _pick(M, (tm, 512, 256, 128))
    tn = _pick(N, (tn, 512, 384, 256, 128))
    tk = K if tk is None else _pick(K, (tk,))
    nk = K // tk

    def body(a_ref, b_ref, o_ref, *scratch):
        part = _dot(a_ref[...], b_ref[...])
        if nk == 1:
            o_ref[...] = part.astype(out_dtype)
        else:
            acc_ref, = scratch
            k = pl.program_id(2)

            @pl.when(k == 0)
            def _():
                acc_ref[...] = part

            @pl.when(k > 0)
            def _():
                acc_ref[...] += part

            @pl.when(k == nk - 1)
            def _():
                o_ref[...] = acc_ref[...].astype(out_dtype)

    return pl.pallas_call(
        body, name=name, grid=(M // tm, N // tn, nk),
        in_specs=[pl.BlockSpec((tm, tk), lambda i, j, k: (i, k)),
                  pl.BlockSpec((tk, tn), lambda i, j, k: (k, j))],
        out_specs=pl.BlockSpec((tm, tn), lambda i, j, k: (i, j)),
        out_shape=jax.ShapeDtypeStruct((M, N), out_dtype),
        scratch_shapes=[] if nk == 1 else [pltpu.VMEM((tm, tn), F32)],
        compiler_params=_cparams(("parallel", "parallel", "arbitrary")),
    )(a, b)


def _mm_tn(a, b, name, tm=512, tn=512, tk=1024):
    K, M = a.shape
    _, N = b.shape
    tm = _pick(M, (tm, 256, 128))
    tn = _pick(N, (tn, 384, 256, 128))
    tk = _pick(K, (tk, 512, 256))
    nk = K // tk

    def body(a_ref, b_ref, o_ref):
        part = _dot_tn(a_ref[...], b_ref[...])
        k = pl.program_id(2)

        @pl.when(k == 0)
        def _():
            o_ref[...] = part

        @pl.when(k > 0)
        def _():
            o_ref[...] += part

    return pl.pallas_call(
        body, name=name, grid=(M // tm, N // tn, nk),
        in_specs=[pl.BlockSpec((tk, tm), lambda i, j, k: (k, i)),
                  pl.BlockSpec((tk, tn), lambda i, j, k: (k, j))],
        out_specs=pl.BlockSpec((tm, tn), lambda i, j, k: (i, j)),
        out_shape=jax.ShapeDtypeStruct((M, N), F32),
        compiler_params=_cparams(("parallel", "parallel", "arbitrary")),
    )(a, b)


def _rms_fn(x, w):
    r = lax.rsqrt(jnp.mean(x * x, axis=-1, keepdims=True) + EPS)
    return x * r * w


def _rms_in(x, w):
    T, D = x.shape
    tm = _pick(T, (512, 256, 128))

    def body(x_ref, w_ref, o_ref):
        o_ref[...] = _rms_fn(x_ref[...], w_ref[...]).astype(BF16)

    return pl.pallas_call(
        body, name="rms_in", grid=(T // tm,),
        in_specs=[pl.BlockSpec((tm, D), lambda i: (i, 0)), pl.BlockSpec((1, D), lambda i: (0, 0))],
        out_specs=pl.BlockSpec((tm, D), lambda i: (i, 0)),
        out_shape=jax.ShapeDtypeStruct((T, D), BF16),
        compiler_params=_cparams(("parallel",)),
    )(x, w)


def _rms_in_bwd(x, w, dxn, dh):
    T, D = x.shape
    tm = _pick(T, (256, 128))

    def body(x_ref, w_ref, dxn_ref, dh_ref, gx_ref, dw_ref):
        _, vjp = jax.vjp(_rms_fn, x_ref[...], w_ref[...])
        dx, dw = vjp(dxn_ref[...])
        gx_ref[...] = dh_ref[...] + dx

        @pl.when(pl.program_id(0) == 0)
        def _():
            dw_ref[...] = dw

        @pl.when(pl.program_id(0) > 0)
        def _():
            dw_ref[...] += dw

    tile = pl.BlockSpec((tm, D), lambda i: (i, 0))
    row = pl.BlockSpec((1, D), lambda i: (0, 0))
    return pl.pallas_call(
        body, name="rms_in_bwd", grid=(T // tm,),
        in_specs=[tile, row, tile, tile], out_specs=[tile, row],
        out_shape=[jax.ShapeDtypeStruct((T, D), F32), jax.ShapeDtypeStruct((1, D), F32)],
        compiler_params=_cparams(("arbitrary",)),
    )(x, w, dxn, dh)


def _conv_fwd(xcat, w, tm):
    c = None
    for k in range(CONV_WIDTH):
        s = CONV_WIDTH - 1 - k
        xs = xcat if s == 0 else pltpu.roll(xcat, s, 0)
        term = xs[HALO:, :] * w[k:k + 1, :]
        c = term if c is None else c + term
    return c


def _gdn_pointwise(c, ba, alog, dtb, H):
    A = H * HEAD_DIM
    s = _silu(c)
    beta = _sigmoid(ba)
    g = -jnp.exp(alog) * _softplus(ba + dtb)
    sel_row = _iota((LANES, LANES), 0)
    qs, ks, vs, gbs, bbs = [], [], [], [], []
    for h in range(H):
        lo = h * HEAD_DIM
        q = s[:, lo:lo + HEAD_DIM]
        k = s[:, A + lo:A + lo + HEAD_DIM]
        qs.append(q * lax.rsqrt(jnp.sum(q * q, axis=-1, keepdims=True) + EPS))
        ks.append(k * lax.rsqrt(jnp.sum(k * k, axis=-1, keepdims=True) + EPS))
        vs.append(s[:, 2 * A + lo:2 * A + lo + HEAD_DIM])
        bbs.append(_dot(beta, (sel_row == h).astype(F32), hi=True))
        gbs.append(_dot(g, (sel_row == H + h).astype(F32), hi=True))
    st = lambda xs: jnp.stack(xs, axis=0)
    return st(qs), st(ks), st(vs), st(gbs), st(bbs)


def _halo_prev(tm):
    return lambda i: (jnp.maximum(i * (tm // HALO) - 1, 0), 0)


def _gdn_pre(proj_m, proj_ba, conv_w, alog_row, dtb_row, H):
    T = proj_m.shape[0]
    A = H * HEAD_DIM
    tm = _pick(T, (256, 128))
    hs = pl.BlockSpec((H, tm, HEAD_DIM), lambda i: (0, i, 0))
    hshape = jax.ShapeDtypeStruct((H, T, HEAD_DIM), F32)

    def body(x_ref, halo_ref, ba_ref, w_ref, al_ref, dt_ref, q_ref, k_ref, v_ref, gb_ref, bb_ref):
        halo = jnp.where(pl.program_id(0) == 0, 0.0, halo_ref[...])
        c = _conv_fwd(jnp.concatenate([halo, x_ref[...]], axis=0), w_ref[...], tm)
        q, k, v, gb, bb = _gdn_pointwise(c, ba_ref[...], al_ref[...], dt_ref[...], H)
        q_ref[...] = q
        k_ref[...] = k
        v_ref[...] = v
        gb_ref[...] = gb
        bb_ref[...] = bb

    return pl.pallas_call(
        body, name="gdn_pre", grid=(T // tm,),
        in_specs=[pl.BlockSpec((tm, 3 * A), lambda i: (i, 0)),
                  pl.BlockSpec((HALO, 3 * A), _halo_prev(tm)),
                  pl.BlockSpec((tm, LANES), lambda i: (i, 0)),
                  pl.BlockSpec((CONV_WIDTH, 3 * A), lambda i: (0, 0)),
                  pl.BlockSpec((1, LANES), lambda i: (0, 0)),
                  pl.BlockSpec((1, LANES), lambda i: (0, 0))],
        out_specs=[hs] * 5, out_shape=[hshape] * 5,
        compiler_params=_cparams(("parallel",)),
    )(proj_m, proj_m, proj_ba, conv_w, alog_row, dtb_row)


def _gdn_pre_bwd(proj_m, proj_ba, conv_w, alog_row, dtb_row, dq, dk, dv, dgb, dbb, H):
    T = proj_m.shape[0]
    A = H * HEAD_DIM
    tm = _pick(T, (256, 128))
    hs = pl.BlockSpec((H, tm, HEAD_DIM), lambda i: (0, i, 0))
    row = pl.BlockSpec((1, LANES), lambda i: (0, 0))

    def body(x_ref, halo_ref, ba_ref, w_ref, al_ref, dt_ref, dq_ref, dk_ref, dv_ref, dgb_ref, dbb_ref,
             dc_ref, dba_ref, dal_ref, ddt_ref):
        halo = jnp.where(pl.program_id(0) == 0, 0.0, halo_ref[...])
        c = _conv_fwd(jnp.concatenate([halo, x_ref[...]], axis=0), w_ref[...], tm)
        _, vjp = jax.vjp(functools.partial(_gdn_pointwise, H=H), c, ba_ref[...], al_ref[...], dt_ref[...])
        dc, dba, dal, ddt = vjp((dq_ref[...], dk_ref[...], dv_ref[...], dgb_ref[...], dbb_ref[...]))
        dc_ref[...] = dc
        dba_ref[...] = dba.astype(BF16)

        @pl.when(pl.program_id(0) == 0)
        def _():
            dal_ref[...] = dal
            ddt_ref[...] = ddt

        @pl.when(pl.program_id(0) > 0)
        def _():
            dal_ref[...] += dal
            ddt_ref[...] += ddt

    return pl.pallas_call(
        body, name="gdn_pre_bwd", grid=(T // tm,),
        in_specs=[pl.BlockSpec((tm, 3 * A), lambda i: (i, 0)),
                  pl.BlockSpec((HALO, 3 * A), _halo_prev(tm)),
                  pl.BlockSpec((tm, LANES), lambda i: (i, 0)),
                  pl.BlockSpec((CONV_WIDTH, 3 * A), lambda i: (0, 0)),
                  row, row, hs, hs, hs, hs, hs],
        out_specs=[pl.BlockSpec((tm, 3 * A), lambda i: (i, 0)),
                   pl.BlockSpec((tm, LANES), lambda i: (i, 0)), row, row],
        out_shape=[jax.ShapeDtypeStruct((T, 3 * A), F32), jax.ShapeDtypeStruct((T, LANES), BF16),
                   jax.ShapeDtypeStruct((1, LANES), F32), jax.ShapeDtypeStruct((1, LANES), F32)],
        compiler_params=_cparams(("arbitrary",)),
    )(proj_m, proj_m, proj_ba, conv_w, alog_row, dtb_row, dq, dk, dv, dgb, dbb)


def _conv_bwd(proj_m, dc, conv_w, H):
    T = proj_m.shape[0]
    A = H * HEAD_DIM
    tm = _pick(T, (256, 128))
    nt = T // tm

    def body(x_ref, halo_ref, dc_ref, nxt_ref, w_ref, dx_ref, dw_ref):
        i = pl.program_id(0)
        halo = jnp.where(i == 0, 0.0, halo_ref[...])
        xcat = jnp.concatenate([halo, x_ref[...]], axis=0)
        nxt = jnp.where(i == nt - 1, 0.0, nxt_ref[...])
        dc = dc_ref[...]
        dcat = jnp.concatenate([dc, nxt], axis=0)
        w = w_ref[...]
        dx = None
        rows = []
        for k in range(CONV_WIDTH):
            s = CONV_WIDTH - 1 - k
            ds = dcat if s == 0 else pltpu.roll(dcat, tm + HALO - s, 0)
            term = ds[:tm, :] * w[k:k + 1, :]
            dx = term if dx is None else dx + term
            xs = xcat if s == 0 else pltpu.roll(xcat, s, 0)
            rows.append(jnp.sum(dc * xs[HALO:, :], axis=0, keepdims=True))
        dx_ref[...] = dx.astype(BF16)
        dw = jnp.concatenate(rows + [jnp.zeros((HALO - CONV_WIDTH, 3 * A), F32)], axis=0)

        @pl.when(i == 0)
        def _():
            dw_ref[...] = dw

        @pl.when(i > 0)
        def _():
            dw_ref[...] += dw

    return pl.pallas_call(
        body, name="conv_bwd", grid=(nt,),
        in_specs=[pl.BlockSpec((tm, 3 * A), lambda i: (i, 0)),
                  pl.BlockSpec((HALO, 3 * A), _halo_prev(tm)),
                  pl.BlockSpec((tm, 3 * A), lambda i: (i, 0)),
                  pl.BlockSpec((HALO, 3 * A), lambda i: (jnp.minimum((i + 1) * (tm // HALO), T // HALO - 1), 0)),
                  pl.BlockSpec((CONV_WIDTH, 3 * A), lambda i: (0, 0))],
        out_specs=[pl.BlockSpec((tm, 3 * A), lambda i: (i, 0)),
                   pl.BlockSpec((HALO, 3 * A), lambda i: (0, 0))],
        out_shape=[jax.ShapeDtypeStruct((T, 3 * A), BF16), jax.ShapeDtypeStruct((HALO, 3 * A), F32)],
        compiler_params=_cparams(("arbitrary",)),
    )(proj_m, proj_m, dc, dc, conv_w)


PAIR = 2 * CHUNK_A


def _b(x):
    return x.astype(BF16)


@jax.custom_vjp
def _bdot(a, b):
    return _dot(_b(a), _b(b))


def _bdot_f(a, b):
    return _bdot(a, b), (a, b)


def _bdot_b(res, g):
    a, b = res
    return _dot_nt(_b(g), _b(b)), _dot_tn(_b(a), _b(g))


_bdot.defvjp(_bdot_f, _bdot_b)


@jax.custom_vjp
def _bdot_nt(a, b):
    return _dot_nt(_b(a), _b(b))


def _bdot_nt_f(a, b):
    return _bdot_nt(a, b), (a, b)


def _bdot_nt_b(res, g):
    a, b = res
    return _dot(_b(g), _b(b)), _dot_tn(_b(g), _b(a))


_bdot_nt.defvjp(_bdot_nt_f, _bdot_nt_b)


@jax.custom_vjp
def _bdot_tn(a, b):
    return _dot_tn(_b(a), _b(b))


def _bdot_tn_f(a, b):
    return _bdot_tn(a, b), (a, b)


def _bdot_tn_b(res, g):
    a, b = res
    return _dot_nt(_b(b), _b(g)), _dot(_b(a), _b(g))


_bdot_tn.defvjp(_bdot_tn_f, _bdot_tn_b)


def _mask_matmul(m, x):
    hi = _b(x)
    r = x - hi.astype(F32)
    mid = _b(r)
    lo = _b(r - mid.astype(F32))
    return (_dot(m, lo) + _dot(m, mid)) + _dot(m, hi)


@jax.custom_vjp
def _mask_dot(m, mt, x):
    return _mask_matmul(m, x)


def _mask_dot_f(m, mt, x):
    return _mask_matmul(m, x), (m, mt)


def _mask_dot_b(res, g):
    m, mt = res
    return jnp.zeros_like(m), jnp.zeros_like(mt), _mask_matmul(mt, g)


_mask_dot.defvjp(_mask_dot_f, _mask_dot_b)


def _hdot(a, b):
    return lax.dot_general(a, b, (((1,), (0,)), ((), ())), precision=lax.Precision.HIGH,
                           preferred_element_type=F32)


def _gdn_prep_fn(q, k, v, gb, bb):
    n = PAIR
    row, col = _iota((n, n), 0), _iota((n, n), 1)
    same = (row >= CHUNK_A) == (col >= CHUNK_A)
    incl = same & (row >= col)
    strict = same & (row > col)
    tril, triu = _b(incl.astype(F32)), _b((same & (row <= col)).astype(F32))
    ones = _b(same.astype(F32))
    gc = _mask_dot(tril, triu, gb)
    gl = _mask_dot(ones, ones, gb)
    decay = jnp.where(incl, jnp.exp(jnp.where(incl, gc - gc.T, 0.0)), 0.0)
    kb = k * bb
    vb = v * bb
    qs = q * (HEAD_DIM ** -0.5)
    L = jnp.where(strict, _bdot_nt(kb, k) * decay, 0.0)
    X = -L
    P = (row == col).astype(F32) + X
    for _ in range(CHUNK_A.bit_length() - 2):
        X = _hdot(X, X)
        P = P + _hdot(P, X)
    egc = jnp.exp(gc)
    u = _bdot(P, vb)
    w = _bdot(P, kb * egc)
    attn = jnp.where(incl, _bdot_nt(qs, k) * decay, 0.0)
    qg = qs * egc
    kdec = k * jnp.exp(gl - gc)
    eg = jnp.exp(gl)
    return u, w, qg, kdec, attn, eg


def _gdn_chain_fn(S, qg, kdec, u, w, attn, eg):
    C = CHUNK_A
    a, b = slice(0, C), slice(C, PAIR)
    vn_a = u[a] - _bdot(w[a], S)
    o_a = _bdot(qg[a], S) + _bdot(attn[a], jnp.concatenate([vn_a, jnp.zeros_like(vn_a)], axis=0))
    S1 = S * jnp.concatenate([eg[a], eg[a]], axis=0) + _bdot_tn(kdec[a], vn_a)
    vn_b = u[b] - _bdot(w[b], S1)
    o_b = _bdot(qg[b], S1) + _bdot(attn[b], jnp.concatenate([vn_a, vn_b], axis=0))
    S2 = S1 * jnp.concatenate([eg[b], eg[b]], axis=0) + _bdot_tn(kdec[b], vn_b)
    return jnp.concatenate([o_a, o_b], axis=0), S2


def _gdn_prep(q, k, v, gb, bb):
    H, T, _ = q.shape
    pb = _pick(T // PAIR, (2, 1))
    hs = pl.BlockSpec((1, PAIR * pb, HEAD_DIM), lambda h, n: (h, n, 0))
    hshape = jax.ShapeDtypeStruct((H, T, HEAD_DIM), F32)

    def body(q_ref, k_ref, v_ref, gb_ref, bb_ref, *out_refs):
        for j in range(pb):
            sl = pl.ds(j * PAIR, PAIR)
            outs = _gdn_prep_fn(q_ref[0, sl, :], k_ref[0, sl, :], v_ref[0, sl, :], gb_ref[0, sl, :],
                                bb_ref[0, sl, :])
            for ref, val in zip(out_refs, outs):
                ref[0, sl, :] = val

    return pl.pallas_call(
        body, name="gdn_prep", grid=(H, T // (PAIR * pb)),
        in_specs=[hs] * 5, out_specs=[hs] * 6, out_shape=[hshape] * 6,
        compiler_params=_cparams(("parallel", "parallel")),
    )(q, k, v, gb, bb)


def _gdn_prep_bwd(q, k, v, gb, bb, du, dw, dqg, dkd, dat, deg):
    H, T, _ = q.shape
    pb = _pick(T // PAIR, (2, 1))
    hs = pl.BlockSpec((1, PAIR * pb, HEAD_DIM), lambda h, n: (h, n, 0))
    hshape = jax.ShapeDtypeStruct((H, T, HEAD_DIM), F32)

    def body(*refs):
        in_refs, ct_refs, out_refs = refs[:5], refs[5:11], refs[11:]
        for j in range(pb):
            sl = pl.ds(j * PAIR, PAIR)
            _, vjp = jax.vjp(_gdn_prep_fn, *[r[0, sl, :] for r in in_refs])
            grads = vjp(tuple(r[0, sl, :] for r in ct_refs))
            for ref, val in zip(out_refs, grads):
                ref[0, sl, :] = val

    return pl.pallas_call(
        body, name="gdn_prep_bwd", grid=(H, T // (PAIR * pb)),
        in_specs=[hs] * 11, out_specs=[hs] * 5, out_shape=[hshape] * 5,
        compiler_params=_cparams(("parallel", "parallel")),
    )(q, k, v, gb, bb, du, dw, dqg, dkd, dat, deg)


def _gdn_chain(qg, kd, u, w, attn, eg):
    H, T, _ = qg.shape
    N = T // PAIR
    hs = pl.BlockSpec((H, PAIR, HEAD_DIM), lambda n: (0, n, 0))
    ss = pl.BlockSpec((1, H, HEAD_DIM, HEAD_DIM), lambda n: (n, 0, 0, 0))

    def body(qg_ref, kd_ref, u_ref, w_ref, at_ref, eg_ref, o_ref, sall_ref, s_ref):
        @pl.when(pl.program_id(0) == 0)
        def _():
            s_ref[...] = jnp.zeros_like(s_ref)

        for h in range(H):
            S = s_ref[h]
            sall_ref[0, h] = S
            o, S2 = _gdn_chain_fn(S, qg_ref[h], kd_ref[h], u_ref[h], w_ref[h], at_ref[h], eg_ref[h])
            o_ref[h] = o
            s_ref[h] = S2

    return pl.pallas_call(
        body, name="gdn_chain", grid=(N,),
        in_specs=[hs] * 6, out_specs=[hs, ss],
        out_shape=[jax.ShapeDtypeStruct((H, T, HEAD_DIM), F32),
                   jax.ShapeDtypeStruct((N, H, HEAD_DIM, HEAD_DIM), F32)],
        scratch_shapes=[pltpu.VMEM((H, HEAD_DIM, HEAD_DIM), F32)],
        compiler_params=_cparams(("arbitrary",)),
    )(qg, kd, u, w, attn, eg)


def _gdn_chain_bwd(qg, kd, u, w, attn, eg, sall, do):
    H, T, _ = qg.shape
    N = T // PAIR
    hs = pl.BlockSpec((H, PAIR, HEAD_DIM), lambda n: (0, N - 1 - n, 0))
    ss = pl.BlockSpec((1, H, HEAD_DIM, HEAD_DIM), lambda n: (N - 1 - n, 0, 0, 0))
    hshape = jax.ShapeDtypeStruct((H, T, HEAD_DIM), F32)

    def body(qg_ref, kd_ref, u_ref, w_ref, at_ref, eg_ref, sall_ref, do_ref, *rest):
        out_refs, ds_ref = rest[:6], rest[6]

        @pl.when(pl.program_id(0) == 0)
        def _():
            ds_ref[...] = jnp.zeros_like(ds_ref)

        for h in range(H):
            _, vjp = jax.vjp(_gdn_chain_fn, sall_ref[0, h], qg_ref[h], kd_ref[h], u_ref[h], w_ref[h],
                             at_ref[h], eg_ref[h])
            grads = vjp((do_ref[h], ds_ref[h]))
            ds_ref[h] = grads[0]
            for ref, val in zip(out_refs, grads[1:]):
                ref[h] = val

    return pl.pallas_call(
        body, name="gdn_chain_bwd", grid=(N,),
        in_specs=[hs] * 6 + [ss, hs], out_specs=[hs] * 6, out_shape=[hshape] * 6,
        scratch_shapes=[pltpu.VMEM((H, HEAD_DIM, HEAD_DIM), F32)],
        compiler_params=_cparams(("arbitrary",)),
    )(qg, kd, u, w, attn, eg, sall, do)


def _post_fn(ogs, za, hw):
    outs = []
    for h, o in enumerate(ogs):
        r = lax.rsqrt(jnp.mean(o * o, axis=-1, keepdims=True) + EPS)
        outs.append(o * r * hw * _silu(za[:, h * HEAD_DIM:(h + 1) * HEAD_DIM]))
    return jnp.concatenate(outs, axis=1)


def _gdn_post(og, proj_m, hw):
    H, T, _ = og.shape
    A = H * HEAD_DIM
    tm = _pick(T, (512, 256, 128))

    def body(og_ref, za_ref, hw_ref, o_ref):
        o_ref[...] = _post_fn(tuple(og_ref[h] for h in range(H)), za_ref[...], hw_ref[...]).astype(BF16)

    return pl.pallas_call(
        body, name="gdn_post", grid=(T // tm,),
        in_specs=[pl.BlockSpec((H, tm, HEAD_DIM), lambda i: (0, i, 0)),
                  pl.BlockSpec((tm, A), lambda i: (i, 3)),
                  pl.BlockSpec((1, HEAD_DIM), lambda i: (0, 0))],
        out_specs=pl.BlockSpec((tm, A), lambda i: (i, 0)),
        out_shape=jax.ShapeDtypeStruct((T, A), BF16),
        compiler_params=_cparams(("parallel",)),
    )(og, proj_m, hw)


def _gdn_post_bwd(og, proj_m, hw, d_o):
    H, T, _ = og.shape
    A = H * HEAD_DIM
    tm = _pick(T, (256, 128))

    def body(og_ref, za_ref, hw_ref, do_ref, dog_ref, dza_ref, dhw_ref):
        _, vjp = jax.vjp(_post_fn, tuple(og_ref[h] for h in range(H)), za_ref[...], hw_ref[...])
        dog, dza, dhw = vjp(do_ref[...])
        for h in range(H):
            dog_ref[h] = dog[h]
        dza_ref[...] = dza.astype(BF16)

        @pl.when(pl.program_id(0) == 0)
        def _():
            dhw_ref[...] = dhw

        @pl.when(pl.program_id(0) > 0)
        def _():
            dhw_ref[...] += dhw

    return pl.pallas_call(
        body, name="gdn_post_bwd", grid=(T // tm,),
        in_specs=[pl.BlockSpec((H, tm, HEAD_DIM), lambda i: (0, i, 0)),
                  pl.BlockSpec((tm, A), lambda i: (i, 3)),
                  pl.BlockSpec((1, HEAD_DIM), lambda i: (0, 0)),
                  pl.BlockSpec((tm, A), lambda i: (i, 0))],
        out_specs=[pl.BlockSpec((H, tm, HEAD_DIM), lambda i: (0, i, 0)),
                   pl.BlockSpec((tm, A), lambda i: (i, 0)),
                   pl.BlockSpec((1, HEAD_DIM), lambda i: (0, 0))],
        out_shape=[jax.ShapeDtypeStruct((H, T, HEAD_DIM), F32), jax.ShapeDtypeStruct((T, A), BF16),
                   jax.ShapeDtypeStruct((1, HEAD_DIM), F32)],
        compiler_params=_cparams(("arbitrary",)),
    )(og, proj_m, hw, d_o)


def _sgu_fn(ub, vb, zb, lw, lb, W, bbc):
    G = len(W)
    tm = ub.shape[0]
    mu = jnp.mean(vb, axis=-1, keepdims=True)
    xc = vb - mu
    var = jnp.mean(xc * xc, axis=-1, keepdims=True)
    vn = xc * lax.rsqrt(var + EPS) * lw + lb
    mask = _iota((CHUNK_B, CHUNK_B), 0) >= _iota((CHUNK_B, CHUNK_B), 1)
    cols = []
    for g in range(G):
        wm = jnp.where(mask, W[g], 0.0).astype(BF16)
        rows = []
        for c in range(tm // CHUNK_B):
            blk = vn[c * CHUNK_B:(c + 1) * CHUNK_B, g * HEAD_DIM:(g + 1) * HEAD_DIM].astype(BF16)
            rows.append(_dot(wm, blk) + bbc[g])
        cols.append(jnp.concatenate(rows, axis=0) if len(rows) > 1 else rows[0])
    s = jnp.concatenate(cols, axis=1)
    return ub * s * _silu(zb)


def _sgu_cols(A, B):
    base = 4 * A // B
    return base, base + 1, base + 2


def _sgu_fwd(proj_m, lw, lb, W, bbc, A):
    T = proj_m.shape[0]
    G = W.shape[0]
    B = G * HEAD_DIM
    tm = _pick(T, (256, 128))
    cu, cv, cz = _sgu_cols(A, B)

    def body(u_ref, v_ref, z_ref, lw_ref, lb_ref, w_ref, b_ref, o_ref):
        o_ref[...] = _sgu_fn(u_ref[...], v_ref[...], z_ref[...], lw_ref[...], lb_ref[...],
                             tuple(w_ref[g] for g in range(G)), tuple(b_ref[g] for g in range(G))).astype(BF16)

    row = pl.BlockSpec((1, B), lambda i: (0, 0))
    cube = pl.BlockSpec((G, CHUNK_B, CHUNK_B), lambda i: (0, 0, 0))
    return pl.pallas_call(
        body, name="sgu_fwd", grid=(T // tm,),
        in_specs=[pl.BlockSpec((tm, B), lambda i: (i, cu)), pl.BlockSpec((tm, B), lambda i: (i, cv)),
                  pl.BlockSpec((tm, B), lambda i: (i, cz)), row, row, cube, cube],
        out_specs=pl.BlockSpec((tm, B), lambda i: (i, 0)),
        out_shape=jax.ShapeDtypeStruct((T, B), BF16),
        compiler_params=_cparams(("parallel",)),
    )(proj_m, proj_m, proj_m, lw, lb, W, bbc)


def _sgu_bwd(proj_m, lw, lb, W, bbc, d_o, A):
    T = proj_m.shape[0]
    G = W.shape[0]
    B = G * HEAD_DIM
    tm = _pick(T, (256, 128))
    nt = T // tm
    cu, cv, cz = _sgu_cols(A, B)

    def body(u_ref, v_ref, z_ref, lw_ref, lb_ref, w_ref, b_ref, do_ref,
             dp_ref, dlw_ref, dlb_ref, dw_ref, db_ref, dbb_ref):
        _, vjp = jax.vjp(_sgu_fn, u_ref[...], v_ref[...], z_ref[...], lw_ref[...], lb_ref[...],
                         tuple(w_ref[g] for g in range(G)), tuple(b_ref[g] for g in range(G)))
        du, dv, dz, dlw, dlb, dW, dbb = vjp(do_ref[...])
        dW, dbb = jnp.stack(dW, axis=0), jnp.stack(dbb, axis=0)
        dp_ref[:, 0:B] = du.astype(BF16)
        dp_ref[:, B:2 * B] = dv.astype(BF16)
        dp_ref[:, 2 * B:3 * B] = dz.astype(BF16)
        i = pl.program_id(0)

        @pl.when(i == 0)
        def _():
            dlw_ref[...] = dlw
            dlb_ref[...] = dlb
            dw_ref[...] = dW
            dbb_ref[...] = dbb

        @pl.when(i > 0)
        def _():
            dlw_ref[...] += dlw
            dlb_ref[...] += dlb
            dw_ref[...] += dW
            dbb_ref[...] += dbb

        @pl.when(i == nt - 1)
        def _():
            db_ref[...] = jnp.sum(dbb_ref[...], axis=-1, keepdims=True)

    row = pl.BlockSpec((1, B), lambda i: (0, 0))
    cube = pl.BlockSpec((G, CHUNK_B, CHUNK_B), lambda i: (0, 0, 0))
    return pl.pallas_call(
        body, name="sgu_bwd", grid=(nt,),
        in_specs=[pl.BlockSpec((tm, B), lambda i: (i, cu)), pl.BlockSpec((tm, B), lambda i: (i, cv)),
                  pl.BlockSpec((tm, B), lambda i: (i, cz)), row, row, cube, cube,
                  pl.BlockSpec((tm, B), lambda i: (i, A // B))],
        out_specs=[pl.BlockSpec((tm, 3 * B), lambda i: (i, 0)), row, row, cube,
                   pl.BlockSpec((G, CHUNK_B, 1), lambda i: (0, 0, 0))],
        out_shape=[jax.ShapeDtypeStruct((T, 3 * B), BF16), jax.ShapeDtypeStruct((1, B), F32),
                   jax.ShapeDtypeStruct((1, B), F32), jax.ShapeDtypeStruct((G, CHUNK_B, CHUNK_B), F32),
                   jax.ShapeDtypeStruct((G, CHUNK_B, 1), F32)],
        scratch_shapes=[pltpu.VMEM((G, CHUNK_B, CHUNK_B), F32)],
        compiler_params=_cparams(("arbitrary",)),
    )(proj_m, proj_m, proj_m, lw, lb, W, bbc, d_o)


def _head_fn(mix, x, fw, tgt):
    h = x + mix
    y = _rms_fn(h, fw)
    e = y - tgt
    return 0.5 * jnp.sum(jnp.mean(e * e, axis=-1, keepdims=True), axis=0, keepdims=True)


def _out_proj_loss(oa, ob, wout, x, tgt, fw):
    T, A = oa.shape
    B = ob.shape[1]
    D = x.shape[1]
    tm = _pick(T, (256, 128))

    def body(oa_ref, ob_ref, w_ref, x_ref, t_ref, fw_ref, dh_ref, dhb_ref, loss_ref, dfw_ref):
        mix = _dot(oa_ref[...], w_ref[0:A, :]) + _dot(ob_ref[...], w_ref[A:A + B, :])
        xv, tv = x_ref[...], t_ref[...]
        loss, vjp = jax.vjp(lambda m, f: _head_fn(m, xv, f, tv), mix, fw_ref[...])
        dh, dfw = vjp(jnp.ones((1, 1), F32))
        dh_ref[...] = dh
        dhb_ref[...] = dh.astype(BF16)
        lrow = jnp.broadcast_to(loss, (1, LANES))

        @pl.when(pl.program_id(0) == 0)
        def _():
            loss_ref[...] = lrow
            dfw_ref[...] = dfw

        @pl.when(pl.program_id(0) > 0)
        def _():
            loss_ref[...] += lrow
            dfw_ref[...] += dfw

    tile = pl.BlockSpec((tm, D), lambda i: (i, 0))
    return pl.pallas_call(
        body, name="out_proj_loss", grid=(T // tm,),
        in_specs=[pl.BlockSpec((tm, A), lambda i: (i, 0)), pl.BlockSpec((tm, B), lambda i: (i, 0)),
                  pl.BlockSpec((A + B, D), lambda i: (0, 0)), tile, tile,
                  pl.BlockSpec((1, D), lambda i: (0, 0))],
        out_specs=[tile, tile, pl.BlockSpec((1, LANES), lambda i: (0, 0)),
                   pl.BlockSpec((1, D), lambda i: (0, 0))],
        out_shape=[jax.ShapeDtypeStruct((T, D), F32), jax.ShapeDtypeStruct((T, D), BF16),
                   jax.ShapeDtypeStruct((1, LANES), F32), jax.ShapeDtypeStruct((1, D), F32)],
        compiler_params=_cparams(("arbitrary",)),
    )(oa, ob, wout, x, tgt, fw)


def _adamw(w, g, m, v, name):
    R, Cn = w.shape
    tr = R if R * Cn <= 512 * 1024 else _pick(R, (256, 128, 64, 32, 16, 8))

    def body(w_ref, g_ref, m_ref, v_ref, d_ref, mo_ref, vo_ref):
        g = g_ref[...]
        m = ADAM_B1 * m_ref[...] + (1.0 - ADAM_B1) * g
        v = ADAM_B2 * v_ref[...] + (1.0 - ADAM_B2) * jnp.square(g)
        m_hat = m / (1.0 - ADAM_B1 ** ADAM_STEP)
        v_hat = v / (1.0 - ADAM_B2 ** ADAM_STEP)
        d_ref[...] = -ADAM_LR * (m_hat / (jnp.sqrt(v_hat) + ADAM_EPS) + ADAM_WD * w_ref[...])
        mo_ref[...] = m
        vo_ref[...] = v

    tile = pl.BlockSpec((tr, Cn), lambda i: (i, 0))
    shape = jax.ShapeDtypeStruct((R, Cn), F32)
    return pl.pallas_call(
        body, name=name, grid=(R // tr,), in_specs=[tile] * 4, out_specs=[tile] * 3,
        out_shape=[shape] * 3, compiler_params=_cparams(("parallel",)),
    )(w, g, m, v)


def _place():
    x, y, c = lax.axis_index("x"), lax.axis_index("y"), lax.axis_index("c")
    others = [(1 - x, y), (x, 1 - y), (1 - x, 1 - y)]
    return x, y, c, others


def _chip_index(px, py):
    return 2 * px + py


ANY = pl.BlockSpec(memory_space=pl.ANY)


def _gather_weights(win_b, wout_b, conv_b):
    Din, Cb = win_b.shape
    Rb, D = wout_b.shape
    hi, ho = Din // 2, Rb // 2

    def body(win_ref, wout_ref, conv_ref, gin_ref, gout_ref, gconv_ref, send_sems, recv_sems, local_sems):
        x, y, c, others = _place()
        me = _chip_index(x, y)
        sibling = (x, y, 1 - c)
        local = [pltpu.make_async_copy(win_ref, gin_ref.at[me], local_sems.at[0]),
                 pltpu.make_async_copy(wout_ref, gout_ref.at[me], local_sems.at[1]),
                 pltpu.make_async_copy(conv_ref, gconv_ref.at[me], local_sems.at[2])]
        for cp in local:
            cp.start()

        def half(ref, chip, core, n):
            return ref.at[chip, pl.ds(core * n, n), :]

        def copy(sem, src, dst, to):
            return pltpu.make_async_remote_copy(src_ref=src, dst_ref=dst, send_sem=send_sems.at[sem],
                                                recv_sem=recv_sems.at[sem], device_id=to, device_id_type=MESH_ID)

        sends = []
        for j, chip in enumerate(others):
            to = (*chip, c)
            sends.append(copy(3 * j, win_ref.at[pl.ds(c * hi, hi), :], half(gin_ref, me, c, hi), to))
            sends.append(copy(3 * j + 1, wout_ref.at[pl.ds(c * ho, ho), :], half(gout_ref, me, c, ho), to))
            sends.append(copy(3 * j + 2, conv_ref, gconv_ref.at[me], to))
        for cp in sends:
            cp.start()
        passed = []
        for j, chip in enumerate(others):
            k = _chip_index(*chip)
            copy(3 * j, half(gin_ref, k, c, hi), half(gin_ref, k, c, hi), (*chip, c)).wait_recv()
            fw = copy(9 + 2 * j, half(gin_ref, k, c, hi), half(gin_ref, k, c, hi), sibling)
            fw.start()
            copy(3 * j + 1, half(gout_ref, k, c, ho), half(gout_ref, k, c, ho), (*chip, c)).wait_recv()
            fo = copy(10 + 2 * j, half(gout_ref, k, c, ho), half(gout_ref, k, c, ho), sibling)
            fo.start()
            copy(3 * j + 2, gconv_ref.at[k], gconv_ref.at[k], (*chip, c)).wait_recv()
            passed += [fw, fo]
        for j, chip in enumerate(others):
            k = _chip_index(*chip)
            copy(9 + 2 * j, half(gin_ref, k, 1 - c, hi), half(gin_ref, k, 1 - c, hi), sibling).wait_recv()
            copy(10 + 2 * j, half(gout_ref, k, 1 - c, ho), half(gout_ref, k, 1 - c, ho), sibling).wait_recv()
        for cp in sends + passed:
            cp.wait_send()
        for cp in local:
            cp.wait()

    return pl.pallas_call(
        body, name="gather_weights",
        in_specs=[ANY, ANY, ANY], out_specs=[ANY, ANY, ANY],
        out_shape=[jax.ShapeDtypeStruct((N_CHIPS, Din, Cb), win_b.dtype),
                   jax.ShapeDtypeStruct((N_CHIPS, Rb, D), wout_b.dtype),
                   jax.ShapeDtypeStruct((N_CHIPS,) + conv_b.shape, conv_b.dtype)],
        scratch_shapes=[pltpu.SemaphoreType.DMA((15,)), pltpu.SemaphoreType.DMA((15,)),
                        pltpu.SemaphoreType.DMA((3,))],
        compiler_params=pltpu.CompilerParams(has_side_effects=True),
    )(win_b, wout_b, conv_b)


def _allreduce_small(buf):
    R, L = buf.shape

    def body(in_ref, out_ref, sib_ref, pair_ref, chips_ref, send_sems, recv_sems):
        x, y, c, others = _place()
        me = _chip_index(x, y)
        sibling = (x, y, 1 - c)
        cp = pltpu.make_async_remote_copy(src_ref=in_ref, dst_ref=sib_ref, send_sem=send_sems.at[0],
                                          recv_sem=recv_sems.at[0], device_id=sibling, device_id_type=MESH_ID)
        cp.start()
        cp.wait()
        pair_ref[...] = in_ref[...] + sib_ref[...]
        sends = []
        for j, chip in enumerate(others):
            s = pltpu.make_async_remote_copy(src_ref=pair_ref, dst_ref=chips_ref.at[me],
                                             send_sem=send_sems.at[1 + j], recv_sem=recv_sems.at[1 + j],
                                             device_id=(*chip, c), device_id_type=MESH_ID)
            s.start()
            sends.append(s)
        chips_ref[me] = pair_ref[...]
        for j, chip in enumerate(others):
            k = _chip_index(*chip)
            pltpu.make_async_remote_copy(src_ref=pair_ref, dst_ref=chips_ref.at[k], send_sem=send_sems.at[1 + j],
                                         recv_sem=recv_sems.at[1 + j], device_id=(*chip, c),
                                         device_id_type=MESH_ID).wait_recv()
        for s in sends:
            s.wait_send()
        out_ref[...] = ((chips_ref[0] + chips_ref[1]) + chips_ref[2]) + chips_ref[3]

    vm = pl.BlockSpec(memory_space=pltpu.VMEM)
    return pl.pallas_call(
        body, name="allreduce_small", in_specs=[vm], out_specs=vm,
        out_shape=jax.ShapeDtypeStruct((R, L), F32),
        scratch_shapes=[pltpu.VMEM((R, L), F32), pltpu.VMEM((R, L), F32), pltpu.VMEM((N_CHIPS, R, L), F32),
                        pltpu.SemaphoreType.DMA((4,)), pltpu.SemaphoreType.DMA((4,))],
        compiler_params=pltpu.CompilerParams(vmem_limit_bytes=VMEM_LIMIT),
    )(buf)


def _pair_exchange(gw, go):
    _, Din, Cb = gw.shape
    _, Rb, D = go.shape
    hi, ho = Din // 2, Rb // 2

    def body(gw_ref, go_ref, lw_ref, lo_ref, send_sems, recv_sems):
        x, y, c, _ = _place()
        sibling = (x, y, 1 - c)
        a = pltpu.make_async_remote_copy(src_ref=gw_ref.at[:, pl.ds((1 - c) * hi, hi), :], dst_ref=lw_ref,
                                         send_sem=send_sems.at[0], recv_sem=recv_sems.at[0],
                                         device_id=sibling, device_id_type=MESH_ID)
        b = pltpu.make_async_remote_copy(src_ref=go_ref.at[:, pl.ds((1 - c) * ho, ho), :], dst_ref=lo_ref,
                                         send_sem=send_sems.at[1], recv_sem=recv_sems.at[1],
                                         device_id=sibling, device_id_type=MESH_ID)
        a.start()
        b.start()
        a.wait()
        b.wait()

    return pl.pallas_call(
        body, name="pair_exchange", in_specs=[ANY, ANY], out_specs=[ANY, ANY],
        out_shape=[jax.ShapeDtypeStruct((N_CHIPS, hi, Cb), gw.dtype),
                   jax.ShapeDtypeStruct((N_CHIPS, ho, D), go.dtype)],
        scratch_shapes=[pltpu.SemaphoreType.DMA((2,)), pltpu.SemaphoreType.DMA((2,))],
        compiler_params=pltpu.CompilerParams(has_side_effects=True),
    )(gw, go)


def _pair_sum(g, land, c_arr, name):
    nb, R, Cn = g.shape
    hr = R // 2
    tr = _pick(hr, (256, 128, 64, 32, 16))
    nt = hr // tr

    def body(c_ref, g_ref, l_ref, o_ref):
        o_ref[...] = (g_ref[...] + l_ref[...]).astype(BF16)

    return pl.pallas_call(
        body, name=name,
        grid_spec=pltpu.PrefetchScalarGridSpec(
            num_scalar_prefetch=1, grid=(nb, nt),
            in_specs=[pl.BlockSpec((1, tr, Cn), lambda b, i, c_ref: (b, c_ref[0] * nt + i, 0)),
                      pl.BlockSpec((1, tr, Cn), lambda b, i, c_ref: (b, i, 0))],
            out_specs=pl.BlockSpec((1, tr, Cn), lambda b, i, c_ref: (b, i, 0))),
        out_shape=jax.ShapeDtypeStruct((nb, hr, Cn), BF16),
        compiler_params=_cparams(("parallel", "parallel")),
    )(c_arr, g, land)


def _chip_exchange(pw, po):
    nb, hi, Cb = pw.shape
    _, ho, D = po.shape

    def body(pw_ref, po_ref, qw_ref, qo_ref, send_sems, recv_sems, local_sems):
        x, y, c, others = _place()
        me = _chip_index(x, y)
        local = [pltpu.make_async_copy(pw_ref.at[me], qw_ref.at[me], local_sems.at[0]),
                 pltpu.make_async_copy(po_ref.at[me], qo_ref.at[me], local_sems.at[1])]
        for cp in local:
            cp.start()
        sends = []
        for j, chip in enumerate(others):
            k = _chip_index(*chip)
            for n, (src, dst) in enumerate(((pw_ref, qw_ref), (po_ref, qo_ref))):
                s = pltpu.make_async_remote_copy(src_ref=src.at[k], dst_ref=dst.at[me],
                                                 send_sem=send_sems.at[2 * j + n], recv_sem=recv_sems.at[2 * j + n],
                                                 device_id=(*chip, c), device_id_type=MESH_ID)
                s.start()
                sends.append(s)
        for j, chip in enumerate(others):
            k = _chip_index(*chip)
            for n, (src, dst) in enumerate(((pw_ref, qw_ref), (po_ref, qo_ref))):
                pltpu.make_async_remote_copy(src_ref=src.at[k], dst_ref=dst.at[k],
                                             send_sem=send_sems.at[2 * j + n], recv_sem=recv_sems.at[2 * j + n],
                                             device_id=(*chip, c), device_id_type=MESH_ID).wait_recv()
        for s in sends:
            s.wait_send()
        for cp in local:
            cp.wait()

    return pl.pallas_call(
        body, name="chip_exchange", in_specs=[ANY, ANY], out_specs=[ANY, ANY],
        out_shape=[jax.ShapeDtypeStruct(pw.shape, pw.dtype), jax.ShapeDtypeStruct(po.shape, po.dtype)],
        scratch_shapes=[pltpu.SemaphoreType.DMA((6,)), pltpu.SemaphoreType.DMA((6,)),
                        pltpu.SemaphoreType.DMA((2,))],
        compiler_params=pltpu.CompilerParams(has_side_effects=True),
    )(pw, po)


def _chip_sum(q, name):
    nb, hr, Cn = q.shape
    tr = _pick(hr, (256, 128, 64, 32, 16))

    def body(q_ref, o_ref):
        f = lambda k: q_ref[k].astype(F32)
        o_ref[...] = ((f(0) + f(1)) + f(2)) + f(3)

    return pl.pallas_call(
        body, name=name, grid=(hr // tr,),
        in_specs=[pl.BlockSpec((nb, tr, Cn), lambda i: (0, i, 0))],
        out_specs=pl.BlockSpec((tr, Cn), lambda i: (i, 0)),
        out_shape=jax.ShapeDtypeStruct((hr, Cn), F32),
        compiler_params=_cparams(("parallel",)),
    )(q)


def _sibling_concat(rw, ro):
    hi, Cb = rw.shape
    ho, D = ro.shape

    def body(rw_ref, ro_ref, fw_ref, fo_ref, send_sems, recv_sems, local_sems):
        x, y, c, _ = _place()
        sibling = (x, y, 1 - c)
        mine_w = fw_ref.at[pl.ds(c * hi, hi), :]
        mine_o = fo_ref.at[pl.ds(c * ho, ho), :]
        local = [pltpu.make_async_copy(rw_ref, mine_w, local_sems.at[0]),
                 pltpu.make_async_copy(ro_ref, mine_o, local_sems.at[1])]
        for cp in local:
            cp.start()
        a = pltpu.make_async_remote_copy(src_ref=rw_ref, dst_ref=mine_w, send_sem=send_sems.at[0],
                                         recv_sem=recv_sems.at[0], device_id=sibling, device_id_type=MESH_ID)
        b = pltpu.make_async_remote_copy(src_ref=ro_ref, dst_ref=mine_o, send_sem=send_sems.at[1],
                                         recv_sem=recv_sems.at[1], device_id=sibling, device_id_type=MESH_ID)
        a.start()
        b.start()
        theirs_w = fw_ref.at[pl.ds((1 - c) * hi, hi), :]
        theirs_o = fo_ref.at[pl.ds((1 - c) * ho, ho), :]
        pltpu.make_async_remote_copy(src_ref=rw_ref, dst_ref=theirs_w, send_sem=send_sems.at[0],
                                     recv_sem=recv_sems.at[0], device_id=sibling, device_id_type=MESH_ID).wait_recv()
        pltpu.make_async_remote_copy(src_ref=ro_ref, dst_ref=theirs_o, send_sem=send_sems.at[1],
                                     recv_sem=recv_sems.at[1], device_id=sibling, device_id_type=MESH_ID).wait_recv()
        a.wait_send()
        b.wait_send()
        for cp in local:
            cp.wait()

    return pl.pallas_call(
        body, name="sibling_concat", in_specs=[ANY, ANY], out_specs=[ANY, ANY],
        out_shape=[jax.ShapeDtypeStruct((2 * hi, Cb), F32), jax.ShapeDtypeStruct((2 * ho, D), F32)],
        scratch_shapes=[pltpu.SemaphoreType.DMA((2,)), pltpu.SemaphoreType.DMA((2,)),
                        pltpu.SemaphoreType.DMA((2,))],
        compiler_params=pltpu.CompilerParams(has_side_effects=True),
    )(rw, ro)


def _pad_lanes(a, n=LANES):
    return jnp.pad(a, ((0, 0), (0, n - a.shape[1])))


def _local_grads(x, tgt, norm_w, w_full, conv_w, a_log, dt_bias, head_norm_w, sgu_ln_w, sgu_ln_b,
                 w_spatial, b_spatial, wout, final_norm_w):
    T, D = x.shape
    H = a_log.shape[1]
    A = H * HEAD_DIM
    G = w_spatial.shape[0]
    B = G * HEAD_DIM
    c_ba = 4 * A
    w_main = jnp.concatenate([w_full[:, :c_ba], w_full[:, c_ba + 2 * H:]], axis=1)
    w_ba = _pad_lanes(w_full[:, c_ba:c_ba + 2 * H])
    wt_all = jnp.concatenate([w_main, w_ba], axis=1).T
    alog_row = jnp.pad(a_log, ((0, 0), (H, LANES - 2 * H)))
    dtb_row = jnp.pad(dt_bias, ((0, 0), (H, LANES - 2 * H)))
    bbc = jnp.broadcast_to(b_spatial[:, :, None], (G, CHUNK_B, CHUNK_B))

    xn = _rms_in(x, norm_w)
    proj_m = _mm_nn(xn, w_main, F32, "in_proj")
    proj_ba = _mm_nn(xn, w_ba, F32, "in_proj_ba")
    q, k, v, gb, bb = _gdn_pre(proj_m, proj_ba, conv_w, alog_row, dtb_row, H)
    u, w, qg, kd, attn, eg = _gdn_prep(q, k, v, gb, bb)
    og, sall = _gdn_chain(qg, kd, u, w, attn, eg)
    oa = _gdn_post(og, proj_m, head_norm_w)
    ob = _sgu_fwd(proj_m, sgu_ln_w, sgu_ln_b, w_spatial, bbc, A)
    dh, dhb, loss_row, d_fnw = _out_proj_loss(oa, ob, wout, x, tgt, final_norm_w.reshape(1, D))

    d_o = _mm_nn(dhb, wout.T, F32, "out_proj_dx")
    d_wout = jnp.concatenate([_mm_tn(oa, dhb, "out_proj_dw_a"), _mm_tn(ob, dhb, "out_proj_dw_b")], axis=0)
    dpb, d_lw, d_lb, d_ws, d_bs = _sgu_bwd(proj_m, sgu_ln_w, sgu_ln_b, w_spatial, bbc, d_o, A)
    dog, dza, d_hw = _gdn_post_bwd(og, proj_m, head_norm_w, d_o)
    dqg, dkd, du, dw, dat, deg = _gdn_chain_bwd(qg, kd, u, w, attn, eg, sall, dog)
    dq, dk, dv, dgb, dbb = _gdn_prep_bwd(q, k, v, gb, bb, du, dw, dqg, dkd, dat, deg)
    dc, dba, d_al, d_dt = _gdn_pre_bwd(proj_m, proj_ba, conv_w, alog_row, dtb_row, dq, dk, dv, dgb, dbb, H)
    dqkv, d_conv = _conv_bwd(proj_m, dc, conv_w, H)
    dproj = jnp.concatenate([dqkv, dza, dpb, dba], axis=1)
    dxn = _mm_nn(dproj, wt_all, F32, "in_proj_dx", tm=512, tn=512, tk=2432)
    grad_x, d_nw = _rms_in_bwd(x, norm_w, dxn, dh)
    d_wall = _mm_tn(xn, dproj, "in_proj_dw")
    nm = w_main.shape[1]
    d_win = jnp.concatenate([d_wall[:, :c_ba], d_wall[:, nm:nm + 2 * H], d_wall[:, c_ba:nm]], axis=1)
    grads = dict(norm_w=d_nw, w_in=d_win, conv_w=d_conv[:CONV_WIDTH], a_log=d_al[:, H:2 * H],
                 dt_bias=d_dt[:, H:2 * H], head_norm_w=d_hw, sgu_ln_w=d_lw, sgu_ln_b=d_lb, w_spatial=d_ws,
                 b_spatial=d_bs[:, :, 0], w_out=d_wout, final_norm_w=d_fnw)
    return loss_row, grad_x, grads


SMALL = ("norm_w", "conv_w", "a_log", "dt_bias", "head_norm_w", "sgu_ln_w", "sgu_ln_b", "w_spatial",
         "b_spatial", "final_norm_w")


def _pack(parts):
    rows = []
    for p in parts:
        f = p.reshape(-1)
        f = jnp.pad(f, (0, (-f.shape[0]) % (8 * LANES)))
        rows.append(f.reshape(-1, LANES))
    return jnp.concatenate(rows, axis=0)


def _unpack(buf, shapes):
    out, r = [], 0
    for s in shapes:
        n = 1
        for d in s:
            n *= d
        nr = -(-n // (8 * LANES)) * 8
        out.append(buf[r:r + nr].reshape(-1)[:n].reshape(s))
        r += nr
    return out


def kernel(x, norm_w, w_in, conv_w, a_log, dt_bias, head_norm_w, sgu_ln_w, sgu_ln_b, w_spatial, b_spatial, w_out, final_norm_w, loss_target, m_norm_w, m_w_in, m_conv_w, m_a_log, m_dt_bias, m_head_norm_w, m_sgu_ln_w, m_sgu_ln_b, m_w_spatial, m_b_spatial, m_w_out, m_final_norm_w, v_norm_w, v_w_in, v_conv_w, v_a_log, v_dt_bias, v_head_norm_w, v_sgu_ln_w, v_sgu_ln_b, v_w_spatial, v_b_spatial, v_w_out, v_final_norm_w):
    T, D = x.shape[1], x.shape[2]
    weights = dict(norm_w=norm_w, w_in=w_in, conv_w=conv_w, a_log=a_log, dt_bias=dt_bias, head_norm_w=head_norm_w,
                   sgu_ln_w=sgu_ln_w, sgu_ln_b=sgu_ln_b, w_spatial=w_spatial, b_spatial=b_spatial, w_out=w_out,
                   final_norm_w=final_norm_w)
    mom_m = dict(norm_w=m_norm_w, w_in=m_w_in, conv_w=m_conv_w, a_log=m_a_log, dt_bias=m_dt_bias,
                 head_norm_w=m_head_norm_w, sgu_ln_w=m_sgu_ln_w, sgu_ln_b=m_sgu_ln_b, w_spatial=m_w_spatial,
                 b_spatial=m_b_spatial, w_out=m_w_out, final_norm_w=m_final_norm_w)
    mom_v = dict(norm_w=v_norm_w, w_in=v_w_in, conv_w=v_conv_w, a_log=v_a_log, dt_bias=v_dt_bias,
                 head_norm_w=v_head_norm_w, sgu_ln_w=v_sgu_ln_w, sgu_ln_b=v_sgu_ln_b, w_spatial=v_w_spatial,
                 b_spatial=v_b_spatial, w_out=v_w_out, final_norm_w=v_final_norm_w)
    me = _chip_index(lax.axis_index("x"), lax.axis_index("y"))
    c_arr = lax.axis_index("c").astype(jnp.int32).reshape(1)
    Din, Cb = w_in.shape[1], w_in.shape[2]
    Rb = w_out.shape[1]
    cconv = conv_w.shape[2]

    g_in, g_out, g_conv = _gather_weights(w_in[0].astype(BF16), w_out[0].astype(BF16), conv_w[0])
    w_full = g_in.transpose(1, 0, 2).reshape(Din, N_CHIPS * Cb)
    wout_full = g_out.reshape(N_CHIPS * Rb, D)
    conv_full = g_conv.transpose(1, 0, 2).reshape(CONV_WIDTH, N_CHIPS * cconv)

    loss_row, grad_x, g = _local_grads(
        x[0], loss_target[0], norm_w, w_full, conv_full, a_log, dt_bias, head_norm_w, sgu_ln_w, sgu_ln_b,
        w_spatial[0], b_spatial[0], wout_full, final_norm_w)

    gw = g["w_in"].reshape(Din, N_CHIPS, Cb).transpose(1, 0, 2)
    go = g["w_out"].reshape(N_CHIPS, Rb, D)
    lw, lo = _pair_exchange(gw, go)
    qw, qo = _chip_exchange(_pair_sum(gw, lw, c_arr, "pair_sum_w_in"), _pair_sum(go, lo, c_arr, "pair_sum_w_out"))
    gsum_in, gsum_out = _sibling_concat(_chip_sum(qw, "chip_sum_w_in"), _chip_sum(qo, "chip_sum_w_out"))
    small_shapes = [tuple(g[n].shape) for n in SMALL]
    small = _unpack(_allreduce_small(_pack([g[n] for n in SMALL])), small_shapes)
    gsmall = dict(zip(SMALL, small))
    gsmall["conv_w"] = lax.dynamic_slice_in_dim(gsmall["conv_w"], me * cconv, cconv, axis=1)

    grads, deltas, new_m, new_v = {}, {}, {}, {}
    for n, gs in (("w_in", gsum_in), ("w_out", gsum_out)):
        d, m2, v2 = _adamw(weights[n][0], gs, mom_m[n][0], mom_v[n][0], "adamw_" + n)
        grads[n], deltas[n], new_m[n], new_v[n] = gs[None], d[None], m2[None], v2[None]
    shapes = [tuple(weights[n].shape) for n in SMALL]
    ds, ms, vs = _adamw(_pack([weights[n] for n in SMALL]), _pack([gsmall[n] for n in SMALL]),
                        _pack([mom_m[n] for n in SMALL]), _pack([mom_v[n] for n in SMALL]), "adamw_small")
    for n, gq, d, m2, v2 in zip(SMALL, [gsmall[n] for n in SMALL], _unpack(ds, shapes), _unpack(ms, shapes),
                                _unpack(vs, shapes)):
        grads[n], deltas[n], new_m[n], new_v[n] = gq.reshape(weights[n].shape), d, m2, v2

    loss = lax.psum(loss_row[0, 0], ("x", "y", "c"))
    order = ("norm_w", "w_in", "conv_w", "a_log", "dt_bias", "head_norm_w", "sgu_ln_w", "sgu_ln_b", "w_spatial",
             "b_spatial", "w_out", "final_norm_w")
    return (loss, grad_x[None], *[grads[n] for n in order], *[deltas[n] for n in order],
            *[new_m[n] for n in order], *[new_v[n] for n in order])
```

```python
import functools

import jax
import jax.numpy as jnp
from jax import lax
from jax.experimental import pallas as pl
from jax.experimental.pallas import tpu as pltpu

F32 = jnp.float32
BF16 = jnp.bfloat16
EPS = 1e-6
HEAD_DIM = 128
CHUNK_A = 64
CHUNK_B = 128
CONV_WIDTH = 4
LANES = 128
HALO = 8
N_CHIPS = 4
ADAM_LR = 0.001
ADAM_B1 = 0.9
ADAM_B2 = 0.999
ADAM_EPS = 1e-08
ADAM_WD = 0.01
ADAM_STEP = 10
VMEM_LIMIT = 56 * 1024 * 1024
MESH_ID = pl.DeviceIdType.MESH
HI = lax.Precision.HIGHEST


def _cparams(sem=None, **kw):
    return pltpu.CompilerParams(dimension_semantics=sem, vmem_limit_bytes=VMEM_LIMIT, **kw)


def _dot(a, b, hi=False):
    return lax.dot_general(a, b, (((1,), (0,)), ((), ())), precision=HI if hi else None,
                           preferred_element_type=F32)


def _dot_nt(a, b, hi=False):
    return lax.dot_general(a, b, (((1,), (1,)), ((), ())), precision=HI if hi else None,
                           preferred_element_type=F32)


def _dot_tn(a, b, hi=False):
    return lax.dot_general(a, b, (((0,), (0,)), ((), ())), precision=HI if hi else None,
                           preferred_element_type=F32)


def _iota(shape, dim):
    return lax.broadcasted_iota(jnp.int32, shape, dim)


def _sigmoid(x):
    return 0.5 * (jnp.tanh(0.5 * x) + 1.0)


def _silu(x):
    return x * _sigmoid(x)


def _softplus(x):
    z = jnp.exp(-jnp.abs(x))
    small = z * (1.0 - z * (0.5 - z * (1.0 / 3.0)))
    return jnp.maximum(x, 0.0) + jnp.where(z < 1e-3, small, jnp.log(1.0 + z))


def _pick(n, pref):
    for t in pref:
        if n % t == 0:
            return t
    return n


def _mm_nn(a, b, out_dtype, name, tm=1024, tn=512, tk=None):
    M, K = a.shape
    _, N = b.shape
    tm = _pick(M, (tm, 512, 256, 128))
    tn = _pick(N, (tn, 512, 384, 256, 128))
    tk = K if tk is None else _pick(K, (tk,))
    nk = K // tk

    def body(a_ref, b_ref, o_ref, *scratch):
        part = _dot(a_ref[...], b_ref[...])
        if nk == 1:
            o_ref[...] = part.astype(out_dtype)
        else:
            acc_ref, = scratch
            k = pl.program_id(2)

            @pl.when(k == 0)
            def _():
                acc_ref[...] = part

            @pl.when(k > 0)
            def _():
                acc_ref[...] += part

            @pl.when(k == nk - 1)
            def _():
                o_ref[...] = acc_ref[...].astype(out_dtype)

    return pl.pallas_call(
        body, name=name, grid=(M // tm, N // tn, nk),
        in_specs=[pl.BlockSpec((tm, tk), lambda i, j, k: (i, k)),
                  pl.BlockSpec((tk, tn), lambda i, j, k: (k, j))],
        out_specs=pl.BlockSpec((tm, tn), lambda i, j, k: (i, j)),
        out_shape=jax.ShapeDtypeStruct((M, N), out_dtype),
        scratch_shapes=[] if nk == 1 else [pltpu.VMEM((tm, tn), F32)],
        compiler_params=_cparams(("parallel", "parallel", "arbitrary")),
    )(a, b)


def _mm_rhs_outer(a, b, out_dtype, name, tm=256, tn=1024):
    M, K = a.shape
    _, N = b.shape
    tm = _pick(M, (tm, 128))
    tn = _pick(N, (tn, 512, 256, 128))

    def body(a_ref, b_ref, o_ref):
        o_ref[...] = _dot(a_ref[...], b_ref[...]).astype(out_dtype)

    return pl.pallas_call(
        body, name=name, grid=(N // tn, M // tm),
        in_specs=[pl.BlockSpec((tm, K), lambda j, i: (i, 0)),
                  pl.BlockSpec((K, tn), lambda j, i: (0, j))],
        out_specs=pl.BlockSpec((tm, tn), lambda j, i: (i, j)),
        out_shape=jax.ShapeDtypeStruct((M, N), out_dtype),
        compiler_params=_cparams(("parallel", "parallel")),
    )(a, b)


def _mm_col_blocks(a, b, nb, name, tm=512):
    M, K = a.shape
    W = b.shape[1] // nb
    tm = _pick(M, (tm, 256, 128))

    def body(a_ref, b_ref, o_ref):
        o_ref[0] = _dot(a_ref[...], b_ref[...])

    return pl.pallas_call(
        body, name=name, grid=(nb, M // tm),
        in_specs=[pl.BlockSpec((tm, K), lambda n, i: (i, 0)),
                  pl.BlockSpec((K, W), lambda n, i: (0, n))],
        out_specs=pl.BlockSpec((1, tm, W), lambda n, i: (n, i, 0)),
        out_shape=jax.ShapeDtypeStruct((nb, M, W), F32),
        compiler_params=_cparams(("parallel", "parallel")),
    )(a, b)


def _mm_tn(a, b, name, tm=512, tn=512, tk=1024):
    K, M = a.shape
    _, N = b.shape
    tm = _pick(M, (tm, 256, 128))
    tn = _pick(N, (tn, 384, 256, 128))
    tk = _pick(K, (tk, 512, 256))
    nk = K // tk

    def body(a_ref, b_ref, o_ref):
        part = _dot_tn(a_ref[...], b_ref[...])
        k = pl.program_id(2)

        @pl.when(k == 0)
        def _():
            o_ref[...] = part

        @pl.when(k > 0)
        def _():
            o_ref[...] += part

    return pl.pallas_call(
        body, name=name, grid=(M // tm, N // tn, nk),
        in_specs=[pl.BlockSpec((tk, tm), lambda i, j, k: (k, i)),
                  pl.BlockSpec((tk, tn), lambda i, j, k: (k, j))],
        out_specs=pl.BlockSpec((tm, tn), lambda i, j, k: (i, j)),
        out_shape=jax.ShapeDtypeStruct((M, N), F32),
        compiler_params=_cparams(("parallel", "parallel", "arbitrary")),
    )(a, b)


def _rms_fn(x, w):
    r = lax.rsqrt(jnp.mean(x * x, axis=-1, keepdims=True) + EPS)
    return x * r * w


def _rms_in(x, w):
    T, D = x.shape
    tm = _pick(T, (512, 256, 128))

    def body(x_ref, w_ref, o_ref):
        o_ref[...] = _rms_fn(x_ref[...], w_ref[...]).astype(BF16)

    return pl.pallas_call(
        body, name="rms_in", grid=(T // tm,),
        in_specs=[pl.BlockSpec((tm, D), lambda i: (i, 0)), pl.BlockSpec((1, D), lambda i: (0, 0))],
        out_specs=pl.BlockSpec((tm, D), lambda i: (i, 0)),
        out_shape=jax.ShapeDtypeStruct((T, D), BF16),
        compiler_params=_cparams(("parallel",)),
    )(x, w)


def _rms_in_bwd(x, w, dxn, dh):
    T, D = x.shape
    tm = _pick(T, (256, 128))

    def body(x_ref, w_ref, dxn_ref, dh_ref, gx_ref, dw_ref):
        _, vjp = jax.vjp(_rms_fn, x_ref[...], w_ref[...])
        dx, dw = vjp(dxn_ref[...])
        gx_ref[...] = dh_ref[...] + dx

        @pl.when(pl.program_id(0) == 0)
        def _():
            dw_ref[...] = dw

        @pl.when(pl.program_id(0) > 0)
        def _():
            dw_ref[...] += dw

    tile = pl.BlockSpec((tm, D), lambda i: (i, 0))
    row = pl.BlockSpec((1, D), lambda i: (0, 0))
    return pl.pallas_call(
        body, name="rms_in_bwd", grid=(T // tm,),
        in_specs=[tile, row, tile, tile], out_specs=[tile, row],
        out_shape=[jax.ShapeDtypeStruct((T, D), F32), jax.ShapeDtypeStruct((1, D), F32)],
        compiler_params=_cparams(("arbitrary",)),
    )(x, w, dxn, dh)


def _conv_fwd(xcat, w, tm):
    c = None
    for k in range(CONV_WIDTH):
        s = CONV_WIDTH - 1 - k
        xs = xcat if s == 0 else pltpu.roll(xcat, s, 0)
        term = xs[HALO:, :] * w[k:k + 1, :]
        c = term if c is None else c + term
    return c


def _gdn_pointwise(c, ba, alog, dtb, H):
    A = H * HEAD_DIM
    s = _silu(c)
    beta = _sigmoid(ba)
    g = -jnp.exp(alog) * _softplus(ba + dtb)
    sel_row = _iota((LANES, LANES), 0)
    qs, ks, vs, gbs, bbs = [], [], [], [], []
    for h in range(H):
        lo = h * HEAD_DIM
        q = s[:, lo:lo + HEAD_DIM]
        k = s[:, A + lo:A + lo + HEAD_DIM]
        qs.append(q * lax.rsqrt(jnp.sum(q * q, axis=-1, keepdims=True) + EPS))
        ks.append(k * lax.rsqrt(jnp.sum(k * k, axis=-1, keepdims=True) + EPS))
        vs.append(s[:, 2 * A + lo:2 * A + lo + HEAD_DIM])
        bbs.append(_dot(beta, (sel_row == h).astype(F32), hi=True))
        gbs.append(_dot(g, (sel_row == H + h).astype(F32), hi=True))
    st = lambda xs: jnp.stack(xs, axis=0)
    return st(qs), st(ks), st(vs), st(gbs), st(bbs)


def _halo_prev(tm):
    return lambda i: (jnp.maximum(i * (tm // HALO) - 1, 0), 0)


def _gdn_pre(proj_m, proj_ba, conv_w, alog_row, dtb_row, H):
    T = proj_m.shape[0]
    A = H * HEAD_DIM
    tm = _pick(T, (256, 128))
    hs = pl.BlockSpec((H, tm, HEAD_DIM), lambda i: (0, i, 0))
    hshape = jax.ShapeDtypeStruct((H, T, HEAD_DIM), F32)

    def body(x_ref, halo_ref, ba_ref, w_ref, al_ref, dt_ref, q_ref, k_ref, v_ref, gb_ref, bb_ref):
        halo = jnp.where(pl.program_id(0) == 0, 0.0, halo_ref[...])
        c = _conv_fwd(jnp.concatenate([halo, x_ref[...]], axis=0), w_ref[...], tm)
        q, k, v, gb, bb = _gdn_pointwise(c, ba_ref[...], al_ref[...], dt_ref[...], H)
        q_ref[...] = q
        k_ref[...] = k
        v_ref[...] = v
        gb_ref[...] = gb
        bb_ref[...] = bb

    return pl.pallas_call(
        body, name="gdn_pre", grid=(T // tm,),
        in_specs=[pl.BlockSpec((tm, 3 * A), lambda i: (i, 0)),
                  pl.BlockSpec((HALO, 3 * A), _halo_prev(tm)),
                  pl.BlockSpec((tm, LANES), lambda i: (i, 0)),
                  pl.BlockSpec((CONV_WIDTH, 3 * A), lambda i: (0, 0)),
                  pl.BlockSpec((1, LANES), lambda i: (0, 0)),
                  pl.BlockSpec((1, LANES), lambda i: (0, 0))],
        out_specs=[hs] * 5, out_shape=[hshape] * 5,
        compiler_params=_cparams(("parallel",)),
    )(proj_m, proj_m, proj_ba, conv_w, alog_row, dtb_row)


def _gdn_pre_bwd(proj_m, proj_ba, conv_w, alog_row, dtb_row, dq, dk, dv, dgb, dbb, H):
    T = proj_m.shape[0]
    A = H * HEAD_DIM
    tm = _pick(T, (256, 128))
    hs = pl.BlockSpec((H, tm, HEAD_DIM), lambda i: (0, i, 0))
    row = pl.BlockSpec((1, LANES), lambda i: (0, 0))

    def body(x_ref, halo_ref, ba_ref, w_ref, al_ref, dt_ref, dq_ref, dk_ref, dv_ref, dgb_ref, dbb_ref,
             dc_ref, dba_ref, dal_ref, ddt_ref):
        halo = jnp.where(pl.program_id(0) == 0, 0.0, halo_ref[...])
        c = _conv_fwd(jnp.concatenate([halo, x_ref[...]], axis=0), w_ref[...], tm)
        _, vjp = jax.vjp(functools.partial(_gdn_pointwise, H=H), c, ba_ref[...], al_ref[...], dt_ref[...])
        dc, dba, dal, ddt = vjp((dq_ref[...], dk_ref[...], dv_ref[...], dgb_ref[...], dbb_ref[...]))
        dc_ref[...] = dc
        dba_ref[...] = dba.astype(BF16)

        @pl.when(pl.program_id(0) == 0)
        def _():
            dal_ref[...] = dal
            ddt_ref[...] = ddt

        @pl.when(pl.program_id(0) > 0)
        def _():
            dal_ref[...] += dal
            ddt_ref[...] += ddt

    return pl.pallas_call(
        body, name="gdn_pre_bwd", grid=(T // tm,),
        in_specs=[pl.BlockSpec((tm, 3 * A), lambda i: (i, 0)),
                  pl.BlockSpec((HALO, 3 * A), _halo_prev(tm)),
                  pl.BlockSpec((tm, LANES), lambda i: (i, 0)),
                  pl.BlockSpec((CONV_WIDTH, 3 * A), lambda i: (0, 0)),
                  row, row, hs, hs, hs, hs, hs],
        out_specs=[pl.BlockSpec((tm, 3 * A), lambda i: (i, 0)),
                   pl.BlockSpec((tm, LANES), lambda i: (i, 0)), row, row],
        out_shape=[jax.ShapeDtypeStruct((T, 3 * A), F32), jax.ShapeDtypeStruct((T, LANES), BF16),
                   jax.ShapeDtypeStruct((1, LANES), F32), jax.ShapeDtypeStruct((1, LANES), F32)],
        compiler_params=_cparams(("arbitrary",)),
    )(proj_m, proj_m, proj_ba, conv_w, alog_row, dtb_row, dq, dk, dv, dgb, dbb)


def _conv_bwd(proj_m, dc, conv_w, H):
    T = proj_m.shape[0]
    A = H * HEAD_DIM
    tm = _pick(T, (256, 128))
    nt = T // tm

    def body(x_ref, halo_ref, dc_ref, nxt_ref, w_ref, dx_ref, dw_ref):
        i = pl.program_id(0)
        halo = jnp.where(i == 0, 0.0, halo_ref[...])
        xcat = jnp.concatenate([halo, x_ref[...]], axis=0)
        nxt = jnp.where(i == nt - 1, 0.0, nxt_ref[...])
        dc = dc_ref[...]
        dcat = jnp.concatenate([dc, nxt], axis=0)
        w = w_ref[...]
        dx = None
        rows = []
        for k in range(CONV_WIDTH):
            s = CONV_WIDTH - 1 - k
            ds = dcat if s == 0 else pltpu.roll(dcat, tm + HALO - s, 0)
            term = ds[:tm, :] * w[k:k + 1, :]
            dx = term if dx is None else dx + term
            xs = xcat if s == 0 else pltpu.roll(xcat, s, 0)
            rows.append(jnp.sum(dc * xs[HALO:, :], axis=0, keepdims=True))
        dx_ref[...] = dx.astype(BF16)
        dw = jnp.concatenate(rows + [jnp.zeros((HALO - CONV_WIDTH, 3 * A), F32)], axis=0)

        @pl.when(i == 0)
        def _():
            dw_ref[...] = dw

        @pl.when(i > 0)
        def _():
            dw_ref[...] += dw

    return pl.pallas_call(
        body, name="conv_bwd", grid=(nt,),
        in_specs=[pl.BlockSpec((tm, 3 * A), lambda i: (i, 0)),
                  pl.BlockSpec((HALO, 3 * A), _halo_prev(tm)),
                  pl.BlockSpec((tm, 3 * A), lambda i: (i, 0)),
                  pl.BlockSpec((HALO, 3 * A), lambda i: (jnp.minimum((i + 1) * (tm // HALO), T // HALO - 1), 0)),
                  pl.BlockSpec((CONV_WIDTH, 3 * A), lambda i: (0, 0))],
        out_specs=[pl.BlockSpec((tm, 3 * A), lambda i: (i, 0)),
                   pl.BlockSpec((HALO, 3 * A), lambda i: (0, 0))],
        out_shape=[jax.ShapeDtypeStruct((T, 3 * A), BF16), jax.ShapeDtypeStruct((HALO, 3 * A), F32)],
        compiler_params=_cparams(("arbitrary",)),
    )(proj_m, proj_m, dc, dc, conv_w)


PAIR = 2 * CHUNK_A


def _b(x):
    return x.astype(BF16)


@jax.custom_vjp
def _bdot(a, b):
    return _dot(_b(a), _b(b))


def _bdot_f(a, b):
    return _bdot(a, b), (a, b)


def _bdot_b(res, g):
    a, b = res
    return _dot_nt(_b(g), _b(b)), _dot_tn(_b(a), _b(g))


_bdot.defvjp(_bdot_f, _bdot_b)


@jax.custom_vjp
def _bdot_nt(a, b):
    return _dot_nt(_b(a), _b(b))


def _bdot_nt_f(a, b):
    return _bdot_nt(a, b), (a, b)


def _bdot_nt_b(res, g):
    a, b = res
    return _dot(_b(g), _b(b)), _dot_tn(_b(g), _b(a))


_bdot_nt.defvjp(_bdot_nt_f, _bdot_nt_b)


@jax.custom_vjp
def _bdot_tn(a, b):
    return _dot_tn(_b(a), _b(b))


def _bdot_tn_f(a, b):
    return _bdot_tn(a, b), (a, b)


def _bdot_tn_b(res, g):
    a, b = res
    return _dot_nt(_b(b), _b(g)), _dot(_b(a), _b(g))


_bdot_tn.defvjp(_bdot_tn_f, _bdot_tn_b)


def _mask_matmul(m, x):
    hi = _b(x)
    r = x - hi.astype(F32)
    mid = _b(r)
    lo = _b(r - mid.astype(F32))
    return (_dot(m, lo) + _dot(m, mid)) + _dot(m, hi)


@jax.custom_vjp
def _mask_dot(m, mt, x):
    return _mask_matmul(m, x)


def _mask_dot_f(m, mt, x):
    return _mask_matmul(m, x), (m, mt)


def _mask_dot_b(res, g):
    m, mt = res
    return jnp.zeros_like(m), jnp.zeros_like(mt), _mask_matmul(mt, g)


_mask_dot.defvjp(_mask_dot_f, _mask_dot_b)


def _hdot(a, b):
    return lax.dot_general(a, b, (((1,), (0,)), ((), ())), precision=lax.Precision.HIGH,
                           preferred_element_type=F32)


def _hdot_general(a, b, dims):
    return lax.dot_general(a, b, (dims, ((), ())), precision=lax.Precision.HIGH, preferred_element_type=F32)


def _unit_lower_inverse(L):
    n = L.shape[0]
    X = -L
    P = (_iota((n, n), 0) == _iota((n, n), 1)).astype(F32) + X
    for _ in range(CHUNK_A.bit_length() - 2):
        X = _hdot(X, X)
        P = P + _hdot(P, X)
    return P


@jax.custom_vjp
def _known_inverse(L, P):
    return P


def _known_inverse_f(L, P):
    return P, P


def _known_inverse_b(P, g):
    t = _hdot_general(P, g, ((0,), (0,)))
    return -_hdot_general(t, P, ((1,), (1,))), jnp.zeros_like(P)


_known_inverse.defvjp(_known_inverse_f, _known_inverse_b)


def _gdn_prep_fn(q, k, v, gb, bb, P_known=None):
    n = PAIR
    row, col = _iota((n, n), 0), _iota((n, n), 1)
    same = (row >= CHUNK_A) == (col >= CHUNK_A)
    incl = same & (row >= col)
    strict = same & (row > col)
    tril, triu = _b(incl.astype(F32)), _b((same & (row <= col)).astype(F32))
    ones = _b(same.astype(F32))
    gc = _mask_dot(tril, triu, gb)
    gl = _mask_dot(ones, ones, gb)
    decay = jnp.where(incl, jnp.exp(jnp.where(incl, gc - gc.T, 0.0)), 0.0)
    kb = k * bb
    vb = v * bb
    qs = q * (HEAD_DIM ** -0.5)
    L = jnp.where(strict, _bdot_nt(kb, k) * decay, 0.0)
    P = _unit_lower_inverse(L) if P_known is None else _known_inverse(L, P_known)
    egc = jnp.exp(gc)
    u = _bdot(P, vb)
    w = _bdot(P, kb * egc)
    attn = jnp.where(incl, _bdot_nt(qs, k) * decay, 0.0)
    qg = qs * egc
    kdec = k * jnp.exp(gl - gc)
    eg = jnp.exp(gl)
    if P_known is None:
        return u, w, qg, kdec, attn, eg, P
    return u, w, qg, kdec, attn, eg


def _gdn_chain_fn(S, qg, kdec, u, w, attn, eg):
    C = CHUNK_A
    a, b = slice(0, C), slice(C, PAIR)
    vn_a = u[a] - _bdot(w[a], S)
    o_a = _bdot(qg[a], S) + _bdot(attn[a], jnp.concatenate([vn_a, jnp.zeros_like(vn_a)], axis=0))
    S1 = S * jnp.concatenate([eg[a], eg[a]], axis=0) + _bdot_tn(kdec[a], vn_a)
    vn_b = u[b] - _bdot(w[b], S1)
    o_b = _bdot(qg[b], S1) + _bdot(attn[b], jnp.concatenate([vn_a, vn_b], axis=0))
    S2 = S1 * jnp.concatenate([eg[b], eg[b]], axis=0) + _bdot_tn(kdec[b], vn_b)
    return jnp.concatenate([o_a, o_b], axis=0), S2


def _gdn_prep(q, k, v, gb, bb):
    H, T, _ = q.shape
    pb = _pick(T // PAIR, (2, 1))
    hs = pl.BlockSpec((1, PAIR * pb, HEAD_DIM), lambda h, n: (h, n, 0))
    hshape = jax.ShapeDtypeStruct((H, T, HEAD_DIM), F32)

    def body(q_ref, k_ref, v_ref, gb_ref, bb_ref, *out_refs):
        for j in range(pb):
            sl = pl.ds(j * PAIR, PAIR)
            outs = _gdn_prep_fn(q_ref[0, sl, :], k_ref[0, sl, :], v_ref[0, sl, :], gb_ref[0, sl, :],
                                bb_ref[0, sl, :])
            for ref, val in zip(out_refs, outs):
                ref[0, sl, :] = val

    return pl.pallas_call(
        body, name="gdn_prep", grid=(H, T // (PAIR * pb)),
        in_specs=[hs] * 5, out_specs=[hs] * 7, out_shape=[hshape] * 7,
        compiler_params=_cparams(("parallel", "parallel")),
    )(q, k, v, gb, bb)


def _gdn_prep_bwd(q, k, v, gb, bb, pinv, du, dw, dqg, dkd, dat, deg):
    H, T, _ = q.shape
    pb = _pick(T // PAIR, (2, 1))
    hs = pl.BlockSpec((1, PAIR * pb, HEAD_DIM), lambda h, n: (h, n, 0))
    hshape = jax.ShapeDtypeStruct((H, T, HEAD_DIM), F32)

    def body(*refs):
        in_refs, p_ref, ct_refs, out_refs = refs[:5], refs[5], refs[6:12], refs[12:]
        for j in range(pb):
            sl = pl.ds(j * PAIR, PAIR)
            P = p_ref[0, sl, :]
            _, vjp = jax.vjp(lambda *a: _gdn_prep_fn(*a, P_known=P), *[r[0, sl, :] for r in in_refs])
            grads = vjp(tuple(r[0, sl, :] for r in ct_refs))
            for ref, val in zip(out_refs, grads):
                ref[0, sl, :] = val

    return pl.pallas_call(
        body, name="gdn_prep_bwd", grid=(H, T // (PAIR * pb)),
        in_specs=[hs] * 12, out_specs=[hs] * 5, out_shape=[hshape] * 5,
        compiler_params=_cparams(("parallel", "parallel")),
    )(q, k, v, gb, bb, pinv, du, dw, dqg, dkd, dat, deg)


def _gdn_chain(qg, kd, u, w, attn, eg):
    H, T, _ = qg.shape
    N = T // PAIR
    hs = pl.BlockSpec((H, PAIR, HEAD_DIM), lambda n: (0, n, 0))
    ss = pl.BlockSpec((1, H, HEAD_DIM, HEAD_DIM), lambda n: (n, 0, 0, 0))

    def body(qg_ref, kd_ref, u_ref, w_ref, at_ref, eg_ref, o_ref, sall_ref, s_ref):
        @pl.when(pl.program_id(0) == 0)
        def _():
            s_ref[...] = jnp.zeros_like(s_ref)

        for h in range(H):
            S = s_ref[h]
            sall_ref[0, h] = S
            o, S2 = _gdn_chain_fn(S, qg_ref[h], kd_ref[h], u_ref[h], w_ref[h], at_ref[h], eg_ref[h])
            o_ref[h] = o
            s_ref[h] = S2

    return pl.pallas_call(
        body, name="gdn_chain", grid=(N,),
        in_specs=[hs] * 6, out_specs=[hs, ss],
        out_shape=[jax.ShapeDtypeStruct((H, T, HEAD_DIM), F32),
                   jax.ShapeDtypeStruct((N, H, HEAD_DIM, HEAD_DIM), F32)],
        scratch_shapes=[pltpu.VMEM((H, HEAD_DIM, HEAD_DIM), F32)],
        compiler_params=_cparams(("arbitrary",)),
    )(qg, kd, u, w, attn, eg)


def _gdn_chain_bwd(qg, kd, u, w, attn, eg, sall, do):
    H, T, _ = qg.shape
    N = T // PAIR
    hs = pl.BlockSpec((H, PAIR, HEAD_DIM), lambda n: (0, N - 1 - n, 0))
    ss = pl.BlockSpec((1, H, HEAD_DIM, HEAD_DIM), lambda n: (N - 1 - n, 0, 0, 0))
    hshape = jax.ShapeDtypeStruct((H, T, HEAD_DIM), F32)

    def body(qg_ref, kd_ref, u_ref, w_ref, at_ref, eg_ref, sall_ref, do_ref, *rest):
        out_refs, ds_ref = rest[:6], rest[6]

        @pl.when(pl.program_id(0) == 0)
        def _():
            ds_ref[...] = jnp.zeros_like(ds_ref)

        for h in range(H):
            _, vjp = jax.vjp(_gdn_chain_fn, sall_ref[0, h], qg_ref[h], kd_ref[h], u_ref[h], w_ref[h],
                             at_ref[h], eg_ref[h])
            grads = vjp((do_ref[h], ds_ref[h]))
            ds_ref[h] = grads[0]
            for ref, val in zip(out_refs, grads[1:]):
                ref[h] = val

    return pl.pallas_call(
        body, name="gdn_chain_bwd", grid=(N,),
        in_specs=[hs] * 6 + [ss, hs], out_specs=[hs] * 6, out_shape=[hshape] * 6,
        scratch_shapes=[pltpu.VMEM((H, HEAD_DIM, HEAD_DIM), F32)],
        compiler_params=_cparams(("arbitrary",)),
    )(qg, kd, u, w, attn, eg, sall, do)


def _post_fn(ogs, za, hw):
    outs = []
    for h, o in enumerate(ogs):
        r = lax.rsqrt(jnp.mean(o * o, axis=-1, keepdims=True) + EPS)
        outs.append(o * r * hw * _silu(za[:, h * HEAD_DIM:(h + 1) * HEAD_DIM]))
    return jnp.concatenate(outs, axis=1)


def _gdn_post(og, proj_m, hw):
    H, T, _ = og.shape
    A = H * HEAD_DIM
    tm = _pick(T, (512, 256, 128))

    def body(og_ref, za_ref, hw_ref, o_ref):
        o_ref[...] = _post_fn(tuple(og_ref[h] for h in range(H)), za_ref[...], hw_ref[...]).astype(BF16)

    return pl.pallas_call(
        body, name="gdn_post", grid=(T // tm,),
        in_specs=[pl.BlockSpec((H, tm, HEAD_DIM), lambda i: (0, i, 0)),
                  pl.BlockSpec((tm, A), lambda i: (i, 3)),
                  pl.BlockSpec((1, HEAD_DIM), lambda i: (0, 0))],
        out_specs=pl.BlockSpec((tm, A), lambda i: (i, 0)),
        out_shape=jax.ShapeDtypeStruct((T, A), BF16),
        compiler_params=_cparams(("parallel",)),
    )(og, proj_m, hw)


def _gdn_post_bwd(og, proj_m, hw, d_o):
    H, T, _ = og.shape
    A = H * HEAD_DIM
    tm = _pick(T, (256, 128))

    def body(og_ref, za_ref, hw_ref, do_ref, dog_ref, dza_ref, dhw_ref):
        _, vjp = jax.vjp(_post_fn, tuple(og_ref[h] for h in range(H)), za_ref[...], hw_ref[...])
        dog, dza, dhw = vjp(do_ref[...])
        for h in range(H):
            dog_ref[h] = dog[h]
        dza_ref[...] = dza.astype(BF16)

        @pl.when(pl.program_id(0) == 0)
        def _():
            dhw_ref[...] = dhw

        @pl.when(pl.program_id(0) > 0)
        def _():
            dhw_ref[...] += dhw

    return pl.pallas_call(
        body, name="gdn_post_bwd", grid=(T // tm,),
        in_specs=[pl.BlockSpec((H, tm, HEAD_DIM), lambda i: (0, i, 0)),
                  pl.BlockSpec((tm, A), lambda i: (i, 3)),
                  pl.BlockSpec((1, HEAD_DIM), lambda i: (0, 0)),
                  pl.BlockSpec((tm, A), lambda i: (i, 0))],
        out_specs=[pl.BlockSpec((H, tm, HEAD_DIM), lambda i: (0, i, 0)),
                   pl.BlockSpec((tm, A), lambda i: (i, 0)),
                   pl.BlockSpec((1, HEAD_DIM), lambda i: (0, 0))],
        out_shape=[jax.ShapeDtypeStruct((H, T, HEAD_DIM), F32), jax.ShapeDtypeStruct((T, A), BF16),
                   jax.ShapeDtypeStruct((1, HEAD_DIM), F32)],
        compiler_params=_cparams(("arbitrary",)),
    )(og, proj_m, hw, d_o)


def _sgu_fn(ub, vb, zb, lw, lb, W, bbc):
    G = len(W)
    tm = ub.shape[0]
    mu = jnp.mean(vb, axis=-1, keepdims=True)
    xc = vb - mu
    var = jnp.mean(xc * xc, axis=-1, keepdims=True)
    vn = xc * lax.rsqrt(var + EPS) * lw + lb
    mask = _iota((CHUNK_B, CHUNK_B), 0) >= _iota((CHUNK_B, CHUNK_B), 1)
    cols = []
    for g in range(G):
        wm = jnp.where(mask, W[g], 0.0).astype(BF16)
        rows = []
        for c in range(tm // CHUNK_B):
            blk = vn[c * CHUNK_B:(c + 1) * CHUNK_B, g * HEAD_DIM:(g + 1) * HEAD_DIM].astype(BF16)
            rows.append(_dot(wm, blk) + bbc[g])
        cols.append(jnp.concatenate(rows, axis=0) if len(rows) > 1 else rows[0])
    s = jnp.concatenate(cols, axis=1)
    return ub * s * _silu(zb)


def _sgu_cols(A, B):
    base = 4 * A // B
    return base, base + 1, base + 2


def _sgu_fwd(proj_m, lw, lb, W, bbc, A):
    T = proj_m.shape[0]
    G = W.shape[0]
    B = G * HEAD_DIM
    tm = _pick(T, (256, 128))
    cu, cv, cz = _sgu_cols(A, B)

    def body(u_ref, v_ref, z_ref, lw_ref, lb_ref, w_ref, b_ref, o_ref):
        o_ref[...] = _sgu_fn(u_ref[...], v_ref[...], z_ref[...], lw_ref[...], lb_ref[...],
                             tuple(w_ref[g] for g in range(G)), tuple(b_ref[g] for g in range(G))).astype(BF16)

    row = pl.BlockSpec((1, B), lambda i: (0, 0))
    cube = pl.BlockSpec((G, CHUNK_B, CHUNK_B), lambda i: (0, 0, 0))
    return pl.pallas_call(
        body, name="sgu_fwd", grid=(T // tm,),
        in_specs=[pl.BlockSpec((tm, B), lambda i: (i, cu)), pl.BlockSpec((tm, B), lambda i: (i, cv)),
                  pl.BlockSpec((tm, B), lambda i: (i, cz)), row, row, cube, cube],
        out_specs=pl.BlockSpec((tm, B), lambda i: (i, 0)),
        out_shape=jax.ShapeDtypeStruct((T, B), BF16),
        compiler_params=_cparams(("parallel",)),
    )(proj_m, proj_m, proj_m, lw, lb, W, bbc)


def _sgu_bwd(proj_m, lw, lb, W, bbc, d_o, A):
    T = proj_m.shape[0]
    G = W.shape[0]
    B = G * HEAD_DIM
    tm = _pick(T, (256, 128))
    nt = T // tm
    cu, cv, cz = _sgu_cols(A, B)

    def body(u_ref, v_ref, z_ref, lw_ref, lb_ref, w_ref, b_ref, do_ref,
             dp_ref, dlw_ref, dlb_ref, dw_ref, db_ref, dbb_ref):
        _, vjp = jax.vjp(_sgu_fn, u_ref[...], v_ref[...], z_ref[...], lw_ref[...], lb_ref[...],
                         tuple(w_ref[g] for g in range(G)), tuple(b_ref[g] for g in range(G)))
        du, dv, dz, dlw, dlb, dW, dbb = vjp(do_ref[...])
        dW, dbb = jnp.stack(dW, axis=0), jnp.stack(dbb, axis=0)
        dp_ref[:, 0:B] = du.astype(BF16)
        dp_ref[:, B:2 * B] = dv.astype(BF16)
        dp_ref[:, 2 * B:3 * B] = dz.astype(BF16)
        i = pl.program_id(0)

        @pl.when(i == 0)
        def _():
            dlw_ref[...] = dlw
            dlb_ref[...] = dlb
            dw_ref[...] = dW
            dbb_ref[...] = dbb

        @pl.when(i > 0)
        def _():
            dlw_ref[...] += dlw
            dlb_ref[...] += dlb
            dw_ref[...] += dW
            dbb_ref[...] += dbb

        @pl.when(i == nt - 1)
        def _():
            db_ref[...] = jnp.sum(dbb_ref[...], axis=-1, keepdims=True)

    row = pl.BlockSpec((1, B), lambda i: (0, 0))
    cube = pl.BlockSpec((G, CHUNK_B, CHUNK_B), lambda i: (0, 0, 0))
    return pl.pallas_call(
        body, name="sgu_bwd", grid=(nt,),
        in_specs=[pl.BlockSpec((tm, B), lambda i: (i, cu)), pl.BlockSpec((tm, B), lambda i: (i, cv)),
                  pl.BlockSpec((tm, B), lambda i: (i, cz)), row, row, cube, cube,
                  pl.BlockSpec((tm, B), lambda i: (i, A // B))],
        out_specs=[pl.BlockSpec((tm, 3 * B), lambda i: (i, 0)), row, row, cube,
                   pl.BlockSpec((G, CHUNK_B, 1), lambda i: (0, 0, 0))],
        out_shape=[jax.ShapeDtypeStruct((T, 3 * B), BF16), jax.ShapeDtypeStruct((1, B), F32),
                   jax.ShapeDtypeStruct((1, B), F32), jax.ShapeDtypeStruct((G, CHUNK_B, CHUNK_B), F32),
                   jax.ShapeDtypeStruct((G, CHUNK_B, 1), F32)],
        scratch_shapes=[pltpu.VMEM((G, CHUNK_B, CHUNK_B), F32)],
        compiler_params=_cparams(("arbitrary",)),
    )(proj_m, proj_m, proj_m, lw, lb, W, bbc, d_o)


def _head_fn(mix, x, fw, tgt):
    h = x + mix
    y = _rms_fn(h, fw)
    e = y - tgt
    return 0.5 * jnp.sum(jnp.mean(e * e, axis=-1, keepdims=True), axis=0, keepdims=True)


def _out_proj_loss(oa, ob, wout, x, tgt, fw):
    T, A = oa.shape
    B = ob.shape[1]
    D = x.shape[1]
    tm = _pick(T, (256, 128))

    def body(oa_ref, ob_ref, w_ref, x_ref, t_ref, fw_ref, dh_ref, dhb_ref, loss_ref, dfw_ref):
        mix = _dot(oa_ref[...], w_ref[0:A, :]) + _dot(ob_ref[...], w_ref[A:A + B, :])
        xv, tv = x_ref[...], t_ref[...]
        loss, vjp = jax.vjp(lambda m, f: _head_fn(m, xv, f, tv), mix, fw_ref[...])
        dh, dfw = vjp(jnp.ones((1, 1), F32))
        dh_ref[...] = dh
        dhb_ref[...] = dh.astype(BF16)
        lrow = jnp.broadcast_to(loss, (1, LANES))

        @pl.when(pl.program_id(0) == 0)
        def _():
            loss_ref[...] = lrow
            dfw_ref[...] = dfw

        @pl.when(pl.program_id(0) > 0)
        def _():
            loss_ref[...] += lrow
            dfw_ref[...] += dfw

    tile = pl.BlockSpec((tm, D), lambda i: (i, 0))
    return pl.pallas_call(
        body, name="out_proj_loss", grid=(T // tm,),
        in_specs=[pl.BlockSpec((tm, A), lambda i: (i, 0)), pl.BlockSpec((tm, B), lambda i: (i, 0)),
                  pl.BlockSpec((A + B, D), lambda i: (0, 0)), tile, tile,
                  pl.BlockSpec((1, D), lambda i: (0, 0))],
        out_specs=[tile, tile, pl.BlockSpec((1, LANES), lambda i: (0, 0)),
                   pl.BlockSpec((1, D), lambda i: (0, 0))],
        out_shape=[jax.ShapeDtypeStruct((T, D), F32), jax.ShapeDtypeStruct((T, D), BF16),
                   jax.ShapeDtypeStruct((1, LANES), F32), jax.ShapeDtypeStruct((1, D), F32)],
        compiler_params=_cparams(("arbitrary",)),
    )(oa, ob, wout, x, tgt, fw)


def _adamw(w, g, m, v, name):
    R, Cn = w.shape
    tr = R if R * Cn <= 512 * 1024 else _pick(R, (256, 128, 64, 32, 16, 8))

    def body(w_ref, g_ref, m_ref, v_ref, d_ref, mo_ref, vo_ref):
        g = g_ref[...]
        m = ADAM_B1 * m_ref[...] + (1.0 - ADAM_B1) * g
        v = ADAM_B2 * v_ref[...] + (1.0 - ADAM_B2) * jnp.square(g)
        m_hat = m / (1.0 - ADAM_B1 ** ADAM_STEP)
        v_hat = v / (1.0 - ADAM_B2 ** ADAM_STEP)
        d_ref[...] = -ADAM_LR * (m_hat / (jnp.sqrt(v_hat) + ADAM_EPS) + ADAM_WD * w_ref[...])
        mo_ref[...] = m
        vo_ref[...] = v

    tile = pl.BlockSpec((tr, Cn), lambda i: (i, 0))
    shape = jax.ShapeDtypeStruct((R, Cn), F32)
    return pl.pallas_call(
        body, name=name, grid=(R // tr,), in_specs=[tile] * 4, out_specs=[tile] * 3,
        out_shape=[shape] * 3, compiler_params=_cparams(("parallel",)),
    )(w, g, m, v)


def _place():
    x, y, c = lax.axis_index("x"), lax.axis_index("y"), lax.axis_index("c")
    others = [(1 - x, y), (x, 1 - y), (1 - x, 1 - y)]
    return x, y, c, others


def _chip_index(px, py):
    return 2 * px + py


ANY = pl.BlockSpec(memory_space=pl.ANY)


def _gather_weights(win_b, wout_b, conv_b):
    Din, Cb = win_b.shape
    Rb, D = wout_b.shape
    hi, ho = Din // 2, Rb // 2

    def body(win_ref, wout_ref, conv_ref, gin_ref, gout_ref, gconv_ref, send_sems, recv_sems):
        x, y, c, others = _place()
        me = _chip_index(x, y)
        sibling = (x, y, 1 - c)

        def half(ref, chip, core, n):
            return ref.at[chip, pl.ds(core * n, n), :]

        def copy(sem, src, dst, to):
            return pltpu.make_async_remote_copy(src_ref=src, dst_ref=dst, send_sem=send_sems.at[sem],
                                                recv_sem=recv_sems.at[sem], device_id=to, device_id_type=MESH_ID)

        sends = []
        for j, chip in enumerate(others):
            to = (*chip, c)
            sends.append(copy(3 * j, win_ref.at[pl.ds(c * hi, hi), :], half(gin_ref, me, c, hi), to))
            sends.append(copy(3 * j + 1, wout_ref.at[pl.ds(c * ho, ho), :], half(gout_ref, me, c, ho), to))
            sends.append(copy(3 * j + 2, conv_ref, gconv_ref.at[me], to))
        for cp in sends:
            cp.start()
        passed = []
        for j, chip in enumerate(others):
            k = _chip_index(*chip)
            copy(3 * j, half(gin_ref, k, c, hi), half(gin_ref, k, c, hi), (*chip, c)).wait_recv()
            fw = copy(9 + 2 * j, half(gin_ref, k, c, hi), half(gin_ref, k, c, hi), sibling)
            fw.start()
            copy(3 * j + 1, half(gout_ref, k, c, ho), half(gout_ref, k, c, ho), (*chip, c)).wait_recv()
            fo = copy(10 + 2 * j, half(gout_ref, k, c, ho), half(gout_ref, k, c, ho), sibling)
            fo.start()
            copy(3 * j + 2, gconv_ref.at[k], gconv_ref.at[k], (*chip, c)).wait_recv()
            passed += [fw, fo]
        for j, chip in enumerate(others):
            k = _chip_index(*chip)
            copy(9 + 2 * j, half(gin_ref, k, 1 - c, hi), half(gin_ref, k, 1 - c, hi), sibling).wait_recv()
            copy(10 + 2 * j, half(gout_ref, k, 1 - c, ho), half(gout_ref, k, 1 - c, ho), sibling).wait_recv()
        for cp in sends + passed:
            cp.wait_send()

    gin, gout, gconv = pl.pallas_call(
        body, name="gather_weights",
        in_specs=[ANY, ANY, ANY], out_specs=[ANY, ANY, ANY],
        out_shape=[jax.ShapeDtypeStruct((N_CHIPS, Din, Cb), win_b.dtype),
                   jax.ShapeDtypeStruct((N_CHIPS, Rb, D), wout_b.dtype),
                   jax.ShapeDtypeStruct((N_CHIPS,) + conv_b.shape, conv_b.dtype)],
        scratch_shapes=[pltpu.SemaphoreType.DMA((15,)), pltpu.SemaphoreType.DMA((15,))],
        compiler_params=pltpu.CompilerParams(has_side_effects=True),
    )(win_b, wout_b, conv_b)
    me = _chip_index(lax.axis_index("x"), lax.axis_index("y"))
    put = lambda g, own: lax.dynamic_update_index_in_dim(g, own, me, 0)
    return put(gin, win_b), put(gout, wout_b), put(gconv, conv_b)


def _allreduce_small(buf):
    R, L = buf.shape

    def body(in_ref, out_ref, sib_ref, pair_ref, chips_ref, send_sems, recv_sems):
        x, y, c, others = _place()
        me = _chip_index(x, y)
        sibling = (x, y, 1 - c)
        cp = pltpu.make_async_remote_copy(src_ref=in_ref, dst_ref=sib_ref, send_sem=send_sems.at[0],
                                          recv_sem=recv_sems.at[0], device_id=sibling, device_id_type=MESH_ID)
        cp.start()
        cp.wait()
        pair_ref[...] = in_ref[...] + sib_ref[...]
        sends = []
        for j, chip in enumerate(others):
            s = pltpu.make_async_remote_copy(src_ref=pair_ref, dst_ref=chips_ref.at[me],
                                             send_sem=send_sems.at[1 + j], recv_sem=recv_sems.at[1 + j],
                                             device_id=(*chip, c), device_id_type=MESH_ID)
            s.start()
            sends.append(s)
        chips_ref[me] = pair_ref[...]
        for j, chip in enumerate(others):
            k = _chip_index(*chip)
            pltpu.make_async_remote_copy(src_ref=pair_ref, dst_ref=chips_ref.at[k], send_sem=send_sems.at[1 + j],
                                         recv_sem=recv_sems.at[1 + j], device_id=(*chip, c),
                                         device_id_type=MESH_ID).wait_recv()
        for s in sends:
            s.wait_send()
        out_ref[...] = ((chips_ref[0] + chips_ref[1]) + chips_ref[2]) + chips_ref[3]

    vm = pl.BlockSpec(memory_space=pltpu.VMEM)
    return pl.pallas_call(
        body, name="allreduce_small", in_specs=[vm], out_specs=vm,
        out_shape=jax.ShapeDtypeStruct((R, L), F32),
        scratch_shapes=[pltpu.VMEM((R, L), F32), pltpu.VMEM((R, L), F32), pltpu.VMEM((N_CHIPS, R, L), F32),
                        pltpu.SemaphoreType.DMA((4,)), pltpu.SemaphoreType.DMA((4,))],
        compiler_params=pltpu.CompilerParams(vmem_limit_bytes=VMEM_LIMIT),
    )(buf)


def _pair_exchange(gw, go):
    _, Din, Cb = gw.shape
    _, Rb, D = go.shape
    hi, ho = Din // 2, Rb // 2

    def body(gw_ref, go_ref, lw_ref, lo_ref, send_sems, recv_sems):
        x, y, c, _ = _place()
        sibling = (x, y, 1 - c)
        a = pltpu.make_async_remote_copy(src_ref=gw_ref.at[:, pl.ds((1 - c) * hi, hi), :], dst_ref=lw_ref,
                                         send_sem=send_sems.at[0], recv_sem=recv_sems.at[0],
                                         device_id=sibling, device_id_type=MESH_ID)
        b = pltpu.make_async_remote_copy(src_ref=go_ref.at[:, pl.ds((1 - c) * ho, ho), :], dst_ref=lo_ref,
                                         send_sem=send_sems.at[1], recv_sem=recv_sems.at[1],
                                         device_id=sibling, device_id_type=MESH_ID)
        a.start()
        b.start()
        a.wait()
        b.wait()

    return pl.pallas_call(
        body, name="pair_exchange", in_specs=[ANY, ANY], out_specs=[ANY, ANY],
        out_shape=[jax.ShapeDtypeStruct((N_CHIPS, hi, Cb), gw.dtype),
                   jax.ShapeDtypeStruct((N_CHIPS, ho, D), go.dtype)],
        scratch_shapes=[pltpu.SemaphoreType.DMA((2,)), pltpu.SemaphoreType.DMA((2,))],
        compiler_params=pltpu.CompilerParams(has_side_effects=True),
    )(gw, go)


def _pair_sum(g, land, c_arr, name):
    nb, R, Cn = g.shape
    hr = R // 2
    tr = _pick(hr, (256, 128, 64, 32, 16))
    nt = hr // tr

    def body(c_ref, g_ref, l_ref, o_ref):
        o_ref[...] = (g_ref[...] + l_ref[...]).astype(BF16)

    return pl.pallas_call(
        body, name=name,
        grid_spec=pltpu.PrefetchScalarGridSpec(
            num_scalar_prefetch=1, grid=(nb, nt),
            in_specs=[pl.BlockSpec((1, tr, Cn), lambda b, i, c_ref: (b, c_ref[0] * nt + i, 0)),
                      pl.BlockSpec((1, tr, Cn), lambda b, i, c_ref: (b, i, 0))],
            out_specs=pl.BlockSpec((1, tr, Cn), lambda b, i, c_ref: (b, i, 0))),
        out_shape=jax.ShapeDtypeStruct((nb, hr, Cn), BF16),
        compiler_params=_cparams(("parallel", "parallel")),
    )(c_arr, g, land)


def _chip_exchange(pw, po):
    nb, hi, Cb = pw.shape
    _, ho, D = po.shape

    def body(pw_ref, po_ref, qw_ref, qo_ref, send_sems, recv_sems):
        x, y, c, others = _place()
        me = _chip_index(x, y)
        sends = []
        for j, chip in enumerate(others):
            k = _chip_index(*chip)
            for n, (src, dst) in enumerate(((pw_ref, qw_ref), (po_ref, qo_ref))):
                s = pltpu.make_async_remote_copy(src_ref=src.at[k], dst_ref=dst.at[me],
                                                 send_sem=send_sems.at[2 * j + n], recv_sem=recv_sems.at[2 * j + n],
                                                 device_id=(*chip, c), device_id_type=MESH_ID)
                s.start()
                sends.append(s)
        for j, chip in enumerate(others):
            k = _chip_index(*chip)
            for n, (src, dst) in enumerate(((pw_ref, qw_ref), (po_ref, qo_ref))):
                pltpu.make_async_remote_copy(src_ref=src.at[k], dst_ref=dst.at[k],
                                             send_sem=send_sems.at[2 * j + n], recv_sem=recv_sems.at[2 * j + n],
                                             device_id=(*chip, c), device_id_type=MESH_ID).wait_recv()
        for s in sends:
            s.wait_send()

    qw, qo = pl.pallas_call(
        body, name="chip_exchange", in_specs=[ANY, ANY], out_specs=[ANY, ANY],
        out_shape=[jax.ShapeDtypeStruct(pw.shape, pw.dtype), jax.ShapeDtypeStruct(po.shape, po.dtype)],
        scratch_shapes=[pltpu.SemaphoreType.DMA((6,)), pltpu.SemaphoreType.DMA((6,))],
        compiler_params=pltpu.CompilerParams(has_side_effects=True),
    )(pw, po)
    me = _chip_index(lax.axis_index("x"), lax.axis_index("y"))
    put = lambda q, p: lax.dynamic_update_index_in_dim(q, lax.dynamic_index_in_dim(p, me, 0, keepdims=False), me, 0)
    return put(qw, pw), put(qo, po)


def _chip_sum(q, name):
    nb, hr, Cn = q.shape
    tr = _pick(hr, (256, 128, 64, 32, 16))

    def body(q_ref, o_ref):
        f = lambda k: q_ref[k].astype(F32)
        o_ref[...] = ((f(0) + f(1)) + f(2)) + f(3)

    return pl.pallas_call(
        body, name=name, grid=(hr // tr,),
        in_specs=[pl.BlockSpec((nb, tr, Cn), lambda i: (0, i, 0))],
        out_specs=pl.BlockSpec((tr, Cn), lambda i: (i, 0)),
        out_shape=jax.ShapeDtypeStruct((hr, Cn), F32),
        compiler_params=_cparams(("parallel",)),
    )(q)


def _sibling_concat(rw, ro):
    hi, Cb = rw.shape
    ho, D = ro.shape

    def body(rw_ref, ro_ref, tw_ref, to_ref, send_sems, recv_sems):
        x, y, c, _ = _place()
        sibling = (x, y, 1 - c)
        a = pltpu.make_async_remote_copy(src_ref=rw_ref, dst_ref=tw_ref, send_sem=send_sems.at[0],
                                         recv_sem=recv_sems.at[0], device_id=sibling, device_id_type=MESH_ID)
        b = pltpu.make_async_remote_copy(src_ref=ro_ref, dst_ref=to_ref, send_sem=send_sems.at[1],
                                         recv_sem=recv_sems.at[1], device_id=sibling, device_id_type=MESH_ID)
        a.start()
        b.start()
        a.wait()
        b.wait()

    tw, to = pl.pallas_call(
        body, name="sibling_concat", in_specs=[ANY, ANY], out_specs=[ANY, ANY],
        out_shape=[jax.ShapeDtypeStruct((hi, Cb), F32), jax.ShapeDtypeStruct((ho, D), F32)],
        scratch_shapes=[pltpu.SemaphoreType.DMA((2,)), pltpu.SemaphoreType.DMA((2,))],
        compiler_params=pltpu.CompilerParams(has_side_effects=True),
    )(rw, ro)
    c = lax.axis_index("c")
    join = lambda mine, theirs: lax.dynamic_update_slice_in_dim(
        jnp.concatenate([mine, mine], axis=0), theirs, (1 - c) * mine.shape[0], axis=0)
    return join(rw, tw), join(ro, to)


def _pad_lanes(a, n=LANES):
    return jnp.pad(a, ((0, 0), (0, n - a.shape[1])))


def _local_grads(x, tgt, norm_w, g_in, conv_w, a_log, dt_bias, head_norm_w, sgu_ln_w, sgu_ln_b,
                 w_spatial, b_spatial, wout, final_norm_w):
    T, D = x.shape
    H = a_log.shape[1]
    A = H * HEAD_DIM
    G = w_spatial.shape[0]
    B = G * HEAD_DIM
    nb, _, Cb = g_in.shape
    Cp = -(-Cb // LANES) * LANES
    c_ba = 4 * A
    w_full = g_in.transpose(1, 0, 2).reshape(D, nb * Cb)
    w_main = jnp.concatenate([w_full[:, :c_ba], w_full[:, c_ba + 2 * H:]], axis=1)
    w_ba = _pad_lanes(w_full[:, c_ba:c_ba + 2 * H])
    wt_pad = jnp.pad(g_in, ((0, 0), (0, 0), (0, Cp - Cb))).transpose(0, 2, 1).reshape(nb * Cp, D)
    alog_row = jnp.pad(a_log, ((0, 0), (H, LANES - 2 * H)))
    dtb_row = jnp.pad(dt_bias, ((0, 0), (H, LANES - 2 * H)))
    bbc = jnp.broadcast_to(b_spatial[:, :, None], (G, CHUNK_B, CHUNK_B))

    xn = _rms_in(x, norm_w)
    proj_m = _mm_nn(xn, w_main, F32, "in_proj")
    proj_ba = _mm_nn(xn, w_ba, F32, "in_proj_ba")
    q, k, v, gb, bb = _gdn_pre(proj_m, proj_ba, conv_w, alog_row, dtb_row, H)
    u, w, qg, kd, attn, eg, pinv = _gdn_prep(q, k, v, gb, bb)
    og, sall = _gdn_chain(qg, kd, u, w, attn, eg)
    oa = _gdn_post(og, proj_m, head_norm_w)
    ob = _sgu_fwd(proj_m, sgu_ln_w, sgu_ln_b, w_spatial, bbc, A)
    dh, dhb, loss_row, d_fnw = _out_proj_loss(oa, ob, wout, x, tgt, final_norm_w.reshape(1, D))

    d_o = _mm_nn(dhb, wout.T, F32, "out_proj_dx")
    d_wout = jnp.concatenate([_mm_tn(oa, dhb, "out_proj_dw_a"), _mm_tn(ob, dhb, "out_proj_dw_b")], axis=0)
    dpb, d_lw, d_lb, d_ws, d_bs = _sgu_bwd(proj_m, sgu_ln_w, sgu_ln_b, w_spatial, bbc, d_o, A)
    dog, dza, d_hw = _gdn_post_bwd(og, proj_m, head_norm_w, d_o)
    dqg, dkd, du, dw, dat, deg = _gdn_chain_bwd(qg, kd, u, w, attn, eg, sall, dog)
    dq, dk, dv, dgb, dbb = _gdn_prep_bwd(q, k, v, gb, bb, pinv, du, dw, dqg, dkd, dat, deg)
    dc, dba, d_al, d_dt = _gdn_pre_bwd(proj_m, proj_ba, conv_w, alog_row, dtb_row, dq, dk, dv, dgb, dbb, H)
    dqkv, d_conv = _conv_bwd(proj_m, dc, conv_w, H)
    dproj = jnp.concatenate([dqkv, dza, dba[:, :2 * H], dpb], axis=1).reshape(T, nb, Cb)
    dproj = jnp.pad(dproj, ((0, 0), (0, 0), (0, Cp - Cb))).reshape(T, nb * Cp)
    dxn = _mm_rhs_outer(dproj, wt_pad, F32, "in_proj_dx")
    grad_x, d_nw = _rms_in_bwd(x, norm_w, dxn, dh)
    d_win = _mm_col_blocks(xn.T, dproj, nb, "in_proj_dw")
    grads = dict(norm_w=d_nw, w_in=d_win, conv_w=d_conv[:CONV_WIDTH], a_log=d_al[:, H:2 * H],
                 dt_bias=d_dt[:, H:2 * H], head_norm_w=d_hw, sgu_ln_w=d_lw, sgu_ln_b=d_lb, w_spatial=d_ws,
                 b_spatial=d_bs[:, :, 0], w_out=d_wout, final_norm_w=d_fnw)
    return loss_row, grad_x, grads


SMALL = ("norm_w", "conv_w", "a_log", "dt_bias", "head_norm_w", "sgu_ln_w", "sgu_ln_b", "w_spatial",
         "b_spatial", "final_norm_w")


def _pack(parts):
    rows = []
    for p in parts:
        f = p.reshape(-1)
        f = jnp.pad(f, (0, (-f.shape[0]) % (8 * LANES)))
        rows.append(f.reshape(-1, LANES))
    return jnp.concatenate(rows, axis=0)


def _unpack(buf, shapes):
    out, r = [], 0
    for s in shapes:
        n = 1
        for d in s:
            n *= d
        nr = -(-n // (8 * LANES)) * 8
        out.append(buf[r:r + nr].reshape(-1)[:n].reshape(s))
        r += nr
    return out


def kernel(x, norm_w, w_in, conv_w, a_log, dt_bias, head_norm_w, sgu_ln_w, sgu_ln_b, w_spatial, b_spatial, w_out, final_norm_w, loss_target, m_norm_w, m_w_in, m_conv_w, m_a_log, m_dt_bias, m_head_norm_w, m_sgu_ln_w, m_sgu_ln_b, m_w_spatial, m_b_spatial, m_w_out, m_final_norm_w, v_norm_w, v_w_in, v_conv_w, v_a_log, v_dt_bias, v_head_norm_w, v_sgu_ln_w, v_sgu_ln_b, v_w_spatial, v_b_spatial, v_w_out, v_final_norm_w):
    T, D = x.shape[1], x.shape[2]
    weights = dict(norm_w=norm_w, w_in=w_in, conv_w=conv_w, a_log=a_log, dt_bias=dt_bias, head_norm_w=head_norm_w,
                   sgu_ln_w=sgu_ln_w, sgu_ln_b=sgu_ln_b, w_spatial=w_spatial, b_spatial=b_spatial, w_out=w_out,
                   final_norm_w=final_norm_w)
    mom_m = dict(norm_w=m_norm_w, w_in=m_w_in, conv_w=m_conv_w, a_log=m_a_log, dt_bias=m_dt_bias,
                 head_norm_w=m_head_norm_w, sgu_ln_w=m_sgu_ln_w, sgu_ln_b=m_sgu_ln_b, w_spatial=m_w_spatial,
                 b_spatial=m_b_spatial, w_out=m_w_out, final_norm_w=m_final_norm_w)
    mom_v = dict(norm_w=v_norm_w, w_in=v_w_in, conv_w=v_conv_w, a_log=v_a_log, dt_bias=v_dt_bias,
                 head_norm_w=v_head_norm_w, sgu_ln_w=v_sgu_ln_w, sgu_ln_b=v_sgu_ln_b, w_spatial=v_w_spatial,
                 b_spatial=v_b_spatial, w_out=v_w_out, final_norm_w=v_final_norm_w)
    me = _chip_index(lax.axis_index("x"), lax.axis_index("y"))
    c_arr = lax.axis_index("c").astype(jnp.int32).reshape(1)
    Din, Cb = w_in.shape[1], w_in.shape[2]
    Rb = w_out.shape[1]
    cconv = conv_w.shape[2]

    g_in, g_out, g_conv = _gather_weights(w_in[0].astype(BF16), w_out[0].astype(BF16), conv_w[0])
    wout_full = g_out.reshape(N_CHIPS * Rb, D)
    conv_full = g_conv.transpose(1, 0, 2).reshape(CONV_WIDTH, N_CHIPS * cconv)

    loss_row, grad_x, g = _local_grads(
        x[0], loss_target[0], norm_w, g_in, conv_full, a_log, dt_bias, head_norm_w, sgu_ln_w, sgu_ln_b,
        w_spatial[0], b_spatial[0], wout_full, final_norm_w)

    gw = g["w_in"]
    go = g["w_out"].reshape(N_CHIPS, Rb, D)
    lw, lo = _pair_exchange(gw, go)
    qw, qo = _chip_exchange(_pair_sum(gw, lw, c_arr, "pair_sum_w_in"), _pair_sum(go, lo, c_arr, "pair_sum_w_out"))
    gsum_in, gsum_out = _sibling_concat(_chip_sum(qw, "chip_sum_w_in"), _chip_sum(qo, "chip_sum_w_out"))
    gsum_in = gsum_in[:, :Cb]
    small_shapes = [tuple(g[n].shape) for n in SMALL]
    small = _unpack(_allreduce_small(_pack([g[n] for n in SMALL])), small_shapes)
    gsmall = dict(zip(SMALL, small))
    gsmall["conv_w"] = lax.dynamic_slice_in_dim(gsmall["conv_w"], me * cconv, cconv, axis=1)

    grads, deltas, new_m, new_v = {}, {}, {}, {}
    for n, gs in (("w_in", gsum_in), ("w_out", gsum_out)):
        d, m2, v2 = _adamw(weights[n][0], gs, mom_m[n][0], mom_v[n][0], "adamw_" + n)
        grads[n], deltas[n], new_m[n], new_v[n] = gs[None], d[None], m2[None], v2[None]
    shapes = [tuple(weights[n].shape) for n in SMALL]
    ds, ms, vs = _adamw(_pack([weights[n] for n in SMALL]), _pack([gsmall[n] for n in SMALL]),
                        _pack([mom_m[n] for n in SMALL]), _pack([mom_v[n] for n in SMALL]), "adamw_small")
    for n, gq, d, m2, v2 in zip(SMALL, [gsmall[n] for n in SMALL], _unpack(ds, shapes), _unpack(ms, shapes),
                                _unpack(vs, shapes)):
        grads[n], deltas[n], new_m[n], new_v[n] = gq.reshape(weights[n].shape), d, m2, v2

    loss = lax.psum(loss_row[0, 0], ("x", "y", "c"))
    order = ("norm_w", "w_in", "conv_w", "a_log", "dt_bias", "head_norm_w", "sgu_ln_w", "sgu_ln_b", "w_spatial",
             "b_spatial", "w_out", "final_norm_w")
    return (loss, grad_x[None], *[grads[n] for n in order], *[deltas[n] for n in order],
            *[new_m[n] for n in order], *[new_v[n] for n in order])
```

```python
import functools

import jax
import jax.numpy as jnp
from jax import lax
from jax.experimental import pallas as pl
from jax.experimental.pallas import tpu as pltpu

F32 = jnp.float32
BF16 = jnp.bfloat16
EPS = 1e-6
HEAD_DIM = 128
CHUNK_A = 64
CHUNK_B = 128
CONV_WIDTH = 4
LANES = 128
HALO = 8
N_CHIPS = 4
ADAM_LR = 0.001
ADAM_B1 = 0.9
ADAM_B2 = 0.999
ADAM_EPS = 1e-08
ADAM_WD = 0.01
ADAM_STEP = 10
VMEM_LIMIT = 56 * 1024 * 1024
MESH_ID = pl.DeviceIdType.MESH
HI = lax.Precision.HIGHEST


def _cparams(sem=None, **kw):
    return pltpu.CompilerParams(dimension_semantics=sem, vmem_limit_bytes=VMEM_LIMIT, **kw)


def _matmul(a, b, ca, cb, precision):
    nb = a.ndim - 2
    batch = tuple(range(nb))
    return lax.dot_general(a, b, (((ca + nb,), (cb + nb,)), (batch, batch)), precision=precision,
                           preferred_element_type=F32)


def _dot(a, b, hi=False, precision=None):
    return _matmul(a, b, 1, 0, HI if hi else precision)


def _dot_nt(a, b, hi=False, precision=None):
    return _matmul(a, b, 1, 1, HI if hi else precision)


def _dot_tn(a, b, hi=False, precision=None):
    return _matmul(a, b, 0, 0, HI if hi else precision)


def _iota(shape, dim):
    return lax.broadcasted_iota(jnp.int32, shape, dim)


def _sigmoid(x):
    return 0.5 * (jnp.tanh(0.5 * x) + 1.0)


def _silu(x):
    return x * _sigmoid(x)


def _softplus(x):
    z = jnp.exp(-jnp.abs(x))
    small = z * (1.0 - z * (0.5 - z * (1.0 / 3.0)))
    return jnp.maximum(x, 0.0) + jnp.where(z < 1e-3, small, jnp.log(1.0 + z))


def _pick(n, pref):
    for t in pref:
        if n % t == 0:
            return t
    return n


def _mm_nn(a, b, out_dtype, name, tm=1024, tn=512, tk=None, cols=None):
    M, K = a.shape
    c0, N = (0, b.shape[1]) if cols is None else cols
    tm = _pick(M, (tm, 512, 256, 128))
    tn = _pick(N, (tn, 512, 384, 256, 128))
    tk = K if tk is None else _pick(K, (tk,))
    nk = K // tk
    j0 = c0 // tn
    assert c0 % tn == 0

    def body(a_ref, b_ref, o_ref, *scratch):
        part = _dot(a_ref[...], b_ref[...])
        if nk == 1:
            o_ref[...] = part.astype(out_dtype)
        else:
            acc_ref, = scratch
            k = pl.program_id(2)

            @pl.when(k == 0)
            def _():
                acc_ref[...] = part

            @pl.when(k > 0)
            def _():
                acc_ref[...] += part

            @pl.when(k == nk - 1)
            def _():
                o_ref[...] = acc_ref[...].astype(out_dtype)

    return pl.pallas_call(
        body, name=name, grid=(M // tm, N // tn, nk),
        in_specs=[pl.BlockSpec((tm, tk), lambda i, j, k: (i, k)),
                  pl.BlockSpec((tk, tn), lambda i, j, k: (k, j + j0))],
        out_specs=pl.BlockSpec((tm, tn), lambda i, j, k: (i, j)),
        out_shape=jax.ShapeDtypeStruct((M, N), out_dtype),
        scratch_shapes=[] if nk == 1 else [pltpu.VMEM((tm, tn), F32)],
        compiler_params=_cparams(("parallel", "parallel", "arbitrary")),
    )(a, b)


def _mm_nt_rhs_outer(a, b, out_dtype, name, tm=256, tn=1024):
    M, K = a.shape
    N, _ = b.shape
    tm = _pick(M, (tm, 128))
    tn = _pick(N, (tn, 512, 256, 128))

    def body(a_ref, b_ref, o_ref):
        o_ref[...] = _dot_nt(a_ref[...], b_ref[...]).astype(out_dtype)

    return pl.pallas_call(
        body, name=name, grid=(N // tn, M // tm),
        in_specs=[pl.BlockSpec((tm, K), lambda j, i: (i, 0)),
                  pl.BlockSpec((tn, K), lambda j, i: (j, 0))],
        out_specs=pl.BlockSpec((tm, tn), lambda j, i: (i, j)),
        out_shape=jax.ShapeDtypeStruct((M, N), out_dtype),
        compiler_params=_cparams(("parallel", "parallel")),
    )(a, b)


WIN_BLOCK = 256


def _mm_windows(a, b, table, nb, name, tm=1024):
    M, K = a.shape
    wb = table.shape[0] // nb
    tm = _pick(M, (tm, 512, 256, 128))

    def body(tab_ref, a_ref, b_ref, o_ref):
        o_ref[0] = _dot(a_ref[...], b_ref[...])

    return pl.pallas_call(
        body, name=name,
        grid_spec=pltpu.PrefetchScalarGridSpec(
            num_scalar_prefetch=1, grid=(nb, M // tm, wb),
            in_specs=[pl.BlockSpec((tm, K), lambda n, i, t, tab: (i, 0)),
                      pl.BlockSpec((K, WIN_BLOCK), lambda n, i, t, tab: (0, tab[n * wb + t]))],
            out_specs=pl.BlockSpec((1, tm, WIN_BLOCK), lambda n, i, t, tab: (n, i, t))),
        out_shape=jax.ShapeDtypeStruct((nb, M, wb * WIN_BLOCK), F32),
        compiler_params=_cparams(("parallel", "parallel", "arbitrary")),
    )(table, a, b)


def _mm_tn(a, b, name, tm=512, tn=512, tk=1024):
    K, M = a.shape
    _, N = b.shape
    tm = _pick(M, (tm, 256, 128))
    tn = _pick(N, (tn, 384, 256, 128))
    tk = _pick(K, (tk, 512, 256))
    nk = K // tk

    def body(a_ref, b_ref, o_ref):
        part = _dot_tn(a_ref[...], b_ref[...])
        k = pl.program_id(2)

        @pl.when(k == 0)
        def _():
            o_ref[...] = part

        @pl.when(k > 0)
        def _():
            o_ref[...] += part

    return pl.pallas_call(
        body, name=name, grid=(M // tm, N // tn, nk),
        in_specs=[pl.BlockSpec((tk, tm), lambda i, j, k: (k, i)),
                  pl.BlockSpec((tk, tn), lambda i, j, k: (k, j))],
        out_specs=pl.BlockSpec((tm, tn), lambda i, j, k: (i, j)),
        out_shape=jax.ShapeDtypeStruct((M, N), F32),
        compiler_params=_cparams(("parallel", "parallel", "arbitrary")),
    )(a, b)


def _rms_fn(x, w):
    r = lax.rsqrt(jnp.mean(x * x, axis=-1, keepdims=True) + EPS)
    return x * r * w


def _rms_in(x, w):
    T, D = x.shape
    tm = _pick(T, (512, 256, 128))

    def body(x_ref, w_ref, o_ref, ot_ref):
        xn = _rms_fn(x_ref[...], w_ref[...])
        o_ref[...] = xn.astype(BF16)
        ot_ref[...] = xn.T.astype(BF16)

    return pl.pallas_call(
        body, name="rms_in", grid=(T // tm,),
        in_specs=[pl.BlockSpec((tm, D), lambda i: (i, 0)), pl.BlockSpec((1, D), lambda i: (0, 0))],
        out_specs=[pl.BlockSpec((tm, D), lambda i: (i, 0)), pl.BlockSpec((D, tm), lambda i: (0, i))],
        out_shape=[jax.ShapeDtypeStruct((T, D), BF16), jax.ShapeDtypeStruct((D, T), BF16)],
        compiler_params=_cparams(("parallel",)),
    )(x, w)


def _rms_in_bwd(x, w, dxn, dh):
    T, D = x.shape
    tm = _pick(T, (256, 128))

    def body(x_ref, w_ref, dxn_ref, dh_ref, gx_ref, dw_ref):
        _, vjp = jax.vjp(_rms_fn, x_ref[...], w_ref[...])
        dx, dw = vjp(dxn_ref[...])
        gx_ref[...] = dh_ref[...] + dx

        @pl.when(pl.program_id(0) == 0)
        def _():
            dw_ref[...] = dw

        @pl.when(pl.program_id(0) > 0)
        def _():
            dw_ref[...] += dw

    tile = pl.BlockSpec((tm, D), lambda i: (i, 0))
    row = pl.BlockSpec((1, D), lambda i: (0, 0))
    return pl.pallas_call(
        body, name="rms_in_bwd", grid=(T // tm,),
        in_specs=[tile, row, tile, tile], out_specs=[tile, row],
        out_shape=[jax.ShapeDtypeStruct((T, D), F32), jax.ShapeDtypeStruct((1, D), F32)],
        compiler_params=_cparams(("arbitrary",)),
    )(x, w, dxn, dh)


def _conv_fwd(xcat, w, tm):
    c = None
    for k in range(CONV_WIDTH):
        s = CONV_WIDTH - 1 - k
        xs = xcat if s == 0 else pltpu.roll(xcat, s, 0)
        term = xs[HALO:, :] * w[k:k + 1, :]
        c = term if c is None else c + term
    return c


def _gdn_pointwise(c, ba, alog, dtb, H):
    A = H * HEAD_DIM
    s = _silu(c)
    beta = _sigmoid(ba)
    g = -jnp.exp(alog) * _softplus(ba + dtb)
    sel_row = _iota((LANES, LANES), 0)
    qs, ks, vs, gbs, bbs = [], [], [], [], []
    for h in range(H):
        lo = h * HEAD_DIM
        q = s[:, lo:lo + HEAD_DIM]
        k = s[:, A + lo:A + lo + HEAD_DIM]
        qs.append(q * lax.rsqrt(jnp.sum(q * q, axis=-1, keepdims=True) + EPS))
        ks.append(k * lax.rsqrt(jnp.sum(k * k, axis=-1, keepdims=True) + EPS))
        vs.append(s[:, 2 * A + lo:2 * A + lo + HEAD_DIM])
        bbs.append(_dot(beta, (sel_row == h).astype(F32), hi=True))
        gbs.append(_dot(g, (sel_row == H + h).astype(F32), hi=True))
    st = lambda xs: jnp.stack(xs, axis=0)
    return st(qs), st(ks), st(vs), st(gbs), st(bbs)


def _halo_prev(tm):
    return lambda i: (jnp.maximum(i * (tm // HALO) - 1, 0), 0)


def _gdn_pre(proj_m, proj_ba, conv_w, alog_row, dtb_row, H):
    T = proj_m.shape[0]
    A = H * HEAD_DIM
    tm = _pick(T, (256, 128))
    hs = pl.BlockSpec((H, tm, HEAD_DIM), lambda i: (0, i, 0))
    hshape = jax.ShapeDtypeStruct((H, T, HEAD_DIM), F32)

    def body(x_ref, halo_ref, ba_ref, w_ref, al_ref, dt_ref, q_ref, k_ref, v_ref, gb_ref, bb_ref):
        halo = jnp.where(pl.program_id(0) == 0, 0.0, halo_ref[...])
        c = _conv_fwd(jnp.concatenate([halo, x_ref[...]], axis=0), w_ref[...], tm)
        q, k, v, gb, bb = _gdn_pointwise(c, ba_ref[...], al_ref[...], dt_ref[...], H)
        q_ref[...] = q
        k_ref[...] = k
        v_ref[...] = v
        gb_ref[...] = gb
        bb_ref[...] = bb

    return pl.pallas_call(
        body, name="gdn_pre", grid=(T // tm,),
        in_specs=[pl.BlockSpec((tm, 3 * A), lambda i: (i, 0)),
                  pl.BlockSpec((HALO, 3 * A), _halo_prev(tm)),
                  pl.BlockSpec((tm, LANES), lambda i: (i, 0)),
                  pl.BlockSpec((CONV_WIDTH, 3 * A), lambda i: (0, 0)),
                  pl.BlockSpec((1, LANES), lambda i: (0, 0)),
                  pl.BlockSpec((1, LANES), lambda i: (0, 0))],
        out_specs=[hs] * 5, out_shape=[hshape] * 5,
        compiler_params=_cparams(("parallel",)),
    )(proj_m, proj_m, proj_ba, conv_w, alog_row, dtb_row)


def _gdn_pre_bwd(proj_m, proj_ba, conv_w, alog_row, dtb_row, dq, dk, dv, dgb, dbb, H, dproj):
    T, n_main = proj_m.shape
    A = H * HEAD_DIM
    tm = _pick(T, (256, 128))
    hs = pl.BlockSpec((H, tm, HEAD_DIM), lambda i: (0, i, 0))
    row = pl.BlockSpec((1, LANES), lambda i: (0, 0))

    def body(x_ref, halo_ref, ba_ref, w_ref, al_ref, dt_ref, dq_ref, dk_ref, dv_ref, dgb_ref, dbb_ref, _,
             dc_ref, dba_ref, dal_ref, ddt_ref):
        halo = jnp.where(pl.program_id(0) == 0, 0.0, halo_ref[...])
        c = _conv_fwd(jnp.concatenate([halo, x_ref[...]], axis=0), w_ref[...], tm)
        _, vjp = jax.vjp(functools.partial(_gdn_pointwise, H=H), c, ba_ref[...], al_ref[...], dt_ref[...])
        dc, dba, dal, ddt = vjp((dq_ref[...], dk_ref[...], dv_ref[...], dgb_ref[...], dbb_ref[...]))
        dc_ref[...] = dc
        dba_ref[:, :LANES] = dba.astype(BF16)
        dba_ref[:, LANES:] = jnp.zeros((tm, WIN_BLOCK - LANES), BF16)

        @pl.when(pl.program_id(0) == 0)
        def _():
            dal_ref[...] = dal
            ddt_ref[...] = ddt

        @pl.when(pl.program_id(0) > 0)
        def _():
            dal_ref[...] += dal
            ddt_ref[...] += ddt

    return pl.pallas_call(
        body, name="gdn_pre_bwd", grid=(T // tm,),
        in_specs=[pl.BlockSpec((tm, 3 * A), lambda i: (i, 0)),
                  pl.BlockSpec((HALO, 3 * A), _halo_prev(tm)),
                  pl.BlockSpec((tm, LANES), lambda i: (i, 0)),
                  pl.BlockSpec((CONV_WIDTH, 3 * A), lambda i: (0, 0)),
                  row, row, hs, hs, hs, hs, hs, ANY],
        out_specs=[pl.BlockSpec((tm, 3 * A), lambda i: (i, 0)),
                   pl.BlockSpec((tm, WIN_BLOCK), lambda i: (i, n_main // WIN_BLOCK)), row, row],
        out_shape=[jax.ShapeDtypeStruct((T, 3 * A), F32), jax.ShapeDtypeStruct(dproj.shape, dproj.dtype),
                   jax.ShapeDtypeStruct((1, LANES), F32), jax.ShapeDtypeStruct((1, LANES), F32)],
        input_output_aliases={11: 1},
        compiler_params=_cparams(("arbitrary",)),
    )(proj_m, proj_m, proj_ba, conv_w, alog_row, dtb_row, dq, dk, dv, dgb, dbb, dproj)


def _conv_bwd(proj_m, dc, conv_w, H, dproj):
    T = proj_m.shape[0]
    A = H * HEAD_DIM
    tm = _pick(T, (256, 128))
    nt = T // tm

    def body(x_ref, halo_ref, dc_ref, nxt_ref, w_ref, _, dx_ref, dw_ref):
        i = pl.program_id(0)
        halo = jnp.where(i == 0, 0.0, halo_ref[...])
        xcat = jnp.concatenate([halo, x_ref[...]], axis=0)
        nxt = jnp.where(i == nt - 1, 0.0, nxt_ref[...])
        dc = dc_ref[...]
        dcat = jnp.concatenate([dc, nxt], axis=0)
        w = w_ref[...]
        dx = None
        rows = []
        for k in range(CONV_WIDTH):
            s = CONV_WIDTH - 1 - k
            ds = dcat if s == 0 else pltpu.roll(dcat, tm + HALO - s, 0)
            term = ds[:tm, :] * w[k:k + 1, :]
            dx = term if dx is None else dx + term
            xs = xcat if s == 0 else pltpu.roll(xcat, s, 0)
            rows.append(jnp.sum(dc * xs[HALO:, :], axis=0, keepdims=True))
        dx_ref[...] = dx.astype(BF16)
        dw = jnp.concatenate(rows + [jnp.zeros((HALO - CONV_WIDTH, 3 * A), F32)], axis=0)

        @pl.when(i == 0)
        def _():
            dw_ref[...] = dw

        @pl.when(i > 0)
        def _():
            dw_ref[...] += dw

    return pl.pallas_call(
        body, name="conv_bwd", grid=(nt,),
        in_specs=[pl.BlockSpec((tm, 3 * A), lambda i: (i, 0)),
                  pl.BlockSpec((HALO, 3 * A), _halo_prev(tm)),
                  pl.BlockSpec((tm, 3 * A), lambda i: (i, 0)),
                  pl.BlockSpec((HALO, 3 * A), lambda i: (jnp.minimum((i + 1) * (tm // HALO), T // HALO - 1), 0)),
                  pl.BlockSpec((CONV_WIDTH, 3 * A), lambda i: (0, 0)), ANY],
        out_specs=[pl.BlockSpec((tm, 3 * A), lambda i: (i, 0)),
                   pl.BlockSpec((HALO, 3 * A), lambda i: (0, 0))],
        out_shape=[jax.ShapeDtypeStruct(dproj.shape, dproj.dtype), jax.ShapeDtypeStruct((HALO, 3 * A), F32)],
        input_output_aliases={5: 0},
        compiler_params=_cparams(("arbitrary",)),
    )(proj_m, proj_m, dc, dc, conv_w, dproj)


PAIR = 2 * CHUNK_A


def _b(x):
    return x.astype(BF16)


@jax.custom_vjp
def _bdot(a, b):
    return _dot(_b(a), _b(b))


def _bdot_f(a, b):
    return _bdot(a, b), (a, b)


def _bdot_b(res, g):
    a, b = res
    return _dot_nt(_b(g), _b(b)), _dot_tn(_b(a), _b(g))


_bdot.defvjp(_bdot_f, _bdot_b)


@jax.custom_vjp
def _bdot_nt(a, b):
    return _dot_nt(_b(a), _b(b))


def _bdot_nt_f(a, b):
    return _bdot_nt(a, b), (a, b)


def _bdot_nt_b(res, g):
    a, b = res
    return _dot(_b(g), _b(b)), _dot_tn(_b(g), _b(a))


_bdot_nt.defvjp(_bdot_nt_f, _bdot_nt_b)


@jax.custom_vjp
def _bdot_tn(a, b):
    return _dot_tn(_b(a), _b(b))


def _bdot_tn_f(a, b):
    return _bdot_tn(a, b), (a, b)


def _bdot_tn_b(res, g):
    a, b = res
    return _dot_nt(_b(b), _b(g)), _dot(_b(a), _b(g))


_bdot_tn.defvjp(_bdot_tn_f, _bdot_tn_b)


def _mask_matmul(m, x):
    hi = _b(x)
    r = x - hi.astype(F32)
    mid = _b(r)
    lo = _b(r - mid.astype(F32))
    return (_dot(m, lo) + _dot(m, mid)) + _dot(m, hi)


@jax.custom_vjp
def _mask_dot(m, mt, x):
    return _mask_matmul(m, x)


def _mask_dot_f(m, mt, x):
    return _mask_matmul(m, x), (m, mt)


def _mask_dot_b(res, g):
    m, mt = res
    return jnp.zeros_like(m), jnp.zeros_like(mt), _mask_matmul(mt, g)


_mask_dot.defvjp(_mask_dot_f, _mask_dot_b)

HIGH = lax.Precision.HIGH


def _unit_lower_inverse(L):
    n = L.shape[-1]
    X = -L
    P = (_iota((n, n), 0) == _iota((n, n), 1)).astype(F32) + X
    for _ in range(CHUNK_A.bit_length() - 2):
        X = _dot(X, X, precision=HIGH)
        P = P + _dot(P, X, precision=HIGH)
    return P


@jax.custom_vjp
def _known_inverse(L, P):
    return P


def _known_inverse_f(L, P):
    return P, P


def _known_inverse_b(P, g):
    t = _dot_tn(P, g, precision=HIGH)
    return -_dot_nt(t, P, precision=HIGH), jnp.zeros_like(P)


_known_inverse.defvjp(_known_inverse_f, _known_inverse_b)


def _gdn_prep_fn(q, k, v, gb, bb, P_known=None):
    n = PAIR
    row, col = _iota((n, n), 0), _iota((n, n), 1)
    same = (row >= CHUNK_A) == (col >= CHUNK_A)
    incl = same & (row >= col)
    strict = same & (row > col)
    bc = lambda m: jnp.broadcast_to(_b(m.astype(F32)), q.shape[:1] + (n, n))
    tril, triu, ones = bc(incl), bc(same & (row <= col)), bc(same)
    gc = _mask_dot(tril, triu, gb)
    gl = _mask_dot(ones, ones, gb)
    decay = jnp.where(incl, jnp.exp(jnp.where(incl, gc - jnp.swapaxes(gc, 1, 2), 0.0)), 0.0)
    kb = k * bb
    vb = v * bb
    qs = q * (HEAD_DIM ** -0.5)
    L = jnp.where(strict, _bdot_nt(kb, k) * decay, 0.0)
    P = _unit_lower_inverse(L) if P_known is None else _known_inverse(L, P_known)
    egc = jnp.exp(gc)
    u = _bdot(P, vb)
    w = _bdot(P, kb * egc)
    attn = jnp.where(incl, _bdot_nt(qs, k) * decay, 0.0)
    qg = qs * egc
    kdec = k * jnp.exp(gl - gc)
    eg = jnp.exp(gl)
    if P_known is None:
        return u, w, qg, kdec, attn, eg, P
    return u, w, qg, kdec, attn, eg


def _gdn_chain_fn(S, qg, kdec, u, w, attn, eg):
    C = CHUNK_A
    a, b = (slice(None), slice(0, C)), (slice(None), slice(C, PAIR))
    cat = lambda xs: jnp.concatenate(xs, axis=1)
    vn_a = u[a] - _bdot(w[a], S)
    o_a = _bdot(qg[a], S) + _bdot(attn[a], cat([vn_a, jnp.zeros_like(vn_a)]))
    S1 = S * cat([eg[a], eg[a]]) + _bdot_tn(kdec[a], vn_a)
    vn_b = u[b] - _bdot(w[b], S1)
    o_b = _bdot(qg[b], S1) + _bdot(attn[b], cat([vn_a, vn_b]))
    S2 = S1 * cat([eg[b], eg[b]]) + _bdot_tn(kdec[b], vn_b)
    return cat([o_a, o_b]), S2


def _gdn_prep(q, k, v, gb, bb):
    H, T, _ = q.shape
    pb = _pick(T // PAIR, (8, 4, 2, 1))
    hs = pl.BlockSpec((1, PAIR * pb, HEAD_DIM), lambda h, n: (h, n, 0))
    hshape = jax.ShapeDtypeStruct((H, T, HEAD_DIM), F32)

    def body(q_ref, k_ref, v_ref, gb_ref, bb_ref, *out_refs):
        pairs = lambda ref: ref[0].reshape(pb, PAIR, HEAD_DIM)
        outs = _gdn_prep_fn(pairs(q_ref), pairs(k_ref), pairs(v_ref), pairs(gb_ref), pairs(bb_ref))
        for ref, val in zip(out_refs, outs):
            ref[0] = val.reshape(pb * PAIR, HEAD_DIM)

    return pl.pallas_call(
        body, name="gdn_prep", grid=(H, T // (PAIR * pb)),
        in_specs=[hs] * 5, out_specs=[hs] * 7, out_shape=[hshape] * 7,
        compiler_params=_cparams(("parallel", "parallel")),
    )(q, k, v, gb, bb)


def _gdn_prep_bwd(q, k, v, gb, bb, pinv, du, dw, dqg, dkd, dat, deg):
    H, T, _ = q.shape
    pb = _pick(T // PAIR, (8, 4, 2, 1))
    hs = pl.BlockSpec((1, PAIR * pb, HEAD_DIM), lambda h, n: (h, n, 0))
    hshape = jax.ShapeDtypeStruct((H, T, HEAD_DIM), F32)

    def body(*refs):
        in_refs, p_ref, ct_refs, out_refs = refs[:5], refs[5], refs[6:12], refs[12:]
        pairs = lambda ref: ref[0].reshape(pb, PAIR, HEAD_DIM)
        P = pairs(p_ref)
        _, vjp = jax.vjp(lambda *a: _gdn_prep_fn(*a, P_known=P), *[pairs(r) for r in in_refs])
        grads = vjp(tuple(pairs(r) for r in ct_refs))
        for ref, val in zip(out_refs, grads):
            ref[0] = val.reshape(pb * PAIR, HEAD_DIM)

    return pl.pallas_call(
        body, name="gdn_prep_bwd", grid=(H, T // (PAIR * pb)),
        in_specs=[hs] * 12, out_specs=[hs] * 5, out_shape=[hshape] * 5,
        compiler_params=_cparams(("parallel", "parallel")),
    )(q, k, v, gb, bb, pinv, du, dw, dqg, dkd, dat, deg)


def _gdn_chain(qg, kd, u, w, attn, eg):
    H, T, _ = qg.shape
    N = T // PAIR
    hs = pl.BlockSpec((H, PAIR, HEAD_DIM), lambda n: (0, n, 0))
    ss = pl.BlockSpec((1, H, HEAD_DIM, HEAD_DIM), lambda n: (n, 0, 0, 0))

    def body(qg_ref, kd_ref, u_ref, w_ref, at_ref, eg_ref, o_ref, sall_ref, s_ref):
        @pl.when(pl.program_id(0) == 0)
        def _():
            s_ref[...] = jnp.zeros_like(s_ref)

        S = s_ref[...]
        sall_ref[0] = S
        o, S2 = _gdn_chain_fn(S, qg_ref[...], kd_ref[...], u_ref[...], w_ref[...], at_ref[...], eg_ref[...])
        o_ref[...] = o
        s_ref[...] = S2

    return pl.pallas_call(
        body, name="gdn_chain", grid=(N,),
        in_specs=[hs] * 6, out_specs=[hs, ss],
        out_shape=[jax.ShapeDtypeStruct((H, T, HEAD_DIM), F32),
                   jax.ShapeDtypeStruct((N, H, HEAD_DIM, HEAD_DIM), F32)],
        scratch_shapes=[pltpu.VMEM((H, HEAD_DIM, HEAD_DIM), F32)],
        compiler_params=_cparams(("arbitrary",)),
    )(qg, kd, u, w, attn, eg)


def _gdn_chain_bwd(qg, kd, u, w, attn, eg, sall, do):
    H, T, _ = qg.shape
    N = T // PAIR
    hs = pl.BlockSpec((H, PAIR, HEAD_DIM), lambda n: (0, N - 1 - n, 0))
    ss = pl.BlockSpec((1, H, HEAD_DIM, HEAD_DIM), lambda n: (N - 1 - n, 0, 0, 0))
    hshape = jax.ShapeDtypeStruct((H, T, HEAD_DIM), F32)

    def body(qg_ref, kd_ref, u_ref, w_ref, at_ref, eg_ref, sall_ref, do_ref, *rest):
        out_refs, ds_ref = rest[:6], rest[6]

        @pl.when(pl.program_id(0) == 0)
        def _():
            ds_ref[...] = jnp.zeros_like(ds_ref)

        _, vjp = jax.vjp(_gdn_chain_fn, sall_ref[0], qg_ref[...], kd_ref[...], u_ref[...], w_ref[...],
                         at_ref[...], eg_ref[...])
        grads = vjp((do_ref[...], ds_ref[...]))
        ds_ref[...] = grads[0]
        for ref, val in zip(out_refs, grads[1:]):
            ref[...] = val

    return pl.pallas_call(
        body, name="gdn_chain_bwd", grid=(N,),
        in_specs=[hs] * 6 + [ss, hs], out_specs=[hs] * 6, out_shape=[hshape] * 6,
        scratch_shapes=[pltpu.VMEM((H, HEAD_DIM, HEAD_DIM), F32)],
        compiler_params=_cparams(("arbitrary",)),
    )(qg, kd, u, w, attn, eg, sall, do)


def _post_fn(ogs, za, hw):
    outs = []
    for h, o in enumerate(ogs):
        r = lax.rsqrt(jnp.mean(o * o, axis=-1, keepdims=True) + EPS)
        outs.append(o * r * hw * _silu(za[:, h * HEAD_DIM:(h + 1) * HEAD_DIM]))
    return jnp.concatenate(outs, axis=1)


def _gdn_post(og, proj_m, hw):
    H, T, _ = og.shape
    A = H * HEAD_DIM
    tm = _pick(T, (512, 256, 128))

    def body(og_ref, za_ref, hw_ref, o_ref):
        o_ref[...] = _post_fn(tuple(og_ref[h] for h in range(H)), za_ref[...], hw_ref[...]).astype(BF16)

    return pl.pallas_call(
        body, name="gdn_post", grid=(T // tm,),
        in_specs=[pl.BlockSpec((H, tm, HEAD_DIM), lambda i: (0, i, 0)),
                  pl.BlockSpec((tm, A), lambda i: (i, ZA_BLOCK)),
                  pl.BlockSpec((1, HEAD_DIM), lambda i: (0, 0))],
        out_specs=pl.BlockSpec((tm, A), lambda i: (i, 0)),
        out_shape=jax.ShapeDtypeStruct((T, A), BF16),
        compiler_params=_cparams(("parallel",)),
    )(og, proj_m, hw)


def _gdn_post_bwd(og, proj_m, hw, d_o, dproj):
    H, T, _ = og.shape
    A = H * HEAD_DIM
    tm = _pick(T, (256, 128))

    def body(og_ref, za_ref, hw_ref, do_ref, _, dog_ref, dza_ref, dhw_ref):
        _, vjp = jax.vjp(_post_fn, tuple(og_ref[h] for h in range(H)), za_ref[...], hw_ref[...])
        dog, dza, dhw = vjp(do_ref[...])
        for h in range(H):
            dog_ref[h] = dog[h]
        dza_ref[...] = dza.astype(BF16)

        @pl.when(pl.program_id(0) == 0)
        def _():
            dhw_ref[...] = dhw

        @pl.when(pl.program_id(0) > 0)
        def _():
            dhw_ref[...] += dhw

    return pl.pallas_call(
        body, name="gdn_post_bwd", grid=(T // tm,),
        in_specs=[pl.BlockSpec((H, tm, HEAD_DIM), lambda i: (0, i, 0)),
                  pl.BlockSpec((tm, A), lambda i: (i, ZA_BLOCK)),
                  pl.BlockSpec((1, HEAD_DIM), lambda i: (0, 0)),
                  pl.BlockSpec((tm, A), lambda i: (i, 0)), ANY],
        out_specs=[pl.BlockSpec((H, tm, HEAD_DIM), lambda i: (0, i, 0)),
                   pl.BlockSpec((tm, A), lambda i: (i, ZA_BLOCK)),
                   pl.BlockSpec((1, HEAD_DIM), lambda i: (0, 0))],
        out_shape=[jax.ShapeDtypeStruct((H, T, HEAD_DIM), F32), jax.ShapeDtypeStruct(dproj.shape, dproj.dtype),
                   jax.ShapeDtypeStruct((1, HEAD_DIM), F32)],
        input_output_aliases={4: 1},
        compiler_params=_cparams(("arbitrary",)),
    )(og, proj_m, hw, d_o, dproj)


def _sgu_fn(ub, vb, zb, lw, lb, W, bbc):
    G = len(W)
    tm = ub.shape[0]
    mu = jnp.mean(vb, axis=-1, keepdims=True)
    xc = vb - mu
    var = jnp.mean(xc * xc, axis=-1, keepdims=True)
    vn = xc * lax.rsqrt(var + EPS) * lw + lb
    mask = _iota((CHUNK_B, CHUNK_B), 0) >= _iota((CHUNK_B, CHUNK_B), 1)
    cols = []
    for g in range(G):
        wm = jnp.where(mask, W[g], 0.0).astype(BF16)
        rows = []
        for c in range(tm // CHUNK_B):
            blk = vn[c * CHUNK_B:(c + 1) * CHUNK_B, g * HEAD_DIM:(g + 1) * HEAD_DIM].astype(BF16)
            rows.append(_dot(wm, blk) + bbc[g])
        cols.append(jnp.concatenate(rows, axis=0) if len(rows) > 1 else rows[0])
    s = jnp.concatenate(cols, axis=1)
    return ub * s * _silu(zb)


ZA_BLOCK = 6


def _sgu_cols(A, B):
    assert A == B
    return 3, 4, 5


def _sgu_fwd(proj_m, lw, lb, W, bbc, A):
    T = proj_m.shape[0]
    G = W.shape[0]
    B = G * HEAD_DIM
    tm = _pick(T, (256, 128))
    cu, cv, cz = _sgu_cols(A, B)

    def body(u_ref, v_ref, z_ref, lw_ref, lb_ref, w_ref, b_ref, o_ref):
        o_ref[...] = _sgu_fn(u_ref[...], v_ref[...], z_ref[...], lw_ref[...], lb_ref[...],
                             tuple(w_ref[g] for g in range(G)), tuple(b_ref[g] for g in range(G))).astype(BF16)

    row = pl.BlockSpec((1, B), lambda i: (0, 0))
    cube = pl.BlockSpec((G, CHUNK_B, CHUNK_B), lambda i: (0, 0, 0))
    return pl.pallas_call(
        body, name="sgu_fwd", grid=(T // tm,),
        in_specs=[pl.BlockSpec((tm, B), lambda i: (i, cu)), pl.BlockSpec((tm, B), lambda i: (i, cv)),
                  pl.BlockSpec((tm, B), lambda i: (i, cz)), row, row, cube, cube],
        out_specs=pl.BlockSpec((tm, B), lambda i: (i, 0)),
        out_shape=jax.ShapeDtypeStruct((T, B), BF16),
        compiler_params=_cparams(("parallel",)),
    )(proj_m, proj_m, proj_m, lw, lb, W, bbc)


def _sgu_bwd(proj_m, lw, lb, W, bbc, d_o, A, dproj):
    T = proj_m.shape[0]
    G = W.shape[0]
    B = G * HEAD_DIM
    tm = _pick(T, (256, 128))
    nt = T // tm
    cu, cv, cz = _sgu_cols(A, B)

    def body(u_ref, v_ref, z_ref, lw_ref, lb_ref, w_ref, b_ref, do_ref, _,
             dp_ref, dlw_ref, dlb_ref, dw_ref, db_ref, dbb_ref):
        _, vjp = jax.vjp(_sgu_fn, u_ref[...], v_ref[...], z_ref[...], lw_ref[...], lb_ref[...],
                         tuple(w_ref[g] for g in range(G)), tuple(b_ref[g] for g in range(G)))
        du, dv, dz, dlw, dlb, dW, dbb = vjp(do_ref[...])
        dW, dbb = jnp.stack(dW, axis=0), jnp.stack(dbb, axis=0)
        dp_ref[:, 0:B] = du.astype(BF16)
        dp_ref[:, B:2 * B] = dv.astype(BF16)
        dp_ref[:, 2 * B:3 * B] = dz.astype(BF16)
        i = pl.program_id(0)

        @pl.when(i == 0)
        def _():
            dlw_ref[...] = dlw
            dlb_ref[...] = dlb
            dw_ref[...] = dW
            dbb_ref[...] = dbb

        @pl.when(i > 0)
        def _():
            dlw_ref[...] += dlw
            dlb_ref[...] += dlb
            dw_ref[...] += dW
            dbb_ref[...] += dbb

        @pl.when(i == nt - 1)
        def _():
            db_ref[...] = jnp.sum(dbb_ref[...], axis=-1, keepdims=True)

    row = pl.BlockSpec((1, B), lambda i: (0, 0))
    cube = pl.BlockSpec((G, CHUNK_B, CHUNK_B), lambda i: (0, 0, 0))
    return pl.pallas_call(
        body, name="sgu_bwd", grid=(nt,),
        in_specs=[pl.BlockSpec((tm, B), lambda i: (i, cu)), pl.BlockSpec((tm, B), lambda i: (i, cv)),
                  pl.BlockSpec((tm, B), lambda i: (i, cz)), row, row, cube, cube,
                  pl.BlockSpec((tm, B), lambda i: (i, A // B)), ANY],
        out_specs=[pl.BlockSpec((tm, 3 * B), lambda i: (i, 1)), row, row, cube,
                   pl.BlockSpec((G, CHUNK_B, 1), lambda i: (0, 0, 0))],
        out_shape=[jax.ShapeDtypeStruct(dproj.shape, dproj.dtype), jax.ShapeDtypeStruct((1, B), F32),
                   jax.ShapeDtypeStruct((1, B), F32), jax.ShapeDtypeStruct((G, CHUNK_B, CHUNK_B), F32),
                   jax.ShapeDtypeStruct((G, CHUNK_B, 1), F32)],
        input_output_aliases={8: 0},
        scratch_shapes=[pltpu.VMEM((G, CHUNK_B, CHUNK_B), F32)],
        compiler_params=_cparams(("arbitrary",)),
    )(proj_m, proj_m, proj_m, lw, lb, W, bbc, d_o, dproj)


def _head_fn(mix, x, fw, tgt):
    h = x + mix
    y = _rms_fn(h, fw)
    e = y - tgt
    return 0.5 * jnp.sum(jnp.mean(e * e, axis=-1, keepdims=True), axis=0, keepdims=True)


def _out_proj_loss(oa, ob, wout, x, tgt, fw):
    T, A = oa.shape
    B = ob.shape[1]
    D = x.shape[1]
    tm = _pick(T, (256, 128))

    def body(oa_ref, ob_ref, w_ref, x_ref, t_ref, fw_ref, dh_ref, dhb_ref, loss_ref, dfw_ref):
        mix = _dot(oa_ref[...], w_ref[0:A, :]) + _dot(ob_ref[...], w_ref[A:A + B, :])
        xv, tv = x_ref[...], t_ref[...]
        loss, vjp = jax.vjp(lambda m, f: _head_fn(m, xv, f, tv), mix, fw_ref[...])
        dh, dfw = vjp(jnp.ones((1, 1), F32))
        dh_ref[...] = dh
        dhb_ref[...] = dh.astype(BF16)
        lrow = jnp.broadcast_to(loss, (1, LANES))

        @pl.when(pl.program_id(0) == 0)
        def _():
            loss_ref[...] = lrow
            dfw_ref[...] = dfw

        @pl.when(pl.program_id(0) > 0)
        def _():
            loss_ref[...] += lrow
            dfw_ref[...] += dfw

    tile = pl.BlockSpec((tm, D), lambda i: (i, 0))
    return pl.pallas_call(
        body, name="out_proj_loss", grid=(T // tm,),
        in_specs=[pl.BlockSpec((tm, A), lambda i: (i, 0)), pl.BlockSpec((tm, B), lambda i: (i, 0)),
                  pl.BlockSpec((A + B, D), lambda i: (0, 0)), tile, tile,
                  pl.BlockSpec((1, D), lambda i: (0, 0))],
        out_specs=[tile, tile, pl.BlockSpec((1, LANES), lambda i: (0, 0)),
                   pl.BlockSpec((1, D), lambda i: (0, 0))],
        out_shape=[jax.ShapeDtypeStruct((T, D), F32), jax.ShapeDtypeStruct((T, D), BF16),
                   jax.ShapeDtypeStruct((1, LANES), F32), jax.ShapeDtypeStruct((1, D), F32)],
        compiler_params=_cparams(("arbitrary",)),
    )(oa, ob, wout, x, tgt, fw)


def _adamw(w, g, m, v, name):
    R, Cn = w.shape
    tr = R if R * Cn <= 512 * 1024 else _pick(R, (256, 128, 64, 32, 16, 8))

    def body(w_ref, g_ref, m_ref, v_ref, d_ref, mo_ref, vo_ref):
        g = g_ref[...]
        m = ADAM_B1 * m_ref[...] + (1.0 - ADAM_B1) * g
        v = ADAM_B2 * v_ref[...] + (1.0 - ADAM_B2) * jnp.square(g)
        m_hat = m / (1.0 - ADAM_B1 ** ADAM_STEP)
        v_hat = v / (1.0 - ADAM_B2 ** ADAM_STEP)
        d_ref[...] = -ADAM_LR * (m_hat / (jnp.sqrt(v_hat) + ADAM_EPS) + ADAM_WD * w_ref[...])
        mo_ref[...] = m
        vo_ref[...] = v

    tile = pl.BlockSpec((tr, Cn), lambda i: (i, 0))
    shape = jax.ShapeDtypeStruct((R, Cn), F32)
    return pl.pallas_call(
        body, name=name, grid=(R // tr,), in_specs=[tile] * 4, out_specs=[tile] * 3,
        out_shape=[shape] * 3, compiler_params=_cparams(("parallel",)),
    )(w, g, m, v)


def _place():
    x, y, c = lax.axis_index("x"), lax.axis_index("y"), lax.axis_index("c")
    others = [(1 - x, y), (x, 1 - y), (1 - x, 1 - y)]
    return x, y, c, others


def _chip_index(px, py):
    return 2 * px + py


ANY = pl.BlockSpec(memory_space=pl.ANY)


def _gather_weights(win_b, wout_b, conv_b):
    Din, Cb = win_b.shape
    Rb, D = wout_b.shape
    hi, ho = Din // 2, Rb // 2

    def body(win_ref, wout_ref, conv_ref, gin_ref, gout_ref, gconv_ref, send_sems, recv_sems):
        x, y, c, others = _place()
        me = _chip_index(x, y)
        sibling = (x, y, 1 - c)

        def half(ref, chip, core, n):
            return ref.at[chip, pl.ds(core * n, n), :]

        def copy(sem, src, dst, to):
            return pltpu.make_async_remote_copy(src_ref=src, dst_ref=dst, send_sem=send_sems.at[sem],
                                                recv_sem=recv_sems.at[sem], device_id=to, device_id_type=MESH_ID)

        sends = []
        for j, chip in enumerate(others):
            to = (*chip, c)
            sends.append(copy(3 * j, win_ref.at[pl.ds(c * hi, hi), :], half(gin_ref, me, c, hi), to))
            sends.append(copy(3 * j + 1, wout_ref.at[pl.ds(c * ho, ho), :], half(gout_ref, me, c, ho), to))
            sends.append(copy(3 * j + 2, conv_ref, gconv_ref.at[me], to))
        for cp in sends:
            cp.start()
        passed = []
        for j, chip in enumerate(others):
            k = _chip_index(*chip)
            copy(3 * j, half(gin_ref, k, c, hi), half(gin_ref, k, c, hi), (*chip, c)).wait_recv()
            fw = copy(9 + 2 * j, half(gin_ref, k, c, hi), half(gin_ref, k, c, hi), sibling)
            fw.start()
            copy(3 * j + 1, half(gout_ref, k, c, ho), half(gout_ref, k, c, ho), (*chip, c)).wait_recv()
            fo = copy(10 + 2 * j, half(gout_ref, k, c, ho), half(gout_ref, k, c, ho), sibling)
            fo.start()
            copy(3 * j + 2, gconv_ref.at[k], gconv_ref.at[k], (*chip, c)).wait_recv()
            passed += [fw, fo]
        for j, chip in enumerate(others):
            k = _chip_index(*chip)
            copy(9 + 2 * j, half(gin_ref, k, 1 - c, hi), half(gin_ref, k, 1 - c, hi), sibling).wait_recv()
            copy(10 + 2 * j, half(gout_ref, k, 1 - c, ho), half(gout_ref, k, 1 - c, ho), sibling).wait_recv()
        for cp in sends + passed:
            cp.wait_send()

    gin, gout, gconv = pl.pallas_call(
        body, name="gather_weights",
        in_specs=[ANY, ANY, ANY], out_specs=[ANY, ANY, ANY],
        out_shape=[jax.ShapeDtypeStruct((N_CHIPS, Din, Cb), win_b.dtype),
                   jax.ShapeDtypeStruct((N_CHIPS, Rb, D), wout_b.dtype),
                   jax.ShapeDtypeStruct((N_CHIPS,) + conv_b.shape, conv_b.dtype)],
        scratch_shapes=[pltpu.SemaphoreType.DMA((15,)), pltpu.SemaphoreType.DMA((15,))],
        compiler_params=pltpu.CompilerParams(has_side_effects=True),
    )(win_b, wout_b, conv_b)
    me = _chip_index(lax.axis_index("x"), lax.axis_index("y"))
    put = lambda g, own: lax.dynamic_update_index_in_dim(g, own, me, 0)
    return put(gin, win_b), put(gout, wout_b), put(gconv, conv_b)


def _allreduce_small(buf):
    R, L = buf.shape

    def body(in_ref, out_ref, sib_ref, pair_ref, chips_ref, send_sems, recv_sems):
        x, y, c, others = _place()
        me = _chip_index(x, y)
        sibling = (x, y, 1 - c)
        cp = pltpu.make_async_remote_copy(src_ref=in_ref, dst_ref=sib_ref, send_sem=send_sems.at[0],
                                          recv_sem=recv_sems.at[0], device_id=sibling, device_id_type=MESH_ID)
        cp.start()
        cp.wait()
        pair_ref[...] = in_ref[...] + sib_ref[...]
        sends = []
        for j, chip in enumerate(others):
            s = pltpu.make_async_remote_copy(src_ref=pair_ref, dst_ref=chips_ref.at[me],
                                             send_sem=send_sems.at[1 + j], recv_sem=recv_sems.at[1 + j],
                                             device_id=(*chip, c), device_id_type=MESH_ID)
            s.start()
            sends.append(s)
        chips_ref[me] = pair_ref[...]
        for j, chip in enumerate(others):
            k = _chip_index(*chip)
            pltpu.make_async_remote_copy(src_ref=pair_ref, dst_ref=chips_ref.at[k], send_sem=send_sems.at[1 + j],
                                         recv_sem=recv_sems.at[1 + j], device_id=(*chip, c),
                                         device_id_type=MESH_ID).wait_recv()
        for s in sends:
            s.wait_send()
        out_ref[...] = ((chips_ref[0] + chips_ref[1]) + chips_ref[2]) + chips_ref[3]

    vm = pl.BlockSpec(memory_space=pltpu.VMEM)
    return pl.pallas_call(
        body, name="allreduce_small", in_specs=[vm], out_specs=vm,
        out_shape=jax.ShapeDtypeStruct((R, L), F32),
        scratch_shapes=[pltpu.VMEM((R, L), F32), pltpu.VMEM((R, L), F32), pltpu.VMEM((N_CHIPS, R, L), F32),
                        pltpu.SemaphoreType.DMA((4,)), pltpu.SemaphoreType.DMA((4,))],
        compiler_params=pltpu.CompilerParams(vmem_limit_bytes=VMEM_LIMIT),
    )(buf)


def _pair_exchange(gw, go):
    _, Din, Cb = gw.shape
    _, Rb, D = go.shape
    hi, ho = Din // 2, Rb // 2

    def body(gw_ref, go_ref, lw_ref, lo_ref, send_sems, recv_sems):
        x, y, c, _ = _place()
        sibling = (x, y, 1 - c)
        a = pltpu.make_async_remote_copy(src_ref=gw_ref.at[:, pl.ds((1 - c) * hi, hi), :], dst_ref=lw_ref,
                                         send_sem=send_sems.at[0], recv_sem=recv_sems.at[0],
                                         device_id=sibling, device_id_type=MESH_ID)
        b = pltpu.make_async_remote_copy(src_ref=go_ref.at[:, pl.ds((1 - c) * ho, ho), :], dst_ref=lo_ref,
                                         send_sem=send_sems.at[1], recv_sem=recv_sems.at[1],
                                         device_id=sibling, device_id_type=MESH_ID)
        a.start()
        b.start()
        a.wait()
        b.wait()

    return pl.pallas_call(
        body, name="pair_exchange", in_specs=[ANY, ANY], out_specs=[ANY, ANY],
        out_shape=[jax.ShapeDtypeStruct((N_CHIPS, hi, Cb), gw.dtype),
                   jax.ShapeDtypeStruct((N_CHIPS, ho, D), go.dtype)],
        scratch_shapes=[pltpu.SemaphoreType.DMA((2,)), pltpu.SemaphoreType.DMA((2,))],
        compiler_params=pltpu.CompilerParams(has_side_effects=True),
    )(gw, go)


def _pair_sum(g, land, c_arr, name):
    nb, R, Cn = g.shape
    hr = R // 2
    tr = _pick(hr, (256, 128, 64, 32, 16))
    nt = hr // tr

    def body(c_ref, g_ref, l_ref, o_ref):
        o_ref[...] = (g_ref[...] + l_ref[...]).astype(BF16)

    return pl.pallas_call(
        body, name=name,
        grid_spec=pltpu.PrefetchScalarGridSpec(
            num_scalar_prefetch=1, grid=(nb, nt),
            in_specs=[pl.BlockSpec((1, tr, Cn), lambda b, i, c_ref: (b, c_ref[0] * nt + i, 0)),
                      pl.BlockSpec((1, tr, Cn), lambda b, i, c_ref: (b, i, 0))],
            out_specs=pl.BlockSpec((1, tr, Cn), lambda b, i, c_ref: (b, i, 0))),
        out_shape=jax.ShapeDtypeStruct((nb, hr, Cn), BF16),
        compiler_params=_cparams(("parallel", "parallel")),
    )(c_arr, g, land)


def _chip_exchange(pw, po):
    nb, hi, Cb = pw.shape
    _, ho, D = po.shape

    def body(pw_ref, po_ref, qw_ref, qo_ref, send_sems, recv_sems):
        x, y, c, others = _place()
        me = _chip_index(x, y)
        sends = []
        for j, chip in enumerate(others):
            k = _chip_index(*chip)
            for n, (src, dst) in enumerate(((pw_ref, qw_ref), (po_ref, qo_ref))):
                s = pltpu.make_async_remote_copy(src_ref=src.at[k], dst_ref=dst.at[me],
                                                 send_sem=send_sems.at[2 * j + n], recv_sem=recv_sems.at[2 * j + n],
                                                 device_id=(*chip, c), device_id_type=MESH_ID)
                s.start()
                sends.append(s)
        for j, chip in enumerate(others):
            k = _chip_index(*chip)
            for n, (src, dst) in enumerate(((pw_ref, qw_ref), (po_ref, qo_ref))):
                pltpu.make_async_remote_copy(src_ref=src.at[k], dst_ref=dst.at[k],
                                             send_sem=send_sems.at[2 * j + n], recv_sem=recv_sems.at[2 * j + n],
                                             device_id=(*chip, c), device_id_type=MESH_ID).wait_recv()
        for s in sends:
            s.wait_send()

    qw, qo = pl.pallas_call(
        body, name="chip_exchange", in_specs=[ANY, ANY], out_specs=[ANY, ANY],
        out_shape=[jax.ShapeDtypeStruct(pw.shape, pw.dtype), jax.ShapeDtypeStruct(po.shape, po.dtype)],
        scratch_shapes=[pltpu.SemaphoreType.DMA((6,)), pltpu.SemaphoreType.DMA((6,))],
        compiler_params=pltpu.CompilerParams(has_side_effects=True),
    )(pw, po)
    me = _chip_index(lax.axis_index("x"), lax.axis_index("y"))
    put = lambda q, p: lax.dynamic_update_index_in_dim(q, lax.dynamic_index_in_dim(p, me, 0, keepdims=False), me, 0)
    return put(qw, pw), put(qo, po)


def _chip_sum(q, name):
    nb, hr, Cn = q.shape
    tr = _pick(hr, (256, 128, 64, 32, 16))

    def body(q_ref, o_ref):
        f = lambda k: q_ref[k].astype(F32)
        o_ref[...] = ((f(0) + f(1)) + f(2)) + f(3)

    return pl.pallas_call(
        body, name=name, grid=(hr // tr,),
        in_specs=[pl.BlockSpec((nb, tr, Cn), lambda i: (0, i, 0))],
        out_specs=pl.BlockSpec((tr, Cn), lambda i: (i, 0)),
        out_shape=jax.ShapeDtypeStruct((hr, Cn), F32),
        compiler_params=_cparams(("parallel",)),
    )(q)


def _sibling_concat(rw, ro):
    hi, Cb = rw.shape
    ho, D = ro.shape

    def body(rw_ref, ro_ref, tw_ref, to_ref, send_sems, recv_sems):
        x, y, c, _ = _place()
        sibling = (x, y, 1 - c)
        a = pltpu.make_async_remote_copy(src_ref=rw_ref, dst_ref=tw_ref, send_sem=send_sems.at[0],
                                         recv_sem=recv_sems.at[0], device_id=sibling, device_id_type=MESH_ID)
        b = pltpu.make_async_remote_copy(src_ref=ro_ref, dst_ref=to_ref, send_sem=send_sems.at[1],
                                         recv_sem=recv_sems.at[1], device_id=sibling, device_id_type=MESH_ID)
        a.start()
        b.start()
        a.wait()
        b.wait()

    tw, to = pl.pallas_call(
        body, name="sibling_concat", in_specs=[ANY, ANY], out_specs=[ANY, ANY],
        out_shape=[jax.ShapeDtypeStruct((hi, Cb), F32), jax.ShapeDtypeStruct((ho, D), F32)],
        scratch_shapes=[pltpu.SemaphoreType.DMA((2,)), pltpu.SemaphoreType.DMA((2,))],
        compiler_params=pltpu.CompilerParams(has_side_effects=True),
    )(rw, ro)
    c = lax.axis_index("c")
    join = lambda mine, theirs: lax.dynamic_update_slice_in_dim(
        jnp.concatenate([mine, mine], axis=0), theirs, (1 - c) * mine.shape[0], axis=0)
    return join(rw, tw), join(ro, to)


class _Layout:
    def __init__(self, H, G, nb, Cb):
        A, B = H * HEAD_DIM, G * HEAD_DIM
        self.n_main = 4 * A + 3 * B
        self.k = -(-(self.n_main + LANES) // WIN_BLOCK) * WIN_BLOCK
        cuts = [0, 3 * A, 4 * A, 4 * A + 2 * H, nb * Cb]
        starts = [0, 3 * A + 3 * B, self.n_main, 3 * A]
        self.pieces = []
        self.windows, self.runs = [], []
        for n in range(nb):
            segs = []
            for s in range(4):
                lo, hi = max(cuts[s], n * Cb), min(cuts[s + 1], (n + 1) * Cb)
                if lo < hi:
                    segs.append((starts[s] + lo - cuts[s], lo - n * Cb, hi - lo))
            self.pieces += [(own, n, col, ln) for own, col, ln in segs]
            blocks = sorted({b for own, _, ln in segs for b in range(own // WIN_BLOCK, (own + ln - 1) // WIN_BLOCK + 1)})
            self.windows.append(blocks)
            self.runs.append([(blocks.index(own // WIN_BLOCK) * WIN_BLOCK + own % WIN_BLOCK, ln)
                              for own, _, ln in segs])
        self.wb = max(len(b) for b in self.windows)
        self.table = [b + [b[-1]] * (self.wb - len(b)) for b in self.windows]
        self.pieces.sort()

    def to_own_order(self, g_in):
        D = g_in.shape[1]
        cols, at = [], 0
        for own, n, col, ln in self.pieces:
            if own > at:
                cols.append(jnp.zeros((D, own - at), g_in.dtype))
            cols.append(g_in[n, :, col:col + ln])
            at = own + ln
        if at < self.k:
            cols.append(jnp.zeros((D, self.k - at), g_in.dtype))
        return jnp.concatenate(cols, axis=1)

    def from_window(self, win, chip, Cb):
        pick = lambda runs: (lambda w: jnp.concatenate([w[:, c:c + ln] for c, ln in runs], axis=1))
        return lax.switch(chip, [pick(r) for r in self.runs], win)


def _local_grads(x, tgt, norm_w, g_in, conv_w, a_log, dt_bias, head_norm_w, sgu_ln_w, sgu_ln_b,
                 w_spatial, b_spatial, wout, final_norm_w):
    T, D = x.shape
    H = a_log.shape[1]
    A = H * HEAD_DIM
    G = w_spatial.shape[0]
    B = G * HEAD_DIM
    nb, _, Cb = g_in.shape
    lay = _Layout(H, G, nb, Cb)
    w_own = lay.to_own_order(g_in)
    alog_row = jnp.pad(a_log, ((0, 0), (H, LANES - 2 * H)))
    dtb_row = jnp.pad(dt_bias, ((0, 0), (H, LANES - 2 * H)))
    bbc = jnp.broadcast_to(b_spatial[:, :, None], (G, CHUNK_B, CHUNK_B))

    xn, xn_t = _rms_in(x, norm_w)
    proj_m = _mm_nn(xn, w_own, F32, "in_proj", cols=(0, lay.n_main))
    proj_ba = _mm_nn(xn, w_own, F32, "in_proj_ba", cols=(lay.n_main, LANES))
    q, k, v, gb, bb = _gdn_pre(proj_m, proj_ba, conv_w, alog_row, dtb_row, H)
    u, w, qg, kd, attn, eg, pinv = _gdn_prep(q, k, v, gb, bb)
    og, sall = _gdn_chain(qg, kd, u, w, attn, eg)
    oa = _gdn_post(og, proj_m, head_norm_w)
    ob = _sgu_fwd(proj_m, sgu_ln_w, sgu_ln_b, w_spatial, bbc, A)
    dh, dhb, loss_row, d_fnw = _out_proj_loss(oa, ob, wout, x, tgt, final_norm_w.reshape(1, D))

    d_o = _mm_nn(dhb, wout.T, F32, "out_proj_dx")
    d_wout = jnp.concatenate([_mm_tn(oa, dhb, "out_proj_dw_a"), _mm_tn(ob, dhb, "out_proj_dw_b")], axis=0)
    dproj = lax.empty((T, lay.k), BF16)
    dproj, d_lw, d_lb, d_ws, d_bs = _sgu_bwd(proj_m, sgu_ln_w, sgu_ln_b, w_spatial, bbc, d_o, A, dproj)
    dog, dproj, d_hw = _gdn_post_bwd(og, proj_m, head_norm_w, d_o, dproj)
    dqg, dkd, du, dw, dat, deg = _gdn_chain_bwd(qg, kd, u, w, attn, eg, sall, dog)
    dq, dk, dv, dgb, dbb = _gdn_prep_bwd(q, k, v, gb, bb, pinv, du, dw, dqg, dkd, dat, deg)
    dc, dproj, d_al, d_dt = _gdn_pre_bwd(proj_m, proj_ba, conv_w, alog_row, dtb_row, dq, dk, dv, dgb, dbb, H,
                                         dproj)
    dproj, d_conv = _conv_bwd(proj_m, dc, conv_w, H, dproj)
    dxn = _mm_nt_rhs_outer(dproj, w_own, F32, "in_proj_dx")
    grad_x, d_nw = _rms_in_bwd(x, norm_w, dxn, dh)
    table = jnp.array([b for row in lay.table for b in row], jnp.int32)
    d_win = _mm_windows(xn_t, dproj, table, nb, "in_proj_dw")
    grads = dict(norm_w=d_nw, w_in=d_win, conv_w=d_conv[:CONV_WIDTH], a_log=d_al[:, H:2 * H],
                 dt_bias=d_dt[:, H:2 * H], head_norm_w=d_hw, sgu_ln_w=d_lw, sgu_ln_b=d_lb, w_spatial=d_ws,
                 b_spatial=d_bs[:, :, 0], w_out=d_wout, final_norm_w=d_fnw)
    return loss_row, grad_x, grads


SMALL = ("norm_w", "conv_w", "a_log", "dt_bias", "head_norm_w", "sgu_ln_w", "sgu_ln_b", "w_spatial",
         "b_spatial", "final_norm_w")


def _pack(parts):
    rows = []
    for p in parts:
        f = p.reshape(-1)
        f = jnp.pad(f, (0, (-f.shape[0]) % (8 * LANES)))
        rows.append(f.reshape(-1, LANES))
    return jnp.concatenate(rows, axis=0)


def _unpack(buf, shapes):
    out, r = [], 0
    for s in shapes:
        n = 1
        for d in s:
            n *= d
        nr = -(-n // (8 * LANES)) * 8
        out.append(buf[r:r + nr].reshape(-1)[:n].reshape(s))
        r += nr
    return out


def kernel(x, norm_w, w_in, conv_w, a_log, dt_bias, head_norm_w, sgu_ln_w, sgu_ln_b, w_spatial, b_spatial, w_out, final_norm_w, loss_target, m_norm_w, m_w_in, m_conv_w, m_a_log, m_dt_bias, m_head_norm_w, m_sgu_ln_w, m_sgu_ln_b, m_w_spatial, m_b_spatial, m_w_out, m_final_norm_w, v_norm_w, v_w_in, v_conv_w, v_a_log, v_dt_bias, v_head_norm_w, v_sgu_ln_w, v_sgu_ln_b, v_w_spatial, v_b_spatial, v_w_out, v_final_norm_w):
    T, D = x.shape[1], x.shape[2]
    weights = dict(norm_w=norm_w, w_in=w_in, conv_w=conv_w, a_log=a_log, dt_bias=dt_bias, head_norm_w=head_norm_w,
                   sgu_ln_w=sgu_ln_w, sgu_ln_b=sgu_ln_b, w_spatial=w_spatial, b_spatial=b_spatial, w_out=w_out,
                   final_norm_w=final_norm_w)
    mom_m = dict(norm_w=m_norm_w, w_in=m_w_in, conv_w=m_conv_w, a_log=m_a_log, dt_bias=m_dt_bias,
                 head_norm_w=m_head_norm_w, sgu_ln_w=m_sgu_ln_w, sgu_ln_b=m_sgu_ln_b, w_spatial=m_w_spatial,
                 b_spatial=m_b_spatial, w_out=m_w_out, final_norm_w=m_final_norm_w)
    mom_v = dict(norm_w=v_norm_w, w_in=v_w_in, conv_w=v_conv_w, a_log=v_a_log, dt_bias=v_dt_bias,
                 head_norm_w=v_head_norm_w, sgu_ln_w=v_sgu_ln_w, sgu_ln_b=v_sgu_ln_b, w_spatial=v_w_spatial,
                 b_spatial=v_b_spatial, w_out=v_w_out, final_norm_w=v_final_norm_w)
    me = _chip_index(lax.axis_index("x"), lax.axis_index("y"))
    c_arr = lax.axis_index("c").astype(jnp.int32).reshape(1)
    Din, Cb = w_in.shape[1], w_in.shape[2]
    Rb = w_out.shape[1]
    cconv = conv_w.shape[2]

    g_in, g_out, g_conv = _gather_weights(w_in[0].astype(BF16), w_out[0].astype(BF16), conv_w[0])
    wout_full = g_out.reshape(N_CHIPS * Rb, D)
    conv_full = g_conv.transpose(1, 0, 2).reshape(CONV_WIDTH, N_CHIPS * cconv)

    loss_row, grad_x, g = _local_grads(
        x[0], loss_target[0], norm_w, g_in, conv_full, a_log, dt_bias, head_norm_w, sgu_ln_w, sgu_ln_b,
        w_spatial[0], b_spatial[0], wout_full, final_norm_w)

    gw = g["w_in"]
    go = g["w_out"].reshape(N_CHIPS, Rb, D)
    lw, lo = _pair_exchange(gw, go)
    qw, qo = _chip_exchange(_pair_sum(gw, lw, c_arr, "pair_sum_w_in"), _pair_sum(go, lo, c_arr, "pair_sum_w_out"))
    gsum_in, gsum_out = _sibling_concat(_chip_sum(qw, "chip_sum_w_in"), _chip_sum(qo, "chip_sum_w_out"))
    gsum_in = _Layout(a_log.shape[1], w_spatial.shape[1], N_CHIPS, Cb).from_window(gsum_in, me, Cb)
    small_shapes = [tuple(g[n].shape) for n in SMALL]
    small = _unpack(_allreduce_small(_pack([g[n] for n in SMALL])), small_shapes)
    gsmall = dict(zip(SMALL, small))
    gsmall["conv_w"] = lax.dynamic_slice_in_dim(gsmall["conv_w"], me * cconv, cconv, axis=1)

    grads, deltas, new_m, new_v = {}, {}, {}, {}
    for n, gs in (("w_in", gsum_in), ("w_out", gsum_out)):
        d, m2, v2 = _adamw(weights[n][0], gs, mom_m[n][0], mom_v[n][0], "adamw_" + n)
        grads[n], deltas[n], new_m[n], new_v[n] = gs[None], d[None], m2[None], v2[None]
    shapes = [tuple(weights[n].shape) for n in SMALL]
    ds, ms, vs = _adamw(_pack([weights[n] for n in SMALL]), _pack([gsmall[n] for n in SMALL]),
                        _pack([mom_m[n] for n in SMALL]), _pack([mom_v[n] for n in SMALL]), "adamw_small")
    for n, gq, d, m2, v2 in zip(SMALL, [gsmall[n] for n in SMALL], _unpack(ds, shapes), _unpack(ms, shapes),
                                _unpack(vs, shapes)):
        grads[n], deltas[n], new_m[n], new_v[n] = gq.reshape(weights[n].shape), d, m2, v2

    loss = lax.psum(loss_row[0, 0], ("x", "y", "c"))
    order = ("norm_w", "w_in", "conv_w", "a_log", "dt_bias", "head_norm_w", "sgu_ln_w", "sgu_ln_b", "w_spatial",
             "b_spatial", "w_out", "final_norm_w")
    return (loss, grad_x[None], *[grads[n] for n in order], *[deltas[n] for n in order],
            *[new_m[n] for n in order], *[new_v[n] for n in order])
```

```python
import functools

import jax
import jax.numpy as jnp
from jax import lax
from jax.experimental import pallas as pl
from jax.experimental.pallas import tpu as pltpu

F32 = jnp.float32
BF16 = jnp.bfloat16
EPS = 1e-6
HEAD_DIM = 128
CHUNK_A = 64
CHUNK_B = 128
CONV_WIDTH = 4
LANES = 128
HALO = 8
N_CHIPS = 4
ADAM_LR = 0.001
ADAM_B1 = 0.9
ADAM_B2 = 0.999
ADAM_EPS = 1e-08
ADAM_WD = 0.01
ADAM_STEP = 10
VMEM_LIMIT = 56 * 1024 * 1024
MESH_ID = pl.DeviceIdType.MESH
HI = lax.Precision.HIGHEST


def _cparams(sem=None, **kw):
    return pltpu.CompilerParams(dimension_semantics=sem, vmem_limit_bytes=VMEM_LIMIT, **kw)


def _matmul(a, b, ca, cb, precision):
    nb = a.ndim - 2
    batch = tuple(range(nb))
    return lax.dot_general(a, b, (((ca + nb,), (cb + nb,)), (batch, batch)), precision=precision,
                           preferred_element_type=F32)


def _dot(a, b, hi=False, precision=None):
    return _matmul(a, b, 1, 0, HI if hi else precision)


def _dot_nt(a, b, hi=False, precision=None):
    return _matmul(a, b, 1, 1, HI if hi else precision)


def _dot_tn(a, b, hi=False, precision=None):
    return _matmul(a, b, 0, 0, HI if hi else precision)


def _iota(shape, dim):
    return lax.broadcasted_iota(jnp.int32, shape, dim)


def _sigmoid(x):
    return 0.5 * (jnp.tanh(0.5 * x) + 1.0)


def _silu(x):
    return x * _sigmoid(x)


def _softplus(x):
    z = jnp.exp(-jnp.abs(x))
    small = z * (1.0 - z * (0.5 - z * (1.0 / 3.0)))
    return jnp.maximum(x, 0.0) + jnp.where(z < 1e-3, small, jnp.log(1.0 + z))


def _pick(n, pref):
    for t in pref:
        if n % t == 0:
            return t
    return n


class _Ride:
    def __init__(self, operands, out_shape, n_sems, start, finish):
        self.operands, self.out_shape, self.n_sems = list(operands), list(out_shape), n_sems
        self.start, self.finish = start, finish


def _pallas(body, operands, *, name, grid, in_specs, out_specs, out_shape, semantics, scratch_shapes=(),
            prefetch=0, ride=None):
    single = not isinstance(out_shape, (list, tuple))
    outs = [out_shape] if single else list(out_shape)
    ospecs = [out_specs] if single else list(out_specs)
    in_specs, scratch = list(in_specs), list(scratch_shapes)
    n_in, n_out, n_sc = len(operands) - prefetch, len(outs), len(scratch)
    kernel = body
    params = _cparams(semantics)
    if ride is not None:
        n_xin, n_xout = len(ride.operands), len(ride.out_shape)

        def kernel(*refs):
            pre, refs = refs[:prefetch], refs[prefetch:]
            ins, refs = refs[:n_in], refs[n_in:]
            xins, refs = refs[:n_xin], refs[n_xin:]
            mains, refs = refs[:n_out], refs[n_out:]
            xouts, refs = refs[:n_xout], refs[n_xout:]
            sc, (send, recv) = refs[:n_sc], refs[n_sc:]
            ids = [pl.program_id(a) for a in range(len(grid))]
            first = functools.reduce(jnp.logical_and, [i == 0 for i in ids])
            last = functools.reduce(jnp.logical_and, [i == g - 1 for i, g in zip(ids, grid)])

            @pl.when(first)
            def _():
                ride.start(xins, xouts, send, recv)

            body(*pre, *ins, *mains, *sc)

            @pl.when(last)
            def _():
                ride.finish(xins, xouts, send, recv)

        operands = list(operands) + ride.operands
        in_specs += [ANY] * n_xin
        ospecs += [ANY] * n_xout
        outs += ride.out_shape
        scratch += [pltpu.SemaphoreType.DMA((ride.n_sems,)), pltpu.SemaphoreType.DMA((ride.n_sems,))]
        params = _cparams(("arbitrary",) * len(grid), has_side_effects=True)
    if prefetch:
        spec = dict(grid_spec=pltpu.PrefetchScalarGridSpec(
            num_scalar_prefetch=prefetch, grid=grid, in_specs=in_specs, out_specs=ospecs, scratch_shapes=scratch))
    else:
        spec = dict(grid=grid, in_specs=in_specs, out_specs=ospecs, scratch_shapes=scratch)
    res = pl.pallas_call(kernel, name=name, out_shape=outs, compiler_params=params, **spec)(*operands)
    main = res[0] if single else list(res[:n_out])
    return main if ride is None else (main, list(res[n_out:]))


def _mm_nn(a, b, out_dtype, name, tm=1024, tn=512, tk=None, cols=None, ride=None):
    M, K = a.shape
    c0, N = (0, b.shape[1]) if cols is None else cols
    tm = _pick(M, (tm, 512, 256, 128))
    tn = _pick(N, (tn, 512, 384, 256, 128))
    tk = K if tk is None else _pick(K, (tk,))
    nk = K // tk
    j0 = c0 // tn
    assert c0 % tn == 0

    def body(a_ref, b_ref, o_ref, *scratch):
        part = _dot(a_ref[...], b_ref[...])
        if nk == 1:
            o_ref[...] = part.astype(out_dtype)
        else:
            acc_ref, = scratch
            k = pl.program_id(2)

            @pl.when(k == 0)
            def _():
                acc_ref[...] = part

            @pl.when(k > 0)
            def _():
                acc_ref[...] += part

            @pl.when(k == nk - 1)
            def _():
                o_ref[...] = acc_ref[...].astype(out_dtype)

    return _pallas(
        body, (a, b), name=name, grid=(M // tm, N // tn, nk),
        in_specs=[pl.BlockSpec((tm, tk), lambda i, j, k: (i, k)),
                  pl.BlockSpec((tk, tn), lambda i, j, k: (k, j + j0))],
        out_specs=pl.BlockSpec((tm, tn), lambda i, j, k: (i, j)),
        out_shape=jax.ShapeDtypeStruct((M, N), out_dtype),
        scratch_shapes=[] if nk == 1 else [pltpu.VMEM((tm, tn), F32)],
        semantics=("parallel", "parallel", "arbitrary"), ride=ride)


def _mm_nt_rhs_outer(a, b, out_dtype, name, tm=256, tn=1024, ride=None):
    M, K = a.shape
    N, _ = b.shape
    tm = _pick(M, (tm, 128))
    tn = _pick(N, (tn, 512, 256, 128))

    def body(a_ref, b_ref, o_ref):
        o_ref[...] = _dot_nt(a_ref[...], b_ref[...]).astype(out_dtype)

    return _pallas(
        body, (a, b), name=name, grid=(N // tn, M // tm),
        in_specs=[pl.BlockSpec((tm, K), lambda j, i: (i, 0)),
                  pl.BlockSpec((tn, K), lambda j, i: (j, 0))],
        out_specs=pl.BlockSpec((tm, tn), lambda j, i: (i, j)),
        out_shape=jax.ShapeDtypeStruct((M, N), out_dtype),
        semantics=("parallel", "parallel"), ride=ride)


WIN_BLOCK = 256


def _mm_windows(a, b, table, nb, name, tm=1024):
    M, K = a.shape
    wb = table.shape[0] // nb
    tm = _pick(M, (tm, 512, 256, 128))

    def body(tab_ref, a_ref, b_ref, o_ref):
        o_ref[0] = _dot(a_ref[...], b_ref[...])

    return pl.pallas_call(
        body, name=name,
        grid_spec=pltpu.PrefetchScalarGridSpec(
            num_scalar_prefetch=1, grid=(nb, M // tm, wb),
            in_specs=[pl.BlockSpec((tm, K), lambda n, i, t, tab: (i, 0)),
                      pl.BlockSpec((K, WIN_BLOCK), lambda n, i, t, tab: (0, tab[n * wb + t]))],
            out_specs=pl.BlockSpec((1, tm, WIN_BLOCK), lambda n, i, t, tab: (n, i, t))),
        out_shape=jax.ShapeDtypeStruct((nb, M, wb * WIN_BLOCK), F32),
        compiler_params=_cparams(("parallel", "parallel", "arbitrary")),
    )(table, a, b)


def _mm_tn_pair(a0, a1, b, name, tm=512, tn=512, tk=1024, ride=None):
    K, M = a0.shape
    _, N = b.shape
    tm = _pick(M, (tm, 256, 128))
    tn = _pick(N, (tn, 384, 256, 128))
    tk = _pick(K, (tk, 512, 256))
    nk, ni = K // tk, M // tm

    def body(a0_ref, a1_ref, b_ref, o_ref):
        p, k = pl.program_id(0), pl.program_id(3)

        def acc(a_ref):
            part = _dot_tn(a_ref[...], b_ref[...])

            @pl.when(k == 0)
            def _():
                o_ref[...] = part

            @pl.when(k > 0)
            def _():
                o_ref[...] += part

        pl.when(p == 0)(lambda: acc(a0_ref))
        pl.when(p == 1)(lambda: acc(a1_ref))

    return _pallas(
        body, (a0, a1, b), name=name, grid=(2, ni, N // tn, nk),
        in_specs=[pl.BlockSpec((tk, tm), lambda p, i, j, k: (k * (1 - p), i * (1 - p))),
                  pl.BlockSpec((tk, tm), lambda p, i, j, k: (k * p, i * p)),
                  pl.BlockSpec((tk, tn), lambda p, i, j, k: (k, j))],
        out_specs=pl.BlockSpec((tm, tn), lambda p, i, j, k: (p * ni + i, j)),
        out_shape=jax.ShapeDtypeStruct((2 * M, N), F32),
        semantics=("parallel", "parallel", "parallel", "arbitrary"), ride=ride)


def _rms_fn(x, w):
    r = lax.rsqrt(jnp.mean(x * x, axis=-1, keepdims=True) + EPS)
    return x * r * w


def _rms_in(x, w):
    T, D = x.shape
    tm = _pick(T, (512, 256, 128))

    def body(x_ref, w_ref, o_ref, ot_ref):
        xn = _rms_fn(x_ref[...], w_ref[...])
        o_ref[...] = xn.astype(BF16)
        ot_ref[...] = xn.T.astype(BF16)

    return pl.pallas_call(
        body, name="rms_in", grid=(T // tm,),
        in_specs=[pl.BlockSpec((tm, D), lambda i: (i, 0)), pl.BlockSpec((1, D), lambda i: (0, 0))],
        out_specs=[pl.BlockSpec((tm, D), lambda i: (i, 0)), pl.BlockSpec((D, tm), lambda i: (0, i))],
        out_shape=[jax.ShapeDtypeStruct((T, D), BF16), jax.ShapeDtypeStruct((D, T), BF16)],
        compiler_params=_cparams(("parallel",)),
    )(x, w)


def _rms_in_bwd(x, w, dxn, dh):
    T, D = x.shape
    tm = _pick(T, (256, 128))

    def body(x_ref, w_ref, dxn_ref, dh_ref, gx_ref, dw_ref):
        _, vjp = jax.vjp(_rms_fn, x_ref[...], w_ref[...])
        dx, dw = vjp(dxn_ref[...])
        gx_ref[...] = dh_ref[...] + dx

        @pl.when(pl.program_id(0) == 0)
        def _():
            dw_ref[...] = dw

        @pl.when(pl.program_id(0) > 0)
        def _():
            dw_ref[...] += dw

    tile = pl.BlockSpec((tm, D), lambda i: (i, 0))
    row = pl.BlockSpec((1, D), lambda i: (0, 0))
    return pl.pallas_call(
        body, name="rms_in_bwd", grid=(T // tm,),
        in_specs=[tile, row, tile, tile], out_specs=[tile, row],
        out_shape=[jax.ShapeDtypeStruct((T, D), F32), jax.ShapeDtypeStruct((1, D), F32)],
        compiler_params=_cparams(("arbitrary",)),
    )(x, w, dxn, dh)


def _conv_fwd(xcat, w, tm):
    c = None
    for k in range(CONV_WIDTH):
        s = CONV_WIDTH - 1 - k
        xs = xcat if s == 0 else pltpu.roll(xcat, s, 0)
        term = xs[HALO:, :] * w[k:k + 1, :]
        c = term if c is None else c + term
    return c


def _lane_to_all(x, lane):
    @jax.custom_vjp
    def f(x):
        return jnp.broadcast_to(x[:, lane:lane + 1], x.shape)

    def f_fwd(x):
        return f(x), None

    def f_bwd(_, g):
        return (jnp.where(_iota(g.shape, 1) == lane, jnp.sum(g, axis=-1, keepdims=True), 0.0),)

    f.defvjp(f_fwd, f_bwd)
    return f(x)


def _gdn_pointwise(c, ba, alog, dtb, H):
    A = H * HEAD_DIM
    s = _silu(c)
    beta = _sigmoid(ba)
    g = -jnp.exp(alog) * _softplus(ba + dtb)
    qs, ks, vs, gbs, bbs = [], [], [], [], []
    for h in range(H):
        lo = h * HEAD_DIM
        q = s[:, lo:lo + HEAD_DIM]
        k = s[:, A + lo:A + lo + HEAD_DIM]
        qs.append(q * lax.rsqrt(jnp.sum(q * q, axis=-1, keepdims=True) + EPS))
        ks.append(k * lax.rsqrt(jnp.sum(k * k, axis=-1, keepdims=True) + EPS))
        vs.append(s[:, 2 * A + lo:2 * A + lo + HEAD_DIM])
        bbs.append(_lane_to_all(beta, h))
        gbs.append(_lane_to_all(g, H + h))
    st = lambda xs: jnp.stack(xs, axis=0)
    return st(qs), st(ks), st(vs), st(gbs), st(bbs)


def _halo_prev(tm):
    return lambda i: (jnp.maximum(i * (tm // HALO) - 1, 0), 0)


def _gdn_pre(proj_m, proj_ba, conv_w, alog_row, dtb_row, H):
    T = proj_m.shape[0]
    A = H * HEAD_DIM
    tm = _pick(T, (256, 128))
    hs = pl.BlockSpec((H, tm, HEAD_DIM), lambda i: (0, i, 0))
    hshape = jax.ShapeDtypeStruct((H, T, HEAD_DIM), F32)

    def body(x_ref, halo_ref, ba_ref, w_ref, al_ref, dt_ref, q_ref, k_ref, v_ref, gb_ref, bb_ref):
        halo = jnp.where(pl.program_id(0) == 0, 0.0, halo_ref[...])
        c = _conv_fwd(jnp.concatenate([halo, x_ref[...]], axis=0), w_ref[...], tm)
        q, k, v, gb, bb = _gdn_pointwise(c, ba_ref[...], al_ref[...], dt_ref[...], H)
        q_ref[...] = q
        k_ref[...] = k
        v_ref[...] = v
        gb_ref[...] = gb
        bb_ref[...] = bb

    return pl.pallas_call(
        body, name="gdn_pre", grid=(T // tm,),
        in_specs=[pl.BlockSpec((tm, 3 * A), lambda i: (i, 0)),
                  pl.BlockSpec((HALO, 3 * A), _halo_prev(tm)),
                  pl.BlockSpec((tm, LANES), lambda i: (i, 0)),
                  pl.BlockSpec((CONV_WIDTH, 3 * A), lambda i: (0, 0)),
                  pl.BlockSpec((1, LANES), lambda i: (0, 0)),
                  pl.BlockSpec((1, LANES), lambda i: (0, 0))],
        out_specs=[hs] * 5, out_shape=[hshape] * 5,
        compiler_params=_cparams(("parallel",)),
    )(proj_m, proj_m, proj_ba, conv_w, alog_row, dtb_row)


def _gdn_pre_bwd(proj_m, proj_ba, conv_w, alog_row, dtb_row, dq, dk, dv, dgb, dbb, H, dproj):
    T, n_main = proj_m.shape
    A = H * HEAD_DIM
    tm = _pick(T, (256, 128))
    hs = pl.BlockSpec((H, tm, HEAD_DIM), lambda i: (0, i, 0))
    row = pl.BlockSpec((1, LANES), lambda i: (0, 0))

    def body(x_ref, halo_ref, ba_ref, w_ref, al_ref, dt_ref, dq_ref, dk_ref, dv_ref, dgb_ref, dbb_ref, _,
             dc_ref, dba_ref, dal_ref, ddt_ref):
        halo = jnp.where(pl.program_id(0) == 0, 0.0, halo_ref[...])
        c = _conv_fwd(jnp.concatenate([halo, x_ref[...]], axis=0), w_ref[...], tm)
        _, vjp = jax.vjp(functools.partial(_gdn_pointwise, H=H), c, ba_ref[...], al_ref[...], dt_ref[...])
        dc, dba, dal, ddt = vjp((dq_ref[...], dk_ref[...], dv_ref[...], dgb_ref[...], dbb_ref[...]))
        dc_ref[...] = dc
        dba_ref[:, :LANES] = dba.astype(BF16)
        dba_ref[:, LANES:] = jnp.zeros((tm, WIN_BLOCK - LANES), BF16)

        @pl.when(pl.program_id(0) == 0)
        def _():
            dal_ref[...] = dal
            ddt_ref[...] = ddt

        @pl.when(pl.program_id(0) > 0)
        def _():
            dal_ref[...] += dal
            ddt_ref[...] += ddt

    return pl.pallas_call(
        body, name="gdn_pre_bwd", grid=(T // tm,),
        in_specs=[pl.BlockSpec((tm, 3 * A), lambda i: (i, 0)),
                  pl.BlockSpec((HALO, 3 * A), _halo_prev(tm)),
                  pl.BlockSpec((tm, LANES), lambda i: (i, 0)),
                  pl.BlockSpec((CONV_WIDTH, 3 * A), lambda i: (0, 0)),
                  row, row, hs, hs, hs, hs, hs, ANY],
        out_specs=[pl.BlockSpec((tm, 3 * A), lambda i: (i, 0)),
                   pl.BlockSpec((tm, WIN_BLOCK), lambda i: (i, n_main // WIN_BLOCK)), row, row],
        out_shape=[jax.ShapeDtypeStruct((T, 3 * A), F32), jax.ShapeDtypeStruct(dproj.shape, dproj.dtype),
                   jax.ShapeDtypeStruct((1, LANES), F32), jax.ShapeDtypeStruct((1, LANES), F32)],
        input_output_aliases={11: 1},
        compiler_params=_cparams(("arbitrary",)),
    )(proj_m, proj_m, proj_ba, conv_w, alog_row, dtb_row, dq, dk, dv, dgb, dbb, dproj)


def _conv_bwd(proj_m, dc, conv_w, H, dproj):
    T = proj_m.shape[0]
    A = H * HEAD_DIM
    tm = _pick(T, (256, 128))
    nt = T // tm

    def body(x_ref, halo_ref, dc_ref, nxt_ref, w_ref, _, dx_ref, dw_ref):
        i = pl.program_id(0)
        halo = jnp.where(i == 0, 0.0, halo_ref[...])
        xcat = jnp.concatenate([halo, x_ref[...]], axis=0)
        nxt = jnp.where(i == nt - 1, 0.0, nxt_ref[...])
        dc = dc_ref[...]
        dcat = jnp.concatenate([dc, nxt], axis=0)
        w = w_ref[...]
        dx = None
        rows = []
        for k in range(CONV_WIDTH):
            s = CONV_WIDTH - 1 - k
            ds = dcat if s == 0 else pltpu.roll(dcat, tm + HALO - s, 0)
            term = ds[:tm, :] * w[k:k + 1, :]
            dx = term if dx is None else dx + term
            xs = xcat if s == 0 else pltpu.roll(xcat, s, 0)
            rows.append(jnp.sum(dc * xs[HALO:, :], axis=0, keepdims=True))
        dx_ref[...] = dx.astype(BF16)
        dw = jnp.concatenate(rows + [jnp.zeros((HALO - CONV_WIDTH, 3 * A), F32)], axis=0)

        @pl.when(i == 0)
        def _():
            dw_ref[...] = dw

        @pl.when(i > 0)
        def _():
            dw_ref[...] += dw

    return pl.pallas_call(
        body, name="conv_bwd", grid=(nt,),
        in_specs=[pl.BlockSpec((tm, 3 * A), lambda i: (i, 0)),
                  pl.BlockSpec((HALO, 3 * A), _halo_prev(tm)),
                  pl.BlockSpec((tm, 3 * A), lambda i: (i, 0)),
                  pl.BlockSpec((HALO, 3 * A), lambda i: (jnp.minimum((i + 1) * (tm // HALO), T // HALO - 1), 0)),
                  pl.BlockSpec((CONV_WIDTH, 3 * A), lambda i: (0, 0)), ANY],
        out_specs=[pl.BlockSpec((tm, 3 * A), lambda i: (i, 0)),
                   pl.BlockSpec((HALO, 3 * A), lambda i: (0, 0))],
        out_shape=[jax.ShapeDtypeStruct(dproj.shape, dproj.dtype), jax.ShapeDtypeStruct((HALO, 3 * A), F32)],
        input_output_aliases={5: 0},
        compiler_params=_cparams(("arbitrary",)),
    )(proj_m, proj_m, dc, dc, conv_w, dproj)


PAIR = 2 * CHUNK_A


def _b(x):
    return x.astype(BF16)


@jax.custom_vjp
def _bdot(a, b):
    return _dot(_b(a), _b(b))


def _bdot_f(a, b):
    return _bdot(a, b), (a, b)


def _bdot_b(res, g):
    a, b = res
    return _dot_nt(_b(g), _b(b)), _dot_tn(_b(a), _b(g))


_bdot.defvjp(_bdot_f, _bdot_b)


@jax.custom_vjp
def _bdot_nt(a, b):
    return _dot_nt(_b(a), _b(b))


def _bdot_nt_f(a, b):
    return _bdot_nt(a, b), (a, b)


def _bdot_nt_b(res, g):
    a, b = res
    return _dot(_b(g), _b(b)), _dot_tn(_b(g), _b(a))


_bdot_nt.defvjp(_bdot_nt_f, _bdot_nt_b)


@jax.custom_vjp
def _bdot_tn(a, b):
    return _dot_tn(_b(a), _b(b))


def _bdot_tn_f(a, b):
    return _bdot_tn(a, b), (a, b)


def _bdot_tn_b(res, g):
    a, b = res
    return _dot_nt(_b(b), _b(g)), _dot(_b(a), _b(g))


_bdot_tn.defvjp(_bdot_tn_f, _bdot_tn_b)


def _mask_matmul(m, x):
    hi = _b(x)
    r = x - hi.astype(F32)
    mid = _b(r)
    lo = _b(r - mid.astype(F32))
    return (_dot(m, lo) + _dot(m, mid)) + _dot(m, hi)


@jax.custom_vjp
def _mask_dot(m, mt, x):
    return _mask_matmul(m, x)


def _mask_dot_f(m, mt, x):
    return _mask_matmul(m, x), (m, mt)


def _mask_dot_b(res, g):
    m, mt = res
    return jnp.zeros_like(m), jnp.zeros_like(mt), _mask_matmul(mt, g)


_mask_dot.defvjp(_mask_dot_f, _mask_dot_b)

HIGH = lax.Precision.HIGH


def _unit_lower_inverse(L):
    n = L.shape[-1]
    X = -L
    P = (_iota((n, n), 0) == _iota((n, n), 1)).astype(F32) + X
    for _ in range(CHUNK_A.bit_length() - 2):
        X = _dot(X, X, precision=HIGH)
        P = P + _dot(P, X, precision=HIGH)
    return P


@jax.custom_vjp
def _known_inverse(L, P):
    return P


def _known_inverse_f(L, P):
    return P, P


def _known_inverse_b(P, g):
    t = _dot_tn(P, g, precision=HIGH)
    return -_dot_nt(t, P, precision=HIGH), jnp.zeros_like(P)


_known_inverse.defvjp(_known_inverse_f, _known_inverse_b)


def _gdn_prep_fn(q, k, v, gb, bb, P_known=None):
    n = PAIR
    row, col = _iota((n, n), 0), _iota((n, n), 1)
    same = (row >= CHUNK_A) == (col >= CHUNK_A)
    incl = same & (row >= col)
    strict = same & (row > col)
    bc = lambda m: jnp.broadcast_to(_b(m.astype(F32)), q.shape[:1] + (n, n))
    tril, triu, ones = bc(incl), bc(same & (row <= col)), bc(same)
    gc = _mask_dot(tril, triu, gb)
    gl = _mask_dot(ones, ones, gb)
    decay = jnp.where(incl, jnp.exp(jnp.where(incl, gc - jnp.swapaxes(gc, 1, 2), 0.0)), 0.0)
    kb = k * bb
    vb = v * bb
    qs = q * (HEAD_DIM ** -0.5)
    L = jnp.where(strict, _bdot_nt(kb, k) * decay, 0.0)
    P = _unit_lower_inverse(L) if P_known is None else _known_inverse(L, P_known)
    egc = jnp.exp(gc)
    u = _bdot(P, vb)
    w = _bdot(P, kb * egc)
    attn = jnp.where(incl, _bdot_nt(qs, k) * decay, 0.0)
    qg = qs * egc
    kdec = k * jnp.exp(gl - gc)
    eg = jnp.exp(gl)
    if P_known is None:
        return u, w, qg, kdec, attn, eg, P
    return u, w, qg, kdec, attn, eg


def _gdn_chain_fn(S, qg, kdec, u, w, attn, eg):
    C = CHUNK_A
    a, b = (slice(None), slice(0, C)), (slice(None), slice(C, PAIR))
    cat = lambda xs: jnp.concatenate(xs, axis=1)
    vn_a = u[a] - _bdot(w[a], S)
    o_a = _bdot(qg[a], S) + _bdot(attn[a], cat([vn_a, jnp.zeros_like(vn_a)]))
    S1 = S * cat([eg[a], eg[a]]) + _bdot_tn(kdec[a], vn_a)
    vn_b = u[b] - _bdot(w[b], S1)
    o_b = _bdot(qg[b], S1) + _bdot(attn[b], cat([vn_a, vn_b]))
    S2 = S1 * cat([eg[b], eg[b]]) + _bdot_tn(kdec[b], vn_b)
    return cat([o_a, o_b]), S2


def _gdn_prep(q, k, v, gb, bb):
    H, T, _ = q.shape
    pb = _pick(T // PAIR, (8, 4, 2, 1))
    hs = pl.BlockSpec((1, PAIR * pb, HEAD_DIM), lambda h, n: (h, n, 0))
    hshape = jax.ShapeDtypeStruct((H, T, HEAD_DIM), F32)

    def body(q_ref, k_ref, v_ref, gb_ref, bb_ref, *out_refs):
        pairs = lambda ref: ref[0].reshape(pb, PAIR, HEAD_DIM)
        outs = _gdn_prep_fn(pairs(q_ref), pairs(k_ref), pairs(v_ref), pairs(gb_ref), pairs(bb_ref))
        for ref, val in zip(out_refs, outs):
            ref[0] = val.reshape(pb * PAIR, HEAD_DIM)

    return pl.pallas_call(
        body, name="gdn_prep", grid=(H, T // (PAIR * pb)),
        in_specs=[hs] * 5, out_specs=[hs] * 7, out_shape=[hshape] * 7,
        compiler_params=_cparams(("parallel", "parallel")),
    )(q, k, v, gb, bb)


def _gdn_prep_bwd(q, k, v, gb, bb, pinv, du, dw, dqg, dkd, dat, deg):
    H, T, _ = q.shape
    pb = _pick(T // PAIR, (8, 4, 2, 1))
    hs = pl.BlockSpec((1, PAIR * pb, HEAD_DIM), lambda h, n: (h, n, 0))
    hshape = jax.ShapeDtypeStruct((H, T, HEAD_DIM), F32)

    def body(*refs):
        in_refs, p_ref, ct_refs, out_refs = refs[:5], refs[5], refs[6:12], refs[12:]
        pairs = lambda ref: ref[0].reshape(pb, PAIR, HEAD_DIM)
        P = pairs(p_ref)
        _, vjp = jax.vjp(lambda *a: _gdn_prep_fn(*a, P_known=P), *[pairs(r) for r in in_refs])
        grads = vjp(tuple(pairs(r) for r in ct_refs))
        for ref, val in zip(out_refs, grads):
            ref[0] = val.reshape(pb * PAIR, HEAD_DIM)

    return pl.pallas_call(
        body, name="gdn_prep_bwd", grid=(H, T // (PAIR * pb)),
        in_specs=[hs] * 12, out_specs=[hs] * 5, out_shape=[hshape] * 5,
        compiler_params=_cparams(("parallel", "parallel")),
    )(q, k, v, gb, bb, pinv, du, dw, dqg, dkd, dat, deg)


def _gdn_chain(qg, kd, u, w, attn, eg):
    H, T, _ = qg.shape
    N = T // PAIR
    hs = pl.BlockSpec((H, PAIR, HEAD_DIM), lambda n: (0, n, 0))
    ss = pl.BlockSpec((1, H, HEAD_DIM, HEAD_DIM), lambda n: (n, 0, 0, 0))

    def body(qg_ref, kd_ref, u_ref, w_ref, at_ref, eg_ref, o_ref, sall_ref, s_ref):
        @pl.when(pl.program_id(0) == 0)
        def _():
            s_ref[...] = jnp.zeros_like(s_ref)

        S = s_ref[...]
        sall_ref[0] = S
        o, S2 = _gdn_chain_fn(S, qg_ref[...], kd_ref[...], u_ref[...], w_ref[...], at_ref[...], eg_ref[...])
        o_ref[...] = o
        s_ref[...] = S2

    return pl.pallas_call(
        body, name="gdn_chain", grid=(N,),
        in_specs=[hs] * 6, out_specs=[hs, ss],
        out_shape=[jax.ShapeDtypeStruct((H, T, HEAD_DIM), F32),
                   jax.ShapeDtypeStruct((N, H, HEAD_DIM, HEAD_DIM), F32)],
        scratch_shapes=[pltpu.VMEM((H, HEAD_DIM, HEAD_DIM), F32)],
        compiler_params=_cparams(("arbitrary",)),
    )(qg, kd, u, w, attn, eg)


def _gdn_chain_bwd(qg, kd, u, w, attn, eg, sall, do):
    H, T, _ = qg.shape
    N = T // PAIR
    hs = pl.BlockSpec((H, PAIR, HEAD_DIM), lambda n: (0, N - 1 - n, 0))
    ss = pl.BlockSpec((1, H, HEAD_DIM, HEAD_DIM), lambda n: (N - 1 - n, 0, 0, 0))
    hshape = jax.ShapeDtypeStruct((H, T, HEAD_DIM), F32)

    def body(qg_ref, kd_ref, u_ref, w_ref, at_ref, eg_ref, sall_ref, do_ref, *rest):
        out_refs, ds_ref = rest[:6], rest[6]

        @pl.when(pl.program_id(0) == 0)
        def _():
            ds_ref[...] = jnp.zeros_like(ds_ref)

        _, vjp = jax.vjp(_gdn_chain_fn, sall_ref[0], qg_ref[...], kd_ref[...], u_ref[...], w_ref[...],
                         at_ref[...], eg_ref[...])
        grads = vjp((do_ref[...], ds_ref[...]))
        ds_ref[...] = grads[0]
        for ref, val in zip(out_refs, grads[1:]):
            ref[...] = val

    return pl.pallas_call(
        body, name="gdn_chain_bwd", grid=(N,),
        in_specs=[hs] * 6 + [ss, hs], out_specs=[hs] * 6, out_shape=[hshape] * 6,
        scratch_shapes=[pltpu.VMEM((H, HEAD_DIM, HEAD_DIM), F32)],
        compiler_params=_cparams(("arbitrary",)),
    )(qg, kd, u, w, attn, eg, sall, do)


def _post_fn(ogs, za, hw):
    outs = []
    for h, o in enumerate(ogs):
        r = lax.rsqrt(jnp.mean(o * o, axis=-1, keepdims=True) + EPS)
        outs.append(o * r * hw * _silu(za[:, h * HEAD_DIM:(h + 1) * HEAD_DIM]))
    return jnp.concatenate(outs, axis=1)


def _gdn_post(og, proj_m, hw):
    H, T, _ = og.shape
    A = H * HEAD_DIM
    tm = _pick(T, (512, 256, 128))

    def body(og_ref, za_ref, hw_ref, o_ref):
        o_ref[...] = _post_fn(tuple(og_ref[h] for h in range(H)), za_ref[...], hw_ref[...]).astype(BF16)

    return pl.pallas_call(
        body, name="gdn_post", grid=(T // tm,),
        in_specs=[pl.BlockSpec((H, tm, HEAD_DIM), lambda i: (0, i, 0)),
                  pl.BlockSpec((tm, A), lambda i: (i, ZA_BLOCK)),
                  pl.BlockSpec((1, HEAD_DIM), lambda i: (0, 0))],
        out_specs=pl.BlockSpec((tm, A), lambda i: (i, 0)),
        out_shape=jax.ShapeDtypeStruct((T, A), BF16),
        compiler_params=_cparams(("parallel",)),
    )(og, proj_m, hw)


def _gdn_post_bwd(og, proj_m, hw, d_o, dproj):
    H, T, _ = og.shape
    A = H * HEAD_DIM
    tm = _pick(T, (256, 128))

    def body(og_ref, za_ref, hw_ref, do_ref, _, dog_ref, dza_ref, dhw_ref):
        _, vjp = jax.vjp(_post_fn, tuple(og_ref[h] for h in range(H)), za_ref[...], hw_ref[...])
        dog, dza, dhw = vjp(do_ref[...])
        for h in range(H):
            dog_ref[h] = dog[h]
        dza_ref[...] = dza.astype(BF16)

        @pl.when(pl.program_id(0) == 0)
        def _():
            dhw_ref[...] = dhw

        @pl.when(pl.program_id(0) > 0)
        def _():
            dhw_ref[...] += dhw

    return pl.pallas_call(
        body, name="gdn_post_bwd", grid=(T // tm,),
        in_specs=[pl.BlockSpec((H, tm, HEAD_DIM), lambda i: (0, i, 0)),
                  pl.BlockSpec((tm, A), lambda i: (i, ZA_BLOCK)),
                  pl.BlockSpec((1, HEAD_DIM), lambda i: (0, 0)),
                  pl.BlockSpec((tm, A), lambda i: (i, 0)), ANY],
        out_specs=[pl.BlockSpec((H, tm, HEAD_DIM), lambda i: (0, i, 0)),
                   pl.BlockSpec((tm, A), lambda i: (i, ZA_BLOCK)),
                   pl.BlockSpec((1, HEAD_DIM), lambda i: (0, 0))],
        out_shape=[jax.ShapeDtypeStruct((H, T, HEAD_DIM), F32), jax.ShapeDtypeStruct(dproj.shape, dproj.dtype),
                   jax.ShapeDtypeStruct((1, HEAD_DIM), F32)],
        input_output_aliases={4: 1},
        compiler_params=_cparams(("arbitrary",)),
    )(og, proj_m, hw, d_o, dproj)


def _sgu_fn(ub, vb, zb, lw, lb, W, bbc):
    G = len(W)
    tm = ub.shape[0]
    mu = jnp.mean(vb, axis=-1, keepdims=True)
    xc = vb - mu
    var = jnp.mean(xc * xc, axis=-1, keepdims=True)
    vn = xc * lax.rsqrt(var + EPS) * lw + lb
    mask = _iota((CHUNK_B, CHUNK_B), 0) >= _iota((CHUNK_B, CHUNK_B), 1)
    cols = []
    for g in range(G):
        wm = jnp.where(mask, W[g], 0.0).astype(BF16)
        rows = []
        for c in range(tm // CHUNK_B):
            blk = vn[c * CHUNK_B:(c + 1) * CHUNK_B, g * HEAD_DIM:(g + 1) * HEAD_DIM].astype(BF16)
            rows.append(_dot(wm, blk) + bbc[g])
        cols.append(jnp.concatenate(rows, axis=0) if len(rows) > 1 else rows[0])
    s = jnp.concatenate(cols, axis=1)
    return ub * s * _silu(zb)


ZA_BLOCK = 6


def _sgu_cols(A, B):
    assert A == B
    return 3, 4, 5


def _sgu_fwd(proj_m, lw, lb, W, bbc, A):
    T = proj_m.shape[0]
    G = W.shape[0]
    B = G * HEAD_DIM
    tm = _pick(T, (256, 128))
    cu, cv, cz = _sgu_cols(A, B)

    def body(u_ref, v_ref, z_ref, lw_ref, lb_ref, w_ref, b_ref, o_ref):
        o_ref[...] = _sgu_fn(u_ref[...], v_ref[...], z_ref[...], lw_ref[...], lb_ref[...],
                             tuple(w_ref[g] for g in range(G)), tuple(b_ref[g] for g in range(G))).astype(BF16)

    row = pl.BlockSpec((1, B), lambda i: (0, 0))
    cube = pl.BlockSpec((G, CHUNK_B, CHUNK_B), lambda i: (0, 0, 0))
    return pl.pallas_call(
        body, name="sgu_fwd", grid=(T // tm,),
        in_specs=[pl.BlockSpec((tm, B), lambda i: (i, cu)), pl.BlockSpec((tm, B), lambda i: (i, cv)),
                  pl.BlockSpec((tm, B), lambda i: (i, cz)), row, row, cube, cube],
        out_specs=pl.BlockSpec((tm, B), lambda i: (i, 0)),
        out_shape=jax.ShapeDtypeStruct((T, B), BF16),
        compiler_params=_cparams(("parallel",)),
    )(proj_m, proj_m, proj_m, lw, lb, W, bbc)


def _sgu_bwd(proj_m, lw, lb, W, bbc, d_o, A, dproj):
    T = proj_m.shape[0]
    G = W.shape[0]
    B = G * HEAD_DIM
    tm = _pick(T, (256, 128))
    nt = T // tm
    cu, cv, cz = _sgu_cols(A, B)

    def body(u_ref, v_ref, z_ref, lw_ref, lb_ref, w_ref, b_ref, do_ref, _,
             dp_ref, dlw_ref, dlb_ref, dw_ref, db_ref, dbb_ref):
        _, vjp = jax.vjp(_sgu_fn, u_ref[...], v_ref[...], z_ref[...], lw_ref[...], lb_ref[...],
                         tuple(w_ref[g] for g in range(G)), tuple(b_ref[g] for g in range(G)))
        du, dv, dz, dlw, dlb, dW, dbb = vjp(do_ref[...])
        dW, dbb = jnp.stack(dW, axis=0), jnp.stack(dbb, axis=0)
        dp_ref[:, 0:B] = du.astype(BF16)
        dp_ref[:, B:2 * B] = dv.astype(BF16)
        dp_ref[:, 2 * B:3 * B] = dz.astype(BF16)
        i = pl.program_id(0)

        @pl.when(i == 0)
        def _():
            dlw_ref[...] = dlw
            dlb_ref[...] = dlb
            dw_ref[...] = dW
            dbb_ref[...] = dbb

        @pl.when(i > 0)
        def _():
            dlw_ref[...] += dlw
            dlb_ref[...] += dlb
            dw_ref[...] += dW
            dbb_ref[...] += dbb

        @pl.when(i == nt - 1)
        def _():
            db_ref[...] = jnp.sum(dbb_ref[...], axis=-1, keepdims=True)

    row = pl.BlockSpec((1, B), lambda i: (0, 0))
    cube = pl.BlockSpec((G, CHUNK_B, CHUNK_B), lambda i: (0, 0, 0))
    return pl.pallas_call(
        body, name="sgu_bwd", grid=(nt,),
        in_specs=[pl.BlockSpec((tm, B), lambda i: (i, cu)), pl.BlockSpec((tm, B), lambda i: (i, cv)),
                  pl.BlockSpec((tm, B), lambda i: (i, cz)), row, row, cube, cube,
                  pl.BlockSpec((tm, B), lambda i: (i, A // B)), ANY],
        out_specs=[pl.BlockSpec((tm, 3 * B), lambda i: (i, 1)), row, row, cube,
                   pl.BlockSpec((G, CHUNK_B, 1), lambda i: (0, 0, 0))],
        out_shape=[jax.ShapeDtypeStruct(dproj.shape, dproj.dtype), jax.ShapeDtypeStruct((1, B), F32),
                   jax.ShapeDtypeStruct((1, B), F32), jax.ShapeDtypeStruct((G, CHUNK_B, CHUNK_B), F32),
                   jax.ShapeDtypeStruct((G, CHUNK_B, 1), F32)],
        input_output_aliases={8: 0},
        scratch_shapes=[pltpu.VMEM((G, CHUNK_B, CHUNK_B), F32)],
        compiler_params=_cparams(("arbitrary",)),
    )(proj_m, proj_m, proj_m, lw, lb, W, bbc, d_o, dproj)


def _head_fn(mix, x, fw, tgt):
    h = x + mix
    y = _rms_fn(h, fw)
    e = y - tgt
    return 0.5 * jnp.sum(jnp.mean(e * e, axis=-1, keepdims=True), axis=0, keepdims=True)


def _out_proj_loss(oa, ob, wout, x, tgt, fw):
    T, A = oa.shape
    B = ob.shape[1]
    D = x.shape[1]
    tm = _pick(T, (256, 128))

    def body(oa_ref, ob_ref, w_ref, x_ref, t_ref, fw_ref, dh_ref, dhb_ref, loss_ref, dfw_ref):
        mix = _dot(oa_ref[...], w_ref[0:A, :]) + _dot(ob_ref[...], w_ref[A:A + B, :])
        xv, tv = x_ref[...], t_ref[...]
        loss, vjp = jax.vjp(lambda m, f: _head_fn(m, xv, f, tv), mix, fw_ref[...])
        dh, dfw = vjp(jnp.ones((1, 1), F32))
        dh_ref[...] = dh
        dhb_ref[...] = dh.astype(BF16)
        lrow = jnp.broadcast_to(loss, (1, LANES))

        @pl.when(pl.program_id(0) == 0)
        def _():
            loss_ref[...] = lrow
            dfw_ref[...] = dfw

        @pl.when(pl.program_id(0) > 0)
        def _():
            loss_ref[...] += lrow
            dfw_ref[...] += dfw

    tile = pl.BlockSpec((tm, D), lambda i: (i, 0))
    return pl.pallas_call(
        body, name="out_proj_loss", grid=(T // tm,),
        in_specs=[pl.BlockSpec((tm, A), lambda i: (i, 0)), pl.BlockSpec((tm, B), lambda i: (i, 0)),
                  pl.BlockSpec((A + B, D), lambda i: (0, 0)), tile, tile,
                  pl.BlockSpec((1, D), lambda i: (0, 0))],
        out_specs=[tile, tile, pl.BlockSpec((1, LANES), lambda i: (0, 0)),
                   pl.BlockSpec((1, D), lambda i: (0, 0))],
        out_shape=[jax.ShapeDtypeStruct((T, D), F32), jax.ShapeDtypeStruct((T, D), BF16),
                   jax.ShapeDtypeStruct((1, LANES), F32), jax.ShapeDtypeStruct((1, D), F32)],
        compiler_params=_cparams(("arbitrary",)),
    )(oa, ob, wout, x, tgt, fw)


def _adamw(w, g, m, v, name):
    R, Cn = w.shape
    tr = R if R * Cn <= 512 * 1024 else _pick(R, (256, 128, 64, 32, 16, 8))

    def body(w_ref, g_ref, m_ref, v_ref, d_ref, mo_ref, vo_ref):
        g = g_ref[...]
        m = ADAM_B1 * m_ref[...] + (1.0 - ADAM_B1) * g
        v = ADAM_B2 * v_ref[...] + (1.0 - ADAM_B2) * jnp.square(g)
        m_hat = m / (1.0 - ADAM_B1 ** ADAM_STEP)
        v_hat = v / (1.0 - ADAM_B2 ** ADAM_STEP)
        d_ref[...] = -ADAM_LR * (m_hat / (jnp.sqrt(v_hat) + ADAM_EPS) + ADAM_WD * w_ref[...])
        mo_ref[...] = m
        vo_ref[...] = v

    tile = pl.BlockSpec((tr, Cn), lambda i: (i, 0))
    shape = jax.ShapeDtypeStruct((R, Cn), F32)
    return pl.pallas_call(
        body, name=name, grid=(R // tr,), in_specs=[tile] * 4, out_specs=[tile] * 3,
        out_shape=[shape] * 3, compiler_params=_cparams(("parallel",)),
    )(w, g, m, v)


def _place():
    x, y, c = lax.axis_index("x"), lax.axis_index("y"), lax.axis_index("c")
    others = [(1 - x, y), (x, 1 - y), (1 - x, 1 - y)]
    return x, y, c, others


def _chip_index(px, py):
    return 2 * px + py


ANY = pl.BlockSpec(memory_space=pl.ANY)


def _gather_ride(blocks, split):
    n = len(blocks)

    def copies(in_refs, out_refs, send_sems, recv_sems):
        x, y, c, others = _place()
        me = _chip_index(x, y)

        def copy(sem, src, dst, to):
            return pltpu.make_async_remote_copy(src_ref=src, dst_ref=dst, send_sem=send_sems.at[sem],
                                                recv_sem=recv_sems.at[sem], device_id=to, device_id_type=MESH_ID)

        def part(a, chip, core):
            if not split[a]:
                return out_refs[a].at[chip]
            h = blocks[a].shape[0] // 2
            return out_refs[a].at[chip, pl.ds(core * h, h), :]

        def mine(a):
            if not split[a]:
                return in_refs[a]
            h = blocks[a].shape[0] // 2
            return in_refs[a].at[pl.ds(c * h, h), :]

        sends = [copy(3 * a + j, mine(a), part(a, me, c), (*chip, c))
                 for j, chip in enumerate(others) for a in range(n)]
        arrivals = [(a, j, copy(3 * a + j, part(a, _chip_index(*chip), c), part(a, _chip_index(*chip), c), (*chip, c)))
                    for j, chip in enumerate(others) for a in range(n)]
        passes = {(a, j): copy(3 * (n + a) + j, part(a, _chip_index(*chip), c), part(a, _chip_index(*chip), c),
                               (x, y, 1 - c))
                  for j, chip in enumerate(others) for a in range(n) if split[a]}
        passed = [copy(3 * (n + a) + j, part(a, _chip_index(*chip), 1 - c), part(a, _chip_index(*chip), 1 - c),
                       (x, y, 1 - c))
                  for j, chip in enumerate(others) for a in range(n) if split[a]]
        return sends, arrivals, passes, passed

    def start(in_refs, out_refs, send_sems, recv_sems):
        for cp in copies(in_refs, out_refs, send_sems, recv_sems)[0]:
            cp.start()

    def finish(in_refs, out_refs, send_sems, recv_sems):
        sends, arrivals, passes, passed = copies(in_refs, out_refs, send_sems, recv_sems)
        for a, j, cp in arrivals:
            cp.wait_recv()
            if split[a]:
                passes[(a, j)].start()
        for cp in passed:
            cp.wait_recv()
        for cp in sends + list(passes.values()):
            cp.wait_send()

    shapes = [jax.ShapeDtypeStruct((N_CHIPS,) + b.shape, b.dtype) for b in blocks]
    return _Ride(blocks, shapes, 6 * n, start, finish)


def _put_own(gathered, own):
    me = _chip_index(lax.axis_index("x"), lax.axis_index("y"))
    return lax.dynamic_update_index_in_dim(gathered, own, me, 0)


def _run_ride(ride, name):
    def body(*refs):
        n_in, n_out = len(ride.operands), len(ride.out_shape)
        ins, outs, (send, recv) = refs[:n_in], refs[n_in:n_in + n_out], refs[n_in + n_out:]
        ride.start(ins, outs, send, recv)
        ride.finish(ins, outs, send, recv)

    return pl.pallas_call(
        body, name=name, in_specs=[ANY] * len(ride.operands), out_specs=[ANY] * len(ride.out_shape),
        out_shape=ride.out_shape,
        scratch_shapes=[pltpu.SemaphoreType.DMA((ride.n_sems,)), pltpu.SemaphoreType.DMA((ride.n_sems,))],
        compiler_params=pltpu.CompilerParams(has_side_effects=True),
    )(*ride.operands)


def _allreduce_small(buf):
    R, L = buf.shape

    def body(in_ref, out_ref, sib_ref, pair_ref, chips_ref, send_sems, recv_sems):
        x, y, c, others = _place()
        me = _chip_index(x, y)
        sibling = (x, y, 1 - c)
        cp = pltpu.make_async_remote_copy(src_ref=in_ref, dst_ref=sib_ref, send_sem=send_sems.at[0],
                                          recv_sem=recv_sems.at[0], device_id=sibling, device_id_type=MESH_ID)
        cp.start()
        cp.wait()
        pair_ref[...] = in_ref[...] + sib_ref[...]
        sends = []
        for j, chip in enumerate(others):
            s = pltpu.make_async_remote_copy(src_ref=pair_ref, dst_ref=chips_ref.at[me],
                                             send_sem=send_sems.at[1 + j], recv_sem=recv_sems.at[1 + j],
                                             device_id=(*chip, c), device_id_type=MESH_ID)
            s.start()
            sends.append(s)
        chips_ref[me] = pair_ref[...]
        for j, chip in enumerate(others):
            k = _chip_index(*chip)
            pltpu.make_async_remote_copy(src_ref=pair_ref, dst_ref=chips_ref.at[k], send_sem=send_sems.at[1 + j],
                                         recv_sem=recv_sems.at[1 + j], device_id=(*chip, c),
                                         device_id_type=MESH_ID).wait_recv()
        for s in sends:
            s.wait_send()
        out_ref[...] = ((chips_ref[0] + chips_ref[1]) + chips_ref[2]) + chips_ref[3]

    vm = pl.BlockSpec(memory_space=pltpu.VMEM)
    return pl.pallas_call(
        body, name="allreduce_small", in_specs=[vm], out_specs=vm,
        out_shape=jax.ShapeDtypeStruct((R, L), F32),
        scratch_shapes=[pltpu.VMEM((R, L), F32), pltpu.VMEM((R, L), F32), pltpu.VMEM((N_CHIPS, R, L), F32),
                        pltpu.SemaphoreType.DMA((4,)), pltpu.SemaphoreType.DMA((4,))],
        compiler_params=pltpu.CompilerParams(vmem_limit_bytes=VMEM_LIMIT),
    )(buf)


def _pair_ride(g):
    nb, R, Cn = g.shape
    h = R // 2

    def copy(in_refs, out_refs, send_sems, recv_sems):
        x, y, c, _ = _place()
        return pltpu.make_async_remote_copy(src_ref=in_refs[0].at[:, pl.ds((1 - c) * h, h), :], dst_ref=out_refs[0],
                                            send_sem=send_sems.at[0], recv_sem=recv_sems.at[0],
                                            device_id=(x, y, 1 - c), device_id_type=MESH_ID)

    return _Ride([g], [jax.ShapeDtypeStruct((nb, h, Cn), g.dtype)], 1,
                 lambda *refs: copy(*refs).start(), lambda *refs: copy(*refs).wait())


def _pair_sum(g, land, c_arr, name, ride=None):
    nb, R, Cn = g.shape
    hr = R // 2
    tr = _pick(hr, (256, 128, 64, 32, 16))
    nt = hr // tr

    def body(c_ref, g_ref, l_ref, o_ref):
        o_ref[...] = (g_ref[...] + l_ref[...]).astype(BF16)

    return _pallas(
        body, (c_arr, g, land), name=name, prefetch=1, grid=(nb, nt),
        in_specs=[pl.BlockSpec((1, tr, Cn), lambda b, i, c_ref: (b, c_ref[0] * nt + i, 0)),
                  pl.BlockSpec((1, tr, Cn), lambda b, i, c_ref: (b, i, 0))],
        out_specs=pl.BlockSpec((1, tr, Cn), lambda b, i, c_ref: (b, i, 0)),
        out_shape=jax.ShapeDtypeStruct((nb, hr, Cn), BF16),
        semantics=("parallel", "parallel"), ride=ride)


def _chip_ride(parts):
    m = len(parts)

    def copies(in_refs, out_refs, send_sems, recv_sems):
        x, y, c, others = _place()
        me = _chip_index(x, y)
        sends, arrivals = [], []
        for j, chip in enumerate(others):
            k = _chip_index(*chip)
            for n in range(m):
                mk = lambda src, dst: pltpu.make_async_remote_copy(
                    src_ref=src, dst_ref=dst, send_sem=send_sems.at[m * j + n], recv_sem=recv_sems.at[m * j + n],
                    device_id=(*chip, c), device_id_type=MESH_ID)
                sends.append(mk(in_refs[n].at[k], out_refs[n].at[me]))
                arrivals.append(mk(in_refs[n].at[k], out_refs[n].at[k]))
        return sends, arrivals

    def start(*refs):
        for cp in copies(*refs)[0]:
            cp.start()

    def finish(*refs):
        sends, arrivals = copies(*refs)
        for cp in arrivals:
            cp.wait_recv()
        for cp in sends:
            cp.wait_send()

    return _Ride(parts, [jax.ShapeDtypeStruct(p.shape, p.dtype) for p in parts], 3 * m, start, finish)


def _put_own_slot(q, p):
    me = _chip_index(lax.axis_index("x"), lax.axis_index("y"))
    return lax.dynamic_update_index_in_dim(q, lax.dynamic_index_in_dim(p, me, 0, keepdims=False), me, 0)


def _chip_sum(q, name):
    nb, hr, Cn = q.shape
    tr = _pick(hr, (256, 128, 64, 32, 16))

    def body(q_ref, o_ref):
        f = lambda k: q_ref[k].astype(F32)
        o_ref[...] = ((f(0) + f(1)) + f(2)) + f(3)

    return pl.pallas_call(
        body, name=name, grid=(hr // tr,),
        in_specs=[pl.BlockSpec((nb, tr, Cn), lambda i: (0, i, 0))],
        out_specs=pl.BlockSpec((tr, Cn), lambda i: (i, 0)),
        out_shape=jax.ShapeDtypeStruct((hr, Cn), F32),
        compiler_params=_cparams(("parallel",)),
    )(q)


def _sibling_concat(rw, ro):
    hi, Cb = rw.shape
    ho, D = ro.shape

    def body(rw_ref, ro_ref, tw_ref, to_ref, send_sems, recv_sems):
        x, y, c, _ = _place()
        sibling = (x, y, 1 - c)
        a = pltpu.make_async_remote_copy(src_ref=rw_ref, dst_ref=tw_ref, send_sem=send_sems.at[0],
                                         recv_sem=recv_sems.at[0], device_id=sibling, device_id_type=MESH_ID)
        b = pltpu.make_async_remote_copy(src_ref=ro_ref, dst_ref=to_ref, send_sem=send_sems.at[1],
                                         recv_sem=recv_sems.at[1], device_id=sibling, device_id_type=MESH_ID)
        a.start()
        b.start()
        a.wait()
        b.wait()

    tw, to = pl.pallas_call(
        body, name="sibling_concat", in_specs=[ANY, ANY], out_specs=[ANY, ANY],
        out_shape=[jax.ShapeDtypeStruct((hi, Cb), F32), jax.ShapeDtypeStruct((ho, D), F32)],
        scratch_shapes=[pltpu.SemaphoreType.DMA((2,)), pltpu.SemaphoreType.DMA((2,))],
        compiler_params=pltpu.CompilerParams(has_side_effects=True),
    )(rw, ro)
    c = lax.axis_index("c")
    join = lambda mine, theirs: lax.dynamic_update_slice_in_dim(
        jnp.concatenate([mine, mine], axis=0), theirs, (1 - c) * mine.shape[0], axis=0)
    return join(rw, tw), join(ro, to)


class _Layout:
    def __init__(self, H, G, nb, Cb):
        A, B = H * HEAD_DIM, G * HEAD_DIM
        self.n_main = 4 * A + 3 * B
        self.k = -(-(self.n_main + LANES) // WIN_BLOCK) * WIN_BLOCK
        cuts = [0, 3 * A, 4 * A, 4 * A + 2 * H, nb * Cb]
        starts = [0, 3 * A + 3 * B, self.n_main, 3 * A]
        self.pieces = []
        self.windows, self.runs = [], []
        for n in range(nb):
            segs = []
            for s in range(4):
                lo, hi = max(cuts[s], n * Cb), min(cuts[s + 1], (n + 1) * Cb)
                if lo < hi:
                    segs.append((starts[s] + lo - cuts[s], lo - n * Cb, hi - lo))
            self.pieces += [(own, n, col, ln) for own, col, ln in segs]
            blocks = sorted({b for own, _, ln in segs for b in range(own // WIN_BLOCK, (own + ln - 1) // WIN_BLOCK + 1)})
            self.windows.append(blocks)
            self.runs.append([(blocks.index(own // WIN_BLOCK) * WIN_BLOCK + own % WIN_BLOCK, ln)
                              for own, _, ln in segs])
        self.wb = max(len(b) for b in self.windows)
        self.table = [b + [b[-1]] * (self.wb - len(b)) for b in self.windows]
        self.pieces.sort()

    def to_own_order(self, g_in):
        D = g_in.shape[1]
        cols, at = [], 0
        for own, n, col, ln in self.pieces:
            if own > at:
                cols.append(jnp.zeros((D, own - at), g_in.dtype))
            cols.append(g_in[n, :, col:col + ln])
            at = own + ln
        if at < self.k:
            cols.append(jnp.zeros((D, self.k - at), g_in.dtype))
        return jnp.concatenate(cols, axis=1)

    def from_window(self, win, chip, Cb):
        pick = lambda runs: (lambda w: jnp.concatenate([w[:, c:c + ln] for c, ln in runs], axis=1))
        return lax.switch(chip, [pick(r) for r in self.runs], win)


def _device_step(x, tgt, norm_w, g_in, wout_b, conv_b, a_log, dt_bias, head_norm_w, sgu_ln_w, sgu_ln_b,
                 w_spatial, b_spatial, final_norm_w, c_arr):
    T, D = x.shape
    H = a_log.shape[1]
    A = H * HEAD_DIM
    G = w_spatial.shape[0]
    B = G * HEAD_DIM
    nb, _, Cb = g_in.shape
    Rb = wout_b.shape[0]
    lay = _Layout(H, G, nb, Cb)
    w_own = lay.to_own_order(g_in)
    alog_row = jnp.pad(a_log, ((0, 0), (H, LANES - 2 * H)))
    dtb_row = jnp.pad(dt_bias, ((0, 0), (H, LANES - 2 * H)))
    bbc = jnp.broadcast_to(b_spatial[:, :, None], (G, CHUNK_B, CHUNK_B))

    xn, xn_t = _rms_in(x, norm_w)
    proj_m, (g_out, g_conv) = _mm_nn(xn, w_own, F32, "in_proj", cols=(0, lay.n_main),
                                     ride=_gather_ride([wout_b, conv_b], [True, False]))
    wout = _put_own(g_out, wout_b).reshape(nb * Rb, D)
    conv_w = _put_own(g_conv, conv_b).transpose(1, 0, 2).reshape(CONV_WIDTH, nb * conv_b.shape[1])
    proj_ba = _mm_nn(xn, w_own, F32, "in_proj_ba", cols=(lay.n_main, LANES))
    q, k, v, gb, bb = _gdn_pre(proj_m, proj_ba, conv_w, alog_row, dtb_row, H)
    u, w, qg, kd, attn, eg, pinv = _gdn_prep(q, k, v, gb, bb)
    og, sall = _gdn_chain(qg, kd, u, w, attn, eg)
    oa = _gdn_post(og, proj_m, head_norm_w)
    ob = _sgu_fwd(proj_m, sgu_ln_w, sgu_ln_b, w_spatial, bbc, A)
    dh, dhb, loss_row, d_fnw = _out_proj_loss(oa, ob, wout, x, tgt, final_norm_w.reshape(1, D))

    d_o = _mm_nn(dhb, wout.T, F32, "out_proj_dx")
    dproj = lax.empty((T, lay.k), BF16)
    dproj, d_lw, d_lb, d_ws, d_bs = _sgu_bwd(proj_m, sgu_ln_w, sgu_ln_b, w_spatial, bbc, d_o, A, dproj)
    dog, dproj, d_hw = _gdn_post_bwd(og, proj_m, head_norm_w, d_o, dproj)
    dqg, dkd, du, dw, dat, deg = _gdn_chain_bwd(qg, kd, u, w, attn, eg, sall, dog)
    dq, dk, dv, dgb, dbb = _gdn_prep_bwd(q, k, v, gb, bb, pinv, du, dw, dqg, dkd, dat, deg)
    dc, dproj, d_al, d_dt = _gdn_pre_bwd(proj_m, proj_ba, conv_w, alog_row, dtb_row, dq, dk, dv, dgb, dbb, H,
                                         dproj)
    dproj, d_conv = _conv_bwd(proj_m, dc, conv_w, H, dproj)

    table = jnp.array([b for row in lay.table for b in row], jnp.int32)
    d_win = _mm_windows(xn_t, dproj, table, nb, "in_proj_dw")
    d_wout, (land_w,) = _mm_tn_pair(oa, ob, dhb, "out_proj_dw", ride=_pair_ride(d_win))
    d_wout = d_wout.reshape(nb, Rb, D)
    pair_w, (land_o,) = _pair_sum(d_win, land_w, c_arr, "pair_sum_w_in", ride=_pair_ride(d_wout))
    pair_o = _pair_sum(d_wout, land_o, c_arr, "pair_sum_w_out")
    dxn, (all_w, all_o) = _mm_nt_rhs_outer(dproj, w_own, F32, "in_proj_dx", ride=_chip_ride([pair_w, pair_o]))
    all_w, all_o = _put_own_slot(all_w, pair_w), _put_own_slot(all_o, pair_o)
    grad_x, d_nw = _rms_in_bwd(x, norm_w, dxn, dh)
    small = dict(norm_w=d_nw, conv_w=d_conv[:CONV_WIDTH], a_log=d_al[:, H:2 * H], dt_bias=d_dt[:, H:2 * H],
                 head_norm_w=d_hw, sgu_ln_w=d_lw, sgu_ln_b=d_lb, w_spatial=d_ws, b_spatial=d_bs[:, :, 0],
                 final_norm_w=d_fnw)
    return loss_row, grad_x, small, all_w, all_o


SMALL = ("norm_w", "conv_w", "a_log", "dt_bias", "head_norm_w", "sgu_ln_w", "sgu_ln_b", "w_spatial",
         "b_spatial", "final_norm_w")


def _pack(parts):
    rows = []
    for p in parts:
        f = p.reshape(-1)
        f = jnp.pad(f, (0, (-f.shape[0]) % (8 * LANES)))
        rows.append(f.reshape(-1, LANES))
    return jnp.concatenate(rows, axis=0)


def _unpack(buf, shapes):
    out, r = [], 0
    for s in shapes:
        n = 1
        for d in s:
            n *= d
        nr = -(-n // (8 * LANES)) * 8
        out.append(buf[r:r + nr].reshape(-1)[:n].reshape(s))
        r += nr
    return out


def kernel(x, norm_w, w_in, conv_w, a_log, dt_bias, head_norm_w, sgu_ln_w, sgu_ln_b, w_spatial, b_spatial, w_out, final_norm_w, loss_target, m_norm_w, m_w_in, m_conv_w, m_a_log, m_dt_bias, m_head_norm_w, m_sgu_ln_w, m_sgu_ln_b, m_w_spatial, m_b_spatial, m_w_out, m_final_norm_w, v_norm_w, v_w_in, v_conv_w, v_a_log, v_dt_bias, v_head_norm_w, v_sgu_ln_w, v_sgu_ln_b, v_w_spatial, v_b_spatial, v_w_out, v_final_norm_w):
    T, D = x.shape[1], x.shape[2]
    weights = dict(norm_w=norm_w, w_in=w_in, conv_w=conv_w, a_log=a_log, dt_bias=dt_bias, head_norm_w=head_norm_w,
                   sgu_ln_w=sgu_ln_w, sgu_ln_b=sgu_ln_b, w_spatial=w_spatial, b_spatial=b_spatial, w_out=w_out,
                   final_norm_w=final_norm_w)
    mom_m = dict(norm_w=m_norm_w, w_in=m_w_in, conv_w=m_conv_w, a_log=m_a_log, dt_bias=m_dt_bias,
                 head_norm_w=m_head_norm_w, sgu_ln_w=m_sgu_ln_w, sgu_ln_b=m_sgu_ln_b, w_spatial=m_w_spatial,
                 b_spatial=m_b_spatial, w_out=m_w_out, final_norm_w=m_final_norm_w)
    mom_v = dict(norm_w=v_norm_w, w_in=v_w_in, conv_w=v_conv_w, a_log=v_a_log, dt_bias=v_dt_bias,
                 head_norm_w=v_head_norm_w, sgu_ln_w=v_sgu_ln_w, sgu_ln_b=v_sgu_ln_b, w_spatial=v_w_spatial,
                 b_spatial=v_b_spatial, w_out=v_w_out, final_norm_w=v_final_norm_w)
    me = _chip_index(lax.axis_index("x"), lax.axis_index("y"))
    c_arr = lax.axis_index("c").astype(jnp.int32).reshape(1)
    Din, Cb = w_in.shape[1], w_in.shape[2]
    Rb = w_out.shape[1]
    cconv = conv_w.shape[2]

    win_b = w_in[0].astype(BF16)
    g_in, = _run_ride(_gather_ride([win_b], [True]), "gather_w_in")
    g_in = _put_own(g_in, win_b)

    loss_row, grad_x, g, qw, qo = _device_step(
        x[0], loss_target[0], norm_w, g_in, w_out[0].astype(BF16), conv_w[0], a_log, dt_bias, head_norm_w,
        sgu_ln_w, sgu_ln_b, w_spatial[0], b_spatial[0], final_norm_w, c_arr)

    gsum_in, gsum_out = _sibling_concat(_chip_sum(qw, "chip_sum_w_in"), _chip_sum(qo, "chip_sum_w_out"))
    gsum_in = _Layout(a_log.shape[1], w_spatial.shape[1], N_CHIPS, Cb).from_window(gsum_in, me, Cb)
    small_shapes = [tuple(g[n].shape) for n in SMALL]
    small = _unpack(_allreduce_small(_pack([g[n] for n in SMALL])), small_shapes)
    gsmall = dict(zip(SMALL, small))
    gsmall["conv_w"] = lax.dynamic_slice_in_dim(gsmall["conv_w"], me * cconv, cconv, axis=1)

    grads, deltas, new_m, new_v = {}, {}, {}, {}
    for n, gs in (("w_in", gsum_in), ("w_out", gsum_out)):
        d, m2, v2 = _adamw(weights[n][0], gs, mom_m[n][0], mom_v[n][0], "adamw_" + n)
        grads[n], deltas[n], new_m[n], new_v[n] = gs[None], d[None], m2[None], v2[None]
    shapes = [tuple(weights[n].shape) for n in SMALL]
    ds, ms, vs = _adamw(_pack([weights[n] for n in SMALL]), _pack([gsmall[n] for n in SMALL]),
                        _pack([mom_m[n] for n in SMALL]), _pack([mom_v[n] for n in SMALL]), "adamw_small")
    for n, gq, d, m2, v2 in zip(SMALL, [gsmall[n] for n in SMALL], _unpack(ds, shapes), _unpack(ms, shapes),
                                _unpack(vs, shapes)):
        grads[n], deltas[n], new_m[n], new_v[n] = gq.reshape(weights[n].shape), d, m2, v2

    loss = lax.psum(loss_row[0, 0], ("x", "y", "c"))
    order = ("norm_w", "w_in", "conv_w", "a_log", "dt_bias", "head_norm_w", "sgu_ln_w", "sgu_ln_b", "w_spatial",
             "b_spatial", "w_out", "final_norm_w")
    return (loss, grad_x[None], *[grads[n] for n in order], *[deltas[n] for n in order],
            *[new_m[n] for n in order], *[new_v[n] for n in order])
```

```python
import functools

import jax
import jax.numpy as jnp
from jax import lax
from jax.experimental import pallas as pl
from jax.experimental.pallas import tpu as pltpu

F32 = jnp.float32
BF16 = jnp.bfloat16
EPS = 1e-6
HEAD_DIM = 128
CHUNK_A = 64
CHUNK_B = 128
CONV_WIDTH = 4
LANES = 128
HALO = 8
N_CHIPS = 4
ADAM_LR = 0.001
ADAM_B1 = 0.9
ADAM_B2 = 0.999
ADAM_EPS = 1e-08
ADAM_WD = 0.01
ADAM_STEP = 10
VMEM_LIMIT = 56 * 1024 * 1024
MESH_ID = pl.DeviceIdType.MESH
HI = lax.Precision.HIGHEST


def _cparams(sem=None, **kw):
    return pltpu.CompilerParams(dimension_semantics=sem, vmem_limit_bytes=VMEM_LIMIT, **kw)


def _matmul(a, b, ca, cb, precision):
    nb = a.ndim - 2
    batch = tuple(range(nb))
    return lax.dot_general(a, b, (((ca + nb,), (cb + nb,)), (batch, batch)), precision=precision,
                           preferred_element_type=F32)


def _dot(a, b, hi=False, precision=None):
    return _matmul(a, b, 1, 0, HI if hi else precision)


def _dot_nt(a, b, hi=False, precision=None):
    return _matmul(a, b, 1, 1, HI if hi else precision)


def _dot_tn(a, b, hi=False, precision=None):
    return _matmul(a, b, 0, 0, HI if hi else precision)


def _iota(shape, dim):
    return lax.broadcasted_iota(jnp.int32, shape, dim)


def _sigmoid(x):
    return 0.5 * (jnp.tanh(0.5 * x) + 1.0)


def _silu(x):
    return x * _sigmoid(x)


def _softplus(x):
    z = jnp.exp(-jnp.abs(x))
    small = z * (1.0 - z * (0.5 - z * (1.0 / 3.0)))
    return jnp.maximum(x, 0.0) + jnp.where(z < 1e-3, small, jnp.log(1.0 + z))


def _pick(n, pref):
    for t in pref:
        if n % t == 0:
            return t
    return n


class _Ride:
    def __init__(self, operands, out_shape, n_sems, start, finish):
        self.operands, self.out_shape, self.n_sems = list(operands), list(out_shape), n_sems
        self.start, self.finish = start, finish


def _pallas(body, operands, *, name, grid, in_specs, out_specs, out_shape, semantics, scratch_shapes=(),
            prefetch=0, ride=None):
    single = not isinstance(out_shape, (list, tuple))
    outs = [out_shape] if single else list(out_shape)
    ospecs = [out_specs] if single else list(out_specs)
    in_specs, scratch = list(in_specs), list(scratch_shapes)
    n_in, n_out, n_sc = len(operands) - prefetch, len(outs), len(scratch)
    kernel = body
    params = _cparams(semantics)
    if ride is not None:
        n_xin, n_xout = len(ride.operands), len(ride.out_shape)

        def kernel(*refs):
            pre, refs = refs[:prefetch], refs[prefetch:]
            ins, refs = refs[:n_in], refs[n_in:]
            xins, refs = refs[:n_xin], refs[n_xin:]
            mains, refs = refs[:n_out], refs[n_out:]
            xouts, refs = refs[:n_xout], refs[n_xout:]
            sc, (send, recv) = refs[:n_sc], refs[n_sc:]
            ids = [pl.program_id(a) for a in range(len(grid))]
            first = functools.reduce(jnp.logical_and, [i == 0 for i in ids])
            last = functools.reduce(jnp.logical_and, [i == g - 1 for i, g in zip(ids, grid)])

            @pl.when(first)
            def _():
                ride.start(xins, xouts, send, recv)

            body(*pre, *ins, *mains, *sc)

            @pl.when(last)
            def _():
                ride.finish(xins, xouts, send, recv)

        operands = list(operands) + ride.operands
        in_specs += [ANY] * n_xin
        ospecs += [ANY] * n_xout
        outs += ride.out_shape
        scratch += [pltpu.SemaphoreType.DMA((ride.n_sems,)), pltpu.SemaphoreType.DMA((ride.n_sems,))]
        params = _cparams(("arbitrary",) * len(grid), has_side_effects=True)
    if prefetch:
        spec = dict(grid_spec=pltpu.PrefetchScalarGridSpec(
            num_scalar_prefetch=prefetch, grid=grid, in_specs=in_specs, out_specs=ospecs, scratch_shapes=scratch))
    else:
        spec = dict(grid=grid, in_specs=in_specs, out_specs=ospecs, scratch_shapes=scratch)
    res = pl.pallas_call(kernel, name=name, out_shape=outs, compiler_params=params, **spec)(*operands)
    main = res[0] if single else list(res[:n_out])
    return main if ride is None else (main, list(res[n_out:]))


def _mm_nn(a, b, out_dtype, name, tm=1024, tn=512, tk=None, cols=None, ride=None):
    M, K = a.shape
    c0, N = (0, b.shape[1]) if cols is None else cols
    tm = _pick(M, (tm, 512, 256, 128))
    tn = _pick(N, (tn, 512, 384, 256, 128))
    tk = K if tk is None else _pick(K, (tk,))
    nk = K // tk
    j0 = c0 // tn
    assert c0 % tn == 0

    def body(a_ref, b_ref, o_ref, *scratch):
        part = _dot(a_ref[...], b_ref[...])
        if nk == 1:
            o_ref[...] = part.astype(out_dtype)
        else:
            acc_ref, = scratch
            k = pl.program_id(2)

            @pl.when(k == 0)
            def _():
                acc_ref[...] = part

            @pl.when(k > 0)
            def _():
                acc_ref[...] += part

            @pl.when(k == nk - 1)
            def _():
                o_ref[...] = acc_ref[...].astype(out_dtype)

    return _pallas(
        body, (a, b), name=name, grid=(M // tm, N // tn, nk),
        in_specs=[pl.BlockSpec((tm, tk), lambda i, j, k: (i, k)),
                  pl.BlockSpec((tk, tn), lambda i, j, k: (k, j + j0))],
        out_specs=pl.BlockSpec((tm, tn), lambda i, j, k: (i, j)),
        out_shape=jax.ShapeDtypeStruct((M, N), out_dtype),
        scratch_shapes=[] if nk == 1 else [pltpu.VMEM((tm, tn), F32)],
        semantics=("parallel", "parallel", "arbitrary"), ride=ride)


def _mm_nt_rhs_outer(a, b, out_dtype, name, tm=256, tn=1024, ride=None):
    M, K = a.shape
    N, _ = b.shape
    tm = _pick(M, (tm, 128))
    tn = _pick(N, (tn, 512, 256, 128))

    def body(a_ref, b_ref, o_ref):
        o_ref[...] = _dot_nt(a_ref[...], b_ref[...]).astype(out_dtype)

    return _pallas(
        body, (a, b), name=name, grid=(N // tn, M // tm),
        in_specs=[pl.BlockSpec((tm, K), lambda j, i: (i, 0)),
                  pl.BlockSpec((tn, K), lambda j, i: (j, 0))],
        out_specs=pl.BlockSpec((tm, tn), lambda j, i: (i, j)),
        out_shape=jax.ShapeDtypeStruct((M, N), out_dtype),
        semantics=("parallel", "parallel"), ride=ride)


WIN_BLOCK = 256


def _mm_windows(a, b, table, nb, name, tm=1024):
    M, K = a.shape
    wb = table.shape[0] // nb
    tm = _pick(M, (tm, 512, 256, 128))

    def body(tab_ref, a_ref, b_ref, o_ref):
        o_ref[0] = _dot(a_ref[...], b_ref[...])

    return pl.pallas_call(
        body, name=name,
        grid_spec=pltpu.PrefetchScalarGridSpec(
            num_scalar_prefetch=1, grid=(nb, M // tm, wb),
            in_specs=[pl.BlockSpec((tm, K), lambda n, i, t, tab: (i, 0)),
                      pl.BlockSpec((K, WIN_BLOCK), lambda n, i, t, tab: (0, tab[n * wb + t]))],
            out_specs=pl.BlockSpec((1, tm, WIN_BLOCK), lambda n, i, t, tab: (n, i, t))),
        out_shape=jax.ShapeDtypeStruct((nb, M, wb * WIN_BLOCK), F32),
        compiler_params=_cparams(("parallel", "parallel", "arbitrary")),
    )(table, a, b)


def _mm_tn_pair(a0, a1, b, name, tm=512, tn=512, tk=1024, ride=None):
    K, M = a0.shape
    _, N = b.shape
    tm = _pick(M, (tm, 256, 128))
    tn = _pick(N, (tn, 384, 256, 128))
    tk = _pick(K, (tk, 512, 256))
    nk, ni = K // tk, M // tm

    def body(a0_ref, a1_ref, b_ref, o_ref):
        p, k = pl.program_id(0), pl.program_id(3)

        def acc(a_ref):
            part = _dot_tn(a_ref[...], b_ref[...])

            @pl.when(k == 0)
            def _():
                o_ref[...] = part

            @pl.when(k > 0)
            def _():
                o_ref[...] += part

        pl.when(p == 0)(lambda: acc(a0_ref))
        pl.when(p == 1)(lambda: acc(a1_ref))

    return _pallas(
        body, (a0, a1, b), name=name, grid=(2, ni, N // tn, nk),
        in_specs=[pl.BlockSpec((tk, tm), lambda p, i, j, k: (k * (1 - p), i * (1 - p))),
                  pl.BlockSpec((tk, tm), lambda p, i, j, k: (k * p, i * p)),
                  pl.BlockSpec((tk, tn), lambda p, i, j, k: (k, j))],
        out_specs=pl.BlockSpec((tm, tn), lambda p, i, j, k: (p * ni + i, j)),
        out_shape=jax.ShapeDtypeStruct((2 * M, N), F32),
        semantics=("parallel", "parallel", "parallel", "arbitrary"), ride=ride)


def _rms_fn(x, w):
    r = lax.rsqrt(jnp.mean(x * x, axis=-1, keepdims=True) + EPS)
    return x * r * w


def _rms_in(x, w):
    T, D = x.shape
    tm = _pick(T, (512, 256, 128))

    def body(x_ref, w_ref, o_ref, ot_ref):
        xn = _rms_fn(x_ref[...], w_ref[...])
        o_ref[...] = xn.astype(BF16)
        ot_ref[...] = xn.T.astype(BF16)

    return pl.pallas_call(
        body, name="rms_in", grid=(T // tm,),
        in_specs=[pl.BlockSpec((tm, D), lambda i: (i, 0)), pl.BlockSpec((1, D), lambda i: (0, 0))],
        out_specs=[pl.BlockSpec((tm, D), lambda i: (i, 0)), pl.BlockSpec((D, tm), lambda i: (0, i))],
        out_shape=[jax.ShapeDtypeStruct((T, D), BF16), jax.ShapeDtypeStruct((D, T), BF16)],
        compiler_params=_cparams(("parallel",)),
    )(x, w)


def _rms_in_bwd(x, w, dxn, dh, ride=None):
    T, D = x.shape
    tm = _pick(T, (256, 128))

    def body(x_ref, w_ref, dxn_ref, dh_ref, gx_ref, dw_ref):
        _, vjp = jax.vjp(_rms_fn, x_ref[...], w_ref[...])
        dx, dw = vjp(dxn_ref[...])
        gx_ref[...] = dh_ref[...] + dx

        @pl.when(pl.program_id(0) == 0)
        def _():
            dw_ref[...] = dw

        @pl.when(pl.program_id(0) > 0)
        def _():
            dw_ref[...] += dw

    tile = pl.BlockSpec((tm, D), lambda i: (i, 0))
    row = pl.BlockSpec((1, D), lambda i: (0, 0))
    return _pallas(
        body, (x, w, dxn, dh), name="rms_in_bwd", grid=(T // tm,),
        in_specs=[tile, row, tile, tile], out_specs=[tile, row],
        out_shape=[jax.ShapeDtypeStruct((T, D), F32), jax.ShapeDtypeStruct((1, D), F32)],
        semantics=("arbitrary",), ride=ride)


def _conv_fwd(cat_ref, halo, x, w):
    tm = x.shape[0]
    cat_ref[0:HALO, :] = halo
    cat_ref[HALO:HALO + tm, :] = x
    c = x * w[CONV_WIDTH - 1:CONV_WIDTH, :]
    for k in range(CONV_WIDTH - 1):
        s = CONV_WIDTH - 1 - k
        c = c + cat_ref[pl.ds(HALO - s, tm), :] * w[k:k + 1, :]
    return c


def _lane_to_all(x, lane):
    @jax.custom_vjp
    def f(x):
        return jnp.broadcast_to(x[:, lane:lane + 1], x.shape)

    def f_fwd(x):
        return f(x), None

    def f_bwd(_, g):
        return (jnp.where(_iota(g.shape, 1) == lane, jnp.sum(g, axis=-1, keepdims=True), 0.0),)

    f.defvjp(f_fwd, f_bwd)
    return f(x)


def _gdn_pointwise(c, ba, alog, dtb, H):
    A = H * HEAD_DIM
    s = _silu(c)
    beta = _sigmoid(ba)
    g = -jnp.exp(alog) * _softplus(ba + dtb)
    qs, ks, vs, gbs, bbs = [], [], [], [], []
    for h in range(H):
        lo = h * HEAD_DIM
        q = s[:, lo:lo + HEAD_DIM]
        k = s[:, A + lo:A + lo + HEAD_DIM]
        qs.append(q * lax.rsqrt(jnp.sum(q * q, axis=-1, keepdims=True) + EPS))
        ks.append(k * lax.rsqrt(jnp.sum(k * k, axis=-1, keepdims=True) + EPS))
        vs.append(s[:, 2 * A + lo:2 * A + lo + HEAD_DIM])
        bbs.append(_lane_to_all(beta, h))
        gbs.append(_lane_to_all(g, H + h))
    st = lambda xs: jnp.stack(xs, axis=0)
    return st(qs), st(ks), st(vs), st(gbs), st(bbs)


def _halo_prev(tm):
    return lambda i: (jnp.maximum(i * (tm // HALO) - 1, 0), 0)


def _gdn_pre(proj_m, proj_ba, conv_w, alog_row, dtb_row, H):
    T = proj_m.shape[0]
    A = H * HEAD_DIM
    tm = _pick(T, (256, 128))
    hs = pl.BlockSpec((H, tm, HEAD_DIM), lambda i: (0, i, 0))
    hshape = jax.ShapeDtypeStruct((H, T, HEAD_DIM), F32)

    def body(x_ref, halo_ref, ba_ref, w_ref, al_ref, dt_ref, q_ref, k_ref, v_ref, gb_ref, bb_ref, cat_ref):
        halo = jnp.where(pl.program_id(0) == 0, 0.0, halo_ref[...])
        c = _conv_fwd(cat_ref, halo, x_ref[...], w_ref[...])
        q, k, v, gb, bb = _gdn_pointwise(c, ba_ref[...], al_ref[...], dt_ref[...], H)
        q_ref[...] = q
        k_ref[...] = k
        v_ref[...] = v
        gb_ref[...] = gb
        bb_ref[...] = bb

    return pl.pallas_call(
        body, name="gdn_pre", grid=(T // tm,),
        in_specs=[pl.BlockSpec((tm, 3 * A), lambda i: (i, 0)),
                  pl.BlockSpec((HALO, 3 * A), _halo_prev(tm)),
                  pl.BlockSpec((tm, LANES), lambda i: (i, 0)),
                  pl.BlockSpec((CONV_WIDTH, 3 * A), lambda i: (0, 0)),
                  pl.BlockSpec((1, LANES), lambda i: (0, 0)),
                  pl.BlockSpec((1, LANES), lambda i: (0, 0))],
        out_specs=[hs] * 5, out_shape=[hshape] * 5,
        scratch_shapes=[pltpu.VMEM((HALO + tm, 3 * A), F32)],
        compiler_params=_cparams(("parallel",)),
    )(proj_m, proj_m, proj_ba, conv_w, alog_row, dtb_row)


def _gdn_pre_bwd(proj_m, proj_ba, conv_w, alog_row, dtb_row, dq, dk, dv, dgb, dbb, H, dproj):
    T, n_main = proj_m.shape
    A = H * HEAD_DIM
    tm = _pick(T, (256, 128))
    hs = pl.BlockSpec((H, tm, HEAD_DIM), lambda i: (0, i, 0))
    row = pl.BlockSpec((1, LANES), lambda i: (0, 0))

    def body(x_ref, halo_ref, ba_ref, w_ref, al_ref, dt_ref, dq_ref, dk_ref, dv_ref, dgb_ref, dbb_ref, _,
             dc_ref, dba_ref, dal_ref, ddt_ref, cat_ref):
        halo = jnp.where(pl.program_id(0) == 0, 0.0, halo_ref[...])
        c = _conv_fwd(cat_ref, halo, x_ref[...], w_ref[...])
        _, vjp = jax.vjp(functools.partial(_gdn_pointwise, H=H), c, ba_ref[...], al_ref[...], dt_ref[...])
        dc, dba, dal, ddt = vjp((dq_ref[...], dk_ref[...], dv_ref[...], dgb_ref[...], dbb_ref[...]))
        dc_ref[...] = dc
        dba_ref[:, :LANES] = dba.astype(BF16)
        dba_ref[:, LANES:] = jnp.zeros((tm, WIN_BLOCK - LANES), BF16)

        @pl.when(pl.program_id(0) == 0)
        def _():
            dal_ref[...] = dal
            ddt_ref[...] = ddt

        @pl.when(pl.program_id(0) > 0)
        def _():
            dal_ref[...] += dal
            ddt_ref[...] += ddt

    return pl.pallas_call(
        body, name="gdn_pre_bwd", grid=(T // tm,),
        in_specs=[pl.BlockSpec((tm, 3 * A), lambda i: (i, 0)),
                  pl.BlockSpec((HALO, 3 * A), _halo_prev(tm)),
                  pl.BlockSpec((tm, LANES), lambda i: (i, 0)),
                  pl.BlockSpec((CONV_WIDTH, 3 * A), lambda i: (0, 0)),
                  row, row, hs, hs, hs, hs, hs, ANY],
        out_specs=[pl.BlockSpec((tm, 3 * A), lambda i: (i, 0)),
                   pl.BlockSpec((tm, WIN_BLOCK), lambda i: (i, n_main // WIN_BLOCK)), row, row],
        out_shape=[jax.ShapeDtypeStruct((T, 3 * A), F32), jax.ShapeDtypeStruct(dproj.shape, dproj.dtype),
                   jax.ShapeDtypeStruct((1, LANES), F32), jax.ShapeDtypeStruct((1, LANES), F32)],
        input_output_aliases={11: 1},
        scratch_shapes=[pltpu.VMEM((HALO + tm, 3 * A), F32)],
        compiler_params=_cparams(("arbitrary",)),
    )(proj_m, proj_m, proj_ba, conv_w, alog_row, dtb_row, dq, dk, dv, dgb, dbb, dproj)


def _conv_bwd(proj_m, dc, conv_w, H, dproj):
    T = proj_m.shape[0]
    A = H * HEAD_DIM
    tm = _pick(T, (256, 128))
    nt = T // tm

    def body(x_ref, halo_ref, dc_ref, nxt_ref, w_ref, _, dx_ref, dw_ref):
        i = pl.program_id(0)
        halo = jnp.where(i == 0, 0.0, halo_ref[...])
        xcat = jnp.concatenate([halo, x_ref[...]], axis=0)
        nxt = jnp.where(i == nt - 1, 0.0, nxt_ref[...])
        dc = dc_ref[...]
        dcat = jnp.concatenate([dc, nxt], axis=0)
        w = w_ref[...]
        dx = None
        rows = []
        for k in range(CONV_WIDTH):
            s = CONV_WIDTH - 1 - k
            ds = dcat if s == 0 else pltpu.roll(dcat, tm + HALO - s, 0)
            term = ds[:tm, :] * w[k:k + 1, :]
            dx = term if dx is None else dx + term
            xs = xcat if s == 0 else pltpu.roll(xcat, s, 0)
            rows.append(jnp.sum(dc * xs[HALO:, :], axis=0, keepdims=True))
        dx_ref[...] = dx.astype(BF16)
        dw = jnp.concatenate(rows + [jnp.zeros((HALO - CONV_WIDTH, 3 * A), F32)], axis=0)

        @pl.when(i == 0)
        def _():
            dw_ref[...] = dw

        @pl.when(i > 0)
        def _():
            dw_ref[...] += dw

    return pl.pallas_call(
        body, name="conv_bwd", grid=(nt,),
        in_specs=[pl.BlockSpec((tm, 3 * A), lambda i: (i, 0)),
                  pl.BlockSpec((HALO, 3 * A), _halo_prev(tm)),
                  pl.BlockSpec((tm, 3 * A), lambda i: (i, 0)),
                  pl.BlockSpec((HALO, 3 * A), lambda i: (jnp.minimum((i + 1) * (tm // HALO), T // HALO - 1), 0)),
                  pl.BlockSpec((CONV_WIDTH, 3 * A), lambda i: (0, 0)), ANY],
        out_specs=[pl.BlockSpec((tm, 3 * A), lambda i: (i, 0)),
                   pl.BlockSpec((HALO, 3 * A), lambda i: (0, 0))],
        out_shape=[jax.ShapeDtypeStruct(dproj.shape, dproj.dtype), jax.ShapeDtypeStruct((HALO, 3 * A), F32)],
        input_output_aliases={5: 0},
        compiler_params=_cparams(("arbitrary",)),
    )(proj_m, proj_m, dc, dc, conv_w, dproj)


PAIR = 2 * CHUNK_A


def _b(x):
    return x.astype(BF16)


@jax.custom_vjp
def _bdot(a, b):
    return _dot(_b(a), _b(b))


def _bdot_f(a, b):
    return _bdot(a, b), (a, b)


def _bdot_b(res, g):
    a, b = res
    return _dot_nt(_b(g), _b(b)), _dot_tn(_b(a), _b(g))


_bdot.defvjp(_bdot_f, _bdot_b)


@jax.custom_vjp
def _bdot_nt(a, b):
    return _dot_nt(_b(a), _b(b))


def _bdot_nt_f(a, b):
    return _bdot_nt(a, b), (a, b)


def _bdot_nt_b(res, g):
    a, b = res
    return _dot(_b(g), _b(b)), _dot_tn(_b(g), _b(a))


_bdot_nt.defvjp(_bdot_nt_f, _bdot_nt_b)


@jax.custom_vjp
def _bdot_tn(a, b):
    return _dot_tn(_b(a), _b(b))


def _bdot_tn_f(a, b):
    return _bdot_tn(a, b), (a, b)


def _bdot_tn_b(res, g):
    a, b = res
    return _dot_nt(_b(b), _b(g)), _dot(_b(a), _b(g))


_bdot_tn.defvjp(_bdot_tn_f, _bdot_tn_b)


def _mask_matmul(m, x):
    hi = _b(x)
    r = x - hi.astype(F32)
    mid = _b(r)
    lo = _b(r - mid.astype(F32))
    return (_dot(m, lo) + _dot(m, mid)) + _dot(m, hi)


@jax.custom_vjp
def _mask_dot(m, mt, x):
    return _mask_matmul(m, x)


def _mask_dot_f(m, mt, x):
    return _mask_matmul(m, x), (m, mt)


def _mask_dot_b(res, g):
    m, mt = res
    return jnp.zeros_like(m), jnp.zeros_like(mt), _mask_matmul(mt, g)


_mask_dot.defvjp(_mask_dot_f, _mask_dot_b)

HIGH = lax.Precision.HIGH


def _unit_lower_inverse(L):
    n = L.shape[-1]
    X = -L
    P = (_iota((n, n), 0) == _iota((n, n), 1)).astype(F32) + X
    for _ in range(CHUNK_A.bit_length() - 2):
        X = _dot(X, X, precision=HIGH)
        P = P + _dot(P, X, precision=HIGH)
    return P


@jax.custom_vjp
def _known_inverse(L, P):
    return P


def _known_inverse_f(L, P):
    return P, P


def _known_inverse_b(P, g):
    t = _dot_tn(P, g, precision=HIGH)
    return -_dot_nt(t, P, precision=HIGH), jnp.zeros_like(P)


_known_inverse.defvjp(_known_inverse_f, _known_inverse_b)


def _gdn_prep_fn(q, k, v, gb, bb, P_known=None):
    n = PAIR
    row, col = _iota((n, n), 0), _iota((n, n), 1)
    same = (row >= CHUNK_A) == (col >= CHUNK_A)
    incl = same & (row >= col)
    strict = same & (row > col)
    bc = lambda m: jnp.broadcast_to(_b(m.astype(F32)), q.shape[:1] + (n, n))
    tril, triu, ones = bc(incl), bc(same & (row <= col)), bc(same)
    gc = _mask_dot(tril, triu, gb)
    gl = _mask_dot(ones, ones, gb)
    decay = jnp.where(incl, jnp.exp(jnp.where(incl, gc - jnp.swapaxes(gc, 1, 2), 0.0)), 0.0)
    kb = k * bb
    vb = v * bb
    qs = q * (HEAD_DIM ** -0.5)
    L = jnp.where(strict, _bdot_nt(kb, k) * decay, 0.0)
    P = _unit_lower_inverse(L) if P_known is None else _known_inverse(L, P_known)
    egc = jnp.exp(gc)
    u = _bdot(P, vb)
    w = _bdot(P, kb * egc)
    attn = jnp.where(incl, _bdot_nt(qs, k) * decay, 0.0)
    qg = qs * egc
    kdec = k * jnp.exp(gl - gc)
    eg = jnp.exp(gl)
    if P_known is None:
        return u, w, qg, kdec, attn, eg, P
    return u, w, qg, kdec, attn, eg


def _gdn_chain_fn(S, qg, kdec, u, w, attn, eg):
    C = CHUNK_A
    a, b = (slice(None), slice(0, C)), (slice(None), slice(C, PAIR))
    cat = lambda xs: jnp.concatenate(xs, axis=1)
    vn_a = u[a] - _bdot(w[a], S)
    o_a = _bdot(qg[a], S) + _bdot(attn[a], cat([vn_a, jnp.zeros_like(vn_a)]))
    S1 = S * cat([eg[a], eg[a]]) + _bdot_tn(kdec[a], vn_a)
    vn_b = u[b] - _bdot(w[b], S1)
    o_b = _bdot(qg[b], S1) + _bdot(attn[b], cat([vn_a, vn_b]))
    S2 = S1 * cat([eg[b], eg[b]]) + _bdot_tn(kdec[b], vn_b)
    return cat([o_a, o_b]), S2


def _gdn_prep(q, k, v, gb, bb):
    H, T, _ = q.shape
    pb = _pick(T // PAIR, (8, 4, 2, 1))
    hs = pl.BlockSpec((1, PAIR * pb, HEAD_DIM), lambda h, n: (h, n, 0))
    hshape = jax.ShapeDtypeStruct((H, T, HEAD_DIM), F32)

    def body(q_ref, k_ref, v_ref, gb_ref, bb_ref, *out_refs):
        pairs = lambda ref: ref[0].reshape(pb, PAIR, HEAD_DIM)
        outs = _gdn_prep_fn(pairs(q_ref), pairs(k_ref), pairs(v_ref), pairs(gb_ref), pairs(bb_ref))
        for ref, val in zip(out_refs, outs):
            ref[0] = val.reshape(pb * PAIR, HEAD_DIM)

    return pl.pallas_call(
        body, name="gdn_prep", grid=(H, T // (PAIR * pb)),
        in_specs=[hs] * 5, out_specs=[hs] * 7, out_shape=[hshape] * 7,
        compiler_params=_cparams(("parallel", "parallel")),
    )(q, k, v, gb, bb)


def _gdn_prep_bwd(q, k, v, gb, bb, pinv, du, dw, dqg, dkd, dat, deg):
    H, T, _ = q.shape
    pb = _pick(T // PAIR, (8, 4, 2, 1))
    hs = pl.BlockSpec((1, PAIR * pb, HEAD_DIM), lambda h, n: (h, n, 0))
    hshape = jax.ShapeDtypeStruct((H, T, HEAD_DIM), F32)

    def body(*refs):
        in_refs, p_ref, ct_refs, out_refs = refs[:5], refs[5], refs[6:12], refs[12:]
        pairs = lambda ref: ref[0].reshape(pb, PAIR, HEAD_DIM)
        P = pairs(p_ref)
        _, vjp = jax.vjp(lambda *a: _gdn_prep_fn(*a, P_known=P), *[pairs(r) for r in in_refs])
        grads = vjp(tuple(pairs(r) for r in ct_refs))
        for ref, val in zip(out_refs, grads):
            ref[0] = val.reshape(pb * PAIR, HEAD_DIM)

    return pl.pallas_call(
        body, name="gdn_prep_bwd", grid=(H, T // (PAIR * pb)),
        in_specs=[hs] * 12, out_specs=[hs] * 5, out_shape=[hshape] * 5,
        compiler_params=_cparams(("parallel", "parallel")),
    )(q, k, v, gb, bb, pinv, du, dw, dqg, dkd, dat, deg)


def _gdn_chain(qg, kd, u, w, attn, eg):
    H, T, _ = qg.shape
    N = T // PAIR
    hs = pl.BlockSpec((H, PAIR, HEAD_DIM), lambda n: (0, n, 0))
    ss = pl.BlockSpec((1, H, HEAD_DIM, HEAD_DIM), lambda n: (n, 0, 0, 0))

    def body(qg_ref, kd_ref, u_ref, w_ref, at_ref, eg_ref, o_ref, sall_ref, s_ref):
        @pl.when(pl.program_id(0) == 0)
        def _():
            s_ref[...] = jnp.zeros_like(s_ref)

        S = s_ref[...]
        sall_ref[0] = S
        o, S2 = _gdn_chain_fn(S, qg_ref[...], kd_ref[...], u_ref[...], w_ref[...], at_ref[...], eg_ref[...])
        o_ref[...] = o
        s_ref[...] = S2

    return pl.pallas_call(
        body, name="gdn_chain", grid=(N,),
        in_specs=[hs] * 6, out_specs=[hs, ss],
        out_shape=[jax.ShapeDtypeStruct((H, T, HEAD_DIM), F32),
                   jax.ShapeDtypeStruct((N, H, HEAD_DIM, HEAD_DIM), F32)],
        scratch_shapes=[pltpu.VMEM((H, HEAD_DIM, HEAD_DIM), F32)],
        compiler_params=_cparams(("arbitrary",)),
    )(qg, kd, u, w, attn, eg)


def _gdn_chain_bwd(qg, kd, u, w, attn, eg, sall, do):
    H, T, _ = qg.shape
    N = T // PAIR
    hs = pl.BlockSpec((H, PAIR, HEAD_DIM), lambda n: (0, N - 1 - n, 0))
    ss = pl.BlockSpec((1, H, HEAD_DIM, HEAD_DIM), lambda n: (N - 1 - n, 0, 0, 0))
    hshape = jax.ShapeDtypeStruct((H, T, HEAD_DIM), F32)

    def body(qg_ref, kd_ref, u_ref, w_ref, at_ref, eg_ref, sall_ref, do_ref, *rest):
        out_refs, ds_ref = rest[:6], rest[6]

        @pl.when(pl.program_id(0) == 0)
        def _():
            ds_ref[...] = jnp.zeros_like(ds_ref)

        _, vjp = jax.vjp(_gdn_chain_fn, sall_ref[0], qg_ref[...], kd_ref[...], u_ref[...], w_ref[...],
                         at_ref[...], eg_ref[...])
        grads = vjp((do_ref[...], ds_ref[...]))
        ds_ref[...] = grads[0]
        for ref, val in zip(out_refs, grads[1:]):
            ref[...] = val

    return pl.pallas_call(
        body, name="gdn_chain_bwd", grid=(N,),
        in_specs=[hs] * 6 + [ss, hs], out_specs=[hs] * 6, out_shape=[hshape] * 6,
        scratch_shapes=[pltpu.VMEM((H, HEAD_DIM, HEAD_DIM), F32)],
        compiler_params=_cparams(("arbitrary",)),
    )(qg, kd, u, w, attn, eg, sall, do)


def _post_fn(ogs, za, hw):
    outs = []
    for h, o in enumerate(ogs):
        r = lax.rsqrt(jnp.mean(o * o, axis=-1, keepdims=True) + EPS)
        outs.append(o * r * hw * _silu(za[:, h * HEAD_DIM:(h + 1) * HEAD_DIM]))
    return jnp.concatenate(outs, axis=1)


def _gdn_post(og, proj_m, hw):
    H, T, _ = og.shape
    A = H * HEAD_DIM
    tm = _pick(T, (512, 256, 128))

    def body(og_ref, za_ref, hw_ref, o_ref):
        o_ref[...] = _post_fn(tuple(og_ref[h] for h in range(H)), za_ref[...], hw_ref[...]).astype(BF16)

    return pl.pallas_call(
        body, name="gdn_post", grid=(T // tm,),
        in_specs=[pl.BlockSpec((H, tm, HEAD_DIM), lambda i: (0, i, 0)),
                  pl.BlockSpec((tm, A), lambda i: (i, ZA_BLOCK)),
                  pl.BlockSpec((1, HEAD_DIM), lambda i: (0, 0))],
        out_specs=pl.BlockSpec((tm, A), lambda i: (i, 0)),
        out_shape=jax.ShapeDtypeStruct((T, A), BF16),
        compiler_params=_cparams(("parallel",)),
    )(og, proj_m, hw)


def _gdn_post_bwd(og, proj_m, hw, d_o, dproj):
    H, T, _ = og.shape
    A = H * HEAD_DIM
    tm = _pick(T, (256, 128))

    def body(og_ref, za_ref, hw_ref, do_ref, _, dog_ref, dza_ref, dhw_ref):
        _, vjp = jax.vjp(_post_fn, tuple(og_ref[h] for h in range(H)), za_ref[...], hw_ref[...])
        dog, dza, dhw = vjp(do_ref[...])
        for h in range(H):
            dog_ref[h] = dog[h]
        dza_ref[...] = dza.astype(BF16)

        @pl.when(pl.program_id(0) == 0)
        def _():
            dhw_ref[...] = dhw

        @pl.when(pl.program_id(0) > 0)
        def _():
            dhw_ref[...] += dhw

    return pl.pallas_call(
        body, name="gdn_post_bwd", grid=(T // tm,),
        in_specs=[pl.BlockSpec((H, tm, HEAD_DIM), lambda i: (0, i, 0)),
                  pl.BlockSpec((tm, A), lambda i: (i, ZA_BLOCK)),
                  pl.BlockSpec((1, HEAD_DIM), lambda i: (0, 0)),
                  pl.BlockSpec((tm, A), lambda i: (i, 0)), ANY],
        out_specs=[pl.BlockSpec((H, tm, HEAD_DIM), lambda i: (0, i, 0)),
                   pl.BlockSpec((tm, A), lambda i: (i, ZA_BLOCK)),
                   pl.BlockSpec((1, HEAD_DIM), lambda i: (0, 0))],
        out_shape=[jax.ShapeDtypeStruct((H, T, HEAD_DIM), F32), jax.ShapeDtypeStruct(dproj.shape, dproj.dtype),
                   jax.ShapeDtypeStruct((1, HEAD_DIM), F32)],
        input_output_aliases={4: 1},
        compiler_params=_cparams(("arbitrary",)),
    )(og, proj_m, hw, d_o, dproj)


def _sgu_fn(ub, vb, zb, lw, lb, W, bbc):
    G = len(W)
    tm = ub.shape[0]
    mu = jnp.mean(vb, axis=-1, keepdims=True)
    xc = vb - mu
    var = jnp.mean(xc * xc, axis=-1, keepdims=True)
    vn = xc * lax.rsqrt(var + EPS) * lw + lb
    mask = _iota((CHUNK_B, CHUNK_B), 0) >= _iota((CHUNK_B, CHUNK_B), 1)
    cols = []
    for g in range(G):
        wm = jnp.where(mask, W[g], 0.0).astype(BF16)
        rows = []
        for c in range(tm // CHUNK_B):
            blk = vn[c * CHUNK_B:(c + 1) * CHUNK_B, g * HEAD_DIM:(g + 1) * HEAD_DIM].astype(BF16)
            rows.append(_dot(wm, blk) + bbc[g])
        cols.append(jnp.concatenate(rows, axis=0) if len(rows) > 1 else rows[0])
    s = jnp.concatenate(cols, axis=1)
    return ub * s * _silu(zb)


ZA_BLOCK = 6


def _sgu_cols(A, B):
    assert A == B
    return 3, 4, 5


def _sgu_fwd(proj_m, lw, lb, W, bbc, A):
    T = proj_m.shape[0]
    G = W.shape[0]
    B = G * HEAD_DIM
    tm = _pick(T, (256, 128))
    cu, cv, cz = _sgu_cols(A, B)

    def body(u_ref, v_ref, z_ref, lw_ref, lb_ref, w_ref, b_ref, o_ref):
        o_ref[...] = _sgu_fn(u_ref[...], v_ref[...], z_ref[...], lw_ref[...], lb_ref[...],
                             tuple(w_ref[g] for g in range(G)), tuple(b_ref[g] for g in range(G))).astype(BF16)

    row = pl.BlockSpec((1, B), lambda i: (0, 0))
    cube = pl.BlockSpec((G, CHUNK_B, CHUNK_B), lambda i: (0, 0, 0))
    return pl.pallas_call(
        body, name="sgu_fwd", grid=(T // tm,),
        in_specs=[pl.BlockSpec((tm, B), lambda i: (i, cu)), pl.BlockSpec((tm, B), lambda i: (i, cv)),
                  pl.BlockSpec((tm, B), lambda i: (i, cz)), row, row, cube, cube],
        out_specs=pl.BlockSpec((tm, B), lambda i: (i, 0)),
        out_shape=jax.ShapeDtypeStruct((T, B), BF16),
        compiler_params=_cparams(("parallel",)),
    )(proj_m, proj_m, proj_m, lw, lb, W, bbc)


def _sgu_bwd(proj_m, lw, lb, W, bbc, d_o, A, dproj):
    T = proj_m.shape[0]
    G = W.shape[0]
    B = G * HEAD_DIM
    tm = _pick(T, (256, 128))
    nt = T // tm
    cu, cv, cz = _sgu_cols(A, B)

    def body(u_ref, v_ref, z_ref, lw_ref, lb_ref, w_ref, b_ref, do_ref, _,
             dp_ref, dlw_ref, dlb_ref, dw_ref, db_ref, dbb_ref):
        _, vjp = jax.vjp(_sgu_fn, u_ref[...], v_ref[...], z_ref[...], lw_ref[...], lb_ref[...],
                         tuple(w_ref[g] for g in range(G)), tuple(b_ref[g] for g in range(G)))
        du, dv, dz, dlw, dlb, dW, dbb = vjp(do_ref[...])
        dW, dbb = jnp.stack(dW, axis=0), jnp.stack(dbb, axis=0)
        dp_ref[:, 0:B] = du.astype(BF16)
        dp_ref[:, B:2 * B] = dv.astype(BF16)
        dp_ref[:, 2 * B:3 * B] = dz.astype(BF16)
        i = pl.program_id(0)

        @pl.when(i == 0)
        def _():
            dlw_ref[...] = dlw
            dlb_ref[...] = dlb
            dw_ref[...] = dW
            dbb_ref[...] = dbb

        @pl.when(i > 0)
        def _():
            dlw_ref[...] += dlw
            dlb_ref[...] += dlb
            dw_ref[...] += dW
            dbb_ref[...] += dbb

        @pl.when(i == nt - 1)
        def _():
            db_ref[...] = jnp.sum(dbb_ref[...], axis=-1, keepdims=True)

    row = pl.BlockSpec((1, B), lambda i: (0, 0))
    cube = pl.BlockSpec((G, CHUNK_B, CHUNK_B), lambda i: (0, 0, 0))
    return pl.pallas_call(
        body, name="sgu_bwd", grid=(nt,),
        in_specs=[pl.BlockSpec((tm, B), lambda i: (i, cu)), pl.BlockSpec((tm, B), lambda i: (i, cv)),
                  pl.BlockSpec((tm, B), lambda i: (i, cz)), row, row, cube, cube,
                  pl.BlockSpec((tm, B), lambda i: (i, A // B)), ANY],
        out_specs=[pl.BlockSpec((tm, 3 * B), lambda i: (i, 1)), row, row, cube,
                   pl.BlockSpec((G, CHUNK_B, 1), lambda i: (0, 0, 0))],
        out_shape=[jax.ShapeDtypeStruct(dproj.shape, dproj.dtype), jax.ShapeDtypeStruct((1, B), F32),
                   jax.ShapeDtypeStruct((1, B), F32), jax.ShapeDtypeStruct((G, CHUNK_B, CHUNK_B), F32),
                   jax.ShapeDtypeStruct((G, CHUNK_B, 1), F32)],
        input_output_aliases={8: 0},
        scratch_shapes=[pltpu.VMEM((G, CHUNK_B, CHUNK_B), F32)],
        compiler_params=_cparams(("arbitrary",)),
    )(proj_m, proj_m, proj_m, lw, lb, W, bbc, d_o, dproj)


def _head_fn(mix, x, fw, tgt):
    h = x + mix
    y = _rms_fn(h, fw)
    e = y - tgt
    return 0.5 * jnp.sum(jnp.mean(e * e, axis=-1, keepdims=True), axis=0, keepdims=True)


def _out_proj_loss(oa, ob, wout, x, tgt, fw):
    T, A = oa.shape
    B = ob.shape[1]
    D = x.shape[1]
    tm = _pick(T, (256, 128))

    def body(oa_ref, ob_ref, w_ref, x_ref, t_ref, fw_ref, dh_ref, dhb_ref, loss_ref, dfw_ref):
        mix = _dot(oa_ref[...], w_ref[0:A, :]) + _dot(ob_ref[...], w_ref[A:A + B, :])
        xv, tv = x_ref[...], t_ref[...]
        loss, vjp = jax.vjp(lambda m, f: _head_fn(m, xv, f, tv), mix, fw_ref[...])
        dh, dfw = vjp(jnp.ones((1, 1), F32))
        dh_ref[...] = dh
        dhb_ref[...] = dh.astype(BF16)
        lrow = jnp.broadcast_to(loss, (1, LANES))

        @pl.when(pl.program_id(0) == 0)
        def _():
            loss_ref[...] = lrow
            dfw_ref[...] = dfw

        @pl.when(pl.program_id(0) > 0)
        def _():
            loss_ref[...] += lrow
            dfw_ref[...] += dfw

    tile = pl.BlockSpec((tm, D), lambda i: (i, 0))
    return pl.pallas_call(
        body, name="out_proj_loss", grid=(T // tm,),
        in_specs=[pl.BlockSpec((tm, A), lambda i: (i, 0)), pl.BlockSpec((tm, B), lambda i: (i, 0)),
                  pl.BlockSpec((A + B, D), lambda i: (0, 0)), tile, tile,
                  pl.BlockSpec((1, D), lambda i: (0, 0))],
        out_specs=[tile, tile, pl.BlockSpec((1, LANES), lambda i: (0, 0)),
                   pl.BlockSpec((1, D), lambda i: (0, 0))],
        out_shape=[jax.ShapeDtypeStruct((T, D), F32), jax.ShapeDtypeStruct((T, D), BF16),
                   jax.ShapeDtypeStruct((1, LANES), F32), jax.ShapeDtypeStruct((1, D), F32)],
        compiler_params=_cparams(("arbitrary",)),
    )(oa, ob, wout, x, tgt, fw)


def _adamw(w, g, m, v, name):
    R, Cn = w.shape
    cap = max(8, 512 * 1024 // Cn)
    tr = max(t for t in range(8, min(R, cap) + 1, 8) if R % t == 0) if R > cap else R

    def body(w_ref, g_ref, m_ref, v_ref, d_ref, mo_ref, vo_ref):
        g = g_ref[...]
        m = ADAM_B1 * m_ref[...] + (1.0 - ADAM_B1) * g
        v = ADAM_B2 * v_ref[...] + (1.0 - ADAM_B2) * jnp.square(g)
        m_hat = m / (1.0 - ADAM_B1 ** ADAM_STEP)
        v_hat = v / (1.0 - ADAM_B2 ** ADAM_STEP)
        d_ref[...] = -ADAM_LR * (m_hat / (jnp.sqrt(v_hat) + ADAM_EPS) + ADAM_WD * w_ref[...])
        mo_ref[...] = m
        vo_ref[...] = v

    tile = pl.BlockSpec((tr, Cn), lambda i: (i, 0))
    shape = jax.ShapeDtypeStruct((R, Cn), F32)
    return pl.pallas_call(
        body, name=name, grid=(R // tr,), in_specs=[tile] * 4, out_specs=[tile] * 3,
        out_shape=[shape] * 3, compiler_params=_cparams(("parallel",)),
    )(w, g, m, v)


def _place():
    x, y, c = lax.axis_index("x"), lax.axis_index("y"), lax.axis_index("c")
    others = [(1 - x, y), (x, 1 - y), (1 - x, 1 - y)]
    return x, y, c, others


def _chip_index(px, py):
    return 2 * px + py


ANY = pl.BlockSpec(memory_space=pl.ANY)


def _gather_ride(blocks, split):
    n = len(blocks)

    def copies(in_refs, out_refs, send_sems, recv_sems):
        x, y, c, others = _place()
        me = _chip_index(x, y)

        def copy(sem, src, dst, to):
            return pltpu.make_async_remote_copy(src_ref=src, dst_ref=dst, send_sem=send_sems.at[sem],
                                                recv_sem=recv_sems.at[sem], device_id=to, device_id_type=MESH_ID)

        def part(a, chip, core):
            if not split[a]:
                return out_refs[a].at[chip]
            h = blocks[a].shape[0] // 2
            return out_refs[a].at[chip, pl.ds(core * h, h), :]

        def mine(a):
            if not split[a]:
                return in_refs[a]
            h = blocks[a].shape[0] // 2
            return in_refs[a].at[pl.ds(c * h, h), :]

        pairs = [(a, j, chip) for j, chip in enumerate(others) for a in range(n)]
        k = lambda chip: _chip_index(*chip)
        send = lambda a, j, chip: copy(3 * a + j, mine(a), part(a, me, c), (*chip, c))
        arrival = lambda a, j, chip: copy(3 * a + j, part(a, k(chip), c), part(a, k(chip), c), (*chip, c))
        passing = lambda a, j, chip: copy(3 * (n + a) + j, part(a, k(chip), c), part(a, k(chip), c), (x, y, 1 - c))
        passed = lambda a, j, chip: copy(3 * (n + a) + j, part(a, k(chip), 1 - c), part(a, k(chip), 1 - c),
                                         (x, y, 1 - c))
        return pairs, send, arrival, passing, passed

    def start(in_refs, out_refs, send_sems, recv_sems):
        pairs, send, _, _, _ = copies(in_refs, out_refs, send_sems, recv_sems)
        for p in pairs:
            send(*p).start()

    def finish(in_refs, out_refs, send_sems, recv_sems):
        pairs, send, arrival, passing, passed = copies(in_refs, out_refs, send_sems, recv_sems)
        for p in pairs:
            arrival(*p).wait_recv()
            if split[p[0]]:
                passing(*p).start()
        for p in pairs:
            if split[p[0]]:
                passed(*p).wait_recv()
        for p in pairs:
            send(*p).wait_send()
            if split[p[0]]:
                passing(*p).wait_send()

    shapes = [jax.ShapeDtypeStruct((N_CHIPS,) + b.shape, b.dtype) for b in blocks]
    return _Ride(blocks, shapes, 6 * n, start, finish)


def _put_own(gathered, own):
    me = _chip_index(lax.axis_index("x"), lax.axis_index("y"))
    return lax.dynamic_update_index_in_dim(gathered, own, me, 0)


def _run_ride(ride, name):
    def body(*refs):
        n_in, n_out = len(ride.operands), len(ride.out_shape)
        ins, outs, (send, recv) = refs[:n_in], refs[n_in:n_in + n_out], refs[n_in + n_out:]
        ride.start(ins, outs, send, recv)
        ride.finish(ins, outs, send, recv)

    return pl.pallas_call(
        body, name=name, in_specs=[ANY] * len(ride.operands), out_specs=[ANY] * len(ride.out_shape),
        out_shape=ride.out_shape,
        scratch_shapes=[pltpu.SemaphoreType.DMA((ride.n_sems,)), pltpu.SemaphoreType.DMA((ride.n_sems,))],
        compiler_params=pltpu.CompilerParams(has_side_effects=True),
    )(*ride.operands)


def _allreduce_small(buf):
    R, L = buf.shape

    def body(in_ref, out_ref, sib_ref, pair_ref, chips_ref, send_sems, recv_sems):
        x, y, c, others = _place()
        me = _chip_index(x, y)
        sibling = (x, y, 1 - c)
        cp = pltpu.make_async_remote_copy(src_ref=in_ref, dst_ref=sib_ref, send_sem=send_sems.at[0],
                                          recv_sem=recv_sems.at[0], device_id=sibling, device_id_type=MESH_ID)
        cp.start()
        cp.wait()
        pair_ref[...] = in_ref[...] + sib_ref[...]
        sends = []
        for j, chip in enumerate(others):
            s = pltpu.make_async_remote_copy(src_ref=pair_ref, dst_ref=chips_ref.at[me],
                                             send_sem=send_sems.at[1 + j], recv_sem=recv_sems.at[1 + j],
                                             device_id=(*chip, c), device_id_type=MESH_ID)
            s.start()
            sends.append(s)
        chips_ref[me] = pair_ref[...]
        for j, chip in enumerate(others):
            k = _chip_index(*chip)
            pltpu.make_async_remote_copy(src_ref=pair_ref, dst_ref=chips_ref.at[k], send_sem=send_sems.at[1 + j],
                                         recv_sem=recv_sems.at[1 + j], device_id=(*chip, c),
                                         device_id_type=MESH_ID).wait_recv()
        for s in sends:
            s.wait_send()
        out_ref[...] = ((chips_ref[0] + chips_ref[1]) + chips_ref[2]) + chips_ref[3]

    vm = pl.BlockSpec(memory_space=pltpu.VMEM)
    return pl.pallas_call(
        body, name="allreduce_small", in_specs=[vm], out_specs=vm,
        out_shape=jax.ShapeDtypeStruct((R, L), F32),
        scratch_shapes=[pltpu.VMEM((R, L), F32), pltpu.VMEM((R, L), F32), pltpu.VMEM((N_CHIPS, R, L), F32),
                        pltpu.SemaphoreType.DMA((4,)), pltpu.SemaphoreType.DMA((4,))],
        compiler_params=pltpu.CompilerParams(vmem_limit_bytes=VMEM_LIMIT),
    )(buf)


def _pair_ride(g):
    nb, R, Cn = g.shape
    h = R // 2

    def copy(in_refs, out_refs, send_sems, recv_sems):
        x, y, c, _ = _place()
        return pltpu.make_async_remote_copy(src_ref=in_refs[0].at[:, pl.ds((1 - c) * h, h), :], dst_ref=out_refs[0],
                                            send_sem=send_sems.at[0], recv_sem=recv_sems.at[0],
                                            device_id=(x, y, 1 - c), device_id_type=MESH_ID)

    return _Ride([g], [jax.ShapeDtypeStruct((nb, h, Cn), g.dtype)], 1,
                 lambda *refs: copy(*refs).start(), lambda *refs: copy(*refs).wait())


def _pair_sum(g, land, c_arr, name, ride=None):
    nb, R, Cn = g.shape
    hr = R // 2
    tr = _pick(hr, (256, 128, 64, 32, 16))
    nt = hr // tr

    def body(c_ref, g_ref, l_ref, o_ref):
        o_ref[...] = (g_ref[...] + l_ref[...]).astype(BF16)

    return _pallas(
        body, (c_arr, g, land), name=name, prefetch=1, grid=(nb, nt),
        in_specs=[pl.BlockSpec((1, tr, Cn), lambda b, i, c_ref: (b, c_ref[0] * nt + i, 0)),
                  pl.BlockSpec((1, tr, Cn), lambda b, i, c_ref: (b, i, 0))],
        out_specs=pl.BlockSpec((1, tr, Cn), lambda b, i, c_ref: (b, i, 0)),
        out_shape=jax.ShapeDtypeStruct((nb, hr, Cn), BF16),
        semantics=("parallel", "parallel"), ride=ride)


def _chip_ride(parts):
    m = len(parts)

    def copies(in_refs, out_refs, send_sems, recv_sems):
        x, y, c, others = _place()
        me = _chip_index(x, y)
        def mk(j, chip, n, landing):
            k = _chip_index(*chip)
            return pltpu.make_async_remote_copy(
                src_ref=in_refs[n].at[k], dst_ref=out_refs[n].at[landing(k)], send_sem=send_sems.at[m * j + n],
                recv_sem=recv_sems.at[m * j + n], device_id=(*chip, c), device_id_type=MESH_ID)

        pairs = [(j, chip, n) for j, chip in enumerate(others) for n in range(m)]
        return pairs, (lambda *p: mk(*p, lambda k: me)), (lambda *p: mk(*p, lambda k: k))

    def start(*refs):
        pairs, send, _ = copies(*refs)
        for p in pairs:
            send(*p).start()

    def finish(*refs):
        pairs, send, arrival = copies(*refs)
        for p in pairs:
            arrival(*p).wait_recv()
        for p in pairs:
            send(*p).wait_send()

    return _Ride(parts, [jax.ShapeDtypeStruct(p.shape, p.dtype) for p in parts], 3 * m, start, finish)


def _put_own_slot(q, p):
    me = _chip_index(lax.axis_index("x"), lax.axis_index("y"))
    return lax.dynamic_update_index_in_dim(q, lax.dynamic_index_in_dim(p, me, 0, keepdims=False), me, 0)


def _chip_sum(q, name):
    nb, hr, Cn = q.shape
    tr = _pick(hr, (256, 128, 64, 32, 16))

    def body(q_ref, o_ref):
        f = lambda k: q_ref[k].astype(F32)
        o_ref[...] = ((f(0) + f(1)) + f(2)) + f(3)

    return pl.pallas_call(
        body, name=name, grid=(hr // tr,),
        in_specs=[pl.BlockSpec((nb, tr, Cn), lambda i: (0, i, 0))],
        out_specs=pl.BlockSpec((tr, Cn), lambda i: (i, 0)),
        out_shape=jax.ShapeDtypeStruct((hr, Cn), F32),
        compiler_params=_cparams(("parallel",)),
    )(q)


def _sibling_concat(rw, ro):
    hi, Cb = rw.shape
    ho, D = ro.shape

    def body(rw_ref, ro_ref, tw_ref, to_ref, send_sems, recv_sems):
        x, y, c, _ = _place()
        sibling = (x, y, 1 - c)
        a = pltpu.make_async_remote_copy(src_ref=rw_ref, dst_ref=tw_ref, send_sem=send_sems.at[0],
                                         recv_sem=recv_sems.at[0], device_id=sibling, device_id_type=MESH_ID)
        b = pltpu.make_async_remote_copy(src_ref=ro_ref, dst_ref=to_ref, send_sem=send_sems.at[1],
                                         recv_sem=recv_sems.at[1], device_id=sibling, device_id_type=MESH_ID)
        a.start()
        b.start()
        a.wait()
        b.wait()

    tw, to = pl.pallas_call(
        body, name="sibling_concat", in_specs=[ANY, ANY], out_specs=[ANY, ANY],
        out_shape=[jax.ShapeDtypeStruct((hi, Cb), F32), jax.ShapeDtypeStruct((ho, D), F32)],
        scratch_shapes=[pltpu.SemaphoreType.DMA((2,)), pltpu.SemaphoreType.DMA((2,))],
        compiler_params=pltpu.CompilerParams(has_side_effects=True),
    )(rw, ro)
    c = lax.axis_index("c")
    join = lambda mine, theirs: lax.dynamic_update_slice_in_dim(
        jnp.concatenate([mine, mine], axis=0), theirs, (1 - c) * mine.shape[0], axis=0)
    return join(rw, tw), join(ro, to)


class _Layout:
    def __init__(self, H, G, nb, Cb):
        A, B = H * HEAD_DIM, G * HEAD_DIM
        self.n_main = 4 * A + 3 * B
        self.k = -(-(self.n_main + LANES) // WIN_BLOCK) * WIN_BLOCK
        cuts = [0, 3 * A, 4 * A, 4 * A + 2 * H, nb * Cb]
        starts = [0, 3 * A + 3 * B, self.n_main, 3 * A]
        self.pieces = []
        self.windows, self.runs = [], []
        for n in range(nb):
            segs = []
            for s in range(4):
                lo, hi = max(cuts[s], n * Cb), min(cuts[s + 1], (n + 1) * Cb)
                if lo < hi:
                    segs.append((starts[s] + lo - cuts[s], lo - n * Cb, hi - lo))
            self.pieces += [(own, n, col, ln) for own, col, ln in segs]
            blocks = sorted({b for own, _, ln in segs for b in range(own // WIN_BLOCK, (own + ln - 1) // WIN_BLOCK + 1)})
            self.windows.append(blocks)
            self.runs.append([(blocks.index(own // WIN_BLOCK) * WIN_BLOCK + own % WIN_BLOCK, ln)
                              for own, _, ln in segs])
        self.wb = max(len(b) for b in self.windows)
        self.table = [b + [b[-1]] * (self.wb - len(b)) for b in self.windows]
        self.pieces.sort()

    def to_own_order(self, g_in):
        D = g_in.shape[1]
        cols, at = [], 0
        for own, n, col, ln in self.pieces:
            if own > at:
                cols.append(jnp.zeros((D, own - at), g_in.dtype))
            cols.append(g_in[n, :, col:col + ln])
            at = own + ln
        if at < self.k:
            cols.append(jnp.zeros((D, self.k - at), g_in.dtype))
        return jnp.concatenate(cols, axis=1)

    def from_window(self, win, chip, Cb):
        pick = lambda runs: (lambda w: jnp.concatenate([w[:, c:c + ln] for c, ln in runs], axis=1))
        return lax.switch(chip, [pick(r) for r in self.runs], win)


def _device_step(x, tgt, norm_w, g_in, wout_b, conv_b, a_log, dt_bias, head_norm_w, sgu_ln_w, sgu_ln_b,
                 w_spatial, b_spatial, final_norm_w, c_arr):
    T, D = x.shape
    H = a_log.shape[1]
    A = H * HEAD_DIM
    G = w_spatial.shape[0]
    B = G * HEAD_DIM
    nb, _, Cb = g_in.shape
    Rb = wout_b.shape[0]
    lay = _Layout(H, G, nb, Cb)
    w_own = lay.to_own_order(g_in)
    alog_row = jnp.pad(a_log, ((0, 0), (H, LANES - 2 * H)))
    dtb_row = jnp.pad(dt_bias, ((0, 0), (H, LANES - 2 * H)))
    bbc = jnp.broadcast_to(b_spatial[:, :, None], (G, CHUNK_B, CHUNK_B))

    xn, xn_t = _rms_in(x, norm_w)
    proj_m, (g_out, g_conv) = _mm_nn(xn, w_own, F32, "in_proj", cols=(0, lay.n_main),
                                     ride=_gather_ride([wout_b, conv_b], [True, False]))
    wout = _put_own(g_out, wout_b).reshape(nb * Rb, D)
    conv_w = _put_own(g_conv, conv_b).transpose(1, 0, 2).reshape(CONV_WIDTH, nb * conv_b.shape[1])
    proj_ba = _mm_nn(xn, w_own, F32, "in_proj_ba", cols=(lay.n_main, LANES))
    q, k, v, gb, bb = _gdn_pre(proj_m, proj_ba, conv_w, alog_row, dtb_row, H)
    u, w, qg, kd, attn, eg, pinv = _gdn_prep(q, k, v, gb, bb)
    og, sall = _gdn_chain(qg, kd, u, w, attn, eg)
    oa = _gdn_post(og, proj_m, head_norm_w)
    ob = _sgu_fwd(proj_m, sgu_ln_w, sgu_ln_b, w_spatial, bbc, A)
    dh, dhb, loss_row, d_fnw = _out_proj_loss(oa, ob, wout, x, tgt, final_norm_w.reshape(1, D))

    d_o = _mm_nn(dhb, wout.T, F32, "out_proj_dx")
    dproj = lax.empty((T, lay.k), BF16)
    dproj, d_lw, d_lb, d_ws, d_bs = _sgu_bwd(proj_m, sgu_ln_w, sgu_ln_b, w_spatial, bbc, d_o, A, dproj)
    dog, dproj, d_hw = _gdn_post_bwd(og, proj_m, head_norm_w, d_o, dproj)
    dqg, dkd, du, dw, dat, deg = _gdn_chain_bwd(qg, kd, u, w, attn, eg, sall, dog)
    dq, dk, dv, dgb, dbb = _gdn_prep_bwd(q, k, v, gb, bb, pinv, du, dw, dqg, dkd, dat, deg)
    dc, dproj, d_al, d_dt = _gdn_pre_bwd(proj_m, proj_ba, conv_w, alog_row, dtb_row, dq, dk, dv, dgb, dbb, H,
                                         dproj)
    dproj, d_conv = _conv_bwd(proj_m, dc, conv_w, H, dproj)

    table = jnp.array([b for row in lay.table for b in row], jnp.int32)
    d_win = _mm_windows(xn_t, dproj, table, nb, "in_proj_dw")
    d_wout, (land_w,) = _mm_tn_pair(oa, ob, dhb, "out_proj_dw", ride=_pair_ride(d_win))
    d_wout = d_wout.reshape(nb, Rb, D)
    pair_w, (land_o,) = _pair_sum(d_win, land_w, c_arr, "pair_sum_w_in", ride=_pair_ride(d_wout))
    pair_o = _pair_sum(d_wout, land_o, c_arr, "pair_sum_w_out")
    dxn, (all_w,) = _mm_nt_rhs_outer(dproj, w_own, F32, "in_proj_dx", ride=_chip_ride([pair_w]))
    (grad_x, d_nw), (all_o,) = _rms_in_bwd(x, norm_w, dxn, dh, ride=_chip_ride([pair_o]))
    all_w, all_o = _put_own_slot(all_w, pair_w), _put_own_slot(all_o, pair_o)
    small = dict(norm_w=d_nw, conv_w=d_conv[:CONV_WIDTH], a_log=d_al[:, H:2 * H], dt_bias=d_dt[:, H:2 * H],
                 head_norm_w=d_hw, sgu_ln_w=d_lw, sgu_ln_b=d_lb, w_spatial=d_ws, b_spatial=d_bs[:, :, 0],
                 final_norm_w=d_fnw)
    return loss_row, grad_x, small, all_w, all_o


SMALL = ("norm_w", "conv_w", "a_log", "dt_bias", "head_norm_w", "sgu_ln_w", "sgu_ln_b", "w_spatial",
         "b_spatial", "final_norm_w")


def _pack(parts):
    rows = []
    for p in parts:
        f = p.reshape(-1)
        f = jnp.pad(f, (0, (-f.shape[0]) % (8 * LANES)))
        rows.append(f.reshape(-1, LANES))
    return jnp.concatenate(rows, axis=0)


def _unpack(buf, shapes):
    out, r = [], 0
    for s in shapes:
        n = 1
        for d in s:
            n *= d
        nr = -(-n // (8 * LANES)) * 8
        out.append(buf[r:r + nr].reshape(-1)[:n].reshape(s))
        r += nr
    return out


def kernel(x, norm_w, w_in, conv_w, a_log, dt_bias, head_norm_w, sgu_ln_w, sgu_ln_b, w_spatial, b_spatial, w_out, final_norm_w, loss_target, m_norm_w, m_w_in, m_conv_w, m_a_log, m_dt_bias, m_head_norm_w, m_sgu_ln_w, m_sgu_ln_b, m_w_spatial, m_b_spatial, m_w_out, m_final_norm_w, v_norm_w, v_w_in, v_conv_w, v_a_log, v_dt_bias, v_head_norm_w, v_sgu_ln_w, v_sgu_ln_b, v_w_spatial, v_b_spatial, v_w_out, v_final_norm_w):
    T, D = x.shape[1], x.shape[2]
    weights = dict(norm_w=norm_w, w_in=w_in, conv_w=conv_w, a_log=a_log, dt_bias=dt_bias, head_norm_w=head_norm_w,
                   sgu_ln_w=sgu_ln_w, sgu_ln_b=sgu_ln_b, w_spatial=w_spatial, b_spatial=b_spatial, w_out=w_out,
                   final_norm_w=final_norm_w)
    mom_m = dict(norm_w=m_norm_w, w_in=m_w_in, conv_w=m_conv_w, a_log=m_a_log, dt_bias=m_dt_bias,
                 head_norm_w=m_head_norm_w, sgu_ln_w=m_sgu_ln_w, sgu_ln_b=m_sgu_ln_b, w_spatial=m_w_spatial,
                 b_spatial=m_b_spatial, w_out=m_w_out, final_norm_w=m_final_norm_w)
    mom_v = dict(norm_w=v_norm_w, w_in=v_w_in, conv_w=v_conv_w, a_log=v_a_log, dt_bias=v_dt_bias,
                 head_norm_w=v_head_norm_w, sgu_ln_w=v_sgu_ln_w, sgu_ln_b=v_sgu_ln_b, w_spatial=v_w_spatial,
                 b_spatial=v_b_spatial, w_out=v_w_out, final_norm_w=v_final_norm_w)
    me = _chip_index(lax.axis_index("x"), lax.axis_index("y"))
    c_arr = lax.axis_index("c").astype(jnp.int32).reshape(1)
    Din, Cb = w_in.shape[1], w_in.shape[2]
    Rb = w_out.shape[1]
    cconv = conv_w.shape[2]

    win_b = w_in[0].astype(BF16)
    g_in, = _run_ride(_gather_ride([win_b], [True]), "gather_w_in")
    g_in = _put_own(g_in, win_b)

    loss_row, grad_x, g, qw, qo = _device_step(
        x[0], loss_target[0], norm_w, g_in, w_out[0].astype(BF16), conv_w[0], a_log, dt_bias, head_norm_w,
        sgu_ln_w, sgu_ln_b, w_spatial[0], b_spatial[0], final_norm_w, c_arr)

    gsum_in, gsum_out = _sibling_concat(_chip_sum(qw, "chip_sum_w_in"), _chip_sum(qo, "chip_sum_w_out"))
    gsum_in = _Layout(a_log.shape[1], w_spatial.shape[1], N_CHIPS, Cb).from_window(gsum_in, me, Cb)
    small_shapes = [tuple(g[n].shape) for n in SMALL]
    small = _unpack(_allreduce_small(_pack([g[n] for n in SMALL])), small_shapes)
    gsmall = dict(zip(SMALL, small))
    gsmall["conv_w"] = lax.dynamic_slice_in_dim(gsmall["conv_w"], me * cconv, cconv, axis=1)

    grads, deltas, new_m, new_v = {}, {}, {}, {}
    d, m2, v2 = _adamw(w_out[0], gsum_out, m_w_out[0], v_w_out[0], "adamw_w_out")
    grads["w_out"], deltas["w_out"], new_m["w_out"], new_v["w_out"] = gsum_out[None], d[None], m2[None], v2[None]
    flat = lambda a: a.transpose(2, 0, 1).reshape(-1, LANES)
    unflat = lambda f: f.reshape(Cb, 1, Din).transpose(1, 2, 0)
    g_flat = gsum_in.T.reshape(-1, LANES)
    d, m2, v2 = _adamw(flat(w_in), g_flat, flat(m_w_in), flat(v_w_in), "adamw_w_in")
    grads["w_in"], deltas["w_in"], new_m["w_in"], new_v["w_in"] = unflat(g_flat), unflat(d), unflat(m2), unflat(v2)
    shapes = [tuple(weights[n].shape) for n in SMALL]
    ds, ms, vs = _adamw(_pack([weights[n] for n in SMALL]), _pack([gsmall[n] for n in SMALL]),
                        _pack([mom_m[n] for n in SMALL]), _pack([mom_v[n] for n in SMALL]), "adamw_small")
    for n, gq, d, m2, v2 in zip(SMALL, [gsmall[n] for n in SMALL], _unpack(ds, shapes), _unpack(ms, shapes),
                                _unpack(vs, shapes)):
        grads[n], deltas[n], new_m[n], new_v[n] = gq.reshape(weights[n].shape), d, m2, v2

    loss = lax.psum(loss_row[0, 0], ("x", "y", "c"))
    order = ("norm_w", "w_in", "conv_w", "a_log", "dt_bias", "head_norm_w", "sgu_ln_w", "sgu_ln_b", "w_spatial",
             "b_spatial", "w_out", "final_norm_w")
    return (loss, grad_x[None], *[grads[n] for n in order], *[deltas[n] for n in order],
            *[new_m[n] for n in order], *[new_v[n] for n in order])
```

```python
import functools

import jax
import jax.numpy as jnp
from jax import lax
from jax.experimental import pallas as pl
from jax.experimental.pallas import tpu as pltpu

F32 = jnp.float32
BF16 = jnp.bfloat16
EPS = 1e-6
HEAD_DIM = 128
CHUNK_A = 64
CHUNK_B = 128
CONV_WIDTH = 4
LANES = 128
HALO = 8
N_CHIPS = 4
ADAM_LR = 0.001
ADAM_B1 = 0.9
ADAM_B2 = 0.999
ADAM_EPS = 1e-08
ADAM_WD = 0.01
ADAM_STEP = 10
VMEM_LIMIT = 56 * 1024 * 1024
MESH_ID = pl.DeviceIdType.MESH
HI = lax.Precision.HIGHEST


def _cparams(sem=None, **kw):
    return pltpu.CompilerParams(dimension_semantics=sem, vmem_limit_bytes=VMEM_LIMIT, **kw)


def _matmul(a, b, ca, cb, precision):
    nb = a.ndim - 2
    batch = tuple(range(nb))
    return lax.dot_general(a, b, (((ca + nb,), (cb + nb,)), (batch, batch)), precision=precision,
                           preferred_element_type=F32)


def _dot(a, b, hi=False, precision=None):
    return _matmul(a, b, 1, 0, HI if hi else precision)


def _dot_nt(a, b, hi=False, precision=None):
    return _matmul(a, b, 1, 1, HI if hi else precision)


def _dot_tn(a, b, hi=False, precision=None):
    return _matmul(a, b, 0, 0, HI if hi else precision)


def _iota(shape, dim):
    return lax.broadcasted_iota(jnp.int32, shape, dim)


def _sigmoid(x):
    return 0.5 * (jnp.tanh(0.5 * x) + 1.0)


def _silu(x):
    return x * _sigmoid(x)


def _softplus(x):
    z = jnp.exp(-jnp.abs(x))
    small = z * (1.0 - z * (0.5 - z * (1.0 / 3.0)))
    return jnp.maximum(x, 0.0) + jnp.where(z < 1e-3, small, jnp.log(1.0 + z))


def _pick(n, pref):
    for t in pref:
        if n % t == 0:
            return t
    return n


class _Ride:
    def __init__(self, operands, out_shape, n_sems, start, finish):
        self.operands, self.out_shape, self.n_sems = list(operands), list(out_shape), n_sems
        self.start, self.finish = start, finish


def _pallas(body, operands, *, name, grid, in_specs, out_specs, out_shape, semantics, scratch_shapes=(),
            prefetch=0, ride=None):
    single = not isinstance(out_shape, (list, tuple))
    outs = [out_shape] if single else list(out_shape)
    ospecs = [out_specs] if single else list(out_specs)
    in_specs, scratch = list(in_specs), list(scratch_shapes)
    n_in, n_out, n_sc = len(operands) - prefetch, len(outs), len(scratch)
    kernel = body
    params = _cparams(semantics)
    if ride is not None:
        n_xin, n_xout = len(ride.operands), len(ride.out_shape)

        def kernel(*refs):
            pre, refs = refs[:prefetch], refs[prefetch:]
            ins, refs = refs[:n_in], refs[n_in:]
            xins, refs = refs[:n_xin], refs[n_xin:]
            mains, refs = refs[:n_out], refs[n_out:]
            xouts, refs = refs[:n_xout], refs[n_xout:]
            sc, (send, recv) = refs[:n_sc], refs[n_sc:]
            ids = [pl.program_id(a) for a in range(len(grid))]
            first = functools.reduce(jnp.logical_and, [i == 0 for i in ids])
            last = functools.reduce(jnp.logical_and, [i == g - 1 for i, g in zip(ids, grid)])

            @pl.when(first)
            def _():
                ride.start(xins, xouts, send, recv)

            body(*pre, *ins, *mains, *sc)

            @pl.when(last)
            def _():
                ride.finish(xins, xouts, send, recv)

        operands = list(operands) + ride.operands
        in_specs += [ANY] * n_xin
        ospecs += [ANY] * n_xout
        outs += ride.out_shape
        scratch += [pltpu.SemaphoreType.DMA((ride.n_sems,)), pltpu.SemaphoreType.DMA((ride.n_sems,))]
        params = _cparams(("arbitrary",) * len(grid), has_side_effects=True)
    if prefetch:
        spec = dict(grid_spec=pltpu.PrefetchScalarGridSpec(
            num_scalar_prefetch=prefetch, grid=grid, in_specs=in_specs, out_specs=ospecs, scratch_shapes=scratch))
    else:
        spec = dict(grid=grid, in_specs=in_specs, out_specs=ospecs, scratch_shapes=scratch)
    res = pl.pallas_call(kernel, name=name, out_shape=outs, compiler_params=params, **spec)(*operands)
    main = res[0] if single else list(res[:n_out])
    return main if ride is None else (main, list(res[n_out:]))


def _mm_nn(a, b, out_dtype, name, tm=1024, tn=512, tk=None, cols=None, ride=None):
    M, K = a.shape
    c0, N = (0, b.shape[1]) if cols is None else cols
    tm = _pick(M, (tm, 512, 256, 128))
    tn = _pick(N, (tn, 512, 384, 256, 128))
    tk = K if tk is None else _pick(K, (tk,))
    nk = K // tk
    j0 = c0 // tn
    assert c0 % tn == 0

    def body(a_ref, b_ref, o_ref, *scratch):
        part = _dot(a_ref[...], b_ref[...])
        if nk == 1:
            o_ref[...] = part.astype(out_dtype)
        else:
            acc_ref, = scratch
            k = pl.program_id(2)

            @pl.when(k == 0)
            def _():
                acc_ref[...] = part

            @pl.when(k > 0)
            def _():
                acc_ref[...] += part

            @pl.when(k == nk - 1)
            def _():
                o_ref[...] = acc_ref[...].astype(out_dtype)

    return _pallas(
        body, (a, b), name=name, grid=(M // tm, N // tn, nk),
        in_specs=[pl.BlockSpec((tm, tk), lambda i, j, k: (i, k)),
                  pl.BlockSpec((tk, tn), lambda i, j, k: (k, j + j0))],
        out_specs=pl.BlockSpec((tm, tn), lambda i, j, k: (i, j)),
        out_shape=jax.ShapeDtypeStruct((M, N), out_dtype),
        scratch_shapes=[] if nk == 1 else [pltpu.VMEM((tm, tn), F32)],
        semantics=("parallel", "parallel", "arbitrary"), ride=ride)


def _mm_nt_rhs_outer(a, b, out_dtype, name, tm=256, tn=1024, ride=None):
    M, K = a.shape
    N, _ = b.shape
    tm = _pick(M, (tm, 128))
    tn = _pick(N, (tn, 512, 256, 128))

    def body(a_ref, b_ref, o_ref):
        o_ref[...] = _dot_nt(a_ref[...], b_ref[...]).astype(out_dtype)

    return _pallas(
        body, (a, b), name=name, grid=(N // tn, M // tm),
        in_specs=[pl.BlockSpec((tm, K), lambda j, i: (i, 0)),
                  pl.BlockSpec((tn, K), lambda j, i: (j, 0))],
        out_specs=pl.BlockSpec((tm, tn), lambda j, i: (i, j)),
        out_shape=jax.ShapeDtypeStruct((M, N), out_dtype),
        semantics=("parallel", "parallel"), ride=ride)


WIN_BLOCK = 256


def _mm_windows(a, b, table, nb, name, tm=2048):
    M, K = a.shape
    wb = table.shape[0] // nb
    tm = _pick(M, (tm, 1024, 512, 256, 128))

    def body(tab_ref, a_ref, b_ref, o_ref):
        o_ref[0] = _dot(a_ref[...], b_ref[...])

    return pl.pallas_call(
        body, name=name,
        grid_spec=pltpu.PrefetchScalarGridSpec(
            num_scalar_prefetch=1, grid=(nb, M // tm, wb),
            in_specs=[pl.BlockSpec((tm, K), lambda n, i, t, tab: (i, 0)),
                      pl.BlockSpec((K, WIN_BLOCK), lambda n, i, t, tab: (0, tab[n * wb + t]))],
            out_specs=pl.BlockSpec((1, tm, WIN_BLOCK), lambda n, i, t, tab: (n, i, t))),
        out_shape=jax.ShapeDtypeStruct((nb, M, wb * WIN_BLOCK), F32),
        compiler_params=_cparams(("parallel", "parallel", "arbitrary")),
    )(table, a, b)


def _mm_nn_pair(a0, a1, b, name, tm=512, tn=1024, ride=None):
    M, K = a0.shape
    _, N = b.shape
    tm = _pick(M, (tm, 256, 128))
    tn = _pick(N, (tn, 512, 256, 128))
    ni = M // tm

    def body(a0_ref, a1_ref, b_ref, o_ref):
        p = pl.program_id(0)

        @pl.when(p == 0)
        def _():
            o_ref[...] = _dot(a0_ref[...], b_ref[...])

        @pl.when(p == 1)
        def _():
            o_ref[...] = _dot(a1_ref[...], b_ref[...])

    return _pallas(
        body, (a0, a1, b), name=name, grid=(2, ni, N // tn),
        in_specs=[pl.BlockSpec((tm, K), lambda p, i, j: (i * (1 - p), 0)),
                  pl.BlockSpec((tm, K), lambda p, i, j: (i * p, 0)),
                  pl.BlockSpec((K, tn), lambda p, i, j: (0, j))],
        out_specs=pl.BlockSpec((tm, tn), lambda p, i, j: (p * ni + i, j)),
        out_shape=jax.ShapeDtypeStruct((2 * M, N), F32),
        semantics=("parallel", "parallel", "parallel"), ride=ride)


def _rms_fn(x, w):
    r = lax.rsqrt(jnp.mean(x * x, axis=-1, keepdims=True) + EPS)
    return x * r * w


def _rms_in(x, w):
    T, D = x.shape
    tm = _pick(T, (512, 256, 128))

    def body(x_ref, w_ref, o_ref, ot_ref):
        xn = _rms_fn(x_ref[...], w_ref[...])
        o_ref[...] = xn.astype(BF16)
        ot_ref[...] = xn.T.astype(BF16)

    return pl.pallas_call(
        body, name="rms_in", grid=(T // tm,),
        in_specs=[pl.BlockSpec((tm, D), lambda i: (i, 0)), pl.BlockSpec((1, D), lambda i: (0, 0))],
        out_specs=[pl.BlockSpec((tm, D), lambda i: (i, 0)), pl.BlockSpec((D, tm), lambda i: (0, i))],
        out_shape=[jax.ShapeDtypeStruct((T, D), BF16), jax.ShapeDtypeStruct((D, T), BF16)],
        compiler_params=_cparams(("parallel",)),
    )(x, w)


def _rms_in_bwd(x, w, dxn, dh, ride=None):
    T, D = x.shape
    tm = _pick(T, (256, 128))

    def body(x_ref, w_ref, dxn_ref, dh_ref, gx_ref, dw_ref):
        _, vjp = jax.vjp(_rms_fn, x_ref[...], w_ref[...])
        dx, dw = vjp(dxn_ref[...])
        gx_ref[...] = dh_ref[...] + dx

        @pl.when(pl.program_id(0) == 0)
        def _():
            dw_ref[...] = dw

        @pl.when(pl.program_id(0) > 0)
        def _():
            dw_ref[...] += dw

    tile = pl.BlockSpec((tm, D), lambda i: (i, 0))
    row = pl.BlockSpec((1, D), lambda i: (0, 0))
    return _pallas(
        body, (x, w, dxn, dh), name="rms_in_bwd", grid=(T // tm,),
        in_specs=[tile, row, tile, tile], out_specs=[tile, row],
        out_shape=[jax.ShapeDtypeStruct((T, D), F32), jax.ShapeDtypeStruct((1, D), F32)],
        semantics=("arbitrary",), ride=ride)


def _conv_fwd(cat_ref, halo, x, w):
    tm = x.shape[0]
    cat_ref[0:HALO, :] = halo
    cat_ref[HALO:HALO + tm, :] = x
    c = x * w[CONV_WIDTH - 1:CONV_WIDTH, :]
    for k in range(CONV_WIDTH - 1):
        s = CONV_WIDTH - 1 - k
        c = c + cat_ref[pl.ds(HALO - s, tm), :] * w[k:k + 1, :]
    return c


def _lane_to_all(x, lane):
    @jax.custom_vjp
    def f(x):
        return jnp.broadcast_to(x[:, lane:lane + 1], x.shape)

    def f_fwd(x):
        return f(x), None

    def f_bwd(_, g):
        return (jnp.where(_iota(g.shape, 1) == lane, jnp.sum(g, axis=-1, keepdims=True), 0.0),)

    f.defvjp(f_fwd, f_bwd)
    return f(x)


def _gdn_pointwise(c, ba, alog, dtb, H):
    A = H * HEAD_DIM
    s = _silu(c)
    beta = _sigmoid(ba)
    g = -jnp.exp(alog) * _softplus(ba + dtb)
    qs, ks, vs, gbs, bbs = [], [], [], [], []
    for h in range(H):
        lo = h * HEAD_DIM
        q = s[:, lo:lo + HEAD_DIM]
        k = s[:, A + lo:A + lo + HEAD_DIM]
        qs.append(q * lax.rsqrt(jnp.sum(q * q, axis=-1, keepdims=True) + EPS))
        ks.append(k * lax.rsqrt(jnp.sum(k * k, axis=-1, keepdims=True) + EPS))
        vs.append(s[:, 2 * A + lo:2 * A + lo + HEAD_DIM])
        bbs.append(_lane_to_all(beta, h))
        gbs.append(_lane_to_all(g, H + h))
    st = lambda xs: jnp.stack(xs, axis=0)
    return st(qs), st(ks), st(vs), st(gbs), st(bbs)


def _halo_prev(tm):
    return lambda i: (jnp.maximum(i * (tm // HALO) - 1, 0), 0)


def _gdn_pre(proj_m, proj_ba, conv_w, alog_row, dtb_row, H):
    T = proj_m.shape[0]
    A = H * HEAD_DIM
    tm = _pick(T, (256, 128))
    hs = pl.BlockSpec((H, tm, HEAD_DIM), lambda i: (0, i, 0))
    hshape = jax.ShapeDtypeStruct((H, T, HEAD_DIM), F32)

    def body(x_ref, halo_ref, ba_ref, w_ref, al_ref, dt_ref, q_ref, k_ref, v_ref, gb_ref, bb_ref, cat_ref):
        halo = jnp.where(pl.program_id(0) == 0, 0.0, halo_ref[...])
        c = _conv_fwd(cat_ref, halo, x_ref[...], w_ref[...])
        q, k, v, gb, bb = _gdn_pointwise(c, ba_ref[...], al_ref[...], dt_ref[...], H)
        q_ref[...] = q
        k_ref[...] = k
        v_ref[...] = v
        gb_ref[...] = gb
        bb_ref[...] = bb

    return pl.pallas_call(
        body, name="gdn_pre", grid=(T // tm,),
        in_specs=[pl.BlockSpec((tm, 3 * A), lambda i: (i, 0)),
                  pl.BlockSpec((HALO, 3 * A), _halo_prev(tm)),
                  pl.BlockSpec((tm, LANES), lambda i: (i, 0)),
                  pl.BlockSpec((CONV_WIDTH, 3 * A), lambda i: (0, 0)),
                  pl.BlockSpec((1, LANES), lambda i: (0, 0)),
                  pl.BlockSpec((1, LANES), lambda i: (0, 0))],
        out_specs=[hs] * 5, out_shape=[hshape] * 5,
        scratch_shapes=[pltpu.VMEM((HALO + tm, 3 * A), F32)],
        compiler_params=_cparams(("parallel",)),
    )(proj_m, proj_m, proj_ba, conv_w, alog_row, dtb_row)


def _gdn_pre_bwd(proj_m, proj_ba, conv_w, alog_row, dtb_row, dq, dk, dv, dgb, dbb, H, dproj):
    T, n_main = proj_m.shape
    A = H * HEAD_DIM
    tm = _pick(T, (256, 128))
    hs = pl.BlockSpec((H, tm, HEAD_DIM), lambda i: (0, i, 0))
    row = pl.BlockSpec((1, LANES), lambda i: (0, 0))

    def body(x_ref, halo_ref, ba_ref, w_ref, al_ref, dt_ref, dq_ref, dk_ref, dv_ref, dgb_ref, dbb_ref, _,
             dc_ref, dba_ref, dal_ref, ddt_ref, cat_ref):
        halo = jnp.where(pl.program_id(0) == 0, 0.0, halo_ref[...])
        c = _conv_fwd(cat_ref, halo, x_ref[...], w_ref[...])
        _, vjp = jax.vjp(functools.partial(_gdn_pointwise, H=H), c, ba_ref[...], al_ref[...], dt_ref[...])
        dc, dba, dal, ddt = vjp((dq_ref[...], dk_ref[...], dv_ref[...], dgb_ref[...], dbb_ref[...]))
        dc_ref[...] = dc
        dba_ref[:, :LANES] = dba.astype(BF16)
        dba_ref[:, LANES:] = jnp.zeros((tm, WIN_BLOCK - LANES), BF16)

        @pl.when(pl.program_id(0) == 0)
        def _():
            dal_ref[...] = dal
            ddt_ref[...] = ddt

        @pl.when(pl.program_id(0) > 0)
        def _():
            dal_ref[...] += dal
            ddt_ref[...] += ddt

    return pl.pallas_call(
        body, name="gdn_pre_bwd", grid=(T // tm,),
        in_specs=[pl.BlockSpec((tm, 3 * A), lambda i: (i, 0)),
                  pl.BlockSpec((HALO, 3 * A), _halo_prev(tm)),
                  pl.BlockSpec((tm, LANES), lambda i: (i, 0)),
                  pl.BlockSpec((CONV_WIDTH, 3 * A), lambda i: (0, 0)),
                  row, row, hs, hs, hs, hs, hs, ANY],
        out_specs=[pl.BlockSpec((tm, 3 * A), lambda i: (i, 0)),
                   pl.BlockSpec((tm, WIN_BLOCK), lambda i: (i, n_main // WIN_BLOCK)), row, row],
        out_shape=[jax.ShapeDtypeStruct((T, 3 * A), F32), jax.ShapeDtypeStruct(dproj.shape, dproj.dtype),
                   jax.ShapeDtypeStruct((1, LANES), F32), jax.ShapeDtypeStruct((1, LANES), F32)],
        input_output_aliases={11: 1},
        scratch_shapes=[pltpu.VMEM((HALO + tm, 3 * A), F32)],
        compiler_params=_cparams(("arbitrary",)),
    )(proj_m, proj_m, proj_ba, conv_w, alog_row, dtb_row, dq, dk, dv, dgb, dbb, dproj)


def _conv_bwd(proj_m, dc, conv_w, H, dproj):
    T = proj_m.shape[0]
    A = H * HEAD_DIM
    tm = _pick(T, (256, 128))
    nt = T // tm

    def body(x_ref, halo_ref, dc_ref, nxt_ref, w_ref, _, dx_ref, dw_ref):
        i = pl.program_id(0)
        halo = jnp.where(i == 0, 0.0, halo_ref[...])
        xcat = jnp.concatenate([halo, x_ref[...]], axis=0)
        nxt = jnp.where(i == nt - 1, 0.0, nxt_ref[...])
        dc = dc_ref[...]
        dcat = jnp.concatenate([dc, nxt], axis=0)
        w = w_ref[...]
        dx = None
        rows = []
        for k in range(CONV_WIDTH):
            s = CONV_WIDTH - 1 - k
            ds = dcat if s == 0 else pltpu.roll(dcat, tm + HALO - s, 0)
            term = ds[:tm, :] * w[k:k + 1, :]
            dx = term if dx is None else dx + term
            xs = xcat if s == 0 else pltpu.roll(xcat, s, 0)
            rows.append(jnp.sum(dc * xs[HALO:, :], axis=0, keepdims=True))
        dx_ref[...] = dx.astype(BF16)
        dw = jnp.concatenate(rows + [jnp.zeros((HALO - CONV_WIDTH, 3 * A), F32)], axis=0)

        @pl.when(i == 0)
        def _():
            dw_ref[...] = dw

        @pl.when(i > 0)
        def _():
            dw_ref[...] += dw

    return pl.pallas_call(
        body, name="conv_bwd", grid=(nt,),
        in_specs=[pl.BlockSpec((tm, 3 * A), lambda i: (i, 0)),
                  pl.BlockSpec((HALO, 3 * A), _halo_prev(tm)),
                  pl.BlockSpec((tm, 3 * A), lambda i: (i, 0)),
                  pl.BlockSpec((HALO, 3 * A), lambda i: (jnp.minimum((i + 1) * (tm // HALO), T // HALO - 1), 0)),
                  pl.BlockSpec((CONV_WIDTH, 3 * A), lambda i: (0, 0)), ANY],
        out_specs=[pl.BlockSpec((tm, 3 * A), lambda i: (i, 0)),
                   pl.BlockSpec((HALO, 3 * A), lambda i: (0, 0))],
        out_shape=[jax.ShapeDtypeStruct(dproj.shape, dproj.dtype), jax.ShapeDtypeStruct((HALO, 3 * A), F32)],
        input_output_aliases={5: 0},
        compiler_params=_cparams(("arbitrary",)),
    )(proj_m, proj_m, dc, dc, conv_w, dproj)


PAIR = 2 * CHUNK_A


def _b(x):
    return x.astype(BF16)


@jax.custom_vjp
def _bdot(a, b):
    return _dot(_b(a), _b(b))


def _bdot_f(a, b):
    return _bdot(a, b), (a, b)


def _bdot_b(res, g):
    a, b = res
    return _dot_nt(_b(g), _b(b)), _dot_tn(_b(a), _b(g))


_bdot.defvjp(_bdot_f, _bdot_b)


@jax.custom_vjp
def _bdot_nt(a, b):
    return _dot_nt(_b(a), _b(b))


def _bdot_nt_f(a, b):
    return _bdot_nt(a, b), (a, b)


def _bdot_nt_b(res, g):
    a, b = res
    return _dot(_b(g), _b(b)), _dot_tn(_b(g), _b(a))


_bdot_nt.defvjp(_bdot_nt_f, _bdot_nt_b)


@jax.custom_vjp
def _bdot_tn(a, b):
    return _dot_tn(_b(a), _b(b))


def _bdot_tn_f(a, b):
    return _bdot_tn(a, b), (a, b)


def _bdot_tn_b(res, g):
    a, b = res
    return _dot_nt(_b(b), _b(g)), _dot(_b(a), _b(g))


_bdot_tn.defvjp(_bdot_tn_f, _bdot_tn_b)


def _mask_matmul(m, x):
    hi = _b(x)
    r = x - hi.astype(F32)
    mid = _b(r)
    lo = _b(r - mid.astype(F32))
    return (_dot(m, lo) + _dot(m, mid)) + _dot(m, hi)


@jax.custom_vjp
def _mask_dot(m, mt, x):
    return _mask_matmul(m, x)


def _mask_dot_f(m, mt, x):
    return _mask_matmul(m, x), (m, mt)


def _mask_dot_b(res, g):
    m, mt = res
    return jnp.zeros_like(m), jnp.zeros_like(mt), _mask_matmul(mt, g)


_mask_dot.defvjp(_mask_dot_f, _mask_dot_b)

HIGH = lax.Precision.HIGH


def _unit_lower_inverse(L):
    n = L.shape[-1]
    X = -L
    Q = X
    for _ in range(CHUNK_A.bit_length() - 2):
        X = _dot(_b(X), _b(X))
        Q = Q + X + _dot(_b(Q), _b(X))
    return (_iota((n, n), 0) == _iota((n, n), 1)).astype(F32) + Q


@jax.custom_vjp
def _known_inverse(L, P):
    return P


def _known_inverse_f(L, P):
    return P, P


def _known_inverse_b(P, g):
    n = P.shape[-1]
    Q = _b(P - (_iota((n, n), 0) == _iota((n, n), 1)).astype(F32))
    t = g + _dot_tn(Q, _b(g))
    return -(t + _dot_nt(_b(t), Q)), jnp.zeros_like(P)


_known_inverse.defvjp(_known_inverse_f, _known_inverse_b)


def _gdn_prep_fn(q, k, v, gb, bb, P_known=None):
    n = PAIR
    row, col = _iota((n, n), 0), _iota((n, n), 1)
    same = (row >= CHUNK_A) == (col >= CHUNK_A)
    incl = same & (row >= col)
    strict = same & (row > col)
    bc = lambda m: jnp.broadcast_to(_b(m.astype(F32)), q.shape[:1] + (n, n))
    tril, triu, ones = bc(incl), bc(same & (row <= col)), bc(same)
    gc = _mask_dot(tril, triu, gb)
    gl = _mask_dot(ones, ones, gb)
    decay = jnp.where(incl, jnp.exp(jnp.where(incl, gc - jnp.swapaxes(gc, 1, 2), 0.0)), 0.0)
    kb = k * bb
    vb = v * bb
    qs = q * (HEAD_DIM ** -0.5)
    L = jnp.where(strict, _bdot_nt(kb, k) * decay, 0.0)
    P = _unit_lower_inverse(L) if P_known is None else _known_inverse(L, P_known)
    egc = jnp.exp(gc)
    u = _bdot(P, vb)
    w = _bdot(P, kb * egc)
    attn = jnp.where(incl, _bdot_nt(qs, k) * decay, 0.0)
    qg = qs * egc
    kdec = k * jnp.exp(gl - gc)
    eg = jnp.exp(gl)
    if P_known is None:
        return u, w, qg, kdec, attn, eg, P
    return u, w, qg, kdec, attn, eg


def _gdn_chain_fn(S, qg, kdec, u, w, attn, eg):
    C = CHUNK_A
    a, b = (slice(None), slice(0, C)), (slice(None), slice(C, PAIR))
    cat = lambda xs: jnp.concatenate(xs, axis=1)
    vn_a = u[a] - _bdot(w[a], S)
    o_a = _bdot(qg[a], S) + _bdot(attn[a], cat([vn_a, jnp.zeros_like(vn_a)]))
    S1 = S * cat([eg[a], eg[a]]) + _bdot_tn(kdec[a], vn_a)
    vn_b = u[b] - _bdot(w[b], S1)
    o_b = _bdot(qg[b], S1) + _bdot(attn[b], cat([vn_a, vn_b]))
    S2 = S1 * cat([eg[b], eg[b]]) + _bdot_tn(kdec[b], vn_b)
    return cat([o_a, o_b]), S2


def _gdn_prep(q, k, v, gb, bb):
    H, T, _ = q.shape
    pb = _pick(T // PAIR, (8, 4, 2, 1))
    hs = pl.BlockSpec((1, PAIR * pb, HEAD_DIM), lambda h, n: (h, n, 0))
    hshape = jax.ShapeDtypeStruct((H, T, HEAD_DIM), F32)

    def body(q_ref, k_ref, v_ref, gb_ref, bb_ref, *out_refs):
        pairs = lambda ref: ref[0].reshape(pb, PAIR, HEAD_DIM)
        outs = _gdn_prep_fn(pairs(q_ref), pairs(k_ref), pairs(v_ref), pairs(gb_ref), pairs(bb_ref))
        for ref, val in zip(out_refs, outs):
            ref[0] = val.reshape(pb * PAIR, HEAD_DIM)

    return pl.pallas_call(
        body, name="gdn_prep", grid=(H, T // (PAIR * pb)),
        in_specs=[hs] * 5, out_specs=[hs] * 7, out_shape=[hshape] * 7,
        compiler_params=_cparams(("parallel", "parallel")),
    )(q, k, v, gb, bb)


def _gdn_prep_bwd(q, k, v, gb, bb, pinv, du, dw, dqg, dkd, dat, deg):
    H, T, _ = q.shape
    pb = _pick(T // PAIR, (8, 4, 2, 1))
    hs = pl.BlockSpec((1, PAIR * pb, HEAD_DIM), lambda h, n: (h, n, 0))
    hshape = jax.ShapeDtypeStruct((H, T, HEAD_DIM), F32)

    def body(*refs):
        in_refs, p_ref, ct_refs, out_refs = refs[:5], refs[5], refs[6:12], refs[12:]
        pairs = lambda ref: ref[0].reshape(pb, PAIR, HEAD_DIM)
        P = pairs(p_ref)
        _, vjp = jax.vjp(lambda *a: _gdn_prep_fn(*a, P_known=P), *[pairs(r) for r in in_refs])
        grads = vjp(tuple(pairs(r) for r in ct_refs))
        for ref, val in zip(out_refs, grads):
            ref[0] = val.reshape(pb * PAIR, HEAD_DIM)

    return pl.pallas_call(
        body, name="gdn_prep_bwd", grid=(H, T // (PAIR * pb)),
        in_specs=[hs] * 12, out_specs=[hs] * 5, out_shape=[hshape] * 5,
        compiler_params=_cparams(("parallel", "parallel")),
    )(q, k, v, gb, bb, pinv, du, dw, dqg, dkd, dat, deg)


def _gdn_chain(qg, kd, u, w, attn, eg):
    H, T, _ = qg.shape
    N = T // PAIR
    hs = pl.BlockSpec((H, PAIR, HEAD_DIM), lambda n: (0, n, 0))
    ss = pl.BlockSpec((1, H, HEAD_DIM, HEAD_DIM), lambda n: (n, 0, 0, 0))

    def body(qg_ref, kd_ref, u_ref, w_ref, at_ref, eg_ref, o_ref, sall_ref, s_ref):
        @pl.when(pl.program_id(0) == 0)
        def _():
            s_ref[...] = jnp.zeros_like(s_ref)

        S = s_ref[...]
        sall_ref[0] = S
        o, S2 = _gdn_chain_fn(S, qg_ref[...], kd_ref[...], u_ref[...], w_ref[...], at_ref[...], eg_ref[...])
        o_ref[...] = o
        s_ref[...] = S2

    return pl.pallas_call(
        body, name="gdn_chain", grid=(N,),
        in_specs=[hs] * 6, out_specs=[hs, ss],
        out_shape=[jax.ShapeDtypeStruct((H, T, HEAD_DIM), F32),
                   jax.ShapeDtypeStruct((N, H, HEAD_DIM, HEAD_DIM), F32)],
        scratch_shapes=[pltpu.VMEM((H, HEAD_DIM, HEAD_DIM), F32)],
        compiler_params=_cparams(("arbitrary",)),
    )(qg, kd, u, w, attn, eg)


def _gdn_chain_bwd(qg, kd, u, w, attn, eg, sall, do):
    H, T, _ = qg.shape
    N = T // PAIR
    hs = pl.BlockSpec((H, PAIR, HEAD_DIM), lambda n: (0, N - 1 - n, 0))
    ss = pl.BlockSpec((1, H, HEAD_DIM, HEAD_DIM), lambda n: (N - 1 - n, 0, 0, 0))
    hshape = jax.ShapeDtypeStruct((H, T, HEAD_DIM), F32)

    def body(qg_ref, kd_ref, u_ref, w_ref, at_ref, eg_ref, sall_ref, do_ref, *rest):
        out_refs, ds_ref = rest[:6], rest[6]

        @pl.when(pl.program_id(0) == 0)
        def _():
            ds_ref[...] = jnp.zeros_like(ds_ref)

        _, vjp = jax.vjp(_gdn_chain_fn, sall_ref[0], qg_ref[...], kd_ref[...], u_ref[...], w_ref[...],
                         at_ref[...], eg_ref[...])
        grads = vjp((do_ref[...], ds_ref[...]))
        ds_ref[...] = grads[0]
        for ref, val in zip(out_refs, grads[1:]):
            ref[...] = val

    return pl.pallas_call(
        body, name="gdn_chain_bwd", grid=(N,),
        in_specs=[hs] * 6 + [ss, hs], out_specs=[hs] * 6, out_shape=[hshape] * 6,
        scratch_shapes=[pltpu.VMEM((H, HEAD_DIM, HEAD_DIM), F32)],
        compiler_params=_cparams(("arbitrary",)),
    )(qg, kd, u, w, attn, eg, sall, do)


def _post_fn(ogs, za, hw):
    outs = []
    for h, o in enumerate(ogs):
        r = lax.rsqrt(jnp.mean(o * o, axis=-1, keepdims=True) + EPS)
        outs.append(o * r * hw * _silu(za[:, h * HEAD_DIM:(h + 1) * HEAD_DIM]))
    return jnp.concatenate(outs, axis=1)


def _gdn_post(og, proj_m, hw):
    H, T, _ = og.shape
    A = H * HEAD_DIM
    tm = _pick(T, (512, 256, 128))

    def body(og_ref, za_ref, hw_ref, o_ref, ot_ref):
        o = _post_fn(tuple(og_ref[h] for h in range(H)), za_ref[...], hw_ref[...])
        o_ref[...] = o.astype(BF16)
        ot_ref[...] = o.T.astype(BF16)

    return pl.pallas_call(
        body, name="gdn_post", grid=(T // tm,),
        in_specs=[pl.BlockSpec((H, tm, HEAD_DIM), lambda i: (0, i, 0)),
                  pl.BlockSpec((tm, A), lambda i: (i, ZA_BLOCK)),
                  pl.BlockSpec((1, HEAD_DIM), lambda i: (0, 0))],
        out_specs=[pl.BlockSpec((tm, A), lambda i: (i, 0)), pl.BlockSpec((A, tm), lambda i: (0, i))],
        out_shape=[jax.ShapeDtypeStruct((T, A), BF16), jax.ShapeDtypeStruct((A, T), BF16)],
        compiler_params=_cparams(("parallel",)),
    )(og, proj_m, hw)


def _gdn_post_bwd(og, proj_m, hw, d_o, dproj):
    H, T, _ = og.shape
    A = H * HEAD_DIM
    tm = _pick(T, (256, 128))

    def body(og_ref, za_ref, hw_ref, do_ref, _, dog_ref, dza_ref, dhw_ref):
        _, vjp = jax.vjp(_post_fn, tuple(og_ref[h] for h in range(H)), za_ref[...], hw_ref[...])
        dog, dza, dhw = vjp(do_ref[...])
        for h in range(H):
            dog_ref[h] = dog[h]
        dza_ref[...] = dza.astype(BF16)

        @pl.when(pl.program_id(0) == 0)
        def _():
            dhw_ref[...] = dhw

        @pl.when(pl.program_id(0) > 0)
        def _():
            dhw_ref[...] += dhw

    return pl.pallas_call(
        body, name="gdn_post_bwd", grid=(T // tm,),
        in_specs=[pl.BlockSpec((H, tm, HEAD_DIM), lambda i: (0, i, 0)),
                  pl.BlockSpec((tm, A), lambda i: (i, ZA_BLOCK)),
                  pl.BlockSpec((1, HEAD_DIM), lambda i: (0, 0)),
                  pl.BlockSpec((tm, A), lambda i: (i, 0)), ANY],
        out_specs=[pl.BlockSpec((H, tm, HEAD_DIM), lambda i: (0, i, 0)),
                   pl.BlockSpec((tm, A), lambda i: (i, ZA_BLOCK)),
                   pl.BlockSpec((1, HEAD_DIM), lambda i: (0, 0))],
        out_shape=[jax.ShapeDtypeStruct((H, T, HEAD_DIM), F32), jax.ShapeDtypeStruct(dproj.shape, dproj.dtype),
                   jax.ShapeDtypeStruct((1, HEAD_DIM), F32)],
        input_output_aliases={4: 1},
        compiler_params=_cparams(("arbitrary",)),
    )(og, proj_m, hw, d_o, dproj)


def _sgu_fn(ub, vb, zb, lw, lb, W, bbc):
    G = len(W)
    tm = ub.shape[0]
    mu = jnp.mean(vb, axis=-1, keepdims=True)
    xc = vb - mu
    var = jnp.mean(xc * xc, axis=-1, keepdims=True)
    vn = xc * lax.rsqrt(var + EPS) * lw + lb
    mask = _iota((CHUNK_B, CHUNK_B), 0) >= _iota((CHUNK_B, CHUNK_B), 1)
    cols = []
    for g in range(G):
        wm = jnp.where(mask, W[g], 0.0).astype(BF16)
        rows = []
        for c in range(tm // CHUNK_B):
            blk = vn[c * CHUNK_B:(c + 1) * CHUNK_B, g * HEAD_DIM:(g + 1) * HEAD_DIM].astype(BF16)
            rows.append(_dot(wm, blk) + bbc[g])
        cols.append(jnp.concatenate(rows, axis=0) if len(rows) > 1 else rows[0])
    s = jnp.concatenate(cols, axis=1)
    return ub * s * _silu(zb)


ZA_BLOCK = 6


def _sgu_cols(A, B):
    assert A == B
    return 3, 4, 5


def _sgu_fwd(proj_m, lw, lb, W, bbc, A):
    T = proj_m.shape[0]
    G = W.shape[0]
    B = G * HEAD_DIM
    tm = _pick(T, (256, 128))
    cu, cv, cz = _sgu_cols(A, B)

    def body(u_ref, v_ref, z_ref, lw_ref, lb_ref, w_ref, b_ref, o_ref, ot_ref):
        o = _sgu_fn(u_ref[...], v_ref[...], z_ref[...], lw_ref[...], lb_ref[...],
                    tuple(w_ref[g] for g in range(G)), tuple(b_ref[g] for g in range(G)))
        o_ref[...] = o.astype(BF16)
        ot_ref[...] = o.T.astype(BF16)

    row = pl.BlockSpec((1, B), lambda i: (0, 0))
    cube = pl.BlockSpec((G, CHUNK_B, CHUNK_B), lambda i: (0, 0, 0))
    return pl.pallas_call(
        body, name="sgu_fwd", grid=(T // tm,),
        in_specs=[pl.BlockSpec((tm, B), lambda i: (i, cu)), pl.BlockSpec((tm, B), lambda i: (i, cv)),
                  pl.BlockSpec((tm, B), lambda i: (i, cz)), row, row, cube, cube],
        out_specs=[pl.BlockSpec((tm, B), lambda i: (i, 0)), pl.BlockSpec((B, tm), lambda i: (0, i))],
        out_shape=[jax.ShapeDtypeStruct((T, B), BF16), jax.ShapeDtypeStruct((B, T), BF16)],
        compiler_params=_cparams(("parallel",)),
    )(proj_m, proj_m, proj_m, lw, lb, W, bbc)


def _sgu_bwd(proj_m, lw, lb, W, bbc, d_o, A, dproj):
    T = proj_m.shape[0]
    G = W.shape[0]
    B = G * HEAD_DIM
    tm = _pick(T, (256, 128))
    nt = T // tm
    cu, cv, cz = _sgu_cols(A, B)

    def body(u_ref, v_ref, z_ref, lw_ref, lb_ref, w_ref, b_ref, do_ref, _,
             dp_ref, dlw_ref, dlb_ref, dw_ref, db_ref, dbb_ref):
        _, vjp = jax.vjp(_sgu_fn, u_ref[...], v_ref[...], z_ref[...], lw_ref[...], lb_ref[...],
                         tuple(w_ref[g] for g in range(G)), tuple(b_ref[g] for g in range(G)))
        du, dv, dz, dlw, dlb, dW, dbb = vjp(do_ref[...])
        dW, dbb = jnp.stack(dW, axis=0), jnp.stack(dbb, axis=0)
        dp_ref[:, 0:B] = du.astype(BF16)
        dp_ref[:, B:2 * B] = dv.astype(BF16)
        dp_ref[:, 2 * B:3 * B] = dz.astype(BF16)
        i = pl.program_id(0)

        @pl.when(i == 0)
        def _():
            dlw_ref[...] = dlw
            dlb_ref[...] = dlb
            dw_ref[...] = dW
            dbb_ref[...] = dbb

        @pl.when(i > 0)
        def _():
            dlw_ref[...] += dlw
            dlb_ref[...] += dlb
            dw_ref[...] += dW
            dbb_ref[...] += dbb

        @pl.when(i == nt - 1)
        def _():
            db_ref[...] = jnp.sum(dbb_ref[...], axis=-1, keepdims=True)

    row = pl.BlockSpec((1, B), lambda i: (0, 0))
    cube = pl.BlockSpec((G, CHUNK_B, CHUNK_B), lambda i: (0, 0, 0))
    return pl.pallas_call(
        body, name="sgu_bwd", grid=(nt,),
        in_specs=[pl.BlockSpec((tm, B), lambda i: (i, cu)), pl.BlockSpec((tm, B), lambda i: (i, cv)),
                  pl.BlockSpec((tm, B), lambda i: (i, cz)), row, row, cube, cube,
                  pl.BlockSpec((tm, B), lambda i: (i, A // B)), ANY],
        out_specs=[pl.BlockSpec((tm, 3 * B), lambda i: (i, 1)), row, row, cube,
                   pl.BlockSpec((G, CHUNK_B, 1), lambda i: (0, 0, 0))],
        out_shape=[jax.ShapeDtypeStruct(dproj.shape, dproj.dtype), jax.ShapeDtypeStruct((1, B), F32),
                   jax.ShapeDtypeStruct((1, B), F32), jax.ShapeDtypeStruct((G, CHUNK_B, CHUNK_B), F32),
                   jax.ShapeDtypeStruct((G, CHUNK_B, 1), F32)],
        input_output_aliases={8: 0},
        scratch_shapes=[pltpu.VMEM((G, CHUNK_B, CHUNK_B), F32)],
        compiler_params=_cparams(("arbitrary",)),
    )(proj_m, proj_m, proj_m, lw, lb, W, bbc, d_o, dproj)


def _head_fn(mix, x, fw, tgt):
    h = x + mix
    y = _rms_fn(h, fw)
    e = y - tgt
    return 0.5 * jnp.sum(jnp.mean(e * e, axis=-1, keepdims=True), axis=0, keepdims=True)


def _out_proj_loss(oa, ob, wout, x, tgt, fw):
    T, A = oa.shape
    B = ob.shape[1]
    D = x.shape[1]
    tm = _pick(T, (256, 128))

    def body(oa_ref, ob_ref, w_ref, x_ref, t_ref, fw_ref, dh_ref, dhb_ref, loss_ref, dfw_ref):
        mix = _dot(oa_ref[...], w_ref[0:A, :]) + _dot(ob_ref[...], w_ref[A:A + B, :])
        xv, tv = x_ref[...], t_ref[...]
        loss, vjp = jax.vjp(lambda m, f: _head_fn(m, xv, f, tv), mix, fw_ref[...])
        dh, dfw = vjp(jnp.ones((1, 1), F32))
        dh_ref[...] = dh
        dhb_ref[...] = dh.astype(BF16)
        lrow = jnp.broadcast_to(loss, (1, LANES))

        @pl.when(pl.program_id(0) == 0)
        def _():
            loss_ref[...] = lrow
            dfw_ref[...] = dfw

        @pl.when(pl.program_id(0) > 0)
        def _():
            loss_ref[...] += lrow
            dfw_ref[...] += dfw

    tile = pl.BlockSpec((tm, D), lambda i: (i, 0))
    return pl.pallas_call(
        body, name="out_proj_loss", grid=(T // tm,),
        in_specs=[pl.BlockSpec((tm, A), lambda i: (i, 0)), pl.BlockSpec((tm, B), lambda i: (i, 0)),
                  pl.BlockSpec((A + B, D), lambda i: (0, 0)), tile, tile,
                  pl.BlockSpec((1, D), lambda i: (0, 0))],
        out_specs=[tile, tile, pl.BlockSpec((1, LANES), lambda i: (0, 0)),
                   pl.BlockSpec((1, D), lambda i: (0, 0))],
        out_shape=[jax.ShapeDtypeStruct((T, D), F32), jax.ShapeDtypeStruct((T, D), BF16),
                   jax.ShapeDtypeStruct((1, LANES), F32), jax.ShapeDtypeStruct((1, D), F32)],
        compiler_params=_cparams(("arbitrary",)),
    )(oa, ob, wout, x, tgt, fw)


def _adamw(w, g, m, v, name):
    R, Cn = w.shape
    cap = max(8, 512 * 1024 // Cn)
    tr = max(t for t in range(8, min(R, cap) + 1, 8) if R % t == 0) if R > cap else R

    def body(w_ref, g_ref, m_ref, v_ref, d_ref, mo_ref, vo_ref):
        g = g_ref[...]
        m = ADAM_B1 * m_ref[...] + (1.0 - ADAM_B1) * g
        v = ADAM_B2 * v_ref[...] + (1.0 - ADAM_B2) * jnp.square(g)
        m_hat = m / (1.0 - ADAM_B1 ** ADAM_STEP)
        v_hat = v / (1.0 - ADAM_B2 ** ADAM_STEP)
        d_ref[...] = -ADAM_LR * (m_hat / (jnp.sqrt(v_hat) + ADAM_EPS) + ADAM_WD * w_ref[...])
        mo_ref[...] = m
        vo_ref[...] = v

    tile = pl.BlockSpec((tr, Cn), lambda i: (i, 0))
    shape = jax.ShapeDtypeStruct((R, Cn), F32)
    return pl.pallas_call(
        body, name=name, grid=(R // tr,), in_specs=[tile] * 4, out_specs=[tile] * 3,
        out_shape=[shape] * 3, compiler_params=_cparams(("parallel",)),
    )(w, g, m, v)


def _place():
    x, y, c = lax.axis_index("x"), lax.axis_index("y"), lax.axis_index("c")
    others = [(1 - x, y), (x, 1 - y), (1 - x, 1 - y)]
    return x, y, c, others


def _chip_index(px, py):
    return 2 * px + py


ANY = pl.BlockSpec(memory_space=pl.ANY)


def _gather_ride(blocks, split):
    n = len(blocks)

    def copies(in_refs, out_refs, send_sems, recv_sems):
        x, y, c, others = _place()
        me = _chip_index(x, y)

        def copy(sem, src, dst, to):
            return pltpu.make_async_remote_copy(src_ref=src, dst_ref=dst, send_sem=send_sems.at[sem],
                                                recv_sem=recv_sems.at[sem], device_id=to, device_id_type=MESH_ID)

        def part(a, chip, core):
            if not split[a]:
                return out_refs[a].at[chip]
            h = blocks[a].shape[0] // 2
            return out_refs[a].at[chip, pl.ds(core * h, h), :]

        def mine(a):
            if not split[a]:
                return in_refs[a]
            h = blocks[a].shape[0] // 2
            return in_refs[a].at[pl.ds(c * h, h), :]

        pairs = [(a, j, chip) for j, chip in enumerate(others) for a in range(n)]
        k = lambda chip: _chip_index(*chip)
        send = lambda a, j, chip: copy(3 * a + j, mine(a), part(a, me, c), (*chip, c))
        arrival = lambda a, j, chip: copy(3 * a + j, part(a, k(chip), c), part(a, k(chip), c), (*chip, c))
        passing = lambda a, j, chip: copy(3 * (n + a) + j, part(a, k(chip), c), part(a, k(chip), c), (x, y, 1 - c))
        passed = lambda a, j, chip: copy(3 * (n + a) + j, part(a, k(chip), 1 - c), part(a, k(chip), 1 - c),
                                         (x, y, 1 - c))
        return pairs, send, arrival, passing, passed

    def start(in_refs, out_refs, send_sems, recv_sems):
        pairs, send, _, _, _ = copies(in_refs, out_refs, send_sems, recv_sems)
        for p in pairs:
            send(*p).start()

    def finish(in_refs, out_refs, send_sems, recv_sems):
        pairs, send, arrival, passing, passed = copies(in_refs, out_refs, send_sems, recv_sems)
        for p in pairs:
            arrival(*p).wait_recv()
            if split[p[0]]:
                passing(*p).start()
        for p in pairs:
            if split[p[0]]:
                passed(*p).wait_recv()
        for p in pairs:
            send(*p).wait_send()
            if split[p[0]]:
                passing(*p).wait_send()

    shapes = [jax.ShapeDtypeStruct((N_CHIPS,) + b.shape, b.dtype) for b in blocks]
    return _Ride(blocks, shapes, 6 * n, start, finish)


def _put_own(gathered, own):
    me = _chip_index(lax.axis_index("x"), lax.axis_index("y"))
    return lax.dynamic_update_index_in_dim(gathered, own, me, 0)


def _run_ride(ride, name):
    def body(*refs):
        n_in, n_out = len(ride.operands), len(ride.out_shape)
        ins, outs, (send, recv) = refs[:n_in], refs[n_in:n_in + n_out], refs[n_in + n_out:]
        ride.start(ins, outs, send, recv)
        ride.finish(ins, outs, send, recv)

    return pl.pallas_call(
        body, name=name, in_specs=[ANY] * len(ride.operands), out_specs=[ANY] * len(ride.out_shape),
        out_shape=ride.out_shape,
        scratch_shapes=[pltpu.SemaphoreType.DMA((ride.n_sems,)), pltpu.SemaphoreType.DMA((ride.n_sems,))],
        compiler_params=pltpu.CompilerParams(has_side_effects=True),
    )(*ride.operands)


def _allreduce_small(buf):
    R, L = buf.shape

    def body(in_ref, out_ref, sib_ref, pair_ref, chips_ref, send_sems, recv_sems):
        x, y, c, others = _place()
        me = _chip_index(x, y)
        sibling = (x, y, 1 - c)
        cp = pltpu.make_async_remote_copy(src_ref=in_ref, dst_ref=sib_ref, send_sem=send_sems.at[0],
                                          recv_sem=recv_sems.at[0], device_id=sibling, device_id_type=MESH_ID)
        cp.start()
        cp.wait()
        pair_ref[...] = in_ref[...] + sib_ref[...]
        sends = []
        for j, chip in enumerate(others):
            s = pltpu.make_async_remote_copy(src_ref=pair_ref, dst_ref=chips_ref.at[me],
                                             send_sem=send_sems.at[1 + j], recv_sem=recv_sems.at[1 + j],
                                             device_id=(*chip, c), device_id_type=MESH_ID)
            s.start()
            sends.append(s)
        chips_ref[me] = pair_ref[...]
        for j, chip in enumerate(others):
            k = _chip_index(*chip)
            pltpu.make_async_remote_copy(src_ref=pair_ref, dst_ref=chips_ref.at[k], send_sem=send_sems.at[1 + j],
                                         recv_sem=recv_sems.at[1 + j], device_id=(*chip, c),
                                         device_id_type=MESH_ID).wait_recv()
        for s in sends:
            s.wait_send()
        out_ref[...] = ((chips_ref[0] + chips_ref[1]) + chips_ref[2]) + chips_ref[3]

    vm = pl.BlockSpec(memory_space=pltpu.VMEM)
    return pl.pallas_call(
        body, name="allreduce_small", in_specs=[vm], out_specs=vm,
        out_shape=jax.ShapeDtypeStruct((R, L), F32),
        scratch_shapes=[pltpu.VMEM((R, L), F32), pltpu.VMEM((R, L), F32), pltpu.VMEM((N_CHIPS, R, L), F32),
                        pltpu.SemaphoreType.DMA((4,)), pltpu.SemaphoreType.DMA((4,))],
        compiler_params=pltpu.CompilerParams(vmem_limit_bytes=VMEM_LIMIT),
    )(buf)


def _pair_ride(g):
    nb, R, Cn = g.shape
    h = R // 2

    def copy(in_refs, out_refs, send_sems, recv_sems):
        x, y, c, _ = _place()
        return pltpu.make_async_remote_copy(src_ref=in_refs[0].at[:, pl.ds((1 - c) * h, h), :], dst_ref=out_refs[0],
                                            send_sem=send_sems.at[0], recv_sem=recv_sems.at[0],
                                            device_id=(x, y, 1 - c), device_id_type=MESH_ID)

    return _Ride([g], [jax.ShapeDtypeStruct((nb, h, Cn), g.dtype)], 1,
                 lambda *refs: copy(*refs).start(), lambda *refs: copy(*refs).wait())


def _pair_sum(g, land, c_arr, name, ride=None):
    nb, R, Cn = g.shape
    hr = R // 2
    tr = _pick(hr, (256, 128, 64, 32, 16))
    nt = hr // tr

    def body(c_ref, g_ref, l_ref, o_ref):
        o_ref[...] = (g_ref[...] + l_ref[...]).astype(BF16)

    return _pallas(
        body, (c_arr, g, land), name=name, prefetch=1, grid=(nb, nt),
        in_specs=[pl.BlockSpec((1, tr, Cn), lambda b, i, c_ref: (b, c_ref[0] * nt + i, 0)),
                  pl.BlockSpec((1, tr, Cn), lambda b, i, c_ref: (b, i, 0))],
        out_specs=pl.BlockSpec((1, tr, Cn), lambda b, i, c_ref: (b, i, 0)),
        out_shape=jax.ShapeDtypeStruct((nb, hr, Cn), BF16),
        semantics=("parallel", "parallel"), ride=ride)


def _chip_ride(parts):
    m = len(parts)

    def copies(in_refs, out_refs, send_sems, recv_sems):
        x, y, c, others = _place()
        me = _chip_index(x, y)
        def mk(j, chip, n, landing):
            k = _chip_index(*chip)
            return pltpu.make_async_remote_copy(
                src_ref=in_refs[n].at[k], dst_ref=out_refs[n].at[landing(k)], send_sem=send_sems.at[m * j + n],
                recv_sem=recv_sems.at[m * j + n], device_id=(*chip, c), device_id_type=MESH_ID)

        pairs = [(j, chip, n) for j, chip in enumerate(others) for n in range(m)]
        return pairs, (lambda *p: mk(*p, lambda k: me)), (lambda *p: mk(*p, lambda k: k))

    def start(*refs):
        pairs, send, _ = copies(*refs)
        for p in pairs:
            send(*p).start()

    def finish(*refs):
        pairs, send, arrival = copies(*refs)
        for p in pairs:
            arrival(*p).wait_recv()
        for p in pairs:
            send(*p).wait_send()

    return _Ride(parts, [jax.ShapeDtypeStruct(p.shape, p.dtype) for p in parts], 3 * m, start, finish)


def _put_own_slot(q, p):
    me = _chip_index(lax.axis_index("x"), lax.axis_index("y"))
    return lax.dynamic_update_index_in_dim(q, lax.dynamic_index_in_dim(p, me, 0, keepdims=False), me, 0)


def _chip_sum(q, name):
    nb, hr, Cn = q.shape
    tr = _pick(hr, (256, 128, 64, 32, 16))

    def body(q_ref, o_ref):
        f = lambda k: q_ref[k].astype(F32)
        o_ref[...] = ((f(0) + f(1)) + f(2)) + f(3)

    return pl.pallas_call(
        body, name=name, grid=(hr // tr,),
        in_specs=[pl.BlockSpec((nb, tr, Cn), lambda i: (0, i, 0))],
        out_specs=pl.BlockSpec((tr, Cn), lambda i: (i, 0)),
        out_shape=jax.ShapeDtypeStruct((hr, Cn), F32),
        compiler_params=_cparams(("parallel",)),
    )(q)


def _sibling_concat(rw, ro):
    hi, Cb = rw.shape
    ho, D = ro.shape

    def body(rw_ref, ro_ref, tw_ref, to_ref, send_sems, recv_sems):
        x, y, c, _ = _place()
        sibling = (x, y, 1 - c)
        a = pltpu.make_async_remote_copy(src_ref=rw_ref, dst_ref=tw_ref, send_sem=send_sems.at[0],
                                         recv_sem=recv_sems.at[0], device_id=sibling, device_id_type=MESH_ID)
        b = pltpu.make_async_remote_copy(src_ref=ro_ref, dst_ref=to_ref, send_sem=send_sems.at[1],
                                         recv_sem=recv_sems.at[1], device_id=sibling, device_id_type=MESH_ID)
        a.start()
        b.start()
        a.wait()
        b.wait()

    tw, to = pl.pallas_call(
        body, name="sibling_concat", in_specs=[ANY, ANY], out_specs=[ANY, ANY],
        out_shape=[jax.ShapeDtypeStruct((hi, Cb), F32), jax.ShapeDtypeStruct((ho, D), F32)],
        scratch_shapes=[pltpu.SemaphoreType.DMA((2,)), pltpu.SemaphoreType.DMA((2,))],
        compiler_params=pltpu.CompilerParams(has_side_effects=True),
    )(rw, ro)
    c = lax.axis_index("c")
    join = lambda mine, theirs: lax.dynamic_update_slice_in_dim(
        jnp.concatenate([mine, mine], axis=0), theirs, (1 - c) * mine.shape[0], axis=0)
    return join(rw, tw), join(ro, to)


class _Layout:
    def __init__(self, H, G, nb, Cb):
        A, B = H * HEAD_DIM, G * HEAD_DIM
        self.n_main = 4 * A + 3 * B
        self.k = -(-(self.n_main + LANES) // WIN_BLOCK) * WIN_BLOCK
        cuts = [0, 3 * A, 4 * A, 4 * A + 2 * H, nb * Cb]
        starts = [0, 3 * A + 3 * B, self.n_main, 3 * A]
        self.pieces = []
        self.windows, self.runs = [], []
        for n in range(nb):
            segs = []
            for s in range(4):
                lo, hi = max(cuts[s], n * Cb), min(cuts[s + 1], (n + 1) * Cb)
                if lo < hi:
                    segs.append((starts[s] + lo - cuts[s], lo - n * Cb, hi - lo))
            self.pieces += [(own, n, col, ln) for own, col, ln in segs]
            blocks = sorted({b for own, _, ln in segs for b in range(own // WIN_BLOCK, (own + ln - 1) // WIN_BLOCK + 1)})
            self.windows.append(blocks)
            self.runs.append([(blocks.index(own // WIN_BLOCK) * WIN_BLOCK + own % WIN_BLOCK, ln)
                              for own, _, ln in segs])
        self.wb = max(len(b) for b in self.windows)
        self.table = [b + [b[-1]] * (self.wb - len(b)) for b in self.windows]
        self.pieces.sort()

    def to_own_order(self, g_in):
        D = g_in.shape[1]
        cols, at = [], 0
        for own, n, col, ln in self.pieces:
            if own > at:
                cols.append(jnp.zeros((D, own - at), g_in.dtype))
            cols.append(g_in[n, :, col:col + ln])
            at = own + ln
        if at < self.k:
            cols.append(jnp.zeros((D, self.k - at), g_in.dtype))
        return jnp.concatenate(cols, axis=1)

    def from_window(self, win, chip, Cb):
        pick = lambda runs: (lambda w: jnp.concatenate([w[:, c:c + ln] for c, ln in runs], axis=1))
        return lax.switch(chip, [pick(r) for r in self.runs], win)


def _device_step(x, tgt, norm_w, g_in, wout_b, conv_b, a_log, dt_bias, head_norm_w, sgu_ln_w, sgu_ln_b,
                 w_spatial, b_spatial, final_norm_w, c_arr):
    T, D = x.shape
    H = a_log.shape[1]
    A = H * HEAD_DIM
    G = w_spatial.shape[0]
    B = G * HEAD_DIM
    nb, _, Cb = g_in.shape
    Rb = wout_b.shape[0]
    lay = _Layout(H, G, nb, Cb)
    w_own = lay.to_own_order(g_in)
    alog_row = jnp.pad(a_log, ((0, 0), (H, LANES - 2 * H)))
    dtb_row = jnp.pad(dt_bias, ((0, 0), (H, LANES - 2 * H)))
    bbc = jnp.broadcast_to(b_spatial[:, :, None], (G, CHUNK_B, CHUNK_B))

    xn, xn_t = _rms_in(x, norm_w)
    proj_m, (g_out, g_conv) = _mm_nn(xn, w_own, F32, "in_proj", cols=(0, lay.n_main),
                                     ride=_gather_ride([wout_b, conv_b], [True, False]))
    wout = _put_own(g_out, wout_b).reshape(nb * Rb, D)
    conv_w = _put_own(g_conv, conv_b).transpose(1, 0, 2).reshape(CONV_WIDTH, nb * conv_b.shape[1])
    proj_ba = _mm_nn(xn, w_own, F32, "in_proj_ba", cols=(lay.n_main, LANES))
    q, k, v, gb, bb = _gdn_pre(proj_m, proj_ba, conv_w, alog_row, dtb_row, H)
    u, w, qg, kd, attn, eg, pinv = _gdn_prep(q, k, v, gb, bb)
    og, sall = _gdn_chain(qg, kd, u, w, attn, eg)
    oa, oa_t = _gdn_post(og, proj_m, head_norm_w)
    ob, ob_t = _sgu_fwd(proj_m, sgu_ln_w, sgu_ln_b, w_spatial, bbc, A)
    dh, dhb, loss_row, d_fnw = _out_proj_loss(oa, ob, wout, x, tgt, final_norm_w.reshape(1, D))

    d_o = _mm_nn(dhb, wout.T, F32, "out_proj_dx")
    dproj = lax.empty((T, lay.k), BF16)
    dproj, d_lw, d_lb, d_ws, d_bs = _sgu_bwd(proj_m, sgu_ln_w, sgu_ln_b, w_spatial, bbc, d_o, A, dproj)
    dog, dproj, d_hw = _gdn_post_bwd(og, proj_m, head_norm_w, d_o, dproj)
    dqg, dkd, du, dw, dat, deg = _gdn_chain_bwd(qg, kd, u, w, attn, eg, sall, dog)
    dq, dk, dv, dgb, dbb = _gdn_prep_bwd(q, k, v, gb, bb, pinv, du, dw, dqg, dkd, dat, deg)
    dc, dproj, d_al, d_dt = _gdn_pre_bwd(proj_m, proj_ba, conv_w, alog_row, dtb_row, dq, dk, dv, dgb, dbb, H,
                                         dproj)
    dproj, d_conv = _conv_bwd(proj_m, dc, conv_w, H, dproj)

    table = jnp.array([b for row in lay.table for b in row], jnp.int32)
    d_win = _mm_windows(xn_t, dproj, table, nb, "in_proj_dw")
    d_wout, (land_w,) = _mm_nn_pair(oa_t, ob_t, dhb, "out_proj_dw", ride=_pair_ride(d_win))
    d_wout = d_wout.reshape(nb, Rb, D)
    pair_w, (land_o,) = _pair_sum(d_win, land_w, c_arr, "pair_sum_w_in", ride=_pair_ride(d_wout))
    pair_o = _pair_sum(d_wout, land_o, c_arr, "pair_sum_w_out")
    dxn, (all_w,) = _mm_nt_rhs_outer(dproj, w_own, F32, "in_proj_dx", ride=_chip_ride([pair_w]))
    (grad_x, d_nw), (all_o,) = _rms_in_bwd(x, norm_w, dxn, dh, ride=_chip_ride([pair_o]))
    all_w, all_o = _put_own_slot(all_w, pair_w), _put_own_slot(all_o, pair_o)
    small = dict(norm_w=d_nw, conv_w=d_conv[:CONV_WIDTH], a_log=d_al[:, H:2 * H], dt_bias=d_dt[:, H:2 * H],
                 head_norm_w=d_hw, sgu_ln_w=d_lw, sgu_ln_b=d_lb, w_spatial=d_ws, b_spatial=d_bs[:, :, 0],
                 final_norm_w=d_fnw)
    return loss_row, grad_x, small, all_w, all_o


SMALL = ("norm_w", "conv_w", "a_log", "dt_bias", "head_norm_w", "sgu_ln_w", "sgu_ln_b", "w_spatial",
         "b_spatial", "final_norm_w")


def _pack(parts):
    rows = []
    for p in parts:
        f = p.reshape(-1)
        f = jnp.pad(f, (0, (-f.shape[0]) % (8 * LANES)))
        rows.append(f.reshape(-1, LANES))
    return jnp.concatenate(rows, axis=0)


def _unpack(buf, shapes):
    out, r = [], 0
    for s in shapes:
        n = 1
        for d in s:
            n *= d
        nr = -(-n // (8 * LANES)) * 8
        out.append(buf[r:r + nr].reshape(-1)[:n].reshape(s))
        r += nr
    return out


def kernel(x, norm_w, w_in, conv_w, a_log, dt_bias, head_norm_w, sgu_ln_w, sgu_ln_b, w_spatial, b_spatial, w_out, final_norm_w, loss_target, m_norm_w, m_w_in, m_conv_w, m_a_log, m_dt_bias, m_head_norm_w, m_sgu_ln_w, m_sgu_ln_b, m_w_spatial, m_b_spatial, m_w_out, m_final_norm_w, v_norm_w, v_w_in, v_conv_w, v_a_log, v_dt_bias, v_head_norm_w, v_sgu_ln_w, v_sgu_ln_b, v_w_spatial, v_b_spatial, v_w_out, v_final_norm_w):
    T, D = x.shape[1], x.shape[2]
    weights = dict(norm_w=norm_w, w_in=w_in, conv_w=conv_w, a_log=a_log, dt_bias=dt_bias, head_norm_w=head_norm_w,
                   sgu_ln_w=sgu_ln_w, sgu_ln_b=sgu_ln_b, w_spatial=w_spatial, b_spatial=b_spatial, w_out=w_out,
                   final_norm_w=final_norm_w)
    mom_m = dict(norm_w=m_norm_w, w_in=m_w_in, conv_w=m_conv_w, a_log=m_a_log, dt_bias=m_dt_bias,
                 head_norm_w=m_head_norm_w, sgu_ln_w=m_sgu_ln_w, sgu_ln_b=m_sgu_ln_b, w_spatial=m_w_spatial,
                 b_spatial=m_b_spatial, w_out=m_w_out, final_norm_w=m_final_norm_w)
    mom_v = dict(norm_w=v_norm_w, w_in=v_w_in, conv_w=v_conv_w, a_log=v_a_log, dt_bias=v_dt_bias,
                 head_norm_w=v_head_norm_w, sgu_ln_w=v_sgu_ln_w, sgu_ln_b=v_sgu_ln_b, w_spatial=v_w_spatial,
                 b_spatial=v_b_spatial, w_out=v_w_out, final_norm_w=v_final_norm_w)
    me = _chip_index(lax.axis_index("x"), lax.axis_index("y"))
    c_arr = lax.axis_index("c").astype(jnp.int32).reshape(1)
    Din, Cb = w_in.shape[1], w_in.shape[2]
    Rb = w_out.shape[1]
    cconv = conv_w.shape[2]

    win_b = w_in[0].astype(BF16)
    g_in, = _run_ride(_gather_ride([win_b], [True]), "gather_w_in")
    g_in = _put_own(g_in, win_b)

    loss_row, grad_x, g, qw, qo = _device_step(
        x[0], loss_target[0], norm_w, g_in, w_out[0].astype(BF16), conv_w[0], a_log, dt_bias, head_norm_w,
        sgu_ln_w, sgu_ln_b, w_spatial[0], b_spatial[0], final_norm_w, c_arr)

    gsum_in, gsum_out = _sibling_concat(_chip_sum(qw, "chip_sum_w_in"), _chip_sum(qo, "chip_sum_w_out"))
    gsum_in = _Layout(a_log.shape[1], w_spatial.shape[1], N_CHIPS, Cb).from_window(gsum_in, me, Cb)
    small_shapes = [tuple(g[n].shape) for n in SMALL]
    small = _unpack(_allreduce_small(_pack([g[n] for n in SMALL])), small_shapes)
    gsmall = dict(zip(SMALL, small))
    gsmall["conv_w"] = lax.dynamic_slice_in_dim(gsmall["conv_w"], me * cconv, cconv, axis=1)

    grads, deltas, new_m, new_v = {}, {}, {}, {}
    d, m2, v2 = _adamw(w_out[0], gsum_out, m_w_out[0], v_w_out[0], "adamw_w_out")
    grads["w_out"], deltas["w_out"], new_m["w_out"], new_v["w_out"] = gsum_out[None], d[None], m2[None], v2[None]
    flat = lambda a: a.transpose(2, 0, 1).reshape(-1, LANES)
    unflat = lambda f: f.reshape(Cb, 1, Din).transpose(1, 2, 0)
    g_flat = gsum_in.T.reshape(-1, LANES)
    d, m2, v2 = _adamw(flat(w_in), g_flat, flat(m_w_in), flat(v_w_in), "adamw_w_in")
    grads["w_in"], deltas["w_in"], new_m["w_in"], new_v["w_in"] = unflat(g_flat), unflat(d), unflat(m2), unflat(v2)
    shapes = [tuple(weights[n].shape) for n in SMALL]
    ds, ms, vs = _adamw(_pack([weights[n] for n in SMALL]), _pack([gsmall[n] for n in SMALL]),
                        _pack([mom_m[n] for n in SMALL]), _pack([mom_v[n] for n in SMALL]), "adamw_small")
    for n, gq, d, m2, v2 in zip(SMALL, [gsmall[n] for n in SMALL], _unpack(ds, shapes), _unpack(ms, shapes),
                                _unpack(vs, shapes)):
        grads[n], deltas[n], new_m[n], new_v[n] = gq.reshape(weights[n].shape), d, m2, v2

    loss = lax.psum(loss_row[0, 0], ("x", "y", "c"))
    order = ("norm_w", "w_in", "conv_w", "a_log", "dt_bias", "head_norm_w", "sgu_ln_w", "sgu_ln_b", "w_spatial",
             "b_spatial", "w_out", "final_norm_w")
    return (loss, grad_x[None], *[grads[n] for n in order], *[deltas[n] for n in order],
            *[new_m[n] for n in order], *[new_v[n] for n in order])
```

```python
import functools

import jax
import jax.numpy as jnp
from jax import lax
from jax.experimental import pallas as pl
from jax.experimental.pallas import tpu as pltpu

F32 = jnp.float32
BF16 = jnp.bfloat16
EPS = 1e-6
HEAD_DIM = 128
CHUNK_A = 64
CHUNK_B = 128
CONV_WIDTH = 4
LANES = 128
HALO = 8
N_CHIPS = 4
ADAM_LR = 0.001
ADAM_B1 = 0.9
ADAM_B2 = 0.999
ADAM_EPS = 1e-08
ADAM_WD = 0.01
ADAM_STEP = 10
VMEM_LIMIT = 56 * 1024 * 1024
MESH_ID = pl.DeviceIdType.MESH
HI = lax.Precision.HIGHEST


def _cparams(sem=None, **kw):
    return pltpu.CompilerParams(dimension_semantics=sem, vmem_limit_bytes=VMEM_LIMIT, **kw)


def _matmul(a, b, ca, cb, precision):
    nb = a.ndim - 2
    batch = tuple(range(nb))
    return lax.dot_general(a, b, (((ca + nb,), (cb + nb,)), (batch, batch)), precision=precision,
                           preferred_element_type=F32)


def _dot(a, b, hi=False, precision=None):
    return _matmul(a, b, 1, 0, HI if hi else precision)


def _dot_nt(a, b, hi=False, precision=None):
    return _matmul(a, b, 1, 1, HI if hi else precision)


def _dot_tn(a, b, hi=False, precision=None):
    return _matmul(a, b, 0, 0, HI if hi else precision)


def _iota(shape, dim):
    return lax.broadcasted_iota(jnp.int32, shape, dim)


def _sigmoid(x):
    return 0.5 * (jnp.tanh(0.5 * x) + 1.0)


def _silu(x):
    return x * _sigmoid(x)


def _softplus(x):
    z = jnp.exp(-jnp.abs(x))
    small = z * (1.0 - z * (0.5 - z * (1.0 / 3.0)))
    return jnp.maximum(x, 0.0) + jnp.where(z < 1e-3, small, jnp.log(1.0 + z))


def _pick(n, pref):
    for t in pref:
        if n % t == 0:
            return t
    return n


class _Ride:
    def __init__(self, operands, out_shape, n_sems, start, finish):
        self.operands, self.out_shape, self.n_sems = list(operands), list(out_shape), n_sems
        self.start, self.finish = start, finish


def _pallas(body, operands, *, name, grid, in_specs, out_specs, out_shape, semantics, scratch_shapes=(),
            prefetch=0, ride=None):
    single = not isinstance(out_shape, (list, tuple))
    outs = [out_shape] if single else list(out_shape)
    ospecs = [out_specs] if single else list(out_specs)
    in_specs, scratch = list(in_specs), list(scratch_shapes)
    n_in, n_out, n_sc = len(operands) - prefetch, len(outs), len(scratch)
    kernel = body
    params = _cparams(semantics)
    if ride is not None:
        n_xin, n_xout = len(ride.operands), len(ride.out_shape)

        def kernel(*refs):
            pre, refs = refs[:prefetch], refs[prefetch:]
            ins, refs = refs[:n_in], refs[n_in:]
            xins, refs = refs[:n_xin], refs[n_xin:]
            mains, refs = refs[:n_out], refs[n_out:]
            xouts, refs = refs[:n_xout], refs[n_xout:]
            sc, (send, recv) = refs[:n_sc], refs[n_sc:]
            ids = [pl.program_id(a) for a in range(len(grid))]
            first = functools.reduce(jnp.logical_and, [i == 0 for i in ids])
            last = functools.reduce(jnp.logical_and, [i == g - 1 for i, g in zip(ids, grid)])

            @pl.when(first)
            def _():
                ride.start(xins, xouts, send, recv)

            body(*pre, *ins, *mains, *sc)

            @pl.when(last)
            def _():
                ride.finish(xins, xouts, send, recv)

        operands = list(operands) + ride.operands
        in_specs += [ANY] * n_xin
        ospecs += [ANY] * n_xout
        outs += ride.out_shape
        scratch += [pltpu.SemaphoreType.DMA((ride.n_sems,)), pltpu.SemaphoreType.DMA((ride.n_sems,))]
        params = _cparams(("arbitrary",) * len(grid), has_side_effects=True)
    if prefetch:
        spec = dict(grid_spec=pltpu.PrefetchScalarGridSpec(
            num_scalar_prefetch=prefetch, grid=grid, in_specs=in_specs, out_specs=ospecs, scratch_shapes=scratch))
    else:
        spec = dict(grid=grid, in_specs=in_specs, out_specs=ospecs, scratch_shapes=scratch)
    res = pl.pallas_call(kernel, name=name, out_shape=outs, compiler_params=params, **spec)(*operands)
    main = res[0] if single else list(res[:n_out])
    return main if ride is None else (main, list(res[n_out:]))


def _mm_nn(a, b, out_dtype, name, tm=1024, tn=512, tk=None, cols=None, ride=None):
    M, K = a.shape
    c0, N = (0, b.shape[1]) if cols is None else cols
    tm = _pick(M, (tm, 512, 256, 128))
    tn = _pick(N, (tn, 512, 384, 256, 128))
    tk = K if tk is None else _pick(K, (tk,))
    nk = K // tk
    j0 = c0 // tn
    assert c0 % tn == 0

    def body(a_ref, b_ref, o_ref, *scratch):
        part = _dot(a_ref[...], b_ref[...])
        if nk == 1:
            o_ref[...] = part.astype(out_dtype)
        else:
            acc_ref, = scratch
            k = pl.program_id(2)

            @pl.when(k == 0)
            def _():
                acc_ref[...] = part

            @pl.when(k > 0)
            def _():
                acc_ref[...] += part

            @pl.when(k == nk - 1)
            def _():
                o_ref[...] = acc_ref[...].astype(out_dtype)

    return _pallas(
        body, (a, b), name=name, grid=(M // tm, N // tn, nk),
        in_specs=[pl.BlockSpec((tm, tk), lambda i, j, k: (i, k)),
                  pl.BlockSpec((tk, tn), lambda i, j, k: (k, j + j0))],
        out_specs=pl.BlockSpec((tm, tn), lambda i, j, k: (i, j)),
        out_shape=jax.ShapeDtypeStruct((M, N), out_dtype),
        scratch_shapes=[] if nk == 1 else [pltpu.VMEM((tm, tn), F32)],
        semantics=("parallel", "parallel", "arbitrary"), ride=ride)


def _mm_nt_rhs_outer(a, b, out_dtype, name, tm=256, tn=1024, ride=None):
    M, K = a.shape
    N, _ = b.shape
    tm = _pick(M, (tm, 128))
    tn = _pick(N, (tn, 512, 256, 128))

    def body(a_ref, b_ref, o_ref):
        o_ref[...] = _dot_nt(a_ref[...], b_ref[...]).astype(out_dtype)

    return _pallas(
        body, (a, b), name=name, grid=(N // tn, M // tm),
        in_specs=[pl.BlockSpec((tm, K), lambda j, i: (i, 0)),
                  pl.BlockSpec((tn, K), lambda j, i: (j, 0))],
        out_specs=pl.BlockSpec((tm, tn), lambda j, i: (i, j)),
        out_shape=jax.ShapeDtypeStruct((M, N), out_dtype),
        semantics=("parallel", "parallel"), ride=ride)


WIN_BLOCK = 256


def _mm_windows(a, b, table, nb, name, tm=2048):
    M, K = a.shape
    wb = table.shape[0] // nb
    tm = _pick(M, (tm, 1024, 512, 256, 128))

    def body(tab_ref, a_ref, b_ref, o_ref):
        o_ref[0] = _dot(a_ref[...], b_ref[...])

    return pl.pallas_call(
        body, name=name,
        grid_spec=pltpu.PrefetchScalarGridSpec(
            num_scalar_prefetch=1, grid=(nb, M // tm, wb),
            in_specs=[pl.BlockSpec((tm, K), lambda n, i, t, tab: (i, 0)),
                      pl.BlockSpec((K, WIN_BLOCK), lambda n, i, t, tab: (0, tab[n * wb + t]))],
            out_specs=pl.BlockSpec((1, tm, WIN_BLOCK), lambda n, i, t, tab: (n, i, t))),
        out_shape=jax.ShapeDtypeStruct((nb, M, wb * WIN_BLOCK), F32),
        compiler_params=_cparams(("parallel", "parallel", "arbitrary")),
    )(table, a, b)


def _mm_nn_pair(a0, a1, b, name, tm=512, tn=1024, ride=None):
    M, K = a0.shape
    _, N = b.shape
    tm = _pick(M, (tm, 256, 128))
    tn = _pick(N, (tn, 512, 256, 128))
    ni = M // tm

    def body(a0_ref, a1_ref, b_ref, o_ref):
        p = pl.program_id(0)

        @pl.when(p == 0)
        def _():
            o_ref[...] = _dot(a0_ref[...], b_ref[...])

        @pl.when(p == 1)
        def _():
            o_ref[...] = _dot(a1_ref[...], b_ref[...])

    return _pallas(
        body, (a0, a1, b), name=name, grid=(2, ni, N // tn),
        in_specs=[pl.BlockSpec((tm, K), lambda p, i, j: (i * (1 - p), 0)),
                  pl.BlockSpec((tm, K), lambda p, i, j: (i * p, 0)),
                  pl.BlockSpec((K, tn), lambda p, i, j: (0, j))],
        out_specs=pl.BlockSpec((tm, tn), lambda p, i, j: (p * ni + i, j)),
        out_shape=jax.ShapeDtypeStruct((2 * M, N), F32),
        semantics=("parallel", "parallel", "parallel"), ride=ride)


def _rms_fn(x, w):
    r = lax.rsqrt(jnp.mean(x * x, axis=-1, keepdims=True) + EPS)
    return x * r * w


def _rms_in(x, w):
    T, D = x.shape
    tm = _pick(T, (512, 256, 128))

    def body(x_ref, w_ref, o_ref, ot_ref):
        xn = _rms_fn(x_ref[...], w_ref[...])
        o_ref[...] = xn.astype(BF16)
        ot_ref[...] = xn.T.astype(BF16)

    return pl.pallas_call(
        body, name="rms_in", grid=(T // tm,),
        in_specs=[pl.BlockSpec((tm, D), lambda i: (i, 0)), pl.BlockSpec((1, D), lambda i: (0, 0))],
        out_specs=[pl.BlockSpec((tm, D), lambda i: (i, 0)), pl.BlockSpec((D, tm), lambda i: (0, i))],
        out_shape=[jax.ShapeDtypeStruct((T, D), BF16), jax.ShapeDtypeStruct((D, T), BF16)],
        compiler_params=_cparams(("parallel",)),
    )(x, w)


def _rms_in_bwd(x, w, dxn, dh, ride=None):
    T, D = x.shape
    tm = _pick(T, (256, 128))

    def body(x_ref, w_ref, dxn_ref, dh_ref, gx_ref, dw_ref):
        _, vjp = jax.vjp(_rms_fn, x_ref[...], w_ref[...])
        dx, dw = vjp(dxn_ref[...])
        gx_ref[...] = dh_ref[...] + dx

        @pl.when(pl.program_id(0) == 0)
        def _():
            dw_ref[...] = dw

        @pl.when(pl.program_id(0) > 0)
        def _():
            dw_ref[...] += dw

    tile = pl.BlockSpec((tm, D), lambda i: (i, 0))
    row = pl.BlockSpec((1, D), lambda i: (0, 0))
    return _pallas(
        body, (x, w, dxn, dh), name="rms_in_bwd", grid=(T // tm,),
        in_specs=[tile, row, tile, tile], out_specs=[tile, row],
        out_shape=[jax.ShapeDtypeStruct((T, D), F32), jax.ShapeDtypeStruct((1, D), F32)],
        semantics=("arbitrary",), ride=ride)


def _conv_fwd(cat_ref, halo, x, w):
    tm = x.shape[0]
    cat_ref[0:HALO, :] = halo
    cat_ref[HALO:HALO + tm, :] = x
    c = x * w[CONV_WIDTH - 1:CONV_WIDTH, :]
    for k in range(CONV_WIDTH - 1):
        s = CONV_WIDTH - 1 - k
        c = c + cat_ref[pl.ds(HALO - s, tm), :] * w[k:k + 1, :]
    return c


def _lane_to_all(x, lane):
    @jax.custom_vjp
    def f(x):
        return jnp.broadcast_to(x[:, lane:lane + 1], x.shape)

    def f_fwd(x):
        return f(x), None

    def f_bwd(_, g):
        return (jnp.where(_iota(g.shape, 1) == lane, jnp.sum(g, axis=-1, keepdims=True), 0.0),)

    f.defvjp(f_fwd, f_bwd)
    return f(x)


def _gdn_pointwise(c, ba, alog, dtb, H):
    A = H * HEAD_DIM
    s = _silu(c)
    beta = _sigmoid(ba)
    g = -jnp.exp(alog) * _softplus(ba + dtb)
    qs, ks, vs, gbs, bbs = [], [], [], [], []
    for h in range(H):
        lo = h * HEAD_DIM
        q = s[:, lo:lo + HEAD_DIM]
        k = s[:, A + lo:A + lo + HEAD_DIM]
        qs.append(q * lax.rsqrt(jnp.sum(q * q, axis=-1, keepdims=True) + EPS))
        ks.append(k * lax.rsqrt(jnp.sum(k * k, axis=-1, keepdims=True) + EPS))
        vs.append(s[:, 2 * A + lo:2 * A + lo + HEAD_DIM])
        bbs.append(_lane_to_all(beta, h))
        gbs.append(_lane_to_all(g, H + h))
    st = lambda xs: jnp.stack(xs, axis=0)
    return st(qs), st(ks), st(vs), st(gbs), st(bbs)


def _halo_prev(tm):
    return lambda i: (jnp.maximum(i * (tm // HALO) - 1, 0), 0)


def _gdn_pre(proj_m, proj_ba, conv_w, alog_row, dtb_row, H):
    T = proj_m.shape[0]
    A = H * HEAD_DIM
    tm = _pick(T, (256, 128))
    hs = pl.BlockSpec((H, tm, HEAD_DIM), lambda i: (0, i, 0))
    hshape = jax.ShapeDtypeStruct((H, T, HEAD_DIM), F32)

    def body(x_ref, halo_ref, ba_ref, w_ref, al_ref, dt_ref, q_ref, k_ref, v_ref, gb_ref, bb_ref, cat_ref):
        halo = jnp.where(pl.program_id(0) == 0, 0.0, halo_ref[...])
        c = _conv_fwd(cat_ref, halo, x_ref[...], w_ref[...])
        q, k, v, gb, bb = _gdn_pointwise(c, ba_ref[...], al_ref[...], dt_ref[...], H)
        q_ref[...] = q
        k_ref[...] = k
        v_ref[...] = v
        gb_ref[...] = gb
        bb_ref[...] = bb

    return pl.pallas_call(
        body, name="gdn_pre", grid=(T // tm,),
        in_specs=[pl.BlockSpec((tm, 3 * A), lambda i: (i, 0)),
                  pl.BlockSpec((HALO, 3 * A), _halo_prev(tm)),
                  pl.BlockSpec((tm, LANES), lambda i: (i, 0)),
                  pl.BlockSpec((CONV_WIDTH, 3 * A), lambda i: (0, 0)),
                  pl.BlockSpec((1, LANES), lambda i: (0, 0)),
                  pl.BlockSpec((1, LANES), lambda i: (0, 0))],
        out_specs=[hs] * 5, out_shape=[hshape] * 5,
        scratch_shapes=[pltpu.VMEM((HALO + tm, 3 * A), F32)],
        compiler_params=_cparams(("parallel",)),
    )(proj_m, proj_m, proj_ba, conv_w, alog_row, dtb_row)


def _gdn_pre_bwd(proj_m, proj_ba, conv_w, alog_row, dtb_row, dq, dk, dv, dgb, dbb, H, dproj):
    T, n_main = proj_m.shape
    A = H * HEAD_DIM
    tm = _pick(T, (256, 128))
    hs = pl.BlockSpec((H, tm, HEAD_DIM), lambda i: (0, i, 0))
    row = pl.BlockSpec((1, LANES), lambda i: (0, 0))

    def body(x_ref, halo_ref, ba_ref, w_ref, al_ref, dt_ref, dq_ref, dk_ref, dv_ref, dgb_ref, dbb_ref, _,
             dc_ref, dba_ref, dal_ref, ddt_ref, cat_ref):
        halo = jnp.where(pl.program_id(0) == 0, 0.0, halo_ref[...])
        c = _conv_fwd(cat_ref, halo, x_ref[...], w_ref[...])
        _, vjp = jax.vjp(functools.partial(_gdn_pointwise, H=H), c, ba_ref[...], al_ref[...], dt_ref[...])
        dc, dba, dal, ddt = vjp((dq_ref[...], dk_ref[...], dv_ref[...], dgb_ref[...], dbb_ref[...]))
        dc_ref[...] = dc
        dba_ref[:, :LANES] = dba.astype(BF16)
        dba_ref[:, LANES:] = jnp.zeros((tm, WIN_BLOCK - LANES), BF16)

        @pl.when(pl.program_id(0) == 0)
        def _():
            dal_ref[...] = dal
            ddt_ref[...] = ddt

        @pl.when(pl.program_id(0) > 0)
        def _():
            dal_ref[...] += dal
            ddt_ref[...] += ddt

    return pl.pallas_call(
        body, name="gdn_pre_bwd", grid=(T // tm,),
        in_specs=[pl.BlockSpec((tm, 3 * A), lambda i: (i, 0)),
                  pl.BlockSpec((HALO, 3 * A), _halo_prev(tm)),
                  pl.BlockSpec((tm, LANES), lambda i: (i, 0)),
                  pl.BlockSpec((CONV_WIDTH, 3 * A), lambda i: (0, 0)),
                  row, row, hs, hs, hs, hs, hs, ANY],
        out_specs=[pl.BlockSpec((tm, 3 * A), lambda i: (i, 0)),
                   pl.BlockSpec((tm, WIN_BLOCK), lambda i: (i, n_main // WIN_BLOCK)), row, row],
        out_shape=[jax.ShapeDtypeStruct((T, 3 * A), F32), jax.ShapeDtypeStruct(dproj.shape, dproj.dtype),
                   jax.ShapeDtypeStruct((1, LANES), F32), jax.ShapeDtypeStruct((1, LANES), F32)],
        input_output_aliases={11: 1},
        scratch_shapes=[pltpu.VMEM((HALO + tm, 3 * A), F32)],
        compiler_params=_cparams(("arbitrary",)),
    )(proj_m, proj_m, proj_ba, conv_w, alog_row, dtb_row, dq, dk, dv, dgb, dbb, dproj)


def _conv_bwd(proj_m, dc, conv_w, H, dproj):
    T = proj_m.shape[0]
    A = H * HEAD_DIM
    tm = _pick(T, (256, 128))
    nt = T // tm

    def body(x_ref, halo_ref, dc_ref, nxt_ref, w_ref, _, dx_ref, dw_ref):
        i = pl.program_id(0)
        halo = jnp.where(i == 0, 0.0, halo_ref[...])
        xcat = jnp.concatenate([halo, x_ref[...]], axis=0)
        nxt = jnp.where(i == nt - 1, 0.0, nxt_ref[...])
        dc = dc_ref[...]
        dcat = jnp.concatenate([dc, nxt], axis=0)
        w = w_ref[...]
        dx = None
        rows = []
        for k in range(CONV_WIDTH):
            s = CONV_WIDTH - 1 - k
            ds = dcat if s == 0 else pltpu.roll(dcat, tm + HALO - s, 0)
            term = ds[:tm, :] * w[k:k + 1, :]
            dx = term if dx is None else dx + term
            xs = xcat if s == 0 else pltpu.roll(xcat, s, 0)
            rows.append(jnp.sum(dc * xs[HALO:, :], axis=0, keepdims=True))
        dx_ref[...] = dx.astype(BF16)
        dw = jnp.concatenate(rows + [jnp.zeros((HALO - CONV_WIDTH, 3 * A), F32)], axis=0)

        @pl.when(i == 0)
        def _():
            dw_ref[...] = dw

        @pl.when(i > 0)
        def _():
            dw_ref[...] += dw

    return pl.pallas_call(
        body, name="conv_bwd", grid=(nt,),
        in_specs=[pl.BlockSpec((tm, 3 * A), lambda i: (i, 0)),
                  pl.BlockSpec((HALO, 3 * A), _halo_prev(tm)),
                  pl.BlockSpec((tm, 3 * A), lambda i: (i, 0)),
                  pl.BlockSpec((HALO, 3 * A), lambda i: (jnp.minimum((i + 1) * (tm // HALO), T // HALO - 1), 0)),
                  pl.BlockSpec((CONV_WIDTH, 3 * A), lambda i: (0, 0)), ANY],
        out_specs=[pl.BlockSpec((tm, 3 * A), lambda i: (i, 0)),
                   pl.BlockSpec((HALO, 3 * A), lambda i: (0, 0))],
        out_shape=[jax.ShapeDtypeStruct(dproj.shape, dproj.dtype), jax.ShapeDtypeStruct((HALO, 3 * A), F32)],
        input_output_aliases={5: 0},
        compiler_params=_cparams(("arbitrary",)),
    )(proj_m, proj_m, dc, dc, conv_w, dproj)


PAIR = 2 * CHUNK_A


def _b(x):
    return x.astype(BF16)


@jax.custom_vjp
def _bdot(a, b):
    return _dot(_b(a), _b(b))


def _bdot_f(a, b):
    return _bdot(a, b), (a, b)


def _bdot_b(res, g):
    a, b = res
    return _dot_nt(_b(g), _b(b)), _dot_tn(_b(a), _b(g))


_bdot.defvjp(_bdot_f, _bdot_b)


@jax.custom_vjp
def _bdot_nt(a, b):
    return _dot_nt(_b(a), _b(b))


def _bdot_nt_f(a, b):
    return _bdot_nt(a, b), (a, b)


def _bdot_nt_b(res, g):
    a, b = res
    return _dot(_b(g), _b(b)), _dot_tn(_b(g), _b(a))


_bdot_nt.defvjp(_bdot_nt_f, _bdot_nt_b)


@jax.custom_vjp
def _bdot_tn(a, b):
    return _dot_tn(_b(a), _b(b))


def _bdot_tn_f(a, b):
    return _bdot_tn(a, b), (a, b)


def _bdot_tn_b(res, g):
    a, b = res
    return _dot_nt(_b(b), _b(g)), _dot(_b(a), _b(g))


_bdot_tn.defvjp(_bdot_tn_f, _bdot_tn_b)


def _mask_matmul(m, x):
    hi = _b(x)
    r = x - hi.astype(F32)
    mid = _b(r)
    lo = _b(r - mid.astype(F32))
    return (_dot(m, lo) + _dot(m, mid)) + _dot(m, hi)


@jax.custom_vjp
def _mask_dot(m, mt, x):
    return _mask_matmul(m, x)


def _mask_dot_f(m, mt, x):
    return _mask_matmul(m, x), (m, mt)


def _mask_dot_b(res, g):
    m, mt = res
    return jnp.zeros_like(m), jnp.zeros_like(mt), _mask_matmul(mt, g)


_mask_dot.defvjp(_mask_dot_f, _mask_dot_b)

HIGH = lax.Precision.HIGH


def _unit_lower_inverse(L):
    n = L.shape[-1]
    X = -L
    Q = X
    for _ in range(CHUNK_A.bit_length() - 2):
        X = _dot(_b(X), _b(X))
        Q = Q + X + _dot(_b(Q), _b(X))
    return (_iota((n, n), 0) == _iota((n, n), 1)).astype(F32) + Q


@jax.custom_vjp
def _known_inverse(L, P):
    return P


def _known_inverse_f(L, P):
    return P, P


def _known_inverse_b(P, g):
    n = P.shape[-1]
    Q = _b(P - (_iota((n, n), 0) == _iota((n, n), 1)).astype(F32))
    t = g + _dot_tn(Q, _b(g))
    return -(t + _dot_nt(_b(t), Q)), jnp.zeros_like(P)


_known_inverse.defvjp(_known_inverse_f, _known_inverse_b)


def _gdn_prep_fn(q, k, v, gb, bb, P_known=None):
    n = PAIR
    row, col = _iota((n, n), 0), _iota((n, n), 1)
    same = (row >= CHUNK_A) == (col >= CHUNK_A)
    incl = same & (row >= col)
    strict = same & (row > col)
    bc = lambda m: jnp.broadcast_to(_b(m.astype(F32)), q.shape[:1] + (n, n))
    tril, triu, ones = bc(incl), bc(same & (row <= col)), bc(same)
    gc = _mask_dot(tril, triu, gb)
    gl = _mask_dot(ones, ones, gb)
    decay = jnp.where(incl, jnp.exp(jnp.where(incl, gc - jnp.swapaxes(gc, 1, 2), 0.0)), 0.0)
    kb = k * bb
    vb = v * bb
    qs = q * (HEAD_DIM ** -0.5)
    L = jnp.where(strict, _bdot_nt(kb, k) * decay, 0.0)
    P = _unit_lower_inverse(L) if P_known is None else _known_inverse(L, P_known)
    egc = jnp.exp(gc)
    u = _bdot(P, vb)
    w = _bdot(P, kb * egc)
    attn = jnp.where(incl, _bdot_nt(qs, k) * decay, 0.0)
    qg = qs * egc
    kdec = k * jnp.exp(gl - gc)
    eg = jnp.exp(gl)
    if P_known is None:
        return u, w, qg, kdec, attn, eg, P
    return u, w, qg, kdec, attn, eg


def _gdn_chain_fn(S, qg, kdec, u, w, attn, eg):
    C = CHUNK_A
    a, b = (slice(None), slice(0, C)), (slice(None), slice(C, PAIR))
    cat = lambda xs: jnp.concatenate(xs, axis=1)
    vn_a = u[a] - _bdot(w[a], S)
    o_a = _bdot(qg[a], S) + _bdot(attn[a], cat([vn_a, jnp.zeros_like(vn_a)]))
    S1 = S * cat([eg[a], eg[a]]) + _bdot_tn(kdec[a], vn_a)
    vn_b = u[b] - _bdot(w[b], S1)
    o_b = _bdot(qg[b], S1) + _bdot(attn[b], cat([vn_a, vn_b]))
    S2 = S1 * cat([eg[b], eg[b]]) + _bdot_tn(kdec[b], vn_b)
    return cat([o_a, o_b]), S2


def _gdn_prep(q, k, v, gb, bb):
    H, T, _ = q.shape
    pb = _pick(T // PAIR, (8, 4, 2, 1))
    hs = pl.BlockSpec((1, PAIR * pb, HEAD_DIM), lambda h, n: (h, n, 0))
    hshape = jax.ShapeDtypeStruct((H, T, HEAD_DIM), F32)

    def body(q_ref, k_ref, v_ref, gb_ref, bb_ref, *out_refs):
        pairs = lambda ref: ref[0].reshape(pb, PAIR, HEAD_DIM)
        outs = _gdn_prep_fn(pairs(q_ref), pairs(k_ref), pairs(v_ref), pairs(gb_ref), pairs(bb_ref))
        for ref, val in zip(out_refs, outs):
            ref[0] = val.reshape(pb * PAIR, HEAD_DIM)

    return pl.pallas_call(
        body, name="gdn_prep", grid=(H, T // (PAIR * pb)),
        in_specs=[hs] * 5, out_specs=[hs] * 7, out_shape=[hshape] * 7,
        compiler_params=_cparams(("parallel", "parallel")),
    )(q, k, v, gb, bb)


def _gdn_prep_bwd(q, k, v, gb, bb, pinv, du, dw, dqg, dkd, dat, deg):
    H, T, _ = q.shape
    pb = _pick(T // PAIR, (8, 4, 2, 1))
    hs = pl.BlockSpec((1, PAIR * pb, HEAD_DIM), lambda h, n: (h, n, 0))
    hshape = jax.ShapeDtypeStruct((H, T, HEAD_DIM), F32)

    def body(*refs):
        in_refs, p_ref, ct_refs, out_refs = refs[:5], refs[5], refs[6:12], refs[12:]
        pairs = lambda ref: ref[0].reshape(pb, PAIR, HEAD_DIM)
        P = pairs(p_ref)
        _, vjp = jax.vjp(lambda *a: _gdn_prep_fn(*a, P_known=P), *[pairs(r) for r in in_refs])
        grads = vjp(tuple(pairs(r) for r in ct_refs))
        for ref, val in zip(out_refs, grads):
            ref[0] = val.reshape(pb * PAIR, HEAD_DIM)

    return pl.pallas_call(
        body, name="gdn_prep_bwd", grid=(H, T // (PAIR * pb)),
        in_specs=[hs] * 12, out_specs=[hs] * 5, out_shape=[hshape] * 5,
        compiler_params=_cparams(("parallel", "parallel")),
    )(q, k, v, gb, bb, pinv, du, dw, dqg, dkd, dat, deg)


def _gdn_chain(qg, kd, u, w, attn, eg):
    H, T, _ = qg.shape
    N = T // PAIR
    hs = pl.BlockSpec((H, PAIR, HEAD_DIM), lambda n: (0, n, 0))
    ss = pl.BlockSpec((1, H, HEAD_DIM, HEAD_DIM), lambda n: (n, 0, 0, 0))

    def body(qg_ref, kd_ref, u_ref, w_ref, at_ref, eg_ref, o_ref, sall_ref, s_ref):
        @pl.when(pl.program_id(0) == 0)
        def _():
            s_ref[...] = jnp.zeros_like(s_ref)

        S = s_ref[...]
        sall_ref[0] = S
        o, S2 = _gdn_chain_fn(S, qg_ref[...], kd_ref[...], u_ref[...], w_ref[...], at_ref[...], eg_ref[...])
        o_ref[...] = o
        s_ref[...] = S2

    return pl.pallas_call(
        body, name="gdn_chain", grid=(N,),
        in_specs=[hs] * 6, out_specs=[hs, ss],
        out_shape=[jax.ShapeDtypeStruct((H, T, HEAD_DIM), F32),
                   jax.ShapeDtypeStruct((N, H, HEAD_DIM, HEAD_DIM), F32)],
        scratch_shapes=[pltpu.VMEM((H, HEAD_DIM, HEAD_DIM), F32)],
        compiler_params=_cparams(("arbitrary",)),
    )(qg, kd, u, w, attn, eg)


def _gdn_chain_bwd(qg, kd, u, w, attn, eg, sall, do):
    H, T, _ = qg.shape
    N = T // PAIR
    hs = pl.BlockSpec((H, PAIR, HEAD_DIM), lambda n: (0, N - 1 - n, 0))
    ss = pl.BlockSpec((1, H, HEAD_DIM, HEAD_DIM), lambda n: (N - 1 - n, 0, 0, 0))
    hshape = jax.ShapeDtypeStruct((H, T, HEAD_DIM), F32)

    def body(qg_ref, kd_ref, u_ref, w_ref, at_ref, eg_ref, sall_ref, do_ref, *rest):
        out_refs, ds_ref = rest[:6], rest[6]

        @pl.when(pl.program_id(0) == 0)
        def _():
            ds_ref[...] = jnp.zeros_like(ds_ref)

        _, vjp = jax.vjp(_gdn_chain_fn, sall_ref[0], qg_ref[...], kd_ref[...], u_ref[...], w_ref[...],
                         at_ref[...], eg_ref[...])
        grads = vjp((do_ref[...], ds_ref[...]))
        ds_ref[...] = grads[0]
        for ref, val in zip(out_refs, grads[1:]):
            ref[...] = val

    return pl.pallas_call(
        body, name="gdn_chain_bwd", grid=(N,),
        in_specs=[hs] * 6 + [ss, hs], out_specs=[hs] * 6, out_shape=[hshape] * 6,
        scratch_shapes=[pltpu.VMEM((H, HEAD_DIM, HEAD_DIM), F32)],
        compiler_params=_cparams(("arbitrary",)),
    )(qg, kd, u, w, attn, eg, sall, do)


def _post_fn(ogs, za, hw):
    outs = []
    for h, o in enumerate(ogs):
        r = lax.rsqrt(jnp.mean(o * o, axis=-1, keepdims=True) + EPS)
        outs.append(o * r * hw * _silu(za[:, h * HEAD_DIM:(h + 1) * HEAD_DIM]))
    return jnp.concatenate(outs, axis=1)


def _gdn_post(og, proj_m, hw):
    H, T, _ = og.shape
    A = H * HEAD_DIM
    tm = _pick(T, (512, 256, 128))

    def body(og_ref, za_ref, hw_ref, o_ref, ot_ref):
        o = _post_fn(tuple(og_ref[h] for h in range(H)), za_ref[...], hw_ref[...])
        o_ref[...] = o.astype(BF16)
        ot_ref[...] = o.T.astype(BF16)

    return pl.pallas_call(
        body, name="gdn_post", grid=(T // tm,),
        in_specs=[pl.BlockSpec((H, tm, HEAD_DIM), lambda i: (0, i, 0)),
                  pl.BlockSpec((tm, A), lambda i: (i, ZA_BLOCK)),
                  pl.BlockSpec((1, HEAD_DIM), lambda i: (0, 0))],
        out_specs=[pl.BlockSpec((tm, A), lambda i: (i, 0)), pl.BlockSpec((A, tm), lambda i: (0, i))],
        out_shape=[jax.ShapeDtypeStruct((T, A), BF16), jax.ShapeDtypeStruct((A, T), BF16)],
        compiler_params=_cparams(("parallel",)),
    )(og, proj_m, hw)


def _gdn_post_bwd(og, proj_m, hw, d_o, dproj):
    H, T, _ = og.shape
    A = H * HEAD_DIM
    tm = _pick(T, (256, 128))

    def body(og_ref, za_ref, hw_ref, do_ref, _, dog_ref, dza_ref, dhw_ref):
        _, vjp = jax.vjp(_post_fn, tuple(og_ref[h] for h in range(H)), za_ref[...], hw_ref[...])
        dog, dza, dhw = vjp(do_ref[...])
        for h in range(H):
            dog_ref[h] = dog[h]
        dza_ref[...] = dza.astype(BF16)

        @pl.when(pl.program_id(0) == 0)
        def _():
            dhw_ref[...] = dhw

        @pl.when(pl.program_id(0) > 0)
        def _():
            dhw_ref[...] += dhw

    return pl.pallas_call(
        body, name="gdn_post_bwd", grid=(T // tm,),
        in_specs=[pl.BlockSpec((H, tm, HEAD_DIM), lambda i: (0, i, 0)),
                  pl.BlockSpec((tm, A), lambda i: (i, ZA_BLOCK)),
                  pl.BlockSpec((1, HEAD_DIM), lambda i: (0, 0)),
                  pl.BlockSpec((tm, A), lambda i: (i, 0)), ANY],
        out_specs=[pl.BlockSpec((H, tm, HEAD_DIM), lambda i: (0, i, 0)),
                   pl.BlockSpec((tm, A), lambda i: (i, ZA_BLOCK)),
                   pl.BlockSpec((1, HEAD_DIM), lambda i: (0, 0))],
        out_shape=[jax.ShapeDtypeStruct((H, T, HEAD_DIM), F32), jax.ShapeDtypeStruct(dproj.shape, dproj.dtype),
                   jax.ShapeDtypeStruct((1, HEAD_DIM), F32)],
        input_output_aliases={4: 1},
        compiler_params=_cparams(("arbitrary",)),
    )(og, proj_m, hw, d_o, dproj)


def _sgu_fn(ub, vb, zb, lw, lb, W, bbc):
    G = len(W)
    tm = ub.shape[0]
    mu = jnp.mean(vb, axis=-1, keepdims=True)
    xc = vb - mu
    var = jnp.mean(xc * xc, axis=-1, keepdims=True)
    vn = xc * lax.rsqrt(var + EPS) * lw + lb
    mask = _iota((CHUNK_B, CHUNK_B), 0) >= _iota((CHUNK_B, CHUNK_B), 1)
    cols = []
    for g in range(G):
        wm = jnp.where(mask, W[g], 0.0).astype(BF16)
        rows = []
        for c in range(tm // CHUNK_B):
            blk = vn[c * CHUNK_B:(c + 1) * CHUNK_B, g * HEAD_DIM:(g + 1) * HEAD_DIM].astype(BF16)
            rows.append(_dot(wm, blk) + bbc[g])
        cols.append(jnp.concatenate(rows, axis=0) if len(rows) > 1 else rows[0])
    s = jnp.concatenate(cols, axis=1)
    return ub * s * _silu(zb)


ZA_BLOCK = 6


def _sgu_cols(A, B):
    assert A == B
    return 3, 4, 5


def _sgu_fwd(proj_m, lw, lb, W, bbc, A):
    T = proj_m.shape[0]
    G = W.shape[0]
    B = G * HEAD_DIM
    tm = _pick(T, (256, 128))
    cu, cv, cz = _sgu_cols(A, B)

    def body(u_ref, v_ref, z_ref, lw_ref, lb_ref, w_ref, b_ref, o_ref, ot_ref):
        o = _sgu_fn(u_ref[...], v_ref[...], z_ref[...], lw_ref[...], lb_ref[...],
                    tuple(w_ref[g] for g in range(G)), tuple(b_ref[g] for g in range(G)))
        o_ref[...] = o.astype(BF16)
        ot_ref[...] = o.T.astype(BF16)

    row = pl.BlockSpec((1, B), lambda i: (0, 0))
    cube = pl.BlockSpec((G, CHUNK_B, CHUNK_B), lambda i: (0, 0, 0))
    return pl.pallas_call(
        body, name="sgu_fwd", grid=(T // tm,),
        in_specs=[pl.BlockSpec((tm, B), lambda i: (i, cu)), pl.BlockSpec((tm, B), lambda i: (i, cv)),
                  pl.BlockSpec((tm, B), lambda i: (i, cz)), row, row, cube, cube],
        out_specs=[pl.BlockSpec((tm, B), lambda i: (i, 0)), pl.BlockSpec((B, tm), lambda i: (0, i))],
        out_shape=[jax.ShapeDtypeStruct((T, B), BF16), jax.ShapeDtypeStruct((B, T), BF16)],
        compiler_params=_cparams(("parallel",)),
    )(proj_m, proj_m, proj_m, lw, lb, W, bbc)


def _sgu_bwd(proj_m, lw, lb, W, bbc, d_o, A, dproj):
    T = proj_m.shape[0]
    G = W.shape[0]
    B = G * HEAD_DIM
    tm = _pick(T, (256, 128))
    nt = T // tm
    cu, cv, cz = _sgu_cols(A, B)

    def body(u_ref, v_ref, z_ref, lw_ref, lb_ref, w_ref, b_ref, do_ref, _,
             dp_ref, dlw_ref, dlb_ref, dw_ref, db_ref, dbb_ref):
        _, vjp = jax.vjp(_sgu_fn, u_ref[...], v_ref[...], z_ref[...], lw_ref[...], lb_ref[...],
                         tuple(w_ref[g] for g in range(G)), tuple(b_ref[g] for g in range(G)))
        du, dv, dz, dlw, dlb, dW, dbb = vjp(do_ref[...])
        dW, dbb = jnp.stack(dW, axis=0), jnp.stack(dbb, axis=0)
        dp_ref[:, 0:B] = du.astype(BF16)
        dp_ref[:, B:2 * B] = dv.astype(BF16)
        dp_ref[:, 2 * B:3 * B] = dz.astype(BF16)
        i = pl.program_id(0)

        @pl.when(i == 0)
        def _():
            dlw_ref[...] = dlw
            dlb_ref[...] = dlb
            dw_ref[...] = dW
            dbb_ref[...] = dbb

        @pl.when(i > 0)
        def _():
            dlw_ref[...] += dlw
            dlb_ref[...] += dlb
            dw_ref[...] += dW
            dbb_ref[...] += dbb

        @pl.when(i == nt - 1)
        def _():
            db_ref[...] = jnp.sum(dbb_ref[...], axis=-1, keepdims=True)

    row = pl.BlockSpec((1, B), lambda i: (0, 0))
    cube = pl.BlockSpec((G, CHUNK_B, CHUNK_B), lambda i: (0, 0, 0))
    return pl.pallas_call(
        body, name="sgu_bwd", grid=(nt,),
        in_specs=[pl.BlockSpec((tm, B), lambda i: (i, cu)), pl.BlockSpec((tm, B), lambda i: (i, cv)),
                  pl.BlockSpec((tm, B), lambda i: (i, cz)), row, row, cube, cube,
                  pl.BlockSpec((tm, B), lambda i: (i, A // B)), ANY],
        out_specs=[pl.BlockSpec((tm, 3 * B), lambda i: (i, 1)), row, row, cube,
                   pl.BlockSpec((G, CHUNK_B, 1), lambda i: (0, 0, 0))],
        out_shape=[jax.ShapeDtypeStruct(dproj.shape, dproj.dtype), jax.ShapeDtypeStruct((1, B), F32),
                   jax.ShapeDtypeStruct((1, B), F32), jax.ShapeDtypeStruct((G, CHUNK_B, CHUNK_B), F32),
                   jax.ShapeDtypeStruct((G, CHUNK_B, 1), F32)],
        input_output_aliases={8: 0},
        scratch_shapes=[pltpu.VMEM((G, CHUNK_B, CHUNK_B), F32)],
        compiler_params=_cparams(("arbitrary",)),
    )(proj_m, proj_m, proj_m, lw, lb, W, bbc, d_o, dproj)


def _head_fn(mix, x, fw, tgt):
    h = x + mix
    y = _rms_fn(h, fw)
    e = y - tgt
    return 0.5 * jnp.sum(jnp.mean(e * e, axis=-1, keepdims=True), axis=0, keepdims=True)


def _out_proj_loss(oa, ob, wout, x, tgt, fw):
    T, A = oa.shape
    B = ob.shape[1]
    D = x.shape[1]
    tm = _pick(T, (256, 128))

    def body(oa_ref, ob_ref, w_ref, x_ref, t_ref, fw_ref, dh_ref, dhb_ref, loss_ref, dfw_ref):
        mix = _dot(oa_ref[...], w_ref[0:A, :]) + _dot(ob_ref[...], w_ref[A:A + B, :])
        xv, tv = x_ref[...], t_ref[...]
        loss, vjp = jax.vjp(lambda m, f: _head_fn(m, xv, f, tv), mix, fw_ref[...])
        dh, dfw = vjp(jnp.ones((1, 1), F32))
        dh_ref[...] = dh
        dhb_ref[...] = dh.astype(BF16)
        lrow = jnp.broadcast_to(loss, (1, LANES))

        @pl.when(pl.program_id(0) == 0)
        def _():
            loss_ref[...] = lrow
            dfw_ref[...] = dfw

        @pl.when(pl.program_id(0) > 0)
        def _():
            loss_ref[...] += lrow
            dfw_ref[...] += dfw

    tile = pl.BlockSpec((tm, D), lambda i: (i, 0))
    return pl.pallas_call(
        body, name="out_proj_loss", grid=(T // tm,),
        in_specs=[pl.BlockSpec((tm, A), lambda i: (i, 0)), pl.BlockSpec((tm, B), lambda i: (i, 0)),
                  pl.BlockSpec((A + B, D), lambda i: (0, 0)), tile, tile,
                  pl.BlockSpec((1, D), lambda i: (0, 0))],
        out_specs=[tile, tile, pl.BlockSpec((1, LANES), lambda i: (0, 0)),
                   pl.BlockSpec((1, D), lambda i: (0, 0))],
        out_shape=[jax.ShapeDtypeStruct((T, D), F32), jax.ShapeDtypeStruct((T, D), BF16),
                   jax.ShapeDtypeStruct((1, LANES), F32), jax.ShapeDtypeStruct((1, D), F32)],
        compiler_params=_cparams(("arbitrary",)),
    )(oa, ob, wout, x, tgt, fw)


def _adamw(w, g, m, v, name):
    R, Cn = w.shape
    cap = max(8, 512 * 1024 // Cn)
    tr = max(t for t in range(8, min(R, cap) + 1, 8) if R % t == 0) if R > cap else R

    def body(w_ref, g_ref, m_ref, v_ref, d_ref, mo_ref, vo_ref):
        g = g_ref[...]
        m = ADAM_B1 * m_ref[...] + (1.0 - ADAM_B1) * g
        v = ADAM_B2 * v_ref[...] + (1.0 - ADAM_B2) * jnp.square(g)
        m_hat = m / (1.0 - ADAM_B1 ** ADAM_STEP)
        v_hat = v / (1.0 - ADAM_B2 ** ADAM_STEP)
        d_ref[...] = -ADAM_LR * (m_hat / (jnp.sqrt(v_hat) + ADAM_EPS) + ADAM_WD * w_ref[...])
        mo_ref[...] = m
        vo_ref[...] = v

    tile = pl.BlockSpec((tr, Cn), lambda i: (i, 0))
    shape = jax.ShapeDtypeStruct((R, Cn), F32)
    return pl.pallas_call(
        body, name=name, grid=(R // tr,), in_specs=[tile] * 4, out_specs=[tile] * 3,
        out_shape=[shape] * 3, compiler_params=_cparams(("parallel",)),
    )(w, g, m, v)


def _place():
    x, y, c = lax.axis_index("x"), lax.axis_index("y"), lax.axis_index("c")
    others = [(1 - x, y), (x, 1 - y), (1 - x, 1 - y)]
    return x, y, c, others


def _chip_index(px, py):
    return 2 * px + py


ANY = pl.BlockSpec(memory_space=pl.ANY)


def _gather_ride(blocks, split):
    n = len(blocks)

    def copies(in_refs, out_refs, send_sems, recv_sems):
        x, y, c, others = _place()
        me = _chip_index(x, y)

        def copy(sem, src, dst, to):
            return pltpu.make_async_remote_copy(src_ref=src, dst_ref=dst, send_sem=send_sems.at[sem],
                                                recv_sem=recv_sems.at[sem], device_id=to, device_id_type=MESH_ID)

        def part(a, chip, core):
            if not split[a]:
                return out_refs[a].at[chip]
            h = blocks[a].shape[0] // 2
            return out_refs[a].at[chip, pl.ds(core * h, h), :]

        def mine(a):
            if not split[a]:
                return in_refs[a]
            h = blocks[a].shape[0] // 2
            return in_refs[a].at[pl.ds(c * h, h), :]

        pairs = [(a, j, chip) for j, chip in enumerate(others) for a in range(n)]
        k = lambda chip: _chip_index(*chip)
        send = lambda a, j, chip: copy(3 * a + j, mine(a), part(a, me, c), (*chip, c))
        arrival = lambda a, j, chip: copy(3 * a + j, part(a, k(chip), c), part(a, k(chip), c), (*chip, c))
        passing = lambda a, j, chip: copy(3 * (n + a) + j, part(a, k(chip), c), part(a, k(chip), c), (x, y, 1 - c))
        passed = lambda a, j, chip: copy(3 * (n + a) + j, part(a, k(chip), 1 - c), part(a, k(chip), 1 - c),
                                         (x, y, 1 - c))
        return pairs, send, arrival, passing, passed

    def start(in_refs, out_refs, send_sems, recv_sems):
        pairs, send, _, _, _ = copies(in_refs, out_refs, send_sems, recv_sems)
        for p in pairs:
            send(*p).start()

    def finish(in_refs, out_refs, send_sems, recv_sems):
        pairs, send, arrival, passing, passed = copies(in_refs, out_refs, send_sems, recv_sems)
        for p in pairs:
            arrival(*p).wait_recv()
            if split[p[0]]:
                passing(*p).start()
        for p in pairs:
            if split[p[0]]:
                passed(*p).wait_recv()
        for p in pairs:
            send(*p).wait_send()
            if split[p[0]]:
                passing(*p).wait_send()

    shapes = [jax.ShapeDtypeStruct((N_CHIPS,) + b.shape, b.dtype) for b in blocks]
    return _Ride(blocks, shapes, 6 * n, start, finish)


def _put_own(gathered, own):
    me = _chip_index(lax.axis_index("x"), lax.axis_index("y"))
    return lax.dynamic_update_index_in_dim(gathered, own, me, 0)


def _run_ride(ride, name):
    def body(*refs):
        n_in, n_out = len(ride.operands), len(ride.out_shape)
        ins, outs, (send, recv) = refs[:n_in], refs[n_in:n_in + n_out], refs[n_in + n_out:]
        ride.start(ins, outs, send, recv)
        ride.finish(ins, outs, send, recv)

    return pl.pallas_call(
        body, name=name, in_specs=[ANY] * len(ride.operands), out_specs=[ANY] * len(ride.out_shape),
        out_shape=ride.out_shape,
        scratch_shapes=[pltpu.SemaphoreType.DMA((ride.n_sems,)), pltpu.SemaphoreType.DMA((ride.n_sems,))],
        compiler_params=pltpu.CompilerParams(has_side_effects=True),
    )(*ride.operands)


def _allreduce_small(buf, ride):
    R, L = buf.shape
    n_xin, n_xout = len(ride.operands), len(ride.out_shape)

    def body(in_ref, *refs):
        xins, out_ref, refs = refs[:n_xin], refs[n_xin], refs[n_xin + 1:]
        xouts, (sib_ref, pair_ref, chips_ref, send_sems, recv_sems, xsend, xrecv) = refs[:n_xout], refs[n_xout:]
        ride.start(xins, xouts, xsend, xrecv)
        x, y, c, others = _place()
        me = _chip_index(x, y)
        sibling = (x, y, 1 - c)
        cp = pltpu.make_async_remote_copy(src_ref=in_ref, dst_ref=sib_ref, send_sem=send_sems.at[0],
                                          recv_sem=recv_sems.at[0], device_id=sibling, device_id_type=MESH_ID)
        cp.start()
        cp.wait()
        pair_ref[...] = in_ref[...] + sib_ref[...]
        sends = []
        for j, chip in enumerate(others):
            s = pltpu.make_async_remote_copy(src_ref=pair_ref, dst_ref=chips_ref.at[me],
                                             send_sem=send_sems.at[1 + j], recv_sem=recv_sems.at[1 + j],
                                             device_id=(*chip, c), device_id_type=MESH_ID)
            s.start()
            sends.append(s)
        chips_ref[me] = pair_ref[...]
        for j, chip in enumerate(others):
            k = _chip_index(*chip)
            pltpu.make_async_remote_copy(src_ref=pair_ref, dst_ref=chips_ref.at[k], send_sem=send_sems.at[1 + j],
                                         recv_sem=recv_sems.at[1 + j], device_id=(*chip, c),
                                         device_id_type=MESH_ID).wait_recv()
        for s in sends:
            s.wait_send()
        out_ref[...] = ((chips_ref[0] + chips_ref[1]) + chips_ref[2]) + chips_ref[3]
        ride.finish(xins, xouts, xsend, xrecv)

    vm = pl.BlockSpec(memory_space=pltpu.VMEM)
    dma = pltpu.SemaphoreType.DMA
    res = pl.pallas_call(
        body, name="allreduce_small", in_specs=[vm] + [ANY] * n_xin, out_specs=[vm] + [ANY] * n_xout,
        out_shape=[jax.ShapeDtypeStruct((R, L), F32)] + ride.out_shape,
        scratch_shapes=[pltpu.VMEM((R, L), F32), pltpu.VMEM((R, L), F32), pltpu.VMEM((N_CHIPS, R, L), F32),
                        dma((4,)), dma((4,)), dma((ride.n_sems,)), dma((ride.n_sems,))],
        compiler_params=pltpu.CompilerParams(vmem_limit_bytes=VMEM_LIMIT, has_side_effects=True),
    )(buf, *ride.operands)
    return res[0], list(res[1:])


def _pair_ride(g):
    nb, R, Cn = g.shape
    h = R // 2

    def copy(in_refs, out_refs, send_sems, recv_sems):
        x, y, c, _ = _place()
        return pltpu.make_async_remote_copy(src_ref=in_refs[0].at[:, pl.ds((1 - c) * h, h), :], dst_ref=out_refs[0],
                                            send_sem=send_sems.at[0], recv_sem=recv_sems.at[0],
                                            device_id=(x, y, 1 - c), device_id_type=MESH_ID)

    return _Ride([g], [jax.ShapeDtypeStruct((nb, h, Cn), g.dtype)], 1,
                 lambda *refs: copy(*refs).start(), lambda *refs: copy(*refs).wait())


def _pair_sum(g, land, c_arr, name, ride=None):
    nb, R, Cn = g.shape
    hr = R // 2
    tr = _pick(hr, (256, 128, 64, 32, 16))
    nt = hr // tr

    def body(c_ref, g_ref, l_ref, o_ref):
        o_ref[...] = (g_ref[...] + l_ref[...]).astype(BF16)

    return _pallas(
        body, (c_arr, g, land), name=name, prefetch=1, grid=(nb, nt),
        in_specs=[pl.BlockSpec((1, tr, Cn), lambda b, i, c_ref: (b, c_ref[0] * nt + i, 0)),
                  pl.BlockSpec((1, tr, Cn), lambda b, i, c_ref: (b, i, 0))],
        out_specs=pl.BlockSpec((1, tr, Cn), lambda b, i, c_ref: (b, i, 0)),
        out_shape=jax.ShapeDtypeStruct((nb, hr, Cn), BF16),
        semantics=("parallel", "parallel"), ride=ride)


def _chip_ride(parts):
    m = len(parts)

    def copies(in_refs, out_refs, send_sems, recv_sems):
        x, y, c, others = _place()
        me = _chip_index(x, y)
        def mk(j, chip, n, landing):
            k = _chip_index(*chip)
            return pltpu.make_async_remote_copy(
                src_ref=in_refs[n].at[k], dst_ref=out_refs[n].at[landing(k)], send_sem=send_sems.at[m * j + n],
                recv_sem=recv_sems.at[m * j + n], device_id=(*chip, c), device_id_type=MESH_ID)

        pairs = [(j, chip, n) for j, chip in enumerate(others) for n in range(m)]
        return pairs, (lambda *p: mk(*p, lambda k: me)), (lambda *p: mk(*p, lambda k: k))

    def start(*refs):
        pairs, send, _ = copies(*refs)
        for p in pairs:
            send(*p).start()

    def finish(*refs):
        pairs, send, arrival = copies(*refs)
        for p in pairs:
            arrival(*p).wait_recv()
        for p in pairs:
            send(*p).wait_send()

    return _Ride(parts, [jax.ShapeDtypeStruct(p.shape, p.dtype) for p in parts], 3 * m, start, finish)


def _put_own_slot(q, p):
    me = _chip_index(lax.axis_index("x"), lax.axis_index("y"))
    return lax.dynamic_update_index_in_dim(q, lax.dynamic_index_in_dim(p, me, 0, keepdims=False), me, 0)


def _chip_sum(q, c_arr, name):
    nb, hr, Cn = q.shape
    tr = _pick(hr, (256, 128, 64, 32, 16))
    nt = hr // tr

    def body(c_ref, q_ref, o_ref):
        f = lambda k: q_ref[k].astype(F32)
        o_ref[...] = ((f(0) + f(1)) + f(2)) + f(3)

    return _pallas(
        body, (c_arr, q), name=name, prefetch=1, grid=(nt,),
        in_specs=[pl.BlockSpec((nb, tr, Cn), lambda i, c_ref: (0, i, 0))],
        out_specs=pl.BlockSpec((tr, Cn), lambda i, c_ref: (c_ref[0] * nt + i, 0)),
        out_shape=jax.ShapeDtypeStruct((2 * hr, Cn), F32),
        semantics=("parallel",))


def _sibling_fill(fw, fo):
    def body(_, __, fw_ref, fo_ref, send_sems, recv_sems):
        x, y, c, _ = _place()
        copies = []
        for n, ref in enumerate((fw_ref, fo_ref)):
            h = ref.shape[0] // 2
            mine = ref.at[pl.ds(c * h, h), :]
            theirs = ref.at[pl.ds((1 - c) * h, h), :]
            mk = lambda src, dst: pltpu.make_async_remote_copy(
                src_ref=src, dst_ref=dst, send_sem=send_sems.at[n], recv_sem=recv_sems.at[n],
                device_id=(x, y, 1 - c), device_id_type=MESH_ID)
            send = mk(mine, mine)
            send.start()
            copies.append((send, mk(theirs, theirs)))
        for send, arrival in copies:
            arrival.wait_recv()
            send.wait_send()

    return pl.pallas_call(
        body, name="sibling_fill", in_specs=[ANY, ANY], out_specs=[ANY, ANY],
        out_shape=[jax.ShapeDtypeStruct(fw.shape, F32), jax.ShapeDtypeStruct(fo.shape, F32)],
        input_output_aliases={0: 0, 1: 1},
        scratch_shapes=[pltpu.SemaphoreType.DMA((2,)), pltpu.SemaphoreType.DMA((2,))],
        compiler_params=pltpu.CompilerParams(has_side_effects=True),
    )(fw, fo)


class _Layout:
    def __init__(self, H, G, nb, Cb):
        A, B = H * HEAD_DIM, G * HEAD_DIM
        self.n_main = 4 * A + 3 * B
        self.k = -(-(self.n_main + LANES) // WIN_BLOCK) * WIN_BLOCK
        cuts = [0, 3 * A, 4 * A, 4 * A + 2 * H, nb * Cb]
        starts = [0, 3 * A + 3 * B, self.n_main, 3 * A]
        self.pieces = []
        self.windows, self.runs = [], []
        for n in range(nb):
            segs = []
            for s in range(4):
                lo, hi = max(cuts[s], n * Cb), min(cuts[s + 1], (n + 1) * Cb)
                if lo < hi:
                    segs.append((starts[s] + lo - cuts[s], lo - n * Cb, hi - lo))
            self.pieces += [(own, n, col, ln) for own, col, ln in segs]
            blocks = sorted({b for own, _, ln in segs for b in range(own // WIN_BLOCK, (own + ln - 1) // WIN_BLOCK + 1)})
            self.windows.append(blocks)
            self.runs.append([(blocks.index(own // WIN_BLOCK) * WIN_BLOCK + own % WIN_BLOCK, ln)
                              for own, _, ln in segs])
        self.wb = max(len(b) for b in self.windows)
        self.table = [b + [b[-1]] * (self.wb - len(b)) for b in self.windows]
        self.pieces.sort()

    def to_own_order(self, g_in):
        D = g_in.shape[1]
        cols, at = [], 0
        for own, n, col, ln in self.pieces:
            if own > at:
                cols.append(jnp.zeros((D, own - at), g_in.dtype))
            cols.append(g_in[n, :, col:col + ln])
            at = own + ln
        if at < self.k:
            cols.append(jnp.zeros((D, self.k - at), g_in.dtype))
        return jnp.concatenate(cols, axis=1)

    def from_window(self, win, chip, Cb):
        pick = lambda runs: (lambda w: jnp.concatenate([w[:, c:c + ln] for c, ln in runs], axis=1))
        return lax.switch(chip, [pick(r) for r in self.runs], win)


def _device_step(x, tgt, norm_w, g_in, wout_b, conv_b, a_log, dt_bias, head_norm_w, sgu_ln_w, sgu_ln_b,
                 w_spatial, b_spatial, final_norm_w, c_arr):
    T, D = x.shape
    H = a_log.shape[1]
    A = H * HEAD_DIM
    G = w_spatial.shape[0]
    B = G * HEAD_DIM
    nb, _, Cb = g_in.shape
    Rb = wout_b.shape[0]
    lay = _Layout(H, G, nb, Cb)
    w_own = lay.to_own_order(g_in)
    alog_row = jnp.pad(a_log, ((0, 0), (H, LANES - 2 * H)))
    dtb_row = jnp.pad(dt_bias, ((0, 0), (H, LANES - 2 * H)))
    bbc = jnp.broadcast_to(b_spatial[:, :, None], (G, CHUNK_B, CHUNK_B))

    xn, xn_t = _rms_in(x, norm_w)
    proj_m, (g_out, g_conv) = _mm_nn(xn, w_own, F32, "in_proj", cols=(0, lay.n_main),
                                     ride=_gather_ride([wout_b, conv_b], [True, False]))
    wout = _put_own(g_out, wout_b).reshape(nb * Rb, D)
    conv_w = _put_own(g_conv, conv_b).transpose(1, 0, 2).reshape(CONV_WIDTH, nb * conv_b.shape[1])
    proj_ba = _mm_nn(xn, w_own, F32, "in_proj_ba", cols=(lay.n_main, LANES))
    q, k, v, gb, bb = _gdn_pre(proj_m, proj_ba, conv_w, alog_row, dtb_row, H)
    u, w, qg, kd, attn, eg, pinv = _gdn_prep(q, k, v, gb, bb)
    og, sall = _gdn_chain(qg, kd, u, w, attn, eg)
    oa, oa_t = _gdn_post(og, proj_m, head_norm_w)
    ob, ob_t = _sgu_fwd(proj_m, sgu_ln_w, sgu_ln_b, w_spatial, bbc, A)
    dh, dhb, loss_row, d_fnw = _out_proj_loss(oa, ob, wout, x, tgt, final_norm_w.reshape(1, D))

    d_o = _mm_nn(dhb, wout.T, F32, "out_proj_dx")
    dproj = lax.empty((T, lay.k), BF16)
    dproj, d_lw, d_lb, d_ws, d_bs = _sgu_bwd(proj_m, sgu_ln_w, sgu_ln_b, w_spatial, bbc, d_o, A, dproj)
    dog, dproj, d_hw = _gdn_post_bwd(og, proj_m, head_norm_w, d_o, dproj)
    dqg, dkd, du, dw, dat, deg = _gdn_chain_bwd(qg, kd, u, w, attn, eg, sall, dog)
    dq, dk, dv, dgb, dbb = _gdn_prep_bwd(q, k, v, gb, bb, pinv, du, dw, dqg, dkd, dat, deg)
    dc, dproj, d_al, d_dt = _gdn_pre_bwd(proj_m, proj_ba, conv_w, alog_row, dtb_row, dq, dk, dv, dgb, dbb, H,
                                         dproj)
    dproj, d_conv = _conv_bwd(proj_m, dc, conv_w, H, dproj)

    table = jnp.array([b for row in lay.table for b in row], jnp.int32)
    d_win = _mm_windows(xn_t, dproj, table, nb, "in_proj_dw")
    d_wout, (land_w,) = _mm_nn_pair(oa_t, ob_t, dhb, "out_proj_dw", ride=_pair_ride(d_win))
    d_wout = d_wout.reshape(nb, Rb, D)
    pair_w, (land_o,) = _pair_sum(d_win, land_w, c_arr, "pair_sum_w_in", ride=_pair_ride(d_wout))
    pair_o = _pair_sum(d_wout, land_o, c_arr, "pair_sum_w_out")
    dxn, (all_w,) = _mm_nt_rhs_outer(dproj, w_own, F32, "in_proj_dx", ride=_chip_ride([pair_w]))
    grad_x, d_nw = _rms_in_bwd(x, norm_w, dxn, dh)
    all_w = _put_own_slot(all_w, pair_w)
    small = dict(norm_w=d_nw, conv_w=d_conv[:CONV_WIDTH], a_log=d_al[:, H:2 * H], dt_bias=d_dt[:, H:2 * H],
                 head_norm_w=d_hw, sgu_ln_w=d_lw, sgu_ln_b=d_lb, w_spatial=d_ws, b_spatial=d_bs[:, :, 0],
                 final_norm_w=d_fnw)
    return loss_row, grad_x, small, all_w, pair_o


SMALL = ("norm_w", "conv_w", "a_log", "dt_bias", "head_norm_w", "sgu_ln_w", "sgu_ln_b", "w_spatial",
         "b_spatial", "final_norm_w")


def _pack(parts):
    rows = []
    for p in parts:
        f = p.reshape(-1)
        f = jnp.pad(f, (0, (-f.shape[0]) % (8 * LANES)))
        rows.append(f.reshape(-1, LANES))
    return jnp.concatenate(rows, axis=0)


def _unpack(buf, shapes):
    out, r = [], 0
    for s in shapes:
        n = 1
        for d in s:
            n *= d
        nr = -(-n // (8 * LANES)) * 8
        out.append(buf[r:r + nr].reshape(-1)[:n].reshape(s))
        r += nr
    return out


def kernel(x, norm_w, w_in, conv_w, a_log, dt_bias, head_norm_w, sgu_ln_w, sgu_ln_b, w_spatial, b_spatial, w_out, final_norm_w, loss_target, m_norm_w, m_w_in, m_conv_w, m_a_log, m_dt_bias, m_head_norm_w, m_sgu_ln_w, m_sgu_ln_b, m_w_spatial, m_b_spatial, m_w_out, m_final_norm_w, v_norm_w, v_w_in, v_conv_w, v_a_log, v_dt_bias, v_head_norm_w, v_sgu_ln_w, v_sgu_ln_b, v_w_spatial, v_b_spatial, v_w_out, v_final_norm_w):
    T, D = x.shape[1], x.shape[2]
    weights = dict(norm_w=norm_w, w_in=w_in, conv_w=conv_w, a_log=a_log, dt_bias=dt_bias, head_norm_w=head_norm_w,
                   sgu_ln_w=sgu_ln_w, sgu_ln_b=sgu_ln_b, w_spatial=w_spatial, b_spatial=b_spatial, w_out=w_out,
                   final_norm_w=final_norm_w)
    mom_m = dict(norm_w=m_norm_w, w_in=m_w_in, conv_w=m_conv_w, a_log=m_a_log, dt_bias=m_dt_bias,
                 head_norm_w=m_head_norm_w, sgu_ln_w=m_sgu_ln_w, sgu_ln_b=m_sgu_ln_b, w_spatial=m_w_spatial,
                 b_spatial=m_b_spatial, w_out=m_w_out, final_norm_w=m_final_norm_w)
    mom_v = dict(norm_w=v_norm_w, w_in=v_w_in, conv_w=v_conv_w, a_log=v_a_log, dt_bias=v_dt_bias,
                 head_norm_w=v_head_norm_w, sgu_ln_w=v_sgu_ln_w, sgu_ln_b=v_sgu_ln_b, w_spatial=v_w_spatial,
                 b_spatial=v_b_spatial, w_out=v_w_out, final_norm_w=v_final_norm_w)
    me = _chip_index(lax.axis_index("x"), lax.axis_index("y"))
    c_arr = lax.axis_index("c").astype(jnp.int32).reshape(1)
    Din, Cb = w_in.shape[1], w_in.shape[2]
    Rb = w_out.shape[1]
    cconv = conv_w.shape[2]

    win_b = w_in[0].astype(BF16)
    g_in, = _run_ride(_gather_ride([win_b], [True]), "gather_w_in")
    g_in = _put_own(g_in, win_b)

    loss_row, grad_x, g, qw, pair_o = _device_step(
        x[0], loss_target[0], norm_w, g_in, w_out[0].astype(BF16), conv_w[0], a_log, dt_bias, head_norm_w,
        sgu_ln_w, sgu_ln_b, w_spatial[0], b_spatial[0], final_norm_w, c_arr)

    small_shapes = [tuple(g[n].shape) for n in SMALL]
    small, (qo,) = _allreduce_small(_pack([g[n] for n in SMALL]), _chip_ride([pair_o]))
    qo = _put_own_slot(qo, pair_o)
    gsum_in, gsum_out = _sibling_fill(_chip_sum(qw, c_arr, "chip_sum_w_in"), _chip_sum(qo, c_arr, "chip_sum_w_out"))
    gsum_in = _Layout(a_log.shape[1], w_spatial.shape[1], N_CHIPS, Cb).from_window(gsum_in, me, Cb)
    gsmall = dict(zip(SMALL, _unpack(small, small_shapes)))
    gsmall["conv_w"] = lax.dynamic_slice_in_dim(gsmall["conv_w"], me * cconv, cconv, axis=1)

    grads, deltas, new_m, new_v = {}, {}, {}, {}
    d, m2, v2 = _adamw(w_out[0], gsum_out, m_w_out[0], v_w_out[0], "adamw_w_out")
    grads["w_out"], deltas["w_out"], new_m["w_out"], new_v["w_out"] = gsum_out[None], d[None], m2[None], v2[None]
    flat = lambda a: a.transpose(2, 0, 1).reshape(-1, LANES)
    unflat = lambda f: f.reshape(Cb, 1, Din).transpose(1, 2, 0)
    g_flat = gsum_in.T.reshape(-1, LANES)
    d, m2, v2 = _adamw(flat(w_in), g_flat, flat(m_w_in), flat(v_w_in), "adamw_w_in")
    grads["w_in"], deltas["w_in"], new_m["w_in"], new_v["w_in"] = unflat(g_flat), unflat(d), unflat(m2), unflat(v2)
    shapes = [tuple(weights[n].shape) for n in SMALL]
    ds, ms, vs = _adamw(_pack([weights[n] for n in SMALL]), _pack([gsmall[n] for n in SMALL]),
                        _pack([mom_m[n] for n in SMALL]), _pack([mom_v[n] for n in SMALL]), "adamw_small")
    for n, gq, d, m2, v2 in zip(SMALL, [gsmall[n] for n in SMALL], _unpack(ds, shapes), _unpack(ms, shapes),
                                _unpack(vs, shapes)):
        grads[n], deltas[n], new_m[n], new_v[n] = gq.reshape(weights[n].shape), d, m2, v2

    loss = lax.psum(loss_row[0, 0], ("x", "y", "c"))
    order = ("norm_w", "w_in", "conv_w", "a_log", "dt_bias", "head_norm_w", "sgu_ln_w", "sgu_ln_b", "w_spatial",
             "b_spatial", "w_out", "final_norm_w")
    return (loss, grad_x[None], *[grads[n] for n in order], *[deltas[n] for n in order],
            *[new_m[n] for n in order], *[new_v[n] for n in order])
```

```python
import functools

import jax
import jax.numpy as jnp
from jax import lax
from jax.experimental import pallas as pl
from jax.experimental.pallas import tpu as pltpu

F32 = jnp.float32
BF16 = jnp.bfloat16
EPS = 1e-6
HEAD_DIM = 128
CHUNK_A = 64
CHUNK_B = 128
CONV_WIDTH = 4
LANES = 128
HALO = 8
N_CHIPS = 4
ADAM_LR = 0.001
ADAM_B1 = 0.9
ADAM_B2 = 0.999
ADAM_EPS = 1e-08
ADAM_WD = 0.01
ADAM_STEP = 10
VMEM_LIMIT = 56 * 1024 * 1024
MESH_ID = pl.DeviceIdType.MESH
HI = lax.Precision.HIGHEST


def _cparams(sem=None, **kw):
    return pltpu.CompilerParams(dimension_semantics=sem, vmem_limit_bytes=VMEM_LIMIT, **kw)


def _matmul(a, b, ca, cb, precision):
    nb = a.ndim - 2
    batch = tuple(range(nb))
    return lax.dot_general(a, b, (((ca + nb,), (cb + nb,)), (batch, batch)), precision=precision,
                           preferred_element_type=F32)


def _dot(a, b, hi=False, precision=None):
    return _matmul(a, b, 1, 0, HI if hi else precision)


def _dot_nt(a, b, hi=False, precision=None):
    return _matmul(a, b, 1, 1, HI if hi else precision)


def _dot_tn(a, b, hi=False, precision=None):
    return _matmul(a, b, 0, 0, HI if hi else precision)


def _iota(shape, dim):
    return lax.broadcasted_iota(jnp.int32, shape, dim)


def _sigmoid(x):
    return 0.5 * (jnp.tanh(0.5 * x) + 1.0)


def _silu(x):
    return x * _sigmoid(x)


def _softplus(x):
    z = jnp.exp(-jnp.abs(x))
    small = z * (1.0 - z * (0.5 - z * (1.0 / 3.0)))
    return jnp.maximum(x, 0.0) + jnp.where(z < 1e-3, small, jnp.log(1.0 + z))


def _pick(n, pref):
    for t in pref:
        if n % t == 0:
            return t
    return n


class _Ride:
    def __init__(self, operands, out_shape, n_sems, start, finish):
        self.operands, self.out_shape, self.n_sems = list(operands), list(out_shape), n_sems
        self.start, self.finish = start, finish


def _pallas(body, operands, *, name, grid, in_specs, out_specs, out_shape, semantics, scratch_shapes=(),
            prefetch=0, ride=None):
    single = not isinstance(out_shape, (list, tuple))
    outs = [out_shape] if single else list(out_shape)
    ospecs = [out_specs] if single else list(out_specs)
    in_specs, scratch = list(in_specs), list(scratch_shapes)
    n_in, n_out, n_sc = len(operands) - prefetch, len(outs), len(scratch)
    kernel = body
    params = _cparams(semantics)
    if ride is not None:
        n_xin, n_xout = len(ride.operands), len(ride.out_shape)

        def kernel(*refs):
            pre, refs = refs[:prefetch], refs[prefetch:]
            ins, refs = refs[:n_in], refs[n_in:]
            xins, refs = refs[:n_xin], refs[n_xin:]
            mains, refs = refs[:n_out], refs[n_out:]
            xouts, refs = refs[:n_xout], refs[n_xout:]
            sc, (send, recv) = refs[:n_sc], refs[n_sc:]
            ids = [pl.program_id(a) for a in range(len(grid))]
            first = functools.reduce(jnp.logical_and, [i == 0 for i in ids])
            last = functools.reduce(jnp.logical_and, [i == g - 1 for i, g in zip(ids, grid)])

            @pl.when(first)
            def _():
                ride.start(xins, xouts, send, recv)

            body(*pre, *ins, *mains, *sc)

            @pl.when(last)
            def _():
                ride.finish(xins, xouts, send, recv)

        operands = list(operands) + ride.operands
        in_specs += [ANY] * n_xin
        ospecs += [ANY] * n_xout
        outs += ride.out_shape
        scratch += [pltpu.SemaphoreType.DMA((ride.n_sems,)), pltpu.SemaphoreType.DMA((ride.n_sems,))]
        params = _cparams(("arbitrary",) * len(grid), has_side_effects=True)
    if prefetch:
        spec = dict(grid_spec=pltpu.PrefetchScalarGridSpec(
            num_scalar_prefetch=prefetch, grid=grid, in_specs=in_specs, out_specs=ospecs, scratch_shapes=scratch))
    else:
        spec = dict(grid=grid, in_specs=in_specs, out_specs=ospecs, scratch_shapes=scratch)
    res = pl.pallas_call(kernel, name=name, out_shape=outs, compiler_params=params, **spec)(*operands)
    main = res[0] if single else list(res[:n_out])
    return main if ride is None else (main, list(res[n_out:]))


def _mm_nn(a, b, out_dtype, name, tm=1024, tn=512, tk=None, cols=None, ride=None):
    M, K = a.shape
    c0, N = (0, b.shape[1]) if cols is None else cols
    tm = _pick(M, (tm, 512, 256, 128))
    tn = _pick(N, (tn, 512, 384, 256, 128))
    tk = K if tk is None else _pick(K, (tk,))
    nk = K // tk
    j0 = c0 // tn
    assert c0 % tn == 0

    def body(a_ref, b_ref, o_ref, *scratch):
        part = _dot(a_ref[...], b_ref[...])
        if nk == 1:
            o_ref[...] = part.astype(out_dtype)
        else:
            acc_ref, = scratch
            k = pl.program_id(2)

            @pl.when(k == 0)
            def _():
                acc_ref[...] = part

            @pl.when(k > 0)
            def _():
                acc_ref[...] += part

            @pl.when(k == nk - 1)
            def _():
                o_ref[...] = acc_ref[...].astype(out_dtype)

    return _pallas(
        body, (a, b), name=name, grid=(M // tm, N // tn, nk),
        in_specs=[pl.BlockSpec((tm, tk), lambda i, j, k: (i, k)),
                  pl.BlockSpec((tk, tn), lambda i, j, k: (k, j + j0))],
        out_specs=pl.BlockSpec((tm, tn), lambda i, j, k: (i, j)),
        out_shape=jax.ShapeDtypeStruct((M, N), out_dtype),
        scratch_shapes=[] if nk == 1 else [pltpu.VMEM((tm, tn), F32)],
        semantics=("parallel", "parallel", "arbitrary"), ride=ride)


def _mm_nt_rhs_outer(a, b, out_dtype, name, tm=256, tn=1024, ride=None):
    M, K = a.shape
    N, _ = b.shape
    tm = _pick(M, (tm, 128))
    tn = _pick(N, (tn, 512, 256, 128))

    def body(a_ref, b_ref, o_ref):
        o_ref[...] = _dot_nt(a_ref[...], b_ref[...]).astype(out_dtype)

    return _pallas(
        body, (a, b), name=name, grid=(N // tn, M // tm),
        in_specs=[pl.BlockSpec((tm, K), lambda j, i: (i, 0)),
                  pl.BlockSpec((tn, K), lambda j, i: (j, 0))],
        out_specs=pl.BlockSpec((tm, tn), lambda j, i: (i, j)),
        out_shape=jax.ShapeDtypeStruct((M, N), out_dtype),
        semantics=("parallel", "parallel"), ride=ride)


WIN_BLOCK = 256


def _mm_windows(a, b, table, nb, name, tm=2048):
    M, K = a.shape
    wb = table.shape[0] // nb
    tm = _pick(M, (tm, 1024, 512, 256, 128))

    def body(tab_ref, a_ref, b_ref, o_ref):
        o_ref[0] = _dot(a_ref[...], b_ref[...])

    return pl.pallas_call(
        body, name=name,
        grid_spec=pltpu.PrefetchScalarGridSpec(
            num_scalar_prefetch=1, grid=(nb, M // tm, wb),
            in_specs=[pl.BlockSpec((tm, K), lambda n, i, t, tab: (i, 0)),
                      pl.BlockSpec((K, WIN_BLOCK), lambda n, i, t, tab: (0, tab[n * wb + t]))],
            out_specs=pl.BlockSpec((1, tm, WIN_BLOCK), lambda n, i, t, tab: (n, i, t))),
        out_shape=jax.ShapeDtypeStruct((nb, M, wb * WIN_BLOCK), F32),
        compiler_params=_cparams(("parallel", "parallel", "arbitrary")),
    )(table, a, b)


def _mm_nn_pair(a0, a1, b, name, tm=512, tn=1024, ride=None):
    M, K = a0.shape
    _, N = b.shape
    tm = _pick(M, (tm, 256, 128))
    tn = _pick(N, (tn, 512, 256, 128))
    ni = M // tm

    def body(a0_ref, a1_ref, b_ref, o_ref):
        p = pl.program_id(0)

        @pl.when(p == 0)
        def _():
            o_ref[...] = _dot(a0_ref[...], b_ref[...])

        @pl.when(p == 1)
        def _():
            o_ref[...] = _dot(a1_ref[...], b_ref[...])

    return _pallas(
        body, (a0, a1, b), name=name, grid=(2, ni, N // tn),
        in_specs=[pl.BlockSpec((tm, K), lambda p, i, j: (i * (1 - p), 0)),
                  pl.BlockSpec((tm, K), lambda p, i, j: (i * p, 0)),
                  pl.BlockSpec((K, tn), lambda p, i, j: (0, j))],
        out_specs=pl.BlockSpec((tm, tn), lambda p, i, j: (p * ni + i, j)),
        out_shape=jax.ShapeDtypeStruct((2 * M, N), F32),
        semantics=("parallel", "parallel", "parallel"), ride=ride)


def _rms_fn(x, w):
    r = lax.rsqrt(jnp.mean(x * x, axis=-1, keepdims=True) + EPS)
    return x * r * w


def _rms_in(x, w, ride=None):
    T, D = x.shape
    tm = _pick(T, (512, 256, 128))

    def body(x_ref, w_ref, o_ref, ot_ref):
        xn = _rms_fn(x_ref[...], w_ref[...])
        o_ref[...] = xn.astype(BF16)
        ot_ref[...] = xn.T.astype(BF16)

    return _pallas(
        body, (x, w), name="rms_in", grid=(T // tm,),
        in_specs=[pl.BlockSpec((tm, D), lambda i: (i, 0)), pl.BlockSpec((1, D), lambda i: (0, 0))],
        out_specs=[pl.BlockSpec((tm, D), lambda i: (i, 0)), pl.BlockSpec((D, tm), lambda i: (0, i))],
        out_shape=[jax.ShapeDtypeStruct((T, D), BF16), jax.ShapeDtypeStruct((D, T), BF16)],
        semantics=("parallel",), ride=ride)


def _rms_in_bwd(x, w, dxn, dh, ride=None):
    T, D = x.shape
    tm = _pick(T, (256, 128))

    def body(x_ref, w_ref, dxn_ref, dh_ref, gx_ref, dw_ref):
        _, vjp = jax.vjp(_rms_fn, x_ref[...], w_ref[...])
        dx, dw = vjp(dxn_ref[...])
        gx_ref[...] = dh_ref[...] + dx

        @pl.when(pl.program_id(0) == 0)
        def _():
            dw_ref[...] = dw

        @pl.when(pl.program_id(0) > 0)
        def _():
            dw_ref[...] += dw

    tile = pl.BlockSpec((tm, D), lambda i: (i, 0))
    row = pl.BlockSpec((1, D), lambda i: (0, 0))
    return _pallas(
        body, (x, w, dxn, dh), name="rms_in_bwd", grid=(T // tm,),
        in_specs=[tile, row, tile, tile], out_specs=[tile, row],
        out_shape=[jax.ShapeDtypeStruct((T, D), F32), jax.ShapeDtypeStruct((1, D), F32)],
        semantics=("arbitrary",), ride=ride)


def _conv_fwd(cat_ref, halo, x, w):
    tm = x.shape[0]
    cat_ref[0:HALO, :] = halo
    cat_ref[HALO:HALO + tm, :] = x
    c = x * w[CONV_WIDTH - 1:CONV_WIDTH, :]
    for k in range(CONV_WIDTH - 1):
        s = CONV_WIDTH - 1 - k
        c = c + cat_ref[pl.ds(HALO - s, tm), :] * w[k:k + 1, :]
    return c


def _lane_to_all(x, lane):
    @jax.custom_vjp
    def f(x):
        return jnp.broadcast_to(x[:, lane:lane + 1], x.shape)

    def f_fwd(x):
        return f(x), None

    def f_bwd(_, g):
        return (jnp.where(_iota(g.shape, 1) == lane, jnp.sum(g, axis=-1, keepdims=True), 0.0),)

    f.defvjp(f_fwd, f_bwd)
    return f(x)


def _gdn_pointwise(c, ba, alog, dtb, H):
    A = H * HEAD_DIM
    s = _silu(c)
    beta = _sigmoid(ba)
    g = -jnp.exp(alog) * _softplus(ba + dtb)
    qs, ks, vs, gbs, bbs = [], [], [], [], []
    for h in range(H):
        lo = h * HEAD_DIM
        q = s[:, lo:lo + HEAD_DIM]
        k = s[:, A + lo:A + lo + HEAD_DIM]
        qs.append(q * lax.rsqrt(jnp.sum(q * q, axis=-1, keepdims=True) + EPS))
        ks.append(k * lax.rsqrt(jnp.sum(k * k, axis=-1, keepdims=True) + EPS))
        vs.append(s[:, 2 * A + lo:2 * A + lo + HEAD_DIM])
        bbs.append(_lane_to_all(beta, h))
        gbs.append(_lane_to_all(g, H + h))
    st = lambda xs: jnp.stack(xs, axis=0)
    return st(qs), st(ks), st(vs), st(gbs), st(bbs)


def _halo_prev(tm):
    return lambda i: (jnp.maximum(i * (tm // HALO) - 1, 0), 0)


def _gdn_pre(proj_m, proj_ba, conv_w, alog_row, dtb_row, H):
    T = proj_m.shape[0]
    A = H * HEAD_DIM
    tm = _pick(T, (256, 128))
    hs = pl.BlockSpec((H, tm, HEAD_DIM), lambda i: (0, i, 0))
    hshape = jax.ShapeDtypeStruct((H, T, HEAD_DIM), F32)

    def body(x_ref, halo_ref, ba_ref, w_ref, al_ref, dt_ref, q_ref, k_ref, v_ref, gb_ref, bb_ref, cat_ref):
        halo = jnp.where(pl.program_id(0) == 0, 0.0, halo_ref[...])
        c = _conv_fwd(cat_ref, halo, x_ref[...], w_ref[...])
        q, k, v, gb, bb = _gdn_pointwise(c, ba_ref[...], al_ref[...], dt_ref[...], H)
        q_ref[...] = q
        k_ref[...] = k
        v_ref[...] = v
        gb_ref[...] = gb
        bb_ref[...] = bb

    return pl.pallas_call(
        body, name="gdn_pre", grid=(T // tm,),
        in_specs=[pl.BlockSpec((tm, 3 * A), lambda i: (i, 0)),
                  pl.BlockSpec((HALO, 3 * A), _halo_prev(tm)),
                  pl.BlockSpec((tm, LANES), lambda i: (i, 0)),
                  pl.BlockSpec((CONV_WIDTH, 3 * A), lambda i: (0, 0)),
                  pl.BlockSpec((1, LANES), lambda i: (0, 0)),
                  pl.BlockSpec((1, LANES), lambda i: (0, 0))],
        out_specs=[hs] * 5, out_shape=[hshape] * 5,
        scratch_shapes=[pltpu.VMEM((HALO + tm, 3 * A), F32)],
        compiler_params=_cparams(("parallel",)),
    )(proj_m, proj_m, proj_ba, conv_w, alog_row, dtb_row)


def _gdn_pre_bwd(proj_m, proj_ba, conv_w, alog_row, dtb_row, dq, dk, dv, dgb, dbb, H, dproj):
    T, n_main = proj_m.shape
    A = H * HEAD_DIM
    tm = _pick(T, (256, 128))
    hs = pl.BlockSpec((H, tm, HEAD_DIM), lambda i: (0, i, 0))
    row = pl.BlockSpec((1, LANES), lambda i: (0, 0))

    def body(x_ref, halo_ref, ba_ref, w_ref, al_ref, dt_ref, dq_ref, dk_ref, dv_ref, dgb_ref, dbb_ref, _,
             dc_ref, dba_ref, dal_ref, ddt_ref, cat_ref):
        halo = jnp.where(pl.program_id(0) == 0, 0.0, halo_ref[...])
        c = _conv_fwd(cat_ref, halo, x_ref[...], w_ref[...])
        _, vjp = jax.vjp(functools.partial(_gdn_pointwise, H=H), c, ba_ref[...], al_ref[...], dt_ref[...])
        dc, dba, dal, ddt = vjp((dq_ref[...], dk_ref[...], dv_ref[...], dgb_ref[...], dbb_ref[...]))
        dc_ref[...] = dc
        dba_ref[:, :LANES] = dba.astype(BF16)
        dba_ref[:, LANES:] = jnp.zeros((tm, WIN_BLOCK - LANES), BF16)

        @pl.when(pl.program_id(0) == 0)
        def _():
            dal_ref[...] = dal
            ddt_ref[...] = ddt

        @pl.when(pl.program_id(0) > 0)
        def _():
            dal_ref[...] += dal
            ddt_ref[...] += ddt

    return pl.pallas_call(
        body, name="gdn_pre_bwd", grid=(T // tm,),
        in_specs=[pl.BlockSpec((tm, 3 * A), lambda i: (i, 0)),
                  pl.BlockSpec((HALO, 3 * A), _halo_prev(tm)),
                  pl.BlockSpec((tm, LANES), lambda i: (i, 0)),
                  pl.BlockSpec((CONV_WIDTH, 3 * A), lambda i: (0, 0)),
                  row, row, hs, hs, hs, hs, hs, ANY],
        out_specs=[pl.BlockSpec((tm, 3 * A), lambda i: (i, 0)),
                   pl.BlockSpec((tm, WIN_BLOCK), lambda i: (i, n_main // WIN_BLOCK)), row, row],
        out_shape=[jax.ShapeDtypeStruct((T, 3 * A), F32), jax.ShapeDtypeStruct(dproj.shape, dproj.dtype),
                   jax.ShapeDtypeStruct((1, LANES), F32), jax.ShapeDtypeStruct((1, LANES), F32)],
        input_output_aliases={11: 1},
        scratch_shapes=[pltpu.VMEM((HALO + tm, 3 * A), F32)],
        compiler_params=_cparams(("arbitrary",)),
    )(proj_m, proj_m, proj_ba, conv_w, alog_row, dtb_row, dq, dk, dv, dgb, dbb, dproj)


def _conv_bwd(proj_m, dc, conv_w, H, dproj):
    T = proj_m.shape[0]
    A = H * HEAD_DIM
    tm = _pick(T, (256, 128))
    nt = T // tm

    def body(x_ref, halo_ref, dc_ref, nxt_ref, w_ref, _, dx_ref, dw_ref):
        i = pl.program_id(0)
        halo = jnp.where(i == 0, 0.0, halo_ref[...])
        xcat = jnp.concatenate([halo, x_ref[...]], axis=0)
        nxt = jnp.where(i == nt - 1, 0.0, nxt_ref[...])
        dc = dc_ref[...]
        dcat = jnp.concatenate([dc, nxt], axis=0)
        w = w_ref[...]
        dx = None
        rows = []
        for k in range(CONV_WIDTH):
            s = CONV_WIDTH - 1 - k
            ds = dcat if s == 0 else pltpu.roll(dcat, tm + HALO - s, 0)
            term = ds[:tm, :] * w[k:k + 1, :]
            dx = term if dx is None else dx + term
            xs = xcat if s == 0 else pltpu.roll(xcat, s, 0)
            rows.append(jnp.sum(dc * xs[HALO:, :], axis=0, keepdims=True))
        dx_ref[...] = dx.astype(BF16)
        dw = jnp.concatenate(rows + [jnp.zeros((HALO - CONV_WIDTH, 3 * A), F32)], axis=0)

        @pl.when(i == 0)
        def _():
            dw_ref[...] = dw

        @pl.when(i > 0)
        def _():
            dw_ref[...] += dw

    return pl.pallas_call(
        body, name="conv_bwd", grid=(nt,),
        in_specs=[pl.BlockSpec((tm, 3 * A), lambda i: (i, 0)),
                  pl.BlockSpec((HALO, 3 * A), _halo_prev(tm)),
                  pl.BlockSpec((tm, 3 * A), lambda i: (i, 0)),
                  pl.BlockSpec((HALO, 3 * A), lambda i: (jnp.minimum((i + 1) * (tm // HALO), T // HALO - 1), 0)),
                  pl.BlockSpec((CONV_WIDTH, 3 * A), lambda i: (0, 0)), ANY],
        out_specs=[pl.BlockSpec((tm, 3 * A), lambda i: (i, 0)),
                   pl.BlockSpec((HALO, 3 * A), lambda i: (0, 0))],
        out_shape=[jax.ShapeDtypeStruct(dproj.shape, dproj.dtype), jax.ShapeDtypeStruct((HALO, 3 * A), F32)],
        input_output_aliases={5: 0},
        compiler_params=_cparams(("arbitrary",)),
    )(proj_m, proj_m, dc, dc, conv_w, dproj)


PAIR = 2 * CHUNK_A


def _b(x):
    return x.astype(BF16)


@jax.custom_vjp
def _bdot(a, b):
    return _dot(_b(a), _b(b))


def _bdot_f(a, b):
    return _bdot(a, b), (a, b)


def _bdot_b(res, g):
    a, b = res
    return _dot_nt(_b(g), _b(b)), _dot_tn(_b(a), _b(g))


_bdot.defvjp(_bdot_f, _bdot_b)


@jax.custom_vjp
def _bdot_nt(a, b):
    return _dot_nt(_b(a), _b(b))


def _bdot_nt_f(a, b):
    return _bdot_nt(a, b), (a, b)


def _bdot_nt_b(res, g):
    a, b = res
    return _dot(_b(g), _b(b)), _dot_tn(_b(g), _b(a))


_bdot_nt.defvjp(_bdot_nt_f, _bdot_nt_b)


@jax.custom_vjp
def _bdot_tn(a, b):
    return _dot_tn(_b(a), _b(b))


def _bdot_tn_f(a, b):
    return _bdot_tn(a, b), (a, b)


def _bdot_tn_b(res, g):
    a, b = res
    return _dot_nt(_b(b), _b(g)), _dot(_b(a), _b(g))


_bdot_tn.defvjp(_bdot_tn_f, _bdot_tn_b)


def _mask_matmul(m, x):
    hi = _b(x)
    r = x - hi.astype(F32)
    mid = _b(r)
    lo = _b(r - mid.astype(F32))
    return (_dot(m, lo) + _dot(m, mid)) + _dot(m, hi)


@jax.custom_vjp
def _mask_dot(m, mt, x):
    return _mask_matmul(m, x)


def _mask_dot_f(m, mt, x):
    return _mask_matmul(m, x), (m, mt)


def _mask_dot_b(res, g):
    m, mt = res
    return jnp.zeros_like(m), jnp.zeros_like(mt), _mask_matmul(mt, g)


_mask_dot.defvjp(_mask_dot_f, _mask_dot_b)

HIGH = lax.Precision.HIGH


def _unit_lower_inverse(L):
    n = L.shape[-1]
    X = -L
    Q = X
    for _ in range(CHUNK_A.bit_length() - 2):
        X = _dot(_b(X), _b(X))
        Q = Q + X + _dot(_b(Q), _b(X))
    return (_iota((n, n), 0) == _iota((n, n), 1)).astype(F32) + Q


@jax.custom_vjp
def _known_inverse(L, P):
    return P


def _known_inverse_f(L, P):
    return P, P


def _known_inverse_b(P, g):
    n = P.shape[-1]
    Q = _b(P - (_iota((n, n), 0) == _iota((n, n), 1)).astype(F32))
    t = g + _dot_tn(Q, _b(g))
    return -(t + _dot_nt(_b(t), Q)), jnp.zeros_like(P)


_known_inverse.defvjp(_known_inverse_f, _known_inverse_b)


def _gdn_prep_fn(q, k, v, gb, bb, P_known=None):
    n = PAIR
    row, col = _iota((n, n), 0), _iota((n, n), 1)
    same = (row >= CHUNK_A) == (col >= CHUNK_A)
    incl = same & (row >= col)
    strict = same & (row > col)
    bc = lambda m: jnp.broadcast_to(_b(m.astype(F32)), q.shape[:1] + (n, n))
    tril, triu, ones = bc(incl), bc(same & (row <= col)), bc(same)
    gc = _mask_dot(tril, triu, gb)
    gl = _mask_dot(ones, ones, gb)
    decay = jnp.where(incl, jnp.exp(jnp.where(incl, gc - jnp.swapaxes(gc, 1, 2), 0.0)), 0.0)
    kb = k * bb
    vb = v * bb
    qs = q * (HEAD_DIM ** -0.5)
    L = jnp.where(strict, _bdot_nt(kb, k) * decay, 0.0)
    P = _unit_lower_inverse(L) if P_known is None else _known_inverse(L, P_known)
    egc = jnp.exp(gc)
    u = _bdot(P, vb)
    w = _bdot(P, kb * egc)
    attn = jnp.where(incl, _bdot_nt(qs, k) * decay, 0.0)
    qg = qs * egc
    kdec = k * jnp.exp(gl - gc)
    eg = jnp.exp(gl)
    if P_known is None:
        return u, w, qg, kdec, attn, eg, P
    return u, w, qg, kdec, attn, eg


def _gdn_chain_fn(S, qg, kdec, u, w, attn, eg):
    C = CHUNK_A
    a, b = (slice(None), slice(0, C)), (slice(None), slice(C, PAIR))
    cat = lambda xs: jnp.concatenate(xs, axis=1)
    vn_a = u[a] - _bdot(w[a], S)
    o_a = _bdot(qg[a], S) + _bdot(attn[a], cat([vn_a, jnp.zeros_like(vn_a)]))
    S1 = S * cat([eg[a], eg[a]]) + _bdot_tn(kdec[a], vn_a)
    vn_b = u[b] - _bdot(w[b], S1)
    o_b = _bdot(qg[b], S1) + _bdot(attn[b], cat([vn_a, vn_b]))
    S2 = S1 * cat([eg[b], eg[b]]) + _bdot_tn(kdec[b], vn_b)
    return cat([o_a, o_b]), S2


def _gdn_prep(q, k, v, gb, bb):
    H, T, _ = q.shape
    pb = _pick(T // PAIR, (8, 4, 2, 1))
    hs = pl.BlockSpec((1, PAIR * pb, HEAD_DIM), lambda h, n: (h, n, 0))
    hshape = jax.ShapeDtypeStruct((H, T, HEAD_DIM), F32)

    def body(q_ref, k_ref, v_ref, gb_ref, bb_ref, *out_refs):
        pairs = lambda ref: ref[0].reshape(pb, PAIR, HEAD_DIM)
        outs = _gdn_prep_fn(pairs(q_ref), pairs(k_ref), pairs(v_ref), pairs(gb_ref), pairs(bb_ref))
        for ref, val in zip(out_refs, outs):
            ref[0] = val.reshape(pb * PAIR, HEAD_DIM)

    return pl.pallas_call(
        body, name="gdn_prep", grid=(H, T // (PAIR * pb)),
        in_specs=[hs] * 5, out_specs=[hs] * 7, out_shape=[hshape] * 7,
        compiler_params=_cparams(("parallel", "parallel")),
    )(q, k, v, gb, bb)


def _gdn_prep_bwd(q, k, v, gb, bb, pinv, du, dw, dqg, dkd, dat, deg):
    H, T, _ = q.shape
    pb = _pick(T // PAIR, (8, 4, 2, 1))
    hs = pl.BlockSpec((1, PAIR * pb, HEAD_DIM), lambda h, n: (h, n, 0))
    hshape = jax.ShapeDtypeStruct((H, T, HEAD_DIM), F32)

    def body(*refs):
        in_refs, p_ref, ct_refs, out_refs = refs[:5], refs[5], refs[6:12], refs[12:]
        pairs = lambda ref: ref[0].reshape(pb, PAIR, HEAD_DIM)
        P = pairs(p_ref)
        _, vjp = jax.vjp(lambda *a: _gdn_prep_fn(*a, P_known=P), *[pairs(r) for r in in_refs])
        grads = vjp(tuple(pairs(r) for r in ct_refs))
        for ref, val in zip(out_refs, grads):
            ref[0] = val.reshape(pb * PAIR, HEAD_DIM)

    return pl.pallas_call(
        body, name="gdn_prep_bwd", grid=(H, T // (PAIR * pb)),
        in_specs=[hs] * 12, out_specs=[hs] * 5, out_shape=[hshape] * 5,
        compiler_params=_cparams(("parallel", "parallel")),
    )(q, k, v, gb, bb, pinv, du, dw, dqg, dkd, dat, deg)


def _gdn_chain(qg, kd, u, w, attn, eg):
    H, T, _ = qg.shape
    N = T // PAIR
    hs = pl.BlockSpec((H, PAIR, HEAD_DIM), lambda n: (0, n, 0))
    ss = pl.BlockSpec((1, H, HEAD_DIM, HEAD_DIM), lambda n: (n, 0, 0, 0))

    def body(qg_ref, kd_ref, u_ref, w_ref, at_ref, eg_ref, o_ref, sall_ref, s_ref):
        @pl.when(pl.program_id(0) == 0)
        def _():
            s_ref[...] = jnp.zeros_like(s_ref)

        S = s_ref[...]
        sall_ref[0] = S
        o, S2 = _gdn_chain_fn(S, qg_ref[...], kd_ref[...], u_ref[...], w_ref[...], at_ref[...], eg_ref[...])
        o_ref[...] = o
        s_ref[...] = S2

    return pl.pallas_call(
        body, name="gdn_chain", grid=(N,),
        in_specs=[hs] * 6, out_specs=[hs, ss],
        out_shape=[jax.ShapeDtypeStruct((H, T, HEAD_DIM), F32),
                   jax.ShapeDtypeStruct((N, H, HEAD_DIM, HEAD_DIM), F32)],
        scratch_shapes=[pltpu.VMEM((H, HEAD_DIM, HEAD_DIM), F32)],
        compiler_params=_cparams(("arbitrary",)),
    )(qg, kd, u, w, attn, eg)


def _gdn_chain_bwd(qg, kd, u, w, attn, eg, sall, do):
    H, T, _ = qg.shape
    N = T // PAIR
    hs = pl.BlockSpec((H, PAIR, HEAD_DIM), lambda n: (0, N - 1 - n, 0))
    ss = pl.BlockSpec((1, H, HEAD_DIM, HEAD_DIM), lambda n: (N - 1 - n, 0, 0, 0))
    hshape = jax.ShapeDtypeStruct((H, T, HEAD_DIM), F32)

    def body(qg_ref, kd_ref, u_ref, w_ref, at_ref, eg_ref, sall_ref, do_ref, *rest):
        out_refs, ds_ref = rest[:6], rest[6]

        @pl.when(pl.program_id(0) == 0)
        def _():
            ds_ref[...] = jnp.zeros_like(ds_ref)

        _, vjp = jax.vjp(_gdn_chain_fn, sall_ref[0], qg_ref[...], kd_ref[...], u_ref[...], w_ref[...],
                         at_ref[...], eg_ref[...])
        grads = vjp((do_ref[...], ds_ref[...]))
        ds_ref[...] = grads[0]
        for ref, val in zip(out_refs, grads[1:]):
            ref[...] = val

    return pl.pallas_call(
        body, name="gdn_chain_bwd", grid=(N,),
        in_specs=[hs] * 6 + [ss, hs], out_specs=[hs] * 6, out_shape=[hshape] * 6,
        scratch_shapes=[pltpu.VMEM((H, HEAD_DIM, HEAD_DIM), F32)],
        compiler_params=_cparams(("arbitrary",)),
    )(qg, kd, u, w, attn, eg, sall, do)


def _post_fn(ogs, za, hw):
    outs = []
    for h, o in enumerate(ogs):
        r = lax.rsqrt(jnp.mean(o * o, axis=-1, keepdims=True) + EPS)
        outs.append(o * r * hw * _silu(za[:, h * HEAD_DIM:(h + 1) * HEAD_DIM]))
    return jnp.concatenate(outs, axis=1)


def _gdn_post(og, proj_m, hw):
    H, T, _ = og.shape
    A = H * HEAD_DIM
    tm = _pick(T, (512, 256, 128))

    def body(og_ref, za_ref, hw_ref, o_ref, ot_ref):
        o = _post_fn(tuple(og_ref[h] for h in range(H)), za_ref[...], hw_ref[...])
        o_ref[...] = o.astype(BF16)
        ot_ref[...] = o.T.astype(BF16)

    return pl.pallas_call(
        body, name="gdn_post", grid=(T // tm,),
        in_specs=[pl.BlockSpec((H, tm, HEAD_DIM), lambda i: (0, i, 0)),
                  pl.BlockSpec((tm, A), lambda i: (i, ZA_BLOCK)),
                  pl.BlockSpec((1, HEAD_DIM), lambda i: (0, 0))],
        out_specs=[pl.BlockSpec((tm, A), lambda i: (i, 0)), pl.BlockSpec((A, tm), lambda i: (0, i))],
        out_shape=[jax.ShapeDtypeStruct((T, A), BF16), jax.ShapeDtypeStruct((A, T), BF16)],
        compiler_params=_cparams(("parallel",)),
    )(og, proj_m, hw)


def _gdn_post_bwd(og, proj_m, hw, d_o, dproj):
    H, T, _ = og.shape
    A = H * HEAD_DIM
    tm = _pick(T, (256, 128))

    def body(og_ref, za_ref, hw_ref, do_ref, _, dog_ref, dza_ref, dhw_ref):
        _, vjp = jax.vjp(_post_fn, tuple(og_ref[h] for h in range(H)), za_ref[...], hw_ref[...])
        dog, dza, dhw = vjp(do_ref[...])
        for h in range(H):
            dog_ref[h] = dog[h]
        dza_ref[...] = dza.astype(BF16)

        @pl.when(pl.program_id(0) == 0)
        def _():
            dhw_ref[...] = dhw

        @pl.when(pl.program_id(0) > 0)
        def _():
            dhw_ref[...] += dhw

    return pl.pallas_call(
        body, name="gdn_post_bwd", grid=(T // tm,),
        in_specs=[pl.BlockSpec((H, tm, HEAD_DIM), lambda i: (0, i, 0)),
                  pl.BlockSpec((tm, A), lambda i: (i, ZA_BLOCK)),
                  pl.BlockSpec((1, HEAD_DIM), lambda i: (0, 0)),
                  pl.BlockSpec((tm, A), lambda i: (i, 0)), ANY],
        out_specs=[pl.BlockSpec((H, tm, HEAD_DIM), lambda i: (0, i, 0)),
                   pl.BlockSpec((tm, A), lambda i: (i, ZA_BLOCK)),
                   pl.BlockSpec((1, HEAD_DIM), lambda i: (0, 0))],
        out_shape=[jax.ShapeDtypeStruct((H, T, HEAD_DIM), F32), jax.ShapeDtypeStruct(dproj.shape, dproj.dtype),
                   jax.ShapeDtypeStruct((1, HEAD_DIM), F32)],
        input_output_aliases={4: 1},
        compiler_params=_cparams(("arbitrary",)),
    )(og, proj_m, hw, d_o, dproj)


def _sgu_fn(ub, vb, zb, lw, lb, W, bbc):
    G = len(W)
    tm = ub.shape[0]
    mu = jnp.mean(vb, axis=-1, keepdims=True)
    xc = vb - mu
    var = jnp.mean(xc * xc, axis=-1, keepdims=True)
    vn = xc * lax.rsqrt(var + EPS) * lw + lb
    mask = _iota((CHUNK_B, CHUNK_B), 0) >= _iota((CHUNK_B, CHUNK_B), 1)
    cols = []
    for g in range(G):
        wm = jnp.where(mask, W[g], 0.0).astype(BF16)
        rows = []
        for c in range(tm // CHUNK_B):
            blk = vn[c * CHUNK_B:(c + 1) * CHUNK_B, g * HEAD_DIM:(g + 1) * HEAD_DIM].astype(BF16)
            rows.append(_dot(wm, blk) + bbc[g])
        cols.append(jnp.concatenate(rows, axis=0) if len(rows) > 1 else rows[0])
    s = jnp.concatenate(cols, axis=1)
    return ub * s * _silu(zb)


ZA_BLOCK = 6


def _sgu_cols(A, B):
    assert A == B
    return 3, 4, 5


def _sgu_fwd(proj_m, lw, lb, W, bbc, A):
    T = proj_m.shape[0]
    G = W.shape[0]
    B = G * HEAD_DIM
    tm = _pick(T, (256, 128))
    cu, cv, cz = _sgu_cols(A, B)

    def body(u_ref, v_ref, z_ref, lw_ref, lb_ref, w_ref, b_ref, o_ref, ot_ref):
        o = _sgu_fn(u_ref[...], v_ref[...], z_ref[...], lw_ref[...], lb_ref[...],
                    tuple(w_ref[g] for g in range(G)), tuple(b_ref[g] for g in range(G)))
        o_ref[...] = o.astype(BF16)
        ot_ref[...] = o.T.astype(BF16)

    row = pl.BlockSpec((1, B), lambda i: (0, 0))
    cube = pl.BlockSpec((G, CHUNK_B, CHUNK_B), lambda i: (0, 0, 0))
    return pl.pallas_call(
        body, name="sgu_fwd", grid=(T // tm,),
        in_specs=[pl.BlockSpec((tm, B), lambda i: (i, cu)), pl.BlockSpec((tm, B), lambda i: (i, cv)),
                  pl.BlockSpec((tm, B), lambda i: (i, cz)), row, row, cube, cube],
        out_specs=[pl.BlockSpec((tm, B), lambda i: (i, 0)), pl.BlockSpec((B, tm), lambda i: (0, i))],
        out_shape=[jax.ShapeDtypeStruct((T, B), BF16), jax.ShapeDtypeStruct((B, T), BF16)],
        compiler_params=_cparams(("parallel",)),
    )(proj_m, proj_m, proj_m, lw, lb, W, bbc)


def _sgu_bwd(proj_m, lw, lb, W, bbc, d_o, A, dproj):
    T = proj_m.shape[0]
    G = W.shape[0]
    B = G * HEAD_DIM
    tm = _pick(T, (256, 128))
    nt = T // tm
    cu, cv, cz = _sgu_cols(A, B)

    def body(u_ref, v_ref, z_ref, lw_ref, lb_ref, w_ref, b_ref, do_ref, _,
             dp_ref, dlw_ref, dlb_ref, dw_ref, db_ref, dbb_ref):
        _, vjp = jax.vjp(_sgu_fn, u_ref[...], v_ref[...], z_ref[...], lw_ref[...], lb_ref[...],
                         tuple(w_ref[g] for g in range(G)), tuple(b_ref[g] for g in range(G)))
        du, dv, dz, dlw, dlb, dW, dbb = vjp(do_ref[...])
        dW, dbb = jnp.stack(dW, axis=0), jnp.stack(dbb, axis=0)
        dp_ref[:, 0:B] = du.astype(BF16)
        dp_ref[:, B:2 * B] = dv.astype(BF16)
        dp_ref[:, 2 * B:3 * B] = dz.astype(BF16)
        i = pl.program_id(0)

        @pl.when(i == 0)
        def _():
            dlw_ref[...] = dlw
            dlb_ref[...] = dlb
            dw_ref[...] = dW
            dbb_ref[...] = dbb

        @pl.when(i > 0)
        def _():
            dlw_ref[...] += dlw
            dlb_ref[...] += dlb
            dw_ref[...] += dW
            dbb_ref[...] += dbb

        @pl.when(i == nt - 1)
        def _():
            db_ref[...] = jnp.sum(dbb_ref[...], axis=-1, keepdims=True)

    row = pl.BlockSpec((1, B), lambda i: (0, 0))
    cube = pl.BlockSpec((G, CHUNK_B, CHUNK_B), lambda i: (0, 0, 0))
    return pl.pallas_call(
        body, name="sgu_bwd", grid=(nt,),
        in_specs=[pl.BlockSpec((tm, B), lambda i: (i, cu)), pl.BlockSpec((tm, B), lambda i: (i, cv)),
                  pl.BlockSpec((tm, B), lambda i: (i, cz)), row, row, cube, cube,
                  pl.BlockSpec((tm, B), lambda i: (i, A // B)), ANY],
        out_specs=[pl.BlockSpec((tm, 3 * B), lambda i: (i, 1)), row, row, cube,
                   pl.BlockSpec((G, CHUNK_B, 1), lambda i: (0, 0, 0))],
        out_shape=[jax.ShapeDtypeStruct(dproj.shape, dproj.dtype), jax.ShapeDtypeStruct((1, B), F32),
                   jax.ShapeDtypeStruct((1, B), F32), jax.ShapeDtypeStruct((G, CHUNK_B, CHUNK_B), F32),
                   jax.ShapeDtypeStruct((G, CHUNK_B, 1), F32)],
        input_output_aliases={8: 0},
        scratch_shapes=[pltpu.VMEM((G, CHUNK_B, CHUNK_B), F32)],
        compiler_params=_cparams(("arbitrary",)),
    )(proj_m, proj_m, proj_m, lw, lb, W, bbc, d_o, dproj)


def _head_fn(mix, x, fw, tgt):
    h = x + mix
    y = _rms_fn(h, fw)
    e = y - tgt
    return 0.5 * jnp.sum(jnp.mean(e * e, axis=-1, keepdims=True), axis=0, keepdims=True)


def _out_proj_loss(oa, ob, wout, x, tgt, fw):
    T, A = oa.shape
    B = ob.shape[1]
    D = x.shape[1]
    tm = _pick(T, (256, 128))

    def body(oa_ref, ob_ref, w_ref, x_ref, t_ref, fw_ref, dh_ref, dhb_ref, loss_ref, dfw_ref):
        mix = _dot(oa_ref[...], w_ref[0:A, :]) + _dot(ob_ref[...], w_ref[A:A + B, :])
        xv, tv = x_ref[...], t_ref[...]
        loss, vjp = jax.vjp(lambda m, f: _head_fn(m, xv, f, tv), mix, fw_ref[...])
        dh, dfw = vjp(jnp.ones((1, 1), F32))
        dh_ref[...] = dh
        dhb_ref[...] = dh.astype(BF16)
        lrow = jnp.broadcast_to(loss, (1, LANES))

        @pl.when(pl.program_id(0) == 0)
        def _():
            loss_ref[...] = lrow
            dfw_ref[...] = dfw

        @pl.when(pl.program_id(0) > 0)
        def _():
            loss_ref[...] += lrow
            dfw_ref[...] += dfw

    tile = pl.BlockSpec((tm, D), lambda i: (i, 0))
    return pl.pallas_call(
        body, name="out_proj_loss", grid=(T // tm,),
        in_specs=[pl.BlockSpec((tm, A), lambda i: (i, 0)), pl.BlockSpec((tm, B), lambda i: (i, 0)),
                  pl.BlockSpec((A + B, D), lambda i: (0, 0)), tile, tile,
                  pl.BlockSpec((1, D), lambda i: (0, 0))],
        out_specs=[tile, tile, pl.BlockSpec((1, LANES), lambda i: (0, 0)),
                   pl.BlockSpec((1, D), lambda i: (0, 0))],
        out_shape=[jax.ShapeDtypeStruct((T, D), F32), jax.ShapeDtypeStruct((T, D), BF16),
                   jax.ShapeDtypeStruct((1, LANES), F32), jax.ShapeDtypeStruct((1, D), F32)],
        compiler_params=_cparams(("arbitrary",)),
    )(oa, ob, wout, x, tgt, fw)


def _adamw(w, g, m, v, name):
    R, Cn = w.shape
    cap = max(8, 512 * 1024 // Cn)
    tr = max(t for t in range(8, min(R, cap) + 1, 8) if R % t == 0) if R > cap else R

    def body(w_ref, g_ref, m_ref, v_ref, d_ref, mo_ref, vo_ref):
        g = g_ref[...]
        m = ADAM_B1 * m_ref[...] + (1.0 - ADAM_B1) * g
        v = ADAM_B2 * v_ref[...] + (1.0 - ADAM_B2) * jnp.square(g)
        m_hat = m / (1.0 - ADAM_B1 ** ADAM_STEP)
        v_hat = v / (1.0 - ADAM_B2 ** ADAM_STEP)
        d_ref[...] = -ADAM_LR * (m_hat / (jnp.sqrt(v_hat) + ADAM_EPS) + ADAM_WD * w_ref[...])
        mo_ref[...] = m
        vo_ref[...] = v

    tile = pl.BlockSpec((tr, Cn), lambda i: (i, 0))
    shape = jax.ShapeDtypeStruct((R, Cn), F32)
    return pl.pallas_call(
        body, name=name, grid=(R // tr,), in_specs=[tile] * 4, out_specs=[tile] * 3,
        out_shape=[shape] * 3, compiler_params=_cparams(("parallel",)),
    )(w, g, m, v)


def _place():
    x, y, c = lax.axis_index("x"), lax.axis_index("y"), lax.axis_index("c")
    others = [(1 - x, y), (x, 1 - y), (1 - x, 1 - y)]
    return x, y, c, others


def _chip_index(px, py):
    return 2 * px + py


ANY = pl.BlockSpec(memory_space=pl.ANY)


def _gather_ride(blocks, split):
    n = len(blocks)

    def copies(in_refs, out_refs, send_sems, recv_sems):
        x, y, c, others = _place()
        me = _chip_index(x, y)

        def copy(sem, src, dst, to):
            return pltpu.make_async_remote_copy(src_ref=src, dst_ref=dst, send_sem=send_sems.at[sem],
                                                recv_sem=recv_sems.at[sem], device_id=to, device_id_type=MESH_ID)

        def part(a, chip, core):
            if not split[a]:
                return out_refs[a].at[chip]
            h = blocks[a].shape[0] // 2
            return out_refs[a].at[chip, pl.ds(core * h, h), :]

        def mine(a):
            if not split[a]:
                return in_refs[a]
            h = blocks[a].shape[0] // 2
            return in_refs[a].at[pl.ds(c * h, h), :]

        pairs = [(a, j, chip) for j, chip in enumerate(others) for a in range(n)]
        k = lambda chip: _chip_index(*chip)
        send = lambda a, j, chip: copy(3 * a + j, mine(a), part(a, me, c), (*chip, c))
        arrival = lambda a, j, chip: copy(3 * a + j, part(a, k(chip), c), part(a, k(chip), c), (*chip, c))
        passing = lambda a, j, chip: copy(3 * (n + a) + j, part(a, k(chip), c), part(a, k(chip), c), (x, y, 1 - c))
        passed = lambda a, j, chip: copy(3 * (n + a) + j, part(a, k(chip), 1 - c), part(a, k(chip), 1 - c),
                                         (x, y, 1 - c))
        return pairs, send, arrival, passing, passed

    def start(in_refs, out_refs, send_sems, recv_sems):
        pairs, send, _, _, _ = copies(in_refs, out_refs, send_sems, recv_sems)
        for p in pairs:
            send(*p).start()

    def finish(in_refs, out_refs, send_sems, recv_sems):
        pairs, send, arrival, passing, passed = copies(in_refs, out_refs, send_sems, recv_sems)
        for p in pairs:
            arrival(*p).wait_recv()
            if split[p[0]]:
                passing(*p).start()
        for p in pairs:
            if split[p[0]]:
                passed(*p).wait_recv()
        for p in pairs:
            send(*p).wait_send()
            if split[p[0]]:
                passing(*p).wait_send()

    shapes = [jax.ShapeDtypeStruct((N_CHIPS,) + b.shape, b.dtype) for b in blocks]
    return _Ride(blocks, shapes, 6 * n, start, finish)


def _put_own(gathered, own):
    me = _chip_index(lax.axis_index("x"), lax.axis_index("y"))
    return lax.dynamic_update_index_in_dim(gathered, own, me, 0)


def _allreduce_small(buf):
    R, L = buf.shape

    def body(in_ref, out_ref, sib_ref, pair_ref, chips_ref, send_sems, recv_sems):
        x, y, c, others = _place()
        me = _chip_index(x, y)
        sibling = (x, y, 1 - c)
        cp = pltpu.make_async_remote_copy(src_ref=in_ref, dst_ref=sib_ref, send_sem=send_sems.at[0],
                                          recv_sem=recv_sems.at[0], device_id=sibling, device_id_type=MESH_ID)
        cp.start()
        cp.wait()
        pair_ref[...] = in_ref[...] + sib_ref[...]
        sends = []
        for j, chip in enumerate(others):
            s = pltpu.make_async_remote_copy(src_ref=pair_ref, dst_ref=chips_ref.at[me],
                                             send_sem=send_sems.at[1 + j], recv_sem=recv_sems.at[1 + j],
                                             device_id=(*chip, c), device_id_type=MESH_ID)
            s.start()
            sends.append(s)
        chips_ref[me] = pair_ref[...]
        for j, chip in enumerate(others):
            k = _chip_index(*chip)
            pltpu.make_async_remote_copy(src_ref=pair_ref, dst_ref=chips_ref.at[k], send_sem=send_sems.at[1 + j],
                                         recv_sem=recv_sems.at[1 + j], device_id=(*chip, c),
                                         device_id_type=MESH_ID).wait_recv()
        for s in sends:
            s.wait_send()
        out_ref[...] = ((chips_ref[0] + chips_ref[1]) + chips_ref[2]) + chips_ref[3]

    vm = pl.BlockSpec(memory_space=pltpu.VMEM)
    return pl.pallas_call(
        body, name="allreduce_small", in_specs=[vm], out_specs=vm,
        out_shape=jax.ShapeDtypeStruct((R, L), F32),
        scratch_shapes=[pltpu.VMEM((R, L), F32), pltpu.VMEM((R, L), F32), pltpu.VMEM((N_CHIPS, R, L), F32),
                        pltpu.SemaphoreType.DMA((4,)), pltpu.SemaphoreType.DMA((4,))],
        compiler_params=pltpu.CompilerParams(vmem_limit_bytes=VMEM_LIMIT),
    )(buf)


def _pair_ride(g):
    nb, R, Cn = g.shape
    h = R // 2

    def copy(in_refs, out_refs, send_sems, recv_sems):
        x, y, c, _ = _place()
        return pltpu.make_async_remote_copy(src_ref=in_refs[0].at[:, pl.ds((1 - c) * h, h), :], dst_ref=out_refs[0],
                                            send_sem=send_sems.at[0], recv_sem=recv_sems.at[0],
                                            device_id=(x, y, 1 - c), device_id_type=MESH_ID)

    return _Ride([g], [jax.ShapeDtypeStruct((nb, h, Cn), g.dtype)], 1,
                 lambda *refs: copy(*refs).start(), lambda *refs: copy(*refs).wait())


def _pair_sum(g, land, c_arr, name, ride=None):
    nb, R, Cn = g.shape
    hr = R // 2
    tr = _pick(hr, (256, 128, 64, 32, 16))
    nt = hr // tr

    def body(c_ref, g_ref, l_ref, o_ref):
        o_ref[...] = (g_ref[...] + l_ref[...]).astype(BF16)

    return _pallas(
        body, (c_arr, g, land), name=name, prefetch=1, grid=(nb, nt),
        in_specs=[pl.BlockSpec((1, tr, Cn), lambda b, i, c_ref: (b, c_ref[0] * nt + i, 0)),
                  pl.BlockSpec((1, tr, Cn), lambda b, i, c_ref: (b, i, 0))],
        out_specs=pl.BlockSpec((1, tr, Cn), lambda b, i, c_ref: (b, i, 0)),
        out_shape=jax.ShapeDtypeStruct((nb, hr, Cn), BF16),
        semantics=("parallel", "parallel"), ride=ride)


def _chip_ride(parts):
    m = len(parts)

    def copies(in_refs, out_refs, send_sems, recv_sems):
        x, y, c, others = _place()
        me = _chip_index(x, y)
        def mk(j, chip, n, landing):
            k = _chip_index(*chip)
            return pltpu.make_async_remote_copy(
                src_ref=in_refs[n].at[k], dst_ref=out_refs[n].at[landing(k)], send_sem=send_sems.at[m * j + n],
                recv_sem=recv_sems.at[m * j + n], device_id=(*chip, c), device_id_type=MESH_ID)

        pairs = [(j, chip, n) for j, chip in enumerate(others) for n in range(m)]
        return pairs, (lambda *p: mk(*p, lambda k: me)), (lambda *p: mk(*p, lambda k: k))

    def start(*refs):
        pairs, send, _ = copies(*refs)
        for p in pairs:
            send(*p).start()

    def finish(*refs):
        pairs, send, arrival = copies(*refs)
        for p in pairs:
            arrival(*p).wait_recv()
        for p in pairs:
            send(*p).wait_send()

    return _Ride(parts, [jax.ShapeDtypeStruct(p.shape, p.dtype) for p in parts], 3 * m, start, finish)


def _put_own_slot(q, p):
    me = _chip_index(lax.axis_index("x"), lax.axis_index("y"))
    return lax.dynamic_update_index_in_dim(q, lax.dynamic_index_in_dim(p, me, 0, keepdims=False), me, 0)


def _chip_sum(q, c_arr, name):
    nb, hr, Cn = q.shape
    tr = _pick(hr, (256, 128, 64, 32, 16))
    nt = hr // tr

    def body(c_ref, q_ref, o_ref):
        f = lambda k: q_ref[k].astype(F32)
        o_ref[...] = ((f(0) + f(1)) + f(2)) + f(3)

    return _pallas(
        body, (c_arr, q), name=name, prefetch=1, grid=(nt,),
        in_specs=[pl.BlockSpec((nb, tr, Cn), lambda i, c_ref: (0, i, 0))],
        out_specs=pl.BlockSpec((tr, Cn), lambda i, c_ref: (c_ref[0] * nt + i, 0)),
        out_shape=jax.ShapeDtypeStruct((2 * hr, Cn), F32),
        semantics=("parallel",))


def _sibling_fill(fw, fo):
    def body(_, __, fw_ref, fo_ref, send_sems, recv_sems):
        x, y, c, _ = _place()
        copies = []
        for n, ref in enumerate((fw_ref, fo_ref)):
            h = ref.shape[0] // 2
            mine = ref.at[pl.ds(c * h, h), :]
            theirs = ref.at[pl.ds((1 - c) * h, h), :]
            mk = lambda src, dst: pltpu.make_async_remote_copy(
                src_ref=src, dst_ref=dst, send_sem=send_sems.at[n], recv_sem=recv_sems.at[n],
                device_id=(x, y, 1 - c), device_id_type=MESH_ID)
            send = mk(mine, mine)
            send.start()
            copies.append((send, mk(theirs, theirs)))
        for send, arrival in copies:
            arrival.wait_recv()
            send.wait_send()

    return pl.pallas_call(
        body, name="sibling_fill", in_specs=[ANY, ANY], out_specs=[ANY, ANY],
        out_shape=[jax.ShapeDtypeStruct(fw.shape, F32), jax.ShapeDtypeStruct(fo.shape, F32)],
        input_output_aliases={0: 0, 1: 1},
        scratch_shapes=[pltpu.SemaphoreType.DMA((2,)), pltpu.SemaphoreType.DMA((2,))],
        compiler_params=pltpu.CompilerParams(has_side_effects=True),
    )(fw, fo)


class _Layout:
    def __init__(self, H, G, nb, Cb):
        A, B = H * HEAD_DIM, G * HEAD_DIM
        self.n_main = 4 * A + 3 * B
        self.k = -(-(self.n_main + LANES) // WIN_BLOCK) * WIN_BLOCK
        cuts = [0, 3 * A, 4 * A, 4 * A + 2 * H, nb * Cb]
        starts = [0, 3 * A + 3 * B, self.n_main, 3 * A]
        self.pieces = []
        self.windows, self.runs = [], []
        for n in range(nb):
            segs = []
            for s in range(4):
                lo, hi = max(cuts[s], n * Cb), min(cuts[s + 1], (n + 1) * Cb)
                if lo < hi:
                    segs.append((starts[s] + lo - cuts[s], lo - n * Cb, hi - lo))
            self.pieces += [(own, n, col, ln) for own, col, ln in segs]
            blocks = sorted({b for own, _, ln in segs for b in range(own // WIN_BLOCK, (own + ln - 1) // WIN_BLOCK + 1)})
            self.windows.append(blocks)
            self.runs.append([(blocks.index(own // WIN_BLOCK) * WIN_BLOCK + own % WIN_BLOCK, ln)
                              for own, _, ln in segs])
        self.wb = max(len(b) for b in self.windows)
        self.table = [b + [b[-1]] * (self.wb - len(b)) for b in self.windows]
        self.pieces.sort()

    def to_own_order(self, g_in):
        D = g_in.shape[1]
        cols, at = [], 0
        for own, n, col, ln in self.pieces:
            if own > at:
                cols.append(jnp.zeros((D, own - at), g_in.dtype))
            cols.append(g_in[n, :, col:col + ln])
            at = own + ln
        if at < self.k:
            cols.append(jnp.zeros((D, self.k - at), g_in.dtype))
        return jnp.concatenate(cols, axis=1)

    def from_window(self, win, chip, Cb):
        pick = lambda runs: (lambda w: jnp.concatenate([w[:, c:c + ln] for c, ln in runs], axis=1))
        return lax.switch(chip, [pick(r) for r in self.runs], win)


def _device_step(x, tgt, norm_w, win_b, wout_b, conv_b, a_log, dt_bias, head_norm_w, sgu_ln_w, sgu_ln_b,
                 w_spatial, b_spatial, final_norm_w, c_arr):
    T, D = x.shape
    H = a_log.shape[1]
    A = H * HEAD_DIM
    G = w_spatial.shape[0]
    B = G * HEAD_DIM
    nb, Cb, Rb = N_CHIPS, win_b.shape[1], wout_b.shape[0]
    lay = _Layout(H, G, nb, Cb)
    alog_row = jnp.pad(a_log, ((0, 0), (H, LANES - 2 * H)))
    dtb_row = jnp.pad(dt_bias, ((0, 0), (H, LANES - 2 * H)))
    bbc = jnp.broadcast_to(b_spatial[:, :, None], (G, CHUNK_B, CHUNK_B))

    (xn, xn_t), (g_in,) = _rms_in(x, norm_w, ride=_gather_ride([win_b], [True]))
    w_own = lay.to_own_order(_put_own(g_in, win_b))
    proj_m, (g_out, g_conv) = _mm_nn(xn, w_own, F32, "in_proj", cols=(0, lay.n_main),
                                     ride=_gather_ride([wout_b, conv_b], [True, False]))
    wout = _put_own(g_out, wout_b).reshape(nb * Rb, D)
    conv_w = _put_own(g_conv, conv_b).transpose(1, 0, 2).reshape(CONV_WIDTH, nb * conv_b.shape[1])
    proj_ba = _mm_nn(xn, w_own, F32, "in_proj_ba", cols=(lay.n_main, LANES))
    q, k, v, gb, bb = _gdn_pre(proj_m, proj_ba, conv_w, alog_row, dtb_row, H)
    u, w, qg, kd, attn, eg, pinv = _gdn_prep(q, k, v, gb, bb)
    og, sall = _gdn_chain(qg, kd, u, w, attn, eg)
    oa, oa_t = _gdn_post(og, proj_m, head_norm_w)
    ob, ob_t = _sgu_fwd(proj_m, sgu_ln_w, sgu_ln_b, w_spatial, bbc, A)
    dh, dhb, loss_row, d_fnw = _out_proj_loss(oa, ob, wout, x, tgt, final_norm_w.reshape(1, D))

    d_o = _mm_nn(dhb, wout.T, F32, "out_proj_dx")
    dproj = lax.empty((T, lay.k), BF16)
    dproj, d_lw, d_lb, d_ws, d_bs = _sgu_bwd(proj_m, sgu_ln_w, sgu_ln_b, w_spatial, bbc, d_o, A, dproj)
    dog, dproj, d_hw = _gdn_post_bwd(og, proj_m, head_norm_w, d_o, dproj)
    dqg, dkd, du, dw, dat, deg = _gdn_chain_bwd(qg, kd, u, w, attn, eg, sall, dog)
    dq, dk, dv, dgb, dbb = _gdn_prep_bwd(q, k, v, gb, bb, pinv, du, dw, dqg, dkd, dat, deg)
    dc, dproj, d_al, d_dt = _gdn_pre_bwd(proj_m, proj_ba, conv_w, alog_row, dtb_row, dq, dk, dv, dgb, dbb, H,
                                         dproj)
    dproj, d_conv = _conv_bwd(proj_m, dc, conv_w, H, dproj)

    table = jnp.array([b for row in lay.table for b in row], jnp.int32)
    d_win = _mm_windows(xn_t, dproj, table, nb, "in_proj_dw")
    d_wout, (land_w,) = _mm_nn_pair(oa_t, ob_t, dhb, "out_proj_dw", ride=_pair_ride(d_win))
    d_wout = d_wout.reshape(nb, Rb, D)
    pair_w, (land_o,) = _pair_sum(d_win, land_w, c_arr, "pair_sum_w_in", ride=_pair_ride(d_wout))
    pair_o = _pair_sum(d_wout, land_o, c_arr, "pair_sum_w_out")
    dxn, (all_w,) = _mm_nt_rhs_outer(dproj, w_own, F32, "in_proj_dx", ride=_chip_ride([pair_w]))
    (grad_x, d_nw), (all_o,) = _rms_in_bwd(x, norm_w, dxn, dh, ride=_chip_ride([pair_o]))
    all_w, all_o = _put_own_slot(all_w, pair_w), _put_own_slot(all_o, pair_o)
    small = dict(norm_w=d_nw, conv_w=d_conv[:CONV_WIDTH], a_log=d_al[:, H:2 * H], dt_bias=d_dt[:, H:2 * H],
                 head_norm_w=d_hw, sgu_ln_w=d_lw, sgu_ln_b=d_lb, w_spatial=d_ws, b_spatial=d_bs[:, :, 0],
                 final_norm_w=d_fnw)
    return loss_row, grad_x, small, all_w, all_o


SMALL = ("norm_w", "conv_w", "a_log", "dt_bias", "head_norm_w", "sgu_ln_w", "sgu_ln_b", "w_spatial",
         "b_spatial", "final_norm_w")


def _pack(parts):
    rows = []
    for p in parts:
        f = p.reshape(-1)
        f = jnp.pad(f, (0, (-f.shape[0]) % (8 * LANES)))
        rows.append(f.reshape(-1, LANES))
    return jnp.concatenate(rows, axis=0)


def _unpack(buf, shapes):
    out, r = [], 0
    for s in shapes:
        n = 1
        for d in s:
            n *= d
        nr = -(-n // (8 * LANES)) * 8
        out.append(buf[r:r + nr].reshape(-1)[:n].reshape(s))
        r += nr
    return out


def kernel(x, norm_w, w_in, conv_w, a_log, dt_bias, head_norm_w, sgu_ln_w, sgu_ln_b, w_spatial, b_spatial, w_out, final_norm_w, loss_target, m_norm_w, m_w_in, m_conv_w, m_a_log, m_dt_bias, m_head_norm_w, m_sgu_ln_w, m_sgu_ln_b, m_w_spatial, m_b_spatial, m_w_out, m_final_norm_w, v_norm_w, v_w_in, v_conv_w, v_a_log, v_dt_bias, v_head_norm_w, v_sgu_ln_w, v_sgu_ln_b, v_w_spatial, v_b_spatial, v_w_out, v_final_norm_w):
    T, D = x.shape[1], x.shape[2]
    weights = dict(norm_w=norm_w, w_in=w_in, conv_w=conv_w, a_log=a_log, dt_bias=dt_bias, head_norm_w=head_norm_w,
                   sgu_ln_w=sgu_ln_w, sgu_ln_b=sgu_ln_b, w_spatial=w_spatial, b_spatial=b_spatial, w_out=w_out,
                   final_norm_w=final_norm_w)
    mom_m = dict(norm_w=m_norm_w, w_in=m_w_in, conv_w=m_conv_w, a_log=m_a_log, dt_bias=m_dt_bias,
                 head_norm_w=m_head_norm_w, sgu_ln_w=m_sgu_ln_w, sgu_ln_b=m_sgu_ln_b, w_spatial=m_w_spatial,
                 b_spatial=m_b_spatial, w_out=m_w_out, final_norm_w=m_final_norm_w)
    mom_v = dict(norm_w=v_norm_w, w_in=v_w_in, conv_w=v_conv_w, a_log=v_a_log, dt_bias=v_dt_bias,
                 head_norm_w=v_head_norm_w, sgu_ln_w=v_sgu_ln_w, sgu_ln_b=v_sgu_ln_b, w_spatial=v_w_spatial,
                 b_spatial=v_b_spatial, w_out=v_w_out, final_norm_w=v_final_norm_w)
    me = _chip_index(lax.axis_index("x"), lax.axis_index("y"))
    c_arr = lax.axis_index("c").astype(jnp.int32).reshape(1)
    Din, Cb = w_in.shape[1], w_in.shape[2]
    Rb = w_out.shape[1]
    cconv = conv_w.shape[2]

    loss_row, grad_x, g, qw, qo = _device_step(
        x[0], loss_target[0], norm_w, w_in[0].astype(BF16), w_out[0].astype(BF16), conv_w[0], a_log, dt_bias,
        head_norm_w, sgu_ln_w, sgu_ln_b, w_spatial[0], b_spatial[0], final_norm_w, c_arr)

    small_shapes = [tuple(g[n].shape) for n in SMALL]
    small = _allreduce_small(_pack([g[n] for n in SMALL]))
    gsum_in, gsum_out = _sibling_fill(_chip_sum(qw, c_arr, "chip_sum_w_in"), _chip_sum(qo, c_arr, "chip_sum_w_out"))
    gsum_in = _Layout(a_log.shape[1], w_spatial.shape[1], N_CHIPS, Cb).from_window(gsum_in, me, Cb)
    gsmall = dict(zip(SMALL, _unpack(small, small_shapes)))
    gsmall["conv_w"] = lax.dynamic_slice_in_dim(gsmall["conv_w"], me * cconv, cconv, axis=1)

    grads, deltas, new_m, new_v = {}, {}, {}, {}
    d, m2, v2 = _adamw(w_out[0], gsum_out, m_w_out[0], v_w_out[0], "adamw_w_out")
    grads["w_out"], deltas["w_out"], new_m["w_out"], new_v["w_out"] = gsum_out[None], d[None], m2[None], v2[None]
    flat = lambda a: a.transpose(2, 0, 1).reshape(-1, LANES)
    unflat = lambda f: f.reshape(Cb, 1, Din).transpose(1, 2, 0)
    g_flat = gsum_in.T.reshape(-1, LANES)
    d, m2, v2 = _adamw(flat(w_in), g_flat, flat(m_w_in), flat(v_w_in), "adamw_w_in")
    grads["w_in"], deltas["w_in"], new_m["w_in"], new_v["w_in"] = unflat(g_flat), unflat(d), unflat(m2), unflat(v2)
    shapes = [tuple(weights[n].shape) for n in SMALL]
    ds, ms, vs = _adamw(_pack([weights[n] for n in SMALL]), _pack([gsmall[n] for n in SMALL]),
                        _pack([mom_m[n] for n in SMALL]), _pack([mom_v[n] for n in SMALL]), "adamw_small")
    for n, gq, d, m2, v2 in zip(SMALL, [gsmall[n] for n in SMALL], _unpack(ds, shapes), _unpack(ms, shapes),
                                _unpack(vs, shapes)):
        grads[n], deltas[n], new_m[n], new_v[n] = gq.reshape(weights[n].shape), d, m2, v2

    loss = lax.psum(loss_row[0, 0], ("x", "y", "c"))
    order = ("norm_w", "w_in", "conv_w", "a_log", "dt_bias", "head_norm_w", "sgu_ln_w", "sgu_ln_b", "w_spatial",
             "b_spatial", "w_out", "final_norm_w")
    return (loss, grad_x[None], *[grads[n] for n in order], *[deltas[n] for n in order],
            *[new_m[n] for n in order], *[new_v[n] for n in order])
```

```python
import functools

import jax
import jax.numpy as jnp
from jax import lax
from jax.experimental import pallas as pl
from jax.experimental.pallas import tpu as pltpu

F32 = jnp.float32
BF16 = jnp.bfloat16
EPS = 1e-6
HEAD_DIM = 128
CHUNK_A = 64
CHUNK_B = 128
CONV_WIDTH = 4
LANES = 128
HALO = 8
N_CHIPS = 4
ADAM_LR = 0.001
ADAM_B1 = 0.9
ADAM_B2 = 0.999
ADAM_EPS = 1e-08
ADAM_WD = 0.01
ADAM_STEP = 10
VMEM_LIMIT = 56 * 1024 * 1024
MESH_ID = pl.DeviceIdType.MESH
HI = lax.Precision.HIGHEST


def _cparams(sem=None, **kw):
    return pltpu.CompilerParams(dimension_semantics=sem, vmem_limit_bytes=VMEM_LIMIT, **kw)


def _matmul(a, b, ca, cb, precision):
    nb = a.ndim - 2
    batch = tuple(range(nb))
    return lax.dot_general(a, b, (((ca + nb,), (cb + nb,)), (batch, batch)), precision=precision,
                           preferred_element_type=F32)


def _dot(a, b, hi=False, precision=None):
    return _matmul(a, b, 1, 0, HI if hi else precision)


def _dot_nt(a, b, hi=False, precision=None):
    return _matmul(a, b, 1, 1, HI if hi else precision)


def _dot_tn(a, b, hi=False, precision=None):
    return _matmul(a, b, 0, 0, HI if hi else precision)


def _iota(shape, dim):
    return lax.broadcasted_iota(jnp.int32, shape, dim)


def _sigmoid(x):
    return 0.5 * (jnp.tanh(0.5 * x) + 1.0)


def _silu(x):
    return x * _sigmoid(x)


def _softplus(x):
    z = jnp.exp(-jnp.abs(x))
    small = z * (1.0 - z * (0.5 - z * (1.0 / 3.0)))
    return jnp.maximum(x, 0.0) + jnp.where(z < 1e-3, small, jnp.log(1.0 + z))


def _pick(n, pref):
    for t in pref:
        if n % t == 0:
            return t
    return n


class _Ride:
    def __init__(self, operands, out_shape, n_sems, start, finish):
        self.operands, self.out_shape, self.n_sems = list(operands), list(out_shape), n_sems
        self.start, self.finish = start, finish


def _pallas(body, operands, *, name, grid, in_specs, out_specs, out_shape, semantics, scratch_shapes=(),
            prefetch=0, ride=None):
    single = not isinstance(out_shape, (list, tuple))
    outs = [out_shape] if single else list(out_shape)
    ospecs = [out_specs] if single else list(out_specs)
    in_specs, scratch = list(in_specs), list(scratch_shapes)
    n_in, n_out, n_sc = len(operands) - prefetch, len(outs), len(scratch)
    kernel = body
    params = _cparams(semantics)
    if ride is not None:
        n_xin, n_xout = len(ride.operands), len(ride.out_shape)

        def kernel(*refs):
            pre, refs = refs[:prefetch], refs[prefetch:]
            ins, refs = refs[:n_in], refs[n_in:]
            xins, refs = refs[:n_xin], refs[n_xin:]
            mains, refs = refs[:n_out], refs[n_out:]
            xouts, refs = refs[:n_xout], refs[n_xout:]
            sc, (send, recv) = refs[:n_sc], refs[n_sc:]
            ids = [pl.program_id(a) for a in range(len(grid))]
            first = functools.reduce(jnp.logical_and, [i == 0 for i in ids])
            last = functools.reduce(jnp.logical_and, [i == g - 1 for i, g in zip(ids, grid)])

            @pl.when(first)
            def _():
                ride.start(xins, xouts, send, recv)

            body(*pre, *ins, *mains, *sc)

            @pl.when(last)
            def _():
                ride.finish(xins, xouts, send, recv)

        operands = list(operands) + ride.operands
        in_specs += [ANY] * n_xin
        ospecs += [ANY] * n_xout
        outs += ride.out_shape
        scratch += [pltpu.SemaphoreType.DMA((ride.n_sems,)), pltpu.SemaphoreType.DMA((ride.n_sems,))]
        params = _cparams(("arbitrary",) * len(grid), has_side_effects=True)
    if prefetch:
        spec = dict(grid_spec=pltpu.PrefetchScalarGridSpec(
            num_scalar_prefetch=prefetch, grid=grid, in_specs=in_specs, out_specs=ospecs, scratch_shapes=scratch))
    else:
        spec = dict(grid=grid, in_specs=in_specs, out_specs=ospecs, scratch_shapes=scratch)
    res = pl.pallas_call(kernel, name=name, out_shape=outs, compiler_params=params, **spec)(*operands)
    main = res[0] if single else list(res[:n_out])
    return main if ride is None else (main, list(res[n_out:]))


def _mm_nn(a, b, out_dtype, name, tm=1024, tn=512, tk=None, cols=None, ride=None):
    M, K = a.shape
    c0, N = (0, b.shape[1]) if cols is None else cols
    tm = _pick(M, (tm, 512, 256, 128))
    tn = _pick(N, (tn, 512, 384, 256, 128))
    tk = K if tk is None else _pick(K, (tk,))
    nk = K // tk
    j0 = c0 // tn
    assert c0 % tn == 0

    def body(a_ref, b_ref, o_ref, *scratch):
        part = _dot(a_ref[...], b_ref[...])
        if nk == 1:
            o_ref[...] = part.astype(out_dtype)
        else:
            acc_ref, = scratch
            k = pl.program_id(2)

            @pl.when(k == 0)
            def _():
                acc_ref[...] = part

            @pl.when(k > 0)
            def _():
                acc_ref[...] += part

            @pl.when(k == nk - 1)
            def _():
                o_ref[...] = acc_ref[...].astype(out_dtype)

    return _pallas(
        body, (a, b), name=name, grid=(M // tm, N // tn, nk),
        in_specs=[pl.BlockSpec((tm, tk), lambda i, j, k: (i, k)),
                  pl.BlockSpec((tk, tn), lambda i, j, k: (k, j + j0))],
        out_specs=pl.BlockSpec((tm, tn), lambda i, j, k: (i, j)),
        out_shape=jax.ShapeDtypeStruct((M, N), out_dtype),
        scratch_shapes=[] if nk == 1 else [pltpu.VMEM((tm, tn), F32)],
        semantics=("parallel", "parallel", "arbitrary"), ride=ride)


def _mm_nt_rhs_outer(a, b, out_dtype, name, tm=256, tn=1024, ride=None):
    M, K = a.shape
    N, _ = b.shape
    tm = _pick(M, (tm, 128))
    tn = _pick(N, (tn, 512, 256, 128))

    def body(a_ref, b_ref, o_ref):
        o_ref[...] = _dot_nt(a_ref[...], b_ref[...]).astype(out_dtype)

    return _pallas(
        body, (a, b), name=name, grid=(N // tn, M // tm),
        in_specs=[pl.BlockSpec((tm, K), lambda j, i: (i, 0)),
                  pl.BlockSpec((tn, K), lambda j, i: (j, 0))],
        out_specs=pl.BlockSpec((tm, tn), lambda j, i: (i, j)),
        out_shape=jax.ShapeDtypeStruct((M, N), out_dtype),
        semantics=("parallel", "parallel"), ride=ride)


WIN_BLOCK = 256


def _mm_windows(a, b, table, nb, name, tm=2048):
    M, K = a.shape
    wb = table.shape[0] // nb
    tm = _pick(M, (tm, 1024, 512, 256, 128))

    def body(tab_ref, a_ref, b_ref, o_ref):
        o_ref[0] = _dot(a_ref[...], b_ref[...])

    return pl.pallas_call(
        body, name=name,
        grid_spec=pltpu.PrefetchScalarGridSpec(
            num_scalar_prefetch=1, grid=(nb, M // tm, wb),
            in_specs=[pl.BlockSpec((tm, K), lambda n, i, t, tab: (i, 0)),
                      pl.BlockSpec((K, WIN_BLOCK), lambda n, i, t, tab: (0, tab[n * wb + t]))],
            out_specs=pl.BlockSpec((1, tm, WIN_BLOCK), lambda n, i, t, tab: (n, i, t))),
        out_shape=jax.ShapeDtypeStruct((nb, M, wb * WIN_BLOCK), F32),
        compiler_params=_cparams(("parallel", "parallel", "arbitrary")),
    )(table, a, b)


def _mm_nn_pair(a0, a1, b, name, tm=512, tn=1024, ride=None):
    M, K = a0.shape
    _, N = b.shape
    tm = _pick(M, (tm, 256, 128))
    tn = _pick(N, (tn, 512, 256, 128))
    ni = M // tm

    def body(a0_ref, a1_ref, b_ref, o_ref):
        p = pl.program_id(0)

        @pl.when(p == 0)
        def _():
            o_ref[...] = _dot(a0_ref[...], b_ref[...])

        @pl.when(p == 1)
        def _():
            o_ref[...] = _dot(a1_ref[...], b_ref[...])

    return _pallas(
        body, (a0, a1, b), name=name, grid=(2, ni, N // tn),
        in_specs=[pl.BlockSpec((tm, K), lambda p, i, j: (i * (1 - p), 0)),
                  pl.BlockSpec((tm, K), lambda p, i, j: (i * p, 0)),
                  pl.BlockSpec((K, tn), lambda p, i, j: (0, j))],
        out_specs=pl.BlockSpec((tm, tn), lambda p, i, j: (p * ni + i, j)),
        out_shape=jax.ShapeDtypeStruct((2 * M, N), F32),
        semantics=("parallel", "parallel", "parallel"), ride=ride)


def _rms_fn(x, w):
    r = lax.rsqrt(jnp.mean(x * x, axis=-1, keepdims=True) + EPS)
    return x * r * w


def _rms_in(x, w, ride=None):
    T, D = x.shape
    tm = _pick(T, (512, 256, 128))

    def body(x_ref, w_ref, o_ref, ot_ref):
        xn = _rms_fn(x_ref[...], w_ref[...])
        o_ref[...] = xn.astype(BF16)
        ot_ref[...] = xn.T.astype(BF16)

    return _pallas(
        body, (x, w), name="rms_in", grid=(T // tm,),
        in_specs=[pl.BlockSpec((tm, D), lambda i: (i, 0)), pl.BlockSpec((1, D), lambda i: (0, 0))],
        out_specs=[pl.BlockSpec((tm, D), lambda i: (i, 0)), pl.BlockSpec((D, tm), lambda i: (0, i))],
        out_shape=[jax.ShapeDtypeStruct((T, D), BF16), jax.ShapeDtypeStruct((D, T), BF16)],
        semantics=("parallel",), ride=ride)


def _rms_in_bwd(x, w, dxn, dh, ride=None):
    T, D = x.shape
    tm = _pick(T, (256, 128))

    def body(x_ref, w_ref, dxn_ref, dh_ref, gx_ref, dw_ref):
        _, vjp = jax.vjp(_rms_fn, x_ref[...], w_ref[...])
        dx, dw = vjp(dxn_ref[...])
        gx_ref[...] = dh_ref[...] + dx

        @pl.when(pl.program_id(0) == 0)
        def _():
            dw_ref[...] = dw

        @pl.when(pl.program_id(0) > 0)
        def _():
            dw_ref[...] += dw

    tile = pl.BlockSpec((tm, D), lambda i: (i, 0))
    row = pl.BlockSpec((1, D), lambda i: (0, 0))
    return _pallas(
        body, (x, w, dxn, dh), name="rms_in_bwd", grid=(T // tm,),
        in_specs=[tile, row, tile, tile], out_specs=[tile, row],
        out_shape=[jax.ShapeDtypeStruct((T, D), F32), jax.ShapeDtypeStruct((1, D), F32)],
        semantics=("arbitrary",), ride=ride)


def _conv_fwd(cat_ref, halo, x, w):
    tm = x.shape[0]
    cat_ref[0:HALO, :] = halo
    cat_ref[HALO:HALO + tm, :] = x
    c = x * w[CONV_WIDTH - 1:CONV_WIDTH, :]
    for k in range(CONV_WIDTH - 1):
        s = CONV_WIDTH - 1 - k
        c = c + cat_ref[pl.ds(HALO - s, tm), :] * w[k:k + 1, :]
    return c


def _lane_to_all(x, lane):
    @jax.custom_vjp
    def f(x):
        return jnp.broadcast_to(x[:, lane:lane + 1], x.shape)

    def f_fwd(x):
        return f(x), None

    def f_bwd(_, g):
        return (jnp.where(_iota(g.shape, 1) == lane, jnp.sum(g, axis=-1, keepdims=True), 0.0),)

    f.defvjp(f_fwd, f_bwd)
    return f(x)


def _gdn_pointwise(c, ba, alog, dtb, H):
    A = H * HEAD_DIM
    s = _silu(c)
    beta = _sigmoid(ba)
    g = -jnp.exp(alog) * _softplus(ba + dtb)
    qs, ks, vs, gbs, bbs = [], [], [], [], []
    for h in range(H):
        lo = h * HEAD_DIM
        q = s[:, lo:lo + HEAD_DIM]
        k = s[:, A + lo:A + lo + HEAD_DIM]
        qs.append(q * lax.rsqrt(jnp.sum(q * q, axis=-1, keepdims=True) + EPS))
        ks.append(k * lax.rsqrt(jnp.sum(k * k, axis=-1, keepdims=True) + EPS))
        vs.append(s[:, 2 * A + lo:2 * A + lo + HEAD_DIM])
        bbs.append(_lane_to_all(beta, h))
        gbs.append(_lane_to_all(g, H + h))
    st = lambda xs: jnp.stack(xs, axis=0)
    return st(qs), st(ks), st(vs), st(gbs), st(bbs)


def _halo_prev(tm):
    return lambda i: (jnp.maximum(i * (tm // HALO) - 1, 0), 0)


def _gdn_pre(proj_m, proj_ba, conv_w, alog_row, dtb_row, H):
    T = proj_m.shape[0]
    A = H * HEAD_DIM
    tm = _pick(T, (256, 128))
    hs = pl.BlockSpec((H, tm, HEAD_DIM), lambda i: (0, i, 0))
    hshape = jax.ShapeDtypeStruct((H, T, HEAD_DIM), F32)

    def body(x_ref, halo_ref, ba_ref, w_ref, al_ref, dt_ref, q_ref, k_ref, v_ref, gb_ref, bb_ref, cat_ref):
        halo = jnp.where(pl.program_id(0) == 0, 0.0, halo_ref[...])
        c = _conv_fwd(cat_ref, halo, x_ref[...], w_ref[...])
        q, k, v, gb, bb = _gdn_pointwise(c, ba_ref[...], al_ref[...], dt_ref[...], H)
        q_ref[...] = q
        k_ref[...] = k
        v_ref[...] = v
        gb_ref[...] = gb
        bb_ref[...] = bb

    return pl.pallas_call(
        body, name="gdn_pre", grid=(T // tm,),
        in_specs=[pl.BlockSpec((tm, 3 * A), lambda i: (i, 0)),
                  pl.BlockSpec((HALO, 3 * A), _halo_prev(tm)),
                  pl.BlockSpec((tm, LANES), lambda i: (i, 0)),
                  pl.BlockSpec((CONV_WIDTH, 3 * A), lambda i: (0, 0)),
                  pl.BlockSpec((1, LANES), lambda i: (0, 0)),
                  pl.BlockSpec((1, LANES), lambda i: (0, 0))],
        out_specs=[hs] * 5, out_shape=[hshape] * 5,
        scratch_shapes=[pltpu.VMEM((HALO + tm, 3 * A), F32)],
        compiler_params=_cparams(("parallel",)),
    )(proj_m, proj_m, proj_ba, conv_w, alog_row, dtb_row)


def _gdn_pre_bwd(proj_m, proj_ba, conv_w, alog_row, dtb_row, dq, dk, dv, dgb, dbb, H, dproj):
    T, n_main = proj_m.shape
    A = H * HEAD_DIM
    tm = _pick(T, (256, 128))
    hs = pl.BlockSpec((H, tm, HEAD_DIM), lambda i: (0, i, 0))
    row = pl.BlockSpec((1, LANES), lambda i: (0, 0))

    def body(x_ref, halo_ref, ba_ref, w_ref, al_ref, dt_ref, dq_ref, dk_ref, dv_ref, dgb_ref, dbb_ref, _,
             dc_ref, dba_ref, dal_ref, ddt_ref, cat_ref):
        halo = jnp.where(pl.program_id(0) == 0, 0.0, halo_ref[...])
        c = _conv_fwd(cat_ref, halo, x_ref[...], w_ref[...])
        _, vjp = jax.vjp(functools.partial(_gdn_pointwise, H=H), c, ba_ref[...], al_ref[...], dt_ref[...])
        dc, dba, dal, ddt = vjp((dq_ref[...], dk_ref[...], dv_ref[...], dgb_ref[...], dbb_ref[...]))
        dc_ref[...] = dc
        dba_ref[:, :LANES] = dba.astype(BF16)
        dba_ref[:, LANES:] = jnp.zeros((tm, WIN_BLOCK - LANES), BF16)

        @pl.when(pl.program_id(0) == 0)
        def _():
            dal_ref[...] = dal
            ddt_ref[...] = ddt

        @pl.when(pl.program_id(0) > 0)
        def _():
            dal_ref[...] += dal
            ddt_ref[...] += ddt

    return pl.pallas_call(
        body, name="gdn_pre_bwd", grid=(T // tm,),
        in_specs=[pl.BlockSpec((tm, 3 * A), lambda i: (i, 0)),
                  pl.BlockSpec((HALO, 3 * A), _halo_prev(tm)),
                  pl.BlockSpec((tm, LANES), lambda i: (i, 0)),
                  pl.BlockSpec((CONV_WIDTH, 3 * A), lambda i: (0, 0)),
                  row, row, hs, hs, hs, hs, hs, ANY],
        out_specs=[pl.BlockSpec((tm, 3 * A), lambda i: (i, 0)),
                   pl.BlockSpec((tm, WIN_BLOCK), lambda i: (i, n_main // WIN_BLOCK)), row, row],
        out_shape=[jax.ShapeDtypeStruct((T, 3 * A), F32), jax.ShapeDtypeStruct(dproj.shape, dproj.dtype),
                   jax.ShapeDtypeStruct((1, LANES), F32), jax.ShapeDtypeStruct((1, LANES), F32)],
        input_output_aliases={11: 1},
        scratch_shapes=[pltpu.VMEM((HALO + tm, 3 * A), F32)],
        compiler_params=_cparams(("arbitrary",)),
    )(proj_m, proj_m, proj_ba, conv_w, alog_row, dtb_row, dq, dk, dv, dgb, dbb, dproj)


def _conv_bwd(proj_m, dc, conv_w, H, dproj):
    T = proj_m.shape[0]
    A = H * HEAD_DIM
    tm = _pick(T, (256, 128))
    nt = T // tm

    def body(x_ref, halo_ref, dc_ref, nxt_ref, w_ref, _, dx_ref, dw_ref):
        i = pl.program_id(0)
        halo = jnp.where(i == 0, 0.0, halo_ref[...])
        xcat = jnp.concatenate([halo, x_ref[...]], axis=0)
        nxt = jnp.where(i == nt - 1, 0.0, nxt_ref[...])
        dc = dc_ref[...]
        dcat = jnp.concatenate([dc, nxt], axis=0)
        w = w_ref[...]
        dx = None
        rows = []
        for k in range(CONV_WIDTH):
            s = CONV_WIDTH - 1 - k
            ds = dcat if s == 0 else pltpu.roll(dcat, tm + HALO - s, 0)
            term = ds[:tm, :] * w[k:k + 1, :]
            dx = term if dx is None else dx + term
            xs = xcat if s == 0 else pltpu.roll(xcat, s, 0)
            rows.append(jnp.sum(dc * xs[HALO:, :], axis=0, keepdims=True))
        dx_ref[...] = dx.astype(BF16)
        dw = jnp.concatenate(rows + [jnp.zeros((HALO - CONV_WIDTH, 3 * A), F32)], axis=0)

        @pl.when(i == 0)
        def _():
            dw_ref[...] = dw

        @pl.when(i > 0)
        def _():
            dw_ref[...] += dw

    return pl.pallas_call(
        body, name="conv_bwd", grid=(nt,),
        in_specs=[pl.BlockSpec((tm, 3 * A), lambda i: (i, 0)),
                  pl.BlockSpec((HALO, 3 * A), _halo_prev(tm)),
                  pl.BlockSpec((tm, 3 * A), lambda i: (i, 0)),
                  pl.BlockSpec((HALO, 3 * A), lambda i: (jnp.minimum((i + 1) * (tm // HALO), T // HALO - 1), 0)),
                  pl.BlockSpec((CONV_WIDTH, 3 * A), lambda i: (0, 0)), ANY],
        out_specs=[pl.BlockSpec((tm, 3 * A), lambda i: (i, 0)),
                   pl.BlockSpec((HALO, 3 * A), lambda i: (0, 0))],
        out_shape=[jax.ShapeDtypeStruct(dproj.shape, dproj.dtype), jax.ShapeDtypeStruct((HALO, 3 * A), F32)],
        input_output_aliases={5: 0},
        compiler_params=_cparams(("arbitrary",)),
    )(proj_m, proj_m, dc, dc, conv_w, dproj)


PAIR = 2 * CHUNK_A


def _b(x):
    return x.astype(BF16)


@jax.custom_vjp
def _bdot(a, b):
    return _dot(_b(a), _b(b))


def _bdot_f(a, b):
    return _bdot(a, b), (a, b)


def _bdot_b(res, g):
    a, b = res
    return _dot_nt(_b(g), _b(b)), _dot_tn(_b(a), _b(g))


_bdot.defvjp(_bdot_f, _bdot_b)


@jax.custom_vjp
def _bdot_nt(a, b):
    return _dot_nt(_b(a), _b(b))


def _bdot_nt_f(a, b):
    return _bdot_nt(a, b), (a, b)


def _bdot_nt_b(res, g):
    a, b = res
    return _dot(_b(g), _b(b)), _dot_tn(_b(g), _b(a))


_bdot_nt.defvjp(_bdot_nt_f, _bdot_nt_b)


@jax.custom_vjp
def _bdot_tn(a, b):
    return _dot_tn(_b(a), _b(b))


def _bdot_tn_f(a, b):
    return _bdot_tn(a, b), (a, b)


def _bdot_tn_b(res, g):
    a, b = res
    return _dot_nt(_b(b), _b(g)), _dot(_b(a), _b(g))


_bdot_tn.defvjp(_bdot_tn_f, _bdot_tn_b)


def _mask_matmul(m, x):
    hi = _b(x)
    r = x - hi.astype(F32)
    mid = _b(r)
    lo = _b(r - mid.astype(F32))
    return (_dot(m, lo) + _dot(m, mid)) + _dot(m, hi)


@jax.custom_vjp
def _mask_dot(m, mt, x):
    return _mask_matmul(m, x)


def _mask_dot_f(m, mt, x):
    return _mask_matmul(m, x), (m, mt)


def _mask_dot_b(res, g):
    m, mt = res
    return jnp.zeros_like(m), jnp.zeros_like(mt), _mask_matmul(mt, g)


_mask_dot.defvjp(_mask_dot_f, _mask_dot_b)

HIGH = lax.Precision.HIGH


def _unit_lower_inverse(L):
    n = L.shape[-1]
    X = -L
    Q = X
    for _ in range(CHUNK_A.bit_length() - 2):
        X = _dot(_b(X), _b(X))
        Q = Q + X + _dot(_b(Q), _b(X))
    return (_iota((n, n), 0) == _iota((n, n), 1)).astype(F32) + Q


@jax.custom_vjp
def _known_inverse(L, P):
    return P


def _known_inverse_f(L, P):
    return P, P


def _known_inverse_b(P, g):
    n = P.shape[-1]
    Q = _b(P - (_iota((n, n), 0) == _iota((n, n), 1)).astype(F32))
    t = g + _dot_tn(Q, _b(g))
    return -(t + _dot_nt(_b(t), Q)), jnp.zeros_like(P)


_known_inverse.defvjp(_known_inverse_f, _known_inverse_b)


def _gdn_prep_fn(q, k, v, gb, bb, P_known=None):
    n = PAIR
    row, col = _iota((n, n), 0), _iota((n, n), 1)
    same = (row >= CHUNK_A) == (col >= CHUNK_A)
    incl = same & (row >= col)
    strict = same & (row > col)
    bc = lambda m: jnp.broadcast_to(_b(m.astype(F32)), q.shape[:1] + (n, n))
    tril, triu, ones = bc(incl), bc(same & (row <= col)), bc(same)
    gc = _mask_dot(tril, triu, gb)
    gl = _mask_dot(ones, ones, gb)
    decay = jnp.where(incl, jnp.exp(jnp.where(incl, gc - jnp.swapaxes(gc, 1, 2), 0.0)), 0.0)
    kb = k * bb
    vb = v * bb
    qs = q * (HEAD_DIM ** -0.5)
    L = jnp.where(strict, _bdot_nt(kb, k) * decay, 0.0)
    P = _unit_lower_inverse(L) if P_known is None else _known_inverse(L, P_known)
    egc = jnp.exp(gc)
    u = _bdot(P, vb)
    w = _bdot(P, kb * egc)
    attn = jnp.where(incl, _bdot_nt(qs, k) * decay, 0.0)
    qg = qs * egc
    kdec = k * jnp.exp(gl - gc)
    eg = jnp.exp(gl)
    if P_known is None:
        return u, w, qg, kdec, attn, eg, P
    return u, w, qg, kdec, attn, eg


def _gdn_chain_fn(S, qg, kdec, u, w, attn, eg):
    C = CHUNK_A
    a, b = (slice(None), slice(0, C)), (slice(None), slice(C, PAIR))
    cat = lambda xs: jnp.concatenate(xs, axis=1)
    vn_a = u[a] - _bdot(w[a], S)
    o_a = _bdot(qg[a], S) + _bdot(attn[a], cat([vn_a, jnp.zeros_like(vn_a)]))
    S1 = S * cat([eg[a], eg[a]]) + _bdot_tn(kdec[a], vn_a)
    vn_b = u[b] - _bdot(w[b], S1)
    o_b = _bdot(qg[b], S1) + _bdot(attn[b], cat([vn_a, vn_b]))
    S2 = S1 * cat([eg[b], eg[b]]) + _bdot_tn(kdec[b], vn_b)
    return cat([o_a, o_b]), S2


def _gdn_prep(q, k, v, gb, bb):
    H, T, _ = q.shape
    pb = _pick(T // PAIR, (8, 4, 2, 1))
    hs = pl.BlockSpec((1, PAIR * pb, HEAD_DIM), lambda h, n: (h, n, 0))
    hshape = jax.ShapeDtypeStruct((H, T, HEAD_DIM), F32)

    def body(q_ref, k_ref, v_ref, gb_ref, bb_ref, *out_refs):
        pairs = lambda ref: ref[0].reshape(pb, PAIR, HEAD_DIM)
        outs = _gdn_prep_fn(pairs(q_ref), pairs(k_ref), pairs(v_ref), pairs(gb_ref), pairs(bb_ref))
        for ref, val in zip(out_refs, outs):
            ref[0] = val.reshape(pb * PAIR, HEAD_DIM)

    return pl.pallas_call(
        body, name="gdn_prep", grid=(H, T // (PAIR * pb)),
        in_specs=[hs] * 5, out_specs=[hs] * 7, out_shape=[hshape] * 7,
        compiler_params=_cparams(("parallel", "parallel")),
    )(q, k, v, gb, bb)


def _gdn_prep_bwd(q, k, v, gb, bb, pinv, du, dw, dqg, dkd, dat, deg):
    H, T, _ = q.shape
    pb = _pick(T // PAIR, (8, 4, 2, 1))
    hs = pl.BlockSpec((1, PAIR * pb, HEAD_DIM), lambda h, n: (h, n, 0))
    hshape = jax.ShapeDtypeStruct((H, T, HEAD_DIM), F32)

    def body(*refs):
        in_refs, p_ref, ct_refs, out_refs = refs[:5], refs[5], refs[6:12], refs[12:]
        pairs = lambda ref: ref[0].reshape(pb, PAIR, HEAD_DIM)
        P = pairs(p_ref)
        _, vjp = jax.vjp(lambda *a: _gdn_prep_fn(*a, P_known=P), *[pairs(r) for r in in_refs])
        grads = vjp(tuple(pairs(r) for r in ct_refs))
        for ref, val in zip(out_refs, grads):
            ref[0] = val.reshape(pb * PAIR, HEAD_DIM)

    return pl.pallas_call(
        body, name="gdn_prep_bwd", grid=(H, T // (PAIR * pb)),
        in_specs=[hs] * 12, out_specs=[hs] * 5, out_shape=[hshape] * 5,
        compiler_params=_cparams(("parallel", "parallel")),
    )(q, k, v, gb, bb, pinv, du, dw, dqg, dkd, dat, deg)


def _gdn_chain(qg, kd, u, w, attn, eg):
    H, T, _ = qg.shape
    N = T // PAIR
    hs = pl.BlockSpec((H, PAIR, HEAD_DIM), lambda n: (0, n, 0))
    ss = pl.BlockSpec((1, H, HEAD_DIM, HEAD_DIM), lambda n: (n, 0, 0, 0))

    def body(qg_ref, kd_ref, u_ref, w_ref, at_ref, eg_ref, o_ref, sall_ref, s_ref):
        @pl.when(pl.program_id(0) == 0)
        def _():
            s_ref[...] = jnp.zeros_like(s_ref)

        S = s_ref[...]
        sall_ref[0] = S
        o, S2 = _gdn_chain_fn(S, qg_ref[...], kd_ref[...], u_ref[...], w_ref[...], at_ref[...], eg_ref[...])
        o_ref[...] = o
        s_ref[...] = S2

    return pl.pallas_call(
        body, name="gdn_chain", grid=(N,),
        in_specs=[hs] * 6, out_specs=[hs, ss],
        out_shape=[jax.ShapeDtypeStruct((H, T, HEAD_DIM), F32),
                   jax.ShapeDtypeStruct((N, H, HEAD_DIM, HEAD_DIM), F32)],
        scratch_shapes=[pltpu.VMEM((H, HEAD_DIM, HEAD_DIM), F32)],
        compiler_params=_cparams(("arbitrary",)),
    )(qg, kd, u, w, attn, eg)


def _gdn_chain_bwd(qg, kd, u, w, attn, eg, sall, do):
    H, T, _ = qg.shape
    N = T // PAIR
    hs = pl.BlockSpec((H, PAIR, HEAD_DIM), lambda n: (0, N - 1 - n, 0))
    ss = pl.BlockSpec((1, H, HEAD_DIM, HEAD_DIM), lambda n: (N - 1 - n, 0, 0, 0))
    hshape = jax.ShapeDtypeStruct((H, T, HEAD_DIM), F32)

    def body(qg_ref, kd_ref, u_ref, w_ref, at_ref, eg_ref, sall_ref, do_ref, *rest):
        out_refs, ds_ref = rest[:6], rest[6]

        @pl.when(pl.program_id(0) == 0)
        def _():
            ds_ref[...] = jnp.zeros_like(ds_ref)

        _, vjp = jax.vjp(_gdn_chain_fn, sall_ref[0], qg_ref[...], kd_ref[...], u_ref[...], w_ref[...],
                         at_ref[...], eg_ref[...])
        grads = vjp((do_ref[...], ds_ref[...]))
        ds_ref[...] = grads[0]
        for ref, val in zip(out_refs, grads[1:]):
            ref[...] = val

    return pl.pallas_call(
        body, name="gdn_chain_bwd", grid=(N,),
        in_specs=[hs] * 6 + [ss, hs], out_specs=[hs] * 6, out_shape=[hshape] * 6,
        scratch_shapes=[pltpu.VMEM((H, HEAD_DIM, HEAD_DIM), F32)],
        compiler_params=_cparams(("arbitrary",)),
    )(qg, kd, u, w, attn, eg, sall, do)


def _post_fn(ogs, za, hw):
    outs = []
    for h, o in enumerate(ogs):
        r = lax.rsqrt(jnp.mean(o * o, axis=-1, keepdims=True) + EPS)
        outs.append(o * r * hw * _silu(za[:, h * HEAD_DIM:(h + 1) * HEAD_DIM]))
    return jnp.concatenate(outs, axis=1)


def _gdn_post(og, proj_m, hw):
    H, T, _ = og.shape
    A = H * HEAD_DIM
    tm = _pick(T, (512, 256, 128))

    def body(og_ref, za_ref, hw_ref, o_ref, ot_ref):
        o = _post_fn(tuple(og_ref[h] for h in range(H)), za_ref[...], hw_ref[...])
        o_ref[...] = o.astype(BF16)
        ot_ref[...] = o.T.astype(BF16)

    return pl.pallas_call(
        body, name="gdn_post", grid=(T // tm,),
        in_specs=[pl.BlockSpec((H, tm, HEAD_DIM), lambda i: (0, i, 0)),
                  pl.BlockSpec((tm, A), lambda i: (i, ZA_BLOCK)),
                  pl.BlockSpec((1, HEAD_DIM), lambda i: (0, 0))],
        out_specs=[pl.BlockSpec((tm, A), lambda i: (i, 0)), pl.BlockSpec((A, tm), lambda i: (0, i))],
        out_shape=[jax.ShapeDtypeStruct((T, A), BF16), jax.ShapeDtypeStruct((A, T), BF16)],
        compiler_params=_cparams(("parallel",)),
    )(og, proj_m, hw)


def _gdn_post_bwd(og, proj_m, hw, d_o, dproj):
    H, T, _ = og.shape
    A = H * HEAD_DIM
    tm = _pick(T, (256, 128))

    def body(og_ref, za_ref, hw_ref, do_ref, _, dog_ref, dza_ref, dhw_ref):
        _, vjp = jax.vjp(_post_fn, tuple(og_ref[h] for h in range(H)), za_ref[...], hw_ref[...])
        dog, dza, dhw = vjp(do_ref[...])
        for h in range(H):
            dog_ref[h] = dog[h]
        dza_ref[...] = dza.astype(BF16)

        @pl.when(pl.program_id(0) == 0)
        def _():
            dhw_ref[...] = dhw

        @pl.when(pl.program_id(0) > 0)
        def _():
            dhw_ref[...] += dhw

    return pl.pallas_call(
        body, name="gdn_post_bwd", grid=(T // tm,),
        in_specs=[pl.BlockSpec((H, tm, HEAD_DIM), lambda i: (0, i, 0)),
                  pl.BlockSpec((tm, A), lambda i: (i, ZA_BLOCK)),
                  pl.BlockSpec((1, HEAD_DIM), lambda i: (0, 0)),
                  pl.BlockSpec((tm, A), lambda i: (i, 0)), ANY],
        out_specs=[pl.BlockSpec((H, tm, HEAD_DIM), lambda i: (0, i, 0)),
                   pl.BlockSpec((tm, A), lambda i: (i, ZA_BLOCK)),
                   pl.BlockSpec((1, HEAD_DIM), lambda i: (0, 0))],
        out_shape=[jax.ShapeDtypeStruct((H, T, HEAD_DIM), F32), jax.ShapeDtypeStruct(dproj.shape, dproj.dtype),
                   jax.ShapeDtypeStruct((1, HEAD_DIM), F32)],
        input_output_aliases={4: 1},
        compiler_params=_cparams(("arbitrary",)),
    )(og, proj_m, hw, d_o, dproj)


def _sgu_fn(ub, vb, zb, lw, lb, W, bbc):
    G = len(W)
    tm = ub.shape[0]
    mu = jnp.mean(vb, axis=-1, keepdims=True)
    xc = vb - mu
    var = jnp.mean(xc * xc, axis=-1, keepdims=True)
    vn = xc * lax.rsqrt(var + EPS) * lw + lb
    mask = _iota((CHUNK_B, CHUNK_B), 0) >= _iota((CHUNK_B, CHUNK_B), 1)
    cols = []
    for g in range(G):
        wm = jnp.where(mask, W[g], 0.0).astype(BF16)
        rows = []
        for c in range(tm // CHUNK_B):
            blk = vn[c * CHUNK_B:(c + 1) * CHUNK_B, g * HEAD_DIM:(g + 1) * HEAD_DIM].astype(BF16)
            rows.append(_dot(wm, blk) + bbc[g])
        cols.append(jnp.concatenate(rows, axis=0) if len(rows) > 1 else rows[0])
    s = jnp.concatenate(cols, axis=1)
    return ub * s * _silu(zb)


ZA_BLOCK = 6


def _sgu_cols(A, B):
    assert A == B
    return 3, 4, 5


def _sgu_fwd(proj_m, lw, lb, W, bbc, A):
    T = proj_m.shape[0]
    G = W.shape[0]
    B = G * HEAD_DIM
    tm = _pick(T, (256, 128))
    cu, cv, cz = _sgu_cols(A, B)

    def body(u_ref, v_ref, z_ref, lw_ref, lb_ref, w_ref, b_ref, o_ref, ot_ref):
        o = _sgu_fn(u_ref[...], v_ref[...], z_ref[...], lw_ref[...], lb_ref[...],
                    tuple(w_ref[g] for g in range(G)), tuple(b_ref[g] for g in range(G)))
        o_ref[...] = o.astype(BF16)
        ot_ref[...] = o.T.astype(BF16)

    row = pl.BlockSpec((1, B), lambda i: (0, 0))
    cube = pl.BlockSpec((G, CHUNK_B, CHUNK_B), lambda i: (0, 0, 0))
    return pl.pallas_call(
        body, name="sgu_fwd", grid=(T // tm,),
        in_specs=[pl.BlockSpec((tm, B), lambda i: (i, cu)), pl.BlockSpec((tm, B), lambda i: (i, cv)),
                  pl.BlockSpec((tm, B), lambda i: (i, cz)), row, row, cube, cube],
        out_specs=[pl.BlockSpec((tm, B), lambda i: (i, 0)), pl.BlockSpec((B, tm), lambda i: (0, i))],
        out_shape=[jax.ShapeDtypeStruct((T, B), BF16), jax.ShapeDtypeStruct((B, T), BF16)],
        compiler_params=_cparams(("parallel",)),
    )(proj_m, proj_m, proj_m, lw, lb, W, bbc)


def _sgu_bwd(proj_m, lw, lb, W, bbc, d_o, A, dproj):
    T = proj_m.shape[0]
    G = W.shape[0]
    B = G * HEAD_DIM
    tm = _pick(T, (256, 128))
    nt = T // tm
    cu, cv, cz = _sgu_cols(A, B)

    def body(u_ref, v_ref, z_ref, lw_ref, lb_ref, w_ref, b_ref, do_ref, _,
             dp_ref, dlw_ref, dlb_ref, dw_ref, db_ref, dbb_ref):
        _, vjp = jax.vjp(_sgu_fn, u_ref[...], v_ref[...], z_ref[...], lw_ref[...], lb_ref[...],
                         tuple(w_ref[g] for g in range(G)), tuple(b_ref[g] for g in range(G)))
        du, dv, dz, dlw, dlb, dW, dbb = vjp(do_ref[...])
        dW, dbb = jnp.stack(dW, axis=0), jnp.stack(dbb, axis=0)
        dp_ref[:, 0:B] = du.astype(BF16)
        dp_ref[:, B:2 * B] = dv.astype(BF16)
        dp_ref[:, 2 * B:3 * B] = dz.astype(BF16)
        i = pl.program_id(0)

        @pl.when(i == 0)
        def _():
            dlw_ref[...] = dlw
            dlb_ref[...] = dlb
            dw_ref[...] = dW
            dbb_ref[...] = dbb

        @pl.when(i > 0)
        def _():
            dlw_ref[...] += dlw
            dlb_ref[...] += dlb
            dw_ref[...] += dW
            dbb_ref[...] += dbb

        @pl.when(i == nt - 1)
        def _():
            db_ref[...] = jnp.sum(dbb_ref[...], axis=-1, keepdims=True)

    row = pl.BlockSpec((1, B), lambda i: (0, 0))
    cube = pl.BlockSpec((G, CHUNK_B, CHUNK_B), lambda i: (0, 0, 0))
    return pl.pallas_call(
        body, name="sgu_bwd", grid=(nt,),
        in_specs=[pl.BlockSpec((tm, B), lambda i: (i, cu)), pl.BlockSpec((tm, B), lambda i: (i, cv)),
                  pl.BlockSpec((tm, B), lambda i: (i, cz)), row, row, cube, cube,
                  pl.BlockSpec((tm, B), lambda i: (i, A // B)), ANY],
        out_specs=[pl.BlockSpec((tm, 3 * B), lambda i: (i, 1)), row, row, cube,
                   pl.BlockSpec((G, CHUNK_B, 1), lambda i: (0, 0, 0))],
        out_shape=[jax.ShapeDtypeStruct(dproj.shape, dproj.dtype), jax.ShapeDtypeStruct((1, B), F32),
                   jax.ShapeDtypeStruct((1, B), F32), jax.ShapeDtypeStruct((G, CHUNK_B, CHUNK_B), F32),
                   jax.ShapeDtypeStruct((G, CHUNK_B, 1), F32)],
        input_output_aliases={8: 0},
        scratch_shapes=[pltpu.VMEM((G, CHUNK_B, CHUNK_B), F32)],
        compiler_params=_cparams(("arbitrary",)),
    )(proj_m, proj_m, proj_m, lw, lb, W, bbc, d_o, dproj)


def _head_fn(mix, x, fw, tgt):
    h = x + mix
    y = _rms_fn(h, fw)
    e = y - tgt
    return 0.5 * jnp.sum(jnp.mean(e * e, axis=-1, keepdims=True), axis=0, keepdims=True)


def _out_proj_loss(oa, ob, wout, x, tgt, fw):
    T, A = oa.shape
    B = ob.shape[1]
    D = x.shape[1]
    tm = _pick(T, (256, 128))

    def body(oa_ref, ob_ref, w_ref, x_ref, t_ref, fw_ref, dh_ref, dhb_ref, loss_ref, dfw_ref):
        mix = _dot(oa_ref[...], w_ref[0:A, :]) + _dot(ob_ref[...], w_ref[A:A + B, :])
        xv, tv = x_ref[...], t_ref[...]
        loss, vjp = jax.vjp(lambda m, f: _head_fn(m, xv, f, tv), mix, fw_ref[...])
        dh, dfw = vjp(jnp.ones((1, 1), F32))
        dh_ref[...] = dh
        dhb_ref[...] = dh.astype(BF16)
        lrow = jnp.broadcast_to(loss, (1, LANES))

        @pl.when(pl.program_id(0) == 0)
        def _():
            loss_ref[...] = lrow
            dfw_ref[...] = dfw

        @pl.when(pl.program_id(0) > 0)
        def _():
            loss_ref[...] += lrow
            dfw_ref[...] += dfw

    tile = pl.BlockSpec((tm, D), lambda i: (i, 0))
    return pl.pallas_call(
        body, name="out_proj_loss", grid=(T // tm,),
        in_specs=[pl.BlockSpec((tm, A), lambda i: (i, 0)), pl.BlockSpec((tm, B), lambda i: (i, 0)),
                  pl.BlockSpec((A + B, D), lambda i: (0, 0)), tile, tile,
                  pl.BlockSpec((1, D), lambda i: (0, 0))],
        out_specs=[tile, tile, pl.BlockSpec((1, LANES), lambda i: (0, 0)),
                   pl.BlockSpec((1, D), lambda i: (0, 0))],
        out_shape=[jax.ShapeDtypeStruct((T, D), F32), jax.ShapeDtypeStruct((T, D), BF16),
                   jax.ShapeDtypeStruct((1, LANES), F32), jax.ShapeDtypeStruct((1, D), F32)],
        compiler_params=_cparams(("arbitrary",)),
    )(oa, ob, wout, x, tgt, fw)


def _adamw(w, g, m, v, name):
    R, Cn = w.shape
    cap = max(8, 512 * 1024 // Cn)
    tr = max(t for t in range(8, min(R, cap) + 1, 8) if R % t == 0) if R > cap else R

    def body(w_ref, g_ref, m_ref, v_ref, d_ref, mo_ref, vo_ref):
        g = g_ref[...]
        m = ADAM_B1 * m_ref[...] + (1.0 - ADAM_B1) * g
        v = ADAM_B2 * v_ref[...] + (1.0 - ADAM_B2) * jnp.square(g)
        m_hat = m / (1.0 - ADAM_B1 ** ADAM_STEP)
        v_hat = v / (1.0 - ADAM_B2 ** ADAM_STEP)
        d_ref[...] = -ADAM_LR * (m_hat / (jnp.sqrt(v_hat) + ADAM_EPS) + ADAM_WD * w_ref[...])
        mo_ref[...] = m
        vo_ref[...] = v

    tile = pl.BlockSpec((tr, Cn), lambda i: (i, 0))
    shape = jax.ShapeDtypeStruct((R, Cn), F32)
    return pl.pallas_call(
        body, name=name, grid=(R // tr,), in_specs=[tile] * 4, out_specs=[tile] * 3,
        out_shape=[shape] * 3, compiler_params=_cparams(("parallel",)),
    )(w, g, m, v)


def _place():
    x, y, c = lax.axis_index("x"), lax.axis_index("y"), lax.axis_index("c")
    others = [(1 - x, y), (x, 1 - y), (1 - x, 1 - y)]
    return x, y, c, others


def _chip_index(px, py):
    return 2 * px + py


ANY = pl.BlockSpec(memory_space=pl.ANY)


def _gather_ride(blocks, split):
    n = len(blocks)

    def plan(in_refs, out_refs, send_sems, recv_sems):
        x, y, c, _ = _place()
        me, kx, ky, kd = (_chip_index(px, py) for px, py in ((x, y), (1 - x, y), (x, 1 - y), (1 - x, 1 - y)))
        to_x, to_y, to_s = (1 - x, y, c), (x, 1 - y, c), (x, y, 1 - c)

        def copy(sem, src, dst, to):
            return pltpu.make_async_remote_copy(src_ref=src, dst_ref=dst, send_sem=send_sems.at[sem],
                                                recv_sem=recv_sems.at[sem], device_id=to, device_id_type=MESH_ID)

        first, second, third, awaited = [], [], [], []
        for a in range(n):
            out, s0 = out_refs[a], 8 * a
            if not split[a]:
                for j, (k, to) in enumerate(((kx, to_x), (ky, to_y), (kd, (1 - x, 1 - y, c)))):
                    first.append(lambda j=j, to=to, a=a, out=out, s0=s0: copy(s0 + j, in_refs[a], out.at[me], to))
                    awaited.append((lambda j=j, k=k, to=to, out=out, s0=s0: copy(s0 + j, out.at[k], out.at[k], to),
                                    None))
                continue
            h = blocks[a].shape[0] // 2
            q = h // 2
            half = lambda k, core, out=out, h=h: out.at[k, pl.ds(core * h, h), :]
            quarter = lambda k, core, i, out=out, h=h, q=q: out.at[k, pl.ds(core * h + i * q, q), :]
            mine = in_refs[a].at[pl.ds(c * h, h), :]
            first.append(lambda s0=s0, mine=mine, half=half: copy(s0, mine, half(me, c), to_x))
            first.append(lambda s0=s0, mine=mine, half=half: copy(s0 + 1, mine, half(me, c), to_y))
            fwd0 = lambda s0=s0, quarter=quarter: copy(s0 + 2, quarter(kx, c, 0), quarter(kx, c, 0), to_y)
            fwd1 = lambda s0=s0, quarter=quarter: copy(s0 + 3, quarter(ky, c, 1), quarter(ky, c, 1), to_x)
            pieces = [(s0 + 0, lambda half=half: half(kx, c), lambda half=half: half(kx, 1 - c), to_x, fwd0),
                      (s0 + 1, lambda half=half: half(ky, c), lambda half=half: half(ky, 1 - c), to_y, fwd1),
                      (s0 + 2, lambda quarter=quarter: quarter(kd, c, 0), lambda quarter=quarter: quarter(kd, 1 - c, 0),
                       to_y, None),
                      (s0 + 3, lambda quarter=quarter: quarter(kd, c, 1), lambda quarter=quarter: quarter(kd, 1 - c, 1),
                       to_x, None)]
            for i, (sem, here, there, frm, fwd) in enumerate(pieces):
                passing = lambda s0=s0, i=i, here=here: copy(s0 + 4 + i, here(), here(), to_s)
                awaited.append((lambda sem=sem, here=here, frm=frm: copy(sem, here(), here(), frm), (fwd, passing)))
                if fwd is not None:
                    second.append(fwd)
                third.append((passing, lambda s0=s0, i=i, there=there: copy(s0 + 4 + i, there(), there(), to_s)))
        return first, second, third, awaited

    def start(*refs):
        for send in plan(*refs)[0]:
            send().start()

    def finish(*refs):
        first, second, third, awaited = plan(*refs)
        for arrival, then in awaited:
            arrival().wait_recv()
            for nxt in (then or ()):
                if nxt is not None:
                    nxt().start()
        for _, from_sibling in third:
            from_sibling().wait_recv()
        for send in first + second + [p for p, _ in third]:
            send().wait_send()

    shapes = [jax.ShapeDtypeStruct((N_CHIPS,) + b.shape, b.dtype) for b in blocks]
    return _Ride(blocks, shapes, 8 * n, start, finish)


def _put_own(gathered, own):
    me = _chip_index(lax.axis_index("x"), lax.axis_index("y"))
    return lax.dynamic_update_index_in_dim(gathered, own, me, 0)


def _allreduce_small(buf):
    R, L = buf.shape

    def body(in_ref, out_ref, sib_ref, pair_ref, chips_ref, send_sems, recv_sems):
        x, y, c, others = _place()
        me = _chip_index(x, y)
        sibling = (x, y, 1 - c)
        cp = pltpu.make_async_remote_copy(src_ref=in_ref, dst_ref=sib_ref, send_sem=send_sems.at[0],
                                          recv_sem=recv_sems.at[0], device_id=sibling, device_id_type=MESH_ID)
        cp.start()
        cp.wait()
        pair_ref[...] = in_ref[...] + sib_ref[...]
        sends = []
        for j, chip in enumerate(others):
            s = pltpu.make_async_remote_copy(src_ref=pair_ref, dst_ref=chips_ref.at[me],
                                             send_sem=send_sems.at[1 + j], recv_sem=recv_sems.at[1 + j],
                                             device_id=(*chip, c), device_id_type=MESH_ID)
            s.start()
            sends.append(s)
        chips_ref[me] = pair_ref[...]
        for j, chip in enumerate(others):
            k = _chip_index(*chip)
            pltpu.make_async_remote_copy(src_ref=pair_ref, dst_ref=chips_ref.at[k], send_sem=send_sems.at[1 + j],
                                         recv_sem=recv_sems.at[1 + j], device_id=(*chip, c),
                                         device_id_type=MESH_ID).wait_recv()
        for s in sends:
            s.wait_send()
        out_ref[...] = ((chips_ref[0] + chips_ref[1]) + chips_ref[2]) + chips_ref[3]

    vm = pl.BlockSpec(memory_space=pltpu.VMEM)
    return pl.pallas_call(
        body, name="allreduce_small", in_specs=[vm], out_specs=vm,
        out_shape=jax.ShapeDtypeStruct((R, L), F32),
        scratch_shapes=[pltpu.VMEM((R, L), F32), pltpu.VMEM((R, L), F32), pltpu.VMEM((N_CHIPS, R, L), F32),
                        pltpu.SemaphoreType.DMA((4,)), pltpu.SemaphoreType.DMA((4,))],
        compiler_params=pltpu.CompilerParams(vmem_limit_bytes=VMEM_LIMIT),
    )(buf)


def _pair_ride(g):
    nb, R, Cn = g.shape
    h = R // 2

    def copy(in_refs, out_refs, send_sems, recv_sems):
        x, y, c, _ = _place()
        return pltpu.make_async_remote_copy(src_ref=in_refs[0].at[:, pl.ds((1 - c) * h, h), :], dst_ref=out_refs[0],
                                            send_sem=send_sems.at[0], recv_sem=recv_sems.at[0],
                                            device_id=(x, y, 1 - c), device_id_type=MESH_ID)

    return _Ride([g], [jax.ShapeDtypeStruct((nb, h, Cn), g.dtype)], 1,
                 lambda *refs: copy(*refs).start(), lambda *refs: copy(*refs).wait())


def _pair_sum(g, land, c_arr, name, ride=None):
    nb, R, Cn = g.shape
    hr = R // 2
    tr = _pick(hr, (256, 128, 64, 32, 16))
    nt = hr // tr

    def body(c_ref, g_ref, l_ref, o_ref):
        o_ref[...] = (g_ref[...] + l_ref[...]).astype(BF16)

    return _pallas(
        body, (c_arr, g, land), name=name, prefetch=1, grid=(nb, nt),
        in_specs=[pl.BlockSpec((1, tr, Cn), lambda b, i, c_ref: (b, c_ref[0] * nt + i, 0)),
                  pl.BlockSpec((1, tr, Cn), lambda b, i, c_ref: (b, i, 0))],
        out_specs=pl.BlockSpec((1, tr, Cn), lambda b, i, c_ref: (b, i, 0)),
        out_shape=jax.ShapeDtypeStruct((nb, hr, Cn), BF16),
        semantics=("parallel", "parallel"), ride=ride)


def _chip_ride(parts):
    m = len(parts)

    def copies(in_refs, out_refs, send_sems, recv_sems):
        x, y, c, others = _place()
        me = _chip_index(x, y)
        def mk(j, chip, n, landing):
            k = _chip_index(*chip)
            return pltpu.make_async_remote_copy(
                src_ref=in_refs[n].at[k], dst_ref=out_refs[n].at[landing(k)], send_sem=send_sems.at[m * j + n],
                recv_sem=recv_sems.at[m * j + n], device_id=(*chip, c), device_id_type=MESH_ID)

        pairs = [(j, chip, n) for j, chip in enumerate(others) for n in range(m)]
        return pairs, (lambda *p: mk(*p, lambda k: me)), (lambda *p: mk(*p, lambda k: k))

    def start(*refs):
        pairs, send, _ = copies(*refs)
        for p in pairs:
            send(*p).start()

    def finish(*refs):
        pairs, send, arrival = copies(*refs)
        for p in pairs:
            arrival(*p).wait_recv()
        for p in pairs:
            send(*p).wait_send()

    return _Ride(parts, [jax.ShapeDtypeStruct(p.shape, p.dtype) for p in parts], 3 * m, start, finish)


def _put_own_slot(q, p):
    me = _chip_index(lax.axis_index("x"), lax.axis_index("y"))
    return lax.dynamic_update_index_in_dim(q, lax.dynamic_index_in_dim(p, me, 0, keepdims=False), me, 0)


def _chip_sum(q, c_arr, name):
    nb, hr, Cn = q.shape
    tr = _pick(hr, (256, 128, 64, 32, 16))
    nt = hr // tr

    def body(c_ref, q_ref, o_ref):
        f = lambda k: q_ref[k].astype(F32)
        o_ref[...] = ((f(0) + f(1)) + f(2)) + f(3)

    return _pallas(
        body, (c_arr, q), name=name, prefetch=1, grid=(nt,),
        in_specs=[pl.BlockSpec((nb, tr, Cn), lambda i, c_ref: (0, i, 0))],
        out_specs=pl.BlockSpec((tr, Cn), lambda i, c_ref: (c_ref[0] * nt + i, 0)),
        out_shape=jax.ShapeDtypeStruct((2 * hr, Cn), F32),
        semantics=("parallel",))


def _sibling_fill(fw, fo):
    def body(_, __, fw_ref, fo_ref, send_sems, recv_sems):
        x, y, c, _ = _place()
        copies = []
        for n, ref in enumerate((fw_ref, fo_ref)):
            h = ref.shape[0] // 2
            mine = ref.at[pl.ds(c * h, h), :]
            theirs = ref.at[pl.ds((1 - c) * h, h), :]
            mk = lambda src, dst: pltpu.make_async_remote_copy(
                src_ref=src, dst_ref=dst, send_sem=send_sems.at[n], recv_sem=recv_sems.at[n],
                device_id=(x, y, 1 - c), device_id_type=MESH_ID)
            send = mk(mine, mine)
            send.start()
            copies.append((send, mk(theirs, theirs)))
        for send, arrival in copies:
            arrival.wait_recv()
            send.wait_send()

    return pl.pallas_call(
        body, name="sibling_fill", in_specs=[ANY, ANY], out_specs=[ANY, ANY],
        out_shape=[jax.ShapeDtypeStruct(fw.shape, F32), jax.ShapeDtypeStruct(fo.shape, F32)],
        input_output_aliases={0: 0, 1: 1},
        scratch_shapes=[pltpu.SemaphoreType.DMA((2,)), pltpu.SemaphoreType.DMA((2,))],
        compiler_params=pltpu.CompilerParams(has_side_effects=True),
    )(fw, fo)


class _Layout:
    def __init__(self, H, G, nb, Cb):
        A, B = H * HEAD_DIM, G * HEAD_DIM
        self.n_main = 4 * A + 3 * B
        self.k = -(-(self.n_main + LANES) // WIN_BLOCK) * WIN_BLOCK
        cuts = [0, 3 * A, 4 * A, 4 * A + 2 * H, nb * Cb]
        starts = [0, 3 * A + 3 * B, self.n_main, 3 * A]
        self.pieces = []
        self.windows, self.runs = [], []
        for n in range(nb):
            segs = []
            for s in range(4):
                lo, hi = max(cuts[s], n * Cb), min(cuts[s + 1], (n + 1) * Cb)
                if lo < hi:
                    segs.append((starts[s] + lo - cuts[s], lo - n * Cb, hi - lo))
            self.pieces += [(own, n, col, ln) for own, col, ln in segs]
            blocks = sorted({b for own, _, ln in segs for b in range(own // WIN_BLOCK, (own + ln - 1) // WIN_BLOCK + 1)})
            self.windows.append(blocks)
            self.runs.append([(blocks.index(own // WIN_BLOCK) * WIN_BLOCK + own % WIN_BLOCK, ln)
                              for own, _, ln in segs])
        self.wb = max(len(b) for b in self.windows)
        self.table = [b + [b[-1]] * (self.wb - len(b)) for b in self.windows]
        self.pieces.sort()

    def to_own_order(self, g_in):
        D = g_in.shape[1]
        cols, at = [], 0
        for own, n, col, ln in self.pieces:
            if own > at:
                cols.append(jnp.zeros((D, own - at), g_in.dtype))
            cols.append(g_in[n, :, col:col + ln])
            at = own + ln
        if at < self.k:
            cols.append(jnp.zeros((D, self.k - at), g_in.dtype))
        return jnp.concatenate(cols, axis=1)

    def from_window(self, win, chip, Cb):
        pick = lambda runs: (lambda w: jnp.concatenate([w[:, c:c + ln] for c, ln in runs], axis=1))
        return lax.switch(chip, [pick(r) for r in self.runs], win)


def _device_step(x, tgt, norm_w, win_b, wout_b, conv_b, a_log, dt_bias, head_norm_w, sgu_ln_w, sgu_ln_b,
                 w_spatial, b_spatial, final_norm_w, c_arr):
    T, D = x.shape
    H = a_log.shape[1]
    A = H * HEAD_DIM
    G = w_spatial.shape[0]
    B = G * HEAD_DIM
    nb, Cb, Rb = N_CHIPS, win_b.shape[1], wout_b.shape[0]
    lay = _Layout(H, G, nb, Cb)
    alog_row = jnp.pad(a_log, ((0, 0), (H, LANES - 2 * H)))
    dtb_row = jnp.pad(dt_bias, ((0, 0), (H, LANES - 2 * H)))
    bbc = jnp.broadcast_to(b_spatial[:, :, None], (G, CHUNK_B, CHUNK_B))

    (xn, xn_t), (g_in,) = _rms_in(x, norm_w, ride=_gather_ride([win_b], [True]))
    w_own = lay.to_own_order(_put_own(g_in, win_b))
    proj_m, (g_out, g_conv) = _mm_nn(xn, w_own, F32, "in_proj", cols=(0, lay.n_main),
                                     ride=_gather_ride([wout_b, conv_b], [True, False]))
    wout = _put_own(g_out, wout_b).reshape(nb * Rb, D)
    conv_w = _put_own(g_conv, conv_b).transpose(1, 0, 2).reshape(CONV_WIDTH, nb * conv_b.shape[1])
    proj_ba = _mm_nn(xn, w_own, F32, "in_proj_ba", cols=(lay.n_main, LANES))
    q, k, v, gb, bb = _gdn_pre(proj_m, proj_ba, conv_w, alog_row, dtb_row, H)
    u, w, qg, kd, attn, eg, pinv = _gdn_prep(q, k, v, gb, bb)
    og, sall = _gdn_chain(qg, kd, u, w, attn, eg)
    oa, oa_t = _gdn_post(og, proj_m, head_norm_w)
    ob, ob_t = _sgu_fwd(proj_m, sgu_ln_w, sgu_ln_b, w_spatial, bbc, A)
    dh, dhb, loss_row, d_fnw = _out_proj_loss(oa, ob, wout, x, tgt, final_norm_w.reshape(1, D))

    d_o = _mm_nn(dhb, wout.T, F32, "out_proj_dx")
    dproj = lax.empty((T, lay.k), BF16)
    dproj, d_lw, d_lb, d_ws, d_bs = _sgu_bwd(proj_m, sgu_ln_w, sgu_ln_b, w_spatial, bbc, d_o, A, dproj)
    dog, dproj, d_hw = _gdn_post_bwd(og, proj_m, head_norm_w, d_o, dproj)
    dqg, dkd, du, dw, dat, deg = _gdn_chain_bwd(qg, kd, u, w, attn, eg, sall, dog)
    dq, dk, dv, dgb, dbb = _gdn_prep_bwd(q, k, v, gb, bb, pinv, du, dw, dqg, dkd, dat, deg)
    dc, dproj, d_al, d_dt = _gdn_pre_bwd(proj_m, proj_ba, conv_w, alog_row, dtb_row, dq, dk, dv, dgb, dbb, H,
                                         dproj)
    dproj, d_conv = _conv_bwd(proj_m, dc, conv_w, H, dproj)

    table = jnp.array([b for row in lay.table for b in row], jnp.int32)
    d_win = _mm_windows(xn_t, dproj, table, nb, "in_proj_dw")
    d_wout, (land_w,) = _mm_nn_pair(oa_t, ob_t, dhb, "out_proj_dw", ride=_pair_ride(d_win))
    d_wout = d_wout.reshape(nb, Rb, D)
    pair_w, (land_o,) = _pair_sum(d_win, land_w, c_arr, "pair_sum_w_in", ride=_pair_ride(d_wout))
    pair_o = _pair_sum(d_wout, land_o, c_arr, "pair_sum_w_out")
    dxn, (all_w,) = _mm_nt_rhs_outer(dproj, w_own, F32, "in_proj_dx", ride=_chip_ride([pair_w]))
    (grad_x, d_nw), (all_o,) = _rms_in_bwd(x, norm_w, dxn, dh, ride=_chip_ride([pair_o]))
    all_w, all_o = _put_own_slot(all_w, pair_w), _put_own_slot(all_o, pair_o)
    small = dict(norm_w=d_nw, conv_w=d_conv[:CONV_WIDTH], a_log=d_al[:, H:2 * H], dt_bias=d_dt[:, H:2 * H],
                 head_norm_w=d_hw, sgu_ln_w=d_lw, sgu_ln_b=d_lb, w_spatial=d_ws, b_spatial=d_bs[:, :, 0],
                 final_norm_w=d_fnw)
    return loss_row, grad_x, small, all_w, all_o


SMALL = ("norm_w", "conv_w", "a_log", "dt_bias", "head_norm_w", "sgu_ln_w", "sgu_ln_b", "w_spatial",
         "b_spatial", "final_norm_w")


def _pack(parts):
    rows = []
    for p in parts:
        f = p.reshape(-1)
        f = jnp.pad(f, (0, (-f.shape[0]) % (8 * LANES)))
        rows.append(f.reshape(-1, LANES))
    return jnp.concatenate(rows, axis=0)


def _unpack(buf, shapes):
    out, r = [], 0
    for s in shapes:
        n = 1
        for d in s:
            n *= d
        nr = -(-n // (8 * LANES)) * 8
        out.append(buf[r:r + nr].reshape(-1)[:n].reshape(s))
        r += nr
    return out


def kernel(x, norm_w, w_in, conv_w, a_log, dt_bias, head_norm_w, sgu_ln_w, sgu_ln_b, w_spatial, b_spatial, w_out, final_norm_w, loss_target, m_norm_w, m_w_in, m_conv_w, m_a_log, m_dt_bias, m_head_norm_w, m_sgu_ln_w, m_sgu_ln_b, m_w_spatial, m_b_spatial, m_w_out, m_final_norm_w, v_norm_w, v_w_in, v_conv_w, v_a_log, v_dt_bias, v_head_norm_w, v_sgu_ln_w, v_sgu_ln_b, v_w_spatial, v_b_spatial, v_w_out, v_final_norm_w):
    T, D = x.shape[1], x.shape[2]
    weights = dict(norm_w=norm_w, w_in=w_in, conv_w=conv_w, a_log=a_log, dt_bias=dt_bias, head_norm_w=head_norm_w,
                   sgu_ln_w=sgu_ln_w, sgu_ln_b=sgu_ln_b, w_spatial=w_spatial, b_spatial=b_spatial, w_out=w_out,
                   final_norm_w=final_norm_w)
    mom_m = dict(norm_w=m_norm_w, w_in=m_w_in, conv_w=m_conv_w, a_log=m_a_log, dt_bias=m_dt_bias,
                 head_norm_w=m_head_norm_w, sgu_ln_w=m_sgu_ln_w, sgu_ln_b=m_sgu_ln_b, w_spatial=m_w_spatial,
                 b_spatial=m_b_spatial, w_out=m_w_out, final_norm_w=m_final_norm_w)
    mom_v = dict(norm_w=v_norm_w, w_in=v_w_in, conv_w=v_conv_w, a_log=v_a_log, dt_bias=v_dt_bias,
                 head_norm_w=v_head_norm_w, sgu_ln_w=v_sgu_ln_w, sgu_ln_b=v_sgu_ln_b, w_spatial=v_w_spatial,
                 b_spatial=v_b_spatial, w_out=v_w_out, final_norm_w=v_final_norm_w)
    me = _chip_index(lax.axis_index("x"), lax.axis_index("y"))
    c_arr = lax.axis_index("c").astype(jnp.int32).reshape(1)
    Din, Cb = w_in.shape[1], w_in.shape[2]
    Rb = w_out.shape[1]
    cconv = conv_w.shape[2]

    loss_row, grad_x, g, qw, qo = _device_step(
        x[0], loss_target[0], norm_w, w_in[0].astype(BF16), w_out[0].astype(BF16), conv_w[0], a_log, dt_bias,
        head_norm_w, sgu_ln_w, sgu_ln_b, w_spatial[0], b_spatial[0], final_norm_w, c_arr)

    small_shapes = [tuple(g[n].shape) for n in SMALL]
    small = _allreduce_small(_pack([g[n] for n in SMALL]))
    gsum_in, gsum_out = _sibling_fill(_chip_sum(qw, c_arr, "chip_sum_w_in"), _chip_sum(qo, c_arr, "chip_sum_w_out"))
    gsum_in = _Layout(a_log.shape[1], w_spatial.shape[1], N_CHIPS, Cb).from_window(gsum_in, me, Cb)
    gsmall = dict(zip(SMALL, _unpack(small, small_shapes)))
    gsmall["conv_w"] = lax.dynamic_slice_in_dim(gsmall["conv_w"], me * cconv, cconv, axis=1)

    grads, deltas, new_m, new_v = {}, {}, {}, {}
    d, m2, v2 = _adamw(w_out[0], gsum_out, m_w_out[0], v_w_out[0], "adamw_w_out")
    grads["w_out"], deltas["w_out"], new_m["w_out"], new_v["w_out"] = gsum_out[None], d[None], m2[None], v2[None]
    flat = lambda a: a.transpose(2, 0, 1).reshape(-1, LANES)
    unflat = lambda f: f.reshape(Cb, 1, Din).transpose(1, 2, 0)
    g_flat = gsum_in.T.reshape(-1, LANES)
    d, m2, v2 = _adamw(flat(w_in), g_flat, flat(m_w_in), flat(v_w_in), "adamw_w_in")
    grads["w_in"], deltas["w_in"], new_m["w_in"], new_v["w_in"] = unflat(g_flat), unflat(d), unflat(m2), unflat(v2)
    shapes = [tuple(weights[n].shape) for n in SMALL]
    ds, ms, vs = _adamw(_pack([weights[n] for n in SMALL]), _pack([gsmall[n] for n in SMALL]),
                        _pack([mom_m[n] for n in SMALL]), _pack([mom_v[n] for n in SMALL]), "adamw_small")
    for n, gq, d, m2, v2 in zip(SMALL, [gsmall[n] for n in SMALL], _unpack(ds, shapes), _unpack(ms, shapes),
                                _unpack(vs, shapes)):
        grads[n], deltas[n], new_m[n], new_v[n] = gq.reshape(weights[n].shape), d, m2, v2

    loss = lax.psum(loss_row[0, 0], ("x", "y", "c"))
    order = ("norm_w", "w_in", "conv_w", "a_log", "dt_bias", "head_norm_w", "sgu_ln_w", "sgu_ln_b", "w_spatial",
             "b_spatial", "w_out", "final_norm_w")
    return (loss, grad_x[None], *[grads[n] for n in order], *[deltas[n] for n in order],
            *[new_m[n] for n in order], *[new_v[n] for n in order])
```

```python
import functools

import jax
import jax.numpy as jnp
from jax import lax
from jax.experimental import pallas as pl
from jax.experimental.pallas import tpu as pltpu

F32 = jnp.float32
BF16 = jnp.bfloat16
EPS = 1e-6
HEAD_DIM = 128
CHUNK_B = 128
CONV_WIDTH = 4
LANES = 128
HALO = 8
N_CHIPS = 4
ADAM_LR = 0.001
ADAM_B1 = 0.9
ADAM_B2 = 0.999
ADAM_EPS = 1e-08
ADAM_WD = 0.01
ADAM_STEP = 10
VMEM_LIMIT = 56 * 1024 * 1024
MESH_ID = pl.DeviceIdType.MESH
HI = lax.Precision.HIGHEST


def _cparams(sem=None, **kw):
    return pltpu.CompilerParams(dimension_semantics=sem, vmem_limit_bytes=VMEM_LIMIT, **kw)


def _matmul(a, b, ca, cb, precision):
    nb = a.ndim - 2
    batch = tuple(range(nb))
    return lax.dot_general(a, b, (((ca + nb,), (cb + nb,)), (batch, batch)), precision=precision,
                           preferred_element_type=F32)


def _dot(a, b, hi=False, precision=None):
    return _matmul(a, b, 1, 0, HI if hi else precision)


def _dot_nt(a, b, hi=False, precision=None):
    return _matmul(a, b, 1, 1, HI if hi else precision)


def _dot_tn(a, b, hi=False, precision=None):
    return _matmul(a, b, 0, 0, HI if hi else precision)


def _iota(shape, dim):
    return lax.broadcasted_iota(jnp.int32, shape, dim)


def _sigmoid(x):
    return 0.5 * (jnp.tanh(0.5 * x) + 1.0)


def _silu(x):
    return x * _sigmoid(x)


def _softplus(x):
    z = jnp.exp(-jnp.abs(x))
    small = z * (1.0 - z * (0.5 - z * (1.0 / 3.0)))
    return jnp.maximum(x, 0.0) + jnp.where(z < 1e-3, small, jnp.log(1.0 + z))


def _pick(n, pref):
    for t in pref:
        if n % t == 0:
            return t
    return n


class _Ride:
    def __init__(self, operands, out_shape, n_sems, start, finish):
        self.operands, self.out_shape, self.n_sems = list(operands), list(out_shape), n_sems
        self.start, self.finish = start, finish


def _pallas(body, operands, *, name, grid, in_specs, out_specs, out_shape, semantics, scratch_shapes=(),
            prefetch=0, ride=None):
    single = not isinstance(out_shape, (list, tuple))
    outs = [out_shape] if single else list(out_shape)
    ospecs = [out_specs] if single else list(out_specs)
    in_specs, scratch = list(in_specs), list(scratch_shapes)
    n_in, n_out, n_sc = len(operands) - prefetch, len(outs), len(scratch)
    kernel = body
    params = _cparams(semantics)
    if ride is not None:
        n_xin, n_xout = len(ride.operands), len(ride.out_shape)

        def kernel(*refs):
            pre, refs = refs[:prefetch], refs[prefetch:]
            ins, refs = refs[:n_in], refs[n_in:]
            xins, refs = refs[:n_xin], refs[n_xin:]
            mains, refs = refs[:n_out], refs[n_out:]
            xouts, refs = refs[:n_xout], refs[n_xout:]
            sc, (send, recv) = refs[:n_sc], refs[n_sc:]
            ids = [pl.program_id(a) for a in range(len(grid))]
            first = functools.reduce(jnp.logical_and, [i == 0 for i in ids])
            last = functools.reduce(jnp.logical_and, [i == g - 1 for i, g in zip(ids, grid)])

            @pl.when(first)
            def _():
                ride.start(xins, xouts, send, recv)

            body(*pre, *ins, *mains, *sc)

            @pl.when(last)
            def _():
                ride.finish(xins, xouts, send, recv)

        operands = list(operands) + ride.operands
        in_specs += [ANY] * n_xin
        ospecs += [ANY] * n_xout
        outs += ride.out_shape
        scratch += [pltpu.SemaphoreType.DMA((ride.n_sems,)), pltpu.SemaphoreType.DMA((ride.n_sems,))]
        params = _cparams(("arbitrary",) * len(grid), has_side_effects=True)
    if prefetch:
        spec = dict(grid_spec=pltpu.PrefetchScalarGridSpec(
            num_scalar_prefetch=prefetch, grid=grid, in_specs=in_specs, out_specs=ospecs, scratch_shapes=scratch))
    else:
        spec = dict(grid=grid, in_specs=in_specs, out_specs=ospecs, scratch_shapes=scratch)
    res = pl.pallas_call(kernel, name=name, out_shape=outs, compiler_params=params, **spec)(*operands)
    main = res[0] if single else list(res[:n_out])
    return main if ride is None else (main, list(res[n_out:]))


def _mm_nn(a, b, out_dtype, name, tm=1024, tn=512, tk=None, cols=None, ride=None):
    M, K = a.shape
    c0, N = (0, b.shape[1]) if cols is None else cols
    tm = _pick(M, (tm, 512, 256, 128))
    tn = _pick(N, (tn, 512, 384, 256, 128))
    tk = K if tk is None else _pick(K, (tk,))
    nk = K // tk
    j0 = c0 // tn
    assert c0 % tn == 0

    def body(a_ref, b_ref, o_ref, *scratch):
        part = _dot(a_ref[...], b_ref[...])
        if nk == 1:
            o_ref[...] = part.astype(out_dtype)
        else:
            acc_ref, = scratch
            k = pl.program_id(2)

            @pl.when(k == 0)
            def _():
                acc_ref[...] = part

            @pl.when(k > 0)
            def _():
                acc_ref[...] += part

            @pl.when(k == nk - 1)
            def _():
                o_ref[...] = acc_ref[...].astype(out_dtype)

    return _pallas(
        body, (a, b), name=name, grid=(M // tm, N // tn, nk),
        in_specs=[pl.BlockSpec((tm, tk), lambda i, j, k: (i, k)),
                  pl.BlockSpec((tk, tn), lambda i, j, k: (k, j + j0))],
        out_specs=pl.BlockSpec((tm, tn), lambda i, j, k: (i, j)),
        out_shape=jax.ShapeDtypeStruct((M, N), out_dtype),
        scratch_shapes=[] if nk == 1 else [pltpu.VMEM((tm, tn), F32)],
        semantics=("parallel", "parallel", "arbitrary"), ride=ride)


def _mm_nt_rhs_outer(a, b, out_dtype, name, tm=256, tn=1024, ride=None):
    M, K = a.shape
    N, _ = b.shape
    tm = _pick(M, (tm, 128))
    tn = _pick(N, (tn, 512, 256, 128))

    def body(a_ref, b_ref, o_ref):
        o_ref[...] = _dot_nt(a_ref[...], b_ref[...]).astype(out_dtype)

    return _pallas(
        body, (a, b), name=name, grid=(N // tn, M // tm),
        in_specs=[pl.BlockSpec((tm, K), lambda j, i: (i, 0)),
                  pl.BlockSpec((tn, K), lambda j, i: (j, 0))],
        out_specs=pl.BlockSpec((tm, tn), lambda j, i: (i, j)),
        out_shape=jax.ShapeDtypeStruct((M, N), out_dtype),
        semantics=("parallel", "parallel"), ride=ride)


WIN_BLOCK = 256


def _mm_windows(a, b, table, nb, name, tm=2048):
    M, K = a.shape
    wb = table.shape[0] // nb
    tm = _pick(M, (tm, 1024, 512, 256, 128))

    def body(tab_ref, a_ref, b_ref, o_ref):
        o_ref[0] = _dot(a_ref[...], b_ref[...])

    return pl.pallas_call(
        body, name=name,
        grid_spec=pltpu.PrefetchScalarGridSpec(
            num_scalar_prefetch=1, grid=(nb, M // tm, wb),
            in_specs=[pl.BlockSpec((tm, K), lambda n, i, t, tab: (i, 0)),
                      pl.BlockSpec((K, WIN_BLOCK), lambda n, i, t, tab: (0, tab[n * wb + t]))],
            out_specs=pl.BlockSpec((1, tm, WIN_BLOCK), lambda n, i, t, tab: (n, i, t))),
        out_shape=jax.ShapeDtypeStruct((nb, M, wb * WIN_BLOCK), F32),
        compiler_params=_cparams(("parallel", "parallel", "arbitrary")),
    )(table, a, b)


def _mm_nn_pair(a0, a1, b, name, tm=512, tn=1024, ride=None):
    M, K = a0.shape
    _, N = b.shape
    tm = _pick(M, (tm, 256, 128))
    tn = _pick(N, (tn, 512, 256, 128))
    ni = M // tm

    def body(a0_ref, a1_ref, b_ref, o_ref):
        p = pl.program_id(0)

        @pl.when(p == 0)
        def _():
            o_ref[...] = _dot(a0_ref[...], b_ref[...])

        @pl.when(p == 1)
        def _():
            o_ref[...] = _dot(a1_ref[...], b_ref[...])

    return _pallas(
        body, (a0, a1, b), name=name, grid=(2, ni, N // tn),
        in_specs=[pl.BlockSpec((tm, K), lambda p, i, j: (i * (1 - p), 0)),
                  pl.BlockSpec((tm, K), lambda p, i, j: (i * p, 0)),
                  pl.BlockSpec((K, tn), lambda p, i, j: (0, j))],
        out_specs=pl.BlockSpec((tm, tn), lambda p, i, j: (p * ni + i, j)),
        out_shape=jax.ShapeDtypeStruct((2 * M, N), F32),
        semantics=("parallel", "parallel", "parallel"), ride=ride)


def _rms_fn(x, w):
    r = lax.rsqrt(jnp.mean(x * x, axis=-1, keepdims=True) + EPS)
    return x * r * w


def _rms_in(x, w, ride=None):
    T, D = x.shape
    tm = _pick(T, (512, 256, 128))

    def body(x_ref, w_ref, o_ref, ot_ref):
        xn = _rms_fn(x_ref[...], w_ref[...])
        o_ref[...] = xn.astype(BF16)
        ot_ref[...] = xn.T.astype(BF16)

    return _pallas(
        body, (x, w), name="rms_in", grid=(T // tm,),
        in_specs=[pl.BlockSpec((tm, D), lambda i: (i, 0)), pl.BlockSpec((1, D), lambda i: (0, 0))],
        out_specs=[pl.BlockSpec((tm, D), lambda i: (i, 0)), pl.BlockSpec((D, tm), lambda i: (0, i))],
        out_shape=[jax.ShapeDtypeStruct((T, D), BF16), jax.ShapeDtypeStruct((D, T), BF16)],
        semantics=("parallel",), ride=ride)


def _rms_in_bwd(x, w, dxn, dh, ride=None):
    T, D = x.shape
    tm = _pick(T, (256, 128))

    def body(x_ref, w_ref, dxn_ref, dh_ref, gx_ref, dw_ref):
        _, vjp = jax.vjp(_rms_fn, x_ref[...], w_ref[...])
        dx, dw = vjp(dxn_ref[...])
        gx_ref[...] = dh_ref[...] + dx

        @pl.when(pl.program_id(0) == 0)
        def _():
            dw_ref[...] = dw

        @pl.when(pl.program_id(0) > 0)
        def _():
            dw_ref[...] += dw

    tile = pl.BlockSpec((tm, D), lambda i: (i, 0))
    row = pl.BlockSpec((1, D), lambda i: (0, 0))
    return _pallas(
        body, (x, w, dxn, dh), name="rms_in_bwd", grid=(T // tm,),
        in_specs=[tile, row, tile, tile], out_specs=[tile, row],
        out_shape=[jax.ShapeDtypeStruct((T, D), F32), jax.ShapeDtypeStruct((1, D), F32)],
        semantics=("arbitrary",), ride=ride)


def _conv_fwd(cat_ref, halo, x, w):
    tm = x.shape[0]
    cat_ref[0:HALO, :] = halo
    cat_ref[HALO:HALO + tm, :] = x
    c = x * w[CONV_WIDTH - 1:CONV_WIDTH, :]
    for k in range(CONV_WIDTH - 1):
        s = CONV_WIDTH - 1 - k
        c = c + cat_ref[pl.ds(HALO - s, tm), :] * w[k:k + 1, :]
    return c


def _lane_to_all(x, lane):
    @jax.custom_vjp
    def f(x):
        return jnp.broadcast_to(x[:, lane:lane + 1], x.shape)

    def f_fwd(x):
        return f(x), None

    def f_bwd(_, g):
        return (jnp.where(_iota(g.shape, 1) == lane, jnp.sum(g, axis=-1, keepdims=True), 0.0),)

    f.defvjp(f_fwd, f_bwd)
    return f(x)


def _gdn_pointwise(c, ba, alog, dtb, H):
    A = H * HEAD_DIM
    s = _silu(c)
    beta = _sigmoid(ba)
    g = -jnp.exp(alog) * _softplus(ba + dtb)
    qs, ks, vs, gbs, bbs = [], [], [], [], []
    for h in range(H):
        lo = h * HEAD_DIM
        q = s[:, lo:lo + HEAD_DIM]
        k = s[:, A + lo:A + lo + HEAD_DIM]
        qs.append(q * lax.rsqrt(jnp.sum(q * q, axis=-1, keepdims=True) + EPS))
        ks.append(k * lax.rsqrt(jnp.sum(k * k, axis=-1, keepdims=True) + EPS))
        vs.append(s[:, 2 * A + lo:2 * A + lo + HEAD_DIM])
        bbs.append(_lane_to_all(beta, h))
        gbs.append(_lane_to_all(g, H + h))
    st = lambda xs: jnp.stack(xs, axis=0)
    return st(qs), st(ks), st(vs), st(gbs), st(bbs)


def _halo_prev(tm):
    return lambda i: (jnp.maximum(i * (tm // HALO) - 1, 0), 0)


def _gdn_pre(proj_m, proj_ba, conv_w, alog_row, dtb_row, H):
    T = proj_m.shape[0]
    A = H * HEAD_DIM
    tm = _pick(T, (256, 128))
    hs = pl.BlockSpec((H, tm, HEAD_DIM), lambda i: (0, i, 0))
    hshape = jax.ShapeDtypeStruct((H, T, HEAD_DIM), F32)

    def body(x_ref, halo_ref, ba_ref, w_ref, al_ref, dt_ref, q_ref, k_ref, v_ref, gb_ref, bb_ref, cat_ref):
        halo = jnp.where(pl.program_id(0) == 0, 0.0, halo_ref[...])
        c = _conv_fwd(cat_ref, halo, x_ref[...], w_ref[...])
        q, k, v, gb, bb = _gdn_pointwise(c, ba_ref[...], al_ref[...], dt_ref[...], H)
        q_ref[...] = q
        k_ref[...] = k
        v_ref[...] = v
        gb_ref[...] = gb
        bb_ref[...] = bb

    return pl.pallas_call(
        body, name="gdn_pre", grid=(T // tm,),
        in_specs=[pl.BlockSpec((tm, 3 * A), lambda i: (i, 0)),
                  pl.BlockSpec((HALO, 3 * A), _halo_prev(tm)),
                  pl.BlockSpec((tm, LANES), lambda i: (i, 0)),
                  pl.BlockSpec((CONV_WIDTH, 3 * A), lambda i: (0, 0)),
                  pl.BlockSpec((1, LANES), lambda i: (0, 0)),
                  pl.BlockSpec((1, LANES), lambda i: (0, 0))],
        out_specs=[hs] * 5, out_shape=[hshape] * 5,
        scratch_shapes=[pltpu.VMEM((HALO + tm, 3 * A), F32)],
        compiler_params=_cparams(("parallel",)),
    )(proj_m, proj_m, proj_ba, conv_w, alog_row, dtb_row)


def _gdn_pre_bwd(proj_m, proj_ba, conv_w, alog_row, dtb_row, dq, dk, dv, dgb, dbb, H, dproj):
    T, n_main = proj_m.shape
    A = H * HEAD_DIM
    tm = _pick(T, (256, 128))
    hs = pl.BlockSpec((H, tm, HEAD_DIM), lambda i: (0, i, 0))
    row = pl.BlockSpec((1, LANES), lambda i: (0, 0))

    def body(x_ref, halo_ref, ba_ref, w_ref, al_ref, dt_ref, dq_ref, dk_ref, dv_ref, dgb_ref, dbb_ref, _,
             dc_ref, dba_ref, dal_ref, ddt_ref, cat_ref):
        halo = jnp.where(pl.program_id(0) == 0, 0.0, halo_ref[...])
        c = _conv_fwd(cat_ref, halo, x_ref[...], w_ref[...])
        _, vjp = jax.vjp(functools.partial(_gdn_pointwise, H=H), c, ba_ref[...], al_ref[...], dt_ref[...])
        dc, dba, dal, ddt = vjp((dq_ref[...], dk_ref[...], dv_ref[...], dgb_ref[...], dbb_ref[...]))
        dc_ref[...] = dc
        dba_ref[:, :LANES] = dba.astype(BF16)
        dba_ref[:, LANES:] = jnp.zeros((tm, WIN_BLOCK - LANES), BF16)

        @pl.when(pl.program_id(0) == 0)
        def _():
            dal_ref[...] = dal
            ddt_ref[...] = ddt

        @pl.when(pl.program_id(0) > 0)
        def _():
            dal_ref[...] += dal
            ddt_ref[...] += ddt

    return pl.pallas_call(
        body, name="gdn_pre_bwd", grid=(T // tm,),
        in_specs=[pl.BlockSpec((tm, 3 * A), lambda i: (i, 0)),
                  pl.BlockSpec((HALO, 3 * A), _halo_prev(tm)),
                  pl.BlockSpec((tm, LANES), lambda i: (i, 0)),
                  pl.BlockSpec((CONV_WIDTH, 3 * A), lambda i: (0, 0)),
                  row, row, hs, hs, hs, hs, hs, ANY],
        out_specs=[pl.BlockSpec((tm, 3 * A), lambda i: (i, 0)),
                   pl.BlockSpec((tm, WIN_BLOCK), lambda i: (i, n_main // WIN_BLOCK)), row, row],
        out_shape=[jax.ShapeDtypeStruct((T, 3 * A), F32), jax.ShapeDtypeStruct(dproj.shape, dproj.dtype),
                   jax.ShapeDtypeStruct((1, LANES), F32), jax.ShapeDtypeStruct((1, LANES), F32)],
        input_output_aliases={11: 1},
        scratch_shapes=[pltpu.VMEM((HALO + tm, 3 * A), F32)],
        compiler_params=_cparams(("arbitrary",)),
    )(proj_m, proj_m, proj_ba, conv_w, alog_row, dtb_row, dq, dk, dv, dgb, dbb, dproj)


def _conv_bwd(proj_m, dc, conv_w, H, dproj):
    T = proj_m.shape[0]
    A = H * HEAD_DIM
    tm = _pick(T, (256, 128))
    nt = T // tm

    def body(x_ref, halo_ref, dc_ref, nxt_ref, w_ref, _, dx_ref, dw_ref):
        i = pl.program_id(0)
        halo = jnp.where(i == 0, 0.0, halo_ref[...])
        xcat = jnp.concatenate([halo, x_ref[...]], axis=0)
        nxt = jnp.where(i == nt - 1, 0.0, nxt_ref[...])
        dc = dc_ref[...]
        dcat = jnp.concatenate([dc, nxt], axis=0)
        w = w_ref[...]
        dx = None
        rows = []
        for k in range(CONV_WIDTH):
            s = CONV_WIDTH - 1 - k
            ds = dcat if s == 0 else pltpu.roll(dcat, tm + HALO - s, 0)
            term = ds[:tm, :] * w[k:k + 1, :]
            dx = term if dx is None else dx + term
            xs = xcat if s == 0 else pltpu.roll(xcat, s, 0)
            rows.append(jnp.sum(dc * xs[HALO:, :], axis=0, keepdims=True))
        dx_ref[...] = dx.astype(BF16)
        dw = jnp.concatenate(rows + [jnp.zeros((HALO - CONV_WIDTH, 3 * A), F32)], axis=0)

        @pl.when(i == 0)
        def _():
            dw_ref[...] = dw

        @pl.when(i > 0)
        def _():
            dw_ref[...] += dw

    return pl.pallas_call(
        body, name="conv_bwd", grid=(nt,),
        in_specs=[pl.BlockSpec((tm, 3 * A), lambda i: (i, 0)),
                  pl.BlockSpec((HALO, 3 * A), _halo_prev(tm)),
                  pl.BlockSpec((tm, 3 * A), lambda i: (i, 0)),
                  pl.BlockSpec((HALO, 3 * A), lambda i: (jnp.minimum((i + 1) * (tm // HALO), T // HALO - 1), 0)),
                  pl.BlockSpec((CONV_WIDTH, 3 * A), lambda i: (0, 0)), ANY],
        out_specs=[pl.BlockSpec((tm, 3 * A), lambda i: (i, 0)),
                   pl.BlockSpec((HALO, 3 * A), lambda i: (0, 0))],
        out_shape=[jax.ShapeDtypeStruct(dproj.shape, dproj.dtype), jax.ShapeDtypeStruct((HALO, 3 * A), F32)],
        input_output_aliases={5: 0},
        compiler_params=_cparams(("arbitrary",)),
    )(proj_m, proj_m, dc, dc, conv_w, dproj)


CHUNK = 128


def _b(x):
    return x.astype(BF16)


@jax.custom_vjp
def _bdot(a, b):
    return _dot(_b(a), _b(b))


def _bdot_f(a, b):
    return _bdot(a, b), (a, b)


def _bdot_b(res, g):
    a, b = res
    return _dot_nt(_b(g), _b(b)), _dot_tn(_b(a), _b(g))


_bdot.defvjp(_bdot_f, _bdot_b)


@jax.custom_vjp
def _bdot_nt(a, b):
    return _dot_nt(_b(a), _b(b))


def _bdot_nt_f(a, b):
    return _bdot_nt(a, b), (a, b)


def _bdot_nt_b(res, g):
    a, b = res
    return _dot(_b(g), _b(b)), _dot_tn(_b(g), _b(a))


_bdot_nt.defvjp(_bdot_nt_f, _bdot_nt_b)


@jax.custom_vjp
def _bdot_tn(a, b):
    return _dot_tn(_b(a), _b(b))


def _bdot_tn_f(a, b):
    return _bdot_tn(a, b), (a, b)


def _bdot_tn_b(res, g):
    a, b = res
    return _dot_nt(_b(b), _b(g)), _dot(_b(a), _b(g))


_bdot_tn.defvjp(_bdot_tn_f, _bdot_tn_b)


def _mask_matmul(m, x):
    hi = _b(x)
    r = x - hi.astype(F32)
    mid = _b(r)
    lo = _b(r - mid.astype(F32))
    return (_dot(m, lo) + _dot(m, mid)) + _dot(m, hi)


@jax.custom_vjp
def _mask_dot(m, mt, x):
    return _mask_matmul(m, x)


def _mask_dot_f(m, mt, x):
    return _mask_matmul(m, x), (m, mt)


def _mask_dot_b(res, g):
    m, mt = res
    return jnp.zeros_like(m), jnp.zeros_like(mt), _mask_matmul(mt, g)


_mask_dot.defvjp(_mask_dot_f, _mask_dot_b)

HIGH = lax.Precision.HIGH


def _unit_lower_inverse(L):
    n = L.shape[-1]
    X = -L
    Q = X
    for _ in range(n.bit_length() - 2):
        X = _dot(_b(X), _b(X))
        Q = Q + X + _dot(_b(Q), _b(X))
    return (_iota((n, n), 0) == _iota((n, n), 1)).astype(F32) + Q


@jax.custom_vjp
def _known_inverse(L, P):
    return P


def _known_inverse_f(L, P):
    return P, P


def _known_inverse_b(P, g):
    n = P.shape[-1]
    Q = _b(P - (_iota((n, n), 0) == _iota((n, n), 1)).astype(F32))
    t = g + _dot_tn(Q, _b(g))
    return -(t + _dot_nt(_b(t), Q)), jnp.zeros_like(P)


_known_inverse.defvjp(_known_inverse_f, _known_inverse_b)


def _gdn_prep_fn(q, k, v, gb, bb, P_known=None):
    n = CHUNK
    row, col = _iota((n, n), 0), _iota((n, n), 1)
    incl, strict = row >= col, row > col
    bc = lambda m: jnp.broadcast_to(_b(m.astype(F32)), q.shape[:1] + (n, n))
    tril, triu, ones = bc(incl), bc(row <= col), bc(row >= 0)
    gc = _mask_dot(tril, triu, gb)
    gl = _mask_dot(ones, ones, gb)
    decay = jnp.where(incl, jnp.exp(jnp.where(incl, gc - jnp.swapaxes(gc, 1, 2), 0.0)), 0.0)
    kb = k * bb
    vb = v * bb
    qs = q * (HEAD_DIM ** -0.5)
    L = jnp.where(strict, _bdot_nt(kb, k) * decay, 0.0)
    P = _unit_lower_inverse(L) if P_known is None else _known_inverse(L, P_known)
    egc = jnp.exp(gc)
    u = _bdot(P, vb)
    w = _bdot(P, kb * egc)
    attn = jnp.where(incl, _bdot_nt(qs, k) * decay, 0.0)
    qg = qs * egc
    kdec = k * jnp.exp(gl - gc)
    eg = jnp.exp(gl)
    if P_known is None:
        return u, w, qg, kdec, attn, eg, P
    return u, w, qg, kdec, attn, eg


def _gdn_chain_fn(S, qg, kdec, u, w, attn, eg):
    v_new = u - _bdot(w, S)
    o = _bdot(qg, S) + _bdot(attn, v_new)
    return o, S * eg + _bdot_tn(kdec, v_new)


def _gdn_prep(q, k, v, gb, bb):
    H, T, _ = q.shape
    pb = _pick(T // CHUNK, (8, 4, 2, 1))
    hs = pl.BlockSpec((1, CHUNK * pb, HEAD_DIM), lambda h, n: (h, n, 0))
    hshape = jax.ShapeDtypeStruct((H, T, HEAD_DIM), F32)

    def body(q_ref, k_ref, v_ref, gb_ref, bb_ref, *out_refs):
        chunks = lambda ref: ref[0].reshape(pb, CHUNK, HEAD_DIM)
        outs = _gdn_prep_fn(chunks(q_ref), chunks(k_ref), chunks(v_ref), chunks(gb_ref), chunks(bb_ref))
        for ref, val in zip(out_refs, outs):
            ref[0] = val.reshape(pb * CHUNK, HEAD_DIM)

    return pl.pallas_call(
        body, name="gdn_prep", grid=(H, T // (CHUNK * pb)),
        in_specs=[hs] * 5, out_specs=[hs] * 7, out_shape=[hshape] * 7,
        compiler_params=_cparams(("parallel", "parallel")),
    )(q, k, v, gb, bb)


def _gdn_prep_bwd(q, k, v, gb, bb, pinv, du, dw, dqg, dkd, dat, deg):
    H, T, _ = q.shape
    pb = _pick(T // CHUNK, (8, 4, 2, 1))
    hs = pl.BlockSpec((1, CHUNK * pb, HEAD_DIM), lambda h, n: (h, n, 0))
    hshape = jax.ShapeDtypeStruct((H, T, HEAD_DIM), F32)

    def body(*refs):
        in_refs, p_ref, ct_refs, out_refs = refs[:5], refs[5], refs[6:12], refs[12:]
        chunks = lambda ref: ref[0].reshape(pb, CHUNK, HEAD_DIM)
        P = chunks(p_ref)
        _, vjp = jax.vjp(lambda *a: _gdn_prep_fn(*a, P_known=P), *[chunks(r) for r in in_refs])
        grads = vjp(tuple(chunks(r) for r in ct_refs))
        for ref, val in zip(out_refs, grads):
            ref[0] = val.reshape(pb * CHUNK, HEAD_DIM)

    return pl.pallas_call(
        body, name="gdn_prep_bwd", grid=(H, T // (CHUNK * pb)),
        in_specs=[hs] * 12, out_specs=[hs] * 5, out_shape=[hshape] * 5,
        compiler_params=_cparams(("parallel", "parallel")),
    )(q, k, v, gb, bb, pinv, du, dw, dqg, dkd, dat, deg)


def _gdn_chain(qg, kd, u, w, attn, eg):
    H, T, _ = qg.shape
    N = T // CHUNK
    hs = pl.BlockSpec((H, CHUNK, HEAD_DIM), lambda n: (0, n, 0))
    ss = pl.BlockSpec((1, H, HEAD_DIM, HEAD_DIM), lambda n: (n, 0, 0, 0))

    def body(qg_ref, kd_ref, u_ref, w_ref, at_ref, eg_ref, o_ref, sall_ref, s_ref):
        @pl.when(pl.program_id(0) == 0)
        def _():
            s_ref[...] = jnp.zeros_like(s_ref)

        S = s_ref[...]
        sall_ref[0] = S
        o, S2 = _gdn_chain_fn(S, qg_ref[...], kd_ref[...], u_ref[...], w_ref[...], at_ref[...], eg_ref[...])
        o_ref[...] = o
        s_ref[...] = S2

    return pl.pallas_call(
        body, name="gdn_chain", grid=(N,),
        in_specs=[hs] * 6, out_specs=[hs, ss],
        out_shape=[jax.ShapeDtypeStruct((H, T, HEAD_DIM), F32),
                   jax.ShapeDtypeStruct((N, H, HEAD_DIM, HEAD_DIM), F32)],
        scratch_shapes=[pltpu.VMEM((H, HEAD_DIM, HEAD_DIM), F32)],
        compiler_params=_cparams(("arbitrary",)),
    )(qg, kd, u, w, attn, eg)


def _gdn_chain_bwd(qg, kd, u, w, attn, eg, sall, do):
    H, T, _ = qg.shape
    N = T // CHUNK
    hs = pl.BlockSpec((H, CHUNK, HEAD_DIM), lambda n: (0, N - 1 - n, 0))
    ss = pl.BlockSpec((1, H, HEAD_DIM, HEAD_DIM), lambda n: (N - 1 - n, 0, 0, 0))
    hshape = jax.ShapeDtypeStruct((H, T, HEAD_DIM), F32)

    def body(qg_ref, kd_ref, u_ref, w_ref, at_ref, eg_ref, sall_ref, do_ref, *rest):
        out_refs, ds_ref = rest[:6], rest[6]

        @pl.when(pl.program_id(0) == 0)
        def _():
            ds_ref[...] = jnp.zeros_like(ds_ref)

        _, vjp = jax.vjp(_gdn_chain_fn, sall_ref[0], qg_ref[...], kd_ref[...], u_ref[...], w_ref[...],
                         at_ref[...], eg_ref[...])
        grads = vjp((do_ref[...], ds_ref[...]))
        ds_ref[...] = grads[0]
        for ref, val in zip(out_refs, grads[1:]):
            ref[...] = val

    return pl.pallas_call(
        body, name="gdn_chain_bwd", grid=(N,),
        in_specs=[hs] * 6 + [ss, hs], out_specs=[hs] * 6, out_shape=[hshape] * 6,
        scratch_shapes=[pltpu.VMEM((H, HEAD_DIM, HEAD_DIM), F32)],
        compiler_params=_cparams(("arbitrary",)),
    )(qg, kd, u, w, attn, eg, sall, do)


def _post_fn(ogs, za, hw):
    outs = []
    for h, o in enumerate(ogs):
        r = lax.rsqrt(jnp.mean(o * o, axis=-1, keepdims=True) + EPS)
        outs.append(o * r * hw * _silu(za[:, h * HEAD_DIM:(h + 1) * HEAD_DIM]))
    return jnp.concatenate(outs, axis=1)


def _gdn_post(og, proj_m, hw):
    H, T, _ = og.shape
    A = H * HEAD_DIM
    tm = _pick(T, (512, 256, 128))

    def body(og_ref, za_ref, hw_ref, o_ref, ot_ref):
        o = _post_fn(tuple(og_ref[h] for h in range(H)), za_ref[...], hw_ref[...])
        o_ref[...] = o.astype(BF16)
        ot_ref[...] = o.T.astype(BF16)

    return pl.pallas_call(
        body, name="gdn_post", grid=(T // tm,),
        in_specs=[pl.BlockSpec((H, tm, HEAD_DIM), lambda i: (0, i, 0)),
                  pl.BlockSpec((tm, A), lambda i: (i, ZA_BLOCK)),
                  pl.BlockSpec((1, HEAD_DIM), lambda i: (0, 0))],
        out_specs=[pl.BlockSpec((tm, A), lambda i: (i, 0)), pl.BlockSpec((A, tm), lambda i: (0, i))],
        out_shape=[jax.ShapeDtypeStruct((T, A), BF16), jax.ShapeDtypeStruct((A, T), BF16)],
        compiler_params=_cparams(("parallel",)),
    )(og, proj_m, hw)


def _gdn_post_bwd(og, proj_m, hw, d_o, dproj):
    H, T, _ = og.shape
    A = H * HEAD_DIM
    tm = _pick(T, (256, 128))

    def body(og_ref, za_ref, hw_ref, do_ref, _, dog_ref, dza_ref, dhw_ref):
        _, vjp = jax.vjp(_post_fn, tuple(og_ref[h] for h in range(H)), za_ref[...], hw_ref[...])
        dog, dza, dhw = vjp(do_ref[...])
        for h in range(H):
            dog_ref[h] = dog[h]
        dza_ref[...] = dza.astype(BF16)

        @pl.when(pl.program_id(0) == 0)
        def _():
            dhw_ref[...] = dhw

        @pl.when(pl.program_id(0) > 0)
        def _():
            dhw_ref[...] += dhw

    return pl.pallas_call(
        body, name="gdn_post_bwd", grid=(T // tm,),
        in_specs=[pl.BlockSpec((H, tm, HEAD_DIM), lambda i: (0, i, 0)),
                  pl.BlockSpec((tm, A), lambda i: (i, ZA_BLOCK)),
                  pl.BlockSpec((1, HEAD_DIM), lambda i: (0, 0)),
                  pl.BlockSpec((tm, A), lambda i: (i, 0)), ANY],
        out_specs=[pl.BlockSpec((H, tm, HEAD_DIM), lambda i: (0, i, 0)),
                   pl.BlockSpec((tm, A), lambda i: (i, ZA_BLOCK)),
                   pl.BlockSpec((1, HEAD_DIM), lambda i: (0, 0))],
        out_shape=[jax.ShapeDtypeStruct((H, T, HEAD_DIM), F32), jax.ShapeDtypeStruct(dproj.shape, dproj.dtype),
                   jax.ShapeDtypeStruct((1, HEAD_DIM), F32)],
        input_output_aliases={4: 1},
        compiler_params=_cparams(("arbitrary",)),
    )(og, proj_m, hw, d_o, dproj)


def _sgu_fn(ub, vb, zb, lw, lb, W, bbc):
    G = len(W)
    tm = ub.shape[0]
    mu = jnp.mean(vb, axis=-1, keepdims=True)
    xc = vb - mu
    var = jnp.mean(xc * xc, axis=-1, keepdims=True)
    vn = xc * lax.rsqrt(var + EPS) * lw + lb
    mask = _iota((CHUNK_B, CHUNK_B), 0) >= _iota((CHUNK_B, CHUNK_B), 1)
    cols = []
    for g in range(G):
        wm = jnp.where(mask, W[g], 0.0).astype(BF16)
        rows = []
        for c in range(tm // CHUNK_B):
            blk = vn[c * CHUNK_B:(c + 1) * CHUNK_B, g * HEAD_DIM:(g + 1) * HEAD_DIM].astype(BF16)
            rows.append(_dot(wm, blk) + bbc[g])
        cols.append(jnp.concatenate(rows, axis=0) if len(rows) > 1 else rows[0])
    s = jnp.concatenate(cols, axis=1)
    return ub * s * _silu(zb)


ZA_BLOCK = 6


def _sgu_cols(A, B):
    assert A == B
    return 3, 4, 5


def _sgu_fwd(proj_m, lw, lb, W, bbc, A):
    T = proj_m.shape[0]
    G = W.shape[0]
    B = G * HEAD_DIM
    tm = _pick(T, (256, 128))
    cu, cv, cz = _sgu_cols(A, B)

    def body(u_ref, v_ref, z_ref, lw_ref, lb_ref, w_ref, b_ref, o_ref, ot_ref):
        o = _sgu_fn(u_ref[...], v_ref[...], z_ref[...], lw_ref[...], lb_ref[...],
                    tuple(w_ref[g] for g in range(G)), tuple(b_ref[g] for g in range(G)))
        o_ref[...] = o.astype(BF16)
        ot_ref[...] = o.T.astype(BF16)

    row = pl.BlockSpec((1, B), lambda i: (0, 0))
    cube = pl.BlockSpec((G, CHUNK_B, CHUNK_B), lambda i: (0, 0, 0))
    return pl.pallas_call(
        body, name="sgu_fwd", grid=(T // tm,),
        in_specs=[pl.BlockSpec((tm, B), lambda i: (i, cu)), pl.BlockSpec((tm, B), lambda i: (i, cv)),
                  pl.BlockSpec((tm, B), lambda i: (i, cz)), row, row, cube, cube],
        out_specs=[pl.BlockSpec((tm, B), lambda i: (i, 0)), pl.BlockSpec((B, tm), lambda i: (0, i))],
        out_shape=[jax.ShapeDtypeStruct((T, B), BF16), jax.ShapeDtypeStruct((B, T), BF16)],
        compiler_params=_cparams(("parallel",)),
    )(proj_m, proj_m, proj_m, lw, lb, W, bbc)


def _sgu_bwd(proj_m, lw, lb, W, bbc, d_o, A, dproj):
    T = proj_m.shape[0]
    G = W.shape[0]
    B = G * HEAD_DIM
    tm = _pick(T, (256, 128))
    nt = T // tm
    cu, cv, cz = _sgu_cols(A, B)

    def body(u_ref, v_ref, z_ref, lw_ref, lb_ref, w_ref, b_ref, do_ref, _,
             dp_ref, dlw_ref, dlb_ref, dw_ref, db_ref, dbb_ref):
        _, vjp = jax.vjp(_sgu_fn, u_ref[...], v_ref[...], z_ref[...], lw_ref[...], lb_ref[...],
                         tuple(w_ref[g] for g in range(G)), tuple(b_ref[g] for g in range(G)))
        du, dv, dz, dlw, dlb, dW, dbb = vjp(do_ref[...])
        dW, dbb = jnp.stack(dW, axis=0), jnp.stack(dbb, axis=0)
        dp_ref[:, 0:B] = du.astype(BF16)
        dp_ref[:, B:2 * B] = dv.astype(BF16)
        dp_ref[:, 2 * B:3 * B] = dz.astype(BF16)
        i = pl.program_id(0)

        @pl.when(i == 0)
        def _():
            dlw_ref[...] = dlw
            dlb_ref[...] = dlb
            dw_ref[...] = dW
            dbb_ref[...] = dbb

        @pl.when(i > 0)
        def _():
            dlw_ref[...] += dlw
            dlb_ref[...] += dlb
            dw_ref[...] += dW
            dbb_ref[...] += dbb

        @pl.when(i == nt - 1)
        def _():
            db_ref[...] = jnp.sum(dbb_ref[...], axis=-1, keepdims=True)

    row = pl.BlockSpec((1, B), lambda i: (0, 0))
    cube = pl.BlockSpec((G, CHUNK_B, CHUNK_B), lambda i: (0, 0, 0))
    return pl.pallas_call(
        body, name="sgu_bwd", grid=(nt,),
        in_specs=[pl.BlockSpec((tm, B), lambda i: (i, cu)), pl.BlockSpec((tm, B), lambda i: (i, cv)),
                  pl.BlockSpec((tm, B), lambda i: (i, cz)), row, row, cube, cube,
                  pl.BlockSpec((tm, B), lambda i: (i, A // B)), ANY],
        out_specs=[pl.BlockSpec((tm, 3 * B), lambda i: (i, 1)), row, row, cube,
                   pl.BlockSpec((G, CHUNK_B, 1), lambda i: (0, 0, 0))],
        out_shape=[jax.ShapeDtypeStruct(dproj.shape, dproj.dtype), jax.ShapeDtypeStruct((1, B), F32),
                   jax.ShapeDtypeStruct((1, B), F32), jax.ShapeDtypeStruct((G, CHUNK_B, CHUNK_B), F32),
                   jax.ShapeDtypeStruct((G, CHUNK_B, 1), F32)],
        input_output_aliases={8: 0},
        scratch_shapes=[pltpu.VMEM((G, CHUNK_B, CHUNK_B), F32)],
        compiler_params=_cparams(("arbitrary",)),
    )(proj_m, proj_m, proj_m, lw, lb, W, bbc, d_o, dproj)


def _head_fn(mix, x, fw, tgt):
    h = x + mix
    y = _rms_fn(h, fw)
    e = y - tgt
    return 0.5 * jnp.sum(jnp.mean(e * e, axis=-1, keepdims=True), axis=0, keepdims=True)


def _out_proj_loss(oa, ob, wout, x, tgt, fw):
    T, A = oa.shape
    B = ob.shape[1]
    D = x.shape[1]
    tm = _pick(T, (256, 128))

    def body(oa_ref, ob_ref, w_ref, x_ref, t_ref, fw_ref, dh_ref, dhb_ref, loss_ref, dfw_ref):
        mix = _dot(oa_ref[...], w_ref[0:A, :]) + _dot(ob_ref[...], w_ref[A:A + B, :])
        xv, tv = x_ref[...], t_ref[...]
        loss, vjp = jax.vjp(lambda m, f: _head_fn(m, xv, f, tv), mix, fw_ref[...])
        dh, dfw = vjp(jnp.ones((1, 1), F32))
        dh_ref[...] = dh
        dhb_ref[...] = dh.astype(BF16)
        lrow = jnp.broadcast_to(loss, (1, LANES))

        @pl.when(pl.program_id(0) == 0)
        def _():
            loss_ref[...] = lrow
            dfw_ref[...] = dfw

        @pl.when(pl.program_id(0) > 0)
        def _():
            loss_ref[...] += lrow
            dfw_ref[...] += dfw

    tile = pl.BlockSpec((tm, D), lambda i: (i, 0))
    return pl.pallas_call(
        body, name="out_proj_loss", grid=(T // tm,),
        in_specs=[pl.BlockSpec((tm, A), lambda i: (i, 0)), pl.BlockSpec((tm, B), lambda i: (i, 0)),
                  pl.BlockSpec((A + B, D), lambda i: (0, 0)), tile, tile,
                  pl.BlockSpec((1, D), lambda i: (0, 0))],
        out_specs=[tile, tile, pl.BlockSpec((1, LANES), lambda i: (0, 0)),
                   pl.BlockSpec((1, D), lambda i: (0, 0))],
        out_shape=[jax.ShapeDtypeStruct((T, D), F32), jax.ShapeDtypeStruct((T, D), BF16),
                   jax.ShapeDtypeStruct((1, LANES), F32), jax.ShapeDtypeStruct((1, D), F32)],
        compiler_params=_cparams(("arbitrary",)),
    )(oa, ob, wout, x, tgt, fw)


def _adamw(w, g, m, v, name):
    R, Cn = w.shape
    cap = max(8, 512 * 1024 // Cn)
    tr = max(t for t in range(8, min(R, cap) + 1, 8) if R % t == 0) if R > cap else R

    def body(w_ref, g_ref, m_ref, v_ref, d_ref, mo_ref, vo_ref):
        g = g_ref[...]
        m = ADAM_B1 * m_ref[...] + (1.0 - ADAM_B1) * g
        v = ADAM_B2 * v_ref[...] + (1.0 - ADAM_B2) * jnp.square(g)
        m_hat = m / (1.0 - ADAM_B1 ** ADAM_STEP)
        v_hat = v / (1.0 - ADAM_B2 ** ADAM_STEP)
        d_ref[...] = -ADAM_LR * (m_hat / (jnp.sqrt(v_hat) + ADAM_EPS) + ADAM_WD * w_ref[...])
        mo_ref[...] = m
        vo_ref[...] = v

    tile = pl.BlockSpec((tr, Cn), lambda i: (i, 0))
    shape = jax.ShapeDtypeStruct((R, Cn), F32)
    return pl.pallas_call(
        body, name=name, grid=(R // tr,), in_specs=[tile] * 4, out_specs=[tile] * 3,
        out_shape=[shape] * 3, compiler_params=_cparams(("parallel",)),
    )(w, g, m, v)


def _place():
    x, y, c = lax.axis_index("x"), lax.axis_index("y"), lax.axis_index("c")
    others = [(1 - x, y), (x, 1 - y), (1 - x, 1 - y)]
    return x, y, c, others


def _chip_index(px, py):
    return 2 * px + py


ANY = pl.BlockSpec(memory_space=pl.ANY)


def _gather_ride(blocks, split):
    n = len(blocks)

    def plan(in_refs, out_refs, send_sems, recv_sems):
        x, y, c, _ = _place()
        me, kx, ky, kd = (_chip_index(px, py) for px, py in ((x, y), (1 - x, y), (x, 1 - y), (1 - x, 1 - y)))
        to_x, to_y, to_s = (1 - x, y, c), (x, 1 - y, c), (x, y, 1 - c)

        def copy(sem, src, dst, to):
            return pltpu.make_async_remote_copy(src_ref=src, dst_ref=dst, send_sem=send_sems.at[sem],
                                                recv_sem=recv_sems.at[sem], device_id=to, device_id_type=MESH_ID)

        first, second, third, awaited = [], [], [], []
        for a in range(n):
            out, s0 = out_refs[a], 8 * a
            if not split[a]:
                for j, (k, to) in enumerate(((kx, to_x), (ky, to_y), (kd, (1 - x, 1 - y, c)))):
                    first.append(lambda j=j, to=to, a=a, out=out, s0=s0: copy(s0 + j, in_refs[a], out.at[me], to))
                    awaited.append((lambda j=j, k=k, to=to, out=out, s0=s0: copy(s0 + j, out.at[k], out.at[k], to),
                                    None))
                continue
            h = blocks[a].shape[0] // 2
            q = h // 2
            half = lambda k, core, out=out, h=h: out.at[k, pl.ds(core * h, h), :]
            quarter = lambda k, core, i, out=out, h=h, q=q: out.at[k, pl.ds(core * h + i * q, q), :]
            mine = in_refs[a].at[pl.ds(c * h, h), :]
            first.append(lambda s0=s0, mine=mine, half=half: copy(s0, mine, half(me, c), to_x))
            first.append(lambda s0=s0, mine=mine, half=half: copy(s0 + 1, mine, half(me, c), to_y))
            fwd0 = lambda s0=s0, quarter=quarter: copy(s0 + 2, quarter(kx, c, 0), quarter(kx, c, 0), to_y)
            fwd1 = lambda s0=s0, quarter=quarter: copy(s0 + 3, quarter(ky, c, 1), quarter(ky, c, 1), to_x)
            pieces = [(s0 + 0, lambda half=half: half(kx, c), lambda half=half: half(kx, 1 - c), to_x, fwd0),
                      (s0 + 1, lambda half=half: half(ky, c), lambda half=half: half(ky, 1 - c), to_y, fwd1),
                      (s0 + 2, lambda quarter=quarter: quarter(kd, c, 0), lambda quarter=quarter: quarter(kd, 1 - c, 0),
                       to_y, None),
                      (s0 + 3, lambda quarter=quarter: quarter(kd, c, 1), lambda quarter=quarter: quarter(kd, 1 - c, 1),
                       to_x, None)]
            for i, (sem, here, there, frm, fwd) in enumerate(pieces):
                passing = lambda s0=s0, i=i, here=here: copy(s0 + 4 + i, here(), here(), to_s)
                awaited.append((lambda sem=sem, here=here, frm=frm: copy(sem, here(), here(), frm), (fwd, passing)))
                if fwd is not None:
                    second.append(fwd)
                third.append((passing, lambda s0=s0, i=i, there=there: copy(s0 + 4 + i, there(), there(), to_s)))
        return first, second, third, awaited

    def start(*refs):
        for send in plan(*refs)[0]:
            send().start()

    def finish(*refs):
        first, second, third, awaited = plan(*refs)
        for arrival, then in awaited:
            arrival().wait_recv()
            for nxt in (then or ()):
                if nxt is not None:
                    nxt().start()
        for _, from_sibling in third:
            from_sibling().wait_recv()
        for send in first + second + [p for p, _ in third]:
            send().wait_send()

    shapes = [jax.ShapeDtypeStruct((N_CHIPS,) + b.shape, b.dtype) for b in blocks]
    return _Ride(blocks, shapes, 8 * n, start, finish)


def _put_own(gathered, own):
    me = _chip_index(lax.axis_index("x"), lax.axis_index("y"))
    return lax.dynamic_update_index_in_dim(gathered, own, me, 0)


def _allreduce_small(buf):
    R, L = buf.shape

    def body(in_ref, out_ref, sib_ref, pair_ref, chips_ref, send_sems, recv_sems):
        x, y, c, others = _place()
        me = _chip_index(x, y)
        sibling = (x, y, 1 - c)
        cp = pltpu.make_async_remote_copy(src_ref=in_ref, dst_ref=sib_ref, send_sem=send_sems.at[0],
                                          recv_sem=recv_sems.at[0], device_id=sibling, device_id_type=MESH_ID)
        cp.start()
        cp.wait()
        pair_ref[...] = in_ref[...] + sib_ref[...]
        sends = []
        for j, chip in enumerate(others):
            s = pltpu.make_async_remote_copy(src_ref=pair_ref, dst_ref=chips_ref.at[me],
                                             send_sem=send_sems.at[1 + j], recv_sem=recv_sems.at[1 + j],
                                             device_id=(*chip, c), device_id_type=MESH_ID)
            s.start()
            sends.append(s)
        chips_ref[me] = pair_ref[...]
        for j, chip in enumerate(others):
            k = _chip_index(*chip)
            pltpu.make_async_remote_copy(src_ref=pair_ref, dst_ref=chips_ref.at[k], send_sem=send_sems.at[1 + j],
                                         recv_sem=recv_sems.at[1 + j], device_id=(*chip, c),
                                         device_id_type=MESH_ID).wait_recv()
        for s in sends:
            s.wait_send()
        out_ref[...] = ((chips_ref[0] + chips_ref[1]) + chips_ref[2]) + chips_ref[3]

    vm = pl.BlockSpec(memory_space=pltpu.VMEM)
    return pl.pallas_call(
        body, name="allreduce_small", in_specs=[vm], out_specs=vm,
        out_shape=jax.ShapeDtypeStruct((R, L), F32),
        scratch_shapes=[pltpu.VMEM((R, L), F32), pltpu.VMEM((R, L), F32), pltpu.VMEM((N_CHIPS, R, L), F32),
                        pltpu.SemaphoreType.DMA((4,)), pltpu.SemaphoreType.DMA((4,))],
        compiler_params=pltpu.CompilerParams(vmem_limit_bytes=VMEM_LIMIT),
    )(buf)


def _pair_ride(g):
    nb, R, Cn = g.shape
    h = R // 2

    def copy(in_refs, out_refs, send_sems, recv_sems):
        x, y, c, _ = _place()
        return pltpu.make_async_remote_copy(src_ref=in_refs[0].at[:, pl.ds((1 - c) * h, h), :], dst_ref=out_refs[0],
                                            send_sem=send_sems.at[0], recv_sem=recv_sems.at[0],
                                            device_id=(x, y, 1 - c), device_id_type=MESH_ID)

    return _Ride([g], [jax.ShapeDtypeStruct((nb, h, Cn), g.dtype)], 1,
                 lambda *refs: copy(*refs).start(), lambda *refs: copy(*refs).wait())


def _pair_sum(g, land, c_arr, name, ride=None):
    nb, R, Cn = g.shape
    hr = R // 2
    tr = _pick(hr, (256, 128, 64, 32, 16))
    nt = hr // tr

    def body(c_ref, g_ref, l_ref, o_ref):
        o_ref[...] = (g_ref[...] + l_ref[...]).astype(BF16)

    return _pallas(
        body, (c_arr, g, land), name=name, prefetch=1, grid=(nb, nt),
        in_specs=[pl.BlockSpec((1, tr, Cn), lambda b, i, c_ref: (b, c_ref[0] * nt + i, 0)),
                  pl.BlockSpec((1, tr, Cn), lambda b, i, c_ref: (b, i, 0))],
        out_specs=pl.BlockSpec((1, tr, Cn), lambda b, i, c_ref: (b, i, 0)),
        out_shape=jax.ShapeDtypeStruct((nb, hr, Cn), BF16),
        semantics=("parallel", "parallel"), ride=ride)


def _chip_ride(parts):
    m = len(parts)

    def copies(in_refs, out_refs, send_sems, recv_sems):
        x, y, c, others = _place()
        me = _chip_index(x, y)
        def mk(j, chip, n, landing):
            k = _chip_index(*chip)
            return pltpu.make_async_remote_copy(
                src_ref=in_refs[n].at[k], dst_ref=out_refs[n].at[landing(k)], send_sem=send_sems.at[m * j + n],
                recv_sem=recv_sems.at[m * j + n], device_id=(*chip, c), device_id_type=MESH_ID)

        pairs = [(j, chip, n) for j, chip in enumerate(others) for n in range(m)]
        return pairs, (lambda *p: mk(*p, lambda k: me)), (lambda *p: mk(*p, lambda k: k))

    def start(*refs):
        pairs, send, _ = copies(*refs)
        for p in pairs:
            send(*p).start()

    def finish(*refs):
        pairs, send, arrival = copies(*refs)
        for p in pairs:
            arrival(*p).wait_recv()
        for p in pairs:
            send(*p).wait_send()

    return _Ride(parts, [jax.ShapeDtypeStruct(p.shape, p.dtype) for p in parts], 3 * m, start, finish)


def _put_own_slot(q, p):
    me = _chip_index(lax.axis_index("x"), lax.axis_index("y"))
    return lax.dynamic_update_index_in_dim(q, lax.dynamic_index_in_dim(p, me, 0, keepdims=False), me, 0)


def _chip_sum(q, c_arr, name):
    nb, hr, Cn = q.shape
    tr = _pick(hr, (256, 128, 64, 32, 16))
    nt = hr // tr

    def body(c_ref, q_ref, o_ref):
        f = lambda k: q_ref[k].astype(F32)
        o_ref[...] = ((f(0) + f(1)) + f(2)) + f(3)

    return _pallas(
        body, (c_arr, q), name=name, prefetch=1, grid=(nt,),
        in_specs=[pl.BlockSpec((nb, tr, Cn), lambda i, c_ref: (0, i, 0))],
        out_specs=pl.BlockSpec((tr, Cn), lambda i, c_ref: (c_ref[0] * nt + i, 0)),
        out_shape=jax.ShapeDtypeStruct((2 * hr, Cn), F32),
        semantics=("parallel",))


def _sibling_fill(fw, fo):
    def body(_, __, fw_ref, fo_ref, send_sems, recv_sems):
        x, y, c, _ = _place()
        copies = []
        for n, ref in enumerate((fw_ref, fo_ref)):
            h = ref.shape[0] // 2
            mine = ref.at[pl.ds(c * h, h), :]
            theirs = ref.at[pl.ds((1 - c) * h, h), :]
            mk = lambda src, dst: pltpu.make_async_remote_copy(
                src_ref=src, dst_ref=dst, send_sem=send_sems.at[n], recv_sem=recv_sems.at[n],
                device_id=(x, y, 1 - c), device_id_type=MESH_ID)
            send = mk(mine, mine)
            send.start()
            copies.append((send, mk(theirs, theirs)))
        for send, arrival in copies:
            arrival.wait_recv()
            send.wait_send()

    return pl.pallas_call(
        body, name="sibling_fill", in_specs=[ANY, ANY], out_specs=[ANY, ANY],
        out_shape=[jax.ShapeDtypeStruct(fw.shape, F32), jax.ShapeDtypeStruct(fo.shape, F32)],
        input_output_aliases={0: 0, 1: 1},
        scratch_shapes=[pltpu.SemaphoreType.DMA((2,)), pltpu.SemaphoreType.DMA((2,))],
        compiler_params=pltpu.CompilerParams(has_side_effects=True),
    )(fw, fo)


class _Layout:
    def __init__(self, H, G, nb, Cb):
        A, B = H * HEAD_DIM, G * HEAD_DIM
        self.n_main = 4 * A + 3 * B
        self.k = -(-(self.n_main + LANES) // WIN_BLOCK) * WIN_BLOCK
        cuts = [0, 3 * A, 4 * A, 4 * A + 2 * H, nb * Cb]
        starts = [0, 3 * A + 3 * B, self.n_main, 3 * A]
        self.pieces = []
        self.windows, self.runs = [], []
        for n in range(nb):
            segs = []
            for s in range(4):
                lo, hi = max(cuts[s], n * Cb), min(cuts[s + 1], (n + 1) * Cb)
                if lo < hi:
                    segs.append((starts[s] + lo - cuts[s], lo - n * Cb, hi - lo))
            self.pieces += [(own, n, col, ln) for own, col, ln in segs]
            blocks = sorted({b for own, _, ln in segs for b in range(own // WIN_BLOCK, (own + ln - 1) // WIN_BLOCK + 1)})
            self.windows.append(blocks)
            self.runs.append([(blocks.index(own // WIN_BLOCK) * WIN_BLOCK + own % WIN_BLOCK, ln)
                              for own, _, ln in segs])
        self.wb = max(len(b) for b in self.windows)
        self.table = [b + [b[-1]] * (self.wb - len(b)) for b in self.windows]
        self.pieces.sort()

    def to_own_order(self, g_in):
        D = g_in.shape[1]
        cols, at = [], 0
        for own, n, col, ln in self.pieces:
            if own > at:
                cols.append(jnp.zeros((D, own - at), g_in.dtype))
            cols.append(g_in[n, :, col:col + ln])
            at = own + ln
        if at < self.k:
            cols.append(jnp.zeros((D, self.k - at), g_in.dtype))
        return jnp.concatenate(cols, axis=1)

    def from_window(self, win, chip, Cb):
        pick = lambda runs: (lambda w: jnp.concatenate([w[:, c:c + ln] for c, ln in runs], axis=1))
        return lax.switch(chip, [pick(r) for r in self.runs], win)


def _device_step(x, tgt, norm_w, win_b, wout_b, conv_b, a_log, dt_bias, head_norm_w, sgu_ln_w, sgu_ln_b,
                 w_spatial, b_spatial, final_norm_w, c_arr):
    T, D = x.shape
    H = a_log.shape[1]
    A = H * HEAD_DIM
    G = w_spatial.shape[0]
    B = G * HEAD_DIM
    nb, Cb, Rb = N_CHIPS, win_b.shape[1], wout_b.shape[0]
    lay = _Layout(H, G, nb, Cb)
    alog_row = jnp.pad(a_log, ((0, 0), (H, LANES - 2 * H)))
    dtb_row = jnp.pad(dt_bias, ((0, 0), (H, LANES - 2 * H)))
    bbc = jnp.broadcast_to(b_spatial[:, :, None], (G, CHUNK_B, CHUNK_B))

    (xn, xn_t), (g_in,) = _rms_in(x, norm_w, ride=_gather_ride([win_b], [True]))
    w_own = lay.to_own_order(_put_own(g_in, win_b))
    proj_m, (g_out, g_conv) = _mm_nn(xn, w_own, F32, "in_proj", cols=(0, lay.n_main),
                                     ride=_gather_ride([wout_b, conv_b], [True, False]))
    wout = _put_own(g_out, wout_b).reshape(nb * Rb, D)
    conv_w = _put_own(g_conv, conv_b).transpose(1, 0, 2).reshape(CONV_WIDTH, nb * conv_b.shape[1])
    proj_ba = _mm_nn(xn, w_own, F32, "in_proj_ba", cols=(lay.n_main, LANES))
    q, k, v, gb, bb = _gdn_pre(proj_m, proj_ba, conv_w, alog_row, dtb_row, H)
    u, w, qg, kd, attn, eg, pinv = _gdn_prep(q, k, v, gb, bb)
    og, sall = _gdn_chain(qg, kd, u, w, attn, eg)
    oa, oa_t = _gdn_post(og, proj_m, head_norm_w)
    ob, ob_t = _sgu_fwd(proj_m, sgu_ln_w, sgu_ln_b, w_spatial, bbc, A)
    dh, dhb, loss_row, d_fnw = _out_proj_loss(oa, ob, wout, x, tgt, final_norm_w.reshape(1, D))

    d_o = _mm_nn(dhb, wout.T, F32, "out_proj_dx")
    dproj = lax.empty((T, lay.k), BF16)
    dproj, d_lw, d_lb, d_ws, d_bs = _sgu_bwd(proj_m, sgu_ln_w, sgu_ln_b, w_spatial, bbc, d_o, A, dproj)
    dog, dproj, d_hw = _gdn_post_bwd(og, proj_m, head_norm_w, d_o, dproj)
    dqg, dkd, du, dw, dat, deg = _gdn_chain_bwd(qg, kd, u, w, attn, eg, sall, dog)
    dq, dk, dv, dgb, dbb = _gdn_prep_bwd(q, k, v, gb, bb, pinv, du, dw, dqg, dkd, dat, deg)
    dc, dproj, d_al, d_dt = _gdn_pre_bwd(proj_m, proj_ba, conv_w, alog_row, dtb_row, dq, dk, dv, dgb, dbb, H,
                                         dproj)
    dproj, d_conv = _conv_bwd(proj_m, dc, conv_w, H, dproj)

    table = jnp.array([b for row in lay.table for b in row], jnp.int32)
    d_win = _mm_windows(xn_t, dproj, table, nb, "in_proj_dw")
    d_wout, (land_w,) = _mm_nn_pair(oa_t, ob_t, dhb, "out_proj_dw", ride=_pair_ride(d_win))
    d_wout = d_wout.reshape(nb, Rb, D)
    pair_w, (land_o,) = _pair_sum(d_win, land_w, c_arr, "pair_sum_w_in", ride=_pair_ride(d_wout))
    pair_o = _pair_sum(d_wout, land_o, c_arr, "pair_sum_w_out")
    dxn, (all_w,) = _mm_nt_rhs_outer(dproj, w_own, F32, "in_proj_dx", ride=_chip_ride([pair_w]))
    (grad_x, d_nw), (all_o,) = _rms_in_bwd(x, norm_w, dxn, dh, ride=_chip_ride([pair_o]))
    all_w, all_o = _put_own_slot(all_w, pair_w), _put_own_slot(all_o, pair_o)
    small = dict(norm_w=d_nw, conv_w=d_conv[:CONV_WIDTH], a_log=d_al[:, H:2 * H], dt_bias=d_dt[:, H:2 * H],
                 head_norm_w=d_hw, sgu_ln_w=d_lw, sgu_ln_b=d_lb, w_spatial=d_ws, b_spatial=d_bs[:, :, 0],
                 final_norm_w=d_fnw)
    return loss_row, grad_x, small, all_w, all_o


SMALL = ("norm_w", "conv_w", "a_log", "dt_bias", "head_norm_w", "sgu_ln_w", "sgu_ln_b", "w_spatial",
         "b_spatial", "final_norm_w")


def _pack(parts):
    rows = []
    for p in parts:
        f = p.reshape(-1)
        f = jnp.pad(f, (0, (-f.shape[0]) % (8 * LANES)))
        rows.append(f.reshape(-1, LANES))
    return jnp.concatenate(rows, axis=0)


def _unpack(buf, shapes):
    out, r = [], 0
    for s in shapes:
        n = 1
        for d in s:
            n *= d
        nr = -(-n // (8 * LANES)) * 8
        out.append(buf[r:r + nr].reshape(-1)[:n].reshape(s))
        r += nr
    return out


def kernel(x, norm_w, w_in, conv_w, a_log, dt_bias, head_norm_w, sgu_ln_w, sgu_ln_b, w_spatial, b_spatial, w_out, final_norm_w, loss_target, m_norm_w, m_w_in, m_conv_w, m_a_log, m_dt_bias, m_head_norm_w, m_sgu_ln_w, m_sgu_ln_b, m_w_spatial, m_b_spatial, m_w_out, m_final_norm_w, v_norm_w, v_w_in, v_conv_w, v_a_log, v_dt_bias, v_head_norm_w, v_sgu_ln_w, v_sgu_ln_b, v_w_spatial, v_b_spatial, v_w_out, v_final_norm_w):
    T, D = x.shape[1], x.shape[2]
    weights = dict(norm_w=norm_w, w_in=w_in, conv_w=conv_w, a_log=a_log, dt_bias=dt_bias, head_norm_w=head_norm_w,
                   sgu_ln_w=sgu_ln_w, sgu_ln_b=sgu_ln_b, w_spatial=w_spatial, b_spatial=b_spatial, w_out=w_out,
                   final_norm_w=final_norm_w)
    mom_m = dict(norm_w=m_norm_w, w_in=m_w_in, conv_w=m_conv_w, a_log=m_a_log, dt_bias=m_dt_bias,
                 head_norm_w=m_head_norm_w, sgu_ln_w=m_sgu_ln_w, sgu_ln_b=m_sgu_ln_b, w_spatial=m_w_spatial,
                 b_spatial=m_b_spatial, w_out=m_w_out, final_norm_w=m_final_norm_w)
    mom_v = dict(norm_w=v_norm_w, w_in=v_w_in, conv_w=v_conv_w, a_log=v_a_log, dt_bias=v_dt_bias,
                 head_norm_w=v_head_norm_w, sgu_ln_w=v_sgu_ln_w, sgu_ln_b=v_sgu_ln_b, w_spatial=v_w_spatial,
                 b_spatial=v_b_spatial, w_out=v_w_out, final_norm_w=v_final_norm_w)
    me = _chip_index(lax.axis_index("x"), lax.axis_index("y"))
    c_arr = lax.axis_index("c").astype(jnp.int32).reshape(1)
    Din, Cb = w_in.shape[1], w_in.shape[2]
    Rb = w_out.shape[1]
    cconv = conv_w.shape[2]

    loss_row, grad_x, g, qw, qo = _device_step(
        x[0], loss_target[0], norm_w, w_in[0].astype(BF16), w_out[0].astype(BF16), conv_w[0], a_log, dt_bias,
        head_norm_w, sgu_ln_w, sgu_ln_b, w_spatial[0], b_spatial[0], final_norm_w, c_arr)

    small_shapes = [tuple(g[n].shape) for n in SMALL]
    small = _allreduce_small(_pack([g[n] for n in SMALL]))
    gsum_in, gsum_out = _sibling_fill(_chip_sum(qw, c_arr, "chip_sum_w_in"), _chip_sum(qo, c_arr, "chip_sum_w_out"))
    gsum_in = _Layout(a_log.shape[1], w_spatial.shape[1], N_CHIPS, Cb).from_window(gsum_in, me, Cb)
    gsmall = dict(zip(SMALL, _unpack(small, small_shapes)))
    gsmall["conv_w"] = lax.dynamic_slice_in_dim(gsmall["conv_w"], me * cconv, cconv, axis=1)

    grads, deltas, new_m, new_v = {}, {}, {}, {}
    d, m2, v2 = _adamw(w_out[0], gsum_out, m_w_out[0], v_w_out[0], "adamw_w_out")
    grads["w_out"], deltas["w_out"], new_m["w_out"], new_v["w_out"] = gsum_out[None], d[None], m2[None], v2[None]
    flat = lambda a: a.transpose(2, 0, 1).reshape(-1, LANES)
    unflat = lambda f: f.reshape(Cb, 1, Din).transpose(1, 2, 0)
    g_flat = gsum_in.T.reshape(-1, LANES)
    d, m2, v2 = _adamw(flat(w_in), g_flat, flat(m_w_in), flat(v_w_in), "adamw_w_in")
    grads["w_in"], deltas["w_in"], new_m["w_in"], new_v["w_in"] = unflat(g_flat), unflat(d), unflat(m2), unflat(v2)
    shapes = [tuple(weights[n].shape) for n in SMALL]
    ds, ms, vs = _adamw(_pack([weights[n] for n in SMALL]), _pack([gsmall[n] for n in SMALL]),
                        _pack([mom_m[n] for n in SMALL]), _pack([mom_v[n] for n in SMALL]), "adamw_small")
    for n, gq, d, m2, v2 in zip(SMALL, [gsmall[n] for n in SMALL], _unpack(ds, shapes), _unpack(ms, shapes),
                                _unpack(vs, shapes)):
        grads[n], deltas[n], new_m[n], new_v[n] = gq.reshape(weights[n].shape), d, m2, v2

    loss = lax.psum(loss_row[0, 0], ("x", "y", "c"))
    order = ("norm_w", "w_in", "conv_w", "a_log", "dt_bias", "head_norm_w", "sgu_ln_w", "sgu_ln_b", "w_spatial",
             "b_spatial", "w_out", "final_norm_w")
    return (loss, grad_x[None], *[grads[n] for n in order], *[deltas[n] for n in order],
            *[new_m[n] for n in order], *[new_v[n] for n in order])
```

```python
import functools

import jax
import jax.numpy as jnp
from jax import lax
from jax.experimental import pallas as pl
from jax.experimental.pallas import tpu as pltpu

F32 = jnp.float32
BF16 = jnp.bfloat16
EPS = 1e-6
HEAD_DIM = 128
CHUNK_B = 128
CONV_WIDTH = 4
LANES = 128
HALO = 8
N_CHIPS = 4
ADAM_LR = 0.001
ADAM_B1 = 0.9
ADAM_B2 = 0.999
ADAM_EPS = 1e-08
ADAM_WD = 0.01
ADAM_STEP = 10
VMEM_LIMIT = 56 * 1024 * 1024
MESH_ID = pl.DeviceIdType.MESH
HI = lax.Precision.HIGHEST


def _cparams(sem=None, **kw):
    return pltpu.CompilerParams(dimension_semantics=sem, vmem_limit_bytes=VMEM_LIMIT, **kw)


def _matmul(a, b, ca, cb, precision):
    nb = a.ndim - 2
    batch = tuple(range(nb))
    return lax.dot_general(a, b, (((ca + nb,), (cb + nb,)), (batch, batch)), precision=precision,
                           preferred_element_type=F32)


def _dot(a, b, hi=False, precision=None):
    return _matmul(a, b, 1, 0, HI if hi else precision)


def _dot_nt(a, b, hi=False, precision=None):
    return _matmul(a, b, 1, 1, HI if hi else precision)


def _dot_tn(a, b, hi=False, precision=None):
    return _matmul(a, b, 0, 0, HI if hi else precision)


def _iota(shape, dim):
    return lax.broadcasted_iota(jnp.int32, shape, dim)


def _sigmoid(x):
    return 0.5 * (jnp.tanh(0.5 * x) + 1.0)


def _silu(x):
    return x * _sigmoid(x)


def _softplus(x):
    z = jnp.exp(-jnp.abs(x))
    small = z * (1.0 - z * (0.5 - z * (1.0 / 3.0)))
    return jnp.maximum(x, 0.0) + jnp.where(z < 1e-3, small, jnp.log(1.0 + z))


def _pick(n, pref):
    for t in pref:
        if n % t == 0:
            return t
    return n


class _Ride:
    def __init__(self, operands, out_shape, n_sems, start, finish):
        self.operands, self.out_shape, self.n_sems = list(operands), list(out_shape), n_sems
        self.start, self.finish = start, finish


def _pallas(body, operands, *, name, grid, in_specs, out_specs, out_shape, semantics, scratch_shapes=(),
            prefetch=0, ride=None):
    single = not isinstance(out_shape, (list, tuple))
    outs = [out_shape] if single else list(out_shape)
    ospecs = [out_specs] if single else list(out_specs)
    in_specs, scratch = list(in_specs), list(scratch_shapes)
    n_in, n_out, n_sc = len(operands) - prefetch, len(outs), len(scratch)
    kernel = body
    params = _cparams(semantics)
    if ride is not None:
        n_xin, n_xout = len(ride.operands), len(ride.out_shape)

        def kernel(*refs):
            pre, refs = refs[:prefetch], refs[prefetch:]
            ins, refs = refs[:n_in], refs[n_in:]
            xins, refs = refs[:n_xin], refs[n_xin:]
            mains, refs = refs[:n_out], refs[n_out:]
            xouts, refs = refs[:n_xout], refs[n_xout:]
            sc, (send, recv) = refs[:n_sc], refs[n_sc:]
            ids = [pl.program_id(a) for a in range(len(grid))]
            first = functools.reduce(jnp.logical_and, [i == 0 for i in ids])
            last = functools.reduce(jnp.logical_and, [i == g - 1 for i, g in zip(ids, grid)])

            @pl.when(first)
            def _():
                ride.start(xins, xouts, send, recv)

            body(*pre, *ins, *mains, *sc)

            @pl.when(last)
            def _():
                ride.finish(xins, xouts, send, recv)

        operands = list(operands) + ride.operands
        in_specs += [ANY] * n_xin
        ospecs += [ANY] * n_xout
        outs += ride.out_shape
        scratch += [pltpu.SemaphoreType.DMA((ride.n_sems,)), pltpu.SemaphoreType.DMA((ride.n_sems,))]
        params = _cparams(("arbitrary",) * len(grid), has_side_effects=True)
    if prefetch:
        spec = dict(grid_spec=pltpu.PrefetchScalarGridSpec(
            num_scalar_prefetch=prefetch, grid=grid, in_specs=in_specs, out_specs=ospecs, scratch_shapes=scratch))
    else:
        spec = dict(grid=grid, in_specs=in_specs, out_specs=ospecs, scratch_shapes=scratch)
    res = pl.pallas_call(kernel, name=name, out_shape=outs, compiler_params=params, **spec)(*operands)
    main = res[0] if single else list(res[:n_out])
    return main if ride is None else (main, list(res[n_out:]))


def _mm_nn(a, b, out_dtype, name, tm=1024, tn=512, tk=None, cols=None, ride=None):
    M, K = a.shape
    c0, N = (0, b.shape[1]) if cols is None else cols
    tm = _pick(M, (tm, 512, 256, 128))
    tn = _pick(N, (tn, 512, 384, 256, 128))
    tk = K if tk is None else _pick(K, (tk,))
    nk = K // tk
    j0 = c0 // tn
    assert c0 % tn == 0

    def body(a_ref, b_ref, o_ref, *scratch):
        part = _dot(a_ref[...], b_ref[...])
        if nk == 1:
            o_ref[...] = part.astype(out_dtype)
        else:
            acc_ref, = scratch
            k = pl.program_id(2)

            @pl.when(k == 0)
            def _():
                acc_ref[...] = part

            @pl.when(k > 0)
            def _():
                acc_ref[...] += part

            @pl.when(k == nk - 1)
            def _():
                o_ref[...] = acc_ref[...].astype(out_dtype)

    return _pallas(
        body, (a, b), name=name, grid=(M // tm, N // tn, nk),
        in_specs=[pl.BlockSpec((tm, tk), lambda i, j, k: (i, k)),
                  pl.BlockSpec((tk, tn), lambda i, j, k: (k, j + j0))],
        out_specs=pl.BlockSpec((tm, tn), lambda i, j, k: (i, j)),
        out_shape=jax.ShapeDtypeStruct((M, N), out_dtype),
        scratch_shapes=[] if nk == 1 else [pltpu.VMEM((tm, tn), F32)],
        semantics=("parallel", "parallel", "arbitrary"), ride=ride)


def _mm_nt_rhs_outer(a, b, out_dtype, name, tm=256, tn=1024, ride=None):
    M, K = a.shape
    N, _ = b.shape
    tm = _pick(M, (tm, 128))
    tn = _pick(N, (tn, 512, 256, 128))

    def body(a_ref, b_ref, o_ref):
        o_ref[...] = _dot_nt(a_ref[...], b_ref[...]).astype(out_dtype)

    return _pallas(
        body, (a, b), name=name, grid=(N // tn, M // tm),
        in_specs=[pl.BlockSpec((tm, K), lambda j, i: (i, 0)),
                  pl.BlockSpec((tn, K), lambda j, i: (j, 0))],
        out_specs=pl.BlockSpec((tm, tn), lambda j, i: (i, j)),
        out_shape=jax.ShapeDtypeStruct((M, N), out_dtype),
        semantics=("parallel", "parallel"), ride=ride)


WIN_BLOCK = 256


def _mm_windows(a, b, table, nb, name, tm=2048):
    M, K = a.shape
    wb = table.shape[0] // nb
    tm = _pick(M, (tm, 1024, 512, 256, 128))

    def body(tab_ref, a_ref, b_ref, o_ref):
        o_ref[0] = _dot(a_ref[...], b_ref[...])

    return pl.pallas_call(
        body, name=name,
        grid_spec=pltpu.PrefetchScalarGridSpec(
            num_scalar_prefetch=1, grid=(nb, M // tm, wb),
            in_specs=[pl.BlockSpec((tm, K), lambda n, i, t, tab: (i, 0)),
                      pl.BlockSpec((K, WIN_BLOCK), lambda n, i, t, tab: (0, tab[n * wb + t]))],
            out_specs=pl.BlockSpec((1, tm, WIN_BLOCK), lambda n, i, t, tab: (n, i, t))),
        out_shape=jax.ShapeDtypeStruct((nb, M, wb * WIN_BLOCK), F32),
        compiler_params=_cparams(("parallel", "parallel", "arbitrary")),
    )(table, a, b)


def _mm_nn_pair(a0, a1, b, name, tm=512, tn=1024, ride=None):
    M, K = a0.shape
    _, N = b.shape
    tm = _pick(M, (tm, 256, 128))
    tn = _pick(N, (tn, 512, 256, 128))
    ni = M // tm

    def body(a0_ref, a1_ref, b_ref, o_ref):
        p = pl.program_id(0)

        @pl.when(p == 0)
        def _():
            o_ref[...] = _dot(a0_ref[...], b_ref[...])

        @pl.when(p == 1)
        def _():
            o_ref[...] = _dot(a1_ref[...], b_ref[...])

    return _pallas(
        body, (a0, a1, b), name=name, grid=(2, ni, N // tn),
        in_specs=[pl.BlockSpec((tm, K), lambda p, i, j: (i * (1 - p), 0)),
                  pl.BlockSpec((tm, K), lambda p, i, j: (i * p, 0)),
                  pl.BlockSpec((K, tn), lambda p, i, j: (0, j))],
        out_specs=pl.BlockSpec((tm, tn), lambda p, i, j: (p * ni + i, j)),
        out_shape=jax.ShapeDtypeStruct((2 * M, N), F32),
        semantics=("parallel", "parallel", "parallel"), ride=ride)


def _rms_fn(x, w):
    r = lax.rsqrt(jnp.mean(x * x, axis=-1, keepdims=True) + EPS)
    return x * r * w


def _rms_in(x, w, ride=None):
    T, D = x.shape
    tm = _pick(T, (512, 256, 128))

    def body(x_ref, w_ref, o_ref, ot_ref):
        xn = _rms_fn(x_ref[...], w_ref[...])
        o_ref[...] = xn.astype(BF16)
        ot_ref[...] = xn.T.astype(BF16)

    return _pallas(
        body, (x, w), name="rms_in", grid=(T // tm,),
        in_specs=[pl.BlockSpec((tm, D), lambda i: (i, 0)), pl.BlockSpec((1, D), lambda i: (0, 0))],
        out_specs=[pl.BlockSpec((tm, D), lambda i: (i, 0)), pl.BlockSpec((D, tm), lambda i: (0, i))],
        out_shape=[jax.ShapeDtypeStruct((T, D), BF16), jax.ShapeDtypeStruct((D, T), BF16)],
        semantics=("parallel",), ride=ride)


def _rms_in_bwd(x, w, dxn, dh, ride=None):
    T, D = x.shape
    tm = _pick(T, (256, 128))

    def body(x_ref, w_ref, dxn_ref, dh_ref, gx_ref, dw_ref):
        _, vjp = jax.vjp(_rms_fn, x_ref[...], w_ref[...])
        dx, dw = vjp(dxn_ref[...])
        gx_ref[...] = dh_ref[...] + dx

        @pl.when(pl.program_id(0) == 0)
        def _():
            dw_ref[...] = dw

        @pl.when(pl.program_id(0) > 0)
        def _():
            dw_ref[...] += dw

    tile = pl.BlockSpec((tm, D), lambda i: (i, 0))
    row = pl.BlockSpec((1, D), lambda i: (0, 0))
    return _pallas(
        body, (x, w, dxn, dh), name="rms_in_bwd", grid=(T // tm,),
        in_specs=[tile, row, tile, tile], out_specs=[tile, row],
        out_shape=[jax.ShapeDtypeStruct((T, D), F32), jax.ShapeDtypeStruct((1, D), F32)],
        semantics=("arbitrary",), ride=ride)


def _conv_fwd(cat_ref, halo, x, w):
    tm = x.shape[0]
    cat_ref[0:HALO, :] = halo
    cat_ref[HALO:HALO + tm, :] = x
    c = x * w[CONV_WIDTH - 1:CONV_WIDTH, :]
    for k in range(CONV_WIDTH - 1):
        s = CONV_WIDTH - 1 - k
        c = c + cat_ref[pl.ds(HALO - s, tm), :] * w[k:k + 1, :]
    return c


def _lane_to_all(x, lane):
    @jax.custom_vjp
    def f(x):
        return jnp.broadcast_to(x[:, lane:lane + 1], x.shape)

    def f_fwd(x):
        return f(x), None

    def f_bwd(_, g):
        return (jnp.where(_iota(g.shape, 1) == lane, jnp.sum(g, axis=-1, keepdims=True), 0.0),)

    f.defvjp(f_fwd, f_bwd)
    return f(x)


def _gdn_pointwise(c, ba, alog, dtb, H):
    A = H * HEAD_DIM
    s = _silu(c)
    beta = _sigmoid(ba)
    g = -jnp.exp(alog) * _softplus(ba + dtb)
    qs, ks, vs, gbs, bbs = [], [], [], [], []
    for h in range(H):
        lo = h * HEAD_DIM
        q = s[:, lo:lo + HEAD_DIM]
        k = s[:, A + lo:A + lo + HEAD_DIM]
        qs.append(q * lax.rsqrt(jnp.sum(q * q, axis=-1, keepdims=True) + EPS))
        ks.append(k * lax.rsqrt(jnp.sum(k * k, axis=-1, keepdims=True) + EPS))
        vs.append(s[:, 2 * A + lo:2 * A + lo + HEAD_DIM])
        bbs.append(_lane_to_all(beta, h))
        gbs.append(_lane_to_all(g, H + h))
    st = lambda xs: jnp.stack(xs, axis=0)
    return st(qs), st(ks), st(vs), st(gbs), st(bbs)


def _halo_prev(tm):
    return lambda i: (jnp.maximum(i * (tm // HALO) - 1, 0), 0)


def _gdn_pre(proj_m, proj_ba, conv_w, alog_row, dtb_row, H):
    T = proj_m.shape[0]
    A = H * HEAD_DIM
    tm = _pick(T, (256, 128))
    hs = pl.BlockSpec((H, tm, HEAD_DIM), lambda i: (0, i, 0))
    hshape = jax.ShapeDtypeStruct((H, T, HEAD_DIM), F32)

    def body(x_ref, halo_ref, ba_ref, w_ref, al_ref, dt_ref, q_ref, k_ref, v_ref, gb_ref, bb_ref, cat_ref):
        halo = jnp.where(pl.program_id(0) == 0, 0.0, halo_ref[...])
        c = _conv_fwd(cat_ref, halo, x_ref[...], w_ref[...])
        q, k, v, gb, bb = _gdn_pointwise(c, ba_ref[...], al_ref[...], dt_ref[...], H)
        q_ref[...] = q
        k_ref[...] = k
        v_ref[...] = v
        gb_ref[...] = gb
        bb_ref[...] = bb

    return pl.pallas_call(
        body, name="gdn_pre", grid=(T // tm,),
        in_specs=[pl.BlockSpec((tm, 3 * A), lambda i: (i, 0)),
                  pl.BlockSpec((HALO, 3 * A), _halo_prev(tm)),
                  pl.BlockSpec((tm, LANES), lambda i: (i, 0)),
                  pl.BlockSpec((CONV_WIDTH, 3 * A), lambda i: (0, 0)),
                  pl.BlockSpec((1, LANES), lambda i: (0, 0)),
                  pl.BlockSpec((1, LANES), lambda i: (0, 0))],
        out_specs=[hs] * 5, out_shape=[hshape] * 5,
        scratch_shapes=[pltpu.VMEM((HALO + tm, 3 * A), F32)],
        compiler_params=_cparams(("parallel",)),
    )(proj_m, proj_m, proj_ba, conv_w, alog_row, dtb_row)


def _gdn_pre_bwd(proj_m, proj_ba, conv_w, alog_row, dtb_row, dq, dk, dv, dgb, dbb, H, dproj):
    T, n_main = proj_m.shape
    A = H * HEAD_DIM
    tm = _pick(T, (256, 128))
    hs = pl.BlockSpec((H, tm, HEAD_DIM), lambda i: (0, i, 0))
    row = pl.BlockSpec((1, LANES), lambda i: (0, 0))

    def body(x_ref, halo_ref, ba_ref, w_ref, al_ref, dt_ref, dq_ref, dk_ref, dv_ref, dgb_ref, dbb_ref, _,
             dc_ref, dba_ref, dal_ref, ddt_ref, cat_ref):
        halo = jnp.where(pl.program_id(0) == 0, 0.0, halo_ref[...])
        c = _conv_fwd(cat_ref, halo, x_ref[...], w_ref[...])
        _, vjp = jax.vjp(functools.partial(_gdn_pointwise, H=H), c, ba_ref[...], al_ref[...], dt_ref[...])
        dc, dba, dal, ddt = vjp((dq_ref[...], dk_ref[...], dv_ref[...], dgb_ref[...], dbb_ref[...]))
        dc_ref[...] = dc
        dba_ref[:, :LANES] = dba.astype(BF16)
        dba_ref[:, LANES:] = jnp.zeros((tm, WIN_BLOCK - LANES), BF16)

        @pl.when(pl.program_id(0) == 0)
        def _():
            dal_ref[...] = dal
            ddt_ref[...] = ddt

        @pl.when(pl.program_id(0) > 0)
        def _():
            dal_ref[...] += dal
            ddt_ref[...] += ddt

    return pl.pallas_call(
        body, name="gdn_pre_bwd", grid=(T // tm,),
        in_specs=[pl.BlockSpec((tm, 3 * A), lambda i: (i, 0)),
                  pl.BlockSpec((HALO, 3 * A), _halo_prev(tm)),
                  pl.BlockSpec((tm, LANES), lambda i: (i, 0)),
                  pl.BlockSpec((CONV_WIDTH, 3 * A), lambda i: (0, 0)),
                  row, row, hs, hs, hs, hs, hs, ANY],
        out_specs=[pl.BlockSpec((tm, 3 * A), lambda i: (i, 0)),
                   pl.BlockSpec((tm, WIN_BLOCK), lambda i: (i, n_main // WIN_BLOCK)), row, row],
        out_shape=[jax.ShapeDtypeStruct((T, 3 * A), F32), jax.ShapeDtypeStruct(dproj.shape, dproj.dtype),
                   jax.ShapeDtypeStruct((1, LANES), F32), jax.ShapeDtypeStruct((1, LANES), F32)],
        input_output_aliases={11: 1},
        scratch_shapes=[pltpu.VMEM((HALO + tm, 3 * A), F32)],
        compiler_params=_cparams(("arbitrary",)),
    )(proj_m, proj_m, proj_ba, conv_w, alog_row, dtb_row, dq, dk, dv, dgb, dbb, dproj)


def _conv_bwd(proj_m, dc, conv_w, H, dproj):
    T = proj_m.shape[0]
    A = H * HEAD_DIM
    tm = _pick(T, (256, 128))
    nt = T // tm

    def body(x_ref, halo_ref, dc_ref, nxt_ref, w_ref, _, dx_ref, dw_ref):
        i = pl.program_id(0)
        halo = jnp.where(i == 0, 0.0, halo_ref[...])
        xcat = jnp.concatenate([halo, x_ref[...]], axis=0)
        nxt = jnp.where(i == nt - 1, 0.0, nxt_ref[...])
        dc = dc_ref[...]
        dcat = jnp.concatenate([dc, nxt], axis=0)
        w = w_ref[...]
        dx = None
        rows = []
        for k in range(CONV_WIDTH):
            s = CONV_WIDTH - 1 - k
            ds = dcat if s == 0 else pltpu.roll(dcat, tm + HALO - s, 0)
            term = ds[:tm, :] * w[k:k + 1, :]
            dx = term if dx is None else dx + term
            xs = xcat if s == 0 else pltpu.roll(xcat, s, 0)
            rows.append(jnp.sum(dc * xs[HALO:, :], axis=0, keepdims=True))
        dx_ref[...] = dx.astype(BF16)
        dw = jnp.concatenate(rows + [jnp.zeros((HALO - CONV_WIDTH, 3 * A), F32)], axis=0)

        @pl.when(i == 0)
        def _():
            dw_ref[...] = dw

        @pl.when(i > 0)
        def _():
            dw_ref[...] += dw

    return pl.pallas_call(
        body, name="conv_bwd", grid=(nt,),
        in_specs=[pl.BlockSpec((tm, 3 * A), lambda i: (i, 0)),
                  pl.BlockSpec((HALO, 3 * A), _halo_prev(tm)),
                  pl.BlockSpec((tm, 3 * A), lambda i: (i, 0)),
                  pl.BlockSpec((HALO, 3 * A), lambda i: (jnp.minimum((i + 1) * (tm // HALO), T // HALO - 1), 0)),
                  pl.BlockSpec((CONV_WIDTH, 3 * A), lambda i: (0, 0)), ANY],
        out_specs=[pl.BlockSpec((tm, 3 * A), lambda i: (i, 0)),
                   pl.BlockSpec((HALO, 3 * A), lambda i: (0, 0))],
        out_shape=[jax.ShapeDtypeStruct(dproj.shape, dproj.dtype), jax.ShapeDtypeStruct((HALO, 3 * A), F32)],
        input_output_aliases={5: 0},
        compiler_params=_cparams(("arbitrary",)),
    )(proj_m, proj_m, dc, dc, conv_w, dproj)


CHUNK = 128


def _b(x):
    return x.astype(BF16)


@jax.custom_vjp
def _bdot(a, b):
    return _dot(_b(a), _b(b))


def _bdot_f(a, b):
    return _bdot(a, b), (a, b)


def _bdot_b(res, g):
    a, b = res
    return _dot_nt(_b(g), _b(b)), _dot_tn(_b(a), _b(g))


_bdot.defvjp(_bdot_f, _bdot_b)


@jax.custom_vjp
def _bdot_nt(a, b):
    return _dot_nt(_b(a), _b(b))


def _bdot_nt_f(a, b):
    return _bdot_nt(a, b), (a, b)


def _bdot_nt_b(res, g):
    a, b = res
    return _dot(_b(g), _b(b)), _dot_tn(_b(g), _b(a))


_bdot_nt.defvjp(_bdot_nt_f, _bdot_nt_b)


@jax.custom_vjp
def _bdot_tn(a, b):
    return _dot_tn(_b(a), _b(b))


def _bdot_tn_f(a, b):
    return _bdot_tn(a, b), (a, b)


def _bdot_tn_b(res, g):
    a, b = res
    return _dot_nt(_b(b), _b(g)), _dot(_b(a), _b(g))


_bdot_tn.defvjp(_bdot_tn_f, _bdot_tn_b)


def _mask_matmul(m, x):
    hi = _b(x)
    r = x - hi.astype(F32)
    mid = _b(r)
    lo = _b(r - mid.astype(F32))
    return (_dot(m, lo) + _dot(m, mid)) + _dot(m, hi)


@jax.custom_vjp
def _mask_dot(m, mt, x):
    return _mask_matmul(m, x)


def _mask_dot_f(m, mt, x):
    return _mask_matmul(m, x), (m, mt)


def _mask_dot_b(res, g):
    m, mt = res
    return jnp.zeros_like(m), jnp.zeros_like(mt), _mask_matmul(mt, g)


_mask_dot.defvjp(_mask_dot_f, _mask_dot_b)

HIGH = lax.Precision.HIGH


def _unit_lower_inverse(L):
    n = L.shape[-1]
    X = -L
    Q = X
    for _ in range(n.bit_length() - 2):
        X = _dot(_b(X), _b(X))
        Q = Q + X + _dot(_b(Q), _b(X))
    return (_iota((n, n), 0) == _iota((n, n), 1)).astype(F32) + Q


@jax.custom_vjp
def _known_inverse(L, P):
    return P


def _known_inverse_f(L, P):
    return P, P


def _known_inverse_b(P, g):
    n = P.shape[-1]
    Q = _b(P - (_iota((n, n), 0) == _iota((n, n), 1)).astype(F32))
    t = g + _dot_tn(Q, _b(g))
    return -(t + _dot_nt(_b(t), Q)), jnp.zeros_like(P)


_known_inverse.defvjp(_known_inverse_f, _known_inverse_b)


def _gdn_prep_fn(q, k, v, gb, bb, P_known=None):
    n = CHUNK
    row, col = _iota((n, n), 0), _iota((n, n), 1)
    incl, strict = row >= col, row > col
    bc = lambda m: jnp.broadcast_to(_b(m.astype(F32)), q.shape[:1] + (n, n))
    tril, triu, ones = bc(incl), bc(row <= col), bc(row >= 0)
    gc = _mask_dot(tril, triu, gb)
    gl = _mask_dot(ones, ones, gb)
    decay = jnp.where(incl, jnp.exp(jnp.where(incl, gc - jnp.swapaxes(gc, 1, 2), 0.0)), 0.0)
    kb = k * bb
    vb = v * bb
    qs = q * (HEAD_DIM ** -0.5)
    L = jnp.where(strict, _bdot_nt(kb, k) * decay, 0.0)
    P = _unit_lower_inverse(L) if P_known is None else _known_inverse(L, P_known)
    egc = jnp.exp(gc)
    u = _bdot(P, vb)
    w = _bdot(P, kb * egc)
    attn = jnp.where(incl, _bdot_nt(qs, k) * decay, 0.0)
    qg = qs * egc
    kdec = k * jnp.exp(gl - gc)
    eg = jnp.exp(gl)
    if P_known is None:
        return u, w, qg, kdec, attn, eg, P
    return u, w, qg, kdec, attn, eg


def _gdn_chain_fn(S, qg, kdec, u, w, attn, eg):
    v_new = u - _bdot(w, S)
    o = _bdot(qg, S) + _bdot(attn, v_new)
    return o, S * eg + _bdot_tn(kdec, v_new)


def _gdn_prep(q, k, v, gb, bb):
    H, T, _ = q.shape
    pb = _pick(T // CHUNK, (8, 4, 2, 1))
    hs = pl.BlockSpec((1, CHUNK * pb, HEAD_DIM), lambda h, n: (h, n, 0))
    hshape = jax.ShapeDtypeStruct((H, T, HEAD_DIM), F32)

    def body(q_ref, k_ref, v_ref, gb_ref, bb_ref, *out_refs):
        chunks = lambda ref: ref[0].reshape(pb, CHUNK, HEAD_DIM)
        outs = _gdn_prep_fn(chunks(q_ref), chunks(k_ref), chunks(v_ref), chunks(gb_ref), chunks(bb_ref))
        for ref, val in zip(out_refs, outs):
            ref[0] = val.reshape(pb * CHUNK, HEAD_DIM).astype(ref.dtype)

    kept = [F32, BF16, BF16, BF16, BF16, F32, BF16]
    return pl.pallas_call(
        body, name="gdn_prep", grid=(H, T // (CHUNK * pb)),
        in_specs=[hs] * 5, out_specs=[hs] * 7,
        out_shape=[jax.ShapeDtypeStruct((H, T, HEAD_DIM), dt) for dt in kept],
        compiler_params=_cparams(("parallel", "parallel")),
    )(q, k, v, gb, bb)


def _gdn_prep_bwd(q, k, v, gb, bb, pinv, du, dw, dqg, dkd, dat, deg):
    H, T, _ = q.shape
    pb = _pick(T // CHUNK, (8, 4, 2, 1))
    hs = pl.BlockSpec((1, CHUNK * pb, HEAD_DIM), lambda h, n: (h, n, 0))
    hshape = jax.ShapeDtypeStruct((H, T, HEAD_DIM), F32)

    def body(*refs):
        in_refs, p_ref, ct_refs, out_refs = refs[:5], refs[5], refs[6:12], refs[12:]
        chunks = lambda ref: ref[0].reshape(pb, CHUNK, HEAD_DIM)
        P = chunks(p_ref).astype(F32)
        _, vjp = jax.vjp(lambda *a: _gdn_prep_fn(*a, P_known=P), *[chunks(r) for r in in_refs])
        grads = vjp(tuple(chunks(r).astype(F32) for r in ct_refs))
        for ref, val in zip(out_refs, grads):
            ref[0] = val.reshape(pb * CHUNK, HEAD_DIM)

    return pl.pallas_call(
        body, name="gdn_prep_bwd", grid=(H, T // (CHUNK * pb)),
        in_specs=[hs] * 12, out_specs=[hs] * 5, out_shape=[hshape] * 5,
        compiler_params=_cparams(("parallel", "parallel")),
    )(q, k, v, gb, bb, pinv, du, dw, dqg, dkd, dat, deg)


def _gdn_chain(qg, kd, u, w, attn, eg):
    H, T, _ = qg.shape
    N = T // CHUNK
    hs = pl.BlockSpec((H, CHUNK, HEAD_DIM), lambda n: (0, n, 0))
    ss = pl.BlockSpec((1, H, HEAD_DIM, HEAD_DIM), lambda n: (n, 0, 0, 0))

    def body(qg_ref, kd_ref, u_ref, w_ref, at_ref, eg_ref, o_ref, sall_ref, s_ref):
        @pl.when(pl.program_id(0) == 0)
        def _():
            s_ref[...] = jnp.zeros_like(s_ref)

        S = s_ref[...]
        sall_ref[0] = S
        o, S2 = _gdn_chain_fn(S, qg_ref[...], kd_ref[...], u_ref[...], w_ref[...], at_ref[...], eg_ref[...])
        o_ref[...] = o
        s_ref[...] = S2

    return pl.pallas_call(
        body, name="gdn_chain", grid=(N,),
        in_specs=[hs] * 6, out_specs=[hs, ss],
        out_shape=[jax.ShapeDtypeStruct((H, T, HEAD_DIM), F32),
                   jax.ShapeDtypeStruct((N, H, HEAD_DIM, HEAD_DIM), F32)],
        scratch_shapes=[pltpu.VMEM((H, HEAD_DIM, HEAD_DIM), F32)],
        compiler_params=_cparams(("arbitrary",)),
    )(qg, kd, u, w, attn, eg)


def _gdn_chain_bwd(qg, kd, u, w, attn, eg, sall, do):
    H, T, _ = qg.shape
    N = T // CHUNK
    hs = pl.BlockSpec((H, CHUNK, HEAD_DIM), lambda n: (0, N - 1 - n, 0))
    ss = pl.BlockSpec((1, H, HEAD_DIM, HEAD_DIM), lambda n: (N - 1 - n, 0, 0, 0))
    hshape = jax.ShapeDtypeStruct((H, T, HEAD_DIM), F32)

    def body(qg_ref, kd_ref, u_ref, w_ref, at_ref, eg_ref, sall_ref, do_ref, *rest):
        out_refs, ds_ref = rest[:6], rest[6]

        @pl.when(pl.program_id(0) == 0)
        def _():
            ds_ref[...] = jnp.zeros_like(ds_ref)

        f32 = lambda ref: ref[...].astype(F32)
        _, vjp = jax.vjp(_gdn_chain_fn, sall_ref[0], f32(qg_ref), f32(kd_ref), u_ref[...], f32(w_ref),
                         f32(at_ref), eg_ref[...])
        grads = vjp((do_ref[...], ds_ref[...]))
        ds_ref[...] = grads[0]
        for ref, val in zip(out_refs, grads[1:]):
            ref[...] = val.astype(ref.dtype)

    kept = [F32, F32, BF16, BF16, F32, F32]
    return pl.pallas_call(
        body, name="gdn_chain_bwd", grid=(N,),
        in_specs=[hs] * 6 + [ss, hs], out_specs=[hs] * 6,
        out_shape=[jax.ShapeDtypeStruct((H, T, HEAD_DIM), dt) for dt in kept],
        scratch_shapes=[pltpu.VMEM((H, HEAD_DIM, HEAD_DIM), F32)],
        compiler_params=_cparams(("arbitrary",)),
    )(qg, kd, u, w, attn, eg, sall, do)


def _post_fn(ogs, za, hw):
    outs = []
    for h, o in enumerate(ogs):
        r = lax.rsqrt(jnp.mean(o * o, axis=-1, keepdims=True) + EPS)
        outs.append(o * r * hw * _silu(za[:, h * HEAD_DIM:(h + 1) * HEAD_DIM]))
    return jnp.concatenate(outs, axis=1)


def _gdn_post(og, proj_m, hw):
    H, T, _ = og.shape
    A = H * HEAD_DIM
    tm = _pick(T, (512, 256, 128))

    def body(og_ref, za_ref, hw_ref, o_ref, ot_ref):
        o = _post_fn(tuple(og_ref[h] for h in range(H)), za_ref[...], hw_ref[...])
        o_ref[...] = o.astype(BF16)
        ot_ref[...] = o.T.astype(BF16)

    return pl.pallas_call(
        body, name="gdn_post", grid=(T // tm,),
        in_specs=[pl.BlockSpec((H, tm, HEAD_DIM), lambda i: (0, i, 0)),
                  pl.BlockSpec((tm, A), lambda i: (i, ZA_BLOCK)),
                  pl.BlockSpec((1, HEAD_DIM), lambda i: (0, 0))],
        out_specs=[pl.BlockSpec((tm, A), lambda i: (i, 0)), pl.BlockSpec((A, tm), lambda i: (0, i))],
        out_shape=[jax.ShapeDtypeStruct((T, A), BF16), jax.ShapeDtypeStruct((A, T), BF16)],
        compiler_params=_cparams(("parallel",)),
    )(og, proj_m, hw)


def _gdn_post_bwd(og, proj_m, hw, d_o, dproj):
    H, T, _ = og.shape
    A = H * HEAD_DIM
    tm = _pick(T, (256, 128))

    def body(og_ref, za_ref, hw_ref, do_ref, _, dog_ref, dza_ref, dhw_ref):
        _, vjp = jax.vjp(_post_fn, tuple(og_ref[h] for h in range(H)), za_ref[...], hw_ref[...])
        dog, dza, dhw = vjp(do_ref[...])
        for h in range(H):
            dog_ref[h] = dog[h]
        dza_ref[...] = dza.astype(BF16)

        @pl.when(pl.program_id(0) == 0)
        def _():
            dhw_ref[...] = dhw

        @pl.when(pl.program_id(0) > 0)
        def _():
            dhw_ref[...] += dhw

    return pl.pallas_call(
        body, name="gdn_post_bwd", grid=(T // tm,),
        in_specs=[pl.BlockSpec((H, tm, HEAD_DIM), lambda i: (0, i, 0)),
                  pl.BlockSpec((tm, A), lambda i: (i, ZA_BLOCK)),
                  pl.BlockSpec((1, HEAD_DIM), lambda i: (0, 0)),
                  pl.BlockSpec((tm, A), lambda i: (i, 0)), ANY],
        out_specs=[pl.BlockSpec((H, tm, HEAD_DIM), lambda i: (0, i, 0)),
                   pl.BlockSpec((tm, A), lambda i: (i, ZA_BLOCK)),
                   pl.BlockSpec((1, HEAD_DIM), lambda i: (0, 0))],
        out_shape=[jax.ShapeDtypeStruct((H, T, HEAD_DIM), F32), jax.ShapeDtypeStruct(dproj.shape, dproj.dtype),
                   jax.ShapeDtypeStruct((1, HEAD_DIM), F32)],
        input_output_aliases={4: 1},
        compiler_params=_cparams(("arbitrary",)),
    )(og, proj_m, hw, d_o, dproj)


def _sgu_fn(ub, vb, zb, lw, lb, W, bbc):
    G = len(W)
    tm = ub.shape[0]
    mu = jnp.mean(vb, axis=-1, keepdims=True)
    xc = vb - mu
    var = jnp.mean(xc * xc, axis=-1, keepdims=True)
    vn = xc * lax.rsqrt(var + EPS) * lw + lb
    mask = _iota((CHUNK_B, CHUNK_B), 0) >= _iota((CHUNK_B, CHUNK_B), 1)
    cols = []
    for g in range(G):
        wm = jnp.where(mask, W[g], 0.0).astype(BF16)
        rows = []
        for c in range(tm // CHUNK_B):
            blk = vn[c * CHUNK_B:(c + 1) * CHUNK_B, g * HEAD_DIM:(g + 1) * HEAD_DIM].astype(BF16)
            rows.append(_dot(wm, blk) + bbc[g])
        cols.append(jnp.concatenate(rows, axis=0) if len(rows) > 1 else rows[0])
    s = jnp.concatenate(cols, axis=1)
    return ub * s * _silu(zb)


ZA_BLOCK = 6


def _sgu_cols(A, B):
    assert A == B
    return 3, 4, 5


def _sgu_fwd(proj_m, lw, lb, W, bbc, A):
    T = proj_m.shape[0]
    G = W.shape[0]
    B = G * HEAD_DIM
    tm = _pick(T, (256, 128))
    cu, cv, cz = _sgu_cols(A, B)

    def body(u_ref, v_ref, z_ref, lw_ref, lb_ref, w_ref, b_ref, o_ref, ot_ref):
        o = _sgu_fn(u_ref[...], v_ref[...], z_ref[...], lw_ref[...], lb_ref[...],
                    tuple(w_ref[g] for g in range(G)), tuple(b_ref[g] for g in range(G)))
        o_ref[...] = o.astype(BF16)
        ot_ref[...] = o.T.astype(BF16)

    row = pl.BlockSpec((1, B), lambda i: (0, 0))
    cube = pl.BlockSpec((G, CHUNK_B, CHUNK_B), lambda i: (0, 0, 0))
    return pl.pallas_call(
        body, name="sgu_fwd", grid=(T // tm,),
        in_specs=[pl.BlockSpec((tm, B), lambda i: (i, cu)), pl.BlockSpec((tm, B), lambda i: (i, cv)),
                  pl.BlockSpec((tm, B), lambda i: (i, cz)), row, row, cube, cube],
        out_specs=[pl.BlockSpec((tm, B), lambda i: (i, 0)), pl.BlockSpec((B, tm), lambda i: (0, i))],
        out_shape=[jax.ShapeDtypeStruct((T, B), BF16), jax.ShapeDtypeStruct((B, T), BF16)],
        compiler_params=_cparams(("parallel",)),
    )(proj_m, proj_m, proj_m, lw, lb, W, bbc)


def _sgu_bwd(proj_m, lw, lb, W, bbc, d_o, A, dproj):
    T = proj_m.shape[0]
    G = W.shape[0]
    B = G * HEAD_DIM
    tm = _pick(T, (256, 128))
    nt = T // tm
    cu, cv, cz = _sgu_cols(A, B)

    def body(u_ref, v_ref, z_ref, lw_ref, lb_ref, w_ref, b_ref, do_ref, _,
             dp_ref, dlw_ref, dlb_ref, dw_ref, db_ref, dbb_ref):
        _, vjp = jax.vjp(_sgu_fn, u_ref[...], v_ref[...], z_ref[...], lw_ref[...], lb_ref[...],
                         tuple(w_ref[g] for g in range(G)), tuple(b_ref[g] for g in range(G)))
        du, dv, dz, dlw, dlb, dW, dbb = vjp(do_ref[...])
        dW, dbb = jnp.stack(dW, axis=0), jnp.stack(dbb, axis=0)
        dp_ref[:, 0:B] = du.astype(BF16)
        dp_ref[:, B:2 * B] = dv.astype(BF16)
        dp_ref[:, 2 * B:3 * B] = dz.astype(BF16)
        i = pl.program_id(0)

        @pl.when(i == 0)
        def _():
            dlw_ref[...] = dlw
            dlb_ref[...] = dlb
            dw_ref[...] = dW
            dbb_ref[...] = dbb

        @pl.when(i > 0)
        def _():
            dlw_ref[...] += dlw
            dlb_ref[...] += dlb
            dw_ref[...] += dW
            dbb_ref[...] += dbb

        @pl.when(i == nt - 1)
        def _():
            db_ref[...] = jnp.sum(dbb_ref[...], axis=-1, keepdims=True)

    row = pl.BlockSpec((1, B), lambda i: (0, 0))
    cube = pl.BlockSpec((G, CHUNK_B, CHUNK_B), lambda i: (0, 0, 0))
    return pl.pallas_call(
        body, name="sgu_bwd", grid=(nt,),
        in_specs=[pl.BlockSpec((tm, B), lambda i: (i, cu)), pl.BlockSpec((tm, B), lambda i: (i, cv)),
                  pl.BlockSpec((tm, B), lambda i: (i, cz)), row, row, cube, cube,
                  pl.BlockSpec((tm, B), lambda i: (i, A // B)), ANY],
        out_specs=[pl.BlockSpec((tm, 3 * B), lambda i: (i, 1)), row, row, cube,
                   pl.BlockSpec((G, CHUNK_B, 1), lambda i: (0, 0, 0))],
        out_shape=[jax.ShapeDtypeStruct(dproj.shape, dproj.dtype), jax.ShapeDtypeStruct((1, B), F32),
                   jax.ShapeDtypeStruct((1, B), F32), jax.ShapeDtypeStruct((G, CHUNK_B, CHUNK_B), F32),
                   jax.ShapeDtypeStruct((G, CHUNK_B, 1), F32)],
        input_output_aliases={8: 0},
        scratch_shapes=[pltpu.VMEM((G, CHUNK_B, CHUNK_B), F32)],
        compiler_params=_cparams(("arbitrary",)),
    )(proj_m, proj_m, proj_m, lw, lb, W, bbc, d_o, dproj)


def _head_fn(mix, x, fw, tgt):
    h = x + mix
    y = _rms_fn(h, fw)
    e = y - tgt
    return 0.5 * jnp.sum(jnp.mean(e * e, axis=-1, keepdims=True), axis=0, keepdims=True)


def _out_proj_loss(oa, ob, wout, x, tgt, fw):
    T, A = oa.shape
    B = ob.shape[1]
    D = x.shape[1]
    tm = _pick(T, (256, 128))

    def body(oa_ref, ob_ref, w_ref, x_ref, t_ref, fw_ref, dh_ref, dhb_ref, loss_ref, dfw_ref):
        mix = _dot(oa_ref[...], w_ref[0:A, :]) + _dot(ob_ref[...], w_ref[A:A + B, :])
        xv, tv = x_ref[...], t_ref[...]
        loss, vjp = jax.vjp(lambda m, f: _head_fn(m, xv, f, tv), mix, fw_ref[...])
        dh, dfw = vjp(jnp.ones((1, 1), F32))
        dh_ref[...] = dh
        dhb_ref[...] = dh.astype(BF16)
        lrow = jnp.broadcast_to(loss, (1, LANES))

        @pl.when(pl.program_id(0) == 0)
        def _():
            loss_ref[...] = lrow
            dfw_ref[...] = dfw

        @pl.when(pl.program_id(0) > 0)
        def _():
            loss_ref[...] += lrow
            dfw_ref[...] += dfw

    tile = pl.BlockSpec((tm, D), lambda i: (i, 0))
    return pl.pallas_call(
        body, name="out_proj_loss", grid=(T // tm,),
        in_specs=[pl.BlockSpec((tm, A), lambda i: (i, 0)), pl.BlockSpec((tm, B), lambda i: (i, 0)),
                  pl.BlockSpec((A + B, D), lambda i: (0, 0)), tile, tile,
                  pl.BlockSpec((1, D), lambda i: (0, 0))],
        out_specs=[tile, tile, pl.BlockSpec((1, LANES), lambda i: (0, 0)),
                   pl.BlockSpec((1, D), lambda i: (0, 0))],
        out_shape=[jax.ShapeDtypeStruct((T, D), F32), jax.ShapeDtypeStruct((T, D), BF16),
                   jax.ShapeDtypeStruct((1, LANES), F32), jax.ShapeDtypeStruct((1, D), F32)],
        compiler_params=_cparams(("arbitrary",)),
    )(oa, ob, wout, x, tgt, fw)


def _adamw(w, g, m, v, name):
    R, Cn = w.shape
    cap = max(8, 512 * 1024 // Cn)
    tr = max(t for t in range(8, min(R, cap) + 1, 8) if R % t == 0) if R > cap else R

    def body(w_ref, g_ref, m_ref, v_ref, d_ref, mo_ref, vo_ref):
        g = g_ref[...]
        m = ADAM_B1 * m_ref[...] + (1.0 - ADAM_B1) * g
        v = ADAM_B2 * v_ref[...] + (1.0 - ADAM_B2) * jnp.square(g)
        m_hat = m / (1.0 - ADAM_B1 ** ADAM_STEP)
        v_hat = v / (1.0 - ADAM_B2 ** ADAM_STEP)
        d_ref[...] = -ADAM_LR * (m_hat / (jnp.sqrt(v_hat) + ADAM_EPS) + ADAM_WD * w_ref[...])
        mo_ref[...] = m
        vo_ref[...] = v

    tile = pl.BlockSpec((tr, Cn), lambda i: (i, 0))
    shape = jax.ShapeDtypeStruct((R, Cn), F32)
    return pl.pallas_call(
        body, name=name, grid=(R // tr,), in_specs=[tile] * 4, out_specs=[tile] * 3,
        out_shape=[shape] * 3, compiler_params=_cparams(("parallel",)),
    )(w, g, m, v)


def _place():
    x, y, c = lax.axis_index("x"), lax.axis_index("y"), lax.axis_index("c")
    others = [(1 - x, y), (x, 1 - y), (1 - x, 1 - y)]
    return x, y, c, others


def _chip_index(px, py):
    return 2 * px + py


ANY = pl.BlockSpec(memory_space=pl.ANY)


def _gather_ride(blocks, split):
    n = len(blocks)

    def plan(in_refs, out_refs, send_sems, recv_sems):
        x, y, c, _ = _place()
        me, kx, ky, kd = (_chip_index(px, py) for px, py in ((x, y), (1 - x, y), (x, 1 - y), (1 - x, 1 - y)))
        to_x, to_y, to_s = (1 - x, y, c), (x, 1 - y, c), (x, y, 1 - c)

        def copy(sem, src, dst, to):
            return pltpu.make_async_remote_copy(src_ref=src, dst_ref=dst, send_sem=send_sems.at[sem],
                                                recv_sem=recv_sems.at[sem], device_id=to, device_id_type=MESH_ID)

        first, second, third, awaited = [], [], [], []
        for a in range(n):
            out, s0 = out_refs[a], 8 * a
            if not split[a]:
                for j, (k, to) in enumerate(((kx, to_x), (ky, to_y), (kd, (1 - x, 1 - y, c)))):
                    first.append(lambda j=j, to=to, a=a, out=out, s0=s0: copy(s0 + j, in_refs[a], out.at[me], to))
                    awaited.append((lambda j=j, k=k, to=to, out=out, s0=s0: copy(s0 + j, out.at[k], out.at[k], to),
                                    None))
                continue
            h = blocks[a].shape[0] // 2
            q = h // 2
            half = lambda k, core, out=out, h=h: out.at[k, pl.ds(core * h, h), :]
            quarter = lambda k, core, i, out=out, h=h, q=q: out.at[k, pl.ds(core * h + i * q, q), :]
            mine = in_refs[a].at[pl.ds(c * h, h), :]
            first.append(lambda s0=s0, mine=mine, half=half: copy(s0, mine, half(me, c), to_x))
            first.append(lambda s0=s0, mine=mine, half=half: copy(s0 + 1, mine, half(me, c), to_y))
            fwd0 = lambda s0=s0, quarter=quarter: copy(s0 + 2, quarter(kx, c, 0), quarter(kx, c, 0), to_y)
            fwd1 = lambda s0=s0, quarter=quarter: copy(s0 + 3, quarter(ky, c, 1), quarter(ky, c, 1), to_x)
            pieces = [(s0 + 0, lambda half=half: half(kx, c), lambda half=half: half(kx, 1 - c), to_x, fwd0),
                      (s0 + 1, lambda half=half: half(ky, c), lambda half=half: half(ky, 1 - c), to_y, fwd1),
                      (s0 + 2, lambda quarter=quarter: quarter(kd, c, 0), lambda quarter=quarter: quarter(kd, 1 - c, 0),
                       to_y, None),
                      (s0 + 3, lambda quarter=quarter: quarter(kd, c, 1), lambda quarter=quarter: quarter(kd, 1 - c, 1),
                       to_x, None)]
            for i, (sem, here, there, frm, fwd) in enumerate(pieces):
                passing = lambda s0=s0, i=i, here=here: copy(s0 + 4 + i, here(), here(), to_s)
                awaited.append((lambda sem=sem, here=here, frm=frm: copy(sem, here(), here(), frm), (fwd, passing)))
                if fwd is not None:
                    second.append(fwd)
                third.append((passing, lambda s0=s0, i=i, there=there: copy(s0 + 4 + i, there(), there(), to_s)))
        return first, second, third, awaited

    def start(*refs):
        for send in plan(*refs)[0]:
            send().start()

    def finish(*refs):
        first, second, third, awaited = plan(*refs)
        for arrival, then in awaited:
            arrival().wait_recv()
            for nxt in (then or ()):
                if nxt is not None:
                    nxt().start()
        for _, from_sibling in third:
            from_sibling().wait_recv()
        for send in first + second + [p for p, _ in third]:
            send().wait_send()

    shapes = [jax.ShapeDtypeStruct((N_CHIPS,) + b.shape, b.dtype) for b in blocks]
    return _Ride(blocks, shapes, 8 * n, start, finish)


def _put_own(gathered, own):
    me = _chip_index(lax.axis_index("x"), lax.axis_index("y"))
    return lax.dynamic_update_index_in_dim(gathered, own, me, 0)


def _allreduce_small(buf):
    R, L = buf.shape

    def body(in_ref, out_ref, sib_ref, pair_ref, chips_ref, send_sems, recv_sems):
        x, y, c, others = _place()
        me = _chip_index(x, y)
        sibling = (x, y, 1 - c)
        cp = pltpu.make_async_remote_copy(src_ref=in_ref, dst_ref=sib_ref, send_sem=send_sems.at[0],
                                          recv_sem=recv_sems.at[0], device_id=sibling, device_id_type=MESH_ID)
        cp.start()
        cp.wait()
        pair_ref[...] = in_ref[...] + sib_ref[...]
        sends = []
        for j, chip in enumerate(others):
            s = pltpu.make_async_remote_copy(src_ref=pair_ref, dst_ref=chips_ref.at[me],
                                             send_sem=send_sems.at[1 + j], recv_sem=recv_sems.at[1 + j],
                                             device_id=(*chip, c), device_id_type=MESH_ID)
            s.start()
            sends.append(s)
        chips_ref[me] = pair_ref[...]
        for j, chip in enumerate(others):
            k = _chip_index(*chip)
            pltpu.make_async_remote_copy(src_ref=pair_ref, dst_ref=chips_ref.at[k], send_sem=send_sems.at[1 + j],
                                         recv_sem=recv_sems.at[1 + j], device_id=(*chip, c),
                                         device_id_type=MESH_ID).wait_recv()
        for s in sends:
            s.wait_send()
        out_ref[...] = ((chips_ref[0] + chips_ref[1]) + chips_ref[2]) + chips_ref[3]

    vm = pl.BlockSpec(memory_space=pltpu.VMEM)
    return pl.pallas_call(
        body, name="allreduce_small", in_specs=[vm], out_specs=vm,
        out_shape=jax.ShapeDtypeStruct((R, L), F32),
        scratch_shapes=[pltpu.VMEM((R, L), F32), pltpu.VMEM((R, L), F32), pltpu.VMEM((N_CHIPS, R, L), F32),
                        pltpu.SemaphoreType.DMA((4,)), pltpu.SemaphoreType.DMA((4,))],
        compiler_params=pltpu.CompilerParams(vmem_limit_bytes=VMEM_LIMIT),
    )(buf)


def _pair_ride(g):
    nb, R, Cn = g.shape
    h = R // 2

    def copy(in_refs, out_refs, send_sems, recv_sems):
        x, y, c, _ = _place()
        return pltpu.make_async_remote_copy(src_ref=in_refs[0].at[:, pl.ds((1 - c) * h, h), :], dst_ref=out_refs[0],
                                            send_sem=send_sems.at[0], recv_sem=recv_sems.at[0],
                                            device_id=(x, y, 1 - c), device_id_type=MESH_ID)

    return _Ride([g], [jax.ShapeDtypeStruct((nb, h, Cn), g.dtype)], 1,
                 lambda *refs: copy(*refs).start(), lambda *refs: copy(*refs).wait())


def _pair_sum(g, land, c_arr, name, ride=None):
    nb, R, Cn = g.shape
    hr = R // 2
    tr = _pick(hr, (256, 128, 64, 32, 16))
    nt = hr // tr

    def body(c_ref, g_ref, l_ref, o_ref):
        o_ref[...] = (g_ref[...] + l_ref[...]).astype(BF16)

    return _pallas(
        body, (c_arr, g, land), name=name, prefetch=1, grid=(nb, nt),
        in_specs=[pl.BlockSpec((1, tr, Cn), lambda b, i, c_ref: (b, c_ref[0] * nt + i, 0)),
                  pl.BlockSpec((1, tr, Cn), lambda b, i, c_ref: (b, i, 0))],
        out_specs=pl.BlockSpec((1, tr, Cn), lambda b, i, c_ref: (b, i, 0)),
        out_shape=jax.ShapeDtypeStruct((nb, hr, Cn), BF16),
        semantics=("parallel", "parallel"), ride=ride)


def _chip_ride(parts):
    m = len(parts)

    def copies(in_refs, out_refs, send_sems, recv_sems):
        x, y, c, others = _place()
        me = _chip_index(x, y)
        def mk(j, chip, n, landing):
            k = _chip_index(*chip)
            return pltpu.make_async_remote_copy(
                src_ref=in_refs[n].at[k], dst_ref=out_refs[n].at[landing(k)], send_sem=send_sems.at[m * j + n],
                recv_sem=recv_sems.at[m * j + n], device_id=(*chip, c), device_id_type=MESH_ID)

        pairs = [(j, chip, n) for j, chip in enumerate(others) for n in range(m)]
        return pairs, (lambda *p: mk(*p, lambda k: me)), (lambda *p: mk(*p, lambda k: k))

    def start(*refs):
        pairs, send, _ = copies(*refs)
        for p in pairs:
            send(*p).start()

    def finish(*refs):
        pairs, send, arrival = copies(*refs)
        for p in pairs:
            arrival(*p).wait_recv()
        for p in pairs:
            send(*p).wait_send()

    return _Ride(parts, [jax.ShapeDtypeStruct(p.shape, p.dtype) for p in parts], 3 * m, start, finish)


def _put_own_slot(q, p):
    me = _chip_index(lax.axis_index("x"), lax.axis_index("y"))
    return lax.dynamic_update_index_in_dim(q, lax.dynamic_index_in_dim(p, me, 0, keepdims=False), me, 0)


def _chip_sum(q, c_arr, name):
    nb, hr, Cn = q.shape
    tr = _pick(hr, (256, 128, 64, 32, 16))
    nt = hr // tr

    def body(c_ref, q_ref, o_ref):
        f = lambda k: q_ref[k].astype(F32)
        o_ref[...] = ((f(0) + f(1)) + f(2)) + f(3)

    return _pallas(
        body, (c_arr, q), name=name, prefetch=1, grid=(nt,),
        in_specs=[pl.BlockSpec((nb, tr, Cn), lambda i, c_ref: (0, i, 0))],
        out_specs=pl.BlockSpec((tr, Cn), lambda i, c_ref: (c_ref[0] * nt + i, 0)),
        out_shape=jax.ShapeDtypeStruct((2 * hr, Cn), F32),
        semantics=("parallel",))


def _sibling_fill(fw, fo):
    def body(_, __, fw_ref, fo_ref, send_sems, recv_sems):
        x, y, c, _ = _place()
        copies = []
        for n, ref in enumerate((fw_ref, fo_ref)):
            h = ref.shape[0] // 2
            mine = ref.at[pl.ds(c * h, h), :]
            theirs = ref.at[pl.ds((1 - c) * h, h), :]
            mk = lambda src, dst: pltpu.make_async_remote_copy(
                src_ref=src, dst_ref=dst, send_sem=send_sems.at[n], recv_sem=recv_sems.at[n],
                device_id=(x, y, 1 - c), device_id_type=MESH_ID)
            send = mk(mine, mine)
            send.start()
            copies.append((send, mk(theirs, theirs)))
        for send, arrival in copies:
            arrival.wait_recv()
            send.wait_send()

    return pl.pallas_call(
        body, name="sibling_fill", in_specs=[ANY, ANY], out_specs=[ANY, ANY],
        out_shape=[jax.ShapeDtypeStruct(fw.shape, F32), jax.ShapeDtypeStruct(fo.shape, F32)],
        input_output_aliases={0: 0, 1: 1},
        scratch_shapes=[pltpu.SemaphoreType.DMA((2,)), pltpu.SemaphoreType.DMA((2,))],
        compiler_params=pltpu.CompilerParams(has_side_effects=True),
    )(fw, fo)


class _Layout:
    def __init__(self, H, G, nb, Cb):
        A, B = H * HEAD_DIM, G * HEAD_DIM
        self.n_main = 4 * A + 3 * B
        self.k = -(-(self.n_main + LANES) // WIN_BLOCK) * WIN_BLOCK
        cuts = [0, 3 * A, 4 * A, 4 * A + 2 * H, nb * Cb]
        starts = [0, 3 * A + 3 * B, self.n_main, 3 * A]
        self.pieces = []
        self.windows, self.runs = [], []
        for n in range(nb):
            segs = []
            for s in range(4):
                lo, hi = max(cuts[s], n * Cb), min(cuts[s + 1], (n + 1) * Cb)
                if lo < hi:
                    segs.append((starts[s] + lo - cuts[s], lo - n * Cb, hi - lo))
            self.pieces += [(own, n, col, ln) for own, col, ln in segs]
            blocks = sorted({b for own, _, ln in segs for b in range(own // WIN_BLOCK, (own + ln - 1) // WIN_BLOCK + 1)})
            self.windows.append(blocks)
            self.runs.append([(blocks.index(own // WIN_BLOCK) * WIN_BLOCK + own % WIN_BLOCK, ln)
                              for own, _, ln in segs])
        self.wb = max(len(b) for b in self.windows)
        self.table = [b + [b[-1]] * (self.wb - len(b)) for b in self.windows]
        self.pieces.sort()

    def to_own_order(self, g_in):
        D = g_in.shape[1]
        cols, at = [], 0
        for own, n, col, ln in self.pieces:
            if own > at:
                cols.append(jnp.zeros((D, own - at), g_in.dtype))
            cols.append(g_in[n, :, col:col + ln])
            at = own + ln
        if at < self.k:
            cols.append(jnp.zeros((D, self.k - at), g_in.dtype))
        return jnp.concatenate(cols, axis=1)

    def from_window(self, win, chip, Cb):
        pick = lambda runs: (lambda w: jnp.concatenate([w[:, c:c + ln] for c, ln in runs], axis=1))
        return lax.switch(chip, [pick(r) for r in self.runs], win)


def _device_step(x, tgt, norm_w, win_b, wout_b, conv_b, a_log, dt_bias, head_norm_w, sgu_ln_w, sgu_ln_b,
                 w_spatial, b_spatial, final_norm_w, c_arr):
    T, D = x.shape
    H = a_log.shape[1]
    A = H * HEAD_DIM
    G = w_spatial.shape[0]
    B = G * HEAD_DIM
    nb, Cb, Rb = N_CHIPS, win_b.shape[1], wout_b.shape[0]
    lay = _Layout(H, G, nb, Cb)
    alog_row = jnp.pad(a_log, ((0, 0), (H, LANES - 2 * H)))
    dtb_row = jnp.pad(dt_bias, ((0, 0), (H, LANES - 2 * H)))
    bbc = jnp.broadcast_to(b_spatial[:, :, None], (G, CHUNK_B, CHUNK_B))

    (xn, xn_t), (g_in,) = _rms_in(x, norm_w, ride=_gather_ride([win_b], [True]))
    w_own = lay.to_own_order(_put_own(g_in, win_b))
    proj_m, (g_out, g_conv) = _mm_nn(xn, w_own, F32, "in_proj", cols=(0, lay.n_main),
                                     ride=_gather_ride([wout_b, conv_b], [True, False]))
    wout = _put_own(g_out, wout_b).reshape(nb * Rb, D)
    conv_w = _put_own(g_conv, conv_b).transpose(1, 0, 2).reshape(CONV_WIDTH, nb * conv_b.shape[1])
    proj_ba = _mm_nn(xn, w_own, F32, "in_proj_ba", cols=(lay.n_main, LANES))
    q, k, v, gb, bb = _gdn_pre(proj_m, proj_ba, conv_w, alog_row, dtb_row, H)
    u, w, qg, kd, attn, eg, pinv = _gdn_prep(q, k, v, gb, bb)
    og, sall = _gdn_chain(qg, kd, u, w, attn, eg)
    oa, oa_t = _gdn_post(og, proj_m, head_norm_w)
    ob, ob_t = _sgu_fwd(proj_m, sgu_ln_w, sgu_ln_b, w_spatial, bbc, A)
    dh, dhb, loss_row, d_fnw = _out_proj_loss(oa, ob, wout, x, tgt, final_norm_w.reshape(1, D))

    d_o = _mm_nn(dhb, wout.T, F32, "out_proj_dx")
    dproj = lax.empty((T, lay.k), BF16)
    dproj, d_lw, d_lb, d_ws, d_bs = _sgu_bwd(proj_m, sgu_ln_w, sgu_ln_b, w_spatial, bbc, d_o, A, dproj)
    dog, dproj, d_hw = _gdn_post_bwd(og, proj_m, head_norm_w, d_o, dproj)
    dqg, dkd, du, dw, dat, deg = _gdn_chain_bwd(qg, kd, u, w, attn, eg, sall, dog)
    dq, dk, dv, dgb, dbb = _gdn_prep_bwd(q, k, v, gb, bb, pinv, du, dw, dqg, dkd, dat, deg)
    dc, dproj, d_al, d_dt = _gdn_pre_bwd(proj_m, proj_ba, conv_w, alog_row, dtb_row, dq, dk, dv, dgb, dbb, H,
                                         dproj)
    dproj, d_conv = _conv_bwd(proj_m, dc, conv_w, H, dproj)

    table = jnp.array([b for row in lay.table for b in row], jnp.int32)
    d_win = _mm_windows(xn_t, dproj, table, nb, "in_proj_dw")
    d_wout, (land_w,) = _mm_nn_pair(oa_t, ob_t, dhb, "out_proj_dw", ride=_pair_ride(d_win))
    d_wout = d_wout.reshape(nb, Rb, D)
    pair_w, (land_o,) = _pair_sum(d_win, land_w, c_arr, "pair_sum_w_in", ride=_pair_ride(d_wout))
    pair_o = _pair_sum(d_wout, land_o, c_arr, "pair_sum_w_out")
    dxn, (all_w,) = _mm_nt_rhs_outer(dproj, w_own, F32, "in_proj_dx", ride=_chip_ride([pair_w]))
    (grad_x, d_nw), (all_o,) = _rms_in_bwd(x, norm_w, dxn, dh, ride=_chip_ride([pair_o]))
    all_w, all_o = _put_own_slot(all_w, pair_w), _put_own_slot(all_o, pair_o)
    small = dict(norm_w=d_nw, conv_w=d_conv[:CONV_WIDTH], a_log=d_al[:, H:2 * H], dt_bias=d_dt[:, H:2 * H],
                 head_norm_w=d_hw, sgu_ln_w=d_lw, sgu_ln_b=d_lb, w_spatial=d_ws, b_spatial=d_bs[:, :, 0],
                 final_norm_w=d_fnw)
    return loss_row, grad_x, small, all_w, all_o


SMALL = ("norm_w", "conv_w", "a_log", "dt_bias", "head_norm_w", "sgu_ln_w", "sgu_ln_b", "w_spatial",
         "b_spatial", "final_norm_w")


def _pack(parts):
    rows = []
    for p in parts:
        f = p.reshape(-1)
        f = jnp.pad(f, (0, (-f.shape[0]) % (8 * LANES)))
        rows.append(f.reshape(-1, LANES))
    return jnp.concatenate(rows, axis=0)


def _unpack(buf, shapes):
    out, r = [], 0
    for s in shapes:
        n = 1
        for d in s:
            n *= d
        nr = -(-n // (8 * LANES)) * 8
        out.append(buf[r:r + nr].reshape(-1)[:n].reshape(s))
        r += nr
    return out


def kernel(x, norm_w, w_in, conv_w, a_log, dt_bias, head_norm_w, sgu_ln_w, sgu_ln_b, w_spatial, b_spatial, w_out, final_norm_w, loss_target, m_norm_w, m_w_in, m_conv_w, m_a_log, m_dt_bias, m_head_norm_w, m_sgu_ln_w, m_sgu_ln_b, m_w_spatial, m_b_spatial, m_w_out, m_final_norm_w, v_norm_w, v_w_in, v_conv_w, v_a_log, v_dt_bias, v_head_norm_w, v_sgu_ln_w, v_sgu_ln_b, v_w_spatial, v_b_spatial, v_w_out, v_final_norm_w):
    T, D = x.shape[1], x.shape[2]
    weights = dict(norm_w=norm_w, w_in=w_in, conv_w=conv_w, a_log=a_log, dt_bias=dt_bias, head_norm_w=head_norm_w,
                   sgu_ln_w=sgu_ln_w, sgu_ln_b=sgu_ln_b, w_spatial=w_spatial, b_spatial=b_spatial, w_out=w_out,
                   final_norm_w=final_norm_w)
    mom_m = dict(norm_w=m_norm_w, w_in=m_w_in, conv_w=m_conv_w, a_log=m_a_log, dt_bias=m_dt_bias,
                 head_norm_w=m_head_norm_w, sgu_ln_w=m_sgu_ln_w, sgu_ln_b=m_sgu_ln_b, w_spatial=m_w_spatial,
                 b_spatial=m_b_spatial, w_out=m_w_out, final_norm_w=m_final_norm_w)
    mom_v = dict(norm_w=v_norm_w, w_in=v_w_in, conv_w=v_conv_w, a_log=v_a_log, dt_bias=v_dt_bias,
                 head_norm_w=v_head_norm_w, sgu_ln_w=v_sgu_ln_w, sgu_ln_b=v_sgu_ln_b, w_spatial=v_w_spatial,
                 b_spatial=v_b_spatial, w_out=v_w_out, final_norm_w=v_final_norm_w)
    me = _chip_index(lax.axis_index("x"), lax.axis_index("y"))
    c_arr = lax.axis_index("c").astype(jnp.int32).reshape(1)
    Din, Cb = w_in.shape[1], w_in.shape[2]
    Rb = w_out.shape[1]
    cconv = conv_w.shape[2]

    loss_row, grad_x, g, qw, qo = _device_step(
        x[0], loss_target[0], norm_w, w_in[0].astype(BF16), w_out[0].astype(BF16), conv_w[0], a_log, dt_bias,
        head_norm_w, sgu_ln_w, sgu_ln_b, w_spatial[0], b_spatial[0], final_norm_w, c_arr)

    small_shapes = [tuple(g[n].shape) for n in SMALL]
    small = _allreduce_small(_pack([g[n] for n in SMALL]))
    gsum_in, gsum_out = _sibling_fill(_chip_sum(qw, c_arr, "chip_sum_w_in"), _chip_sum(qo, c_arr, "chip_sum_w_out"))
    gsum_in = _Layout(a_log.shape[1], w_spatial.shape[1], N_CHIPS, Cb).from_window(gsum_in, me, Cb)
    gsmall = dict(zip(SMALL, _unpack(small, small_shapes)))
    gsmall["conv_w"] = lax.dynamic_slice_in_dim(gsmall["conv_w"], me * cconv, cconv, axis=1)

    grads, deltas, new_m, new_v = {}, {}, {}, {}
    d, m2, v2 = _adamw(w_out[0], gsum_out, m_w_out[0], v_w_out[0], "adamw_w_out")
    grads["w_out"], deltas["w_out"], new_m["w_out"], new_v["w_out"] = gsum_out[None], d[None], m2[None], v2[None]
    flat = lambda a: a.transpose(2, 0, 1).reshape(-1, LANES)
    unflat = lambda f: f.reshape(Cb, 1, Din).transpose(1, 2, 0)
    g_flat = gsum_in.T.reshape(-1, LANES)
    d, m2, v2 = _adamw(flat(w_in), g_flat, flat(m_w_in), flat(v_w_in), "adamw_w_in")
    grads["w_in"], deltas["w_in"], new_m["w_in"], new_v["w_in"] = unflat(g_flat), unflat(d), unflat(m2), unflat(v2)
    shapes = [tuple(weights[n].shape) for n in SMALL]
    ds, ms, vs = _adamw(_pack([weights[n] for n in SMALL]), _pack([gsmall[n] for n in SMALL]),
                        _pack([mom_m[n] for n in SMALL]), _pack([mom_v[n] for n in SMALL]), "adamw_small")
    for n, gq, d, m2, v2 in zip(SMALL, [gsmall[n] for n in SMALL], _unpack(ds, shapes), _unpack(ms, shapes),
                                _unpack(vs, shapes)):
        grads[n], deltas[n], new_m[n], new_v[n] = gq.reshape(weights[n].shape), d, m2, v2

    loss = lax.psum(loss_row[0, 0], ("x", "y", "c"))
    order = ("norm_w", "w_in", "conv_w", "a_log", "dt_bias", "head_norm_w", "sgu_ln_w", "sgu_ln_b", "w_spatial",
             "b_spatial", "w_out", "final_norm_w")
    return (loss, grad_x[None], *[grads[n] for n in order], *[deltas[n] for n in order],
            *[new_m[n] for n in order], *[new_v[n] for n in order])
```

```python
import functools

import jax
import jax.numpy as jnp
from jax import lax
from jax.experimental import pallas as pl
from jax.experimental.pallas import tpu as pltpu

F32 = jnp.float32
BF16 = jnp.bfloat16
EPS = 1e-6
HEAD_DIM = 128
CHUNK_B = 128
CONV_WIDTH = 4
LANES = 128
HALO = 8
N_CHIPS = 4
ADAM_LR = 0.001
ADAM_B1 = 0.9
ADAM_B2 = 0.999
ADAM_EPS = 1e-08
ADAM_WD = 0.01
ADAM_STEP = 10
VMEM_LIMIT = 56 * 1024 * 1024
MESH_ID = pl.DeviceIdType.MESH
HI = lax.Precision.HIGHEST


def _cparams(sem=None, **kw):
    return pltpu.CompilerParams(dimension_semantics=sem, vmem_limit_bytes=VMEM_LIMIT, **kw)


def _matmul(a, b, ca, cb, precision):
    nb = a.ndim - 2
    batch = tuple(range(nb))
    return lax.dot_general(a, b, (((ca + nb,), (cb + nb,)), (batch, batch)), precision=precision,
                           preferred_element_type=F32)


def _dot(a, b, hi=False, precision=None):
    return _matmul(a, b, 1, 0, HI if hi else precision)


def _dot_nt(a, b, hi=False, precision=None):
    return _matmul(a, b, 1, 1, HI if hi else precision)


def _dot_tn(a, b, hi=False, precision=None):
    return _matmul(a, b, 0, 0, HI if hi else precision)


def _iota(shape, dim):
    return lax.broadcasted_iota(jnp.int32, shape, dim)


def _sigmoid(x):
    return 0.5 * (jnp.tanh(0.5 * x) + 1.0)


def _silu(x):
    return x * _sigmoid(x)


def _softplus(x):
    z = jnp.exp(-jnp.abs(x))
    small = z * (1.0 - z * (0.5 - z * (1.0 / 3.0)))
    return jnp.maximum(x, 0.0) + jnp.where(z < 1e-3, small, jnp.log(1.0 + z))


def _pick(n, pref):
    for t in pref:
        if n % t == 0:
            return t
    return n


class _Ride:
    def __init__(self, operands, out_shape, n_sems, start, finish):
        self.operands, self.out_shape, self.n_sems = list(operands), list(out_shape), n_sems
        self.start, self.finish = start, finish


def _pallas(body, operands, *, name, grid, in_specs, out_specs, out_shape, semantics, scratch_shapes=(),
            prefetch=0, ride=None):
    single = not isinstance(out_shape, (list, tuple))
    outs = [out_shape] if single else list(out_shape)
    ospecs = [out_specs] if single else list(out_specs)
    in_specs, scratch = list(in_specs), list(scratch_shapes)
    n_in, n_out, n_sc = len(operands) - prefetch, len(outs), len(scratch)
    kernel = body
    params = _cparams(semantics)
    if ride is not None:
        n_xin, n_xout = len(ride.operands), len(ride.out_shape)

        def kernel(*refs):
            pre, refs = refs[:prefetch], refs[prefetch:]
            ins, refs = refs[:n_in], refs[n_in:]
            xins, refs = refs[:n_xin], refs[n_xin:]
            mains, refs = refs[:n_out], refs[n_out:]
            xouts, refs = refs[:n_xout], refs[n_xout:]
            sc, (send, recv) = refs[:n_sc], refs[n_sc:]
            ids = [pl.program_id(a) for a in range(len(grid))]
            first = functools.reduce(jnp.logical_and, [i == 0 for i in ids])
            last = functools.reduce(jnp.logical_and, [i == g - 1 for i, g in zip(ids, grid)])

            @pl.when(first)
            def _():
                ride.start(xins, xouts, send, recv)

            body(*pre, *ins, *mains, *sc)

            @pl.when(last)
            def _():
                ride.finish(xins, xouts, send, recv)

        operands = list(operands) + ride.operands
        in_specs += [ANY] * n_xin
        ospecs += [ANY] * n_xout
        outs += ride.out_shape
        scratch += [pltpu.SemaphoreType.DMA((ride.n_sems,)), pltpu.SemaphoreType.DMA((ride.n_sems,))]
        params = _cparams(("arbitrary",) * len(grid), has_side_effects=True)
    if prefetch:
        spec = dict(grid_spec=pltpu.PrefetchScalarGridSpec(
            num_scalar_prefetch=prefetch, grid=grid, in_specs=in_specs, out_specs=ospecs, scratch_shapes=scratch))
    else:
        spec = dict(grid=grid, in_specs=in_specs, out_specs=ospecs, scratch_shapes=scratch)
    res = pl.pallas_call(kernel, name=name, out_shape=outs, compiler_params=params, **spec)(*operands)
    main = res[0] if single else list(res[:n_out])
    return main if ride is None else (main, list(res[n_out:]))


def _mm_nn(a, b, out_dtype, name, tm=1024, tn=512, tk=None, cols=None, ride=None):
    M, K = a.shape
    c0, N = (0, b.shape[1]) if cols is None else cols
    tm = _pick(M, (tm, 512, 256, 128))
    tn = _pick(N, (tn, 512, 384, 256, 128))
    tk = K if tk is None else _pick(K, (tk,))
    nk = K // tk
    j0 = c0 // tn
    assert c0 % tn == 0

    def body(a_ref, b_ref, o_ref, *scratch):
        part = _dot(a_ref[...], b_ref[...])
        if nk == 1:
            o_ref[...] = part.astype(out_dtype)
        else:
            acc_ref, = scratch
            k = pl.program_id(2)

            @pl.when(k == 0)
            def _():
                acc_ref[...] = part

            @pl.when(k > 0)
            def _():
                acc_ref[...] += part

            @pl.when(k == nk - 1)
            def _():
                o_ref[...] = acc_ref[...].astype(out_dtype)

    return _pallas(
        body, (a, b), name=name, grid=(M // tm, N // tn, nk),
        in_specs=[pl.BlockSpec((tm, tk), lambda i, j, k: (i, k)),
                  pl.BlockSpec((tk, tn), lambda i, j, k: (k, j + j0))],
        out_specs=pl.BlockSpec((tm, tn), lambda i, j, k: (i, j)),
        out_shape=jax.ShapeDtypeStruct((M, N), out_dtype),
        scratch_shapes=[] if nk == 1 else [pltpu.VMEM((tm, tn), F32)],
        semantics=("parallel", "parallel", "arbitrary"), ride=ride)


def _mm_nt_rhs_outer(a, b, out_dtype, name, tm=256, tn=1024, ride=None):
    M, K = a.shape
    N, _ = b.shape
    tm = _pick(M, (tm, 128))
    tn = _pick(N, (tn, 512, 256, 128))

    def body(a_ref, b_ref, o_ref):
        o_ref[...] = _dot_nt(a_ref[...], b_ref[...]).astype(out_dtype)

    return _pallas(
        body, (a, b), name=name, grid=(N // tn, M // tm),
        in_specs=[pl.BlockSpec((tm, K), lambda j, i: (i, 0)),
                  pl.BlockSpec((tn, K), lambda j, i: (j, 0))],
        out_specs=pl.BlockSpec((tm, tn), lambda j, i: (i, j)),
        out_shape=jax.ShapeDtypeStruct((M, N), out_dtype),
        semantics=("parallel", "parallel"), ride=ride)


WIN_BLOCK = 256


def _mm_windows(a, b, table, nb, name, tm=2048):
    M, K = a.shape
    wb = table.shape[0] // nb
    tm = _pick(M, (tm, 1024, 512, 256, 128))

    def body(tab_ref, a_ref, b_ref, o_ref):
        o_ref[0] = _dot(a_ref[...], b_ref[...])

    return pl.pallas_call(
        body, name=name,
        grid_spec=pltpu.PrefetchScalarGridSpec(
            num_scalar_prefetch=1, grid=(nb, M // tm, wb),
            in_specs=[pl.BlockSpec((tm, K), lambda n, i, t, tab: (i, 0)),
                      pl.BlockSpec((K, WIN_BLOCK), lambda n, i, t, tab: (0, tab[n * wb + t]))],
            out_specs=pl.BlockSpec((1, tm, WIN_BLOCK), lambda n, i, t, tab: (n, i, t))),
        out_shape=jax.ShapeDtypeStruct((nb, M, wb * WIN_BLOCK), F32),
        compiler_params=_cparams(("parallel", "parallel", "arbitrary")),
    )(table, a, b)


def _mm_nn_pair(a0, a1, b, name, tm=512, tn=1024, ride=None):
    M, K = a0.shape
    _, N = b.shape
    tm = _pick(M, (tm, 256, 128))
    tn = _pick(N, (tn, 512, 256, 128))
    ni = M // tm

    def body(a0_ref, a1_ref, b_ref, o_ref):
        p = pl.program_id(0)

        @pl.when(p == 0)
        def _():
            o_ref[...] = _dot(a0_ref[...], b_ref[...])

        @pl.when(p == 1)
        def _():
            o_ref[...] = _dot(a1_ref[...], b_ref[...])

    return _pallas(
        body, (a0, a1, b), name=name, grid=(2, ni, N // tn),
        in_specs=[pl.BlockSpec((tm, K), lambda p, i, j: (i * (1 - p), 0)),
                  pl.BlockSpec((tm, K), lambda p, i, j: (i * p, 0)),
                  pl.BlockSpec((K, tn), lambda p, i, j: (0, j))],
        out_specs=pl.BlockSpec((tm, tn), lambda p, i, j: (p * ni + i, j)),
        out_shape=jax.ShapeDtypeStruct((2 * M, N), F32),
        semantics=("parallel", "parallel", "parallel"), ride=ride)


def _rms_fn(x, w):
    r = lax.rsqrt(jnp.mean(x * x, axis=-1, keepdims=True) + EPS)
    return x * r * w


def _rms_in(x, w, ride=None):
    T, D = x.shape
    tm = _pick(T, (512, 256, 128))

    def body(x_ref, w_ref, o_ref, ot_ref):
        xn = _rms_fn(x_ref[...], w_ref[...])
        o_ref[...] = xn.astype(BF16)
        ot_ref[...] = xn.T.astype(BF16)

    return _pallas(
        body, (x, w), name="rms_in", grid=(T // tm,),
        in_specs=[pl.BlockSpec((tm, D), lambda i: (i, 0)), pl.BlockSpec((1, D), lambda i: (0, 0))],
        out_specs=[pl.BlockSpec((tm, D), lambda i: (i, 0)), pl.BlockSpec((D, tm), lambda i: (0, i))],
        out_shape=[jax.ShapeDtypeStruct((T, D), BF16), jax.ShapeDtypeStruct((D, T), BF16)],
        semantics=("parallel",), ride=ride)


def _rms_in_bwd(x, w, dxn, dh, ride=None):
    T, D = x.shape
    tm = _pick(T, (256, 128))

    def body(x_ref, w_ref, dxn_ref, dh_ref, gx_ref, dw_ref):
        _, vjp = jax.vjp(_rms_fn, x_ref[...], w_ref[...])
        dx, dw = vjp(dxn_ref[...])
        gx_ref[...] = dh_ref[...] + dx

        @pl.when(pl.program_id(0) == 0)
        def _():
            dw_ref[...] = dw

        @pl.when(pl.program_id(0) > 0)
        def _():
            dw_ref[...] += dw

    tile = pl.BlockSpec((tm, D), lambda i: (i, 0))
    row = pl.BlockSpec((1, D), lambda i: (0, 0))
    return _pallas(
        body, (x, w, dxn, dh), name="rms_in_bwd", grid=(T // tm,),
        in_specs=[tile, row, tile, tile], out_specs=[tile, row],
        out_shape=[jax.ShapeDtypeStruct((T, D), F32), jax.ShapeDtypeStruct((1, D), F32)],
        semantics=("arbitrary",), ride=ride)


def _conv_fwd(cat_ref, halo, x, w):
    tm = x.shape[0]
    cat_ref[0:HALO, :] = halo
    cat_ref[HALO:HALO + tm, :] = x
    c = x * w[CONV_WIDTH - 1:CONV_WIDTH, :]
    for k in range(CONV_WIDTH - 1):
        s = CONV_WIDTH - 1 - k
        c = c + cat_ref[pl.ds(HALO - s, tm), :] * w[k:k + 1, :]
    return c


def _lane_to_all(x, lane):
    @jax.custom_vjp
    def f(x):
        return jnp.broadcast_to(x[:, lane:lane + 1], x.shape)

    def f_fwd(x):
        return f(x), None

    def f_bwd(_, g):
        return (jnp.where(_iota(g.shape, 1) == lane, jnp.sum(g, axis=-1, keepdims=True), 0.0),)

    f.defvjp(f_fwd, f_bwd)
    return f(x)


def _gdn_pointwise(c, ba, alog, dtb, H):
    A = H * HEAD_DIM
    s = _silu(c)
    beta = _sigmoid(ba)
    g = -jnp.exp(alog) * _softplus(ba + dtb)
    qs, ks, vs, gbs, bbs = [], [], [], [], []
    for h in range(H):
        lo = h * HEAD_DIM
        q = s[:, lo:lo + HEAD_DIM]
        k = s[:, A + lo:A + lo + HEAD_DIM]
        qs.append(q * lax.rsqrt(jnp.sum(q * q, axis=-1, keepdims=True) + EPS))
        ks.append(k * lax.rsqrt(jnp.sum(k * k, axis=-1, keepdims=True) + EPS))
        vs.append(s[:, 2 * A + lo:2 * A + lo + HEAD_DIM])
        bbs.append(_lane_to_all(beta, h))
        gbs.append(_lane_to_all(g, H + h))
    st = lambda xs: jnp.stack(xs, axis=0)
    return st(qs), st(ks), st(vs), st(gbs), st(bbs)


def _halo_prev(tm):
    return lambda i: (jnp.maximum(i * (tm // HALO) - 1, 0), 0)


def _gdn_pre(proj_m, proj_ba, conv_w, alog_row, dtb_row, H):
    T = proj_m.shape[0]
    A = H * HEAD_DIM
    tm = _pick(T, (256, 128))
    hs = pl.BlockSpec((H, tm, HEAD_DIM), lambda i: (0, i, 0))
    hshape = jax.ShapeDtypeStruct((H, T, HEAD_DIM), F32)

    def body(x_ref, halo_ref, ba_ref, w_ref, al_ref, dt_ref, q_ref, k_ref, v_ref, gb_ref, bb_ref, cat_ref):
        halo = jnp.where(pl.program_id(0) == 0, 0.0, halo_ref[...])
        c = _conv_fwd(cat_ref, halo, x_ref[...], w_ref[...])
        q, k, v, gb, bb = _gdn_pointwise(c, ba_ref[...], al_ref[...], dt_ref[...], H)
        q_ref[...] = q
        k_ref[...] = k
        v_ref[...] = v
        gb_ref[...] = gb
        bb_ref[...] = bb

    return pl.pallas_call(
        body, name="gdn_pre", grid=(T // tm,),
        in_specs=[pl.BlockSpec((tm, 3 * A), lambda i: (i, 0)),
                  pl.BlockSpec((HALO, 3 * A), _halo_prev(tm)),
                  pl.BlockSpec((tm, LANES), lambda i: (i, 0)),
                  pl.BlockSpec((CONV_WIDTH, 3 * A), lambda i: (0, 0)),
                  pl.BlockSpec((1, LANES), lambda i: (0, 0)),
                  pl.BlockSpec((1, LANES), lambda i: (0, 0))],
        out_specs=[hs] * 5, out_shape=[hshape] * 5,
        scratch_shapes=[pltpu.VMEM((HALO + tm, 3 * A), F32)],
        compiler_params=_cparams(("parallel",)),
    )(proj_m, proj_m, proj_ba, conv_w, alog_row, dtb_row)


def _gdn_pre_bwd(proj_m, proj_ba, conv_w, alog_row, dtb_row, dq, dk, dv, dgb, dbb, H, dproj):
    T, n_main = proj_m.shape
    A = H * HEAD_DIM
    tm = _pick(T, (256, 128))
    hs = pl.BlockSpec((H, tm, HEAD_DIM), lambda i: (0, i, 0))
    row = pl.BlockSpec((1, LANES), lambda i: (0, 0))

    def body(x_ref, halo_ref, ba_ref, w_ref, al_ref, dt_ref, dq_ref, dk_ref, dv_ref, dgb_ref, dbb_ref, _,
             dc_ref, dba_ref, dal_ref, ddt_ref, cat_ref):
        halo = jnp.where(pl.program_id(0) == 0, 0.0, halo_ref[...])
        c = _conv_fwd(cat_ref, halo, x_ref[...], w_ref[...])
        _, vjp = jax.vjp(functools.partial(_gdn_pointwise, H=H), c, ba_ref[...], al_ref[...], dt_ref[...])
        dc, dba, dal, ddt = vjp((dq_ref[...], dk_ref[...], dv_ref[...], dgb_ref[...], dbb_ref[...]))
        dc_ref[...] = dc
        dba_ref[:, :LANES] = dba.astype(BF16)
        dba_ref[:, LANES:] = jnp.zeros((tm, WIN_BLOCK - LANES), BF16)

        @pl.when(pl.program_id(0) == 0)
        def _():
            dal_ref[...] = dal
            ddt_ref[...] = ddt

        @pl.when(pl.program_id(0) > 0)
        def _():
            dal_ref[...] += dal
            ddt_ref[...] += ddt

    return pl.pallas_call(
        body, name="gdn_pre_bwd", grid=(T // tm,),
        in_specs=[pl.BlockSpec((tm, 3 * A), lambda i: (i, 0)),
                  pl.BlockSpec((HALO, 3 * A), _halo_prev(tm)),
                  pl.BlockSpec((tm, LANES), lambda i: (i, 0)),
                  pl.BlockSpec((CONV_WIDTH, 3 * A), lambda i: (0, 0)),
                  row, row, hs, hs, hs, hs, hs, ANY],
        out_specs=[pl.BlockSpec((tm, 3 * A), lambda i: (i, 0)),
                   pl.BlockSpec((tm, WIN_BLOCK), lambda i: (i, n_main // WIN_BLOCK)), row, row],
        out_shape=[jax.ShapeDtypeStruct((T, 3 * A), F32), jax.ShapeDtypeStruct(dproj.shape, dproj.dtype),
                   jax.ShapeDtypeStruct((1, LANES), F32), jax.ShapeDtypeStruct((1, LANES), F32)],
        input_output_aliases={11: 1},
        scratch_shapes=[pltpu.VMEM((HALO + tm, 3 * A), F32)],
        compiler_params=_cparams(("arbitrary",)),
    )(proj_m, proj_m, proj_ba, conv_w, alog_row, dtb_row, dq, dk, dv, dgb, dbb, dproj)


def _conv_bwd(proj_m, dc, conv_w, H, dproj):
    T = proj_m.shape[0]
    A = H * HEAD_DIM
    tm = _pick(T, (256, 128))
    nt = T // tm

    def body(x_ref, halo_ref, dc_ref, nxt_ref, w_ref, _, dx_ref, dw_ref):
        i = pl.program_id(0)
        halo = jnp.where(i == 0, 0.0, halo_ref[...])
        xcat = jnp.concatenate([halo, x_ref[...]], axis=0)
        nxt = jnp.where(i == nt - 1, 0.0, nxt_ref[...])
        dc = dc_ref[...]
        dcat = jnp.concatenate([dc, nxt], axis=0)
        w = w_ref[...]
        dx = None
        rows = []
        for k in range(CONV_WIDTH):
            s = CONV_WIDTH - 1 - k
            ds = dcat if s == 0 else pltpu.roll(dcat, tm + HALO - s, 0)
            term = ds[:tm, :] * w[k:k + 1, :]
            dx = term if dx is None else dx + term
            xs = xcat if s == 0 else pltpu.roll(xcat, s, 0)
            rows.append(jnp.sum(dc * xs[HALO:, :], axis=0, keepdims=True))
        dx_ref[...] = dx.astype(BF16)
        dw = jnp.concatenate(rows + [jnp.zeros((HALO - CONV_WIDTH, 3 * A), F32)], axis=0)

        @pl.when(i == 0)
        def _():
            dw_ref[...] = dw

        @pl.when(i > 0)
        def _():
            dw_ref[...] += dw

    return pl.pallas_call(
        body, name="conv_bwd", grid=(nt,),
        in_specs=[pl.BlockSpec((tm, 3 * A), lambda i: (i, 0)),
                  pl.BlockSpec((HALO, 3 * A), _halo_prev(tm)),
                  pl.BlockSpec((tm, 3 * A), lambda i: (i, 0)),
                  pl.BlockSpec((HALO, 3 * A), lambda i: (jnp.minimum((i + 1) * (tm // HALO), T // HALO - 1), 0)),
                  pl.BlockSpec((CONV_WIDTH, 3 * A), lambda i: (0, 0)), ANY],
        out_specs=[pl.BlockSpec((tm, 3 * A), lambda i: (i, 0)),
                   pl.BlockSpec((HALO, 3 * A), lambda i: (0, 0))],
        out_shape=[jax.ShapeDtypeStruct(dproj.shape, dproj.dtype), jax.ShapeDtypeStruct((HALO, 3 * A), F32)],
        input_output_aliases={5: 0},
        compiler_params=_cparams(("arbitrary",)),
    )(proj_m, proj_m, dc, dc, conv_w, dproj)


CHUNK = 128
BLOCK = 64


def _b(x):
    return x.astype(BF16)


@jax.custom_vjp
def _bdot(a, b):
    return _dot(_b(a), _b(b))


def _bdot_f(a, b):
    return _bdot(a, b), (a, b)


def _bdot_b(res, g):
    a, b = res
    return _dot_nt(_b(g), _b(b)), _dot_tn(_b(a), _b(g))


_bdot.defvjp(_bdot_f, _bdot_b)


@jax.custom_vjp
def _bdot_nt(a, b):
    return _dot_nt(_b(a), _b(b))


def _bdot_nt_f(a, b):
    return _bdot_nt(a, b), (a, b)


def _bdot_nt_b(res, g):
    a, b = res
    return _dot(_b(g), _b(b)), _dot_tn(_b(g), _b(a))


_bdot_nt.defvjp(_bdot_nt_f, _bdot_nt_b)


@jax.custom_vjp
def _bdot_tn(a, b):
    return _dot_tn(_b(a), _b(b))


def _bdot_tn_f(a, b):
    return _bdot_tn(a, b), (a, b)


def _bdot_tn_b(res, g):
    a, b = res
    return _dot_nt(_b(b), _b(g)), _dot(_b(a), _b(g))


_bdot_tn.defvjp(_bdot_tn_f, _bdot_tn_b)


def _mask_matmul(m, x):
    hi = _b(x)
    r = x - hi.astype(F32)
    mid = _b(r)
    lo = _b(r - mid.astype(F32))
    return (_dot(m, lo) + _dot(m, mid)) + _dot(m, hi)


@jax.custom_vjp
def _mask_dot(m, mt, x):
    return _mask_matmul(m, x)


def _mask_dot_f(m, mt, x):
    return _mask_matmul(m, x), (m, mt)


def _mask_dot_b(res, g):
    m, mt = res
    return jnp.zeros_like(m), jnp.zeros_like(mt), _mask_matmul(mt, g)


_mask_dot.defvjp(_mask_dot_f, _mask_dot_b)

HIGH = lax.Precision.HIGH


def _unit_lower_inverse(L):
    n = L.shape[-1]
    X = -L
    Q = X
    for _ in range(BLOCK.bit_length() - 2):
        X = _dot(_b(X), _b(X))
        Q = Q + X + _dot(_b(Q), _b(X))
    return (_iota((n, n), 0) == _iota((n, n), 1)).astype(F32) + Q


@jax.custom_vjp
def _known_inverse(L, P):
    return P


def _known_inverse_f(L, P):
    return P, P


def _known_inverse_b(P, g):
    n = P.shape[-1]
    Q = _b(P - (_iota((n, n), 0) == _iota((n, n), 1)).astype(F32))
    t = g + _dot_tn(Q, _b(g))
    return -(t + _dot_nt(_b(t), Q)), jnp.zeros_like(P)


_known_inverse.defvjp(_known_inverse_f, _known_inverse_b)


def _gdn_prep_fn(q, k, v, gb, bb, P_known=None):
    n = CHUNK
    row, col = _iota((n, n), 0), _iota((n, n), 1)
    same = (row // BLOCK) == (col // BLOCK)
    incl, strict = same & (row >= col), same & (row > col)
    bc = lambda m: jnp.broadcast_to(_b(m.astype(F32)), q.shape[:1] + (n, n))
    tril, triu, ones = bc(incl), bc(same & (row <= col)), bc(same)
    gc = _mask_dot(tril, triu, gb)
    gl = _mask_dot(ones, ones, gb)
    decay = jnp.where(incl, jnp.exp(jnp.where(incl, gc - jnp.swapaxes(gc, 1, 2), 0.0)), 0.0)
    kb = k * bb
    vb = v * bb
    qs = q * (HEAD_DIM ** -0.5)
    L = jnp.where(strict, _bdot_nt(kb, k) * decay, 0.0)
    P = _unit_lower_inverse(L) if P_known is None else _known_inverse(L, P_known)
    egc = jnp.exp(gc)
    u = _bdot(P, vb)
    w = _bdot(P, kb * egc)
    attn = jnp.where(incl, _bdot_nt(qs, k) * decay, 0.0)
    qg = qs * egc
    kdec = k * jnp.exp(gl - gc)
    eg = jnp.exp(gl).reshape(-1, n // BLOCK, BLOCK, LANES).sum(axis=2) * (1.0 / BLOCK)
    if P_known is None:
        return u, w, qg, kdec, attn, eg, P
    return u, w, qg, kdec, attn, eg


def _gdn_chain_fn(S, qg, kdec, u, w, attn, eg):
    nblk = CHUNK // BLOCK
    cat = lambda xs: jnp.concatenate(xs, axis=1)
    outs, found = [], []
    for i in range(nblk):
        r = (slice(None), slice(i * BLOCK, (i + 1) * BLOCK))
        v_new = u[r] - _bdot(w[r], S)
        found.append(v_new)
        outs.append(_bdot(qg[r], S) + _bdot(attn[r], cat(found + [jnp.zeros_like(v_new)] * (nblk - 1 - i))))
        S = S * eg[i] + _bdot_tn(kdec[r], v_new)
    return cat(outs), S


def _eg_spec(H, T, chunks, index_map, per_head):
    nblk = CHUNK // BLOCK
    block = (chunks, 1 if per_head else H, nblk, LANES)
    return pl.BlockSpec(block, index_map), jax.ShapeDtypeStruct((T // CHUNK, H, nblk, LANES), F32)


def _gdn_prep(q, k, v, gb, bb):
    H, T, _ = q.shape
    pb = _pick(T // CHUNK, (8, 4, 2, 1))
    hs = pl.BlockSpec((1, CHUNK * pb, HEAD_DIM), lambda h, n: (h, n, 0))
    hshape = jax.ShapeDtypeStruct((H, T, HEAD_DIM), F32)

    def body(q_ref, k_ref, v_ref, gb_ref, bb_ref, *out_refs):
        chunks = lambda ref: ref[0].reshape(pb, CHUNK, HEAD_DIM)
        outs = _gdn_prep_fn(chunks(q_ref), chunks(k_ref), chunks(v_ref), chunks(gb_ref), chunks(bb_ref))
        for i, (ref, val) in enumerate(zip(out_refs, outs)):
            if i == 5:
                ref[:, 0] = val
            else:
                ref[0] = val.reshape(pb * CHUNK, HEAD_DIM).astype(ref.dtype)

    kept = [F32, BF16, BF16, BF16, BF16, None, BF16]
    es, eshape = _eg_spec(H, T, pb, lambda h, n: (n, h, 0, 0), per_head=True)
    return pl.pallas_call(
        body, name="gdn_prep", grid=(H, T // (CHUNK * pb)),
        in_specs=[hs] * 5, out_specs=[es if dt is None else hs for dt in kept],
        out_shape=[eshape if dt is None else jax.ShapeDtypeStruct((H, T, HEAD_DIM), dt) for dt in kept],
        compiler_params=_cparams(("parallel", "parallel")),
    )(q, k, v, gb, bb)


def _gdn_prep_bwd(q, k, v, gb, bb, pinv, du, dw, dqg, dkd, dat, deg):
    H, T, _ = q.shape
    pb = _pick(T // CHUNK, (8, 4, 2, 1))
    hs = pl.BlockSpec((1, CHUNK * pb, HEAD_DIM), lambda h, n: (h, n, 0))
    hshape = jax.ShapeDtypeStruct((H, T, HEAD_DIM), F32)

    def body(*refs):
        in_refs, p_ref, ct_refs, out_refs = refs[:5], refs[5], refs[6:12], refs[12:]
        chunks = lambda ref: ref[0].reshape(pb, CHUNK, HEAD_DIM)
        P = chunks(p_ref).astype(F32)
        _, vjp = jax.vjp(lambda *a: _gdn_prep_fn(*a, P_known=P), *[chunks(r) for r in in_refs])
        grads = vjp(tuple(chunks(r).astype(F32) for r in ct_refs[:5]) + (ct_refs[5][:, 0],))
        for ref, val in zip(out_refs, grads):
            ref[0] = val.reshape(pb * CHUNK, HEAD_DIM)

    es, _ = _eg_spec(H, T, pb, lambda h, n: (n, h, 0, 0), per_head=True)
    return pl.pallas_call(
        body, name="gdn_prep_bwd", grid=(H, T // (CHUNK * pb)),
        in_specs=[hs] * 11 + [es], out_specs=[hs] * 5, out_shape=[hshape] * 5,
        compiler_params=_cparams(("parallel", "parallel")),
    )(q, k, v, gb, bb, pinv, du, dw, dqg, dkd, dat, deg)


def _gdn_chain(qg, kd, u, w, attn, eg):
    H, T, _ = qg.shape
    N = T // CHUNK
    hs = pl.BlockSpec((H, CHUNK, HEAD_DIM), lambda n: (0, n, 0))
    ss = pl.BlockSpec((1, H, HEAD_DIM, HEAD_DIM), lambda n: (n, 0, 0, 0))

    def body(qg_ref, kd_ref, u_ref, w_ref, at_ref, eg_ref, o_ref, sall_ref, s_ref):
        @pl.when(pl.program_id(0) == 0)
        def _():
            s_ref[...] = jnp.zeros_like(s_ref)

        S = s_ref[...]
        sall_ref[0] = S
        eg = tuple(eg_ref[0, :, i:i + 1, :] for i in range(CHUNK // BLOCK))
        o, S2 = _gdn_chain_fn(S, qg_ref[...], kd_ref[...], u_ref[...], w_ref[...], at_ref[...], eg)
        o_ref[...] = o
        s_ref[...] = S2

    es, _ = _eg_spec(H, T, 1, lambda n: (n, 0, 0, 0), per_head=False)
    return pl.pallas_call(
        body, name="gdn_chain", grid=(N,),
        in_specs=[hs] * 5 + [es], out_specs=[hs, ss],
        out_shape=[jax.ShapeDtypeStruct((H, T, HEAD_DIM), F32),
                   jax.ShapeDtypeStruct((N, H, HEAD_DIM, HEAD_DIM), F32)],
        scratch_shapes=[pltpu.VMEM((H, HEAD_DIM, HEAD_DIM), F32)],
        compiler_params=_cparams(("arbitrary",)),
    )(qg, kd, u, w, attn, eg)


def _gdn_chain_bwd(qg, kd, u, w, attn, eg, sall, do):
    H, T, _ = qg.shape
    N = T // CHUNK
    hs = pl.BlockSpec((H, CHUNK, HEAD_DIM), lambda n: (0, N - 1 - n, 0))
    ss = pl.BlockSpec((1, H, HEAD_DIM, HEAD_DIM), lambda n: (N - 1 - n, 0, 0, 0))
    hshape = jax.ShapeDtypeStruct((H, T, HEAD_DIM), F32)

    def body(qg_ref, kd_ref, u_ref, w_ref, at_ref, eg_ref, sall_ref, do_ref, *rest):
        out_refs, ds_ref = rest[:6], rest[6]

        @pl.when(pl.program_id(0) == 0)
        def _():
            ds_ref[...] = jnp.zeros_like(ds_ref)

        f32 = lambda ref: ref[...].astype(F32)
        nblk = CHUNK // BLOCK
        eg = tuple(eg_ref[0, :, i:i + 1, :] for i in range(nblk))
        _, vjp = jax.vjp(_gdn_chain_fn, sall_ref[0], f32(qg_ref), f32(kd_ref), u_ref[...], f32(w_ref),
                         f32(at_ref), eg)
        grads = vjp((do_ref[...], ds_ref[...]))
        ds_ref[...] = grads[0]
        for ref, val in zip(out_refs[:5], grads[1:6]):
            ref[...] = val.astype(ref.dtype)
        for i in range(nblk):
            out_refs[5][0, :, i:i + 1, :] = grads[6][i]

    kept = [F32, F32, BF16, BF16, F32]
    es, eshape = _eg_spec(H, T, 1, lambda n: (N - 1 - n, 0, 0, 0), per_head=False)
    return pl.pallas_call(
        body, name="gdn_chain_bwd", grid=(N,),
        in_specs=[hs] * 5 + [es, ss, hs], out_specs=[hs] * 5 + [es],
        out_shape=[jax.ShapeDtypeStruct((H, T, HEAD_DIM), dt) for dt in kept] + [eshape],
        scratch_shapes=[pltpu.VMEM((H, HEAD_DIM, HEAD_DIM), F32)],
        compiler_params=_cparams(("arbitrary",)),
    )(qg, kd, u, w, attn, eg, sall, do)


def _post_fn(ogs, za, hw):
    outs = []
    for h, o in enumerate(ogs):
        r = lax.rsqrt(jnp.mean(o * o, axis=-1, keepdims=True) + EPS)
        outs.append(o * r * hw * _silu(za[:, h * HEAD_DIM:(h + 1) * HEAD_DIM]))
    return jnp.concatenate(outs, axis=1)


def _gdn_post(og, proj_m, hw):
    H, T, _ = og.shape
    A = H * HEAD_DIM
    tm = _pick(T, (512, 256, 128))

    def body(og_ref, za_ref, hw_ref, o_ref, ot_ref):
        o = _post_fn(tuple(og_ref[h] for h in range(H)), za_ref[...], hw_ref[...])
        o_ref[...] = o.astype(BF16)
        ot_ref[...] = o.T.astype(BF16)

    return pl.pallas_call(
        body, name="gdn_post", grid=(T // tm,),
        in_specs=[pl.BlockSpec((H, tm, HEAD_DIM), lambda i: (0, i, 0)),
                  pl.BlockSpec((tm, A), lambda i: (i, ZA_BLOCK)),
                  pl.BlockSpec((1, HEAD_DIM), lambda i: (0, 0))],
        out_specs=[pl.BlockSpec((tm, A), lambda i: (i, 0)), pl.BlockSpec((A, tm), lambda i: (0, i))],
        out_shape=[jax.ShapeDtypeStruct((T, A), BF16), jax.ShapeDtypeStruct((A, T), BF16)],
        compiler_params=_cparams(("parallel",)),
    )(og, proj_m, hw)


def _gdn_post_bwd(og, proj_m, hw, d_o, dproj):
    H, T, _ = og.shape
    A = H * HEAD_DIM
    tm = _pick(T, (256, 128))

    def body(og_ref, za_ref, hw_ref, do_ref, _, dog_ref, dza_ref, dhw_ref):
        _, vjp = jax.vjp(_post_fn, tuple(og_ref[h] for h in range(H)), za_ref[...], hw_ref[...])
        dog, dza, dhw = vjp(do_ref[...])
        for h in range(H):
            dog_ref[h] = dog[h]
        dza_ref[...] = dza.astype(BF16)

        @pl.when(pl.program_id(0) == 0)
        def _():
            dhw_ref[...] = dhw

        @pl.when(pl.program_id(0) > 0)
        def _():
            dhw_ref[...] += dhw

    return pl.pallas_call(
        body, name="gdn_post_bwd", grid=(T // tm,),
        in_specs=[pl.BlockSpec((H, tm, HEAD_DIM), lambda i: (0, i, 0)),
                  pl.BlockSpec((tm, A), lambda i: (i, ZA_BLOCK)),
                  pl.BlockSpec((1, HEAD_DIM), lambda i: (0, 0)),
                  pl.BlockSpec((tm, A), lambda i: (i, 0)), ANY],
        out_specs=[pl.BlockSpec((H, tm, HEAD_DIM), lambda i: (0, i, 0)),
                   pl.BlockSpec((tm, A), lambda i: (i, ZA_BLOCK)),
                   pl.BlockSpec((1, HEAD_DIM), lambda i: (0, 0))],
        out_shape=[jax.ShapeDtypeStruct((H, T, HEAD_DIM), F32), jax.ShapeDtypeStruct(dproj.shape, dproj.dtype),
                   jax.ShapeDtypeStruct((1, HEAD_DIM), F32)],
        input_output_aliases={4: 1},
        compiler_params=_cparams(("arbitrary",)),
    )(og, proj_m, hw, d_o, dproj)


def _sgu_fn(ub, vb, zb, lw, lb, W, bbc):
    G = len(W)
    tm = ub.shape[0]
    mu = jnp.mean(vb, axis=-1, keepdims=True)
    xc = vb - mu
    var = jnp.mean(xc * xc, axis=-1, keepdims=True)
    vn = xc * lax.rsqrt(var + EPS) * lw + lb
    mask = _iota((CHUNK_B, CHUNK_B), 0) >= _iota((CHUNK_B, CHUNK_B), 1)
    cols = []
    for g in range(G):
        wm = jnp.where(mask, W[g], 0.0).astype(BF16)
        rows = []
        for c in range(tm // CHUNK_B):
            blk = vn[c * CHUNK_B:(c + 1) * CHUNK_B, g * HEAD_DIM:(g + 1) * HEAD_DIM].astype(BF16)
            rows.append(_dot(wm, blk) + bbc[g])
        cols.append(jnp.concatenate(rows, axis=0) if len(rows) > 1 else rows[0])
    s = jnp.concatenate(cols, axis=1)
    return ub * s * _silu(zb)


ZA_BLOCK = 6


def _sgu_cols(A, B):
    assert A == B
    return 3, 4, 5


def _sgu_fwd(proj_m, lw, lb, W, bbc, A):
    T = proj_m.shape[0]
    G = W.shape[0]
    B = G * HEAD_DIM
    tm = _pick(T, (256, 128))
    cu, cv, cz = _sgu_cols(A, B)

    def body(u_ref, v_ref, z_ref, lw_ref, lb_ref, w_ref, b_ref, o_ref, ot_ref):
        o = _sgu_fn(u_ref[...], v_ref[...], z_ref[...], lw_ref[...], lb_ref[...],
                    tuple(w_ref[g] for g in range(G)), tuple(b_ref[g] for g in range(G)))
        o_ref[...] = o.astype(BF16)
        ot_ref[...] = o.T.astype(BF16)

    row = pl.BlockSpec((1, B), lambda i: (0, 0))
    cube = pl.BlockSpec((G, CHUNK_B, CHUNK_B), lambda i: (0, 0, 0))
    return pl.pallas_call(
        body, name="sgu_fwd", grid=(T // tm,),
        in_specs=[pl.BlockSpec((tm, B), lambda i: (i, cu)), pl.BlockSpec((tm, B), lambda i: (i, cv)),
                  pl.BlockSpec((tm, B), lambda i: (i, cz)), row, row, cube, cube],
        out_specs=[pl.BlockSpec((tm, B), lambda i: (i, 0)), pl.BlockSpec((B, tm), lambda i: (0, i))],
        out_shape=[jax.ShapeDtypeStruct((T, B), BF16), jax.ShapeDtypeStruct((B, T), BF16)],
        compiler_params=_cparams(("parallel",)),
    )(proj_m, proj_m, proj_m, lw, lb, W, bbc)


def _sgu_bwd(proj_m, lw, lb, W, bbc, d_o, A, dproj):
    T = proj_m.shape[0]
    G = W.shape[0]
    B = G * HEAD_DIM
    tm = _pick(T, (256, 128))
    nt = T // tm
    cu, cv, cz = _sgu_cols(A, B)

    def body(u_ref, v_ref, z_ref, lw_ref, lb_ref, w_ref, b_ref, do_ref, _,
             dp_ref, dlw_ref, dlb_ref, dw_ref, db_ref, dbb_ref):
        _, vjp = jax.vjp(_sgu_fn, u_ref[...], v_ref[...], z_ref[...], lw_ref[...], lb_ref[...],
                         tuple(w_ref[g] for g in range(G)), tuple(b_ref[g] for g in range(G)))
        du, dv, dz, dlw, dlb, dW, dbb = vjp(do_ref[...])
        dW, dbb = jnp.stack(dW, axis=0), jnp.stack(dbb, axis=0)
        dp_ref[:, 0:B] = du.astype(BF16)
        dp_ref[:, B:2 * B] = dv.astype(BF16)
        dp_ref[:, 2 * B:3 * B] = dz.astype(BF16)
        i = pl.program_id(0)

        @pl.when(i == 0)
        def _():
            dlw_ref[...] = dlw
            dlb_ref[...] = dlb
            dw_ref[...] = dW
            dbb_ref[...] = dbb

        @pl.when(i > 0)
        def _():
            dlw_ref[...] += dlw
            dlb_ref[...] += dlb
            dw_ref[...] += dW
            dbb_ref[...] += dbb

        @pl.when(i == nt - 1)
        def _():
            db_ref[...] = jnp.sum(dbb_ref[...], axis=-1, keepdims=True)

    row = pl.BlockSpec((1, B), lambda i: (0, 0))
    cube = pl.BlockSpec((G, CHUNK_B, CHUNK_B), lambda i: (0, 0, 0))
    return pl.pallas_call(
        body, name="sgu_bwd", grid=(nt,),
        in_specs=[pl.BlockSpec((tm, B), lambda i: (i, cu)), pl.BlockSpec((tm, B), lambda i: (i, cv)),
                  pl.BlockSpec((tm, B), lambda i: (i, cz)), row, row, cube, cube,
                  pl.BlockSpec((tm, B), lambda i: (i, A // B)), ANY],
        out_specs=[pl.BlockSpec((tm, 3 * B), lambda i: (i, 1)), row, row, cube,
                   pl.BlockSpec((G, CHUNK_B, 1), lambda i: (0, 0, 0))],
        out_shape=[jax.ShapeDtypeStruct(dproj.shape, dproj.dtype), jax.ShapeDtypeStruct((1, B), F32),
                   jax.ShapeDtypeStruct((1, B), F32), jax.ShapeDtypeStruct((G, CHUNK_B, CHUNK_B), F32),
                   jax.ShapeDtypeStruct((G, CHUNK_B, 1), F32)],
        input_output_aliases={8: 0},
        scratch_shapes=[pltpu.VMEM((G, CHUNK_B, CHUNK_B), F32)],
        compiler_params=_cparams(("arbitrary",)),
    )(proj_m, proj_m, proj_m, lw, lb, W, bbc, d_o, dproj)


def _head_fn(mix, x, fw, tgt):
    h = x + mix
    y = _rms_fn(h, fw)
    e = y - tgt
    return 0.5 * jnp.sum(jnp.mean(e * e, axis=-1, keepdims=True), axis=0, keepdims=True)


def _out_proj_loss(oa, ob, wout, x, tgt, fw):
    T, A = oa.shape
    B = ob.shape[1]
    D = x.shape[1]
    tm = _pick(T, (256, 128))

    def body(oa_ref, ob_ref, w_ref, x_ref, t_ref, fw_ref, dh_ref, dhb_ref, loss_ref, dfw_ref):
        mix = _dot(oa_ref[...], w_ref[0:A, :]) + _dot(ob_ref[...], w_ref[A:A + B, :])
        xv, tv = x_ref[...], t_ref[...]
        loss, vjp = jax.vjp(lambda m, f: _head_fn(m, xv, f, tv), mix, fw_ref[...])
        dh, dfw = vjp(jnp.ones((1, 1), F32))
        dh_ref[...] = dh
        dhb_ref[...] = dh.astype(BF16)
        lrow = jnp.broadcast_to(loss, (1, LANES))

        @pl.when(pl.program_id(0) == 0)
        def _():
            loss_ref[...] = lrow
            dfw_ref[...] = dfw

        @pl.when(pl.program_id(0) > 0)
        def _():
            loss_ref[...] += lrow
            dfw_ref[...] += dfw

    tile = pl.BlockSpec((tm, D), lambda i: (i, 0))
    return pl.pallas_call(
        body, name="out_proj_loss", grid=(T // tm,),
        in_specs=[pl.BlockSpec((tm, A), lambda i: (i, 0)), pl.BlockSpec((tm, B), lambda i: (i, 0)),
                  pl.BlockSpec((A + B, D), lambda i: (0, 0)), tile, tile,
                  pl.BlockSpec((1, D), lambda i: (0, 0))],
        out_specs=[tile, tile, pl.BlockSpec((1, LANES), lambda i: (0, 0)),
                   pl.BlockSpec((1, D), lambda i: (0, 0))],
        out_shape=[jax.ShapeDtypeStruct((T, D), F32), jax.ShapeDtypeStruct((T, D), BF16),
                   jax.ShapeDtypeStruct((1, LANES), F32), jax.ShapeDtypeStruct((1, D), F32)],
        compiler_params=_cparams(("arbitrary",)),
    )(oa, ob, wout, x, tgt, fw)


def _adamw(w, g, m, v, name):
    R, Cn = w.shape
    cap = max(8, 512 * 1024 // Cn)
    tr = max(t for t in range(8, min(R, cap) + 1, 8) if R % t == 0) if R > cap else R

    def body(w_ref, g_ref, m_ref, v_ref, d_ref, mo_ref, vo_ref):
        g = g_ref[...]
        m = ADAM_B1 * m_ref[...] + (1.0 - ADAM_B1) * g
        v = ADAM_B2 * v_ref[...] + (1.0 - ADAM_B2) * jnp.square(g)
        m_hat = m / (1.0 - ADAM_B1 ** ADAM_STEP)
        v_hat = v / (1.0 - ADAM_B2 ** ADAM_STEP)
        d_ref[...] = -ADAM_LR * (m_hat / (jnp.sqrt(v_hat) + ADAM_EPS) + ADAM_WD * w_ref[...])
        mo_ref[...] = m
        vo_ref[...] = v

    tile = pl.BlockSpec((tr, Cn), lambda i: (i, 0))
    shape = jax.ShapeDtypeStruct((R, Cn), F32)
    return pl.pallas_call(
        body, name=name, grid=(R // tr,), in_specs=[tile] * 4, out_specs=[tile] * 3,
        out_shape=[shape] * 3, compiler_params=_cparams(("parallel",)),
    )(w, g, m, v)


def _place():
    x, y, c = lax.axis_index("x"), lax.axis_index("y"), lax.axis_index("c")
    others = [(1 - x, y), (x, 1 - y), (1 - x, 1 - y)]
    return x, y, c, others


def _chip_index(px, py):
    return 2 * px + py


ANY = pl.BlockSpec(memory_space=pl.ANY)


def _gather_ride(blocks, split):
    n = len(blocks)

    def plan(in_refs, out_refs, send_sems, recv_sems):
        x, y, c, _ = _place()
        me, kx, ky, kd = (_chip_index(px, py) for px, py in ((x, y), (1 - x, y), (x, 1 - y), (1 - x, 1 - y)))
        to_x, to_y, to_s = (1 - x, y, c), (x, 1 - y, c), (x, y, 1 - c)

        def copy(sem, src, dst, to):
            return pltpu.make_async_remote_copy(src_ref=src, dst_ref=dst, send_sem=send_sems.at[sem],
                                                recv_sem=recv_sems.at[sem], device_id=to, device_id_type=MESH_ID)

        first, second, third, awaited = [], [], [], []
        for a in range(n):
            out, s0 = out_refs[a], 8 * a
            if not split[a]:
                for j, (k, to) in enumerate(((kx, to_x), (ky, to_y), (kd, (1 - x, 1 - y, c)))):
                    first.append(lambda j=j, to=to, a=a, out=out, s0=s0: copy(s0 + j, in_refs[a], out.at[me], to))
                    awaited.append((lambda j=j, k=k, to=to, out=out, s0=s0: copy(s0 + j, out.at[k], out.at[k], to),
                                    None))
                continue
            h = blocks[a].shape[0] // 2
            q = h // 2
            half = lambda k, core, out=out, h=h: out.at[k, pl.ds(core * h, h), :]
            quarter = lambda k, core, i, out=out, h=h, q=q: out.at[k, pl.ds(core * h + i * q, q), :]
            mine = in_refs[a].at[pl.ds(c * h, h), :]
            first.append(lambda s0=s0, mine=mine, half=half: copy(s0, mine, half(me, c), to_x))
            first.append(lambda s0=s0, mine=mine, half=half: copy(s0 + 1, mine, half(me, c), to_y))
            fwd0 = lambda s0=s0, quarter=quarter: copy(s0 + 2, quarter(kx, c, 0), quarter(kx, c, 0), to_y)
            fwd1 = lambda s0=s0, quarter=quarter: copy(s0 + 3, quarter(ky, c, 1), quarter(ky, c, 1), to_x)
            pieces = [(s0 + 0, lambda half=half: half(kx, c), lambda half=half: half(kx, 1 - c), to_x, fwd0),
                      (s0 + 1, lambda half=half: half(ky, c), lambda half=half: half(ky, 1 - c), to_y, fwd1),
                      (s0 + 2, lambda quarter=quarter: quarter(kd, c, 0), lambda quarter=quarter: quarter(kd, 1 - c, 0),
                       to_y, None),
                      (s0 + 3, lambda quarter=quarter: quarter(kd, c, 1), lambda quarter=quarter: quarter(kd, 1 - c, 1),
                       to_x, None)]
            for i, (sem, here, there, frm, fwd) in enumerate(pieces):
                passing = lambda s0=s0, i=i, here=here: copy(s0 + 4 + i, here(), here(), to_s)
                awaited.append((lambda sem=sem, here=here, frm=frm: copy(sem, here(), here(), frm), (fwd, passing)))
                if fwd is not None:
                    second.append(fwd)
                third.append((passing, lambda s0=s0, i=i, there=there: copy(s0 + 4 + i, there(), there(), to_s)))
        return first, second, third, awaited

    def start(*refs):
        for send in plan(*refs)[0]:
            send().start()

    def finish(*refs):
        first, second, third, awaited = plan(*refs)
        for arrival, then in awaited:
            arrival().wait_recv()
            for nxt in (then or ()):
                if nxt is not None:
                    nxt().start()
        for _, from_sibling in third:
            from_sibling().wait_recv()
        for send in first + second + [p for p, _ in third]:
            send().wait_send()

    shapes = [jax.ShapeDtypeStruct((N_CHIPS,) + b.shape, b.dtype) for b in blocks]
    return _Ride(blocks, shapes, 8 * n, start, finish)


def _put_own(gathered, own):
    me = _chip_index(lax.axis_index("x"), lax.axis_index("y"))
    return lax.dynamic_update_index_in_dim(gathered, own, me, 0)


def _allreduce_small(buf):
    R, L = buf.shape

    def body(in_ref, out_ref, sib_ref, pair_ref, chips_ref, send_sems, recv_sems):
        x, y, c, others = _place()
        me = _chip_index(x, y)
        sibling = (x, y, 1 - c)
        cp = pltpu.make_async_remote_copy(src_ref=in_ref, dst_ref=sib_ref, send_sem=send_sems.at[0],
                                          recv_sem=recv_sems.at[0], device_id=sibling, device_id_type=MESH_ID)
        cp.start()
        cp.wait()
        pair_ref[...] = in_ref[...] + sib_ref[...]
        sends = []
        for j, chip in enumerate(others):
            s = pltpu.make_async_remote_copy(src_ref=pair_ref, dst_ref=chips_ref.at[me],
                                             send_sem=send_sems.at[1 + j], recv_sem=recv_sems.at[1 + j],
                                             device_id=(*chip, c), device_id_type=MESH_ID)
            s.start()
            sends.append(s)
        chips_ref[me] = pair_ref[...]
        for j, chip in enumerate(others):
            k = _chip_index(*chip)
            pltpu.make_async_remote_copy(src_ref=pair_ref, dst_ref=chips_ref.at[k], send_sem=send_sems.at[1 + j],
                                         recv_sem=recv_sems.at[1 + j], device_id=(*chip, c),
                                         device_id_type=MESH_ID).wait_recv()
        for s in sends:
            s.wait_send()
        out_ref[...] = ((chips_ref[0] + chips_ref[1]) + chips_ref[2]) + chips_ref[3]

    vm = pl.BlockSpec(memory_space=pltpu.VMEM)
    return pl.pallas_call(
        body, name="allreduce_small", in_specs=[vm], out_specs=vm,
        out_shape=jax.ShapeDtypeStruct((R, L), F32),
        scratch_shapes=[pltpu.VMEM((R, L), F32), pltpu.VMEM((R, L), F32), pltpu.VMEM((N_CHIPS, R, L), F32),
                        pltpu.SemaphoreType.DMA((4,)), pltpu.SemaphoreType.DMA((4,))],
        compiler_params=pltpu.CompilerParams(vmem_limit_bytes=VMEM_LIMIT),
    )(buf)


def _pair_ride(g):
    nb, R, Cn = g.shape
    h = R // 2

    def copy(in_refs, out_refs, send_sems, recv_sems):
        x, y, c, _ = _place()
        return pltpu.make_async_remote_copy(src_ref=in_refs[0].at[:, pl.ds((1 - c) * h, h), :], dst_ref=out_refs[0],
                                            send_sem=send_sems.at[0], recv_sem=recv_sems.at[0],
                                            device_id=(x, y, 1 - c), device_id_type=MESH_ID)

    return _Ride([g], [jax.ShapeDtypeStruct((nb, h, Cn), g.dtype)], 1,
                 lambda *refs: copy(*refs).start(), lambda *refs: copy(*refs).wait())


def _pair_sum(g, land, c_arr, name, ride=None):
    nb, R, Cn = g.shape
    hr = R // 2
    tr = _pick(hr, (256, 128, 64, 32, 16))
    nt = hr // tr

    def body(c_ref, g_ref, l_ref, o_ref):
        o_ref[...] = (g_ref[...] + l_ref[...]).astype(BF16)

    return _pallas(
        body, (c_arr, g, land), name=name, prefetch=1, grid=(nb, nt),
        in_specs=[pl.BlockSpec((1, tr, Cn), lambda b, i, c_ref: (b, c_ref[0] * nt + i, 0)),
                  pl.BlockSpec((1, tr, Cn), lambda b, i, c_ref: (b, i, 0))],
        out_specs=pl.BlockSpec((1, tr, Cn), lambda b, i, c_ref: (b, i, 0)),
        out_shape=jax.ShapeDtypeStruct((nb, hr, Cn), BF16),
        semantics=("parallel", "parallel"), ride=ride)


def _chip_ride(parts):
    m = len(parts)

    def copies(in_refs, out_refs, send_sems, recv_sems):
        x, y, c, others = _place()
        me = _chip_index(x, y)
        def mk(j, chip, n, landing):
            k = _chip_index(*chip)
            return pltpu.make_async_remote_copy(
                src_ref=in_refs[n].at[k], dst_ref=out_refs[n].at[landing(k)], send_sem=send_sems.at[m * j + n],
                recv_sem=recv_sems.at[m * j + n], device_id=(*chip, c), device_id_type=MESH_ID)

        pairs = [(j, chip, n) for j, chip in enumerate(others) for n in range(m)]
        return pairs, (lambda *p: mk(*p, lambda k: me)), (lambda *p: mk(*p, lambda k: k))

    def start(*refs):
        pairs, send, _ = copies(*refs)
        for p in pairs:
            send(*p).start()

    def finish(*refs):
        pairs, send, arrival = copies(*refs)
        for p in pairs:
            arrival(*p).wait_recv()
        for p in pairs:
            send(*p).wait_send()

    return _Ride(parts, [jax.ShapeDtypeStruct(p.shape, p.dtype) for p in parts], 3 * m, start, finish)


def _put_own_slot(q, p):
    me = _chip_index(lax.axis_index("x"), lax.axis_index("y"))
    return lax.dynamic_update_index_in_dim(q, lax.dynamic_index_in_dim(p, me, 0, keepdims=False), me, 0)


def _chip_sum(q, c_arr, name):
    nb, hr, Cn = q.shape
    tr = _pick(hr, (256, 128, 64, 32, 16))
    nt = hr // tr

    def body(c_ref, q_ref, o_ref):
        f = lambda k: q_ref[k].astype(F32)
        o_ref[...] = ((f(0) + f(1)) + f(2)) + f(3)

    return _pallas(
        body, (c_arr, q), name=name, prefetch=1, grid=(nt,),
        in_specs=[pl.BlockSpec((nb, tr, Cn), lambda i, c_ref: (0, i, 0))],
        out_specs=pl.BlockSpec((tr, Cn), lambda i, c_ref: (c_ref[0] * nt + i, 0)),
        out_shape=jax.ShapeDtypeStruct((2 * hr, Cn), F32),
        semantics=("parallel",))


def _sibling_fill(fw, fo):
    def body(_, __, fw_ref, fo_ref, send_sems, recv_sems):
        x, y, c, _ = _place()
        copies = []
        for n, ref in enumerate((fw_ref, fo_ref)):
            h = ref.shape[0] // 2
            mine = ref.at[pl.ds(c * h, h), :]
            theirs = ref.at[pl.ds((1 - c) * h, h), :]
            mk = lambda src, dst: pltpu.make_async_remote_copy(
                src_ref=src, dst_ref=dst, send_sem=send_sems.at[n], recv_sem=recv_sems.at[n],
                device_id=(x, y, 1 - c), device_id_type=MESH_ID)
            send = mk(mine, mine)
            send.start()
            copies.append((send, mk(theirs, theirs)))
        for send, arrival in copies:
            arrival.wait_recv()
            send.wait_send()

    return pl.pallas_call(
        body, name="sibling_fill", in_specs=[ANY, ANY], out_specs=[ANY, ANY],
        out_shape=[jax.ShapeDtypeStruct(fw.shape, F32), jax.ShapeDtypeStruct(fo.shape, F32)],
        input_output_aliases={0: 0, 1: 1},
        scratch_shapes=[pltpu.SemaphoreType.DMA((2,)), pltpu.SemaphoreType.DMA((2,))],
        compiler_params=pltpu.CompilerParams(has_side_effects=True),
    )(fw, fo)


class _Layout:
    def __init__(self, H, G, nb, Cb):
        A, B = H * HEAD_DIM, G * HEAD_DIM
        self.n_main = 4 * A + 3 * B
        self.k = -(-(self.n_main + LANES) // WIN_BLOCK) * WIN_BLOCK
        cuts = [0, 3 * A, 4 * A, 4 * A + 2 * H, nb * Cb]
        starts = [0, 3 * A + 3 * B, self.n_main, 3 * A]
        self.pieces = []
        self.windows, self.runs = [], []
        for n in range(nb):
            segs = []
            for s in range(4):
                lo, hi = max(cuts[s], n * Cb), min(cuts[s + 1], (n + 1) * Cb)
                if lo < hi:
                    segs.append((starts[s] + lo - cuts[s], lo - n * Cb, hi - lo))
            self.pieces += [(own, n, col, ln) for own, col, ln in segs]
            blocks = sorted({b for own, _, ln in segs for b in range(own // WIN_BLOCK, (own + ln - 1) // WIN_BLOCK + 1)})
            self.windows.append(blocks)
            self.runs.append([(blocks.index(own // WIN_BLOCK) * WIN_BLOCK + own % WIN_BLOCK, ln)
                              for own, _, ln in segs])
        self.wb = max(len(b) for b in self.windows)
        self.table = [b + [b[-1]] * (self.wb - len(b)) for b in self.windows]
        self.pieces.sort()

    def to_own_order(self, g_in):
        D = g_in.shape[1]
        cols, at = [], 0
        for own, n, col, ln in self.pieces:
            if own > at:
                cols.append(jnp.zeros((D, own - at), g_in.dtype))
            cols.append(g_in[n, :, col:col + ln])
            at = own + ln
        if at < self.k:
            cols.append(jnp.zeros((D, self.k - at), g_in.dtype))
        return jnp.concatenate(cols, axis=1)

    def from_window(self, win, chip, Cb):
        pick = lambda runs: (lambda w: jnp.concatenate([w[:, c:c + ln] for c, ln in runs], axis=1))
        return lax.switch(chip, [pick(r) for r in self.runs], win)


def _device_step(x, tgt, norm_w, win_b, wout_b, conv_b, a_log, dt_bias, head_norm_w, sgu_ln_w, sgu_ln_b,
                 w_spatial, b_spatial, final_norm_w, c_arr):
    T, D = x.shape
    H = a_log.shape[1]
    A = H * HEAD_DIM
    G = w_spatial.shape[0]
    B = G * HEAD_DIM
    nb, Cb, Rb = N_CHIPS, win_b.shape[1], wout_b.shape[0]
    lay = _Layout(H, G, nb, Cb)
    alog_row = jnp.pad(a_log, ((0, 0), (H, LANES - 2 * H)))
    dtb_row = jnp.pad(dt_bias, ((0, 0), (H, LANES - 2 * H)))
    bbc = jnp.broadcast_to(b_spatial[:, :, None], (G, CHUNK_B, CHUNK_B))

    (xn, xn_t), (g_in,) = _rms_in(x, norm_w, ride=_gather_ride([win_b], [True]))
    w_own = lay.to_own_order(_put_own(g_in, win_b))
    proj_m, (g_out, g_conv) = _mm_nn(xn, w_own, F32, "in_proj", cols=(0, lay.n_main),
                                     ride=_gather_ride([wout_b, conv_b], [True, False]))
    wout = _put_own(g_out, wout_b).reshape(nb * Rb, D)
    conv_w = _put_own(g_conv, conv_b).transpose(1, 0, 2).reshape(CONV_WIDTH, nb * conv_b.shape[1])
    proj_ba = _mm_nn(xn, w_own, F32, "in_proj_ba", cols=(lay.n_main, LANES))
    q, k, v, gb, bb = _gdn_pre(proj_m, proj_ba, conv_w, alog_row, dtb_row, H)
    u, w, qg, kd, attn, eg, pinv = _gdn_prep(q, k, v, gb, bb)
    og, sall = _gdn_chain(qg, kd, u, w, attn, eg)
    oa, oa_t = _gdn_post(og, proj_m, head_norm_w)
    ob, ob_t = _sgu_fwd(proj_m, sgu_ln_w, sgu_ln_b, w_spatial, bbc, A)
    dh, dhb, loss_row, d_fnw = _out_proj_loss(oa, ob, wout, x, tgt, final_norm_w.reshape(1, D))

    d_o = _mm_nn(dhb, wout.T, F32, "out_proj_dx")
    dproj = lax.empty((T, lay.k), BF16)
    dproj, d_lw, d_lb, d_ws, d_bs = _sgu_bwd(proj_m, sgu_ln_w, sgu_ln_b, w_spatial, bbc, d_o, A, dproj)
    dog, dproj, d_hw = _gdn_post_bwd(og, proj_m, head_norm_w, d_o, dproj)
    dqg, dkd, du, dw, dat, deg = _gdn_chain_bwd(qg, kd, u, w, attn, eg, sall, dog)
    dq, dk, dv, dgb, dbb = _gdn_prep_bwd(q, k, v, gb, bb, pinv, du, dw, dqg, dkd, dat, deg)
    dc, dproj, d_al, d_dt = _gdn_pre_bwd(proj_m, proj_ba, conv_w, alog_row, dtb_row, dq, dk, dv, dgb, dbb, H,
                                         dproj)
    dproj, d_conv = _conv_bwd(proj_m, dc, conv_w, H, dproj)

    table = jnp.array([b for row in lay.table for b in row], jnp.int32)
    d_win = _mm_windows(xn_t, dproj, table, nb, "in_proj_dw")
    d_wout, (land_w,) = _mm_nn_pair(oa_t, ob_t, dhb, "out_proj_dw", ride=_pair_ride(d_win))
    d_wout = d_wout.reshape(nb, Rb, D)
    pair_w, (land_o,) = _pair_sum(d_win, land_w, c_arr, "pair_sum_w_in", ride=_pair_ride(d_wout))
    pair_o = _pair_sum(d_wout, land_o, c_arr, "pair_sum_w_out")
    dxn, (all_w,) = _mm_nt_rhs_outer(dproj, w_own, F32, "in_proj_dx", ride=_chip_ride([pair_w]))
    (grad_x, d_nw), (all_o,) = _rms_in_bwd(x, norm_w, dxn, dh, ride=_chip_ride([pair_o]))
    all_w, all_o = _put_own_slot(all_w, pair_w), _put_own_slot(all_o, pair_o)
    small = dict(norm_w=d_nw, conv_w=d_conv[:CONV_WIDTH], a_log=d_al[:, H:2 * H], dt_bias=d_dt[:, H:2 * H],
                 head_norm_w=d_hw, sgu_ln_w=d_lw, sgu_ln_b=d_lb, w_spatial=d_ws, b_spatial=d_bs[:, :, 0],
                 final_norm_w=d_fnw)
    return loss_row, grad_x, small, all_w, all_o


SMALL = ("norm_w", "conv_w", "a_log", "dt_bias", "head_norm_w", "sgu_ln_w", "sgu_ln_b", "w_spatial",
         "b_spatial", "final_norm_w")


def _pack(parts):
    rows = []
    for p in parts:
        f = p.reshape(-1)
        f = jnp.pad(f, (0, (-f.shape[0]) % (8 * LANES)))
        rows.append(f.reshape(-1, LANES))
    return jnp.concatenate(rows, axis=0)


def _unpack(buf, shapes):
    out, r = [], 0
    for s in shapes:
        n = 1
        for d in s:
            n *= d
        nr = -(-n // (8 * LANES)) * 8
        out.append(buf[r:r + nr].reshape(-1)[:n].reshape(s))
        r += nr
    return out


def kernel(x, norm_w, w_in, conv_w, a_log, dt_bias, head_norm_w, sgu_ln_w, sgu_ln_b, w_spatial, b_spatial, w_out, final_norm_w, loss_target, m_norm_w, m_w_in, m_conv_w, m_a_log, m_dt_bias, m_head_norm_w, m_sgu_ln_w, m_sgu_ln_b, m_w_spatial, m_b_spatial, m_w_out, m_final_norm_w, v_norm_w, v_w_in, v_conv_w, v_a_log, v_dt_bias, v_head_norm_w, v_sgu_ln_w, v_sgu_ln_b, v_w_spatial, v_b_spatial, v_w_out, v_final_norm_w):
    T, D = x.shape[1], x.shape[2]
    weights = dict(norm_w=norm_w, w_in=w_in, conv_w=conv_w, a_log=a_log, dt_bias=dt_bias, head_norm_w=head_norm_w,
                   sgu_ln_w=sgu_ln_w, sgu_ln_b=sgu_ln_b, w_spatial=w_spatial, b_spatial=b_spatial, w_out=w_out,
                   final_norm_w=final_norm_w)
    mom_m = dict(norm_w=m_norm_w, w_in=m_w_in, conv_w=m_conv_w, a_log=m_a_log, dt_bias=m_dt_bias,
                 head_norm_w=m_head_norm_w, sgu_ln_w=m_sgu_ln_w, sgu_ln_b=m_sgu_ln_b, w_spatial=m_w_spatial,
                 b_spatial=m_b_spatial, w_out=m_w_out, final_norm_w=m_final_norm_w)
    mom_v = dict(norm_w=v_norm_w, w_in=v_w_in, conv_w=v_conv_w, a_log=v_a_log, dt_bias=v_dt_bias,
                 head_norm_w=v_head_norm_w, sgu_ln_w=v_sgu_ln_w, sgu_ln_b=v_sgu_ln_b, w_spatial=v_w_spatial,
                 b_spatial=v_b_spatial, w_out=v_w_out, final_norm_w=v_final_norm_w)
    me = _chip_index(lax.axis_index("x"), lax.axis_index("y"))
    c_arr = lax.axis_index("c").astype(jnp.int32).reshape(1)
    Din, Cb = w_in.shape[1], w_in.shape[2]
    Rb = w_out.shape[1]
    cconv = conv_w.shape[2]

    loss_row, grad_x, g, qw, qo = _device_step(
        x[0], loss_target[0], norm_w, w_in[0].astype(BF16), w_out[0].astype(BF16), conv_w[0], a_log, dt_bias,
        head_norm_w, sgu_ln_w, sgu_ln_b, w_spatial[0], b_spatial[0], final_norm_w, c_arr)

    small_shapes = [tuple(g[n].shape) for n in SMALL]
    small = _allreduce_small(_pack([g[n] for n in SMALL]))
    gsum_in, gsum_out = _sibling_fill(_chip_sum(qw, c_arr, "chip_sum_w_in"), _chip_sum(qo, c_arr, "chip_sum_w_out"))
    gsum_in = _Layout(a_log.shape[1], w_spatial.shape[1], N_CHIPS, Cb).from_window(gsum_in, me, Cb)
    gsmall = dict(zip(SMALL, _unpack(small, small_shapes)))
    gsmall["conv_w"] = lax.dynamic_slice_in_dim(gsmall["conv_w"], me * cconv, cconv, axis=1)

    grads, deltas, new_m, new_v = {}, {}, {}, {}
    d, m2, v2 = _adamw(w_out[0], gsum_out, m_w_out[0], v_w_out[0], "adamw_w_out")
    grads["w_out"], deltas["w_out"], new_m["w_out"], new_v["w_out"] = gsum_out[None], d[None], m2[None], v2[None]
    flat = lambda a: a.transpose(2, 0, 1).reshape(-1, LANES)
    unflat = lambda f: f.reshape(Cb, 1, Din).transpose(1, 2, 0)
    g_flat = gsum_in.T.reshape(-1, LANES)
    d, m2, v2 = _adamw(flat(w_in), g_flat, flat(m_w_in), flat(v_w_in), "adamw_w_in")
    grads["w_in"], deltas["w_in"], new_m["w_in"], new_v["w_in"] = unflat(g_flat), unflat(d), unflat(m2), unflat(v2)
    shapes = [tuple(weights[n].shape) for n in SMALL]
    ds, ms, vs = _adamw(_pack([weights[n] for n in SMALL]), _pack([gsmall[n] for n in SMALL]),
                        _pack([mom_m[n] for n in SMALL]), _pack([mom_v[n] for n in SMALL]), "adamw_small")
    for n, gq, d, m2, v2 in zip(SMALL, [gsmall[n] for n in SMALL], _unpack(ds, shapes), _unpack(ms, shapes),
                                _unpack(vs, shapes)):
        grads[n], deltas[n], new_m[n], new_v[n] = gq.reshape(weights[n].shape), d, m2, v2

    loss = lax.psum(loss_row[0, 0], ("x", "y", "c"))
    order = ("norm_w", "w_in", "conv_w", "a_log", "dt_bias", "head_norm_w", "sgu_ln_w", "sgu_ln_b", "w_spatial",
             "b_spatial", "w_out", "final_norm_w")
    return (loss, grad_x[None], *[grads[n] for n in order], *[deltas[n] for n in order],
            *[new_m[n] for n in order], *[new_v[n] for n in order])
```

```python
import functools

import jax
import jax.numpy as jnp
from jax import lax
from jax.experimental import pallas as pl
from jax.experimental.pallas import tpu as pltpu

F32 = jnp.float32
BF16 = jnp.bfloat16
EPS = 1e-6
HEAD_DIM = 128
CHUNK_B = 128
CONV_WIDTH = 4
LANES = 128
HALO = 8
N_CHIPS = 4
ADAM_LR = 0.001
ADAM_B1 = 0.9
ADAM_B2 = 0.999
ADAM_EPS = 1e-08
ADAM_WD = 0.01
ADAM_STEP = 10
VMEM_LIMIT = 56 * 1024 * 1024
MESH_ID = pl.DeviceIdType.MESH
HI = lax.Precision.HIGHEST


def _cparams(sem=None, **kw):
    return pltpu.CompilerParams(dimension_semantics=sem, vmem_limit_bytes=VMEM_LIMIT, **kw)


def _matmul(a, b, ca, cb, precision):
    nb = a.ndim - 2
    batch = tuple(range(nb))
    return lax.dot_general(a, b, (((ca + nb,), (cb + nb,)), (batch, batch)), precision=precision,
                           preferred_element_type=F32)


def _dot(a, b, hi=False, precision=None):
    return _matmul(a, b, 1, 0, HI if hi else precision)


def _dot_nt(a, b, hi=False, precision=None):
    return _matmul(a, b, 1, 1, HI if hi else precision)


def _dot_tn(a, b, hi=False, precision=None):
    return _matmul(a, b, 0, 0, HI if hi else precision)


def _iota(shape, dim):
    return lax.broadcasted_iota(jnp.int32, shape, dim)


def _sigmoid(x):
    return 0.5 * (jnp.tanh(0.5 * x) + 1.0)


def _silu(x):
    return x * _sigmoid(x)


def _softplus(x):
    z = jnp.exp(-jnp.abs(x))
    small = z * (1.0 - z * (0.5 - z * (1.0 / 3.0)))
    return jnp.maximum(x, 0.0) + jnp.where(z < 1e-3, small, jnp.log(1.0 + z))


def _pick(n, pref):
    for t in pref:
        if n % t == 0:
            return t
    return n


class _Ride:
    def __init__(self, operands, out_shape, n_sems, start, finish):
        self.operands, self.out_shape, self.n_sems = list(operands), list(out_shape), n_sems
        self.start, self.finish = start, finish


def _pallas(body, operands, *, name, grid, in_specs, out_specs, out_shape, semantics, scratch_shapes=(),
            prefetch=0, ride=None):
    single = not isinstance(out_shape, (list, tuple))
    outs = [out_shape] if single else list(out_shape)
    ospecs = [out_specs] if single else list(out_specs)
    in_specs, scratch = list(in_specs), list(scratch_shapes)
    n_in, n_out, n_sc = len(operands) - prefetch, len(outs), len(scratch)
    kernel = body
    params = _cparams(semantics)
    if ride is not None:
        n_xin, n_xout = len(ride.operands), len(ride.out_shape)

        def kernel(*refs):
            pre, refs = refs[:prefetch], refs[prefetch:]
            ins, refs = refs[:n_in], refs[n_in:]
            xins, refs = refs[:n_xin], refs[n_xin:]
            mains, refs = refs[:n_out], refs[n_out:]
            xouts, refs = refs[:n_xout], refs[n_xout:]
            sc, (send, recv) = refs[:n_sc], refs[n_sc:]
            ids = [pl.program_id(a) for a in range(len(grid))]
            first = functools.reduce(jnp.logical_and, [i == 0 for i in ids])
            last = functools.reduce(jnp.logical_and, [i == g - 1 for i, g in zip(ids, grid)])

            @pl.when(first)
            def _():
                ride.start(xins, xouts, send, recv)

            body(*pre, *ins, *mains, *sc)

            @pl.when(last)
            def _():
                ride.finish(xins, xouts, send, recv)

        operands = list(operands) + ride.operands
        in_specs += [ANY] * n_xin
        ospecs += [ANY] * n_xout
        outs += ride.out_shape
        scratch += [pltpu.SemaphoreType.DMA((ride.n_sems,)), pltpu.SemaphoreType.DMA((ride.n_sems,))]
        params = _cparams(("arbitrary",) * len(grid), has_side_effects=True)
    if prefetch:
        spec = dict(grid_spec=pltpu.PrefetchScalarGridSpec(
            num_scalar_prefetch=prefetch, grid=grid, in_specs=in_specs, out_specs=ospecs, scratch_shapes=scratch))
    else:
        spec = dict(grid=grid, in_specs=in_specs, out_specs=ospecs, scratch_shapes=scratch)
    res = pl.pallas_call(kernel, name=name, out_shape=outs, compiler_params=params, **spec)(*operands)
    main = res[0] if single else list(res[:n_out])
    return main if ride is None else (main, list(res[n_out:]))


def _mm_nn(a, b, out_dtype, name, tm=1024, tn=512, tk=None, cols=None, ride=None):
    M, K = a.shape
    c0, N = (0, b.shape[1]) if cols is None else cols
    tm = _pick(M, (tm, 512, 256, 128))
    tn = _pick(N, (tn, 512, 384, 256, 128))
    tk = K if tk is None else _pick(K, (tk,))
    nk = K // tk
    j0 = c0 // tn
    assert c0 % tn == 0

    def body(a_ref, b_ref, o_ref, *scratch):
        part = _dot(a_ref[...], b_ref[...])
        if nk == 1:
            o_ref[...] = part.astype(out_dtype)
        else:
            acc_ref, = scratch
            k = pl.program_id(2)

            @pl.when(k == 0)
            def _():
                acc_ref[...] = part

            @pl.when(k > 0)
            def _():
                acc_ref[...] += part

            @pl.when(k == nk - 1)
            def _():
                o_ref[...] = acc_ref[...].astype(out_dtype)

    return _pallas(
        body, (a, b), name=name, grid=(M // tm, N // tn, nk),
        in_specs=[pl.BlockSpec((tm, tk), lambda i, j, k: (i, k)),
                  pl.BlockSpec((tk, tn), lambda i, j, k: (k, j + j0))],
        out_specs=pl.BlockSpec((tm, tn), lambda i, j, k: (i, j)),
        out_shape=jax.ShapeDtypeStruct((M, N), out_dtype),
        scratch_shapes=[] if nk == 1 else [pltpu.VMEM((tm, tn), F32)],
        semantics=("parallel", "parallel", "arbitrary"), ride=ride)


def _mm_nt_rhs_outer(a, b, out_dtype, name, tm=256, tn=1024, ride=None):
    M, K = a.shape
    N, _ = b.shape
    tm = _pick(M, (tm, 128))
    tn = _pick(N, (tn, 512, 256, 128))

    def body(a_ref, b_ref, o_ref):
        o_ref[...] = _dot_nt(a_ref[...], b_ref[...]).astype(out_dtype)

    return _pallas(
        body, (a, b), name=name, grid=(N // tn, M // tm),
        in_specs=[pl.BlockSpec((tm, K), lambda j, i: (i, 0)),
                  pl.BlockSpec((tn, K), lambda j, i: (j, 0))],
        out_specs=pl.BlockSpec((tm, tn), lambda j, i: (i, j)),
        out_shape=jax.ShapeDtypeStruct((M, N), out_dtype),
        semantics=("parallel", "parallel"), ride=ride)


WIN_BLOCK = 256


def _mm_windows(a, b, table, nb, name, tm=2048):
    M, K = a.shape
    wb = table.shape[0] // nb
    tm = _pick(M, (tm, 1024, 512, 256, 128))

    def body(tab_ref, a_ref, b_ref, o_ref):
        o_ref[0] = _dot(a_ref[...], b_ref[...])

    return pl.pallas_call(
        body, name=name,
        grid_spec=pltpu.PrefetchScalarGridSpec(
            num_scalar_prefetch=1, grid=(nb, M // tm, wb),
            in_specs=[pl.BlockSpec((tm, K), lambda n, i, t, tab: (i, 0)),
                      pl.BlockSpec((K, WIN_BLOCK), lambda n, i, t, tab: (0, tab[n * wb + t]))],
            out_specs=pl.BlockSpec((1, tm, WIN_BLOCK), lambda n, i, t, tab: (n, i, t))),
        out_shape=jax.ShapeDtypeStruct((nb, M, wb * WIN_BLOCK), F32),
        compiler_params=_cparams(("parallel", "parallel", "arbitrary")),
    )(table, a, b)


def _mm_nn_pair(a0, a1, b, name, tm=512, tn=1024, ride=None):
    M, K = a0.shape
    _, N = b.shape
    tm = _pick(M, (tm, 256, 128))
    tn = _pick(N, (tn, 512, 256, 128))
    ni = M // tm

    def body(a0_ref, a1_ref, b_ref, o_ref):
        p = pl.program_id(0)

        @pl.when(p == 0)
        def _():
            o_ref[...] = _dot(a0_ref[...], b_ref[...])

        @pl.when(p == 1)
        def _():
            o_ref[...] = _dot(a1_ref[...], b_ref[...])

    return _pallas(
        body, (a0, a1, b), name=name, grid=(2, ni, N // tn),
        in_specs=[pl.BlockSpec((tm, K), lambda p, i, j: (i * (1 - p), 0)),
                  pl.BlockSpec((tm, K), lambda p, i, j: (i * p, 0)),
                  pl.BlockSpec((K, tn), lambda p, i, j: (0, j))],
        out_specs=pl.BlockSpec((tm, tn), lambda p, i, j: (p * ni + i, j)),
        out_shape=jax.ShapeDtypeStruct((2 * M, N), F32),
        semantics=("parallel", "parallel", "parallel"), ride=ride)


def _rms_fn(x, w):
    r = lax.rsqrt(jnp.mean(x * x, axis=-1, keepdims=True) + EPS)
    return x * r * w


def _rms_in(x, w, ride=None):
    T, D = x.shape
    tm = _pick(T, (512, 256, 128))

    def body(x_ref, w_ref, o_ref, ot_ref):
        xn = _rms_fn(x_ref[...], w_ref[...])
        o_ref[...] = xn.astype(BF16)
        ot_ref[...] = xn.T.astype(BF16)

    return _pallas(
        body, (x, w), name="rms_in", grid=(T // tm,),
        in_specs=[pl.BlockSpec((tm, D), lambda i: (i, 0)), pl.BlockSpec((1, D), lambda i: (0, 0))],
        out_specs=[pl.BlockSpec((tm, D), lambda i: (i, 0)), pl.BlockSpec((D, tm), lambda i: (0, i))],
        out_shape=[jax.ShapeDtypeStruct((T, D), BF16), jax.ShapeDtypeStruct((D, T), BF16)],
        semantics=("parallel",), ride=ride)


def _rms_in_bwd(x, w, dxn, dh, ride=None):
    T, D = x.shape
    tm = _pick(T, (256, 128))

    def body(x_ref, w_ref, dxn_ref, dh_ref, gx_ref, dw_ref):
        _, vjp = jax.vjp(_rms_fn, x_ref[...], w_ref[...])
        dx, dw = vjp(dxn_ref[...])
        gx_ref[...] = dh_ref[...] + dx

        @pl.when(pl.program_id(0) == 0)
        def _():
            dw_ref[...] = dw

        @pl.when(pl.program_id(0) > 0)
        def _():
            dw_ref[...] += dw

    tile = pl.BlockSpec((tm, D), lambda i: (i, 0))
    row = pl.BlockSpec((1, D), lambda i: (0, 0))
    return _pallas(
        body, (x, w, dxn, dh), name="rms_in_bwd", grid=(T // tm,),
        in_specs=[tile, row, tile, tile], out_specs=[tile, row],
        out_shape=[jax.ShapeDtypeStruct((T, D), F32), jax.ShapeDtypeStruct((1, D), F32)],
        semantics=("arbitrary",), ride=ride)


def _conv_fwd(cat_ref, halo, x, w):
    tm = x.shape[0]
    cat_ref[0:HALO, :] = halo
    cat_ref[HALO:HALO + tm, :] = x
    c = x * w[CONV_WIDTH - 1:CONV_WIDTH, :]
    for k in range(CONV_WIDTH - 1):
        s = CONV_WIDTH - 1 - k
        c = c + cat_ref[pl.ds(HALO - s, tm), :] * w[k:k + 1, :]
    return c


def _lane_to_all(x, lane):
    @jax.custom_vjp
    def f(x):
        return jnp.broadcast_to(x[:, lane:lane + 1], x.shape)

    def f_fwd(x):
        return f(x), None

    def f_bwd(_, g):
        return (jnp.where(_iota(g.shape, 1) == lane, jnp.sum(g, axis=-1, keepdims=True), 0.0),)

    f.defvjp(f_fwd, f_bwd)
    return f(x)


def _gdn_pointwise(c, ba, alog, dtb, H):
    A = H * HEAD_DIM
    s = _silu(c)
    beta = _sigmoid(ba)
    g = -jnp.exp(alog) * _softplus(ba + dtb)
    qs, ks, vs, gbs, bbs = [], [], [], [], []
    for h in range(H):
        lo = h * HEAD_DIM
        q = s[:, lo:lo + HEAD_DIM]
        k = s[:, A + lo:A + lo + HEAD_DIM]
        qs.append(q * lax.rsqrt(jnp.sum(q * q, axis=-1, keepdims=True) + EPS))
        ks.append(k * lax.rsqrt(jnp.sum(k * k, axis=-1, keepdims=True) + EPS))
        vs.append(s[:, 2 * A + lo:2 * A + lo + HEAD_DIM])
        bbs.append(_lane_to_all(beta, h))
        gbs.append(_lane_to_all(g, H + h))
    st = lambda xs: jnp.stack(xs, axis=0)
    return st(qs), st(ks), st(vs), st(gbs), st(bbs)


def _halo_prev(tm):
    return lambda i: (jnp.maximum(i * (tm // HALO) - 1, 0), 0)


def _gdn_pre(proj_m, proj_ba, conv_w, alog_row, dtb_row, H):
    T = proj_m.shape[0]
    A = H * HEAD_DIM
    tm = _pick(T, (256, 128))
    hs = pl.BlockSpec((H, tm, HEAD_DIM), lambda i: (0, i, 0))
    hshape = jax.ShapeDtypeStruct((H, T, HEAD_DIM), F32)

    def body(x_ref, halo_ref, ba_ref, w_ref, al_ref, dt_ref, q_ref, k_ref, v_ref, gb_ref, bb_ref, cat_ref):
        halo = jnp.where(pl.program_id(0) == 0, 0.0, halo_ref[...])
        c = _conv_fwd(cat_ref, halo, x_ref[...], w_ref[...])
        q, k, v, gb, bb = _gdn_pointwise(c, ba_ref[...], al_ref[...], dt_ref[...], H)
        q_ref[...] = q
        k_ref[...] = k
        v_ref[...] = v
        gb_ref[...] = gb
        bb_ref[...] = bb

    return pl.pallas_call(
        body, name="gdn_pre", grid=(T // tm,),
        in_specs=[pl.BlockSpec((tm, 3 * A), lambda i: (i, 0)),
                  pl.BlockSpec((HALO, 3 * A), _halo_prev(tm)),
                  pl.BlockSpec((tm, LANES), lambda i: (i, 0)),
                  pl.BlockSpec((CONV_WIDTH, 3 * A), lambda i: (0, 0)),
                  pl.BlockSpec((1, LANES), lambda i: (0, 0)),
                  pl.BlockSpec((1, LANES), lambda i: (0, 0))],
        out_specs=[hs] * 5, out_shape=[hshape] * 5,
        scratch_shapes=[pltpu.VMEM((HALO + tm, 3 * A), F32)],
        compiler_params=_cparams(("parallel",)),
    )(proj_m, proj_m, proj_ba, conv_w, alog_row, dtb_row)


def _gdn_pre_bwd(proj_m, proj_ba, conv_w, alog_row, dtb_row, dq, dk, dv, dgb, dbb, H, dproj):
    T, n_main = proj_m.shape
    A = H * HEAD_DIM
    tm = _pick(T, (256, 128))
    hs = pl.BlockSpec((H, tm, HEAD_DIM), lambda i: (0, i, 0))
    row = pl.BlockSpec((1, LANES), lambda i: (0, 0))

    def body(x_ref, halo_ref, ba_ref, w_ref, al_ref, dt_ref, dq_ref, dk_ref, dv_ref, dgb_ref, dbb_ref, _,
             dc_ref, dba_ref, dal_ref, ddt_ref, cat_ref):
        halo = jnp.where(pl.program_id(0) == 0, 0.0, halo_ref[...])
        c = _conv_fwd(cat_ref, halo, x_ref[...], w_ref[...])
        _, vjp = jax.vjp(functools.partial(_gdn_pointwise, H=H), c, ba_ref[...], al_ref[...], dt_ref[...])
        dc, dba, dal, ddt = vjp((dq_ref[...], dk_ref[...], dv_ref[...], dgb_ref[...], dbb_ref[...]))
        dc_ref[...] = dc
        dba_ref[:, :LANES] = dba.astype(BF16)
        dba_ref[:, LANES:] = jnp.zeros((tm, WIN_BLOCK - LANES), BF16)

        @pl.when(pl.program_id(0) == 0)
        def _():
            dal_ref[...] = dal
            ddt_ref[...] = ddt

        @pl.when(pl.program_id(0) > 0)
        def _():
            dal_ref[...] += dal
            ddt_ref[...] += ddt

    return pl.pallas_call(
        body, name="gdn_pre_bwd", grid=(T // tm,),
        in_specs=[pl.BlockSpec((tm, 3 * A), lambda i: (i, 0)),
                  pl.BlockSpec((HALO, 3 * A), _halo_prev(tm)),
                  pl.BlockSpec((tm, LANES), lambda i: (i, 0)),
                  pl.BlockSpec((CONV_WIDTH, 3 * A), lambda i: (0, 0)),
                  row, row, hs, hs, hs, hs, hs, ANY],
        out_specs=[pl.BlockSpec((tm, 3 * A), lambda i: (i, 0)),
                   pl.BlockSpec((tm, WIN_BLOCK), lambda i: (i, n_main // WIN_BLOCK)), row, row],
        out_shape=[jax.ShapeDtypeStruct((T, 3 * A), F32), jax.ShapeDtypeStruct(dproj.shape, dproj.dtype),
                   jax.ShapeDtypeStruct((1, LANES), F32), jax.ShapeDtypeStruct((1, LANES), F32)],
        input_output_aliases={11: 1},
        scratch_shapes=[pltpu.VMEM((HALO + tm, 3 * A), F32)],
        compiler_params=_cparams(("arbitrary",)),
    )(proj_m, proj_m, proj_ba, conv_w, alog_row, dtb_row, dq, dk, dv, dgb, dbb, dproj)


def _conv_bwd(proj_m, dc, conv_w, H, dproj):
    T = proj_m.shape[0]
    A = H * HEAD_DIM
    tm = _pick(T, (256, 128))
    nt = T // tm

    def body(x_ref, halo_ref, dc_ref, nxt_ref, w_ref, _, dx_ref, dw_ref):
        i = pl.program_id(0)
        halo = jnp.where(i == 0, 0.0, halo_ref[...])
        xcat = jnp.concatenate([halo, x_ref[...]], axis=0)
        nxt = jnp.where(i == nt - 1, 0.0, nxt_ref[...])
        dc = dc_ref[...]
        dcat = jnp.concatenate([dc, nxt], axis=0)
        w = w_ref[...]
        dx = None
        rows = []
        for k in range(CONV_WIDTH):
            s = CONV_WIDTH - 1 - k
            ds = dcat if s == 0 else pltpu.roll(dcat, tm + HALO - s, 0)
            term = ds[:tm, :] * w[k:k + 1, :]
            dx = term if dx is None else dx + term
            xs = xcat if s == 0 else pltpu.roll(xcat, s, 0)
            rows.append(jnp.sum(dc * xs[HALO:, :], axis=0, keepdims=True))
        dx_ref[...] = dx.astype(BF16)
        dw = jnp.concatenate(rows + [jnp.zeros((HALO - CONV_WIDTH, 3 * A), F32)], axis=0)

        @pl.when(i == 0)
        def _():
            dw_ref[...] = dw

        @pl.when(i > 0)
        def _():
            dw_ref[...] += dw

    return pl.pallas_call(
        body, name="conv_bwd", grid=(nt,),
        in_specs=[pl.BlockSpec((tm, 3 * A), lambda i: (i, 0)),
                  pl.BlockSpec((HALO, 3 * A), _halo_prev(tm)),
                  pl.BlockSpec((tm, 3 * A), lambda i: (i, 0)),
                  pl.BlockSpec((HALO, 3 * A), lambda i: (jnp.minimum((i + 1) * (tm // HALO), T // HALO - 1), 0)),
                  pl.BlockSpec((CONV_WIDTH, 3 * A), lambda i: (0, 0)), ANY],
        out_specs=[pl.BlockSpec((tm, 3 * A), lambda i: (i, 0)),
                   pl.BlockSpec((HALO, 3 * A), lambda i: (0, 0))],
        out_shape=[jax.ShapeDtypeStruct(dproj.shape, dproj.dtype), jax.ShapeDtypeStruct((HALO, 3 * A), F32)],
        input_output_aliases={5: 0},
        compiler_params=_cparams(("arbitrary",)),
    )(proj_m, proj_m, dc, dc, conv_w, dproj)


CHUNK = 128
BLOCK = 64


def _b(x):
    return x.astype(BF16)


@jax.custom_vjp
def _bdot(a, b):
    return _dot(_b(a), _b(b))


def _bdot_f(a, b):
    return _bdot(a, b), (a, b)


def _bdot_b(res, g):
    a, b = res
    return _dot_nt(_b(g), _b(b)), _dot_tn(_b(a), _b(g))


_bdot.defvjp(_bdot_f, _bdot_b)


@jax.custom_vjp
def _bdot_nt(a, b):
    return _dot_nt(_b(a), _b(b))


def _bdot_nt_f(a, b):
    return _bdot_nt(a, b), (a, b)


def _bdot_nt_b(res, g):
    a, b = res
    return _dot(_b(g), _b(b)), _dot_tn(_b(g), _b(a))


_bdot_nt.defvjp(_bdot_nt_f, _bdot_nt_b)


@jax.custom_vjp
def _bdot_tn(a, b):
    return _dot_tn(_b(a), _b(b))


def _bdot_tn_f(a, b):
    return _bdot_tn(a, b), (a, b)


def _bdot_tn_b(res, g):
    a, b = res
    return _dot_nt(_b(b), _b(g)), _dot(_b(a), _b(g))


_bdot_tn.defvjp(_bdot_tn_f, _bdot_tn_b)


def _mask_matmul(m, x):
    hi = _b(x)
    r = x - hi.astype(F32)
    mid = _b(r)
    lo = _b(r - mid.astype(F32))
    return (_dot(m, lo) + _dot(m, mid)) + _dot(m, hi)


@jax.custom_vjp
def _mask_dot(m, mt, x):
    return _mask_matmul(m, x)


def _mask_dot_f(m, mt, x):
    return _mask_matmul(m, x), (m, mt)


def _mask_dot_b(res, g):
    m, mt = res
    return jnp.zeros_like(m), jnp.zeros_like(mt), _mask_matmul(mt, g)


_mask_dot.defvjp(_mask_dot_f, _mask_dot_b)

HIGH = lax.Precision.HIGH


def _unit_lower_inverse(L):
    n = L.shape[-1]
    X = -L
    Q = X
    for _ in range(BLOCK.bit_length() - 2):
        X = _dot(_b(X), _b(X))
        Q = Q + X + _dot(_b(Q), _b(X))
    return (_iota((n, n), 0) == _iota((n, n), 1)).astype(F32) + Q


@jax.custom_vjp
def _known_inverse(L, P):
    return P


def _known_inverse_f(L, P):
    return P, P


def _known_inverse_b(P, g):
    n = P.shape[-1]
    Q = _b(P - (_iota((n, n), 0) == _iota((n, n), 1)).astype(F32))
    t = g + _dot_tn(Q, _b(g))
    return -(t + _dot_nt(_b(t), Q)), jnp.zeros_like(P)


_known_inverse.defvjp(_known_inverse_f, _known_inverse_b)


def _gdn_prep_fn(q, k, v, gb, bb, P_known=None):
    n = CHUNK
    row, col = _iota((n, n), 0), _iota((n, n), 1)
    same = (row // BLOCK) == (col // BLOCK)
    incl, strict = same & (row >= col), same & (row > col)
    bc = lambda m: jnp.broadcast_to(_b(m.astype(F32)), q.shape[:1] + (n, n))
    tril, triu, ones = bc(incl), bc(same & (row <= col)), bc(same)
    gc = _mask_dot(tril, triu, gb)
    gl = _mask_dot(ones, ones, gb)
    decay = jnp.where(incl, jnp.exp(jnp.where(incl, gc - jnp.swapaxes(gc, 1, 2), 0.0)), 0.0)
    kb = k * bb
    vb = v * bb
    qs = q * (HEAD_DIM ** -0.5)
    L = jnp.where(strict, _bdot_nt(kb, k) * decay, 0.0)
    P = _unit_lower_inverse(L) if P_known is None else _known_inverse(L, P_known)
    egc = jnp.exp(gc)
    u = _bdot(P, vb)
    w = _bdot(P, kb * egc)
    attn = jnp.where(incl, _bdot_nt(qs, k) * decay, 0.0)
    qg = qs * egc
    kdec = k * jnp.exp(gl - gc)
    eg = jnp.exp(gl).reshape(-1, n // BLOCK, BLOCK, LANES).sum(axis=2) * (1.0 / BLOCK)
    if P_known is None:
        return u, w, qg, kdec, attn, eg, P
    return u, w, qg, kdec, attn, eg


def _gdn_chain_fn(S, qg, kdec, u, w, attn, eg):
    nblk = CHUNK // BLOCK
    cat = lambda xs: jnp.concatenate(xs, axis=1)
    outs, found = [], []
    for i in range(nblk):
        r = (slice(None), slice(i * BLOCK, (i + 1) * BLOCK))
        v_new = u[r] - _bdot(w[r], S)
        found.append(v_new)
        outs.append(_bdot(qg[r], S) + _bdot(attn[r], cat(found + [jnp.zeros_like(v_new)] * (nblk - 1 - i))))
        S = S * eg[i] + _bdot_tn(kdec[r], v_new)
    return cat(outs), S


def _eg_spec(H, T, chunks, index_map, per_head):
    nblk = CHUNK // BLOCK
    block = (chunks, 1 if per_head else H, nblk, LANES)
    return pl.BlockSpec(block, index_map), jax.ShapeDtypeStruct((T // CHUNK, H, nblk, LANES), F32)


def _gdn_prep(q, k, v, gb, bb):
    H, T, _ = q.shape
    pb = _pick(T // CHUNK, (8, 4, 2, 1))
    hs = pl.BlockSpec((1, CHUNK * pb, HEAD_DIM), lambda h, n: (h, n, 0))
    hshape = jax.ShapeDtypeStruct((H, T, HEAD_DIM), F32)

    def body(q_ref, k_ref, v_ref, gb_ref, bb_ref, *out_refs):
        chunks = lambda ref: ref[0].reshape(pb, CHUNK, HEAD_DIM)
        outs = _gdn_prep_fn(chunks(q_ref), chunks(k_ref), chunks(v_ref), chunks(gb_ref), chunks(bb_ref))
        for i, (ref, val) in enumerate(zip(out_refs, outs)):
            if i == 5:
                ref[:, 0] = val
            else:
                ref[0] = val.reshape(pb * CHUNK, HEAD_DIM).astype(ref.dtype)

    kept = [F32, BF16, BF16, BF16, BF16, None, BF16]
    es, eshape = _eg_spec(H, T, pb, lambda h, n: (n, h, 0, 0), per_head=True)
    return pl.pallas_call(
        body, name="gdn_prep", grid=(H, T // (CHUNK * pb)),
        in_specs=[hs] * 5, out_specs=[es if dt is None else hs for dt in kept],
        out_shape=[eshape if dt is None else jax.ShapeDtypeStruct((H, T, HEAD_DIM), dt) for dt in kept],
        compiler_params=_cparams(("parallel", "parallel")),
    )(q, k, v, gb, bb)


def _gdn_prep_bwd(q, k, v, gb, bb, pinv, du, dw, dqg, dkd, dat, deg):
    H, T, _ = q.shape
    pb = _pick(T // CHUNK, (8, 4, 2, 1))
    hs = pl.BlockSpec((1, CHUNK * pb, HEAD_DIM), lambda h, n: (h, n, 0))
    hshape = jax.ShapeDtypeStruct((H, T, HEAD_DIM), F32)

    def body(*refs):
        in_refs, p_ref, ct_refs, out_refs = refs[:5], refs[5], refs[6:12], refs[12:]
        chunks = lambda ref: ref[0].reshape(pb, CHUNK, HEAD_DIM)
        P = chunks(p_ref).astype(F32)
        _, vjp = jax.vjp(lambda *a: _gdn_prep_fn(*a, P_known=P), *[chunks(r) for r in in_refs])
        grads = vjp(tuple(chunks(r).astype(F32) for r in ct_refs[:5]) + (ct_refs[5][:, 0],))
        for ref, val in zip(out_refs, grads):
            ref[0] = val.reshape(pb * CHUNK, HEAD_DIM)

    es, _ = _eg_spec(H, T, pb, lambda h, n: (n, h, 0, 0), per_head=True)
    return pl.pallas_call(
        body, name="gdn_prep_bwd", grid=(H, T // (CHUNK * pb)),
        in_specs=[hs] * 11 + [es], out_specs=[hs] * 5, out_shape=[hshape] * 5,
        compiler_params=_cparams(("parallel", "parallel")),
    )(q, k, v, gb, bb, pinv, du, dw, dqg, dkd, dat, deg)


def _gdn_chain(qg, kd, u, w, attn, eg):
    H, T, _ = qg.shape
    N = T // CHUNK
    hs = pl.BlockSpec((H, CHUNK, HEAD_DIM), lambda n: (0, n, 0))
    ss = pl.BlockSpec((1, H, HEAD_DIM, HEAD_DIM), lambda n: (n, 0, 0, 0))

    def body(qg_ref, kd_ref, u_ref, w_ref, at_ref, eg_ref, o_ref, sall_ref, s_ref):
        @pl.when(pl.program_id(0) == 0)
        def _():
            s_ref[...] = jnp.zeros_like(s_ref)

        S = s_ref[...]
        sall_ref[0] = S
        eg = tuple(eg_ref[0, :, i:i + 1, :] for i in range(CHUNK // BLOCK))
        o, S2 = _gdn_chain_fn(S, qg_ref[...], kd_ref[...], u_ref[...], w_ref[...], at_ref[...], eg)
        o_ref[...] = o
        s_ref[...] = S2

    es, _ = _eg_spec(H, T, 1, lambda n: (n, 0, 0, 0), per_head=False)
    return pl.pallas_call(
        body, name="gdn_chain", grid=(N,),
        in_specs=[hs] * 5 + [es], out_specs=[hs, ss],
        out_shape=[jax.ShapeDtypeStruct((H, T, HEAD_DIM), F32),
                   jax.ShapeDtypeStruct((N, H, HEAD_DIM, HEAD_DIM), F32)],
        scratch_shapes=[pltpu.VMEM((H, HEAD_DIM, HEAD_DIM), F32)],
        compiler_params=_cparams(("arbitrary",)),
    )(qg, kd, u, w, attn, eg)


def _gdn_chain_bwd(qg, kd, u, w, attn, eg, sall, do):
    H, T, _ = qg.shape
    N = T // CHUNK
    hs = pl.BlockSpec((H, CHUNK, HEAD_DIM), lambda n: (0, N - 1 - n, 0))
    ss = pl.BlockSpec((1, H, HEAD_DIM, HEAD_DIM), lambda n: (N - 1 - n, 0, 0, 0))
    hshape = jax.ShapeDtypeStruct((H, T, HEAD_DIM), F32)

    def body(qg_ref, kd_ref, u_ref, w_ref, at_ref, eg_ref, sall_ref, do_ref, *rest):
        out_refs, ds_ref = rest[:6], rest[6]

        @pl.when(pl.program_id(0) == 0)
        def _():
            ds_ref[...] = jnp.zeros_like(ds_ref)

        f32 = lambda ref: ref[...].astype(F32)
        nblk = CHUNK // BLOCK
        eg = tuple(eg_ref[0, :, i:i + 1, :] for i in range(nblk))
        _, vjp = jax.vjp(_gdn_chain_fn, sall_ref[0], f32(qg_ref), f32(kd_ref), u_ref[...], f32(w_ref),
                         f32(at_ref), eg)
        grads = vjp((do_ref[...], ds_ref[...]))
        ds_ref[...] = grads[0]
        for ref, val in zip(out_refs[:5], grads[1:6]):
            ref[...] = val.astype(ref.dtype)
        for i in range(nblk):
            out_refs[5][0, :, i:i + 1, :] = grads[6][i]

    kept = [F32, F32, BF16, BF16, F32]
    es, eshape = _eg_spec(H, T, 1, lambda n: (N - 1 - n, 0, 0, 0), per_head=False)
    return pl.pallas_call(
        body, name="gdn_chain_bwd", grid=(N,),
        in_specs=[hs] * 5 + [es, ss, hs], out_specs=[hs] * 5 + [es],
        out_shape=[jax.ShapeDtypeStruct((H, T, HEAD_DIM), dt) for dt in kept] + [eshape],
        scratch_shapes=[pltpu.VMEM((H, HEAD_DIM, HEAD_DIM), F32)],
        compiler_params=_cparams(("arbitrary",)),
    )(qg, kd, u, w, attn, eg, sall, do)


def _post_fn(ogs, za, hw):
    outs = []
    for h, o in enumerate(ogs):
        r = lax.rsqrt(jnp.mean(o * o, axis=-1, keepdims=True) + EPS)
        outs.append(o * r * hw * _silu(za[:, h * HEAD_DIM:(h + 1) * HEAD_DIM]))
    return jnp.concatenate(outs, axis=1)


def _gdn_post(og, proj_m, hw):
    H, T, _ = og.shape
    A = H * HEAD_DIM
    tm = _pick(T, (512, 256, 128))

    def body(og_ref, za_ref, hw_ref, o_ref, ot_ref):
        o = _post_fn(tuple(og_ref[h] for h in range(H)), za_ref[...], hw_ref[...])
        o_ref[...] = o.astype(BF16)
        ot_ref[...] = o.T.astype(BF16)

    return pl.pallas_call(
        body, name="gdn_post", grid=(T // tm,),
        in_specs=[pl.BlockSpec((H, tm, HEAD_DIM), lambda i: (0, i, 0)),
                  pl.BlockSpec((tm, A), lambda i: (i, ZA_BLOCK)),
                  pl.BlockSpec((1, HEAD_DIM), lambda i: (0, 0))],
        out_specs=[pl.BlockSpec((tm, A), lambda i: (i, 0)), pl.BlockSpec((A, tm), lambda i: (0, i))],
        out_shape=[jax.ShapeDtypeStruct((T, A), BF16), jax.ShapeDtypeStruct((A, T), BF16)],
        compiler_params=_cparams(("parallel",)),
    )(og, proj_m, hw)


def _gdn_post_bwd(og, proj_m, hw, d_o, dproj):
    H, T, _ = og.shape
    A = H * HEAD_DIM
    tm = _pick(T, (256, 128))

    def body(og_ref, za_ref, hw_ref, do_ref, _, dog_ref, dza_ref, dhw_ref):
        _, vjp = jax.vjp(_post_fn, tuple(og_ref[h] for h in range(H)), za_ref[...], hw_ref[...])
        dog, dza, dhw = vjp(do_ref[...])
        for h in range(H):
            dog_ref[h] = dog[h]
        dza_ref[...] = dza.astype(BF16)

        @pl.when(pl.program_id(0) == 0)
        def _():
            dhw_ref[...] = dhw

        @pl.when(pl.program_id(0) > 0)
        def _():
            dhw_ref[...] += dhw

    return pl.pallas_call(
        body, name="gdn_post_bwd", grid=(T // tm,),
        in_specs=[pl.BlockSpec((H, tm, HEAD_DIM), lambda i: (0, i, 0)),
                  pl.BlockSpec((tm, A), lambda i: (i, ZA_BLOCK)),
                  pl.BlockSpec((1, HEAD_DIM), lambda i: (0, 0)),
                  pl.BlockSpec((tm, A), lambda i: (i, 0)), ANY],
        out_specs=[pl.BlockSpec((H, tm, HEAD_DIM), lambda i: (0, i, 0)),
                   pl.BlockSpec((tm, A), lambda i: (i, ZA_BLOCK)),
                   pl.BlockSpec((1, HEAD_DIM), lambda i: (0, 0))],
        out_shape=[jax.ShapeDtypeStruct((H, T, HEAD_DIM), F32), jax.ShapeDtypeStruct(dproj.shape, dproj.dtype),
                   jax.ShapeDtypeStruct((1, HEAD_DIM), F32)],
        input_output_aliases={4: 1},
        compiler_params=_cparams(("arbitrary",)),
    )(og, proj_m, hw, d_o, dproj)


def _sgu_fn(ub, vb, zb, lw, lb, W, bbc):
    G = len(W)
    tm = ub.shape[0]
    mu = jnp.mean(vb, axis=-1, keepdims=True)
    xc = vb - mu
    var = jnp.mean(xc * xc, axis=-1, keepdims=True)
    vn = xc * lax.rsqrt(var + EPS) * lw + lb
    mask = _iota((CHUNK_B, CHUNK_B), 0) >= _iota((CHUNK_B, CHUNK_B), 1)
    cols = []
    for g in range(G):
        wm = jnp.where(mask, W[g], 0.0).astype(BF16)
        rows = []
        for c in range(tm // CHUNK_B):
            blk = vn[c * CHUNK_B:(c + 1) * CHUNK_B, g * HEAD_DIM:(g + 1) * HEAD_DIM].astype(BF16)
            rows.append(_dot(wm, blk) + bbc[g])
        cols.append(jnp.concatenate(rows, axis=0) if len(rows) > 1 else rows[0])
    s = jnp.concatenate(cols, axis=1)
    return ub * s * _silu(zb)


ZA_BLOCK = 6


def _sgu_cols(A, B):
    assert A == B
    return 3, 4, 5


def _sgu_fwd(proj_m, lw, lb, W, bbc, A):
    T = proj_m.shape[0]
    G = W.shape[0]
    B = G * HEAD_DIM
    tm = _pick(T, (256, 128))
    cu, cv, cz = _sgu_cols(A, B)

    def body(u_ref, v_ref, z_ref, lw_ref, lb_ref, w_ref, b_ref, o_ref, ot_ref):
        o = _sgu_fn(u_ref[...], v_ref[...], z_ref[...], lw_ref[...], lb_ref[...],
                    tuple(w_ref[g] for g in range(G)), tuple(b_ref[g] for g in range(G)))
        o_ref[...] = o.astype(BF16)
        ot_ref[...] = o.T.astype(BF16)

    row = pl.BlockSpec((1, B), lambda i: (0, 0))
    cube = pl.BlockSpec((G, CHUNK_B, CHUNK_B), lambda i: (0, 0, 0))
    return pl.pallas_call(
        body, name="sgu_fwd", grid=(T // tm,),
        in_specs=[pl.BlockSpec((tm, B), lambda i: (i, cu)), pl.BlockSpec((tm, B), lambda i: (i, cv)),
                  pl.BlockSpec((tm, B), lambda i: (i, cz)), row, row, cube, cube],
        out_specs=[pl.BlockSpec((tm, B), lambda i: (i, 0)), pl.BlockSpec((B, tm), lambda i: (0, i))],
        out_shape=[jax.ShapeDtypeStruct((T, B), BF16), jax.ShapeDtypeStruct((B, T), BF16)],
        compiler_params=_cparams(("parallel",)),
    )(proj_m, proj_m, proj_m, lw, lb, W, bbc)


def _sgu_bwd(proj_m, lw, lb, W, bbc, d_o, A, dproj):
    T = proj_m.shape[0]
    G = W.shape[0]
    B = G * HEAD_DIM
    tm = _pick(T, (256, 128))
    nt = T // tm
    cu, cv, cz = _sgu_cols(A, B)

    def body(u_ref, v_ref, z_ref, lw_ref, lb_ref, w_ref, b_ref, do_ref, _,
             dp_ref, dlw_ref, dlb_ref, dw_ref, db_ref, dbb_ref):
        _, vjp = jax.vjp(_sgu_fn, u_ref[...], v_ref[...], z_ref[...], lw_ref[...], lb_ref[...],
                         tuple(w_ref[g] for g in range(G)), tuple(b_ref[g] for g in range(G)))
        du, dv, dz, dlw, dlb, dW, dbb = vjp(do_ref[...])
        dW, dbb = jnp.stack(dW, axis=0), jnp.stack(dbb, axis=0)
        dp_ref[:, 0:B] = du.astype(BF16)
        dp_ref[:, B:2 * B] = dv.astype(BF16)
        dp_ref[:, 2 * B:3 * B] = dz.astype(BF16)
        i = pl.program_id(0)

        @pl.when(i == 0)
        def _():
            dlw_ref[...] = dlw
            dlb_ref[...] = dlb
            dw_ref[...] = dW
            dbb_ref[...] = dbb

        @pl.when(i > 0)
        def _():
            dlw_ref[...] += dlw
            dlb_ref[...] += dlb
            dw_ref[...] += dW
            dbb_ref[...] += dbb

        @pl.when(i == nt - 1)
        def _():
            db_ref[...] = jnp.sum(dbb_ref[...], axis=-1, keepdims=True)

    row = pl.BlockSpec((1, B), lambda i: (0, 0))
    cube = pl.BlockSpec((G, CHUNK_B, CHUNK_B), lambda i: (0, 0, 0))
    return pl.pallas_call(
        body, name="sgu_bwd", grid=(nt,),
        in_specs=[pl.BlockSpec((tm, B), lambda i: (i, cu)), pl.BlockSpec((tm, B), lambda i: (i, cv)),
                  pl.BlockSpec((tm, B), lambda i: (i, cz)), row, row, cube, cube,
                  pl.BlockSpec((tm, B), lambda i: (i, A // B)), ANY],
        out_specs=[pl.BlockSpec((tm, 3 * B), lambda i: (i, 1)), row, row, cube,
                   pl.BlockSpec((G, CHUNK_B, 1), lambda i: (0, 0, 0))],
        out_shape=[jax.ShapeDtypeStruct(dproj.shape, dproj.dtype), jax.ShapeDtypeStruct((1, B), F32),
                   jax.ShapeDtypeStruct((1, B), F32), jax.ShapeDtypeStruct((G, CHUNK_B, CHUNK_B), F32),
                   jax.ShapeDtypeStruct((G, CHUNK_B, 1), F32)],
        input_output_aliases={8: 0},
        scratch_shapes=[pltpu.VMEM((G, CHUNK_B, CHUNK_B), F32)],
        compiler_params=_cparams(("arbitrary",)),
    )(proj_m, proj_m, proj_m, lw, lb, W, bbc, d_o, dproj)


def _head_fn(mix, x, fw, tgt):
    h = x + mix
    y = _rms_fn(h, fw)
    e = y - tgt
    return 0.5 * jnp.sum(jnp.mean(e * e, axis=-1, keepdims=True), axis=0, keepdims=True)


def _out_proj_loss(oa, ob, wout, x, tgt, fw):
    T, A = oa.shape
    B = ob.shape[1]
    D = x.shape[1]
    tm = _pick(T, (256, 128))

    def body(oa_ref, ob_ref, w_ref, x_ref, t_ref, fw_ref, dh_ref, dhb_ref, loss_ref, dfw_ref):
        mix = _dot(oa_ref[...], w_ref[0:A, :]) + _dot(ob_ref[...], w_ref[A:A + B, :])
        xv, tv = x_ref[...], t_ref[...]
        loss, vjp = jax.vjp(lambda m, f: _head_fn(m, xv, f, tv), mix, fw_ref[...])
        dh, dfw = vjp(jnp.ones((1, 1), F32))
        dh_ref[...] = dh
        dhb_ref[...] = dh.astype(BF16)
        lrow = jnp.broadcast_to(loss, (1, LANES))

        @pl.when(pl.program_id(0) == 0)
        def _():
            loss_ref[...] = lrow
            dfw_ref[...] = dfw

        @pl.when(pl.program_id(0) > 0)
        def _():
            loss_ref[...] += lrow
            dfw_ref[...] += dfw

    tile = pl.BlockSpec((tm, D), lambda i: (i, 0))
    return pl.pallas_call(
        body, name="out_proj_loss", grid=(T // tm,),
        in_specs=[pl.BlockSpec((tm, A), lambda i: (i, 0)), pl.BlockSpec((tm, B), lambda i: (i, 0)),
                  pl.BlockSpec((A + B, D), lambda i: (0, 0)), tile, tile,
                  pl.BlockSpec((1, D), lambda i: (0, 0))],
        out_specs=[tile, tile, pl.BlockSpec((1, LANES), lambda i: (0, 0)),
                   pl.BlockSpec((1, D), lambda i: (0, 0))],
        out_shape=[jax.ShapeDtypeStruct((T, D), F32), jax.ShapeDtypeStruct((T, D), BF16),
                   jax.ShapeDtypeStruct((1, LANES), F32), jax.ShapeDtypeStruct((1, D), F32)],
        compiler_params=_cparams(("arbitrary",)),
    )(oa, ob, wout, x, tgt, fw)


def _adamw(w, g, m, v, name):
    R, Cn = w.shape
    cap = max(8, 512 * 1024 // Cn)
    tr = max(t for t in range(8, min(R, cap) + 1, 8) if R % t == 0) if R > cap else R

    def body(w_ref, g_ref, m_ref, v_ref, d_ref, mo_ref, vo_ref):
        g = g_ref[...]
        m = ADAM_B1 * m_ref[...] + (1.0 - ADAM_B1) * g
        v = ADAM_B2 * v_ref[...] + (1.0 - ADAM_B2) * jnp.square(g)
        m_hat = m / (1.0 - ADAM_B1 ** ADAM_STEP)
        v_hat = v / (1.0 - ADAM_B2 ** ADAM_STEP)
        d_ref[...] = -ADAM_LR * (m_hat / (jnp.sqrt(v_hat) + ADAM_EPS) + ADAM_WD * w_ref[...])
        mo_ref[...] = m
        vo_ref[...] = v

    tile = pl.BlockSpec((tr, Cn), lambda i: (i, 0))
    shape = jax.ShapeDtypeStruct((R, Cn), F32)
    return pl.pallas_call(
        body, name=name, grid=(R // tr,), in_specs=[tile] * 4, out_specs=[tile] * 3,
        out_shape=[shape] * 3, compiler_params=_cparams(("parallel",)),
    )(w, g, m, v)


def _place():
    x, y, c = lax.axis_index("x"), lax.axis_index("y"), lax.axis_index("c")
    others = [(1 - x, y), (x, 1 - y), (1 - x, 1 - y)]
    return x, y, c, others


def _chip_index(px, py):
    return 2 * px + py


ANY = pl.BlockSpec(memory_space=pl.ANY)


def _gather_ride(blocks, split):
    n = len(blocks)

    def plan(in_refs, out_refs, send_sems, recv_sems):
        x, y, c, _ = _place()
        me, kx, ky, kd = (_chip_index(px, py) for px, py in ((x, y), (1 - x, y), (x, 1 - y), (1 - x, 1 - y)))
        to_x, to_y, to_s = (1 - x, y, c), (x, 1 - y, c), (x, y, 1 - c)

        def copy(sem, src, dst, to):
            return pltpu.make_async_remote_copy(src_ref=src, dst_ref=dst, send_sem=send_sems.at[sem],
                                                recv_sem=recv_sems.at[sem], device_id=to, device_id_type=MESH_ID)

        first, second, third, awaited = [], [], [], []
        for a in range(n):
            out, s0 = out_refs[a], 8 * a
            if not split[a]:
                for j, (k, to) in enumerate(((kx, to_x), (ky, to_y), (kd, (1 - x, 1 - y, c)))):
                    first.append(lambda j=j, to=to, a=a, out=out, s0=s0: copy(s0 + j, in_refs[a], out.at[me], to))
                    awaited.append((lambda j=j, k=k, to=to, out=out, s0=s0: copy(s0 + j, out.at[k], out.at[k], to),
                                    None))
                continue
            h = blocks[a].shape[0] // 2
            q = h // 2
            half = lambda k, core, out=out, h=h: out.at[k, pl.ds(core * h, h), :]
            quarter = lambda k, core, i, out=out, h=h, q=q: out.at[k, pl.ds(core * h + i * q, q), :]
            mine = in_refs[a].at[pl.ds(c * h, h), :]
            first.append(lambda s0=s0, mine=mine, half=half: copy(s0, mine, half(me, c), to_x))
            first.append(lambda s0=s0, mine=mine, half=half: copy(s0 + 1, mine, half(me, c), to_y))
            fwd0 = lambda s0=s0, quarter=quarter: copy(s0 + 2, quarter(kx, c, 0), quarter(kx, c, 0), to_y)
            fwd1 = lambda s0=s0, quarter=quarter: copy(s0 + 3, quarter(ky, c, 1), quarter(ky, c, 1), to_x)
            pieces = [(s0 + 0, lambda half=half: half(kx, c), lambda half=half: half(kx, 1 - c), to_x, fwd0),
                      (s0 + 1, lambda half=half: half(ky, c), lambda half=half: half(ky, 1 - c), to_y, fwd1),
                      (s0 + 2, lambda quarter=quarter: quarter(kd, c, 0), lambda quarter=quarter: quarter(kd, 1 - c, 0),
                       to_y, None),
                      (s0 + 3, lambda quarter=quarter: quarter(kd, c, 1), lambda quarter=quarter: quarter(kd, 1 - c, 1),
                       to_x, None)]
            for i, (sem, here, there, frm, fwd) in enumerate(pieces):
                passing = lambda s0=s0, i=i, here=here: copy(s0 + 4 + i, here(), here(), to_s)
                awaited.append((lambda sem=sem, here=here, frm=frm: copy(sem, here(), here(), frm), (fwd, passing)))
                if fwd is not None:
                    second.append(fwd)
                third.append((passing, lambda s0=s0, i=i, there=there: copy(s0 + 4 + i, there(), there(), to_s)))
        return first, second, third, awaited

    def start(*refs):
        for send in plan(*refs)[0]:
            send().start()

    def finish(*refs):
        first, second, third, awaited = plan(*refs)
        for arrival, then in awaited:
            arrival().wait_recv()
            for nxt in (then or ()):
                if nxt is not None:
                    nxt().start()
        for _, from_sibling in third:
            from_sibling().wait_recv()
        for send in first + second + [p for p, _ in third]:
            send().wait_send()

    shapes = [jax.ShapeDtypeStruct((N_CHIPS,) + b.shape, b.dtype) for b in blocks]
    return _Ride(blocks, shapes, 8 * n, start, finish)


def _put_own(gathered, own):
    me = _chip_index(lax.axis_index("x"), lax.axis_index("y"))
    return lax.dynamic_update_index_in_dim(gathered, own, me, 0)


def _allreduce_small(buf):
    R, L = buf.shape

    def body(in_ref, out_ref, sib_ref, pair_ref, chips_ref, send_sems, recv_sems):
        x, y, c, others = _place()
        me = _chip_index(x, y)
        sibling = (x, y, 1 - c)
        cp = pltpu.make_async_remote_copy(src_ref=in_ref, dst_ref=sib_ref, send_sem=send_sems.at[0],
                                          recv_sem=recv_sems.at[0], device_id=sibling, device_id_type=MESH_ID)
        cp.start()
        cp.wait()
        pair_ref[...] = in_ref[...] + sib_ref[...]
        sends = []
        for j, chip in enumerate(others):
            s = pltpu.make_async_remote_copy(src_ref=pair_ref, dst_ref=chips_ref.at[me],
                                             send_sem=send_sems.at[1 + j], recv_sem=recv_sems.at[1 + j],
                                             device_id=(*chip, c), device_id_type=MESH_ID)
            s.start()
            sends.append(s)
        chips_ref[me] = pair_ref[...]
        for j, chip in enumerate(others):
            k = _chip_index(*chip)
            pltpu.make_async_remote_copy(src_ref=pair_ref, dst_ref=chips_ref.at[k], send_sem=send_sems.at[1 + j],
                                         recv_sem=recv_sems.at[1 + j], device_id=(*chip, c),
                                         device_id_type=MESH_ID).wait_recv()
        for s in sends:
            s.wait_send()
        out_ref[...] = ((chips_ref[0] + chips_ref[1]) + chips_ref[2]) + chips_ref[3]

    vm = pl.BlockSpec(memory_space=pltpu.VMEM)
    return pl.pallas_call(
        body, name="allreduce_small", in_specs=[vm], out_specs=vm,
        out_shape=jax.ShapeDtypeStruct((R, L), F32),
        scratch_shapes=[pltpu.VMEM((R, L), F32), pltpu.VMEM((R, L), F32), pltpu.VMEM((N_CHIPS, R, L), F32),
                        pltpu.SemaphoreType.DMA((4,)), pltpu.SemaphoreType.DMA((4,))],
        compiler_params=pltpu.CompilerParams(vmem_limit_bytes=VMEM_LIMIT),
    )(buf)


def _pair_ride(g):
    nb, R, Cn = g.shape
    h = R // 2

    def copy(in_refs, out_refs, send_sems, recv_sems):
        x, y, c, _ = _place()
        return pltpu.make_async_remote_copy(src_ref=in_refs[0].at[:, pl.ds((1 - c) * h, h), :], dst_ref=out_refs[0],
                                            send_sem=send_sems.at[0], recv_sem=recv_sems.at[0],
                                            device_id=(x, y, 1 - c), device_id_type=MESH_ID)

    return _Ride([g], [jax.ShapeDtypeStruct((nb, h, Cn), g.dtype)], 1,
                 lambda *refs: copy(*refs).start(), lambda *refs: copy(*refs).wait())


def _pair_sum(g, land, c_arr, name, ride=None):
    nb, R, Cn = g.shape
    hr = R // 2
    tr = _pick(hr, (256, 128, 64, 32, 16))
    nt = hr // tr

    def body(c_ref, g_ref, l_ref, o_ref):
        o_ref[...] = (g_ref[...] + l_ref[...]).astype(BF16)

    return _pallas(
        body, (c_arr, g, land), name=name, prefetch=1, grid=(nb, nt),
        in_specs=[pl.BlockSpec((1, tr, Cn), lambda b, i, c_ref: (b, c_ref[0] * nt + i, 0)),
                  pl.BlockSpec((1, tr, Cn), lambda b, i, c_ref: (b, i, 0))],
        out_specs=pl.BlockSpec((1, tr, Cn), lambda b, i, c_ref: (b, i, 0)),
        out_shape=jax.ShapeDtypeStruct((nb, hr, Cn), BF16),
        semantics=("parallel", "parallel"), ride=ride)


def _chip_ride(parts):
    m = len(parts)

    def copies(in_refs, out_refs, send_sems, recv_sems):
        x, y, c, others = _place()
        me = _chip_index(x, y)
        def mk(j, chip, n, landing):
            k = _chip_index(*chip)
            return pltpu.make_async_remote_copy(
                src_ref=in_refs[n].at[k], dst_ref=out_refs[n].at[landing(k)], send_sem=send_sems.at[m * j + n],
                recv_sem=recv_sems.at[m * j + n], device_id=(*chip, c), device_id_type=MESH_ID)

        pairs = [(j, chip, n) for j, chip in enumerate(others) for n in range(m)]
        return pairs, (lambda *p: mk(*p, lambda k: me)), (lambda *p: mk(*p, lambda k: k))

    def start(*refs):
        pairs, send, _ = copies(*refs)
        for p in pairs:
            send(*p).start()

    def finish(*refs):
        pairs, send, arrival = copies(*refs)
        for p in pairs:
            arrival(*p).wait_recv()
        for p in pairs:
            send(*p).wait_send()

    return _Ride(parts, [jax.ShapeDtypeStruct(p.shape, p.dtype) for p in parts], 3 * m, start, finish)


def _put_own_slot(q, p):
    me = _chip_index(lax.axis_index("x"), lax.axis_index("y"))
    return lax.dynamic_update_index_in_dim(q, lax.dynamic_index_in_dim(p, me, 0, keepdims=False), me, 0)


def _chip_sum(q, c_arr, name):
    nb, hr, Cn = q.shape
    tr = _pick(hr, (256, 128, 64, 32, 16))
    nt = hr // tr

    def body(c_ref, q_ref, o_ref):
        f = lambda k: q_ref[k].astype(F32)
        o_ref[...] = ((f(0) + f(1)) + f(2)) + f(3)

    return _pallas(
        body, (c_arr, q), name=name, prefetch=1, grid=(nt,),
        in_specs=[pl.BlockSpec((nb, tr, Cn), lambda i, c_ref: (0, i, 0))],
        out_specs=pl.BlockSpec((tr, Cn), lambda i, c_ref: (c_ref[0] * nt + i, 0)),
        out_shape=jax.ShapeDtypeStruct((2 * hr, Cn), F32),
        semantics=("parallel",))


def _sibling_fill(fw, fo):
    def body(_, __, fw_ref, fo_ref, send_sems, recv_sems):
        x, y, c, _ = _place()
        copies = []
        for n, ref in enumerate((fw_ref, fo_ref)):
            h = ref.shape[0] // 2
            mine = ref.at[pl.ds(c * h, h), :]
            theirs = ref.at[pl.ds((1 - c) * h, h), :]
            mk = lambda src, dst: pltpu.make_async_remote_copy(
                src_ref=src, dst_ref=dst, send_sem=send_sems.at[n], recv_sem=recv_sems.at[n],
                device_id=(x, y, 1 - c), device_id_type=MESH_ID)
            send = mk(mine, mine)
            send.start()
            copies.append((send, mk(theirs, theirs)))
        for send, arrival in copies:
            arrival.wait_recv()
            send.wait_send()

    return pl.pallas_call(
        body, name="sibling_fill", in_specs=[ANY, ANY], out_specs=[ANY, ANY],
        out_shape=[jax.ShapeDtypeStruct(fw.shape, F32), jax.ShapeDtypeStruct(fo.shape, F32)],
        input_output_aliases={0: 0, 1: 1},
        scratch_shapes=[pltpu.SemaphoreType.DMA((2,)), pltpu.SemaphoreType.DMA((2,))],
        compiler_params=pltpu.CompilerParams(has_side_effects=True),
    )(fw, fo)


class _Layout:
    def __init__(self, H, G, nb, Cb):
        A, B = H * HEAD_DIM, G * HEAD_DIM
        self.n_main = 4 * A + 3 * B
        self.k = -(-(self.n_main + LANES) // WIN_BLOCK) * WIN_BLOCK
        cuts = [0, 3 * A, 4 * A, 4 * A + 2 * H, nb * Cb]
        starts = [0, 3 * A + 3 * B, self.n_main, 3 * A]
        self.pieces = []
        self.windows, self.runs = [], []
        for n in range(nb):
            segs = []
            for s in range(4):
                lo, hi = max(cuts[s], n * Cb), min(cuts[s + 1], (n + 1) * Cb)
                if lo < hi:
                    segs.append((starts[s] + lo - cuts[s], lo - n * Cb, hi - lo))
            self.pieces += [(own, n, col, ln) for own, col, ln in segs]
            blocks = sorted({b for own, _, ln in segs for b in range(own // WIN_BLOCK, (own + ln - 1) // WIN_BLOCK + 1)})
            self.windows.append(blocks)
            self.runs.append([(blocks.index(own // WIN_BLOCK) * WIN_BLOCK + own % WIN_BLOCK, ln)
                              for own, _, ln in segs])
        self.wb = max(len(b) for b in self.windows)
        self.table = [b + [b[-1]] * (self.wb - len(b)) for b in self.windows]
        self.pieces.sort()

    def to_own_order(self, g_in):
        D = g_in.shape[1]
        cols, at = [], 0
        for own, n, col, ln in self.pieces:
            if own > at:
                cols.append(jnp.zeros((D, own - at), g_in.dtype))
            cols.append(g_in[n, :, col:col + ln])
            at = own + ln
        if at < self.k:
            cols.append(jnp.zeros((D, self.k - at), g_in.dtype))
        return jnp.concatenate(cols, axis=1)

    def from_window(self, win, chip, Cb):
        pick = lambda runs: (lambda w: jnp.concatenate([w[:, c:c + ln] for c, ln in runs], axis=1))
        return lax.switch(chip, [pick(r) for r in self.runs], win)


def _device_step(x, tgt, norm_w, win_b, wout_b, conv_b, a_log, dt_bias, head_norm_w, sgu_ln_w, sgu_ln_b,
                 w_spatial, b_spatial, final_norm_w, c_arr):
    T, D = x.shape
    H = a_log.shape[1]
    A = H * HEAD_DIM
    G = w_spatial.shape[0]
    B = G * HEAD_DIM
    nb, Cb, Rb = N_CHIPS, win_b.shape[1], wout_b.shape[0]
    lay = _Layout(H, G, nb, Cb)
    alog_row = jnp.pad(a_log, ((0, 0), (H, LANES - 2 * H)))
    dtb_row = jnp.pad(dt_bias, ((0, 0), (H, LANES - 2 * H)))
    bbc = jnp.broadcast_to(b_spatial[:, :, None], (G, CHUNK_B, CHUNK_B))

    (xn, xn_t), (g_in,) = _rms_in(x, norm_w, ride=_gather_ride([win_b], [True]))
    w_own = lay.to_own_order(_put_own(g_in, win_b))
    proj_m, (g_out, g_conv) = _mm_nn(xn, w_own, F32, "in_proj", cols=(0, lay.n_main),
                                     ride=_gather_ride([wout_b, conv_b], [False, False]))
    wout = _put_own(g_out, wout_b).reshape(nb * Rb, D)
    conv_w = _put_own(g_conv, conv_b).transpose(1, 0, 2).reshape(CONV_WIDTH, nb * conv_b.shape[1])
    proj_ba = _mm_nn(xn, w_own, F32, "in_proj_ba", cols=(lay.n_main, LANES))
    q, k, v, gb, bb = _gdn_pre(proj_m, proj_ba, conv_w, alog_row, dtb_row, H)
    u, w, qg, kd, attn, eg, pinv = _gdn_prep(q, k, v, gb, bb)
    og, sall = _gdn_chain(qg, kd, u, w, attn, eg)
    oa, oa_t = _gdn_post(og, proj_m, head_norm_w)
    ob, ob_t = _sgu_fwd(proj_m, sgu_ln_w, sgu_ln_b, w_spatial, bbc, A)
    dh, dhb, loss_row, d_fnw = _out_proj_loss(oa, ob, wout, x, tgt, final_norm_w.reshape(1, D))

    d_o = _mm_nn(dhb, wout.T, F32, "out_proj_dx")
    dproj = lax.empty((T, lay.k), BF16)
    dproj, d_lw, d_lb, d_ws, d_bs = _sgu_bwd(proj_m, sgu_ln_w, sgu_ln_b, w_spatial, bbc, d_o, A, dproj)
    dog, dproj, d_hw = _gdn_post_bwd(og, proj_m, head_norm_w, d_o, dproj)
    dqg, dkd, du, dw, dat, deg = _gdn_chain_bwd(qg, kd, u, w, attn, eg, sall, dog)
    dq, dk, dv, dgb, dbb = _gdn_prep_bwd(q, k, v, gb, bb, pinv, du, dw, dqg, dkd, dat, deg)
    dc, dproj, d_al, d_dt = _gdn_pre_bwd(proj_m, proj_ba, conv_w, alog_row, dtb_row, dq, dk, dv, dgb, dbb, H,
                                         dproj)
    dproj, d_conv = _conv_bwd(proj_m, dc, conv_w, H, dproj)

    table = jnp.array([b for row in lay.table for b in row], jnp.int32)
    d_win = _mm_windows(xn_t, dproj, table, nb, "in_proj_dw")
    d_wout, (land_w,) = _mm_nn_pair(oa_t, ob_t, dhb, "out_proj_dw", ride=_pair_ride(d_win))
    d_wout = d_wout.reshape(nb, Rb, D)
    pair_w, (land_o,) = _pair_sum(d_win, land_w, c_arr, "pair_sum_w_in", ride=_pair_ride(d_wout))
    pair_o = _pair_sum(d_wout, land_o, c_arr, "pair_sum_w_out")
    dxn, (all_w,) = _mm_nt_rhs_outer(dproj, w_own, F32, "in_proj_dx", ride=_chip_ride([pair_w]))
    (grad_x, d_nw), (all_o,) = _rms_in_bwd(x, norm_w, dxn, dh, ride=_chip_ride([pair_o]))
    all_w, all_o = _put_own_slot(all_w, pair_w), _put_own_slot(all_o, pair_o)
    small = dict(norm_w=d_nw, conv_w=d_conv[:CONV_WIDTH], a_log=d_al[:, H:2 * H], dt_bias=d_dt[:, H:2 * H],
                 head_norm_w=d_hw, sgu_ln_w=d_lw, sgu_ln_b=d_lb, w_spatial=d_ws, b_spatial=d_bs[:, :, 0],
                 final_norm_w=d_fnw)
    return loss_row, grad_x, small, all_w, all_o


SMALL = ("norm_w", "conv_w", "a_log", "dt_bias", "head_norm_w", "sgu_ln_w", "sgu_ln_b", "w_spatial",
         "b_spatial", "final_norm_w")


def _pack(parts):
    rows = []
    for p in parts:
        f = p.reshape(-1)
        f = jnp.pad(f, (0, (-f.shape[0]) % (8 * LANES)))
        rows.append(f.reshape(-1, LANES))
    return jnp.concatenate(rows, axis=0)


def _unpack(buf, shapes):
    out, r = [], 0
    for s in shapes:
        n = 1
        for d in s:
            n *= d
        nr = -(-n // (8 * LANES)) * 8
        out.append(buf[r:r + nr].reshape(-1)[:n].reshape(s))
        r += nr
    return out


def kernel(x, norm_w, w_in, conv_w, a_log, dt_bias, head_norm_w, sgu_ln_w, sgu_ln_b, w_spatial, b_spatial, w_out, final_norm_w, loss_target, m_norm_w, m_w_in, m_conv_w, m_a_log, m_dt_bias, m_head_norm_w, m_sgu_ln_w, m_sgu_ln_b, m_w_spatial, m_b_spatial, m_w_out, m_final_norm_w, v_norm_w, v_w_in, v_conv_w, v_a_log, v_dt_bias, v_head_norm_w, v_sgu_ln_w, v_sgu_ln_b, v_w_spatial, v_b_spatial, v_w_out, v_final_norm_w):
    T, D = x.shape[1], x.shape[2]
    weights = dict(norm_w=norm_w, w_in=w_in, conv_w=conv_w, a_log=a_log, dt_bias=dt_bias, head_norm_w=head_norm_w,
                   sgu_ln_w=sgu_ln_w, sgu_ln_b=sgu_ln_b, w_spatial=w_spatial, b_spatial=b_spatial, w_out=w_out,
                   final_norm_w=final_norm_w)
    mom_m = dict(norm_w=m_norm_w, w_in=m_w_in, conv_w=m_conv_w, a_log=m_a_log, dt_bias=m_dt_bias,
                 head_norm_w=m_head_norm_w, sgu_ln_w=m_sgu_ln_w, sgu_ln_b=m_sgu_ln_b, w_spatial=m_w_spatial,
                 b_spatial=m_b_spatial, w_out=m_w_out, final_norm_w=m_final_norm_w)
    mom_v = dict(norm_w=v_norm_w, w_in=v_w_in, conv_w=v_conv_w, a_log=v_a_log, dt_bias=v_dt_bias,
                 head_norm_w=v_head_norm_w, sgu_ln_w=v_sgu_ln_w, sgu_ln_b=v_sgu_ln_b, w_spatial=v_w_spatial,
                 b_spatial=v_b_spatial, w_out=v_w_out, final_norm_w=v_final_norm_w)
    me = _chip_index(lax.axis_index("x"), lax.axis_index("y"))
    c_arr = lax.axis_index("c").astype(jnp.int32).reshape(1)
    Din, Cb = w_in.shape[1], w_in.shape[2]
    Rb = w_out.shape[1]
    cconv = conv_w.shape[2]

    loss_row, grad_x, g, qw, qo = _device_step(
        x[0], loss_target[0], norm_w, w_in[0].astype(BF16), w_out[0].astype(BF16), conv_w[0], a_log, dt_bias,
        head_norm_w, sgu_ln_w, sgu_ln_b, w_spatial[0], b_spatial[0], final_norm_w, c_arr)

    small_shapes = [tuple(g[n].shape) for n in SMALL]
    small = _allreduce_small(_pack([g[n] for n in SMALL]))
    gsum_in, gsum_out = _sibling_fill(_chip_sum(qw, c_arr, "chip_sum_w_in"), _chip_sum(qo, c_arr, "chip_sum_w_out"))
    gsum_in = _Layout(a_log.shape[1], w_spatial.shape[1], N_CHIPS, Cb).from_window(gsum_in, me, Cb)
    gsmall = dict(zip(SMALL, _unpack(small, small_shapes)))
    gsmall["conv_w"] = lax.dynamic_slice_in_dim(gsmall["conv_w"], me * cconv, cconv, axis=1)

    grads, deltas, new_m, new_v = {}, {}, {}, {}
    d, m2, v2 = _adamw(w_out[0], gsum_out, m_w_out[0], v_w_out[0], "adamw_w_out")
    grads["w_out"], deltas["w_out"], new_m["w_out"], new_v["w_out"] = gsum_out[None], d[None], m2[None], v2[None]
    flat = lambda a: a.transpose(2, 0, 1).reshape(-1, LANES)
    unflat = lambda f: f.reshape(Cb, 1, Din).transpose(1, 2, 0)
    g_flat = gsum_in.T.reshape(-1, LANES)
    d, m2, v2 = _adamw(flat(w_in), g_flat, flat(m_w_in), flat(v_w_in), "adamw_w_in")
    grads["w_in"], deltas["w_in"], new_m["w_in"], new_v["w_in"] = unflat(g_flat), unflat(d), unflat(m2), unflat(v2)
    shapes = [tuple(weights[n].shape) for n in SMALL]
    ds, ms, vs = _adamw(_pack([weights[n] for n in SMALL]), _pack([gsmall[n] for n in SMALL]),
                        _pack([mom_m[n] for n in SMALL]), _pack([mom_v[n] for n in SMALL]), "adamw_small")
    for n, gq, d, m2, v2 in zip(SMALL, [gsmall[n] for n in SMALL], _unpack(ds, shapes), _unpack(ms, shapes),
                                _unpack(vs, shapes)):
        grads[n], deltas[n], new_m[n], new_v[n] = gq.reshape(weights[n].shape), d, m2, v2

    loss = lax.psum(loss_row[0, 0], ("x", "y", "c"))
    order = ("norm_w", "w_in", "conv_w", "a_log", "dt_bias", "head_norm_w", "sgu_ln_w", "sgu_ln_b", "w_spatial",
             "b_spatial", "w_out", "final_norm_w")
    return (loss, grad_x[None], *[grads[n] for n in order], *[deltas[n] for n in order],
            *[new_m[n] for n in order], *[new_v[n] for n in order])
```

```python
import functools

import jax
import jax.numpy as jnp
from jax import lax
from jax.experimental import pallas as pl
from jax.experimental.pallas import tpu as pltpu

F32 = jnp.float32
BF16 = jnp.bfloat16
EPS = 1e-6
HEAD_DIM = 128
CHUNK_B = 128
CONV_WIDTH = 4
LANES = 128
HALO = 8
N_CHIPS = 4
ADAM_LR = 0.001
ADAM_B1 = 0.9
ADAM_B2 = 0.999
ADAM_EPS = 1e-08
ADAM_WD = 0.01
ADAM_STEP = 10
VMEM_LIMIT = 56 * 1024 * 1024
MESH_ID = pl.DeviceIdType.MESH
HI = lax.Precision.HIGHEST


def _cparams(sem=None, **kw):
    return pltpu.CompilerParams(dimension_semantics=sem, vmem_limit_bytes=VMEM_LIMIT, **kw)


def _matmul(a, b, ca, cb, precision):
    nb = a.ndim - 2
    batch = tuple(range(nb))
    return lax.dot_general(a, b, (((ca + nb,), (cb + nb,)), (batch, batch)), precision=precision,
                           preferred_element_type=F32)


def _dot(a, b, hi=False, precision=None):
    return _matmul(a, b, 1, 0, HI if hi else precision)


def _dot_nt(a, b, hi=False, precision=None):
    return _matmul(a, b, 1, 1, HI if hi else precision)


def _dot_tn(a, b, hi=False, precision=None):
    return _matmul(a, b, 0, 0, HI if hi else precision)


def _iota(shape, dim):
    return lax.broadcasted_iota(jnp.int32, shape, dim)


def _sigmoid(x):
    return 0.5 * (jnp.tanh(0.5 * x) + 1.0)


def _silu(x):
    return x * _sigmoid(x)


def _softplus(x):
    z = jnp.exp(-jnp.abs(x))
    small = z * (1.0 - z * (0.5 - z * (1.0 / 3.0)))
    return jnp.maximum(x, 0.0) + jnp.where(z < 1e-3, small, jnp.log(1.0 + z))


def _pick(n, pref):
    for t in pref:
        if n % t == 0:
            return t
    return n


class _Ride:
    def __init__(self, operands, out_shape, n_sems, start, finish):
        self.operands, self.out_shape, self.n_sems = list(operands), list(out_shape), n_sems
        self.start, self.finish = start, finish


def _pallas(body, operands, *, name, grid, in_specs, out_specs, out_shape, semantics, scratch_shapes=(),
            prefetch=0, ride=None):
    single = not isinstance(out_shape, (list, tuple))
    outs = [out_shape] if single else list(out_shape)
    ospecs = [out_specs] if single else list(out_specs)
    in_specs, scratch = list(in_specs), list(scratch_shapes)
    n_in, n_out, n_sc = len(operands) - prefetch, len(outs), len(scratch)
    kernel = body
    params = _cparams(semantics)
    if ride is not None:
        n_xin, n_xout = len(ride.operands), len(ride.out_shape)

        def kernel(*refs):
            pre, refs = refs[:prefetch], refs[prefetch:]
            ins, refs = refs[:n_in], refs[n_in:]
            xins, refs = refs[:n_xin], refs[n_xin:]
            mains, refs = refs[:n_out], refs[n_out:]
            xouts, refs = refs[:n_xout], refs[n_xout:]
            sc, (send, recv) = refs[:n_sc], refs[n_sc:]
            ids = [pl.program_id(a) for a in range(len(grid))]
            first = functools.reduce(jnp.logical_and, [i == 0 for i in ids])
            last = functools.reduce(jnp.logical_and, [i == g - 1 for i, g in zip(ids, grid)])

            @pl.when(first)
            def _():
                ride.start(xins, xouts, send, recv)

            body(*pre, *ins, *mains, *sc)

            @pl.when(last)
            def _():
                ride.finish(xins, xouts, send, recv)

        operands = list(operands) + ride.operands
        in_specs += [ANY] * n_xin
        ospecs += [ANY] * n_xout
        outs += ride.out_shape
        scratch += [pltpu.SemaphoreType.DMA((ride.n_sems,)), pltpu.SemaphoreType.DMA((ride.n_sems,))]
        params = _cparams(("arbitrary",) * len(grid), has_side_effects=True)
    if prefetch:
        spec = dict(grid_spec=pltpu.PrefetchScalarGridSpec(
            num_scalar_prefetch=prefetch, grid=grid, in_specs=in_specs, out_specs=ospecs, scratch_shapes=scratch))
    else:
        spec = dict(grid=grid, in_specs=in_specs, out_specs=ospecs, scratch_shapes=scratch)
    res = pl.pallas_call(kernel, name=name, out_shape=outs, compiler_params=params, **spec)(*operands)
    main = res[0] if single else list(res[:n_out])
    return main if ride is None else (main, list(res[n_out:]))


def _mm_nn(a, b, out_dtype, name, tm=1024, tn=512, tk=None, cols=None, ride=None):
    M, K = a.shape
    c0, N = (0, b.shape[1]) if cols is None else cols
    tm = _pick(M, (tm, 512, 256, 128))
    tn = _pick(N, (tn, 512, 384, 256, 128))
    tk = K if tk is None else _pick(K, (tk,))
    nk = K // tk
    j0 = c0 // tn
    assert c0 % tn == 0

    def body(a_ref, b_ref, o_ref, *scratch):
        part = _dot(a_ref[...], b_ref[...])
        if nk == 1:
            o_ref[...] = part.astype(out_dtype)
        else:
            acc_ref, = scratch
            k = pl.program_id(2)

            @pl.when(k == 0)
            def _():
                acc_ref[...] = part

            @pl.when(k > 0)
            def _():
                acc_ref[...] += part

            @pl.when(k == nk - 1)
            def _():
                o_ref[...] = acc_ref[...].astype(out_dtype)

    return _pallas(
        body, (a, b), name=name, grid=(M // tm, N // tn, nk),
        in_specs=[pl.BlockSpec((tm, tk), lambda i, j, k: (i, k)),
                  pl.BlockSpec((tk, tn), lambda i, j, k: (k, j + j0))],
        out_specs=pl.BlockSpec((tm, tn), lambda i, j, k: (i, j)),
        out_shape=jax.ShapeDtypeStruct((M, N), out_dtype),
        scratch_shapes=[] if nk == 1 else [pltpu.VMEM((tm, tn), F32)],
        semantics=("parallel", "parallel", "arbitrary"), ride=ride)


def _mm_nt_rhs_outer(a, b, out_dtype, name, tm=256, tn=1024, ride=None):
    M, K = a.shape
    N, _ = b.shape
    tm = _pick(M, (tm, 128))
    tn = _pick(N, (tn, 512, 256, 128))

    def body(a_ref, b_ref, o_ref):
        o_ref[...] = _dot_nt(a_ref[...], b_ref[...]).astype(out_dtype)

    return _pallas(
        body, (a, b), name=name, grid=(N // tn, M // tm),
        in_specs=[pl.BlockSpec((tm, K), lambda j, i: (i, 0)),
                  pl.BlockSpec((tn, K), lambda j, i: (j, 0))],
        out_specs=pl.BlockSpec((tm, tn), lambda j, i: (i, j)),
        out_shape=jax.ShapeDtypeStruct((M, N), out_dtype),
        semantics=("parallel", "parallel"), ride=ride)


WIN_BLOCK = 256


def _mm_windows(a, b, table, nb, name, tm=2048):
    M, K = a.shape
    wb = table.shape[0] // nb
    tm = _pick(M, (tm, 1024, 512, 256, 128))

    def body(tab_ref, a_ref, b_ref, o_ref):
        o_ref[0] = _dot(a_ref[...], b_ref[...]).astype(BF16)

    return pl.pallas_call(
        body, name=name,
        grid_spec=pltpu.PrefetchScalarGridSpec(
            num_scalar_prefetch=1, grid=(nb, M // tm, wb),
            in_specs=[pl.BlockSpec((tm, K), lambda n, i, t, tab: (i, 0)),
                      pl.BlockSpec((K, WIN_BLOCK), lambda n, i, t, tab: (0, tab[n * wb + t]))],
            out_specs=pl.BlockSpec((1, tm, WIN_BLOCK), lambda n, i, t, tab: (n, i, t))),
        out_shape=jax.ShapeDtypeStruct((nb, M, wb * WIN_BLOCK), BF16),
        compiler_params=_cparams(("parallel", "parallel", "arbitrary")),
    )(table, a, b)


def _mm_nn_pair(a0, a1, b, name, tm=512, tn=1024, ride=None):
    M, K = a0.shape
    _, N = b.shape
    tm = _pick(M, (tm, 256, 128))
    tn = _pick(N, (tn, 512, 256, 128))
    ni = M // tm

    def body(a0_ref, a1_ref, b_ref, o_ref):
        p = pl.program_id(0)

        @pl.when(p == 0)
        def _():
            o_ref[...] = _dot(a0_ref[...], b_ref[...]).astype(BF16)

        @pl.when(p == 1)
        def _():
            o_ref[...] = _dot(a1_ref[...], b_ref[...]).astype(BF16)

    return _pallas(
        body, (a0, a1, b), name=name, grid=(2, ni, N // tn),
        in_specs=[pl.BlockSpec((tm, K), lambda p, i, j: (i * (1 - p), 0)),
                  pl.BlockSpec((tm, K), lambda p, i, j: (i * p, 0)),
                  pl.BlockSpec((K, tn), lambda p, i, j: (0, j))],
        out_specs=pl.BlockSpec((tm, tn), lambda p, i, j: (p * ni + i, j)),
        out_shape=jax.ShapeDtypeStruct((2 * M, N), BF16),
        semantics=("parallel", "parallel", "parallel"), ride=ride)


def _rms_fn(x, w):
    r = lax.rsqrt(jnp.mean(x * x, axis=-1, keepdims=True) + EPS)
    return x * r * w


def _rms_in(x, w, ride=None):
    T, D = x.shape
    tm = _pick(T, (512, 256, 128))

    def body(x_ref, w_ref, o_ref, ot_ref):
        xn = _rms_fn(x_ref[...], w_ref[...])
        o_ref[...] = xn.astype(BF16)
        ot_ref[...] = xn.T.astype(BF16)

    return _pallas(
        body, (x, w), name="rms_in", grid=(T // tm,),
        in_specs=[pl.BlockSpec((tm, D), lambda i: (i, 0)), pl.BlockSpec((1, D), lambda i: (0, 0))],
        out_specs=[pl.BlockSpec((tm, D), lambda i: (i, 0)), pl.BlockSpec((D, tm), lambda i: (0, i))],
        out_shape=[jax.ShapeDtypeStruct((T, D), BF16), jax.ShapeDtypeStruct((D, T), BF16)],
        semantics=("parallel",), ride=ride)


def _rms_in_bwd(x, w, dxn, dh, ride=None):
    T, D = x.shape
    tm = _pick(T, (256, 128))

    def body(x_ref, w_ref, dxn_ref, dh_ref, gx_ref, dw_ref):
        _, vjp = jax.vjp(_rms_fn, x_ref[...], w_ref[...])
        dx, dw = vjp(dxn_ref[...])
        gx_ref[...] = dh_ref[...] + dx

        @pl.when(pl.program_id(0) == 0)
        def _():
            dw_ref[...] = dw

        @pl.when(pl.program_id(0) > 0)
        def _():
            dw_ref[...] += dw

    tile = pl.BlockSpec((tm, D), lambda i: (i, 0))
    row = pl.BlockSpec((1, D), lambda i: (0, 0))
    return _pallas(
        body, (x, w, dxn, dh), name="rms_in_bwd", grid=(T // tm,),
        in_specs=[tile, row, tile, tile], out_specs=[tile, row],
        out_shape=[jax.ShapeDtypeStruct((T, D), F32), jax.ShapeDtypeStruct((1, D), F32)],
        semantics=("arbitrary",), ride=ride)


def _conv_fwd(cat_ref, halo, x, w):
    tm = x.shape[0]
    cat_ref[0:HALO, :] = halo
    cat_ref[HALO:HALO + tm, :] = x
    c = x * w[CONV_WIDTH - 1:CONV_WIDTH, :]
    for k in range(CONV_WIDTH - 1):
        s = CONV_WIDTH - 1 - k
        c = c + cat_ref[pl.ds(HALO - s, tm), :] * w[k:k + 1, :]
    return c


def _lane_to_all(x, lane):
    @jax.custom_vjp
    def f(x):
        return jnp.broadcast_to(x[:, lane:lane + 1], x.shape)

    def f_fwd(x):
        return f(x), None

    def f_bwd(_, g):
        return (jnp.where(_iota(g.shape, 1) == lane, jnp.sum(g, axis=-1, keepdims=True), 0.0),)

    f.defvjp(f_fwd, f_bwd)
    return f(x)


def _gdn_pointwise(c, ba, alog, dtb, H):
    A = H * HEAD_DIM
    s = _silu(c)
    beta = _sigmoid(ba)
    g = -jnp.exp(alog) * _softplus(ba + dtb)
    qs, ks, vs, gbs, bbs = [], [], [], [], []
    for h in range(H):
        lo = h * HEAD_DIM
        q = s[:, lo:lo + HEAD_DIM]
        k = s[:, A + lo:A + lo + HEAD_DIM]
        qs.append(q * lax.rsqrt(jnp.sum(q * q, axis=-1, keepdims=True) + EPS))
        ks.append(k * lax.rsqrt(jnp.sum(k * k, axis=-1, keepdims=True) + EPS))
        vs.append(s[:, 2 * A + lo:2 * A + lo + HEAD_DIM])
        bbs.append(_lane_to_all(beta, h))
        gbs.append(_lane_to_all(g, H + h))
    st = lambda xs: jnp.stack(xs, axis=0)
    return st(qs), st(ks), st(vs), st(gbs), st(bbs)


def _halo_prev(tm):
    return lambda i: (jnp.maximum(i * (tm // HALO) - 1, 0), 0)


def _gdn_pre(proj_m, proj_ba, conv_w, alog_row, dtb_row, H):
    T = proj_m.shape[0]
    A = H * HEAD_DIM
    tm = _pick(T, (256, 128))
    hs = pl.BlockSpec((H, tm, HEAD_DIM), lambda i: (0, i, 0))
    hshape = jax.ShapeDtypeStruct((H, T, HEAD_DIM), F32)

    def body(x_ref, halo_ref, ba_ref, w_ref, al_ref, dt_ref, q_ref, k_ref, v_ref, gb_ref, bb_ref, cat_ref):
        halo = jnp.where(pl.program_id(0) == 0, 0.0, halo_ref[...])
        c = _conv_fwd(cat_ref, halo, x_ref[...], w_ref[...])
        q, k, v, gb, bb = _gdn_pointwise(c, ba_ref[...], al_ref[...], dt_ref[...], H)
        q_ref[...] = q
        k_ref[...] = k
        v_ref[...] = v
        gb_ref[...] = gb
        bb_ref[...] = bb

    return pl.pallas_call(
        body, name="gdn_pre", grid=(T // tm,),
        in_specs=[pl.BlockSpec((tm, 3 * A), lambda i: (i, 0)),
                  pl.BlockSpec((HALO, 3 * A), _halo_prev(tm)),
                  pl.BlockSpec((tm, LANES), lambda i: (i, 0)),
                  pl.BlockSpec((CONV_WIDTH, 3 * A), lambda i: (0, 0)),
                  pl.BlockSpec((1, LANES), lambda i: (0, 0)),
                  pl.BlockSpec((1, LANES), lambda i: (0, 0))],
        out_specs=[hs] * 5, out_shape=[hshape] * 5,
        scratch_shapes=[pltpu.VMEM((HALO + tm, 3 * A), F32)],
        compiler_params=_cparams(("parallel",)),
    )(proj_m, proj_m, proj_ba, conv_w, alog_row, dtb_row)


def _gdn_pre_bwd(proj_m, proj_ba, conv_w, alog_row, dtb_row, dq, dk, dv, dgb, dbb, H, dproj):
    T, n_main = proj_m.shape
    A = H * HEAD_DIM
    tm = _pick(T, (256, 128))
    hs = pl.BlockSpec((H, tm, HEAD_DIM), lambda i: (0, i, 0))
    row = pl.BlockSpec((1, LANES), lambda i: (0, 0))

    def body(x_ref, halo_ref, ba_ref, w_ref, al_ref, dt_ref, dq_ref, dk_ref, dv_ref, dgb_ref, dbb_ref, _,
             dc_ref, dba_ref, dal_ref, ddt_ref, cat_ref):
        halo = jnp.where(pl.program_id(0) == 0, 0.0, halo_ref[...])
        c = _conv_fwd(cat_ref, halo, x_ref[...], w_ref[...])
        _, vjp = jax.vjp(functools.partial(_gdn_pointwise, H=H), c, ba_ref[...], al_ref[...], dt_ref[...])
        dc, dba, dal, ddt = vjp((dq_ref[...], dk_ref[...], dv_ref[...], dgb_ref[...], dbb_ref[...]))
        dc_ref[...] = dc
        dba_ref[:, :LANES] = dba.astype(BF16)
        dba_ref[:, LANES:] = jnp.zeros((tm, WIN_BLOCK - LANES), BF16)

        @pl.when(pl.program_id(0) == 0)
        def _():
            dal_ref[...] = dal
            ddt_ref[...] = ddt

        @pl.when(pl.program_id(0) > 0)
        def _():
            dal_ref[...] += dal
            ddt_ref[...] += ddt

    return pl.pallas_call(
        body, name="gdn_pre_bwd", grid=(T // tm,),
        in_specs=[pl.BlockSpec((tm, 3 * A), lambda i: (i, 0)),
                  pl.BlockSpec((HALO, 3 * A), _halo_prev(tm)),
                  pl.BlockSpec((tm, LANES), lambda i: (i, 0)),
                  pl.BlockSpec((CONV_WIDTH, 3 * A), lambda i: (0, 0)),
                  row, row, hs, hs, hs, hs, hs, ANY],
        out_specs=[pl.BlockSpec((tm, 3 * A), lambda i: (i, 0)),
                   pl.BlockSpec((tm, WIN_BLOCK), lambda i: (i, n_main // WIN_BLOCK)), row, row],
        out_shape=[jax.ShapeDtypeStruct((T, 3 * A), F32), jax.ShapeDtypeStruct(dproj.shape, dproj.dtype),
                   jax.ShapeDtypeStruct((1, LANES), F32), jax.ShapeDtypeStruct((1, LANES), F32)],
        input_output_aliases={11: 1},
        scratch_shapes=[pltpu.VMEM((HALO + tm, 3 * A), F32)],
        compiler_params=_cparams(("arbitrary",)),
    )(proj_m, proj_m, proj_ba, conv_w, alog_row, dtb_row, dq, dk, dv, dgb, dbb, dproj)


def _conv_bwd(proj_m, dc, conv_w, H, dproj):
    T = proj_m.shape[0]
    A = H * HEAD_DIM
    tm = _pick(T, (256, 128))
    nt = T // tm

    def body(x_ref, halo_ref, dc_ref, nxt_ref, w_ref, _, dx_ref, dw_ref):
        i = pl.program_id(0)
        halo = jnp.where(i == 0, 0.0, halo_ref[...])
        xcat = jnp.concatenate([halo, x_ref[...]], axis=0)
        nxt = jnp.where(i == nt - 1, 0.0, nxt_ref[...])
        dc = dc_ref[...]
        dcat = jnp.concatenate([dc, nxt], axis=0)
        w = w_ref[...]
        dx = None
        rows = []
        for k in range(CONV_WIDTH):
            s = CONV_WIDTH - 1 - k
            ds = dcat if s == 0 else pltpu.roll(dcat, tm + HALO - s, 0)
            term = ds[:tm, :] * w[k:k + 1, :]
            dx = term if dx is None else dx + term
            xs = xcat if s == 0 else pltpu.roll(xcat, s, 0)
            rows.append(jnp.sum(dc * xs[HALO:, :], axis=0, keepdims=True))
        dx_ref[...] = dx.astype(BF16)
        dw = jnp.concatenate(rows + [jnp.zeros((HALO - CONV_WIDTH, 3 * A), F32)], axis=0)

        @pl.when(i == 0)
        def _():
            dw_ref[...] = dw

        @pl.when(i > 0)
        def _():
            dw_ref[...] += dw

    return pl.pallas_call(
        body, name="conv_bwd", grid=(nt,),
        in_specs=[pl.BlockSpec((tm, 3 * A), lambda i: (i, 0)),
                  pl.BlockSpec((HALO, 3 * A), _halo_prev(tm)),
                  pl.BlockSpec((tm, 3 * A), lambda i: (i, 0)),
                  pl.BlockSpec((HALO, 3 * A), lambda i: (jnp.minimum((i + 1) * (tm // HALO), T // HALO - 1), 0)),
                  pl.BlockSpec((CONV_WIDTH, 3 * A), lambda i: (0, 0)), ANY],
        out_specs=[pl.BlockSpec((tm, 3 * A), lambda i: (i, 0)),
                   pl.BlockSpec((HALO, 3 * A), lambda i: (0, 0))],
        out_shape=[jax.ShapeDtypeStruct(dproj.shape, dproj.dtype), jax.ShapeDtypeStruct((HALO, 3 * A), F32)],
        input_output_aliases={5: 0},
        compiler_params=_cparams(("arbitrary",)),
    )(proj_m, proj_m, dc, dc, conv_w, dproj)


CHUNK = 128
BLOCK = 64


def _b(x):
    return x.astype(BF16)


@jax.custom_vjp
def _bdot(a, b):
    return _dot(_b(a), _b(b))


def _bdot_f(a, b):
    return _bdot(a, b), (a, b)


def _bdot_b(res, g):
    a, b = res
    return _dot_nt(_b(g), _b(b)), _dot_tn(_b(a), _b(g))


_bdot.defvjp(_bdot_f, _bdot_b)


@jax.custom_vjp
def _bdot_nt(a, b):
    return _dot_nt(_b(a), _b(b))


def _bdot_nt_f(a, b):
    return _bdot_nt(a, b), (a, b)


def _bdot_nt_b(res, g):
    a, b = res
    return _dot(_b(g), _b(b)), _dot_tn(_b(g), _b(a))


_bdot_nt.defvjp(_bdot_nt_f, _bdot_nt_b)


@jax.custom_vjp
def _bdot_tn(a, b):
    return _dot_tn(_b(a), _b(b))


def _bdot_tn_f(a, b):
    return _bdot_tn(a, b), (a, b)


def _bdot_tn_b(res, g):
    a, b = res
    return _dot_nt(_b(b), _b(g)), _dot(_b(a), _b(g))


_bdot_tn.defvjp(_bdot_tn_f, _bdot_tn_b)


def _mask_matmul(m, x):
    hi = _b(x)
    r = x - hi.astype(F32)
    mid = _b(r)
    lo = _b(r - mid.astype(F32))
    return (_dot(m, lo) + _dot(m, mid)) + _dot(m, hi)


@jax.custom_vjp
def _mask_dot(m, mt, x):
    return _mask_matmul(m, x)


def _mask_dot_f(m, mt, x):
    return _mask_matmul(m, x), (m, mt)


def _mask_dot_b(res, g):
    m, mt = res
    return jnp.zeros_like(m), jnp.zeros_like(mt), _mask_matmul(mt, g)


_mask_dot.defvjp(_mask_dot_f, _mask_dot_b)

HIGH = lax.Precision.HIGH


def _unit_lower_inverse(L):
    n = L.shape[-1]
    X = -L
    Q = X
    for _ in range(BLOCK.bit_length() - 2):
        X = _dot(_b(X), _b(X))
        Q = Q + X + _dot(_b(Q), _b(X))
    return (_iota((n, n), 0) == _iota((n, n), 1)).astype(F32) + Q


@jax.custom_vjp
def _known_inverse(L, P):
    return P


def _known_inverse_f(L, P):
    return P, P


def _known_inverse_b(P, g):
    n = P.shape[-1]
    Q = _b(P - (_iota((n, n), 0) == _iota((n, n), 1)).astype(F32))
    t = g + _dot_tn(Q, _b(g))
    return -(t + _dot_nt(_b(t), Q)), jnp.zeros_like(P)


_known_inverse.defvjp(_known_inverse_f, _known_inverse_b)


def _gdn_prep_fn(q, k, v, gb, bb, P_known=None):
    n = CHUNK
    row, col = _iota((n, n), 0), _iota((n, n), 1)
    same = (row // BLOCK) == (col // BLOCK)
    incl, strict = same & (row >= col), same & (row > col)
    bc = lambda m: jnp.broadcast_to(_b(m.astype(F32)), q.shape[:1] + (n, n))
    tril, triu, ones = bc(incl), bc(same & (row <= col)), bc(same)
    gc = _mask_dot(tril, triu, gb)
    gl = _mask_dot(ones, ones, gb)
    decay = jnp.where(incl, jnp.exp(jnp.where(incl, gc - jnp.swapaxes(gc, 1, 2), 0.0)), 0.0)
    kb = k * bb
    vb = v * bb
    qs = q * (HEAD_DIM ** -0.5)
    L = jnp.where(strict, _bdot_nt(kb, k) * decay, 0.0)
    P = _unit_lower_inverse(L) if P_known is None else _known_inverse(L, P_known)
    egc = jnp.exp(gc)
    u = _bdot(P, vb)
    w = _bdot(P, kb * egc)
    attn = jnp.where(incl, _bdot_nt(qs, k) * decay, 0.0)
    qg = qs * egc
    kdec = k * jnp.exp(gl - gc)
    eg = jnp.exp(gl).reshape(-1, n // BLOCK, BLOCK, LANES).sum(axis=2) * (1.0 / BLOCK)
    if P_known is None:
        return u, w, qg, kdec, attn, eg, P
    return u, w, qg, kdec, attn, eg


def _gdn_chain_fn(S, qg, kdec, u, w, attn, eg):
    nblk = CHUNK // BLOCK
    cat = lambda xs: jnp.concatenate(xs, axis=1)
    outs, found = [], []
    for i in range(nblk):
        r = (slice(None), slice(i * BLOCK, (i + 1) * BLOCK))
        v_new = u[r] - _bdot(w[r], S)
        found.append(v_new)
        outs.append(_bdot(qg[r], S) + _bdot(attn[r], cat(found + [jnp.zeros_like(v_new)] * (nblk - 1 - i))))
        S = S * eg[i] + _bdot_tn(kdec[r], v_new)
    return cat(outs), S


def _eg_spec(H, T, chunks, index_map, per_head):
    nblk = CHUNK // BLOCK
    block = (chunks, 1 if per_head else H, nblk, LANES)
    return pl.BlockSpec(block, index_map), jax.ShapeDtypeStruct((T // CHUNK, H, nblk, LANES), F32)


def _gdn_prep(q, k, v, gb, bb):
    H, T, _ = q.shape
    pb = _pick(T // CHUNK, (8, 4, 2, 1))
    hs = pl.BlockSpec((1, CHUNK * pb, HEAD_DIM), lambda h, n: (h, n, 0))
    hshape = jax.ShapeDtypeStruct((H, T, HEAD_DIM), F32)

    def body(q_ref, k_ref, v_ref, gb_ref, bb_ref, *out_refs):
        chunks = lambda ref: ref[0].reshape(pb, CHUNK, HEAD_DIM)
        outs = _gdn_prep_fn(chunks(q_ref), chunks(k_ref), chunks(v_ref), chunks(gb_ref), chunks(bb_ref))
        for i, (ref, val) in enumerate(zip(out_refs, outs)):
            if i == 5:
                ref[:, 0] = val
            else:
                ref[0] = val.reshape(pb * CHUNK, HEAD_DIM).astype(ref.dtype)

    kept = [F32, BF16, BF16, BF16, BF16, None, BF16]
    es, eshape = _eg_spec(H, T, pb, lambda h, n: (n, h, 0, 0), per_head=True)
    return pl.pallas_call(
        body, name="gdn_prep", grid=(H, T // (CHUNK * pb)),
        in_specs=[hs] * 5, out_specs=[es if dt is None else hs for dt in kept],
        out_shape=[eshape if dt is None else jax.ShapeDtypeStruct((H, T, HEAD_DIM), dt) for dt in kept],
        compiler_params=_cparams(("parallel", "parallel")),
    )(q, k, v, gb, bb)


def _gdn_prep_bwd(q, k, v, gb, bb, pinv, du, dw, dqg, dkd, dat, deg):
    H, T, _ = q.shape
    pb = _pick(T // CHUNK, (8, 4, 2, 1))
    hs = pl.BlockSpec((1, CHUNK * pb, HEAD_DIM), lambda h, n: (h, n, 0))
    hshape = jax.ShapeDtypeStruct((H, T, HEAD_DIM), F32)

    def body(*refs):
        in_refs, p_ref, ct_refs, out_refs = refs[:5], refs[5], refs[6:12], refs[12:]
        chunks = lambda ref: ref[0].reshape(pb, CHUNK, HEAD_DIM)
        P = chunks(p_ref).astype(F32)
        _, vjp = jax.vjp(lambda *a: _gdn_prep_fn(*a, P_known=P), *[chunks(r) for r in in_refs])
        grads = vjp(tuple(chunks(r).astype(F32) for r in ct_refs[:5]) + (ct_refs[5][:, 0],))
        for ref, val in zip(out_refs, grads):
            ref[0] = val.reshape(pb * CHUNK, HEAD_DIM)

    es, _ = _eg_spec(H, T, pb, lambda h, n: (n, h, 0, 0), per_head=True)
    return pl.pallas_call(
        body, name="gdn_prep_bwd", grid=(H, T // (CHUNK * pb)),
        in_specs=[hs] * 11 + [es], out_specs=[hs] * 5, out_shape=[hshape] * 5,
        compiler_params=_cparams(("parallel", "parallel")),
    )(q, k, v, gb, bb, pinv, du, dw, dqg, dkd, dat, deg)


def _gdn_chain(qg, kd, u, w, attn, eg):
    H, T, _ = qg.shape
    N = T // CHUNK
    hs = pl.BlockSpec((H, CHUNK, HEAD_DIM), lambda n: (0, n, 0))
    ss = pl.BlockSpec((1, H, HEAD_DIM, HEAD_DIM), lambda n: (n, 0, 0, 0))

    def body(qg_ref, kd_ref, u_ref, w_ref, at_ref, eg_ref, o_ref, sall_ref, s_ref):
        @pl.when(pl.program_id(0) == 0)
        def _():
            s_ref[...] = jnp.zeros_like(s_ref)

        S = s_ref[...]
        sall_ref[0] = S
        eg = tuple(eg_ref[0, :, i:i + 1, :] for i in range(CHUNK // BLOCK))
        o, S2 = _gdn_chain_fn(S, qg_ref[...], kd_ref[...], u_ref[...], w_ref[...], at_ref[...], eg)
        o_ref[...] = o
        s_ref[...] = S2

    es, _ = _eg_spec(H, T, 1, lambda n: (n, 0, 0, 0), per_head=False)
    return pl.pallas_call(
        body, name="gdn_chain", grid=(N,),
        in_specs=[hs] * 5 + [es], out_specs=[hs, ss],
        out_shape=[jax.ShapeDtypeStruct((H, T, HEAD_DIM), F32),
                   jax.ShapeDtypeStruct((N, H, HEAD_DIM, HEAD_DIM), F32)],
        scratch_shapes=[pltpu.VMEM((H, HEAD_DIM, HEAD_DIM), F32)],
        compiler_params=_cparams(("arbitrary",)),
    )(qg, kd, u, w, attn, eg)


def _gdn_chain_bwd(qg, kd, u, w, attn, eg, sall, do):
    H, T, _ = qg.shape
    N = T // CHUNK
    hs = pl.BlockSpec((H, CHUNK, HEAD_DIM), lambda n: (0, N - 1 - n, 0))
    ss = pl.BlockSpec((1, H, HEAD_DIM, HEAD_DIM), lambda n: (N - 1 - n, 0, 0, 0))
    hshape = jax.ShapeDtypeStruct((H, T, HEAD_DIM), F32)

    def body(qg_ref, kd_ref, u_ref, w_ref, at_ref, eg_ref, sall_ref, do_ref, *rest):
        out_refs, ds_ref = rest[:6], rest[6]

        @pl.when(pl.program_id(0) == 0)
        def _():
            ds_ref[...] = jnp.zeros_like(ds_ref)

        f32 = lambda ref: ref[...].astype(F32)
        nblk = CHUNK // BLOCK
        eg = tuple(eg_ref[0, :, i:i + 1, :] for i in range(nblk))
        _, vjp = jax.vjp(_gdn_chain_fn, sall_ref[0], f32(qg_ref), f32(kd_ref), u_ref[...], f32(w_ref),
                         f32(at_ref), eg)
        grads = vjp((do_ref[...], ds_ref[...]))
        ds_ref[...] = grads[0]
        for ref, val in zip(out_refs[:5], grads[1:6]):
            ref[...] = val.astype(ref.dtype)
        for i in range(nblk):
            out_refs[5][0, :, i:i + 1, :] = grads[6][i]

    kept = [F32, F32, BF16, BF16, F32]
    es, eshape = _eg_spec(H, T, 1, lambda n: (N - 1 - n, 0, 0, 0), per_head=False)
    return pl.pallas_call(
        body, name="gdn_chain_bwd", grid=(N,),
        in_specs=[hs] * 5 + [es, ss, hs], out_specs=[hs] * 5 + [es],
        out_shape=[jax.ShapeDtypeStruct((H, T, HEAD_DIM), dt) for dt in kept] + [eshape],
        scratch_shapes=[pltpu.VMEM((H, HEAD_DIM, HEAD_DIM), F32)],
        compiler_params=_cparams(("arbitrary",)),
    )(qg, kd, u, w, attn, eg, sall, do)


def _post_fn(ogs, za, hw):
    outs = []
    for h, o in enumerate(ogs):
        r = lax.rsqrt(jnp.mean(o * o, axis=-1, keepdims=True) + EPS)
        outs.append(o * r * hw * _silu(za[:, h * HEAD_DIM:(h + 1) * HEAD_DIM]))
    return jnp.concatenate(outs, axis=1)


def _gdn_post(og, proj_m, hw):
    H, T, _ = og.shape
    A = H * HEAD_DIM
    tm = _pick(T, (512, 256, 128))

    def body(og_ref, za_ref, hw_ref, o_ref, ot_ref):
        o = _post_fn(tuple(og_ref[h] for h in range(H)), za_ref[...], hw_ref[...])
        o_ref[...] = o.astype(BF16)
        ot_ref[...] = o.T.astype(BF16)

    return pl.pallas_call(
        body, name="gdn_post", grid=(T // tm,),
        in_specs=[pl.BlockSpec((H, tm, HEAD_DIM), lambda i: (0, i, 0)),
                  pl.BlockSpec((tm, A), lambda i: (i, ZA_BLOCK)),
                  pl.BlockSpec((1, HEAD_DIM), lambda i: (0, 0))],
        out_specs=[pl.BlockSpec((tm, A), lambda i: (i, 0)), pl.BlockSpec((A, tm), lambda i: (0, i))],
        out_shape=[jax.ShapeDtypeStruct((T, A), BF16), jax.ShapeDtypeStruct((A, T), BF16)],
        compiler_params=_cparams(("parallel",)),
    )(og, proj_m, hw)


def _gdn_post_bwd(og, proj_m, hw, d_o, dproj):
    H, T, _ = og.shape
    A = H * HEAD_DIM
    tm = _pick(T, (256, 128))

    def body(og_ref, za_ref, hw_ref, do_ref, _, dog_ref, dza_ref, dhw_ref):
        _, vjp = jax.vjp(_post_fn, tuple(og_ref[h] for h in range(H)), za_ref[...], hw_ref[...])
        dog, dza, dhw = vjp(do_ref[...])
        for h in range(H):
            dog_ref[h] = dog[h]
        dza_ref[...] = dza.astype(BF16)

        @pl.when(pl.program_id(0) == 0)
        def _():
            dhw_ref[...] = dhw

        @pl.when(pl.program_id(0) > 0)
        def _():
            dhw_ref[...] += dhw

    return pl.pallas_call(
        body, name="gdn_post_bwd", grid=(T // tm,),
        in_specs=[pl.BlockSpec((H, tm, HEAD_DIM), lambda i: (0, i, 0)),
                  pl.BlockSpec((tm, A), lambda i: (i, ZA_BLOCK)),
                  pl.BlockSpec((1, HEAD_DIM), lambda i: (0, 0)),
                  pl.BlockSpec((tm, A), lambda i: (i, 0)), ANY],
        out_specs=[pl.BlockSpec((H, tm, HEAD_DIM), lambda i: (0, i, 0)),
                   pl.BlockSpec((tm, A), lambda i: (i, ZA_BLOCK)),
                   pl.BlockSpec((1, HEAD_DIM), lambda i: (0, 0))],
        out_shape=[jax.ShapeDtypeStruct((H, T, HEAD_DIM), F32), jax.ShapeDtypeStruct(dproj.shape, dproj.dtype),
                   jax.ShapeDtypeStruct((1, HEAD_DIM), F32)],
        input_output_aliases={4: 1},
        compiler_params=_cparams(("arbitrary",)),
    )(og, proj_m, hw, d_o, dproj)


def _sgu_fn(ub, vb, zb, lw, lb, W, bbc):
    G = len(W)
    tm = ub.shape[0]
    mu = jnp.mean(vb, axis=-1, keepdims=True)
    xc = vb - mu
    var = jnp.mean(xc * xc, axis=-1, keepdims=True)
    vn = xc * lax.rsqrt(var + EPS) * lw + lb
    mask = _iota((CHUNK_B, CHUNK_B), 0) >= _iota((CHUNK_B, CHUNK_B), 1)
    cols = []
    for g in range(G):
        wm = jnp.where(mask, W[g], 0.0).astype(BF16)
        rows = []
        for c in range(tm // CHUNK_B):
            blk = vn[c * CHUNK_B:(c + 1) * CHUNK_B, g * HEAD_DIM:(g + 1) * HEAD_DIM].astype(BF16)
            rows.append(_dot(wm, blk) + bbc[g])
        cols.append(jnp.concatenate(rows, axis=0) if len(rows) > 1 else rows[0])
    s = jnp.concatenate(cols, axis=1)
    return ub * s * _silu(zb)


ZA_BLOCK = 6


def _sgu_cols(A, B):
    assert A == B
    return 3, 4, 5


def _sgu_fwd(proj_m, lw, lb, W, bbc, A):
    T = proj_m.shape[0]
    G = W.shape[0]
    B = G * HEAD_DIM
    tm = _pick(T, (256, 128))
    cu, cv, cz = _sgu_cols(A, B)

    def body(u_ref, v_ref, z_ref, lw_ref, lb_ref, w_ref, b_ref, o_ref, ot_ref):
        o = _sgu_fn(u_ref[...], v_ref[...], z_ref[...], lw_ref[...], lb_ref[...],
                    tuple(w_ref[g] for g in range(G)), tuple(b_ref[g] for g in range(G)))
        o_ref[...] = o.astype(BF16)
        ot_ref[...] = o.T.astype(BF16)

    row = pl.BlockSpec((1, B), lambda i: (0, 0))
    cube = pl.BlockSpec((G, CHUNK_B, CHUNK_B), lambda i: (0, 0, 0))
    return pl.pallas_call(
        body, name="sgu_fwd", grid=(T // tm,),
        in_specs=[pl.BlockSpec((tm, B), lambda i: (i, cu)), pl.BlockSpec((tm, B), lambda i: (i, cv)),
                  pl.BlockSpec((tm, B), lambda i: (i, cz)), row, row, cube, cube],
        out_specs=[pl.BlockSpec((tm, B), lambda i: (i, 0)), pl.BlockSpec((B, tm), lambda i: (0, i))],
        out_shape=[jax.ShapeDtypeStruct((T, B), BF16), jax.ShapeDtypeStruct((B, T), BF16)],
        compiler_params=_cparams(("parallel",)),
    )(proj_m, proj_m, proj_m, lw, lb, W, bbc)


def _sgu_bwd(proj_m, lw, lb, W, bbc, d_o, A, dproj):
    T = proj_m.shape[0]
    G = W.shape[0]
    B = G * HEAD_DIM
    tm = _pick(T, (256, 128))
    nt = T // tm
    cu, cv, cz = _sgu_cols(A, B)

    def body(u_ref, v_ref, z_ref, lw_ref, lb_ref, w_ref, b_ref, do_ref, _,
             dp_ref, dlw_ref, dlb_ref, dw_ref, db_ref, dbb_ref):
        _, vjp = jax.vjp(_sgu_fn, u_ref[...], v_ref[...], z_ref[...], lw_ref[...], lb_ref[...],
                         tuple(w_ref[g] for g in range(G)), tuple(b_ref[g] for g in range(G)))
        du, dv, dz, dlw, dlb, dW, dbb = vjp(do_ref[...])
        dW, dbb = jnp.stack(dW, axis=0), jnp.stack(dbb, axis=0)
        dp_ref[:, 0:B] = du.astype(BF16)
        dp_ref[:, B:2 * B] = dv.astype(BF16)
        dp_ref[:, 2 * B:3 * B] = dz.astype(BF16)
        i = pl.program_id(0)

        @pl.when(i == 0)
        def _():
            dlw_ref[...] = dlw
            dlb_ref[...] = dlb
            dw_ref[...] = dW
            dbb_ref[...] = dbb

        @pl.when(i > 0)
        def _():
            dlw_ref[...] += dlw
            dlb_ref[...] += dlb
            dw_ref[...] += dW
            dbb_ref[...] += dbb

        @pl.when(i == nt - 1)
        def _():
            db_ref[...] = jnp.sum(dbb_ref[...], axis=-1, keepdims=True)

    row = pl.BlockSpec((1, B), lambda i: (0, 0))
    cube = pl.BlockSpec((G, CHUNK_B, CHUNK_B), lambda i: (0, 0, 0))
    return pl.pallas_call(
        body, name="sgu_bwd", grid=(nt,),
        in_specs=[pl.BlockSpec((tm, B), lambda i: (i, cu)), pl.BlockSpec((tm, B), lambda i: (i, cv)),
                  pl.BlockSpec((tm, B), lambda i: (i, cz)), row, row, cube, cube,
                  pl.BlockSpec((tm, B), lambda i: (i, A // B)), ANY],
        out_specs=[pl.BlockSpec((tm, 3 * B), lambda i: (i, 1)), row, row, cube,
                   pl.BlockSpec((G, CHUNK_B, 1), lambda i: (0, 0, 0))],
        out_shape=[jax.ShapeDtypeStruct(dproj.shape, dproj.dtype), jax.ShapeDtypeStruct((1, B), F32),
                   jax.ShapeDtypeStruct((1, B), F32), jax.ShapeDtypeStruct((G, CHUNK_B, CHUNK_B), F32),
                   jax.ShapeDtypeStruct((G, CHUNK_B, 1), F32)],
        input_output_aliases={8: 0},
        scratch_shapes=[pltpu.VMEM((G, CHUNK_B, CHUNK_B), F32)],
        compiler_params=_cparams(("arbitrary",)),
    )(proj_m, proj_m, proj_m, lw, lb, W, bbc, d_o, dproj)


def _head_fn(mix, x, fw, tgt):
    h = x + mix
    y = _rms_fn(h, fw)
    e = y - tgt
    return 0.5 * jnp.sum(jnp.mean(e * e, axis=-1, keepdims=True), axis=0, keepdims=True)


def _out_proj_loss(oa, ob, wout, x, tgt, fw):
    T, A = oa.shape
    B = ob.shape[1]
    D = x.shape[1]
    tm = _pick(T, (256, 128))

    def body(oa_ref, ob_ref, w_ref, x_ref, t_ref, fw_ref, dh_ref, dhb_ref, loss_ref, dfw_ref):
        mix = _dot(oa_ref[...], w_ref[0:A, :]) + _dot(ob_ref[...], w_ref[A:A + B, :])
        xv, tv = x_ref[...], t_ref[...]
        loss, vjp = jax.vjp(lambda m, f: _head_fn(m, xv, f, tv), mix, fw_ref[...])
        dh, dfw = vjp(jnp.ones((1, 1), F32))
        dh_ref[...] = dh
        dhb_ref[...] = dh.astype(BF16)
        lrow = jnp.broadcast_to(loss, (1, LANES))

        @pl.when(pl.program_id(0) == 0)
        def _():
            loss_ref[...] = lrow
            dfw_ref[...] = dfw

        @pl.when(pl.program_id(0) > 0)
        def _():
            loss_ref[...] += lrow
            dfw_ref[...] += dfw

    tile = pl.BlockSpec((tm, D), lambda i: (i, 0))
    return pl.pallas_call(
        body, name="out_proj_loss", grid=(T // tm,),
        in_specs=[pl.BlockSpec((tm, A), lambda i: (i, 0)), pl.BlockSpec((tm, B), lambda i: (i, 0)),
                  pl.BlockSpec((A + B, D), lambda i: (0, 0)), tile, tile,
                  pl.BlockSpec((1, D), lambda i: (0, 0))],
        out_specs=[tile, tile, pl.BlockSpec((1, LANES), lambda i: (0, 0)),
                   pl.BlockSpec((1, D), lambda i: (0, 0))],
        out_shape=[jax.ShapeDtypeStruct((T, D), F32), jax.ShapeDtypeStruct((T, D), BF16),
                   jax.ShapeDtypeStruct((1, LANES), F32), jax.ShapeDtypeStruct((1, D), F32)],
        compiler_params=_cparams(("arbitrary",)),
    )(oa, ob, wout, x, tgt, fw)


def _adamw(w, g, m, v, name):
    R, Cn = w.shape
    cap = max(8, 512 * 1024 // Cn)
    tr = max(t for t in range(8, min(R, cap) + 1, 8) if R % t == 0) if R > cap else R

    def body(w_ref, g_ref, m_ref, v_ref, d_ref, mo_ref, vo_ref):
        g = g_ref[...]
        m = ADAM_B1 * m_ref[...] + (1.0 - ADAM_B1) * g
        v = ADAM_B2 * v_ref[...] + (1.0 - ADAM_B2) * jnp.square(g)
        m_hat = m / (1.0 - ADAM_B1 ** ADAM_STEP)
        v_hat = v / (1.0 - ADAM_B2 ** ADAM_STEP)
        d_ref[...] = -ADAM_LR * (m_hat / (jnp.sqrt(v_hat) + ADAM_EPS) + ADAM_WD * w_ref[...])
        mo_ref[...] = m
        vo_ref[...] = v

    tile = pl.BlockSpec((tr, Cn), lambda i: (i, 0))
    shape = jax.ShapeDtypeStruct((R, Cn), F32)
    return pl.pallas_call(
        body, name=name, grid=(R // tr,), in_specs=[tile] * 4, out_specs=[tile] * 3,
        out_shape=[shape] * 3, compiler_params=_cparams(("parallel",)),
    )(w, g, m, v)


def _place():
    x, y, c = lax.axis_index("x"), lax.axis_index("y"), lax.axis_index("c")
    others = [(1 - x, y), (x, 1 - y), (1 - x, 1 - y)]
    return x, y, c, others


def _chip_index(px, py):
    return 2 * px + py


ANY = pl.BlockSpec(memory_space=pl.ANY)


def _gather_ride(blocks, split):
    n = len(blocks)

    def plan(in_refs, out_refs, send_sems, recv_sems):
        x, y, c, _ = _place()
        me, kx, ky, kd = (_chip_index(px, py) for px, py in ((x, y), (1 - x, y), (x, 1 - y), (1 - x, 1 - y)))
        to_x, to_y, to_s = (1 - x, y, c), (x, 1 - y, c), (x, y, 1 - c)

        def copy(sem, src, dst, to):
            return pltpu.make_async_remote_copy(src_ref=src, dst_ref=dst, send_sem=send_sems.at[sem],
                                                recv_sem=recv_sems.at[sem], device_id=to, device_id_type=MESH_ID)

        first, second, third, awaited = [], [], [], []
        for a in range(n):
            out, s0 = out_refs[a], 8 * a
            if not split[a]:
                for j, (k, to) in enumerate(((kx, to_x), (ky, to_y), (kd, (1 - x, 1 - y, c)))):
                    first.append(lambda j=j, to=to, a=a, out=out, s0=s0: copy(s0 + j, in_refs[a], out.at[me], to))
                    awaited.append((lambda j=j, k=k, to=to, out=out, s0=s0: copy(s0 + j, out.at[k], out.at[k], to),
                                    None))
                continue
            h = blocks[a].shape[0] // 2
            q = h // 2
            half = lambda k, core, out=out, h=h: out.at[k, pl.ds(core * h, h), :]
            quarter = lambda k, core, i, out=out, h=h, q=q: out.at[k, pl.ds(core * h + i * q, q), :]
            mine = in_refs[a].at[pl.ds(c * h, h), :]
            first.append(lambda s0=s0, mine=mine, half=half: copy(s0, mine, half(me, c), to_x))
            first.append(lambda s0=s0, mine=mine, half=half: copy(s0 + 1, mine, half(me, c), to_y))
            fwd0 = lambda s0=s0, quarter=quarter: copy(s0 + 2, quarter(kx, c, 0), quarter(kx, c, 0), to_y)
            fwd1 = lambda s0=s0, quarter=quarter: copy(s0 + 3, quarter(ky, c, 1), quarter(ky, c, 1), to_x)
            pieces = [(s0 + 0, lambda half=half: half(kx, c), lambda half=half: half(kx, 1 - c), to_x, fwd0),
                      (s0 + 1, lambda half=half: half(ky, c), lambda half=half: half(ky, 1 - c), to_y, fwd1),
                      (s0 + 2, lambda quarter=quarter: quarter(kd, c, 0), lambda quarter=quarter: quarter(kd, 1 - c, 0),
                       to_y, None),
                      (s0 + 3, lambda quarter=quarter: quarter(kd, c, 1), lambda quarter=quarter: quarter(kd, 1 - c, 1),
                       to_x, None)]
            for i, (sem, here, there, frm, fwd) in enumerate(pieces):
                passing = lambda s0=s0, i=i, here=here: copy(s0 + 4 + i, here(), here(), to_s)
                awaited.append((lambda sem=sem, here=here, frm=frm: copy(sem, here(), here(), frm), (fwd, passing)))
                if fwd is not None:
                    second.append(fwd)
                third.append((passing, lambda s0=s0, i=i, there=there: copy(s0 + 4 + i, there(), there(), to_s)))
        return first, second, third, awaited

    def start(*refs):
        for send in plan(*refs)[0]:
            send().start()

    def finish(*refs):
        first, second, third, awaited = plan(*refs)
        for arrival, then in awaited:
            arrival().wait_recv()
            for nxt in (then or ()):
                if nxt is not None:
                    nxt().start()
        for _, from_sibling in third:
            from_sibling().wait_recv()
        for send in first + second + [p for p, _ in third]:
            send().wait_send()

    shapes = [jax.ShapeDtypeStruct((N_CHIPS,) + b.shape, b.dtype) for b in blocks]
    return _Ride(blocks, shapes, 8 * n, start, finish)


def _put_own(gathered, own):
    me = _chip_index(lax.axis_index("x"), lax.axis_index("y"))
    return lax.dynamic_update_index_in_dim(gathered, own, me, 0)


def _allreduce_small(buf):
    R, L = buf.shape

    def body(in_ref, out_ref, sib_ref, pair_ref, chips_ref, send_sems, recv_sems):
        x, y, c, others = _place()
        me = _chip_index(x, y)
        sibling = (x, y, 1 - c)
        cp = pltpu.make_async_remote_copy(src_ref=in_ref, dst_ref=sib_ref, send_sem=send_sems.at[0],
                                          recv_sem=recv_sems.at[0], device_id=sibling, device_id_type=MESH_ID)
        cp.start()
        cp.wait()
        pair_ref[...] = in_ref[...] + sib_ref[...]
        sends = []
        for j, chip in enumerate(others):
            s = pltpu.make_async_remote_copy(src_ref=pair_ref, dst_ref=chips_ref.at[me],
                                             send_sem=send_sems.at[1 + j], recv_sem=recv_sems.at[1 + j],
                                             device_id=(*chip, c), device_id_type=MESH_ID)
            s.start()
            sends.append(s)
        chips_ref[me] = pair_ref[...]
        for j, chip in enumerate(others):
            k = _chip_index(*chip)
            pltpu.make_async_remote_copy(src_ref=pair_ref, dst_ref=chips_ref.at[k], send_sem=send_sems.at[1 + j],
                                         recv_sem=recv_sems.at[1 + j], device_id=(*chip, c),
                                         device_id_type=MESH_ID).wait_recv()
        for s in sends:
            s.wait_send()
        out_ref[...] = ((chips_ref[0] + chips_ref[1]) + chips_ref[2]) + chips_ref[3]

    vm = pl.BlockSpec(memory_space=pltpu.VMEM)
    return pl.pallas_call(
        body, name="allreduce_small", in_specs=[vm], out_specs=vm,
        out_shape=jax.ShapeDtypeStruct((R, L), F32),
        scratch_shapes=[pltpu.VMEM((R, L), F32), pltpu.VMEM((R, L), F32), pltpu.VMEM((N_CHIPS, R, L), F32),
                        pltpu.SemaphoreType.DMA((4,)), pltpu.SemaphoreType.DMA((4,))],
        compiler_params=pltpu.CompilerParams(vmem_limit_bytes=VMEM_LIMIT),
    )(buf)


def _pair_ride(g):
    nb, R, Cn = g.shape
    h = R // 2

    def copy(in_refs, out_refs, send_sems, recv_sems):
        x, y, c, _ = _place()
        return pltpu.make_async_remote_copy(src_ref=in_refs[0].at[:, pl.ds((1 - c) * h, h), :], dst_ref=out_refs[0],
                                            send_sem=send_sems.at[0], recv_sem=recv_sems.at[0],
                                            device_id=(x, y, 1 - c), device_id_type=MESH_ID)

    return _Ride([g], [jax.ShapeDtypeStruct((nb, h, Cn), g.dtype)], 1,
                 lambda *refs: copy(*refs).start(), lambda *refs: copy(*refs).wait())


def _pair_sum(g, land, c_arr, name, ride=None):
    nb, R, Cn = g.shape
    hr = R // 2
    tr = _pick(hr, (256, 128, 64, 32, 16))
    nt = hr // tr

    def body(c_ref, g_ref, l_ref, o_ref):
        o_ref[...] = (g_ref[...].astype(F32) + l_ref[...].astype(F32)).astype(BF16)

    return _pallas(
        body, (c_arr, g, land), name=name, prefetch=1, grid=(nb, nt),
        in_specs=[pl.BlockSpec((1, tr, Cn), lambda b, i, c_ref: (b, c_ref[0] * nt + i, 0)),
                  pl.BlockSpec((1, tr, Cn), lambda b, i, c_ref: (b, i, 0))],
        out_specs=pl.BlockSpec((1, tr, Cn), lambda b, i, c_ref: (b, i, 0)),
        out_shape=jax.ShapeDtypeStruct((nb, hr, Cn), BF16),
        semantics=("parallel", "parallel"), ride=ride)


def _chip_ride(parts):
    m = len(parts)

    def copies(in_refs, out_refs, send_sems, recv_sems):
        x, y, c, others = _place()
        me = _chip_index(x, y)
        def mk(j, chip, n, landing):
            k = _chip_index(*chip)
            return pltpu.make_async_remote_copy(
                src_ref=in_refs[n].at[k], dst_ref=out_refs[n].at[landing(k)], send_sem=send_sems.at[m * j + n],
                recv_sem=recv_sems.at[m * j + n], device_id=(*chip, c), device_id_type=MESH_ID)

        pairs = [(j, chip, n) for j, chip in enumerate(others) for n in range(m)]
        return pairs, (lambda *p: mk(*p, lambda k: me)), (lambda *p: mk(*p, lambda k: k))

    def start(*refs):
        pairs, send, _ = copies(*refs)
        for p in pairs:
            send(*p).start()

    def finish(*refs):
        pairs, send, arrival = copies(*refs)
        for p in pairs:
            arrival(*p).wait_recv()
        for p in pairs:
            send(*p).wait_send()

    return _Ride(parts, [jax.ShapeDtypeStruct(p.shape, p.dtype) for p in parts], 3 * m, start, finish)


def _put_own_slot(q, p):
    me = _chip_index(lax.axis_index("x"), lax.axis_index("y"))
    return lax.dynamic_update_index_in_dim(q, lax.dynamic_index_in_dim(p, me, 0, keepdims=False), me, 0)


def _chip_sum(q, c_arr, name):
    nb, hr, Cn = q.shape
    tr = _pick(hr, (256, 128, 64, 32, 16))
    nt = hr // tr

    def body(c_ref, q_ref, o_ref):
        f = lambda k: q_ref[k].astype(F32)
        o_ref[...] = ((f(0) + f(1)) + f(2)) + f(3)

    return _pallas(
        body, (c_arr, q), name=name, prefetch=1, grid=(nt,),
        in_specs=[pl.BlockSpec((nb, tr, Cn), lambda i, c_ref: (0, i, 0))],
        out_specs=pl.BlockSpec((tr, Cn), lambda i, c_ref: (c_ref[0] * nt + i, 0)),
        out_shape=jax.ShapeDtypeStruct((2 * hr, Cn), F32),
        semantics=("parallel",))


def _sibling_fill(fw, fo):
    def body(_, __, fw_ref, fo_ref, send_sems, recv_sems):
        x, y, c, _ = _place()
        copies = []
        for n, ref in enumerate((fw_ref, fo_ref)):
            h = ref.shape[0] // 2
            mine = ref.at[pl.ds(c * h, h), :]
            theirs = ref.at[pl.ds((1 - c) * h, h), :]
            mk = lambda src, dst: pltpu.make_async_remote_copy(
                src_ref=src, dst_ref=dst, send_sem=send_sems.at[n], recv_sem=recv_sems.at[n],
                device_id=(x, y, 1 - c), device_id_type=MESH_ID)
            send = mk(mine, mine)
            send.start()
            copies.append((send, mk(theirs, theirs)))
        for send, arrival in copies:
            arrival.wait_recv()
            send.wait_send()

    return pl.pallas_call(
        body, name="sibling_fill", in_specs=[ANY, ANY], out_specs=[ANY, ANY],
        out_shape=[jax.ShapeDtypeStruct(fw.shape, F32), jax.ShapeDtypeStruct(fo.shape, F32)],
        input_output_aliases={0: 0, 1: 1},
        scratch_shapes=[pltpu.SemaphoreType.DMA((2,)), pltpu.SemaphoreType.DMA((2,))],
        compiler_params=pltpu.CompilerParams(has_side_effects=True),
    )(fw, fo)


class _Layout:
    def __init__(self, H, G, nb, Cb):
        A, B = H * HEAD_DIM, G * HEAD_DIM
        self.n_main = 4 * A + 3 * B
        self.k = -(-(self.n_main + LANES) // WIN_BLOCK) * WIN_BLOCK
        cuts = [0, 3 * A, 4 * A, 4 * A + 2 * H, nb * Cb]
        starts = [0, 3 * A + 3 * B, self.n_main, 3 * A]
        self.pieces = []
        self.windows, self.runs = [], []
        for n in range(nb):
            segs = []
            for s in range(4):
                lo, hi = max(cuts[s], n * Cb), min(cuts[s + 1], (n + 1) * Cb)
                if lo < hi:
                    segs.append((starts[s] + lo - cuts[s], lo - n * Cb, hi - lo))
            self.pieces += [(own, n, col, ln) for own, col, ln in segs]
            blocks = sorted({b for own, _, ln in segs for b in range(own // WIN_BLOCK, (own + ln - 1) // WIN_BLOCK + 1)})
            self.windows.append(blocks)
            self.runs.append([(blocks.index(own // WIN_BLOCK) * WIN_BLOCK + own % WIN_BLOCK, ln)
                              for own, _, ln in segs])
        self.wb = max(len(b) for b in self.windows)
        self.table = [b + [b[-1]] * (self.wb - len(b)) for b in self.windows]
        self.pieces.sort()

    def to_own_order(self, g_in):
        D = g_in.shape[1]
        cols, at = [], 0
        for own, n, col, ln in self.pieces:
            if own > at:
                cols.append(jnp.zeros((D, own - at), g_in.dtype))
            cols.append(g_in[n, :, col:col + ln])
            at = own + ln
        if at < self.k:
            cols.append(jnp.zeros((D, self.k - at), g_in.dtype))
        return jnp.concatenate(cols, axis=1)

    def from_window(self, win, chip, Cb):
        pick = lambda runs: (lambda w: jnp.concatenate([w[:, c:c + ln] for c, ln in runs], axis=1))
        return lax.switch(chip, [pick(r) for r in self.runs], win)


def _device_step(x, tgt, norm_w, win_b, wout_b, conv_b, a_log, dt_bias, head_norm_w, sgu_ln_w, sgu_ln_b,
                 w_spatial, b_spatial, final_norm_w, c_arr):
    T, D = x.shape
    H = a_log.shape[1]
    A = H * HEAD_DIM
    G = w_spatial.shape[0]
    B = G * HEAD_DIM
    nb, Cb, Rb = N_CHIPS, win_b.shape[1], wout_b.shape[0]
    lay = _Layout(H, G, nb, Cb)
    alog_row = jnp.pad(a_log, ((0, 0), (H, LANES - 2 * H)))
    dtb_row = jnp.pad(dt_bias, ((0, 0), (H, LANES - 2 * H)))
    bbc = jnp.broadcast_to(b_spatial[:, :, None], (G, CHUNK_B, CHUNK_B))

    (xn, xn_t), (g_in,) = _rms_in(x, norm_w, ride=_gather_ride([win_b], [True]))
    w_own = lay.to_own_order(_put_own(g_in, win_b))
    proj_m, (g_out, g_conv) = _mm_nn(xn, w_own, F32, "in_proj", cols=(0, lay.n_main),
                                     ride=_gather_ride([wout_b, conv_b], [False, False]))
    wout = _put_own(g_out, wout_b).reshape(nb * Rb, D)
    conv_w = _put_own(g_conv, conv_b).transpose(1, 0, 2).reshape(CONV_WIDTH, nb * conv_b.shape[1])
    proj_ba = _mm_nn(xn, w_own, F32, "in_proj_ba", cols=(lay.n_main, LANES))
    q, k, v, gb, bb = _gdn_pre(proj_m, proj_ba, conv_w, alog_row, dtb_row, H)
    u, w, qg, kd, attn, eg, pinv = _gdn_prep(q, k, v, gb, bb)
    og, sall = _gdn_chain(qg, kd, u, w, attn, eg)
    oa, oa_t = _gdn_post(og, proj_m, head_norm_w)
    ob, ob_t = _sgu_fwd(proj_m, sgu_ln_w, sgu_ln_b, w_spatial, bbc, A)
    dh, dhb, loss_row, d_fnw = _out_proj_loss(oa, ob, wout, x, tgt, final_norm_w.reshape(1, D))

    d_o = _mm_nn(dhb, wout.T, F32, "out_proj_dx")
    dproj = lax.empty((T, lay.k), BF16)
    dproj, d_lw, d_lb, d_ws, d_bs = _sgu_bwd(proj_m, sgu_ln_w, sgu_ln_b, w_spatial, bbc, d_o, A, dproj)
    dog, dproj, d_hw = _gdn_post_bwd(og, proj_m, head_norm_w, d_o, dproj)
    dqg, dkd, du, dw, dat, deg = _gdn_chain_bwd(qg, kd, u, w, attn, eg, sall, dog)
    dq, dk, dv, dgb, dbb = _gdn_prep_bwd(q, k, v, gb, bb, pinv, du, dw, dqg, dkd, dat, deg)
    dc, dproj, d_al, d_dt = _gdn_pre_bwd(proj_m, proj_ba, conv_w, alog_row, dtb_row, dq, dk, dv, dgb, dbb, H,
                                         dproj)
    dproj, d_conv = _conv_bwd(proj_m, dc, conv_w, H, dproj)

    table = jnp.array([b for row in lay.table for b in row], jnp.int32)
    d_win = _mm_windows(xn_t, dproj, table, nb, "in_proj_dw")
    d_wout, (land_w,) = _mm_nn_pair(oa_t, ob_t, dhb, "out_proj_dw", ride=_pair_ride(d_win))
    d_wout = d_wout.reshape(nb, Rb, D)
    pair_w, (land_o,) = _pair_sum(d_win, land_w, c_arr, "pair_sum_w_in", ride=_pair_ride(d_wout))
    pair_o = _pair_sum(d_wout, land_o, c_arr, "pair_sum_w_out")
    dxn, (all_w,) = _mm_nt_rhs_outer(dproj, w_own, F32, "in_proj_dx", ride=_chip_ride([pair_w]))
    (grad_x, d_nw), (all_o,) = _rms_in_bwd(x, norm_w, dxn, dh, ride=_chip_ride([pair_o]))
    all_w, all_o = _put_own_slot(all_w, pair_w), _put_own_slot(all_o, pair_o)
    small = dict(norm_w=d_nw, conv_w=d_conv[:CONV_WIDTH], a_log=d_al[:, H:2 * H], dt_bias=d_dt[:, H:2 * H],
                 head_norm_w=d_hw, sgu_ln_w=d_lw, sgu_ln_b=d_lb, w_spatial=d_ws, b_spatial=d_bs[:, :, 0],
                 final_norm_w=d_fnw)
    return loss_row, grad_x, small, all_w, all_o


SMALL = ("norm_w", "conv_w", "a_log", "dt_bias", "head_norm_w", "sgu_ln_w", "sgu_ln_b", "w_spatial",
         "b_spatial", "final_norm_w")


def _pack(parts):
    rows = []
    for p in parts:
        f = p.reshape(-1)
        f = jnp.pad(f, (0, (-f.shape[0]) % (8 * LANES)))
        rows.append(f.reshape(-1, LANES))
    return jnp.concatenate(rows, axis=0)


def _unpack(buf, shapes):
    out, r = [], 0
    for s in shapes:
        n = 1
        for d in s:
            n *= d
        nr = -(-n // (8 * LANES)) * 8
        out.append(buf[r:r + nr].reshape(-1)[:n].reshape(s))
        r += nr
    return out


def kernel(x, norm_w, w_in, conv_w, a_log, dt_bias, head_norm_w, sgu_ln_w, sgu_ln_b, w_spatial, b_spatial, w_out, final_norm_w, loss_target, m_norm_w, m_w_in, m_conv_w, m_a_log, m_dt_bias, m_head_norm_w, m_sgu_ln_w, m_sgu_ln_b, m_w_spatial, m_b_spatial, m_w_out, m_final_norm_w, v_norm_w, v_w_in, v_conv_w, v_a_log, v_dt_bias, v_head_norm_w, v_sgu_ln_w, v_sgu_ln_b, v_w_spatial, v_b_spatial, v_w_out, v_final_norm_w):
    T, D = x.shape[1], x.shape[2]
    weights = dict(norm_w=norm_w, w_in=w_in, conv_w=conv_w, a_log=a_log, dt_bias=dt_bias, head_norm_w=head_norm_w,
                   sgu_ln_w=sgu_ln_w, sgu_ln_b=sgu_ln_b, w_spatial=w_spatial, b_spatial=b_spatial, w_out=w_out,
                   final_norm_w=final_norm_w)
    mom_m = dict(norm_w=m_norm_w, w_in=m_w_in, conv_w=m_conv_w, a_log=m_a_log, dt_bias=m_dt_bias,
                 head_norm_w=m_head_norm_w, sgu_ln_w=m_sgu_ln_w, sgu_ln_b=m_sgu_ln_b, w_spatial=m_w_spatial,
                 b_spatial=m_b_spatial, w_out=m_w_out, final_norm_w=m_final_norm_w)
    mom_v = dict(norm_w=v_norm_w, w_in=v_w_in, conv_w=v_conv_w, a_log=v_a_log, dt_bias=v_dt_bias,
                 head_norm_w=v_head_norm_w, sgu_ln_w=v_sgu_ln_w, sgu_ln_b=v_sgu_ln_b, w_spatial=v_w_spatial,
                 b_spatial=v_b_spatial, w_out=v_w_out, final_norm_w=v_final_norm_w)
    me = _chip_index(lax.axis_index("x"), lax.axis_index("y"))
    c_arr = lax.axis_index("c").astype(jnp.int32).reshape(1)
    Din, Cb = w_in.shape[1], w_in.shape[2]
    Rb = w_out.shape[1]
    cconv = conv_w.shape[2]

    loss_row, grad_x, g, qw, qo = _device_step(
        x[0], loss_target[0], norm_w, w_in[0].astype(BF16), w_out[0].astype(BF16), conv_w[0], a_log, dt_bias,
        head_norm_w, sgu_ln_w, sgu_ln_b, w_spatial[0], b_spatial[0], final_norm_w, c_arr)

    small_shapes = [tuple(g[n].shape) for n in SMALL]
    small = _allreduce_small(_pack([g[n] for n in SMALL]))
    gsum_in, gsum_out = _sibling_fill(_chip_sum(qw, c_arr, "chip_sum_w_in"), _chip_sum(qo, c_arr, "chip_sum_w_out"))
    gsum_in = _Layout(a_log.shape[1], w_spatial.shape[1], N_CHIPS, Cb).from_window(gsum_in, me, Cb)
    gsmall = dict(zip(SMALL, _unpack(small, small_shapes)))
    gsmall["conv_w"] = lax.dynamic_slice_in_dim(gsmall["conv_w"], me * cconv, cconv, axis=1)

    grads, deltas, new_m, new_v = {}, {}, {}, {}
    d, m2, v2 = _adamw(w_out[0], gsum_out, m_w_out[0], v_w_out[0], "adamw_w_out")
    grads["w_out"], deltas["w_out"], new_m["w_out"], new_v["w_out"] = gsum_out[None], d[None], m2[None], v2[None]
    flat = lambda a: a.transpose(2, 0, 1).reshape(-1, LANES)
    unflat = lambda f: f.reshape(Cb, 1, Din).transpose(1, 2, 0)
    g_flat = gsum_in.T.reshape(-1, LANES)
    d, m2, v2 = _adamw(flat(w_in), g_flat, flat(m_w_in), flat(v_w_in), "adamw_w_in")
    grads["w_in"], deltas["w_in"], new_m["w_in"], new_v["w_in"] = unflat(g_flat), unflat(d), unflat(m2), unflat(v2)
    shapes = [tuple(weights[n].shape) for n in SMALL]
    ds, ms, vs = _adamw(_pack([weights[n] for n in SMALL]), _pack([gsmall[n] for n in SMALL]),
                        _pack([mom_m[n] for n in SMALL]), _pack([mom_v[n] for n in SMALL]), "adamw_small")
    for n, gq, d, m2, v2 in zip(SMALL, [gsmall[n] for n in SMALL], _unpack(ds, shapes), _unpack(ms, shapes),
                                _unpack(vs, shapes)):
        grads[n], deltas[n], new_m[n], new_v[n] = gq.reshape(weights[n].shape), d, m2, v2

    loss = lax.psum(loss_row[0, 0], ("x", "y", "c"))
    order = ("norm_w", "w_in", "conv_w", "a_log", "dt_bias", "head_norm_w", "sgu_ln_w", "sgu_ln_b", "w_spatial",
             "b_spatial", "w_out", "final_norm_w")
    return (loss, grad_x[None], *[grads[n] for n in order], *[deltas[n] for n in order],
            *[new_m[n] for n in order], *[new_v[n] for n in order])
```

```python
import functools

import jax
import jax.numpy as jnp
from jax import lax
from jax.experimental import pallas as pl
from jax.experimental.pallas import tpu as pltpu

F32 = jnp.float32
BF16 = jnp.bfloat16
EPS = 1e-6
HEAD_DIM = 128
CHUNK_B = 128
CONV_WIDTH = 4
LANES = 128
HALO = 8
N_CHIPS = 4
ADAM_LR = 0.001
ADAM_B1 = 0.9
ADAM_B2 = 0.999
ADAM_EPS = 1e-08
ADAM_WD = 0.01
ADAM_STEP = 10
VMEM_LIMIT = 56 * 1024 * 1024
MESH_ID = pl.DeviceIdType.MESH
HI = lax.Precision.HIGHEST


def _cparams(sem=None, **kw):
    return pltpu.CompilerParams(dimension_semantics=sem, vmem_limit_bytes=VMEM_LIMIT, **kw)


def _matmul(a, b, ca, cb, precision):
    nb = a.ndim - 2
    batch = tuple(range(nb))
    return lax.dot_general(a, b, (((ca + nb,), (cb + nb,)), (batch, batch)), precision=precision,
                           preferred_element_type=F32)


def _dot(a, b, hi=False, precision=None):
    return _matmul(a, b, 1, 0, HI if hi else precision)


def _dot_nt(a, b, hi=False, precision=None):
    return _matmul(a, b, 1, 1, HI if hi else precision)


def _dot_tn(a, b, hi=False, precision=None):
    return _matmul(a, b, 0, 0, HI if hi else precision)


def _iota(shape, dim):
    return lax.broadcasted_iota(jnp.int32, shape, dim)


def _sigmoid(x):
    return 0.5 * (jnp.tanh(0.5 * x) + 1.0)


def _silu(x):
    return x * _sigmoid(x)


def _softplus(x):
    z = jnp.exp(-jnp.abs(x))
    small = z * (1.0 - z * (0.5 - z * (1.0 / 3.0)))
    return jnp.maximum(x, 0.0) + jnp.where(z < 1e-3, small, jnp.log(1.0 + z))


def _pick(n, pref):
    for t in pref:
        if n % t == 0:
            return t
    return n


class _Ride:
    def __init__(self, operands, out_shape, n_sems, start, finish):
        self.operands, self.out_shape, self.n_sems = list(operands), list(out_shape), n_sems
        self.start, self.finish = start, finish


def _pallas(body, operands, *, name, grid, in_specs, out_specs, out_shape, semantics, scratch_shapes=(),
            prefetch=0, ride=None):
    single = not isinstance(out_shape, (list, tuple))
    outs = [out_shape] if single else list(out_shape)
    ospecs = [out_specs] if single else list(out_specs)
    in_specs, scratch = list(in_specs), list(scratch_shapes)
    n_in, n_out, n_sc = len(operands) - prefetch, len(outs), len(scratch)
    kernel = body
    params = _cparams(semantics)
    if ride is not None:
        n_xin, n_xout = len(ride.operands), len(ride.out_shape)

        def kernel(*refs):
            pre, refs = refs[:prefetch], refs[prefetch:]
            ins, refs = refs[:n_in], refs[n_in:]
            xins, refs = refs[:n_xin], refs[n_xin:]
            mains, refs = refs[:n_out], refs[n_out:]
            xouts, refs = refs[:n_xout], refs[n_xout:]
            sc, (send, recv) = refs[:n_sc], refs[n_sc:]
            ids = [pl.program_id(a) for a in range(len(grid))]
            first = functools.reduce(jnp.logical_and, [i == 0 for i in ids])
            last = functools.reduce(jnp.logical_and, [i == g - 1 for i, g in zip(ids, grid)])

            @pl.when(first)
            def _():
                ride.start(xins, xouts, send, recv)

            body(*pre, *ins, *mains, *sc)

            @pl.when(last)
            def _():
                ride.finish(xins, xouts, send, recv)

        operands = list(operands) + ride.operands
        in_specs += [ANY] * n_xin
        ospecs += [ANY] * n_xout
        outs += ride.out_shape
        scratch += [pltpu.SemaphoreType.DMA((ride.n_sems,)), pltpu.SemaphoreType.DMA((ride.n_sems,))]
        params = _cparams(("arbitrary",) * len(grid), has_side_effects=True)
    if prefetch:
        spec = dict(grid_spec=pltpu.PrefetchScalarGridSpec(
            num_scalar_prefetch=prefetch, grid=grid, in_specs=in_specs, out_specs=ospecs, scratch_shapes=scratch))
    else:
        spec = dict(grid=grid, in_specs=in_specs, out_specs=ospecs, scratch_shapes=scratch)
    res = pl.pallas_call(kernel, name=name, out_shape=outs, compiler_params=params, **spec)(*operands)
    main = res[0] if single else list(res[:n_out])
    return main if ride is None else (main, list(res[n_out:]))


def _mm_nn(a, b, out_dtype, name, tm=1024, tn=512, tk=None, cols=None, ride=None):
    M, K = a.shape
    c0, N = (0, b.shape[1]) if cols is None else cols
    tm = _pick(M, (tm, 1024, 512, 256, 128))
    tn = _pick(N, (tn, 512, 384, 256, 128))
    tk = K if tk is None else _pick(K, (tk,))
    nk = K // tk
    j0 = c0 // tn
    assert c0 % tn == 0

    def body(a_ref, b_ref, o_ref, *scratch):
        part = _dot(a_ref[...], b_ref[...])
        if nk == 1:
            o_ref[...] = part.astype(out_dtype)
        else:
            acc_ref, = scratch
            k = pl.program_id(2)

            @pl.when(k == 0)
            def _():
                acc_ref[...] = part

            @pl.when(k > 0)
            def _():
                acc_ref[...] += part

            @pl.when(k == nk - 1)
            def _():
                o_ref[...] = acc_ref[...].astype(out_dtype)

    return _pallas(
        body, (a, b), name=name, grid=(M // tm, N // tn, nk),
        in_specs=[pl.BlockSpec((tm, tk), lambda i, j, k: (i, k)),
                  pl.BlockSpec((tk, tn), lambda i, j, k: (k, j + j0))],
        out_specs=pl.BlockSpec((tm, tn), lambda i, j, k: (i, j)),
        out_shape=jax.ShapeDtypeStruct((M, N), out_dtype),
        scratch_shapes=[] if nk == 1 else [pltpu.VMEM((tm, tn), F32)],
        semantics=("parallel", "parallel", "arbitrary"), ride=ride)


def _mm_nt_rhs_outer(a, b, out_dtype, name, tm=256, tn=1024, ride=None):
    M, K = a.shape
    N, _ = b.shape
    tm = _pick(M, (tm, 128))
    tn = _pick(N, (tn, 512, 256, 128))

    def body(a_ref, b_ref, o_ref):
        o_ref[...] = _dot_nt(a_ref[...], b_ref[...]).astype(out_dtype)

    return _pallas(
        body, (a, b), name=name, grid=(N // tn, M // tm),
        in_specs=[pl.BlockSpec((tm, K), lambda j, i: (i, 0)),
                  pl.BlockSpec((tn, K), lambda j, i: (j, 0))],
        out_specs=pl.BlockSpec((tm, tn), lambda j, i: (i, j)),
        out_shape=jax.ShapeDtypeStruct((M, N), out_dtype),
        semantics=("parallel", "parallel"), ride=ride)


WIN_BLOCK = 256


def _mm_windows(a, b, table, nb, name, tm=2048):
    M, K = a.shape
    wb = table.shape[0] // nb
    tm = _pick(M, (tm, 1024, 512, 256, 128))

    def body(tab_ref, a_ref, b_ref, o_ref):
        o_ref[0] = _dot(a_ref[...], b_ref[...]).astype(BF16)

    return pl.pallas_call(
        body, name=name,
        grid_spec=pltpu.PrefetchScalarGridSpec(
            num_scalar_prefetch=1, grid=(nb, M // tm, wb),
            in_specs=[pl.BlockSpec((tm, K), lambda n, i, t, tab: (i, 0)),
                      pl.BlockSpec((K, WIN_BLOCK), lambda n, i, t, tab: (0, tab[n * wb + t]))],
            out_specs=pl.BlockSpec((1, tm, WIN_BLOCK), lambda n, i, t, tab: (n, i, t))),
        out_shape=jax.ShapeDtypeStruct((nb, M, wb * WIN_BLOCK), BF16),
        compiler_params=_cparams(("parallel", "parallel", "arbitrary")),
    )(table, a, b)


def _mm_nn_pair(a0, a1, b, name, tm=512, tn=1024, ride=None):
    M, K = a0.shape
    _, N = b.shape
    tm = _pick(M, (tm, 256, 128))
    tn = _pick(N, (tn, 512, 256, 128))
    ni = M // tm

    def body(a0_ref, a1_ref, b_ref, o_ref):
        p = pl.program_id(0)

        @pl.when(p == 0)
        def _():
            o_ref[...] = _dot(a0_ref[...], b_ref[...]).astype(BF16)

        @pl.when(p == 1)
        def _():
            o_ref[...] = _dot(a1_ref[...], b_ref[...]).astype(BF16)

    return _pallas(
        body, (a0, a1, b), name=name, grid=(2, ni, N // tn),
        in_specs=[pl.BlockSpec((tm, K), lambda p, i, j: (i * (1 - p), 0)),
                  pl.BlockSpec((tm, K), lambda p, i, j: (i * p, 0)),
                  pl.BlockSpec((K, tn), lambda p, i, j: (0, j))],
        out_specs=pl.BlockSpec((tm, tn), lambda p, i, j: (p * ni + i, j)),
        out_shape=jax.ShapeDtypeStruct((2 * M, N), BF16),
        semantics=("parallel", "parallel", "parallel"), ride=ride)


def _rms_fn(x, w):
    r = lax.rsqrt(jnp.mean(x * x, axis=-1, keepdims=True) + EPS)
    return x * r * w


def _rms_in(x, w, ride=None):
    T, D = x.shape
    tm = _pick(T, (512, 256, 128))

    def body(x_ref, w_ref, o_ref, ot_ref):
        xn = _rms_fn(x_ref[...], w_ref[...])
        o_ref[...] = xn.astype(BF16)
        ot_ref[...] = xn.T.astype(BF16)

    return _pallas(
        body, (x, w), name="rms_in", grid=(T // tm,),
        in_specs=[pl.BlockSpec((tm, D), lambda i: (i, 0)), pl.BlockSpec((1, D), lambda i: (0, 0))],
        out_specs=[pl.BlockSpec((tm, D), lambda i: (i, 0)), pl.BlockSpec((D, tm), lambda i: (0, i))],
        out_shape=[jax.ShapeDtypeStruct((T, D), BF16), jax.ShapeDtypeStruct((D, T), BF16)],
        semantics=("parallel",), ride=ride)


def _rms_in_bwd(x, w, dxn, dh, ride=None):
    T, D = x.shape
    tm = _pick(T, (256, 128))

    def body(x_ref, w_ref, dxn_ref, dh_ref, gx_ref, dw_ref):
        _, vjp = jax.vjp(_rms_fn, x_ref[...], w_ref[...])
        dx, dw = vjp(dxn_ref[...])
        gx_ref[...] = dh_ref[...] + dx

        @pl.when(pl.program_id(0) == 0)
        def _():
            dw_ref[...] = dw

        @pl.when(pl.program_id(0) > 0)
        def _():
            dw_ref[...] += dw

    tile = pl.BlockSpec((tm, D), lambda i: (i, 0))
    row = pl.BlockSpec((1, D), lambda i: (0, 0))
    return _pallas(
        body, (x, w, dxn, dh), name="rms_in_bwd", grid=(T // tm,),
        in_specs=[tile, row, tile, tile], out_specs=[tile, row],
        out_shape=[jax.ShapeDtypeStruct((T, D), F32), jax.ShapeDtypeStruct((1, D), F32)],
        semantics=("arbitrary",), ride=ride)


def _conv_fwd(cat_ref, halo, x, w):
    tm = x.shape[0]
    cat_ref[0:HALO, :] = halo
    cat_ref[HALO:HALO + tm, :] = x
    c = x * w[CONV_WIDTH - 1:CONV_WIDTH, :]
    for k in range(CONV_WIDTH - 1):
        s = CONV_WIDTH - 1 - k
        c = c + cat_ref[pl.ds(HALO - s, tm), :] * w[k:k + 1, :]
    return c


def _lane_to_all(x, lane):
    @jax.custom_vjp
    def f(x):
        return jnp.broadcast_to(x[:, lane:lane + 1], x.shape)

    def f_fwd(x):
        return f(x), None

    def f_bwd(_, g):
        return (jnp.where(_iota(g.shape, 1) == lane, jnp.sum(g, axis=-1, keepdims=True), 0.0),)

    f.defvjp(f_fwd, f_bwd)
    return f(x)


def _gdn_pointwise(c, ba, alog, dtb, H):
    A = H * HEAD_DIM
    s = _silu(c)
    beta = _sigmoid(ba)
    g = -jnp.exp(alog) * _softplus(ba + dtb)
    qs, ks, vs, gbs, bbs = [], [], [], [], []
    for h in range(H):
        lo = h * HEAD_DIM
        q = s[:, lo:lo + HEAD_DIM]
        k = s[:, A + lo:A + lo + HEAD_DIM]
        qs.append(q * lax.rsqrt(jnp.sum(q * q, axis=-1, keepdims=True) + EPS))
        ks.append(k * lax.rsqrt(jnp.sum(k * k, axis=-1, keepdims=True) + EPS))
        vs.append(s[:, 2 * A + lo:2 * A + lo + HEAD_DIM])
        bbs.append(_lane_to_all(beta, h))
        gbs.append(_lane_to_all(g, H + h))
    st = lambda xs: jnp.stack(xs, axis=0)
    return st(qs), st(ks), st(vs), st(gbs), st(bbs)


def _halo_prev(tm):
    return lambda i: (jnp.maximum(i * (tm // HALO) - 1, 0), 0)


def _gdn_pre(proj_m, proj_ba, conv_w, alog_row, dtb_row, H):
    T = proj_m.shape[0]
    A = H * HEAD_DIM
    tm = _pick(T, (256, 128))
    hs = pl.BlockSpec((H, tm, HEAD_DIM), lambda i: (0, i, 0))
    hshape = jax.ShapeDtypeStruct((H, T, HEAD_DIM), F32)

    def body(x_ref, halo_ref, ba_ref, w_ref, al_ref, dt_ref, q_ref, k_ref, v_ref, gb_ref, bb_ref, cat_ref):
        halo = jnp.where(pl.program_id(0) == 0, 0.0, halo_ref[...])
        c = _conv_fwd(cat_ref, halo, x_ref[...], w_ref[...])
        q, k, v, gb, bb = _gdn_pointwise(c, ba_ref[...], al_ref[...], dt_ref[...], H)
        q_ref[...] = q
        k_ref[...] = k
        v_ref[...] = v
        gb_ref[...] = gb
        bb_ref[...] = bb

    return pl.pallas_call(
        body, name="gdn_pre", grid=(T // tm,),
        in_specs=[pl.BlockSpec((tm, 3 * A), lambda i: (i, 0)),
                  pl.BlockSpec((HALO, 3 * A), _halo_prev(tm)),
                  pl.BlockSpec((tm, LANES), lambda i: (i, 0)),
                  pl.BlockSpec((CONV_WIDTH, 3 * A), lambda i: (0, 0)),
                  pl.BlockSpec((1, LANES), lambda i: (0, 0)),
                  pl.BlockSpec((1, LANES), lambda i: (0, 0))],
        out_specs=[hs] * 5, out_shape=[hshape] * 5,
        scratch_shapes=[pltpu.VMEM((HALO + tm, 3 * A), F32)],
        compiler_params=_cparams(("parallel",)),
    )(proj_m, proj_m, proj_ba, conv_w, alog_row, dtb_row)


def _gdn_pre_bwd(proj_m, proj_ba, conv_w, alog_row, dtb_row, dq, dk, dv, dgb, dbb, H, dproj):
    T, n_main = proj_m.shape
    A = H * HEAD_DIM
    tm = _pick(T, (256, 128))
    hs = pl.BlockSpec((H, tm, HEAD_DIM), lambda i: (0, i, 0))
    row = pl.BlockSpec((1, LANES), lambda i: (0, 0))

    def body(x_ref, halo_ref, ba_ref, w_ref, al_ref, dt_ref, dq_ref, dk_ref, dv_ref, dgb_ref, dbb_ref, _,
             dc_ref, dba_ref, dal_ref, ddt_ref, cat_ref):
        halo = jnp.where(pl.program_id(0) == 0, 0.0, halo_ref[...])
        c = _conv_fwd(cat_ref, halo, x_ref[...], w_ref[...])
        _, vjp = jax.vjp(functools.partial(_gdn_pointwise, H=H), c, ba_ref[...], al_ref[...], dt_ref[...])
        dc, dba, dal, ddt = vjp((dq_ref[...], dk_ref[...], dv_ref[...], dgb_ref[...], dbb_ref[...]))
        dc_ref[...] = dc
        dba_ref[:, :LANES] = dba.astype(BF16)
        dba_ref[:, LANES:] = jnp.zeros((tm, WIN_BLOCK - LANES), BF16)

        @pl.when(pl.program_id(0) == 0)
        def _():
            dal_ref[...] = dal
            ddt_ref[...] = ddt

        @pl.when(pl.program_id(0) > 0)
        def _():
            dal_ref[...] += dal
            ddt_ref[...] += ddt

    return pl.pallas_call(
        body, name="gdn_pre_bwd", grid=(T // tm,),
        in_specs=[pl.BlockSpec((tm, 3 * A), lambda i: (i, 0)),
                  pl.BlockSpec((HALO, 3 * A), _halo_prev(tm)),
                  pl.BlockSpec((tm, LANES), lambda i: (i, 0)),
                  pl.BlockSpec((CONV_WIDTH, 3 * A), lambda i: (0, 0)),
                  row, row, hs, hs, hs, hs, hs, ANY],
        out_specs=[pl.BlockSpec((tm, 3 * A), lambda i: (i, 0)),
                   pl.BlockSpec((tm, WIN_BLOCK), lambda i: (i, n_main // WIN_BLOCK)), row, row],
        out_shape=[jax.ShapeDtypeStruct((T, 3 * A), F32), jax.ShapeDtypeStruct(dproj.shape, dproj.dtype),
                   jax.ShapeDtypeStruct((1, LANES), F32), jax.ShapeDtypeStruct((1, LANES), F32)],
        input_output_aliases={11: 1},
        scratch_shapes=[pltpu.VMEM((HALO + tm, 3 * A), F32)],
        compiler_params=_cparams(("arbitrary",)),
    )(proj_m, proj_m, proj_ba, conv_w, alog_row, dtb_row, dq, dk, dv, dgb, dbb, dproj)


def _conv_bwd(proj_m, dc, conv_w, H, dproj):
    T = proj_m.shape[0]
    A = H * HEAD_DIM
    tm = _pick(T, (256, 128))
    nt = T // tm

    def body(x_ref, halo_ref, dc_ref, nxt_ref, w_ref, _, dx_ref, dw_ref):
        i = pl.program_id(0)
        halo = jnp.where(i == 0, 0.0, halo_ref[...])
        xcat = jnp.concatenate([halo, x_ref[...]], axis=0)
        nxt = jnp.where(i == nt - 1, 0.0, nxt_ref[...])
        dc = dc_ref[...]
        dcat = jnp.concatenate([dc, nxt], axis=0)
        w = w_ref[...]
        dx = None
        rows = []
        for k in range(CONV_WIDTH):
            s = CONV_WIDTH - 1 - k
            ds = dcat if s == 0 else pltpu.roll(dcat, tm + HALO - s, 0)
            term = ds[:tm, :] * w[k:k + 1, :]
            dx = term if dx is None else dx + term
            xs = xcat if s == 0 else pltpu.roll(xcat, s, 0)
            rows.append(jnp.sum(dc * xs[HALO:, :], axis=0, keepdims=True))
        dx_ref[...] = dx.astype(BF16)
        dw = jnp.concatenate(rows + [jnp.zeros((HALO - CONV_WIDTH, 3 * A), F32)], axis=0)

        @pl.when(i == 0)
        def _():
            dw_ref[...] = dw

        @pl.when(i > 0)
        def _():
            dw_ref[...] += dw

    return pl.pallas_call(
        body, name="conv_bwd", grid=(nt,),
        in_specs=[pl.BlockSpec((tm, 3 * A), lambda i: (i, 0)),
                  pl.BlockSpec((HALO, 3 * A), _halo_prev(tm)),
                  pl.BlockSpec((tm, 3 * A), lambda i: (i, 0)),
                  pl.BlockSpec((HALO, 3 * A), lambda i: (jnp.minimum((i + 1) * (tm // HALO), T // HALO - 1), 0)),
                  pl.BlockSpec((CONV_WIDTH, 3 * A), lambda i: (0, 0)), ANY],
        out_specs=[pl.BlockSpec((tm, 3 * A), lambda i: (i, 0)),
                   pl.BlockSpec((HALO, 3 * A), lambda i: (0, 0))],
        out_shape=[jax.ShapeDtypeStruct(dproj.shape, dproj.dtype), jax.ShapeDtypeStruct((HALO, 3 * A), F32)],
        input_output_aliases={5: 0},
        compiler_params=_cparams(("arbitrary",)),
    )(proj_m, proj_m, dc, dc, conv_w, dproj)


CHUNK = 128
BLOCK = 64


def _b(x):
    return x.astype(BF16)


@jax.custom_vjp
def _bdot(a, b):
    return _dot(_b(a), _b(b))


def _bdot_f(a, b):
    return _bdot(a, b), (a, b)


def _bdot_b(res, g):
    a, b = res
    return _dot_nt(_b(g), _b(b)), _dot_tn(_b(a), _b(g))


_bdot.defvjp(_bdot_f, _bdot_b)


@jax.custom_vjp
def _bdot_nt(a, b):
    return _dot_nt(_b(a), _b(b))


def _bdot_nt_f(a, b):
    return _bdot_nt(a, b), (a, b)


def _bdot_nt_b(res, g):
    a, b = res
    return _dot(_b(g), _b(b)), _dot_tn(_b(g), _b(a))


_bdot_nt.defvjp(_bdot_nt_f, _bdot_nt_b)


@jax.custom_vjp
def _bdot_tn(a, b):
    return _dot_tn(_b(a), _b(b))


def _bdot_tn_f(a, b):
    return _bdot_tn(a, b), (a, b)


def _bdot_tn_b(res, g):
    a, b = res
    return _dot_nt(_b(b), _b(g)), _dot(_b(a), _b(g))


_bdot_tn.defvjp(_bdot_tn_f, _bdot_tn_b)


def _mask_matmul(m, x):
    hi = _b(x)
    r = x - hi.astype(F32)
    mid = _b(r)
    lo = _b(r - mid.astype(F32))
    return (_dot(m, lo) + _dot(m, mid)) + _dot(m, hi)


@jax.custom_vjp
def _mask_dot(m, mt, x):
    return _mask_matmul(m, x)


def _mask_dot_f(m, mt, x):
    return _mask_matmul(m, x), (m, mt)


def _mask_dot_b(res, g):
    m, mt = res
    return jnp.zeros_like(m), jnp.zeros_like(mt), _mask_matmul(mt, g)


_mask_dot.defvjp(_mask_dot_f, _mask_dot_b)

HIGH = lax.Precision.HIGH


def _unit_lower_inverse(L):
    n = L.shape[-1]
    X = -L
    Q = X
    for _ in range(BLOCK.bit_length() - 2):
        X = _dot(_b(X), _b(X))
        Q = Q + X + _dot(_b(Q), _b(X))
    return (_iota((n, n), 0) == _iota((n, n), 1)).astype(F32) + Q


@jax.custom_vjp
def _known_inverse(L, P):
    return P


def _known_inverse_f(L, P):
    return P, P


def _known_inverse_b(P, g):
    n = P.shape[-1]
    Q = _b(P - (_iota((n, n), 0) == _iota((n, n), 1)).astype(F32))
    t = g + _dot_tn(Q, _b(g))
    return -(t + _dot_nt(_b(t), Q)), jnp.zeros_like(P)


_known_inverse.defvjp(_known_inverse_f, _known_inverse_b)


def _gdn_prep_fn(q, k, v, gb, bb, P_known=None):
    n = CHUNK
    row, col = _iota((n, n), 0), _iota((n, n), 1)
    same = (row // BLOCK) == (col // BLOCK)
    incl, strict = same & (row >= col), same & (row > col)
    bc = lambda m: jnp.broadcast_to(_b(m.astype(F32)), q.shape[:1] + (n, n))
    tril, triu, ones = bc(incl), bc(same & (row <= col)), bc(same)
    gc = _mask_dot(tril, triu, gb)
    gl = _mask_dot(ones, ones, gb)
    decay = jnp.where(incl, jnp.exp(jnp.where(incl, gc - jnp.swapaxes(gc, 1, 2), 0.0)), 0.0)
    kb = k * bb
    vb = v * bb
    qs = q * (HEAD_DIM ** -0.5)
    L = jnp.where(strict, _bdot_nt(kb, k) * decay, 0.0)
    P = _unit_lower_inverse(L) if P_known is None else _known_inverse(L, P_known)
    egc = jnp.exp(gc)
    u = _bdot(P, vb)
    w = _bdot(P, kb * egc)
    attn = jnp.where(incl, _bdot_nt(qs, k) * decay, 0.0)
    qg = qs * egc
    kdec = k * jnp.exp(gl - gc)
    eg = jnp.exp(gl).reshape(-1, n // BLOCK, BLOCK, LANES).sum(axis=2) * (1.0 / BLOCK)
    if P_known is None:
        return u, w, qg, kdec, attn, eg, P
    return u, w, qg, kdec, attn, eg


def _gdn_chain_fn(S, qg, kdec, u, w, attn, eg):
    nblk = CHUNK // BLOCK
    cat = lambda xs: jnp.concatenate(xs, axis=1)
    outs, found = [], []
    for i in range(nblk):
        r = (slice(None), slice(i * BLOCK, (i + 1) * BLOCK))
        v_new = u[r] - _bdot(w[r], S)
        found.append(v_new)
        outs.append(_bdot(qg[r], S) + _bdot(attn[r], cat(found + [jnp.zeros_like(v_new)] * (nblk - 1 - i))))
        S = S * eg[i] + _bdot_tn(kdec[r], v_new)
    return cat(outs), S


def _eg_spec(H, T, chunks, index_map, per_head):
    nblk = CHUNK // BLOCK
    block = (chunks, 1 if per_head else H, nblk, LANES)
    return pl.BlockSpec(block, index_map), jax.ShapeDtypeStruct((T // CHUNK, H, nblk, LANES), F32)


def _gdn_prep(q, k, v, gb, bb):
    H, T, _ = q.shape
    pb = _pick(T // CHUNK, (8, 4, 2, 1))
    hs = pl.BlockSpec((1, CHUNK * pb, HEAD_DIM), lambda h, n: (h, n, 0))
    hshape = jax.ShapeDtypeStruct((H, T, HEAD_DIM), F32)

    def body(q_ref, k_ref, v_ref, gb_ref, bb_ref, *out_refs):
        chunks = lambda ref: ref[0].reshape(pb, CHUNK, HEAD_DIM)
        outs = _gdn_prep_fn(chunks(q_ref), chunks(k_ref), chunks(v_ref), chunks(gb_ref), chunks(bb_ref))
        for i, (ref, val) in enumerate(zip(out_refs, outs)):
            if i == 5:
                ref[:, 0] = val
            else:
                ref[0] = val.reshape(pb * CHUNK, HEAD_DIM).astype(ref.dtype)

    kept = [F32, BF16, BF16, BF16, BF16, None, BF16]
    es, eshape = _eg_spec(H, T, pb, lambda h, n: (n, h, 0, 0), per_head=True)
    return pl.pallas_call(
        body, name="gdn_prep", grid=(H, T // (CHUNK * pb)),
        in_specs=[hs] * 5, out_specs=[es if dt is None else hs for dt in kept],
        out_shape=[eshape if dt is None else jax.ShapeDtypeStruct((H, T, HEAD_DIM), dt) for dt in kept],
        compiler_params=_cparams(("parallel", "parallel")),
    )(q, k, v, gb, bb)


def _gdn_prep_bwd(q, k, v, gb, bb, pinv, du, dw, dqg, dkd, dat, deg):
    H, T, _ = q.shape
    pb = _pick(T // CHUNK, (8, 4, 2, 1))
    hs = pl.BlockSpec((1, CHUNK * pb, HEAD_DIM), lambda h, n: (h, n, 0))
    hshape = jax.ShapeDtypeStruct((H, T, HEAD_DIM), F32)

    def body(*refs):
        in_refs, p_ref, ct_refs, out_refs = refs[:5], refs[5], refs[6:12], refs[12:]
        chunks = lambda ref: ref[0].reshape(pb, CHUNK, HEAD_DIM)
        P = chunks(p_ref).astype(F32)
        _, vjp = jax.vjp(lambda *a: _gdn_prep_fn(*a, P_known=P), *[chunks(r) for r in in_refs])
        grads = vjp(tuple(chunks(r).astype(F32) for r in ct_refs[:5]) + (ct_refs[5][:, 0],))
        for ref, val in zip(out_refs, grads):
            ref[0] = val.reshape(pb * CHUNK, HEAD_DIM)

    es, _ = _eg_spec(H, T, pb, lambda h, n: (n, h, 0, 0), per_head=True)
    return pl.pallas_call(
        body, name="gdn_prep_bwd", grid=(H, T // (CHUNK * pb)),
        in_specs=[hs] * 11 + [es], out_specs=[hs] * 5, out_shape=[hshape] * 5,
        compiler_params=_cparams(("parallel", "parallel")),
    )(q, k, v, gb, bb, pinv, du, dw, dqg, dkd, dat, deg)


def _gdn_chain(qg, kd, u, w, attn, eg):
    H, T, _ = qg.shape
    N = T // CHUNK
    hs = pl.BlockSpec((H, CHUNK, HEAD_DIM), lambda n: (0, n, 0))
    ss = pl.BlockSpec((1, H, HEAD_DIM, HEAD_DIM), lambda n: (n, 0, 0, 0))

    def body(qg_ref, kd_ref, u_ref, w_ref, at_ref, eg_ref, o_ref, sall_ref, s_ref):
        @pl.when(pl.program_id(0) == 0)
        def _():
            s_ref[...] = jnp.zeros_like(s_ref)

        S = s_ref[...]
        sall_ref[0] = S
        eg = tuple(eg_ref[0, :, i:i + 1, :] for i in range(CHUNK // BLOCK))
        o, S2 = _gdn_chain_fn(S, qg_ref[...], kd_ref[...], u_ref[...], w_ref[...], at_ref[...], eg)
        o_ref[...] = o
        s_ref[...] = S2

    es, _ = _eg_spec(H, T, 1, lambda n: (n, 0, 0, 0), per_head=False)
    return pl.pallas_call(
        body, name="gdn_chain", grid=(N,),
        in_specs=[hs] * 5 + [es], out_specs=[hs, ss],
        out_shape=[jax.ShapeDtypeStruct((H, T, HEAD_DIM), F32),
                   jax.ShapeDtypeStruct((N, H, HEAD_DIM, HEAD_DIM), F32)],
        scratch_shapes=[pltpu.VMEM((H, HEAD_DIM, HEAD_DIM), F32)],
        compiler_params=_cparams(("arbitrary",)),
    )(qg, kd, u, w, attn, eg)


def _gdn_chain_bwd(qg, kd, u, w, attn, eg, sall, do):
    H, T, _ = qg.shape
    N = T // CHUNK
    hs = pl.BlockSpec((H, CHUNK, HEAD_DIM), lambda n: (0, N - 1 - n, 0))
    ss = pl.BlockSpec((1, H, HEAD_DIM, HEAD_DIM), lambda n: (N - 1 - n, 0, 0, 0))
    hshape = jax.ShapeDtypeStruct((H, T, HEAD_DIM), F32)

    def body(qg_ref, kd_ref, u_ref, w_ref, at_ref, eg_ref, sall_ref, do_ref, *rest):
        out_refs, ds_ref = rest[:6], rest[6]

        @pl.when(pl.program_id(0) == 0)
        def _():
            ds_ref[...] = jnp.zeros_like(ds_ref)

        f32 = lambda ref: ref[...].astype(F32)
        nblk = CHUNK // BLOCK
        eg = tuple(eg_ref[0, :, i:i + 1, :] for i in range(nblk))
        _, vjp = jax.vjp(_gdn_chain_fn, sall_ref[0], f32(qg_ref), f32(kd_ref), u_ref[...], f32(w_ref),
                         f32(at_ref), eg)
        grads = vjp((do_ref[...], ds_ref[...]))
        ds_ref[...] = grads[0]
        for ref, val in zip(out_refs[:5], grads[1:6]):
            ref[...] = val.astype(ref.dtype)
        for i in range(nblk):
            out_refs[5][0, :, i:i + 1, :] = grads[6][i]

    kept = [F32, F32, BF16, BF16, F32]
    es, eshape = _eg_spec(H, T, 1, lambda n: (N - 1 - n, 0, 0, 0), per_head=False)
    return pl.pallas_call(
        body, name="gdn_chain_bwd", grid=(N,),
        in_specs=[hs] * 5 + [es, ss, hs], out_specs=[hs] * 5 + [es],
        out_shape=[jax.ShapeDtypeStruct((H, T, HEAD_DIM), dt) for dt in kept] + [eshape],
        scratch_shapes=[pltpu.VMEM((H, HEAD_DIM, HEAD_DIM), F32)],
        compiler_params=_cparams(("arbitrary",)),
    )(qg, kd, u, w, attn, eg, sall, do)


def _post_fn(ogs, za, hw):
    outs = []
    for h, o in enumerate(ogs):
        r = lax.rsqrt(jnp.mean(o * o, axis=-1, keepdims=True) + EPS)
        outs.append(o * r * hw * _silu(za[:, h * HEAD_DIM:(h + 1) * HEAD_DIM]))
    return jnp.concatenate(outs, axis=1)


def _gdn_post(og, proj_m, hw):
    H, T, _ = og.shape
    A = H * HEAD_DIM
    tm = _pick(T, (512, 256, 128))

    def body(og_ref, za_ref, hw_ref, o_ref, ot_ref):
        o = _post_fn(tuple(og_ref[h] for h in range(H)), za_ref[...], hw_ref[...])
        o_ref[...] = o.astype(BF16)
        ot_ref[...] = o.T.astype(BF16)

    return pl.pallas_call(
        body, name="gdn_post", grid=(T // tm,),
        in_specs=[pl.BlockSpec((H, tm, HEAD_DIM), lambda i: (0, i, 0)),
                  pl.BlockSpec((tm, A), lambda i: (i, ZA_BLOCK)),
                  pl.BlockSpec((1, HEAD_DIM), lambda i: (0, 0))],
        out_specs=[pl.BlockSpec((tm, A), lambda i: (i, 0)), pl.BlockSpec((A, tm), lambda i: (0, i))],
        out_shape=[jax.ShapeDtypeStruct((T, A), BF16), jax.ShapeDtypeStruct((A, T), BF16)],
        compiler_params=_cparams(("parallel",)),
    )(og, proj_m, hw)


def _gdn_post_bwd(og, proj_m, hw, d_o, dproj):
    H, T, _ = og.shape
    A = H * HEAD_DIM
    tm = _pick(T, (256, 128))

    def body(og_ref, za_ref, hw_ref, do_ref, _, dog_ref, dza_ref, dhw_ref):
        _, vjp = jax.vjp(_post_fn, tuple(og_ref[h] for h in range(H)), za_ref[...], hw_ref[...])
        dog, dza, dhw = vjp(do_ref[...])
        for h in range(H):
            dog_ref[h] = dog[h]
        dza_ref[...] = dza.astype(BF16)

        @pl.when(pl.program_id(0) == 0)
        def _():
            dhw_ref[...] = dhw

        @pl.when(pl.program_id(0) > 0)
        def _():
            dhw_ref[...] += dhw

    return pl.pallas_call(
        body, name="gdn_post_bwd", grid=(T // tm,),
        in_specs=[pl.BlockSpec((H, tm, HEAD_DIM), lambda i: (0, i, 0)),
                  pl.BlockSpec((tm, A), lambda i: (i, ZA_BLOCK)),
                  pl.BlockSpec((1, HEAD_DIM), lambda i: (0, 0)),
                  pl.BlockSpec((tm, A), lambda i: (i, 0)), ANY],
        out_specs=[pl.BlockSpec((H, tm, HEAD_DIM), lambda i: (0, i, 0)),
                   pl.BlockSpec((tm, A), lambda i: (i, ZA_BLOCK)),
                   pl.BlockSpec((1, HEAD_DIM), lambda i: (0, 0))],
        out_shape=[jax.ShapeDtypeStruct((H, T, HEAD_DIM), F32), jax.ShapeDtypeStruct(dproj.shape, dproj.dtype),
                   jax.ShapeDtypeStruct((1, HEAD_DIM), F32)],
        input_output_aliases={4: 1},
        compiler_params=_cparams(("arbitrary",)),
    )(og, proj_m, hw, d_o, dproj)


def _sgu_fn(ub, vb, zb, lw, lb, W, bbc):
    G = len(W)
    tm = ub.shape[0]
    mu = jnp.mean(vb, axis=-1, keepdims=True)
    xc = vb - mu
    var = jnp.mean(xc * xc, axis=-1, keepdims=True)
    vn = xc * lax.rsqrt(var + EPS) * lw + lb
    mask = _iota((CHUNK_B, CHUNK_B), 0) >= _iota((CHUNK_B, CHUNK_B), 1)
    cols = []
    for g in range(G):
        wm = jnp.where(mask, W[g], 0.0).astype(BF16)
        rows = []
        for c in range(tm // CHUNK_B):
            blk = vn[c * CHUNK_B:(c + 1) * CHUNK_B, g * HEAD_DIM:(g + 1) * HEAD_DIM].astype(BF16)
            rows.append(_dot(wm, blk) + bbc[g])
        cols.append(jnp.concatenate(rows, axis=0) if len(rows) > 1 else rows[0])
    s = jnp.concatenate(cols, axis=1)
    return ub * s * _silu(zb)


ZA_BLOCK = 6


def _sgu_cols(A, B):
    assert A == B
    return 3, 4, 5


def _sgu_fwd(proj_m, lw, lb, W, bbc, A):
    T = proj_m.shape[0]
    G = W.shape[0]
    B = G * HEAD_DIM
    tm = _pick(T, (256, 128))
    cu, cv, cz = _sgu_cols(A, B)

    def body(u_ref, v_ref, z_ref, lw_ref, lb_ref, w_ref, b_ref, o_ref, ot_ref):
        o = _sgu_fn(u_ref[...], v_ref[...], z_ref[...], lw_ref[...], lb_ref[...],
                    tuple(w_ref[g] for g in range(G)), tuple(b_ref[g] for g in range(G)))
        o_ref[...] = o.astype(BF16)
        ot_ref[...] = o.T.astype(BF16)

    row = pl.BlockSpec((1, B), lambda i: (0, 0))
    cube = pl.BlockSpec((G, CHUNK_B, CHUNK_B), lambda i: (0, 0, 0))
    return pl.pallas_call(
        body, name="sgu_fwd", grid=(T // tm,),
        in_specs=[pl.BlockSpec((tm, B), lambda i: (i, cu)), pl.BlockSpec((tm, B), lambda i: (i, cv)),
                  pl.BlockSpec((tm, B), lambda i: (i, cz)), row, row, cube, cube],
        out_specs=[pl.BlockSpec((tm, B), lambda i: (i, 0)), pl.BlockSpec((B, tm), lambda i: (0, i))],
        out_shape=[jax.ShapeDtypeStruct((T, B), BF16), jax.ShapeDtypeStruct((B, T), BF16)],
        compiler_params=_cparams(("parallel",)),
    )(proj_m, proj_m, proj_m, lw, lb, W, bbc)


def _sgu_bwd(proj_m, lw, lb, W, bbc, d_o, A, dproj):
    T = proj_m.shape[0]
    G = W.shape[0]
    B = G * HEAD_DIM
    tm = _pick(T, (256, 128))
    nt = T // tm
    cu, cv, cz = _sgu_cols(A, B)

    def body(u_ref, v_ref, z_ref, lw_ref, lb_ref, w_ref, b_ref, do_ref, _,
             dp_ref, dlw_ref, dlb_ref, dw_ref, db_ref, dbb_ref):
        _, vjp = jax.vjp(_sgu_fn, u_ref[...], v_ref[...], z_ref[...], lw_ref[...], lb_ref[...],
                         tuple(w_ref[g] for g in range(G)), tuple(b_ref[g] for g in range(G)))
        du, dv, dz, dlw, dlb, dW, dbb = vjp(do_ref[...])
        dW, dbb = jnp.stack(dW, axis=0), jnp.stack(dbb, axis=0)
        dp_ref[:, 0:B] = du.astype(BF16)
        dp_ref[:, B:2 * B] = dv.astype(BF16)
        dp_ref[:, 2 * B:3 * B] = dz.astype(BF16)
        i = pl.program_id(0)

        @pl.when(i == 0)
        def _():
            dlw_ref[...] = dlw
            dlb_ref[...] = dlb
            dw_ref[...] = dW
            dbb_ref[...] = dbb

        @pl.when(i > 0)
        def _():
            dlw_ref[...] += dlw
            dlb_ref[...] += dlb
            dw_ref[...] += dW
            dbb_ref[...] += dbb

        @pl.when(i == nt - 1)
        def _():
            db_ref[...] = jnp.sum(dbb_ref[...], axis=-1, keepdims=True)

    row = pl.BlockSpec((1, B), lambda i: (0, 0))
    cube = pl.BlockSpec((G, CHUNK_B, CHUNK_B), lambda i: (0, 0, 0))
    return pl.pallas_call(
        body, name="sgu_bwd", grid=(nt,),
        in_specs=[pl.BlockSpec((tm, B), lambda i: (i, cu)), pl.BlockSpec((tm, B), lambda i: (i, cv)),
                  pl.BlockSpec((tm, B), lambda i: (i, cz)), row, row, cube, cube,
                  pl.BlockSpec((tm, B), lambda i: (i, A // B)), ANY],
        out_specs=[pl.BlockSpec((tm, 3 * B), lambda i: (i, 1)), row, row, cube,
                   pl.BlockSpec((G, CHUNK_B, 1), lambda i: (0, 0, 0))],
        out_shape=[jax.ShapeDtypeStruct(dproj.shape, dproj.dtype), jax.ShapeDtypeStruct((1, B), F32),
                   jax.ShapeDtypeStruct((1, B), F32), jax.ShapeDtypeStruct((G, CHUNK_B, CHUNK_B), F32),
                   jax.ShapeDtypeStruct((G, CHUNK_B, 1), F32)],
        input_output_aliases={8: 0},
        scratch_shapes=[pltpu.VMEM((G, CHUNK_B, CHUNK_B), F32)],
        compiler_params=_cparams(("arbitrary",)),
    )(proj_m, proj_m, proj_m, lw, lb, W, bbc, d_o, dproj)


def _head_fn(mix, x, fw, tgt):
    h = x + mix
    y = _rms_fn(h, fw)
    e = y - tgt
    return 0.5 * jnp.sum(jnp.mean(e * e, axis=-1, keepdims=True), axis=0, keepdims=True)


def _out_proj_loss(oa, ob, wout, x, tgt, fw):
    T, A = oa.shape
    B = ob.shape[1]
    D = x.shape[1]
    tm = _pick(T, (256, 128))

    def body(oa_ref, ob_ref, w_ref, x_ref, t_ref, fw_ref, dh_ref, dhb_ref, loss_ref, dfw_ref):
        mix = _dot(oa_ref[...], w_ref[0:A, :]) + _dot(ob_ref[...], w_ref[A:A + B, :])
        xv, tv = x_ref[...], t_ref[...]
        loss, vjp = jax.vjp(lambda m, f: _head_fn(m, xv, f, tv), mix, fw_ref[...])
        dh, dfw = vjp(jnp.ones((1, 1), F32))
        dh_ref[...] = dh
        dhb_ref[...] = dh.astype(BF16)
        lrow = jnp.broadcast_to(loss, (1, LANES))

        @pl.when(pl.program_id(0) == 0)
        def _():
            loss_ref[...] = lrow
            dfw_ref[...] = dfw

        @pl.when(pl.program_id(0) > 0)
        def _():
            loss_ref[...] += lrow
            dfw_ref[...] += dfw

    tile = pl.BlockSpec((tm, D), lambda i: (i, 0))
    return pl.pallas_call(
        body, name="out_proj_loss", grid=(T // tm,),
        in_specs=[pl.BlockSpec((tm, A), lambda i: (i, 0)), pl.BlockSpec((tm, B), lambda i: (i, 0)),
                  pl.BlockSpec((A + B, D), lambda i: (0, 0)), tile, tile,
                  pl.BlockSpec((1, D), lambda i: (0, 0))],
        out_specs=[tile, tile, pl.BlockSpec((1, LANES), lambda i: (0, 0)),
                   pl.BlockSpec((1, D), lambda i: (0, 0))],
        out_shape=[jax.ShapeDtypeStruct((T, D), F32), jax.ShapeDtypeStruct((T, D), BF16),
                   jax.ShapeDtypeStruct((1, LANES), F32), jax.ShapeDtypeStruct((1, D), F32)],
        compiler_params=_cparams(("arbitrary",)),
    )(oa, ob, wout, x, tgt, fw)


def _adamw(w, g, m, v, name):
    R, Cn = w.shape
    cap = max(8, 512 * 1024 // Cn)
    tr = max(t for t in range(8, min(R, cap) + 1, 8) if R % t == 0) if R > cap else R

    def body(w_ref, g_ref, m_ref, v_ref, d_ref, mo_ref, vo_ref):
        g = g_ref[...]
        m = ADAM_B1 * m_ref[...] + (1.0 - ADAM_B1) * g
        v = ADAM_B2 * v_ref[...] + (1.0 - ADAM_B2) * jnp.square(g)
        m_hat = m / (1.0 - ADAM_B1 ** ADAM_STEP)
        v_hat = v / (1.0 - ADAM_B2 ** ADAM_STEP)
        d_ref[...] = -ADAM_LR * (m_hat / (jnp.sqrt(v_hat) + ADAM_EPS) + ADAM_WD * w_ref[...])
        mo_ref[...] = m
        vo_ref[...] = v

    tile = pl.BlockSpec((tr, Cn), lambda i: (i, 0))
    shape = jax.ShapeDtypeStruct((R, Cn), F32)
    return pl.pallas_call(
        body, name=name, grid=(R // tr,), in_specs=[tile] * 4, out_specs=[tile] * 3,
        out_shape=[shape] * 3, compiler_params=_cparams(("parallel",)),
    )(w, g, m, v)


def _place():
    x, y, c = lax.axis_index("x"), lax.axis_index("y"), lax.axis_index("c")
    others = [(1 - x, y), (x, 1 - y), (1 - x, 1 - y)]
    return x, y, c, others


def _chip_index(px, py):
    return 2 * px + py


ANY = pl.BlockSpec(memory_space=pl.ANY)


def _gather_ride(blocks, split):
    n = len(blocks)

    def plan(in_refs, out_refs, send_sems, recv_sems):
        x, y, c, _ = _place()
        me, kx, ky, kd = (_chip_index(px, py) for px, py in ((x, y), (1 - x, y), (x, 1 - y), (1 - x, 1 - y)))
        to_x, to_y, to_s = (1 - x, y, c), (x, 1 - y, c), (x, y, 1 - c)

        def copy(sem, src, dst, to):
            return pltpu.make_async_remote_copy(src_ref=src, dst_ref=dst, send_sem=send_sems.at[sem],
                                                recv_sem=recv_sems.at[sem], device_id=to, device_id_type=MESH_ID)

        first, second, third, awaited = [], [], [], []
        for a in range(n):
            out, s0 = out_refs[a], 8 * a
            if not split[a]:
                for j, (k, to) in enumerate(((kx, to_x), (ky, to_y), (kd, (1 - x, 1 - y, c)))):
                    first.append(lambda j=j, to=to, a=a, out=out, s0=s0: copy(s0 + j, in_refs[a], out.at[me], to))
                    awaited.append((lambda j=j, k=k, to=to, out=out, s0=s0: copy(s0 + j, out.at[k], out.at[k], to),
                                    None))
                continue
            h = blocks[a].shape[0] // 2
            q = h // 2
            half = lambda k, core, out=out, h=h: out.at[k, pl.ds(core * h, h), :]
            quarter = lambda k, core, i, out=out, h=h, q=q: out.at[k, pl.ds(core * h + i * q, q), :]
            mine = in_refs[a].at[pl.ds(c * h, h), :]
            first.append(lambda s0=s0, mine=mine, half=half: copy(s0, mine, half(me, c), to_x))
            first.append(lambda s0=s0, mine=mine, half=half: copy(s0 + 1, mine, half(me, c), to_y))
            fwd0 = lambda s0=s0, quarter=quarter: copy(s0 + 2, quarter(kx, c, 0), quarter(kx, c, 0), to_y)
            fwd1 = lambda s0=s0, quarter=quarter: copy(s0 + 3, quarter(ky, c, 1), quarter(ky, c, 1), to_x)
            pieces = [(s0 + 0, lambda half=half: half(kx, c), lambda half=half: half(kx, 1 - c), to_x, fwd0),
                      (s0 + 1, lambda half=half: half(ky, c), lambda half=half: half(ky, 1 - c), to_y, fwd1),
                      (s0 + 2, lambda quarter=quarter: quarter(kd, c, 0), lambda quarter=quarter: quarter(kd, 1 - c, 0),
                       to_y, None),
                      (s0 + 3, lambda quarter=quarter: quarter(kd, c, 1), lambda quarter=quarter: quarter(kd, 1 - c, 1),
                       to_x, None)]
            for i, (sem, here, there, frm, fwd) in enumerate(pieces):
                passing = lambda s0=s0, i=i, here=here: copy(s0 + 4 + i, here(), here(), to_s)
                awaited.append((lambda sem=sem, here=here, frm=frm: copy(sem, here(), here(), frm), (fwd, passing)))
                if fwd is not None:
                    second.append(fwd)
                third.append((passing, lambda s0=s0, i=i, there=there: copy(s0 + 4 + i, there(), there(), to_s)))
        return first, second, third, awaited

    def start(*refs):
        for send in plan(*refs)[0]:
            send().start()

    def finish(*refs):
        first, second, third, awaited = plan(*refs)
        for arrival, then in awaited:
            arrival().wait_recv()
            for nxt in (then or ()):
                if nxt is not None:
                    nxt().start()
        for _, from_sibling in third:
            from_sibling().wait_recv()
        for send in first + second + [p for p, _ in third]:
            send().wait_send()

    shapes = [jax.ShapeDtypeStruct((N_CHIPS,) + b.shape, b.dtype) for b in blocks]
    return _Ride(blocks, shapes, 8 * n, start, finish)


def _put_own(gathered, own):
    me = _chip_index(lax.axis_index("x"), lax.axis_index("y"))
    return lax.dynamic_update_index_in_dim(gathered, own, me, 0)


def _allreduce_small(buf):
    R0, L = buf.shape
    R = -(-R0 // 16) * 16
    h = R // 2
    buf = jnp.pad(buf, ((0, R - R0), (0, 0)))

    def body(in_ref, out_ref, sib_ref, pair_ref, chips_ref, send_sems, recv_sems):
        x, y, c, others = _place()
        me = _chip_index(x, y)
        sibling = (x, y, 1 - c)

        def copy(sem, src, dst, to):
            return pltpu.make_async_remote_copy(src_ref=src, dst_ref=dst, send_sem=send_sems.at[sem],
                                                recv_sem=recv_sems.at[sem], device_id=to, device_id_type=MESH_ID)

        cp = copy(0, in_ref, sib_ref, sibling)
        cp.start()
        cp.wait()
        pair_ref[...] = in_ref[...] + sib_ref[...]
        rows = lambda core: pl.ds(pl.multiple_of(core * h, 8), h)
        sends = [copy(1 + j, pair_ref.at[rows(c), :], chips_ref.at[me], (*chip, c)) for j, chip in enumerate(others)]
        for s in sends:
            s.start()
        chips_ref[me] = pair_ref[rows(c), :]
        for j, chip in enumerate(others):
            k = _chip_index(*chip)
            copy(1 + j, chips_ref.at[k], chips_ref.at[k], (*chip, c)).wait_recv()
        out_ref[rows(c), :] = ((chips_ref[0] + chips_ref[1]) + chips_ref[2]) + chips_ref[3]
        swap = copy(4, out_ref.at[rows(c), :], out_ref.at[rows(c), :], sibling)
        swap.start()
        copy(4, out_ref.at[rows(1 - c), :], out_ref.at[rows(1 - c), :], sibling).wait_recv()
        for s in sends + [swap]:
            s.wait_send()

    vm = pl.BlockSpec(memory_space=pltpu.VMEM)
    return pl.pallas_call(
        body, name="allreduce_small", in_specs=[vm], out_specs=vm,
        out_shape=jax.ShapeDtypeStruct((R, L), F32),
        scratch_shapes=[pltpu.VMEM((R, L), F32), pltpu.VMEM((R, L), F32), pltpu.VMEM((N_CHIPS, h, L), F32),
                        pltpu.SemaphoreType.DMA((5,)), pltpu.SemaphoreType.DMA((5,))],
        compiler_params=pltpu.CompilerParams(vmem_limit_bytes=VMEM_LIMIT),
    )(buf)[:R0]


def _pair_ride(g):
    nb, R, Cn = g.shape
    h = R // 2

    def copy(in_refs, out_refs, send_sems, recv_sems):
        x, y, c, _ = _place()
        return pltpu.make_async_remote_copy(src_ref=in_refs[0].at[:, pl.ds((1 - c) * h, h), :], dst_ref=out_refs[0],
                                            send_sem=send_sems.at[0], recv_sem=recv_sems.at[0],
                                            device_id=(x, y, 1 - c), device_id_type=MESH_ID)

    return _Ride([g], [jax.ShapeDtypeStruct((nb, h, Cn), g.dtype)], 1,
                 lambda *refs: copy(*refs).start(), lambda *refs: copy(*refs).wait())


def _pair_sum(g, land, c_arr, name, ride=None):
    nb, R, Cn = g.shape
    hr = R // 2
    tr = _pick(hr, (256, 128, 64, 32, 16))
    nt = hr // tr

    def body(c_ref, g_ref, l_ref, o_ref):
        o_ref[...] = (g_ref[...].astype(F32) + l_ref[...].astype(F32)).astype(BF16)

    return _pallas(
        body, (c_arr, g, land), name=name, prefetch=1, grid=(nb, nt),
        in_specs=[pl.BlockSpec((1, tr, Cn), lambda b, i, c_ref: (b, c_ref[0] * nt + i, 0)),
                  pl.BlockSpec((1, tr, Cn), lambda b, i, c_ref: (b, i, 0))],
        out_specs=pl.BlockSpec((1, tr, Cn), lambda b, i, c_ref: (b, i, 0)),
        out_shape=jax.ShapeDtypeStruct((nb, hr, Cn), BF16),
        semantics=("parallel", "parallel"), ride=ride)


def _chip_ride(parts):
    m = len(parts)

    def copies(in_refs, out_refs, send_sems, recv_sems):
        x, y, c, others = _place()
        me = _chip_index(x, y)
        def mk(j, chip, n, landing):
            k = _chip_index(*chip)
            return pltpu.make_async_remote_copy(
                src_ref=in_refs[n].at[k], dst_ref=out_refs[n].at[landing(k)], send_sem=send_sems.at[m * j + n],
                recv_sem=recv_sems.at[m * j + n], device_id=(*chip, c), device_id_type=MESH_ID)

        pairs = [(j, chip, n) for j, chip in enumerate(others) for n in range(m)]
        return pairs, (lambda *p: mk(*p, lambda k: me)), (lambda *p: mk(*p, lambda k: k))

    def start(*refs):
        pairs, send, _ = copies(*refs)
        for p in pairs:
            send(*p).start()

    def finish(*refs):
        pairs, send, arrival = copies(*refs)
        for p in pairs:
            arrival(*p).wait_recv()
        for p in pairs:
            send(*p).wait_send()

    return _Ride(parts, [jax.ShapeDtypeStruct(p.shape, p.dtype) for p in parts], 3 * m, start, finish)


def _put_own_slot(q, p):
    me = _chip_index(lax.axis_index("x"), lax.axis_index("y"))
    return lax.dynamic_update_index_in_dim(q, lax.dynamic_index_in_dim(p, me, 0, keepdims=False), me, 0)


def _chip_sum(q, c_arr, name):
    nb, hr, Cn = q.shape
    tr = _pick(hr, (256, 128, 64, 32, 16))
    nt = hr // tr

    def body(c_ref, q_ref, o_ref):
        f = lambda k: q_ref[k].astype(F32)
        o_ref[...] = ((f(0) + f(1)) + f(2)) + f(3)

    return _pallas(
        body, (c_arr, q), name=name, prefetch=1, grid=(nt,),
        in_specs=[pl.BlockSpec((nb, tr, Cn), lambda i, c_ref: (0, i, 0))],
        out_specs=pl.BlockSpec((tr, Cn), lambda i, c_ref: (c_ref[0] * nt + i, 0)),
        out_shape=jax.ShapeDtypeStruct((2 * hr, Cn), F32),
        semantics=("parallel",))


def _sibling_fill(fw, fo):
    def body(_, __, fw_ref, fo_ref, send_sems, recv_sems):
        x, y, c, _ = _place()
        copies = []
        for n, ref in enumerate((fw_ref, fo_ref)):
            h = ref.shape[0] // 2
            mine = ref.at[pl.ds(c * h, h), :]
            theirs = ref.at[pl.ds((1 - c) * h, h), :]
            mk = lambda src, dst: pltpu.make_async_remote_copy(
                src_ref=src, dst_ref=dst, send_sem=send_sems.at[n], recv_sem=recv_sems.at[n],
                device_id=(x, y, 1 - c), device_id_type=MESH_ID)
            send = mk(mine, mine)
            send.start()
            copies.append((send, mk(theirs, theirs)))
        for send, arrival in copies:
            arrival.wait_recv()
            send.wait_send()

    return pl.pallas_call(
        body, name="sibling_fill", in_specs=[ANY, ANY], out_specs=[ANY, ANY],
        out_shape=[jax.ShapeDtypeStruct(fw.shape, F32), jax.ShapeDtypeStruct(fo.shape, F32)],
        input_output_aliases={0: 0, 1: 1},
        scratch_shapes=[pltpu.SemaphoreType.DMA((2,)), pltpu.SemaphoreType.DMA((2,))],
        compiler_params=pltpu.CompilerParams(has_side_effects=True),
    )(fw, fo)


class _Layout:
    def __init__(self, H, G, nb, Cb):
        A, B = H * HEAD_DIM, G * HEAD_DIM
        self.n_main = 4 * A + 3 * B
        self.k = -(-(self.n_main + LANES) // WIN_BLOCK) * WIN_BLOCK
        cuts = [0, 3 * A, 4 * A, 4 * A + 2 * H, nb * Cb]
        starts = [0, 3 * A + 3 * B, self.n_main, 3 * A]
        self.pieces = []
        self.windows, self.runs = [], []
        for n in range(nb):
            segs = []
            for s in range(4):
                lo, hi = max(cuts[s], n * Cb), min(cuts[s + 1], (n + 1) * Cb)
                if lo < hi:
                    segs.append((starts[s] + lo - cuts[s], lo - n * Cb, hi - lo))
            self.pieces += [(own, n, col, ln) for own, col, ln in segs]
            blocks = sorted({b for own, _, ln in segs for b in range(own // WIN_BLOCK, (own + ln - 1) // WIN_BLOCK + 1)})
            self.windows.append(blocks)
            self.runs.append([(blocks.index(own // WIN_BLOCK) * WIN_BLOCK + own % WIN_BLOCK, ln)
                              for own, _, ln in segs])
        self.wb = max(len(b) for b in self.windows)
        self.table = [b + [b[-1]] * (self.wb - len(b)) for b in self.windows]
        self.pieces.sort()

    def to_own_order(self, g_in):
        D = g_in.shape[1]
        cols, at = [], 0
        for own, n, col, ln in self.pieces:
            if own > at:
                cols.append(jnp.zeros((D, own - at), g_in.dtype))
            cols.append(g_in[n, :, col:col + ln])
            at = own + ln
        if at < self.k:
            cols.append(jnp.zeros((D, self.k - at), g_in.dtype))
        return jnp.concatenate(cols, axis=1)

    def from_window(self, win, chip, Cb):
        pick = lambda runs: (lambda w: jnp.concatenate([w[:, c:c + ln] for c, ln in runs], axis=1))
        return lax.switch(chip, [pick(r) for r in self.runs], win)


def _device_step(x, tgt, norm_w, win_b, wout_b, conv_b, a_log, dt_bias, head_norm_w, sgu_ln_w, sgu_ln_b,
                 w_spatial, b_spatial, final_norm_w, c_arr):
    T, D = x.shape
    H = a_log.shape[1]
    A = H * HEAD_DIM
    G = w_spatial.shape[0]
    B = G * HEAD_DIM
    nb, Cb, Rb = N_CHIPS, win_b.shape[1], wout_b.shape[0]
    lay = _Layout(H, G, nb, Cb)
    alog_row = jnp.pad(a_log, ((0, 0), (H, LANES - 2 * H)))
    dtb_row = jnp.pad(dt_bias, ((0, 0), (H, LANES - 2 * H)))
    bbc = jnp.broadcast_to(b_spatial[:, :, None], (G, CHUNK_B, CHUNK_B))

    (xn, xn_t), (g_in,) = _rms_in(x, norm_w, ride=_gather_ride([win_b], [True]))
    w_own = lay.to_own_order(_put_own(g_in, win_b))
    proj_m, (g_out, g_conv) = _mm_nn(xn, w_own, F32, "in_proj", tm=2048, cols=(0, lay.n_main),
                                     ride=_gather_ride([wout_b, conv_b], [False, False]))
    wout = _put_own(g_out, wout_b).reshape(nb * Rb, D)
    conv_w = _put_own(g_conv, conv_b).transpose(1, 0, 2).reshape(CONV_WIDTH, nb * conv_b.shape[1])
    proj_ba = _mm_nn(xn, w_own, F32, "in_proj_ba", cols=(lay.n_main, LANES))
    q, k, v, gb, bb = _gdn_pre(proj_m, proj_ba, conv_w, alog_row, dtb_row, H)
    u, w, qg, kd, attn, eg, pinv = _gdn_prep(q, k, v, gb, bb)
    og, sall = _gdn_chain(qg, kd, u, w, attn, eg)
    oa, oa_t = _gdn_post(og, proj_m, head_norm_w)
    ob, ob_t = _sgu_fwd(proj_m, sgu_ln_w, sgu_ln_b, w_spatial, bbc, A)
    dh, dhb, loss_row, d_fnw = _out_proj_loss(oa, ob, wout, x, tgt, final_norm_w.reshape(1, D))

    d_o = _mm_nn(dhb, wout.T, F32, "out_proj_dx")
    dproj = lax.empty((T, lay.k), BF16)
    dproj, d_lw, d_lb, d_ws, d_bs = _sgu_bwd(proj_m, sgu_ln_w, sgu_ln_b, w_spatial, bbc, d_o, A, dproj)
    dog, dproj, d_hw = _gdn_post_bwd(og, proj_m, head_norm_w, d_o, dproj)
    dqg, dkd, du, dw, dat, deg = _gdn_chain_bwd(qg, kd, u, w, attn, eg, sall, dog)
    dq, dk, dv, dgb, dbb = _gdn_prep_bwd(q, k, v, gb, bb, pinv, du, dw, dqg, dkd, dat, deg)
    dc, dproj, d_al, d_dt = _gdn_pre_bwd(proj_m, proj_ba, conv_w, alog_row, dtb_row, dq, dk, dv, dgb, dbb, H,
                                         dproj)
    dproj, d_conv = _conv_bwd(proj_m, dc, conv_w, H, dproj)

    table = jnp.array([b for row in lay.table for b in row], jnp.int32)
    d_win = _mm_windows(xn_t, dproj, table, nb, "in_proj_dw")
    d_wout, (land_w,) = _mm_nn_pair(oa_t, ob_t, dhb, "out_proj_dw", ride=_pair_ride(d_win))
    d_wout = d_wout.reshape(nb, Rb, D)
    pair_w, (land_o,) = _pair_sum(d_win, land_w, c_arr, "pair_sum_w_in", ride=_pair_ride(d_wout))
    pair_o = _pair_sum(d_wout, land_o, c_arr, "pair_sum_w_out")
    dxn, (all_w,) = _mm_nt_rhs_outer(dproj, w_own, F32, "in_proj_dx", ride=_chip_ride([pair_w]))
    (grad_x, d_nw), (all_o,) = _rms_in_bwd(x, norm_w, dxn, dh, ride=_chip_ride([pair_o]))
    all_w, all_o = _put_own_slot(all_w, pair_w), _put_own_slot(all_o, pair_o)
    small = dict(norm_w=d_nw, conv_w=d_conv[:CONV_WIDTH], a_log=d_al[:, H:2 * H], dt_bias=d_dt[:, H:2 * H],
                 head_norm_w=d_hw, sgu_ln_w=d_lw, sgu_ln_b=d_lb, w_spatial=d_ws, b_spatial=d_bs[:, :, 0],
                 final_norm_w=d_fnw)
    return loss_row, grad_x, small, all_w, all_o


SMALL = ("norm_w", "conv_w", "a_log", "dt_bias", "head_norm_w", "sgu_ln_w", "sgu_ln_b", "w_spatial",
         "b_spatial", "final_norm_w")


def _pack(parts):
    rows = []
    for p in parts:
        f = p.reshape(-1)
        f = jnp.pad(f, (0, (-f.shape[0]) % (8 * LANES)))
        rows.append(f.reshape(-1, LANES))
    return jnp.concatenate(rows, axis=0)


def _unpack(buf, shapes):
    out, r = [], 0
    for s in shapes:
        n = 1
        for d in s:
            n *= d
        nr = -(-n // (8 * LANES)) * 8
        out.append(buf[r:r + nr].reshape(-1)[:n].reshape(s))
        r += nr
    return out


def kernel(x, norm_w, w_in, conv_w, a_log, dt_bias, head_norm_w, sgu_ln_w, sgu_ln_b, w_spatial, b_spatial, w_out, final_norm_w, loss_target, m_norm_w, m_w_in, m_conv_w, m_a_log, m_dt_bias, m_head_norm_w, m_sgu_ln_w, m_sgu_ln_b, m_w_spatial, m_b_spatial, m_w_out, m_final_norm_w, v_norm_w, v_w_in, v_conv_w, v_a_log, v_dt_bias, v_head_norm_w, v_sgu_ln_w, v_sgu_ln_b, v_w_spatial, v_b_spatial, v_w_out, v_final_norm_w):
    T, D = x.shape[1], x.shape[2]
    weights = dict(norm_w=norm_w, w_in=w_in, conv_w=conv_w, a_log=a_log, dt_bias=dt_bias, head_norm_w=head_norm_w,
                   sgu_ln_w=sgu_ln_w, sgu_ln_b=sgu_ln_b, w_spatial=w_spatial, b_spatial=b_spatial, w_out=w_out,
                   final_norm_w=final_norm_w)
    mom_m = dict(norm_w=m_norm_w, w_in=m_w_in, conv_w=m_conv_w, a_log=m_a_log, dt_bias=m_dt_bias,
                 head_norm_w=m_head_norm_w, sgu_ln_w=m_sgu_ln_w, sgu_ln_b=m_sgu_ln_b, w_spatial=m_w_spatial,
                 b_spatial=m_b_spatial, w_out=m_w_out, final_norm_w=m_final_norm_w)
    mom_v = dict(norm_w=v_norm_w, w_in=v_w_in, conv_w=v_conv_w, a_log=v_a_log, dt_bias=v_dt_bias,
                 head_norm_w=v_head_norm_w, sgu_ln_w=v_sgu_ln_w, sgu_ln_b=v_sgu_ln_b, w_spatial=v_w_spatial,
                 b_spatial=v_b_spatial, w_out=v_w_out, final_norm_w=v_final_norm_w)
    me = _chip_index(lax.axis_index("x"), lax.axis_index("y"))
    c_arr = lax.axis_index("c").astype(jnp.int32).reshape(1)
    Din, Cb = w_in.shape[1], w_in.shape[2]
    Rb = w_out.shape[1]
    cconv = conv_w.shape[2]

    loss_row, grad_x, g, qw, qo = _device_step(
        x[0], loss_target[0], norm_w, w_in[0].astype(BF16), w_out[0].astype(BF16), conv_w[0], a_log, dt_bias,
        head_norm_w, sgu_ln_w, sgu_ln_b, w_spatial[0], b_spatial[0], final_norm_w, c_arr)

    small_shapes = [tuple(g[n].shape) for n in SMALL]
    small = _allreduce_small(_pack([g[n] for n in SMALL]))
    gsum_in, gsum_out = _sibling_fill(_chip_sum(qw, c_arr, "chip_sum_w_in"), _chip_sum(qo, c_arr, "chip_sum_w_out"))
    gsum_in = _Layout(a_log.shape[1], w_spatial.shape[1], N_CHIPS, Cb).from_window(gsum_in, me, Cb)
    gsmall = dict(zip(SMALL, _unpack(small, small_shapes)))
    gsmall["conv_w"] = lax.dynamic_slice_in_dim(gsmall["conv_w"], me * cconv, cconv, axis=1)

    grads, deltas, new_m, new_v = {}, {}, {}, {}
    d, m2, v2 = _adamw(w_out[0], gsum_out, m_w_out[0], v_w_out[0], "adamw_w_out")
    grads["w_out"], deltas["w_out"], new_m["w_out"], new_v["w_out"] = gsum_out[None], d[None], m2[None], v2[None]
    flat = lambda a: a.transpose(2, 0, 1).reshape(-1, LANES)
    unflat = lambda f: f.reshape(Cb, 1, Din).transpose(1, 2, 0)
    g_flat = gsum_in.T.reshape(-1, LANES)
    d, m2, v2 = _adamw(flat(w_in), g_flat, flat(m_w_in), flat(v_w_in), "adamw_w_in")
    grads["w_in"], deltas["w_in"], new_m["w_in"], new_v["w_in"] = unflat(g_flat), unflat(d), unflat(m2), unflat(v2)
    shapes = [tuple(weights[n].shape) for n in SMALL]
    ds, ms, vs = _adamw(_pack([weights[n] for n in SMALL]), _pack([gsmall[n] for n in SMALL]),
                        _pack([mom_m[n] for n in SMALL]), _pack([mom_v[n] for n in SMALL]), "adamw_small")
    for n, gq, d, m2, v2 in zip(SMALL, [gsmall[n] for n in SMALL], _unpack(ds, shapes), _unpack(ms, shapes),
                                _unpack(vs, shapes)):
        grads[n], deltas[n], new_m[n], new_v[n] = gq.reshape(weights[n].shape), d, m2, v2

    loss = lax.psum(loss_row[0, 0], ("x", "y", "c"))
    order = ("norm_w", "w_in", "conv_w", "a_log", "dt_bias", "head_norm_w", "sgu_ln_w", "sgu_ln_b", "w_spatial",
             "b_spatial", "w_out", "final_norm_w")
    return (loss, grad_x[None], *[grads[n] for n in order], *[deltas[n] for n in order],
            *[new_m[n] for n in order], *[new_v[n] for n in order])
```

```python
import functools

import jax
import jax.numpy as jnp
from jax import lax
from jax.experimental import pallas as pl
from jax.experimental.pallas import tpu as pltpu

F32 = jnp.float32
BF16 = jnp.bfloat16
EPS = 1e-6
HEAD_DIM = 128
CHUNK_B = 128
CONV_WIDTH = 4
LANES = 128
HALO = 8
N_CHIPS = 4
ADAM_LR = 0.001
ADAM_B1 = 0.9
ADAM_B2 = 0.999
ADAM_EPS = 1e-08
ADAM_WD = 0.01
ADAM_STEP = 10
VMEM_LIMIT = 56 * 1024 * 1024
MESH_ID = pl.DeviceIdType.MESH


def _cparams(sem=None, **kw):
    return pltpu.CompilerParams(dimension_semantics=sem, vmem_limit_bytes=VMEM_LIMIT, **kw)


def _matmul(a, b, ca, cb):
    nb = a.ndim - 2
    batch = tuple(range(nb))
    return lax.dot_general(a, b, (((ca + nb,), (cb + nb,)), (batch, batch)), preferred_element_type=F32)


def _dot(a, b):
    return _matmul(a, b, 1, 0)


def _dot_nt(a, b):
    return _matmul(a, b, 1, 1)


def _dot_tn(a, b):
    return _matmul(a, b, 0, 0)


def _iota(shape, dim):
    return lax.broadcasted_iota(jnp.int32, shape, dim)


def _sigmoid(x):
    return 0.5 * (jnp.tanh(0.5 * x) + 1.0)


def _silu(x):
    return x * _sigmoid(x)


def _softplus(x):
    z = jnp.exp(-jnp.abs(x))
    small = z * (1.0 - z * (0.5 - z * (1.0 / 3.0)))
    return jnp.maximum(x, 0.0) + jnp.where(z < 1e-3, small, jnp.log(1.0 + z))


def _pick(n, pref):
    for t in pref:
        if n % t == 0:
            return t
    return n


class _Ride:
    def __init__(self, operands, out_shape, n_sems, start, finish):
        self.operands, self.out_shape, self.n_sems = list(operands), list(out_shape), n_sems
        self.start, self.finish = start, finish


def _pallas(body, operands, *, name, grid, in_specs, out_specs, out_shape, semantics, scratch_shapes=(),
            prefetch=0, ride=None):
    single = not isinstance(out_shape, (list, tuple))
    outs = [out_shape] if single else list(out_shape)
    ospecs = [out_specs] if single else list(out_specs)
    in_specs, scratch = list(in_specs), list(scratch_shapes)
    n_in, n_out, n_sc = len(operands) - prefetch, len(outs), len(scratch)
    kernel = body
    params = _cparams(semantics)
    if ride is not None:
        n_xin, n_xout = len(ride.operands), len(ride.out_shape)

        def kernel(*refs):
            pre, refs = refs[:prefetch], refs[prefetch:]
            ins, refs = refs[:n_in], refs[n_in:]
            xins, refs = refs[:n_xin], refs[n_xin:]
            mains, refs = refs[:n_out], refs[n_out:]
            xouts, refs = refs[:n_xout], refs[n_xout:]
            sc, (send, recv) = refs[:n_sc], refs[n_sc:]
            ids = [pl.program_id(a) for a in range(len(grid))]
            first = functools.reduce(jnp.logical_and, [i == 0 for i in ids])
            last = functools.reduce(jnp.logical_and, [i == g - 1 for i, g in zip(ids, grid)])

            @pl.when(first)
            def _():
                ride.start(xins, xouts, send, recv)

            body(*pre, *ins, *mains, *sc)

            @pl.when(last)
            def _():
                ride.finish(xins, xouts, send, recv)

        operands = list(operands) + ride.operands
        in_specs += [ANY] * n_xin
        ospecs += [ANY] * n_xout
        outs += ride.out_shape
        scratch += [pltpu.SemaphoreType.DMA((ride.n_sems,)), pltpu.SemaphoreType.DMA((ride.n_sems,))]
        params = _cparams(("arbitrary",) * len(grid), has_side_effects=True)
    if prefetch:
        spec = dict(grid_spec=pltpu.PrefetchScalarGridSpec(
            num_scalar_prefetch=prefetch, grid=grid, in_specs=in_specs, out_specs=ospecs, scratch_shapes=scratch))
    else:
        spec = dict(grid=grid, in_specs=in_specs, out_specs=ospecs, scratch_shapes=scratch)
    res = pl.pallas_call(kernel, name=name, out_shape=outs, compiler_params=params, **spec)(*operands)
    main = res[0] if single else list(res[:n_out])
    return main if ride is None else (main, list(res[n_out:]))


def _mm_nn(a, b, out_dtype, name, tm=1024, tn=512, tk=None, cols=None, ride=None):
    M, K = a.shape
    c0, N = (0, b.shape[1]) if cols is None else cols
    tm = _pick(M, (tm, 1024, 512, 256, 128))
    tn = _pick(N, (tn, 512, 384, 256, 128))
    tk = K if tk is None else _pick(K, (tk,))
    nk = K // tk
    j0 = c0 // tn
    assert c0 % tn == 0

    def body(a_ref, b_ref, o_ref, *scratch):
        part = _dot(a_ref[...], b_ref[...])
        if nk == 1:
            o_ref[...] = part.astype(out_dtype)
        else:
            acc_ref, = scratch
            k = pl.program_id(2)

            @pl.when(k == 0)
            def _():
                acc_ref[...] = part

            @pl.when(k > 0)
            def _():
                acc_ref[...] += part

            @pl.when(k == nk - 1)
            def _():
                o_ref[...] = acc_ref[...].astype(out_dtype)

    return _pallas(
        body, (a, b), name=name, grid=(M // tm, N // tn, nk),
        in_specs=[pl.BlockSpec((tm, tk), lambda i, j, k: (i, k)),
                  pl.BlockSpec((tk, tn), lambda i, j, k: (k, j + j0))],
        out_specs=pl.BlockSpec((tm, tn), lambda i, j, k: (i, j)),
        out_shape=jax.ShapeDtypeStruct((M, N), out_dtype),
        scratch_shapes=[] if nk == 1 else [pltpu.VMEM((tm, tn), F32)],
        semantics=("parallel", "parallel", "arbitrary"), ride=ride)


def _mm_nt_rhs_outer(a, b, out_dtype, name, tm=256, tn=1024, ride=None):
    M, K = a.shape
    N, _ = b.shape
    tm = _pick(M, (tm, 128))
    tn = _pick(N, (tn, 512, 256, 128))

    def body(a_ref, b_ref, o_ref):
        o_ref[...] = _dot_nt(a_ref[...], b_ref[...]).astype(out_dtype)

    return _pallas(
        body, (a, b), name=name, grid=(N // tn, M // tm),
        in_specs=[pl.BlockSpec((tm, K), lambda j, i: (i, 0)),
                  pl.BlockSpec((tn, K), lambda j, i: (j, 0))],
        out_specs=pl.BlockSpec((tm, tn), lambda j, i: (i, j)),
        out_shape=jax.ShapeDtypeStruct((M, N), out_dtype),
        semantics=("parallel", "parallel"), ride=ride)


WIN_BLOCK = 256


def _mm_windows(a, b, table, nb, name, tm=2048):
    M, K = a.shape
    wb = table.shape[0] // nb
    tm = _pick(M, (tm, 1024, 512, 256, 128))

    def body(tab_ref, a_ref, b_ref, o_ref):
        o_ref[0] = _dot(a_ref[...], b_ref[...]).astype(BF16)

    return pl.pallas_call(
        body, name=name,
        grid_spec=pltpu.PrefetchScalarGridSpec(
            num_scalar_prefetch=1, grid=(nb, M // tm, wb),
            in_specs=[pl.BlockSpec((tm, K), lambda n, i, t, tab: (i, 0)),
                      pl.BlockSpec((K, WIN_BLOCK), lambda n, i, t, tab: (0, tab[n * wb + t]))],
            out_specs=pl.BlockSpec((1, tm, WIN_BLOCK), lambda n, i, t, tab: (n, i, t))),
        out_shape=jax.ShapeDtypeStruct((nb, M, wb * WIN_BLOCK), BF16),
        compiler_params=_cparams(("parallel", "parallel", "arbitrary")),
    )(table, a, b)


def _mm_nn_pair(a0, a1, b, name, tm=512, tn=1024, ride=None):
    M, K = a0.shape
    _, N = b.shape
    tm = _pick(M, (tm, 256, 128))
    tn = _pick(N, (tn, 512, 256, 128))
    ni = M // tm

    def body(a0_ref, a1_ref, b_ref, o_ref):
        p = pl.program_id(0)

        @pl.when(p == 0)
        def _():
            o_ref[...] = _dot(a0_ref[...], b_ref[...]).astype(BF16)

        @pl.when(p == 1)
        def _():
            o_ref[...] = _dot(a1_ref[...], b_ref[...]).astype(BF16)

    return _pallas(
        body, (a0, a1, b), name=name, grid=(2, ni, N // tn),
        in_specs=[pl.BlockSpec((tm, K), lambda p, i, j: (i * (1 - p), 0)),
                  pl.BlockSpec((tm, K), lambda p, i, j: (i * p, 0)),
                  pl.BlockSpec((K, tn), lambda p, i, j: (0, j))],
        out_specs=pl.BlockSpec((tm, tn), lambda p, i, j: (p * ni + i, j)),
        out_shape=jax.ShapeDtypeStruct((2 * M, N), BF16),
        semantics=("parallel", "parallel", "parallel"), ride=ride)


def _rms_fn(x, w):
    r = lax.rsqrt(jnp.mean(x * x, axis=-1, keepdims=True) + EPS)
    return x * r * w


def _rms_in(x, w, ride=None):
    T, D = x.shape
    tm = _pick(T, (512, 256, 128))

    def body(x_ref, w_ref, o_ref, ot_ref):
        xn = _rms_fn(x_ref[...], w_ref[...])
        o_ref[...] = xn.astype(BF16)
        ot_ref[...] = xn.T.astype(BF16)

    return _pallas(
        body, (x, w), name="rms_in", grid=(T // tm,),
        in_specs=[pl.BlockSpec((tm, D), lambda i: (i, 0)), pl.BlockSpec((1, D), lambda i: (0, 0))],
        out_specs=[pl.BlockSpec((tm, D), lambda i: (i, 0)), pl.BlockSpec((D, tm), lambda i: (0, i))],
        out_shape=[jax.ShapeDtypeStruct((T, D), BF16), jax.ShapeDtypeStruct((D, T), BF16)],
        semantics=("parallel",), ride=ride)


def _rms_in_bwd(x, w, dxn, dh, ride=None):
    T, D = x.shape
    tm = _pick(T, (256, 128))

    def body(x_ref, w_ref, dxn_ref, dh_ref, gx_ref, dw_ref):
        _, vjp = jax.vjp(_rms_fn, x_ref[...], w_ref[...])
        dx, dw = vjp(dxn_ref[...])
        gx_ref[...] = dh_ref[...] + dx

        @pl.when(pl.program_id(0) == 0)
        def _():
            dw_ref[...] = dw

        @pl.when(pl.program_id(0) > 0)
        def _():
            dw_ref[...] += dw

    tile = pl.BlockSpec((tm, D), lambda i: (i, 0))
    row = pl.BlockSpec((1, D), lambda i: (0, 0))
    return _pallas(
        body, (x, w, dxn, dh), name="rms_in_bwd", grid=(T // tm,),
        in_specs=[tile, row, tile, tile], out_specs=[tile, row],
        out_shape=[jax.ShapeDtypeStruct((T, D), F32), jax.ShapeDtypeStruct((1, D), F32)],
        semantics=("arbitrary",), ride=ride)


def _conv_fwd(cat_ref, halo, x, w):
    tm = x.shape[0]
    cat_ref[0:HALO, :] = halo
    cat_ref[HALO:HALO + tm, :] = x
    c = x * w[CONV_WIDTH - 1:CONV_WIDTH, :]
    for k in range(CONV_WIDTH - 1):
        s = CONV_WIDTH - 1 - k
        c = c + cat_ref[pl.ds(HALO - s, tm), :] * w[k:k + 1, :]
    return c


def _lane_to_all(x, lane):
    @jax.custom_vjp
    def f(x):
        return jnp.broadcast_to(x[:, lane:lane + 1], x.shape)

    def f_fwd(x):
        return f(x), None

    def f_bwd(_, g):
        return (jnp.where(_iota(g.shape, 1) == lane, jnp.sum(g, axis=-1, keepdims=True), 0.0),)

    f.defvjp(f_fwd, f_bwd)
    return f(x)


def _gdn_pointwise(c, ba, alog, dtb, H):
    A = H * HEAD_DIM
    s = _silu(c)
    beta = _sigmoid(ba)
    g = -jnp.exp(alog) * _softplus(ba + dtb)
    qs, ks, vs, gbs, bbs = [], [], [], [], []
    for h in range(H):
        lo = h * HEAD_DIM
        q = s[:, lo:lo + HEAD_DIM]
        k = s[:, A + lo:A + lo + HEAD_DIM]
        qs.append(q * lax.rsqrt(jnp.sum(q * q, axis=-1, keepdims=True) + EPS))
        ks.append(k * lax.rsqrt(jnp.sum(k * k, axis=-1, keepdims=True) + EPS))
        vs.append(s[:, 2 * A + lo:2 * A + lo + HEAD_DIM])
        bbs.append(_lane_to_all(beta, h))
        gbs.append(_lane_to_all(g, H + h))
    st = lambda xs: jnp.stack(xs, axis=0)
    return st(qs), st(ks), st(vs), st(gbs), st(bbs)


def _halo_prev(tm):
    return lambda i: (jnp.maximum(i * (tm // HALO) - 1, 0), 0)


def _gdn_pre(proj_m, xn, w_own, conv_w, alog_row, dtb_row, H):
    T, n_main = proj_m.shape
    D = xn.shape[1]
    A = H * HEAD_DIM
    tm = _pick(T, (256, 128))
    hs = pl.BlockSpec((H, tm, HEAD_DIM), lambda i: (0, i, 0))
    hshape = jax.ShapeDtypeStruct((H, T, HEAD_DIM), F32)

    def body(x_ref, halo_ref, xn_ref, wba_ref, w_ref, al_ref, dt_ref,
             q_ref, k_ref, v_ref, gb_ref, bb_ref, ba_ref, cat_ref):
        halo = jnp.where(pl.program_id(0) == 0, 0.0, halo_ref[...])
        c = _conv_fwd(cat_ref, halo, x_ref[...], w_ref[...])
        ba = _dot(xn_ref[...], wba_ref[...])
        q, k, v, gb, bb = _gdn_pointwise(c, ba, al_ref[...], dt_ref[...], H)
        q_ref[...] = q
        k_ref[...] = k
        v_ref[...] = v
        gb_ref[...] = gb
        bb_ref[...] = bb
        ba_ref[...] = ba

    return pl.pallas_call(
        body, name="gdn_pre", grid=(T // tm,),
        in_specs=[pl.BlockSpec((tm, 3 * A), lambda i: (i, 0)),
                  pl.BlockSpec((HALO, 3 * A), _halo_prev(tm)),
                  pl.BlockSpec((tm, D), lambda i: (i, 0)),
                  pl.BlockSpec((D, LANES), lambda i: (0, n_main // LANES)),
                  pl.BlockSpec((CONV_WIDTH, 3 * A), lambda i: (0, 0)),
                  pl.BlockSpec((1, LANES), lambda i: (0, 0)),
                  pl.BlockSpec((1, LANES), lambda i: (0, 0))],
        out_specs=[hs] * 5 + [pl.BlockSpec((tm, LANES), lambda i: (i, 0))],
        out_shape=[hshape] * 5 + [jax.ShapeDtypeStruct((T, LANES), F32)],
        scratch_shapes=[pltpu.VMEM((HALO + tm, 3 * A), F32)],
        compiler_params=_cparams(("parallel",)),
    )(proj_m, proj_m, xn, w_own, conv_w, alog_row, dtb_row)


def _gdn_pre_bwd(proj_m, proj_ba, conv_w, alog_row, dtb_row, dq, dk, dv, dgb, dbb, H, dproj):
    T, n_main = proj_m.shape
    A = H * HEAD_DIM
    tm = _pick(T, (256, 128))
    hs = pl.BlockSpec((H, tm, HEAD_DIM), lambda i: (0, i, 0))
    row = pl.BlockSpec((1, LANES), lambda i: (0, 0))

    def body(x_ref, halo_ref, ba_ref, w_ref, al_ref, dt_ref, dq_ref, dk_ref, dv_ref, dgb_ref, dbb_ref, _,
             dc_ref, dba_ref, dal_ref, ddt_ref, cat_ref):
        halo = jnp.where(pl.program_id(0) == 0, 0.0, halo_ref[...])
        c = _conv_fwd(cat_ref, halo, x_ref[...], w_ref[...])
        _, vjp = jax.vjp(functools.partial(_gdn_pointwise, H=H), c, ba_ref[...], al_ref[...], dt_ref[...])
        dc, dba, dal, ddt = vjp((dq_ref[...], dk_ref[...], dv_ref[...], dgb_ref[...], dbb_ref[...]))
        dc_ref[...] = dc
        dba_ref[:, :LANES] = dba.astype(BF16)
        dba_ref[:, LANES:] = jnp.zeros((tm, WIN_BLOCK - LANES), BF16)

        @pl.when(pl.program_id(0) == 0)
        def _():
            dal_ref[...] = dal
            ddt_ref[...] = ddt

        @pl.when(pl.program_id(0) > 0)
        def _():
            dal_ref[...] += dal
            ddt_ref[...] += ddt

    return pl.pallas_call(
        body, name="gdn_pre_bwd", grid=(T // tm,),
        in_specs=[pl.BlockSpec((tm, 3 * A), lambda i: (i, 0)),
                  pl.BlockSpec((HALO, 3 * A), _halo_prev(tm)),
                  pl.BlockSpec((tm, LANES), lambda i: (i, 0)),
                  pl.BlockSpec((CONV_WIDTH, 3 * A), lambda i: (0, 0)),
                  row, row, hs, hs, hs, hs, hs, ANY],
        out_specs=[pl.BlockSpec((tm, 3 * A), lambda i: (i, 0)),
                   pl.BlockSpec((tm, WIN_BLOCK), lambda i: (i, n_main // WIN_BLOCK)), row, row],
        out_shape=[jax.ShapeDtypeStruct((T, 3 * A), F32), jax.ShapeDtypeStruct(dproj.shape, dproj.dtype),
                   jax.ShapeDtypeStruct((1, LANES), F32), jax.ShapeDtypeStruct((1, LANES), F32)],
        input_output_aliases={11: 1},
        scratch_shapes=[pltpu.VMEM((HALO + tm, 3 * A), F32)],
        compiler_params=_cparams(("arbitrary",)),
    )(proj_m, proj_m, proj_ba, conv_w, alog_row, dtb_row, dq, dk, dv, dgb, dbb, dproj)


def _conv_bwd(proj_m, dc, conv_w, H, dproj):
    T = proj_m.shape[0]
    A = H * HEAD_DIM
    tm = _pick(T, (256, 128))
    nt = T // tm

    def body(x_ref, halo_ref, dc_ref, nxt_ref, w_ref, _, dx_ref, dw_ref):
        i = pl.program_id(0)
        halo = jnp.where(i == 0, 0.0, halo_ref[...])
        xcat = jnp.concatenate([halo, x_ref[...]], axis=0)
        nxt = jnp.where(i == nt - 1, 0.0, nxt_ref[...])
        dc = dc_ref[...]
        dcat = jnp.concatenate([dc, nxt], axis=0)
        w = w_ref[...]
        dx = None
        rows = []
        for k in range(CONV_WIDTH):
            s = CONV_WIDTH - 1 - k
            ds = dcat if s == 0 else pltpu.roll(dcat, tm + HALO - s, 0)
            term = ds[:tm, :] * w[k:k + 1, :]
            dx = term if dx is None else dx + term
            xs = xcat if s == 0 else pltpu.roll(xcat, s, 0)
            rows.append(jnp.sum(dc * xs[HALO:, :], axis=0, keepdims=True))
        dx_ref[...] = dx.astype(BF16)
        dw = jnp.concatenate(rows + [jnp.zeros((HALO - CONV_WIDTH, 3 * A), F32)], axis=0)

        @pl.when(i == 0)
        def _():
            dw_ref[...] = dw

        @pl.when(i > 0)
        def _():
            dw_ref[...] += dw

    return pl.pallas_call(
        body, name="conv_bwd", grid=(nt,),
        in_specs=[pl.BlockSpec((tm, 3 * A), lambda i: (i, 0)),
                  pl.BlockSpec((HALO, 3 * A), _halo_prev(tm)),
                  pl.BlockSpec((tm, 3 * A), lambda i: (i, 0)),
                  pl.BlockSpec((HALO, 3 * A), lambda i: (jnp.minimum((i + 1) * (tm // HALO), T // HALO - 1), 0)),
                  pl.BlockSpec((CONV_WIDTH, 3 * A), lambda i: (0, 0)), ANY],
        out_specs=[pl.BlockSpec((tm, 3 * A), lambda i: (i, 0)),
                   pl.BlockSpec((HALO, 3 * A), lambda i: (0, 0))],
        out_shape=[jax.ShapeDtypeStruct(dproj.shape, dproj.dtype), jax.ShapeDtypeStruct((HALO, 3 * A), F32)],
        input_output_aliases={5: 0},
        compiler_params=_cparams(("arbitrary",)),
    )(proj_m, proj_m, dc, dc, conv_w, dproj)


CHUNK = 128
BLOCK = 64


def _b(x):
    return x.astype(BF16)


@jax.custom_vjp
def _bdot(a, b):
    return _dot(_b(a), _b(b))


def _bdot_f(a, b):
    return _bdot(a, b), (a, b)


def _bdot_b(res, g):
    a, b = res
    return _dot_nt(_b(g), _b(b)), _dot_tn(_b(a), _b(g))


_bdot.defvjp(_bdot_f, _bdot_b)


@jax.custom_vjp
def _bdot_nt(a, b):
    return _dot_nt(_b(a), _b(b))


def _bdot_nt_f(a, b):
    return _bdot_nt(a, b), (a, b)


def _bdot_nt_b(res, g):
    a, b = res
    return _dot(_b(g), _b(b)), _dot_tn(_b(g), _b(a))


_bdot_nt.defvjp(_bdot_nt_f, _bdot_nt_b)


@jax.custom_vjp
def _bdot_tn(a, b):
    return _dot_tn(_b(a), _b(b))


def _bdot_tn_f(a, b):
    return _bdot_tn(a, b), (a, b)


def _bdot_tn_b(res, g):
    a, b = res
    return _dot_nt(_b(b), _b(g)), _dot(_b(a), _b(g))


_bdot_tn.defvjp(_bdot_tn_f, _bdot_tn_b)


def _mask_matmul(m, x):
    hi = _b(x)
    r = x - hi.astype(F32)
    mid = _b(r)
    lo = _b(r - mid.astype(F32))
    return (_dot(m, lo) + _dot(m, mid)) + _dot(m, hi)


@jax.custom_vjp
def _mask_dot(m, mt, x):
    return _mask_matmul(m, x)


def _mask_dot_f(m, mt, x):
    return _mask_matmul(m, x), (m, mt)


def _mask_dot_b(res, g):
    m, mt = res
    return jnp.zeros_like(m), jnp.zeros_like(mt), _mask_matmul(mt, g)


_mask_dot.defvjp(_mask_dot_f, _mask_dot_b)

def _unit_lower_inverse(L):
    n = L.shape[-1]
    X = -L
    Q = X
    for _ in range(BLOCK.bit_length() - 2):
        X = _dot(_b(X), _b(X))
        Q = Q + X + _dot(_b(Q), _b(X))
    return (_iota((n, n), 0) == _iota((n, n), 1)).astype(F32) + Q


@jax.custom_vjp
def _known_inverse(L, P):
    return P


def _known_inverse_f(L, P):
    return P, P


def _known_inverse_b(P, g):
    n = P.shape[-1]
    Q = _b(P - (_iota((n, n), 0) == _iota((n, n), 1)).astype(F32))
    t = g + _dot_tn(Q, _b(g))
    return -(t + _dot_nt(_b(t), Q)), jnp.zeros_like(P)


_known_inverse.defvjp(_known_inverse_f, _known_inverse_b)


def _gdn_prep_fn(q, k, v, gb, bb, P_known=None):
    n = CHUNK
    row, col = _iota((n, n), 0), _iota((n, n), 1)
    same = (row // BLOCK) == (col // BLOCK)
    incl, strict = same & (row >= col), same & (row > col)
    bc = lambda m: jnp.broadcast_to(_b(m.astype(F32)), q.shape[:1] + (n, n))
    tril, triu, ones = bc(incl), bc(same & (row <= col)), bc(same)
    gc = _mask_dot(tril, triu, gb)
    gl = _mask_dot(ones, ones, gb)
    decay = jnp.where(incl, jnp.exp(jnp.where(incl, gc - jnp.swapaxes(gc, 1, 2), 0.0)), 0.0)
    kb = k * bb
    vb = v * bb
    qs = q * (HEAD_DIM ** -0.5)
    L = jnp.where(strict, _bdot_nt(kb, k) * decay, 0.0)
    P = _unit_lower_inverse(L) if P_known is None else _known_inverse(L, P_known)
    egc = jnp.exp(gc)
    u = _bdot(P, vb)
    w = _bdot(P, kb * egc)
    attn = jnp.where(incl, _bdot_nt(qs, k) * decay, 0.0)
    qg = qs * egc
    kdec = k * jnp.exp(gl - gc)
    eg = jnp.exp(gl).reshape(-1, n // BLOCK, BLOCK, LANES).sum(axis=2) * (1.0 / BLOCK)
    if P_known is None:
        return u, w, qg, kdec, attn, eg, P
    return u, w, qg, kdec, attn, eg


def _gdn_chain_fn(S, qg, kdec, u, w, attn, eg):
    nblk = CHUNK // BLOCK
    cat = lambda xs: jnp.concatenate(xs, axis=1)
    outs, found = [], []
    for i in range(nblk):
        r = (slice(None), slice(i * BLOCK, (i + 1) * BLOCK))
        v_new = u[r] - _bdot(w[r], S)
        found.append(v_new)
        outs.append(_bdot(qg[r], S) + _bdot(attn[r], cat(found + [jnp.zeros_like(v_new)] * (nblk - 1 - i))))
        S = S * eg[i] + _bdot_tn(kdec[r], v_new)
    return cat(outs), S


def _eg_spec(H, T, chunks, index_map, per_head):
    nblk = CHUNK // BLOCK
    block = (chunks, 1 if per_head else H, nblk, LANES)
    return pl.BlockSpec(block, index_map), jax.ShapeDtypeStruct((T // CHUNK, H, nblk, LANES), F32)


def _gdn_prep(q, k, v, gb, bb):
    H, T, _ = q.shape
    pb = _pick(T // CHUNK, (8, 4, 2, 1))
    hs = pl.BlockSpec((1, CHUNK * pb, HEAD_DIM), lambda h, n: (h, n, 0))

    def body(q_ref, k_ref, v_ref, gb_ref, bb_ref, *out_refs):
        chunks = lambda ref: ref[0].reshape(pb, CHUNK, HEAD_DIM)
        outs = _gdn_prep_fn(chunks(q_ref), chunks(k_ref), chunks(v_ref), chunks(gb_ref), chunks(bb_ref))
        for i, (ref, val) in enumerate(zip(out_refs, outs)):
            if i == 5:
                ref[:, 0] = val
            else:
                ref[0] = val.reshape(pb * CHUNK, HEAD_DIM).astype(ref.dtype)

    kept = [F32, BF16, BF16, BF16, BF16, None, BF16]
    es, eshape = _eg_spec(H, T, pb, lambda h, n: (n, h, 0, 0), per_head=True)
    return pl.pallas_call(
        body, name="gdn_prep", grid=(H, T // (CHUNK * pb)),
        in_specs=[hs] * 5, out_specs=[es if dt is None else hs for dt in kept],
        out_shape=[eshape if dt is None else jax.ShapeDtypeStruct((H, T, HEAD_DIM), dt) for dt in kept],
        compiler_params=_cparams(("parallel", "parallel")),
    )(q, k, v, gb, bb)


def _gdn_prep_bwd(q, k, v, gb, bb, pinv, du, dw, dqg, dkd, dat, deg):
    H, T, _ = q.shape
    pb = _pick(T // CHUNK, (8, 4, 2, 1))
    hs = pl.BlockSpec((1, CHUNK * pb, HEAD_DIM), lambda h, n: (h, n, 0))
    hshape = jax.ShapeDtypeStruct((H, T, HEAD_DIM), F32)

    def body(*refs):
        in_refs, p_ref, ct_refs, out_refs = refs[:5], refs[5], refs[6:12], refs[12:]
        chunks = lambda ref: ref[0].reshape(pb, CHUNK, HEAD_DIM)
        P = chunks(p_ref).astype(F32)
        _, vjp = jax.vjp(lambda *a: _gdn_prep_fn(*a, P_known=P), *[chunks(r) for r in in_refs])
        grads = vjp(tuple(chunks(r).astype(F32) for r in ct_refs[:5]) + (ct_refs[5][:, 0],))
        for ref, val in zip(out_refs, grads):
            ref[0] = val.reshape(pb * CHUNK, HEAD_DIM)

    es, _ = _eg_spec(H, T, pb, lambda h, n: (n, h, 0, 0), per_head=True)
    return pl.pallas_call(
        body, name="gdn_prep_bwd", grid=(H, T // (CHUNK * pb)),
        in_specs=[hs] * 11 + [es], out_specs=[hs] * 5, out_shape=[hshape] * 5,
        compiler_params=_cparams(("parallel", "parallel")),
    )(q, k, v, gb, bb, pinv, du, dw, dqg, dkd, dat, deg)


def _gdn_chain(qg, kd, u, w, attn, eg):
    H, T, _ = qg.shape
    N = T // CHUNK
    hs = pl.BlockSpec((H, CHUNK, HEAD_DIM), lambda n: (0, n, 0))
    ss = pl.BlockSpec((1, H, HEAD_DIM, HEAD_DIM), lambda n: (n, 0, 0, 0))

    def body(qg_ref, kd_ref, u_ref, w_ref, at_ref, eg_ref, o_ref, sall_ref, s_ref):
        @pl.when(pl.program_id(0) == 0)
        def _():
            s_ref[...] = jnp.zeros_like(s_ref)

        S = s_ref[...]
        sall_ref[0] = S
        eg = tuple(eg_ref[0, :, i:i + 1, :] for i in range(CHUNK // BLOCK))
        o, S2 = _gdn_chain_fn(S, qg_ref[...], kd_ref[...], u_ref[...], w_ref[...], at_ref[...], eg)
        o_ref[...] = o
        s_ref[...] = S2

    es, _ = _eg_spec(H, T, 1, lambda n: (n, 0, 0, 0), per_head=False)
    return pl.pallas_call(
        body, name="gdn_chain", grid=(N,),
        in_specs=[hs] * 5 + [es], out_specs=[hs, ss],
        out_shape=[jax.ShapeDtypeStruct((H, T, HEAD_DIM), F32),
                   jax.ShapeDtypeStruct((N, H, HEAD_DIM, HEAD_DIM), F32)],
        scratch_shapes=[pltpu.VMEM((H, HEAD_DIM, HEAD_DIM), F32)],
        compiler_params=_cparams(("arbitrary",)),
    )(qg, kd, u, w, attn, eg)


def _gdn_chain_bwd(qg, kd, u, w, attn, eg, sall, do):
    H, T, _ = qg.shape
    N = T // CHUNK
    hs = pl.BlockSpec((H, CHUNK, HEAD_DIM), lambda n: (0, N - 1 - n, 0))
    ss = pl.BlockSpec((1, H, HEAD_DIM, HEAD_DIM), lambda n: (N - 1 - n, 0, 0, 0))

    def body(qg_ref, kd_ref, u_ref, w_ref, at_ref, eg_ref, sall_ref, do_ref, *rest):
        out_refs, ds_ref = rest[:6], rest[6]

        @pl.when(pl.program_id(0) == 0)
        def _():
            ds_ref[...] = jnp.zeros_like(ds_ref)

        f32 = lambda ref: ref[...].astype(F32)
        nblk = CHUNK // BLOCK
        eg = tuple(eg_ref[0, :, i:i + 1, :] for i in range(nblk))
        _, vjp = jax.vjp(_gdn_chain_fn, sall_ref[0], f32(qg_ref), f32(kd_ref), u_ref[...], f32(w_ref),
                         f32(at_ref), eg)
        grads = vjp((do_ref[...], ds_ref[...]))
        ds_ref[...] = grads[0]
        for ref, val in zip(out_refs[:5], grads[1:6]):
            ref[...] = val.astype(ref.dtype)
        for i in range(nblk):
            out_refs[5][0, :, i:i + 1, :] = grads[6][i]

    kept = [F32, F32, BF16, BF16, F32]
    es, eshape = _eg_spec(H, T, 1, lambda n: (N - 1 - n, 0, 0, 0), per_head=False)
    return pl.pallas_call(
        body, name="gdn_chain_bwd", grid=(N,),
        in_specs=[hs] * 5 + [es, ss, hs], out_specs=[hs] * 5 + [es],
        out_shape=[jax.ShapeDtypeStruct((H, T, HEAD_DIM), dt) for dt in kept] + [eshape],
        scratch_shapes=[pltpu.VMEM((H, HEAD_DIM, HEAD_DIM), F32)],
        compiler_params=_cparams(("arbitrary",)),
    )(qg, kd, u, w, attn, eg, sall, do)


def _post_fn(ogs, za, hw):
    outs = []
    for h, o in enumerate(ogs):
        r = lax.rsqrt(jnp.mean(o * o, axis=-1, keepdims=True) + EPS)
        outs.append(o * r * hw * _silu(za[:, h * HEAD_DIM:(h + 1) * HEAD_DIM]))
    return jnp.concatenate(outs, axis=1)


def _gdn_post(og, proj_m, hw):
    H, T, _ = og.shape
    A = H * HEAD_DIM
    tm = _pick(T, (512, 256, 128))

    def body(og_ref, za_ref, hw_ref, o_ref, ot_ref):
        o = _post_fn(tuple(og_ref[h] for h in range(H)), za_ref[...], hw_ref[...])
        o_ref[...] = o.astype(BF16)
        ot_ref[...] = o.T.astype(BF16)

    return pl.pallas_call(
        body, name="gdn_post", grid=(T // tm,),
        in_specs=[pl.BlockSpec((H, tm, HEAD_DIM), lambda i: (0, i, 0)),
                  pl.BlockSpec((tm, A), lambda i: (i, ZA_BLOCK)),
                  pl.BlockSpec((1, HEAD_DIM), lambda i: (0, 0))],
        out_specs=[pl.BlockSpec((tm, A), lambda i: (i, 0)), pl.BlockSpec((A, tm), lambda i: (0, i))],
        out_shape=[jax.ShapeDtypeStruct((T, A), BF16), jax.ShapeDtypeStruct((A, T), BF16)],
        compiler_params=_cparams(("parallel",)),
    )(og, proj_m, hw)


def _gdn_post_bwd(og, proj_m, hw, d_o, dproj):
    H, T, _ = og.shape
    A = H * HEAD_DIM
    tm = _pick(T, (256, 128))

    def body(og_ref, za_ref, hw_ref, do_ref, _, dog_ref, dza_ref, dhw_ref):
        _, vjp = jax.vjp(_post_fn, tuple(og_ref[h] for h in range(H)), za_ref[...], hw_ref[...])
        dog, dza, dhw = vjp(do_ref[...])
        for h in range(H):
            dog_ref[h] = dog[h]
        dza_ref[...] = dza.astype(BF16)

        @pl.when(pl.program_id(0) == 0)
        def _():
            dhw_ref[...] = dhw

        @pl.when(pl.program_id(0) > 0)
        def _():
            dhw_ref[...] += dhw

    return pl.pallas_call(
        body, name="gdn_post_bwd", grid=(T // tm,),
        in_specs=[pl.BlockSpec((H, tm, HEAD_DIM), lambda i: (0, i, 0)),
                  pl.BlockSpec((tm, A), lambda i: (i, ZA_BLOCK)),
                  pl.BlockSpec((1, HEAD_DIM), lambda i: (0, 0)),
                  pl.BlockSpec((tm, A), lambda i: (i, 0)), ANY],
        out_specs=[pl.BlockSpec((H, tm, HEAD_DIM), lambda i: (0, i, 0)),
                   pl.BlockSpec((tm, A), lambda i: (i, ZA_BLOCK)),
                   pl.BlockSpec((1, HEAD_DIM), lambda i: (0, 0))],
        out_shape=[jax.ShapeDtypeStruct((H, T, HEAD_DIM), F32), jax.ShapeDtypeStruct(dproj.shape, dproj.dtype),
                   jax.ShapeDtypeStruct((1, HEAD_DIM), F32)],
        input_output_aliases={4: 1},
        compiler_params=_cparams(("arbitrary",)),
    )(og, proj_m, hw, d_o, dproj)


def _sgu_fn(ub, vb, zb, lw, lb, W, bbc):
    G = len(W)
    tm = ub.shape[0]
    mu = jnp.mean(vb, axis=-1, keepdims=True)
    xc = vb - mu
    var = jnp.mean(xc * xc, axis=-1, keepdims=True)
    vn = xc * lax.rsqrt(var + EPS) * lw + lb
    mask = _iota((CHUNK_B, CHUNK_B), 0) >= _iota((CHUNK_B, CHUNK_B), 1)
    cols = []
    for g in range(G):
        wm = jnp.where(mask, W[g], 0.0).astype(BF16)
        rows = []
        for c in range(tm // CHUNK_B):
            blk = vn[c * CHUNK_B:(c + 1) * CHUNK_B, g * HEAD_DIM:(g + 1) * HEAD_DIM].astype(BF16)
            rows.append(_dot(wm, blk) + bbc[g])
        cols.append(jnp.concatenate(rows, axis=0) if len(rows) > 1 else rows[0])
    s = jnp.concatenate(cols, axis=1)
    return ub * s * _silu(zb)


ZA_BLOCK = 6


def _sgu_cols(A, B):
    assert A == B
    return 3, 4, 5


def _sgu_fwd(proj_m, lw, lb, W, bbc, A):
    T = proj_m.shape[0]
    G = W.shape[0]
    B = G * HEAD_DIM
    tm = _pick(T, (256, 128))
    cu, cv, cz = _sgu_cols(A, B)

    def body(u_ref, v_ref, z_ref, lw_ref, lb_ref, w_ref, b_ref, o_ref, ot_ref):
        o = _sgu_fn(u_ref[...], v_ref[...], z_ref[...], lw_ref[...], lb_ref[...],
                    tuple(w_ref[g] for g in range(G)), tuple(b_ref[g] for g in range(G)))
        o_ref[...] = o.astype(BF16)
        ot_ref[...] = o.T.astype(BF16)

    row = pl.BlockSpec((1, B), lambda i: (0, 0))
    cube = pl.BlockSpec((G, CHUNK_B, CHUNK_B), lambda i: (0, 0, 0))
    return pl.pallas_call(
        body, name="sgu_fwd", grid=(T // tm,),
        in_specs=[pl.BlockSpec((tm, B), lambda i: (i, cu)), pl.BlockSpec((tm, B), lambda i: (i, cv)),
                  pl.BlockSpec((tm, B), lambda i: (i, cz)), row, row, cube, cube],
        out_specs=[pl.BlockSpec((tm, B), lambda i: (i, 0)), pl.BlockSpec((B, tm), lambda i: (0, i))],
        out_shape=[jax.ShapeDtypeStruct((T, B), BF16), jax.ShapeDtypeStruct((B, T), BF16)],
        compiler_params=_cparams(("parallel",)),
    )(proj_m, proj_m, proj_m, lw, lb, W, bbc)


def _sgu_bwd(proj_m, lw, lb, W, bbc, d_o, A, dproj):
    T = proj_m.shape[0]
    G = W.shape[0]
    B = G * HEAD_DIM
    tm = _pick(T, (256, 128))
    nt = T // tm
    cu, cv, cz = _sgu_cols(A, B)

    def body(u_ref, v_ref, z_ref, lw_ref, lb_ref, w_ref, b_ref, do_ref, _,
             dp_ref, dlw_ref, dlb_ref, dw_ref, db_ref, dbb_ref):
        _, vjp = jax.vjp(_sgu_fn, u_ref[...], v_ref[...], z_ref[...], lw_ref[...], lb_ref[...],
                         tuple(w_ref[g] for g in range(G)), tuple(b_ref[g] for g in range(G)))
        du, dv, dz, dlw, dlb, dW, dbb = vjp(do_ref[...])
        dW, dbb = jnp.stack(dW, axis=0), jnp.stack(dbb, axis=0)
        dp_ref[:, 0:B] = du.astype(BF16)
        dp_ref[:, B:2 * B] = dv.astype(BF16)
        dp_ref[:, 2 * B:3 * B] = dz.astype(BF16)
        i = pl.program_id(0)

        @pl.when(i == 0)
        def _():
            dlw_ref[...] = dlw
            dlb_ref[...] = dlb
            dw_ref[...] = dW
            dbb_ref[...] = dbb

        @pl.when(i > 0)
        def _():
            dlw_ref[...] += dlw
            dlb_ref[...] += dlb
            dw_ref[...] += dW
            dbb_ref[...] += dbb

        @pl.when(i == nt - 1)
        def _():
            db_ref[...] = jnp.sum(dbb_ref[...], axis=-1, keepdims=True)

    row = pl.BlockSpec((1, B), lambda i: (0, 0))
    cube = pl.BlockSpec((G, CHUNK_B, CHUNK_B), lambda i: (0, 0, 0))
    return pl.pallas_call(
        body, name="sgu_bwd", grid=(nt,),
        in_specs=[pl.BlockSpec((tm, B), lambda i: (i, cu)), pl.BlockSpec((tm, B), lambda i: (i, cv)),
                  pl.BlockSpec((tm, B), lambda i: (i, cz)), row, row, cube, cube,
                  pl.BlockSpec((tm, B), lambda i: (i, A // B)), ANY],
        out_specs=[pl.BlockSpec((tm, 3 * B), lambda i: (i, 1)), row, row, cube,
                   pl.BlockSpec((G, CHUNK_B, 1), lambda i: (0, 0, 0))],
        out_shape=[jax.ShapeDtypeStruct(dproj.shape, dproj.dtype), jax.ShapeDtypeStruct((1, B), F32),
                   jax.ShapeDtypeStruct((1, B), F32), jax.ShapeDtypeStruct((G, CHUNK_B, CHUNK_B), F32),
                   jax.ShapeDtypeStruct((G, CHUNK_B, 1), F32)],
        input_output_aliases={8: 0},
        scratch_shapes=[pltpu.VMEM((G, CHUNK_B, CHUNK_B), F32)],
        compiler_params=_cparams(("arbitrary",)),
    )(proj_m, proj_m, proj_m, lw, lb, W, bbc, d_o, dproj)


def _head_fn(mix, x, fw, tgt):
    h = x + mix
    y = _rms_fn(h, fw)
    e = y - tgt
    return 0.5 * jnp.sum(jnp.mean(e * e, axis=-1, keepdims=True), axis=0, keepdims=True)


def _out_proj_loss(oa, ob, wout, x, tgt, fw):
    T, A = oa.shape
    B = ob.shape[1]
    D = x.shape[1]
    tm = _pick(T, (256, 128))

    def body(oa_ref, ob_ref, w_ref, x_ref, t_ref, fw_ref, dh_ref, dhb_ref, loss_ref, dfw_ref):
        mix = _dot(oa_ref[...], w_ref[0:A, :]) + _dot(ob_ref[...], w_ref[A:A + B, :])
        xv, tv = x_ref[...], t_ref[...]
        loss, vjp = jax.vjp(lambda m, f: _head_fn(m, xv, f, tv), mix, fw_ref[...])
        dh, dfw = vjp(jnp.ones((1, 1), F32))
        dh_ref[...] = dh
        dhb_ref[...] = dh.astype(BF16)
        lrow = jnp.broadcast_to(loss, (1, LANES))

        @pl.when(pl.program_id(0) == 0)
        def _():
            loss_ref[...] = lrow
            dfw_ref[...] = dfw

        @pl.when(pl.program_id(0) > 0)
        def _():
            loss_ref[...] += lrow
            dfw_ref[...] += dfw

    tile = pl.BlockSpec((tm, D), lambda i: (i, 0))
    return pl.pallas_call(
        body, name="out_proj_loss", grid=(T // tm,),
        in_specs=[pl.BlockSpec((tm, A), lambda i: (i, 0)), pl.BlockSpec((tm, B), lambda i: (i, 0)),
                  pl.BlockSpec((A + B, D), lambda i: (0, 0)), tile, tile,
                  pl.BlockSpec((1, D), lambda i: (0, 0))],
        out_specs=[tile, tile, pl.BlockSpec((1, LANES), lambda i: (0, 0)),
                   pl.BlockSpec((1, D), lambda i: (0, 0))],
        out_shape=[jax.ShapeDtypeStruct((T, D), F32), jax.ShapeDtypeStruct((T, D), BF16),
                   jax.ShapeDtypeStruct((1, LANES), F32), jax.ShapeDtypeStruct((1, D), F32)],
        compiler_params=_cparams(("arbitrary",)),
    )(oa, ob, wout, x, tgt, fw)


def _adamw(w, g, m, v, name):
    R, Cn = w.shape
    cap = max(8, 512 * 1024 // Cn)
    tr = max(t for t in range(8, min(R, cap) + 1, 8) if R % t == 0) if R > cap else R

    def body(w_ref, g_ref, m_ref, v_ref, d_ref, mo_ref, vo_ref):
        g = g_ref[...]
        m = ADAM_B1 * m_ref[...] + (1.0 - ADAM_B1) * g
        v = ADAM_B2 * v_ref[...] + (1.0 - ADAM_B2) * jnp.square(g)
        m_hat = m / (1.0 - ADAM_B1 ** ADAM_STEP)
        v_hat = v / (1.0 - ADAM_B2 ** ADAM_STEP)
        d_ref[...] = -ADAM_LR * (m_hat / (jnp.sqrt(v_hat) + ADAM_EPS) + ADAM_WD * w_ref[...])
        mo_ref[...] = m
        vo_ref[...] = v

    tile = pl.BlockSpec((tr, Cn), lambda i: (i, 0))
    shape = jax.ShapeDtypeStruct((R, Cn), F32)
    return pl.pallas_call(
        body, name=name, grid=(R // tr,), in_specs=[tile] * 4, out_specs=[tile] * 3,
        out_shape=[shape] * 3, compiler_params=_cparams(("parallel",)),
    )(w, g, m, v)


def _place():
    x, y, c = lax.axis_index("x"), lax.axis_index("y"), lax.axis_index("c")
    others = [(1 - x, y), (x, 1 - y), (1 - x, 1 - y)]
    return x, y, c, others


def _chip_index(px, py):
    return 2 * px + py


ANY = pl.BlockSpec(memory_space=pl.ANY)


def _gather_ride(blocks, split):
    n = len(blocks)

    def plan(in_refs, out_refs, send_sems, recv_sems):
        x, y, c, _ = _place()
        me, kx, ky, kd = (_chip_index(px, py) for px, py in ((x, y), (1 - x, y), (x, 1 - y), (1 - x, 1 - y)))
        to_x, to_y, to_s = (1 - x, y, c), (x, 1 - y, c), (x, y, 1 - c)

        def copy(sem, src, dst, to):
            return pltpu.make_async_remote_copy(src_ref=src, dst_ref=dst, send_sem=send_sems.at[sem],
                                                recv_sem=recv_sems.at[sem], device_id=to, device_id_type=MESH_ID)

        first, second, third, awaited = [], [], [], []
        for a in range(n):
            out, s0 = out_refs[a], 8 * a
            if not split[a]:
                for j, (k, to) in enumerate(((kx, to_x), (ky, to_y), (kd, (1 - x, 1 - y, c)))):
                    first.append(lambda j=j, to=to, a=a, out=out, s0=s0: copy(s0 + j, in_refs[a], out.at[me], to))
                    awaited.append((lambda j=j, k=k, to=to, out=out, s0=s0: copy(s0 + j, out.at[k], out.at[k], to),
                                    None))
                continue
            h = blocks[a].shape[0] // 2
            q = h // 2
            half = lambda k, core, out=out, h=h: out.at[k, pl.ds(core * h, h), :]
            quarter = lambda k, core, i, out=out, h=h, q=q: out.at[k, pl.ds(core * h + i * q, q), :]
            mine = in_refs[a].at[pl.ds(c * h, h), :]
            first.append(lambda s0=s0, mine=mine, half=half: copy(s0, mine, half(me, c), to_x))
            first.append(lambda s0=s0, mine=mine, half=half: copy(s0 + 1, mine, half(me, c), to_y))
            fwd0 = lambda s0=s0, quarter=quarter: copy(s0 + 2, quarter(kx, c, 0), quarter(kx, c, 0), to_y)
            fwd1 = lambda s0=s0, quarter=quarter: copy(s0 + 3, quarter(ky, c, 1), quarter(ky, c, 1), to_x)
            pieces = [(s0 + 0, lambda half=half: half(kx, c), lambda half=half: half(kx, 1 - c), to_x, fwd0),
                      (s0 + 1, lambda half=half: half(ky, c), lambda half=half: half(ky, 1 - c), to_y, fwd1),
                      (s0 + 2, lambda quarter=quarter: quarter(kd, c, 0), lambda quarter=quarter: quarter(kd, 1 - c, 0),
                       to_y, None),
                      (s0 + 3, lambda quarter=quarter: quarter(kd, c, 1), lambda quarter=quarter: quarter(kd, 1 - c, 1),
                       to_x, None)]
            for i, (sem, here, there, frm, fwd) in enumerate(pieces):
                passing = lambda s0=s0, i=i, here=here: copy(s0 + 4 + i, here(), here(), to_s)
                awaited.append((lambda sem=sem, here=here, frm=frm: copy(sem, here(), here(), frm), (fwd, passing)))
                if fwd is not None:
                    second.append(fwd)
                third.append((passing, lambda s0=s0, i=i, there=there: copy(s0 + 4 + i, there(), there(), to_s)))
        return first, second, third, awaited

    def start(*refs):
        for send in plan(*refs)[0]:
            send().start()

    def finish(*refs):
        first, second, third, awaited = plan(*refs)
        for arrival, then in awaited:
            arrival().wait_recv()
            for nxt in (then or ()):
                if nxt is not None:
                    nxt().start()
        for _, from_sibling in third:
            from_sibling().wait_recv()
        for send in first + second + [p for p, _ in third]:
            send().wait_send()

    shapes = [jax.ShapeDtypeStruct((N_CHIPS,) + b.shape, b.dtype) for b in blocks]
    return _Ride(blocks, shapes, 8 * n, start, finish)


def _put_own(gathered, own):
    me = _chip_index(lax.axis_index("x"), lax.axis_index("y"))
    return lax.dynamic_update_index_in_dim(gathered, own, me, 0)


def _allreduce_small(buf):
    R0, L = buf.shape
    R = -(-R0 // 16) * 16
    h = R // 2
    buf = jnp.pad(buf, ((0, R - R0), (0, 0)))

    def body(in_ref, out_ref, sib_ref, pair_ref, chips_ref, send_sems, recv_sems):
        x, y, c, others = _place()
        me = _chip_index(x, y)
        sibling = (x, y, 1 - c)

        def copy(sem, src, dst, to):
            return pltpu.make_async_remote_copy(src_ref=src, dst_ref=dst, send_sem=send_sems.at[sem],
                                                recv_sem=recv_sems.at[sem], device_id=to, device_id_type=MESH_ID)

        cp = copy(0, in_ref, sib_ref, sibling)
        cp.start()
        cp.wait()
        pair_ref[...] = in_ref[...] + sib_ref[...]
        rows = lambda core: pl.ds(pl.multiple_of(core * h, 8), h)
        sends = [copy(1 + j, pair_ref.at[rows(c), :], chips_ref.at[me], (*chip, c)) for j, chip in enumerate(others)]
        for s in sends:
            s.start()
        chips_ref[me] = pair_ref[rows(c), :]
        for j, chip in enumerate(others):
            k = _chip_index(*chip)
            copy(1 + j, chips_ref.at[k], chips_ref.at[k], (*chip, c)).wait_recv()
        out_ref[rows(c), :] = ((chips_ref[0] + chips_ref[1]) + chips_ref[2]) + chips_ref[3]
        swap = copy(4, out_ref.at[rows(c), :], out_ref.at[rows(c), :], sibling)
        swap.start()
        copy(4, out_ref.at[rows(1 - c), :], out_ref.at[rows(1 - c), :], sibling).wait_recv()
        for s in sends + [swap]:
            s.wait_send()

    vm = pl.BlockSpec(memory_space=pltpu.VMEM)
    return pl.pallas_call(
        body, name="allreduce_small", in_specs=[vm], out_specs=vm,
        out_shape=jax.ShapeDtypeStruct((R, L), F32),
        scratch_shapes=[pltpu.VMEM((R, L), F32), pltpu.VMEM((R, L), F32), pltpu.VMEM((N_CHIPS, h, L), F32),
                        pltpu.SemaphoreType.DMA((5,)), pltpu.SemaphoreType.DMA((5,))],
        compiler_params=pltpu.CompilerParams(vmem_limit_bytes=VMEM_LIMIT),
    )(buf)[:R0]


def _pair_ride(g):
    nb, R, Cn = g.shape
    h = R // 2

    def copy(in_refs, out_refs, send_sems, recv_sems):
        x, y, c, _ = _place()
        return pltpu.make_async_remote_copy(src_ref=in_refs[0].at[:, pl.ds((1 - c) * h, h), :], dst_ref=out_refs[0],
                                            send_sem=send_sems.at[0], recv_sem=recv_sems.at[0],
                                            device_id=(x, y, 1 - c), device_id_type=MESH_ID)

    return _Ride([g], [jax.ShapeDtypeStruct((nb, h, Cn), g.dtype)], 1,
                 lambda *refs: copy(*refs).start(), lambda *refs: copy(*refs).wait())


def _pair_sum(g, land, c_arr, name, ride=None):
    nb, R, Cn = g.shape
    hr = R // 2
    tr = _pick(hr, (256, 128, 64, 32, 16))
    nt = hr // tr

    def body(c_ref, g_ref, l_ref, o_ref):
        o_ref[...] = (g_ref[...].astype(F32) + l_ref[...].astype(F32)).astype(BF16)

    return _pallas(
        body, (c_arr, g, land), name=name, prefetch=1, grid=(nb, nt),
        in_specs=[pl.BlockSpec((1, tr, Cn), lambda b, i, c_ref: (b, c_ref[0] * nt + i, 0)),
                  pl.BlockSpec((1, tr, Cn), lambda b, i, c_ref: (b, i, 0))],
        out_specs=pl.BlockSpec((1, tr, Cn), lambda b, i, c_ref: (b, i, 0)),
        out_shape=jax.ShapeDtypeStruct((nb, hr, Cn), BF16),
        semantics=("parallel", "parallel"), ride=ride)


def _chip_ride(parts):
    m = len(parts)

    def copies(in_refs, out_refs, send_sems, recv_sems):
        x, y, c, others = _place()
        me = _chip_index(x, y)
        def mk(j, chip, n, landing):
            k = _chip_index(*chip)
            return pltpu.make_async_remote_copy(
                src_ref=in_refs[n].at[k], dst_ref=out_refs[n].at[landing(k)], send_sem=send_sems.at[m * j + n],
                recv_sem=recv_sems.at[m * j + n], device_id=(*chip, c), device_id_type=MESH_ID)

        pairs = [(j, chip, n) for j, chip in enumerate(others) for n in range(m)]
        return pairs, (lambda *p: mk(*p, lambda k: me)), (lambda *p: mk(*p, lambda k: k))

    def start(*refs):
        pairs, send, _ = copies(*refs)
        for p in pairs:
            send(*p).start()

    def finish(*refs):
        pairs, send, arrival = copies(*refs)
        for p in pairs:
            arrival(*p).wait_recv()
        for p in pairs:
            send(*p).wait_send()

    return _Ride(parts, [jax.ShapeDtypeStruct(p.shape, p.dtype) for p in parts], 3 * m, start, finish)


def _put_own_slot(q, p):
    me = _chip_index(lax.axis_index("x"), lax.axis_index("y"))
    return lax.dynamic_update_index_in_dim(q, lax.dynamic_index_in_dim(p, me, 0, keepdims=False), me, 0)


def _chip_sum(q, c_arr, name):
    nb, hr, Cn = q.shape
    tr = _pick(hr, (256, 128, 64, 32, 16))
    nt = hr // tr

    def body(c_ref, q_ref, o_ref):
        f = lambda k: q_ref[k].astype(F32)
        o_ref[...] = ((f(0) + f(1)) + f(2)) + f(3)

    return _pallas(
        body, (c_arr, q), name=name, prefetch=1, grid=(nt,),
        in_specs=[pl.BlockSpec((nb, tr, Cn), lambda i, c_ref: (0, i, 0))],
        out_specs=pl.BlockSpec((tr, Cn), lambda i, c_ref: (c_ref[0] * nt + i, 0)),
        out_shape=jax.ShapeDtypeStruct((2 * hr, Cn), F32),
        semantics=("parallel",))


def _sibling_fill(fw, fo):
    def body(_, __, fw_ref, fo_ref, send_sems, recv_sems):
        x, y, c, _ = _place()
        copies = []
        for n, ref in enumerate((fw_ref, fo_ref)):
            h = ref.shape[0] // 2
            mine = ref.at[pl.ds(c * h, h), :]
            theirs = ref.at[pl.ds((1 - c) * h, h), :]
            mk = lambda src, dst: pltpu.make_async_remote_copy(
                src_ref=src, dst_ref=dst, send_sem=send_sems.at[n], recv_sem=recv_sems.at[n],
                device_id=(x, y, 1 - c), device_id_type=MESH_ID)
            send = mk(mine, mine)
            send.start()
            copies.append((send, mk(theirs, theirs)))
        for send, arrival in copies:
            arrival.wait_recv()
            send.wait_send()

    return pl.pallas_call(
        body, name="sibling_fill", in_specs=[ANY, ANY], out_specs=[ANY, ANY],
        out_shape=[jax.ShapeDtypeStruct(fw.shape, F32), jax.ShapeDtypeStruct(fo.shape, F32)],
        input_output_aliases={0: 0, 1: 1},
        scratch_shapes=[pltpu.SemaphoreType.DMA((2,)), pltpu.SemaphoreType.DMA((2,))],
        compiler_params=pltpu.CompilerParams(has_side_effects=True),
    )(fw, fo)


class _Layout:
    def __init__(self, H, G, nb, Cb):
        A, B = H * HEAD_DIM, G * HEAD_DIM
        self.n_main = 4 * A + 3 * B
        self.k = -(-(self.n_main + LANES) // WIN_BLOCK) * WIN_BLOCK
        cuts = [0, 3 * A, 4 * A, 4 * A + 2 * H, nb * Cb]
        starts = [0, 3 * A + 3 * B, self.n_main, 3 * A]
        self.pieces = []
        self.windows, self.runs = [], []
        for n in range(nb):
            segs = []
            for s in range(4):
                lo, hi = max(cuts[s], n * Cb), min(cuts[s + 1], (n + 1) * Cb)
                if lo < hi:
                    segs.append((starts[s] + lo - cuts[s], lo - n * Cb, hi - lo))
            self.pieces += [(own, n, col, ln) for own, col, ln in segs]
            blocks = sorted({b for own, _, ln in segs for b in range(own // WIN_BLOCK, (own + ln - 1) // WIN_BLOCK + 1)})
            self.windows.append(blocks)
            self.runs.append([(blocks.index(own // WIN_BLOCK) * WIN_BLOCK + own % WIN_BLOCK, ln)
                              for own, _, ln in segs])
        self.wb = max(len(b) for b in self.windows)
        self.table = [b + [b[-1]] * (self.wb - len(b)) for b in self.windows]
        self.pieces.sort()

    def to_own_order(self, g_in):
        D = g_in.shape[1]
        cols, at = [], 0
        for own, n, col, ln in self.pieces:
            if own > at:
                cols.append(jnp.zeros((D, own - at), g_in.dtype))
            cols.append(g_in[n, :, col:col + ln])
            at = own + ln
        if at < self.k:
            cols.append(jnp.zeros((D, self.k - at), g_in.dtype))
        return jnp.concatenate(cols, axis=1)

    def from_window(self, win, chip, Cb):
        pick = lambda runs: (lambda w: jnp.concatenate([w[:, c:c + ln] for c, ln in runs], axis=1))
        return lax.switch(chip, [pick(r) for r in self.runs], win)


def _device_step(x, tgt, norm_w, win_b, wout_b, conv_b, a_log, dt_bias, head_norm_w, sgu_ln_w, sgu_ln_b,
                 w_spatial, b_spatial, final_norm_w, c_arr):
    T, D = x.shape
    H = a_log.shape[1]
    A = H * HEAD_DIM
    G = w_spatial.shape[0]
    B = G * HEAD_DIM
    nb, Cb, Rb = N_CHIPS, win_b.shape[1], wout_b.shape[0]
    lay = _Layout(H, G, nb, Cb)
    alog_row = jnp.pad(a_log, ((0, 0), (H, LANES - 2 * H)))
    dtb_row = jnp.pad(dt_bias, ((0, 0), (H, LANES - 2 * H)))
    bbc = jnp.broadcast_to(b_spatial[:, :, None], (G, CHUNK_B, CHUNK_B))

    (xn, xn_t), (g_in,) = _rms_in(x, norm_w, ride=_gather_ride([win_b], [True]))
    w_own = lay.to_own_order(_put_own(g_in, win_b))
    proj_m, (g_out, g_conv) = _mm_nn(xn, w_own, F32, "in_proj", tm=2048, cols=(0, lay.n_main),
                                     ride=_gather_ride([wout_b, conv_b], [False, False]))
    wout = _put_own(g_out, wout_b).reshape(nb * Rb, D)
    conv_w = _put_own(g_conv, conv_b).transpose(1, 0, 2).reshape(CONV_WIDTH, nb * conv_b.shape[1])
    q, k, v, gb, bb, proj_ba = _gdn_pre(proj_m, xn, w_own, conv_w, alog_row, dtb_row, H)
    u, w, qg, kd, attn, eg, pinv = _gdn_prep(q, k, v, gb, bb)
    og, sall = _gdn_chain(qg, kd, u, w, attn, eg)
    oa, oa_t = _gdn_post(og, proj_m, head_norm_w)
    ob, ob_t = _sgu_fwd(proj_m, sgu_ln_w, sgu_ln_b, w_spatial, bbc, A)
    dh, dhb, loss_row, d_fnw = _out_proj_loss(oa, ob, wout, x, tgt, final_norm_w.reshape(1, D))

    d_o = _mm_nn(dhb, wout.T, F32, "out_proj_dx", tm=2048)
    dproj = lax.empty((T, lay.k), BF16)
    dproj, d_lw, d_lb, d_ws, d_bs = _sgu_bwd(proj_m, sgu_ln_w, sgu_ln_b, w_spatial, bbc, d_o, A, dproj)
    dog, dproj, d_hw = _gdn_post_bwd(og, proj_m, head_norm_w, d_o, dproj)
    dqg, dkd, du, dw, dat, deg = _gdn_chain_bwd(qg, kd, u, w, attn, eg, sall, dog)
    dq, dk, dv, dgb, dbb = _gdn_prep_bwd(q, k, v, gb, bb, pinv, du, dw, dqg, dkd, dat, deg)
    dc, dproj, d_al, d_dt = _gdn_pre_bwd(proj_m, proj_ba, conv_w, alog_row, dtb_row, dq, dk, dv, dgb, dbb, H,
                                         dproj)
    dproj, d_conv = _conv_bwd(proj_m, dc, conv_w, H, dproj)

    table = jnp.array([b for row in lay.table for b in row], jnp.int32)
    d_win = _mm_windows(xn_t, dproj, table, nb, "in_proj_dw")
    d_wout, (land_w,) = _mm_nn_pair(oa_t, ob_t, dhb, "out_proj_dw", ride=_pair_ride(d_win))
    d_wout = d_wout.reshape(nb, Rb, D)
    pair_w, (land_o,) = _pair_sum(d_win, land_w, c_arr, "pair_sum_w_in", ride=_pair_ride(d_wout))
    pair_o = _pair_sum(d_wout, land_o, c_arr, "pair_sum_w_out")
    dxn, (all_w,) = _mm_nt_rhs_outer(dproj, w_own, F32, "in_proj_dx", ride=_chip_ride([pair_w]))
    (grad_x, d_nw), (all_o,) = _rms_in_bwd(x, norm_w, dxn, dh, ride=_chip_ride([pair_o]))
    all_w, all_o = _put_own_slot(all_w, pair_w), _put_own_slot(all_o, pair_o)
    small = dict(norm_w=d_nw, conv_w=d_conv[:CONV_WIDTH], a_log=d_al[:, H:2 * H], dt_bias=d_dt[:, H:2 * H],
                 head_norm_w=d_hw, sgu_ln_w=d_lw, sgu_ln_b=d_lb, w_spatial=d_ws, b_spatial=d_bs[:, :, 0],
                 final_norm_w=d_fnw)
    return loss_row, grad_x, small, all_w, all_o


SMALL = ("norm_w", "conv_w", "a_log", "dt_bias", "head_norm_w", "sgu_ln_w", "sgu_ln_b", "w_spatial",
         "b_spatial", "final_norm_w")


def _pack(parts):
    rows = []
    for p in parts:
        f = p.reshape(-1)
        f = jnp.pad(f, (0, (-f.shape[0]) % (8 * LANES)))
        rows.append(f.reshape(-1, LANES))
    return jnp.concatenate(rows, axis=0)


def _unpack(buf, shapes):
    out, r = [], 0
    for s in shapes:
        n = 1
        for d in s:
            n *= d
        nr = -(-n // (8 * LANES)) * 8
        out.append(buf[r:r + nr].reshape(-1)[:n].reshape(s))
        r += nr
    return out


def kernel(x, norm_w, w_in, conv_w, a_log, dt_bias, head_norm_w, sgu_ln_w, sgu_ln_b, w_spatial, b_spatial, w_out, final_norm_w, loss_target, m_norm_w, m_w_in, m_conv_w, m_a_log, m_dt_bias, m_head_norm_w, m_sgu_ln_w, m_sgu_ln_b, m_w_spatial, m_b_spatial, m_w_out, m_final_norm_w, v_norm_w, v_w_in, v_conv_w, v_a_log, v_dt_bias, v_head_norm_w, v_sgu_ln_w, v_sgu_ln_b, v_w_spatial, v_b_spatial, v_w_out, v_final_norm_w):
    T, D = x.shape[1], x.shape[2]
    weights = dict(norm_w=norm_w, w_in=w_in, conv_w=conv_w, a_log=a_log, dt_bias=dt_bias, head_norm_w=head_norm_w,
                   sgu_ln_w=sgu_ln_w, sgu_ln_b=sgu_ln_b, w_spatial=w_spatial, b_spatial=b_spatial, w_out=w_out,
                   final_norm_w=final_norm_w)
    mom_m = dict(norm_w=m_norm_w, w_in=m_w_in, conv_w=m_conv_w, a_log=m_a_log, dt_bias=m_dt_bias,
                 head_norm_w=m_head_norm_w, sgu_ln_w=m_sgu_ln_w, sgu_ln_b=m_sgu_ln_b, w_spatial=m_w_spatial,
                 b_spatial=m_b_spatial, w_out=m_w_out, final_norm_w=m_final_norm_w)
    mom_v = dict(norm_w=v_norm_w, w_in=v_w_in, conv_w=v_conv_w, a_log=v_a_log, dt_bias=v_dt_bias,
                 head_norm_w=v_head_norm_w, sgu_ln_w=v_sgu_ln_w, sgu_ln_b=v_sgu_ln_b, w_spatial=v_w_spatial,
                 b_spatial=v_b_spatial, w_out=v_w_out, final_norm_w=v_final_norm_w)
    me = _chip_index(lax.axis_index("x"), lax.axis_index("y"))
    c_arr = lax.axis_index("c").astype(jnp.int32).reshape(1)
    Din, Cb = w_in.shape[1], w_in.shape[2]
    Rb = w_out.shape[1]
    cconv = conv_w.shape[2]

    loss_row, grad_x, g, qw, qo = _device_step(
        x[0], loss_target[0], norm_w, w_in[0].astype(BF16), w_out[0].astype(BF16), conv_w[0], a_log, dt_bias,
        head_norm_w, sgu_ln_w, sgu_ln_b, w_spatial[0], b_spatial[0], final_norm_w, c_arr)

    small_shapes = [tuple(g[n].shape) for n in SMALL]
    small = _allreduce_small(_pack([g[n] for n in SMALL]))
    gsum_in, gsum_out = _sibling_fill(_chip_sum(qw, c_arr, "chip_sum_w_in"), _chip_sum(qo, c_arr, "chip_sum_w_out"))
    gsum_in = _Layout(a_log.shape[1], w_spatial.shape[1], N_CHIPS, Cb).from_window(gsum_in, me, Cb)
    gsmall = dict(zip(SMALL, _unpack(small, small_shapes)))
    gsmall["conv_w"] = lax.dynamic_slice_in_dim(gsmall["conv_w"], me * cconv, cconv, axis=1)

    grads, deltas, new_m, new_v = {}, {}, {}, {}
    d, m2, v2 = _adamw(w_out[0], gsum_out, m_w_out[0], v_w_out[0], "adamw_w_out")
    grads["w_out"], deltas["w_out"], new_m["w_out"], new_v["w_out"] = gsum_out[None], d[None], m2[None], v2[None]
    flat = lambda a: a.transpose(2, 0, 1).reshape(-1, LANES)
    unflat = lambda f: f.reshape(Cb, 1, Din).transpose(1, 2, 0)
    g_flat = gsum_in.T.reshape(-1, LANES)
    d, m2, v2 = _adamw(flat(w_in), g_flat, flat(m_w_in), flat(v_w_in), "adamw_w_in")
    grads["w_in"], deltas["w_in"], new_m["w_in"], new_v["w_in"] = unflat(g_flat), unflat(d), unflat(m2), unflat(v2)
    shapes = [tuple(weights[n].shape) for n in SMALL]
    ds, ms, vs = _adamw(_pack([weights[n] for n in SMALL]), _pack([gsmall[n] for n in SMALL]),
                        _pack([mom_m[n] for n in SMALL]), _pack([mom_v[n] for n in SMALL]), "adamw_small")
    for n, gq, d, m2, v2 in zip(SMALL, [gsmall[n] for n in SMALL], _unpack(ds, shapes), _unpack(ms, shapes),
                                _unpack(vs, shapes)):
        grads[n], deltas[n], new_m[n], new_v[n] = gq.reshape(weights[n].shape), d, m2, v2

    loss = lax.psum(loss_row[0, 0], ("x", "y", "c"))
    order = ("norm_w", "w_in", "conv_w", "a_log", "dt_bias", "head_norm_w", "sgu_ln_w", "sgu_ln_b", "w_spatial",
             "b_spatial", "w_out", "final_norm_w")
    return (loss, grad_x[None], *[grads[n] for n in order], *[deltas[n] for n in order],
            *[new_m[n] for n in order], *[new_v[n] for n in order])
```

```python
import functools

import jax
import jax.numpy as jnp
from jax import lax
from jax.experimental import pallas as pl
from jax.experimental.pallas import tpu as pltpu

F32 = jnp.float32
BF16 = jnp.bfloat16
EPS = 1e-6
HEAD_DIM = 128
CHUNK_B = 128
CONV_WIDTH = 4
LANES = 128
HALO = 8
N_CHIPS = 4
ADAM_LR = 0.001
ADAM_B1 = 0.9
ADAM_B2 = 0.999
ADAM_EPS = 1e-08
ADAM_WD = 0.01
ADAM_STEP = 10
VMEM_LIMIT = 56 * 1024 * 1024
MESH_ID = pl.DeviceIdType.MESH


def _cparams(sem=None, **kw):
    return pltpu.CompilerParams(dimension_semantics=sem, vmem_limit_bytes=VMEM_LIMIT, **kw)


def _matmul(a, b, ca, cb):
    nb = a.ndim - 2
    batch = tuple(range(nb))
    return lax.dot_general(a, b, (((ca + nb,), (cb + nb,)), (batch, batch)), preferred_element_type=F32)


def _dot(a, b):
    return _matmul(a, b, 1, 0)


def _dot_nt(a, b):
    return _matmul(a, b, 1, 1)


def _dot_tn(a, b):
    return _matmul(a, b, 0, 0)


def _iota(shape, dim):
    return lax.broadcasted_iota(jnp.int32, shape, dim)


def _sigmoid(x):
    return 0.5 * (jnp.tanh(0.5 * x) + 1.0)


def _silu(x):
    return x * _sigmoid(x)


def _softplus(x):
    z = jnp.exp(-jnp.abs(x))
    small = z * (1.0 - z * (0.5 - z * (1.0 / 3.0)))
    return jnp.maximum(x, 0.0) + jnp.where(z < 1e-3, small, jnp.log(1.0 + z))


def _pick(n, pref):
    for t in pref:
        if n % t == 0:
            return t
    return n


class _Ride:
    def __init__(self, operands, out_shape, n_sems, start, finish):
        self.operands, self.out_shape, self.n_sems = list(operands), list(out_shape), n_sems
        self.start, self.finish = start, finish


def _pallas(body, operands, *, name, grid, in_specs, out_specs, out_shape, semantics, scratch_shapes=(),
            prefetch=0, ride=None):
    single = not isinstance(out_shape, (list, tuple))
    outs = [out_shape] if single else list(out_shape)
    ospecs = [out_specs] if single else list(out_specs)
    in_specs, scratch = list(in_specs), list(scratch_shapes)
    n_in, n_out, n_sc = len(operands) - prefetch, len(outs), len(scratch)
    kernel = body
    params = _cparams(semantics)
    if ride is not None:
        n_xin, n_xout = len(ride.operands), len(ride.out_shape)

        def kernel(*refs):
            pre, refs = refs[:prefetch], refs[prefetch:]
            ins, refs = refs[:n_in], refs[n_in:]
            xins, refs = refs[:n_xin], refs[n_xin:]
            mains, refs = refs[:n_out], refs[n_out:]
            xouts, refs = refs[:n_xout], refs[n_xout:]
            sc, (send, recv) = refs[:n_sc], refs[n_sc:]
            ids = [pl.program_id(a) for a in range(len(grid))]
            first = functools.reduce(jnp.logical_and, [i == 0 for i in ids])
            last = functools.reduce(jnp.logical_and, [i == g - 1 for i, g in zip(ids, grid)])

            @pl.when(first)
            def _():
                ride.start(xins, xouts, send, recv)

            body(*pre, *ins, *mains, *sc)

            @pl.when(last)
            def _():
                ride.finish(xins, xouts, send, recv)

        operands = list(operands) + ride.operands
        in_specs += [ANY] * n_xin
        ospecs += [ANY] * n_xout
        outs += ride.out_shape
        scratch += [pltpu.SemaphoreType.DMA((ride.n_sems,)), pltpu.SemaphoreType.DMA((ride.n_sems,))]
        params = _cparams(("arbitrary",) * len(grid), has_side_effects=True)
    if prefetch:
        spec = dict(grid_spec=pltpu.PrefetchScalarGridSpec(
            num_scalar_prefetch=prefetch, grid=grid, in_specs=in_specs, out_specs=ospecs, scratch_shapes=scratch))
    else:
        spec = dict(grid=grid, in_specs=in_specs, out_specs=ospecs, scratch_shapes=scratch)
    res = pl.pallas_call(kernel, name=name, out_shape=outs, compiler_params=params, **spec)(*operands)
    main = res[0] if single else list(res[:n_out])
    return main if ride is None else (main, list(res[n_out:]))


def _mm_nn(a, b, out_dtype, name, tm=1024, tn=512, tk=None, cols=None, ride=None):
    M, K = a.shape
    c0, N = (0, b.shape[1]) if cols is None else cols
    tm = _pick(M, (tm, 1024, 512, 256, 128))
    tn = _pick(N, (tn, 512, 384, 256, 128))
    tk = K if tk is None else _pick(K, (tk,))
    nk = K // tk
    j0 = c0 // tn
    assert c0 % tn == 0

    def body(a_ref, b_ref, o_ref, *scratch):
        part = _dot(a_ref[...], b_ref[...])
        if nk == 1:
            o_ref[...] = part.astype(out_dtype)
        else:
            acc_ref, = scratch
            k = pl.program_id(2)

            @pl.when(k == 0)
            def _():
                acc_ref[...] = part

            @pl.when(k > 0)
            def _():
                acc_ref[...] += part

            @pl.when(k == nk - 1)
            def _():
                o_ref[...] = acc_ref[...].astype(out_dtype)

    return _pallas(
        body, (a, b), name=name, grid=(M // tm, N // tn, nk),
        in_specs=[pl.BlockSpec((tm, tk), lambda i, j, k: (i, k)),
                  pl.BlockSpec((tk, tn), lambda i, j, k: (k, j + j0))],
        out_specs=pl.BlockSpec((tm, tn), lambda i, j, k: (i, j)),
        out_shape=jax.ShapeDtypeStruct((M, N), out_dtype),
        scratch_shapes=[] if nk == 1 else [pltpu.VMEM((tm, tn), F32)],
        semantics=("parallel", "parallel", "arbitrary"), ride=ride)


def _mm_nt_rhs_outer(a, b, out_dtype, name, tm=256, tn=1024, ride=None):
    M, K = a.shape
    N, _ = b.shape
    tm = _pick(M, (tm, 128))
    tn = _pick(N, (tn, 512, 256, 128))

    def body(a_ref, b_ref, o_ref):
        o_ref[...] = _dot_nt(a_ref[...], b_ref[...]).astype(out_dtype)

    return _pallas(
        body, (a, b), name=name, grid=(N // tn, M // tm),
        in_specs=[pl.BlockSpec((tm, K), lambda j, i: (i, 0)),
                  pl.BlockSpec((tn, K), lambda j, i: (j, 0))],
        out_specs=pl.BlockSpec((tm, tn), lambda j, i: (i, j)),
        out_shape=jax.ShapeDtypeStruct((M, N), out_dtype),
        semantics=("parallel", "parallel"), ride=ride)


WIN_BLOCK = 256


def _mm_windows(a, b, table, nb, name, tm=2048):
    M, K = a.shape
    wb = table.shape[0] // nb
    tm = _pick(M, (tm, 1024, 512, 256, 128))

    def body(tab_ref, a_ref, b_ref, o_ref):
        o_ref[0] = _dot(a_ref[...], b_ref[...]).astype(BF16)

    return pl.pallas_call(
        body, name=name,
        grid_spec=pltpu.PrefetchScalarGridSpec(
            num_scalar_prefetch=1, grid=(nb, M // tm, wb),
            in_specs=[pl.BlockSpec((tm, K), lambda n, i, t, tab: (i, 0)),
                      pl.BlockSpec((K, WIN_BLOCK), lambda n, i, t, tab: (0, tab[n * wb + t]))],
            out_specs=pl.BlockSpec((1, tm, WIN_BLOCK), lambda n, i, t, tab: (n, i, t))),
        out_shape=jax.ShapeDtypeStruct((nb, M, wb * WIN_BLOCK), BF16),
        compiler_params=_cparams(("parallel", "parallel", "arbitrary")),
    )(table, a, b)


def _mm_nn_pair(a0, a1, b, name, tm=512, tn=1024, ride=None):
    M, K = a0.shape
    _, N = b.shape
    tm = _pick(M, (tm, 256, 128))
    tn = _pick(N, (tn, 512, 256, 128))
    ni = M // tm

    def body(a0_ref, a1_ref, b_ref, o_ref):
        p = pl.program_id(0)

        @pl.when(p == 0)
        def _():
            o_ref[...] = _dot(a0_ref[...], b_ref[...]).astype(BF16)

        @pl.when(p == 1)
        def _():
            o_ref[...] = _dot(a1_ref[...], b_ref[...]).astype(BF16)

    return _pallas(
        body, (a0, a1, b), name=name, grid=(2, ni, N // tn),
        in_specs=[pl.BlockSpec((tm, K), lambda p, i, j: (i * (1 - p), 0)),
                  pl.BlockSpec((tm, K), lambda p, i, j: (i * p, 0)),
                  pl.BlockSpec((K, tn), lambda p, i, j: (0, j))],
        out_specs=pl.BlockSpec((tm, tn), lambda p, i, j: (p * ni + i, j)),
        out_shape=jax.ShapeDtypeStruct((2 * M, N), BF16),
        semantics=("parallel", "parallel", "parallel"), ride=ride)


def _rms_fn(x, w):
    r = lax.rsqrt(jnp.mean(x * x, axis=-1, keepdims=True) + EPS)
    return x * r * w


def _rms_in(x, w, ride=None):
    T, D = x.shape
    tm = _pick(T, (512, 256, 128))

    def body(x_ref, w_ref, o_ref, ot_ref):
        xn = _rms_fn(x_ref[...], w_ref[...])
        o_ref[...] = xn.astype(BF16)
        ot_ref[...] = xn.T.astype(BF16)

    return _pallas(
        body, (x, w), name="rms_in", grid=(T // tm,),
        in_specs=[pl.BlockSpec((tm, D), lambda i: (i, 0)), pl.BlockSpec((1, D), lambda i: (0, 0))],
        out_specs=[pl.BlockSpec((tm, D), lambda i: (i, 0)), pl.BlockSpec((D, tm), lambda i: (0, i))],
        out_shape=[jax.ShapeDtypeStruct((T, D), BF16), jax.ShapeDtypeStruct((D, T), BF16)],
        semantics=("parallel",), ride=ride)


def _rms_in_bwd(x, w, dxn, dh, ride=None):
    T, D = x.shape
    tm = _pick(T, (256, 128))

    def body(x_ref, w_ref, dxn_ref, dh_ref, gx_ref, dw_ref):
        _, vjp = jax.vjp(_rms_fn, x_ref[...], w_ref[...])
        dx, dw = vjp(dxn_ref[...])
        gx_ref[...] = dh_ref[...] + dx

        @pl.when(pl.program_id(0) == 0)
        def _():
            dw_ref[...] = dw

        @pl.when(pl.program_id(0) > 0)
        def _():
            dw_ref[...] += dw

    tile = pl.BlockSpec((tm, D), lambda i: (i, 0))
    row = pl.BlockSpec((1, D), lambda i: (0, 0))
    return _pallas(
        body, (x, w, dxn, dh), name="rms_in_bwd", grid=(T // tm,),
        in_specs=[tile, row, tile, tile], out_specs=[tile, row],
        out_shape=[jax.ShapeDtypeStruct((T, D), F32), jax.ShapeDtypeStruct((1, D), F32)],
        semantics=("arbitrary",), ride=ride)


def _conv_fwd(cat_ref, halo, x, w):
    tm = x.shape[0]
    cat_ref[0:HALO, :] = halo
    cat_ref[HALO:HALO + tm, :] = x
    c = x * w[CONV_WIDTH - 1:CONV_WIDTH, :]
    for k in range(CONV_WIDTH - 1):
        s = CONV_WIDTH - 1 - k
        c = c + cat_ref[pl.ds(HALO - s, tm), :] * w[k:k + 1, :]
    return c


def _lane_to_all(x, lane):
    @jax.custom_vjp
    def f(x):
        return jnp.broadcast_to(x[:, lane:lane + 1], x.shape)

    def f_fwd(x):
        return f(x), None

    def f_bwd(_, g):
        return (jnp.where(_iota(g.shape, 1) == lane, jnp.sum(g, axis=-1, keepdims=True), 0.0),)

    f.defvjp(f_fwd, f_bwd)
    return f(x)


def _gdn_pointwise(c, ba, alog, dtb, H):
    A = H * HEAD_DIM
    s = _silu(c)
    beta = _sigmoid(ba)
    g = -jnp.exp(alog) * _softplus(ba + dtb)
    qs, ks, vs, gbs, bbs = [], [], [], [], []
    for h in range(H):
        lo = h * HEAD_DIM
        q = s[:, lo:lo + HEAD_DIM]
        k = s[:, A + lo:A + lo + HEAD_DIM]
        qs.append(q * lax.rsqrt(jnp.sum(q * q, axis=-1, keepdims=True) + EPS))
        ks.append(k * lax.rsqrt(jnp.sum(k * k, axis=-1, keepdims=True) + EPS))
        vs.append(s[:, 2 * A + lo:2 * A + lo + HEAD_DIM])
        bbs.append(_lane_to_all(beta, h))
        gbs.append(_lane_to_all(g, H + h))
    st = lambda xs: jnp.stack(xs, axis=0)
    return st(qs), st(ks), st(vs), st(gbs), st(bbs)


def _halo_prev(tm):
    return lambda i: (jnp.maximum(i * (tm // HALO) - 1, 0), 0)


def _gdn_pre(proj_m, xn, w_own, conv_w, alog_row, dtb_row, H):
    T, n_main = proj_m.shape
    D = xn.shape[1]
    A = H * HEAD_DIM
    tm = _pick(T, (256, 128))
    hs = pl.BlockSpec((H, tm, HEAD_DIM), lambda i: (0, i, 0))
    hshape = jax.ShapeDtypeStruct((H, T, HEAD_DIM), F32)

    def body(x_ref, halo_ref, xn_ref, wba_ref, w_ref, al_ref, dt_ref,
             q_ref, k_ref, v_ref, gb_ref, bb_ref, ba_ref, cat_ref):
        halo = jnp.where(pl.program_id(0) == 0, 0.0, halo_ref[...])
        c = _conv_fwd(cat_ref, halo, x_ref[...], w_ref[...])
        ba = _dot(xn_ref[...], wba_ref[...])
        q, k, v, gb, bb = _gdn_pointwise(c, ba, al_ref[...], dt_ref[...], H)
        q_ref[...] = q
        k_ref[...] = k
        v_ref[...] = v
        gb_ref[...] = gb
        bb_ref[...] = bb
        ba_ref[...] = ba

    return pl.pallas_call(
        body, name="gdn_pre", grid=(T // tm,),
        in_specs=[pl.BlockSpec((tm, 3 * A), lambda i: (i, 0)),
                  pl.BlockSpec((HALO, 3 * A), _halo_prev(tm)),
                  pl.BlockSpec((tm, D), lambda i: (i, 0)),
                  pl.BlockSpec((D, LANES), lambda i: (0, n_main // LANES)),
                  pl.BlockSpec((CONV_WIDTH, 3 * A), lambda i: (0, 0)),
                  pl.BlockSpec((1, LANES), lambda i: (0, 0)),
                  pl.BlockSpec((1, LANES), lambda i: (0, 0))],
        out_specs=[hs] * 5 + [pl.BlockSpec((tm, LANES), lambda i: (i, 0))],
        out_shape=[hshape] * 5 + [jax.ShapeDtypeStruct((T, LANES), F32)],
        scratch_shapes=[pltpu.VMEM((HALO + tm, 3 * A), F32)],
        compiler_params=_cparams(("parallel",)),
    )(proj_m, proj_m, xn, w_own, conv_w, alog_row, dtb_row)


def _gdn_pre_bwd(proj_m, proj_ba, conv_w, alog_row, dtb_row, dq, dk, dv, dgb, dbb, H, dproj):
    T, n_main = proj_m.shape
    A = H * HEAD_DIM
    tm = _pick(T, (256, 128))
    hs = pl.BlockSpec((H, tm, HEAD_DIM), lambda i: (0, i, 0))
    row = pl.BlockSpec((1, LANES), lambda i: (0, 0))

    def body(x_ref, halo_ref, ba_ref, w_ref, al_ref, dt_ref, dq_ref, dk_ref, dv_ref, dgb_ref, dbb_ref, _,
             dc_ref, dba_ref, dal_ref, ddt_ref, cat_ref):
        halo = jnp.where(pl.program_id(0) == 0, 0.0, halo_ref[...])
        c = _conv_fwd(cat_ref, halo, x_ref[...], w_ref[...])
        _, vjp = jax.vjp(functools.partial(_gdn_pointwise, H=H), c, ba_ref[...], al_ref[...], dt_ref[...])
        dc, dba, dal, ddt = vjp((dq_ref[...], dk_ref[...], dv_ref[...], dgb_ref[...], dbb_ref[...]))
        dc_ref[...] = dc
        dba_ref[:, :LANES] = dba.astype(BF16)
        dba_ref[:, LANES:] = jnp.zeros((tm, WIN_BLOCK - LANES), BF16)

        @pl.when(pl.program_id(0) == 0)
        def _():
            dal_ref[...] = dal
            ddt_ref[...] = ddt

        @pl.when(pl.program_id(0) > 0)
        def _():
            dal_ref[...] += dal
            ddt_ref[...] += ddt

    return pl.pallas_call(
        body, name="gdn_pre_bwd", grid=(T // tm,),
        in_specs=[pl.BlockSpec((tm, 3 * A), lambda i: (i, 0)),
                  pl.BlockSpec((HALO, 3 * A), _halo_prev(tm)),
                  pl.BlockSpec((tm, LANES), lambda i: (i, 0)),
                  pl.BlockSpec((CONV_WIDTH, 3 * A), lambda i: (0, 0)),
                  row, row, hs, hs, hs, hs, hs, ANY],
        out_specs=[pl.BlockSpec((tm, 3 * A), lambda i: (i, 0)),
                   pl.BlockSpec((tm, WIN_BLOCK), lambda i: (i, n_main // WIN_BLOCK)), row, row],
        out_shape=[jax.ShapeDtypeStruct((T, 3 * A), F32), jax.ShapeDtypeStruct(dproj.shape, dproj.dtype),
                   jax.ShapeDtypeStruct((1, LANES), F32), jax.ShapeDtypeStruct((1, LANES), F32)],
        input_output_aliases={11: 1},
        scratch_shapes=[pltpu.VMEM((HALO + tm, 3 * A), F32)],
        compiler_params=_cparams(("arbitrary",)),
    )(proj_m, proj_m, proj_ba, conv_w, alog_row, dtb_row, dq, dk, dv, dgb, dbb, dproj)


def _conv_bwd(proj_m, dc, conv_w, H, dproj):
    T = proj_m.shape[0]
    A = H * HEAD_DIM
    tm = _pick(T, (256, 128))
    nt = T // tm

    def body(x_ref, halo_ref, dc_ref, nxt_ref, w_ref, _, dx_ref, dw_ref):
        i = pl.program_id(0)
        halo = jnp.where(i == 0, 0.0, halo_ref[...])
        xcat = jnp.concatenate([halo, x_ref[...]], axis=0)
        nxt = jnp.where(i == nt - 1, 0.0, nxt_ref[...])
        dc = dc_ref[...]
        dcat = jnp.concatenate([dc, nxt], axis=0)
        w = w_ref[...]
        dx = None
        rows = []
        for k in range(CONV_WIDTH):
            s = CONV_WIDTH - 1 - k
            ds = dcat if s == 0 else pltpu.roll(dcat, tm + HALO - s, 0)
            term = ds[:tm, :] * w[k:k + 1, :]
            dx = term if dx is None else dx + term
            xs = xcat if s == 0 else pltpu.roll(xcat, s, 0)
            rows.append(jnp.sum(dc * xs[HALO:, :], axis=0, keepdims=True))
        dx_ref[...] = dx.astype(BF16)
        dw = jnp.concatenate(rows + [jnp.zeros((HALO - CONV_WIDTH, 3 * A), F32)], axis=0)

        @pl.when(i == 0)
        def _():
            dw_ref[...] = dw

        @pl.when(i > 0)
        def _():
            dw_ref[...] += dw

    return pl.pallas_call(
        body, name="conv_bwd", grid=(nt,),
        in_specs=[pl.BlockSpec((tm, 3 * A), lambda i: (i, 0)),
                  pl.BlockSpec((HALO, 3 * A), _halo_prev(tm)),
                  pl.BlockSpec((tm, 3 * A), lambda i: (i, 0)),
                  pl.BlockSpec((HALO, 3 * A), lambda i: (jnp.minimum((i + 1) * (tm // HALO), T // HALO - 1), 0)),
                  pl.BlockSpec((CONV_WIDTH, 3 * A), lambda i: (0, 0)), ANY],
        out_specs=[pl.BlockSpec((tm, 3 * A), lambda i: (i, 0)),
                   pl.BlockSpec((HALO, 3 * A), lambda i: (0, 0))],
        out_shape=[jax.ShapeDtypeStruct(dproj.shape, dproj.dtype), jax.ShapeDtypeStruct((HALO, 3 * A), F32)],
        input_output_aliases={5: 0},
        compiler_params=_cparams(("arbitrary",)),
    )(proj_m, proj_m, dc, dc, conv_w, dproj)


CHUNK = 128
BLOCK = 64


def _b(x):
    return x.astype(BF16)


@jax.custom_vjp
def _bdot(a, b):
    return _dot(_b(a), _b(b))


def _bdot_f(a, b):
    return _bdot(a, b), (a, b)


def _bdot_b(res, g):
    a, b = res
    return _dot_nt(_b(g), _b(b)), _dot_tn(_b(a), _b(g))


_bdot.defvjp(_bdot_f, _bdot_b)


@jax.custom_vjp
def _bdot_nt(a, b):
    return _dot_nt(_b(a), _b(b))


def _bdot_nt_f(a, b):
    return _bdot_nt(a, b), (a, b)


def _bdot_nt_b(res, g):
    a, b = res
    return _dot(_b(g), _b(b)), _dot_tn(_b(g), _b(a))


_bdot_nt.defvjp(_bdot_nt_f, _bdot_nt_b)


@jax.custom_vjp
def _bdot_tn(a, b):
    return _dot_tn(_b(a), _b(b))


def _bdot_tn_f(a, b):
    return _bdot_tn(a, b), (a, b)


def _bdot_tn_b(res, g):
    a, b = res
    return _dot_nt(_b(b), _b(g)), _dot(_b(a), _b(g))


_bdot_tn.defvjp(_bdot_tn_f, _bdot_tn_b)


def _mask_matmul(m, x):
    hi = _b(x)
    r = x - hi.astype(F32)
    mid = _b(r)
    lo = _b(r - mid.astype(F32))
    return (_dot(m, lo) + _dot(m, mid)) + _dot(m, hi)


@jax.custom_vjp
def _mask_dot(m, mt, x):
    return _mask_matmul(m, x)


def _mask_dot_f(m, mt, x):
    return _mask_matmul(m, x), (m, mt)


def _mask_dot_b(res, g):
    m, mt = res
    return jnp.zeros_like(m), jnp.zeros_like(mt), _mask_matmul(mt, g)


_mask_dot.defvjp(_mask_dot_f, _mask_dot_b)

def _unit_lower_inverse(L):
    n = L.shape[-1]
    X = -L
    Q = X
    for _ in range(BLOCK.bit_length() - 2):
        X = _dot(_b(X), _b(X))
        Q = Q + X + _dot(_b(Q), _b(X))
    return (_iota((n, n), 0) == _iota((n, n), 1)).astype(F32) + Q


@jax.custom_vjp
def _known_inverse(L, P):
    return P


def _known_inverse_f(L, P):
    return P, P


def _known_inverse_b(P, g):
    n = P.shape[-1]
    Q = _b(P - (_iota((n, n), 0) == _iota((n, n), 1)).astype(F32))
    t = g + _dot_tn(Q, _b(g))
    return -(t + _dot_nt(_b(t), Q)), jnp.zeros_like(P)


_known_inverse.defvjp(_known_inverse_f, _known_inverse_b)


def _gdn_prep_fn(q, k, v, gb, bb, P_known=None):
    n = CHUNK
    row, col = _iota((n, n), 0), _iota((n, n), 1)
    same = (row // BLOCK) == (col // BLOCK)
    incl, strict = same & (row >= col), same & (row > col)
    bc = lambda m: jnp.broadcast_to(_b(m.astype(F32)), q.shape[:1] + (n, n))
    tril, triu, ones = bc(incl), bc(same & (row <= col)), bc(same)
    gc = _mask_dot(tril, triu, gb)
    gl = _mask_dot(ones, ones, gb)
    decay = jnp.where(incl, jnp.exp(jnp.where(incl, gc - jnp.swapaxes(gc, 1, 2), 0.0)), 0.0)
    kb = k * bb
    vb = v * bb
    qs = q * (HEAD_DIM ** -0.5)
    L = jnp.where(strict, _bdot_nt(kb, k) * decay, 0.0)
    P = _unit_lower_inverse(L) if P_known is None else _known_inverse(L, P_known)
    egc = jnp.exp(gc)
    u = _bdot(P, vb)
    w = _bdot(P, kb * egc)
    attn = jnp.where(incl, _bdot_nt(qs, k) * decay, 0.0)
    qg = qs * egc
    kdec = k * jnp.exp(gl - gc)
    eg = jnp.exp(gl).reshape(-1, n // BLOCK, BLOCK, LANES).sum(axis=2) * (1.0 / BLOCK)
    if P_known is None:
        return u, w, qg, kdec, attn, eg, P
    return u, w, qg, kdec, attn, eg


def _gdn_chain_fn(S, qg, kdec, u, w, attn, eg):
    nblk = CHUNK // BLOCK
    cat = lambda xs: jnp.concatenate(xs, axis=1)
    outs, found = [], []
    for i in range(nblk):
        r = (slice(None), slice(i * BLOCK, (i + 1) * BLOCK))
        v_new = u[r] - _bdot(w[r], S)
        found.append(v_new)
        outs.append(_bdot(qg[r], S) + _bdot(attn[r], cat(found + [jnp.zeros_like(v_new)] * (nblk - 1 - i))))
        S = S * eg[i] + _bdot_tn(kdec[r], v_new)
    return cat(outs), S


def _eg_spec(H, T, chunks, index_map, per_head):
    nblk = CHUNK // BLOCK
    block = (chunks, 1 if per_head else H, nblk, LANES)
    return pl.BlockSpec(block, index_map), jax.ShapeDtypeStruct((T // CHUNK, H, nblk, LANES), F32)


def _gdn_prep(q, k, v, gb, bb):
    H, T, _ = q.shape
    pb = _pick(T // CHUNK, (8, 4, 2, 1))
    hs = pl.BlockSpec((1, CHUNK * pb, HEAD_DIM), lambda h, n: (h, n, 0))

    def body(q_ref, k_ref, v_ref, gb_ref, bb_ref, *out_refs):
        chunks = lambda ref: ref[0].reshape(pb, CHUNK, HEAD_DIM)
        outs = _gdn_prep_fn(chunks(q_ref), chunks(k_ref), chunks(v_ref), chunks(gb_ref), chunks(bb_ref))
        for i, (ref, val) in enumerate(zip(out_refs, outs)):
            if i == 5:
                ref[:, 0] = val
            else:
                ref[0] = val.reshape(pb * CHUNK, HEAD_DIM).astype(ref.dtype)

    kept = [F32, BF16, BF16, BF16, BF16, None, BF16]
    es, eshape = _eg_spec(H, T, pb, lambda h, n: (n, h, 0, 0), per_head=True)
    return pl.pallas_call(
        body, name="gdn_prep", grid=(H, T // (CHUNK * pb)),
        in_specs=[hs] * 5, out_specs=[es if dt is None else hs for dt in kept],
        out_shape=[eshape if dt is None else jax.ShapeDtypeStruct((H, T, HEAD_DIM), dt) for dt in kept],
        compiler_params=_cparams(("parallel", "parallel")),
    )(q, k, v, gb, bb)


def _gdn_prep_bwd(q, k, v, gb, bb, pinv, du, dw, dqg, dkd, dat, deg):
    H, T, _ = q.shape
    pb = _pick(T // CHUNK, (8, 4, 2, 1))
    hs = pl.BlockSpec((1, CHUNK * pb, HEAD_DIM), lambda h, n: (h, n, 0))
    hshape = jax.ShapeDtypeStruct((H, T, HEAD_DIM), F32)

    def body(*refs):
        in_refs, p_ref, ct_refs, out_refs = refs[:5], refs[5], refs[6:12], refs[12:]
        chunks = lambda ref: ref[0].reshape(pb, CHUNK, HEAD_DIM)
        P = chunks(p_ref).astype(F32)
        _, vjp = jax.vjp(lambda *a: _gdn_prep_fn(*a, P_known=P), *[chunks(r) for r in in_refs])
        grads = vjp(tuple(chunks(r).astype(F32) for r in ct_refs[:5]) + (ct_refs[5][:, 0],))
        for ref, val in zip(out_refs, grads):
            ref[0] = val.reshape(pb * CHUNK, HEAD_DIM)

    es, _ = _eg_spec(H, T, pb, lambda h, n: (n, h, 0, 0), per_head=True)
    return pl.pallas_call(
        body, name="gdn_prep_bwd", grid=(H, T // (CHUNK * pb)),
        in_specs=[hs] * 11 + [es], out_specs=[hs] * 5, out_shape=[hshape] * 5,
        compiler_params=_cparams(("parallel", "parallel")),
    )(q, k, v, gb, bb, pinv, du, dw, dqg, dkd, dat, deg)


def _gdn_chain(qg, kd, u, w, attn, eg):
    H, T, _ = qg.shape
    N = T // CHUNK
    hs = pl.BlockSpec((H, CHUNK, HEAD_DIM), lambda n: (0, n, 0))
    ss = pl.BlockSpec((1, H, HEAD_DIM, HEAD_DIM), lambda n: (n, 0, 0, 0))

    def body(qg_ref, kd_ref, u_ref, w_ref, at_ref, eg_ref, o_ref, sall_ref, s_ref):
        @pl.when(pl.program_id(0) == 0)
        def _():
            s_ref[...] = jnp.zeros_like(s_ref)

        S = s_ref[...]
        sall_ref[0] = S
        eg = tuple(eg_ref[0, :, i:i + 1, :] for i in range(CHUNK // BLOCK))
        o, S2 = _gdn_chain_fn(S, qg_ref[...], kd_ref[...], u_ref[...], w_ref[...], at_ref[...], eg)
        o_ref[...] = o
        s_ref[...] = S2

    es, _ = _eg_spec(H, T, 1, lambda n: (n, 0, 0, 0), per_head=False)
    return pl.pallas_call(
        body, name="gdn_chain", grid=(N,),
        in_specs=[hs] * 5 + [es], out_specs=[hs, ss],
        out_shape=[jax.ShapeDtypeStruct((H, T, HEAD_DIM), F32),
                   jax.ShapeDtypeStruct((N, H, HEAD_DIM, HEAD_DIM), F32)],
        scratch_shapes=[pltpu.VMEM((H, HEAD_DIM, HEAD_DIM), F32)],
        compiler_params=_cparams(("arbitrary",)),
    )(qg, kd, u, w, attn, eg)


def _gdn_chain_bwd(qg, kd, u, w, attn, eg, sall, do):
    H, T, _ = qg.shape
    N = T // CHUNK
    hs = pl.BlockSpec((H, CHUNK, HEAD_DIM), lambda n: (0, N - 1 - n, 0))
    ss = pl.BlockSpec((1, H, HEAD_DIM, HEAD_DIM), lambda n: (N - 1 - n, 0, 0, 0))

    def body(qg_ref, kd_ref, u_ref, w_ref, at_ref, eg_ref, sall_ref, do_ref, *rest):
        out_refs, ds_ref = rest[:6], rest[6]

        @pl.when(pl.program_id(0) == 0)
        def _():
            ds_ref[...] = jnp.zeros_like(ds_ref)

        f32 = lambda ref: ref[...].astype(F32)
        nblk = CHUNK // BLOCK
        eg = tuple(eg_ref[0, :, i:i + 1, :] for i in range(nblk))
        _, vjp = jax.vjp(_gdn_chain_fn, sall_ref[0], f32(qg_ref), f32(kd_ref), u_ref[...], f32(w_ref),
                         f32(at_ref), eg)
        grads = vjp((do_ref[...], ds_ref[...]))
        ds_ref[...] = grads[0]
        for ref, val in zip(out_refs[:5], grads[1:6]):
            ref[...] = val.astype(ref.dtype)
        for i in range(nblk):
            out_refs[5][0, :, i:i + 1, :] = grads[6][i]

    kept = [F32, F32, BF16, BF16, F32]
    es, eshape = _eg_spec(H, T, 1, lambda n: (N - 1 - n, 0, 0, 0), per_head=False)
    return pl.pallas_call(
        body, name="gdn_chain_bwd", grid=(N,),
        in_specs=[hs] * 5 + [es, ss, hs], out_specs=[hs] * 5 + [es],
        out_shape=[jax.ShapeDtypeStruct((H, T, HEAD_DIM), dt) for dt in kept] + [eshape],
        scratch_shapes=[pltpu.VMEM((H, HEAD_DIM, HEAD_DIM), F32)],
        compiler_params=_cparams(("arbitrary",)),
    )(qg, kd, u, w, attn, eg, sall, do)


def _post_fn(ogs, za, hw):
    outs = []
    for h, o in enumerate(ogs):
        r = lax.rsqrt(jnp.mean(o * o, axis=-1, keepdims=True) + EPS)
        outs.append(o * r * hw * _silu(za[:, h * HEAD_DIM:(h + 1) * HEAD_DIM]))
    return jnp.concatenate(outs, axis=1)


def _gdn_post(og, proj_m, hw):
    H, T, _ = og.shape
    A = H * HEAD_DIM
    tm = _pick(T, (512, 256, 128))

    def body(og_ref, za_ref, hw_ref, o_ref, ot_ref):
        o = _post_fn(tuple(og_ref[h] for h in range(H)), za_ref[...], hw_ref[...])
        o_ref[...] = o.astype(BF16)
        ot_ref[...] = o.T.astype(BF16)

    return pl.pallas_call(
        body, name="gdn_post", grid=(T // tm,),
        in_specs=[pl.BlockSpec((H, tm, HEAD_DIM), lambda i: (0, i, 0)),
                  pl.BlockSpec((tm, A), lambda i: (i, ZA_BLOCK)),
                  pl.BlockSpec((1, HEAD_DIM), lambda i: (0, 0))],
        out_specs=[pl.BlockSpec((tm, A), lambda i: (i, 0)), pl.BlockSpec((A, tm), lambda i: (0, i))],
        out_shape=[jax.ShapeDtypeStruct((T, A), BF16), jax.ShapeDtypeStruct((A, T), BF16)],
        compiler_params=_cparams(("parallel",)),
    )(og, proj_m, hw)


def _gdn_post_bwd(og, proj_m, hw, d_o, dproj):
    H, T, _ = og.shape
    A = H * HEAD_DIM
    tm = _pick(T, (256, 128))

    def body(og_ref, za_ref, hw_ref, do_ref, _, dog_ref, dza_ref, dhw_ref):
        _, vjp = jax.vjp(_post_fn, tuple(og_ref[h] for h in range(H)), za_ref[...], hw_ref[...])
        dog, dza, dhw = vjp(do_ref[...])
        for h in range(H):
            dog_ref[h] = dog[h]
        dza_ref[...] = dza.astype(BF16)

        @pl.when(pl.program_id(0) == 0)
        def _():
            dhw_ref[...] = dhw

        @pl.when(pl.program_id(0) > 0)
        def _():
            dhw_ref[...] += dhw

    return pl.pallas_call(
        body, name="gdn_post_bwd", grid=(T // tm,),
        in_specs=[pl.BlockSpec((H, tm, HEAD_DIM), lambda i: (0, i, 0)),
                  pl.BlockSpec((tm, A), lambda i: (i, ZA_BLOCK)),
                  pl.BlockSpec((1, HEAD_DIM), lambda i: (0, 0)),
                  pl.BlockSpec((tm, A), lambda i: (i, 0)), ANY],
        out_specs=[pl.BlockSpec((H, tm, HEAD_DIM), lambda i: (0, i, 0)),
                   pl.BlockSpec((tm, A), lambda i: (i, ZA_BLOCK)),
                   pl.BlockSpec((1, HEAD_DIM), lambda i: (0, 0))],
        out_shape=[jax.ShapeDtypeStruct((H, T, HEAD_DIM), F32), jax.ShapeDtypeStruct(dproj.shape, dproj.dtype),
                   jax.ShapeDtypeStruct((1, HEAD_DIM), F32)],
        input_output_aliases={4: 1},
        compiler_params=_cparams(("arbitrary",)),
    )(og, proj_m, hw, d_o, dproj)


def _sgu_fn(ub, vb, zb, lw, lb, W, bbc):
    G = len(W)
    tm = ub.shape[0]
    mu = jnp.mean(vb, axis=-1, keepdims=True)
    xc = vb - mu
    var = jnp.mean(xc * xc, axis=-1, keepdims=True)
    vn = xc * lax.rsqrt(var + EPS) * lw + lb
    mask = _iota((CHUNK_B, CHUNK_B), 0) >= _iota((CHUNK_B, CHUNK_B), 1)
    cols = []
    for g in range(G):
        wm = jnp.where(mask, W[g], 0.0).astype(BF16)
        rows = []
        for c in range(tm // CHUNK_B):
            blk = vn[c * CHUNK_B:(c + 1) * CHUNK_B, g * HEAD_DIM:(g + 1) * HEAD_DIM].astype(BF16)
            rows.append(_dot(wm, blk) + bbc[g])
        cols.append(jnp.concatenate(rows, axis=0) if len(rows) > 1 else rows[0])
    s = jnp.concatenate(cols, axis=1)
    return ub * s * _silu(zb)


ZA_BLOCK = 6


def _sgu_cols(A, B):
    assert A == B
    return 3, 4, 5


def _sgu_fwd(proj_m, lw, lb, W, bbc, A):
    T = proj_m.shape[0]
    G = W.shape[0]
    B = G * HEAD_DIM
    tm = _pick(T, (256, 128))
    cu, cv, cz = _sgu_cols(A, B)

    def body(u_ref, v_ref, z_ref, lw_ref, lb_ref, w_ref, b_ref, o_ref, ot_ref):
        o = _sgu_fn(u_ref[...], v_ref[...], z_ref[...], lw_ref[...], lb_ref[...],
                    tuple(w_ref[g] for g in range(G)), tuple(b_ref[g] for g in range(G)))
        o_ref[...] = o.astype(BF16)
        ot_ref[...] = o.T.astype(BF16)

    row = pl.BlockSpec((1, B), lambda i: (0, 0))
    cube = pl.BlockSpec((G, CHUNK_B, CHUNK_B), lambda i: (0, 0, 0))
    return pl.pallas_call(
        body, name="sgu_fwd", grid=(T // tm,),
        in_specs=[pl.BlockSpec((tm, B), lambda i: (i, cu)), pl.BlockSpec((tm, B), lambda i: (i, cv)),
                  pl.BlockSpec((tm, B), lambda i: (i, cz)), row, row, cube, cube],
        out_specs=[pl.BlockSpec((tm, B), lambda i: (i, 0)), pl.BlockSpec((B, tm), lambda i: (0, i))],
        out_shape=[jax.ShapeDtypeStruct((T, B), BF16), jax.ShapeDtypeStruct((B, T), BF16)],
        compiler_params=_cparams(("parallel",)),
    )(proj_m, proj_m, proj_m, lw, lb, W, bbc)


def _sgu_bwd(proj_m, lw, lb, W, bbc, d_o, A, dproj):
    T = proj_m.shape[0]
    G = W.shape[0]
    B = G * HEAD_DIM
    tm = _pick(T, (256, 128))
    nt = T // tm
    cu, cv, cz = _sgu_cols(A, B)

    def body(u_ref, v_ref, z_ref, lw_ref, lb_ref, w_ref, b_ref, do_ref, _,
             dp_ref, dlw_ref, dlb_ref, dw_ref, db_ref, dbb_ref):
        _, vjp = jax.vjp(_sgu_fn, u_ref[...], v_ref[...], z_ref[...], lw_ref[...], lb_ref[...],
                         tuple(w_ref[g] for g in range(G)), tuple(b_ref[g] for g in range(G)))
        du, dv, dz, dlw, dlb, dW, dbb = vjp(do_ref[...])
        dW, dbb = jnp.stack(dW, axis=0), jnp.stack(dbb, axis=0)
        dp_ref[:, 0:B] = du.astype(BF16)
        dp_ref[:, B:2 * B] = dv.astype(BF16)
        dp_ref[:, 2 * B:3 * B] = dz.astype(BF16)
        i = pl.program_id(0)

        @pl.when(i == 0)
        def _():
            dlw_ref[...] = dlw
            dlb_ref[...] = dlb
            dw_ref[...] = dW
            dbb_ref[...] = dbb

        @pl.when(i > 0)
        def _():
            dlw_ref[...] += dlw
            dlb_ref[...] += dlb
            dw_ref[...] += dW
            dbb_ref[...] += dbb

        @pl.when(i == nt - 1)
        def _():
            db_ref[...] = jnp.sum(dbb_ref[...], axis=-1, keepdims=True)

    row = pl.BlockSpec((1, B), lambda i: (0, 0))
    cube = pl.BlockSpec((G, CHUNK_B, CHUNK_B), lambda i: (0, 0, 0))
    return pl.pallas_call(
        body, name="sgu_bwd", grid=(nt,),
        in_specs=[pl.BlockSpec((tm, B), lambda i: (i, cu)), pl.BlockSpec((tm, B), lambda i: (i, cv)),
                  pl.BlockSpec((tm, B), lambda i: (i, cz)), row, row, cube, cube,
                  pl.BlockSpec((tm, B), lambda i: (i, A // B)), ANY],
        out_specs=[pl.BlockSpec((tm, 3 * B), lambda i: (i, 1)), row, row, cube,
                   pl.BlockSpec((G, CHUNK_B, 1), lambda i: (0, 0, 0))],
        out_shape=[jax.ShapeDtypeStruct(dproj.shape, dproj.dtype), jax.ShapeDtypeStruct((1, B), F32),
                   jax.ShapeDtypeStruct((1, B), F32), jax.ShapeDtypeStruct((G, CHUNK_B, CHUNK_B), F32),
                   jax.ShapeDtypeStruct((G, CHUNK_B, 1), F32)],
        input_output_aliases={8: 0},
        scratch_shapes=[pltpu.VMEM((G, CHUNK_B, CHUNK_B), F32)],
        compiler_params=_cparams(("arbitrary",)),
    )(proj_m, proj_m, proj_m, lw, lb, W, bbc, d_o, dproj)


def _head_fn(mix, x, fw, tgt):
    h = x + mix
    y = _rms_fn(h, fw)
    e = y - tgt
    return 0.5 * jnp.sum(jnp.mean(e * e, axis=-1, keepdims=True), axis=0, keepdims=True)


def _out_proj_loss(oa, ob, wout, x, tgt, fw):
    T, A = oa.shape
    B = ob.shape[1]
    D = x.shape[1]
    tm = _pick(T, (256, 128))

    def body(oa_ref, ob_ref, w_ref, x_ref, t_ref, fw_ref, dh_ref, dhb_ref, loss_ref, dfw_ref):
        mix = _dot(oa_ref[...], w_ref[0:A, :]) + _dot(ob_ref[...], w_ref[A:A + B, :])
        xv, tv = x_ref[...], t_ref[...]
        loss, vjp = jax.vjp(lambda m, f: _head_fn(m, xv, f, tv), mix, fw_ref[...])
        dh, dfw = vjp(jnp.ones((1, 1), F32))
        dh_ref[...] = dh
        dhb_ref[...] = dh.astype(BF16)
        lrow = jnp.broadcast_to(loss, (1, LANES))

        @pl.when(pl.program_id(0) == 0)
        def _():
            loss_ref[...] = lrow
            dfw_ref[...] = dfw

        @pl.when(pl.program_id(0) > 0)
        def _():
            loss_ref[...] += lrow
            dfw_ref[...] += dfw

    tile = pl.BlockSpec((tm, D), lambda i: (i, 0))
    return pl.pallas_call(
        body, name="out_proj_loss", grid=(T // tm,),
        in_specs=[pl.BlockSpec((tm, A), lambda i: (i, 0)), pl.BlockSpec((tm, B), lambda i: (i, 0)),
                  pl.BlockSpec((A + B, D), lambda i: (0, 0)), tile, tile,
                  pl.BlockSpec((1, D), lambda i: (0, 0))],
        out_specs=[tile, tile, pl.BlockSpec((1, LANES), lambda i: (0, 0)),
                   pl.BlockSpec((1, D), lambda i: (0, 0))],
        out_shape=[jax.ShapeDtypeStruct((T, D), F32), jax.ShapeDtypeStruct((T, D), BF16),
                   jax.ShapeDtypeStruct((1, LANES), F32), jax.ShapeDtypeStruct((1, D), F32)],
        compiler_params=_cparams(("arbitrary",)),
    )(oa, ob, wout, x, tgt, fw)


def _adamw(w, g, m, v, name):
    R, Cn = w.shape
    cap = max(8, 512 * 1024 // Cn)
    tr = max(t for t in range(8, min(R, cap) + 1, 8) if R % t == 0) if R > cap else R

    def body(w_ref, g_ref, m_ref, v_ref, d_ref, mo_ref, vo_ref):
        g = g_ref[...]
        m = ADAM_B1 * m_ref[...] + (1.0 - ADAM_B1) * g
        v = ADAM_B2 * v_ref[...] + (1.0 - ADAM_B2) * jnp.square(g)
        m_hat = m / (1.0 - ADAM_B1 ** ADAM_STEP)
        v_hat = v / (1.0 - ADAM_B2 ** ADAM_STEP)
        d_ref[...] = -ADAM_LR * (m_hat / (jnp.sqrt(v_hat) + ADAM_EPS) + ADAM_WD * w_ref[...])
        mo_ref[...] = m
        vo_ref[...] = v

    tile = pl.BlockSpec((tr, Cn), lambda i: (i, 0))
    shape = jax.ShapeDtypeStruct((R, Cn), F32)
    return pl.pallas_call(
        body, name=name, grid=(R // tr,), in_specs=[tile] * 4, out_specs=[tile] * 3,
        out_shape=[shape] * 3, compiler_params=_cparams(("parallel",)),
    )(w, g, m, v)


def _place():
    x, y, c = lax.axis_index("x"), lax.axis_index("y"), lax.axis_index("c")
    others = [(1 - x, y), (x, 1 - y), (1 - x, 1 - y)]
    return x, y, c, others


def _chip_index(px, py):
    return 2 * px + py


ANY = pl.BlockSpec(memory_space=pl.ANY)


def _gather_ride(blocks, split):
    n = len(blocks)

    def plan(in_refs, out_refs, send_sems, recv_sems):
        x, y, c, _ = _place()
        me, kx, ky, kd = (_chip_index(px, py) for px, py in ((x, y), (1 - x, y), (x, 1 - y), (1 - x, 1 - y)))
        to_x, to_y, to_s = (1 - x, y, c), (x, 1 - y, c), (x, y, 1 - c)

        def copy(sem, src, dst, to):
            return pltpu.make_async_remote_copy(src_ref=src, dst_ref=dst, send_sem=send_sems.at[sem],
                                                recv_sem=recv_sems.at[sem], device_id=to, device_id_type=MESH_ID)

        first, second, third, awaited = [], [], [], []
        for a in range(n):
            out, s0 = out_refs[a], 8 * a
            if not split[a]:
                for j, (k, to) in enumerate(((kx, to_x), (ky, to_y), (kd, (1 - x, 1 - y, c)))):
                    first.append(lambda j=j, to=to, a=a, out=out, s0=s0: copy(s0 + j, in_refs[a], out.at[me], to))
                    awaited.append((lambda j=j, k=k, to=to, out=out, s0=s0: copy(s0 + j, out.at[k], out.at[k], to),
                                    None))
                continue
            h = blocks[a].shape[0] // 2
            q = h // 2
            half = lambda k, core, out=out, h=h: out.at[k, pl.ds(core * h, h), :]
            quarter = lambda k, core, i, out=out, h=h, q=q: out.at[k, pl.ds(core * h + i * q, q), :]
            mine = in_refs[a].at[pl.ds(c * h, h), :]
            first.append(lambda s0=s0, mine=mine, half=half: copy(s0, mine, half(me, c), to_x))
            first.append(lambda s0=s0, mine=mine, half=half: copy(s0 + 1, mine, half(me, c), to_y))
            fwd0 = lambda s0=s0, quarter=quarter: copy(s0 + 2, quarter(kx, c, 0), quarter(kx, c, 0), to_y)
            fwd1 = lambda s0=s0, quarter=quarter: copy(s0 + 3, quarter(ky, c, 1), quarter(ky, c, 1), to_x)
            pieces = [(s0 + 0, lambda half=half: half(kx, c), lambda half=half: half(kx, 1 - c), to_x, fwd0),
                      (s0 + 1, lambda half=half: half(ky, c), lambda half=half: half(ky, 1 - c), to_y, fwd1),
                      (s0 + 2, lambda quarter=quarter: quarter(kd, c, 0), lambda quarter=quarter: quarter(kd, 1 - c, 0),
                       to_y, None),
                      (s0 + 3, lambda quarter=quarter: quarter(kd, c, 1), lambda quarter=quarter: quarter(kd, 1 - c, 1),
                       to_x, None)]
            for i, (sem, here, there, frm, fwd) in enumerate(pieces):
                passing = lambda s0=s0, i=i, here=here: copy(s0 + 4 + i, here(), here(), to_s)
                awaited.append((lambda sem=sem, here=here, frm=frm: copy(sem, here(), here(), frm), (fwd, passing)))
                if fwd is not None:
                    second.append(fwd)
                third.append((passing, lambda s0=s0, i=i, there=there: copy(s0 + 4 + i, there(), there(), to_s)))
        return first, second, third, awaited

    def start(*refs):
        for send in plan(*refs)[0]:
            send().start()

    def finish(*refs):
        first, second, third, awaited = plan(*refs)
        for arrival, then in awaited:
            arrival().wait_recv()
            for nxt in (then or ()):
                if nxt is not None:
                    nxt().start()
        for _, from_sibling in third:
            from_sibling().wait_recv()
        for send in first + second + [p for p, _ in third]:
            send().wait_send()

    shapes = [jax.ShapeDtypeStruct((N_CHIPS,) + b.shape, b.dtype) for b in blocks]
    return _Ride(blocks, shapes, 8 * n, start, finish)


def _put_own(gathered, own):
    me = _chip_index(lax.axis_index("x"), lax.axis_index("y"))
    return lax.dynamic_update_index_in_dim(gathered, own, me, 0)


def _allreduce_small(buf):
    R0, L = buf.shape
    R = -(-R0 // 16) * 16
    h = R // 2
    buf = jnp.pad(buf, ((0, R - R0), (0, 0)))

    def body(in_ref, out_ref, sib_ref, pair_ref, chips_ref, send_sems, recv_sems):
        x, y, c, others = _place()
        me = _chip_index(x, y)
        sibling = (x, y, 1 - c)

        def copy(sem, src, dst, to):
            return pltpu.make_async_remote_copy(src_ref=src, dst_ref=dst, send_sem=send_sems.at[sem],
                                                recv_sem=recv_sems.at[sem], device_id=to, device_id_type=MESH_ID)

        cp = copy(0, in_ref, sib_ref, sibling)
        cp.start()
        cp.wait()
        pair_ref[...] = in_ref[...] + sib_ref[...]
        rows = lambda core: pl.ds(pl.multiple_of(core * h, 8), h)
        sends = [copy(1 + j, pair_ref.at[rows(c), :], chips_ref.at[me], (*chip, c)) for j, chip in enumerate(others)]
        for s in sends:
            s.start()
        chips_ref[me] = pair_ref[rows(c), :]
        for j, chip in enumerate(others):
            k = _chip_index(*chip)
            copy(1 + j, chips_ref.at[k], chips_ref.at[k], (*chip, c)).wait_recv()
        out_ref[rows(c), :] = ((chips_ref[0] + chips_ref[1]) + chips_ref[2]) + chips_ref[3]
        swap = copy(4, out_ref.at[rows(c), :], out_ref.at[rows(c), :], sibling)
        swap.start()
        copy(4, out_ref.at[rows(1 - c), :], out_ref.at[rows(1 - c), :], sibling).wait_recv()
        for s in sends + [swap]:
            s.wait_send()

    vm = pl.BlockSpec(memory_space=pltpu.VMEM)
    return pl.pallas_call(
        body, name="allreduce_small", in_specs=[vm], out_specs=vm,
        out_shape=jax.ShapeDtypeStruct((R, L), F32),
        scratch_shapes=[pltpu.VMEM((R, L), F32), pltpu.VMEM((R, L), F32), pltpu.VMEM((N_CHIPS, h, L), F32),
                        pltpu.SemaphoreType.DMA((5,)), pltpu.SemaphoreType.DMA((5,))],
        compiler_params=pltpu.CompilerParams(vmem_limit_bytes=VMEM_LIMIT),
    )(buf)[:R0]


def _pair_ride(g):
    nb, R, Cn = g.shape
    h = R // 2

    def copy(in_refs, out_refs, send_sems, recv_sems):
        x, y, c, _ = _place()
        return pltpu.make_async_remote_copy(src_ref=in_refs[0].at[:, pl.ds((1 - c) * h, h), :], dst_ref=out_refs[0],
                                            send_sem=send_sems.at[0], recv_sem=recv_sems.at[0],
                                            device_id=(x, y, 1 - c), device_id_type=MESH_ID)

    return _Ride([g], [jax.ShapeDtypeStruct((nb, h, Cn), g.dtype)], 1,
                 lambda *refs: copy(*refs).start(), lambda *refs: copy(*refs).wait())


def _pair_sum(g, land, c_arr, name, ride=None):
    nb, R, Cn = g.shape
    hr = R // 2
    tr = _pick(hr, (256, 128, 64, 32, 16))
    nt = hr // tr

    def body(c_ref, g_ref, l_ref, o_ref):
        o_ref[...] = (g_ref[...].astype(F32) + l_ref[...].astype(F32)).astype(BF16)

    return _pallas(
        body, (c_arr, g, land), name=name, prefetch=1, grid=(nb, nt),
        in_specs=[pl.BlockSpec((1, tr, Cn), lambda b, i, c_ref: (b, c_ref[0] * nt + i, 0)),
                  pl.BlockSpec((1, tr, Cn), lambda b, i, c_ref: (b, i, 0))],
        out_specs=pl.BlockSpec((1, tr, Cn), lambda b, i, c_ref: (b, i, 0)),
        out_shape=jax.ShapeDtypeStruct((nb, hr, Cn), BF16),
        semantics=("parallel", "parallel"), ride=ride)


def _chip_ride(parts, cols=None):
    m = len(parts)

    def copies(in_refs, out_refs, send_sems, recv_sems):
        x, y, c, others = _place()
        me = _chip_index(x, y)
        def mk(j, chip, n, landing):
            k = _chip_index(*chip)
            src = in_refs[n].at[k]
            if cols is not None:
                src = src.at[:, pl.ds(pl.multiple_of(cols[0](k), LANES), cols[1])]
            return pltpu.make_async_remote_copy(
                src_ref=src, dst_ref=out_refs[n].at[landing(k)], send_sem=send_sems.at[m * j + n],
                recv_sem=recv_sems.at[m * j + n], device_id=(*chip, c), device_id_type=MESH_ID)

        pairs = [(j, chip, n) for j, chip in enumerate(others) for n in range(m)]
        return pairs, (lambda *p: mk(*p, lambda k: me)), (lambda *p: mk(*p, lambda k: k))

    def start(*refs):
        pairs, send, _ = copies(*refs)
        for p in pairs:
            send(*p).start()

    def finish(*refs):
        pairs, send, arrival = copies(*refs)
        for p in pairs:
            arrival(*p).wait_recv()
        for p in pairs:
            send(*p).wait_send()

    width = lambda p: p.shape[2] if cols is None else cols[1]
    return _Ride(parts, [jax.ShapeDtypeStruct(p.shape[:2] + (width(p),), p.dtype) for p in parts], 3 * m,
                 start, finish)


def _put_own_slot(q, p, cols=None):
    me = _chip_index(lax.axis_index("x"), lax.axis_index("y"))
    own = lax.dynamic_index_in_dim(p, me, 0, keepdims=False)
    if cols is not None:
        own = lax.dynamic_slice_in_dim(own, cols[0](me), cols[1], axis=1)
    return lax.dynamic_update_index_in_dim(q, own, me, 0)


def _chip_sum(q, c_arr, name):
    nb, hr, Cn = q.shape
    tr = _pick(hr, (256, 128, 64, 32, 16))
    nt = hr // tr

    def body(c_ref, q_ref, o_ref):
        f = lambda k: q_ref[k].astype(F32)
        o_ref[...] = ((f(0) + f(1)) + f(2)) + f(3)

    return _pallas(
        body, (c_arr, q), name=name, prefetch=1, grid=(nt,),
        in_specs=[pl.BlockSpec((nb, tr, Cn), lambda i, c_ref: (0, i, 0))],
        out_specs=pl.BlockSpec((tr, Cn), lambda i, c_ref: (c_ref[0] * nt + i, 0)),
        out_shape=jax.ShapeDtypeStruct((2 * hr, Cn), F32),
        semantics=("parallel",))


def _sibling_fill(fw, fo):
    def body(_, __, fw_ref, fo_ref, send_sems, recv_sems):
        x, y, c, _ = _place()
        copies = []
        for n, ref in enumerate((fw_ref, fo_ref)):
            h = ref.shape[0] // 2
            mine = ref.at[pl.ds(c * h, h), :]
            theirs = ref.at[pl.ds((1 - c) * h, h), :]
            mk = lambda src, dst: pltpu.make_async_remote_copy(
                src_ref=src, dst_ref=dst, send_sem=send_sems.at[n], recv_sem=recv_sems.at[n],
                device_id=(x, y, 1 - c), device_id_type=MESH_ID)
            send = mk(mine, mine)
            send.start()
            copies.append((send, mk(theirs, theirs)))
        for send, arrival in copies:
            arrival.wait_recv()
            send.wait_send()

    return pl.pallas_call(
        body, name="sibling_fill", in_specs=[ANY, ANY], out_specs=[ANY, ANY],
        out_shape=[jax.ShapeDtypeStruct(fw.shape, F32), jax.ShapeDtypeStruct(fo.shape, F32)],
        input_output_aliases={0: 0, 1: 1},
        scratch_shapes=[pltpu.SemaphoreType.DMA((2,)), pltpu.SemaphoreType.DMA((2,))],
        compiler_params=pltpu.CompilerParams(has_side_effects=True),
    )(fw, fo)


class _Layout:
    def __init__(self, H, G, nb, Cb):
        A, B = H * HEAD_DIM, G * HEAD_DIM
        self.n_main = 4 * A + 3 * B
        self.k = -(-(self.n_main + LANES) // WIN_BLOCK) * WIN_BLOCK
        cuts = [0, 3 * A, 4 * A, 4 * A + 2 * H, nb * Cb]
        starts = [0, 3 * A + 3 * B, self.n_main, 3 * A]
        self.pieces = []
        self.windows, self.runs = [], []
        for n in range(nb):
            segs = []
            for s in range(4):
                lo, hi = max(cuts[s], n * Cb), min(cuts[s + 1], (n + 1) * Cb)
                if lo < hi:
                    segs.append((starts[s] + lo - cuts[s], lo - n * Cb, hi - lo))
            self.pieces += [(own, n, col, ln) for own, col, ln in segs]
            blocks = sorted({b for own, _, ln in segs for b in range(own // WIN_BLOCK, (own + ln - 1) // WIN_BLOCK + 1)})
            self.windows.append(blocks)
            self.runs.append([(blocks.index(own // WIN_BLOCK) * WIN_BLOCK + own % WIN_BLOCK, ln)
                              for own, _, ln in segs])
        self.wb = max(len(b) for b in self.windows)
        self.table = [b + [b[-1]] * (self.wb - len(b)) for b in self.windows]
        self.pieces.sort()
        self.used_from = [min(c for c, _ in r) // LANES * LANES for r in self.runs]
        self.used = max(-(-max(c + ln for c, ln in r) // LANES) * LANES - f for r, f in zip(self.runs, self.used_from))
        self.used_from = [min(f, self.wb * WIN_BLOCK - self.used) for f in self.used_from]

    def to_own_order(self, g_in):
        D = g_in.shape[1]
        cols, at = [], 0
        for own, n, col, ln in self.pieces:
            if own > at:
                cols.append(jnp.zeros((D, own - at), g_in.dtype))
            cols.append(g_in[n, :, col:col + ln])
            at = own + ln
        if at < self.k:
            cols.append(jnp.zeros((D, self.k - at), g_in.dtype))
        return jnp.concatenate(cols, axis=1)

    def from_window(self, win, chip, Cb):
        pick = lambda runs, f: (lambda w: jnp.concatenate([w[:, c - f:c - f + ln] for c, ln in runs], axis=1))
        return lax.switch(chip, [pick(r, f) for r, f in zip(self.runs, self.used_from)], win)

    def used_start(self, chip):
        return sum(jnp.where(chip == n, f, 0) for n, f in enumerate(self.used_from))


def _device_step(x, tgt, norm_w, win_b, wout_b, conv_b, a_log, dt_bias, head_norm_w, sgu_ln_w, sgu_ln_b,
                 w_spatial, b_spatial, final_norm_w, c_arr):
    T, D = x.shape
    H = a_log.shape[1]
    A = H * HEAD_DIM
    G = w_spatial.shape[0]
    B = G * HEAD_DIM
    nb, Cb, Rb = N_CHIPS, win_b.shape[1], wout_b.shape[0]
    lay = _Layout(H, G, nb, Cb)
    alog_row = jnp.pad(a_log, ((0, 0), (H, LANES - 2 * H)))
    dtb_row = jnp.pad(dt_bias, ((0, 0), (H, LANES - 2 * H)))
    bbc = jnp.broadcast_to(b_spatial[:, :, None], (G, CHUNK_B, CHUNK_B))

    (xn, xn_t), (g_in,) = _rms_in(x, norm_w, ride=_gather_ride([win_b], [True]))
    w_own = lay.to_own_order(_put_own(g_in, win_b))
    proj_m, (g_out, g_conv) = _mm_nn(xn, w_own, F32, "in_proj", tm=2048, cols=(0, lay.n_main),
                                     ride=_gather_ride([wout_b, conv_b], [False, False]))
    wout = _put_own(g_out, wout_b).reshape(nb * Rb, D)
    conv_w = _put_own(g_conv, conv_b).transpose(1, 0, 2).reshape(CONV_WIDTH, nb * conv_b.shape[1])
    q, k, v, gb, bb, proj_ba = _gdn_pre(proj_m, xn, w_own, conv_w, alog_row, dtb_row, H)
    u, w, qg, kd, attn, eg, pinv = _gdn_prep(q, k, v, gb, bb)
    og, sall = _gdn_chain(qg, kd, u, w, attn, eg)
    oa, oa_t = _gdn_post(og, proj_m, head_norm_w)
    ob, ob_t = _sgu_fwd(proj_m, sgu_ln_w, sgu_ln_b, w_spatial, bbc, A)
    dh, dhb, loss_row, d_fnw = _out_proj_loss(oa, ob, wout, x, tgt, final_norm_w.reshape(1, D))

    d_o = _mm_nn(dhb, wout.T, F32, "out_proj_dx", tm=2048)
    dproj = lax.empty((T, lay.k), BF16)
    dproj, d_lw, d_lb, d_ws, d_bs = _sgu_bwd(proj_m, sgu_ln_w, sgu_ln_b, w_spatial, bbc, d_o, A, dproj)
    dog, dproj, d_hw = _gdn_post_bwd(og, proj_m, head_norm_w, d_o, dproj)
    dqg, dkd, du, dw, dat, deg = _gdn_chain_bwd(qg, kd, u, w, attn, eg, sall, dog)
    dq, dk, dv, dgb, dbb = _gdn_prep_bwd(q, k, v, gb, bb, pinv, du, dw, dqg, dkd, dat, deg)
    dc, dproj, d_al, d_dt = _gdn_pre_bwd(proj_m, proj_ba, conv_w, alog_row, dtb_row, dq, dk, dv, dgb, dbb, H,
                                         dproj)
    dproj, d_conv = _conv_bwd(proj_m, dc, conv_w, H, dproj)

    table = jnp.array([b for row in lay.table for b in row], jnp.int32)
    d_win = _mm_windows(xn_t, dproj, table, nb, "in_proj_dw")
    d_wout, (land_w,) = _mm_nn_pair(oa_t, ob_t, dhb, "out_proj_dw", ride=_pair_ride(d_win))
    d_wout = d_wout.reshape(nb, Rb, D)
    pair_w, (land_o,) = _pair_sum(d_win, land_w, c_arr, "pair_sum_w_in", ride=_pair_ride(d_wout))
    pair_o = _pair_sum(d_wout, land_o, c_arr, "pair_sum_w_out")
    used = (lay.used_start, lay.used)
    dxn, (all_w,) = _mm_nt_rhs_outer(dproj, w_own, F32, "in_proj_dx", ride=_chip_ride([pair_w], used))
    (grad_x, d_nw), (all_o,) = _rms_in_bwd(x, norm_w, dxn, dh, ride=_chip_ride([pair_o]))
    all_w, all_o = _put_own_slot(all_w, pair_w, used), _put_own_slot(all_o, pair_o)
    small = dict(norm_w=d_nw, conv_w=d_conv[:CONV_WIDTH], a_log=d_al[:, H:2 * H], dt_bias=d_dt[:, H:2 * H],
                 head_norm_w=d_hw, sgu_ln_w=d_lw, sgu_ln_b=d_lb, w_spatial=d_ws, b_spatial=d_bs[:, :, 0],
                 final_norm_w=d_fnw)
    return loss_row, grad_x, small, all_w, all_o


SMALL = ("norm_w", "conv_w", "a_log", "dt_bias", "head_norm_w", "sgu_ln_w", "sgu_ln_b", "w_spatial",
         "b_spatial", "final_norm_w")


def _pack(parts):
    rows = []
    for p in parts:
        f = p.reshape(-1)
        f = jnp.pad(f, (0, (-f.shape[0]) % (8 * LANES)))
        rows.append(f.reshape(-1, LANES))
    return jnp.concatenate(rows, axis=0)


def _unpack(buf, shapes):
    out, r = [], 0
    for s in shapes:
        n = 1
        for d in s:
            n *= d
        nr = -(-n // (8 * LANES)) * 8
        out.append(buf[r:r + nr].reshape(-1)[:n].reshape(s))
        r += nr
    return out


def kernel(x, norm_w, w_in, conv_w, a_log, dt_bias, head_norm_w, sgu_ln_w, sgu_ln_b, w_spatial, b_spatial, w_out, final_norm_w, loss_target, m_norm_w, m_w_in, m_conv_w, m_a_log, m_dt_bias, m_head_norm_w, m_sgu_ln_w, m_sgu_ln_b, m_w_spatial, m_b_spatial, m_w_out, m_final_norm_w, v_norm_w, v_w_in, v_conv_w, v_a_log, v_dt_bias, v_head_norm_w, v_sgu_ln_w, v_sgu_ln_b, v_w_spatial, v_b_spatial, v_w_out, v_final_norm_w):
    T, D = x.shape[1], x.shape[2]
    weights = dict(norm_w=norm_w, w_in=w_in, conv_w=conv_w, a_log=a_log, dt_bias=dt_bias, head_norm_w=head_norm_w,
                   sgu_ln_w=sgu_ln_w, sgu_ln_b=sgu_ln_b, w_spatial=w_spatial, b_spatial=b_spatial, w_out=w_out,
                   final_norm_w=final_norm_w)
    mom_m = dict(norm_w=m_norm_w, w_in=m_w_in, conv_w=m_conv_w, a_log=m_a_log, dt_bias=m_dt_bias,
                 head_norm_w=m_head_norm_w, sgu_ln_w=m_sgu_ln_w, sgu_ln_b=m_sgu_ln_b, w_spatial=m_w_spatial,
                 b_spatial=m_b_spatial, w_out=m_w_out, final_norm_w=m_final_norm_w)
    mom_v = dict(norm_w=v_norm_w, w_in=v_w_in, conv_w=v_conv_w, a_log=v_a_log, dt_bias=v_dt_bias,
                 head_norm_w=v_head_norm_w, sgu_ln_w=v_sgu_ln_w, sgu_ln_b=v_sgu_ln_b, w_spatial=v_w_spatial,
                 b_spatial=v_b_spatial, w_out=v_w_out, final_norm_w=v_final_norm_w)
    me = _chip_index(lax.axis_index("x"), lax.axis_index("y"))
    c_arr = lax.axis_index("c").astype(jnp.int32).reshape(1)
    Din, Cb = w_in.shape[1], w_in.shape[2]
    Rb = w_out.shape[1]
    cconv = conv_w.shape[2]

    loss_row, grad_x, g, qw, qo = _device_step(
        x[0], loss_target[0], norm_w, w_in[0].astype(BF16), w_out[0].astype(BF16), conv_w[0], a_log, dt_bias,
        head_norm_w, sgu_ln_w, sgu_ln_b, w_spatial[0], b_spatial[0], final_norm_w, c_arr)

    small_shapes = [tuple(g[n].shape) for n in SMALL] + [(1, LANES)]
    small = _allreduce_small(_pack([g[n] for n in SMALL] + [loss_row]))
    gsum_in, gsum_out = _sibling_fill(_chip_sum(qw, c_arr, "chip_sum_w_in"), _chip_sum(qo, c_arr, "chip_sum_w_out"))
    gsum_in = _Layout(a_log.shape[1], w_spatial.shape[1], N_CHIPS, Cb).from_window(gsum_in, me, Cb)
    *small, loss_sum = _unpack(small, small_shapes)
    gsmall = dict(zip(SMALL, small))
    gsmall["conv_w"] = lax.dynamic_slice_in_dim(gsmall["conv_w"], me * cconv, cconv, axis=1)

    grads, deltas, new_m, new_v = {}, {}, {}, {}
    d, m2, v2 = _adamw(w_out[0], gsum_out, m_w_out[0], v_w_out[0], "adamw_w_out")
    grads["w_out"], deltas["w_out"], new_m["w_out"], new_v["w_out"] = gsum_out[None], d[None], m2[None], v2[None]
    flat = lambda a: a.transpose(2, 0, 1).reshape(-1, LANES)
    unflat = lambda f: f.reshape(Cb, 1, Din).transpose(1, 2, 0)
    g_flat = gsum_in.T.reshape(-1, LANES)
    d, m2, v2 = _adamw(flat(w_in), g_flat, flat(m_w_in), flat(v_w_in), "adamw_w_in")
    grads["w_in"], deltas["w_in"], new_m["w_in"], new_v["w_in"] = unflat(g_flat), unflat(d), unflat(m2), unflat(v2)
    shapes = [tuple(weights[n].shape) for n in SMALL]
    ds, ms, vs = _adamw(_pack([weights[n] for n in SMALL]), _pack([gsmall[n] for n in SMALL]),
                        _pack([mom_m[n] for n in SMALL]), _pack([mom_v[n] for n in SMALL]), "adamw_small")
    for n, gq, d, m2, v2 in zip(SMALL, [gsmall[n] for n in SMALL], _unpack(ds, shapes), _unpack(ms, shapes),
                                _unpack(vs, shapes)):
        grads[n], deltas[n], new_m[n], new_v[n] = gq.reshape(weights[n].shape), d, m2, v2

    loss = loss_sum[0, 0]
    order = ("norm_w", "w_in", "conv_w", "a_log", "dt_bias", "head_norm_w", "sgu_ln_w", "sgu_ln_b", "w_spatial",
             "b_spatial", "w_out", "final_norm_w")
    return (loss, grad_x[None], *[grads[n] for n in order], *[deltas[n] for n in order],
            *[new_m[n] for n in order], *[new_v[n] for n in order])
```

```python
import functools

import jax
import jax.numpy as jnp
from jax import lax
from jax.experimental import pallas as pl
from jax.experimental.pallas import tpu as pltpu

F32 = jnp.float32
BF16 = jnp.bfloat16
EPS = 1e-6
HEAD_DIM = 128
CHUNK_B = 128
CONV_WIDTH = 4
LANES = 128
HALO = 8
N_CHIPS = 4
ADAM_LR = 0.001
ADAM_B1 = 0.9
ADAM_B2 = 0.999
ADAM_EPS = 1e-08
ADAM_WD = 0.01
ADAM_STEP = 10
VMEM_LIMIT = 56 * 1024 * 1024
MESH_ID = pl.DeviceIdType.MESH


def _cparams(sem=None, **kw):
    return pltpu.CompilerParams(dimension_semantics=sem, vmem_limit_bytes=VMEM_LIMIT, **kw)


def _matmul(a, b, ca, cb):
    nb = a.ndim - 2
    batch = tuple(range(nb))
    return lax.dot_general(a, b, (((ca + nb,), (cb + nb,)), (batch, batch)), preferred_element_type=F32)


def _dot(a, b):
    return _matmul(a, b, 1, 0)


def _dot_nt(a, b):
    return _matmul(a, b, 1, 1)


def _dot_tn(a, b):
    return _matmul(a, b, 0, 0)


def _iota(shape, dim):
    return lax.broadcasted_iota(jnp.int32, shape, dim)


def _sigmoid(x):
    return 0.5 * (jnp.tanh(0.5 * x) + 1.0)


def _silu(x):
    return x * _sigmoid(x)


def _softplus(x):
    z = jnp.exp(-jnp.abs(x))
    small = z * (1.0 - z * (0.5 - z * (1.0 / 3.0)))
    return jnp.maximum(x, 0.0) + jnp.where(z < 1e-3, small, jnp.log(1.0 + z))


def _pick(n, pref):
    for t in pref:
        if n % t == 0:
            return t
    return n


class _Ride:
    def __init__(self, operands, out_shape, n_sems, start, finish):
        self.operands, self.out_shape, self.n_sems = list(operands), list(out_shape), n_sems
        self.start, self.finish = start, finish


def _pallas(body, operands, *, name, grid, in_specs, out_specs, out_shape, semantics, scratch_shapes=(),
            prefetch=0, ride=None):
    single = not isinstance(out_shape, (list, tuple))
    outs = [out_shape] if single else list(out_shape)
    ospecs = [out_specs] if single else list(out_specs)
    in_specs, scratch = list(in_specs), list(scratch_shapes)
    n_in, n_out, n_sc = len(operands) - prefetch, len(outs), len(scratch)
    kernel = body
    params = _cparams(semantics)
    if ride is not None:
        n_xin, n_xout = len(ride.operands), len(ride.out_shape)

        def kernel(*refs):
            pre, refs = refs[:prefetch], refs[prefetch:]
            ins, refs = refs[:n_in], refs[n_in:]
            xins, refs = refs[:n_xin], refs[n_xin:]
            mains, refs = refs[:n_out], refs[n_out:]
            xouts, refs = refs[:n_xout], refs[n_xout:]
            sc, (send, recv) = refs[:n_sc], refs[n_sc:]
            ids = [pl.program_id(a) for a in range(len(grid))]
            first = functools.reduce(jnp.logical_and, [i == 0 for i in ids])
            last = functools.reduce(jnp.logical_and, [i == g - 1 for i, g in zip(ids, grid)])

            @pl.when(first)
            def _():
                ride.start(xins, xouts, send, recv)

            body(*pre, *ins, *mains, *sc)

            @pl.when(last)
            def _():
                ride.finish(xins, xouts, send, recv)

        operands = list(operands) + ride.operands
        in_specs += [ANY] * n_xin
        ospecs += [ANY] * n_xout
        outs += ride.out_shape
        scratch += [pltpu.SemaphoreType.DMA((ride.n_sems,)), pltpu.SemaphoreType.DMA((ride.n_sems,))]
        params = _cparams(("arbitrary",) * len(grid), has_side_effects=True)
    if prefetch:
        spec = dict(grid_spec=pltpu.PrefetchScalarGridSpec(
            num_scalar_prefetch=prefetch, grid=grid, in_specs=in_specs, out_specs=ospecs, scratch_shapes=scratch))
    else:
        spec = dict(grid=grid, in_specs=in_specs, out_specs=ospecs, scratch_shapes=scratch)
    res = pl.pallas_call(kernel, name=name, out_shape=outs, compiler_params=params, **spec)(*operands)
    main = res[0] if single else list(res[:n_out])
    return main if ride is None else (main, list(res[n_out:]))


def _mm_nn(a, b, out_dtype, name, tm=1024, tn=512, tk=None, cols=None, ride=None):
    M, K = a.shape
    c0, N = (0, b.shape[1]) if cols is None else cols
    tm = _pick(M, (tm, 1024, 512, 256, 128))
    tn = _pick(N, (tn, 512, 384, 256, 128))
    tk = K if tk is None else _pick(K, (tk,))
    nk = K // tk
    j0 = c0 // tn
    assert c0 % tn == 0

    def body(a_ref, b_ref, o_ref, *scratch):
        part = _dot(a_ref[...], b_ref[...])
        if nk == 1:
            o_ref[...] = part.astype(out_dtype)
        else:
            acc_ref, = scratch
            k = pl.program_id(2)

            @pl.when(k == 0)
            def _():
                acc_ref[...] = part

            @pl.when(k > 0)
            def _():
                acc_ref[...] += part

            @pl.when(k == nk - 1)
            def _():
                o_ref[...] = acc_ref[...].astype(out_dtype)

    return _pallas(
        body, (a, b), name=name, grid=(M // tm, N // tn, nk),
        in_specs=[pl.BlockSpec((tm, tk), lambda i, j, k: (i, k)),
                  pl.BlockSpec((tk, tn), lambda i, j, k: (k, j + j0))],
        out_specs=pl.BlockSpec((tm, tn), lambda i, j, k: (i, j)),
        out_shape=jax.ShapeDtypeStruct((M, N), out_dtype),
        scratch_shapes=[] if nk == 1 else [pltpu.VMEM((tm, tn), F32)],
        semantics=("parallel", "parallel", "arbitrary"), ride=ride)


def _mm_nt_rhs_outer(a, b, out_dtype, name, tm=256, tn=1024, ride=None):
    M, K = a.shape
    N, _ = b.shape
    tm = _pick(M, (tm, 128))
    tn = _pick(N, (tn, 512, 256, 128))

    def body(a_ref, b_ref, o_ref):
        o_ref[...] = _dot_nt(a_ref[...], b_ref[...]).astype(out_dtype)

    return _pallas(
        body, (a, b), name=name, grid=(N // tn, M // tm),
        in_specs=[pl.BlockSpec((tm, K), lambda j, i: (i, 0)),
                  pl.BlockSpec((tn, K), lambda j, i: (j, 0))],
        out_specs=pl.BlockSpec((tm, tn), lambda j, i: (i, j)),
        out_shape=jax.ShapeDtypeStruct((M, N), out_dtype),
        semantics=("parallel", "parallel"), ride=ride)


WIN_BLOCK = 256


def _mm_windows(a, b, table, nb, name, tm=2048):
    M, K = a.shape
    wb = table.shape[0] // nb
    tm = _pick(M, (tm, 1024, 512, 256, 128))

    def body(tab_ref, a_ref, b_ref, o_ref):
        o_ref[0] = _dot(a_ref[...], b_ref[...]).astype(BF16)

    return pl.pallas_call(
        body, name=name,
        grid_spec=pltpu.PrefetchScalarGridSpec(
            num_scalar_prefetch=1, grid=(nb, M // tm, wb),
            in_specs=[pl.BlockSpec((tm, K), lambda n, i, t, tab: (i, 0)),
                      pl.BlockSpec((K, WIN_BLOCK), lambda n, i, t, tab: (0, tab[n * wb + t]))],
            out_specs=pl.BlockSpec((1, tm, WIN_BLOCK), lambda n, i, t, tab: (n, i, t))),
        out_shape=jax.ShapeDtypeStruct((nb, M, wb * WIN_BLOCK), BF16),
        compiler_params=_cparams(("parallel", "parallel", "arbitrary")),
    )(table, a, b)


def _mm_nn_pair(a0, a1, b, name, tm=512, tn=1024, ride=None):
    M, K = a0.shape
    _, N = b.shape
    tm = _pick(M, (tm, 256, 128))
    tn = _pick(N, (tn, 512, 256, 128))
    ni = M // tm

    def body(a0_ref, a1_ref, b_ref, o_ref):
        p = pl.program_id(0)

        @pl.when(p == 0)
        def _():
            o_ref[...] = _dot(a0_ref[...], b_ref[...]).astype(BF16)

        @pl.when(p == 1)
        def _():
            o_ref[...] = _dot(a1_ref[...], b_ref[...]).astype(BF16)

    return _pallas(
        body, (a0, a1, b), name=name, grid=(2, ni, N // tn),
        in_specs=[pl.BlockSpec((tm, K), lambda p, i, j: (i * (1 - p), 0)),
                  pl.BlockSpec((tm, K), lambda p, i, j: (i * p, 0)),
                  pl.BlockSpec((K, tn), lambda p, i, j: (0, j))],
        out_specs=pl.BlockSpec((tm, tn), lambda p, i, j: (p * ni + i, j)),
        out_shape=jax.ShapeDtypeStruct((2 * M, N), BF16),
        semantics=("parallel", "parallel", "parallel"), ride=ride)


def _rms_fn(x, w):
    r = lax.rsqrt(jnp.mean(x * x, axis=-1, keepdims=True) + EPS)
    return x * r * w


def _rms_in(x, w, ride=None):
    T, D = x.shape
    tm = _pick(T, (512, 256, 128))

    def body(x_ref, w_ref, o_ref, ot_ref):
        xn = _rms_fn(x_ref[...], w_ref[...])
        o_ref[...] = xn.astype(BF16)
        ot_ref[...] = xn.T.astype(BF16)

    return _pallas(
        body, (x, w), name="rms_in", grid=(T // tm,),
        in_specs=[pl.BlockSpec((tm, D), lambda i: (i, 0)), pl.BlockSpec((1, D), lambda i: (0, 0))],
        out_specs=[pl.BlockSpec((tm, D), lambda i: (i, 0)), pl.BlockSpec((D, tm), lambda i: (0, i))],
        out_shape=[jax.ShapeDtypeStruct((T, D), BF16), jax.ShapeDtypeStruct((D, T), BF16)],
        semantics=("parallel",), ride=ride)


def _rms_in_bwd(x, w, dxn, dh, ride=None):
    T, D = x.shape
    tm = _pick(T, (256, 128))

    def body(x_ref, w_ref, dxn_ref, dh_ref, gx_ref, dw_ref):
        _, vjp = jax.vjp(_rms_fn, x_ref[...], w_ref[...])
        dx, dw = vjp(dxn_ref[...])
        gx_ref[...] = dh_ref[...] + dx

        @pl.when(pl.program_id(0) == 0)
        def _():
            dw_ref[...] = dw

        @pl.when(pl.program_id(0) > 0)
        def _():
            dw_ref[...] += dw

    tile = pl.BlockSpec((tm, D), lambda i: (i, 0))
    row = pl.BlockSpec((1, D), lambda i: (0, 0))
    return _pallas(
        body, (x, w, dxn, dh), name="rms_in_bwd", grid=(T // tm,),
        in_specs=[tile, row, tile, tile], out_specs=[tile, row],
        out_shape=[jax.ShapeDtypeStruct((T, D), F32), jax.ShapeDtypeStruct((1, D), F32)],
        semantics=("arbitrary",), ride=ride)


def _conv_fwd(cat_ref, halo, x, w):
    tm = x.shape[0]
    cat_ref[0:HALO, :] = halo
    cat_ref[HALO:HALO + tm, :] = x
    c = x * w[CONV_WIDTH - 1:CONV_WIDTH, :]
    for k in range(CONV_WIDTH - 1):
        s = CONV_WIDTH - 1 - k
        c = c + cat_ref[pl.ds(HALO - s, tm), :] * w[k:k + 1, :]
    return c


def _lane_to_all(x, lane):
    @jax.custom_vjp
    def f(x):
        return jnp.broadcast_to(x[:, lane:lane + 1], x.shape)

    def f_fwd(x):
        return f(x), None

    def f_bwd(_, g):
        return (jnp.where(_iota(g.shape, 1) == lane, jnp.sum(g, axis=-1, keepdims=True), 0.0),)

    f.defvjp(f_fwd, f_bwd)
    return f(x)


def _gdn_pointwise(c, ba, alog, dtb, H):
    A = H * HEAD_DIM
    s = _silu(c)
    beta = _sigmoid(ba)
    g = -jnp.exp(alog) * _softplus(ba + dtb)
    qs, ks, vs, gbs, bbs = [], [], [], [], []
    for h in range(H):
        lo = h * HEAD_DIM
        q = s[:, lo:lo + HEAD_DIM]
        k = s[:, A + lo:A + lo + HEAD_DIM]
        qs.append(q * lax.rsqrt(jnp.sum(q * q, axis=-1, keepdims=True) + EPS))
        ks.append(k * lax.rsqrt(jnp.sum(k * k, axis=-1, keepdims=True) + EPS))
        vs.append(s[:, 2 * A + lo:2 * A + lo + HEAD_DIM])
        bbs.append(_lane_to_all(beta, h))
        gbs.append(_lane_to_all(g, H + h))
    st = lambda xs: jnp.stack(xs, axis=0)
    return st(qs), st(ks), st(vs), st(gbs), st(bbs)


def _halo_prev(tm):
    return lambda i: (jnp.maximum(i * (tm // HALO) - 1, 0), 0)


def _gdn_pre(proj_m, xn, w_own, conv_w, alog_row, dtb_row, H):
    T, n_main = proj_m.shape
    D = xn.shape[1]
    A = H * HEAD_DIM
    tm = _pick(T, (256, 128))
    hs = pl.BlockSpec((H, tm, HEAD_DIM), lambda i: (0, i, 0))
    hshape = jax.ShapeDtypeStruct((H, T, HEAD_DIM), F32)

    def body(x_ref, halo_ref, xn_ref, wba_ref, w_ref, al_ref, dt_ref,
             q_ref, k_ref, v_ref, gb_ref, bb_ref, ba_ref, cat_ref):
        halo = jnp.where(pl.program_id(0) == 0, 0.0, halo_ref[...])
        c = _conv_fwd(cat_ref, halo, x_ref[...], w_ref[...])
        ba = _dot(xn_ref[...], wba_ref[...])
        q, k, v, gb, bb = _gdn_pointwise(c, ba, al_ref[...], dt_ref[...], H)
        q_ref[...] = q
        k_ref[...] = k
        v_ref[...] = v
        gb_ref[...] = gb
        bb_ref[...] = bb
        ba_ref[...] = ba

    return pl.pallas_call(
        body, name="gdn_pre", grid=(T // tm,),
        in_specs=[pl.BlockSpec((tm, 3 * A), lambda i: (i, 0)),
                  pl.BlockSpec((HALO, 3 * A), _halo_prev(tm)),
                  pl.BlockSpec((tm, D), lambda i: (i, 0)),
                  pl.BlockSpec((D, LANES), lambda i: (0, n_main // LANES)),
                  pl.BlockSpec((CONV_WIDTH, 3 * A), lambda i: (0, 0)),
                  pl.BlockSpec((1, LANES), lambda i: (0, 0)),
                  pl.BlockSpec((1, LANES), lambda i: (0, 0))],
        out_specs=[hs] * 5 + [pl.BlockSpec((tm, LANES), lambda i: (i, 0))],
        out_shape=[hshape] * 5 + [jax.ShapeDtypeStruct((T, LANES), F32)],
        scratch_shapes=[pltpu.VMEM((HALO + tm, 3 * A), F32)],
        compiler_params=_cparams(("parallel",)),
    )(proj_m, proj_m, xn, w_own, conv_w, alog_row, dtb_row)


def _gdn_pre_bwd(proj_m, proj_ba, conv_w, alog_row, dtb_row, dq, dk, dv, dgb, dbb, H, dproj):
    T, n_main = proj_m.shape
    A = H * HEAD_DIM
    tm = _pick(T, (256, 128))
    hs = pl.BlockSpec((H, tm, HEAD_DIM), lambda i: (0, i, 0))
    row = pl.BlockSpec((1, LANES), lambda i: (0, 0))

    def body(x_ref, halo_ref, ba_ref, w_ref, al_ref, dt_ref, dq_ref, dk_ref, dv_ref, dgb_ref, dbb_ref, _,
             dc_ref, dba_ref, dal_ref, ddt_ref, cat_ref):
        halo = jnp.where(pl.program_id(0) == 0, 0.0, halo_ref[...])
        c = _conv_fwd(cat_ref, halo, x_ref[...], w_ref[...])
        _, vjp = jax.vjp(functools.partial(_gdn_pointwise, H=H), c, ba_ref[...], al_ref[...], dt_ref[...])
        dc, dba, dal, ddt = vjp((dq_ref[...], dk_ref[...], dv_ref[...], dgb_ref[...], dbb_ref[...]))
        dc_ref[...] = dc
        dba_ref[:, :LANES] = dba.astype(BF16)
        dba_ref[:, LANES:] = jnp.zeros((tm, WIN_BLOCK - LANES), BF16)

        @pl.when(pl.program_id(0) == 0)
        def _():
            dal_ref[...] = dal
            ddt_ref[...] = ddt

        @pl.when(pl.program_id(0) > 0)
        def _():
            dal_ref[...] += dal
            ddt_ref[...] += ddt

    return pl.pallas_call(
        body, name="gdn_pre_bwd", grid=(T // tm,),
        in_specs=[pl.BlockSpec((tm, 3 * A), lambda i: (i, 0)),
                  pl.BlockSpec((HALO, 3 * A), _halo_prev(tm)),
                  pl.BlockSpec((tm, LANES), lambda i: (i, 0)),
                  pl.BlockSpec((CONV_WIDTH, 3 * A), lambda i: (0, 0)),
                  row, row, hs, hs, hs, hs, hs, ANY],
        out_specs=[pl.BlockSpec((tm, 3 * A), lambda i: (i, 0)),
                   pl.BlockSpec((tm, WIN_BLOCK), lambda i: (i, n_main // WIN_BLOCK)), row, row],
        out_shape=[jax.ShapeDtypeStruct((T, 3 * A), F32), jax.ShapeDtypeStruct(dproj.shape, dproj.dtype),
                   jax.ShapeDtypeStruct((1, LANES), F32), jax.ShapeDtypeStruct((1, LANES), F32)],
        input_output_aliases={11: 1},
        scratch_shapes=[pltpu.VMEM((HALO + tm, 3 * A), F32)],
        compiler_params=_cparams(("arbitrary",)),
    )(proj_m, proj_m, proj_ba, conv_w, alog_row, dtb_row, dq, dk, dv, dgb, dbb, dproj)


def _conv_bwd(proj_m, dc, conv_w, H, dproj):
    T = proj_m.shape[0]
    A = H * HEAD_DIM
    tm = _pick(T, (256, 128))
    nt = T // tm

    def body(x_ref, halo_ref, dc_ref, nxt_ref, w_ref, _, dx_ref, dw_ref):
        i = pl.program_id(0)
        halo = jnp.where(i == 0, 0.0, halo_ref[...])
        xcat = jnp.concatenate([halo, x_ref[...]], axis=0)
        nxt = jnp.where(i == nt - 1, 0.0, nxt_ref[...])
        dc = dc_ref[...]
        dcat = jnp.concatenate([dc, nxt], axis=0)
        w = w_ref[...]
        dx = None
        rows = []
        for k in range(CONV_WIDTH):
            s = CONV_WIDTH - 1 - k
            ds = dcat if s == 0 else pltpu.roll(dcat, tm + HALO - s, 0)
            term = ds[:tm, :] * w[k:k + 1, :]
            dx = term if dx is None else dx + term
            xs = xcat if s == 0 else pltpu.roll(xcat, s, 0)
            rows.append(jnp.sum(dc * xs[HALO:, :], axis=0, keepdims=True))
        dx_ref[...] = dx.astype(BF16)
        dw = jnp.concatenate(rows + [jnp.zeros((HALO - CONV_WIDTH, 3 * A), F32)], axis=0)

        @pl.when(i == 0)
        def _():
            dw_ref[...] = dw

        @pl.when(i > 0)
        def _():
            dw_ref[...] += dw

    return pl.pallas_call(
        body, name="conv_bwd", grid=(nt,),
        in_specs=[pl.BlockSpec((tm, 3 * A), lambda i: (i, 0)),
                  pl.BlockSpec((HALO, 3 * A), _halo_prev(tm)),
                  pl.BlockSpec((tm, 3 * A), lambda i: (i, 0)),
                  pl.BlockSpec((HALO, 3 * A), lambda i: (jnp.minimum((i + 1) * (tm // HALO), T // HALO - 1), 0)),
                  pl.BlockSpec((CONV_WIDTH, 3 * A), lambda i: (0, 0)), ANY],
        out_specs=[pl.BlockSpec((tm, 3 * A), lambda i: (i, 0)),
                   pl.BlockSpec((HALO, 3 * A), lambda i: (0, 0))],
        out_shape=[jax.ShapeDtypeStruct(dproj.shape, dproj.dtype), jax.ShapeDtypeStruct((HALO, 3 * A), F32)],
        input_output_aliases={5: 0},
        compiler_params=_cparams(("arbitrary",)),
    )(proj_m, proj_m, dc, dc, conv_w, dproj)


CHUNK = 128
BLOCK = 64


def _b(x):
    return x.astype(BF16)


@jax.custom_vjp
def _bdot(a, b):
    return _dot(_b(a), _b(b))


def _bdot_f(a, b):
    return _bdot(a, b), (a, b)


def _bdot_b(res, g):
    a, b = res
    return _dot_nt(_b(g), _b(b)), _dot_tn(_b(a), _b(g))


_bdot.defvjp(_bdot_f, _bdot_b)


@jax.custom_vjp
def _bdot_nt(a, b):
    return _dot_nt(_b(a), _b(b))


def _bdot_nt_f(a, b):
    return _bdot_nt(a, b), (a, b)


def _bdot_nt_b(res, g):
    a, b = res
    return _dot(_b(g), _b(b)), _dot_tn(_b(g), _b(a))


_bdot_nt.defvjp(_bdot_nt_f, _bdot_nt_b)


@jax.custom_vjp
def _bdot_tn(a, b):
    return _dot_tn(_b(a), _b(b))


def _bdot_tn_f(a, b):
    return _bdot_tn(a, b), (a, b)


def _bdot_tn_b(res, g):
    a, b = res
    return _dot_nt(_b(b), _b(g)), _dot(_b(a), _b(g))


_bdot_tn.defvjp(_bdot_tn_f, _bdot_tn_b)


def _mask_matmul(m, x):
    hi = _b(x)
    r = x - hi.astype(F32)
    mid = _b(r)
    lo = _b(r - mid.astype(F32))
    return (_dot(m, lo) + _dot(m, mid)) + _dot(m, hi)


@jax.custom_vjp
def _mask_dot(m, mt, x):
    return _mask_matmul(m, x)


def _mask_dot_f(m, mt, x):
    return _mask_matmul(m, x), (m, mt)


def _mask_dot_b(res, g):
    m, mt = res
    return jnp.zeros_like(m), jnp.zeros_like(mt), _mask_matmul(mt, g)


_mask_dot.defvjp(_mask_dot_f, _mask_dot_b)

def _unit_lower_inverse(L):
    n = L.shape[-1]
    X = -L
    Q = X
    for _ in range(BLOCK.bit_length() - 2):
        X = _dot(_b(X), _b(X))
        Q = Q + X + _dot(_b(Q), _b(X))
    return (_iota((n, n), 0) == _iota((n, n), 1)).astype(F32) + Q


@jax.custom_vjp
def _known_inverse(L, P):
    return P


def _known_inverse_f(L, P):
    return P, P


def _known_inverse_b(P, g):
    n = P.shape[-1]
    Q = _b(P - (_iota((n, n), 0) == _iota((n, n), 1)).astype(F32))
    t = g + _dot_tn(Q, _b(g))
    return -(t + _dot_nt(_b(t), Q)), jnp.zeros_like(P)


_known_inverse.defvjp(_known_inverse_f, _known_inverse_b)


def _gdn_prep_fn(q, k, v, gb, bb, P_known=None):
    n = CHUNK
    row, col = _iota((n, n), 0), _iota((n, n), 1)
    same = (row // BLOCK) == (col // BLOCK)
    incl, strict = same & (row >= col), same & (row > col)
    bc = lambda m: jnp.broadcast_to(_b(m.astype(F32)), q.shape[:1] + (n, n))
    tril, triu, ones = bc(incl), bc(same & (row <= col)), bc(same)
    gc = _mask_dot(tril, triu, gb)
    gl = _mask_dot(ones, ones, gb)
    decay = jnp.where(incl, jnp.exp(jnp.where(incl, gc - jnp.swapaxes(gc, 1, 2), 0.0)), 0.0)
    kb = k * bb
    vb = v * bb
    qs = q * (HEAD_DIM ** -0.5)
    L = jnp.where(strict, _bdot_nt(kb, k) * decay, 0.0)
    P = _unit_lower_inverse(L) if P_known is None else _known_inverse(L, P_known)
    egc = jnp.exp(gc)
    u = _bdot(P, vb)
    w = _bdot(P, kb * egc)
    attn = jnp.where(incl, _bdot_nt(qs, k) * decay, 0.0)
    qg = qs * egc
    kdec = k * jnp.exp(gl - gc)
    eg = jnp.exp(gl).reshape(-1, n // BLOCK, BLOCK, LANES).sum(axis=2) * (1.0 / BLOCK)
    if P_known is None:
        return u, w, qg, kdec, attn, eg, P
    return u, w, qg, kdec, attn, eg


def _gdn_chain_fn(S, qg, kdec, u, w, attn, eg):
    nblk = CHUNK // BLOCK
    cat = lambda xs: jnp.concatenate(xs, axis=1)
    outs, found = [], []
    for i in range(nblk):
        r = (slice(None), slice(i * BLOCK, (i + 1) * BLOCK))
        v_new = u[r] - _bdot(w[r], S)
        found.append(v_new)
        outs.append(_bdot(qg[r], S) + _bdot(attn[r], cat(found + [jnp.zeros_like(v_new)] * (nblk - 1 - i))))
        S = S * eg[i] + _bdot_tn(kdec[r], v_new)
    return cat(outs), S


def _eg_spec(H, T, chunks, index_map, per_head):
    nblk = CHUNK // BLOCK
    block = (chunks, 1 if per_head else H, nblk, LANES)
    return pl.BlockSpec(block, index_map), jax.ShapeDtypeStruct((T // CHUNK, H, nblk, LANES), F32)


def _gdn_prep(q, k, v, gb, bb):
    H, T, _ = q.shape
    pb = _pick(T // CHUNK, (8, 4, 2, 1))
    hs = pl.BlockSpec((1, CHUNK * pb, HEAD_DIM), lambda h, n: (h, n, 0))

    def body(q_ref, k_ref, v_ref, gb_ref, bb_ref, *out_refs):
        chunks = lambda ref: ref[0].reshape(pb, CHUNK, HEAD_DIM)
        outs = _gdn_prep_fn(chunks(q_ref), chunks(k_ref), chunks(v_ref), chunks(gb_ref), chunks(bb_ref))
        for i, (ref, val) in enumerate(zip(out_refs, outs)):
            if i == 5:
                ref[:, 0] = val
            else:
                ref[0] = val.reshape(pb * CHUNK, HEAD_DIM).astype(ref.dtype)

    kept = [F32, BF16, BF16, BF16, BF16, None, BF16]
    es, eshape = _eg_spec(H, T, pb, lambda h, n: (n, h, 0, 0), per_head=True)
    return pl.pallas_call(
        body, name="gdn_prep", grid=(H, T // (CHUNK * pb)),
        in_specs=[hs] * 5, out_specs=[es if dt is None else hs for dt in kept],
        out_shape=[eshape if dt is None else jax.ShapeDtypeStruct((H, T, HEAD_DIM), dt) for dt in kept],
        compiler_params=_cparams(("parallel", "parallel")),
    )(q, k, v, gb, bb)


def _gdn_prep_bwd(q, k, v, gb, bb, pinv, du, dw, dqg, dkd, dat, deg):
    H, T, _ = q.shape
    pb = _pick(T // CHUNK, (8, 4, 2, 1))
    hs = pl.BlockSpec((1, CHUNK * pb, HEAD_DIM), lambda h, n: (h, n, 0))
    hshape = jax.ShapeDtypeStruct((H, T, HEAD_DIM), F32)

    def body(*refs):
        in_refs, p_ref, ct_refs, out_refs = refs[:5], refs[5], refs[6:12], refs[12:]
        chunks = lambda ref: ref[0].reshape(pb, CHUNK, HEAD_DIM)
        P = chunks(p_ref).astype(F32)
        _, vjp = jax.vjp(lambda *a: _gdn_prep_fn(*a, P_known=P), *[chunks(r) for r in in_refs])
        grads = vjp(tuple(chunks(r).astype(F32) for r in ct_refs[:5]) + (ct_refs[5][:, 0],))
        for ref, val in zip(out_refs, grads):
            ref[0] = val.reshape(pb * CHUNK, HEAD_DIM)

    es, _ = _eg_spec(H, T, pb, lambda h, n: (n, h, 0, 0), per_head=True)
    return pl.pallas_call(
        body, name="gdn_prep_bwd", grid=(H, T // (CHUNK * pb)),
        in_specs=[hs] * 11 + [es], out_specs=[hs] * 5, out_shape=[hshape] * 5,
        compiler_params=_cparams(("parallel", "parallel")),
    )(q, k, v, gb, bb, pinv, du, dw, dqg, dkd, dat, deg)


def _gdn_chain(qg, kd, u, w, attn, eg):
    H, T, _ = qg.shape
    N = T // CHUNK
    hs = pl.BlockSpec((H, CHUNK, HEAD_DIM), lambda n: (0, n, 0))
    ss = pl.BlockSpec((1, H, HEAD_DIM, HEAD_DIM), lambda n: (n, 0, 0, 0))

    def body(qg_ref, kd_ref, u_ref, w_ref, at_ref, eg_ref, o_ref, sall_ref, s_ref):
        @pl.when(pl.program_id(0) == 0)
        def _():
            s_ref[...] = jnp.zeros_like(s_ref)

        S = s_ref[...]
        sall_ref[0] = S
        eg = tuple(eg_ref[0, :, i:i + 1, :] for i in range(CHUNK // BLOCK))
        o, S2 = _gdn_chain_fn(S, qg_ref[...], kd_ref[...], u_ref[...], w_ref[...], at_ref[...], eg)
        o_ref[...] = o
        s_ref[...] = S2

    es, _ = _eg_spec(H, T, 1, lambda n: (n, 0, 0, 0), per_head=False)
    return pl.pallas_call(
        body, name="gdn_chain", grid=(N,),
        in_specs=[hs] * 5 + [es], out_specs=[hs, ss],
        out_shape=[jax.ShapeDtypeStruct((H, T, HEAD_DIM), F32),
                   jax.ShapeDtypeStruct((N, H, HEAD_DIM, HEAD_DIM), F32)],
        scratch_shapes=[pltpu.VMEM((H, HEAD_DIM, HEAD_DIM), F32)],
        compiler_params=_cparams(("arbitrary",)),
    )(qg, kd, u, w, attn, eg)


def _gdn_chain_bwd(qg, kd, u, w, attn, eg, sall, do):
    H, T, _ = qg.shape
    N = T // CHUNK
    hs = pl.BlockSpec((H, CHUNK, HEAD_DIM), lambda n: (0, N - 1 - n, 0))
    ss = pl.BlockSpec((1, H, HEAD_DIM, HEAD_DIM), lambda n: (N - 1 - n, 0, 0, 0))

    def body(qg_ref, kd_ref, u_ref, w_ref, at_ref, eg_ref, sall_ref, do_ref, *rest):
        out_refs, ds_ref = rest[:6], rest[6]

        @pl.when(pl.program_id(0) == 0)
        def _():
            ds_ref[...] = jnp.zeros_like(ds_ref)

        f32 = lambda ref: ref[...].astype(F32)
        nblk = CHUNK // BLOCK
        eg = tuple(eg_ref[0, :, i:i + 1, :] for i in range(nblk))
        _, vjp = jax.vjp(_gdn_chain_fn, sall_ref[0], f32(qg_ref), f32(kd_ref), u_ref[...], f32(w_ref),
                         f32(at_ref), eg)
        grads = vjp((do_ref[...], ds_ref[...]))
        ds_ref[...] = grads[0]
        for ref, val in zip(out_refs[:5], grads[1:6]):
            ref[...] = val.astype(ref.dtype)
        for i in range(nblk):
            out_refs[5][0, :, i:i + 1, :] = grads[6][i]

    kept = [F32, F32, BF16, BF16, F32]
    es, eshape = _eg_spec(H, T, 1, lambda n: (N - 1 - n, 0, 0, 0), per_head=False)
    return pl.pallas_call(
        body, name="gdn_chain_bwd", grid=(N,),
        in_specs=[hs] * 5 + [es, ss, hs], out_specs=[hs] * 5 + [es],
        out_shape=[jax.ShapeDtypeStruct((H, T, HEAD_DIM), dt) for dt in kept] + [eshape],
        scratch_shapes=[pltpu.VMEM((H, HEAD_DIM, HEAD_DIM), F32)],
        compiler_params=_cparams(("arbitrary",)),
    )(qg, kd, u, w, attn, eg, sall, do)


def _post_fn(ogs, za, hw):
    outs = []
    for h, o in enumerate(ogs):
        r = lax.rsqrt(jnp.mean(o * o, axis=-1, keepdims=True) + EPS)
        outs.append(o * r * hw * _silu(za[:, h * HEAD_DIM:(h + 1) * HEAD_DIM]))
    return jnp.concatenate(outs, axis=1)


def _gdn_post(og, proj_m, hw):
    H, T, _ = og.shape
    A = H * HEAD_DIM
    tm = _pick(T, (512, 256, 128))

    def body(og_ref, za_ref, hw_ref, o_ref, ot_ref):
        o = _post_fn(tuple(og_ref[h] for h in range(H)), za_ref[...], hw_ref[...])
        o_ref[...] = o.astype(BF16)
        ot_ref[...] = o.T.astype(BF16)

    return pl.pallas_call(
        body, name="gdn_post", grid=(T // tm,),
        in_specs=[pl.BlockSpec((H, tm, HEAD_DIM), lambda i: (0, i, 0)),
                  pl.BlockSpec((tm, A), lambda i: (i, ZA_BLOCK)),
                  pl.BlockSpec((1, HEAD_DIM), lambda i: (0, 0))],
        out_specs=[pl.BlockSpec((tm, A), lambda i: (i, 0)), pl.BlockSpec((A, tm), lambda i: (0, i))],
        out_shape=[jax.ShapeDtypeStruct((T, A), BF16), jax.ShapeDtypeStruct((A, T), BF16)],
        compiler_params=_cparams(("parallel",)),
    )(og, proj_m, hw)


def _gdn_post_bwd(og, proj_m, hw, d_o, dproj):
    H, T, _ = og.shape
    A = H * HEAD_DIM
    tm = _pick(T, (256, 128))

    def body(og_ref, za_ref, hw_ref, do_ref, _, dog_ref, dza_ref, dhw_ref):
        _, vjp = jax.vjp(_post_fn, tuple(og_ref[h] for h in range(H)), za_ref[...], hw_ref[...])
        dog, dza, dhw = vjp(do_ref[...])
        for h in range(H):
            dog_ref[h] = dog[h]
        dza_ref[...] = dza.astype(BF16)

        @pl.when(pl.program_id(0) == 0)
        def _():
            dhw_ref[...] = dhw

        @pl.when(pl.program_id(0) > 0)
        def _():
            dhw_ref[...] += dhw

    return pl.pallas_call(
        body, name="gdn_post_bwd", grid=(T // tm,),
        in_specs=[pl.BlockSpec((H, tm, HEAD_DIM), lambda i: (0, i, 0)),
                  pl.BlockSpec((tm, A), lambda i: (i, ZA_BLOCK)),
                  pl.BlockSpec((1, HEAD_DIM), lambda i: (0, 0)),
                  pl.BlockSpec((tm, A), lambda i: (i, 0)), ANY],
        out_specs=[pl.BlockSpec((H, tm, HEAD_DIM), lambda i: (0, i, 0)),
                   pl.BlockSpec((tm, A), lambda i: (i, ZA_BLOCK)),
                   pl.BlockSpec((1, HEAD_DIM), lambda i: (0, 0))],
        out_shape=[jax.ShapeDtypeStruct((H, T, HEAD_DIM), F32), jax.ShapeDtypeStruct(dproj.shape, dproj.dtype),
                   jax.ShapeDtypeStruct((1, HEAD_DIM), F32)],
        input_output_aliases={4: 1},
        compiler_params=_cparams(("arbitrary",)),
    )(og, proj_m, hw, d_o, dproj)


def _sgu_fn(ub, vb, zb, lw, lb, W, bbc):
    G = len(W)
    tm = ub.shape[0]
    mu = jnp.mean(vb, axis=-1, keepdims=True)
    xc = vb - mu
    var = jnp.mean(xc * xc, axis=-1, keepdims=True)
    vn = xc * lax.rsqrt(var + EPS) * lw + lb
    mask = _iota((CHUNK_B, CHUNK_B), 0) >= _iota((CHUNK_B, CHUNK_B), 1)
    cols = []
    for g in range(G):
        wm = jnp.where(mask, W[g], 0.0).astype(BF16)
        rows = []
        for c in range(tm // CHUNK_B):
            blk = vn[c * CHUNK_B:(c + 1) * CHUNK_B, g * HEAD_DIM:(g + 1) * HEAD_DIM].astype(BF16)
            rows.append(_dot(wm, blk) + bbc[g])
        cols.append(jnp.concatenate(rows, axis=0) if len(rows) > 1 else rows[0])
    s = jnp.concatenate(cols, axis=1)
    return ub * s * _silu(zb)


ZA_BLOCK = 6


def _sgu_cols(A, B):
    assert A == B
    return 3, 4, 5


def _sgu_fwd(proj_m, lw, lb, W, bbc, A):
    T = proj_m.shape[0]
    G = W.shape[0]
    B = G * HEAD_DIM
    tm = _pick(T, (256, 128))
    cu, cv, cz = _sgu_cols(A, B)

    def body(u_ref, v_ref, z_ref, lw_ref, lb_ref, w_ref, b_ref, o_ref, ot_ref):
        o = _sgu_fn(u_ref[...], v_ref[...], z_ref[...], lw_ref[...], lb_ref[...],
                    tuple(w_ref[g] for g in range(G)), tuple(b_ref[g] for g in range(G)))
        o_ref[...] = o.astype(BF16)
        ot_ref[...] = o.T.astype(BF16)

    row = pl.BlockSpec((1, B), lambda i: (0, 0))
    cube = pl.BlockSpec((G, CHUNK_B, CHUNK_B), lambda i: (0, 0, 0))
    return pl.pallas_call(
        body, name="sgu_fwd", grid=(T // tm,),
        in_specs=[pl.BlockSpec((tm, B), lambda i: (i, cu)), pl.BlockSpec((tm, B), lambda i: (i, cv)),
                  pl.BlockSpec((tm, B), lambda i: (i, cz)), row, row, cube, cube],
        out_specs=[pl.BlockSpec((tm, B), lambda i: (i, 0)), pl.BlockSpec((B, tm), lambda i: (0, i))],
        out_shape=[jax.ShapeDtypeStruct((T, B), BF16), jax.ShapeDtypeStruct((B, T), BF16)],
        compiler_params=_cparams(("parallel",)),
    )(proj_m, proj_m, proj_m, lw, lb, W, bbc)


def _sgu_bwd(proj_m, lw, lb, W, bbc, d_o, A, dproj):
    T = proj_m.shape[0]
    G = W.shape[0]
    B = G * HEAD_DIM
    tm = _pick(T, (256, 128))
    nt = T // tm
    cu, cv, cz = _sgu_cols(A, B)

    def body(u_ref, v_ref, z_ref, lw_ref, lb_ref, w_ref, b_ref, do_ref, _,
             dp_ref, dlw_ref, dlb_ref, dw_ref, db_ref, dbb_ref):
        _, vjp = jax.vjp(_sgu_fn, u_ref[...], v_ref[...], z_ref[...], lw_ref[...], lb_ref[...],
                         tuple(w_ref[g] for g in range(G)), tuple(b_ref[g] for g in range(G)))
        du, dv, dz, dlw, dlb, dW, dbb = vjp(do_ref[...])
        dW, dbb = jnp.stack(dW, axis=0), jnp.stack(dbb, axis=0)
        dp_ref[:, 0:B] = du.astype(BF16)
        dp_ref[:, B:2 * B] = dv.astype(BF16)
        dp_ref[:, 2 * B:3 * B] = dz.astype(BF16)
        i = pl.program_id(0)

        @pl.when(i == 0)
        def _():
            dlw_ref[...] = dlw
            dlb_ref[...] = dlb
            dw_ref[...] = dW
            dbb_ref[...] = dbb

        @pl.when(i > 0)
        def _():
            dlw_ref[...] += dlw
            dlb_ref[...] += dlb
            dw_ref[...] += dW
            dbb_ref[...] += dbb

        @pl.when(i == nt - 1)
        def _():
            db_ref[...] = jnp.sum(dbb_ref[...], axis=-1, keepdims=True)

    row = pl.BlockSpec((1, B), lambda i: (0, 0))
    cube = pl.BlockSpec((G, CHUNK_B, CHUNK_B), lambda i: (0, 0, 0))
    return pl.pallas_call(
        body, name="sgu_bwd", grid=(nt,),
        in_specs=[pl.BlockSpec((tm, B), lambda i: (i, cu)), pl.BlockSpec((tm, B), lambda i: (i, cv)),
                  pl.BlockSpec((tm, B), lambda i: (i, cz)), row, row, cube, cube,
                  pl.BlockSpec((tm, B), lambda i: (i, A // B)), ANY],
        out_specs=[pl.BlockSpec((tm, 3 * B), lambda i: (i, 1)), row, row, cube,
                   pl.BlockSpec((G, CHUNK_B, 1), lambda i: (0, 0, 0))],
        out_shape=[jax.ShapeDtypeStruct(dproj.shape, dproj.dtype), jax.ShapeDtypeStruct((1, B), F32),
                   jax.ShapeDtypeStruct((1, B), F32), jax.ShapeDtypeStruct((G, CHUNK_B, CHUNK_B), F32),
                   jax.ShapeDtypeStruct((G, CHUNK_B, 1), F32)],
        input_output_aliases={8: 0},
        scratch_shapes=[pltpu.VMEM((G, CHUNK_B, CHUNK_B), F32)],
        compiler_params=_cparams(("arbitrary",)),
    )(proj_m, proj_m, proj_m, lw, lb, W, bbc, d_o, dproj)


def _head_fn(mix, x, fw, tgt):
    h = x + mix
    y = _rms_fn(h, fw)
    e = y - tgt
    return 0.5 * jnp.sum(jnp.mean(e * e, axis=-1, keepdims=True), axis=0, keepdims=True)


def _out_proj_loss(oa, ob, wout, x, tgt, fw):
    T, A = oa.shape
    B = ob.shape[1]
    D = x.shape[1]
    tm = _pick(T, (256, 128))

    def body(oa_ref, ob_ref, w_ref, x_ref, t_ref, fw_ref, dh_ref, dhb_ref, loss_ref, dfw_ref):
        mix = _dot(oa_ref[...], w_ref[0:A, :]) + _dot(ob_ref[...], w_ref[A:A + B, :])
        xv, tv = x_ref[...], t_ref[...]
        loss, vjp = jax.vjp(lambda m, f: _head_fn(m, xv, f, tv), mix, fw_ref[...])
        dh, dfw = vjp(jnp.ones((1, 1), F32))
        dh_ref[...] = dh
        dhb_ref[...] = dh.astype(BF16)
        lrow = jnp.broadcast_to(loss, (1, LANES))

        @pl.when(pl.program_id(0) == 0)
        def _():
            loss_ref[...] = lrow
            dfw_ref[...] = dfw

        @pl.when(pl.program_id(0) > 0)
        def _():
            loss_ref[...] += lrow
            dfw_ref[...] += dfw

    tile = pl.BlockSpec((tm, D), lambda i: (i, 0))
    return pl.pallas_call(
        body, name="out_proj_loss", grid=(T // tm,),
        in_specs=[pl.BlockSpec((tm, A), lambda i: (i, 0)), pl.BlockSpec((tm, B), lambda i: (i, 0)),
                  pl.BlockSpec((A + B, D), lambda i: (0, 0)), tile, tile,
                  pl.BlockSpec((1, D), lambda i: (0, 0))],
        out_specs=[tile, tile, pl.BlockSpec((1, LANES), lambda i: (0, 0)),
                   pl.BlockSpec((1, D), lambda i: (0, 0))],
        out_shape=[jax.ShapeDtypeStruct((T, D), F32), jax.ShapeDtypeStruct((T, D), BF16),
                   jax.ShapeDtypeStruct((1, LANES), F32), jax.ShapeDtypeStruct((1, D), F32)],
        compiler_params=_cparams(("arbitrary",)),
    )(oa, ob, wout, x, tgt, fw)


def _adamw(w, g, m, v, name):
    R, Cn = w.shape
    cap = max(8, 512 * 1024 // Cn)
    tr = max(t for t in range(8, min(R, cap) + 1, 8) if R % t == 0) if R > cap else R

    def body(w_ref, g_ref, m_ref, v_ref, d_ref, mo_ref, vo_ref):
        g = g_ref[...]
        m = ADAM_B1 * m_ref[...] + (1.0 - ADAM_B1) * g
        v = ADAM_B2 * v_ref[...] + (1.0 - ADAM_B2) * jnp.square(g)
        m_hat = m / (1.0 - ADAM_B1 ** ADAM_STEP)
        v_hat = v / (1.0 - ADAM_B2 ** ADAM_STEP)
        d_ref[...] = -ADAM_LR * (m_hat / (jnp.sqrt(v_hat) + ADAM_EPS) + ADAM_WD * w_ref[...])
        mo_ref[...] = m
        vo_ref[...] = v

    tile = pl.BlockSpec((tr, Cn), lambda i: (i, 0))
    shape = jax.ShapeDtypeStruct((R, Cn), F32)
    return pl.pallas_call(
        body, name=name, grid=(R // tr,), in_specs=[tile] * 4, out_specs=[tile] * 3,
        out_shape=[shape] * 3, compiler_params=_cparams(("parallel",)),
    )(w, g, m, v)


def _place():
    x, y, c = lax.axis_index("x"), lax.axis_index("y"), lax.axis_index("c")
    others = [(1 - x, y), (x, 1 - y), (1 - x, 1 - y)]
    return x, y, c, others


def _chip_index(px, py):
    return 2 * px + py


ANY = pl.BlockSpec(memory_space=pl.ANY)


def _gather_ride(blocks, split):
    n = len(blocks)

    def plan(in_refs, out_refs, send_sems, recv_sems):
        x, y, c, _ = _place()
        me, kx, ky, kd = (_chip_index(px, py) for px, py in ((x, y), (1 - x, y), (x, 1 - y), (1 - x, 1 - y)))
        to_x, to_y, to_s = (1 - x, y, c), (x, 1 - y, c), (x, y, 1 - c)

        def copy(sem, src, dst, to):
            return pltpu.make_async_remote_copy(src_ref=src, dst_ref=dst, send_sem=send_sems.at[sem],
                                                recv_sem=recv_sems.at[sem], device_id=to, device_id_type=MESH_ID)

        first, second, third, awaited = [], [], [], []
        for a in range(n):
            out, s0 = out_refs[a], 8 * a
            if not split[a]:
                for j, (k, to) in enumerate(((kx, to_x), (ky, to_y), (kd, (1 - x, 1 - y, c)))):
                    first.append(lambda j=j, to=to, a=a, out=out, s0=s0: copy(s0 + j, in_refs[a], out.at[me], to))
                    awaited.append((lambda j=j, k=k, to=to, out=out, s0=s0: copy(s0 + j, out.at[k], out.at[k], to),
                                    None))
                continue
            h = blocks[a].shape[0] // 2
            q = h // 2
            half = lambda k, core, out=out, h=h: out.at[k, pl.ds(core * h, h), :]
            quarter = lambda k, core, i, out=out, h=h, q=q: out.at[k, pl.ds(core * h + i * q, q), :]
            mine = in_refs[a].at[pl.ds(c * h, h), :]
            first.append(lambda s0=s0, mine=mine, half=half: copy(s0, mine, half(me, c), to_x))
            first.append(lambda s0=s0, mine=mine, half=half: copy(s0 + 1, mine, half(me, c), to_y))
            fwd0 = lambda s0=s0, quarter=quarter: copy(s0 + 2, quarter(kx, c, 0), quarter(kx, c, 0), to_y)
            fwd1 = lambda s0=s0, quarter=quarter: copy(s0 + 3, quarter(ky, c, 1), quarter(ky, c, 1), to_x)
            pieces = [(s0 + 0, lambda half=half: half(kx, c), lambda half=half: half(kx, 1 - c), to_x, fwd0),
                      (s0 + 1, lambda half=half: half(ky, c), lambda half=half: half(ky, 1 - c), to_y, fwd1),
                      (s0 + 2, lambda quarter=quarter: quarter(kd, c, 0), lambda quarter=quarter: quarter(kd, 1 - c, 0),
                       to_y, None),
                      (s0 + 3, lambda quarter=quarter: quarter(kd, c, 1), lambda quarter=quarter: quarter(kd, 1 - c, 1),
                       to_x, None)]
            for i, (sem, here, there, frm, fwd) in enumerate(pieces):
                passing = lambda s0=s0, i=i, here=here: copy(s0 + 4 + i, here(), here(), to_s)
                awaited.append((lambda sem=sem, here=here, frm=frm: copy(sem, here(), here(), frm), (fwd, passing)))
                if fwd is not None:
                    second.append(fwd)
                third.append((passing, lambda s0=s0, i=i, there=there: copy(s0 + 4 + i, there(), there(), to_s)))
        return first, second, third, awaited

    def start(*refs):
        for send in plan(*refs)[0]:
            send().start()

    def finish(*refs):
        first, second, third, awaited = plan(*refs)
        for arrival, then in awaited:
            arrival().wait_recv()
            for nxt in (then or ()):
                if nxt is not None:
                    nxt().start()
        for _, from_sibling in third:
            from_sibling().wait_recv()
        for send in first + second + [p for p, _ in third]:
            send().wait_send()

    shapes = [jax.ShapeDtypeStruct((N_CHIPS,) + b.shape, b.dtype) for b in blocks]
    return _Ride(blocks, shapes, 8 * n, start, finish)


def _put_own(gathered, own):
    me = _chip_index(lax.axis_index("x"), lax.axis_index("y"))
    return lax.dynamic_update_index_in_dim(gathered, own, me, 0)


def _allreduce_small(buf):
    R0, L = buf.shape
    R = -(-R0 // 16) * 16
    h = R // 2
    buf = jnp.pad(buf, ((0, R - R0), (0, 0)))

    def body(in_ref, out_ref, sib_ref, pair_ref, chips_ref, send_sems, recv_sems):
        x, y, c, others = _place()
        me = _chip_index(x, y)
        sibling = (x, y, 1 - c)

        def copy(sem, src, dst, to):
            return pltpu.make_async_remote_copy(src_ref=src, dst_ref=dst, send_sem=send_sems.at[sem],
                                                recv_sem=recv_sems.at[sem], device_id=to, device_id_type=MESH_ID)

        cp = copy(0, in_ref, sib_ref, sibling)
        cp.start()
        cp.wait()
        pair_ref[...] = in_ref[...] + sib_ref[...]
        rows = lambda core: pl.ds(pl.multiple_of(core * h, 8), h)
        sends = [copy(1 + j, pair_ref.at[rows(c), :], chips_ref.at[me], (*chip, c)) for j, chip in enumerate(others)]
        for s in sends:
            s.start()
        chips_ref[me] = pair_ref[rows(c), :]
        for j, chip in enumerate(others):
            k = _chip_index(*chip)
            copy(1 + j, chips_ref.at[k], chips_ref.at[k], (*chip, c)).wait_recv()
        out_ref[rows(c), :] = ((chips_ref[0] + chips_ref[1]) + chips_ref[2]) + chips_ref[3]
        swap = copy(4, out_ref.at[rows(c), :], out_ref.at[rows(c), :], sibling)
        swap.start()
        copy(4, out_ref.at[rows(1 - c), :], out_ref.at[rows(1 - c), :], sibling).wait_recv()
        for s in sends + [swap]:
            s.wait_send()

    vm = pl.BlockSpec(memory_space=pltpu.VMEM)
    return pl.pallas_call(
        body, name="allreduce_small", in_specs=[vm], out_specs=vm,
        out_shape=jax.ShapeDtypeStruct((R, L), F32),
        scratch_shapes=[pltpu.VMEM((R, L), F32), pltpu.VMEM((R, L), F32), pltpu.VMEM((N_CHIPS, h, L), F32),
                        pltpu.SemaphoreType.DMA((5,)), pltpu.SemaphoreType.DMA((5,))],
        compiler_params=pltpu.CompilerParams(vmem_limit_bytes=VMEM_LIMIT),
    )(buf)[:R0]


def _pair_ride(g):
    nb, R, Cn = g.shape
    h = R // 2

    def copy(in_refs, out_refs, send_sems, recv_sems):
        x, y, c, _ = _place()
        return pltpu.make_async_remote_copy(src_ref=in_refs[0].at[:, pl.ds((1 - c) * h, h), :], dst_ref=out_refs[0],
                                            send_sem=send_sems.at[0], recv_sem=recv_sems.at[0],
                                            device_id=(x, y, 1 - c), device_id_type=MESH_ID)

    return _Ride([g], [jax.ShapeDtypeStruct((nb, h, Cn), g.dtype)], 1,
                 lambda *refs: copy(*refs).start(), lambda *refs: copy(*refs).wait())


def _pair_sum(g, land, c_arr, name, ride=None):
    nb, R, Cn = g.shape
    hr = R // 2
    tr = _pick(hr, (256, 128, 64, 32, 16))
    nt = hr // tr

    def body(c_ref, g_ref, l_ref, o_ref):
        o_ref[...] = (g_ref[...].astype(F32) + l_ref[...].astype(F32)).astype(BF16)

    return _pallas(
        body, (c_arr, g, land), name=name, prefetch=1, grid=(nb, nt),
        in_specs=[pl.BlockSpec((1, tr, Cn), lambda b, i, c_ref: (b, c_ref[0] * nt + i, 0)),
                  pl.BlockSpec((1, tr, Cn), lambda b, i, c_ref: (b, i, 0))],
        out_specs=pl.BlockSpec((1, tr, Cn), lambda b, i, c_ref: (b, i, 0)),
        out_shape=jax.ShapeDtypeStruct((nb, hr, Cn), BF16),
        semantics=("parallel", "parallel"), ride=ride)


def _chip_ride(parts, cols=None):
    m = len(parts)

    def copies(in_refs, out_refs, send_sems, recv_sems):
        x, y, c, others = _place()
        me = _chip_index(x, y)
        def mk(j, chip, n, landing):
            k = _chip_index(*chip)
            src = in_refs[n].at[k]
            if cols is not None:
                src = src.at[:, pl.ds(pl.multiple_of(cols[0](k), LANES), cols[1])]
            return pltpu.make_async_remote_copy(
                src_ref=src, dst_ref=out_refs[n].at[landing(k)], send_sem=send_sems.at[m * j + n],
                recv_sem=recv_sems.at[m * j + n], device_id=(*chip, c), device_id_type=MESH_ID)

        pairs = [(j, chip, n) for j, chip in enumerate(others) for n in range(m)]
        return pairs, (lambda *p: mk(*p, lambda k: me)), (lambda *p: mk(*p, lambda k: k))

    def start(*refs):
        pairs, send, _ = copies(*refs)
        for p in pairs:
            send(*p).start()

    def finish(*refs):
        pairs, send, arrival = copies(*refs)
        for p in pairs:
            arrival(*p).wait_recv()
        for p in pairs:
            send(*p).wait_send()

    width = lambda p: p.shape[2] if cols is None else cols[1]
    return _Ride(parts, [jax.ShapeDtypeStruct(p.shape[:2] + (width(p),), p.dtype) for p in parts], 3 * m,
                 start, finish)


def _put_own_slot(q, p, cols=None):
    me = _chip_index(lax.axis_index("x"), lax.axis_index("y"))
    own = lax.dynamic_index_in_dim(p, me, 0, keepdims=False)
    if cols is not None:
        own = lax.dynamic_slice_in_dim(own, cols[0](me), cols[1], axis=1)
    return lax.dynamic_update_index_in_dim(q, own, me, 0)


def _chip_sum(q, c_arr, name):
    nb, hr, Cn = q.shape
    tr = _pick(hr, (256, 128, 64, 32, 16))
    nt = hr // tr

    def body(c_ref, q_ref, o_ref):
        f = lambda k: q_ref[k].astype(F32)
        o_ref[...] = ((f(0) + f(1)) + f(2)) + f(3)

    return _pallas(
        body, (c_arr, q), name=name, prefetch=1, grid=(nt,),
        in_specs=[pl.BlockSpec((nb, tr, Cn), lambda i, c_ref: (0, i, 0))],
        out_specs=pl.BlockSpec((tr, Cn), lambda i, c_ref: (c_ref[0] * nt + i, 0)),
        out_shape=jax.ShapeDtypeStruct((2 * hr, Cn), F32),
        semantics=("parallel",))


def _sibling_fill(fw, fo):
    def body(_, __, fw_ref, fo_ref, send_sems, recv_sems):
        x, y, c, _ = _place()
        copies = []
        for n, ref in enumerate((fw_ref, fo_ref)):
            h = ref.shape[0] // 2
            mine = ref.at[pl.ds(c * h, h), :]
            theirs = ref.at[pl.ds((1 - c) * h, h), :]
            mk = lambda src, dst: pltpu.make_async_remote_copy(
                src_ref=src, dst_ref=dst, send_sem=send_sems.at[n], recv_sem=recv_sems.at[n],
                device_id=(x, y, 1 - c), device_id_type=MESH_ID)
            send = mk(mine, mine)
            send.start()
            copies.append((send, mk(theirs, theirs)))
        for send, arrival in copies:
            arrival.wait_recv()
            send.wait_send()

    return pl.pallas_call(
        body, name="sibling_fill", in_specs=[ANY, ANY], out_specs=[ANY, ANY],
        out_shape=[jax.ShapeDtypeStruct(fw.shape, F32), jax.ShapeDtypeStruct(fo.shape, F32)],
        input_output_aliases={0: 0, 1: 1},
        scratch_shapes=[pltpu.SemaphoreType.DMA((2,)), pltpu.SemaphoreType.DMA((2,))],
        compiler_params=pltpu.CompilerParams(has_side_effects=True),
    )(fw, fo)


class _Layout:
    def __init__(self, H, G, nb, Cb):
        A, B = H * HEAD_DIM, G * HEAD_DIM
        self.n_main = 4 * A + 3 * B
        self.k = -(-(self.n_main + LANES) // WIN_BLOCK) * WIN_BLOCK
        cuts = [0, 3 * A, 4 * A, 4 * A + 2 * H, nb * Cb]
        starts = [0, 3 * A + 3 * B, self.n_main, 3 * A]
        self.pieces = []
        self.windows, self.runs = [], []
        for n in range(nb):
            segs = []
            for s in range(4):
                lo, hi = max(cuts[s], n * Cb), min(cuts[s + 1], (n + 1) * Cb)
                if lo < hi:
                    segs.append((starts[s] + lo - cuts[s], lo - n * Cb, hi - lo))
            self.pieces += [(own, n, col, ln) for own, col, ln in segs]
            blocks = sorted({b for own, _, ln in segs for b in range(own // WIN_BLOCK, (own + ln - 1) // WIN_BLOCK + 1)})
            self.windows.append(blocks)
            self.runs.append([(blocks.index(own // WIN_BLOCK) * WIN_BLOCK + own % WIN_BLOCK, ln)
                              for own, _, ln in segs])
        self.wb = max(len(b) for b in self.windows)
        self.table = [b + [b[-1]] * (self.wb - len(b)) for b in self.windows]
        self.pieces.sort()
        self.used_from = [min(c for c, _ in r) // LANES * LANES for r in self.runs]
        self.used = max(-(-max(c + ln for c, ln in r) // LANES) * LANES - f for r, f in zip(self.runs, self.used_from))
        self.used_from = [min(f, self.wb * WIN_BLOCK - self.used) for f in self.used_from]

    def to_own_order(self, g_in):
        nb, D, Cb = g_in.shape
        tr = _pick(D, (256, 128))

        def body(g_ref, o_ref):
            cols, at = [], 0
            for own, n, col, ln in self.pieces:
                if own > at:
                    cols.append(jnp.zeros((tr, own - at), g_in.dtype))
                cols.append(g_ref[n, :, col:col + ln])
                at = own + ln
            if at < self.k:
                cols.append(jnp.zeros((tr, self.k - at), g_in.dtype))
            o_ref[...] = jnp.concatenate(cols, axis=1)

        return pl.pallas_call(
            body, name="own_order", grid=(D // tr,),
            in_specs=[pl.BlockSpec((nb, tr, Cb), lambda i: (0, i, 0))],
            out_specs=pl.BlockSpec((tr, self.k), lambda i: (i, 0)),
            out_shape=jax.ShapeDtypeStruct((D, self.k), g_in.dtype),
            compiler_params=_cparams(("parallel",)),
        )(g_in)

    def from_window(self, win, chip, Cb):
        pick = lambda runs, f: (lambda w: jnp.concatenate([w[:, c - f:c - f + ln] for c, ln in runs], axis=1))
        return lax.switch(chip, [pick(r, f) for r, f in zip(self.runs, self.used_from)], win)

    def used_start(self, chip):
        return sum(jnp.where(chip == n, f, 0) for n, f in enumerate(self.used_from))


def _device_step(x, tgt, norm_w, win_b, wout_b, conv_b, a_log, dt_bias, head_norm_w, sgu_ln_w, sgu_ln_b,
                 w_spatial, b_spatial, final_norm_w, c_arr):
    T, D = x.shape
    H = a_log.shape[1]
    A = H * HEAD_DIM
    G = w_spatial.shape[0]
    B = G * HEAD_DIM
    nb, Cb, Rb = N_CHIPS, win_b.shape[1], wout_b.shape[0]
    lay = _Layout(H, G, nb, Cb)
    alog_row = jnp.pad(a_log, ((0, 0), (H, LANES - 2 * H)))
    dtb_row = jnp.pad(dt_bias, ((0, 0), (H, LANES - 2 * H)))
    bbc = jnp.broadcast_to(b_spatial[:, :, None], (G, CHUNK_B, CHUNK_B))

    (xn, xn_t), (g_in,) = _rms_in(x, norm_w, ride=_gather_ride([win_b], [True]))
    w_own = lay.to_own_order(_put_own(g_in, win_b))
    proj_m, (g_out, g_conv) = _mm_nn(xn, w_own, F32, "in_proj", tm=2048, cols=(0, lay.n_main),
                                     ride=_gather_ride([wout_b, conv_b], [False, False]))
    wout = _put_own(g_out, wout_b).reshape(nb * Rb, D)
    conv_w = _put_own(g_conv, conv_b).transpose(1, 0, 2).reshape(CONV_WIDTH, nb * conv_b.shape[1])
    q, k, v, gb, bb, proj_ba = _gdn_pre(proj_m, xn, w_own, conv_w, alog_row, dtb_row, H)
    u, w, qg, kd, attn, eg, pinv = _gdn_prep(q, k, v, gb, bb)
    og, sall = _gdn_chain(qg, kd, u, w, attn, eg)
    oa, oa_t = _gdn_post(og, proj_m, head_norm_w)
    ob, ob_t = _sgu_fwd(proj_m, sgu_ln_w, sgu_ln_b, w_spatial, bbc, A)
    dh, dhb, loss_row, d_fnw = _out_proj_loss(oa, ob, wout, x, tgt, final_norm_w.reshape(1, D))

    d_o = _mm_nn(dhb, wout.T, F32, "out_proj_dx", tm=2048)
    dproj = lax.empty((T, lay.k), BF16)
    dproj, d_lw, d_lb, d_ws, d_bs = _sgu_bwd(proj_m, sgu_ln_w, sgu_ln_b, w_spatial, bbc, d_o, A, dproj)
    dog, dproj, d_hw = _gdn_post_bwd(og, proj_m, head_norm_w, d_o, dproj)
    dqg, dkd, du, dw, dat, deg = _gdn_chain_bwd(qg, kd, u, w, attn, eg, sall, dog)
    dq, dk, dv, dgb, dbb = _gdn_prep_bwd(q, k, v, gb, bb, pinv, du, dw, dqg, dkd, dat, deg)
    dc, dproj, d_al, d_dt = _gdn_pre_bwd(proj_m, proj_ba, conv_w, alog_row, dtb_row, dq, dk, dv, dgb, dbb, H,
                                         dproj)
    dproj, d_conv = _conv_bwd(proj_m, dc, conv_w, H, dproj)

    table = jnp.array([b for row in lay.table for b in row], jnp.int32)
    d_win = _mm_windows(xn_t, dproj, table, nb, "in_proj_dw")
    d_wout, (land_w,) = _mm_nn_pair(oa_t, ob_t, dhb, "out_proj_dw", ride=_pair_ride(d_win))
    d_wout = d_wout.reshape(nb, Rb, D)
    pair_w, (land_o,) = _pair_sum(d_win, land_w, c_arr, "pair_sum_w_in", ride=_pair_ride(d_wout))
    pair_o = _pair_sum(d_wout, land_o, c_arr, "pair_sum_w_out")
    used = (lay.used_start, lay.used)
    dxn, (all_w,) = _mm_nt_rhs_outer(dproj, w_own, F32, "in_proj_dx", ride=_chip_ride([pair_w], used))
    (grad_x, d_nw), (all_o,) = _rms_in_bwd(x, norm_w, dxn, dh, ride=_chip_ride([pair_o]))
    all_w, all_o = _put_own_slot(all_w, pair_w, used), _put_own_slot(all_o, pair_o)
    small = dict(norm_w=d_nw, conv_w=d_conv[:CONV_WIDTH], a_log=d_al[:, H:2 * H], dt_bias=d_dt[:, H:2 * H],
                 head_norm_w=d_hw, sgu_ln_w=d_lw, sgu_ln_b=d_lb, w_spatial=d_ws, b_spatial=d_bs[:, :, 0],
                 final_norm_w=d_fnw)
    return loss_row, grad_x, small, all_w, all_o


SMALL = ("norm_w", "conv_w", "a_log", "dt_bias", "head_norm_w", "sgu_ln_w", "sgu_ln_b", "w_spatial",
         "b_spatial", "final_norm_w")


def _pack(parts):
    rows = []
    for p in parts:
        f = p.reshape(-1)
        f = jnp.pad(f, (0, (-f.shape[0]) % (8 * LANES)))
        rows.append(f.reshape(-1, LANES))
    return jnp.concatenate(rows, axis=0)


def _unpack(buf, shapes):
    out, r = [], 0
    for s in shapes:
        n = 1
        for d in s:
            n *= d
        nr = -(-n // (8 * LANES)) * 8
        out.append(buf[r:r + nr].reshape(-1)[:n].reshape(s))
        r += nr
    return out


def kernel(x, norm_w, w_in, conv_w, a_log, dt_bias, head_norm_w, sgu_ln_w, sgu_ln_b, w_spatial, b_spatial, w_out, final_norm_w, loss_target, m_norm_w, m_w_in, m_conv_w, m_a_log, m_dt_bias, m_head_norm_w, m_sgu_ln_w, m_sgu_ln_b, m_w_spatial, m_b_spatial, m_w_out, m_final_norm_w, v_norm_w, v_w_in, v_conv_w, v_a_log, v_dt_bias, v_head_norm_w, v_sgu_ln_w, v_sgu_ln_b, v_w_spatial, v_b_spatial, v_w_out, v_final_norm_w):
    T, D = x.shape[1], x.shape[2]
    weights = dict(norm_w=norm_w, w_in=w_in, conv_w=conv_w, a_log=a_log, dt_bias=dt_bias, head_norm_w=head_norm_w,
                   sgu_ln_w=sgu_ln_w, sgu_ln_b=sgu_ln_b, w_spatial=w_spatial, b_spatial=b_spatial, w_out=w_out,
                   final_norm_w=final_norm_w)
    mom_m = dict(norm_w=m_norm_w, w_in=m_w_in, conv_w=m_conv_w, a_log=m_a_log, dt_bias=m_dt_bias,
                 head_norm_w=m_head_norm_w, sgu_ln_w=m_sgu_ln_w, sgu_ln_b=m_sgu_ln_b, w_spatial=m_w_spatial,
                 b_spatial=m_b_spatial, w_out=m_w_out, final_norm_w=m_final_norm_w)
    mom_v = dict(norm_w=v_norm_w, w_in=v_w_in, conv_w=v_conv_w, a_log=v_a_log, dt_bias=v_dt_bias,
                 head_norm_w=v_head_norm_w, sgu_ln_w=v_sgu_ln_w, sgu_ln_b=v_sgu_ln_b, w_spatial=v_w_spatial,
                 b_spatial=v_b_spatial, w_out=v_w_out, final_norm_w=v_final_norm_w)
    me = _chip_index(lax.axis_index("x"), lax.axis_index("y"))
    c_arr = lax.axis_index("c").astype(jnp.int32).reshape(1)
    Din, Cb = w_in.shape[1], w_in.shape[2]
    Rb = w_out.shape[1]
    cconv = conv_w.shape[2]

    loss_row, grad_x, g, qw, qo = _device_step(
        x[0], loss_target[0], norm_w, w_in[0].astype(BF16), w_out[0].astype(BF16), conv_w[0], a_log, dt_bias,
        head_norm_w, sgu_ln_w, sgu_ln_b, w_spatial[0], b_spatial[0], final_norm_w, c_arr)

    small_shapes = [tuple(g[n].shape) for n in SMALL] + [(1, LANES)]
    small = _allreduce_small(_pack([g[n] for n in SMALL] + [loss_row]))
    gsum_in, gsum_out = _sibling_fill(_chip_sum(qw, c_arr, "chip_sum_w_in"), _chip_sum(qo, c_arr, "chip_sum_w_out"))
    gsum_in = _Layout(a_log.shape[1], w_spatial.shape[1], N_CHIPS, Cb).from_window(gsum_in, me, Cb)
    *small, loss_sum = _unpack(small, small_shapes)
    gsmall = dict(zip(SMALL, small))
    gsmall["conv_w"] = lax.dynamic_slice_in_dim(gsmall["conv_w"], me * cconv, cconv, axis=1)

    grads, deltas, new_m, new_v = {}, {}, {}, {}
    d, m2, v2 = _adamw(w_out[0], gsum_out, m_w_out[0], v_w_out[0], "adamw_w_out")
    grads["w_out"], deltas["w_out"], new_m["w_out"], new_v["w_out"] = gsum_out[None], d[None], m2[None], v2[None]
    flat = lambda a: a.transpose(2, 0, 1).reshape(-1, LANES)
    unflat = lambda f: f.reshape(Cb, 1, Din).transpose(1, 2, 0)
    g_flat = gsum_in.T.reshape(-1, LANES)
    d, m2, v2 = _adamw(flat(w_in), g_flat, flat(m_w_in), flat(v_w_in), "adamw_w_in")
    grads["w_in"], deltas["w_in"], new_m["w_in"], new_v["w_in"] = unflat(g_flat), unflat(d), unflat(m2), unflat(v2)
    shapes = [tuple(weights[n].shape) for n in SMALL]
    ds, ms, vs = _adamw(_pack([weights[n] for n in SMALL]), _pack([gsmall[n] for n in SMALL]),
                        _pack([mom_m[n] for n in SMALL]), _pack([mom_v[n] for n in SMALL]), "adamw_small")
    for n, gq, d, m2, v2 in zip(SMALL, [gsmall[n] for n in SMALL], _unpack(ds, shapes), _unpack(ms, shapes),
                                _unpack(vs, shapes)):
        grads[n], deltas[n], new_m[n], new_v[n] = gq.reshape(weights[n].shape), d, m2, v2

    loss = loss_sum[0, 0]
    order = ("norm_w", "w_in", "conv_w", "a_log", "dt_bias", "head_norm_w", "sgu_ln_w", "sgu_ln_b", "w_spatial",
             "b_spatial", "w_out", "final_norm_w")
    return (loss, grad_x[None], *[grads[n] for n in order], *[deltas[n] for n in order],
            *[new_m[n] for n in order], *[new_v[n] for n in order])
```

```python
import functools

import jax
import jax.numpy as jnp
from jax import lax
from jax.experimental import pallas as pl
from jax.experimental.pallas import tpu as pltpu

F32 = jnp.float32
BF16 = jnp.bfloat16
EPS = 1e-6
HEAD_DIM = 128
CHUNK_B = 128
CONV_WIDTH = 4
LANES = 128
HALO = 8
N_CHIPS = 4
ADAM_LR = 0.001
ADAM_B1 = 0.9
ADAM_B2 = 0.999
ADAM_EPS = 1e-08
ADAM_WD = 0.01
ADAM_STEP = 10
VMEM_LIMIT = 56 * 1024 * 1024
MESH_ID = pl.DeviceIdType.MESH


def _cparams(sem=None, **kw):
    return pltpu.CompilerParams(dimension_semantics=sem, vmem_limit_bytes=VMEM_LIMIT, **kw)


def _matmul(a, b, ca, cb):
    nb = a.ndim - 2
    batch = tuple(range(nb))
    return lax.dot_general(a, b, (((ca + nb,), (cb + nb,)), (batch, batch)), preferred_element_type=F32)


def _dot(a, b):
    return _matmul(a, b, 1, 0)


def _dot_nt(a, b):
    return _matmul(a, b, 1, 1)


def _dot_tn(a, b):
    return _matmul(a, b, 0, 0)


def _iota(shape, dim):
    return lax.broadcasted_iota(jnp.int32, shape, dim)


def _sigmoid(x):
    return 0.5 * (jnp.tanh(0.5 * x) + 1.0)


def _silu(x):
    return x * _sigmoid(x)


def _softplus(x):
    z = jnp.exp(-jnp.abs(x))
    small = z * (1.0 - z * (0.5 - z * (1.0 / 3.0)))
    return jnp.maximum(x, 0.0) + jnp.where(z < 1e-3, small, jnp.log(1.0 + z))


def _pick(n, pref):
    for t in pref:
        if n % t == 0:
            return t
    return n


class _Ride:
    def __init__(self, operands, out_shape, n_sems, start, finish):
        self.operands, self.out_shape, self.n_sems = list(operands), list(out_shape), n_sems
        self.start, self.finish = start, finish


def _pallas(body, operands, *, name, grid, in_specs, out_specs, out_shape, semantics, scratch_shapes=(),
            prefetch=0, ride=None):
    single = not isinstance(out_shape, (list, tuple))
    outs = [out_shape] if single else list(out_shape)
    ospecs = [out_specs] if single else list(out_specs)
    in_specs, scratch = list(in_specs), list(scratch_shapes)
    n_in, n_out, n_sc = len(operands) - prefetch, len(outs), len(scratch)
    kernel = body
    params = _cparams(semantics)
    if ride is not None:
        n_xin, n_xout = len(ride.operands), len(ride.out_shape)

        def kernel(*refs):
            pre, refs = refs[:prefetch], refs[prefetch:]
            ins, refs = refs[:n_in], refs[n_in:]
            xins, refs = refs[:n_xin], refs[n_xin:]
            mains, refs = refs[:n_out], refs[n_out:]
            xouts, refs = refs[:n_xout], refs[n_xout:]
            sc, (send, recv) = refs[:n_sc], refs[n_sc:]
            ids = [pl.program_id(a) for a in range(len(grid))]
            first = functools.reduce(jnp.logical_and, [i == 0 for i in ids])
            last = functools.reduce(jnp.logical_and, [i == g - 1 for i, g in zip(ids, grid)])

            @pl.when(first)
            def _():
                ride.start(xins, xouts, send, recv)

            body(*pre, *ins, *mains, *sc)

            @pl.when(last)
            def _():
                ride.finish(xins, xouts, send, recv)

        operands = list(operands) + ride.operands
        in_specs += [ANY] * n_xin
        ospecs += [ANY] * n_xout
        outs += ride.out_shape
        scratch += [pltpu.SemaphoreType.DMA((ride.n_sems,)), pltpu.SemaphoreType.DMA((ride.n_sems,))]
        params = _cparams(("arbitrary",) * len(grid), has_side_effects=True)
    if prefetch:
        spec = dict(grid_spec=pltpu.PrefetchScalarGridSpec(
            num_scalar_prefetch=prefetch, grid=grid, in_specs=in_specs, out_specs=ospecs, scratch_shapes=scratch))
    else:
        spec = dict(grid=grid, in_specs=in_specs, out_specs=ospecs, scratch_shapes=scratch)
    res = pl.pallas_call(kernel, name=name, out_shape=outs, compiler_params=params, **spec)(*operands)
    main = res[0] if single else list(res[:n_out])
    return main if ride is None else (main, list(res[n_out:]))


def _mm_nn(a, b, out_dtype, name, tm=1024, tn=512, tk=None, cols=None, ride=None):
    M, K = a.shape
    c0, N = (0, b.shape[1]) if cols is None else cols
    tm = _pick(M, (tm, 1024, 512, 256, 128))
    tn = _pick(N, (tn, 512, 384, 256, 128))
    tk = K if tk is None else _pick(K, (tk,))
    nk = K // tk
    j0 = c0 // tn
    assert c0 % tn == 0

    def body(a_ref, b_ref, o_ref, *scratch):
        part = _dot(a_ref[...], b_ref[...])
        if nk == 1:
            o_ref[...] = part.astype(out_dtype)
        else:
            acc_ref, = scratch
            k = pl.program_id(2)

            @pl.when(k == 0)
            def _():
                acc_ref[...] = part

            @pl.when(k > 0)
            def _():
                acc_ref[...] += part

            @pl.when(k == nk - 1)
            def _():
                o_ref[...] = acc_ref[...].astype(out_dtype)

    return _pallas(
        body, (a, b), name=name, grid=(M // tm, N // tn, nk),
        in_specs=[pl.BlockSpec((tm, tk), lambda i, j, k: (i, k)),
                  pl.BlockSpec((tk, tn), lambda i, j, k: (k, j + j0))],
        out_specs=pl.BlockSpec((tm, tn), lambda i, j, k: (i, j)),
        out_shape=jax.ShapeDtypeStruct((M, N), out_dtype),
        scratch_shapes=[] if nk == 1 else [pltpu.VMEM((tm, tn), F32)],
        semantics=("parallel", "parallel", "arbitrary"), ride=ride)


def _mm_nt_rhs_outer(a, b, out_dtype, name, tm=256, tn=1024, ride=None):
    M, K = a.shape
    N, _ = b.shape
    tm = _pick(M, (tm, 128))
    tn = _pick(N, (tn, 512, 256, 128))

    def body(a_ref, b_ref, o_ref):
        o_ref[...] = _dot_nt(a_ref[...], b_ref[...]).astype(out_dtype)

    return _pallas(
        body, (a, b), name=name, grid=(N // tn, M // tm),
        in_specs=[pl.BlockSpec((tm, K), lambda j, i: (i, 0)),
                  pl.BlockSpec((tn, K), lambda j, i: (j, 0))],
        out_specs=pl.BlockSpec((tm, tn), lambda j, i: (i, j)),
        out_shape=jax.ShapeDtypeStruct((M, N), out_dtype),
        semantics=("parallel", "parallel"), ride=ride)


WIN_BLOCK = 256


def _mm_windows(a, b, table, nb, name, tm=2048):
    M, K = a.shape
    wb = table.shape[0] // nb
    tm = _pick(M, (tm, 1024, 512, 256, 128))

    def body(tab_ref, a_ref, b_ref, o_ref):
        o_ref[0] = _dot(a_ref[...], b_ref[...]).astype(BF16)

    return pl.pallas_call(
        body, name=name,
        grid_spec=pltpu.PrefetchScalarGridSpec(
            num_scalar_prefetch=1, grid=(nb, M // tm, wb),
            in_specs=[pl.BlockSpec((tm, K), lambda n, i, t, tab: (i, 0)),
                      pl.BlockSpec((K, WIN_BLOCK), lambda n, i, t, tab: (0, tab[n * wb + t]))],
            out_specs=pl.BlockSpec((1, tm, WIN_BLOCK), lambda n, i, t, tab: (n, i, t))),
        out_shape=jax.ShapeDtypeStruct((nb, M, wb * WIN_BLOCK), BF16),
        compiler_params=_cparams(("parallel", "parallel", "arbitrary")),
    )(table, a, b)


def _mm_nn_pair(a0, a1, b, name, tm=512, tn=1024, ride=None):
    M, K = a0.shape
    _, N = b.shape
    tm = _pick(M, (tm, 256, 128))
    tn = _pick(N, (tn, 512, 256, 128))
    ni = M // tm

    def body(a0_ref, a1_ref, b_ref, o_ref):
        p = pl.program_id(0)

        @pl.when(p == 0)
        def _():
            o_ref[...] = _dot(a0_ref[...], b_ref[...]).astype(BF16)

        @pl.when(p == 1)
        def _():
            o_ref[...] = _dot(a1_ref[...], b_ref[...]).astype(BF16)

    return _pallas(
        body, (a0, a1, b), name=name, grid=(2, ni, N // tn),
        in_specs=[pl.BlockSpec((tm, K), lambda p, i, j: (i * (1 - p), 0)),
                  pl.BlockSpec((tm, K), lambda p, i, j: (i * p, 0)),
                  pl.BlockSpec((K, tn), lambda p, i, j: (0, j))],
        out_specs=pl.BlockSpec((tm, tn), lambda p, i, j: (p * ni + i, j)),
        out_shape=jax.ShapeDtypeStruct((2 * M, N), BF16),
        semantics=("parallel", "parallel", "parallel"), ride=ride)


def _rms_fn(x, w):
    r = lax.rsqrt(jnp.mean(x * x, axis=-1, keepdims=True) + EPS)
    return x * r * w


def _rms_in(x, w, ride=None):
    T, D = x.shape
    tm = _pick(T, (512, 256, 128))

    def body(x_ref, w_ref, o_ref, ot_ref):
        xn = _rms_fn(x_ref[...], w_ref[...])
        o_ref[...] = xn.astype(BF16)
        ot_ref[...] = xn.T.astype(BF16)

    return _pallas(
        body, (x, w), name="rms_in", grid=(T // tm,),
        in_specs=[pl.BlockSpec((tm, D), lambda i: (i, 0)), pl.BlockSpec((1, D), lambda i: (0, 0))],
        out_specs=[pl.BlockSpec((tm, D), lambda i: (i, 0)), pl.BlockSpec((D, tm), lambda i: (0, i))],
        out_shape=[jax.ShapeDtypeStruct((T, D), BF16), jax.ShapeDtypeStruct((D, T), BF16)],
        semantics=("parallel",), ride=ride)


def _rms_in_bwd(x, w, dxn, dh, ride=None):
    T, D = x.shape
    tm = _pick(T, (256, 128))

    def body(x_ref, w_ref, dxn_ref, dh_ref, gx_ref, dw_ref):
        _, vjp = jax.vjp(_rms_fn, x_ref[...], w_ref[...])
        dx, dw = vjp(dxn_ref[...])
        gx_ref[...] = dh_ref[...] + dx

        @pl.when(pl.program_id(0) == 0)
        def _():
            dw_ref[...] = dw

        @pl.when(pl.program_id(0) > 0)
        def _():
            dw_ref[...] += dw

    tile = pl.BlockSpec((tm, D), lambda i: (i, 0))
    row = pl.BlockSpec((1, D), lambda i: (0, 0))
    return _pallas(
        body, (x, w, dxn, dh), name="rms_in_bwd", grid=(T // tm,),
        in_specs=[tile, row, tile, tile], out_specs=[tile, row],
        out_shape=[jax.ShapeDtypeStruct((T, D), F32), jax.ShapeDtypeStruct((1, D), F32)],
        semantics=("arbitrary",), ride=ride)


def _conv_fwd(cat_ref, halo, x, w):
    tm = x.shape[0]
    cat_ref[0:HALO, :] = halo
    cat_ref[HALO:HALO + tm, :] = x
    c = x * w[CONV_WIDTH - 1:CONV_WIDTH, :]
    for k in range(CONV_WIDTH - 1):
        s = CONV_WIDTH - 1 - k
        c = c + cat_ref[pl.ds(HALO - s, tm), :] * w[k:k + 1, :]
    return c


def _lane_to_all(x, lane):
    @jax.custom_vjp
    def f(x):
        return jnp.broadcast_to(x[:, lane:lane + 1], x.shape)

    def f_fwd(x):
        return f(x), None

    def f_bwd(_, g):
        return (jnp.where(_iota(g.shape, 1) == lane, jnp.sum(g, axis=-1, keepdims=True), 0.0),)

    f.defvjp(f_fwd, f_bwd)
    return f(x)


def _gdn_pointwise(c, ba, alog, dtb, H):
    A = H * HEAD_DIM
    s = _silu(c)
    beta = _sigmoid(ba)
    g = -jnp.exp(alog) * _softplus(ba + dtb)
    qs, ks, vs, gbs, bbs = [], [], [], [], []
    for h in range(H):
        lo = h * HEAD_DIM
        q = s[:, lo:lo + HEAD_DIM]
        k = s[:, A + lo:A + lo + HEAD_DIM]
        qs.append(q * lax.rsqrt(jnp.sum(q * q, axis=-1, keepdims=True) + EPS))
        ks.append(k * lax.rsqrt(jnp.sum(k * k, axis=-1, keepdims=True) + EPS))
        vs.append(s[:, 2 * A + lo:2 * A + lo + HEAD_DIM])
        bbs.append(_lane_to_all(beta, h))
        gbs.append(_lane_to_all(g, H + h))
    st = lambda xs: jnp.stack(xs, axis=0)
    return st(qs), st(ks), st(vs), st(gbs), st(bbs)


def _halo_prev(tm):
    return lambda i: (jnp.maximum(i * (tm // HALO) - 1, 0), 0)


def _gdn_pre(proj_m, xn, w_own, conv_w, alog_row, dtb_row, H):
    T, n_main = proj_m.shape
    D = xn.shape[1]
    A = H * HEAD_DIM
    tm = _pick(T, (256, 128))
    hs = pl.BlockSpec((H, tm, HEAD_DIM), lambda i: (0, i, 0))
    hshape = jax.ShapeDtypeStruct((H, T, HEAD_DIM), F32)

    def body(x_ref, halo_ref, xn_ref, wba_ref, w_ref, al_ref, dt_ref,
             q_ref, k_ref, v_ref, gb_ref, bb_ref, ba_ref, cat_ref):
        halo = jnp.where(pl.program_id(0) == 0, 0.0, halo_ref[...])
        c = _conv_fwd(cat_ref, halo, x_ref[...], w_ref[...])
        ba = _dot(xn_ref[...], wba_ref[...])
        q, k, v, gb, bb = _gdn_pointwise(c, ba, al_ref[...], dt_ref[...], H)
        q_ref[...] = q
        k_ref[...] = k
        v_ref[...] = v
        gb_ref[...] = gb
        bb_ref[...] = bb
        ba_ref[...] = ba

    return pl.pallas_call(
        body, name="gdn_pre", grid=(T // tm,),
        in_specs=[pl.BlockSpec((tm, 3 * A), lambda i: (i, 0)),
                  pl.BlockSpec((HALO, 3 * A), _halo_prev(tm)),
                  pl.BlockSpec((tm, D), lambda i: (i, 0)),
                  pl.BlockSpec((D, LANES), lambda i: (0, n_main // LANES)),
                  pl.BlockSpec((CONV_WIDTH, 3 * A), lambda i: (0, 0)),
                  pl.BlockSpec((1, LANES), lambda i: (0, 0)),
                  pl.BlockSpec((1, LANES), lambda i: (0, 0))],
        out_specs=[hs] * 5 + [pl.BlockSpec((tm, LANES), lambda i: (i, 0))],
        out_shape=[hshape] * 5 + [jax.ShapeDtypeStruct((T, LANES), F32)],
        scratch_shapes=[pltpu.VMEM((HALO + tm, 3 * A), F32)],
        compiler_params=_cparams(("parallel",)),
    )(proj_m, proj_m, xn, w_own, conv_w, alog_row, dtb_row)


def _gdn_pre_bwd(proj_m, proj_ba, conv_w, alog_row, dtb_row, dq, dk, dv, dgb, dbb, H, dproj):
    T, n_main = proj_m.shape
    A = H * HEAD_DIM
    tm = _pick(T, (256, 128))
    hs = pl.BlockSpec((H, tm, HEAD_DIM), lambda i: (0, i, 0))
    row = pl.BlockSpec((1, LANES), lambda i: (0, 0))

    def body(x_ref, halo_ref, ba_ref, w_ref, al_ref, dt_ref, dq_ref, dk_ref, dv_ref, dgb_ref, dbb_ref, _,
             dc_ref, dba_ref, dal_ref, ddt_ref, cat_ref):
        halo = jnp.where(pl.program_id(0) == 0, 0.0, halo_ref[...])
        c = _conv_fwd(cat_ref, halo, x_ref[...], w_ref[...])
        _, vjp = jax.vjp(functools.partial(_gdn_pointwise, H=H), c, ba_ref[...], al_ref[...], dt_ref[...])
        dc, dba, dal, ddt = vjp((dq_ref[...], dk_ref[...], dv_ref[...], dgb_ref[...], dbb_ref[...]))
        dc_ref[...] = dc
        dba_ref[:, :LANES] = dba.astype(BF16)
        dba_ref[:, LANES:] = jnp.zeros((tm, WIN_BLOCK - LANES), BF16)

        @pl.when(pl.program_id(0) == 0)
        def _():
            dal_ref[...] = dal
            ddt_ref[...] = ddt

        @pl.when(pl.program_id(0) > 0)
        def _():
            dal_ref[...] += dal
            ddt_ref[...] += ddt

    return pl.pallas_call(
        body, name="gdn_pre_bwd", grid=(T // tm,),
        in_specs=[pl.BlockSpec((tm, 3 * A), lambda i: (i, 0)),
                  pl.BlockSpec((HALO, 3 * A), _halo_prev(tm)),
                  pl.BlockSpec((tm, LANES), lambda i: (i, 0)),
                  pl.BlockSpec((CONV_WIDTH, 3 * A), lambda i: (0, 0)),
                  row, row, hs, hs, hs, hs, hs, ANY],
        out_specs=[pl.BlockSpec((tm, 3 * A), lambda i: (i, 0)),
                   pl.BlockSpec((tm, WIN_BLOCK), lambda i: (i, n_main // WIN_BLOCK)), row, row],
        out_shape=[jax.ShapeDtypeStruct((T, 3 * A), F32), jax.ShapeDtypeStruct(dproj.shape, dproj.dtype),
                   jax.ShapeDtypeStruct((1, LANES), F32), jax.ShapeDtypeStruct((1, LANES), F32)],
        input_output_aliases={11: 1},
        scratch_shapes=[pltpu.VMEM((HALO + tm, 3 * A), F32)],
        compiler_params=_cparams(("arbitrary",)),
    )(proj_m, proj_m, proj_ba, conv_w, alog_row, dtb_row, dq, dk, dv, dgb, dbb, dproj)


def _conv_bwd(proj_m, dc, conv_w, H, dproj):
    T = proj_m.shape[0]
    A = H * HEAD_DIM
    tm = _pick(T, (256, 128))
    nt = T // tm

    def body(x_ref, halo_ref, dc_ref, nxt_ref, w_ref, _, dx_ref, dw_ref):
        i = pl.program_id(0)
        halo = jnp.where(i == 0, 0.0, halo_ref[...])
        xcat = jnp.concatenate([halo, x_ref[...]], axis=0)
        nxt = jnp.where(i == nt - 1, 0.0, nxt_ref[...])
        dc = dc_ref[...]
        dcat = jnp.concatenate([dc, nxt], axis=0)
        w = w_ref[...]
        dx = None
        rows = []
        for k in range(CONV_WIDTH):
            s = CONV_WIDTH - 1 - k
            ds = dcat if s == 0 else pltpu.roll(dcat, tm + HALO - s, 0)
            term = ds[:tm, :] * w[k:k + 1, :]
            dx = term if dx is None else dx + term
            xs = xcat if s == 0 else pltpu.roll(xcat, s, 0)
            rows.append(jnp.sum(dc * xs[HALO:, :], axis=0, keepdims=True))
        dx_ref[...] = dx.astype(BF16)
        dw = jnp.concatenate(rows + [jnp.zeros((HALO - CONV_WIDTH, 3 * A), F32)], axis=0)

        @pl.when(i == 0)
        def _():
            dw_ref[...] = dw

        @pl.when(i > 0)
        def _():
            dw_ref[...] += dw

    return pl.pallas_call(
        body, name="conv_bwd", grid=(nt,),
        in_specs=[pl.BlockSpec((tm, 3 * A), lambda i: (i, 0)),
                  pl.BlockSpec((HALO, 3 * A), _halo_prev(tm)),
                  pl.BlockSpec((tm, 3 * A), lambda i: (i, 0)),
                  pl.BlockSpec((HALO, 3 * A), lambda i: (jnp.minimum((i + 1) * (tm // HALO), T // HALO - 1), 0)),
                  pl.BlockSpec((CONV_WIDTH, 3 * A), lambda i: (0, 0)), ANY],
        out_specs=[pl.BlockSpec((tm, 3 * A), lambda i: (i, 0)),
                   pl.BlockSpec((HALO, 3 * A), lambda i: (0, 0))],
        out_shape=[jax.ShapeDtypeStruct(dproj.shape, dproj.dtype), jax.ShapeDtypeStruct((HALO, 3 * A), F32)],
        input_output_aliases={5: 0},
        compiler_params=_cparams(("arbitrary",)),
    )(proj_m, proj_m, dc, dc, conv_w, dproj)


CHUNK = 128
BLOCK = 64


def _b(x):
    return x.astype(BF16)


@jax.custom_vjp
def _bdot(a, b):
    return _dot(_b(a), _b(b))


def _bdot_f(a, b):
    return _bdot(a, b), (a, b)


def _bdot_b(res, g):
    a, b = res
    return _dot_nt(_b(g), _b(b)), _dot_tn(_b(a), _b(g))


_bdot.defvjp(_bdot_f, _bdot_b)


@jax.custom_vjp
def _bdot_nt(a, b):
    return _dot_nt(_b(a), _b(b))


def _bdot_nt_f(a, b):
    return _bdot_nt(a, b), (a, b)


def _bdot_nt_b(res, g):
    a, b = res
    return _dot(_b(g), _b(b)), _dot_tn(_b(g), _b(a))


_bdot_nt.defvjp(_bdot_nt_f, _bdot_nt_b)


@jax.custom_vjp
def _bdot_tn(a, b):
    return _dot_tn(_b(a), _b(b))


def _bdot_tn_f(a, b):
    return _bdot_tn(a, b), (a, b)


def _bdot_tn_b(res, g):
    a, b = res
    return _dot_nt(_b(b), _b(g)), _dot(_b(a), _b(g))


_bdot_tn.defvjp(_bdot_tn_f, _bdot_tn_b)


def _mask_matmul(m, x):
    hi = _b(x)
    r = x - hi.astype(F32)
    mid = _b(r)
    lo = _b(r - mid.astype(F32))
    return (_dot(m, lo) + _dot(m, mid)) + _dot(m, hi)


@jax.custom_vjp
def _mask_dot(m, mt, x):
    return _mask_matmul(m, x)


def _mask_dot_f(m, mt, x):
    return _mask_matmul(m, x), (m, mt)


def _mask_dot_b(res, g):
    m, mt = res
    return jnp.zeros_like(m), jnp.zeros_like(mt), _mask_matmul(mt, g)


_mask_dot.defvjp(_mask_dot_f, _mask_dot_b)

def _unit_lower_inverse(L):
    n = L.shape[-1]
    X = -L
    Q = X
    for _ in range(BLOCK.bit_length() - 2):
        X = _dot(_b(X), _b(X))
        Q = Q + X + _dot(_b(Q), _b(X))
    return (_iota((n, n), 0) == _iota((n, n), 1)).astype(F32) + Q


@jax.custom_vjp
def _known_inverse(L, P):
    return P


def _known_inverse_f(L, P):
    return P, P


def _known_inverse_b(P, g):
    n = P.shape[-1]
    Q = _b(P - (_iota((n, n), 0) == _iota((n, n), 1)).astype(F32))
    t = g + _dot_tn(Q, _b(g))
    return -(t + _dot_nt(_b(t), Q)), jnp.zeros_like(P)


_known_inverse.defvjp(_known_inverse_f, _known_inverse_b)


def _gdn_prep_fn(q, k, v, gb, bb, P_known=None):
    n = CHUNK
    row, col = _iota((n, n), 0), _iota((n, n), 1)
    same = (row // BLOCK) == (col // BLOCK)
    incl, strict = same & (row >= col), same & (row > col)
    bc = lambda m: jnp.broadcast_to(_b(m.astype(F32)), q.shape[:1] + (n, n))
    tril, triu, ones = bc(incl), bc(same & (row <= col)), bc(same)
    gc = _mask_dot(tril, triu, gb)
    gl = _mask_dot(ones, ones, gb)
    decay = jnp.where(incl, jnp.exp(jnp.where(incl, gc - jnp.swapaxes(gc, 1, 2), 0.0)), 0.0)
    kb = k * bb
    vb = v * bb
    qs = q * (HEAD_DIM ** -0.5)
    L = jnp.where(strict, _bdot_nt(kb, k) * decay, 0.0)
    P = _unit_lower_inverse(L) if P_known is None else _known_inverse(L, P_known)
    egc = jnp.exp(gc)
    u = _bdot(P, vb)
    w = _bdot(P, kb * egc)
    attn = jnp.where(incl, _bdot_nt(qs, k) * decay, 0.0)
    qg = qs * egc
    kdec = k * jnp.exp(gl - gc)
    eg = jnp.exp(gl).reshape(-1, n // BLOCK, BLOCK, LANES).sum(axis=2) * (1.0 / BLOCK)
    if P_known is None:
        return u, w, qg, kdec, attn, eg, P
    return u, w, qg, kdec, attn, eg


def _gdn_chain_fn(S, qg, kdec, u, w, attn, eg):
    nblk = CHUNK // BLOCK
    cat = lambda xs: jnp.concatenate(xs, axis=1)
    outs, found = [], []
    for i in range(nblk):
        r = (slice(None), slice(i * BLOCK, (i + 1) * BLOCK))
        v_new = u[r] - _bdot(w[r], S)
        found.append(v_new)
        outs.append(_bdot(qg[r], S) + _bdot(attn[r], cat(found + [jnp.zeros_like(v_new)] * (nblk - 1 - i))))
        S = S * eg[i] + _bdot_tn(kdec[r], v_new)
    return cat(outs), S


def _eg_spec(H, T, chunks, index_map, per_head):
    nblk = CHUNK // BLOCK
    block = (chunks, 1 if per_head else H, nblk, LANES)
    return pl.BlockSpec(block, index_map), jax.ShapeDtypeStruct((T // CHUNK, H, nblk, LANES), F32)


def _gdn_prep(q, k, v, gb, bb):
    H, T, _ = q.shape
    pb = _pick(T // CHUNK, (16, 8, 4, 2, 1))
    hs = pl.BlockSpec((1, CHUNK * pb, HEAD_DIM), lambda h, n: (h, n, 0))

    def body(q_ref, k_ref, v_ref, gb_ref, bb_ref, *out_refs):
        chunks = lambda ref: ref[0].reshape(pb, CHUNK, HEAD_DIM)
        outs = _gdn_prep_fn(chunks(q_ref), chunks(k_ref), chunks(v_ref), chunks(gb_ref), chunks(bb_ref))
        for i, (ref, val) in enumerate(zip(out_refs, outs)):
            if i == 5:
                ref[:, 0] = val
            else:
                ref[0] = val.reshape(pb * CHUNK, HEAD_DIM).astype(ref.dtype)

    kept = [F32, BF16, BF16, BF16, BF16, None, BF16]
    es, eshape = _eg_spec(H, T, pb, lambda h, n: (n, h, 0, 0), per_head=True)
    return pl.pallas_call(
        body, name="gdn_prep", grid=(H, T // (CHUNK * pb)),
        in_specs=[hs] * 5, out_specs=[es if dt is None else hs for dt in kept],
        out_shape=[eshape if dt is None else jax.ShapeDtypeStruct((H, T, HEAD_DIM), dt) for dt in kept],
        compiler_params=_cparams(("parallel", "parallel")),
    )(q, k, v, gb, bb)


def _gdn_prep_bwd(q, k, v, gb, bb, pinv, du, dw, dqg, dkd, dat, deg):
    H, T, _ = q.shape
    pb = _pick(T // CHUNK, (16, 8, 4, 2, 1))
    hs = pl.BlockSpec((1, CHUNK * pb, HEAD_DIM), lambda h, n: (h, n, 0))
    hshape = jax.ShapeDtypeStruct((H, T, HEAD_DIM), F32)

    def body(*refs):
        in_refs, p_ref, ct_refs, out_refs = refs[:5], refs[5], refs[6:12], refs[12:]
        chunks = lambda ref: ref[0].reshape(pb, CHUNK, HEAD_DIM)
        P = chunks(p_ref).astype(F32)
        _, vjp = jax.vjp(lambda *a: _gdn_prep_fn(*a, P_known=P), *[chunks(r) for r in in_refs])
        grads = vjp(tuple(chunks(r).astype(F32) for r in ct_refs[:5]) + (ct_refs[5][:, 0],))
        for ref, val in zip(out_refs, grads):
            ref[0] = val.reshape(pb * CHUNK, HEAD_DIM)

    es, _ = _eg_spec(H, T, pb, lambda h, n: (n, h, 0, 0), per_head=True)
    return pl.pallas_call(
        body, name="gdn_prep_bwd", grid=(H, T // (CHUNK * pb)),
        in_specs=[hs] * 11 + [es], out_specs=[hs] * 5, out_shape=[hshape] * 5,
        compiler_params=_cparams(("parallel", "parallel")),
    )(q, k, v, gb, bb, pinv, du, dw, dqg, dkd, dat, deg)


def _gdn_chain(qg, kd, u, w, attn, eg):
    H, T, _ = qg.shape
    N = T // CHUNK
    hs = pl.BlockSpec((H, CHUNK, HEAD_DIM), lambda n: (0, n, 0))
    ss = pl.BlockSpec((1, H, HEAD_DIM, HEAD_DIM), lambda n: (n, 0, 0, 0))

    def body(qg_ref, kd_ref, u_ref, w_ref, at_ref, eg_ref, o_ref, sall_ref, s_ref):
        @pl.when(pl.program_id(0) == 0)
        def _():
            s_ref[...] = jnp.zeros_like(s_ref)

        S = s_ref[...]
        sall_ref[0] = S
        eg = tuple(eg_ref[0, :, i:i + 1, :] for i in range(CHUNK // BLOCK))
        o, S2 = _gdn_chain_fn(S, qg_ref[...], kd_ref[...], u_ref[...], w_ref[...], at_ref[...], eg)
        o_ref[...] = o
        s_ref[...] = S2

    es, _ = _eg_spec(H, T, 1, lambda n: (n, 0, 0, 0), per_head=False)
    return pl.pallas_call(
        body, name="gdn_chain", grid=(N,),
        in_specs=[hs] * 5 + [es], out_specs=[hs, ss],
        out_shape=[jax.ShapeDtypeStruct((H, T, HEAD_DIM), F32),
                   jax.ShapeDtypeStruct((N, H, HEAD_DIM, HEAD_DIM), F32)],
        scratch_shapes=[pltpu.VMEM((H, HEAD_DIM, HEAD_DIM), F32)],
        compiler_params=_cparams(("arbitrary",)),
    )(qg, kd, u, w, attn, eg)


def _gdn_chain_bwd(qg, kd, u, w, attn, eg, sall, do):
    H, T, _ = qg.shape
    N = T // CHUNK
    hs = pl.BlockSpec((H, CHUNK, HEAD_DIM), lambda n: (0, N - 1 - n, 0))
    ss = pl.BlockSpec((1, H, HEAD_DIM, HEAD_DIM), lambda n: (N - 1 - n, 0, 0, 0))

    def body(qg_ref, kd_ref, u_ref, w_ref, at_ref, eg_ref, sall_ref, do_ref, *rest):
        out_refs, ds_ref = rest[:6], rest[6]

        @pl.when(pl.program_id(0) == 0)
        def _():
            ds_ref[...] = jnp.zeros_like(ds_ref)

        f32 = lambda ref: ref[...].astype(F32)
        nblk = CHUNK // BLOCK
        eg = tuple(eg_ref[0, :, i:i + 1, :] for i in range(nblk))
        _, vjp = jax.vjp(_gdn_chain_fn, sall_ref[0], f32(qg_ref), f32(kd_ref), u_ref[...], f32(w_ref),
                         f32(at_ref), eg)
        grads = vjp((do_ref[...], ds_ref[...]))
        ds_ref[...] = grads[0]
        for ref, val in zip(out_refs[:5], grads[1:6]):
            ref[...] = val.astype(ref.dtype)
        for i in range(nblk):
            out_refs[5][0, :, i:i + 1, :] = grads[6][i]

    kept = [F32, F32, BF16, BF16, F32]
    es, eshape = _eg_spec(H, T, 1, lambda n: (N - 1 - n, 0, 0, 0), per_head=False)
    return pl.pallas_call(
        body, name="gdn_chain_bwd", grid=(N,),
        in_specs=[hs] * 5 + [es, ss, hs], out_specs=[hs] * 5 + [es],
        out_shape=[jax.ShapeDtypeStruct((H, T, HEAD_DIM), dt) for dt in kept] + [eshape],
        scratch_shapes=[pltpu.VMEM((H, HEAD_DIM, HEAD_DIM), F32)],
        compiler_params=_cparams(("arbitrary",)),
    )(qg, kd, u, w, attn, eg, sall, do)


def _post_fn(ogs, za, hw):
    outs = []
    for h, o in enumerate(ogs):
        r = lax.rsqrt(jnp.mean(o * o, axis=-1, keepdims=True) + EPS)
        outs.append(o * r * hw * _silu(za[:, h * HEAD_DIM:(h + 1) * HEAD_DIM]))
    return jnp.concatenate(outs, axis=1)


def _gdn_post(og, proj_m, hw):
    H, T, _ = og.shape
    A = H * HEAD_DIM
    tm = _pick(T, (512, 256, 128))

    def body(og_ref, za_ref, hw_ref, o_ref, ot_ref):
        o = _post_fn(tuple(og_ref[h] for h in range(H)), za_ref[...], hw_ref[...])
        o_ref[...] = o.astype(BF16)
        ot_ref[...] = o.T.astype(BF16)

    return pl.pallas_call(
        body, name="gdn_post", grid=(T // tm,),
        in_specs=[pl.BlockSpec((H, tm, HEAD_DIM), lambda i: (0, i, 0)),
                  pl.BlockSpec((tm, A), lambda i: (i, ZA_BLOCK)),
                  pl.BlockSpec((1, HEAD_DIM), lambda i: (0, 0))],
        out_specs=[pl.BlockSpec((tm, A), lambda i: (i, 0)), pl.BlockSpec((A, tm), lambda i: (0, i))],
        out_shape=[jax.ShapeDtypeStruct((T, A), BF16), jax.ShapeDtypeStruct((A, T), BF16)],
        compiler_params=_cparams(("parallel",)),
    )(og, proj_m, hw)


def _gdn_post_bwd(og, proj_m, hw, d_o, dproj):
    H, T, _ = og.shape
    A = H * HEAD_DIM
    tm = _pick(T, (256, 128))

    def body(og_ref, za_ref, hw_ref, do_ref, _, dog_ref, dza_ref, dhw_ref):
        _, vjp = jax.vjp(_post_fn, tuple(og_ref[h] for h in range(H)), za_ref[...], hw_ref[...])
        dog, dza, dhw = vjp(do_ref[...])
        for h in range(H):
            dog_ref[h] = dog[h]
        dza_ref[...] = dza.astype(BF16)

        @pl.when(pl.program_id(0) == 0)
        def _():
            dhw_ref[...] = dhw

        @pl.when(pl.program_id(0) > 0)
        def _():
            dhw_ref[...] += dhw

    return pl.pallas_call(
        body, name="gdn_post_bwd", grid=(T // tm,),
        in_specs=[pl.BlockSpec((H, tm, HEAD_DIM), lambda i: (0, i, 0)),
                  pl.BlockSpec((tm, A), lambda i: (i, ZA_BLOCK)),
                  pl.BlockSpec((1, HEAD_DIM), lambda i: (0, 0)),
                  pl.BlockSpec((tm, A), lambda i: (i, 0)), ANY],
        out_specs=[pl.BlockSpec((H, tm, HEAD_DIM), lambda i: (0, i, 0)),
                   pl.BlockSpec((tm, A), lambda i: (i, ZA_BLOCK)),
                   pl.BlockSpec((1, HEAD_DIM), lambda i: (0, 0))],
        out_shape=[jax.ShapeDtypeStruct((H, T, HEAD_DIM), F32), jax.ShapeDtypeStruct(dproj.shape, dproj.dtype),
                   jax.ShapeDtypeStruct((1, HEAD_DIM), F32)],
        input_output_aliases={4: 1},
        compiler_params=_cparams(("arbitrary",)),
    )(og, proj_m, hw, d_o, dproj)


def _sgu_fn(ub, vb, zb, lw, lb, W, bbc):
    G = len(W)
    tm = ub.shape[0]
    mu = jnp.mean(vb, axis=-1, keepdims=True)
    xc = vb - mu
    var = jnp.mean(xc * xc, axis=-1, keepdims=True)
    vn = xc * lax.rsqrt(var + EPS) * lw + lb
    mask = _iota((CHUNK_B, CHUNK_B), 0) >= _iota((CHUNK_B, CHUNK_B), 1)
    cols = []
    for g in range(G):
        wm = jnp.where(mask, W[g], 0.0).astype(BF16)
        rows = []
        for c in range(tm // CHUNK_B):
            blk = vn[c * CHUNK_B:(c + 1) * CHUNK_B, g * HEAD_DIM:(g + 1) * HEAD_DIM].astype(BF16)
            rows.append(_dot(wm, blk) + bbc[g])
        cols.append(jnp.concatenate(rows, axis=0) if len(rows) > 1 else rows[0])
    s = jnp.concatenate(cols, axis=1)
    return ub * s * _silu(zb)


ZA_BLOCK = 6


def _sgu_cols(A, B):
    assert A == B
    return 3, 4, 5


def _sgu_fwd(proj_m, lw, lb, W, bbc, A):
    T = proj_m.shape[0]
    G = W.shape[0]
    B = G * HEAD_DIM
    tm = _pick(T, (256, 128))
    cu, cv, cz = _sgu_cols(A, B)

    def body(u_ref, v_ref, z_ref, lw_ref, lb_ref, w_ref, b_ref, o_ref, ot_ref):
        o = _sgu_fn(u_ref[...], v_ref[...], z_ref[...], lw_ref[...], lb_ref[...],
                    tuple(w_ref[g] for g in range(G)), tuple(b_ref[g] for g in range(G)))
        o_ref[...] = o.astype(BF16)
        ot_ref[...] = o.T.astype(BF16)

    row = pl.BlockSpec((1, B), lambda i: (0, 0))
    cube = pl.BlockSpec((G, CHUNK_B, CHUNK_B), lambda i: (0, 0, 0))
    return pl.pallas_call(
        body, name="sgu_fwd", grid=(T // tm,),
        in_specs=[pl.BlockSpec((tm, B), lambda i: (i, cu)), pl.BlockSpec((tm, B), lambda i: (i, cv)),
                  pl.BlockSpec((tm, B), lambda i: (i, cz)), row, row, cube, cube],
        out_specs=[pl.BlockSpec((tm, B), lambda i: (i, 0)), pl.BlockSpec((B, tm), lambda i: (0, i))],
        out_shape=[jax.ShapeDtypeStruct((T, B), BF16), jax.ShapeDtypeStruct((B, T), BF16)],
        compiler_params=_cparams(("parallel",)),
    )(proj_m, proj_m, proj_m, lw, lb, W, bbc)


def _sgu_bwd(proj_m, lw, lb, W, bbc, d_o, A, dproj):
    T = proj_m.shape[0]
    G = W.shape[0]
    B = G * HEAD_DIM
    tm = _pick(T, (256, 128))
    nt = T // tm
    cu, cv, cz = _sgu_cols(A, B)

    def body(u_ref, v_ref, z_ref, lw_ref, lb_ref, w_ref, b_ref, do_ref, _,
             dp_ref, dlw_ref, dlb_ref, dw_ref, db_ref, dbb_ref):
        _, vjp = jax.vjp(_sgu_fn, u_ref[...], v_ref[...], z_ref[...], lw_ref[...], lb_ref[...],
                         tuple(w_ref[g] for g in range(G)), tuple(b_ref[g] for g in range(G)))
        du, dv, dz, dlw, dlb, dW, dbb = vjp(do_ref[...])
        dW, dbb = jnp.stack(dW, axis=0), jnp.stack(dbb, axis=0)
        dp_ref[:, 0:B] = du.astype(BF16)
        dp_ref[:, B:2 * B] = dv.astype(BF16)
        dp_ref[:, 2 * B:3 * B] = dz.astype(BF16)
        i = pl.program_id(0)

        @pl.when(i == 0)
        def _():
            dlw_ref[...] = dlw
            dlb_ref[...] = dlb
            dw_ref[...] = dW
            dbb_ref[...] = dbb

        @pl.when(i > 0)
        def _():
            dlw_ref[...] += dlw
            dlb_ref[...] += dlb
            dw_ref[...] += dW
            dbb_ref[...] += dbb

        @pl.when(i == nt - 1)
        def _():
            db_ref[...] = jnp.sum(dbb_ref[...], axis=-1, keepdims=True)

    row = pl.BlockSpec((1, B), lambda i: (0, 0))
    cube = pl.BlockSpec((G, CHUNK_B, CHUNK_B), lambda i: (0, 0, 0))
    return pl.pallas_call(
        body, name="sgu_bwd", grid=(nt,),
        in_specs=[pl.BlockSpec((tm, B), lambda i: (i, cu)), pl.BlockSpec((tm, B), lambda i: (i, cv)),
                  pl.BlockSpec((tm, B), lambda i: (i, cz)), row, row, cube, cube,
                  pl.BlockSpec((tm, B), lambda i: (i, A // B)), ANY],
        out_specs=[pl.BlockSpec((tm, 3 * B), lambda i: (i, 1)), row, row, cube,
                   pl.BlockSpec((G, CHUNK_B, 1), lambda i: (0, 0, 0))],
        out_shape=[jax.ShapeDtypeStruct(dproj.shape, dproj.dtype), jax.ShapeDtypeStruct((1, B), F32),
                   jax.ShapeDtypeStruct((1, B), F32), jax.ShapeDtypeStruct((G, CHUNK_B, CHUNK_B), F32),
                   jax.ShapeDtypeStruct((G, CHUNK_B, 1), F32)],
        input_output_aliases={8: 0},
        scratch_shapes=[pltpu.VMEM((G, CHUNK_B, CHUNK_B), F32)],
        compiler_params=_cparams(("arbitrary",)),
    )(proj_m, proj_m, proj_m, lw, lb, W, bbc, d_o, dproj)


def _head_fn(mix, x, fw, tgt):
    h = x + mix
    y = _rms_fn(h, fw)
    e = y - tgt
    return 0.5 * jnp.sum(jnp.mean(e * e, axis=-1, keepdims=True), axis=0, keepdims=True)


def _out_proj_loss(oa, ob, wout, x, tgt, fw):
    T, A = oa.shape
    B = ob.shape[1]
    D = x.shape[1]
    tm = _pick(T, (256, 128))

    def body(oa_ref, ob_ref, w_ref, x_ref, t_ref, fw_ref, dh_ref, dhb_ref, loss_ref, dfw_ref):
        mix = _dot(oa_ref[...], w_ref[0:A, :]) + _dot(ob_ref[...], w_ref[A:A + B, :])
        xv, tv = x_ref[...], t_ref[...]
        loss, vjp = jax.vjp(lambda m, f: _head_fn(m, xv, f, tv), mix, fw_ref[...])
        dh, dfw = vjp(jnp.ones((1, 1), F32))
        dh_ref[...] = dh
        dhb_ref[...] = dh.astype(BF16)
        lrow = jnp.broadcast_to(loss, (1, LANES))

        @pl.when(pl.program_id(0) == 0)
        def _():
            loss_ref[...] = lrow
            dfw_ref[...] = dfw

        @pl.when(pl.program_id(0) > 0)
        def _():
            loss_ref[...] += lrow
            dfw_ref[...] += dfw

    tile = pl.BlockSpec((tm, D), lambda i: (i, 0))
    return pl.pallas_call(
        body, name="out_proj_loss", grid=(T // tm,),
        in_specs=[pl.BlockSpec((tm, A), lambda i: (i, 0)), pl.BlockSpec((tm, B), lambda i: (i, 0)),
                  pl.BlockSpec((A + B, D), lambda i: (0, 0)), tile, tile,
                  pl.BlockSpec((1, D), lambda i: (0, 0))],
        out_specs=[tile, tile, pl.BlockSpec((1, LANES), lambda i: (0, 0)),
                   pl.BlockSpec((1, D), lambda i: (0, 0))],
        out_shape=[jax.ShapeDtypeStruct((T, D), F32), jax.ShapeDtypeStruct((T, D), BF16),
                   jax.ShapeDtypeStruct((1, LANES), F32), jax.ShapeDtypeStruct((1, D), F32)],
        compiler_params=_cparams(("arbitrary",)),
    )(oa, ob, wout, x, tgt, fw)


def _adamw(w, g, m, v, name):
    R, Cn = w.shape
    cap = max(8, 512 * 1024 // Cn)
    tr = max(t for t in range(8, min(R, cap) + 1, 8) if R % t == 0) if R > cap else R

    def body(w_ref, g_ref, m_ref, v_ref, d_ref, mo_ref, vo_ref):
        g = g_ref[...]
        m = ADAM_B1 * m_ref[...] + (1.0 - ADAM_B1) * g
        v = ADAM_B2 * v_ref[...] + (1.0 - ADAM_B2) * jnp.square(g)
        m_hat = m / (1.0 - ADAM_B1 ** ADAM_STEP)
        v_hat = v / (1.0 - ADAM_B2 ** ADAM_STEP)
        d_ref[...] = -ADAM_LR * (m_hat / (jnp.sqrt(v_hat) + ADAM_EPS) + ADAM_WD * w_ref[...])
        mo_ref[...] = m
        vo_ref[...] = v

    tile = pl.BlockSpec((tr, Cn), lambda i: (i, 0))
    shape = jax.ShapeDtypeStruct((R, Cn), F32)
    return pl.pallas_call(
        body, name=name, grid=(R // tr,), in_specs=[tile] * 4, out_specs=[tile] * 3,
        out_shape=[shape] * 3, compiler_params=_cparams(("parallel",)),
    )(w, g, m, v)


def _place():
    x, y, c = lax.axis_index("x"), lax.axis_index("y"), lax.axis_index("c")
    others = [(1 - x, y), (x, 1 - y), (1 - x, 1 - y)]
    return x, y, c, others


def _chip_index(px, py):
    return 2 * px + py


ANY = pl.BlockSpec(memory_space=pl.ANY)


def _gather_ride(blocks, split):
    n = len(blocks)

    def plan(in_refs, out_refs, send_sems, recv_sems):
        x, y, c, _ = _place()
        me, kx, ky, kd = (_chip_index(px, py) for px, py in ((x, y), (1 - x, y), (x, 1 - y), (1 - x, 1 - y)))
        to_x, to_y, to_s = (1 - x, y, c), (x, 1 - y, c), (x, y, 1 - c)

        def copy(sem, src, dst, to):
            return pltpu.make_async_remote_copy(src_ref=src, dst_ref=dst, send_sem=send_sems.at[sem],
                                                recv_sem=recv_sems.at[sem], device_id=to, device_id_type=MESH_ID)

        first, second, third, awaited = [], [], [], []
        for a in range(n):
            out, s0 = out_refs[a], 8 * a
            if not split[a]:
                for j, (k, to) in enumerate(((kx, to_x), (ky, to_y), (kd, (1 - x, 1 - y, c)))):
                    first.append(lambda j=j, to=to, a=a, out=out, s0=s0: copy(s0 + j, in_refs[a], out.at[me], to))
                    awaited.append((lambda j=j, k=k, to=to, out=out, s0=s0: copy(s0 + j, out.at[k], out.at[k], to),
                                    None))
                continue
            h = blocks[a].shape[0] // 2
            q = h // 2
            half = lambda k, core, out=out, h=h: out.at[k, pl.ds(core * h, h), :]
            quarter = lambda k, core, i, out=out, h=h, q=q: out.at[k, pl.ds(core * h + i * q, q), :]
            mine = in_refs[a].at[pl.ds(c * h, h), :]
            first.append(lambda s0=s0, mine=mine, half=half: copy(s0, mine, half(me, c), to_x))
            first.append(lambda s0=s0, mine=mine, half=half: copy(s0 + 1, mine, half(me, c), to_y))
            fwd0 = lambda s0=s0, quarter=quarter: copy(s0 + 2, quarter(kx, c, 0), quarter(kx, c, 0), to_y)
            fwd1 = lambda s0=s0, quarter=quarter: copy(s0 + 3, quarter(ky, c, 1), quarter(ky, c, 1), to_x)
            pieces = [(s0 + 0, lambda half=half: half(kx, c), lambda half=half: half(kx, 1 - c), to_x, fwd0),
                      (s0 + 1, lambda half=half: half(ky, c), lambda half=half: half(ky, 1 - c), to_y, fwd1),
                      (s0 + 2, lambda quarter=quarter: quarter(kd, c, 0), lambda quarter=quarter: quarter(kd, 1 - c, 0),
                       to_y, None),
                      (s0 + 3, lambda quarter=quarter: quarter(kd, c, 1), lambda quarter=quarter: quarter(kd, 1 - c, 1),
                       to_x, None)]
            for i, (sem, here, there, frm, fwd) in enumerate(pieces):
                passing = lambda s0=s0, i=i, here=here: copy(s0 + 4 + i, here(), here(), to_s)
                awaited.append((lambda sem=sem, here=here, frm=frm: copy(sem, here(), here(), frm), (fwd, passing)))
                if fwd is not None:
                    second.append(fwd)
                third.append((passing, lambda s0=s0, i=i, there=there: copy(s0 + 4 + i, there(), there(), to_s)))
        return first, second, third, awaited

    def start(*refs):
        for send in plan(*refs)[0]:
            send().start()

    def finish(*refs):
        first, second, third, awaited = plan(*refs)
        for arrival, then in awaited:
            arrival().wait_recv()
            for nxt in (then or ()):
                if nxt is not None:
                    nxt().start()
        for _, from_sibling in third:
            from_sibling().wait_recv()
        for send in first + second + [p for p, _ in third]:
            send().wait_send()

    shapes = [jax.ShapeDtypeStruct((N_CHIPS,) + b.shape, b.dtype) for b in blocks]
    return _Ride(blocks, shapes, 8 * n, start, finish)


def _put_own(gathered, own):
    me = _chip_index(lax.axis_index("x"), lax.axis_index("y"))
    return lax.dynamic_update_index_in_dim(gathered, own, me, 0)


def _allreduce_small(buf):
    R0, L = buf.shape
    R = -(-R0 // 16) * 16
    h = R // 2
    buf = jnp.pad(buf, ((0, R - R0), (0, 0)))

    def body(in_ref, out_ref, sib_ref, pair_ref, chips_ref, send_sems, recv_sems):
        x, y, c, others = _place()
        me = _chip_index(x, y)
        sibling = (x, y, 1 - c)

        def copy(sem, src, dst, to):
            return pltpu.make_async_remote_copy(src_ref=src, dst_ref=dst, send_sem=send_sems.at[sem],
                                                recv_sem=recv_sems.at[sem], device_id=to, device_id_type=MESH_ID)

        cp = copy(0, in_ref, sib_ref, sibling)
        cp.start()
        cp.wait()
        pair_ref[...] = in_ref[...] + sib_ref[...]
        rows = lambda core: pl.ds(pl.multiple_of(core * h, 8), h)
        sends = [copy(1 + j, pair_ref.at[rows(c), :], chips_ref.at[me], (*chip, c)) for j, chip in enumerate(others)]
        for s in sends:
            s.start()
        chips_ref[me] = pair_ref[rows(c), :]
        for j, chip in enumerate(others):
            k = _chip_index(*chip)
            copy(1 + j, chips_ref.at[k], chips_ref.at[k], (*chip, c)).wait_recv()
        out_ref[rows(c), :] = ((chips_ref[0] + chips_ref[1]) + chips_ref[2]) + chips_ref[3]
        swap = copy(4, out_ref.at[rows(c), :], out_ref.at[rows(c), :], sibling)
        swap.start()
        copy(4, out_ref.at[rows(1 - c), :], out_ref.at[rows(1 - c), :], sibling).wait_recv()
        for s in sends + [swap]:
            s.wait_send()

    vm = pl.BlockSpec(memory_space=pltpu.VMEM)
    return pl.pallas_call(
        body, name="allreduce_small", in_specs=[vm], out_specs=vm,
        out_shape=jax.ShapeDtypeStruct((R, L), F32),
        scratch_shapes=[pltpu.VMEM((R, L), F32), pltpu.VMEM((R, L), F32), pltpu.VMEM((N_CHIPS, h, L), F32),
                        pltpu.SemaphoreType.DMA((5,)), pltpu.SemaphoreType.DMA((5,))],
        compiler_params=pltpu.CompilerParams(vmem_limit_bytes=VMEM_LIMIT),
    )(buf)[:R0]


def _pair_ride(g):
    nb, R, Cn = g.shape
    h = R // 2

    def copy(in_refs, out_refs, send_sems, recv_sems):
        x, y, c, _ = _place()
        return pltpu.make_async_remote_copy(src_ref=in_refs[0].at[:, pl.ds((1 - c) * h, h), :], dst_ref=out_refs[0],
                                            send_sem=send_sems.at[0], recv_sem=recv_sems.at[0],
                                            device_id=(x, y, 1 - c), device_id_type=MESH_ID)

    return _Ride([g], [jax.ShapeDtypeStruct((nb, h, Cn), g.dtype)], 1,
                 lambda *refs: copy(*refs).start(), lambda *refs: copy(*refs).wait())


def _pair_sum(g, land, c_arr, name, ride=None):
    nb, R, Cn = g.shape
    hr = R // 2
    tr = _pick(hr, (256, 128, 64, 32, 16))
    nt = hr // tr

    def body(c_ref, g_ref, l_ref, o_ref):
        o_ref[...] = (g_ref[...].astype(F32) + l_ref[...].astype(F32)).astype(BF16)

    return _pallas(
        body, (c_arr, g, land), name=name, prefetch=1, grid=(nb, nt),
        in_specs=[pl.BlockSpec((1, tr, Cn), lambda b, i, c_ref: (b, c_ref[0] * nt + i, 0)),
                  pl.BlockSpec((1, tr, Cn), lambda b, i, c_ref: (b, i, 0))],
        out_specs=pl.BlockSpec((1, tr, Cn), lambda b, i, c_ref: (b, i, 0)),
        out_shape=jax.ShapeDtypeStruct((nb, hr, Cn), BF16),
        semantics=("parallel", "parallel"), ride=ride)


def _chip_ride(parts, cols=None):
    m = len(parts)

    def copies(in_refs, out_refs, send_sems, recv_sems):
        x, y, c, others = _place()
        me = _chip_index(x, y)
        def mk(j, chip, n, landing):
            k = _chip_index(*chip)
            src = in_refs[n].at[k]
            if cols is not None:
                src = src.at[:, pl.ds(pl.multiple_of(cols[0](k), LANES), cols[1])]
            return pltpu.make_async_remote_copy(
                src_ref=src, dst_ref=out_refs[n].at[landing(k)], send_sem=send_sems.at[m * j + n],
                recv_sem=recv_sems.at[m * j + n], device_id=(*chip, c), device_id_type=MESH_ID)

        pairs = [(j, chip, n) for j, chip in enumerate(others) for n in range(m)]
        return pairs, (lambda *p: mk(*p, lambda k: me)), (lambda *p: mk(*p, lambda k: k))

    def start(*refs):
        pairs, send, _ = copies(*refs)
        for p in pairs:
            send(*p).start()

    def finish(*refs):
        pairs, send, arrival = copies(*refs)
        for p in pairs:
            arrival(*p).wait_recv()
        for p in pairs:
            send(*p).wait_send()

    width = lambda p: p.shape[2] if cols is None else cols[1]
    return _Ride(parts, [jax.ShapeDtypeStruct(p.shape[:2] + (width(p),), p.dtype) for p in parts], 3 * m,
                 start, finish)


def _put_own_slot(q, p, cols=None):
    me = _chip_index(lax.axis_index("x"), lax.axis_index("y"))
    own = lax.dynamic_index_in_dim(p, me, 0, keepdims=False)
    if cols is not None:
        own = lax.dynamic_slice_in_dim(own, cols[0](me), cols[1], axis=1)
    return lax.dynamic_update_index_in_dim(q, own, me, 0)


def _chip_sum(q, c_arr, name):
    nb, hr, Cn = q.shape
    tr = _pick(hr, (256, 128, 64, 32, 16))
    nt = hr // tr

    def body(c_ref, q_ref, o_ref):
        f = lambda k: q_ref[k].astype(F32)
        o_ref[...] = ((f(0) + f(1)) + f(2)) + f(3)

    return _pallas(
        body, (c_arr, q), name=name, prefetch=1, grid=(nt,),
        in_specs=[pl.BlockSpec((nb, tr, Cn), lambda i, c_ref: (0, i, 0))],
        out_specs=pl.BlockSpec((tr, Cn), lambda i, c_ref: (c_ref[0] * nt + i, 0)),
        out_shape=jax.ShapeDtypeStruct((2 * hr, Cn), F32),
        semantics=("parallel",))


def _sibling_fill(fw, fo):
    def body(_, __, fw_ref, fo_ref, send_sems, recv_sems):
        x, y, c, _ = _place()
        copies = []
        for n, ref in enumerate((fw_ref, fo_ref)):
            h = ref.shape[0] // 2
            mine = ref.at[pl.ds(c * h, h), :]
            theirs = ref.at[pl.ds((1 - c) * h, h), :]
            mk = lambda src, dst: pltpu.make_async_remote_copy(
                src_ref=src, dst_ref=dst, send_sem=send_sems.at[n], recv_sem=recv_sems.at[n],
                device_id=(x, y, 1 - c), device_id_type=MESH_ID)
            send = mk(mine, mine)
            send.start()
            copies.append((send, mk(theirs, theirs)))
        for send, arrival in copies:
            arrival.wait_recv()
            send.wait_send()

    return pl.pallas_call(
        body, name="sibling_fill", in_specs=[ANY, ANY], out_specs=[ANY, ANY],
        out_shape=[jax.ShapeDtypeStruct(fw.shape, F32), jax.ShapeDtypeStruct(fo.shape, F32)],
        input_output_aliases={0: 0, 1: 1},
        scratch_shapes=[pltpu.SemaphoreType.DMA((2,)), pltpu.SemaphoreType.DMA((2,))],
        compiler_params=pltpu.CompilerParams(has_side_effects=True),
    )(fw, fo)


class _Layout:
    def __init__(self, H, G, nb, Cb):
        A, B = H * HEAD_DIM, G * HEAD_DIM
        self.n_main = 4 * A + 3 * B
        self.k = -(-(self.n_main + LANES) // WIN_BLOCK) * WIN_BLOCK
        cuts = [0, 3 * A, 4 * A, 4 * A + 2 * H, nb * Cb]
        starts = [0, 3 * A + 3 * B, self.n_main, 3 * A]
        self.pieces = []
        self.windows, self.runs = [], []
        for n in range(nb):
            segs = []
            for s in range(4):
                lo, hi = max(cuts[s], n * Cb), min(cuts[s + 1], (n + 1) * Cb)
                if lo < hi:
                    segs.append((starts[s] + lo - cuts[s], lo - n * Cb, hi - lo))
            self.pieces += [(own, n, col, ln) for own, col, ln in segs]
            blocks = sorted({b for own, _, ln in segs for b in range(own // WIN_BLOCK, (own + ln - 1) // WIN_BLOCK + 1)})
            self.windows.append(blocks)
            self.runs.append([(blocks.index(own // WIN_BLOCK) * WIN_BLOCK + own % WIN_BLOCK, ln)
                              for own, _, ln in segs])
        self.wb = max(len(b) for b in self.windows)
        self.table = [b + [b[-1]] * (self.wb - len(b)) for b in self.windows]
        self.pieces.sort()
        self.used_from = [min(c for c, _ in r) // LANES * LANES for r in self.runs]
        self.used = max(-(-max(c + ln for c, ln in r) // LANES) * LANES - f for r, f in zip(self.runs, self.used_from))
        self.used_from = [min(f, self.wb * WIN_BLOCK - self.used) for f in self.used_from]

    def to_own_order(self, g_in):
        nb, D, Cb = g_in.shape
        tr = _pick(D, (256, 128))

        def body(g_ref, o_ref):
            cols, at = [], 0
            for own, n, col, ln in self.pieces:
                if own > at:
                    cols.append(jnp.zeros((tr, own - at), g_in.dtype))
                cols.append(g_ref[n, :, col:col + ln])
                at = own + ln
            if at < self.k:
                cols.append(jnp.zeros((tr, self.k - at), g_in.dtype))
            o_ref[...] = jnp.concatenate(cols, axis=1)

        return pl.pallas_call(
            body, name="own_order", grid=(D // tr,),
            in_specs=[pl.BlockSpec((nb, tr, Cb), lambda i: (0, i, 0))],
            out_specs=pl.BlockSpec((tr, self.k), lambda i: (i, 0)),
            out_shape=jax.ShapeDtypeStruct((D, self.k), g_in.dtype),
            compiler_params=_cparams(("parallel",)),
        )(g_in)

    def from_window(self, win, chip, Cb):
        pick = lambda runs, f: (lambda w: jnp.concatenate([w[:, c - f:c - f + ln] for c, ln in runs], axis=1))
        return lax.switch(chip, [pick(r, f) for r, f in zip(self.runs, self.used_from)], win)

    def used_start(self, chip):
        return sum(jnp.where(chip == n, f, 0) for n, f in enumerate(self.used_from))


def _device_step(x, tgt, norm_w, win_b, wout_b, conv_b, a_log, dt_bias, head_norm_w, sgu_ln_w, sgu_ln_b,
                 w_spatial, b_spatial, final_norm_w, c_arr):
    T, D = x.shape
    H = a_log.shape[1]
    A = H * HEAD_DIM
    G = w_spatial.shape[0]
    B = G * HEAD_DIM
    nb, Cb, Rb = N_CHIPS, win_b.shape[1], wout_b.shape[0]
    lay = _Layout(H, G, nb, Cb)
    alog_row = jnp.pad(a_log, ((0, 0), (H, LANES - 2 * H)))
    dtb_row = jnp.pad(dt_bias, ((0, 0), (H, LANES - 2 * H)))
    bbc = jnp.broadcast_to(b_spatial[:, :, None], (G, CHUNK_B, CHUNK_B))

    (xn, xn_t), (g_in,) = _rms_in(x, norm_w, ride=_gather_ride([win_b], [True]))
    w_own = lay.to_own_order(_put_own(g_in, win_b))
    proj_m, (g_out, g_conv) = _mm_nn(xn, w_own, F32, "in_proj", tm=2048, cols=(0, lay.n_main),
                                     ride=_gather_ride([wout_b, conv_b], [False, False]))
    wout = _put_own(g_out, wout_b).reshape(nb * Rb, D)
    conv_w = _put_own(g_conv, conv_b).transpose(1, 0, 2).reshape(CONV_WIDTH, nb * conv_b.shape[1])
    q, k, v, gb, bb, proj_ba = _gdn_pre(proj_m, xn, w_own, conv_w, alog_row, dtb_row, H)
    u, w, qg, kd, attn, eg, pinv = _gdn_prep(q, k, v, gb, bb)
    og, sall = _gdn_chain(qg, kd, u, w, attn, eg)
    oa, oa_t = _gdn_post(og, proj_m, head_norm_w)
    ob, ob_t = _sgu_fwd(proj_m, sgu_ln_w, sgu_ln_b, w_spatial, bbc, A)
    dh, dhb, loss_row, d_fnw = _out_proj_loss(oa, ob, wout, x, tgt, final_norm_w.reshape(1, D))

    d_o = _mm_nn(dhb, wout.T, F32, "out_proj_dx", tm=2048)
    dproj = lax.empty((T, lay.k), BF16)
    dproj, d_lw, d_lb, d_ws, d_bs = _sgu_bwd(proj_m, sgu_ln_w, sgu_ln_b, w_spatial, bbc, d_o, A, dproj)
    dog, dproj, d_hw = _gdn_post_bwd(og, proj_m, head_norm_w, d_o, dproj)
    dqg, dkd, du, dw, dat, deg = _gdn_chain_bwd(qg, kd, u, w, attn, eg, sall, dog)
    dq, dk, dv, dgb, dbb = _gdn_prep_bwd(q, k, v, gb, bb, pinv, du, dw, dqg, dkd, dat, deg)
    dc, dproj, d_al, d_dt = _gdn_pre_bwd(proj_m, proj_ba, conv_w, alog_row, dtb_row, dq, dk, dv, dgb, dbb, H,
                                         dproj)
    dproj, d_conv = _conv_bwd(proj_m, dc, conv_w, H, dproj)

    table = jnp.array([b for row in lay.table for b in row], jnp.int32)
    d_win = _mm_windows(xn_t, dproj, table, nb, "in_proj_dw")
    d_wout, (land_w,) = _mm_nn_pair(oa_t, ob_t, dhb, "out_proj_dw", ride=_pair_ride(d_win))
    d_wout = d_wout.reshape(nb, Rb, D)
    pair_w, (land_o,) = _pair_sum(d_win, land_w, c_arr, "pair_sum_w_in", ride=_pair_ride(d_wout))
    pair_o = _pair_sum(d_wout, land_o, c_arr, "pair_sum_w_out")
    used = (lay.used_start, lay.used)
    dxn, (all_w,) = _mm_nt_rhs_outer(dproj, w_own, F32, "in_proj_dx", ride=_chip_ride([pair_w], used))
    (grad_x, d_nw), (all_o,) = _rms_in_bwd(x, norm_w, dxn, dh, ride=_chip_ride([pair_o]))
    all_w, all_o = _put_own_slot(all_w, pair_w, used), _put_own_slot(all_o, pair_o)
    small = dict(norm_w=d_nw, conv_w=d_conv[:CONV_WIDTH], a_log=d_al[:, H:2 * H], dt_bias=d_dt[:, H:2 * H],
                 head_norm_w=d_hw, sgu_ln_w=d_lw, sgu_ln_b=d_lb, w_spatial=d_ws, b_spatial=d_bs[:, :, 0],
                 final_norm_w=d_fnw)
    return loss_row, grad_x, small, all_w, all_o


SMALL = ("norm_w", "conv_w", "a_log", "dt_bias", "head_norm_w", "sgu_ln_w", "sgu_ln_b", "w_spatial",
         "b_spatial", "final_norm_w")


def _pack(parts):
    rows = []
    for p in parts:
        f = p.reshape(-1)
        f = jnp.pad(f, (0, (-f.shape[0]) % (8 * LANES)))
        rows.append(f.reshape(-1, LANES))
    return jnp.concatenate(rows, axis=0)


def _unpack(buf, shapes):
    out, r = [], 0
    for s in shapes:
        n = 1
        for d in s:
            n *= d
        nr = -(-n // (8 * LANES)) * 8
        out.append(buf[r:r + nr].reshape(-1)[:n].reshape(s))
        r += nr
    return out


def kernel(x, norm_w, w_in, conv_w, a_log, dt_bias, head_norm_w, sgu_ln_w, sgu_ln_b, w_spatial, b_spatial, w_out, final_norm_w, loss_target, m_norm_w, m_w_in, m_conv_w, m_a_log, m_dt_bias, m_head_norm_w, m_sgu_ln_w, m_sgu_ln_b, m_w_spatial, m_b_spatial, m_w_out, m_final_norm_w, v_norm_w, v_w_in, v_conv_w, v_a_log, v_dt_bias, v_head_norm_w, v_sgu_ln_w, v_sgu_ln_b, v_w_spatial, v_b_spatial, v_w_out, v_final_norm_w):
    T, D = x.shape[1], x.shape[2]
    weights = dict(norm_w=norm_w, w_in=w_in, conv_w=conv_w, a_log=a_log, dt_bias=dt_bias, head_norm_w=head_norm_w,
                   sgu_ln_w=sgu_ln_w, sgu_ln_b=sgu_ln_b, w_spatial=w_spatial, b_spatial=b_spatial, w_out=w_out,
                   final_norm_w=final_norm_w)
    mom_m = dict(norm_w=m_norm_w, w_in=m_w_in, conv_w=m_conv_w, a_log=m_a_log, dt_bias=m_dt_bias,
                 head_norm_w=m_head_norm_w, sgu_ln_w=m_sgu_ln_w, sgu_ln_b=m_sgu_ln_b, w_spatial=m_w_spatial,
                 b_spatial=m_b_spatial, w_out=m_w_out, final_norm_w=m_final_norm_w)
    mom_v = dict(norm_w=v_norm_w, w_in=v_w_in, conv_w=v_conv_w, a_log=v_a_log, dt_bias=v_dt_bias,
                 head_norm_w=v_head_norm_w, sgu_ln_w=v_sgu_ln_w, sgu_ln_b=v_sgu_ln_b, w_spatial=v_w_spatial,
                 b_spatial=v_b_spatial, w_out=v_w_out, final_norm_w=v_final_norm_w)
    me = _chip_index(lax.axis_index("x"), lax.axis_index("y"))
    c_arr = lax.axis_index("c").astype(jnp.int32).reshape(1)
    Din, Cb = w_in.shape[1], w_in.shape[2]
    Rb = w_out.shape[1]
    cconv = conv_w.shape[2]

    loss_row, grad_x, g, qw, qo = _device_step(
        x[0], loss_target[0], norm_w, w_in[0].astype(BF16), w_out[0].astype(BF16), conv_w[0], a_log, dt_bias,
        head_norm_w, sgu_ln_w, sgu_ln_b, w_spatial[0], b_spatial[0], final_norm_w, c_arr)

    small_shapes = [tuple(g[n].shape) for n in SMALL] + [(1, LANES)]
    small = _allreduce_small(_pack([g[n] for n in SMALL] + [loss_row]))
    gsum_in, gsum_out = _sibling_fill(_chip_sum(qw, c_arr, "chip_sum_w_in"), _chip_sum(qo, c_arr, "chip_sum_w_out"))
    gsum_in = _Layout(a_log.shape[1], w_spatial.shape[1], N_CHIPS, Cb).from_window(gsum_in, me, Cb)
    *small, loss_sum = _unpack(small, small_shapes)
    gsmall = dict(zip(SMALL, small))
    gsmall["conv_w"] = lax.dynamic_slice_in_dim(gsmall["conv_w"], me * cconv, cconv, axis=1)

    grads, deltas, new_m, new_v = {}, {}, {}, {}
    d, m2, v2 = _adamw(w_out[0], gsum_out, m_w_out[0], v_w_out[0], "adamw_w_out")
    grads["w_out"], deltas["w_out"], new_m["w_out"], new_v["w_out"] = gsum_out[None], d[None], m2[None], v2[None]
    flat = lambda a: a.transpose(2, 0, 1).reshape(-1, LANES)
    unflat = lambda f: f.reshape(Cb, 1, Din).transpose(1, 2, 0)
    g_flat = gsum_in.T.reshape(-1, LANES)
    d, m2, v2 = _adamw(flat(w_in), g_flat, flat(m_w_in), flat(v_w_in), "adamw_w_in")
    grads["w_in"], deltas["w_in"], new_m["w_in"], new_v["w_in"] = unflat(g_flat), unflat(d), unflat(m2), unflat(v2)
    shapes = [tuple(weights[n].shape) for n in SMALL]
    ds, ms, vs = _adamw(_pack([weights[n] for n in SMALL]), _pack([gsmall[n] for n in SMALL]),
                        _pack([mom_m[n] for n in SMALL]), _pack([mom_v[n] for n in SMALL]), "adamw_small")
    for n, gq, d, m2, v2 in zip(SMALL, [gsmall[n] for n in SMALL], _unpack(ds, shapes), _unpack(ms, shapes),
                                _unpack(vs, shapes)):
        grads[n], deltas[n], new_m[n], new_v[n] = gq.reshape(weights[n].shape), d, m2, v2

    loss = loss_sum[0, 0]
    order = ("norm_w", "w_in", "conv_w", "a_log", "dt_bias", "head_norm_w", "sgu_ln_w", "sgu_ln_b", "w_spatial",
             "b_spatial", "w_out", "final_norm_w")
    return (loss, grad_x[None], *[grads[n] for n in order], *[deltas[n] for n in order],
            *[new_m[n] for n in order], *[new_v[n] for n in order])
```

```python
import functools

import jax
import jax.numpy as jnp
from jax import lax
from jax.experimental import pallas as pl
from jax.experimental.pallas import tpu as pltpu

F32 = jnp.float32
BF16 = jnp.bfloat16
EPS = 1e-6
HEAD_DIM = 128
CHUNK_B = 128
CONV_WIDTH = 4
LANES = 128
HALO = 8
N_CHIPS = 4
ADAM_LR = 0.001
ADAM_B1 = 0.9
ADAM_B2 = 0.999
ADAM_EPS = 1e-08
ADAM_WD = 0.01
ADAM_STEP = 10
VMEM_LIMIT = 56 * 1024 * 1024
MESH_ID = pl.DeviceIdType.MESH


def _cparams(sem=None, **kw):
    return pltpu.CompilerParams(dimension_semantics=sem, vmem_limit_bytes=VMEM_LIMIT, **kw)


def _matmul(a, b, ca, cb):
    nb = a.ndim - 2
    batch = tuple(range(nb))
    return lax.dot_general(a, b, (((ca + nb,), (cb + nb,)), (batch, batch)), preferred_element_type=F32)


def _dot(a, b):
    return _matmul(a, b, 1, 0)


def _dot_nt(a, b):
    return _matmul(a, b, 1, 1)


def _dot_tn(a, b):
    return _matmul(a, b, 0, 0)


def _iota(shape, dim):
    return lax.broadcasted_iota(jnp.int32, shape, dim)


def _sigmoid(x):
    return 0.5 * (jnp.tanh(0.5 * x) + 1.0)


def _silu(x):
    return x * _sigmoid(x)


def _softplus(x):
    z = jnp.exp(-jnp.abs(x))
    small = z * (1.0 - z * (0.5 - z * (1.0 / 3.0)))
    return jnp.maximum(x, 0.0) + jnp.where(z < 1e-3, small, jnp.log(1.0 + z))


def _pick(n, pref):
    for t in pref:
        if n % t == 0:
            return t
    return n


class _Ride:
    def __init__(self, operands, out_shape, n_sems, start, finish):
        self.operands, self.out_shape, self.n_sems = list(operands), list(out_shape), n_sems
        self.start, self.finish = start, finish


def _pallas(body, operands, *, name, grid, in_specs, out_specs, out_shape, semantics, scratch_shapes=(),
            prefetch=0, ride=None):
    single = not isinstance(out_shape, (list, tuple))
    outs = [out_shape] if single else list(out_shape)
    ospecs = [out_specs] if single else list(out_specs)
    in_specs, scratch = list(in_specs), list(scratch_shapes)
    n_in, n_out, n_sc = len(operands) - prefetch, len(outs), len(scratch)
    kernel = body
    params = _cparams(semantics)
    if ride is not None:
        n_xin, n_xout = len(ride.operands), len(ride.out_shape)

        def kernel(*refs):
            pre, refs = refs[:prefetch], refs[prefetch:]
            ins, refs = refs[:n_in], refs[n_in:]
            xins, refs = refs[:n_xin], refs[n_xin:]
            mains, refs = refs[:n_out], refs[n_out:]
            xouts, refs = refs[:n_xout], refs[n_xout:]
            sc, (send, recv) = refs[:n_sc], refs[n_sc:]
            ids = [pl.program_id(a) for a in range(len(grid))]
            first = functools.reduce(jnp.logical_and, [i == 0 for i in ids])
            last = functools.reduce(jnp.logical_and, [i == g - 1 for i, g in zip(ids, grid)])

            @pl.when(first)
            def _():
                ride.start(xins, xouts, send, recv)

            body(*pre, *ins, *mains, *sc)

            @pl.when(last)
            def _():
                ride.finish(xins, xouts, send, recv)

        operands = list(operands) + ride.operands
        in_specs += [ANY] * n_xin
        ospecs += [ANY] * n_xout
        outs += ride.out_shape
        scratch += [pltpu.SemaphoreType.DMA((ride.n_sems,)), pltpu.SemaphoreType.DMA((ride.n_sems,))]
        params = _cparams(("arbitrary",) * len(grid), has_side_effects=True)
    if prefetch:
        spec = dict(grid_spec=pltpu.PrefetchScalarGridSpec(
            num_scalar_prefetch=prefetch, grid=grid, in_specs=in_specs, out_specs=ospecs, scratch_shapes=scratch))
    else:
        spec = dict(grid=grid, in_specs=in_specs, out_specs=ospecs, scratch_shapes=scratch)
    res = pl.pallas_call(kernel, name=name, out_shape=outs, compiler_params=params, **spec)(*operands)
    main = res[0] if single else list(res[:n_out])
    return main if ride is None else (main, list(res[n_out:]))


def _mm_nn(a, b, out_dtype, name, tm=1024, tn=512, tk=None, cols=None, ride=None):
    M, K = a.shape
    c0, N = (0, b.shape[1]) if cols is None else cols
    tm = _pick(M, (tm, 1024, 512, 256, 128))
    tn = _pick(N, (tn, 512, 384, 256, 128))
    tk = K if tk is None else _pick(K, (tk,))
    nk = K // tk
    j0 = c0 // tn
    assert c0 % tn == 0

    def body(a_ref, b_ref, o_ref, *scratch):
        part = _dot(a_ref[...], b_ref[...])
        if nk == 1:
            o_ref[...] = part.astype(out_dtype)
        else:
            acc_ref, = scratch
            k = pl.program_id(2)

            @pl.when(k == 0)
            def _():
                acc_ref[...] = part

            @pl.when(k > 0)
            def _():
                acc_ref[...] += part

            @pl.when(k == nk - 1)
            def _():
                o_ref[...] = acc_ref[...].astype(out_dtype)

    return _pallas(
        body, (a, b), name=name, grid=(M // tm, N // tn, nk),
        in_specs=[pl.BlockSpec((tm, tk), lambda i, j, k: (i, k)),
                  pl.BlockSpec((tk, tn), lambda i, j, k: (k, j + j0))],
        out_specs=pl.BlockSpec((tm, tn), lambda i, j, k: (i, j)),
        out_shape=jax.ShapeDtypeStruct((M, N), out_dtype),
        scratch_shapes=[] if nk == 1 else [pltpu.VMEM((tm, tn), F32)],
        semantics=("parallel", "parallel", "arbitrary"), ride=ride)


def _mm_nt_rhs_outer(a, b, out_dtype, name, tm=256, tn=1024, ride=None):
    M, K = a.shape
    N, _ = b.shape
    tm = _pick(M, (tm, 128))
    tn = _pick(N, (tn, 512, 256, 128))

    def body(a_ref, b_ref, o_ref):
        o_ref[...] = _dot_nt(a_ref[...], b_ref[...]).astype(out_dtype)

    return _pallas(
        body, (a, b), name=name, grid=(N // tn, M // tm),
        in_specs=[pl.BlockSpec((tm, K), lambda j, i: (i, 0)),
                  pl.BlockSpec((tn, K), lambda j, i: (j, 0))],
        out_specs=pl.BlockSpec((tm, tn), lambda j, i: (i, j)),
        out_shape=jax.ShapeDtypeStruct((M, N), out_dtype),
        semantics=("parallel", "parallel"), ride=ride)


WIN_BLOCK = 256


def _mm_windows(a, b, table, nb, name, tm=2048):
    M, K = a.shape
    wb = table.shape[0] // nb
    tm = _pick(M, (tm, 1024, 512, 256, 128))

    def body(tab_ref, a_ref, b_ref, o_ref):
        o_ref[0] = _dot(a_ref[...], b_ref[...]).astype(BF16)

    return pl.pallas_call(
        body, name=name,
        grid_spec=pltpu.PrefetchScalarGridSpec(
            num_scalar_prefetch=1, grid=(nb, M // tm, wb),
            in_specs=[pl.BlockSpec((tm, K), lambda n, i, t, tab: (i, 0)),
                      pl.BlockSpec((K, WIN_BLOCK), lambda n, i, t, tab: (0, tab[n * wb + t]))],
            out_specs=pl.BlockSpec((1, tm, WIN_BLOCK), lambda n, i, t, tab: (n, i, t))),
        out_shape=jax.ShapeDtypeStruct((nb, M, wb * WIN_BLOCK), BF16),
        compiler_params=_cparams(("parallel", "parallel", "arbitrary")),
    )(table, a, b)


def _mm_nn_pair(a0, a1, b, name, tm=512, tn=1024, ride=None):
    M, K = a0.shape
    _, N = b.shape
    tm = _pick(M, (tm, 256, 128))
    tn = _pick(N, (tn, 512, 256, 128))
    ni = M // tm

    def body(a0_ref, a1_ref, b_ref, o_ref):
        p = pl.program_id(0)

        @pl.when(p == 0)
        def _():
            o_ref[...] = _dot(a0_ref[...], b_ref[...]).astype(BF16)

        @pl.when(p == 1)
        def _():
            o_ref[...] = _dot(a1_ref[...], b_ref[...]).astype(BF16)

    return _pallas(
        body, (a0, a1, b), name=name, grid=(2, ni, N // tn),
        in_specs=[pl.BlockSpec((tm, K), lambda p, i, j: (i * (1 - p), 0)),
                  pl.BlockSpec((tm, K), lambda p, i, j: (i * p, 0)),
                  pl.BlockSpec((K, tn), lambda p, i, j: (0, j))],
        out_specs=pl.BlockSpec((tm, tn), lambda p, i, j: (p * ni + i, j)),
        out_shape=jax.ShapeDtypeStruct((2 * M, N), BF16),
        semantics=("parallel", "parallel", "parallel"), ride=ride)


def _rms_fn(x, w):
    r = lax.rsqrt(jnp.mean(x * x, axis=-1, keepdims=True) + EPS)
    return x * r * w


def _rms_in(x, w, ride=None):
    T, D = x.shape
    tm = _pick(T, (512, 256, 128))

    def body(x_ref, w_ref, o_ref, ot_ref):
        xn = _rms_fn(x_ref[...], w_ref[...])
        o_ref[...] = xn.astype(BF16)
        ot_ref[...] = xn.T.astype(BF16)

    return _pallas(
        body, (x, w), name="rms_in", grid=(T // tm,),
        in_specs=[pl.BlockSpec((tm, D), lambda i: (i, 0)), pl.BlockSpec((1, D), lambda i: (0, 0))],
        out_specs=[pl.BlockSpec((tm, D), lambda i: (i, 0)), pl.BlockSpec((D, tm), lambda i: (0, i))],
        out_shape=[jax.ShapeDtypeStruct((T, D), BF16), jax.ShapeDtypeStruct((D, T), BF16)],
        semantics=("parallel",), ride=ride)


def _rms_in_bwd(x, w, dxn, dh, ride=None):
    T, D = x.shape
    tm = _pick(T, (256, 128))

    def body(x_ref, w_ref, dxn_ref, dh_ref, gx_ref, dw_ref):
        _, vjp = jax.vjp(_rms_fn, x_ref[...], w_ref[...])
        dx, dw = vjp(dxn_ref[...])
        gx_ref[...] = dh_ref[...] + dx

        @pl.when(pl.program_id(0) == 0)
        def _():
            dw_ref[...] = dw

        @pl.when(pl.program_id(0) > 0)
        def _():
            dw_ref[...] += dw

    tile = pl.BlockSpec((tm, D), lambda i: (i, 0))
    row = pl.BlockSpec((1, D), lambda i: (0, 0))
    return _pallas(
        body, (x, w, dxn, dh), name="rms_in_bwd", grid=(T // tm,),
        in_specs=[tile, row, tile, tile], out_specs=[tile, row],
        out_shape=[jax.ShapeDtypeStruct((T, D), F32), jax.ShapeDtypeStruct((1, D), F32)],
        semantics=("arbitrary",), ride=ride)


def _conv_fwd(cat_ref, halo, x, w):
    tm = x.shape[0]
    cat_ref[0:HALO, :] = halo
    cat_ref[HALO:HALO + tm, :] = x
    c = x * w[CONV_WIDTH - 1:CONV_WIDTH, :]
    for k in range(CONV_WIDTH - 1):
        s = CONV_WIDTH - 1 - k
        c = c + cat_ref[pl.ds(HALO - s, tm), :] * w[k:k + 1, :]
    return c


def _lane_to_all(x, lane):
    @jax.custom_vjp
    def f(x):
        return jnp.broadcast_to(x[:, lane:lane + 1], x.shape)

    def f_fwd(x):
        return f(x), None

    def f_bwd(_, g):
        return (jnp.where(_iota(g.shape, 1) == lane, jnp.sum(g, axis=-1, keepdims=True), 0.0),)

    f.defvjp(f_fwd, f_bwd)
    return f(x)


def _gdn_pointwise(c, ba, alog, dtb, H):
    A = H * HEAD_DIM
    s = _silu(c)
    beta = _sigmoid(ba)
    g = -jnp.exp(alog) * _softplus(ba + dtb)
    qs, ks, vs, gbs, bbs = [], [], [], [], []
    for h in range(H):
        lo = h * HEAD_DIM
        q = s[:, lo:lo + HEAD_DIM]
        k = s[:, A + lo:A + lo + HEAD_DIM]
        qs.append(q * lax.rsqrt(jnp.sum(q * q, axis=-1, keepdims=True) + EPS))
        ks.append(k * lax.rsqrt(jnp.sum(k * k, axis=-1, keepdims=True) + EPS))
        vs.append(s[:, 2 * A + lo:2 * A + lo + HEAD_DIM])
        bbs.append(_lane_to_all(beta, h))
        gbs.append(_lane_to_all(g, H + h))
    st = lambda xs: jnp.stack(xs, axis=0)
    return st(qs), st(ks), st(vs), st(gbs), st(bbs)


def _halo_prev(tm):
    return lambda i: (jnp.maximum(i * (tm // HALO) - 1, 0), 0)


def _gdn_pre(proj_m, xn, w_own, conv_w, alog_row, dtb_row, H):
    T, n_main = proj_m.shape
    D = xn.shape[1]
    A = H * HEAD_DIM
    tm = _pick(T, (256, 128))
    hs = pl.BlockSpec((H, tm, HEAD_DIM), lambda i: (0, i, 0))
    hshape = jax.ShapeDtypeStruct((H, T, HEAD_DIM), F32)

    def body(x_ref, halo_ref, xn_ref, wba_ref, w_ref, al_ref, dt_ref,
             q_ref, k_ref, v_ref, gb_ref, bb_ref, ba_ref, cat_ref):
        halo = jnp.where(pl.program_id(0) == 0, 0.0, halo_ref[...])
        c = _conv_fwd(cat_ref, halo, x_ref[...], w_ref[...])
        ba = _dot(xn_ref[...], wba_ref[...])
        q, k, v, gb, bb = _gdn_pointwise(c, ba, al_ref[...], dt_ref[...], H)
        q_ref[...] = q
        k_ref[...] = k
        v_ref[...] = v
        gb_ref[...] = gb
        bb_ref[...] = bb
        ba_ref[...] = ba

    return pl.pallas_call(
        body, name="gdn_pre", grid=(T // tm,),
        in_specs=[pl.BlockSpec((tm, 3 * A), lambda i: (i, 0)),
                  pl.BlockSpec((HALO, 3 * A), _halo_prev(tm)),
                  pl.BlockSpec((tm, D), lambda i: (i, 0)),
                  pl.BlockSpec((D, LANES), lambda i: (0, n_main // LANES)),
                  pl.BlockSpec((CONV_WIDTH, 3 * A), lambda i: (0, 0)),
                  pl.BlockSpec((1, LANES), lambda i: (0, 0)),
                  pl.BlockSpec((1, LANES), lambda i: (0, 0))],
        out_specs=[hs] * 5 + [pl.BlockSpec((tm, LANES), lambda i: (i, 0))],
        out_shape=[hshape] * 5 + [jax.ShapeDtypeStruct((T, LANES), F32)],
        scratch_shapes=[pltpu.VMEM((HALO + tm, 3 * A), F32)],
        compiler_params=_cparams(("parallel",)),
    )(proj_m, proj_m, xn, w_own, conv_w, alog_row, dtb_row)


def _gdn_pre_bwd(proj_m, proj_ba, conv_w, alog_row, dtb_row, dq, dk, dv, dgb, dbb, H, dproj):
    T, n_main = proj_m.shape
    A = H * HEAD_DIM
    tm = _pick(T, (256, 128))
    hs = pl.BlockSpec((H, tm, HEAD_DIM), lambda i: (0, i, 0))
    row = pl.BlockSpec((1, LANES), lambda i: (0, 0))

    def body(x_ref, halo_ref, ba_ref, w_ref, al_ref, dt_ref, dq_ref, dk_ref, dv_ref, dgb_ref, dbb_ref, _,
             dc_ref, dba_ref, dal_ref, ddt_ref, cat_ref):
        halo = jnp.where(pl.program_id(0) == 0, 0.0, halo_ref[...])
        c = _conv_fwd(cat_ref, halo, x_ref[...], w_ref[...])
        _, vjp = jax.vjp(functools.partial(_gdn_pointwise, H=H), c, ba_ref[...], al_ref[...], dt_ref[...])
        dc, dba, dal, ddt = vjp((dq_ref[...], dk_ref[...], dv_ref[...], dgb_ref[...], dbb_ref[...]))
        dc_ref[...] = dc
        dba_ref[:, :LANES] = dba.astype(BF16)
        dba_ref[:, LANES:] = jnp.zeros((tm, WIN_BLOCK - LANES), BF16)

        @pl.when(pl.program_id(0) == 0)
        def _():
            dal_ref[...] = dal
            ddt_ref[...] = ddt

        @pl.when(pl.program_id(0) > 0)
        def _():
            dal_ref[...] += dal
            ddt_ref[...] += ddt

    return pl.pallas_call(
        body, name="gdn_pre_bwd", grid=(T // tm,),
        in_specs=[pl.BlockSpec((tm, 3 * A), lambda i: (i, 0)),
                  pl.BlockSpec((HALO, 3 * A), _halo_prev(tm)),
                  pl.BlockSpec((tm, LANES), lambda i: (i, 0)),
                  pl.BlockSpec((CONV_WIDTH, 3 * A), lambda i: (0, 0)),
                  row, row, hs, hs, hs, hs, hs, ANY],
        out_specs=[pl.BlockSpec((tm, 3 * A), lambda i: (i, 0)),
                   pl.BlockSpec((tm, WIN_BLOCK), lambda i: (i, n_main // WIN_BLOCK)), row, row],
        out_shape=[jax.ShapeDtypeStruct((T, 3 * A), F32), jax.ShapeDtypeStruct(dproj.shape, dproj.dtype),
                   jax.ShapeDtypeStruct((1, LANES), F32), jax.ShapeDtypeStruct((1, LANES), F32)],
        input_output_aliases={11: 1},
        scratch_shapes=[pltpu.VMEM((HALO + tm, 3 * A), F32)],
        compiler_params=_cparams(("arbitrary",)),
    )(proj_m, proj_m, proj_ba, conv_w, alog_row, dtb_row, dq, dk, dv, dgb, dbb, dproj)


def _conv_bwd(proj_m, dc, conv_w, H, dproj):
    T = proj_m.shape[0]
    A = H * HEAD_DIM
    tm = _pick(T, (256, 128))
    nt = T // tm

    def body(x_ref, halo_ref, dc_ref, nxt_ref, w_ref, _, dx_ref, dw_ref):
        i = pl.program_id(0)
        halo = jnp.where(i == 0, 0.0, halo_ref[...])
        xcat = jnp.concatenate([halo, x_ref[...]], axis=0)
        nxt = jnp.where(i == nt - 1, 0.0, nxt_ref[...])
        dc = dc_ref[...]
        dcat = jnp.concatenate([dc, nxt], axis=0)
        w = w_ref[...]
        dx = None
        rows = []
        for k in range(CONV_WIDTH):
            s = CONV_WIDTH - 1 - k
            ds = dcat if s == 0 else pltpu.roll(dcat, tm + HALO - s, 0)
            term = ds[:tm, :] * w[k:k + 1, :]
            dx = term if dx is None else dx + term
            xs = xcat if s == 0 else pltpu.roll(xcat, s, 0)
            rows.append(jnp.sum(dc * xs[HALO:, :], axis=0, keepdims=True))
        dx_ref[...] = dx.astype(BF16)
        dw = jnp.concatenate(rows + [jnp.zeros((HALO - CONV_WIDTH, 3 * A), F32)], axis=0)

        @pl.when(i == 0)
        def _():
            dw_ref[...] = dw

        @pl.when(i > 0)
        def _():
            dw_ref[...] += dw

    return pl.pallas_call(
        body, name="conv_bwd", grid=(nt,),
        in_specs=[pl.BlockSpec((tm, 3 * A), lambda i: (i, 0)),
                  pl.BlockSpec((HALO, 3 * A), _halo_prev(tm)),
                  pl.BlockSpec((tm, 3 * A), lambda i: (i, 0)),
                  pl.BlockSpec((HALO, 3 * A), lambda i: (jnp.minimum((i + 1) * (tm // HALO), T // HALO - 1), 0)),
                  pl.BlockSpec((CONV_WIDTH, 3 * A), lambda i: (0, 0)), ANY],
        out_specs=[pl.BlockSpec((tm, 3 * A), lambda i: (i, 0)),
                   pl.BlockSpec((HALO, 3 * A), lambda i: (0, 0))],
        out_shape=[jax.ShapeDtypeStruct(dproj.shape, dproj.dtype), jax.ShapeDtypeStruct((HALO, 3 * A), F32)],
        input_output_aliases={5: 0},
        compiler_params=_cparams(("arbitrary",)),
    )(proj_m, proj_m, dc, dc, conv_w, dproj)


CHUNK = 128
BLOCK = 64


def _b(x):
    return x.astype(BF16)


@jax.custom_vjp
def _bdot(a, b):
    return _dot(_b(a), _b(b))


def _bdot_f(a, b):
    return _bdot(a, b), (a, b)


def _bdot_b(res, g):
    a, b = res
    return _dot_nt(_b(g), _b(b)), _dot_tn(_b(a), _b(g))


_bdot.defvjp(_bdot_f, _bdot_b)


@jax.custom_vjp
def _bdot_nt(a, b):
    return _dot_nt(_b(a), _b(b))


def _bdot_nt_f(a, b):
    return _bdot_nt(a, b), (a, b)


def _bdot_nt_b(res, g):
    a, b = res
    return _dot(_b(g), _b(b)), _dot_tn(_b(g), _b(a))


_bdot_nt.defvjp(_bdot_nt_f, _bdot_nt_b)


@jax.custom_vjp
def _bdot_tn(a, b):
    return _dot_tn(_b(a), _b(b))


def _bdot_tn_f(a, b):
    return _bdot_tn(a, b), (a, b)


def _bdot_tn_b(res, g):
    a, b = res
    return _dot_nt(_b(b), _b(g)), _dot(_b(a), _b(g))


_bdot_tn.defvjp(_bdot_tn_f, _bdot_tn_b)


def _mask_matmul(m, x):
    hi = _b(x)
    r = x - hi.astype(F32)
    mid = _b(r)
    lo = _b(r - mid.astype(F32))
    return (_dot(m, lo) + _dot(m, mid)) + _dot(m, hi)


@jax.custom_vjp
def _mask_dot(m, mt, x):
    return _mask_matmul(m, x)


def _mask_dot_f(m, mt, x):
    return _mask_matmul(m, x), (m, mt)


def _mask_dot_b(res, g):
    m, mt = res
    return jnp.zeros_like(m), jnp.zeros_like(mt), _mask_matmul(mt, g)


_mask_dot.defvjp(_mask_dot_f, _mask_dot_b)

def _unit_lower_inverse(L):
    n = L.shape[-1]
    X = -L
    Q = X
    for _ in range(BLOCK.bit_length() - 2):
        X = _dot(_b(X), _b(X))
        Q = Q + X + _dot(_b(Q), _b(X))
    return (_iota((n, n), 0) == _iota((n, n), 1)).astype(F32) + Q


@jax.custom_vjp
def _known_inverse(L, P):
    return P


def _known_inverse_f(L, P):
    return P, P


def _known_inverse_b(P, g):
    n = P.shape[-1]
    Q = _b(P - (_iota((n, n), 0) == _iota((n, n), 1)).astype(F32))
    t = g + _dot_tn(Q, _b(g))
    return -(t + _dot_nt(_b(t), Q)), jnp.zeros_like(P)


_known_inverse.defvjp(_known_inverse_f, _known_inverse_b)


def _gdn_prep_fn(q, k, v, gb, bb, P_known=None):
    n = CHUNK
    row, col = _iota((n, n), 0), _iota((n, n), 1)
    same = (row // BLOCK) == (col // BLOCK)
    incl, strict = same & (row >= col), same & (row > col)
    bc = lambda m: jnp.broadcast_to(_b(m.astype(F32)), q.shape[:1] + (n, n))
    tril, triu, ones = bc(incl), bc(same & (row <= col)), bc(same)
    gc = _mask_dot(tril, triu, gb)
    gl = _mask_dot(ones, ones, gb)
    decay = jnp.where(incl, jnp.exp(jnp.where(incl, gc - jnp.swapaxes(gc, 1, 2), 0.0)), 0.0)
    kb = k * bb
    vb = v * bb
    qs = q * (HEAD_DIM ** -0.5)
    L = jnp.where(strict, _bdot_nt(kb, k) * decay, 0.0)
    P = _unit_lower_inverse(L) if P_known is None else _known_inverse(L, P_known)
    egc = jnp.exp(gc)
    u = _bdot(P, vb)
    w = _bdot(P, kb * egc)
    attn = jnp.where(incl, _bdot_nt(qs, k) * decay, 0.0)
    qg = qs * egc
    kdec = k * jnp.exp(gl - gc)
    eg = jnp.exp(gl).reshape(-1, n // BLOCK, BLOCK, LANES).sum(axis=2) * (1.0 / BLOCK)
    if P_known is None:
        return u, w, qg, kdec, attn, eg, P
    return u, w, qg, kdec, attn, eg


def _gdn_block_fn(S, qg, kdec, u, w, attn, eg, i):
    nblk = CHUNK // BLOCK
    v_new = u - _bdot(w, S)
    zeros = jnp.zeros_like(v_new)
    o = _bdot(qg, S) + _bdot(attn, jnp.concatenate([zeros] * i + [v_new] + [zeros] * (nblk - 1 - i), axis=1))
    return o, S * eg + _bdot_tn(kdec, v_new)


def _eg_spec(H, T, chunks, index_map, per_head):
    nblk = CHUNK // BLOCK
    block = (chunks, 1 if per_head else H, nblk, LANES)
    return pl.BlockSpec(block, index_map), jax.ShapeDtypeStruct((T // CHUNK, H, nblk, LANES), F32)


def _gdn_prep(q, k, v, gb, bb):
    H, T, _ = q.shape
    pb = _pick(T // CHUNK, (16, 8, 4, 2, 1))
    hs = pl.BlockSpec((1, CHUNK * pb, HEAD_DIM), lambda h, n: (h, n, 0))

    def body(q_ref, k_ref, v_ref, gb_ref, bb_ref, *out_refs):
        chunks = lambda ref: ref[0].reshape(pb, CHUNK, HEAD_DIM)
        outs = _gdn_prep_fn(chunks(q_ref), chunks(k_ref), chunks(v_ref), chunks(gb_ref), chunks(bb_ref))
        for i, (ref, val) in enumerate(zip(out_refs, outs)):
            if i == 5:
                ref[:, 0] = val
            else:
                ref[0] = val.reshape(pb * CHUNK, HEAD_DIM).astype(ref.dtype)

    kept = [F32, BF16, BF16, BF16, BF16, None, BF16]
    es, eshape = _eg_spec(H, T, pb, lambda h, n: (n, h, 0, 0), per_head=True)
    return pl.pallas_call(
        body, name="gdn_prep", grid=(H, T // (CHUNK * pb)),
        in_specs=[hs] * 5, out_specs=[es if dt is None else hs for dt in kept],
        out_shape=[eshape if dt is None else jax.ShapeDtypeStruct((H, T, HEAD_DIM), dt) for dt in kept],
        compiler_params=_cparams(("parallel", "parallel")),
    )(q, k, v, gb, bb)


def _gdn_prep_bwd(q, k, v, gb, bb, pinv, du, dw, dqg, dkd, dat, deg):
    H, T, _ = q.shape
    pb = _pick(T // CHUNK, (16, 8, 4, 2, 1))
    hs = pl.BlockSpec((1, CHUNK * pb, HEAD_DIM), lambda h, n: (h, n, 0))
    hshape = jax.ShapeDtypeStruct((H, T, HEAD_DIM), F32)

    def body(*refs):
        in_refs, p_ref, ct_refs, out_refs = refs[:5], refs[5], refs[6:12], refs[12:]
        chunks = lambda ref: ref[0].reshape(pb, CHUNK, HEAD_DIM)
        P = chunks(p_ref).astype(F32)
        _, vjp = jax.vjp(lambda *a: _gdn_prep_fn(*a, P_known=P), *[chunks(r) for r in in_refs])
        grads = vjp(tuple(chunks(r).astype(F32) for r in ct_refs[:5]) + (ct_refs[5][:, 0],))
        for ref, val in zip(out_refs, grads):
            ref[0] = val.reshape(pb * CHUNK, HEAD_DIM)

    es, _ = _eg_spec(H, T, pb, lambda h, n: (n, h, 0, 0), per_head=True)
    return pl.pallas_call(
        body, name="gdn_prep_bwd", grid=(H, T // (CHUNK * pb)),
        in_specs=[hs] * 11 + [es], out_specs=[hs] * 5, out_shape=[hshape] * 5,
        compiler_params=_cparams(("parallel", "parallel")),
    )(q, k, v, gb, bb, pinv, du, dw, dqg, dkd, dat, deg)


def _gdn_chain(qg, kd, u, w, attn, eg):
    H, T, _ = qg.shape
    N, nblk = T // CHUNK, CHUNK // BLOCK
    hs = pl.BlockSpec((H, CHUNK, HEAD_DIM), lambda n: (0, n, 0))
    ss = pl.BlockSpec((1, nblk, H, HEAD_DIM, HEAD_DIM), lambda n: (n, 0, 0, 0, 0))

    def body(qg_ref, kd_ref, u_ref, w_ref, at_ref, eg_ref, o_ref, sall_ref, s_ref):
        @pl.when(pl.program_id(0) == 0)
        def _():
            s_ref[...] = jnp.zeros_like(s_ref)

        S = s_ref[...]
        for i in range(nblk):
            r = pl.ds(i * BLOCK, BLOCK)
            sall_ref[0, i] = S
            o_ref[:, r, :], S = _gdn_block_fn(S, qg_ref[:, r, :], kd_ref[:, r, :], u_ref[:, r, :], w_ref[:, r, :],
                                              at_ref[:, r, :], eg_ref[0, :, i:i + 1, :], i)
        s_ref[...] = S

    es, _ = _eg_spec(H, T, 1, lambda n: (n, 0, 0, 0), per_head=False)
    return pl.pallas_call(
        body, name="gdn_chain", grid=(N,),
        in_specs=[hs] * 5 + [es], out_specs=[hs, ss],
        out_shape=[jax.ShapeDtypeStruct((H, T, HEAD_DIM), F32),
                   jax.ShapeDtypeStruct((N, nblk, H, HEAD_DIM, HEAD_DIM), F32)],
        scratch_shapes=[pltpu.VMEM((H, HEAD_DIM, HEAD_DIM), F32)],
        compiler_params=_cparams(("arbitrary",)),
    )(qg, kd, u, w, attn, eg)


def _gdn_chain_bwd(qg, kd, u, w, attn, eg, sall, do):
    H, T, _ = qg.shape
    N, nblk = T // CHUNK, CHUNK // BLOCK
    hs = pl.BlockSpec((H, CHUNK, HEAD_DIM), lambda n: (0, N - 1 - n, 0))
    ss = pl.BlockSpec((1, nblk, H, HEAD_DIM, HEAD_DIM), lambda n: (N - 1 - n, 0, 0, 0, 0))

    def body(qg_ref, kd_ref, u_ref, w_ref, at_ref, eg_ref, sall_ref, do_ref, *rest):
        out_refs, ds_ref = rest[:6], rest[6]

        @pl.when(pl.program_id(0) == 0)
        def _():
            ds_ref[...] = jnp.zeros_like(ds_ref)

        dS = ds_ref[...]
        for i in reversed(range(nblk)):
            r = pl.ds(i * BLOCK, BLOCK)
            f32 = lambda ref: ref[:, r, :].astype(F32)
            _, vjp = jax.vjp(functools.partial(_gdn_block_fn, i=i), sall_ref[0, i], f32(qg_ref), f32(kd_ref),
                             u_ref[:, r, :], f32(w_ref), f32(at_ref), eg_ref[0, :, i:i + 1, :])
            grads = vjp((do_ref[:, r, :], dS))
            dS = grads[0]
            for ref, val in zip(out_refs[:5], grads[1:6]):
                ref[:, r, :] = val.astype(ref.dtype)
            out_refs[5][0, :, i:i + 1, :] = grads[6]
        ds_ref[...] = dS

    kept = [F32, F32, BF16, BF16, F32]
    es, eshape = _eg_spec(H, T, 1, lambda n: (N - 1 - n, 0, 0, 0), per_head=False)
    return pl.pallas_call(
        body, name="gdn_chain_bwd", grid=(N,),
        in_specs=[hs] * 5 + [es, ss, hs], out_specs=[hs] * 5 + [es],
        out_shape=[jax.ShapeDtypeStruct((H, T, HEAD_DIM), dt) for dt in kept] + [eshape],
        scratch_shapes=[pltpu.VMEM((H, HEAD_DIM, HEAD_DIM), F32)],
        compiler_params=_cparams(("arbitrary",)),
    )(qg, kd, u, w, attn, eg, sall, do)


def _post_fn(ogs, za, hw):
    outs = []
    for h, o in enumerate(ogs):
        r = lax.rsqrt(jnp.mean(o * o, axis=-1, keepdims=True) + EPS)
        outs.append(o * r * hw * _silu(za[:, h * HEAD_DIM:(h + 1) * HEAD_DIM]))
    return jnp.concatenate(outs, axis=1)


def _gdn_post(og, proj_m, hw):
    H, T, _ = og.shape
    A = H * HEAD_DIM
    tm = _pick(T, (512, 256, 128))

    def body(og_ref, za_ref, hw_ref, o_ref, ot_ref):
        o = _post_fn(tuple(og_ref[h] for h in range(H)), za_ref[...], hw_ref[...])
        o_ref[...] = o.astype(BF16)
        ot_ref[...] = o.T.astype(BF16)

    return pl.pallas_call(
        body, name="gdn_post", grid=(T // tm,),
        in_specs=[pl.BlockSpec((H, tm, HEAD_DIM), lambda i: (0, i, 0)),
                  pl.BlockSpec((tm, A), lambda i: (i, ZA_BLOCK)),
                  pl.BlockSpec((1, HEAD_DIM), lambda i: (0, 0))],
        out_specs=[pl.BlockSpec((tm, A), lambda i: (i, 0)), pl.BlockSpec((A, tm), lambda i: (0, i))],
        out_shape=[jax.ShapeDtypeStruct((T, A), BF16), jax.ShapeDtypeStruct((A, T), BF16)],
        compiler_params=_cparams(("parallel",)),
    )(og, proj_m, hw)


def _gdn_post_bwd(og, proj_m, hw, d_o, dproj):
    H, T, _ = og.shape
    A = H * HEAD_DIM
    tm = _pick(T, (256, 128))

    def body(og_ref, za_ref, hw_ref, do_ref, _, dog_ref, dza_ref, dhw_ref):
        _, vjp = jax.vjp(_post_fn, tuple(og_ref[h] for h in range(H)), za_ref[...], hw_ref[...])
        dog, dza, dhw = vjp(do_ref[...])
        for h in range(H):
            dog_ref[h] = dog[h]
        dza_ref[...] = dza.astype(BF16)

        @pl.when(pl.program_id(0) == 0)
        def _():
            dhw_ref[...] = dhw

        @pl.when(pl.program_id(0) > 0)
        def _():
            dhw_ref[...] += dhw

    return pl.pallas_call(
        body, name="gdn_post_bwd", grid=(T // tm,),
        in_specs=[pl.BlockSpec((H, tm, HEAD_DIM), lambda i: (0, i, 0)),
                  pl.BlockSpec((tm, A), lambda i: (i, ZA_BLOCK)),
                  pl.BlockSpec((1, HEAD_DIM), lambda i: (0, 0)),
                  pl.BlockSpec((tm, A), lambda i: (i, 0)), ANY],
        out_specs=[pl.BlockSpec((H, tm, HEAD_DIM), lambda i: (0, i, 0)),
                   pl.BlockSpec((tm, A), lambda i: (i, ZA_BLOCK)),
                   pl.BlockSpec((1, HEAD_DIM), lambda i: (0, 0))],
        out_shape=[jax.ShapeDtypeStruct((H, T, HEAD_DIM), F32), jax.ShapeDtypeStruct(dproj.shape, dproj.dtype),
                   jax.ShapeDtypeStruct((1, HEAD_DIM), F32)],
        input_output_aliases={4: 1},
        compiler_params=_cparams(("arbitrary",)),
    )(og, proj_m, hw, d_o, dproj)


def _sgu_fn(ub, vb, zb, lw, lb, W, bbc):
    G = len(W)
    tm = ub.shape[0]
    mu = jnp.mean(vb, axis=-1, keepdims=True)
    xc = vb - mu
    var = jnp.mean(xc * xc, axis=-1, keepdims=True)
    vn = xc * lax.rsqrt(var + EPS) * lw + lb
    mask = _iota((CHUNK_B, CHUNK_B), 0) >= _iota((CHUNK_B, CHUNK_B), 1)
    cols = []
    for g in range(G):
        wm = jnp.where(mask, W[g], 0.0).astype(BF16)
        rows = []
        for c in range(tm // CHUNK_B):
            blk = vn[c * CHUNK_B:(c + 1) * CHUNK_B, g * HEAD_DIM:(g + 1) * HEAD_DIM].astype(BF16)
            rows.append(_dot(wm, blk) + bbc[g])
        cols.append(jnp.concatenate(rows, axis=0) if len(rows) > 1 else rows[0])
    s = jnp.concatenate(cols, axis=1)
    return ub * s * _silu(zb)


ZA_BLOCK = 6


def _sgu_cols(A, B):
    assert A == B
    return 3, 4, 5


def _sgu_fwd(proj_m, lw, lb, W, bbc, A):
    T = proj_m.shape[0]
    G = W.shape[0]
    B = G * HEAD_DIM
    tm = _pick(T, (256, 128))
    cu, cv, cz = _sgu_cols(A, B)

    def body(u_ref, v_ref, z_ref, lw_ref, lb_ref, w_ref, b_ref, o_ref, ot_ref):
        o = _sgu_fn(u_ref[...], v_ref[...], z_ref[...], lw_ref[...], lb_ref[...],
                    tuple(w_ref[g] for g in range(G)), tuple(b_ref[g] for g in range(G)))
        o_ref[...] = o.astype(BF16)
        ot_ref[...] = o.T.astype(BF16)

    row = pl.BlockSpec((1, B), lambda i: (0, 0))
    cube = pl.BlockSpec((G, CHUNK_B, CHUNK_B), lambda i: (0, 0, 0))
    return pl.pallas_call(
        body, name="sgu_fwd", grid=(T // tm,),
        in_specs=[pl.BlockSpec((tm, B), lambda i: (i, cu)), pl.BlockSpec((tm, B), lambda i: (i, cv)),
                  pl.BlockSpec((tm, B), lambda i: (i, cz)), row, row, cube, cube],
        out_specs=[pl.BlockSpec((tm, B), lambda i: (i, 0)), pl.BlockSpec((B, tm), lambda i: (0, i))],
        out_shape=[jax.ShapeDtypeStruct((T, B), BF16), jax.ShapeDtypeStruct((B, T), BF16)],
        compiler_params=_cparams(("parallel",)),
    )(proj_m, proj_m, proj_m, lw, lb, W, bbc)


def _sgu_bwd(proj_m, lw, lb, W, bbc, d_o, A, dproj):
    T = proj_m.shape[0]
    G = W.shape[0]
    B = G * HEAD_DIM
    tm = _pick(T, (256, 128))
    nt = T // tm
    cu, cv, cz = _sgu_cols(A, B)

    def body(u_ref, v_ref, z_ref, lw_ref, lb_ref, w_ref, b_ref, do_ref, _,
             dp_ref, dlw_ref, dlb_ref, dw_ref, db_ref, dbb_ref):
        _, vjp = jax.vjp(_sgu_fn, u_ref[...], v_ref[...], z_ref[...], lw_ref[...], lb_ref[...],
                         tuple(w_ref[g] for g in range(G)), tuple(b_ref[g] for g in range(G)))
        du, dv, dz, dlw, dlb, dW, dbb = vjp(do_ref[...])
        dW, dbb = jnp.stack(dW, axis=0), jnp.stack(dbb, axis=0)
        dp_ref[:, 0:B] = du.astype(BF16)
        dp_ref[:, B:2 * B] = dv.astype(BF16)
        dp_ref[:, 2 * B:3 * B] = dz.astype(BF16)
        i = pl.program_id(0)

        @pl.when(i == 0)
        def _():
            dlw_ref[...] = dlw
            dlb_ref[...] = dlb
            dw_ref[...] = dW
            dbb_ref[...] = dbb

        @pl.when(i > 0)
        def _():
            dlw_ref[...] += dlw
            dlb_ref[...] += dlb
            dw_ref[...] += dW
            dbb_ref[...] += dbb

        @pl.when(i == nt - 1)
        def _():
            db_ref[...] = jnp.sum(dbb_ref[...], axis=-1, keepdims=True)

    row = pl.BlockSpec((1, B), lambda i: (0, 0))
    cube = pl.BlockSpec((G, CHUNK_B, CHUNK_B), lambda i: (0, 0, 0))
    return pl.pallas_call(
        body, name="sgu_bwd", grid=(nt,),
        in_specs=[pl.BlockSpec((tm, B), lambda i: (i, cu)), pl.BlockSpec((tm, B), lambda i: (i, cv)),
                  pl.BlockSpec((tm, B), lambda i: (i, cz)), row, row, cube, cube,
                  pl.BlockSpec((tm, B), lambda i: (i, A // B)), ANY],
        out_specs=[pl.BlockSpec((tm, 3 * B), lambda i: (i, 1)), row, row, cube,
                   pl.BlockSpec((G, CHUNK_B, 1), lambda i: (0, 0, 0))],
        out_shape=[jax.ShapeDtypeStruct(dproj.shape, dproj.dtype), jax.ShapeDtypeStruct((1, B), F32),
                   jax.ShapeDtypeStruct((1, B), F32), jax.ShapeDtypeStruct((G, CHUNK_B, CHUNK_B), F32),
                   jax.ShapeDtypeStruct((G, CHUNK_B, 1), F32)],
        input_output_aliases={8: 0},
        scratch_shapes=[pltpu.VMEM((G, CHUNK_B, CHUNK_B), F32)],
        compiler_params=_cparams(("arbitrary",)),
    )(proj_m, proj_m, proj_m, lw, lb, W, bbc, d_o, dproj)


def _head_fn(mix, x, fw, tgt):
    h = x + mix
    y = _rms_fn(h, fw)
    e = y - tgt
    return 0.5 * jnp.sum(jnp.mean(e * e, axis=-1, keepdims=True), axis=0, keepdims=True)


def _out_proj_loss(oa, ob, wout, x, tgt, fw):
    T, A = oa.shape
    B = ob.shape[1]
    D = x.shape[1]
    tm = _pick(T, (256, 128))

    def body(oa_ref, ob_ref, w_ref, x_ref, t_ref, fw_ref, dh_ref, dhb_ref, loss_ref, dfw_ref):
        mix = _dot(oa_ref[...], w_ref[0:A, :]) + _dot(ob_ref[...], w_ref[A:A + B, :])
        xv, tv = x_ref[...], t_ref[...]
        loss, vjp = jax.vjp(lambda m, f: _head_fn(m, xv, f, tv), mix, fw_ref[...])
        dh, dfw = vjp(jnp.ones((1, 1), F32))
        dh_ref[...] = dh
        dhb_ref[...] = dh.astype(BF16)
        lrow = jnp.broadcast_to(loss, (1, LANES))

        @pl.when(pl.program_id(0) == 0)
        def _():
            loss_ref[...] = lrow
            dfw_ref[...] = dfw

        @pl.when(pl.program_id(0) > 0)
        def _():
            loss_ref[...] += lrow
            dfw_ref[...] += dfw

    tile = pl.BlockSpec((tm, D), lambda i: (i, 0))
    return pl.pallas_call(
        body, name="out_proj_loss", grid=(T // tm,),
        in_specs=[pl.BlockSpec((tm, A), lambda i: (i, 0)), pl.BlockSpec((tm, B), lambda i: (i, 0)),
                  pl.BlockSpec((A + B, D), lambda i: (0, 0)), tile, tile,
                  pl.BlockSpec((1, D), lambda i: (0, 0))],
        out_specs=[tile, tile, pl.BlockSpec((1, LANES), lambda i: (0, 0)),
                   pl.BlockSpec((1, D), lambda i: (0, 0))],
        out_shape=[jax.ShapeDtypeStruct((T, D), F32), jax.ShapeDtypeStruct((T, D), BF16),
                   jax.ShapeDtypeStruct((1, LANES), F32), jax.ShapeDtypeStruct((1, D), F32)],
        compiler_params=_cparams(("arbitrary",)),
    )(oa, ob, wout, x, tgt, fw)


def _adamw(w, g, m, v, name):
    R, Cn = w.shape
    cap = max(8, 512 * 1024 // Cn)
    tr = max(t for t in range(8, min(R, cap) + 1, 8) if R % t == 0) if R > cap else R

    def body(w_ref, g_ref, m_ref, v_ref, d_ref, mo_ref, vo_ref):
        g = g_ref[...]
        m = ADAM_B1 * m_ref[...] + (1.0 - ADAM_B1) * g
        v = ADAM_B2 * v_ref[...] + (1.0 - ADAM_B2) * jnp.square(g)
        m_hat = m / (1.0 - ADAM_B1 ** ADAM_STEP)
        v_hat = v / (1.0 - ADAM_B2 ** ADAM_STEP)
        d_ref[...] = -ADAM_LR * (m_hat / (jnp.sqrt(v_hat) + ADAM_EPS) + ADAM_WD * w_ref[...])
        mo_ref[...] = m
        vo_ref[...] = v

    tile = pl.BlockSpec((tr, Cn), lambda i: (i, 0))
    shape = jax.ShapeDtypeStruct((R, Cn), F32)
    return pl.pallas_call(
        body, name=name, grid=(R // tr,), in_specs=[tile] * 4, out_specs=[tile] * 3,
        out_shape=[shape] * 3, compiler_params=_cparams(("parallel",)),
    )(w, g, m, v)


def _place():
    x, y, c = lax.axis_index("x"), lax.axis_index("y"), lax.axis_index("c")
    others = [(1 - x, y), (x, 1 - y), (1 - x, 1 - y)]
    return x, y, c, others


def _chip_index(px, py):
    return 2 * px + py


ANY = pl.BlockSpec(memory_space=pl.ANY)


def _gather_ride(blocks, split):
    n = len(blocks)

    def plan(in_refs, out_refs, send_sems, recv_sems):
        x, y, c, _ = _place()
        me, kx, ky, kd = (_chip_index(px, py) for px, py in ((x, y), (1 - x, y), (x, 1 - y), (1 - x, 1 - y)))
        to_x, to_y, to_s = (1 - x, y, c), (x, 1 - y, c), (x, y, 1 - c)

        def copy(sem, src, dst, to):
            return pltpu.make_async_remote_copy(src_ref=src, dst_ref=dst, send_sem=send_sems.at[sem],
                                                recv_sem=recv_sems.at[sem], device_id=to, device_id_type=MESH_ID)

        first, second, third, awaited = [], [], [], []
        for a in range(n):
            out, s0 = out_refs[a], 8 * a
            if not split[a]:
                for j, (k, to) in enumerate(((kx, to_x), (ky, to_y), (kd, (1 - x, 1 - y, c)))):
                    first.append(lambda j=j, to=to, a=a, out=out, s0=s0: copy(s0 + j, in_refs[a], out.at[me], to))
                    awaited.append((lambda j=j, k=k, to=to, out=out, s0=s0: copy(s0 + j, out.at[k], out.at[k], to),
                                    None))
                continue
            h = blocks[a].shape[0] // 2
            q = h // 2
            half = lambda k, core, out=out, h=h: out.at[k, pl.ds(core * h, h), :]
            quarter = lambda k, core, i, out=out, h=h, q=q: out.at[k, pl.ds(core * h + i * q, q), :]
            mine = in_refs[a].at[pl.ds(c * h, h), :]
            first.append(lambda s0=s0, mine=mine, half=half: copy(s0, mine, half(me, c), to_x))
            first.append(lambda s0=s0, mine=mine, half=half: copy(s0 + 1, mine, half(me, c), to_y))
            fwd0 = lambda s0=s0, quarter=quarter: copy(s0 + 2, quarter(kx, c, 0), quarter(kx, c, 0), to_y)
            fwd1 = lambda s0=s0, quarter=quarter: copy(s0 + 3, quarter(ky, c, 1), quarter(ky, c, 1), to_x)
            pieces = [(s0 + 0, lambda half=half: half(kx, c), lambda half=half: half(kx, 1 - c), to_x, fwd0),
                      (s0 + 1, lambda half=half: half(ky, c), lambda half=half: half(ky, 1 - c), to_y, fwd1),
                      (s0 + 2, lambda quarter=quarter: quarter(kd, c, 0), lambda quarter=quarter: quarter(kd, 1 - c, 0),
                       to_y, None),
                      (s0 + 3, lambda quarter=quarter: quarter(kd, c, 1), lambda quarter=quarter: quarter(kd, 1 - c, 1),
                       to_x, None)]
            for i, (sem, here, there, frm, fwd) in enumerate(pieces):
                passing = lambda s0=s0, i=i, here=here: copy(s0 + 4 + i, here(), here(), to_s)
                awaited.append((lambda sem=sem, here=here, frm=frm: copy(sem, here(), here(), frm), (fwd, passing)))
                if fwd is not None:
                    second.append(fwd)
                third.append((passing, lambda s0=s0, i=i, there=there: copy(s0 + 4 + i, there(), there(), to_s)))
        return first, second, third, awaited

    def start(*refs):
        for send in plan(*refs)[0]:
            send().start()

    def finish(*refs):
        first, second, third, awaited = plan(*refs)
        for arrival, then in awaited:
            arrival().wait_recv()
            for nxt in (then or ()):
                if nxt is not None:
                    nxt().start()
        for _, from_sibling in third:
            from_sibling().wait_recv()
        for send in first + second + [p for p, _ in third]:
            send().wait_send()

    shapes = [jax.ShapeDtypeStruct((N_CHIPS,) + b.shape, b.dtype) for b in blocks]
    return _Ride(blocks, shapes, 8 * n, start, finish)


def _put_own(gathered, own):
    me = _chip_index(lax.axis_index("x"), lax.axis_index("y"))
    return lax.dynamic_update_index_in_dim(gathered, own, me, 0)


def _allreduce_small(buf):
    R0, L = buf.shape
    R = -(-R0 // 16) * 16
    h = R // 2
    buf = jnp.pad(buf, ((0, R - R0), (0, 0)))

    def body(in_ref, out_ref, sib_ref, pair_ref, chips_ref, send_sems, recv_sems):
        x, y, c, others = _place()
        me = _chip_index(x, y)
        sibling = (x, y, 1 - c)

        def copy(sem, src, dst, to):
            return pltpu.make_async_remote_copy(src_ref=src, dst_ref=dst, send_sem=send_sems.at[sem],
                                                recv_sem=recv_sems.at[sem], device_id=to, device_id_type=MESH_ID)

        cp = copy(0, in_ref, sib_ref, sibling)
        cp.start()
        cp.wait()
        pair_ref[...] = in_ref[...] + sib_ref[...]
        rows = lambda core: pl.ds(pl.multiple_of(core * h, 8), h)
        sends = [copy(1 + j, pair_ref.at[rows(c), :], chips_ref.at[me], (*chip, c)) for j, chip in enumerate(others)]
        for s in sends:
            s.start()
        chips_ref[me] = pair_ref[rows(c), :]
        for j, chip in enumerate(others):
            k = _chip_index(*chip)
            copy(1 + j, chips_ref.at[k], chips_ref.at[k], (*chip, c)).wait_recv()
        out_ref[rows(c), :] = ((chips_ref[0] + chips_ref[1]) + chips_ref[2]) + chips_ref[3]
        swap = copy(4, out_ref.at[rows(c), :], out_ref.at[rows(c), :], sibling)
        swap.start()
        copy(4, out_ref.at[rows(1 - c), :], out_ref.at[rows(1 - c), :], sibling).wait_recv()
        for s in sends + [swap]:
            s.wait_send()

    vm = pl.BlockSpec(memory_space=pltpu.VMEM)
    return pl.pallas_call(
        body, name="allreduce_small", in_specs=[vm], out_specs=vm,
        out_shape=jax.ShapeDtypeStruct((R, L), F32),
        scratch_shapes=[pltpu.VMEM((R, L), F32), pltpu.VMEM((R, L), F32), pltpu.VMEM((N_CHIPS, h, L), F32),
                        pltpu.SemaphoreType.DMA((5,)), pltpu.SemaphoreType.DMA((5,))],
        compiler_params=pltpu.CompilerParams(vmem_limit_bytes=VMEM_LIMIT),
    )(buf)[:R0]


def _pair_ride(g):
    nb, R, Cn = g.shape
    h = R // 2

    def copy(in_refs, out_refs, send_sems, recv_sems):
        x, y, c, _ = _place()
        return pltpu.make_async_remote_copy(src_ref=in_refs[0].at[:, pl.ds((1 - c) * h, h), :], dst_ref=out_refs[0],
                                            send_sem=send_sems.at[0], recv_sem=recv_sems.at[0],
                                            device_id=(x, y, 1 - c), device_id_type=MESH_ID)

    return _Ride([g], [jax.ShapeDtypeStruct((nb, h, Cn), g.dtype)], 1,
                 lambda *refs: copy(*refs).start(), lambda *refs: copy(*refs).wait())


def _pair_sum(g, land, c_arr, name, ride=None):
    nb, R, Cn = g.shape
    hr = R // 2
    tr = _pick(hr, (256, 128, 64, 32, 16))
    nt = hr // tr

    def body(c_ref, g_ref, l_ref, o_ref):
        o_ref[...] = (g_ref[...].astype(F32) + l_ref[...].astype(F32)).astype(BF16)

    return _pallas(
        body, (c_arr, g, land), name=name, prefetch=1, grid=(nb, nt),
        in_specs=[pl.BlockSpec((1, tr, Cn), lambda b, i, c_ref: (b, c_ref[0] * nt + i, 0)),
                  pl.BlockSpec((1, tr, Cn), lambda b, i, c_ref: (b, i, 0))],
        out_specs=pl.BlockSpec((1, tr, Cn), lambda b, i, c_ref: (b, i, 0)),
        out_shape=jax.ShapeDtypeStruct((nb, hr, Cn), BF16),
        semantics=("parallel", "parallel"), ride=ride)


def _chip_ride(parts, cols=None):
    m = len(parts)

    def copies(in_refs, out_refs, send_sems, recv_sems):
        x, y, c, others = _place()
        me = _chip_index(x, y)
        def mk(j, chip, n, landing):
            k = _chip_index(*chip)
            src = in_refs[n].at[k]
            if cols is not None:
                src = src.at[:, pl.ds(pl.multiple_of(cols[0](k), LANES), cols[1])]
            return pltpu.make_async_remote_copy(
                src_ref=src, dst_ref=out_refs[n].at[landing(k)], send_sem=send_sems.at[m * j + n],
                recv_sem=recv_sems.at[m * j + n], device_id=(*chip, c), device_id_type=MESH_ID)

        pairs = [(j, chip, n) for j, chip in enumerate(others) for n in range(m)]
        return pairs, (lambda *p: mk(*p, lambda k: me)), (lambda *p: mk(*p, lambda k: k))

    def start(*refs):
        pairs, send, _ = copies(*refs)
        for p in pairs:
            send(*p).start()

    def finish(*refs):
        pairs, send, arrival = copies(*refs)
        for p in pairs:
            arrival(*p).wait_recv()
        for p in pairs:
            send(*p).wait_send()

    width = lambda p: p.shape[2] if cols is None else cols[1]
    return _Ride(parts, [jax.ShapeDtypeStruct(p.shape[:2] + (width(p),), p.dtype) for p in parts], 3 * m,
                 start, finish)


def _put_own_slot(q, p, cols=None):
    me = _chip_index(lax.axis_index("x"), lax.axis_index("y"))
    own = lax.dynamic_index_in_dim(p, me, 0, keepdims=False)
    if cols is not None:
        own = lax.dynamic_slice_in_dim(own, cols[0](me), cols[1], axis=1)
    return lax.dynamic_update_index_in_dim(q, own, me, 0)


def _chip_sum(q, c_arr, name):
    nb, hr, Cn = q.shape
    tr = _pick(hr, (256, 128, 64, 32, 16))
    nt = hr // tr

    def body(c_ref, q_ref, o_ref):
        f = lambda k: q_ref[k].astype(F32)
        o_ref[...] = ((f(0) + f(1)) + f(2)) + f(3)

    return _pallas(
        body, (c_arr, q), name=name, prefetch=1, grid=(nt,),
        in_specs=[pl.BlockSpec((nb, tr, Cn), lambda i, c_ref: (0, i, 0))],
        out_specs=pl.BlockSpec((tr, Cn), lambda i, c_ref: (c_ref[0] * nt + i, 0)),
        out_shape=jax.ShapeDtypeStruct((2 * hr, Cn), F32),
        semantics=("parallel",))


def _sibling_fill(fw, fo):
    def body(_, __, fw_ref, fo_ref, send_sems, recv_sems):
        x, y, c, _ = _place()
        copies = []
        for n, ref in enumerate((fw_ref, fo_ref)):
            h = ref.shape[0] // 2
            mine = ref.at[pl.ds(c * h, h), :]
            theirs = ref.at[pl.ds((1 - c) * h, h), :]
            mk = lambda src, dst: pltpu.make_async_remote_copy(
                src_ref=src, dst_ref=dst, send_sem=send_sems.at[n], recv_sem=recv_sems.at[n],
                device_id=(x, y, 1 - c), device_id_type=MESH_ID)
            send = mk(mine, mine)
            send.start()
            copies.append((send, mk(theirs, theirs)))
        for send, arrival in copies:
            arrival.wait_recv()
            send.wait_send()

    return pl.pallas_call(
        body, name="sibling_fill", in_specs=[ANY, ANY], out_specs=[ANY, ANY],
        out_shape=[jax.ShapeDtypeStruct(fw.shape, F32), jax.ShapeDtypeStruct(fo.shape, F32)],
        input_output_aliases={0: 0, 1: 1},
        scratch_shapes=[pltpu.SemaphoreType.DMA((2,)), pltpu.SemaphoreType.DMA((2,))],
        compiler_params=pltpu.CompilerParams(has_side_effects=True),
    )(fw, fo)


class _Layout:
    def __init__(self, H, G, nb, Cb):
        A, B = H * HEAD_DIM, G * HEAD_DIM
        self.n_main = 4 * A + 3 * B
        self.k = -(-(self.n_main + LANES) // WIN_BLOCK) * WIN_BLOCK
        cuts = [0, 3 * A, 4 * A, 4 * A + 2 * H, nb * Cb]
        starts = [0, 3 * A + 3 * B, self.n_main, 3 * A]
        self.pieces = []
        self.windows, self.runs = [], []
        for n in range(nb):
            segs = []
            for s in range(4):
                lo, hi = max(cuts[s], n * Cb), min(cuts[s + 1], (n + 1) * Cb)
                if lo < hi:
                    segs.append((starts[s] + lo - cuts[s], lo - n * Cb, hi - lo))
            self.pieces += [(own, n, col, ln) for own, col, ln in segs]
            blocks = sorted({b for own, _, ln in segs for b in range(own // WIN_BLOCK, (own + ln - 1) // WIN_BLOCK + 1)})
            self.windows.append(blocks)
            self.runs.append([(blocks.index(own // WIN_BLOCK) * WIN_BLOCK + own % WIN_BLOCK, ln)
                              for own, _, ln in segs])
        self.wb = max(len(b) for b in self.windows)
        self.table = [b + [b[-1]] * (self.wb - len(b)) for b in self.windows]
        self.pieces.sort()
        self.used_from = [min(c for c, _ in r) // LANES * LANES for r in self.runs]
        self.used = max(-(-max(c + ln for c, ln in r) // LANES) * LANES - f for r, f in zip(self.runs, self.used_from))
        self.used_from = [min(f, self.wb * WIN_BLOCK - self.used) for f in self.used_from]

    def to_own_order(self, g_in):
        nb, D, Cb = g_in.shape
        tr = _pick(D, (256, 128))

        def body(g_ref, o_ref):
            cols, at = [], 0
            for own, n, col, ln in self.pieces:
                if own > at:
                    cols.append(jnp.zeros((tr, own - at), g_in.dtype))
                cols.append(g_ref[n, :, col:col + ln])
                at = own + ln
            if at < self.k:
                cols.append(jnp.zeros((tr, self.k - at), g_in.dtype))
            o_ref[...] = jnp.concatenate(cols, axis=1)

        return pl.pallas_call(
            body, name="own_order", grid=(D // tr,),
            in_specs=[pl.BlockSpec((nb, tr, Cb), lambda i: (0, i, 0))],
            out_specs=pl.BlockSpec((tr, self.k), lambda i: (i, 0)),
            out_shape=jax.ShapeDtypeStruct((D, self.k), g_in.dtype),
            compiler_params=_cparams(("parallel",)),
        )(g_in)

    def from_window(self, win, chip, Cb):
        pick = lambda runs, f: (lambda w: jnp.concatenate([w[:, c - f:c - f + ln] for c, ln in runs], axis=1))
        return lax.switch(chip, [pick(r, f) for r, f in zip(self.runs, self.used_from)], win)

    def used_start(self, chip):
        return sum(jnp.where(chip == n, f, 0) for n, f in enumerate(self.used_from))


def _device_step(x, tgt, norm_w, win_b, wout_b, conv_b, a_log, dt_bias, head_norm_w, sgu_ln_w, sgu_ln_b,
                 w_spatial, b_spatial, final_norm_w, c_arr):
    T, D = x.shape
    H = a_log.shape[1]
    A = H * HEAD_DIM
    G = w_spatial.shape[0]
    B = G * HEAD_DIM
    nb, Cb, Rb = N_CHIPS, win_b.shape[1], wout_b.shape[0]
    lay = _Layout(H, G, nb, Cb)
    alog_row = jnp.pad(a_log, ((0, 0), (H, LANES - 2 * H)))
    dtb_row = jnp.pad(dt_bias, ((0, 0), (H, LANES - 2 * H)))
    bbc = jnp.broadcast_to(b_spatial[:, :, None], (G, CHUNK_B, CHUNK_B))

    (xn, xn_t), (g_in,) = _rms_in(x, norm_w, ride=_gather_ride([win_b], [True]))
    w_own = lay.to_own_order(_put_own(g_in, win_b))
    proj_m, (g_out, g_conv) = _mm_nn(xn, w_own, F32, "in_proj", tm=2048, cols=(0, lay.n_main),
                                     ride=_gather_ride([wout_b, conv_b], [False, False]))
    wout = _put_own(g_out, wout_b).reshape(nb * Rb, D)
    conv_w = _put_own(g_conv, conv_b).transpose(1, 0, 2).reshape(CONV_WIDTH, nb * conv_b.shape[1])
    q, k, v, gb, bb, proj_ba = _gdn_pre(proj_m, xn, w_own, conv_w, alog_row, dtb_row, H)
    u, w, qg, kd, attn, eg, pinv = _gdn_prep(q, k, v, gb, bb)
    og, sall = _gdn_chain(qg, kd, u, w, attn, eg)
    oa, oa_t = _gdn_post(og, proj_m, head_norm_w)
    ob, ob_t = _sgu_fwd(proj_m, sgu_ln_w, sgu_ln_b, w_spatial, bbc, A)
    dh, dhb, loss_row, d_fnw = _out_proj_loss(oa, ob, wout, x, tgt, final_norm_w.reshape(1, D))

    d_o = _mm_nn(dhb, wout.T, F32, "out_proj_dx", tm=2048)
    dproj = lax.empty((T, lay.k), BF16)
    dproj, d_lw, d_lb, d_ws, d_bs = _sgu_bwd(proj_m, sgu_ln_w, sgu_ln_b, w_spatial, bbc, d_o, A, dproj)
    dog, dproj, d_hw = _gdn_post_bwd(og, proj_m, head_norm_w, d_o, dproj)
    dqg, dkd, du, dw, dat, deg = _gdn_chain_bwd(qg, kd, u, w, attn, eg, sall, dog)
    dq, dk, dv, dgb, dbb = _gdn_prep_bwd(q, k, v, gb, bb, pinv, du, dw, dqg, dkd, dat, deg)
    dc, dproj, d_al, d_dt = _gdn_pre_bwd(proj_m, proj_ba, conv_w, alog_row, dtb_row, dq, dk, dv, dgb, dbb, H,
                                         dproj)
    dproj, d_conv = _conv_bwd(proj_m, dc, conv_w, H, dproj)

    table = jnp.array([b for row in lay.table for b in row], jnp.int32)
    d_win = _mm_windows(xn_t, dproj, table, nb, "in_proj_dw")
    d_wout, (land_w,) = _mm_nn_pair(oa_t, ob_t, dhb, "out_proj_dw", ride=_pair_ride(d_win))
    d_wout = d_wout.reshape(nb, Rb, D)
    pair_w, (land_o,) = _pair_sum(d_win, land_w, c_arr, "pair_sum_w_in", ride=_pair_ride(d_wout))
    pair_o = _pair_sum(d_wout, land_o, c_arr, "pair_sum_w_out")
    used = (lay.used_start, lay.used)
    dxn, (all_w,) = _mm_nt_rhs_outer(dproj, w_own, F32, "in_proj_dx", ride=_chip_ride([pair_w], used))
    (grad_x, d_nw), (all_o,) = _rms_in_bwd(x, norm_w, dxn, dh, ride=_chip_ride([pair_o]))
    all_w, all_o = _put_own_slot(all_w, pair_w, used), _put_own_slot(all_o, pair_o)
    small = dict(norm_w=d_nw, conv_w=d_conv[:CONV_WIDTH], a_log=d_al[:, H:2 * H], dt_bias=d_dt[:, H:2 * H],
                 head_norm_w=d_hw, sgu_ln_w=d_lw, sgu_ln_b=d_lb, w_spatial=d_ws, b_spatial=d_bs[:, :, 0],
                 final_norm_w=d_fnw)
    return loss_row, grad_x, small, all_w, all_o


SMALL = ("norm_w", "conv_w", "a_log", "dt_bias", "head_norm_w", "sgu_ln_w", "sgu_ln_b", "w_spatial",
         "b_spatial", "final_norm_w")


def _pack(parts):
    rows = []
    for p in parts:
        f = p.reshape(-1)
        f = jnp.pad(f, (0, (-f.shape[0]) % (8 * LANES)))
        rows.append(f.reshape(-1, LANES))
    return jnp.concatenate(rows, axis=0)


def _unpack(buf, shapes):
    out, r = [], 0
    for s in shapes:
        n = 1
        for d in s:
            n *= d
        nr = -(-n // (8 * LANES)) * 8
        out.append(buf[r:r + nr].reshape(-1)[:n].reshape(s))
        r += nr
    return out


def kernel(x, norm_w, w_in, conv_w, a_log, dt_bias, head_norm_w, sgu_ln_w, sgu_ln_b, w_spatial, b_spatial, w_out, final_norm_w, loss_target, m_norm_w, m_w_in, m_conv_w, m_a_log, m_dt_bias, m_head_norm_w, m_sgu_ln_w, m_sgu_ln_b, m_w_spatial, m_b_spatial, m_w_out, m_final_norm_w, v_norm_w, v_w_in, v_conv_w, v_a_log, v_dt_bias, v_head_norm_w, v_sgu_ln_w, v_sgu_ln_b, v_w_spatial, v_b_spatial, v_w_out, v_final_norm_w):
    T, D = x.shape[1], x.shape[2]
    weights = dict(norm_w=norm_w, w_in=w_in, conv_w=conv_w, a_log=a_log, dt_bias=dt_bias, head_norm_w=head_norm_w,
                   sgu_ln_w=sgu_ln_w, sgu_ln_b=sgu_ln_b, w_spatial=w_spatial, b_spatial=b_spatial, w_out=w_out,
                   final_norm_w=final_norm_w)
    mom_m = dict(norm_w=m_norm_w, w_in=m_w_in, conv_w=m_conv_w, a_log=m_a_log, dt_bias=m_dt_bias,
                 head_norm_w=m_head_norm_w, sgu_ln_w=m_sgu_ln_w, sgu_ln_b=m_sgu_ln_b, w_spatial=m_w_spatial,
                 b_spatial=m_b_spatial, w_out=m_w_out, final_norm_w=m_final_norm_w)
    mom_v = dict(norm_w=v_norm_w, w_in=v_w_in, conv_w=v_conv_w, a_log=v_a_log, dt_bias=v_dt_bias,
                 head_norm_w=v_head_norm_w, sgu_ln_w=v_sgu_ln_w, sgu_ln_b=v_sgu_ln_b, w_spatial=v_w_spatial,
                 b_spatial=v_b_spatial, w_out=v_w_out, final_norm_w=v_final_norm_w)
    me = _chip_index(lax.axis_index("x"), lax.axis_index("y"))
    c_arr = lax.axis_index("c").astype(jnp.int32).reshape(1)
    Din, Cb = w_in.shape[1], w_in.shape[2]
    Rb = w_out.shape[1]
    cconv = conv_w.shape[2]

    loss_row, grad_x, g, qw, qo = _device_step(
        x[0], loss_target[0], norm_w, w_in[0].astype(BF16), w_out[0].astype(BF16), conv_w[0], a_log, dt_bias,
        head_norm_w, sgu_ln_w, sgu_ln_b, w_spatial[0], b_spatial[0], final_norm_w, c_arr)

    small_shapes = [tuple(g[n].shape) for n in SMALL] + [(1, LANES)]
    small = _allreduce_small(_pack([g[n] for n in SMALL] + [loss_row]))
    gsum_in, gsum_out = _sibling_fill(_chip_sum(qw, c_arr, "chip_sum_w_in"), _chip_sum(qo, c_arr, "chip_sum_w_out"))
    gsum_in = _Layout(a_log.shape[1], w_spatial.shape[1], N_CHIPS, Cb).from_window(gsum_in, me, Cb)
    *small, loss_sum = _unpack(small, small_shapes)
    gsmall = dict(zip(SMALL, small))
    gsmall["conv_w"] = lax.dynamic_slice_in_dim(gsmall["conv_w"], me * cconv, cconv, axis=1)

    grads, deltas, new_m, new_v = {}, {}, {}, {}
    d, m2, v2 = _adamw(w_out[0], gsum_out, m_w_out[0], v_w_out[0], "adamw_w_out")
    grads["w_out"], deltas["w_out"], new_m["w_out"], new_v["w_out"] = gsum_out[None], d[None], m2[None], v2[None]
    flat = lambda a: a.transpose(2, 0, 1).reshape(-1, LANES)
    unflat = lambda f: f.reshape(Cb, 1, Din).transpose(1, 2, 0)
    g_flat = gsum_in.T.reshape(-1, LANES)
    d, m2, v2 = _adamw(flat(w_in), g_flat, flat(m_w_in), flat(v_w_in), "adamw_w_in")
    grads["w_in"], deltas["w_in"], new_m["w_in"], new_v["w_in"] = unflat(g_flat), unflat(d), unflat(m2), unflat(v2)
    shapes = [tuple(weights[n].shape) for n in SMALL]
    ds, ms, vs = _adamw(_pack([weights[n] for n in SMALL]), _pack([gsmall[n] for n in SMALL]),
                        _pack([mom_m[n] for n in SMALL]), _pack([mom_v[n] for n in SMALL]), "adamw_small")
    for n, gq, d, m2, v2 in zip(SMALL, [gsmall[n] for n in SMALL], _unpack(ds, shapes), _unpack(ms, shapes),
                                _unpack(vs, shapes)):
        grads[n], deltas[n], new_m[n], new_v[n] = gq.reshape(weights[n].shape), d, m2, v2

    loss = loss_sum[0, 0]
    order = ("norm_w", "w_in", "conv_w", "a_log", "dt_bias", "head_norm_w", "sgu_ln_w", "sgu_ln_b", "w_spatial",
             "b_spatial", "w_out", "final_norm_w")
    return (loss, grad_x[None], *[grads[n] for n in order], *[deltas[n] for n in order],
            *[new_m[n] for n in order], *[new_v[n] for n in order])
```

```python
import functools

import jax
import jax.numpy as jnp
from jax import lax
from jax.experimental import pallas as pl
from jax.experimental.pallas import tpu as pltpu

F32 = jnp.float32
BF16 = jnp.bfloat16
EPS = 1e-6
HEAD_DIM = 128
CHUNK_B = 128
CONV_WIDTH = 4
LANES = 128
HALO = 8
N_CHIPS = 4
ADAM_LR = 0.001
ADAM_B1 = 0.9
ADAM_B2 = 0.999
ADAM_EPS = 1e-08
ADAM_WD = 0.01
ADAM_STEP = 10
VMEM_LIMIT = 56 * 1024 * 1024
MESH_ID = pl.DeviceIdType.MESH


def _cparams(sem=None, **kw):
    return pltpu.CompilerParams(dimension_semantics=sem, vmem_limit_bytes=VMEM_LIMIT, **kw)


def _matmul(a, b, ca, cb):
    nb = a.ndim - 2
    batch = tuple(range(nb))
    return lax.dot_general(a, b, (((ca + nb,), (cb + nb,)), (batch, batch)), preferred_element_type=F32)


def _dot(a, b):
    return _matmul(a, b, 1, 0)


def _dot_nt(a, b):
    return _matmul(a, b, 1, 1)


def _dot_tn(a, b):
    return _matmul(a, b, 0, 0)


def _iota(shape, dim):
    return lax.broadcasted_iota(jnp.int32, shape, dim)


def _sigmoid(x):
    return 0.5 * (jnp.tanh(0.5 * x) + 1.0)


def _silu(x):
    return x * _sigmoid(x)


def _softplus(x):
    z = jnp.exp(-jnp.abs(x))
    small = z * (1.0 - z * (0.5 - z * (1.0 / 3.0)))
    return jnp.maximum(x, 0.0) + jnp.where(z < 1e-3, small, jnp.log(1.0 + z))


def _pick(n, pref):
    for t in pref:
        if n % t == 0:
            return t
    return n


class _Ride:
    def __init__(self, operands, out_shape, n_sems, start, finish):
        self.operands, self.out_shape, self.n_sems = list(operands), list(out_shape), n_sems
        self.start, self.finish = start, finish


def _pallas(body, operands, *, name, grid, in_specs, out_specs, out_shape, semantics, scratch_shapes=(),
            prefetch=0, ride=None):
    single = not isinstance(out_shape, (list, tuple))
    outs = [out_shape] if single else list(out_shape)
    ospecs = [out_specs] if single else list(out_specs)
    in_specs, scratch = list(in_specs), list(scratch_shapes)
    n_in, n_out, n_sc = len(operands) - prefetch, len(outs), len(scratch)
    kernel = body
    params = _cparams(semantics)
    if ride is not None:
        n_xin, n_xout = len(ride.operands), len(ride.out_shape)

        def kernel(*refs):
            pre, refs = refs[:prefetch], refs[prefetch:]
            ins, refs = refs[:n_in], refs[n_in:]
            xins, refs = refs[:n_xin], refs[n_xin:]
            mains, refs = refs[:n_out], refs[n_out:]
            xouts, refs = refs[:n_xout], refs[n_xout:]
            sc, (send, recv) = refs[:n_sc], refs[n_sc:]
            ids = [pl.program_id(a) for a in range(len(grid))]
            first = functools.reduce(jnp.logical_and, [i == 0 for i in ids])
            last = functools.reduce(jnp.logical_and, [i == g - 1 for i, g in zip(ids, grid)])

            @pl.when(first)
            def _():
                ride.start(xins, xouts, send, recv)

            body(*pre, *ins, *mains, *sc)

            @pl.when(last)
            def _():
                ride.finish(xins, xouts, send, recv)

        operands = list(operands) + ride.operands
        in_specs += [ANY] * n_xin
        ospecs += [ANY] * n_xout
        outs += ride.out_shape
        scratch += [pltpu.SemaphoreType.DMA((ride.n_sems,)), pltpu.SemaphoreType.DMA((ride.n_sems,))]
        params = _cparams(("arbitrary",) * len(grid), has_side_effects=True)
    if prefetch:
        spec = dict(grid_spec=pltpu.PrefetchScalarGridSpec(
            num_scalar_prefetch=prefetch, grid=grid, in_specs=in_specs, out_specs=ospecs, scratch_shapes=scratch))
    else:
        spec = dict(grid=grid, in_specs=in_specs, out_specs=ospecs, scratch_shapes=scratch)
    res = pl.pallas_call(kernel, name=name, out_shape=outs, compiler_params=params, **spec)(*operands)
    main = res[0] if single else list(res[:n_out])
    return main if ride is None else (main, list(res[n_out:]))


def _mm_nn(a, b, out_dtype, name, tm=1024, tn=512, tk=None, cols=None, ride=None):
    M, K = a.shape
    c0, N = (0, b.shape[1]) if cols is None else cols
    tm = _pick(M, (tm, 1024, 512, 256, 128))
    tn = _pick(N, (tn, 512, 384, 256, 128))
    tk = K if tk is None else _pick(K, (tk,))
    nk = K // tk
    j0 = c0 // tn
    assert c0 % tn == 0

    def body(a_ref, b_ref, o_ref, *scratch):
        part = _dot(a_ref[...], b_ref[...])
        if nk == 1:
            o_ref[...] = part.astype(out_dtype)
        else:
            acc_ref, = scratch
            k = pl.program_id(2)

            @pl.when(k == 0)
            def _():
                acc_ref[...] = part

            @pl.when(k > 0)
            def _():
                acc_ref[...] += part

            @pl.when(k == nk - 1)
            def _():
                o_ref[...] = acc_ref[...].astype(out_dtype)

    return _pallas(
        body, (a, b), name=name, grid=(M // tm, N // tn, nk),
        in_specs=[pl.BlockSpec((tm, tk), lambda i, j, k: (i, k)),
                  pl.BlockSpec((tk, tn), lambda i, j, k: (k, j + j0))],
        out_specs=pl.BlockSpec((tm, tn), lambda i, j, k: (i, j)),
        out_shape=jax.ShapeDtypeStruct((M, N), out_dtype),
        scratch_shapes=[] if nk == 1 else [pltpu.VMEM((tm, tn), F32)],
        semantics=("parallel", "parallel", "arbitrary"), ride=ride)


def _mm_nt_rhs_outer(a, b, out_dtype, name, tm=256, tn=1024, ride=None):
    M, K = a.shape
    N, _ = b.shape
    tm = _pick(M, (tm, 128))
    tn = _pick(N, (tn, 512, 256, 128))

    def body(a_ref, b_ref, o_ref):
        o_ref[...] = _dot_nt(a_ref[...], b_ref[...]).astype(out_dtype)

    return _pallas(
        body, (a, b), name=name, grid=(N // tn, M // tm),
        in_specs=[pl.BlockSpec((tm, K), lambda j, i: (i, 0)),
                  pl.BlockSpec((tn, K), lambda j, i: (j, 0))],
        out_specs=pl.BlockSpec((tm, tn), lambda j, i: (i, j)),
        out_shape=jax.ShapeDtypeStruct((M, N), out_dtype),
        semantics=("parallel", "parallel"), ride=ride)


WIN_BLOCK = 256


def _mm_windows(a, b, table, nb, name, tm=2048):
    M, K = a.shape
    wb = table.shape[0] // nb
    tm = _pick(M, (tm, 1024, 512, 256, 128))

    def body(tab_ref, a_ref, b_ref, o_ref):
        o_ref[0] = _dot(a_ref[...], b_ref[...]).astype(BF16)

    return pl.pallas_call(
        body, name=name,
        grid_spec=pltpu.PrefetchScalarGridSpec(
            num_scalar_prefetch=1, grid=(nb, M // tm, wb),
            in_specs=[pl.BlockSpec((tm, K), lambda n, i, t, tab: (i, 0)),
                      pl.BlockSpec((K, WIN_BLOCK), lambda n, i, t, tab: (0, tab[n * wb + t]))],
            out_specs=pl.BlockSpec((1, tm, WIN_BLOCK), lambda n, i, t, tab: (n, i, t))),
        out_shape=jax.ShapeDtypeStruct((nb, M, wb * WIN_BLOCK), BF16),
        compiler_params=_cparams(("parallel", "parallel", "arbitrary")),
    )(table, a, b)


def _mm_nn_pair(a0, a1, b, name, tm=512, tn=1024, ride=None):
    M, K = a0.shape
    _, N = b.shape
    tm = _pick(M, (tm, 256, 128))
    tn = _pick(N, (tn, 512, 256, 128))
    ni = M // tm

    def body(a0_ref, a1_ref, b_ref, o_ref):
        p = pl.program_id(0)

        @pl.when(p == 0)
        def _():
            o_ref[...] = _dot(a0_ref[...], b_ref[...]).astype(BF16)

        @pl.when(p == 1)
        def _():
            o_ref[...] = _dot(a1_ref[...], b_ref[...]).astype(BF16)

    return _pallas(
        body, (a0, a1, b), name=name, grid=(2, ni, N // tn),
        in_specs=[pl.BlockSpec((tm, K), lambda p, i, j: (i * (1 - p), 0)),
                  pl.BlockSpec((tm, K), lambda p, i, j: (i * p, 0)),
                  pl.BlockSpec((K, tn), lambda p, i, j: (0, j))],
        out_specs=pl.BlockSpec((tm, tn), lambda p, i, j: (p * ni + i, j)),
        out_shape=jax.ShapeDtypeStruct((2 * M, N), BF16),
        semantics=("parallel", "parallel", "parallel"), ride=ride)


def _rms_fn(x, w):
    r = lax.rsqrt(jnp.mean(x * x, axis=-1, keepdims=True) + EPS)
    return x * r * w


def _rms_in(x, w, ride=None):
    T, D = x.shape
    tm = _pick(T, (512, 256, 128))

    def body(x_ref, w_ref, o_ref, ot_ref):
        xn = _rms_fn(x_ref[...], w_ref[...])
        o_ref[...] = xn.astype(BF16)
        ot_ref[...] = xn.T.astype(BF16)

    return _pallas(
        body, (x, w), name="rms_in", grid=(T // tm,),
        in_specs=[pl.BlockSpec((tm, D), lambda i: (i, 0)), pl.BlockSpec((1, D), lambda i: (0, 0))],
        out_specs=[pl.BlockSpec((tm, D), lambda i: (i, 0)), pl.BlockSpec((D, tm), lambda i: (0, i))],
        out_shape=[jax.ShapeDtypeStruct((T, D), BF16), jax.ShapeDtypeStruct((D, T), BF16)],
        semantics=("parallel",), ride=ride)


def _rms_in_bwd(x, w, dxn, dh, ride=None):
    T, D = x.shape
    tm = _pick(T, (256, 128))

    def body(x_ref, w_ref, dxn_ref, dh_ref, gx_ref, dw_ref):
        _, vjp = jax.vjp(_rms_fn, x_ref[...], w_ref[...])
        dx, dw = vjp(dxn_ref[...])
        gx_ref[...] = dh_ref[...] + dx

        @pl.when(pl.program_id(0) == 0)
        def _():
            dw_ref[...] = dw

        @pl.when(pl.program_id(0) > 0)
        def _():
            dw_ref[...] += dw

    tile = pl.BlockSpec((tm, D), lambda i: (i, 0))
    row = pl.BlockSpec((1, D), lambda i: (0, 0))
    return _pallas(
        body, (x, w, dxn, dh), name="rms_in_bwd", grid=(T // tm,),
        in_specs=[tile, row, tile, tile], out_specs=[tile, row],
        out_shape=[jax.ShapeDtypeStruct((T, D), F32), jax.ShapeDtypeStruct((1, D), F32)],
        semantics=("arbitrary",), ride=ride)


def _conv_fwd(cat_ref, halo, x, w):
    tm = x.shape[0]
    cat_ref[0:HALO, :] = halo
    cat_ref[HALO:HALO + tm, :] = x
    c = x * w[CONV_WIDTH - 1:CONV_WIDTH, :]
    for k in range(CONV_WIDTH - 1):
        s = CONV_WIDTH - 1 - k
        c = c + cat_ref[pl.ds(HALO - s, tm), :] * w[k:k + 1, :]
    return c


def _lane_to_all(x, lane):
    @jax.custom_vjp
    def f(x):
        return jnp.broadcast_to(x[:, lane:lane + 1], x.shape)

    def f_fwd(x):
        return f(x), None

    def f_bwd(_, g):
        return (jnp.where(_iota(g.shape, 1) == lane, jnp.sum(g, axis=-1, keepdims=True), 0.0),)

    f.defvjp(f_fwd, f_bwd)
    return f(x)


def _gdn_pointwise(c, ba, alog, dtb, H):
    A = H * HEAD_DIM
    s = _silu(c)
    beta = _sigmoid(ba)
    g = -jnp.exp(alog) * _softplus(ba + dtb)
    qs, ks, vs, gbs, bbs = [], [], [], [], []
    for h in range(H):
        lo = h * HEAD_DIM
        q = s[:, lo:lo + HEAD_DIM]
        k = s[:, A + lo:A + lo + HEAD_DIM]
        qs.append(q * lax.rsqrt(jnp.sum(q * q, axis=-1, keepdims=True) + EPS))
        ks.append(k * lax.rsqrt(jnp.sum(k * k, axis=-1, keepdims=True) + EPS))
        vs.append(s[:, 2 * A + lo:2 * A + lo + HEAD_DIM])
        bbs.append(_lane_to_all(beta, h))
        gbs.append(_lane_to_all(g, H + h))
    st = lambda xs: jnp.stack(xs, axis=0)
    return st(qs), st(ks), st(vs), st(gbs), st(bbs)


def _halo_prev(tm):
    return lambda i: (jnp.maximum(i * (tm // HALO) - 1, 0), 0)


def _gdn_pre(proj_m, xn, w_own, conv_w, alog_row, dtb_row, H):
    T, n_main = proj_m.shape
    D = xn.shape[1]
    A = H * HEAD_DIM
    tm = _pick(T, (256, 128))
    hs = pl.BlockSpec((H, tm, HEAD_DIM), lambda i: (0, i, 0))
    hshape = jax.ShapeDtypeStruct((H, T, HEAD_DIM), F32)

    def body(x_ref, halo_ref, xn_ref, wba_ref, w_ref, al_ref, dt_ref,
             q_ref, k_ref, v_ref, gb_ref, bb_ref, ba_ref, cat_ref):
        halo = jnp.where(pl.program_id(0) == 0, 0.0, halo_ref[...])
        c = _conv_fwd(cat_ref, halo, x_ref[...], w_ref[...])
        ba = _dot(xn_ref[...], wba_ref[...])
        q, k, v, gb, bb = _gdn_pointwise(c, ba, al_ref[...], dt_ref[...], H)
        q_ref[...] = q
        k_ref[...] = k
        v_ref[...] = v
        gb_ref[...] = gb
        bb_ref[...] = bb
        ba_ref[...] = ba

    return pl.pallas_call(
        body, name="gdn_pre", grid=(T // tm,),
        in_specs=[pl.BlockSpec((tm, 3 * A), lambda i: (i, 0)),
                  pl.BlockSpec((HALO, 3 * A), _halo_prev(tm)),
                  pl.BlockSpec((tm, D), lambda i: (i, 0)),
                  pl.BlockSpec((D, LANES), lambda i: (0, n_main // LANES)),
                  pl.BlockSpec((CONV_WIDTH, 3 * A), lambda i: (0, 0)),
                  pl.BlockSpec((1, LANES), lambda i: (0, 0)),
                  pl.BlockSpec((1, LANES), lambda i: (0, 0))],
        out_specs=[hs] * 5 + [pl.BlockSpec((tm, LANES), lambda i: (i, 0))],
        out_shape=[hshape] * 5 + [jax.ShapeDtypeStruct((T, LANES), F32)],
        scratch_shapes=[pltpu.VMEM((HALO + tm, 3 * A), F32)],
        compiler_params=_cparams(("parallel",)),
    )(proj_m, proj_m, xn, w_own, conv_w, alog_row, dtb_row)


def _gdn_pre_bwd(proj_m, proj_ba, conv_w, alog_row, dtb_row, dq, dk, dv, dgb, dbb, H, dproj):
    T, n_main = proj_m.shape
    A = H * HEAD_DIM
    tm = _pick(T, (256, 128))
    hs = pl.BlockSpec((H, tm, HEAD_DIM), lambda i: (0, i, 0))
    row = pl.BlockSpec((1, LANES), lambda i: (0, 0))

    def body(x_ref, halo_ref, ba_ref, w_ref, al_ref, dt_ref, dq_ref, dk_ref, dv_ref, dgb_ref, dbb_ref, _,
             dc_ref, dba_ref, dal_ref, ddt_ref, cat_ref):
        halo = jnp.where(pl.program_id(0) == 0, 0.0, halo_ref[...])
        c = _conv_fwd(cat_ref, halo, x_ref[...], w_ref[...])
        _, vjp = jax.vjp(functools.partial(_gdn_pointwise, H=H), c, ba_ref[...], al_ref[...], dt_ref[...])
        dc, dba, dal, ddt = vjp((dq_ref[...], dk_ref[...], dv_ref[...], dgb_ref[...], dbb_ref[...]))
        dc_ref[...] = dc
        dba_ref[:, :LANES] = dba.astype(BF16)
        dba_ref[:, LANES:] = jnp.zeros((tm, WIN_BLOCK - LANES), BF16)

        @pl.when(pl.program_id(0) == 0)
        def _():
            dal_ref[...] = dal
            ddt_ref[...] = ddt

        @pl.when(pl.program_id(0) > 0)
        def _():
            dal_ref[...] += dal
            ddt_ref[...] += ddt

    return pl.pallas_call(
        body, name="gdn_pre_bwd", grid=(T // tm,),
        in_specs=[pl.BlockSpec((tm, 3 * A), lambda i: (i, 0)),
                  pl.BlockSpec((HALO, 3 * A), _halo_prev(tm)),
                  pl.BlockSpec((tm, LANES), lambda i: (i, 0)),
                  pl.BlockSpec((CONV_WIDTH, 3 * A), lambda i: (0, 0)),
                  row, row, hs, hs, hs, hs, hs, ANY],
        out_specs=[pl.BlockSpec((tm, 3 * A), lambda i: (i, 0)),
                   pl.BlockSpec((tm, WIN_BLOCK), lambda i: (i, n_main // WIN_BLOCK)), row, row],
        out_shape=[jax.ShapeDtypeStruct((T, 3 * A), F32), jax.ShapeDtypeStruct(dproj.shape, dproj.dtype),
                   jax.ShapeDtypeStruct((1, LANES), F32), jax.ShapeDtypeStruct((1, LANES), F32)],
        input_output_aliases={11: 1},
        scratch_shapes=[pltpu.VMEM((HALO + tm, 3 * A), F32)],
        compiler_params=_cparams(("arbitrary",)),
    )(proj_m, proj_m, proj_ba, conv_w, alog_row, dtb_row, dq, dk, dv, dgb, dbb, dproj)


def _conv_bwd(proj_m, dc, conv_w, H, dproj):
    T = proj_m.shape[0]
    A = H * HEAD_DIM
    tm = _pick(T, (256, 128))
    nt = T // tm

    def body(x_ref, halo_ref, dc_ref, nxt_ref, w_ref, _, dx_ref, dw_ref):
        i = pl.program_id(0)
        halo = jnp.where(i == 0, 0.0, halo_ref[...])
        xcat = jnp.concatenate([halo, x_ref[...]], axis=0)
        nxt = jnp.where(i == nt - 1, 0.0, nxt_ref[...])
        dc = dc_ref[...]
        dcat = jnp.concatenate([dc, nxt], axis=0)
        w = w_ref[...]
        dx = None
        rows = []
        for k in range(CONV_WIDTH):
            s = CONV_WIDTH - 1 - k
            ds = dcat if s == 0 else pltpu.roll(dcat, tm + HALO - s, 0)
            term = ds[:tm, :] * w[k:k + 1, :]
            dx = term if dx is None else dx + term
            xs = xcat if s == 0 else pltpu.roll(xcat, s, 0)
            rows.append(jnp.sum(dc * xs[HALO:, :], axis=0, keepdims=True))
        dx_ref[...] = dx.astype(BF16)
        dw = jnp.concatenate(rows + [jnp.zeros((HALO - CONV_WIDTH, 3 * A), F32)], axis=0)

        @pl.when(i == 0)
        def _():
            dw_ref[...] = dw

        @pl.when(i > 0)
        def _():
            dw_ref[...] += dw

    return pl.pallas_call(
        body, name="conv_bwd", grid=(nt,),
        in_specs=[pl.BlockSpec((tm, 3 * A), lambda i: (i, 0)),
                  pl.BlockSpec((HALO, 3 * A), _halo_prev(tm)),
                  pl.BlockSpec((tm, 3 * A), lambda i: (i, 0)),
                  pl.BlockSpec((HALO, 3 * A), lambda i: (jnp.minimum((i + 1) * (tm // HALO), T // HALO - 1), 0)),
                  pl.BlockSpec((CONV_WIDTH, 3 * A), lambda i: (0, 0)), ANY],
        out_specs=[pl.BlockSpec((tm, 3 * A), lambda i: (i, 0)),
                   pl.BlockSpec((HALO, 3 * A), lambda i: (0, 0))],
        out_shape=[jax.ShapeDtypeStruct(dproj.shape, dproj.dtype), jax.ShapeDtypeStruct((HALO, 3 * A), F32)],
        input_output_aliases={5: 0},
        compiler_params=_cparams(("arbitrary",)),
    )(proj_m, proj_m, dc, dc, conv_w, dproj)


CHUNK = 128
BLOCK = 64


def _b(x):
    return x.astype(BF16)


@jax.custom_vjp
def _bdot(a, b):
    return _dot(_b(a), _b(b))


def _bdot_f(a, b):
    return _bdot(a, b), (a, b)


def _bdot_b(res, g):
    a, b = res
    return _dot_nt(_b(g), _b(b)), _dot_tn(_b(a), _b(g))


_bdot.defvjp(_bdot_f, _bdot_b)


@jax.custom_vjp
def _bdot_nt(a, b):
    return _dot_nt(_b(a), _b(b))


def _bdot_nt_f(a, b):
    return _bdot_nt(a, b), (a, b)


def _bdot_nt_b(res, g):
    a, b = res
    return _dot(_b(g), _b(b)), _dot_tn(_b(g), _b(a))


_bdot_nt.defvjp(_bdot_nt_f, _bdot_nt_b)


@jax.custom_vjp
def _bdot_tn(a, b):
    return _dot_tn(_b(a), _b(b))


def _bdot_tn_f(a, b):
    return _bdot_tn(a, b), (a, b)


def _bdot_tn_b(res, g):
    a, b = res
    return _dot_nt(_b(b), _b(g)), _dot(_b(a), _b(g))


_bdot_tn.defvjp(_bdot_tn_f, _bdot_tn_b)


def _mask_matmul(m, x):
    hi = _b(x)
    r = x - hi.astype(F32)
    mid = _b(r)
    lo = _b(r - mid.astype(F32))
    return (_dot(m, lo) + _dot(m, mid)) + _dot(m, hi)


@jax.custom_vjp
def _mask_dot(m, mt, x):
    return _mask_matmul(m, x)


def _mask_dot_f(m, mt, x):
    return _mask_matmul(m, x), (m, mt)


def _mask_dot_b(res, g):
    m, mt = res
    return jnp.zeros_like(m), jnp.zeros_like(mt), _mask_matmul(mt, g)


_mask_dot.defvjp(_mask_dot_f, _mask_dot_b)

def _unit_lower_inverse(L):
    n = L.shape[-1]
    X = -L
    Q = X
    for _ in range(BLOCK.bit_length() - 2):
        X = _dot(_b(X), _b(X))
        Q = Q + X + _dot(_b(Q), _b(X))
    return (_iota((n, n), 0) == _iota((n, n), 1)).astype(F32) + Q


@jax.custom_vjp
def _known_inverse(L, P):
    return P


def _known_inverse_f(L, P):
    return P, P


def _known_inverse_b(P, g):
    n = P.shape[-1]
    Q = _b(P - (_iota((n, n), 0) == _iota((n, n), 1)).astype(F32))
    t = g + _dot_tn(Q, _b(g))
    return -(t + _dot_nt(_b(t), Q)), jnp.zeros_like(P)


_known_inverse.defvjp(_known_inverse_f, _known_inverse_b)


def _gdn_prep_fn(q, k, v, gb, bb, P_known=None):
    n = CHUNK
    row, col = _iota((n, n), 0), _iota((n, n), 1)
    same = (row // BLOCK) == (col // BLOCK)
    incl, strict = same & (row >= col), same & (row > col)
    bc = lambda m: jnp.broadcast_to(_b(m.astype(F32)), q.shape[:1] + (n, n))
    tril, triu, ones = bc(incl), bc(same & (row <= col)), bc(same)
    gc = _mask_dot(tril, triu, gb)
    gl = _mask_dot(ones, ones, gb)
    decay = jnp.where(incl, jnp.exp(jnp.where(incl, gc - jnp.swapaxes(gc, 1, 2), 0.0)), 0.0)
    kb = k * bb
    vb = v * bb
    qs = q * (HEAD_DIM ** -0.5)
    L = jnp.where(strict, _bdot_nt(kb, k) * decay, 0.0)
    P = _unit_lower_inverse(L) if P_known is None else _known_inverse(L, P_known)
    egc = jnp.exp(gc)
    u = _bdot(P, vb)
    w = _bdot(P, kb * egc)
    attn = jnp.where(incl, _bdot_nt(qs, k) * decay, 0.0)
    qg = qs * egc
    kdec = k * jnp.exp(gl - gc)
    eg = jnp.exp(gl).reshape(-1, n // BLOCK, BLOCK, LANES).sum(axis=2) * (1.0 / BLOCK)
    if P_known is None:
        return u, w, qg, kdec, attn, eg, P
    return u, w, qg, kdec, attn, eg


def _gdn_block_fn(S, qg, kdec, u, w, attn, eg, i):
    nblk = CHUNK // BLOCK
    v_new = u - _bdot(w, S)
    zeros = jnp.zeros_like(v_new)
    o = _bdot(qg, S) + _bdot(attn, jnp.concatenate([zeros] * i + [v_new] + [zeros] * (nblk - 1 - i), axis=1))
    return o, S * eg + _bdot_tn(kdec, v_new)


def _eg_spec(H, T, chunks, index_map, per_head):
    nblk = CHUNK // BLOCK
    block = (chunks, 1 if per_head else H, nblk, LANES)
    return pl.BlockSpec(block, index_map), jax.ShapeDtypeStruct((T // CHUNK, H, nblk, LANES), F32)


def _gdn_prep(q, k, v, gb, bb):
    H, T, _ = q.shape
    pb = _pick(T // CHUNK, (16, 8, 4, 2, 1))
    hs = pl.BlockSpec((1, CHUNK * pb, HEAD_DIM), lambda h, n: (h, n, 0))

    def body(q_ref, k_ref, v_ref, gb_ref, bb_ref, *out_refs):
        chunks = lambda ref: ref[0].reshape(pb, CHUNK, HEAD_DIM)
        outs = _gdn_prep_fn(chunks(q_ref), chunks(k_ref), chunks(v_ref), chunks(gb_ref), chunks(bb_ref))
        for i, (ref, val) in enumerate(zip(out_refs, outs)):
            if i == 5:
                ref[:, 0] = val
            else:
                ref[0] = val.reshape(pb * CHUNK, HEAD_DIM).astype(ref.dtype)

    kept = [F32, BF16, BF16, BF16, BF16, None, BF16]
    es, eshape = _eg_spec(H, T, pb, lambda h, n: (n, h, 0, 0), per_head=True)
    return pl.pallas_call(
        body, name="gdn_prep", grid=(H, T // (CHUNK * pb)),
        in_specs=[hs] * 5, out_specs=[es if dt is None else hs for dt in kept],
        out_shape=[eshape if dt is None else jax.ShapeDtypeStruct((H, T, HEAD_DIM), dt) for dt in kept],
        compiler_params=_cparams(("parallel", "parallel")),
    )(q, k, v, gb, bb)


def _gdn_prep_bwd(q, k, v, gb, bb, pinv, du, dw, dqg, dkd, dat, deg):
    H, T, _ = q.shape
    pb = _pick(T // CHUNK, (16, 8, 4, 2, 1))
    hs = pl.BlockSpec((1, CHUNK * pb, HEAD_DIM), lambda h, n: (h, n, 0))
    hshape = jax.ShapeDtypeStruct((H, T, HEAD_DIM), F32)

    def body(*refs):
        in_refs, p_ref, ct_refs, out_refs = refs[:5], refs[5], refs[6:12], refs[12:]
        chunks = lambda ref: ref[0].reshape(pb, CHUNK, HEAD_DIM)
        P = chunks(p_ref).astype(F32)
        _, vjp = jax.vjp(lambda *a: _gdn_prep_fn(*a, P_known=P), *[chunks(r) for r in in_refs])
        grads = vjp(tuple(chunks(r).astype(F32) for r in ct_refs[:5]) + (ct_refs[5][:, 0],))
        for ref, val in zip(out_refs, grads):
            ref[0] = val.reshape(pb * CHUNK, HEAD_DIM)

    es, _ = _eg_spec(H, T, pb, lambda h, n: (n, h, 0, 0), per_head=True)
    return pl.pallas_call(
        body, name="gdn_prep_bwd", grid=(H, T // (CHUNK * pb)),
        in_specs=[hs] * 11 + [es], out_specs=[hs] * 5, out_shape=[hshape] * 5,
        compiler_params=_cparams(("parallel", "parallel")),
    )(q, k, v, gb, bb, pinv, du, dw, dqg, dkd, dat, deg)


def _gdn_chain(qg, kd, u, w, attn, eg):
    H, T, _ = qg.shape
    N, nblk = T // CHUNK, CHUNK // BLOCK
    cs = _pick(N, (2, 1))
    hs = pl.BlockSpec((H, cs * CHUNK, HEAD_DIM), lambda n: (0, n, 0))
    ss = pl.BlockSpec((cs, nblk, H, HEAD_DIM, HEAD_DIM), lambda n: (n, 0, 0, 0, 0))

    def body(qg_ref, kd_ref, u_ref, w_ref, at_ref, eg_ref, o_ref, sall_ref, s_ref):
        @pl.when(pl.program_id(0) == 0)
        def _():
            s_ref[...] = jnp.zeros_like(s_ref)

        S = s_ref[...]
        for j in range(cs):
            for i in range(nblk):
                r = pl.ds(j * CHUNK + i * BLOCK, BLOCK)
                sall_ref[j, i] = S
                o_ref[:, r, :], S = _gdn_block_fn(S, qg_ref[:, r, :], kd_ref[:, r, :], u_ref[:, r, :],
                                                  w_ref[:, r, :], at_ref[:, r, :], eg_ref[j, :, i:i + 1, :], i)
        s_ref[...] = S

    es, _ = _eg_spec(H, T, cs, lambda n: (n, 0, 0, 0), per_head=False)
    return pl.pallas_call(
        body, name="gdn_chain", grid=(N // cs,),
        in_specs=[hs] * 5 + [es], out_specs=[hs, ss],
        out_shape=[jax.ShapeDtypeStruct((H, T, HEAD_DIM), F32),
                   jax.ShapeDtypeStruct((N, nblk, H, HEAD_DIM, HEAD_DIM), F32)],
        scratch_shapes=[pltpu.VMEM((H, HEAD_DIM, HEAD_DIM), F32)],
        compiler_params=_cparams(("arbitrary",)),
    )(qg, kd, u, w, attn, eg)


def _gdn_chain_bwd(qg, kd, u, w, attn, eg, sall, do):
    H, T, _ = qg.shape
    N, nblk = T // CHUNK, CHUNK // BLOCK
    cs = _pick(N, (2, 1))
    last = N // cs - 1
    hs = pl.BlockSpec((H, cs * CHUNK, HEAD_DIM), lambda n: (0, last - n, 0))
    ss = pl.BlockSpec((cs, nblk, H, HEAD_DIM, HEAD_DIM), lambda n: (last - n, 0, 0, 0, 0))

    def body(qg_ref, kd_ref, u_ref, w_ref, at_ref, eg_ref, sall_ref, do_ref, *rest):
        out_refs, ds_ref = rest[:6], rest[6]

        @pl.when(pl.program_id(0) == 0)
        def _():
            ds_ref[...] = jnp.zeros_like(ds_ref)

        dS = ds_ref[...]
        for j in reversed(range(cs)):
            for i in reversed(range(nblk)):
                r = pl.ds(j * CHUNK + i * BLOCK, BLOCK)
                f32 = lambda ref: ref[:, r, :].astype(F32)
                _, vjp = jax.vjp(functools.partial(_gdn_block_fn, i=i), sall_ref[j, i], f32(qg_ref), f32(kd_ref),
                                 u_ref[:, r, :], f32(w_ref), f32(at_ref), eg_ref[j, :, i:i + 1, :])
                grads = vjp((do_ref[:, r, :], dS))
                dS = grads[0]
                for ref, val in zip(out_refs[:5], grads[1:6]):
                    ref[:, r, :] = val.astype(ref.dtype)
                out_refs[5][j, :, i:i + 1, :] = grads[6]
        ds_ref[...] = dS

    kept = [F32, F32, BF16, BF16, F32]
    es, eshape = _eg_spec(H, T, cs, lambda n: (last - n, 0, 0, 0), per_head=False)
    return pl.pallas_call(
        body, name="gdn_chain_bwd", grid=(N // cs,),
        in_specs=[hs] * 5 + [es, ss, hs], out_specs=[hs] * 5 + [es],
        out_shape=[jax.ShapeDtypeStruct((H, T, HEAD_DIM), dt) for dt in kept] + [eshape],
        scratch_shapes=[pltpu.VMEM((H, HEAD_DIM, HEAD_DIM), F32)],
        compiler_params=_cparams(("arbitrary",)),
    )(qg, kd, u, w, attn, eg, sall, do)


def _post_fn(ogs, za, hw):
    outs = []
    for h, o in enumerate(ogs):
        r = lax.rsqrt(jnp.mean(o * o, axis=-1, keepdims=True) + EPS)
        outs.append(o * r * hw * _silu(za[:, h * HEAD_DIM:(h + 1) * HEAD_DIM]))
    return jnp.concatenate(outs, axis=1)


def _gdn_post(og, proj_m, hw):
    H, T, _ = og.shape
    A = H * HEAD_DIM
    tm = _pick(T, (512, 256, 128))

    def body(og_ref, za_ref, hw_ref, o_ref, ot_ref):
        o = _post_fn(tuple(og_ref[h] for h in range(H)), za_ref[...], hw_ref[...])
        o_ref[...] = o.astype(BF16)
        ot_ref[...] = o.T.astype(BF16)

    return pl.pallas_call(
        body, name="gdn_post", grid=(T // tm,),
        in_specs=[pl.BlockSpec((H, tm, HEAD_DIM), lambda i: (0, i, 0)),
                  pl.BlockSpec((tm, A), lambda i: (i, ZA_BLOCK)),
                  pl.BlockSpec((1, HEAD_DIM), lambda i: (0, 0))],
        out_specs=[pl.BlockSpec((tm, A), lambda i: (i, 0)), pl.BlockSpec((A, tm), lambda i: (0, i))],
        out_shape=[jax.ShapeDtypeStruct((T, A), BF16), jax.ShapeDtypeStruct((A, T), BF16)],
        compiler_params=_cparams(("parallel",)),
    )(og, proj_m, hw)


def _gdn_post_bwd(og, proj_m, hw, d_o, dproj):
    H, T, _ = og.shape
    A = H * HEAD_DIM
    tm = _pick(T, (256, 128))

    def body(og_ref, za_ref, hw_ref, do_ref, _, dog_ref, dza_ref, dhw_ref):
        _, vjp = jax.vjp(_post_fn, tuple(og_ref[h] for h in range(H)), za_ref[...], hw_ref[...])
        dog, dza, dhw = vjp(do_ref[...])
        for h in range(H):
            dog_ref[h] = dog[h]
        dza_ref[...] = dza.astype(BF16)

        @pl.when(pl.program_id(0) == 0)
        def _():
            dhw_ref[...] = dhw

        @pl.when(pl.program_id(0) > 0)
        def _():
            dhw_ref[...] += dhw

    return pl.pallas_call(
        body, name="gdn_post_bwd", grid=(T // tm,),
        in_specs=[pl.BlockSpec((H, tm, HEAD_DIM), lambda i: (0, i, 0)),
                  pl.BlockSpec((tm, A), lambda i: (i, ZA_BLOCK)),
                  pl.BlockSpec((1, HEAD_DIM), lambda i: (0, 0)),
                  pl.BlockSpec((tm, A), lambda i: (i, 0)), ANY],
        out_specs=[pl.BlockSpec((H, tm, HEAD_DIM), lambda i: (0, i, 0)),
                   pl.BlockSpec((tm, A), lambda i: (i, ZA_BLOCK)),
                   pl.BlockSpec((1, HEAD_DIM), lambda i: (0, 0))],
        out_shape=[jax.ShapeDtypeStruct((H, T, HEAD_DIM), F32), jax.ShapeDtypeStruct(dproj.shape, dproj.dtype),
                   jax.ShapeDtypeStruct((1, HEAD_DIM), F32)],
        input_output_aliases={4: 1},
        compiler_params=_cparams(("arbitrary",)),
    )(og, proj_m, hw, d_o, dproj)


def _sgu_fn(ub, vb, zb, lw, lb, W, bbc):
    G = len(W)
    tm = ub.shape[0]
    mu = jnp.mean(vb, axis=-1, keepdims=True)
    xc = vb - mu
    var = jnp.mean(xc * xc, axis=-1, keepdims=True)
    vn = xc * lax.rsqrt(var + EPS) * lw + lb
    mask = _iota((CHUNK_B, CHUNK_B), 0) >= _iota((CHUNK_B, CHUNK_B), 1)
    cols = []
    for g in range(G):
        wm = jnp.where(mask, W[g], 0.0).astype(BF16)
        rows = []
        for c in range(tm // CHUNK_B):
            blk = vn[c * CHUNK_B:(c + 1) * CHUNK_B, g * HEAD_DIM:(g + 1) * HEAD_DIM].astype(BF16)
            rows.append(_dot(wm, blk) + bbc[g])
        cols.append(jnp.concatenate(rows, axis=0) if len(rows) > 1 else rows[0])
    s = jnp.concatenate(cols, axis=1)
    return ub * s * _silu(zb)


ZA_BLOCK = 6


def _sgu_cols(A, B):
    assert A == B
    return 3, 4, 5


def _sgu_fwd(proj_m, lw, lb, W, bbc, A):
    T = proj_m.shape[0]
    G = W.shape[0]
    B = G * HEAD_DIM
    tm = _pick(T, (256, 128))
    cu, cv, cz = _sgu_cols(A, B)

    def body(u_ref, v_ref, z_ref, lw_ref, lb_ref, w_ref, b_ref, o_ref, ot_ref):
        o = _sgu_fn(u_ref[...], v_ref[...], z_ref[...], lw_ref[...], lb_ref[...],
                    tuple(w_ref[g] for g in range(G)), tuple(b_ref[g] for g in range(G)))
        o_ref[...] = o.astype(BF16)
        ot_ref[...] = o.T.astype(BF16)

    row = pl.BlockSpec((1, B), lambda i: (0, 0))
    cube = pl.BlockSpec((G, CHUNK_B, CHUNK_B), lambda i: (0, 0, 0))
    return pl.pallas_call(
        body, name="sgu_fwd", grid=(T // tm,),
        in_specs=[pl.BlockSpec((tm, B), lambda i: (i, cu)), pl.BlockSpec((tm, B), lambda i: (i, cv)),
                  pl.BlockSpec((tm, B), lambda i: (i, cz)), row, row, cube, cube],
        out_specs=[pl.BlockSpec((tm, B), lambda i: (i, 0)), pl.BlockSpec((B, tm), lambda i: (0, i))],
        out_shape=[jax.ShapeDtypeStruct((T, B), BF16), jax.ShapeDtypeStruct((B, T), BF16)],
        compiler_params=_cparams(("parallel",)),
    )(proj_m, proj_m, proj_m, lw, lb, W, bbc)


def _sgu_bwd(proj_m, lw, lb, W, bbc, d_o, A, dproj):
    T = proj_m.shape[0]
    G = W.shape[0]
    B = G * HEAD_DIM
    tm = _pick(T, (256, 128))
    nt = T // tm
    cu, cv, cz = _sgu_cols(A, B)

    def body(u_ref, v_ref, z_ref, lw_ref, lb_ref, w_ref, b_ref, do_ref, _,
             dp_ref, dlw_ref, dlb_ref, dw_ref, db_ref, dbb_ref):
        _, vjp = jax.vjp(_sgu_fn, u_ref[...], v_ref[...], z_ref[...], lw_ref[...], lb_ref[...],
                         tuple(w_ref[g] for g in range(G)), tuple(b_ref[g] for g in range(G)))
        du, dv, dz, dlw, dlb, dW, dbb = vjp(do_ref[...])
        dW, dbb = jnp.stack(dW, axis=0), jnp.stack(dbb, axis=0)
        dp_ref[:, 0:B] = du.astype(BF16)
        dp_ref[:, B:2 * B] = dv.astype(BF16)
        dp_ref[:, 2 * B:3 * B] = dz.astype(BF16)
        i = pl.program_id(0)

        @pl.when(i == 0)
        def _():
            dlw_ref[...] = dlw
            dlb_ref[...] = dlb
            dw_ref[...] = dW
            dbb_ref[...] = dbb

        @pl.when(i > 0)
        def _():
            dlw_ref[...] += dlw
            dlb_ref[...] += dlb
            dw_ref[...] += dW
            dbb_ref[...] += dbb

        @pl.when(i == nt - 1)
        def _():
            db_ref[...] = jnp.sum(dbb_ref[...], axis=-1, keepdims=True)

    row = pl.BlockSpec((1, B), lambda i: (0, 0))
    cube = pl.BlockSpec((G, CHUNK_B, CHUNK_B), lambda i: (0, 0, 0))
    return pl.pallas_call(
        body, name="sgu_bwd", grid=(nt,),
        in_specs=[pl.BlockSpec((tm, B), lambda i: (i, cu)), pl.BlockSpec((tm, B), lambda i: (i, cv)),
                  pl.BlockSpec((tm, B), lambda i: (i, cz)), row, row, cube, cube,
                  pl.BlockSpec((tm, B), lambda i: (i, A // B)), ANY],
        out_specs=[pl.BlockSpec((tm, 3 * B), lambda i: (i, 1)), row, row, cube,
                   pl.BlockSpec((G, CHUNK_B, 1), lambda i: (0, 0, 0))],
        out_shape=[jax.ShapeDtypeStruct(dproj.shape, dproj.dtype), jax.ShapeDtypeStruct((1, B), F32),
                   jax.ShapeDtypeStruct((1, B), F32), jax.ShapeDtypeStruct((G, CHUNK_B, CHUNK_B), F32),
                   jax.ShapeDtypeStruct((G, CHUNK_B, 1), F32)],
        input_output_aliases={8: 0},
        scratch_shapes=[pltpu.VMEM((G, CHUNK_B, CHUNK_B), F32)],
        compiler_params=_cparams(("arbitrary",)),
    )(proj_m, proj_m, proj_m, lw, lb, W, bbc, d_o, dproj)


def _head_fn(mix, x, fw, tgt):
    h = x + mix
    y = _rms_fn(h, fw)
    e = y - tgt
    return 0.5 * jnp.sum(jnp.mean(e * e, axis=-1, keepdims=True), axis=0, keepdims=True)


def _out_proj_loss(oa, ob, wout, x, tgt, fw):
    T, A = oa.shape
    B = ob.shape[1]
    D = x.shape[1]
    tm = _pick(T, (256, 128))

    def body(oa_ref, ob_ref, w_ref, x_ref, t_ref, fw_ref, dh_ref, dhb_ref, loss_ref, dfw_ref):
        mix = _dot(oa_ref[...], w_ref[0:A, :]) + _dot(ob_ref[...], w_ref[A:A + B, :])
        xv, tv = x_ref[...], t_ref[...]
        loss, vjp = jax.vjp(lambda m, f: _head_fn(m, xv, f, tv), mix, fw_ref[...])
        dh, dfw = vjp(jnp.ones((1, 1), F32))
        dh_ref[...] = dh
        dhb_ref[...] = dh.astype(BF16)
        lrow = jnp.broadcast_to(loss, (1, LANES))

        @pl.when(pl.program_id(0) == 0)
        def _():
            loss_ref[...] = lrow
            dfw_ref[...] = dfw

        @pl.when(pl.program_id(0) > 0)
        def _():
            loss_ref[...] += lrow
            dfw_ref[...] += dfw

    tile = pl.BlockSpec((tm, D), lambda i: (i, 0))
    return pl.pallas_call(
        body, name="out_proj_loss", grid=(T // tm,),
        in_specs=[pl.BlockSpec((tm, A), lambda i: (i, 0)), pl.BlockSpec((tm, B), lambda i: (i, 0)),
                  pl.BlockSpec((A + B, D), lambda i: (0, 0)), tile, tile,
                  pl.BlockSpec((1, D), lambda i: (0, 0))],
        out_specs=[tile, tile, pl.BlockSpec((1, LANES), lambda i: (0, 0)),
                   pl.BlockSpec((1, D), lambda i: (0, 0))],
        out_shape=[jax.ShapeDtypeStruct((T, D), F32), jax.ShapeDtypeStruct((T, D), BF16),
                   jax.ShapeDtypeStruct((1, LANES), F32), jax.ShapeDtypeStruct((1, D), F32)],
        compiler_params=_cparams(("arbitrary",)),
    )(oa, ob, wout, x, tgt, fw)


def _adamw(w, g, m, v, name):
    R, Cn = w.shape
    cap = max(8, 512 * 1024 // Cn)
    tr = max(t for t in range(8, min(R, cap) + 1, 8) if R % t == 0) if R > cap else R

    def body(w_ref, g_ref, m_ref, v_ref, d_ref, mo_ref, vo_ref):
        g = g_ref[...]
        m = ADAM_B1 * m_ref[...] + (1.0 - ADAM_B1) * g
        v = ADAM_B2 * v_ref[...] + (1.0 - ADAM_B2) * jnp.square(g)
        m_hat = m / (1.0 - ADAM_B1 ** ADAM_STEP)
        v_hat = v / (1.0 - ADAM_B2 ** ADAM_STEP)
        d_ref[...] = -ADAM_LR * (m_hat / (jnp.sqrt(v_hat) + ADAM_EPS) + ADAM_WD * w_ref[...])
        mo_ref[...] = m
        vo_ref[...] = v

    tile = pl.BlockSpec((tr, Cn), lambda i: (i, 0))
    shape = jax.ShapeDtypeStruct((R, Cn), F32)
    return pl.pallas_call(
        body, name=name, grid=(R // tr,), in_specs=[tile] * 4, out_specs=[tile] * 3,
        out_shape=[shape] * 3, compiler_params=_cparams(("parallel",)),
    )(w, g, m, v)


def _place():
    x, y, c = lax.axis_index("x"), lax.axis_index("y"), lax.axis_index("c")
    others = [(1 - x, y), (x, 1 - y), (1 - x, 1 - y)]
    return x, y, c, others


def _chip_index(px, py):
    return 2 * px + py


ANY = pl.BlockSpec(memory_space=pl.ANY)


def _gather_ride(blocks, split):
    n = len(blocks)

    def plan(in_refs, out_refs, send_sems, recv_sems):
        x, y, c, _ = _place()
        me, kx, ky, kd = (_chip_index(px, py) for px, py in ((x, y), (1 - x, y), (x, 1 - y), (1 - x, 1 - y)))
        to_x, to_y, to_s = (1 - x, y, c), (x, 1 - y, c), (x, y, 1 - c)

        def copy(sem, src, dst, to):
            return pltpu.make_async_remote_copy(src_ref=src, dst_ref=dst, send_sem=send_sems.at[sem],
                                                recv_sem=recv_sems.at[sem], device_id=to, device_id_type=MESH_ID)

        first, second, third, awaited = [], [], [], []
        for a in range(n):
            out, s0 = out_refs[a], 8 * a
            if not split[a]:
                for j, (k, to) in enumerate(((kx, to_x), (ky, to_y), (kd, (1 - x, 1 - y, c)))):
                    first.append(lambda j=j, to=to, a=a, out=out, s0=s0: copy(s0 + j, in_refs[a], out.at[me], to))
                    awaited.append((lambda j=j, k=k, to=to, out=out, s0=s0: copy(s0 + j, out.at[k], out.at[k], to),
                                    None))
                continue
            h = blocks[a].shape[0] // 2
            q = h // 2
            half = lambda k, core, out=out, h=h: out.at[k, pl.ds(core * h, h), :]
            quarter = lambda k, core, i, out=out, h=h, q=q: out.at[k, pl.ds(core * h + i * q, q), :]
            mine = in_refs[a].at[pl.ds(c * h, h), :]
            first.append(lambda s0=s0, mine=mine, half=half: copy(s0, mine, half(me, c), to_x))
            first.append(lambda s0=s0, mine=mine, half=half: copy(s0 + 1, mine, half(me, c), to_y))
            fwd0 = lambda s0=s0, quarter=quarter: copy(s0 + 2, quarter(kx, c, 0), quarter(kx, c, 0), to_y)
            fwd1 = lambda s0=s0, quarter=quarter: copy(s0 + 3, quarter(ky, c, 1), quarter(ky, c, 1), to_x)
            pieces = [(s0 + 0, lambda half=half: half(kx, c), lambda half=half: half(kx, 1 - c), to_x, fwd0),
                      (s0 + 1, lambda half=half: half(ky, c), lambda half=half: half(ky, 1 - c), to_y, fwd1),
                      (s0 + 2, lambda quarter=quarter: quarter(kd, c, 0), lambda quarter=quarter: quarter(kd, 1 - c, 0),
                       to_y, None),
                      (s0 + 3, lambda quarter=quarter: quarter(kd, c, 1), lambda quarter=quarter: quarter(kd, 1 - c, 1),
                       to_x, None)]
            for i, (sem, here, there, frm, fwd) in enumerate(pieces):
                passing = lambda s0=s0, i=i, here=here: copy(s0 + 4 + i, here(), here(), to_s)
                awaited.append((lambda sem=sem, here=here, frm=frm: copy(sem, here(), here(), frm), (fwd, passing)))
                if fwd is not None:
                    second.append(fwd)
                third.append((passing, lambda s0=s0, i=i, there=there: copy(s0 + 4 + i, there(), there(), to_s)))
        return first, second, third, awaited

    def start(*refs):
        for send in plan(*refs)[0]:
            send().start()

    def finish(*refs):
        first, second, third, awaited = plan(*refs)
        for arrival, then in awaited:
            arrival().wait_recv()
            for nxt in (then or ()):
                if nxt is not None:
                    nxt().start()
        for _, from_sibling in third:
            from_sibling().wait_recv()
        for send in first + second + [p for p, _ in third]:
            send().wait_send()

    shapes = [jax.ShapeDtypeStruct((N_CHIPS,) + b.shape, b.dtype) for b in blocks]
    return _Ride(blocks, shapes, 8 * n, start, finish)


def _put_own(gathered, own):
    me = _chip_index(lax.axis_index("x"), lax.axis_index("y"))
    return lax.dynamic_update_index_in_dim(gathered, own, me, 0)


def _allreduce_small(buf):
    R0, L = buf.shape
    R = -(-R0 // 16) * 16
    h = R // 2
    buf = jnp.pad(buf, ((0, R - R0), (0, 0)))

    def body(in_ref, out_ref, sib_ref, pair_ref, chips_ref, send_sems, recv_sems):
        x, y, c, others = _place()
        me = _chip_index(x, y)
        sibling = (x, y, 1 - c)

        def copy(sem, src, dst, to):
            return pltpu.make_async_remote_copy(src_ref=src, dst_ref=dst, send_sem=send_sems.at[sem],
                                                recv_sem=recv_sems.at[sem], device_id=to, device_id_type=MESH_ID)

        cp = copy(0, in_ref, sib_ref, sibling)
        cp.start()
        cp.wait()
        pair_ref[...] = in_ref[...] + sib_ref[...]
        rows = lambda core: pl.ds(pl.multiple_of(core * h, 8), h)
        sends = [copy(1 + j, pair_ref.at[rows(c), :], chips_ref.at[me], (*chip, c)) for j, chip in enumerate(others)]
        for s in sends:
            s.start()
        chips_ref[me] = pair_ref[rows(c), :]
        for j, chip in enumerate(others):
            k = _chip_index(*chip)
            copy(1 + j, chips_ref.at[k], chips_ref.at[k], (*chip, c)).wait_recv()
        out_ref[rows(c), :] = ((chips_ref[0] + chips_ref[1]) + chips_ref[2]) + chips_ref[3]
        swap = copy(4, out_ref.at[rows(c), :], out_ref.at[rows(c), :], sibling)
        swap.start()
        copy(4, out_ref.at[rows(1 - c), :], out_ref.at[rows(1 - c), :], sibling).wait_recv()
        for s in sends + [swap]:
            s.wait_send()

    vm = pl.BlockSpec(memory_space=pltpu.VMEM)
    return pl.pallas_call(
        body, name="allreduce_small", in_specs=[vm], out_specs=vm,
        out_shape=jax.ShapeDtypeStruct((R, L), F32),
        scratch_shapes=[pltpu.VMEM((R, L), F32), pltpu.VMEM((R, L), F32), pltpu.VMEM((N_CHIPS, h, L), F32),
                        pltpu.SemaphoreType.DMA((5,)), pltpu.SemaphoreType.DMA((5,))],
        compiler_params=pltpu.CompilerParams(vmem_limit_bytes=VMEM_LIMIT),
    )(buf)[:R0]


def _pair_ride(g):
    nb, R, Cn = g.shape
    h = R // 2

    def copy(in_refs, out_refs, send_sems, recv_sems):
        x, y, c, _ = _place()
        return pltpu.make_async_remote_copy(src_ref=in_refs[0].at[:, pl.ds((1 - c) * h, h), :], dst_ref=out_refs[0],
                                            send_sem=send_sems.at[0], recv_sem=recv_sems.at[0],
                                            device_id=(x, y, 1 - c), device_id_type=MESH_ID)

    return _Ride([g], [jax.ShapeDtypeStruct((nb, h, Cn), g.dtype)], 1,
                 lambda *refs: copy(*refs).start(), lambda *refs: copy(*refs).wait())


def _pair_sum(g, land, c_arr, name, ride=None):
    nb, R, Cn = g.shape
    hr = R // 2
    tr = _pick(hr, (256, 128, 64, 32, 16))
    nt = hr // tr

    def body(c_ref, g_ref, l_ref, o_ref):
        o_ref[...] = (g_ref[...].astype(F32) + l_ref[...].astype(F32)).astype(BF16)

    return _pallas(
        body, (c_arr, g, land), name=name, prefetch=1, grid=(nb, nt),
        in_specs=[pl.BlockSpec((1, tr, Cn), lambda b, i, c_ref: (b, c_ref[0] * nt + i, 0)),
                  pl.BlockSpec((1, tr, Cn), lambda b, i, c_ref: (b, i, 0))],
        out_specs=pl.BlockSpec((1, tr, Cn), lambda b, i, c_ref: (b, i, 0)),
        out_shape=jax.ShapeDtypeStruct((nb, hr, Cn), BF16),
        semantics=("parallel", "parallel"), ride=ride)


def _chip_ride(parts, cols=None):
    m = len(parts)

    def copies(in_refs, out_refs, send_sems, recv_sems):
        x, y, c, others = _place()
        me = _chip_index(x, y)
        def mk(j, chip, n, landing):
            k = _chip_index(*chip)
            src = in_refs[n].at[k]
            if cols is not None:
                src = src.at[:, pl.ds(pl.multiple_of(cols[0](k), LANES), cols[1])]
            return pltpu.make_async_remote_copy(
                src_ref=src, dst_ref=out_refs[n].at[landing(k)], send_sem=send_sems.at[m * j + n],
                recv_sem=recv_sems.at[m * j + n], device_id=(*chip, c), device_id_type=MESH_ID)

        pairs = [(j, chip, n) for j, chip in enumerate(others) for n in range(m)]
        return pairs, (lambda *p: mk(*p, lambda k: me)), (lambda *p: mk(*p, lambda k: k))

    def start(*refs):
        pairs, send, _ = copies(*refs)
        for p in pairs:
            send(*p).start()

    def finish(*refs):
        pairs, send, arrival = copies(*refs)
        for p in pairs:
            arrival(*p).wait_recv()
        for p in pairs:
            send(*p).wait_send()

    width = lambda p: p.shape[2] if cols is None else cols[1]
    return _Ride(parts, [jax.ShapeDtypeStruct(p.shape[:2] + (width(p),), p.dtype) for p in parts], 3 * m,
                 start, finish)


def _put_own_slot(q, p, cols=None):
    me = _chip_index(lax.axis_index("x"), lax.axis_index("y"))
    own = lax.dynamic_index_in_dim(p, me, 0, keepdims=False)
    if cols is not None:
        own = lax.dynamic_slice_in_dim(own, cols[0](me), cols[1], axis=1)
    return lax.dynamic_update_index_in_dim(q, own, me, 0)


def _chip_sum(q, c_arr, name):
    nb, hr, Cn = q.shape
    tr = _pick(hr, (256, 128, 64, 32, 16))
    nt = hr // tr

    def body(c_ref, q_ref, o_ref):
        f = lambda k: q_ref[k].astype(F32)
        o_ref[...] = ((f(0) + f(1)) + f(2)) + f(3)

    return _pallas(
        body, (c_arr, q), name=name, prefetch=1, grid=(nt,),
        in_specs=[pl.BlockSpec((nb, tr, Cn), lambda i, c_ref: (0, i, 0))],
        out_specs=pl.BlockSpec((tr, Cn), lambda i, c_ref: (c_ref[0] * nt + i, 0)),
        out_shape=jax.ShapeDtypeStruct((2 * hr, Cn), F32),
        semantics=("parallel",))


def _sibling_fill(fw, fo):
    def body(_, __, fw_ref, fo_ref, send_sems, recv_sems):
        x, y, c, _ = _place()
        copies = []
        for n, ref in enumerate((fw_ref, fo_ref)):
            h = ref.shape[0] // 2
            mine = ref.at[pl.ds(c * h, h), :]
            theirs = ref.at[pl.ds((1 - c) * h, h), :]
            mk = lambda src, dst: pltpu.make_async_remote_copy(
                src_ref=src, dst_ref=dst, send_sem=send_sems.at[n], recv_sem=recv_sems.at[n],
                device_id=(x, y, 1 - c), device_id_type=MESH_ID)
            send = mk(mine, mine)
            send.start()
            copies.append((send, mk(theirs, theirs)))
        for send, arrival in copies:
            arrival.wait_recv()
            send.wait_send()

    return pl.pallas_call(
        body, name="sibling_fill", in_specs=[ANY, ANY], out_specs=[ANY, ANY],
        out_shape=[jax.ShapeDtypeStruct(fw.shape, F32), jax.ShapeDtypeStruct(fo.shape, F32)],
        input_output_aliases={0: 0, 1: 1},
        scratch_shapes=[pltpu.SemaphoreType.DMA((2,)), pltpu.SemaphoreType.DMA((2,))],
        compiler_params=pltpu.CompilerParams(has_side_effects=True),
    )(fw, fo)


class _Layout:
    def __init__(self, H, G, nb, Cb):
        A, B = H * HEAD_DIM, G * HEAD_DIM
        self.n_main = 4 * A + 3 * B
        self.k = -(-(self.n_main + LANES) // WIN_BLOCK) * WIN_BLOCK
        cuts = [0, 3 * A, 4 * A, 4 * A + 2 * H, nb * Cb]
        starts = [0, 3 * A + 3 * B, self.n_main, 3 * A]
        self.pieces = []
        self.windows, self.runs = [], []
        for n in range(nb):
            segs = []
            for s in range(4):
                lo, hi = max(cuts[s], n * Cb), min(cuts[s + 1], (n + 1) * Cb)
                if lo < hi:
                    segs.append((starts[s] + lo - cuts[s], lo - n * Cb, hi - lo))
            self.pieces += [(own, n, col, ln) for own, col, ln in segs]
            blocks = sorted({b for own, _, ln in segs for b in range(own // WIN_BLOCK, (own + ln - 1) // WIN_BLOCK + 1)})
            self.windows.append(blocks)
            self.runs.append([(blocks.index(own // WIN_BLOCK) * WIN_BLOCK + own % WIN_BLOCK, ln)
                              for own, _, ln in segs])
        self.wb = max(len(b) for b in self.windows)
        self.table = [b + [b[-1]] * (self.wb - len(b)) for b in self.windows]
        self.pieces.sort()
        self.used_from = [min(c for c, _ in r) // LANES * LANES for r in self.runs]
        self.used = max(-(-max(c + ln for c, ln in r) // LANES) * LANES - f for r, f in zip(self.runs, self.used_from))
        self.used_from = [min(f, self.wb * WIN_BLOCK - self.used) for f in self.used_from]

    def to_own_order(self, g_in):
        nb, D, Cb = g_in.shape
        tr = _pick(D, (256, 128))

        def body(g_ref, o_ref):
            cols, at = [], 0
            for own, n, col, ln in self.pieces:
                if own > at:
                    cols.append(jnp.zeros((tr, own - at), g_in.dtype))
                cols.append(g_ref[n, :, col:col + ln])
                at = own + ln
            if at < self.k:
                cols.append(jnp.zeros((tr, self.k - at), g_in.dtype))
            o_ref[...] = jnp.concatenate(cols, axis=1)

        return pl.pallas_call(
            body, name="own_order", grid=(D // tr,),
            in_specs=[pl.BlockSpec((nb, tr, Cb), lambda i: (0, i, 0))],
            out_specs=pl.BlockSpec((tr, self.k), lambda i: (i, 0)),
            out_shape=jax.ShapeDtypeStruct((D, self.k), g_in.dtype),
            compiler_params=_cparams(("parallel",)),
        )(g_in)

    def from_window(self, win, chip, Cb):
        pick = lambda runs, f: (lambda w: jnp.concatenate([w[:, c - f:c - f + ln] for c, ln in runs], axis=1))
        return lax.switch(chip, [pick(r, f) for r, f in zip(self.runs, self.used_from)], win)

    def used_start(self, chip):
        return sum(jnp.where(chip == n, f, 0) for n, f in enumerate(self.used_from))


def _device_step(x, tgt, norm_w, win_b, wout_b, conv_b, a_log, dt_bias, head_norm_w, sgu_ln_w, sgu_ln_b,
                 w_spatial, b_spatial, final_norm_w, c_arr):
    T, D = x.shape
    H = a_log.shape[1]
    A = H * HEAD_DIM
    G = w_spatial.shape[0]
    B = G * HEAD_DIM
    nb, Cb, Rb = N_CHIPS, win_b.shape[1], wout_b.shape[0]
    lay = _Layout(H, G, nb, Cb)
    alog_row = jnp.pad(a_log, ((0, 0), (H, LANES - 2 * H)))
    dtb_row = jnp.pad(dt_bias, ((0, 0), (H, LANES - 2 * H)))
    bbc = jnp.broadcast_to(b_spatial[:, :, None], (G, CHUNK_B, CHUNK_B))

    (xn, xn_t), (g_in,) = _rms_in(x, norm_w, ride=_gather_ride([win_b], [True]))
    w_own = lay.to_own_order(_put_own(g_in, win_b))
    proj_m, (g_out, g_conv) = _mm_nn(xn, w_own, F32, "in_proj", tm=2048, cols=(0, lay.n_main),
                                     ride=_gather_ride([wout_b, conv_b], [False, False]))
    wout = _put_own(g_out, wout_b).reshape(nb * Rb, D)
    conv_w = _put_own(g_conv, conv_b).transpose(1, 0, 2).reshape(CONV_WIDTH, nb * conv_b.shape[1])
    q, k, v, gb, bb, proj_ba = _gdn_pre(proj_m, xn, w_own, conv_w, alog_row, dtb_row, H)
    u, w, qg, kd, attn, eg, pinv = _gdn_prep(q, k, v, gb, bb)
    og, sall = _gdn_chain(qg, kd, u, w, attn, eg)
    oa, oa_t = _gdn_post(og, proj_m, head_norm_w)
    ob, ob_t = _sgu_fwd(proj_m, sgu_ln_w, sgu_ln_b, w_spatial, bbc, A)
    dh, dhb, loss_row, d_fnw = _out_proj_loss(oa, ob, wout, x, tgt, final_norm_w.reshape(1, D))

    d_o = _mm_nn(dhb, wout.T, F32, "out_proj_dx", tm=2048)
    dproj = lax.empty((T, lay.k), BF16)
    dproj, d_lw, d_lb, d_ws, d_bs = _sgu_bwd(proj_m, sgu_ln_w, sgu_ln_b, w_spatial, bbc, d_o, A, dproj)
    dog, dproj, d_hw = _gdn_post_bwd(og, proj_m, head_norm_w, d_o, dproj)
    dqg, dkd, du, dw, dat, deg = _gdn_chain_bwd(qg, kd, u, w, attn, eg, sall, dog)
    dq, dk, dv, dgb, dbb = _gdn_prep_bwd(q, k, v, gb, bb, pinv, du, dw, dqg, dkd, dat, deg)
    dc, dproj, d_al, d_dt = _gdn_pre_bwd(proj_m, proj_ba, conv_w, alog_row, dtb_row, dq, dk, dv, dgb, dbb, H,
                                         dproj)
    dproj, d_conv = _conv_bwd(proj_m, dc, conv_w, H, dproj)

    table = jnp.array([b for row in lay.table for b in row], jnp.int32)
    d_win = _mm_windows(xn_t, dproj, table, nb, "in_proj_dw")
    d_wout, (land_w,) = _mm_nn_pair(oa_t, ob_t, dhb, "out_proj_dw", ride=_pair_ride(d_win))
    d_wout = d_wout.reshape(nb, Rb, D)
    pair_w, (land_o,) = _pair_sum(d_win, land_w, c_arr, "pair_sum_w_in", ride=_pair_ride(d_wout))
    pair_o = _pair_sum(d_wout, land_o, c_arr, "pair_sum_w_out")
    used = (lay.used_start, lay.used)
    dxn, (all_w,) = _mm_nt_rhs_outer(dproj, w_own, F32, "in_proj_dx", ride=_chip_ride([pair_w], used))
    (grad_x, d_nw), (all_o,) = _rms_in_bwd(x, norm_w, dxn, dh, ride=_chip_ride([pair_o]))
    all_w, all_o = _put_own_slot(all_w, pair_w, used), _put_own_slot(all_o, pair_o)
    small = dict(norm_w=d_nw, conv_w=d_conv[:CONV_WIDTH], a_log=d_al[:, H:2 * H], dt_bias=d_dt[:, H:2 * H],
                 head_norm_w=d_hw, sgu_ln_w=d_lw, sgu_ln_b=d_lb, w_spatial=d_ws, b_spatial=d_bs[:, :, 0],
                 final_norm_w=d_fnw)
    return loss_row, grad_x, small, all_w, all_o


SMALL = ("norm_w", "conv_w", "a_log", "dt_bias", "head_norm_w", "sgu_ln_w", "sgu_ln_b", "w_spatial",
         "b_spatial", "final_norm_w")


def _pack(parts):
    rows = []
    for p in parts:
        f = p.reshape(-1)
        f = jnp.pad(f, (0, (-f.shape[0]) % (8 * LANES)))
        rows.append(f.reshape(-1, LANES))
    return jnp.concatenate(rows, axis=0)


def _unpack(buf, shapes):
    out, r = [], 0
    for s in shapes:
        n = 1
        for d in s:
            n *= d
        nr = -(-n // (8 * LANES)) * 8
        out.append(buf[r:r + nr].reshape(-1)[:n].reshape(s))
        r += nr
    return out


def kernel(x, norm_w, w_in, conv_w, a_log, dt_bias, head_norm_w, sgu_ln_w, sgu_ln_b, w_spatial, b_spatial, w_out, final_norm_w, loss_target, m_norm_w, m_w_in, m_conv_w, m_a_log, m_dt_bias, m_head_norm_w, m_sgu_ln_w, m_sgu_ln_b, m_w_spatial, m_b_spatial, m_w_out, m_final_norm_w, v_norm_w, v_w_in, v_conv_w, v_a_log, v_dt_bias, v_head_norm_w, v_sgu_ln_w, v_sgu_ln_b, v_w_spatial, v_b_spatial, v_w_out, v_final_norm_w):
    T, D = x.shape[1], x.shape[2]
    weights = dict(norm_w=norm_w, w_in=w_in, conv_w=conv_w, a_log=a_log, dt_bias=dt_bias, head_norm_w=head_norm_w,
                   sgu_ln_w=sgu_ln_w, sgu_ln_b=sgu_ln_b, w_spatial=w_spatial, b_spatial=b_spatial, w_out=w_out,
                   final_norm_w=final_norm_w)
    mom_m = dict(norm_w=m_norm_w, w_in=m_w_in, conv_w=m_conv_w, a_log=m_a_log, dt_bias=m_dt_bias,
                 head_norm_w=m_head_norm_w, sgu_ln_w=m_sgu_ln_w, sgu_ln_b=m_sgu_ln_b, w_spatial=m_w_spatial,
                 b_spatial=m_b_spatial, w_out=m_w_out, final_norm_w=m_final_norm_w)
    mom_v = dict(norm_w=v_norm_w, w_in=v_w_in, conv_w=v_conv_w, a_log=v_a_log, dt_bias=v_dt_bias,
                 head_norm_w=v_head_norm_w, sgu_ln_w=v_sgu_ln_w, sgu_ln_b=v_sgu_ln_b, w_spatial=v_w_spatial,
                 b_spatial=v_b_spatial, w_out=v_w_out, final_norm_w=v_final_norm_w)
    me = _chip_index(lax.axis_index("x"), lax.axis_index("y"))
    c_arr = lax.axis_index("c").astype(jnp.int32).reshape(1)
    Din, Cb = w_in.shape[1], w_in.shape[2]
    Rb = w_out.shape[1]
    cconv = conv_w.shape[2]

    loss_row, grad_x, g, qw, qo = _device_step(
        x[0], loss_target[0], norm_w, w_in[0].astype(BF16), w_out[0].astype(BF16), conv_w[0], a_log, dt_bias,
        head_norm_w, sgu_ln_w, sgu_ln_b, w_spatial[0], b_spatial[0], final_norm_w, c_arr)

    small_shapes = [tuple(g[n].shape) for n in SMALL] + [(1, LANES)]
    small = _allreduce_small(_pack([g[n] for n in SMALL] + [loss_row]))
    gsum_in, gsum_out = _sibling_fill(_chip_sum(qw, c_arr, "chip_sum_w_in"), _chip_sum(qo, c_arr, "chip_sum_w_out"))
    gsum_in = _Layout(a_log.shape[1], w_spatial.shape[1], N_CHIPS, Cb).from_window(gsum_in, me, Cb)
    *small, loss_sum = _unpack(small, small_shapes)
    gsmall = dict(zip(SMALL, small))
    gsmall["conv_w"] = lax.dynamic_slice_in_dim(gsmall["conv_w"], me * cconv, cconv, axis=1)

    grads, deltas, new_m, new_v = {}, {}, {}, {}
    d, m2, v2 = _adamw(w_out[0], gsum_out, m_w_out[0], v_w_out[0], "adamw_w_out")
    grads["w_out"], deltas["w_out"], new_m["w_out"], new_v["w_out"] = gsum_out[None], d[None], m2[None], v2[None]
    flat = lambda a: a.transpose(2, 0, 1).reshape(-1, LANES)
    unflat = lambda f: f.reshape(Cb, 1, Din).transpose(1, 2, 0)
    g_flat = gsum_in.T.reshape(-1, LANES)
    d, m2, v2 = _adamw(flat(w_in), g_flat, flat(m_w_in), flat(v_w_in), "adamw_w_in")
    grads["w_in"], deltas["w_in"], new_m["w_in"], new_v["w_in"] = unflat(g_flat), unflat(d), unflat(m2), unflat(v2)
    shapes = [tuple(weights[n].shape) for n in SMALL]
    ds, ms, vs = _adamw(_pack([weights[n] for n in SMALL]), _pack([gsmall[n] for n in SMALL]),
                        _pack([mom_m[n] for n in SMALL]), _pack([mom_v[n] for n in SMALL]), "adamw_small")
    for n, gq, d, m2, v2 in zip(SMALL, [gsmall[n] for n in SMALL], _unpack(ds, shapes), _unpack(ms, shapes),
                                _unpack(vs, shapes)):
        grads[n], deltas[n], new_m[n], new_v[n] = gq.reshape(weights[n].shape), d, m2, v2

    loss = loss_sum[0, 0]
    order = ("norm_w", "w_in", "conv_w", "a_log", "dt_bias", "head_norm_w", "sgu_ln_w", "sgu_ln_b", "w_spatial",
             "b_spatial", "w_out", "final_norm_w")
    return (loss, grad_x[None], *[grads[n] for n in order], *[deltas[n] for n in order],
            *[new_m[n] for n in order], *[new_v[n] for n in order])
```

```python
import functools

import jax
import jax.numpy as jnp
from jax import lax
from jax.experimental import pallas as pl
from jax.experimental.pallas import tpu as pltpu

F32 = jnp.float32
BF16 = jnp.bfloat16
EPS = 1e-6
HEAD_DIM = 128
CHUNK_B = 128
CONV_WIDTH = 4
LANES = 128
HALO = 8
N_CHIPS = 4
ADAM_LR = 0.001
ADAM_B1 = 0.9
ADAM_B2 = 0.999
ADAM_EPS = 1e-08
ADAM_WD = 0.01
ADAM_STEP = 10
VMEM_LIMIT = 56 * 1024 * 1024
MESH_ID = pl.DeviceIdType.MESH


def _cparams(sem=None, **kw):
    return pltpu.CompilerParams(dimension_semantics=sem, vmem_limit_bytes=VMEM_LIMIT, **kw)


def _matmul(a, b, ca, cb):
    nb = a.ndim - 2
    batch = tuple(range(nb))
    return lax.dot_general(a, b, (((ca + nb,), (cb + nb,)), (batch, batch)), preferred_element_type=F32)


def _dot(a, b):
    return _matmul(a, b, 1, 0)


def _dot_nt(a, b):
    return _matmul(a, b, 1, 1)


def _dot_tn(a, b):
    return _matmul(a, b, 0, 0)


def _iota(shape, dim):
    return lax.broadcasted_iota(jnp.int32, shape, dim)


def _sigmoid(x):
    return 0.5 * (jnp.tanh(0.5 * x) + 1.0)


def _silu(x):
    return x * _sigmoid(x)


def _softplus(x):
    z = jnp.exp(-jnp.abs(x))
    small = z * (1.0 - z * (0.5 - z * (1.0 / 3.0)))
    return jnp.maximum(x, 0.0) + jnp.where(z < 1e-3, small, jnp.log(1.0 + z))


def _pick(n, pref):
    for t in pref:
        if n % t == 0:
            return t
    return n


class _Ride:
    def __init__(self, operands, out_shape, n_sems, start, finish):
        self.operands, self.out_shape, self.n_sems = list(operands), list(out_shape), n_sems
        self.start, self.finish = start, finish


def _pallas(body, operands, *, name, grid, in_specs, out_specs, out_shape, semantics, scratch_shapes=(),
            prefetch=0, ride=None):
    single = not isinstance(out_shape, (list, tuple))
    outs = [out_shape] if single else list(out_shape)
    ospecs = [out_specs] if single else list(out_specs)
    in_specs, scratch = list(in_specs), list(scratch_shapes)
    n_in, n_out, n_sc = len(operands) - prefetch, len(outs), len(scratch)
    kernel = body
    params = _cparams(semantics)
    if ride is not None:
        n_xin, n_xout = len(ride.operands), len(ride.out_shape)

        def kernel(*refs):
            pre, refs = refs[:prefetch], refs[prefetch:]
            ins, refs = refs[:n_in], refs[n_in:]
            xins, refs = refs[:n_xin], refs[n_xin:]
            mains, refs = refs[:n_out], refs[n_out:]
            xouts, refs = refs[:n_xout], refs[n_xout:]
            sc, (send, recv) = refs[:n_sc], refs[n_sc:]
            ids = [pl.program_id(a) for a in range(len(grid))]
            first = functools.reduce(jnp.logical_and, [i == 0 for i in ids])
            last = functools.reduce(jnp.logical_and, [i == g - 1 for i, g in zip(ids, grid)])

            @pl.when(first)
            def _():
                ride.start(xins, xouts, send, recv)

            body(*pre, *ins, *mains, *sc)

            @pl.when(last)
            def _():
                ride.finish(xins, xouts, send, recv)

        operands = list(operands) + ride.operands
        in_specs += [ANY] * n_xin
        ospecs += [ANY] * n_xout
        outs += ride.out_shape
        scratch += [pltpu.SemaphoreType.DMA((ride.n_sems,)), pltpu.SemaphoreType.DMA((ride.n_sems,))]
        params = _cparams(("arbitrary",) * len(grid), has_side_effects=True)
    if prefetch:
        spec = dict(grid_spec=pltpu.PrefetchScalarGridSpec(
            num_scalar_prefetch=prefetch, grid=grid, in_specs=in_specs, out_specs=ospecs, scratch_shapes=scratch))
    else:
        spec = dict(grid=grid, in_specs=in_specs, out_specs=ospecs, scratch_shapes=scratch)
    res = pl.pallas_call(kernel, name=name, out_shape=outs, compiler_params=params, **spec)(*operands)
    main = res[0] if single else list(res[:n_out])
    return main if ride is None else (main, list(res[n_out:]))


def _mm_nn(a, b, out_dtype, name, tm=1024, tn=512, tk=None, cols=None, ride=None):
    M, K = a.shape
    c0, N = (0, b.shape[1]) if cols is None else cols
    tm = _pick(M, (tm, 1024, 512, 256, 128))
    tn = _pick(N, (tn, 512, 384, 256, 128))
    tk = K if tk is None else _pick(K, (tk,))
    nk = K // tk
    j0 = c0 // tn
    assert c0 % tn == 0

    def body(a_ref, b_ref, o_ref, *scratch):
        part = _dot(a_ref[...], b_ref[...])
        if nk == 1:
            o_ref[...] = part.astype(out_dtype)
        else:
            acc_ref, = scratch
            k = pl.program_id(2)

            @pl.when(k == 0)
            def _():
                acc_ref[...] = part

            @pl.when(k > 0)
            def _():
                acc_ref[...] += part

            @pl.when(k == nk - 1)
            def _():
                o_ref[...] = acc_ref[...].astype(out_dtype)

    return _pallas(
        body, (a, b), name=name, grid=(M // tm, N // tn, nk),
        in_specs=[pl.BlockSpec((tm, tk), lambda i, j, k: (i, k)),
                  pl.BlockSpec((tk, tn), lambda i, j, k: (k, j + j0))],
        out_specs=pl.BlockSpec((tm, tn), lambda i, j, k: (i, j)),
        out_shape=jax.ShapeDtypeStruct((M, N), out_dtype),
        scratch_shapes=[] if nk == 1 else [pltpu.VMEM((tm, tn), F32)],
        semantics=("parallel", "parallel", "arbitrary"), ride=ride)


def _mm_nt_rhs_outer(a, b, out_dtype, name, tm=256, tn=1024, ride=None):
    M, K = a.shape
    N, _ = b.shape
    tm = _pick(M, (tm, 128))
    tn = _pick(N, (tn, 512, 256, 128))

    def body(a_ref, b_ref, o_ref):
        o_ref[...] = _dot_nt(a_ref[...], b_ref[...]).astype(out_dtype)

    return _pallas(
        body, (a, b), name=name, grid=(N // tn, M // tm),
        in_specs=[pl.BlockSpec((tm, K), lambda j, i: (i, 0)),
                  pl.BlockSpec((tn, K), lambda j, i: (j, 0))],
        out_specs=pl.BlockSpec((tm, tn), lambda j, i: (i, j)),
        out_shape=jax.ShapeDtypeStruct((M, N), out_dtype),
        semantics=("parallel", "parallel"), ride=ride)


WIN_BLOCK = 256
WIN_DEPTH = 3
WIN_PARTS = 4


def _mm_windows(a, b, table, nb, name, tm=2048):
    M, K = a.shape
    wb = table.shape[0] // nb
    steps = nb * wb
    depth = min(WIN_DEPTH, steps)
    parts = WIN_PARTS
    rows = K // parts

    def body(tab_ref, a_ref, b_ref, o_ref, buf, sems):
        s = pl.program_id(0)

        def fetch(step, slot):
            col = pl.multiple_of(tab_ref[step] * WIN_BLOCK, WIN_BLOCK)
            return [pltpu.make_async_copy(
                b_ref.at[pl.ds(p * rows, rows), pl.ds(col, WIN_BLOCK)],
                buf.at[slot, pl.ds(p * rows, rows)], sems.at[slot, p]) for p in range(parts)]

        @pl.when(s == 0)
        def _():
            for j in range(depth - 1):
                for c in fetch(j, j):
                    c.start()

        ahead = s + (depth - 1)

        @pl.when(ahead < steps)
        def _():
            for c in fetch(ahead, ahead % depth):
                c.start()

        slot = s % depth
        for c in fetch(s, slot):
            c.wait()
        o_ref[0] = _dot(a_ref[...], buf[slot]).astype(BF16)

    return pl.pallas_call(
        body, name=name,
        grid_spec=pltpu.PrefetchScalarGridSpec(
            num_scalar_prefetch=1, grid=(steps,),
            in_specs=[pl.BlockSpec((M, K), lambda s, tab: (0, 0)), ANY],
            out_specs=pl.BlockSpec((1, M, WIN_BLOCK), lambda s, tab: (s // wb, 0, s % wb)),
            scratch_shapes=[pltpu.VMEM((depth, K, WIN_BLOCK), BF16),
                            pltpu.SemaphoreType.DMA((depth, parts))]),
        out_shape=jax.ShapeDtypeStruct((nb, M, wb * WIN_BLOCK), BF16),
        compiler_params=_cparams(("arbitrary",)),
    )(table, a, b)


def _mm_nn_pair(a0, a1, b, name, tm=512, tn=1024, ride=None):
    M, K = a0.shape
    _, N = b.shape
    tm = _pick(M, (tm, 256, 128))
    tn = _pick(N, (tn, 512, 256, 128))
    ni = M // tm

    def body(a0_ref, a1_ref, b_ref, o_ref):
        p = pl.program_id(0)

        @pl.when(p == 0)
        def _():
            o_ref[...] = _dot(a0_ref[...], b_ref[...]).astype(BF16)

        @pl.when(p == 1)
        def _():
            o_ref[...] = _dot(a1_ref[...], b_ref[...]).astype(BF16)

    return _pallas(
        body, (a0, a1, b), name=name, grid=(2, ni, N // tn),
        in_specs=[pl.BlockSpec((tm, K), lambda p, i, j: (i * (1 - p), 0)),
                  pl.BlockSpec((tm, K), lambda p, i, j: (i * p, 0)),
                  pl.BlockSpec((K, tn), lambda p, i, j: (0, j))],
        out_specs=pl.BlockSpec((tm, tn), lambda p, i, j: (p * ni + i, j)),
        out_shape=jax.ShapeDtypeStruct((2 * M, N), BF16),
        semantics=("parallel", "parallel", "parallel"), ride=ride)


def _rms_fn(x, w):
    r = lax.rsqrt(jnp.mean(x * x, axis=-1, keepdims=True) + EPS)
    return x * r * w


def _rms_in(x, w, ride=None):
    T, D = x.shape
    tm = _pick(T, (512, 256, 128))

    def body(x_ref, w_ref, o_ref, ot_ref):
        xn = _rms_fn(x_ref[...], w_ref[...])
        o_ref[...] = xn.astype(BF16)
        ot_ref[...] = xn.T.astype(BF16)

    return _pallas(
        body, (x, w), name="rms_in", grid=(T // tm,),
        in_specs=[pl.BlockSpec((tm, D), lambda i: (i, 0)), pl.BlockSpec((1, D), lambda i: (0, 0))],
        out_specs=[pl.BlockSpec((tm, D), lambda i: (i, 0)), pl.BlockSpec((D, tm), lambda i: (0, i))],
        out_shape=[jax.ShapeDtypeStruct((T, D), BF16), jax.ShapeDtypeStruct((D, T), BF16)],
        semantics=("parallel",), ride=ride)


def _rms_in_bwd(x, w, dxn, dh, ride=None):
    T, D = x.shape
    tm = _pick(T, (256, 128))

    def body(x_ref, w_ref, dxn_ref, dh_ref, gx_ref, dw_ref):
        _, vjp = jax.vjp(_rms_fn, x_ref[...], w_ref[...])
        dx, dw = vjp(dxn_ref[...])
        gx_ref[...] = dh_ref[...] + dx

        @pl.when(pl.program_id(0) == 0)
        def _():
            dw_ref[...] = dw

        @pl.when(pl.program_id(0) > 0)
        def _():
            dw_ref[...] += dw

    tile = pl.BlockSpec((tm, D), lambda i: (i, 0))
    row = pl.BlockSpec((1, D), lambda i: (0, 0))
    return _pallas(
        body, (x, w, dxn, dh), name="rms_in_bwd", grid=(T // tm,),
        in_specs=[tile, row, tile, tile], out_specs=[tile, row],
        out_shape=[jax.ShapeDtypeStruct((T, D), F32), jax.ShapeDtypeStruct((1, D), F32)],
        semantics=("arbitrary",), ride=ride)


def _conv_fwd(cat_ref, halo, x, w):
    tm = x.shape[0]
    cat_ref[0:HALO, :] = halo
    cat_ref[HALO:HALO + tm, :] = x
    c = x * w[CONV_WIDTH - 1:CONV_WIDTH, :]
    for k in range(CONV_WIDTH - 1):
        s = CONV_WIDTH - 1 - k
        c = c + cat_ref[pl.ds(HALO - s, tm), :] * w[k:k + 1, :]
    return c


def _lane_to_all(x, lane):
    @jax.custom_vjp
    def f(x):
        return jnp.broadcast_to(x[:, lane:lane + 1], x.shape)

    def f_fwd(x):
        return f(x), None

    def f_bwd(_, g):
        return (jnp.where(_iota(g.shape, 1) == lane, jnp.sum(g, axis=-1, keepdims=True), 0.0),)

    f.defvjp(f_fwd, f_bwd)
    return f(x)


def _gdn_pointwise(c, ba, alog, dtb, H):
    A = H * HEAD_DIM
    s = _silu(c)
    beta = _sigmoid(ba)
    g = -jnp.exp(alog) * _softplus(ba + dtb)
    qs, ks, vs, gbs, bbs = [], [], [], [], []
    for h in range(H):
        lo = h * HEAD_DIM
        q = s[:, lo:lo + HEAD_DIM]
        k = s[:, A + lo:A + lo + HEAD_DIM]
        qs.append(q * lax.rsqrt(jnp.sum(q * q, axis=-1, keepdims=True) + EPS))
        ks.append(k * lax.rsqrt(jnp.sum(k * k, axis=-1, keepdims=True) + EPS))
        vs.append(s[:, 2 * A + lo:2 * A + lo + HEAD_DIM])
        bbs.append(_lane_to_all(beta, h))
        gbs.append(_lane_to_all(g, H + h))
    st = lambda xs: jnp.stack(xs, axis=0)
    return st(qs), st(ks), st(vs), st(gbs), st(bbs)


def _halo_prev(tm):
    return lambda i: (jnp.maximum(i * (tm // HALO) - 1, 0), 0)


def _gdn_pre(proj_m, xn, w_own, conv_w, alog_row, dtb_row, H):
    T, n_main = proj_m.shape
    D = xn.shape[1]
    A = H * HEAD_DIM
    tm = _pick(T, (256, 128))
    hs = pl.BlockSpec((H, tm, HEAD_DIM), lambda i: (0, i, 0))
    hshape = jax.ShapeDtypeStruct((H, T, HEAD_DIM), F32)

    def body(x_ref, halo_ref, xn_ref, wba_ref, w_ref, al_ref, dt_ref,
             q_ref, k_ref, v_ref, gb_ref, bb_ref, ba_ref, cat_ref):
        halo = jnp.where(pl.program_id(0) == 0, 0.0, halo_ref[...])
        c = _conv_fwd(cat_ref, halo, x_ref[...], w_ref[...])
        ba = _dot(xn_ref[...], wba_ref[...])
        q, k, v, gb, bb = _gdn_pointwise(c, ba, al_ref[...], dt_ref[...], H)
        q_ref[...] = q
        k_ref[...] = k
        v_ref[...] = v
        gb_ref[...] = gb
        bb_ref[...] = bb
        ba_ref[...] = ba

    return pl.pallas_call(
        body, name="gdn_pre", grid=(T // tm,),
        in_specs=[pl.BlockSpec((tm, 3 * A), lambda i: (i, 0)),
                  pl.BlockSpec((HALO, 3 * A), _halo_prev(tm)),
                  pl.BlockSpec((tm, D), lambda i: (i, 0)),
                  pl.BlockSpec((D, LANES), lambda i: (0, n_main // LANES)),
                  pl.BlockSpec((CONV_WIDTH, 3 * A), lambda i: (0, 0)),
                  pl.BlockSpec((1, LANES), lambda i: (0, 0)),
                  pl.BlockSpec((1, LANES), lambda i: (0, 0))],
        out_specs=[hs] * 5 + [pl.BlockSpec((tm, LANES), lambda i: (i, 0))],
        out_shape=[hshape] * 5 + [jax.ShapeDtypeStruct((T, LANES), F32)],
        scratch_shapes=[pltpu.VMEM((HALO + tm, 3 * A), F32)],
        compiler_params=_cparams(("parallel",)),
    )(proj_m, proj_m, xn, w_own, conv_w, alog_row, dtb_row)


def _gdn_pre_bwd(proj_m, proj_ba, conv_w, alog_row, dtb_row, dq, dk, dv, dgb, dbb, H, dproj):
    T, n_main = proj_m.shape
    A = H * HEAD_DIM
    tm = _pick(T, (256, 128))
    hs = pl.BlockSpec((H, tm, HEAD_DIM), lambda i: (0, i, 0))
    row = pl.BlockSpec((1, LANES), lambda i: (0, 0))

    def body(x_ref, halo_ref, ba_ref, w_ref, al_ref, dt_ref, dq_ref, dk_ref, dv_ref, dgb_ref, dbb_ref, _,
             dc_ref, dba_ref, dal_ref, ddt_ref, cat_ref):
        halo = jnp.where(pl.program_id(0) == 0, 0.0, halo_ref[...])
        c = _conv_fwd(cat_ref, halo, x_ref[...], w_ref[...])
        _, vjp = jax.vjp(functools.partial(_gdn_pointwise, H=H), c, ba_ref[...], al_ref[...], dt_ref[...])
        dc, dba, dal, ddt = vjp((dq_ref[...], dk_ref[...], dv_ref[...], dgb_ref[...], dbb_ref[...]))
        dc_ref[...] = dc
        dba_ref[:, :LANES] = dba.astype(BF16)
        dba_ref[:, LANES:] = jnp.zeros((tm, WIN_BLOCK - LANES), BF16)

        @pl.when(pl.program_id(0) == 0)
        def _():
            dal_ref[...] = dal
            ddt_ref[...] = ddt

        @pl.when(pl.program_id(0) > 0)
        def _():
            dal_ref[...] += dal
            ddt_ref[...] += ddt

    return pl.pallas_call(
        body, name="gdn_pre_bwd", grid=(T // tm,),
        in_specs=[pl.BlockSpec((tm, 3 * A), lambda i: (i, 0)),
                  pl.BlockSpec((HALO, 3 * A), _halo_prev(tm)),
                  pl.BlockSpec((tm, LANES), lambda i: (i, 0)),
                  pl.BlockSpec((CONV_WIDTH, 3 * A), lambda i: (0, 0)),
                  row, row, hs, hs, hs, hs, hs, ANY],
        out_specs=[pl.BlockSpec((tm, 3 * A), lambda i: (i, 0)),
                   pl.BlockSpec((tm, WIN_BLOCK), lambda i: (i, n_main // WIN_BLOCK)), row, row],
        out_shape=[jax.ShapeDtypeStruct((T, 3 * A), F32), jax.ShapeDtypeStruct(dproj.shape, dproj.dtype),
                   jax.ShapeDtypeStruct((1, LANES), F32), jax.ShapeDtypeStruct((1, LANES), F32)],
        input_output_aliases={11: 1},
        scratch_shapes=[pltpu.VMEM((HALO + tm, 3 * A), F32)],
        compiler_params=_cparams(("arbitrary",)),
    )(proj_m, proj_m, proj_ba, conv_w, alog_row, dtb_row, dq, dk, dv, dgb, dbb, dproj)


def _conv_bwd(proj_m, dc, conv_w, H, dproj):
    T = proj_m.shape[0]
    A = H * HEAD_DIM
    tm = _pick(T, (256, 128))
    nt = T // tm

    def body(x_ref, halo_ref, dc_ref, nxt_ref, w_ref, _, dx_ref, dw_ref):
        i = pl.program_id(0)
        halo = jnp.where(i == 0, 0.0, halo_ref[...])
        xcat = jnp.concatenate([halo, x_ref[...]], axis=0)
        nxt = jnp.where(i == nt - 1, 0.0, nxt_ref[...])
        dc = dc_ref[...]
        dcat = jnp.concatenate([dc, nxt], axis=0)
        w = w_ref[...]
        dx = None
        rows = []
        for k in range(CONV_WIDTH):
            s = CONV_WIDTH - 1 - k
            ds = dcat if s == 0 else pltpu.roll(dcat, tm + HALO - s, 0)
            term = ds[:tm, :] * w[k:k + 1, :]
            dx = term if dx is None else dx + term
            xs = xcat if s == 0 else pltpu.roll(xcat, s, 0)
            rows.append(jnp.sum(dc * xs[HALO:, :], axis=0, keepdims=True))
        dx_ref[...] = dx.astype(BF16)
        dw = jnp.concatenate(rows + [jnp.zeros((HALO - CONV_WIDTH, 3 * A), F32)], axis=0)

        @pl.when(i == 0)
        def _():
            dw_ref[...] = dw

        @pl.when(i > 0)
        def _():
            dw_ref[...] += dw

    return pl.pallas_call(
        body, name="conv_bwd", grid=(nt,),
        in_specs=[pl.BlockSpec((tm, 3 * A), lambda i: (i, 0)),
                  pl.BlockSpec((HALO, 3 * A), _halo_prev(tm)),
                  pl.BlockSpec((tm, 3 * A), lambda i: (i, 0)),
                  pl.BlockSpec((HALO, 3 * A), lambda i: (jnp.minimum((i + 1) * (tm // HALO), T // HALO - 1), 0)),
                  pl.BlockSpec((CONV_WIDTH, 3 * A), lambda i: (0, 0)), ANY],
        out_specs=[pl.BlockSpec((tm, 3 * A), lambda i: (i, 0)),
                   pl.BlockSpec((HALO, 3 * A), lambda i: (0, 0))],
        out_shape=[jax.ShapeDtypeStruct(dproj.shape, dproj.dtype), jax.ShapeDtypeStruct((HALO, 3 * A), F32)],
        input_output_aliases={5: 0},
        compiler_params=_cparams(("arbitrary",)),
    )(proj_m, proj_m, dc, dc, conv_w, dproj)


CHUNK = 128
BLOCK = 64


def _b(x):
    return x.astype(BF16)


@jax.custom_vjp
def _bdot(a, b):
    return _dot(_b(a), _b(b))


def _bdot_f(a, b):
    return _bdot(a, b), (a, b)


def _bdot_b(res, g):
    a, b = res
    return _dot_nt(_b(g), _b(b)), _dot_tn(_b(a), _b(g))


_bdot.defvjp(_bdot_f, _bdot_b)


@jax.custom_vjp
def _bdot_nt(a, b):
    return _dot_nt(_b(a), _b(b))


def _bdot_nt_f(a, b):
    return _bdot_nt(a, b), (a, b)


def _bdot_nt_b(res, g):
    a, b = res
    return _dot(_b(g), _b(b)), _dot_tn(_b(g), _b(a))


_bdot_nt.defvjp(_bdot_nt_f, _bdot_nt_b)


@jax.custom_vjp
def _bdot_tn(a, b):
    return _dot_tn(_b(a), _b(b))


def _bdot_tn_f(a, b):
    return _bdot_tn(a, b), (a, b)


def _bdot_tn_b(res, g):
    a, b = res
    return _dot_nt(_b(b), _b(g)), _dot(_b(a), _b(g))


_bdot_tn.defvjp(_bdot_tn_f, _bdot_tn_b)


def _mask_matmul(m, x):
    hi = _b(x)
    r = x - hi.astype(F32)
    mid = _b(r)
    lo = _b(r - mid.astype(F32))
    return (_dot(m, lo) + _dot(m, mid)) + _dot(m, hi)


@jax.custom_vjp
def _mask_dot(m, mt, x):
    return _mask_matmul(m, x)


def _mask_dot_f(m, mt, x):
    return _mask_matmul(m, x), (m, mt)


def _mask_dot_b(res, g):
    m, mt = res
    return jnp.zeros_like(m), jnp.zeros_like(mt), _mask_matmul(mt, g)


_mask_dot.defvjp(_mask_dot_f, _mask_dot_b)

def _unit_lower_inverse(L):
    n = L.shape[-1]
    X = -L
    Q = X
    for _ in range(BLOCK.bit_length() - 2):
        X = _dot(_b(X), _b(X))
        Q = Q + X + _dot(_b(Q), _b(X))
    return (_iota((n, n), 0) == _iota((n, n), 1)).astype(F32) + Q


@jax.custom_vjp
def _known_inverse(L, P):
    return P


def _known_inverse_f(L, P):
    return P, P


def _known_inverse_b(P, g):
    n = P.shape[-1]
    Q = _b(P - (_iota((n, n), 0) == _iota((n, n), 1)).astype(F32))
    t = g + _dot_tn(Q, _b(g))
    return -(t + _dot_nt(_b(t), Q)), jnp.zeros_like(P)


_known_inverse.defvjp(_known_inverse_f, _known_inverse_b)


def _gdn_prep_fn(q, k, v, gb, bb, P_known=None):
    n = CHUNK
    row, col = _iota((n, n), 0), _iota((n, n), 1)
    same = (row // BLOCK) == (col // BLOCK)
    incl, strict = same & (row >= col), same & (row > col)
    bc = lambda m: jnp.broadcast_to(_b(m.astype(F32)), q.shape[:1] + (n, n))
    tril, triu, ones = bc(incl), bc(same & (row <= col)), bc(same)
    gc = _mask_dot(tril, triu, gb)
    gl = _mask_dot(ones, ones, gb)
    decay = jnp.where(incl, jnp.exp(jnp.where(incl, gc - jnp.swapaxes(gc, 1, 2), 0.0)), 0.0)
    kb = k * bb
    vb = v * bb
    qs = q * (HEAD_DIM ** -0.5)
    L = jnp.where(strict, _bdot_nt(kb, k) * decay, 0.0)
    P = _unit_lower_inverse(L) if P_known is None else _known_inverse(L, P_known)
    egc = jnp.exp(gc)
    u = _bdot(P, vb)
    w = _bdot(P, kb * egc)
    attn = jnp.where(incl, _bdot_nt(qs, k) * decay, 0.0)
    qg = qs * egc
    kdec = k * jnp.exp(gl - gc)
    eg = jnp.exp(gl).reshape(-1, n // BLOCK, BLOCK, LANES).sum(axis=2) * (1.0 / BLOCK)
    if P_known is None:
        return u, w, qg, kdec, attn, eg, P
    return u, w, qg, kdec, attn, eg


def _gdn_block_fn(S, qg, kdec, u, w, attn, eg, i):
    nblk = CHUNK // BLOCK
    v_new = u - _bdot(w, S)
    zeros = jnp.zeros_like(v_new)
    o = _bdot(qg, S) + _bdot(attn, jnp.concatenate([zeros] * i + [v_new] + [zeros] * (nblk - 1 - i), axis=1))
    return o, S * eg + _bdot_tn(kdec, v_new)


def _eg_spec(H, T, chunks, index_map, per_head):
    nblk = CHUNK // BLOCK
    block = (chunks, 1 if per_head else H, nblk, LANES)
    return pl.BlockSpec(block, index_map), jax.ShapeDtypeStruct((T // CHUNK, H, nblk, LANES), F32)


def _gdn_prep(q, k, v, gb, bb):
    H, T, _ = q.shape
    pb = _pick(T // CHUNK, (16, 8, 4, 2, 1))
    hs = pl.BlockSpec((1, CHUNK * pb, HEAD_DIM), lambda h, n: (h, n, 0))

    def body(q_ref, k_ref, v_ref, gb_ref, bb_ref, *out_refs):
        chunks = lambda ref: ref[0].reshape(pb, CHUNK, HEAD_DIM)
        outs = _gdn_prep_fn(chunks(q_ref), chunks(k_ref), chunks(v_ref), chunks(gb_ref), chunks(bb_ref))
        for i, (ref, val) in enumerate(zip(out_refs, outs)):
            if i == 5:
                ref[:, 0] = val
            else:
                ref[0] = val.reshape(pb * CHUNK, HEAD_DIM).astype(ref.dtype)

    kept = [F32, BF16, BF16, BF16, BF16, None, BF16]
    es, eshape = _eg_spec(H, T, pb, lambda h, n: (n, h, 0, 0), per_head=True)
    return pl.pallas_call(
        body, name="gdn_prep", grid=(H, T // (CHUNK * pb)),
        in_specs=[hs] * 5, out_specs=[es if dt is None else hs for dt in kept],
        out_shape=[eshape if dt is None else jax.ShapeDtypeStruct((H, T, HEAD_DIM), dt) for dt in kept],
        compiler_params=_cparams(("parallel", "parallel")),
    )(q, k, v, gb, bb)


def _gdn_prep_bwd(q, k, v, gb, bb, pinv, du, dw, dqg, dkd, dat, deg):
    H, T, _ = q.shape
    pb = _pick(T // CHUNK, (16, 8, 4, 2, 1))
    hs = pl.BlockSpec((1, CHUNK * pb, HEAD_DIM), lambda h, n: (h, n, 0))
    hshape = jax.ShapeDtypeStruct((H, T, HEAD_DIM), F32)

    def body(*refs):
        in_refs, p_ref, ct_refs, out_refs = refs[:5], refs[5], refs[6:12], refs[12:]
        chunks = lambda ref: ref[0].reshape(pb, CHUNK, HEAD_DIM)
        P = chunks(p_ref).astype(F32)
        _, vjp = jax.vjp(lambda *a: _gdn_prep_fn(*a, P_known=P), *[chunks(r) for r in in_refs])
        grads = vjp(tuple(chunks(r).astype(F32) for r in ct_refs[:5]) + (ct_refs[5][:, 0],))
        for ref, val in zip(out_refs, grads):
            ref[0] = val.reshape(pb * CHUNK, HEAD_DIM)

    es, _ = _eg_spec(H, T, pb, lambda h, n: (n, h, 0, 0), per_head=True)
    return pl.pallas_call(
        body, name="gdn_prep_bwd", grid=(H, T // (CHUNK * pb)),
        in_specs=[hs] * 11 + [es], out_specs=[hs] * 5, out_shape=[hshape] * 5,
        compiler_params=_cparams(("parallel", "parallel")),
    )(q, k, v, gb, bb, pinv, du, dw, dqg, dkd, dat, deg)


def _gdn_chain(qg, kd, u, w, attn, eg):
    H, T, _ = qg.shape
    N, nblk = T // CHUNK, CHUNK // BLOCK
    cs = _pick(N, (2, 1))
    hs = pl.BlockSpec((H, cs * CHUNK, HEAD_DIM), lambda n: (0, n, 0))
    ss = pl.BlockSpec((cs, nblk, H, HEAD_DIM, HEAD_DIM), lambda n: (n, 0, 0, 0, 0))

    def body(qg_ref, kd_ref, u_ref, w_ref, at_ref, eg_ref, o_ref, sall_ref, s_ref):
        @pl.when(pl.program_id(0) == 0)
        def _():
            s_ref[...] = jnp.zeros_like(s_ref)

        S = s_ref[...]
        for j in range(cs):
            for i in range(nblk):
                r = pl.ds(j * CHUNK + i * BLOCK, BLOCK)
                sall_ref[j, i] = S
                o_ref[:, r, :], S = _gdn_block_fn(S, qg_ref[:, r, :], kd_ref[:, r, :], u_ref[:, r, :],
                                                  w_ref[:, r, :], at_ref[:, r, :], eg_ref[j, :, i:i + 1, :], i)
        s_ref[...] = S

    es, _ = _eg_spec(H, T, cs, lambda n: (n, 0, 0, 0), per_head=False)
    return pl.pallas_call(
        body, name="gdn_chain", grid=(N // cs,),
        in_specs=[hs] * 5 + [es], out_specs=[hs, ss],
        out_shape=[jax.ShapeDtypeStruct((H, T, HEAD_DIM), F32),
                   jax.ShapeDtypeStruct((N, nblk, H, HEAD_DIM, HEAD_DIM), F32)],
        scratch_shapes=[pltpu.VMEM((H, HEAD_DIM, HEAD_DIM), F32)],
        compiler_params=_cparams(("arbitrary",)),
    )(qg, kd, u, w, attn, eg)


def _gdn_chain_bwd(qg, kd, u, w, attn, eg, sall, do):
    H, T, _ = qg.shape
    N, nblk = T // CHUNK, CHUNK // BLOCK
    cs = _pick(N, (2, 1))
    last = N // cs - 1
    hs = pl.BlockSpec((H, cs * CHUNK, HEAD_DIM), lambda n: (0, last - n, 0))
    ss = pl.BlockSpec((cs, nblk, H, HEAD_DIM, HEAD_DIM), lambda n: (last - n, 0, 0, 0, 0))

    def body(qg_ref, kd_ref, u_ref, w_ref, at_ref, eg_ref, sall_ref, do_ref, *rest):
        out_refs, ds_ref = rest[:6], rest[6]

        @pl.when(pl.program_id(0) == 0)
        def _():
            ds_ref[...] = jnp.zeros_like(ds_ref)

        dS = ds_ref[...]
        for j in reversed(range(cs)):
            for i in reversed(range(nblk)):
                r = pl.ds(j * CHUNK + i * BLOCK, BLOCK)
                f32 = lambda ref: ref[:, r, :].astype(F32)
                _, vjp = jax.vjp(functools.partial(_gdn_block_fn, i=i), sall_ref[j, i], f32(qg_ref), f32(kd_ref),
                                 u_ref[:, r, :], f32(w_ref), f32(at_ref), eg_ref[j, :, i:i + 1, :])
                grads = vjp((do_ref[:, r, :], dS))
                dS = grads[0]
                for ref, val in zip(out_refs[:5], grads[1:6]):
                    ref[:, r, :] = val.astype(ref.dtype)
                out_refs[5][j, :, i:i + 1, :] = grads[6]
        ds_ref[...] = dS

    kept = [F32, F32, BF16, BF16, F32]
    es, eshape = _eg_spec(H, T, cs, lambda n: (last - n, 0, 0, 0), per_head=False)
    return pl.pallas_call(
        body, name="gdn_chain_bwd", grid=(N // cs,),
        in_specs=[hs] * 5 + [es, ss, hs], out_specs=[hs] * 5 + [es],
        out_shape=[jax.ShapeDtypeStruct((H, T, HEAD_DIM), dt) for dt in kept] + [eshape],
        scratch_shapes=[pltpu.VMEM((H, HEAD_DIM, HEAD_DIM), F32)],
        compiler_params=_cparams(("arbitrary",)),
    )(qg, kd, u, w, attn, eg, sall, do)


def _post_fn(ogs, za, hw):
    outs = []
    for h, o in enumerate(ogs):
        r = lax.rsqrt(jnp.mean(o * o, axis=-1, keepdims=True) + EPS)
        outs.append(o * r * hw * _silu(za[:, h * HEAD_DIM:(h + 1) * HEAD_DIM]))
    return jnp.concatenate(outs, axis=1)


def _gdn_post(og, proj_m, hw):
    H, T, _ = og.shape
    A = H * HEAD_DIM
    tm = _pick(T, (512, 256, 128))

    def body(og_ref, za_ref, hw_ref, o_ref, ot_ref):
        o = _post_fn(tuple(og_ref[h] for h in range(H)), za_ref[...], hw_ref[...])
        o_ref[...] = o.astype(BF16)
        ot_ref[...] = o.T.astype(BF16)

    return pl.pallas_call(
        body, name="gdn_post", grid=(T // tm,),
        in_specs=[pl.BlockSpec((H, tm, HEAD_DIM), lambda i: (0, i, 0)),
                  pl.BlockSpec((tm, A), lambda i: (i, ZA_BLOCK)),
                  pl.BlockSpec((1, HEAD_DIM), lambda i: (0, 0))],
        out_specs=[pl.BlockSpec((tm, A), lambda i: (i, 0)), pl.BlockSpec((A, tm), lambda i: (0, i))],
        out_shape=[jax.ShapeDtypeStruct((T, A), BF16), jax.ShapeDtypeStruct((A, T), BF16)],
        compiler_params=_cparams(("parallel",)),
    )(og, proj_m, hw)


def _gdn_post_bwd(og, proj_m, hw, d_o, dproj):
    H, T, _ = og.shape
    A = H * HEAD_DIM
    tm = _pick(T, (256, 128))

    def body(og_ref, za_ref, hw_ref, do_ref, _, dog_ref, dza_ref, dhw_ref):
        _, vjp = jax.vjp(_post_fn, tuple(og_ref[h] for h in range(H)), za_ref[...], hw_ref[...])
        dog, dza, dhw = vjp(do_ref[...])
        for h in range(H):
            dog_ref[h] = dog[h]
        dza_ref[...] = dza.astype(BF16)

        @pl.when(pl.program_id(0) == 0)
        def _():
            dhw_ref[...] = dhw

        @pl.when(pl.program_id(0) > 0)
        def _():
            dhw_ref[...] += dhw

    return pl.pallas_call(
        body, name="gdn_post_bwd", grid=(T // tm,),
        in_specs=[pl.BlockSpec((H, tm, HEAD_DIM), lambda i: (0, i, 0)),
                  pl.BlockSpec((tm, A), lambda i: (i, ZA_BLOCK)),
                  pl.BlockSpec((1, HEAD_DIM), lambda i: (0, 0)),
                  pl.BlockSpec((tm, A), lambda i: (i, 0)), ANY],
        out_specs=[pl.BlockSpec((H, tm, HEAD_DIM), lambda i: (0, i, 0)),
                   pl.BlockSpec((tm, A), lambda i: (i, ZA_BLOCK)),
                   pl.BlockSpec((1, HEAD_DIM), lambda i: (0, 0))],
        out_shape=[jax.ShapeDtypeStruct((H, T, HEAD_DIM), F32), jax.ShapeDtypeStruct(dproj.shape, dproj.dtype),
                   jax.ShapeDtypeStruct((1, HEAD_DIM), F32)],
        input_output_aliases={4: 1},
        compiler_params=_cparams(("arbitrary",)),
    )(og, proj_m, hw, d_o, dproj)


def _sgu_fn(ub, vb, zb, lw, lb, W, bbc):
    G = len(W)
    tm = ub.shape[0]
    mu = jnp.mean(vb, axis=-1, keepdims=True)
    xc = vb - mu
    var = jnp.mean(xc * xc, axis=-1, keepdims=True)
    vn = xc * lax.rsqrt(var + EPS) * lw + lb
    mask = _iota((CHUNK_B, CHUNK_B), 0) >= _iota((CHUNK_B, CHUNK_B), 1)
    cols = []
    for g in range(G):
        wm = jnp.where(mask, W[g], 0.0).astype(BF16)
        rows = []
        for c in range(tm // CHUNK_B):
            blk = vn[c * CHUNK_B:(c + 1) * CHUNK_B, g * HEAD_DIM:(g + 1) * HEAD_DIM].astype(BF16)
            rows.append(_dot(wm, blk) + bbc[g])
        cols.append(jnp.concatenate(rows, axis=0) if len(rows) > 1 else rows[0])
    s = jnp.concatenate(cols, axis=1)
    return ub * s * _silu(zb)


ZA_BLOCK = 6


def _sgu_cols(A, B):
    assert A == B
    return 3, 4, 5


def _sgu_fwd(proj_m, lw, lb, W, bbc, A):
    T = proj_m.shape[0]
    G = W.shape[0]
    B = G * HEAD_DIM
    tm = _pick(T, (256, 128))
    cu, cv, cz = _sgu_cols(A, B)

    def body(u_ref, v_ref, z_ref, lw_ref, lb_ref, w_ref, b_ref, o_ref, ot_ref):
        o = _sgu_fn(u_ref[...], v_ref[...], z_ref[...], lw_ref[...], lb_ref[...],
                    tuple(w_ref[g] for g in range(G)), tuple(b_ref[g] for g in range(G)))
        o_ref[...] = o.astype(BF16)
        ot_ref[...] = o.T.astype(BF16)

    row = pl.BlockSpec((1, B), lambda i: (0, 0))
    cube = pl.BlockSpec((G, CHUNK_B, CHUNK_B), lambda i: (0, 0, 0))
    return pl.pallas_call(
        body, name="sgu_fwd", grid=(T // tm,),
        in_specs=[pl.BlockSpec((tm, B), lambda i: (i, cu)), pl.BlockSpec((tm, B), lambda i: (i, cv)),
                  pl.BlockSpec((tm, B), lambda i: (i, cz)), row, row, cube, cube],
        out_specs=[pl.BlockSpec((tm, B), lambda i: (i, 0)), pl.BlockSpec((B, tm), lambda i: (0, i))],
        out_shape=[jax.ShapeDtypeStruct((T, B), BF16), jax.ShapeDtypeStruct((B, T), BF16)],
        compiler_params=_cparams(("parallel",)),
    )(proj_m, proj_m, proj_m, lw, lb, W, bbc)


def _sgu_bwd(proj_m, lw, lb, W, bbc, d_o, A, dproj):
    T = proj_m.shape[0]
    G = W.shape[0]
    B = G * HEAD_DIM
    tm = _pick(T, (256, 128))
    nt = T // tm
    cu, cv, cz = _sgu_cols(A, B)

    def body(u_ref, v_ref, z_ref, lw_ref, lb_ref, w_ref, b_ref, do_ref, _,
             dp_ref, dlw_ref, dlb_ref, dw_ref, db_ref, dbb_ref):
        _, vjp = jax.vjp(_sgu_fn, u_ref[...], v_ref[...], z_ref[...], lw_ref[...], lb_ref[...],
                         tuple(w_ref[g] for g in range(G)), tuple(b_ref[g] for g in range(G)))
        du, dv, dz, dlw, dlb, dW, dbb = vjp(do_ref[...])
        dW, dbb = jnp.stack(dW, axis=0), jnp.stack(dbb, axis=0)
        dp_ref[:, 0:B] = du.astype(BF16)
        dp_ref[:, B:2 * B] = dv.astype(BF16)
        dp_ref[:, 2 * B:3 * B] = dz.astype(BF16)
        i = pl.program_id(0)

        @pl.when(i == 0)
        def _():
            dlw_ref[...] = dlw
            dlb_ref[...] = dlb
            dw_ref[...] = dW
            dbb_ref[...] = dbb

        @pl.when(i > 0)
        def _():
            dlw_ref[...] += dlw
            dlb_ref[...] += dlb
            dw_ref[...] += dW
            dbb_ref[...] += dbb

        @pl.when(i == nt - 1)
        def _():
            db_ref[...] = jnp.sum(dbb_ref[...], axis=-1, keepdims=True)

    row = pl.BlockSpec((1, B), lambda i: (0, 0))
    cube = pl.BlockSpec((G, CHUNK_B, CHUNK_B), lambda i: (0, 0, 0))
    return pl.pallas_call(
        body, name="sgu_bwd", grid=(nt,),
        in_specs=[pl.BlockSpec((tm, B), lambda i: (i, cu)), pl.BlockSpec((tm, B), lambda i: (i, cv)),
                  pl.BlockSpec((tm, B), lambda i: (i, cz)), row, row, cube, cube,
                  pl.BlockSpec((tm, B), lambda i: (i, A // B)), ANY],
        out_specs=[pl.BlockSpec((tm, 3 * B), lambda i: (i, 1)), row, row, cube,
                   pl.BlockSpec((G, CHUNK_B, 1), lambda i: (0, 0, 0))],
        out_shape=[jax.ShapeDtypeStruct(dproj.shape, dproj.dtype), jax.ShapeDtypeStruct((1, B), F32),
                   jax.ShapeDtypeStruct((1, B), F32), jax.ShapeDtypeStruct((G, CHUNK_B, CHUNK_B), F32),
                   jax.ShapeDtypeStruct((G, CHUNK_B, 1), F32)],
        input_output_aliases={8: 0},
        scratch_shapes=[pltpu.VMEM((G, CHUNK_B, CHUNK_B), F32)],
        compiler_params=_cparams(("arbitrary",)),
    )(proj_m, proj_m, proj_m, lw, lb, W, bbc, d_o, dproj)


def _head_fn(mix, x, fw, tgt):
    h = x + mix
    y = _rms_fn(h, fw)
    e = y - tgt
    return 0.5 * jnp.sum(jnp.mean(e * e, axis=-1, keepdims=True), axis=0, keepdims=True)


def _out_proj_loss(oa, ob, wout, x, tgt, fw):
    T, A = oa.shape
    B = ob.shape[1]
    D = x.shape[1]
    tm = _pick(T, (256, 128))

    def body(oa_ref, ob_ref, w_ref, x_ref, t_ref, fw_ref, dh_ref, dhb_ref, loss_ref, dfw_ref):
        mix = _dot(oa_ref[...], w_ref[0:A, :]) + _dot(ob_ref[...], w_ref[A:A + B, :])
        xv, tv = x_ref[...], t_ref[...]
        loss, vjp = jax.vjp(lambda m, f: _head_fn(m, xv, f, tv), mix, fw_ref[...])
        dh, dfw = vjp(jnp.ones((1, 1), F32))
        dh_ref[...] = dh
        dhb_ref[...] = dh.astype(BF16)
        lrow = jnp.broadcast_to(loss, (1, LANES))

        @pl.when(pl.program_id(0) == 0)
        def _():
            loss_ref[...] = lrow
            dfw_ref[...] = dfw

        @pl.when(pl.program_id(0) > 0)
        def _():
            loss_ref[...] += lrow
            dfw_ref[...] += dfw

    tile = pl.BlockSpec((tm, D), lambda i: (i, 0))
    return pl.pallas_call(
        body, name="out_proj_loss", grid=(T // tm,),
        in_specs=[pl.BlockSpec((tm, A), lambda i: (i, 0)), pl.BlockSpec((tm, B), lambda i: (i, 0)),
                  pl.BlockSpec((A + B, D), lambda i: (0, 0)), tile, tile,
                  pl.BlockSpec((1, D), lambda i: (0, 0))],
        out_specs=[tile, tile, pl.BlockSpec((1, LANES), lambda i: (0, 0)),
                   pl.BlockSpec((1, D), lambda i: (0, 0))],
        out_shape=[jax.ShapeDtypeStruct((T, D), F32), jax.ShapeDtypeStruct((T, D), BF16),
                   jax.ShapeDtypeStruct((1, LANES), F32), jax.ShapeDtypeStruct((1, D), F32)],
        compiler_params=_cparams(("arbitrary",)),
    )(oa, ob, wout, x, tgt, fw)


def _adamw(w, g, m, v, name):
    R, Cn = w.shape
    cap = max(8, 512 * 1024 // Cn)
    tr = max(t for t in range(8, min(R, cap) + 1, 8) if R % t == 0) if R > cap else R

    def body(w_ref, g_ref, m_ref, v_ref, d_ref, mo_ref, vo_ref):
        g = g_ref[...]
        m = ADAM_B1 * m_ref[...] + (1.0 - ADAM_B1) * g
        v = ADAM_B2 * v_ref[...] + (1.0 - ADAM_B2) * jnp.square(g)
        m_hat = m / (1.0 - ADAM_B1 ** ADAM_STEP)
        v_hat = v / (1.0 - ADAM_B2 ** ADAM_STEP)
        d_ref[...] = -ADAM_LR * (m_hat / (jnp.sqrt(v_hat) + ADAM_EPS) + ADAM_WD * w_ref[...])
        mo_ref[...] = m
        vo_ref[...] = v

    tile = pl.BlockSpec((tr, Cn), lambda i: (i, 0))
    shape = jax.ShapeDtypeStruct((R, Cn), F32)
    return pl.pallas_call(
        body, name=name, grid=(R // tr,), in_specs=[tile] * 4, out_specs=[tile] * 3,
        out_shape=[shape] * 3, compiler_params=_cparams(("parallel",)),
    )(w, g, m, v)


def _place():
    x, y, c = lax.axis_index("x"), lax.axis_index("y"), lax.axis_index("c")
    others = [(1 - x, y), (x, 1 - y), (1 - x, 1 - y)]
    return x, y, c, others


def _chip_index(px, py):
    return 2 * px + py


ANY = pl.BlockSpec(memory_space=pl.ANY)


def _gather_ride(blocks, split):
    n = len(blocks)

    def plan(in_refs, out_refs, send_sems, recv_sems):
        x, y, c, _ = _place()
        me, kx, ky, kd = (_chip_index(px, py) for px, py in ((x, y), (1 - x, y), (x, 1 - y), (1 - x, 1 - y)))
        to_x, to_y, to_s = (1 - x, y, c), (x, 1 - y, c), (x, y, 1 - c)

        def copy(sem, src, dst, to):
            return pltpu.make_async_remote_copy(src_ref=src, dst_ref=dst, send_sem=send_sems.at[sem],
                                                recv_sem=recv_sems.at[sem], device_id=to, device_id_type=MESH_ID)

        first, second, third, awaited = [], [], [], []
        for a in range(n):
            out, s0 = out_refs[a], 8 * a
            if not split[a]:
                for j, (k, to) in enumerate(((kx, to_x), (ky, to_y), (kd, (1 - x, 1 - y, c)))):
                    first.append(lambda j=j, to=to, a=a, out=out, s0=s0: copy(s0 + j, in_refs[a], out.at[me], to))
                    awaited.append((lambda j=j, k=k, to=to, out=out, s0=s0: copy(s0 + j, out.at[k], out.at[k], to),
                                    None))
                continue
            h = blocks[a].shape[0] // 2
            q = h // 2
            half = lambda k, core, out=out, h=h: out.at[k, pl.ds(core * h, h), :]
            quarter = lambda k, core, i, out=out, h=h, q=q: out.at[k, pl.ds(core * h + i * q, q), :]
            mine = in_refs[a].at[pl.ds(c * h, h), :]
            first.append(lambda s0=s0, mine=mine, half=half: copy(s0, mine, half(me, c), to_x))
            first.append(lambda s0=s0, mine=mine, half=half: copy(s0 + 1, mine, half(me, c), to_y))
            fwd0 = lambda s0=s0, quarter=quarter: copy(s0 + 2, quarter(kx, c, 0), quarter(kx, c, 0), to_y)
            fwd1 = lambda s0=s0, quarter=quarter: copy(s0 + 3, quarter(ky, c, 1), quarter(ky, c, 1), to_x)
            pieces = [(s0 + 0, lambda half=half: half(kx, c), lambda half=half: half(kx, 1 - c), to_x, fwd0),
                      (s0 + 1, lambda half=half: half(ky, c), lambda half=half: half(ky, 1 - c), to_y, fwd1),
                      (s0 + 2, lambda quarter=quarter: quarter(kd, c, 0), lambda quarter=quarter: quarter(kd, 1 - c, 0),
                       to_y, None),
                      (s0 + 3, lambda quarter=quarter: quarter(kd, c, 1), lambda quarter=quarter: quarter(kd, 1 - c, 1),
                       to_x, None)]
            for i, (sem, here, there, frm, fwd) in enumerate(pieces):
                passing = lambda s0=s0, i=i, here=here: copy(s0 + 4 + i, here(), here(), to_s)
                awaited.append((lambda sem=sem, here=here, frm=frm: copy(sem, here(), here(), frm), (fwd, passing)))
                if fwd is not None:
                    second.append(fwd)
                third.append((passing, lambda s0=s0, i=i, there=there: copy(s0 + 4 + i, there(), there(), to_s)))
        return first, second, third, awaited

    def start(*refs):
        for send in plan(*refs)[0]:
            send().start()

    def finish(*refs):
        first, second, third, awaited = plan(*refs)
        for arrival, then in awaited:
            arrival().wait_recv()
            for nxt in (then or ()):
                if nxt is not None:
                    nxt().start()
        for _, from_sibling in third:
            from_sibling().wait_recv()
        for send in first + second + [p for p, _ in third]:
            send().wait_send()

    shapes = [jax.ShapeDtypeStruct((N_CHIPS,) + b.shape, b.dtype) for b in blocks]
    return _Ride(blocks, shapes, 8 * n, start, finish)


def _put_own(gathered, own):
    me = _chip_index(lax.axis_index("x"), lax.axis_index("y"))
    return lax.dynamic_update_index_in_dim(gathered, own, me, 0)


def _allreduce_small(buf):
    R0, L = buf.shape
    R = -(-R0 // 16) * 16
    h = R // 2
    buf = jnp.pad(buf, ((0, R - R0), (0, 0)))

    def body(in_ref, out_ref, sib_ref, pair_ref, chips_ref, send_sems, recv_sems):
        x, y, c, others = _place()
        me = _chip_index(x, y)
        sibling = (x, y, 1 - c)

        def copy(sem, src, dst, to):
            return pltpu.make_async_remote_copy(src_ref=src, dst_ref=dst, send_sem=send_sems.at[sem],
                                                recv_sem=recv_sems.at[sem], device_id=to, device_id_type=MESH_ID)

        cp = copy(0, in_ref, sib_ref, sibling)
        cp.start()
        cp.wait()
        pair_ref[...] = in_ref[...] + sib_ref[...]
        rows = lambda core: pl.ds(pl.multiple_of(core * h, 8), h)
        sends = [copy(1 + j, pair_ref.at[rows(c), :], chips_ref.at[me], (*chip, c)) for j, chip in enumerate(others)]
        for s in sends:
            s.start()
        chips_ref[me] = pair_ref[rows(c), :]
        for j, chip in enumerate(others):
            k = _chip_index(*chip)
            copy(1 + j, chips_ref.at[k], chips_ref.at[k], (*chip, c)).wait_recv()
        out_ref[rows(c), :] = ((chips_ref[0] + chips_ref[1]) + chips_ref[2]) + chips_ref[3]
        swap = copy(4, out_ref.at[rows(c), :], out_ref.at[rows(c), :], sibling)
        swap.start()
        copy(4, out_ref.at[rows(1 - c), :], out_ref.at[rows(1 - c), :], sibling).wait_recv()
        for s in sends + [swap]:
            s.wait_send()

    vm = pl.BlockSpec(memory_space=pltpu.VMEM)
    return pl.pallas_call(
        body, name="allreduce_small", in_specs=[vm], out_specs=vm,
        out_shape=jax.ShapeDtypeStruct((R, L), F32),
        scratch_shapes=[pltpu.VMEM((R, L), F32), pltpu.VMEM((R, L), F32), pltpu.VMEM((N_CHIPS, h, L), F32),
                        pltpu.SemaphoreType.DMA((5,)), pltpu.SemaphoreType.DMA((5,))],
        compiler_params=pltpu.CompilerParams(vmem_limit_bytes=VMEM_LIMIT),
    )(buf)[:R0]


def _pair_ride(g):
    nb, R, Cn = g.shape
    h = R // 2

    def copy(in_refs, out_refs, send_sems, recv_sems):
        x, y, c, _ = _place()
        return pltpu.make_async_remote_copy(src_ref=in_refs[0].at[:, pl.ds((1 - c) * h, h), :], dst_ref=out_refs[0],
                                            send_sem=send_sems.at[0], recv_sem=recv_sems.at[0],
                                            device_id=(x, y, 1 - c), device_id_type=MESH_ID)

    return _Ride([g], [jax.ShapeDtypeStruct((nb, h, Cn), g.dtype)], 1,
                 lambda *refs: copy(*refs).start(), lambda *refs: copy(*refs).wait())


def _pair_sum(g, land, c_arr, name, ride=None):
    nb, R, Cn = g.shape
    hr = R // 2
    tr = _pick(hr, (256, 128, 64, 32, 16))
    nt = hr // tr

    def body(c_ref, g_ref, l_ref, o_ref):
        o_ref[...] = (g_ref[...].astype(F32) + l_ref[...].astype(F32)).astype(BF16)

    return _pallas(
        body, (c_arr, g, land), name=name, prefetch=1, grid=(nb, nt),
        in_specs=[pl.BlockSpec((1, tr, Cn), lambda b, i, c_ref: (b, c_ref[0] * nt + i, 0)),
                  pl.BlockSpec((1, tr, Cn), lambda b, i, c_ref: (b, i, 0))],
        out_specs=pl.BlockSpec((1, tr, Cn), lambda b, i, c_ref: (b, i, 0)),
        out_shape=jax.ShapeDtypeStruct((nb, hr, Cn), BF16),
        semantics=("parallel", "parallel"), ride=ride)


def _chip_ride(parts, cols=None):
    m = len(parts)

    def copies(in_refs, out_refs, send_sems, recv_sems):
        x, y, c, others = _place()
        me = _chip_index(x, y)
        def mk(j, chip, n, landing):
            k = _chip_index(*chip)
            src = in_refs[n].at[k]
            if cols is not None:
                src = src.at[:, pl.ds(pl.multiple_of(cols[0](k), LANES), cols[1])]
            return pltpu.make_async_remote_copy(
                src_ref=src, dst_ref=out_refs[n].at[landing(k)], send_sem=send_sems.at[m * j + n],
                recv_sem=recv_sems.at[m * j + n], device_id=(*chip, c), device_id_type=MESH_ID)

        pairs = [(j, chip, n) for j, chip in enumerate(others) for n in range(m)]
        return pairs, (lambda *p: mk(*p, lambda k: me)), (lambda *p: mk(*p, lambda k: k))

    def start(*refs):
        pairs, send, _ = copies(*refs)
        for p in pairs:
            send(*p).start()

    def finish(*refs):
        pairs, send, arrival = copies(*refs)
        for p in pairs:
            arrival(*p).wait_recv()
        for p in pairs:
            send(*p).wait_send()

    width = lambda p: p.shape[2] if cols is None else cols[1]
    return _Ride(parts, [jax.ShapeDtypeStruct(p.shape[:2] + (width(p),), p.dtype) for p in parts], 3 * m,
                 start, finish)


def _put_own_slot(q, p, cols=None):
    me = _chip_index(lax.axis_index("x"), lax.axis_index("y"))
    own = lax.dynamic_index_in_dim(p, me, 0, keepdims=False)
    if cols is not None:
        own = lax.dynamic_slice_in_dim(own, cols[0](me), cols[1], axis=1)
    return lax.dynamic_update_index_in_dim(q, own, me, 0)


def _chip_sum(q, c_arr, name):
    nb, hr, Cn = q.shape
    tr = _pick(hr, (256, 128, 64, 32, 16))
    nt = hr // tr

    def body(c_ref, q_ref, o_ref):
        f = lambda k: q_ref[k].astype(F32)
        o_ref[...] = ((f(0) + f(1)) + f(2)) + f(3)

    return _pallas(
        body, (c_arr, q), name=name, prefetch=1, grid=(nt,),
        in_specs=[pl.BlockSpec((nb, tr, Cn), lambda i, c_ref: (0, i, 0))],
        out_specs=pl.BlockSpec((tr, Cn), lambda i, c_ref: (c_ref[0] * nt + i, 0)),
        out_shape=jax.ShapeDtypeStruct((2 * hr, Cn), F32),
        semantics=("parallel",))


def _sibling_fill(fw, fo):
    def body(_, __, fw_ref, fo_ref, send_sems, recv_sems):
        x, y, c, _ = _place()
        copies = []
        for n, ref in enumerate((fw_ref, fo_ref)):
            h = ref.shape[0] // 2
            mine = ref.at[pl.ds(c * h, h), :]
            theirs = ref.at[pl.ds((1 - c) * h, h), :]
            mk = lambda src, dst: pltpu.make_async_remote_copy(
                src_ref=src, dst_ref=dst, send_sem=send_sems.at[n], recv_sem=recv_sems.at[n],
                device_id=(x, y, 1 - c), device_id_type=MESH_ID)
            send = mk(mine, mine)
            send.start()
            copies.append((send, mk(theirs, theirs)))
        for send, arrival in copies:
            arrival.wait_recv()
            send.wait_send()

    return pl.pallas_call(
        body, name="sibling_fill", in_specs=[ANY, ANY], out_specs=[ANY, ANY],
        out_shape=[jax.ShapeDtypeStruct(fw.shape, F32), jax.ShapeDtypeStruct(fo.shape, F32)],
        input_output_aliases={0: 0, 1: 1},
        scratch_shapes=[pltpu.SemaphoreType.DMA((2,)), pltpu.SemaphoreType.DMA((2,))],
        compiler_params=pltpu.CompilerParams(has_side_effects=True),
    )(fw, fo)


class _Layout:
    def __init__(self, H, G, nb, Cb):
        A, B = H * HEAD_DIM, G * HEAD_DIM
        self.n_main = 4 * A + 3 * B
        self.k = -(-(self.n_main + LANES) // WIN_BLOCK) * WIN_BLOCK
        cuts = [0, 3 * A, 4 * A, 4 * A + 2 * H, nb * Cb]
        starts = [0, 3 * A + 3 * B, self.n_main, 3 * A]
        self.pieces = []
        self.windows, self.runs = [], []
        for n in range(nb):
            segs = []
            for s in range(4):
                lo, hi = max(cuts[s], n * Cb), min(cuts[s + 1], (n + 1) * Cb)
                if lo < hi:
                    segs.append((starts[s] + lo - cuts[s], lo - n * Cb, hi - lo))
            self.pieces += [(own, n, col, ln) for own, col, ln in segs]
            blocks = sorted({b for own, _, ln in segs for b in range(own // WIN_BLOCK, (own + ln - 1) // WIN_BLOCK + 1)})
            self.windows.append(blocks)
            self.runs.append([(blocks.index(own // WIN_BLOCK) * WIN_BLOCK + own % WIN_BLOCK, ln)
                              for own, _, ln in segs])
        self.wb = max(len(b) for b in self.windows)
        self.table = [b + [b[-1]] * (self.wb - len(b)) for b in self.windows]
        self.pieces.sort()
        self.used_from = [min(c for c, _ in r) // LANES * LANES for r in self.runs]
        self.used = max(-(-max(c + ln for c, ln in r) // LANES) * LANES - f for r, f in zip(self.runs, self.used_from))
        self.used_from = [min(f, self.wb * WIN_BLOCK - self.used) for f in self.used_from]

    def to_own_order(self, g_in):
        nb, D, Cb = g_in.shape
        tr = _pick(D, (256, 128))

        def body(g_ref, o_ref):
            cols, at = [], 0
            for own, n, col, ln in self.pieces:
                if own > at:
                    cols.append(jnp.zeros((tr, own - at), g_in.dtype))
                cols.append(g_ref[n, :, col:col + ln])
                at = own + ln
            if at < self.k:
                cols.append(jnp.zeros((tr, self.k - at), g_in.dtype))
            o_ref[...] = jnp.concatenate(cols, axis=1)

        return pl.pallas_call(
            body, name="own_order", grid=(D // tr,),
            in_specs=[pl.BlockSpec((nb, tr, Cb), lambda i: (0, i, 0))],
            out_specs=pl.BlockSpec((tr, self.k), lambda i: (i, 0)),
            out_shape=jax.ShapeDtypeStruct((D, self.k), g_in.dtype),
            compiler_params=_cparams(("parallel",)),
        )(g_in)

    def from_window(self, win, chip, Cb):
        pick = lambda runs, f: (lambda w: jnp.concatenate([w[:, c - f:c - f + ln] for c, ln in runs], axis=1))
        return lax.switch(chip, [pick(r, f) for r, f in zip(self.runs, self.used_from)], win)

    def used_start(self, chip):
        return sum(jnp.where(chip == n, f, 0) for n, f in enumerate(self.used_from))


def _device_step(x, tgt, norm_w, win_b, wout_b, conv_b, a_log, dt_bias, head_norm_w, sgu_ln_w, sgu_ln_b,
                 w_spatial, b_spatial, final_norm_w, c_arr):
    T, D = x.shape
    H = a_log.shape[1]
    A = H * HEAD_DIM
    G = w_spatial.shape[0]
    B = G * HEAD_DIM
    nb, Cb, Rb = N_CHIPS, win_b.shape[1], wout_b.shape[0]
    lay = _Layout(H, G, nb, Cb)
    alog_row = jnp.pad(a_log, ((0, 0), (H, LANES - 2 * H)))
    dtb_row = jnp.pad(dt_bias, ((0, 0), (H, LANES - 2 * H)))
    bbc = jnp.broadcast_to(b_spatial[:, :, None], (G, CHUNK_B, CHUNK_B))

    (xn, xn_t), (g_in,) = _rms_in(x, norm_w, ride=_gather_ride([win_b], [True]))
    w_own = lay.to_own_order(_put_own(g_in, win_b))
    proj_m, (g_out, g_conv) = _mm_nn(xn, w_own, F32, "in_proj", tm=2048, cols=(0, lay.n_main),
                                     ride=_gather_ride([wout_b, conv_b], [False, False]))
    wout = _put_own(g_out, wout_b).reshape(nb * Rb, D)
    conv_w = _put_own(g_conv, conv_b).transpose(1, 0, 2).reshape(CONV_WIDTH, nb * conv_b.shape[1])
    q, k, v, gb, bb, proj_ba = _gdn_pre(proj_m, xn, w_own, conv_w, alog_row, dtb_row, H)
    u, w, qg, kd, attn, eg, pinv = _gdn_prep(q, k, v, gb, bb)
    og, sall = _gdn_chain(qg, kd, u, w, attn, eg)
    oa, oa_t = _gdn_post(og, proj_m, head_norm_w)
    ob, ob_t = _sgu_fwd(proj_m, sgu_ln_w, sgu_ln_b, w_spatial, bbc, A)
    dh, dhb, loss_row, d_fnw = _out_proj_loss(oa, ob, wout, x, tgt, final_norm_w.reshape(1, D))

    d_o = _mm_nn(dhb, wout.T, F32, "out_proj_dx", tm=2048)
    dproj = lax.empty((T, lay.k), BF16)
    dproj, d_lw, d_lb, d_ws, d_bs = _sgu_bwd(proj_m, sgu_ln_w, sgu_ln_b, w_spatial, bbc, d_o, A, dproj)
    dog, dproj, d_hw = _gdn_post_bwd(og, proj_m, head_norm_w, d_o, dproj)
    dqg, dkd, du, dw, dat, deg = _gdn_chain_bwd(qg, kd, u, w, attn, eg, sall, dog)
    dq, dk, dv, dgb, dbb = _gdn_prep_bwd(q, k, v, gb, bb, pinv, du, dw, dqg, dkd, dat, deg)
    dc, dproj, d_al, d_dt = _gdn_pre_bwd(proj_m, proj_ba, conv_w, alog_row, dtb_row, dq, dk, dv, dgb, dbb, H,
                                         dproj)
    dproj, d_conv = _conv_bwd(proj_m, dc, conv_w, H, dproj)

    table = jnp.array([b for row in lay.table for b in row], jnp.int32)
    d_win = _mm_windows(xn_t, dproj, table, nb, "in_proj_dw")
    d_wout, (land_w,) = _mm_nn_pair(oa_t, ob_t, dhb, "out_proj_dw", ride=_pair_ride(d_win))
    d_wout = d_wout.reshape(nb, Rb, D)
    pair_w, (land_o,) = _pair_sum(d_win, land_w, c_arr, "pair_sum_w_in", ride=_pair_ride(d_wout))
    pair_o = _pair_sum(d_wout, land_o, c_arr, "pair_sum_w_out")
    used = (lay.used_start, lay.used)
    dxn, (all_w,) = _mm_nt_rhs_outer(dproj, w_own, F32, "in_proj_dx", ride=_chip_ride([pair_w], used))
    (grad_x, d_nw), (all_o,) = _rms_in_bwd(x, norm_w, dxn, dh, ride=_chip_ride([pair_o]))
    all_w, all_o = _put_own_slot(all_w, pair_w, used), _put_own_slot(all_o, pair_o)
    small = dict(norm_w=d_nw, conv_w=d_conv[:CONV_WIDTH], a_log=d_al[:, H:2 * H], dt_bias=d_dt[:, H:2 * H],
                 head_norm_w=d_hw, sgu_ln_w=d_lw, sgu_ln_b=d_lb, w_spatial=d_ws, b_spatial=d_bs[:, :, 0],
                 final_norm_w=d_fnw)
    return loss_row, grad_x, small, all_w, all_o


SMALL = ("norm_w", "conv_w", "a_log", "dt_bias", "head_norm_w", "sgu_ln_w", "sgu_ln_b", "w_spatial",
         "b_spatial", "final_norm_w")


def _pack(parts):
    rows = []
    for p in parts:
        f = p.reshape(-1)
        f = jnp.pad(f, (0, (-f.shape[0]) % (8 * LANES)))
        rows.append(f.reshape(-1, LANES))
    return jnp.concatenate(rows, axis=0)


def _unpack(buf, shapes):
    out, r = [], 0
    for s in shapes:
        n = 1
        for d in s:
            n *= d
        nr = -(-n // (8 * LANES)) * 8
        out.append(buf[r:r + nr].reshape(-1)[:n].reshape(s))
        r += nr
    return out


def kernel(x, norm_w, w_in, conv_w, a_log, dt_bias, head_norm_w, sgu_ln_w, sgu_ln_b, w_spatial, b_spatial, w_out, final_norm_w, loss_target, m_norm_w, m_w_in, m_conv_w, m_a_log, m_dt_bias, m_head_norm_w, m_sgu_ln_w, m_sgu_ln_b, m_w_spatial, m_b_spatial, m_w_out, m_final_norm_w, v_norm_w, v_w_in, v_conv_w, v_a_log, v_dt_bias, v_head_norm_w, v_sgu_ln_w, v_sgu_ln_b, v_w_spatial, v_b_spatial, v_w_out, v_final_norm_w):
    T, D = x.shape[1], x.shape[2]
    weights = dict(norm_w=norm_w, w_in=w_in, conv_w=conv_w, a_log=a_log, dt_bias=dt_bias, head_norm_w=head_norm_w,
                   sgu_ln_w=sgu_ln_w, sgu_ln_b=sgu_ln_b, w_spatial=w_spatial, b_spatial=b_spatial, w_out=w_out,
                   final_norm_w=final_norm_w)
    mom_m = dict(norm_w=m_norm_w, w_in=m_w_in, conv_w=m_conv_w, a_log=m_a_log, dt_bias=m_dt_bias,
                 head_norm_w=m_head_norm_w, sgu_ln_w=m_sgu_ln_w, sgu_ln_b=m_sgu_ln_b, w_spatial=m_w_spatial,
                 b_spatial=m_b_spatial, w_out=m_w_out, final_norm_w=m_final_norm_w)
    mom_v = dict(norm_w=v_norm_w, w_in=v_w_in, conv_w=v_conv_w, a_log=v_a_log, dt_bias=v_dt_bias,
                 head_norm_w=v_head_norm_w, sgu_ln_w=v_sgu_ln_w, sgu_ln_b=v_sgu_ln_b, w_spatial=v_w_spatial,
                 b_spatial=v_b_spatial, w_out=v_w_out, final_norm_w=v_final_norm_w)
    me = _chip_index(lax.axis_index("x"), lax.axis_index("y"))
    c_arr = lax.axis_index("c").astype(jnp.int32).reshape(1)
    Din, Cb = w_in.shape[1], w_in.shape[2]
    Rb = w_out.shape[1]
    cconv = conv_w.shape[2]

    loss_row, grad_x, g, qw, qo = _device_step(
        x[0], loss_target[0], norm_w, w_in[0].astype(BF16), w_out[0].astype(BF16), conv_w[0], a_log, dt_bias,
        head_norm_w, sgu_ln_w, sgu_ln_b, w_spatial[0], b_spatial[0], final_norm_w, c_arr)

    small_shapes = [tuple(g[n].shape) for n in SMALL] + [(1, LANES)]
    small = _allreduce_small(_pack([g[n] for n in SMALL] + [loss_row]))
    gsum_in, gsum_out = _sibling_fill(_chip_sum(qw, c_arr, "chip_sum_w_in"), _chip_sum(qo, c_arr, "chip_sum_w_out"))
    gsum_in = _Layout(a_log.shape[1], w_spatial.shape[1], N_CHIPS, Cb).from_window(gsum_in, me, Cb)
    *small, loss_sum = _unpack(small, small_shapes)
    gsmall = dict(zip(SMALL, small))
    gsmall["conv_w"] = lax.dynamic_slice_in_dim(gsmall["conv_w"], me * cconv, cconv, axis=1)

    grads, deltas, new_m, new_v = {}, {}, {}, {}
    d, m2, v2 = _adamw(w_out[0], gsum_out, m_w_out[0], v_w_out[0], "adamw_w_out")
    grads["w_out"], deltas["w_out"], new_m["w_out"], new_v["w_out"] = gsum_out[None], d[None], m2[None], v2[None]
    flat = lambda a: a.transpose(2, 0, 1).reshape(-1, LANES)
    unflat = lambda f: f.reshape(Cb, 1, Din).transpose(1, 2, 0)
    g_flat = gsum_in.T.reshape(-1, LANES)
    d, m2, v2 = _adamw(flat(w_in), g_flat, flat(m_w_in), flat(v_w_in), "adamw_w_in")
    grads["w_in"], deltas["w_in"], new_m["w_in"], new_v["w_in"] = unflat(g_flat), unflat(d), unflat(m2), unflat(v2)
    shapes = [tuple(weights[n].shape) for n in SMALL]
    ds, ms, vs = _adamw(_pack([weights[n] for n in SMALL]), _pack([gsmall[n] for n in SMALL]),
                        _pack([mom_m[n] for n in SMALL]), _pack([mom_v[n] for n in SMALL]), "adamw_small")
    for n, gq, d, m2, v2 in zip(SMALL, [gsmall[n] for n in SMALL], _unpack(ds, shapes), _unpack(ms, shapes),
                                _unpack(vs, shapes)):
        grads[n], deltas[n], new_m[n], new_v[n] = gq.reshape(weights[n].shape), d, m2, v2

    loss = loss_sum[0, 0]
    order = ("norm_w", "w_in", "conv_w", "a_log", "dt_bias", "head_norm_w", "sgu_ln_w", "sgu_ln_b", "w_spatial",
             "b_spatial", "w_out", "final_norm_w")
    return (loss, grad_x[None], *[grads[n] for n in order], *[deltas[n] for n in order],
            *[new_m[n] for n in order], *[new_v[n] for n in order])
```

```python
import functools

import jax
import jax.numpy as jnp
from jax import lax
from jax.experimental import pallas as pl
from jax.experimental.pallas import tpu as pltpu

F32 = jnp.float32
BF16 = jnp.bfloat16
EPS = 1e-6
HEAD_DIM = 128
CHUNK_B = 128
CONV_WIDTH = 4
LANES = 128
HALO = 8
N_CHIPS = 4
ADAM_LR = 0.001
ADAM_B1 = 0.9
ADAM_B2 = 0.999
ADAM_EPS = 1e-08
ADAM_WD = 0.01
ADAM_STEP = 10
VMEM_LIMIT = 56 * 1024 * 1024
MESH_ID = pl.DeviceIdType.MESH


def _cparams(sem=None, **kw):
    return pltpu.CompilerParams(dimension_semantics=sem, vmem_limit_bytes=VMEM_LIMIT, **kw)


def _matmul(a, b, ca, cb):
    nb = a.ndim - 2
    batch = tuple(range(nb))
    return lax.dot_general(a, b, (((ca + nb,), (cb + nb,)), (batch, batch)), preferred_element_type=F32)


def _dot(a, b):
    return _matmul(a, b, 1, 0)


def _dot_nt(a, b):
    return _matmul(a, b, 1, 1)


def _dot_tn(a, b):
    return _matmul(a, b, 0, 0)


def _iota(shape, dim):
    return lax.broadcasted_iota(jnp.int32, shape, dim)


def _sigmoid(x):
    return 0.5 * (jnp.tanh(0.5 * x) + 1.0)


def _silu(x):
    return x * _sigmoid(x)


def _softplus(x):
    z = jnp.exp(-jnp.abs(x))
    small = z * (1.0 - z * (0.5 - z * (1.0 / 3.0)))
    return jnp.maximum(x, 0.0) + jnp.where(z < 1e-3, small, jnp.log(1.0 + z))


def _pick(n, pref):
    for t in pref:
        if n % t == 0:
            return t
    return n


class _Ride:
    def __init__(self, operands, out_shape, n_sems, start, finish):
        self.operands, self.out_shape, self.n_sems = list(operands), list(out_shape), n_sems
        self.start, self.finish = start, finish


def _pallas(body, operands, *, name, grid, in_specs, out_specs, out_shape, semantics, scratch_shapes=(),
            prefetch=0, ride=None):
    single = not isinstance(out_shape, (list, tuple))
    outs = [out_shape] if single else list(out_shape)
    ospecs = [out_specs] if single else list(out_specs)
    in_specs, scratch = list(in_specs), list(scratch_shapes)
    n_in, n_out, n_sc = len(operands) - prefetch, len(outs), len(scratch)
    kernel = body
    params = _cparams(semantics)
    if ride is not None:
        n_xin, n_xout = len(ride.operands), len(ride.out_shape)

        def kernel(*refs):
            pre, refs = refs[:prefetch], refs[prefetch:]
            ins, refs = refs[:n_in], refs[n_in:]
            xins, refs = refs[:n_xin], refs[n_xin:]
            mains, refs = refs[:n_out], refs[n_out:]
            xouts, refs = refs[:n_xout], refs[n_xout:]
            sc, (send, recv) = refs[:n_sc], refs[n_sc:]
            ids = [pl.program_id(a) for a in range(len(grid))]
            first = functools.reduce(jnp.logical_and, [i == 0 for i in ids])
            last = functools.reduce(jnp.logical_and, [i == g - 1 for i, g in zip(ids, grid)])

            @pl.when(first)
            def _():
                ride.start(xins, xouts, send, recv)

            body(*pre, *ins, *mains, *sc)

            @pl.when(last)
            def _():
                ride.finish(xins, xouts, send, recv)

        operands = list(operands) + ride.operands
        in_specs += [ANY] * n_xin
        ospecs += [ANY] * n_xout
        outs += ride.out_shape
        scratch += [pltpu.SemaphoreType.DMA((ride.n_sems,)), pltpu.SemaphoreType.DMA((ride.n_sems,))]
        params = _cparams(("arbitrary",) * len(grid), has_side_effects=True)
    if prefetch:
        spec = dict(grid_spec=pltpu.PrefetchScalarGridSpec(
            num_scalar_prefetch=prefetch, grid=grid, in_specs=in_specs, out_specs=ospecs, scratch_shapes=scratch))
    else:
        spec = dict(grid=grid, in_specs=in_specs, out_specs=ospecs, scratch_shapes=scratch)
    res = pl.pallas_call(kernel, name=name, out_shape=outs, compiler_params=params, **spec)(*operands)
    main = res[0] if single else list(res[:n_out])
    return main if ride is None else (main, list(res[n_out:]))


def _mm_nn(a, b, out_dtype, name, tm=1024, tn=512, tk=None, cols=None, ride=None):
    M, K = a.shape
    c0, N = (0, b.shape[1]) if cols is None else cols
    tm = _pick(M, (tm, 1024, 512, 256, 128))
    tn = _pick(N, (tn, 512, 384, 256, 128))
    tk = K if tk is None else _pick(K, (tk,))
    nk = K // tk
    j0 = c0 // tn
    assert c0 % tn == 0

    def body(a_ref, b_ref, o_ref, *scratch):
        part = _dot(a_ref[...], b_ref[...])
        if nk == 1:
            o_ref[...] = part.astype(out_dtype)
        else:
            acc_ref, = scratch
            k = pl.program_id(2)

            @pl.when(k == 0)
            def _():
                acc_ref[...] = part

            @pl.when(k > 0)
            def _():
                acc_ref[...] += part

            @pl.when(k == nk - 1)
            def _():
                o_ref[...] = acc_ref[...].astype(out_dtype)

    return _pallas(
        body, (a, b), name=name, grid=(M // tm, N // tn, nk),
        in_specs=[pl.BlockSpec((tm, tk), lambda i, j, k: (i, k)),
                  pl.BlockSpec((tk, tn), lambda i, j, k: (k, j + j0))],
        out_specs=pl.BlockSpec((tm, tn), lambda i, j, k: (i, j)),
        out_shape=jax.ShapeDtypeStruct((M, N), out_dtype),
        scratch_shapes=[] if nk == 1 else [pltpu.VMEM((tm, tn), F32)],
        semantics=("parallel", "parallel", "arbitrary"), ride=ride)


def _mm_nt_rhs_outer(a, b, out_dtype, name, tm=256, tn=1024, ride=None):
    M, K = a.shape
    N, _ = b.shape
    tm = _pick(M, (tm, 128))
    tn = _pick(N, (tn, 512, 256, 128))

    def body(a_ref, b_ref, o_ref):
        o_ref[...] = _dot_nt(a_ref[...], b_ref[...]).astype(out_dtype)

    return _pallas(
        body, (a, b), name=name, grid=(N // tn, M // tm),
        in_specs=[pl.BlockSpec((tm, K), lambda j, i: (i, 0)),
                  pl.BlockSpec((tn, K), lambda j, i: (j, 0))],
        out_specs=pl.BlockSpec((tm, tn), lambda j, i: (i, j)),
        out_shape=jax.ShapeDtypeStruct((M, N), out_dtype),
        semantics=("parallel", "parallel"), ride=ride)


WIN_BLOCK = 256
WIN_DEPTH = 3
WIN_PARTS = 4


def _mm_windows(a, b, table, nb, name, tm=2048):
    M, K = a.shape
    wb = table.shape[0] // nb
    steps = nb * wb
    depth = min(WIN_DEPTH, steps)
    parts = WIN_PARTS
    rows = K // parts

    def body(tab_ref, a_ref, b_ref, o_ref, buf, sems, a_buf, a_sems):
        s = pl.program_id(0)

        def fetch(step, slot):
            col = pl.multiple_of(tab_ref[step] * WIN_BLOCK, WIN_BLOCK)
            return [pltpu.make_async_copy(
                b_ref.at[pl.ds(p * rows, rows), pl.ds(col, WIN_BLOCK)],
                buf.at[slot, pl.ds(p * rows, rows)], sems.at[slot, p]) for p in range(parts)]

        a_copies = [pltpu.make_async_copy(a_ref.at[pl.ds(i * a_rows, a_rows)],
                                          a_buf.at[pl.ds(i * a_rows, a_rows)], a_sems.at[i])
                    for i in range(WIN_PARTS)]

        @pl.when(s == 0)
        def _():
            for c in fetch(0, 0):
                c.start()
            for c in a_copies:
                c.start()
            for j in range(1, depth - 1):
                for c in fetch(j, j):
                    c.start()

        ahead = s + (depth - 1)

        @pl.when(ahead < steps)
        def _():
            for c in fetch(ahead, ahead % depth):
                c.start()

        slot = s % depth
        for c in fetch(s, slot):
            c.wait()

        @pl.when(s == 0)
        def _():
            for i, c in enumerate(a_copies):
                c.wait()
                o_ref[0, pl.ds(i * a_rows, a_rows), :] = _dot(
                    a_buf[pl.ds(i * a_rows, a_rows), :], buf[0]).astype(BF16)

        @pl.when(s > 0)
        def _():
            o_ref[0] = _dot(a_buf[...], buf[slot]).astype(BF16)

    a_rows = M // WIN_PARTS
    return pl.pallas_call(
        body, name=name,
        grid_spec=pltpu.PrefetchScalarGridSpec(
            num_scalar_prefetch=1, grid=(steps,),
            in_specs=[ANY, ANY],
            out_specs=pl.BlockSpec((1, M, WIN_BLOCK), lambda s, tab: (s // wb, 0, s % wb)),
            scratch_shapes=[pltpu.VMEM((depth, K, WIN_BLOCK), BF16),
                            pltpu.SemaphoreType.DMA((depth, parts)),
                            pltpu.VMEM((M, K), BF16),
                            pltpu.SemaphoreType.DMA((WIN_PARTS,))]),
        out_shape=jax.ShapeDtypeStruct((nb, M, wb * WIN_BLOCK), BF16),
        compiler_params=_cparams(("arbitrary",)),
    )(table, a, b)


def _mm_nn_pair(a0, a1, b, name, tm=512, tn=1024, ride=None):
    M, K = a0.shape
    _, N = b.shape
    tm = _pick(M, (tm, 256, 128))
    tn = _pick(N, (tn, 512, 256, 128))
    ni = M // tm

    def body(a0_ref, a1_ref, b_ref, o_ref):
        p = pl.program_id(0)

        @pl.when(p == 0)
        def _():
            o_ref[...] = _dot(a0_ref[...], b_ref[...]).astype(BF16)

        @pl.when(p == 1)
        def _():
            o_ref[...] = _dot(a1_ref[...], b_ref[...]).astype(BF16)

    return _pallas(
        body, (a0, a1, b), name=name, grid=(2, ni, N // tn),
        in_specs=[pl.BlockSpec((tm, K), lambda p, i, j: (i * (1 - p), 0)),
                  pl.BlockSpec((tm, K), lambda p, i, j: (i * p, 0)),
                  pl.BlockSpec((K, tn), lambda p, i, j: (0, j))],
        out_specs=pl.BlockSpec((tm, tn), lambda p, i, j: (p * ni + i, j)),
        out_shape=jax.ShapeDtypeStruct((2 * M, N), BF16),
        semantics=("parallel", "parallel", "parallel"), ride=ride)


def _rms_fn(x, w):
    r = lax.rsqrt(jnp.mean(x * x, axis=-1, keepdims=True) + EPS)
    return x * r * w


def _rms_in(x, w, ride=None):
    T, D = x.shape
    tm = _pick(T, (512, 256, 128))

    def body(x_ref, w_ref, o_ref, ot_ref):
        xn = _rms_fn(x_ref[...], w_ref[...])
        o_ref[...] = xn.astype(BF16)
        ot_ref[...] = xn.T.astype(BF16)

    return _pallas(
        body, (x, w), name="rms_in", grid=(T // tm,),
        in_specs=[pl.BlockSpec((tm, D), lambda i: (i, 0)), pl.BlockSpec((1, D), lambda i: (0, 0))],
        out_specs=[pl.BlockSpec((tm, D), lambda i: (i, 0)), pl.BlockSpec((D, tm), lambda i: (0, i))],
        out_shape=[jax.ShapeDtypeStruct((T, D), BF16), jax.ShapeDtypeStruct((D, T), BF16)],
        semantics=("parallel",), ride=ride)


def _rms_in_bwd(x, w, dxn, dh, ride=None):
    T, D = x.shape
    tm = _pick(T, (256, 128))

    def body(x_ref, w_ref, dxn_ref, dh_ref, gx_ref, dw_ref):
        _, vjp = jax.vjp(_rms_fn, x_ref[...], w_ref[...])
        dx, dw = vjp(dxn_ref[...])
        gx_ref[...] = dh_ref[...] + dx

        @pl.when(pl.program_id(0) == 0)
        def _():
            dw_ref[...] = dw

        @pl.when(pl.program_id(0) > 0)
        def _():
            dw_ref[...] += dw

    tile = pl.BlockSpec((tm, D), lambda i: (i, 0))
    row = pl.BlockSpec((1, D), lambda i: (0, 0))
    return _pallas(
        body, (x, w, dxn, dh), name="rms_in_bwd", grid=(T // tm,),
        in_specs=[tile, row, tile, tile], out_specs=[tile, row],
        out_shape=[jax.ShapeDtypeStruct((T, D), F32), jax.ShapeDtypeStruct((1, D), F32)],
        semantics=("arbitrary",), ride=ride)


def _conv_fwd(cat_ref, halo, x, w):
    tm = x.shape[0]
    cat_ref[0:HALO, :] = halo
    cat_ref[HALO:HALO + tm, :] = x
    c = x * w[CONV_WIDTH - 1:CONV_WIDTH, :]
    for k in range(CONV_WIDTH - 1):
        s = CONV_WIDTH - 1 - k
        c = c + cat_ref[pl.ds(HALO - s, tm), :] * w[k:k + 1, :]
    return c


def _lane_to_all(x, lane):
    @jax.custom_vjp
    def f(x):
        return jnp.broadcast_to(x[:, lane:lane + 1], x.shape)

    def f_fwd(x):
        return f(x), None

    def f_bwd(_, g):
        return (jnp.where(_iota(g.shape, 1) == lane, jnp.sum(g, axis=-1, keepdims=True), 0.0),)

    f.defvjp(f_fwd, f_bwd)
    return f(x)


def _gdn_pointwise(c, ba, alog, dtb, H):
    A = H * HEAD_DIM
    s = _silu(c)
    beta = _sigmoid(ba)
    g = -jnp.exp(alog) * _softplus(ba + dtb)
    qs, ks, vs, gbs, bbs = [], [], [], [], []
    for h in range(H):
        lo = h * HEAD_DIM
        q = s[:, lo:lo + HEAD_DIM]
        k = s[:, A + lo:A + lo + HEAD_DIM]
        qs.append(q * lax.rsqrt(jnp.sum(q * q, axis=-1, keepdims=True) + EPS))
        ks.append(k * lax.rsqrt(jnp.sum(k * k, axis=-1, keepdims=True) + EPS))
        vs.append(s[:, 2 * A + lo:2 * A + lo + HEAD_DIM])
        bbs.append(_lane_to_all(beta, h))
        gbs.append(_lane_to_all(g, H + h))
    st = lambda xs: jnp.stack(xs, axis=0)
    return st(qs), st(ks), st(vs), st(gbs), st(bbs)


def _halo_prev(tm):
    return lambda i: (jnp.maximum(i * (tm // HALO) - 1, 0), 0)


def _gdn_pre(proj_m, xn, w_own, conv_w, alog_row, dtb_row, H):
    T, n_main = proj_m.shape
    D = xn.shape[1]
    A = H * HEAD_DIM
    tm = _pick(T, (256, 128))
    hs = pl.BlockSpec((H, tm, HEAD_DIM), lambda i: (0, i, 0))
    hshape = jax.ShapeDtypeStruct((H, T, HEAD_DIM), F32)

    def body(x_ref, halo_ref, xn_ref, wba_ref, w_ref, al_ref, dt_ref,
             q_ref, k_ref, v_ref, gb_ref, bb_ref, ba_ref, cat_ref):
        halo = jnp.where(pl.program_id(0) == 0, 0.0, halo_ref[...])
        c = _conv_fwd(cat_ref, halo, x_ref[...], w_ref[...])
        ba = _dot(xn_ref[...], wba_ref[...])
        q, k, v, gb, bb = _gdn_pointwise(c, ba, al_ref[...], dt_ref[...], H)
        q_ref[...] = q
        k_ref[...] = k
        v_ref[...] = v
        gb_ref[...] = gb
        bb_ref[...] = bb
        ba_ref[...] = ba

    return pl.pallas_call(
        body, name="gdn_pre", grid=(T // tm,),
        in_specs=[pl.BlockSpec((tm, 3 * A), lambda i: (i, 0)),
                  pl.BlockSpec((HALO, 3 * A), _halo_prev(tm)),
                  pl.BlockSpec((tm, D), lambda i: (i, 0)),
                  pl.BlockSpec((D, LANES), lambda i: (0, n_main // LANES)),
                  pl.BlockSpec((CONV_WIDTH, 3 * A), lambda i: (0, 0)),
                  pl.BlockSpec((1, LANES), lambda i: (0, 0)),
                  pl.BlockSpec((1, LANES), lambda i: (0, 0))],
        out_specs=[hs] * 5 + [pl.BlockSpec((tm, LANES), lambda i: (i, 0))],
        out_shape=[hshape] * 5 + [jax.ShapeDtypeStruct((T, LANES), F32)],
        scratch_shapes=[pltpu.VMEM((HALO + tm, 3 * A), F32)],
        compiler_params=_cparams(("parallel",)),
    )(proj_m, proj_m, xn, w_own, conv_w, alog_row, dtb_row)


def _gdn_pre_bwd(proj_m, proj_ba, conv_w, alog_row, dtb_row, dq, dk, dv, dgb, dbb, H, dproj):
    T, n_main = proj_m.shape
    A = H * HEAD_DIM
    tm = _pick(T, (256, 128))
    hs = pl.BlockSpec((H, tm, HEAD_DIM), lambda i: (0, i, 0))
    row = pl.BlockSpec((1, LANES), lambda i: (0, 0))

    def body(x_ref, halo_ref, ba_ref, w_ref, al_ref, dt_ref, dq_ref, dk_ref, dv_ref, dgb_ref, dbb_ref, _,
             dc_ref, dba_ref, dal_ref, ddt_ref, cat_ref):
        halo = jnp.where(pl.program_id(0) == 0, 0.0, halo_ref[...])
        c = _conv_fwd(cat_ref, halo, x_ref[...], w_ref[...])
        _, vjp = jax.vjp(functools.partial(_gdn_pointwise, H=H), c, ba_ref[...], al_ref[...], dt_ref[...])
        dc, dba, dal, ddt = vjp((dq_ref[...], dk_ref[...], dv_ref[...], dgb_ref[...], dbb_ref[...]))
        dc_ref[...] = dc
        dba_ref[:, :LANES] = dba.astype(BF16)
        dba_ref[:, LANES:] = jnp.zeros((tm, WIN_BLOCK - LANES), BF16)

        @pl.when(pl.program_id(0) == 0)
        def _():
            dal_ref[...] = dal
            ddt_ref[...] = ddt

        @pl.when(pl.program_id(0) > 0)
        def _():
            dal_ref[...] += dal
            ddt_ref[...] += ddt

    return pl.pallas_call(
        body, name="gdn_pre_bwd", grid=(T // tm,),
        in_specs=[pl.BlockSpec((tm, 3 * A), lambda i: (i, 0)),
                  pl.BlockSpec((HALO, 3 * A), _halo_prev(tm)),
                  pl.BlockSpec((tm, LANES), lambda i: (i, 0)),
                  pl.BlockSpec((CONV_WIDTH, 3 * A), lambda i: (0, 0)),
                  row, row, hs, hs, hs, hs, hs, ANY],
        out_specs=[pl.BlockSpec((tm, 3 * A), lambda i: (i, 0)),
                   pl.BlockSpec((tm, WIN_BLOCK), lambda i: (i, n_main // WIN_BLOCK)), row, row],
        out_shape=[jax.ShapeDtypeStruct((T, 3 * A), F32), jax.ShapeDtypeStruct(dproj.shape, dproj.dtype),
                   jax.ShapeDtypeStruct((1, LANES), F32), jax.ShapeDtypeStruct((1, LANES), F32)],
        input_output_aliases={11: 1},
        scratch_shapes=[pltpu.VMEM((HALO + tm, 3 * A), F32)],
        compiler_params=_cparams(("arbitrary",)),
    )(proj_m, proj_m, proj_ba, conv_w, alog_row, dtb_row, dq, dk, dv, dgb, dbb, dproj)


def _conv_bwd(proj_m, dc, conv_w, H, dproj):
    T = proj_m.shape[0]
    A = H * HEAD_DIM
    tm = _pick(T, (256, 128))
    nt = T // tm

    def body(x_ref, halo_ref, dc_ref, nxt_ref, w_ref, _, dx_ref, dw_ref):
        i = pl.program_id(0)
        halo = jnp.where(i == 0, 0.0, halo_ref[...])
        xcat = jnp.concatenate([halo, x_ref[...]], axis=0)
        nxt = jnp.where(i == nt - 1, 0.0, nxt_ref[...])
        dc = dc_ref[...]
        dcat = jnp.concatenate([dc, nxt], axis=0)
        w = w_ref[...]
        dx = None
        rows = []
        for k in range(CONV_WIDTH):
            s = CONV_WIDTH - 1 - k
            ds = dcat if s == 0 else pltpu.roll(dcat, tm + HALO - s, 0)
            term = ds[:tm, :] * w[k:k + 1, :]
            dx = term if dx is None else dx + term
            xs = xcat if s == 0 else pltpu.roll(xcat, s, 0)
            rows.append(jnp.sum(dc * xs[HALO:, :], axis=0, keepdims=True))
        dx_ref[...] = dx.astype(BF16)
        dw = jnp.concatenate(rows + [jnp.zeros((HALO - CONV_WIDTH, 3 * A), F32)], axis=0)

        @pl.when(i == 0)
        def _():
            dw_ref[...] = dw

        @pl.when(i > 0)
        def _():
            dw_ref[...] += dw

    return pl.pallas_call(
        body, name="conv_bwd", grid=(nt,),
        in_specs=[pl.BlockSpec((tm, 3 * A), lambda i: (i, 0)),
                  pl.BlockSpec((HALO, 3 * A), _halo_prev(tm)),
                  pl.BlockSpec((tm, 3 * A), lambda i: (i, 0)),
                  pl.BlockSpec((HALO, 3 * A), lambda i: (jnp.minimum((i + 1) * (tm // HALO), T // HALO - 1), 0)),
                  pl.BlockSpec((CONV_WIDTH, 3 * A), lambda i: (0, 0)), ANY],
        out_specs=[pl.BlockSpec((tm, 3 * A), lambda i: (i, 0)),
                   pl.BlockSpec((HALO, 3 * A), lambda i: (0, 0))],
        out_shape=[jax.ShapeDtypeStruct(dproj.shape, dproj.dtype), jax.ShapeDtypeStruct((HALO, 3 * A), F32)],
        input_output_aliases={5: 0},
        compiler_params=_cparams(("arbitrary",)),
    )(proj_m, proj_m, dc, dc, conv_w, dproj)


CHUNK = 128
BLOCK = 64


def _b(x):
    return x.astype(BF16)


@jax.custom_vjp
def _bdot(a, b):
    return _dot(_b(a), _b(b))


def _bdot_f(a, b):
    return _bdot(a, b), (a, b)


def _bdot_b(res, g):
    a, b = res
    return _dot_nt(_b(g), _b(b)), _dot_tn(_b(a), _b(g))


_bdot.defvjp(_bdot_f, _bdot_b)


@jax.custom_vjp
def _bdot_nt(a, b):
    return _dot_nt(_b(a), _b(b))


def _bdot_nt_f(a, b):
    return _bdot_nt(a, b), (a, b)


def _bdot_nt_b(res, g):
    a, b = res
    return _dot(_b(g), _b(b)), _dot_tn(_b(g), _b(a))


_bdot_nt.defvjp(_bdot_nt_f, _bdot_nt_b)


@jax.custom_vjp
def _bdot_tn(a, b):
    return _dot_tn(_b(a), _b(b))


def _bdot_tn_f(a, b):
    return _bdot_tn(a, b), (a, b)


def _bdot_tn_b(res, g):
    a, b = res
    return _dot_nt(_b(b), _b(g)), _dot(_b(a), _b(g))


_bdot_tn.defvjp(_bdot_tn_f, _bdot_tn_b)


def _mask_matmul(m, x):
    hi = _b(x)
    r = x - hi.astype(F32)
    mid = _b(r)
    lo = _b(r - mid.astype(F32))
    return (_dot(m, lo) + _dot(m, mid)) + _dot(m, hi)


@jax.custom_vjp
def _mask_dot(m, mt, x):
    return _mask_matmul(m, x)


def _mask_dot_f(m, mt, x):
    return _mask_matmul(m, x), (m, mt)


def _mask_dot_b(res, g):
    m, mt = res
    return jnp.zeros_like(m), jnp.zeros_like(mt), _mask_matmul(mt, g)


_mask_dot.defvjp(_mask_dot_f, _mask_dot_b)

def _unit_lower_inverse(L):
    n = L.shape[-1]
    X = -L
    Q = X
    for _ in range(BLOCK.bit_length() - 2):
        X = _dot(_b(X), _b(X))
        Q = Q + X + _dot(_b(Q), _b(X))
    return (_iota((n, n), 0) == _iota((n, n), 1)).astype(F32) + Q


@jax.custom_vjp
def _known_inverse(L, P):
    return P


def _known_inverse_f(L, P):
    return P, P


def _known_inverse_b(P, g):
    n = P.shape[-1]
    Q = _b(P - (_iota((n, n), 0) == _iota((n, n), 1)).astype(F32))
    t = g + _dot_tn(Q, _b(g))
    return -(t + _dot_nt(_b(t), Q)), jnp.zeros_like(P)


_known_inverse.defvjp(_known_inverse_f, _known_inverse_b)


def _gdn_prep_fn(q, k, v, gb, bb, P_known=None):
    n = CHUNK
    row, col = _iota((n, n), 0), _iota((n, n), 1)
    same = (row // BLOCK) == (col // BLOCK)
    incl, strict = same & (row >= col), same & (row > col)
    bc = lambda m: jnp.broadcast_to(_b(m.astype(F32)), q.shape[:1] + (n, n))
    tril, triu, ones = bc(incl), bc(same & (row <= col)), bc(same)
    gc = _mask_dot(tril, triu, gb)
    gl = _mask_dot(ones, ones, gb)
    decay = jnp.where(incl, jnp.exp(jnp.where(incl, gc - jnp.swapaxes(gc, 1, 2), 0.0)), 0.0)
    kb = k * bb
    vb = v * bb
    qs = q * (HEAD_DIM ** -0.5)
    L = jnp.where(strict, _bdot_nt(kb, k) * decay, 0.0)
    P = _unit_lower_inverse(L) if P_known is None else _known_inverse(L, P_known)
    egc = jnp.exp(gc)
    u = _bdot(P, vb)
    w = _bdot(P, kb * egc)
    attn = jnp.where(incl, _bdot_nt(qs, k) * decay, 0.0)
    qg = qs * egc
    kdec = k * jnp.exp(gl - gc)
    eg = jnp.exp(gl).reshape(-1, n // BLOCK, BLOCK, LANES).sum(axis=2) * (1.0 / BLOCK)
    if P_known is None:
        return u, w, qg, kdec, attn, eg, P
    return u, w, qg, kdec, attn, eg


def _gdn_block_fn(S, qg, kdec, u, w, attn, eg, i):
    nblk = CHUNK // BLOCK
    v_new = u - _bdot(w, S)
    zeros = jnp.zeros_like(v_new)
    o = _bdot(qg, S) + _bdot(attn, jnp.concatenate([zeros] * i + [v_new] + [zeros] * (nblk - 1 - i), axis=1))
    return o, S * eg + _bdot_tn(kdec, v_new)


def _eg_spec(H, T, chunks, index_map, per_head):
    nblk = CHUNK // BLOCK
    block = (chunks, 1 if per_head else H, nblk, LANES)
    return pl.BlockSpec(block, index_map), jax.ShapeDtypeStruct((T // CHUNK, H, nblk, LANES), F32)


def _gdn_prep(q, k, v, gb, bb):
    H, T, _ = q.shape
    pb = _pick(T // CHUNK, (16, 8, 4, 2, 1))
    hs = pl.BlockSpec((1, CHUNK * pb, HEAD_DIM), lambda h, n: (h, n, 0))

    def body(q_ref, k_ref, v_ref, gb_ref, bb_ref, *out_refs):
        chunks = lambda ref: ref[0].reshape(pb, CHUNK, HEAD_DIM)
        outs = _gdn_prep_fn(chunks(q_ref), chunks(k_ref), chunks(v_ref), chunks(gb_ref), chunks(bb_ref))
        for i, (ref, val) in enumerate(zip(out_refs, outs)):
            if i == 5:
                ref[:, 0] = val
            else:
                ref[0] = val.reshape(pb * CHUNK, HEAD_DIM).astype(ref.dtype)

    kept = [F32, BF16, BF16, BF16, BF16, None, BF16]
    es, eshape = _eg_spec(H, T, pb, lambda h, n: (n, h, 0, 0), per_head=True)
    return pl.pallas_call(
        body, name="gdn_prep", grid=(H, T // (CHUNK * pb)),
        in_specs=[hs] * 5, out_specs=[es if dt is None else hs for dt in kept],
        out_shape=[eshape if dt is None else jax.ShapeDtypeStruct((H, T, HEAD_DIM), dt) for dt in kept],
        compiler_params=_cparams(("parallel", "parallel")),
    )(q, k, v, gb, bb)


def _gdn_prep_bwd(q, k, v, gb, bb, pinv, du, dw, dqg, dkd, dat, deg):
    H, T, _ = q.shape
    pb = _pick(T // CHUNK, (16, 8, 4, 2, 1))
    hs = pl.BlockSpec((1, CHUNK * pb, HEAD_DIM), lambda h, n: (h, n, 0))
    hshape = jax.ShapeDtypeStruct((H, T, HEAD_DIM), F32)

    def body(*refs):
        in_refs, p_ref, ct_refs, out_refs = refs[:5], refs[5], refs[6:12], refs[12:]
        chunks = lambda ref: ref[0].reshape(pb, CHUNK, HEAD_DIM)
        P = chunks(p_ref).astype(F32)
        _, vjp = jax.vjp(lambda *a: _gdn_prep_fn(*a, P_known=P), *[chunks(r) for r in in_refs])
        grads = vjp(tuple(chunks(r).astype(F32) for r in ct_refs[:5]) + (ct_refs[5][:, 0],))
        for ref, val in zip(out_refs, grads):
            ref[0] = val.reshape(pb * CHUNK, HEAD_DIM)

    es, _ = _eg_spec(H, T, pb, lambda h, n: (n, h, 0, 0), per_head=True)
    return pl.pallas_call(
        body, name="gdn_prep_bwd", grid=(H, T // (CHUNK * pb)),
        in_specs=[hs] * 11 + [es], out_specs=[hs] * 5, out_shape=[hshape] * 5,
        compiler_params=_cparams(("parallel", "parallel")),
    )(q, k, v, gb, bb, pinv, du, dw, dqg, dkd, dat, deg)


def _gdn_chain(qg, kd, u, w, attn, eg):
    H, T, _ = qg.shape
    N, nblk = T // CHUNK, CHUNK // BLOCK
    cs = _pick(N, (2, 1))
    hs = pl.BlockSpec((H, cs * CHUNK, HEAD_DIM), lambda n: (0, n, 0))
    ss = pl.BlockSpec((cs, nblk, H, HEAD_DIM, HEAD_DIM), lambda n: (n, 0, 0, 0, 0))

    def body(qg_ref, kd_ref, u_ref, w_ref, at_ref, eg_ref, o_ref, sall_ref, s_ref):
        @pl.when(pl.program_id(0) == 0)
        def _():
            s_ref[...] = jnp.zeros_like(s_ref)

        S = s_ref[...]
        for j in range(cs):
            for i in range(nblk):
                r = pl.ds(j * CHUNK + i * BLOCK, BLOCK)
                sall_ref[j, i] = S
                o_ref[:, r, :], S = _gdn_block_fn(S, qg_ref[:, r, :], kd_ref[:, r, :], u_ref[:, r, :],
                                                  w_ref[:, r, :], at_ref[:, r, :], eg_ref[j, :, i:i + 1, :], i)
        s_ref[...] = S

    es, _ = _eg_spec(H, T, cs, lambda n: (n, 0, 0, 0), per_head=False)
    return pl.pallas_call(
        body, name="gdn_chain", grid=(N // cs,),
        in_specs=[hs] * 5 + [es], out_specs=[hs, ss],
        out_shape=[jax.ShapeDtypeStruct((H, T, HEAD_DIM), F32),
                   jax.ShapeDtypeStruct((N, nblk, H, HEAD_DIM, HEAD_DIM), F32)],
        scratch_shapes=[pltpu.VMEM((H, HEAD_DIM, HEAD_DIM), F32)],
        compiler_params=_cparams(("arbitrary",)),
    )(qg, kd, u, w, attn, eg)


def _gdn_chain_bwd(qg, kd, u, w, attn, eg, sall, do):
    H, T, _ = qg.shape
    N, nblk = T // CHUNK, CHUNK // BLOCK
    cs = _pick(N, (2, 1))
    last = N // cs - 1
    hs = pl.BlockSpec((H, cs * CHUNK, HEAD_DIM), lambda n: (0, last - n, 0))
    ss = pl.BlockSpec((cs, nblk, H, HEAD_DIM, HEAD_DIM), lambda n: (last - n, 0, 0, 0, 0))

    def body(qg_ref, kd_ref, u_ref, w_ref, at_ref, eg_ref, sall_ref, do_ref, *rest):
        out_refs, ds_ref = rest[:6], rest[6]

        @pl.when(pl.program_id(0) == 0)
        def _():
            ds_ref[...] = jnp.zeros_like(ds_ref)

        dS = ds_ref[...]
        for j in reversed(range(cs)):
            for i in reversed(range(nblk)):
                r = pl.ds(j * CHUNK + i * BLOCK, BLOCK)
                f32 = lambda ref: ref[:, r, :].astype(F32)
                _, vjp = jax.vjp(functools.partial(_gdn_block_fn, i=i), sall_ref[j, i], f32(qg_ref), f32(kd_ref),
                                 u_ref[:, r, :], f32(w_ref), f32(at_ref), eg_ref[j, :, i:i + 1, :])
                grads = vjp((do_ref[:, r, :], dS))
                dS = grads[0]
                for ref, val in zip(out_refs[:5], grads[1:6]):
                    ref[:, r, :] = val.astype(ref.dtype)
                out_refs[5][j, :, i:i + 1, :] = grads[6]
        ds_ref[...] = dS

    kept = [F32, F32, BF16, BF16, F32]
    es, eshape = _eg_spec(H, T, cs, lambda n: (last - n, 0, 0, 0), per_head=False)
    return pl.pallas_call(
        body, name="gdn_chain_bwd", grid=(N // cs,),
        in_specs=[hs] * 5 + [es, ss, hs], out_specs=[hs] * 5 + [es],
        out_shape=[jax.ShapeDtypeStruct((H, T, HEAD_DIM), dt) for dt in kept] + [eshape],
        scratch_shapes=[pltpu.VMEM((H, HEAD_DIM, HEAD_DIM), F32)],
        compiler_params=_cparams(("arbitrary",)),
    )(qg, kd, u, w, attn, eg, sall, do)


def _post_fn(ogs, za, hw):
    outs = []
    for h, o in enumerate(ogs):
        r = lax.rsqrt(jnp.mean(o * o, axis=-1, keepdims=True) + EPS)
        outs.append(o * r * hw * _silu(za[:, h * HEAD_DIM:(h + 1) * HEAD_DIM]))
    return jnp.concatenate(outs, axis=1)


def _gdn_post(og, proj_m, hw):
    H, T, _ = og.shape
    A = H * HEAD_DIM
    tm = _pick(T, (512, 256, 128))

    def body(og_ref, za_ref, hw_ref, o_ref, ot_ref):
        o = _post_fn(tuple(og_ref[h] for h in range(H)), za_ref[...], hw_ref[...])
        o_ref[...] = o.astype(BF16)
        ot_ref[...] = o.T.astype(BF16)

    return pl.pallas_call(
        body, name="gdn_post", grid=(T // tm,),
        in_specs=[pl.BlockSpec((H, tm, HEAD_DIM), lambda i: (0, i, 0)),
                  pl.BlockSpec((tm, A), lambda i: (i, ZA_BLOCK)),
                  pl.BlockSpec((1, HEAD_DIM), lambda i: (0, 0))],
        out_specs=[pl.BlockSpec((tm, A), lambda i: (i, 0)), pl.BlockSpec((A, tm), lambda i: (0, i))],
        out_shape=[jax.ShapeDtypeStruct((T, A), BF16), jax.ShapeDtypeStruct((A, T), BF16)],
        compiler_params=_cparams(("parallel",)),
    )(og, proj_m, hw)


def _gdn_post_bwd(og, proj_m, hw, d_o, dproj):
    H, T, _ = og.shape
    A = H * HEAD_DIM
    tm = _pick(T, (256, 128))

    def body(og_ref, za_ref, hw_ref, do_ref, _, dog_ref, dza_ref, dhw_ref):
        _, vjp = jax.vjp(_post_fn, tuple(og_ref[h] for h in range(H)), za_ref[...], hw_ref[...])
        dog, dza, dhw = vjp(do_ref[...])
        for h in range(H):
            dog_ref[h] = dog[h]
        dza_ref[...] = dza.astype(BF16)

        @pl.when(pl.program_id(0) == 0)
        def _():
            dhw_ref[...] = dhw

        @pl.when(pl.program_id(0) > 0)
        def _():
            dhw_ref[...] += dhw

    return pl.pallas_call(
        body, name="gdn_post_bwd", grid=(T // tm,),
        in_specs=[pl.BlockSpec((H, tm, HEAD_DIM), lambda i: (0, i, 0)),
                  pl.BlockSpec((tm, A), lambda i: (i, ZA_BLOCK)),
                  pl.BlockSpec((1, HEAD_DIM), lambda i: (0, 0)),
                  pl.BlockSpec((tm, A), lambda i: (i, 0)), ANY],
        out_specs=[pl.BlockSpec((H, tm, HEAD_DIM), lambda i: (0, i, 0)),
                   pl.BlockSpec((tm, A), lambda i: (i, ZA_BLOCK)),
                   pl.BlockSpec((1, HEAD_DIM), lambda i: (0, 0))],
        out_shape=[jax.ShapeDtypeStruct((H, T, HEAD_DIM), F32), jax.ShapeDtypeStruct(dproj.shape, dproj.dtype),
                   jax.ShapeDtypeStruct((1, HEAD_DIM), F32)],
        input_output_aliases={4: 1},
        compiler_params=_cparams(("arbitrary",)),
    )(og, proj_m, hw, d_o, dproj)


def _sgu_fn(ub, vb, zb, lw, lb, W, bbc):
    G = len(W)
    tm = ub.shape[0]
    mu = jnp.mean(vb, axis=-1, keepdims=True)
    xc = vb - mu
    var = jnp.mean(xc * xc, axis=-1, keepdims=True)
    vn = xc * lax.rsqrt(var + EPS) * lw + lb
    mask = _iota((CHUNK_B, CHUNK_B), 0) >= _iota((CHUNK_B, CHUNK_B), 1)
    cols = []
    for g in range(G):
        wm = jnp.where(mask, W[g], 0.0).astype(BF16)
        rows = []
        for c in range(tm // CHUNK_B):
            blk = vn[c * CHUNK_B:(c + 1) * CHUNK_B, g * HEAD_DIM:(g + 1) * HEAD_DIM].astype(BF16)
            rows.append(_dot(wm, blk) + bbc[g])
        cols.append(jnp.concatenate(rows, axis=0) if len(rows) > 1 else rows[0])
    s = jnp.concatenate(cols, axis=1)
    return ub * s * _silu(zb)


ZA_BLOCK = 6


def _sgu_cols(A, B):
    assert A == B
    return 3, 4, 5


def _sgu_fwd(proj_m, lw, lb, W, bbc, A):
    T = proj_m.shape[0]
    G = W.shape[0]
    B = G * HEAD_DIM
    tm = _pick(T, (256, 128))
    cu, cv, cz = _sgu_cols(A, B)

    def body(u_ref, v_ref, z_ref, lw_ref, lb_ref, w_ref, b_ref, o_ref, ot_ref):
        o = _sgu_fn(u_ref[...], v_ref[...], z_ref[...], lw_ref[...], lb_ref[...],
                    tuple(w_ref[g] for g in range(G)), tuple(b_ref[g] for g in range(G)))
        o_ref[...] = o.astype(BF16)
        ot_ref[...] = o.T.astype(BF16)

    row = pl.BlockSpec((1, B), lambda i: (0, 0))
    cube = pl.BlockSpec((G, CHUNK_B, CHUNK_B), lambda i: (0, 0, 0))
    return pl.pallas_call(
        body, name="sgu_fwd", grid=(T // tm,),
        in_specs=[pl.BlockSpec((tm, B), lambda i: (i, cu)), pl.BlockSpec((tm, B), lambda i: (i, cv)),
                  pl.BlockSpec((tm, B), lambda i: (i, cz)), row, row, cube, cube],
        out_specs=[pl.BlockSpec((tm, B), lambda i: (i, 0)), pl.BlockSpec((B, tm), lambda i: (0, i))],
        out_shape=[jax.ShapeDtypeStruct((T, B), BF16), jax.ShapeDtypeStruct((B, T), BF16)],
        compiler_params=_cparams(("parallel",)),
    )(proj_m, proj_m, proj_m, lw, lb, W, bbc)


def _sgu_bwd(proj_m, lw, lb, W, bbc, d_o, A, dproj):
    T = proj_m.shape[0]
    G = W.shape[0]
    B = G * HEAD_DIM
    tm = _pick(T, (256, 128))
    nt = T // tm
    cu, cv, cz = _sgu_cols(A, B)

    def body(u_ref, v_ref, z_ref, lw_ref, lb_ref, w_ref, b_ref, do_ref, _,
             dp_ref, dlw_ref, dlb_ref, dw_ref, db_ref, dbb_ref):
        _, vjp = jax.vjp(_sgu_fn, u_ref[...], v_ref[...], z_ref[...], lw_ref[...], lb_ref[...],
                         tuple(w_ref[g] for g in range(G)), tuple(b_ref[g] for g in range(G)))
        du, dv, dz, dlw, dlb, dW, dbb = vjp(do_ref[...])
        dW, dbb = jnp.stack(dW, axis=0), jnp.stack(dbb, axis=0)
        dp_ref[:, 0:B] = du.astype(BF16)
        dp_ref[:, B:2 * B] = dv.astype(BF16)
        dp_ref[:, 2 * B:3 * B] = dz.astype(BF16)
        i = pl.program_id(0)

        @pl.when(i == 0)
        def _():
            dlw_ref[...] = dlw
            dlb_ref[...] = dlb
            dw_ref[...] = dW
            dbb_ref[...] = dbb

        @pl.when(i > 0)
        def _():
            dlw_ref[...] += dlw
            dlb_ref[...] += dlb
            dw_ref[...] += dW
            dbb_ref[...] += dbb

        @pl.when(i == nt - 1)
        def _():
            db_ref[...] = jnp.sum(dbb_ref[...], axis=-1, keepdims=True)

    row = pl.BlockSpec((1, B), lambda i: (0, 0))
    cube = pl.BlockSpec((G, CHUNK_B, CHUNK_B), lambda i: (0, 0, 0))
    return pl.pallas_call(
        body, name="sgu_bwd", grid=(nt,),
        in_specs=[pl.BlockSpec((tm, B), lambda i: (i, cu)), pl.BlockSpec((tm, B), lambda i: (i, cv)),
                  pl.BlockSpec((tm, B), lambda i: (i, cz)), row, row, cube, cube,
                  pl.BlockSpec((tm, B), lambda i: (i, A // B)), ANY],
        out_specs=[pl.BlockSpec((tm, 3 * B), lambda i: (i, 1)), row, row, cube,
                   pl.BlockSpec((G, CHUNK_B, 1), lambda i: (0, 0, 0))],
        out_shape=[jax.ShapeDtypeStruct(dproj.shape, dproj.dtype), jax.ShapeDtypeStruct((1, B), F32),
                   jax.ShapeDtypeStruct((1, B), F32), jax.ShapeDtypeStruct((G, CHUNK_B, CHUNK_B), F32),
                   jax.ShapeDtypeStruct((G, CHUNK_B, 1), F32)],
        input_output_aliases={8: 0},
        scratch_shapes=[pltpu.VMEM((G, CHUNK_B, CHUNK_B), F32)],
        compiler_params=_cparams(("arbitrary",)),
    )(proj_m, proj_m, proj_m, lw, lb, W, bbc, d_o, dproj)


def _head_fn(mix, x, fw, tgt):
    h = x + mix
    y = _rms_fn(h, fw)
    e = y - tgt
    return 0.5 * jnp.sum(jnp.mean(e * e, axis=-1, keepdims=True), axis=0, keepdims=True)


def _out_proj_loss(oa, ob, wout, x, tgt, fw):
    T, A = oa.shape
    B = ob.shape[1]
    D = x.shape[1]
    tm = _pick(T, (256, 128))

    def body(oa_ref, ob_ref, w_ref, x_ref, t_ref, fw_ref, dh_ref, dhb_ref, loss_ref, dfw_ref):
        mix = _dot(oa_ref[...], w_ref[0:A, :]) + _dot(ob_ref[...], w_ref[A:A + B, :])
        xv, tv = x_ref[...], t_ref[...]
        loss, vjp = jax.vjp(lambda m, f: _head_fn(m, xv, f, tv), mix, fw_ref[...])
        dh, dfw = vjp(jnp.ones((1, 1), F32))
        dh_ref[...] = dh
        dhb_ref[...] = dh.astype(BF16)
        lrow = jnp.broadcast_to(loss, (1, LANES))

        @pl.when(pl.program_id(0) == 0)
        def _():
            loss_ref[...] = lrow
            dfw_ref[...] = dfw

        @pl.when(pl.program_id(0) > 0)
        def _():
            loss_ref[...] += lrow
            dfw_ref[...] += dfw

    tile = pl.BlockSpec((tm, D), lambda i: (i, 0))
    return pl.pallas_call(
        body, name="out_proj_loss", grid=(T // tm,),
        in_specs=[pl.BlockSpec((tm, A), lambda i: (i, 0)), pl.BlockSpec((tm, B), lambda i: (i, 0)),
                  pl.BlockSpec((A + B, D), lambda i: (0, 0)), tile, tile,
                  pl.BlockSpec((1, D), lambda i: (0, 0))],
        out_specs=[tile, tile, pl.BlockSpec((1, LANES), lambda i: (0, 0)),
                   pl.BlockSpec((1, D), lambda i: (0, 0))],
        out_shape=[jax.ShapeDtypeStruct((T, D), F32), jax.ShapeDtypeStruct((T, D), BF16),
                   jax.ShapeDtypeStruct((1, LANES), F32), jax.ShapeDtypeStruct((1, D), F32)],
        compiler_params=_cparams(("arbitrary",)),
    )(oa, ob, wout, x, tgt, fw)


def _adamw(w, g, m, v, name):
    R, Cn = w.shape
    cap = max(8, 512 * 1024 // Cn)
    tr = max(t for t in range(8, min(R, cap) + 1, 8) if R % t == 0) if R > cap else R

    def body(w_ref, g_ref, m_ref, v_ref, d_ref, mo_ref, vo_ref):
        g = g_ref[...]
        m = ADAM_B1 * m_ref[...] + (1.0 - ADAM_B1) * g
        v = ADAM_B2 * v_ref[...] + (1.0 - ADAM_B2) * jnp.square(g)
        m_hat = m / (1.0 - ADAM_B1 ** ADAM_STEP)
        v_hat = v / (1.0 - ADAM_B2 ** ADAM_STEP)
        d_ref[...] = -ADAM_LR * (m_hat / (jnp.sqrt(v_hat) + ADAM_EPS) + ADAM_WD * w_ref[...])
        mo_ref[...] = m
        vo_ref[...] = v

    tile = pl.BlockSpec((tr, Cn), lambda i: (i, 0))
    shape = jax.ShapeDtypeStruct((R, Cn), F32)
    return pl.pallas_call(
        body, name=name, grid=(R // tr,), in_specs=[tile] * 4, out_specs=[tile] * 3,
        out_shape=[shape] * 3, compiler_params=_cparams(("parallel",)),
    )(w, g, m, v)


def _place():
    x, y, c = lax.axis_index("x"), lax.axis_index("y"), lax.axis_index("c")
    others = [(1 - x, y), (x, 1 - y), (1 - x, 1 - y)]
    return x, y, c, others


def _chip_index(px, py):
    return 2 * px + py


ANY = pl.BlockSpec(memory_space=pl.ANY)


def _gather_ride(blocks, split):
    n = len(blocks)

    def plan(in_refs, out_refs, send_sems, recv_sems):
        x, y, c, _ = _place()
        me, kx, ky, kd = (_chip_index(px, py) for px, py in ((x, y), (1 - x, y), (x, 1 - y), (1 - x, 1 - y)))
        to_x, to_y, to_s = (1 - x, y, c), (x, 1 - y, c), (x, y, 1 - c)

        def copy(sem, src, dst, to):
            return pltpu.make_async_remote_copy(src_ref=src, dst_ref=dst, send_sem=send_sems.at[sem],
                                                recv_sem=recv_sems.at[sem], device_id=to, device_id_type=MESH_ID)

        first, second, third, awaited = [], [], [], []
        for a in range(n):
            out, s0 = out_refs[a], 8 * a
            if not split[a]:
                for j, (k, to) in enumerate(((kx, to_x), (ky, to_y), (kd, (1 - x, 1 - y, c)))):
                    first.append(lambda j=j, to=to, a=a, out=out, s0=s0: copy(s0 + j, in_refs[a], out.at[me], to))
                    awaited.append((lambda j=j, k=k, to=to, out=out, s0=s0: copy(s0 + j, out.at[k], out.at[k], to),
                                    None))
                continue
            h = blocks[a].shape[0] // 2
            q = h // 2
            half = lambda k, core, out=out, h=h: out.at[k, pl.ds(core * h, h), :]
            quarter = lambda k, core, i, out=out, h=h, q=q: out.at[k, pl.ds(core * h + i * q, q), :]
            mine = in_refs[a].at[pl.ds(c * h, h), :]
            first.append(lambda s0=s0, mine=mine, half=half: copy(s0, mine, half(me, c), to_x))
            first.append(lambda s0=s0, mine=mine, half=half: copy(s0 + 1, mine, half(me, c), to_y))
            fwd0 = lambda s0=s0, quarter=quarter: copy(s0 + 2, quarter(kx, c, 0), quarter(kx, c, 0), to_y)
            fwd1 = lambda s0=s0, quarter=quarter: copy(s0 + 3, quarter(ky, c, 1), quarter(ky, c, 1), to_x)
            pieces = [(s0 + 0, lambda half=half: half(kx, c), lambda half=half: half(kx, 1 - c), to_x, fwd0),
                      (s0 + 1, lambda half=half: half(ky, c), lambda half=half: half(ky, 1 - c), to_y, fwd1),
                      (s0 + 2, lambda quarter=quarter: quarter(kd, c, 0), lambda quarter=quarter: quarter(kd, 1 - c, 0),
                       to_y, None),
                      (s0 + 3, lambda quarter=quarter: quarter(kd, c, 1), lambda quarter=quarter: quarter(kd, 1 - c, 1),
                       to_x, None)]
            for i, (sem, here, there, frm, fwd) in enumerate(pieces):
                passing = lambda s0=s0, i=i, here=here: copy(s0 + 4 + i, here(), here(), to_s)
                awaited.append((lambda sem=sem, here=here, frm=frm: copy(sem, here(), here(), frm), (fwd, passing)))
                if fwd is not None:
                    second.append(fwd)
                third.append((passing, lambda s0=s0, i=i, there=there: copy(s0 + 4 + i, there(), there(), to_s)))
        return first, second, third, awaited

    def start(*refs):
        for send in plan(*refs)[0]:
            send().start()

    def finish(*refs):
        first, second, third, awaited = plan(*refs)
        for arrival, then in awaited:
            arrival().wait_recv()
            for nxt in (then or ()):
                if nxt is not None:
                    nxt().start()
        for _, from_sibling in third:
            from_sibling().wait_recv()
        for send in first + second + [p for p, _ in third]:
            send().wait_send()

    shapes = [jax.ShapeDtypeStruct((N_CHIPS,) + b.shape, b.dtype) for b in blocks]
    return _Ride(blocks, shapes, 8 * n, start, finish)


def _put_own(gathered, own):
    me = _chip_index(lax.axis_index("x"), lax.axis_index("y"))
    return lax.dynamic_update_index_in_dim(gathered, own, me, 0)


def _allreduce_small(buf):
    R0, L = buf.shape
    R = -(-R0 // 16) * 16
    h = R // 2
    buf = jnp.pad(buf, ((0, R - R0), (0, 0)))

    def body(in_ref, out_ref, sib_ref, pair_ref, chips_ref, send_sems, recv_sems):
        x, y, c, others = _place()
        me = _chip_index(x, y)
        sibling = (x, y, 1 - c)

        def copy(sem, src, dst, to):
            return pltpu.make_async_remote_copy(src_ref=src, dst_ref=dst, send_sem=send_sems.at[sem],
                                                recv_sem=recv_sems.at[sem], device_id=to, device_id_type=MESH_ID)

        cp = copy(0, in_ref, sib_ref, sibling)
        cp.start()
        cp.wait()
        pair_ref[...] = in_ref[...] + sib_ref[...]
        rows = lambda core: pl.ds(pl.multiple_of(core * h, 8), h)
        sends = [copy(1 + j, pair_ref.at[rows(c), :], chips_ref.at[me], (*chip, c)) for j, chip in enumerate(others)]
        for s in sends:
            s.start()
        chips_ref[me] = pair_ref[rows(c), :]
        for j, chip in enumerate(others):
            k = _chip_index(*chip)
            copy(1 + j, chips_ref.at[k], chips_ref.at[k], (*chip, c)).wait_recv()
        out_ref[rows(c), :] = ((chips_ref[0] + chips_ref[1]) + chips_ref[2]) + chips_ref[3]
        swap = copy(4, out_ref.at[rows(c), :], out_ref.at[rows(c), :], sibling)
        swap.start()
        copy(4, out_ref.at[rows(1 - c), :], out_ref.at[rows(1 - c), :], sibling).wait_recv()
        for s in sends + [swap]:
            s.wait_send()

    vm = pl.BlockSpec(memory_space=pltpu.VMEM)
    return pl.pallas_call(
        body, name="allreduce_small", in_specs=[vm], out_specs=vm,
        out_shape=jax.ShapeDtypeStruct((R, L), F32),
        scratch_shapes=[pltpu.VMEM((R, L), F32), pltpu.VMEM((R, L), F32), pltpu.VMEM((N_CHIPS, h, L), F32),
                        pltpu.SemaphoreType.DMA((5,)), pltpu.SemaphoreType.DMA((5,))],
        compiler_params=pltpu.CompilerParams(vmem_limit_bytes=VMEM_LIMIT),
    )(buf)[:R0]


def _pair_ride(g):
    nb, R, Cn = g.shape
    h = R // 2

    def copy(in_refs, out_refs, send_sems, recv_sems):
        x, y, c, _ = _place()
        return pltpu.make_async_remote_copy(src_ref=in_refs[0].at[:, pl.ds((1 - c) * h, h), :], dst_ref=out_refs[0],
                                            send_sem=send_sems.at[0], recv_sem=recv_sems.at[0],
                                            device_id=(x, y, 1 - c), device_id_type=MESH_ID)

    return _Ride([g], [jax.ShapeDtypeStruct((nb, h, Cn), g.dtype)], 1,
                 lambda *refs: copy(*refs).start(), lambda *refs: copy(*refs).wait())


def _pair_sum(g, land, c_arr, name, ride=None):
    nb, R, Cn = g.shape
    hr = R // 2
    tr = _pick(hr, (256, 128, 64, 32, 16))
    nt = hr // tr

    def body(c_ref, g_ref, l_ref, o_ref):
        o_ref[...] = (g_ref[...].astype(F32) + l_ref[...].astype(F32)).astype(BF16)

    return _pallas(
        body, (c_arr, g, land), name=name, prefetch=1, grid=(nb, nt),
        in_specs=[pl.BlockSpec((1, tr, Cn), lambda b, i, c_ref: (b, c_ref[0] * nt + i, 0)),
                  pl.BlockSpec((1, tr, Cn), lambda b, i, c_ref: (b, i, 0))],
        out_specs=pl.BlockSpec((1, tr, Cn), lambda b, i, c_ref: (b, i, 0)),
        out_shape=jax.ShapeDtypeStruct((nb, hr, Cn), BF16),
        semantics=("parallel", "parallel"), ride=ride)


def _chip_ride(parts, cols=None):
    m = len(parts)

    def copies(in_refs, out_refs, send_sems, recv_sems):
        x, y, c, others = _place()
        me = _chip_index(x, y)
        def mk(j, chip, n, landing):
            k = _chip_index(*chip)
            src = in_refs[n].at[k]
            if cols is not None:
                src = src.at[:, pl.ds(pl.multiple_of(cols[0](k), LANES), cols[1])]
            return pltpu.make_async_remote_copy(
                src_ref=src, dst_ref=out_refs[n].at[landing(k)], send_sem=send_sems.at[m * j + n],
                recv_sem=recv_sems.at[m * j + n], device_id=(*chip, c), device_id_type=MESH_ID)

        pairs = [(j, chip, n) for j, chip in enumerate(others) for n in range(m)]
        return pairs, (lambda *p: mk(*p, lambda k: me)), (lambda *p: mk(*p, lambda k: k))

    def start(*refs):
        pairs, send, _ = copies(*refs)
        for p in pairs:
            send(*p).start()

    def finish(*refs):
        pairs, send, arrival = copies(*refs)
        for p in pairs:
            arrival(*p).wait_recv()
        for p in pairs:
            send(*p).wait_send()

    width = lambda p: p.shape[2] if cols is None else cols[1]
    return _Ride(parts, [jax.ShapeDtypeStruct(p.shape[:2] + (width(p),), p.dtype) for p in parts], 3 * m,
                 start, finish)


def _put_own_slot(q, p, cols=None):
    me = _chip_index(lax.axis_index("x"), lax.axis_index("y"))
    own = lax.dynamic_index_in_dim(p, me, 0, keepdims=False)
    if cols is not None:
        own = lax.dynamic_slice_in_dim(own, cols[0](me), cols[1], axis=1)
    return lax.dynamic_update_index_in_dim(q, own, me, 0)


def _chip_sum(q, c_arr, name):
    nb, hr, Cn = q.shape
    tr = _pick(hr, (256, 128, 64, 32, 16))
    nt = hr // tr

    def body(c_ref, q_ref, o_ref):
        f = lambda k: q_ref[k].astype(F32)
        o_ref[...] = ((f(0) + f(1)) + f(2)) + f(3)

    return _pallas(
        body, (c_arr, q), name=name, prefetch=1, grid=(nt,),
        in_specs=[pl.BlockSpec((nb, tr, Cn), lambda i, c_ref: (0, i, 0))],
        out_specs=pl.BlockSpec((tr, Cn), lambda i, c_ref: (c_ref[0] * nt + i, 0)),
        out_shape=jax.ShapeDtypeStruct((2 * hr, Cn), F32),
        semantics=("parallel",))


def _sibling_fill(fw, fo):
    def body(_, __, fw_ref, fo_ref, send_sems, recv_sems):
        x, y, c, _ = _place()
        copies = []
        for n, ref in enumerate((fw_ref, fo_ref)):
            h = ref.shape[0] // 2
            mine = ref.at[pl.ds(c * h, h), :]
            theirs = ref.at[pl.ds((1 - c) * h, h), :]
            mk = lambda src, dst: pltpu.make_async_remote_copy(
                src_ref=src, dst_ref=dst, send_sem=send_sems.at[n], recv_sem=recv_sems.at[n],
                device_id=(x, y, 1 - c), device_id_type=MESH_ID)
            send = mk(mine, mine)
            send.start()
            copies.append((send, mk(theirs, theirs)))
        for send, arrival in copies:
            arrival.wait_recv()
            send.wait_send()

    return pl.pallas_call(
        body, name="sibling_fill", in_specs=[ANY, ANY], out_specs=[ANY, ANY],
        out_shape=[jax.ShapeDtypeStruct(fw.shape, F32), jax.ShapeDtypeStruct(fo.shape, F32)],
        input_output_aliases={0: 0, 1: 1},
        scratch_shapes=[pltpu.SemaphoreType.DMA((2,)), pltpu.SemaphoreType.DMA((2,))],
        compiler_params=pltpu.CompilerParams(has_side_effects=True),
    )(fw, fo)


class _Layout:
    def __init__(self, H, G, nb, Cb):
        A, B = H * HEAD_DIM, G * HEAD_DIM
        self.n_main = 4 * A + 3 * B
        self.k = -(-(self.n_main + LANES) // WIN_BLOCK) * WIN_BLOCK
        cuts = [0, 3 * A, 4 * A, 4 * A + 2 * H, nb * Cb]
        starts = [0, 3 * A + 3 * B, self.n_main, 3 * A]
        self.pieces = []
        self.windows, self.runs = [], []
        for n in range(nb):
            segs = []
            for s in range(4):
                lo, hi = max(cuts[s], n * Cb), min(cuts[s + 1], (n + 1) * Cb)
                if lo < hi:
                    segs.append((starts[s] + lo - cuts[s], lo - n * Cb, hi - lo))
            self.pieces += [(own, n, col, ln) for own, col, ln in segs]
            blocks = sorted({b for own, _, ln in segs for b in range(own // WIN_BLOCK, (own + ln - 1) // WIN_BLOCK + 1)})
            self.windows.append(blocks)
            self.runs.append([(blocks.index(own // WIN_BLOCK) * WIN_BLOCK + own % WIN_BLOCK, ln)
                              for own, _, ln in segs])
        self.wb = max(len(b) for b in self.windows)
        self.table = [b + [b[-1]] * (self.wb - len(b)) for b in self.windows]
        self.pieces.sort()
        self.used_from = [min(c for c, _ in r) // LANES * LANES for r in self.runs]
        self.used = max(-(-max(c + ln for c, ln in r) // LANES) * LANES - f for r, f in zip(self.runs, self.used_from))
        self.used_from = [min(f, self.wb * WIN_BLOCK - self.used) for f in self.used_from]

    def to_own_order(self, g_in):
        nb, D, Cb = g_in.shape
        tr = _pick(D, (256, 128))

        def body(g_ref, o_ref):
            cols, at = [], 0
            for own, n, col, ln in self.pieces:
                if own > at:
                    cols.append(jnp.zeros((tr, own - at), g_in.dtype))
                cols.append(g_ref[n, :, col:col + ln])
                at = own + ln
            if at < self.k:
                cols.append(jnp.zeros((tr, self.k - at), g_in.dtype))
            o_ref[...] = jnp.concatenate(cols, axis=1)

        return pl.pallas_call(
            body, name="own_order", grid=(D // tr,),
            in_specs=[pl.BlockSpec((nb, tr, Cb), lambda i: (0, i, 0))],
            out_specs=pl.BlockSpec((tr, self.k), lambda i: (i, 0)),
            out_shape=jax.ShapeDtypeStruct((D, self.k), g_in.dtype),
            compiler_params=_cparams(("parallel",)),
        )(g_in)

    def from_window(self, win, chip, Cb):
        pick = lambda runs, f: (lambda w: jnp.concatenate([w[:, c - f:c - f + ln] for c, ln in runs], axis=1))
        return lax.switch(chip, [pick(r, f) for r, f in zip(self.runs, self.used_from)], win)

    def used_start(self, chip):
        return sum(jnp.where(chip == n, f, 0) for n, f in enumerate(self.used_from))


def _device_step(x, tgt, norm_w, win_b, wout_b, conv_b, a_log, dt_bias, head_norm_w, sgu_ln_w, sgu_ln_b,
                 w_spatial, b_spatial, final_norm_w, c_arr):
    T, D = x.shape
    H = a_log.shape[1]
    A = H * HEAD_DIM
    G = w_spatial.shape[0]
    B = G * HEAD_DIM
    nb, Cb, Rb = N_CHIPS, win_b.shape[1], wout_b.shape[0]
    lay = _Layout(H, G, nb, Cb)
    alog_row = jnp.pad(a_log, ((0, 0), (H, LANES - 2 * H)))
    dtb_row = jnp.pad(dt_bias, ((0, 0), (H, LANES - 2 * H)))
    bbc = jnp.broadcast_to(b_spatial[:, :, None], (G, CHUNK_B, CHUNK_B))

    (xn, xn_t), (g_in,) = _rms_in(x, norm_w, ride=_gather_ride([win_b], [True]))
    w_own = lay.to_own_order(_put_own(g_in, win_b))
    proj_m, (g_out, g_conv) = _mm_nn(xn, w_own, F32, "in_proj", tm=2048, cols=(0, lay.n_main),
                                     ride=_gather_ride([wout_b, conv_b], [False, False]))
    wout = _put_own(g_out, wout_b).reshape(nb * Rb, D)
    conv_w = _put_own(g_conv, conv_b).transpose(1, 0, 2).reshape(CONV_WIDTH, nb * conv_b.shape[1])
    q, k, v, gb, bb, proj_ba = _gdn_pre(proj_m, xn, w_own, conv_w, alog_row, dtb_row, H)
    u, w, qg, kd, attn, eg, pinv = _gdn_prep(q, k, v, gb, bb)
    og, sall = _gdn_chain(qg, kd, u, w, attn, eg)
    oa, oa_t = _gdn_post(og, proj_m, head_norm_w)
    ob, ob_t = _sgu_fwd(proj_m, sgu_ln_w, sgu_ln_b, w_spatial, bbc, A)
    dh, dhb, loss_row, d_fnw = _out_proj_loss(oa, ob, wout, x, tgt, final_norm_w.reshape(1, D))

    d_o = _mm_nn(dhb, wout.T, F32, "out_proj_dx", tm=2048)
    dproj = lax.empty((T, lay.k), BF16)
    dproj, d_lw, d_lb, d_ws, d_bs = _sgu_bwd(proj_m, sgu_ln_w, sgu_ln_b, w_spatial, bbc, d_o, A, dproj)
    dog, dproj, d_hw = _gdn_post_bwd(og, proj_m, head_norm_w, d_o, dproj)
    dqg, dkd, du, dw, dat, deg = _gdn_chain_bwd(qg, kd, u, w, attn, eg, sall, dog)
    dq, dk, dv, dgb, dbb = _gdn_prep_bwd(q, k, v, gb, bb, pinv, du, dw, dqg, dkd, dat, deg)
    dc, dproj, d_al, d_dt = _gdn_pre_bwd(proj_m, proj_ba, conv_w, alog_row, dtb_row, dq, dk, dv, dgb, dbb, H,
                                         dproj)
    dproj, d_conv = _conv_bwd(proj_m, dc, conv_w, H, dproj)

    table = jnp.array([b for row in lay.table for b in row], jnp.int32)
    d_win = _mm_windows(xn_t, dproj, table, nb, "in_proj_dw")
    d_wout, (land_w,) = _mm_nn_pair(oa_t, ob_t, dhb, "out_proj_dw", ride=_pair_ride(d_win))
    d_wout = d_wout.reshape(nb, Rb, D)
    pair_w, (land_o,) = _pair_sum(d_win, land_w, c_arr, "pair_sum_w_in", ride=_pair_ride(d_wout))
    pair_o = _pair_sum(d_wout, land_o, c_arr, "pair_sum_w_out")
    used = (lay.used_start, lay.used)
    dxn, (all_w,) = _mm_nt_rhs_outer(dproj, w_own, F32, "in_proj_dx", ride=_chip_ride([pair_w], used))
    (grad_x, d_nw), (all_o,) = _rms_in_bwd(x, norm_w, dxn, dh, ride=_chip_ride([pair_o]))
    all_w, all_o = _put_own_slot(all_w, pair_w, used), _put_own_slot(all_o, pair_o)
    small = dict(norm_w=d_nw, conv_w=d_conv[:CONV_WIDTH], a_log=d_al[:, H:2 * H], dt_bias=d_dt[:, H:2 * H],
                 head_norm_w=d_hw, sgu_ln_w=d_lw, sgu_ln_b=d_lb, w_spatial=d_ws, b_spatial=d_bs[:, :, 0],
                 final_norm_w=d_fnw)
    return loss_row, grad_x, small, all_w, all_o


SMALL = ("norm_w", "conv_w", "a_log", "dt_bias", "head_norm_w", "sgu_ln_w", "sgu_ln_b", "w_spatial",
         "b_spatial", "final_norm_w")


def _pack(parts):
    rows = []
    for p in parts:
        f = p.reshape(-1)
        f = jnp.pad(f, (0, (-f.shape[0]) % (8 * LANES)))
        rows.append(f.reshape(-1, LANES))
    return jnp.concatenate(rows, axis=0)


def _unpack(buf, shapes):
    out, r = [], 0
    for s in shapes:
        n = 1
        for d in s:
            n *= d
        nr = -(-n // (8 * LANES)) * 8
        out.append(buf[r:r + nr].reshape(-1)[:n].reshape(s))
        r += nr
    return out


def kernel(x, norm_w, w_in, conv_w, a_log, dt_bias, head_norm_w, sgu_ln_w, sgu_ln_b, w_spatial, b_spatial, w_out, final_norm_w, loss_target, m_norm_w, m_w_in, m_conv_w, m_a_log, m_dt_bias, m_head_norm_w, m_sgu_ln_w, m_sgu_ln_b, m_w_spatial, m_b_spatial, m_w_out, m_final_norm_w, v_norm_w, v_w_in, v_conv_w, v_a_log, v_dt_bias, v_head_norm_w, v_sgu_ln_w, v_sgu_ln_b, v_w_spatial, v_b_spatial, v_w_out, v_final_norm_w):
    T, D = x.shape[1], x.shape[2]
    weights = dict(norm_w=norm_w, w_in=w_in, conv_w=conv_w, a_log=a_log, dt_bias=dt_bias, head_norm_w=head_norm_w,
                   sgu_ln_w=sgu_ln_w, sgu_ln_b=sgu_ln_b, w_spatial=w_spatial, b_spatial=b_spatial, w_out=w_out,
                   final_norm_w=final_norm_w)
    mom_m = dict(norm_w=m_norm_w, w_in=m_w_in, conv_w=m_conv_w, a_log=m_a_log, dt_bias=m_dt_bias,
                 head_norm_w=m_head_norm_w, sgu_ln_w=m_sgu_ln_w, sgu_ln_b=m_sgu_ln_b, w_spatial=m_w_spatial,
                 b_spatial=m_b_spatial, w_out=m_w_out, final_norm_w=m_final_norm_w)
    mom_v = dict(norm_w=v_norm_w, w_in=v_w_in, conv_w=v_conv_w, a_log=v_a_log, dt_bias=v_dt_bias,
                 head_norm_w=v_head_norm_w, sgu_ln_w=v_sgu_ln_w, sgu_ln_b=v_sgu_ln_b, w_spatial=v_w_spatial,
                 b_spatial=v_b_spatial, w_out=v_w_out, final_norm_w=v_final_norm_w)
    me = _chip_index(lax.axis_index("x"), lax.axis_index("y"))
    c_arr = lax.axis_index("c").astype(jnp.int32).reshape(1)
    Din, Cb = w_in.shape[1], w_in.shape[2]
    Rb = w_out.shape[1]
    cconv = conv_w.shape[2]

    loss_row, grad_x, g, qw, qo = _device_step(
        x[0], loss_target[0], norm_w, w_in[0].astype(BF16), w_out[0].astype(BF16), conv_w[0], a_log, dt_bias,
        head_norm_w, sgu_ln_w, sgu_ln_b, w_spatial[0], b_spatial[0], final_norm_w, c_arr)

    small_shapes = [tuple(g[n].shape) for n in SMALL] + [(1, LANES)]
    small = _allreduce_small(_pack([g[n] for n in SMALL] + [loss_row]))
    gsum_in, gsum_out = _sibling_fill(_chip_sum(qw, c_arr, "chip_sum_w_in"), _chip_sum(qo, c_arr, "chip_sum_w_out"))
    gsum_in = _Layout(a_log.shape[1], w_spatial.shape[1], N_CHIPS, Cb).from_window(gsum_in, me, Cb)
    *small, loss_sum = _unpack(small, small_shapes)
    gsmall = dict(zip(SMALL, small))
    gsmall["conv_w"] = lax.dynamic_slice_in_dim(gsmall["conv_w"], me * cconv, cconv, axis=1)

    grads, deltas, new_m, new_v = {}, {}, {}, {}
    d, m2, v2 = _adamw(w_out[0], gsum_out, m_w_out[0], v_w_out[0], "adamw_w_out")
    grads["w_out"], deltas["w_out"], new_m["w_out"], new_v["w_out"] = gsum_out[None], d[None], m2[None], v2[None]
    flat = lambda a: a.transpose(2, 0, 1).reshape(-1, LANES)
    unflat = lambda f: f.reshape(Cb, 1, Din).transpose(1, 2, 0)
    g_flat = gsum_in.T.reshape(-1, LANES)
    d, m2, v2 = _adamw(flat(w_in), g_flat, flat(m_w_in), flat(v_w_in), "adamw_w_in")
    grads["w_in"], deltas["w_in"], new_m["w_in"], new_v["w_in"] = unflat(g_flat), unflat(d), unflat(m2), unflat(v2)
    shapes = [tuple(weights[n].shape) for n in SMALL]
    ds, ms, vs = _adamw(_pack([weights[n] for n in SMALL]), _pack([gsmall[n] for n in SMALL]),
                        _pack([mom_m[n] for n in SMALL]), _pack([mom_v[n] for n in SMALL]), "adamw_small")
    for n, gq, d, m2, v2 in zip(SMALL, [gsmall[n] for n in SMALL], _unpack(ds, shapes), _unpack(ms, shapes),
                                _unpack(vs, shapes)):
        grads[n], deltas[n], new_m[n], new_v[n] = gq.reshape(weights[n].shape), d, m2, v2

    loss = loss_sum[0, 0]
    order = ("norm_w", "w_in", "conv_w", "a_log", "dt_bias", "head_norm_w", "sgu_ln_w", "sgu_ln_b", "w_spatial",
             "b_spatial", "w_out", "final_norm_w")
    return (loss, grad_x[None], *[grads[n] for n in order], *[deltas[n] for n in order],
            *[new_m[n] for n in order], *[new_v[n] for n in order])
```
